```python
import functools
import jax, jax.numpy as jnp
from jax import lax
import numpy as np

D_MODEL = 1024
BATCH = 8
SEQ = 2048
DEPTH = 2

CHUNK = 64
Q_BLOCK = 128
MLA_HEADS = 8
MLA_NOPE = 64
MLA_ROPE = 32
MLA_V = 64
Q_LORA = 384
KV_LORA = 256
ROPE_THETA = 10000.0
SB_HEADS = 8
SB_DIM = 64
C_HEADS = 16
C_DIM = 64
LEFT_CHUNKS = 8
BAND = (LEFT_CHUNKS + 1) * CHUNK
REL_CLIP = 256
D_FF = -(-8 * D_MODEL // (3 * 256)) * 256
EVEN_IN = Q_LORA + KV_LORA + MLA_ROPE + 3 * SB_HEADS * SB_DIM
MIX_EVEN = MLA_HEADS * MLA_V + SB_HEADS * SB_DIM
MIX_ODD = C_HEADS * C_DIM
N_EVEN = (DEPTH + 1) // 2
N_ODD = DEPTH // 2
RMS_EPS = 1e-6

kernel_name = "hybrid_mla_stickbreak_chunkband_encoder"


def rms_norm(x, g):
    x32 = x.astype(jnp.float32)
    y = x32 * lax.rsqrt(jnp.mean(x32 * x32, axis=-1, keepdims=True) + RMS_EPS)
    return (y * g.astype(jnp.float32)).astype(x.dtype)


def rope_tables(seq, dim):
    pos = jnp.arange(seq, dtype=jnp.float32)
    inv_freq = ROPE_THETA ** (-jnp.arange(0, dim, 2, dtype=jnp.float32) / dim)
    ang = pos[:, None] * inv_freq[None, :]
    return jnp.cos(ang), jnp.sin(ang)


def apply_rope(x, cos, sin):
    half = x.shape[-1] // 2
    c, s = cos.astype(x.dtype), sin.astype(x.dtype)
    x1, x2 = x[..., :half], x[..., half:]
    return jnp.concatenate([x1 * c - x2 * s, x2 * c + x1 * s], axis=-1)


def swiglu(u, w_gate, w_up, w_down):
    return (jax.nn.silu(u @ w_gate) * (u @ w_up)) @ w_down


def mla_stick_breaking_mixer(u, w_in, g_cq, w_uq, g_ckv, w_ukv, w_out):
    bsz, seq, _ = u.shape
    proj = u @ w_in
    o1 = Q_LORA
    o2 = o1 + KV_LORA
    o3 = o2 + MLA_ROPE
    nb = SB_HEADS * SB_DIM
    c_q, c_kv, k_r = proj[..., :o1], proj[..., o1:o2], proj[..., o2:o3]
    q_b = proj[..., o3:o3 + nb].reshape(bsz, seq, SB_HEADS, SB_DIM)
    k_b = proj[..., o3 + nb:o3 + 2 * nb].reshape(bsz, seq, SB_HEADS, SB_DIM)
    v_b = proj[..., o3 + 2 * nb:].reshape(bsz, seq, SB_HEADS, SB_DIM)

    cos, sin = rope_tables(seq, MLA_ROPE)
    q_a = (rms_norm(c_q, g_cq) @ w_uq).reshape(bsz, seq, MLA_HEADS, MLA_NOPE + MLA_ROPE)
    q_a = jnp.concatenate([q_a[..., :MLA_NOPE],
                           apply_rope(q_a[..., MLA_NOPE:], cos[:, None, :], sin[:, None, :])], axis=-1)
    kv = (rms_norm(c_kv, g_ckv) @ w_ukv).reshape(bsz, seq, MLA_HEADS, MLA_NOPE + MLA_V)
    k_rope = apply_rope(k_r, cos, sin)
    k_a = jnp.concatenate([kv[..., :MLA_NOPE],
                           jnp.broadcast_to(k_rope[:, :, None, :], (bsz, seq, MLA_HEADS, MLA_ROPE))], axis=-1)
    v_a = kv[..., MLA_NOPE:]
    scale_a = (MLA_NOPE + MLA_ROPE) ** -0.5
    scale_b = SB_DIM ** -0.5

    outs_a, outs_b = [], []
    for blk in range(seq // Q_BLOCK):
        t0 = blk * Q_BLOCK
        kend = t0 + Q_BLOCK
        t_pos = t0 + jnp.arange(Q_BLOCK)[:, None]
        s_pos = jnp.arange(kend)[None, :]
        s_a = jnp.einsum('bqhd,bkhd->bhqk', q_a[:, t0:kend], k_a[:, :kend]).astype(jnp.float32) * scale_a
        chunk_ok = (s_pos // CHUNK) <= (t_pos // CHUNK)
        p_a = jax.nn.softmax(jnp.where(chunk_ok, s_a, -jnp.inf), axis=-1)
        outs_a.append(jnp.einsum('bhqk,bkhd->bqhd', p_a.astype(v_a.dtype), v_a[:, :kend]))
        z = jnp.einsum('bqhd,bkhd->bhqk', q_b[:, t0:kend], k_b[:, :kend]).astype(jnp.float32) * scale_b
        before = s_pos < t_pos
        log_keep = jnp.where(before, jax.nn.log_sigmoid(-z), 0.0)
        log_between = lax.cumsum(log_keep, axis=3, reverse=True) - log_keep
        w_b = jnp.where(before, jnp.exp(jax.nn.log_sigmoid(z) + log_between), 0.0)
        outs_b.append(jnp.einsum('bhqk,bkhd->bqhd', w_b.astype(v_b.dtype), v_b[:, :kend]))

    o_a = jnp.concatenate(outs_a, axis=1).reshape(bsz, seq, MLA_HEADS * MLA_V)
    o_b = jnp.concatenate(outs_b, axis=1).reshape(bsz, seq, SB_HEADS * SB_DIM)
    return jnp.concatenate([o_a, o_b], axis=-1) @ w_out


def chunk_band_mixer(u, w_qkv, rel_bias, w_out):
    bsz, seq, _ = u.shape
    n_chunks = seq // CHUNK
    qkv = (u @ w_qkv).reshape(bsz, n_chunks, CHUNK, 3, C_HEADS, C_DIM)
    q, k, v = qkv[:, :, :, 0], qkv[:, :, :, 1], qkv[:, :, :, 2]
    pad = ((0, 0), (LEFT_CHUNKS, 0), (0, 0), (0, 0), (0, 0))
    k_p = jnp.pad(k, pad)
    v_p = jnp.pad(v, pad)
    scores = jnp.concatenate(
        [jnp.einsum('bnqhd,bnkhd->bnhqk', q, k_p[:, i:i + n_chunks]) for i in range(LEFT_CHUNKS + 1)],
        axis=-1).astype(jnp.float32) * (C_DIM ** -0.5)
    q_in = jnp.arange(CHUNK)[:, None]
    j = jnp.arange(BAND)[None, :]
    rel = (LEFT_CHUNKS - j // CHUNK) * CHUNK + q_in - j % CHUNK
    bias = rel_bias[:, jnp.clip(rel, -REL_CLIP, REL_CLIP) + REL_CLIP]
    valid = (jnp.arange(n_chunks)[:, None] - LEFT_CHUNKS + j // CHUNK) >= 0
    scores = scores + bias.astype(jnp.float32)[None, None]
    p = jax.nn.softmax(jnp.where(valid[None, :, None, None, :], scores, -jnp.inf), axis=-1).astype(v.dtype)
    parts = [jnp.einsum('bnhqk,bnkhd->bnqhd', p[..., i * CHUNK:(i + 1) * CHUNK], v_p[:, i:i + n_chunks])
             for i in range(LEFT_CHUNKS + 1)]
    o = functools.reduce(jnp.add, parts)
    return o.reshape(bsz, seq, MIX_ODD) @ w_out


def _fwd_setup_inputs(seed: int = 0) -> dict:
    key = jax.random.key(seed)
    ks = jax.random.split(key, 17)
    f32 = jnp.float32

    def nrm(k, shape, fan_in):
        return jax.random.normal(k, shape, f32) * (fan_in ** -0.5)

    def gain(k, shape):
        return 1.0 + 0.05 * jax.random.normal(k, shape, f32)

    return {
        "x": jax.random.normal(ks[0], (BATCH, SEQ, D_MODEL), f32),
        "ev_w_in": nrm(ks[1], (N_EVEN, D_MODEL, EVEN_IN), D_MODEL),
        "ev_g_cq": gain(ks[2], (N_EVEN, Q_LORA)),
        "ev_w_uq": nrm(ks[3], (N_EVEN, Q_LORA, MLA_HEADS * (MLA_NOPE + MLA_ROPE)), Q_LORA),
        "ev_g_ckv": gain(ks[4], (N_EVEN, KV_LORA)),
        "ev_w_ukv": nrm(ks[5], (N_EVEN, KV_LORA, MLA_HEADS * (MLA_NOPE + MLA_V)), KV_LORA),
        "ev_w_out": nrm(ks[6], (N_EVEN, MIX_EVEN, D_MODEL), MIX_EVEN),
        "od_w_qkv": nrm(ks[7], (N_ODD, D_MODEL, 3 * MIX_ODD), D_MODEL),
        "od_rel_bias": 0.1 * jax.random.normal(ks[8], (N_ODD, C_HEADS, 2 * REL_CLIP + 1), f32),
        "od_w_out": nrm(ks[9], (N_ODD, MIX_ODD, D_MODEL), MIX_ODD),
        "g_mix": gain(ks[10], (DEPTH, D_MODEL)),
        "g_ffn": gain(ks[11], (DEPTH, D_MODEL)),
        "w_gate": nrm(ks[12], (DEPTH, D_MODEL, D_FF), D_MODEL),
        "w_up": nrm(ks[13], (DEPTH, D_MODEL, D_FF), D_MODEL),
        "w_down": nrm(ks[14], (DEPTH, D_FF, D_MODEL), D_FF),
        "g_final": gain(ks[15], (D_MODEL,)),
    }


def _fwd_reference(x, ev_w_in, ev_g_cq, ev_w_uq, ev_g_ckv, ev_w_ukv, ev_w_out,
              od_w_qkv, od_rel_bias, od_w_out, g_mix, g_ffn, w_gate, w_up, w_down, g_final):
    h = x
    for layer in range(DEPTH):
        u = rms_norm(h, g_mix[layer])
        if layer % 2 == 0:
            i = layer // 2
            h = h + mla_stick_breaking_mixer(u, ev_w_in[i], ev_g_cq[i], ev_w_uq[i],
                                             ev_g_ckv[i], ev_w_ukv[i], ev_w_out[i])
        else:
            i = layer // 2
            h = h + chunk_band_mixer(u, od_w_qkv[i], od_rel_bias[i], od_w_out[i])
        u = rms_norm(h, g_ffn[layer])
        h = h + swiglu(u, w_gate[layer], w_up[layer], w_down[layer])
    return rms_norm(h, g_final)


import jax as _jax
import jax.numpy as _jnp

TWIN_FORMAT = 'train_step'
FWD_PARAMS = ['x', 'ev_w_in', 'ev_g_cq', 'ev_w_uq', 'ev_g_ckv', 'ev_w_ukv', 'ev_w_out', 'od_w_qkv', 'od_rel_bias', 'od_w_out', 'g_mix', 'g_ffn', 'w_gate', 'w_up', 'w_down', 'g_final']
TWIN_WEIGHTS = ['ev_w_in', 'ev_g_cq', 'ev_w_uq', 'ev_g_ckv', 'ev_w_ukv', 'ev_w_out', 'od_w_qkv', 'od_rel_bias', 'od_w_out', 'g_mix', 'g_ffn', 'w_gate', 'w_up', 'w_down', 'g_final']
TWIN_DIFF_INPUT = 'x'
TWIN_INPUTS = ['x', 'ev_w_in', 'ev_g_cq', 'ev_w_uq', 'ev_g_ckv', 'ev_w_ukv', 'ev_w_out', 'od_w_qkv', 'od_rel_bias', 'od_w_out', 'g_mix', 'g_ffn', 'w_gate', 'w_up', 'w_down', 'g_final', 'loss_target', 'm_ev_w_in', 'm_ev_g_cq', 'm_ev_w_uq', 'm_ev_g_ckv', 'm_ev_w_ukv', 'm_ev_w_out', 'm_od_w_qkv', 'm_od_rel_bias', 'm_od_w_out', 'm_g_mix', 'm_g_ffn', 'm_w_gate', 'm_w_up', 'm_w_down', 'm_g_final', 'v_ev_w_in', 'v_ev_g_cq', 'v_ev_w_uq', 'v_ev_g_ckv', 'v_ev_w_ukv', 'v_ev_w_out', 'v_od_w_qkv', 'v_od_rel_bias', 'v_od_w_out', 'v_g_mix', 'v_g_ffn', 'v_w_gate', 'v_w_up', 'v_w_down', 'v_g_final']
TWIN_OUTPUTS = ['loss', 'grad_x', 'grad_ev_w_in', 'grad_ev_g_cq', 'grad_ev_w_uq', 'grad_ev_g_ckv', 'grad_ev_w_ukv', 'grad_ev_w_out', 'grad_od_w_qkv', 'grad_od_rel_bias', 'grad_od_w_out', 'grad_g_mix', 'grad_g_ffn', 'grad_w_gate', 'grad_w_up', 'grad_w_down', 'grad_g_final', 'delta_ev_w_in', 'delta_ev_g_cq', 'delta_ev_w_uq', 'delta_ev_g_ckv', 'delta_ev_w_ukv', 'delta_ev_w_out', 'delta_od_w_qkv', 'delta_od_rel_bias', 'delta_od_w_out', 'delta_g_mix', 'delta_g_ffn', 'delta_w_gate', 'delta_w_up', 'delta_w_down', 'delta_g_final', 'new_m_ev_w_in', 'new_m_ev_g_cq', 'new_m_ev_w_uq', 'new_m_ev_g_ckv', 'new_m_ev_w_ukv', 'new_m_ev_w_out', 'new_m_od_w_qkv', 'new_m_od_rel_bias', 'new_m_od_w_out', 'new_m_g_mix', 'new_m_g_ffn', 'new_m_w_gate', 'new_m_w_up', 'new_m_w_down', 'new_m_g_final', 'new_v_ev_w_in', 'new_v_ev_g_cq', 'new_v_ev_w_uq', 'new_v_ev_g_ckv', 'new_v_ev_w_ukv', 'new_v_ev_w_out', 'new_v_od_w_qkv', 'new_v_od_rel_bias', 'new_v_od_w_out', 'new_v_g_mix', 'new_v_g_ffn', 'new_v_w_gate', 'new_v_w_up', 'new_v_w_down', 'new_v_g_final']
TWIN_LEAF_KINDS = {'loss': 'loss', 'grad_x': 'grad_x', 'grad_ev_w_in': 'grad_w', 'grad_ev_g_cq': 'grad_w', 'grad_ev_w_uq': 'grad_w', 'grad_ev_g_ckv': 'grad_w', 'grad_ev_w_ukv': 'grad_w', 'grad_ev_w_out': 'grad_w', 'grad_od_w_qkv': 'grad_w', 'grad_od_rel_bias': 'grad_w', 'grad_od_w_out': 'grad_w', 'grad_g_mix': 'grad_w', 'grad_g_ffn': 'grad_w', 'grad_w_gate': 'grad_w', 'grad_w_up': 'grad_w', 'grad_w_down': 'grad_w', 'grad_g_final': 'grad_w', 'delta_ev_w_in': 'delta_w', 'delta_ev_g_cq': 'delta_w', 'delta_ev_w_uq': 'delta_w', 'delta_ev_g_ckv': 'delta_w', 'delta_ev_w_ukv': 'delta_w', 'delta_ev_w_out': 'delta_w', 'delta_od_w_qkv': 'delta_w', 'delta_od_rel_bias': 'delta_w', 'delta_od_w_out': 'delta_w', 'delta_g_mix': 'delta_w', 'delta_g_ffn': 'delta_w', 'delta_w_gate': 'delta_w', 'delta_w_up': 'delta_w', 'delta_w_down': 'delta_w', 'delta_g_final': 'delta_w', 'new_m_ev_w_in': 'new_m', 'new_m_ev_g_cq': 'new_m', 'new_m_ev_w_uq': 'new_m', 'new_m_ev_g_ckv': 'new_m', 'new_m_ev_w_ukv': 'new_m', 'new_m_ev_w_out': 'new_m', 'new_m_od_w_qkv': 'new_m', 'new_m_od_rel_bias': 'new_m', 'new_m_od_w_out': 'new_m', 'new_m_g_mix': 'new_m', 'new_m_g_ffn': 'new_m', 'new_m_w_gate': 'new_m', 'new_m_w_up': 'new_m', 'new_m_w_down': 'new_m', 'new_m_g_final': 'new_m', 'new_v_ev_w_in': 'new_v', 'new_v_ev_g_cq': 'new_v', 'new_v_ev_w_uq': 'new_v', 'new_v_ev_g_ckv': 'new_v', 'new_v_ev_w_ukv': 'new_v', 'new_v_ev_w_out': 'new_v', 'new_v_od_w_qkv': 'new_v', 'new_v_od_rel_bias': 'new_v', 'new_v_od_w_out': 'new_v', 'new_v_g_mix': 'new_v', 'new_v_g_ffn': 'new_v', 'new_v_w_gate': 'new_v', 'new_v_w_up': 'new_v', 'new_v_w_down': 'new_v', 'new_v_g_final': 'new_v'}


def _forward(args):
    return _fwd_reference(*[args[k] for k in FWD_PARAMS])


def _output_shape():
    out = _jax.eval_shape(lambda: _forward(_fwd_setup_inputs(0)))
    return out.shape, out.dtype

N_MICROBATCH = 1
ADAM_LR = 0.001
ADAM_B1 = 0.9
ADAM_B2 = 0.999
ADAM_EPS = 1e-08
ADAM_WD = 0.01
ADAM_STEP = 10
PER_EXAMPLE_BATCH_AXIS = {'x': 0, 'loss_target': 0}
SHARED_INPUTS = []
_WEIGHT_DTYPES = {'ev_w_in': _jnp.float32, 'ev_g_cq': _jnp.float32, 'ev_w_uq': _jnp.float32, 'ev_g_ckv': _jnp.float32, 'ev_w_ukv': _jnp.float32, 'ev_w_out': _jnp.float32, 'od_w_qkv': _jnp.float32, 'od_rel_bias': _jnp.float32, 'od_w_out': _jnp.float32, 'g_mix': _jnp.float32, 'g_ffn': _jnp.float32, 'w_gate': _jnp.float32, 'w_up': _jnp.float32, 'w_down': _jnp.float32, 'g_final': _jnp.float32}
MOMENT_SCALE = {'ev_w_in': 5.505353e-02, 'ev_g_cq': 2.381504e-02, 'ev_w_uq': 1.711554e-02, 'ev_g_ckv': 4.449017e-02, 'ev_w_ukv': 2.085936e-02, 'ev_w_out': 6.596855e-02, 'od_w_qkv': 1.642169e-02, 'od_rel_bias': 5.420092e-03, 'od_w_out': 1.791145e-02, 'g_mix': 6.078837e-02, 'g_ffn': 8.706690e-02, 'w_gate': 3.755674e-02, 'w_up': 3.639940e-02, 'w_down': 6.050160e-02, 'g_final': 1.601513e+01}


def _to_microbatches(a, axis):
    t = _jnp.moveaxis(a, axis, 0)
    t = t.reshape((N_MICROBATCH, t.shape[0] // N_MICROBATCH) + t.shape[1:])
    return _jnp.moveaxis(t, 1, axis + 1)


def setup_inputs(seed: int = 0) -> dict:
    inp = _fwd_setup_inputs(seed)
    key = _jax.random.fold_in(_jax.random.key(seed), 7919)
    shape, _ = _output_shape()
    out = dict(inp)
    out["loss_target"] = _jax.random.normal(_jax.random.fold_in(key, 0), shape, _jnp.float32)
    for i, name in enumerate(TWIN_WEIGHTS):
        w = inp[name].astype(_jnp.float32)
        if MOMENT_SCALE is None:
            s = _jnp.sqrt(_jnp.mean(_jnp.square(w)) + 1e-30)
        else:
            s = MOMENT_SCALE[name]
        km, kv = _jax.random.split(_jax.random.fold_in(key, i + 1))
        out[name] = w
        out["m_" + name] = s * _jax.random.normal(km, w.shape, _jnp.float32)
        out["v_" + name] = (s * s) * _jax.random.uniform(kv, w.shape, _jnp.float32, 0.5, 1.5)
    if N_MICROBATCH > 1:
        for name, axis in PER_EXAMPLE_BATCH_AXIS.items():
            out[name] = _to_microbatches(out[name], axis)
    return {'x': out['x'], 'ev_w_in': out['ev_w_in'], 'ev_g_cq': out['ev_g_cq'], 'ev_w_uq': out['ev_w_uq'], 'ev_g_ckv': out['ev_g_ckv'], 'ev_w_ukv': out['ev_w_ukv'], 'ev_w_out': out['ev_w_out'], 'od_w_qkv': out['od_w_qkv'], 'od_rel_bias': out['od_rel_bias'], 'od_w_out': out['od_w_out'], 'g_mix': out['g_mix'], 'g_ffn': out['g_ffn'], 'w_gate': out['w_gate'], 'w_up': out['w_up'], 'w_down': out['w_down'], 'g_final': out['g_final'], 'loss_target': out['loss_target'], 'm_ev_w_in': out['m_ev_w_in'], 'm_ev_g_cq': out['m_ev_g_cq'], 'm_ev_w_uq': out['m_ev_w_uq'], 'm_ev_g_ckv': out['m_ev_g_ckv'], 'm_ev_w_ukv': out['m_ev_w_ukv'], 'm_ev_w_out': out['m_ev_w_out'], 'm_od_w_qkv': out['m_od_w_qkv'], 'm_od_rel_bias': out['m_od_rel_bias'], 'm_od_w_out': out['m_od_w_out'], 'm_g_mix': out['m_g_mix'], 'm_g_ffn': out['m_g_ffn'], 'm_w_gate': out['m_w_gate'], 'm_w_up': out['m_w_up'], 'm_w_down': out['m_w_down'], 'm_g_final': out['m_g_final'], 'v_ev_w_in': out['v_ev_w_in'], 'v_ev_g_cq': out['v_ev_g_cq'], 'v_ev_w_uq': out['v_ev_w_uq'], 'v_ev_g_ckv': out['v_ev_g_ckv'], 'v_ev_w_ukv': out['v_ev_w_ukv'], 'v_ev_w_out': out['v_ev_w_out'], 'v_od_w_qkv': out['v_od_w_qkv'], 'v_od_rel_bias': out['v_od_rel_bias'], 'v_od_w_out': out['v_od_w_out'], 'v_g_mix': out['v_g_mix'], 'v_g_ffn': out['v_g_ffn'], 'v_w_gate': out['v_w_gate'], 'v_w_up': out['v_w_up'], 'v_w_down': out['v_w_down'], 'v_g_final': out['v_g_final']}


def _loss(weights, diff, rest, loss_target):
    with _jax.named_scope("forward"):
        args = {**rest, TWIN_DIFF_INPUT: diff, **{k: w.astype(_WEIGHT_DTYPES[k]) for k, w in weights.items()}}
        y = _forward(args)
    with _jax.named_scope("loss_head"):
        err = _jnp.square(y.astype(_jnp.float32) - loss_target)
        return 0.5 * _jnp.sum(_jnp.mean(err, axis=-1)) if err.ndim else 0.5 * err


def _adamw(w, g, m, v):
    m = ADAM_B1 * m + (1.0 - ADAM_B1) * g
    v = ADAM_B2 * v + (1.0 - ADAM_B2) * _jnp.square(g)
    m_hat = m / (1.0 - ADAM_B1 ** ADAM_STEP)
    v_hat = v / (1.0 - ADAM_B2 ** ADAM_STEP)
    delta = -ADAM_LR * (m_hat / (_jnp.sqrt(v_hat) + ADAM_EPS) + ADAM_WD * w)
    return delta, m, v


def reference(x, ev_w_in, ev_g_cq, ev_w_uq, ev_g_ckv, ev_w_ukv, ev_w_out, od_w_qkv, od_rel_bias, od_w_out, g_mix, g_ffn, w_gate, w_up, w_down, g_final, loss_target, m_ev_w_in, m_ev_g_cq, m_ev_w_uq, m_ev_g_ckv, m_ev_w_ukv, m_ev_w_out, m_od_w_qkv, m_od_rel_bias, m_od_w_out, m_g_mix, m_g_ffn, m_w_gate, m_w_up, m_w_down, m_g_final, v_ev_w_in, v_ev_g_cq, v_ev_w_uq, v_ev_g_ckv, v_ev_w_ukv, v_ev_w_out, v_od_w_qkv, v_od_rel_bias, v_od_w_out, v_g_mix, v_g_ffn, v_w_gate, v_w_up, v_w_down, v_g_final):
    given = dict(x=x, ev_w_in=ev_w_in, ev_g_cq=ev_g_cq, ev_w_uq=ev_w_uq, ev_g_ckv=ev_g_ckv, ev_w_ukv=ev_w_ukv, ev_w_out=ev_w_out, od_w_qkv=od_w_qkv, od_rel_bias=od_rel_bias, od_w_out=od_w_out, g_mix=g_mix, g_ffn=g_ffn, w_gate=w_gate, w_up=w_up, w_down=w_down, g_final=g_final, loss_target=loss_target, m_ev_w_in=m_ev_w_in, m_ev_g_cq=m_ev_g_cq, m_ev_w_uq=m_ev_w_uq, m_ev_g_ckv=m_ev_g_ckv, m_ev_w_ukv=m_ev_w_ukv, m_ev_w_out=m_ev_w_out, m_od_w_qkv=m_od_w_qkv, m_od_rel_bias=m_od_rel_bias, m_od_w_out=m_od_w_out, m_g_mix=m_g_mix, m_g_ffn=m_g_ffn, m_w_gate=m_w_gate, m_w_up=m_w_up, m_w_down=m_w_down, m_g_final=m_g_final, v_ev_w_in=v_ev_w_in, v_ev_g_cq=v_ev_g_cq, v_ev_w_uq=v_ev_w_uq, v_ev_g_ckv=v_ev_g_ckv, v_ev_w_ukv=v_ev_w_ukv, v_ev_w_out=v_ev_w_out, v_od_w_qkv=v_od_w_qkv, v_od_rel_bias=v_od_rel_bias, v_od_w_out=v_od_w_out, v_g_mix=v_g_mix, v_g_ffn=v_g_ffn, v_w_gate=v_w_gate, v_w_up=v_w_up, v_w_down=v_w_down, v_g_final=v_g_final)
    weights = {n: given[n] for n in TWIN_WEIGHTS}
    shared = {n: given[n] for n in SHARED_INPUTS}
    per_example = {n: given[n] for n in ['x']}
    grad_fn = _jax.value_and_grad(_loss, argnums=(0, 1))

    def one_microbatch(ex, loss_target):
        ex = dict(ex)
        diff = ex.pop(TWIN_DIFF_INPUT)
        return grad_fn(weights, diff, {**shared, **ex}, loss_target)

    if N_MICROBATCH == 1:
        loss, (grad_w, grad_x) = one_microbatch(per_example, given["loss_target"])
    else:
        def body(carry, xs):
            loss_sum, grad_sum = carry
            l_k, (gw_k, gx_k) = one_microbatch(xs[0], xs[1])
            with _jax.named_scope("update"):
                return (loss_sum + l_k, _jax.tree.map(_jnp.add, grad_sum, gw_k)), gx_k

        init = (_jnp.zeros((), _jnp.float32), _jax.tree.map(_jnp.zeros_like, weights))
        (loss, grad_w), grad_x = _jax.lax.scan(body, init, (per_example, given["loss_target"]))
    with _jax.named_scope("update"):
        delta_w, new_m, new_v = {}, {}, {}
        for n in TWIN_WEIGHTS:
            delta_w[n], new_m[n], new_v[n] = _adamw(weights[n], grad_w[n], given["m_" + n], given["v_" + n])
    return (loss, grad_x, *[grad_w[n] for n in TWIN_WEIGHTS], *[delta_w[n] for n in TWIN_WEIGHTS],
            *[new_m[n] for n in TWIN_WEIGHTS], *[new_v[n] for n in TWIN_WEIGHTS])
```

```python
import functools

import jax
import jax.numpy as jnp
import numpy as np
from jax import lax
from jax.experimental import pallas as pl
from jax.experimental.pallas import tpu as pltpu

F32 = jnp.float32
BF16 = jnp.bfloat16

S = 2048
D = 1024
CHUNK = 64
MLA_H, MLA_NOPE, MLA_ROPE, MLA_V = 8, 64, 32, 64
Q_LORA, KV_LORA = 384, 256
ROPE_THETA = 10000.0
SB_H, SB_DIM = 8, 64
C_H, C_DIM = 16, 64
LEFT_CHUNKS = 8
REL_CLIP = 256
D_FF = 2816
EVEN_IN = 2208
RMS_EPS = 1e-6
ADAM_LR, ADAM_B1, ADAM_B2, ADAM_EPS, ADAM_WD, ADAM_STEP = 0.001, 0.9, 0.999, 1e-08, 0.01, 10

N_CHIPS = 4
FF_SHARD = D_FF // N_CHIPS
SCALE_A = (MLA_NOPE + MLA_ROPE) ** -0.5
SCALE_B = SB_DIM ** -0.5
SCALE_C = C_DIM ** -0.5
NEG = -1e30

LANES = 128
VMEM_LIMIT_BYTES = 56 * 1024 * 1024
TM = 512
QB = 128

P_CQ, P_CKV, P_QB, P_KB, P_VB, P_KR = 0, 512, 768, 1280, 1792, 2304
P_IN = 2432
KR_LANE = 64
BAND_W = 640
BAND_PAD = 512
TOEP_W = 1024


def _params(*sem):
    return pltpu.CompilerParams(dimension_semantics=sem, vmem_limit_bytes=VMEM_LIMIT_BYTES)


_DIMS = {"nn": (((1,), (0,)), ((), ())), "nt": (((1,), (1,)), ((), ())), "tn": (((0,), (0,)), ((), ()))}


def _dot(a, b, kind="nn"):
    return lax.dot_general(a, b, _DIMS[kind], preferred_element_type=F32)


def _iota(shape, dim):
    return lax.broadcasted_iota(jnp.int32, shape, dim)


def _sigmoid(x):
    return 1.0 / (1.0 + jnp.exp(-x))


def _softplus(x):
    return jnp.maximum(x, 0.0) + jnp.log(1.0 + jnp.exp(-jnp.abs(x)))


def _split_dot(x, tri):
    hi = x.astype(BF16)
    lo = (x - hi.astype(F32)).astype(BF16)
    return _dot(hi, tri) + _dot(lo, tri)


def _mm(name, a, b, *, kind, grid, a_spec, b_spec, o_spec, out_shape, out_dtype, acc_shape, resid=None, r_spec=None):
    nk = grid[-1]
    has_r = resid is not None

    def body(*refs):
        a_ref, b_ref = refs[0], refs[1]
        r_ref = refs[2] if has_r else None
        o_ref = refs[2 + has_r]
        part = _dot(a_ref[...].astype(BF16), b_ref[...].astype(BF16), kind)

        def finish(total):
            if has_r:
                total = total + r_ref[...].astype(F32)
            o_ref[...] = total.astype(out_dtype)

        if nk == 1:
            finish(part)
        else:
            acc_ref = refs[3 + has_r]
            k = pl.program_id(len(grid) - 1)

            @pl.when(k == 0)
            def _():
                acc_ref[...] = part

            @pl.when(k > 0)
            def _():
                acc_ref[...] += part

            @pl.when(k == nk - 1)
            def _():
                finish(acc_ref[...])

    in_specs = [a_spec, b_spec] + ([r_spec] if has_r else [])
    args = (a, b) + ((resid,) if has_r else ())
    sem = ("parallel",) * (len(grid) - 1) + ("arbitrary",)
    return pl.pallas_call(
        body, name=name, grid=grid, in_specs=in_specs, out_specs=o_spec,
        out_shape=jax.ShapeDtypeStruct(out_shape, out_dtype),
        scratch_shapes=[pltpu.VMEM(acc_shape, F32)] if nk > 1 else [],
        compiler_params=_params(*sem),
    )(*args)


def _rms_fwd(name, x, g, col_block=0):
    c = g.shape[1]

    def body(x_ref, g_ref, u_ref):
        xv = x_ref[...]
        r = lax.rsqrt(jnp.mean(xv * xv, axis=-1, keepdims=True) + RMS_EPS)
        u_ref[...] = (xv * r * g_ref[...]).astype(BF16)

    return pl.pallas_call(
        body, name=name, grid=(S // TM,),
        in_specs=[pl.BlockSpec((TM, c), lambda i: (i, col_block)), pl.BlockSpec((1, c), lambda i: (0, 0))],
        out_specs=pl.BlockSpec((TM, c), lambda i: (i, 0)),
        out_shape=jax.ShapeDtypeStruct((S, c), BF16),
        compiler_params=_params("parallel"),
    )(x, g)


def _rms_bwd(name, dy, x, g, resid):
    def body(dy_ref, x_ref, g_ref, r_ref, dx_ref, dg_ref):
        i = pl.program_id(0)
        xv = x_ref[...]
        r = lax.rsqrt(jnp.mean(xv * xv, axis=-1, keepdims=True) + RMS_EPS)
        xh = xv * r
        dyv = dy_ref[...]
        dxh = dyv * g_ref[...]
        dx_ref[...] = r_ref[...] + r * (dxh - xh * jnp.mean(dxh * xh, axis=-1, keepdims=True))
        part = jnp.sum(dyv * xh, axis=0, keepdims=True)

        @pl.when(i == 0)
        def _():
            dg_ref[...] = part

        @pl.when(i > 0)
        def _():
            dg_ref[...] += part

    row = pl.BlockSpec((TM, D), lambda i: (i, 0))
    vec = pl.BlockSpec((1, D), lambda i: (0, 0))
    return pl.pallas_call(
        body, name=name, grid=(S // TM,), in_specs=[row, row, vec, row], out_specs=[row, vec],
        out_shape=[jax.ShapeDtypeStruct((S, D), F32), jax.ShapeDtypeStruct((1, D), F32)],
        compiler_params=_params("arbitrary"),
    )(dy, x, g, resid)


def _loss_bwd(name, h, g, tgt):
    def body(h_ref, g_ref, t_ref, loss_ref, dh_ref, dg_ref):
        i = pl.program_id(0)
        xv = h_ref[...]
        gv = g_ref[...]
        r = lax.rsqrt(jnp.mean(xv * xv, axis=-1, keepdims=True) + RMS_EPS)
        xh = xv * r
        diff = xh * gv - t_ref[...]
        part_loss = 0.5 * jnp.sum(jnp.sum(diff * diff, axis=-1, keepdims=True) * (1.0 / D), axis=0, keepdims=True)
        dy = diff * (1.0 / D)
        dxh = dy * gv
        dh_ref[...] = r * (dxh - xh * jnp.mean(dxh * xh, axis=-1, keepdims=True))
        part_g = jnp.sum(dy * xh, axis=0, keepdims=True)

        @pl.when(i == 0)
        def _():
            dg_ref[...] = part_g
            loss_ref[...] = jnp.broadcast_to(part_loss, (1, LANES))

        @pl.when(i > 0)
        def _():
            dg_ref[...] += part_g
            loss_ref[...] += jnp.broadcast_to(part_loss, (1, LANES))

    row = pl.BlockSpec((TM, D), lambda i: (i, 0))
    vec = pl.BlockSpec((1, D), lambda i: (0, 0))
    return pl.pallas_call(
        body, name=name, grid=(S // TM,), in_specs=[row, vec, row],
        out_specs=[pl.BlockSpec((1, LANES), lambda i: (0, 0)), row, vec],
        out_shape=[jax.ShapeDtypeStruct((1, LANES), F32), jax.ShapeDtypeStruct((S, D), F32),
                   jax.ShapeDtypeStruct((1, D), F32)],
        compiler_params=_params("arbitrary"),
    )(h, g, tgt)


def _ffn_fwd(name, h, g, wg, wu, wd, layer):
    def body(h_ref, g_ref, wg_ref, wu_ref, wd_ref, o_ref, gate_ref, up_ref, u_scr):
        s = pl.program_id(1)

        @pl.when(s == 0)
        def _():
            xv = h_ref[...]
            r = lax.rsqrt(jnp.mean(xv * xv, axis=-1, keepdims=True) + RMS_EPS)
            u_scr[...] = (xv * r * g_ref[...]).astype(BF16)
            o_ref[...] = xv

        u = u_scr[...]
        gate = _dot(u, wg_ref[...])
        up = _dot(u, wu_ref[...])
        act = gate * _sigmoid(gate) * up
        o_ref[...] += _dot(act.astype(BF16), wd_ref[...])
        gate_ref[...] = gate.astype(BF16)
        up_ref[...] = up.astype(BF16)

    row = pl.BlockSpec((TM, D), lambda i, s: (i, 0))
    hid = pl.BlockSpec((None, TM, FF_SHARD), lambda i, s: (s, i, 0))
    return pl.pallas_call(
        body, name=name, grid=(S // TM, N_CHIPS),
        in_specs=[row, pl.BlockSpec((1, D), lambda i, s: (0, 0)),
                  pl.BlockSpec((None, D, FF_SHARD), lambda i, s: (s, layer, 0)),
                  pl.BlockSpec((None, D, FF_SHARD), lambda i, s: (s, layer, 0)),
                  pl.BlockSpec((None, FF_SHARD, D), lambda i, s: (s, layer, 0))],
        out_specs=[row, hid, hid],
        out_shape=[jax.ShapeDtypeStruct((S, D), F32), jax.ShapeDtypeStruct((N_CHIPS, S, FF_SHARD), BF16),
                   jax.ShapeDtypeStruct((N_CHIPS, S, FF_SHARD), BF16)],
        scratch_shapes=[pltpu.VMEM((TM, D), BF16)],
        compiler_params=_params("parallel", "arbitrary"),
    )(h, g, wg, wu, wd)


def _ffn_bwd(name, dh, h, g, gate, up, wg, wu, wd, layer):
    def body(dh_ref, h_ref, g_ref, gate_ref, up_ref, wg_ref, wu_ref, wd_ref,
             dhin_ref, dg_ref, u_ref, dgate_ref, dup_ref, act_ref, dhb_scr, du_scr):
        i = pl.program_id(0)
        s = pl.program_id(1)

        @pl.when(s == 0)
        def _():
            xv = h_ref[...]
            r = lax.rsqrt(jnp.mean(xv * xv, axis=-1, keepdims=True) + RMS_EPS)
            u_ref[...] = (xv * r * g_ref[...]).astype(BF16)
            dhb_scr[...] = dh_ref[...].astype(BF16)
            du_scr[...] = jnp.zeros_like(du_scr)

        dact = _dot(dhb_scr[...], wd_ref[...], "nt")
        gv = gate_ref[...].astype(F32)
        uv = up_ref[...].astype(F32)
        sig = _sigmoid(gv)
        sil = gv * sig
        dup = dact * sil
        dgate = dact * uv * (sig * (1.0 + gv * (1.0 - sig)))
        dgb = dgate.astype(BF16)
        dub = dup.astype(BF16)
        act_ref[...] = (sil * uv).astype(BF16)
        dgate_ref[...] = dgb
        dup_ref[...] = dub
        du_scr[...] += _dot(dgb, wg_ref[...], "nt") + _dot(dub, wu_ref[...], "nt")

        @pl.when(s == N_CHIPS - 1)
        def _():
            xv = h_ref[...]
            r = lax.rsqrt(jnp.mean(xv * xv, axis=-1, keepdims=True) + RMS_EPS)
            xh = xv * r
            du = du_scr[...]
            dxh = du * g_ref[...]
            dhin_ref[...] = dh_ref[...] + r * (dxh - xh * jnp.mean(dxh * xh, axis=-1, keepdims=True))
            part = jnp.sum(du * xh, axis=0, keepdims=True)

            @pl.when(i == 0)
            def _():
                dg_ref[...] = part

            @pl.when(i > 0)
            def _():
                dg_ref[...] += part

    row = pl.BlockSpec((TM, D), lambda i, s: (i, 0))
    vec = pl.BlockSpec((1, D), lambda i, s: (0, 0))
    hid = pl.BlockSpec((None, TM, FF_SHARD), lambda i, s: (s, i, 0))
    hid_shape = jax.ShapeDtypeStruct((N_CHIPS, S, FF_SHARD), BF16)
    return pl.pallas_call(
        body, name=name, grid=(S // TM, N_CHIPS),
        in_specs=[row, row, vec, hid, hid,
                  pl.BlockSpec((None, D, FF_SHARD), lambda i, s: (s, layer, 0)),
                  pl.BlockSpec((None, D, FF_SHARD), lambda i, s: (s, layer, 0)),
                  pl.BlockSpec((None, FF_SHARD, D), lambda i, s: (s, layer, 0))],
        out_specs=[row, vec, row, hid, hid, hid],
        out_shape=[jax.ShapeDtypeStruct((S, D), F32), jax.ShapeDtypeStruct((1, D), F32),
                   jax.ShapeDtypeStruct((S, D), BF16), hid_shape, hid_shape, hid_shape],
        scratch_shapes=[pltpu.VMEM((TM, D), BF16), pltpu.VMEM((TM, D), F32)],
        compiler_params=_params("arbitrary", "arbitrary"),
    )(dh, h, g, gate, up, wg, wu, wd)


def _ffn_wgrads(name, u, dgate, dup, act, dh):
    nk = S // TM
    tok = lambda s, k: (k, 0)
    hid = pl.BlockSpec((None, TM, FF_SHARD), lambda s, k: (s, k, 0))
    common = dict(kind="tn", grid=(N_CHIPS, nk), out_dtype=BF16)
    d_wg = _mm(name + "_g", u, dgate, a_spec=pl.BlockSpec((TM, D), tok), b_spec=hid,
               o_spec=pl.BlockSpec((None, D, FF_SHARD), lambda s, k: (s, 0, 0)),
               out_shape=(N_CHIPS, D, FF_SHARD), acc_shape=(D, FF_SHARD), **common)
    d_wu = _mm(name + "_u", u, dup, a_spec=pl.BlockSpec((TM, D), tok), b_spec=hid,
               o_spec=pl.BlockSpec((None, D, FF_SHARD), lambda s, k: (s, 0, 0)),
               out_shape=(N_CHIPS, D, FF_SHARD), acc_shape=(D, FF_SHARD), **common)
    d_wd = _mm(name + "_d", act, dh, a_spec=hid, b_spec=pl.BlockSpec((TM, D), tok),
               o_spec=pl.BlockSpec((None, FF_SHARD, D), lambda s, k: (s, 0, 0)),
               out_shape=(N_CHIPS, FF_SHARD, D), acc_shape=(FF_SHARD, D), **common)
    return d_wg, d_wu, d_wd


def _rope_tables():
    pos = jnp.arange(S, dtype=F32)
    inv = ROPE_THETA ** (-jnp.arange(0, MLA_ROPE, 2, dtype=F32) / MLA_ROPE)
    ang = pos[:, None] * inv[None, :]
    half = MLA_ROPE // 2
    cos = jnp.cos(ang)
    sin = jnp.sin(ang)
    one = jnp.ones((S, KR_LANE), F32)
    zero = jnp.zeros((S, KR_LANE), F32)
    tail_one = jnp.ones((S, LANES - KR_LANE - MLA_ROPE), F32)
    tail_zero = jnp.zeros((S, LANES - KR_LANE - MLA_ROPE), F32)
    cos_t = jnp.concatenate([one, cos, cos, tail_one], axis=1)
    sin_t = jnp.concatenate([zero, -sin, sin, tail_zero], axis=1)
    assert cos_t.shape == (S, LANES) and half * 2 == MLA_ROPE
    return cos_t, sin_t


def _rope(x, cos_t, sin_t, sign):
    n = x.shape[1] // LANES
    half = MLA_ROPE // 2
    lane = _iota(x.shape, 1) & (LANES - 1)
    first = (lane >= KR_LANE) & (lane < KR_LANE + half)
    swapped = jnp.where(first, pltpu.roll(x, x.shape[1] - half, 1), pltpu.roll(x, half, 1))
    c = jnp.tile(cos_t, (1, n)) if n > 1 else cos_t
    s = jnp.tile(sin_t, (1, n)) if n > 1 else sin_t
    return x * c + swapped * (s * sign)


def _mla_prep_fwd(name, proj, g_cq, g_ckv, w_uq, w_uk, w_uv, cos_t, sin_t):
    nh = MLA_H * LANES

    def body(cq_ref, ckv_ref, kr_ref, gq_ref, gkv_ref, wq_ref, wk_ref, wv_ref, cos_ref, sin_ref,
             qa_ref, ka_ref, va_ref):
        cos_v, sin_v = cos_ref[...], sin_ref[...]
        cq = cq_ref[...]
        r = lax.rsqrt(jnp.mean(cq * cq, axis=-1, keepdims=True) + RMS_EPS)
        cqn = (cq * r * gq_ref[...]).astype(BF16)
        qa_ref[...] = _rope(_dot(cqn, wq_ref[...]), cos_v, sin_v, 1.0).astype(BF16)
        ckv = ckv_ref[...]
        r = lax.rsqrt(jnp.mean(ckv * ckv, axis=-1, keepdims=True) + RMS_EPS)
        ckvn = (ckv * r * gkv_ref[...]).astype(BF16)
        lane = _iota((TM, LANES), 1)
        rot = (lane >= KR_LANE) & (lane < KR_LANE + MLA_ROPE)
        kr = jnp.where(rot, _rope(kr_ref[...], cos_v, sin_v, 1.0), 0.0)
        ka_ref[...] = (_dot(ckvn, wk_ref[...]) + jnp.tile(kr, (1, MLA_H))).astype(BF16)
        va_ref[...] = _dot(ckvn, wv_ref[...]).astype(BF16)

    full = lambda shape: pl.BlockSpec(shape, lambda i: (0, 0))
    return pl.pallas_call(
        body, name=name, grid=(S // TM,),
        in_specs=[pl.BlockSpec((TM, Q_LORA), lambda i: (i, P_CQ // Q_LORA)),
                  pl.BlockSpec((TM, KV_LORA), lambda i: (i, P_CKV // KV_LORA)),
                  pl.BlockSpec((TM, LANES), lambda i: (i, P_KR // LANES)),
                  full((1, Q_LORA)), full((1, KV_LORA)), full((Q_LORA, nh)), full((KV_LORA, nh)),
                  full((KV_LORA, MLA_H * MLA_V)),
                  pl.BlockSpec((TM, LANES), lambda i: (i, 0)), pl.BlockSpec((TM, LANES), lambda i: (i, 0))],
        out_specs=[pl.BlockSpec((TM, nh), lambda i: (i, 0)), pl.BlockSpec((TM, nh), lambda i: (i, 0)),
                   pl.BlockSpec((TM, MLA_H * MLA_V), lambda i: (i, 0))],
        out_shape=[jax.ShapeDtypeStruct((S, nh), BF16), jax.ShapeDtypeStruct((S, nh), BF16),
                   jax.ShapeDtypeStruct((S, MLA_H * MLA_V), BF16)],
        compiler_params=_params("parallel"),
    )(proj, proj, proj, g_cq, g_ckv, w_uq, w_uk, w_uv, cos_t, sin_t)


def _mla_prep_bwd(name, dqa, dka, dva, proj, g_cq, g_ckv, w_uq, w_uk, w_uv, cos_t, sin_t):
    nh = MLA_H * LANES

    def body(dqa_ref, dka_ref, dva_ref, cq_ref, ckv_ref, gq_ref, gkv_ref, wq_ref, wk_ref, wv_ref, cos_ref, sin_ref,
             dcq_ref, dckv_ref, dkr_ref, dwq_ref, dwk_ref, dwv_ref, dgq_ref, dgkv_ref):
        i = pl.program_id(0)
        cos_v, sin_v = cos_ref[...], sin_ref[...]

        def norm_bwd(x, g, dn):
            r = lax.rsqrt(jnp.mean(x * x, axis=-1, keepdims=True) + RMS_EPS)
            xh = x * r
            dxh = dn * g
            dx = r * (dxh - xh * jnp.mean(dxh * xh, axis=-1, keepdims=True))
            return dx, jnp.sum(dn * xh, axis=0, keepdims=True), (xh * g).astype(BF16)

        dq = _rope(dqa_ref[...], cos_v, sin_v, -1.0).astype(BF16)
        dcqn = _dot(dq, wq_ref[...], "nt")
        dcq, dgq, cqn = norm_bwd(cq_ref[...], gq_ref[...], dcqn)
        dcq_ref[...] = dcq.astype(BF16)
        dwq = _dot(cqn, dq, "tn")

        dka = dka_ref[...]
        dkab = dka.astype(BF16)
        dvab = dva_ref[...].astype(BF16)
        dckvn = _dot(dkab, wk_ref[...], "nt") + _dot(dvab, wv_ref[...], "nt")
        dckv, dgkv, ckvn = norm_bwd(ckv_ref[...], gkv_ref[...], dckvn)
        dckv_ref[...] = dckv.astype(BF16)
        dwk = _dot(ckvn, dkab, "tn")
        dwv = _dot(ckvn, dvab, "tn")

        fold = dka[:, 0:LANES]
        for hh in range(1, MLA_H):
            fold = fold + dka[:, hh * LANES:(hh + 1) * LANES]
        lane = _iota((TM, LANES), 1)
        rot = (lane >= KR_LANE) & (lane < KR_LANE + MLA_ROPE)
        dkr = _rope(jnp.where(rot, fold, 0.0), cos_v, sin_v, -1.0)
        dkr_ref[...] = jnp.where(rot, dkr, 0.0).astype(BF16)

        @pl.when(i == 0)
        def _():
            dwq_ref[...] = dwq
            dwk_ref[...] = dwk
            dwv_ref[...] = dwv
            dgq_ref[...] = dgq
            dgkv_ref[...] = dgkv

        @pl.when(i > 0)
        def _():
            dwq_ref[...] += dwq
            dwk_ref[...] += dwk
            dwv_ref[...] += dwv
            dgq_ref[...] += dgq
            dgkv_ref[...] += dgkv

    full = lambda shape: pl.BlockSpec(shape, lambda i: (0, 0))
    rows = lambda c: pl.BlockSpec((TM, c), lambda i: (i, 0))
    nv = MLA_H * MLA_V
    return pl.pallas_call(
        body, name=name, grid=(S // TM,),
        in_specs=[rows(nh), rows(nh), rows(nv),
                  pl.BlockSpec((TM, Q_LORA), lambda i: (i, P_CQ // Q_LORA)),
                  pl.BlockSpec((TM, KV_LORA), lambda i: (i, P_CKV // KV_LORA)),
                  full((1, Q_LORA)), full((1, KV_LORA)), full((Q_LORA, nh)), full((KV_LORA, nh)), full((KV_LORA, nv)),
                  rows(LANES), rows(LANES)],
        out_specs=[rows(Q_LORA), rows(KV_LORA), rows(LANES), full((Q_LORA, nh)), full((KV_LORA, nh)),
                   full((KV_LORA, nv)), full((1, Q_LORA)), full((1, KV_LORA))],
        out_shape=[jax.ShapeDtypeStruct((S, Q_LORA), BF16), jax.ShapeDtypeStruct((S, KV_LORA), BF16),
                   jax.ShapeDtypeStruct((S, LANES), BF16), jax.ShapeDtypeStruct((Q_LORA, nh), F32),
                   jax.ShapeDtypeStruct((KV_LORA, nh), F32), jax.ShapeDtypeStruct((KV_LORA, nv), F32),
                   jax.ShapeDtypeStruct((1, Q_LORA), F32), jax.ShapeDtypeStruct((1, KV_LORA), F32)],
        compiler_params=_params("arbitrary"),
    )(dqa, dka, dva, proj, proj, g_cq, g_ckv, w_uq, w_uk, w_uv, cos_t, sin_t)


def _head_masks(dtype):
    lane = _iota((1, LANES), 1)
    return (lane < 64).astype(dtype), (lane >= 64).astype(dtype)


def _mla_fwd(name, qa, ka, va):
    def body(q_ref, k_ref, v_ref, o_ref, lse_ref):
        m0b, m1b = _head_masks(BF16)
        lane = _iota((QB, LANES), 1)
        left = lane < 64

        def qblock(i, _):
            r0 = pl.multiple_of(i * QB, QB)
            qs = [q_ref[pl.ds(r0, QB), hh * LANES:(hh + 1) * LANES] for hh in range(2)]
            rowc = lax.shift_right_logical(r0 + _iota((QB, QB), 0), 6)

            def kv(kb, carry):
                ms, ls, acc = carry
                c0 = pl.multiple_of(kb * QB, QB)
                v = v_ref[pl.ds(c0, QB), :]
                ok = lax.shift_right_logical(c0 + _iota((QB, QB), 1), 6) <= rowc
                new_m, new_l, alphas = [], [], []
                pv = None
                for hh in range(2):
                    k = k_ref[pl.ds(c0, QB), hh * LANES:(hh + 1) * LANES]
                    s = jnp.where(ok, _dot(qs[hh], k, "nt") * SCALE_A, NEG)
                    mn = jnp.maximum(ms[hh], jnp.max(s, axis=-1, keepdims=True))
                    p = jnp.exp(s - mn)
                    a = jnp.exp(ms[hh] - mn)
                    new_m.append(mn)
                    new_l.append(a * ls[hh] + jnp.sum(p, axis=-1, keepdims=True))
                    alphas.append(a)
                    part = _dot(p.astype(BF16), v * (m0b if hh == 0 else m1b))
                    pv = part if pv is None else pv + part
                acc = acc * jnp.where(left, alphas[0], alphas[1]) + pv
                return tuple(new_m), tuple(new_l), acc

            init = ((jnp.full((QB, 1), NEG, F32),) * 2, (jnp.zeros((QB, 1), F32),) * 2, jnp.zeros((QB, LANES), F32))
            ms, ls, acc = lax.fori_loop(0, i + 1, kv, init)
            o_ref[pl.ds(r0, QB), :] = acc * jnp.where(left, 1.0 / ls[0], 1.0 / ls[1])
            lse_ref[pl.ds(r0, QB), :] = jnp.where(left, ms[0] + jnp.log(ls[0]), ms[1] + jnp.log(ls[1]))
            return 0

        lax.fori_loop(0, S // QB, qblock, 0)

    pair = lambda w: pl.BlockSpec((S, w), lambda p: (0, p))
    return pl.pallas_call(
        body, name=name, grid=(MLA_H // 2,), in_specs=[pair(2 * LANES), pair(2 * LANES), pair(LANES)],
        out_specs=[pair(LANES), pair(LANES)],
        out_shape=[jax.ShapeDtypeStruct((S, MLA_H * MLA_V), F32), jax.ShapeDtypeStruct((S, MLA_H * MLA_V), F32)],
        compiler_params=_params("parallel"),
    )(qa, ka, va)


def _mla_bwd(name, qa, ka, va, o, lse, do, do_block0):
    def body(q_ref, k_ref, v_ref, o_ref, lse_ref, do_ref, dq_ref, dk_ref, dv_ref):
        m0f, m1f = _head_masks(F32)
        m0b, m1b = _head_masks(BF16)
        dk_ref[...] = jnp.zeros_like(dk_ref)
        dv_ref[...] = jnp.zeros_like(dv_ref)

        def qblock(i, _):
            r0 = pl.multiple_of(i * QB, QB)
            rows = pl.ds(r0, QB)
            do_f = do_ref[rows, :]
            prod = do_f * o_ref[rows, :]
            deltas = [jnp.sum(prod * m0f, axis=-1, keepdims=True), jnp.sum(prod * m1f, axis=-1, keepdims=True)]
            lse_v = lse_ref[rows, :]
            lses = [lse_v[:, 0:1], lse_v[:, 64:65]]
            dob = do_f.astype(BF16)
            dos = [dob * m0b, dob * m1b]
            qs = [q_ref[rows, hh * LANES:(hh + 1) * LANES] for hh in range(2)]
            rowc = lax.shift_right_logical(r0 + _iota((QB, QB), 0), 6)

            def kv(kb, dqs):
                c0 = pl.multiple_of(kb * QB, QB)
                cols = pl.ds(c0, QB)
                v = v_ref[cols, :]
                ok = lax.shift_right_logical(c0 + _iota((QB, QB), 1), 6) <= rowc
                out = []
                dv = None
                for hh in range(2):
                    k = k_ref[cols, hh * LANES:(hh + 1) * LANES]
                    s = _dot(qs[hh], k, "nt") * SCALE_A
                    p = jnp.where(ok, jnp.exp(s - lses[hh]), 0.0)
                    dp = _dot(dos[hh], v, "nt")
                    ds = (p * (dp - deltas[hh]) * SCALE_A).astype(BF16)
                    out.append(dqs[hh] + _dot(ds, k))
                    dk_ref[cols, hh * LANES:(hh + 1) * LANES] += _dot(ds, qs[hh], "tn")
                    part = _dot(p.astype(BF16), dos[hh], "tn")
                    dv = part if dv is None else dv + part
                dv_ref[cols, :] += dv
                return tuple(out)

            dqs = lax.fori_loop(0, i + 1, kv, (jnp.zeros((QB, LANES), F32),) * 2)
            for hh in range(2):
                dq_ref[rows, hh * LANES:(hh + 1) * LANES] = dqs[hh]
            return 0

        lax.fori_loop(0, S // QB, qblock, 0)

    pair = lambda w: pl.BlockSpec((S, w), lambda p: (0, p))
    return pl.pallas_call(
        body, name=name, grid=(MLA_H // 2,),
        in_specs=[pair(2 * LANES), pair(2 * LANES), pair(LANES), pair(LANES), pair(LANES),
                  pl.BlockSpec((S, LANES), lambda p: (0, do_block0 + p))],
        out_specs=[pair(2 * LANES), pair(2 * LANES), pair(LANES)],
        out_shape=[jax.ShapeDtypeStruct((S, MLA_H * LANES), F32), jax.ShapeDtypeStruct((S, MLA_H * LANES), F32),
                   jax.ShapeDtypeStruct((S, MLA_H * MLA_V), F32)],
        compiler_params=_params("parallel"),
    )(qa, ka, va, o, lse, do)


def _sb_weights(q_h, k, c, before, tri_suffix):
    z = _dot(q_h, k, "nt") * SCALE_B
    sp = _softplus(z)
    log_keep = jnp.where(before, -sp, 0.0)
    log_between = _split_dot(log_keep, tri_suffix) + c
    w = jnp.where(before, jnp.exp(z - sp + log_between), 0.0)
    return w, jnp.exp(z - sp), jnp.sum(log_keep, axis=-1, keepdims=True)


def _sb_fwd(name, proj):
    def body(q_ref, k_ref, v_ref, o_ref):
        m0b, m1b = _head_masks(BF16)
        tri_suffix = (_iota((QB, QB), 0) > _iota((QB, QB), 1)).astype(BF16)

        def qblock(i, _):
            r0 = pl.multiple_of(i * QB, QB)
            q = q_ref[pl.ds(r0, QB), :].astype(BF16)
            qs = [q * m0b, q * m1b]
            rowg = r0 + _iota((QB, QB), 0)

            def kv(step, carry):
                cs, acc = carry
                c0 = pl.multiple_of((i - step) * QB, QB)
                k = k_ref[pl.ds(c0, QB), :].astype(BF16)
                v = v_ref[pl.ds(c0, QB), :].astype(BF16)
                before = (c0 + _iota((QB, QB), 1)) < rowg
                new_c = []
                for hh in range(2):
                    w, _, tot = _sb_weights(qs[hh], k, cs[hh], before, tri_suffix)
                    new_c.append(cs[hh] + tot)
                    acc = acc + _dot(w.astype(BF16), v * (m0b if hh == 0 else m1b))
                return tuple(new_c), acc

            init = ((jnp.zeros((QB, 1), F32),) * 2, jnp.zeros((QB, LANES), F32))
            _, acc = lax.fori_loop(0, i + 1, kv, init)
            o_ref[pl.ds(r0, QB), :] = acc.astype(BF16)
            return 0

        lax.fori_loop(0, S // QB, qblock, 0)

    col = lambda base: pl.BlockSpec((S, LANES), lambda p: (0, base // LANES + p))
    return pl.pallas_call(
        body, name=name, grid=(SB_H // 2,), in_specs=[col(P_QB), col(P_KB), col(P_VB)],
        out_specs=pl.BlockSpec((S, LANES), lambda p: (0, p)),
        out_shape=jax.ShapeDtypeStruct((S, SB_H * SB_DIM), BF16),
        compiler_params=_params("parallel"),
    )(proj, proj, proj)


def _sb_bwd(name, proj, do, do_block0):
    nb = S // QB

    def body(q_ref, k_ref, v_ref, do_ref, dq_ref, dk_ref, dv_ref, sig_scr, dl_scr, dk_acc, dv_acc):
        m0b, m1b = _head_masks(BF16)
        tri_suffix = (_iota((QB, QB), 0) > _iota((QB, QB), 1)).astype(BF16)
        tri_prefix = (_iota((QB, QB), 0) < _iota((QB, QB), 1)).astype(BF16)
        dk_acc[...] = jnp.zeros_like(dk_acc)
        dv_acc[...] = jnp.zeros_like(dv_acc)

        def qblock(i, _):
            r0 = pl.multiple_of(i * QB, QB)
            rows = pl.ds(r0, QB)
            q = q_ref[rows, :].astype(BF16)
            qs = [q * m0b, q * m1b]
            dob = do_ref[rows, :].astype(BF16)
            dos = [dob * m0b, dob * m1b]
            rowg = r0 + _iota((QB, QB), 0)

            def sweep_left(step, cs):
                kb = i - step
                c0 = pl.multiple_of(kb * QB, QB)
                cols = pl.ds(c0, QB)
                k = k_ref[cols, :].astype(BF16)
                v = v_ref[cols, :].astype(BF16)
                before = (c0 + _iota((QB, QB), 1)) < rowg
                new_c = []
                dv = None
                for hh in range(2):
                    w, sig, tot = _sb_weights(qs[hh], k, cs[hh], before, tri_suffix)
                    new_c.append(cs[hh] + tot)
                    sig_scr[hh, kb] = sig
                    dl_scr[hh, kb] = _dot(dos[hh], v, "nt") * w
                    part = _dot(w.astype(BF16), dos[hh], "tn")
                    dv = part if dv is None else dv + part
                dv_acc[cols, :] += dv
                return tuple(new_c)

            lax.fori_loop(0, i + 1, sweep_left, (jnp.zeros((QB, 1), F32),) * 2)

            def sweep_right(kb, carry):
                ps, dq = carry
                c0 = pl.multiple_of(kb * QB, QB)
                cols = pl.ds(c0, QB)
                k = k_ref[cols, :].astype(BF16)
                before = (c0 + _iota((QB, QB), 1)) < rowg
                new_p = []
                dk = None
                for hh in range(2):
                    dl = dl_scr[hh, kb]
                    sig = sig_scr[hh, kb]
                    earlier = _split_dot(dl, tri_prefix) + ps[hh]
                    new_p.append(ps[hh] + jnp.sum(dl, axis=-1, keepdims=True))
                    dz = (jnp.where(before, dl * (1.0 - sig) - earlier * sig, 0.0) * SCALE_B).astype(BF16)
                    dq = dq + _dot(dz, k * (m0b if hh == 0 else m1b))
                    part = _dot(dz, qs[hh], "tn")
                    dk = part if dk is None else dk + part
                dk_acc[cols, :] += dk
                return tuple(new_p), dq

            init = ((jnp.zeros((QB, 1), F32),) * 2, jnp.zeros((QB, LANES), F32))
            _, dq = lax.fori_loop(0, i + 1, sweep_right, init)
            dq_ref[rows, :] = dq.astype(BF16)
            return 0

        lax.fori_loop(0, nb, qblock, 0)
        dk_ref[...] = dk_acc[...].astype(BF16)
        dv_ref[...] = dv_acc[...].astype(BF16)

    col = lambda base: pl.BlockSpec((S, LANES), lambda p: (0, base // LANES + p))
    out = pl.BlockSpec((S, LANES), lambda p: (0, p))
    shape = jax.ShapeDtypeStruct((S, SB_H * SB_DIM), BF16)
    return pl.pallas_call(
        body, name=name, grid=(SB_H // 2,),
        in_specs=[col(P_QB), col(P_KB), col(P_VB), pl.BlockSpec((S, LANES), lambda p: (0, do_block0 + p))],
        out_specs=[out, out, out], out_shape=[shape, shape, shape],
        scratch_shapes=[pltpu.VMEM((2, nb, QB, QB), F32), pltpu.VMEM((2, nb, QB, QB), F32),
                        pltpu.VMEM((S, LANES), F32), pltpu.VMEM((S, LANES), F32)],
        compiler_params=_params("parallel"),
    )(proj, proj, proj, do)


def _band_row_index():
    j = np.arange(TOEP_W)
    rel = np.clip(LEFT_CHUNKS * CHUNK - j, -REL_CLIP, REL_CLIP) + REL_CLIP
    rel[BAND_W:] = 2 * REL_CLIP
    return rel.astype(np.int32)


def _band_tiles(r0_ref, q_ref, kpad, vpad, m, m0b, m1b, static_ok, bias):
    r0 = pl.multiple_of(m * QB, QB)
    q = q_ref[0, pl.ds(r0, QB), :]
    kw = kpad[pl.ds(r0, BAND_W), :]
    vw = vpad[pl.ds(r0, BAND_W), :]
    ok = static_ok & ((r0 - BAND_PAD + _iota((QB, BAND_W), 1)) >= 0)
    qs = [q * m0b, q * m1b]
    ps = []
    for hh in range(2):
        s = jnp.where(ok, _dot(qs[hh], kw, "nt") * SCALE_C + bias[hh], NEG)
        e = jnp.exp(s - jnp.max(s, axis=-1, keepdims=True))
        ps.append(e * (1.0 / jnp.sum(e, axis=-1, keepdims=True)))
    return r0, qs, kw, vw, ps


def _band_setup(qkv_ref, r0_ref, kpad, vpad):
    kpad[0:BAND_PAD, :] = jnp.zeros((BAND_PAD, LANES), BF16)
    vpad[0:BAND_PAD, :] = jnp.zeros((BAND_PAD, LANES), BF16)
    kpad[BAND_PAD:, :] = qkv_ref[1]
    vpad[BAND_PAD:, :] = qkv_ref[2]
    jc = lax.shift_right_logical(_iota((QB, BAND_W), 1), 6)
    rc = lax.shift_right_logical(_iota((QB, BAND_W), 0), 6)
    static_ok = (jc >= rc) & (jc <= rc + LEFT_CHUNKS)
    bias = []
    for hh in range(2):
        row = jnp.broadcast_to(r0_ref[hh:hh + 1, :], (QB, TOEP_W))
        bias.append(pltpu.roll(row, 0, 1, stride=1, stride_axis=0)[:, :BAND_W])
    return static_ok, bias


def _band_fwd(name, qkv, r0):
    def body(qkv_ref, r0_ref, o_ref, kpad, vpad):
        m0b, m1b = _head_masks(BF16)
        static_ok, bias = _band_setup(qkv_ref, r0_ref, kpad, vpad)

        def qblock(m, _):
            r0_, _, _, vw, ps = _band_tiles(r0_ref, qkv_ref, kpad, vpad, m, m0b, m1b, static_ok, bias)
            o = _dot(ps[0].astype(BF16), vw * m0b) + _dot(ps[1].astype(BF16), vw * m1b)
            o_ref[pl.ds(r0_, QB), :] = o.astype(BF16)
            return 0

        lax.fori_loop(0, S // QB, qblock, 0)

    return pl.pallas_call(
        body, name=name, grid=(C_H // 2,),
        in_specs=[pl.BlockSpec((3, S, LANES), lambda p: (0, 0, p)), pl.BlockSpec((None, 2, TOEP_W), lambda p: (p, 0, 0))],
        out_specs=pl.BlockSpec((S, LANES), lambda p: (0, p)),
        out_shape=jax.ShapeDtypeStruct((S, C_H * C_DIM), BF16),
        scratch_shapes=[pltpu.VMEM((S + BAND_PAD, LANES), BF16), pltpu.VMEM((S + BAND_PAD, LANES), BF16)],
        compiler_params=_params("parallel"),
    )(qkv, r0)


def _band_bwd(name, qkv, r0, do):
    def body(qkv_ref, r0_ref, do_ref, dqkv_ref, dr0_ref, kpad, vpad, dkpad, dvpad, db_acc):
        m0b, m1b = _head_masks(BF16)
        static_ok, bias = _band_setup(qkv_ref, r0_ref, kpad, vpad)
        dkpad[...] = jnp.zeros_like(dkpad)
        dvpad[...] = jnp.zeros_like(dvpad)
        db_acc[...] = jnp.zeros_like(db_acc)

        def qblock(m, _):
            r0_, qs, kw, vw, ps = _band_tiles(r0_ref, qkv_ref, kpad, vpad, m, m0b, m1b, static_ok, bias)
            dob = do_ref[pl.ds(r0_, QB), :].astype(BF16)
            dos = [dob * m0b, dob * m1b]
            dq = None
            dk = None
            dv = None
            for hh in range(2):
                p = ps[hh]
                dp = _dot(dos[hh], vw, "nt")
                ds = p * (dp - jnp.sum(dp * p, axis=-1, keepdims=True))
                db_acc[hh, :, 0:BAND_W] += ds
                dsb = (ds * SCALE_C).astype(BF16)
                t = _dot(dsb, kw * (m0b if hh == 0 else m1b))
                dq = t if dq is None else dq + t
                t = _dot(dsb, qs[hh], "tn")
                dk = t if dk is None else dk + t
                t = _dot(p.astype(BF16), dos[hh], "tn")
                dv = t if dv is None else dv + t
            dqkv_ref[0, pl.ds(r0_, QB), :] = dq.astype(BF16)
            dkpad[pl.ds(r0_, BAND_W), :] += dk
            dvpad[pl.ds(r0_, BAND_W), :] += dv
            return 0

        lax.fori_loop(0, S // QB, qblock, 0)
        dqkv_ref[1] = dkpad[BAND_PAD:, :].astype(BF16)
        dqkv_ref[2] = dvpad[BAND_PAD:, :].astype(BF16)
        row = _iota((QB, TOEP_W), 0)
        for hh in range(2):
            unrolled = db_acc[hh]
            for bit in range(QB.bit_length() - 1):
                moved = pltpu.roll(unrolled, TOEP_W - (1 << bit), 1)
                unrolled = jnp.where((row & (1 << bit)) != 0, moved, unrolled)
            dr0_ref[hh:hh + 1, :] = jnp.sum(unrolled, axis=0, keepdims=True)

    return pl.pallas_call(
        body, name=name, grid=(C_H // 2,),
        in_specs=[pl.BlockSpec((3, S, LANES), lambda p: (0, 0, p)), pl.BlockSpec((None, 2, TOEP_W), lambda p: (p, 0, 0)),
                  pl.BlockSpec((S, LANES), lambda p: (0, p))],
        out_specs=[pl.BlockSpec((3, S, LANES), lambda p: (0, 0, p)), pl.BlockSpec((None, 2, TOEP_W), lambda p: (p, 0, 0))],
        out_shape=[jax.ShapeDtypeStruct((3, S, C_H * C_DIM), BF16), jax.ShapeDtypeStruct((C_H // 2, 2, TOEP_W), F32)],
        scratch_shapes=[pltpu.VMEM((S + BAND_PAD, LANES), BF16), pltpu.VMEM((S + BAND_PAD, LANES), BF16),
                        pltpu.VMEM((S + BAND_PAD, LANES), F32), pltpu.VMEM((S + BAND_PAD, LANES), F32),
                        pltpu.VMEM((2, QB, TOEP_W), F32)],
        compiler_params=_params("parallel"),
    )(qkv, r0, do)


def _bias_table_grad(name, dr0):
    w_out = 5 * LANES

    def body(d_ref, o_ref):
        j = _iota((TOEP_W, w_out), 0)
        rel = jnp.clip(LEFT_CHUNKS * CHUNK - j, -REL_CLIP, REL_CLIP) + REL_CLIP
        rel = jnp.where(j >= BAND_W, 2 * REL_CLIP, rel)
        onehot = (rel == _iota((TOEP_W, w_out), 1)).astype(BF16)
        d = d_ref[...]
        hi = d.astype(BF16)
        mid = (d - hi.astype(F32))
        mid_b = mid.astype(BF16)
        lo = (mid - mid_b.astype(F32)).astype(BF16)
        o_ref[...] = _dot(hi, onehot) + _dot(mid_b, onehot) + _dot(lo, onehot)

    return pl.pallas_call(
        body, name=name, out_shape=jax.ShapeDtypeStruct((C_H, w_out), F32),
        in_specs=[pl.BlockSpec((C_H, TOEP_W), lambda: (0, 0))], out_specs=pl.BlockSpec((C_H, w_out), lambda: (0, 0)),
        grid=(),
    )(dr0)


def _dense_weights(gw):
    w_in = jnp.moveaxis(gw["ev_w_in"], 0, 1).reshape(D, EVEN_IN)
    z = lambda n: jnp.zeros((D, n), BF16)
    w_in_p = jnp.concatenate(
        [w_in[:, 0:384], z(128), w_in[:, 384:640], w_in[:, 672:2208], z(KR_LANE), w_in[:, 640:672],
         z(LANES - KR_LANE - MLA_ROPE)], axis=1)
    w_uq = jnp.moveaxis(gw["ev_w_uq"], 0, 1).reshape(Q_LORA, MLA_H, MLA_NOPE + MLA_ROPE)
    w_uq_p = jnp.concatenate([w_uq, jnp.zeros((Q_LORA, MLA_H, LANES - MLA_NOPE - MLA_ROPE), BF16)], axis=2)
    w_ukv = jnp.moveaxis(gw["ev_w_ukv"], 0, 1).reshape(KV_LORA, MLA_H, MLA_NOPE + MLA_V)
    w_uk_p = jnp.concatenate([w_ukv[:, :, :MLA_NOPE], jnp.zeros((KV_LORA, MLA_H, LANES - MLA_NOPE), BF16)], axis=2)
    return dict(
        w_in=w_in_p, w_uq=w_uq_p.reshape(Q_LORA, MLA_H * LANES), w_uk=w_uk_p.reshape(KV_LORA, MLA_H * LANES),
        w_uv=w_ukv[:, :, MLA_NOPE:].reshape(KV_LORA, MLA_H * MLA_V),
        ev_w_out=gw["ev_w_out"].reshape(D, D), od_w_out=gw["od_w_out"].reshape(D, D),
        w_qkv=gw["od_w_qkv"], w_gate=gw["w_gate"], w_up=gw["w_up"], w_down=gw["w_down"])


def _proj_mm(name, u, w_in):
    return _mm(name, u, w_in, kind="nn", grid=(S // TM, 1, 1),
               a_spec=pl.BlockSpec((TM, D), lambda i, j, k: (i, 0)), b_spec=pl.BlockSpec((D, P_IN), lambda i, j, k: (0, 0)),
               o_spec=pl.BlockSpec((TM, P_IN), lambda i, j, k: (i, 0)), out_shape=(S, P_IN), out_dtype=F32, acc_shape=None)


def _out_proj(name, o, w, resid):
    return _mm(name, o, w, kind="nn", grid=(S // TM, 1, 1),
               a_spec=pl.BlockSpec((TM, D), lambda i, j, k: (i, 0)), b_spec=pl.BlockSpec((D, D), lambda i, j, k: (0, 0)),
               o_spec=pl.BlockSpec((TM, D), lambda i, j, k: (i, 0)), out_shape=(S, D), out_dtype=F32, acc_shape=None,
               resid=resid, r_spec=pl.BlockSpec((TM, D), lambda i, j, k: (i, 0)))


def _out_proj_bwd(name, dh, o, w):
    d_o = _mm(name + "_x", dh, w, kind="nt", grid=(S // TM, 1, 1),
              a_spec=pl.BlockSpec((TM, D), lambda i, j, k: (i, 0)), b_spec=pl.BlockSpec((D, D), lambda i, j, k: (0, 0)),
              o_spec=pl.BlockSpec((TM, D), lambda i, j, k: (i, 0)), out_shape=(S, D), out_dtype=F32, acc_shape=None)
    d_w = _mm(name + "_w", o, dh, kind="tn", grid=(2, S // TM),
              a_spec=pl.BlockSpec((TM, TM), lambda j, k: (k, j)), b_spec=pl.BlockSpec((TM, D), lambda j, k: (k, 0)),
              o_spec=pl.BlockSpec((TM, D), lambda j, k: (j, 0)), out_shape=(D, D), out_dtype=BF16, acc_shape=(TM, D))
    return d_o, d_w


def _local_step(x, tgt, gw, sm):
    w = _dense_weights(gw)
    cos_t, sin_t = _rope_tables()
    g_mix, g_ffn = sm["g_mix"], sm["g_ffn"]
    r0 = sm["od_rel_bias"][0][:, _band_row_index()].reshape(C_H // 2, 2, TOEP_W)
    nt = 3 * D // 256

    u0 = _rms_fwd("rms_mix0", x, g_mix[0:1])
    proj = _proj_mm("proj_in", u0, w["w_in"])
    qa, ka, va = _mla_prep_fwd("mla_prep", proj, sm["ev_g_cq"], sm["ev_g_ckv"], w["w_uq"], w["w_uk"], w["w_uv"], cos_t, sin_t)
    o_a, lse = _mla_fwd("mla_attn", qa, ka, va)
    o_b = _sb_fwd("sb_attn", proj)
    o_ev = jnp.concatenate([o_a.astype(BF16), o_b], axis=1)
    h1 = _out_proj("ev_out", o_ev, w["ev_w_out"], x)
    h2, gate0, up0 = _ffn_fwd("ffn0", h1, g_ffn[0:1], w["w_gate"], w["w_up"], w["w_down"], 0)
    u2 = _rms_fwd("rms_mix1", h2, g_mix[1:2])
    qkv = _mm("qkv", u2, w["w_qkv"], kind="nn", grid=(S // TM, nt, 1),
              a_spec=pl.BlockSpec((TM, D), lambda i, t, k: (i, 0)),
              b_spec=pl.BlockSpec((None, D, 256), lambda i, t, k: (t // 3, 0, t % 3)),
              o_spec=pl.BlockSpec((None, TM, 256), lambda i, t, k: (t // 4, i, t % 4)),
              out_shape=(3, S, D), out_dtype=BF16, acc_shape=None)
    o_od = _band_fwd("band_attn", qkv, r0)
    h3 = _out_proj("od_out", o_od, w["od_w_out"], h2)
    h4, gate1, up1 = _ffn_fwd("ffn1", h3, g_ffn[1:2], w["w_gate"], w["w_up"], w["w_down"], 1)

    loss, dh4, dg_final = _loss_bwd("loss", h4, sm["g_final"].reshape(1, D), tgt)

    dh3, dg_ffn1, u3, dgate, dup, act = _ffn_bwd("ffn1_bwd", dh4, h3, g_ffn[1:2], gate1, up1,
                                                 w["w_gate"], w["w_up"], w["w_down"], 1)
    d_wg1, d_wu1, d_wd1 = _ffn_wgrads("ffn1_dw", u3, dgate, dup, act, dh4)

    d_ood, d_w_od_out = _out_proj_bwd("od_out_bwd", dh3, o_od, w["od_w_out"])
    dqkv, dr0 = _band_bwd("band_attn_bwd", qkv, r0, d_ood)
    du2 = _mm("qkv_bwd_x", dqkv, w["w_qkv"], kind="nt", grid=(S // TM, nt),
              a_spec=pl.BlockSpec((None, TM, 256), lambda i, t: (t // 4, i, t % 4)),
              b_spec=pl.BlockSpec((None, D, 256), lambda i, t: (t // 3, 0, t % 3)),
              o_spec=pl.BlockSpec((TM, D), lambda i, t: (i, 0)), out_shape=(S, D), out_dtype=F32, acc_shape=(TM, D))
    d_w_qkv = _mm("qkv_bwd_w", u2, dqkv, kind="tn", grid=(nt, S // TM),
                  a_spec=pl.BlockSpec((TM, D), lambda t, k: (k, 0)),
                  b_spec=pl.BlockSpec((None, TM, 256), lambda t, k: (t // 4, k, t % 4)),
                  o_spec=pl.BlockSpec((None, D, 256), lambda t, k: (t // 3, 0, t % 3)),
                  out_shape=(N_CHIPS, D, 768), out_dtype=BF16, acc_shape=(D, 256))
    dh2, dg_mix1 = _rms_bwd("rms_mix1_bwd", du2, h2, g_mix[1:2], dh3)
    d_rel = _bias_table_grad("rel_bias_grad", dr0.reshape(C_H, TOEP_W))[:, :2 * REL_CLIP + 1]

    dh1, dg_ffn0, u1, dgate, dup, act = _ffn_bwd("ffn0_bwd", dh2, h1, g_ffn[0:1], gate0, up0,
                                                 w["w_gate"], w["w_up"], w["w_down"], 0)
    d_wg0, d_wu0, d_wd0 = _ffn_wgrads("ffn0_dw", u1, dgate, dup, act, dh2)

    d_oev, d_w_ev_out = _out_proj_bwd("ev_out_bwd", dh1, o_ev, w["ev_w_out"])
    dqa, dka, dva = _mla_bwd("mla_attn_bwd", qa, ka, va, o_a, lse, d_oev, 0)
    dqb, dkb, dvb = _sb_bwd("sb_attn_bwd", proj, d_oev, MLA_H * MLA_V // LANES)
    dcq, dckv, dkr, d_w_uq, d_w_uk, d_w_uv, dg_cq, dg_ckv = _mla_prep_bwd(
        "mla_prep_bwd", dqa, dka, dva, proj, sm["ev_g_cq"], sm["ev_g_ckv"], w["w_uq"], w["w_uk"], w["w_uv"], cos_t, sin_t)
    dproj = jnp.concatenate([dcq, jnp.zeros((S, LANES), BF16), dckv, dqb, dkb, dvb, dkr], axis=1)
    du0 = _mm("proj_in_bwd_x", dproj, w["w_in"], kind="nt", grid=(S // TM, 1, 1),
              a_spec=pl.BlockSpec((TM, P_IN), lambda i, j, k: (i, 0)), b_spec=pl.BlockSpec((D, P_IN), lambda i, j, k: (0, 0)),
              o_spec=pl.BlockSpec((TM, D), lambda i, j, k: (i, 0)), out_shape=(S, D), out_dtype=F32, acc_shape=None)
    d_w_in_p = _mm("proj_in_bwd_w", u0, dproj, kind="tn", grid=(1, S // TM),
                   a_spec=pl.BlockSpec((TM, D), lambda j, k: (k, 0)), b_spec=pl.BlockSpec((TM, P_IN), lambda j, k: (k, 0)),
                   o_spec=pl.BlockSpec((D, P_IN), lambda j, k: (0, 0)), out_shape=(D, P_IN), out_dtype=BF16,
                   acc_shape=(D, P_IN))
    grad_x, dg_mix0 = _rms_bwd("rms_mix0_bwd", du0, x, g_mix[0:1], dh1)

    d_w_in = jnp.concatenate([d_w_in_p[:, 0:384], d_w_in_p[:, 512:768],
                              d_w_in_p[:, P_KR + KR_LANE:P_KR + KR_LANE + MLA_ROPE], d_w_in_p[:, 768:2304]], axis=1)
    shard_cols = lambda a: jnp.moveaxis(a.reshape(a.shape[0], N_CHIPS, a.shape[1] // N_CHIPS), 1, 0)
    d_w_uq_std = d_w_uq.reshape(Q_LORA, MLA_H, LANES)[:, :, :MLA_NOPE + MLA_ROPE].reshape(Q_LORA, -1)
    d_w_ukv = jnp.concatenate([d_w_uk.reshape(KV_LORA, MLA_H, LANES)[:, :, :MLA_NOPE],
                               d_w_uv.reshape(KV_LORA, MLA_H, MLA_V)], axis=2).reshape(KV_LORA, -1)
    big = {
        "ev_w_in": shard_cols(d_w_in), "ev_w_uq": shard_cols(d_w_uq_std.astype(BF16)),
        "ev_w_ukv": shard_cols(d_w_ukv.astype(BF16)), "ev_w_out": d_w_ev_out.reshape(N_CHIPS, D // N_CHIPS, D),
        "od_w_qkv": d_w_qkv, "od_w_out": d_w_od_out.reshape(N_CHIPS, D // N_CHIPS, D),
        "w_gate0": d_wg0, "w_gate1": d_wg1, "w_up0": d_wu0, "w_up1": d_wu1, "w_down0": d_wd0, "w_down1": d_wd1,
    }
    small = {
        "ev_g_cq": dg_cq, "ev_g_ckv": dg_ckv, "od_rel_bias": d_rel.reshape(1, C_H, 2 * REL_CLIP + 1),
        "g_mix": jnp.concatenate([dg_mix0, dg_mix1], axis=0), "g_ffn": jnp.concatenate([dg_ffn0, dg_ffn1], axis=0),
        "g_final": dg_final.reshape(D),
    }
    return loss, grad_x, big, small


MESH = pl.DeviceIdType.MESH
ANY = pl.BlockSpec(memory_space=pl.ANY)
BIG = ("ev_w_in", "ev_w_uq", "ev_w_ukv", "ev_w_out", "od_w_qkv", "od_w_out", "w_gate", "w_up", "w_down")
SMALL = ("ev_g_cq", "ev_g_ckv", "od_rel_bias", "g_mix", "g_ffn", "g_final")
WEIGHTS = ("ev_w_in", "ev_g_cq", "ev_w_uq", "ev_g_ckv", "ev_w_ukv", "ev_w_out", "od_w_qkv", "od_rel_bias", "od_w_out",
           "g_mix", "g_ffn", "w_gate", "w_up", "w_down", "g_final")
GRAD_PARTS = (("ev_w_in", "ev_w_in", 0), ("ev_w_uq", "ev_w_uq", 0), ("ev_w_ukv", "ev_w_ukv", 0),
              ("ev_w_out", "ev_w_out", 0), ("od_w_qkv", "od_w_qkv", 0), ("od_w_out", "od_w_out", 0),
              ("w_gate0", "w_gate", 0), ("w_gate1", "w_gate", 1), ("w_up0", "w_up", 0), ("w_up1", "w_up", 1),
              ("w_down0", "w_down", 0), ("w_down1", "w_down", 1))
SMALL_ROWS = 112


def _row_tile(rows, cap=512):
    for t in range(min(rows, cap), 0, -1):
        if rows % t == 0 and t % 16 == 0:
            return t
    return rows


def _position():
    x, y, c = lax.axis_index("x"), lax.axis_index("y"), lax.axis_index("c")
    other_chips = [(1 - x, y), (x, 1 - y), (1 - x, 1 - y)]
    return x, y, c, other_chips


def _half_rows(c, half):
    return pl.ds(pl.multiple_of(c * half, 16), half)


def _cast_bf16(name, w):
    rows, cols = w.shape
    tr = _row_tile(rows)

    def body(w_ref, o_ref):
        o_ref[...] = w_ref[...].astype(BF16)

    spec = pl.BlockSpec((tr, cols), lambda i: (i, 0))
    return pl.pallas_call(body, name=name, grid=(rows // tr,), in_specs=[spec], out_specs=spec,
                          out_shape=jax.ShapeDtypeStruct((rows, cols), BF16), compiler_params=_params("parallel"))(w)


def _all_gather_weights(name, shards):
    n = len(shards)

    def body(*refs):
        w, g = refs[:n], refs[n:2 * n]
        send_sem, recv_sem, local_sem = refs[2 * n:]
        x, y, c, chips = _position()
        me = 2 * x + y
        sibling = (x, y, 1 - c)

        def over_ici(t, j):
            rows = _half_rows(c, shards[t].shape[0] // 2)
            return pltpu.make_async_remote_copy(
                src_ref=w[t].at[rows, :], dst_ref=g[t].at[me, rows, :], send_sem=send_sem.at[t, j],
                recv_sem=recv_sem.at[t, j], device_id=(*chips[j], c), device_id_type=MESH)

        def landed(t, j, cc):
            slot = 2 * chips[j][0] + chips[j][1]
            return g[t].at[slot, _half_rows(cc, shards[t].shape[0] // 2), :]

        def to_sibling(t, j):
            return pltpu.make_async_remote_copy(
                src_ref=landed(t, j, c), dst_ref=landed(t, j, c), send_sem=send_sem.at[t, 3 + j],
                recv_sem=recv_sem.at[t, 3 + j], device_id=sibling, device_id_type=MESH)

        own = [pltpu.make_async_copy(w[t], g[t].at[me], local_sem.at[t]) for t in range(n)]
        for cp in own:
            cp.start()
        first = [[over_ici(t, j) for j in range(3)] for t in range(n)]
        for t in range(n):
            for j in range(3):
                first[t][j].start()
        passed = [[to_sibling(t, j) for j in range(3)] for t in range(n)]
        for t in range(n):
            for j in range(3):
                pltpu.make_async_remote_copy(
                    src_ref=landed(t, j, c), dst_ref=landed(t, j, c), send_sem=send_sem.at[t, j],
                    recv_sem=recv_sem.at[t, j], device_id=(*chips[j], c), device_id_type=MESH).wait_recv()
                passed[t][j].start()
        for t in range(n):
            for j in range(3):
                pltpu.make_async_remote_copy(
                    src_ref=landed(t, j, 1 - c), dst_ref=landed(t, j, 1 - c), send_sem=send_sem.at[t, 3 + j],
                    recv_sem=recv_sem.at[t, 3 + j], device_id=sibling, device_id_type=MESH).wait_recv()
        for t in range(n):
            for j in range(3):
                first[t][j].wait_send()
                passed[t][j].wait_send()
            own[t].wait()

    return pl.pallas_call(
        body, name=name, in_specs=[ANY] * n, out_specs=[ANY] * n,
        out_shape=[jax.ShapeDtypeStruct((N_CHIPS,) + s.shape, BF16) for s in shards],
        scratch_shapes=[pltpu.SemaphoreType.DMA((n, 6)), pltpu.SemaphoreType.DMA((n, 6)), pltpu.SemaphoreType.DMA((n,))],
    )(*shards)


def _pair_exchange(name, parts):
    n = len(parts)

    def body(*refs):
        f, mine, theirs = refs[:n], refs[n:2 * n], refs[2 * n:3 * n]
        send_sem, recv_sem, local_sem = refs[3 * n:]
        x, y, c, _ = _position()
        keep, out = [], []
        for t in range(n):
            half = parts[t].shape[1] // 2
            keep.append(pltpu.make_async_copy(f[t].at[:, _half_rows(c, half), :], mine[t], local_sem.at[t]))
            out.append(pltpu.make_async_remote_copy(
                src_ref=f[t].at[:, _half_rows(1 - c, half), :], dst_ref=theirs[t], send_sem=send_sem.at[t],
                recv_sem=recv_sem.at[t], device_id=(x, y, 1 - c), device_id_type=MESH))
        for t in range(n):
            out[t].start()
            keep[t].start()
        for t in range(n):
            out[t].wait()
            keep[t].wait()

    shapes = [jax.ShapeDtypeStruct((N_CHIPS, p.shape[1] // 2, p.shape[2]), BF16) for p in parts]
    res = pl.pallas_call(
        body, name=name, in_specs=[ANY] * n, out_specs=[ANY] * (2 * n), out_shape=shapes + shapes,
        scratch_shapes=[pltpu.SemaphoreType.DMA((n,)), pltpu.SemaphoreType.DMA((n,)), pltpu.SemaphoreType.DMA((n,))],
    )(*parts)
    return list(zip(res[:n], res[n:]))


def _add_bf16(name, a, b):
    _, rows, cols = a.shape
    tr = _row_tile(rows)

    def body(a_ref, b_ref, o_ref):
        o_ref[...] = (a_ref[...].astype(F32) + b_ref[...].astype(F32)).astype(BF16)

    spec = pl.BlockSpec((None, tr, cols), lambda s, i: (s, i, 0))
    return pl.pallas_call(body, name=name, grid=(N_CHIPS, rows // tr), in_specs=[spec, spec], out_specs=spec,
                          out_shape=jax.ShapeDtypeStruct(a.shape, BF16),
                          compiler_params=_params("parallel", "parallel"))(a, b)


def _chip_exchange(name, sums):
    n = len(sums)

    def body(*refs):
        r, own, got = refs[:n], refs[n:2 * n], refs[2 * n:3 * n]
        send_sem, recv_sem, local_sem = refs[3 * n:]
        x, y, c, chips = _position()
        me = 2 * x + y
        keep = [pltpu.make_async_copy(r[t].at[me], own[t], local_sem.at[t]) for t in range(n)]
        out = [[pltpu.make_async_remote_copy(
            src_ref=r[t].at[2 * chips[j][0] + chips[j][1]], dst_ref=got[t].at[j], send_sem=send_sem.at[t, j],
            recv_sem=recv_sem.at[t, j], device_id=(*chips[j], c), device_id_type=MESH) for j in range(3)] for t in range(n)]
        for t in range(n):
            for j in range(3):
                out[t][j].start()
            keep[t].start()
        for t in range(n):
            for j in range(3):
                out[t][j].wait()
            keep[t].wait()

    own_shapes = [jax.ShapeDtypeStruct(s.shape[1:], BF16) for s in sums]
    got_shapes = [jax.ShapeDtypeStruct((3,) + s.shape[1:], BF16) for s in sums]
    res = pl.pallas_call(
        body, name=name, in_specs=[ANY] * n, out_specs=[ANY] * (2 * n), out_shape=own_shapes + got_shapes,
        scratch_shapes=[pltpu.SemaphoreType.DMA((n, 3)), pltpu.SemaphoreType.DMA((n, 3)), pltpu.SemaphoreType.DMA((n,))],
    )(*sums)
    return list(zip(res[:n], res[n:]))


def _add_four(name, own, got):
    rows, cols = own.shape
    tr = _row_tile(rows)

    def body(o_ref, g_ref, out_ref):
        out_ref[...] = ((o_ref[...].astype(F32) + g_ref[0].astype(F32)) + g_ref[1].astype(F32)) + g_ref[2].astype(F32)

    return pl.pallas_call(
        body, name=name, grid=(rows // tr,),
        in_specs=[pl.BlockSpec((tr, cols), lambda i: (i, 0)), pl.BlockSpec((3, tr, cols), lambda i: (0, i, 0))],
        out_specs=pl.BlockSpec((tr, cols), lambda i: (i, 0)), out_shape=jax.ShapeDtypeStruct((rows, cols), F32),
        compiler_params=_params("parallel"))(own, got)


def _sibling_exchange(name, halves, out_shapes):
    n = len(halves)
    params = list(out_shapes)

    def body(*refs):
        h, full = refs[:n], refs[n:n + len(params)]
        send_sem, recv_sem, local_sem = refs[n + len(params):]
        x, y, c, _ = _position()
        keep, out, arrive = [], [], []
        for t, (_, pname, layer) in enumerate(GRAD_PARTS):
            dst = full[params.index(pname)]
            half = halves[t].shape[0]
            keep.append(pltpu.make_async_copy(h[t], dst.at[layer, _half_rows(c, half), :], local_sem.at[t]))
            out.append(pltpu.make_async_remote_copy(
                src_ref=h[t], dst_ref=dst.at[layer, _half_rows(c, half), :], send_sem=send_sem.at[t],
                recv_sem=recv_sem.at[t], device_id=(x, y, 1 - c), device_id_type=MESH))
            arrive.append(pltpu.make_async_remote_copy(
                src_ref=h[t], dst_ref=dst.at[layer, _half_rows(1 - c, half), :], send_sem=send_sem.at[t],
                recv_sem=recv_sem.at[t], device_id=(x, y, 1 - c), device_id_type=MESH))
        for t in range(n):
            out[t].start()
            keep[t].start()
        for t in range(n):
            out[t].wait_send()
            arrive[t].wait_recv()
            keep[t].wait()

    return pl.pallas_call(
        body, name=name, in_specs=[ANY] * n, out_specs=[ANY] * len(params),
        out_shape=[jax.ShapeDtypeStruct(out_shapes[p], F32) for p in params],
        scratch_shapes=[pltpu.SemaphoreType.DMA((n,)), pltpu.SemaphoreType.DMA((n,)), pltpu.SemaphoreType.DMA((n,))],
    )(*halves)


def _all_reduce_small(name, packed):
    n_dev = 8

    def body(p_ref, o_ref, slots, send_sem, recv_sem):
        x, y, c, _ = _position()
        me = 4 * x + 2 * y + c

        def peer(k):
            return (1 - x if k & 4 else x, 1 - y if k & 2 else y, 1 - c if k & 1 else c)

        def logical(k):
            px, py, pc = peer(k)
            return 4 * px + 2 * py + pc

        slots[me] = p_ref[...]
        sends = [pltpu.make_async_remote_copy(
            src_ref=p_ref, dst_ref=slots.at[me], send_sem=send_sem.at[k], recv_sem=recv_sem.at[k],
            device_id=peer(k), device_id_type=MESH) for k in range(1, n_dev)]
        for cp in sends:
            cp.start()
        for k in range(1, n_dev):
            pltpu.make_async_remote_copy(
                src_ref=p_ref, dst_ref=slots.at[logical(k)], send_sem=send_sem.at[k], recv_sem=recv_sem.at[k],
                device_id=peer(k), device_id_type=MESH).wait_recv()
        for cp in sends:
            cp.wait_send()
        total = slots[0]
        for d in range(1, n_dev):
            total = total + slots[d]
        o_ref[...] = total

    vm = pl.BlockSpec(memory_space=pltpu.VMEM)
    return pl.pallas_call(
        body, name=name, in_specs=[vm], out_specs=vm, out_shape=jax.ShapeDtypeStruct(packed.shape, F32),
        scratch_shapes=[pltpu.VMEM((n_dev,) + packed.shape, F32), pltpu.SemaphoreType.DMA((n_dev,)),
                        pltpu.SemaphoreType.DMA((n_dev,))],
    )(packed)


def _adamw(name, w, g, m, v):
    rows, cols = w.shape
    tr = _row_tile(rows)

    def body(w_ref, g_ref, m_ref, v_ref, d_ref, mo_ref, vo_ref):
        gv = g_ref[...]
        m_new = ADAM_B1 * m_ref[...] + (1.0 - ADAM_B1) * gv
        v_new = ADAM_B2 * v_ref[...] + (1.0 - ADAM_B2) * (gv * gv)
        m_hat = m_new / (1.0 - ADAM_B1 ** ADAM_STEP)
        v_hat = v_new / (1.0 - ADAM_B2 ** ADAM_STEP)
        d_ref[...] = -ADAM_LR * (m_hat / (jnp.sqrt(v_hat) + ADAM_EPS) + ADAM_WD * w_ref[...])
        mo_ref[...] = m_new
        vo_ref[...] = v_new

    spec = pl.BlockSpec((tr, cols), lambda i: (i, 0))
    shape = jax.ShapeDtypeStruct((rows, cols), F32)
    return pl.pallas_call(body, name=name, grid=(rows // tr,), in_specs=[spec] * 4, out_specs=[spec] * 3,
                          out_shape=[shape] * 3, compiler_params=_params("parallel"))(w, g, m, v)


def _pack_small(tree):
    flat = jnp.concatenate([tree[n].reshape(-1).astype(F32) for n in SMALL])
    return jnp.pad(flat, (0, SMALL_ROWS * LANES - flat.shape[0])).reshape(SMALL_ROWS, LANES)


def _unpack_small(packed, like):
    flat = packed.reshape(-1)
    out, off = {}, 0
    for n in SMALL:
        size = int(np.prod(like[n].shape))
        out[n] = flat[off:off + size].reshape(like[n].shape)
        off += size
    return out


def kernel(x, ev_w_in, ev_g_cq, ev_w_uq, ev_g_ckv, ev_w_ukv, ev_w_out, od_w_qkv, od_rel_bias, od_w_out, g_mix, g_ffn, w_gate, w_up, w_down, g_final, loss_target, m_ev_w_in, m_ev_g_cq, m_ev_w_uq, m_ev_g_ckv, m_ev_w_ukv, m_ev_w_out, m_od_w_qkv, m_od_rel_bias, m_od_w_out, m_g_mix, m_g_ffn, m_w_gate, m_w_up, m_w_down, m_g_final, v_ev_w_in, v_ev_g_cq, v_ev_w_uq, v_ev_g_ckv, v_ev_w_ukv, v_ev_w_out, v_od_w_qkv, v_od_rel_bias, v_od_w_out, v_g_mix, v_g_ffn, v_w_gate, v_w_up, v_w_down, v_g_final):
    w = dict(ev_w_in=ev_w_in, ev_g_cq=ev_g_cq, ev_w_uq=ev_w_uq, ev_g_ckv=ev_g_ckv, ev_w_ukv=ev_w_ukv, ev_w_out=ev_w_out,
             od_w_qkv=od_w_qkv, od_rel_bias=od_rel_bias, od_w_out=od_w_out, g_mix=g_mix, g_ffn=g_ffn, w_gate=w_gate,
             w_up=w_up, w_down=w_down, g_final=g_final)
    m = dict(ev_w_in=m_ev_w_in, ev_g_cq=m_ev_g_cq, ev_w_uq=m_ev_w_uq, ev_g_ckv=m_ev_g_ckv, ev_w_ukv=m_ev_w_ukv,
             ev_w_out=m_ev_w_out, od_w_qkv=m_od_w_qkv, od_rel_bias=m_od_rel_bias, od_w_out=m_od_w_out, g_mix=m_g_mix,
             g_ffn=m_g_ffn, w_gate=m_w_gate, w_up=m_w_up, w_down=m_w_down, g_final=m_g_final)
    v = dict(ev_w_in=v_ev_w_in, ev_g_cq=v_ev_g_cq, ev_w_uq=v_ev_w_uq, ev_g_ckv=v_ev_g_ckv, ev_w_ukv=v_ev_w_ukv,
             ev_w_out=v_ev_w_out, od_w_qkv=v_od_w_qkv, od_rel_bias=v_od_rel_bias, od_w_out=v_od_w_out, g_mix=v_g_mix,
             g_ffn=v_g_ffn, w_gate=v_w_gate, w_up=v_w_up, w_down=v_w_down, g_final=v_g_final)
    flat2d = lambda a: a.reshape(-1, a.shape[-1])

    shards = [_cast_bf16("cast_" + n, flat2d(w[n])) for n in BIG]
    gw = dict(zip(BIG, _all_gather_weights("gather_weights", shards)))

    loss_local, grad_x, big, small = _local_step(x[0], loss_target[0], gw, {n: w[n] for n in SMALL})

    pairs = _pair_exchange("grads_pair", [big[p] for p, _, _ in GRAD_PARTS])
    sums = [_add_bf16("pair_sum_" + GRAD_PARTS[t][0], a, b) for t, (a, b) in enumerate(pairs)]
    arrived = _chip_exchange("grads_chips", sums)
    halves = [_add_four("chip_sum_" + GRAD_PARTS[t][0], a, b) for t, (a, b) in enumerate(arrived)]
    grads = dict(zip(BIG, _sibling_exchange("grads_sibling", halves, {n: w[n].shape for n in BIG})))
    small_sum = _all_reduce_small("small_sum", _pack_small(small))
    grads.update(_unpack_small(small_sum, w))

    delta, new_m, new_v = {}, {}, {}
    for n in BIG:
        d_, m_, v_ = _adamw("adamw_" + n, flat2d(w[n]), flat2d(grads[n]), flat2d(m[n]), flat2d(v[n]))
        delta[n], new_m[n], new_v[n] = d_.reshape(w[n].shape), m_.reshape(w[n].shape), v_.reshape(w[n].shape)
    d_, m_, v_ = _adamw("adamw_small", _pack_small(w), small_sum, _pack_small(m), _pack_small(v))
    delta.update(_unpack_small(d_, w))
    new_m.update(_unpack_small(m_, w))
    new_v.update(_unpack_small(v_, w))

    loss = lax.psum(loss_local[0, 0], ("x", "y", "c"))
    return (loss, grad_x[None], *[grads[n] for n in WEIGHTS], *[delta[n] for n in WEIGHTS],
            *[new_m[n] for n in WEIGHTS], *[new_v[n] for n in WEIGHTS])
```

```python
import functools

import jax
import jax.numpy as jnp
import numpy as np
from jax import lax
from jax.experimental import pallas as pl
from jax.experimental.pallas import tpu as pltpu

F32 = jnp.float32
BF16 = jnp.bfloat16

S = 2048
D = 1024
CHUNK = 64
MLA_H, MLA_NOPE, MLA_ROPE, MLA_V = 8, 64, 32, 64
Q_LORA, KV_LORA = 384, 256
ROPE_THETA = 10000.0
SB_H, SB_DIM = 8, 64
C_H, C_DIM = 16, 64
LEFT_CHUNKS = 8
REL_CLIP = 256
D_FF = 2816
EVEN_IN = 2208
RMS_EPS = 1e-6
ADAM_LR, ADAM_B1, ADAM_B2, ADAM_EPS, ADAM_WD, ADAM_STEP = 0.001, 0.9, 0.999, 1e-08, 0.01, 10

N_CHIPS = 4
FF_SHARD = D_FF // N_CHIPS
SCALE_A = (MLA_NOPE + MLA_ROPE) ** -0.5
SCALE_B = SB_DIM ** -0.5
SCALE_C = C_DIM ** -0.5
NEG = -1e30

LANES = 128
VMEM_LIMIT_BYTES = 56 * 1024 * 1024
TM = 512
QB = 128

P_CQ, P_CKV, P_QB, P_KB, P_VB, P_KR = 0, 512, 768, 1280, 1792, 2304
P_IN = 2432
KR_LANE = 64
BAND_W = 640
BAND_PAD = 512
TOEP_W = 1024


def _params(*sem):
    return pltpu.CompilerParams(dimension_semantics=sem, vmem_limit_bytes=VMEM_LIMIT_BYTES)


_DIMS = {"nn": (((1,), (0,)), ((), ())), "nt": (((1,), (1,)), ((), ())), "tn": (((0,), (0,)), ((), ()))}


def _dot(a, b, kind="nn"):
    return lax.dot_general(a, b, _DIMS[kind], preferred_element_type=F32)


def _iota(shape, dim):
    return lax.broadcasted_iota(jnp.int32, shape, dim)


def _sigmoid(x):
    return 1.0 / (1.0 + jnp.exp(-x))


def _softplus(x):
    return jnp.maximum(x, 0.0) + jnp.log(1.0 + jnp.exp(-jnp.abs(x)))


def _split_dot(x, tri):
    hi = x.astype(BF16)
    lo = (x - hi.astype(F32)).astype(BF16)
    return _dot(hi, tri) + _dot(lo, tri)


def _mm(name, a, b, *, kind, grid, a_spec, b_spec, o_spec, out_shape, out_dtype, acc_shape, resid=None, r_spec=None):
    nk = grid[-1]
    has_r = resid is not None

    def body(*refs):
        a_ref, b_ref = refs[0], refs[1]
        r_ref = refs[2] if has_r else None
        o_ref = refs[2 + has_r]
        part = _dot(a_ref[...].astype(BF16), b_ref[...].astype(BF16), kind)

        def finish(total):
            if has_r:
                total = total + r_ref[...].astype(F32)
            o_ref[...] = total.astype(out_dtype)

        if nk == 1:
            finish(part)
        else:
            acc_ref = refs[3 + has_r]
            k = pl.program_id(len(grid) - 1)

            @pl.when(k == 0)
            def _():
                acc_ref[...] = part

            @pl.when(k > 0)
            def _():
                acc_ref[...] += part

            @pl.when(k == nk - 1)
            def _():
                finish(acc_ref[...])

    in_specs = [a_spec, b_spec] + ([r_spec] if has_r else [])
    args = (a, b) + ((resid,) if has_r else ())
    sem = ("parallel",) * (len(grid) - 1) + ("arbitrary",)
    return pl.pallas_call(
        body, name=name, grid=grid, in_specs=in_specs, out_specs=o_spec,
        out_shape=jax.ShapeDtypeStruct(out_shape, out_dtype),
        scratch_shapes=[pltpu.VMEM(acc_shape, F32)] if nk > 1 else [],
        compiler_params=_params(*sem),
    )(*args)


def _rms_fwd(name, x, g, col_block=0):
    c = g.shape[1]

    def body(x_ref, g_ref, u_ref):
        xv = x_ref[...]
        r = lax.rsqrt(jnp.mean(xv * xv, axis=-1, keepdims=True) + RMS_EPS)
        u_ref[...] = (xv * r * g_ref[...]).astype(BF16)

    return pl.pallas_call(
        body, name=name, grid=(S // TM,),
        in_specs=[pl.BlockSpec((TM, c), lambda i: (i, col_block)), pl.BlockSpec((1, c), lambda i: (0, 0))],
        out_specs=pl.BlockSpec((TM, c), lambda i: (i, 0)),
        out_shape=jax.ShapeDtypeStruct((S, c), BF16),
        compiler_params=_params("parallel"),
    )(x, g)


def _rms_bwd(name, dy, x, g, resid):
    def body(dy_ref, x_ref, g_ref, r_ref, dx_ref, dg_ref):
        i = pl.program_id(0)
        xv = x_ref[...]
        r = lax.rsqrt(jnp.mean(xv * xv, axis=-1, keepdims=True) + RMS_EPS)
        xh = xv * r
        dyv = dy_ref[...]
        dxh = dyv * g_ref[...]
        dx_ref[...] = r_ref[...] + r * (dxh - xh * jnp.mean(dxh * xh, axis=-1, keepdims=True))
        part = jnp.sum(dyv * xh, axis=0, keepdims=True)

        @pl.when(i == 0)
        def _():
            dg_ref[...] = part

        @pl.when(i > 0)
        def _():
            dg_ref[...] += part

    row = pl.BlockSpec((TM, D), lambda i: (i, 0))
    vec = pl.BlockSpec((1, D), lambda i: (0, 0))
    return pl.pallas_call(
        body, name=name, grid=(S // TM,), in_specs=[row, row, vec, row], out_specs=[row, vec],
        out_shape=[jax.ShapeDtypeStruct((S, D), F32), jax.ShapeDtypeStruct((1, D), F32)],
        compiler_params=_params("arbitrary"),
    )(dy, x, g, resid)


def _loss_bwd(name, h, g, tgt):
    def body(h_ref, g_ref, t_ref, loss_ref, dh_ref, dg_ref):
        i = pl.program_id(0)
        xv = h_ref[...]
        gv = g_ref[...]
        r = lax.rsqrt(jnp.mean(xv * xv, axis=-1, keepdims=True) + RMS_EPS)
        xh = xv * r
        diff = xh * gv - t_ref[...]
        part_loss = 0.5 * jnp.sum(jnp.sum(diff * diff, axis=-1, keepdims=True) * (1.0 / D), axis=0, keepdims=True)
        dy = diff * (1.0 / D)
        dxh = dy * gv
        dh_ref[...] = r * (dxh - xh * jnp.mean(dxh * xh, axis=-1, keepdims=True))
        part_g = jnp.sum(dy * xh, axis=0, keepdims=True)

        @pl.when(i == 0)
        def _():
            dg_ref[...] = part_g
            loss_ref[...] = jnp.broadcast_to(part_loss, (1, LANES))

        @pl.when(i > 0)
        def _():
            dg_ref[...] += part_g
            loss_ref[...] += jnp.broadcast_to(part_loss, (1, LANES))

    row = pl.BlockSpec((TM, D), lambda i: (i, 0))
    vec = pl.BlockSpec((1, D), lambda i: (0, 0))
    return pl.pallas_call(
        body, name=name, grid=(S // TM,), in_specs=[row, vec, row],
        out_specs=[pl.BlockSpec((1, LANES), lambda i: (0, 0)), row, vec],
        out_shape=[jax.ShapeDtypeStruct((1, LANES), F32), jax.ShapeDtypeStruct((S, D), F32),
                   jax.ShapeDtypeStruct((1, D), F32)],
        compiler_params=_params("arbitrary"),
    )(h, g, tgt)


def _ffn_fwd(name, h, g, wg, wu, wd, layer):
    def body(h_ref, g_ref, wg_ref, wu_ref, wd_ref, o_ref, gate_ref, up_ref, u_scr):
        s = pl.program_id(1)

        @pl.when(s == 0)
        def _():
            xv = h_ref[...]
            r = lax.rsqrt(jnp.mean(xv * xv, axis=-1, keepdims=True) + RMS_EPS)
            u_scr[...] = (xv * r * g_ref[...]).astype(BF16)
            o_ref[...] = xv

        u = u_scr[...]
        gate = _dot(u, wg_ref[...])
        up = _dot(u, wu_ref[...])
        act = gate * _sigmoid(gate) * up
        o_ref[...] += _dot(act.astype(BF16), wd_ref[...])
        gate_ref[...] = gate.astype(BF16)
        up_ref[...] = up.astype(BF16)

    row = pl.BlockSpec((TM, D), lambda i, s: (i, 0))
    hid = pl.BlockSpec((None, TM, FF_SHARD), lambda i, s: (s, i, 0))
    return pl.pallas_call(
        body, name=name, grid=(S // TM, N_CHIPS),
        in_specs=[row, pl.BlockSpec((1, D), lambda i, s: (0, 0)),
                  pl.BlockSpec((None, D, FF_SHARD), lambda i, s: (s, layer, 0)),
                  pl.BlockSpec((None, D, FF_SHARD), lambda i, s: (s, layer, 0)),
                  pl.BlockSpec((None, FF_SHARD, D), lambda i, s: (s, layer, 0))],
        out_specs=[row, hid, hid],
        out_shape=[jax.ShapeDtypeStruct((S, D), F32), jax.ShapeDtypeStruct((N_CHIPS, S, FF_SHARD), BF16),
                   jax.ShapeDtypeStruct((N_CHIPS, S, FF_SHARD), BF16)],
        scratch_shapes=[pltpu.VMEM((TM, D), BF16)],
        compiler_params=_params("parallel", "arbitrary"),
    )(h, g, wg, wu, wd)


def _ffn_bwd(name, dh, h, g, gate, up, wg, wu, wd, layer):
    def body(dh_ref, h_ref, g_ref, gate_ref, up_ref, wg_ref, wu_ref, wd_ref,
             dhin_ref, dg_ref, u_ref, dgate_ref, dup_ref, act_ref, dhb_scr, du_scr):
        i = pl.program_id(0)
        s = pl.program_id(1)

        @pl.when(s == 0)
        def _():
            xv = h_ref[...]
            r = lax.rsqrt(jnp.mean(xv * xv, axis=-1, keepdims=True) + RMS_EPS)
            u_ref[...] = (xv * r * g_ref[...]).astype(BF16)
            dhb_scr[...] = dh_ref[...].astype(BF16)
            du_scr[...] = jnp.zeros_like(du_scr)

        dact = _dot(dhb_scr[...], wd_ref[...], "nt")
        gv = gate_ref[...].astype(F32)
        uv = up_ref[...].astype(F32)
        sig = _sigmoid(gv)
        sil = gv * sig
        dup = dact * sil
        dgate = dact * uv * (sig * (1.0 + gv * (1.0 - sig)))
        dgb = dgate.astype(BF16)
        dub = dup.astype(BF16)
        act_ref[...] = (sil * uv).astype(BF16)
        dgate_ref[...] = dgb
        dup_ref[...] = dub
        du_scr[...] += _dot(dgb, wg_ref[...], "nt") + _dot(dub, wu_ref[...], "nt")

        @pl.when(s == N_CHIPS - 1)
        def _():
            xv = h_ref[...]
            r = lax.rsqrt(jnp.mean(xv * xv, axis=-1, keepdims=True) + RMS_EPS)
            xh = xv * r
            du = du_scr[...]
            dxh = du * g_ref[...]
            dhin_ref[...] = dh_ref[...] + r * (dxh - xh * jnp.mean(dxh * xh, axis=-1, keepdims=True))
            part = jnp.sum(du * xh, axis=0, keepdims=True)

            @pl.when(i == 0)
            def _():
                dg_ref[...] = part

            @pl.when(i > 0)
            def _():
                dg_ref[...] += part

    row = pl.BlockSpec((TM, D), lambda i, s: (i, 0))
    vec = pl.BlockSpec((1, D), lambda i, s: (0, 0))
    hid = pl.BlockSpec((None, TM, FF_SHARD), lambda i, s: (s, i, 0))
    hid_shape = jax.ShapeDtypeStruct((N_CHIPS, S, FF_SHARD), BF16)
    return pl.pallas_call(
        body, name=name, grid=(S // TM, N_CHIPS),
        in_specs=[row, row, vec, hid, hid,
                  pl.BlockSpec((None, D, FF_SHARD), lambda i, s: (s, layer, 0)),
                  pl.BlockSpec((None, D, FF_SHARD), lambda i, s: (s, layer, 0)),
                  pl.BlockSpec((None, FF_SHARD, D), lambda i, s: (s, layer, 0))],
        out_specs=[row, vec, row, hid, hid, hid],
        out_shape=[jax.ShapeDtypeStruct((S, D), F32), jax.ShapeDtypeStruct((1, D), F32),
                   jax.ShapeDtypeStruct((S, D), BF16), hid_shape, hid_shape, hid_shape],
        scratch_shapes=[pltpu.VMEM((TM, D), BF16), pltpu.VMEM((TM, D), F32)],
        compiler_params=_params("arbitrary", "arbitrary"),
    )(dh, h, g, gate, up, wg, wu, wd)


def _ffn_wgrads(name, u, dgate, dup, act, dh):
    nk = S // TM
    tok = lambda s, k: (k, 0)
    hid = pl.BlockSpec((None, TM, FF_SHARD), lambda s, k: (s, k, 0))
    common = dict(kind="tn", grid=(N_CHIPS, nk), out_dtype=BF16)
    d_wg = _mm(name + "_g", u, dgate, a_spec=pl.BlockSpec((TM, D), tok), b_spec=hid,
               o_spec=pl.BlockSpec((None, D, FF_SHARD), lambda s, k: (s, 0, 0)),
               out_shape=(N_CHIPS, D, FF_SHARD), acc_shape=(D, FF_SHARD), **common)
    d_wu = _mm(name + "_u", u, dup, a_spec=pl.BlockSpec((TM, D), tok), b_spec=hid,
               o_spec=pl.BlockSpec((None, D, FF_SHARD), lambda s, k: (s, 0, 0)),
               out_shape=(N_CHIPS, D, FF_SHARD), acc_shape=(D, FF_SHARD), **common)
    d_wd = _mm(name + "_d", act, dh, a_spec=hid, b_spec=pl.BlockSpec((TM, D), tok),
               o_spec=pl.BlockSpec((None, FF_SHARD, D), lambda s, k: (s, 0, 0)),
               out_shape=(N_CHIPS, FF_SHARD, D), acc_shape=(FF_SHARD, D), **common)
    return d_wg, d_wu, d_wd


def _rope_tables():
    pos = jnp.arange(S, dtype=F32)
    inv = ROPE_THETA ** (-jnp.arange(0, MLA_ROPE, 2, dtype=F32) / MLA_ROPE)
    ang = pos[:, None] * inv[None, :]
    half = MLA_ROPE // 2
    cos = jnp.cos(ang)
    sin = jnp.sin(ang)
    one = jnp.ones((S, KR_LANE), F32)
    zero = jnp.zeros((S, KR_LANE), F32)
    tail_one = jnp.ones((S, LANES - KR_LANE - MLA_ROPE), F32)
    tail_zero = jnp.zeros((S, LANES - KR_LANE - MLA_ROPE), F32)
    cos_t = jnp.concatenate([one, cos, cos, tail_one], axis=1)
    sin_t = jnp.concatenate([zero, -sin, sin, tail_zero], axis=1)
    assert cos_t.shape == (S, LANES) and half * 2 == MLA_ROPE
    return cos_t, sin_t


def _rope(x, cos_t, sin_t, sign):
    n = x.shape[1] // LANES
    half = MLA_ROPE // 2
    lane = _iota(x.shape, 1) & (LANES - 1)
    first = (lane >= KR_LANE) & (lane < KR_LANE + half)
    swapped = jnp.where(first, pltpu.roll(x, x.shape[1] - half, 1), pltpu.roll(x, half, 1))
    c = jnp.tile(cos_t, (1, n)) if n > 1 else cos_t
    s = jnp.tile(sin_t, (1, n)) if n > 1 else sin_t
    return x * c + swapped * (s * sign)


def _mla_prep_fwd(name, proj, g_cq, g_ckv, w_uq, w_uk, w_uv, cos_t, sin_t):
    nh = MLA_H * LANES

    def body(cq_ref, ckv_ref, kr_ref, gq_ref, gkv_ref, wq_ref, wk_ref, wv_ref, cos_ref, sin_ref,
             qa_ref, ka_ref, va_ref):
        cos_v, sin_v = cos_ref[...], sin_ref[...]
        cq = cq_ref[...]
        r = lax.rsqrt(jnp.mean(cq * cq, axis=-1, keepdims=True) + RMS_EPS)
        cqn = (cq * r * gq_ref[...]).astype(BF16)
        qa_ref[...] = _rope(_dot(cqn, wq_ref[...]), cos_v, sin_v, 1.0).astype(BF16)
        ckv = ckv_ref[...]
        r = lax.rsqrt(jnp.mean(ckv * ckv, axis=-1, keepdims=True) + RMS_EPS)
        ckvn = (ckv * r * gkv_ref[...]).astype(BF16)
        lane = _iota((TM, LANES), 1)
        rot = (lane >= KR_LANE) & (lane < KR_LANE + MLA_ROPE)
        kr = jnp.where(rot, _rope(kr_ref[...], cos_v, sin_v, 1.0), 0.0)
        ka_ref[...] = (_dot(ckvn, wk_ref[...]) + jnp.tile(kr, (1, MLA_H))).astype(BF16)
        va_ref[...] = _dot(ckvn, wv_ref[...]).astype(BF16)

    full = lambda shape: pl.BlockSpec(shape, lambda i: (0, 0))
    return pl.pallas_call(
        body, name=name, grid=(S // TM,),
        in_specs=[pl.BlockSpec((TM, Q_LORA), lambda i: (i, P_CQ // Q_LORA)),
                  pl.BlockSpec((TM, KV_LORA), lambda i: (i, P_CKV // KV_LORA)),
                  pl.BlockSpec((TM, LANES), lambda i: (i, P_KR // LANES)),
                  full((1, Q_LORA)), full((1, KV_LORA)), full((Q_LORA, nh)), full((KV_LORA, nh)),
                  full((KV_LORA, MLA_H * MLA_V)),
                  pl.BlockSpec((TM, LANES), lambda i: (i, 0)), pl.BlockSpec((TM, LANES), lambda i: (i, 0))],
        out_specs=[pl.BlockSpec((TM, nh), lambda i: (i, 0)), pl.BlockSpec((TM, nh), lambda i: (i, 0)),
                   pl.BlockSpec((TM, MLA_H * MLA_V), lambda i: (i, 0))],
        out_shape=[jax.ShapeDtypeStruct((S, nh), BF16), jax.ShapeDtypeStruct((S, nh), BF16),
                   jax.ShapeDtypeStruct((S, MLA_H * MLA_V), BF16)],
        compiler_params=_params("parallel"),
    )(proj, proj, proj, g_cq, g_ckv, w_uq, w_uk, w_uv, cos_t, sin_t)


def _mla_prep_bwd(name, dqa, dka, dva, proj, g_cq, g_ckv, w_uq, w_uk, w_uv, cos_t, sin_t):
    nh = MLA_H * LANES

    def body(dqa_ref, dka_ref, dva_ref, cq_ref, ckv_ref, gq_ref, gkv_ref, wq_ref, wk_ref, wv_ref, cos_ref, sin_ref,
             dcq_ref, dckv_ref, dkr_ref, dwq_ref, dwk_ref, dwv_ref, dgq_ref, dgkv_ref):
        i = pl.program_id(0)
        cos_v, sin_v = cos_ref[...], sin_ref[...]

        def norm_bwd(x, g, dn):
            r = lax.rsqrt(jnp.mean(x * x, axis=-1, keepdims=True) + RMS_EPS)
            xh = x * r
            dxh = dn * g
            dx = r * (dxh - xh * jnp.mean(dxh * xh, axis=-1, keepdims=True))
            return dx, jnp.sum(dn * xh, axis=0, keepdims=True), (xh * g).astype(BF16)

        dq = _rope(dqa_ref[...], cos_v, sin_v, -1.0).astype(BF16)
        dcqn = _dot(dq, wq_ref[...], "nt")
        dcq, dgq, cqn = norm_bwd(cq_ref[...], gq_ref[...], dcqn)
        dcq_ref[...] = dcq.astype(BF16)
        dwq = _dot(cqn, dq, "tn")

        dka = dka_ref[...]
        dkab = dka.astype(BF16)
        dvab = dva_ref[...].astype(BF16)
        dckvn = _dot(dkab, wk_ref[...], "nt") + _dot(dvab, wv_ref[...], "nt")
        dckv, dgkv, ckvn = norm_bwd(ckv_ref[...], gkv_ref[...], dckvn)
        dckv_ref[...] = dckv.astype(BF16)
        dwk = _dot(ckvn, dkab, "tn")
        dwv = _dot(ckvn, dvab, "tn")

        fold = dka[:, 0:LANES]
        for hh in range(1, MLA_H):
            fold = fold + dka[:, hh * LANES:(hh + 1) * LANES]
        lane = _iota((TM, LANES), 1)
        rot = (lane >= KR_LANE) & (lane < KR_LANE + MLA_ROPE)
        dkr = _rope(jnp.where(rot, fold, 0.0), cos_v, sin_v, -1.0)
        dkr_ref[...] = jnp.where(rot, dkr, 0.0).astype(BF16)

        @pl.when(i == 0)
        def _():
            dwq_ref[...] = dwq
            dwk_ref[...] = dwk
            dwv_ref[...] = dwv
            dgq_ref[...] = dgq
            dgkv_ref[...] = dgkv

        @pl.when(i > 0)
        def _():
            dwq_ref[...] += dwq
            dwk_ref[...] += dwk
            dwv_ref[...] += dwv
            dgq_ref[...] += dgq
            dgkv_ref[...] += dgkv

    full = lambda shape: pl.BlockSpec(shape, lambda i: (0, 0))
    rows = lambda c: pl.BlockSpec((TM, c), lambda i: (i, 0))
    nv = MLA_H * MLA_V
    return pl.pallas_call(
        body, name=name, grid=(S // TM,),
        in_specs=[rows(nh), rows(nh), rows(nv),
                  pl.BlockSpec((TM, Q_LORA), lambda i: (i, P_CQ // Q_LORA)),
                  pl.BlockSpec((TM, KV_LORA), lambda i: (i, P_CKV // KV_LORA)),
                  full((1, Q_LORA)), full((1, KV_LORA)), full((Q_LORA, nh)), full((KV_LORA, nh)), full((KV_LORA, nv)),
                  rows(LANES), rows(LANES)],
        out_specs=[rows(Q_LORA), rows(KV_LORA), rows(LANES), full((Q_LORA, nh)), full((KV_LORA, nh)),
                   full((KV_LORA, nv)), full((1, Q_LORA)), full((1, KV_LORA))],
        out_shape=[jax.ShapeDtypeStruct((S, Q_LORA), BF16), jax.ShapeDtypeStruct((S, KV_LORA), BF16),
                   jax.ShapeDtypeStruct((S, LANES), BF16), jax.ShapeDtypeStruct((Q_LORA, nh), F32),
                   jax.ShapeDtypeStruct((KV_LORA, nh), F32), jax.ShapeDtypeStruct((KV_LORA, nv), F32),
                   jax.ShapeDtypeStruct((1, Q_LORA), F32), jax.ShapeDtypeStruct((1, KV_LORA), F32)],
        compiler_params=_params("arbitrary"),
    )(dqa, dka, dva, proj, proj, g_cq, g_ckv, w_uq, w_uk, w_uv, cos_t, sin_t)


def _head_masks(dtype):
    lane = _iota((1, LANES), 1)
    return (lane < 64).astype(dtype), (lane >= 64).astype(dtype)


def _mla_fwd(name, qa, ka, va):
    def body(q_ref, k_ref, v_ref, o_ref, lse_ref):
        m0b, m1b = _head_masks(BF16)
        lane = _iota((QB, LANES), 1)
        left = lane < 64

        def qblock(i, _):
            r0 = pl.multiple_of(i * QB, QB)
            qs = [q_ref[pl.ds(r0, QB), hh * LANES:(hh + 1) * LANES] for hh in range(2)]
            rowc = lax.shift_right_logical(r0 + _iota((QB, QB), 0), 6)

            def kv(kb, carry):
                ms, ls, acc = carry
                c0 = pl.multiple_of(kb * QB, QB)
                v = v_ref[pl.ds(c0, QB), :]
                ok = lax.shift_right_logical(c0 + _iota((QB, QB), 1), 6) <= rowc
                new_m, new_l, alphas = [], [], []
                pv = None
                for hh in range(2):
                    k = k_ref[pl.ds(c0, QB), hh * LANES:(hh + 1) * LANES]
                    s = jnp.where(ok, _dot(qs[hh], k, "nt") * SCALE_A, NEG)
                    mn = jnp.maximum(ms[hh], jnp.max(s, axis=-1, keepdims=True))
                    p = jnp.exp(s - mn)
                    a = jnp.exp(ms[hh] - mn)
                    new_m.append(mn)
                    new_l.append(a * ls[hh] + jnp.sum(p, axis=-1, keepdims=True))
                    alphas.append(a)
                    part = _dot(p.astype(BF16), v * (m0b if hh == 0 else m1b))
                    pv = part if pv is None else pv + part
                acc = acc * jnp.where(left, alphas[0], alphas[1]) + pv
                return tuple(new_m), tuple(new_l), acc

            init = ((jnp.full((QB, 1), NEG, F32),) * 2, (jnp.zeros((QB, 1), F32),) * 2, jnp.zeros((QB, LANES), F32))
            ms, ls, acc = lax.fori_loop(0, i + 1, kv, init)
            o_ref[pl.ds(r0, QB), :] = acc * jnp.where(left, 1.0 / ls[0], 1.0 / ls[1])
            lse_ref[pl.ds(r0, QB), :] = jnp.where(left, ms[0] + jnp.log(ls[0]), ms[1] + jnp.log(ls[1]))
            return 0

        lax.fori_loop(0, S // QB, qblock, 0)

    pair = lambda w: pl.BlockSpec((S, w), lambda p: (0, p))
    return pl.pallas_call(
        body, name=name, grid=(MLA_H // 2,), in_specs=[pair(2 * LANES), pair(2 * LANES), pair(LANES)],
        out_specs=[pair(LANES), pair(LANES)],
        out_shape=[jax.ShapeDtypeStruct((S, MLA_H * MLA_V), F32), jax.ShapeDtypeStruct((S, MLA_H * MLA_V), F32)],
        compiler_params=_params("parallel"),
    )(qa, ka, va)


def _mla_bwd(name, qa, ka, va, o, lse, do, do_block0):
    def body(q_ref, k_ref, v_ref, o_ref, lse_ref, do_ref, dq_ref, dk_ref, dv_ref):
        m0f, m1f = _head_masks(F32)
        m0b, m1b = _head_masks(BF16)
        dk_ref[...] = jnp.zeros_like(dk_ref)
        dv_ref[...] = jnp.zeros_like(dv_ref)

        def qblock(i, _):
            r0 = pl.multiple_of(i * QB, QB)
            rows = pl.ds(r0, QB)
            do_f = do_ref[rows, :]
            prod = do_f * o_ref[rows, :]
            deltas = [jnp.sum(prod * m0f, axis=-1, keepdims=True), jnp.sum(prod * m1f, axis=-1, keepdims=True)]
            lse_v = lse_ref[rows, :]
            lses = [lse_v[:, 0:1], lse_v[:, 64:65]]
            dob = do_f.astype(BF16)
            dos = [dob * m0b, dob * m1b]
            qs = [q_ref[rows, hh * LANES:(hh + 1) * LANES] for hh in range(2)]
            rowc = lax.shift_right_logical(r0 + _iota((QB, QB), 0), 6)

            def kv(kb, dqs):
                c0 = pl.multiple_of(kb * QB, QB)
                cols = pl.ds(c0, QB)
                v = v_ref[cols, :]
                ok = lax.shift_right_logical(c0 + _iota((QB, QB), 1), 6) <= rowc
                out = []
                dv = None
                for hh in range(2):
                    k = k_ref[cols, hh * LANES:(hh + 1) * LANES]
                    s = _dot(qs[hh], k, "nt") * SCALE_A
                    p = jnp.where(ok, jnp.exp(s - lses[hh]), 0.0)
                    dp = _dot(dos[hh], v, "nt")
                    ds = (p * (dp - deltas[hh]) * SCALE_A).astype(BF16)
                    out.append(dqs[hh] + _dot(ds, k))
                    dk_ref[cols, hh * LANES:(hh + 1) * LANES] += _dot(ds, qs[hh], "tn")
                    part = _dot(p.astype(BF16), dos[hh], "tn")
                    dv = part if dv is None else dv + part
                dv_ref[cols, :] += dv
                return tuple(out)

            dqs = lax.fori_loop(0, i + 1, kv, (jnp.zeros((QB, LANES), F32),) * 2)
            for hh in range(2):
                dq_ref[rows, hh * LANES:(hh + 1) * LANES] = dqs[hh]
            return 0

        lax.fori_loop(0, S // QB, qblock, 0)

    pair = lambda w: pl.BlockSpec((S, w), lambda p: (0, p))
    return pl.pallas_call(
        body, name=name, grid=(MLA_H // 2,),
        in_specs=[pair(2 * LANES), pair(2 * LANES), pair(LANES), pair(LANES), pair(LANES),
                  pl.BlockSpec((S, LANES), lambda p: (0, do_block0 + p))],
        out_specs=[pair(2 * LANES), pair(2 * LANES), pair(LANES)],
        out_shape=[jax.ShapeDtypeStruct((S, MLA_H * LANES), F32), jax.ShapeDtypeStruct((S, MLA_H * LANES), F32),
                   jax.ShapeDtypeStruct((S, MLA_H * MLA_V), F32)],
        compiler_params=_params("parallel"),
    )(qa, ka, va, o, lse, do)


def _sb_weights(q_h, k, c, before, tri_suffix):
    z = _dot(q_h, k, "nt") * SCALE_B
    sp = _softplus(z)
    log_keep = jnp.where(before, -sp, 0.0)
    log_between = _split_dot(log_keep, tri_suffix) + c
    w = jnp.where(before, jnp.exp(z - sp + log_between), 0.0)
    return w, jnp.exp(z - sp), jnp.sum(log_keep, axis=-1, keepdims=True)


def _sb_fwd(name, proj):
    def body(q_ref, k_ref, v_ref, o_ref):
        m0b, m1b = _head_masks(BF16)
        tri_suffix = (_iota((QB, QB), 0) > _iota((QB, QB), 1)).astype(BF16)

        def qblock(i, _):
            r0 = pl.multiple_of(i * QB, QB)
            q = q_ref[pl.ds(r0, QB), :].astype(BF16)
            qs = [q * m0b, q * m1b]
            rowg = r0 + _iota((QB, QB), 0)

            def kv(step, carry):
                cs, acc = carry
                c0 = pl.multiple_of((i - step) * QB, QB)
                k = k_ref[pl.ds(c0, QB), :].astype(BF16)
                v = v_ref[pl.ds(c0, QB), :].astype(BF16)
                before = (c0 + _iota((QB, QB), 1)) < rowg
                new_c = []
                for hh in range(2):
                    w, _, tot = _sb_weights(qs[hh], k, cs[hh], before, tri_suffix)
                    new_c.append(cs[hh] + tot)
                    acc = acc + _dot(w.astype(BF16), v * (m0b if hh == 0 else m1b))
                return tuple(new_c), acc

            init = ((jnp.zeros((QB, 1), F32),) * 2, jnp.zeros((QB, LANES), F32))
            _, acc = lax.fori_loop(0, i + 1, kv, init)
            o_ref[pl.ds(r0, QB), :] = acc.astype(BF16)
            return 0

        lax.fori_loop(0, S // QB, qblock, 0)

    col = lambda base: pl.BlockSpec((S, LANES), lambda p: (0, base // LANES + p))
    return pl.pallas_call(
        body, name=name, grid=(SB_H // 2,), in_specs=[col(P_QB), col(P_KB), col(P_VB)],
        out_specs=pl.BlockSpec((S, LANES), lambda p: (0, p)),
        out_shape=jax.ShapeDtypeStruct((S, SB_H * SB_DIM), BF16),
        compiler_params=_params("parallel"),
    )(proj, proj, proj)


def _sb_bwd(name, proj, do, do_block0):
    nb = S // QB

    def body(q_ref, k_ref, v_ref, do_ref, dq_ref, dk_ref, dv_ref, sig_scr, dl_scr, dk_acc, dv_acc):
        m0b, m1b = _head_masks(BF16)
        tri_suffix = (_iota((QB, QB), 0) > _iota((QB, QB), 1)).astype(BF16)
        tri_prefix = (_iota((QB, QB), 0) < _iota((QB, QB), 1)).astype(BF16)
        dk_acc[...] = jnp.zeros_like(dk_acc)
        dv_acc[...] = jnp.zeros_like(dv_acc)

        def qblock(i, _):
            r0 = pl.multiple_of(i * QB, QB)
            rows = pl.ds(r0, QB)
            q = q_ref[rows, :].astype(BF16)
            qs = [q * m0b, q * m1b]
            dob = do_ref[rows, :].astype(BF16)
            dos = [dob * m0b, dob * m1b]
            rowg = r0 + _iota((QB, QB), 0)

            def sweep_left(step, cs):
                kb = i - step
                c0 = pl.multiple_of(kb * QB, QB)
                cols = pl.ds(c0, QB)
                k = k_ref[cols, :].astype(BF16)
                v = v_ref[cols, :].astype(BF16)
                before = (c0 + _iota((QB, QB), 1)) < rowg
                new_c = []
                dv = None
                for hh in range(2):
                    w, sig, tot = _sb_weights(qs[hh], k, cs[hh], before, tri_suffix)
                    new_c.append(cs[hh] + tot)
                    sig_scr[hh, kb] = sig
                    dl_scr[hh, kb] = _dot(dos[hh], v, "nt") * w
                    part = _dot(w.astype(BF16), dos[hh], "tn")
                    dv = part if dv is None else dv + part
                dv_acc[cols, :] += dv
                return tuple(new_c)

            lax.fori_loop(0, i + 1, sweep_left, (jnp.zeros((QB, 1), F32),) * 2)

            def sweep_right(kb, carry):
                ps, dq = carry
                c0 = pl.multiple_of(kb * QB, QB)
                cols = pl.ds(c0, QB)
                k = k_ref[cols, :].astype(BF16)
                before = (c0 + _iota((QB, QB), 1)) < rowg
                new_p = []
                dk = None
                for hh in range(2):
                    dl = dl_scr[hh, kb]
                    sig = sig_scr[hh, kb]
                    earlier = _split_dot(dl, tri_prefix) + ps[hh]
                    new_p.append(ps[hh] + jnp.sum(dl, axis=-1, keepdims=True))
                    dz = (jnp.where(before, dl * (1.0 - sig) - earlier * sig, 0.0) * SCALE_B).astype(BF16)
                    dq = dq + _dot(dz, k * (m0b if hh == 0 else m1b))
                    part = _dot(dz, qs[hh], "tn")
                    dk = part if dk is None else dk + part
                dk_acc[cols, :] += dk
                return tuple(new_p), dq

            init = ((jnp.zeros((QB, 1), F32),) * 2, jnp.zeros((QB, LANES), F32))
            _, dq = lax.fori_loop(0, i + 1, sweep_right, init)
            dq_ref[rows, :] = dq.astype(BF16)
            return 0

        lax.fori_loop(0, nb, qblock, 0)
        dk_ref[...] = dk_acc[...].astype(BF16)
        dv_ref[...] = dv_acc[...].astype(BF16)

    col = lambda base: pl.BlockSpec((S, LANES), lambda p: (0, base // LANES + p))
    out = pl.BlockSpec((S, LANES), lambda p: (0, p))
    shape = jax.ShapeDtypeStruct((S, SB_H * SB_DIM), BF16)
    return pl.pallas_call(
        body, name=name, grid=(SB_H // 2,),
        in_specs=[col(P_QB), col(P_KB), col(P_VB), pl.BlockSpec((S, LANES), lambda p: (0, do_block0 + p))],
        out_specs=[out, out, out], out_shape=[shape, shape, shape],
        scratch_shapes=[pltpu.VMEM((2, nb, QB, QB), F32), pltpu.VMEM((2, nb, QB, QB), F32),
                        pltpu.VMEM((S, LANES), F32), pltpu.VMEM((S, LANES), F32)],
        compiler_params=_params("parallel"),
    )(proj, proj, proj, do)


def _band_row_index():
    j = np.arange(TOEP_W)
    rel = np.clip(LEFT_CHUNKS * CHUNK - j, -REL_CLIP, REL_CLIP) + REL_CLIP
    rel[BAND_W:] = 2 * REL_CLIP
    return rel.astype(np.int32)


def _band_tiles(r0_ref, q_ref, kpad, vpad, m, m0b, m1b, static_ok, bias):
    r0 = pl.multiple_of(m * QB, QB)
    q = q_ref[0, pl.ds(r0, QB), :]
    kw = kpad[pl.ds(r0, BAND_W), :]
    vw = vpad[pl.ds(r0, BAND_W), :]
    ok = static_ok & ((r0 - BAND_PAD + _iota((QB, BAND_W), 1)) >= 0)
    qs = [q * m0b, q * m1b]
    ps = []
    for hh in range(2):
        s = jnp.where(ok, _dot(qs[hh], kw, "nt") * SCALE_C + bias[hh], NEG)
        e = jnp.exp(s - jnp.max(s, axis=-1, keepdims=True))
        ps.append(e * (1.0 / jnp.sum(e, axis=-1, keepdims=True)))
    return r0, qs, kw, vw, ps


def _band_setup(qkv_ref, r0_ref, kpad, vpad):
    kpad[0:BAND_PAD, :] = jnp.zeros((BAND_PAD, LANES), BF16)
    vpad[0:BAND_PAD, :] = jnp.zeros((BAND_PAD, LANES), BF16)
    kpad[BAND_PAD:, :] = qkv_ref[1]
    vpad[BAND_PAD:, :] = qkv_ref[2]
    jc = lax.shift_right_logical(_iota((QB, BAND_W), 1), 6)
    rc = lax.shift_right_logical(_iota((QB, BAND_W), 0), 6)
    static_ok = (jc >= rc) & (jc <= rc + LEFT_CHUNKS)
    bias = []
    for hh in range(2):
        row = jnp.broadcast_to(r0_ref[hh:hh + 1, :], (QB, TOEP_W))
        bias.append(pltpu.roll(row, 0, 1, stride=1, stride_axis=0)[:, :BAND_W])
    return static_ok, bias


def _band_fwd(name, qkv, r0):
    def body(qkv_ref, r0_ref, o_ref, kpad, vpad):
        m0b, m1b = _head_masks(BF16)
        static_ok, bias = _band_setup(qkv_ref, r0_ref, kpad, vpad)

        def qblock(m, _):
            r0_, _, _, vw, ps = _band_tiles(r0_ref, qkv_ref, kpad, vpad, m, m0b, m1b, static_ok, bias)
            o = _dot(ps[0].astype(BF16), vw * m0b) + _dot(ps[1].astype(BF16), vw * m1b)
            o_ref[pl.ds(r0_, QB), :] = o.astype(BF16)
            return 0

        lax.fori_loop(0, S // QB, qblock, 0)

    return pl.pallas_call(
        body, name=name, grid=(C_H // 2,),
        in_specs=[pl.BlockSpec((3, S, LANES), lambda p: (0, 0, p)), pl.BlockSpec((None, 2, TOEP_W), lambda p: (p, 0, 0))],
        out_specs=pl.BlockSpec((S, LANES), lambda p: (0, p)),
        out_shape=jax.ShapeDtypeStruct((S, C_H * C_DIM), BF16),
        scratch_shapes=[pltpu.VMEM((S + BAND_PAD, LANES), BF16), pltpu.VMEM((S + BAND_PAD, LANES), BF16)],
        compiler_params=_params("parallel"),
    )(qkv, r0)


def _band_bwd(name, qkv, r0, do):
    def body(qkv_ref, r0_ref, do_ref, dqkv_ref, dr0_ref, kpad, vpad, dkpad, dvpad, db_acc):
        m0b, m1b = _head_masks(BF16)
        static_ok, bias = _band_setup(qkv_ref, r0_ref, kpad, vpad)
        dkpad[...] = jnp.zeros_like(dkpad)
        dvpad[...] = jnp.zeros_like(dvpad)
        db_acc[...] = jnp.zeros_like(db_acc)

        def qblock(m, _):
            r0_, qs, kw, vw, ps = _band_tiles(r0_ref, qkv_ref, kpad, vpad, m, m0b, m1b, static_ok, bias)
            dob = do_ref[pl.ds(r0_, QB), :].astype(BF16)
            dos = [dob * m0b, dob * m1b]
            dq = None
            dk = None
            dv = None
            for hh in range(2):
                p = ps[hh]
                dp = _dot(dos[hh], vw, "nt")
                ds = p * (dp - jnp.sum(dp * p, axis=-1, keepdims=True))
                db_acc[hh, :, 0:BAND_W] += ds
                dsb = (ds * SCALE_C).astype(BF16)
                t = _dot(dsb, kw * (m0b if hh == 0 else m1b))
                dq = t if dq is None else dq + t
                t = _dot(dsb, qs[hh], "tn")
                dk = t if dk is None else dk + t
                t = _dot(p.astype(BF16), dos[hh], "tn")
                dv = t if dv is None else dv + t
            dqkv_ref[0, pl.ds(r0_, QB), :] = dq.astype(BF16)
            dkpad[pl.ds(r0_, BAND_W), :] += dk
            dvpad[pl.ds(r0_, BAND_W), :] += dv
            return 0

        lax.fori_loop(0, S // QB, qblock, 0)
        dqkv_ref[1] = dkpad[BAND_PAD:, :].astype(BF16)
        dqkv_ref[2] = dvpad[BAND_PAD:, :].astype(BF16)
        row = _iota((QB, TOEP_W), 0)
        for hh in range(2):
            unrolled = db_acc[hh]
            for bit in range(QB.bit_length() - 1):
                moved = pltpu.roll(unrolled, TOEP_W - (1 << bit), 1)
                unrolled = jnp.where((row & (1 << bit)) != 0, moved, unrolled)
            dr0_ref[hh:hh + 1, :] = jnp.sum(unrolled, axis=0, keepdims=True)

    return pl.pallas_call(
        body, name=name, grid=(C_H // 2,),
        in_specs=[pl.BlockSpec((3, S, LANES), lambda p: (0, 0, p)), pl.BlockSpec((None, 2, TOEP_W), lambda p: (p, 0, 0)),
                  pl.BlockSpec((S, LANES), lambda p: (0, p))],
        out_specs=[pl.BlockSpec((3, S, LANES), lambda p: (0, 0, p)), pl.BlockSpec((None, 2, TOEP_W), lambda p: (p, 0, 0))],
        out_shape=[jax.ShapeDtypeStruct((3, S, C_H * C_DIM), BF16), jax.ShapeDtypeStruct((C_H // 2, 2, TOEP_W), F32)],
        scratch_shapes=[pltpu.VMEM((S + BAND_PAD, LANES), BF16), pltpu.VMEM((S + BAND_PAD, LANES), BF16),
                        pltpu.VMEM((S + BAND_PAD, LANES), F32), pltpu.VMEM((S + BAND_PAD, LANES), F32),
                        pltpu.VMEM((2, QB, TOEP_W), F32)],
        compiler_params=_params("parallel"),
    )(qkv, r0, do)


def _bias_table_grad(name, dr0):
    w_out = 5 * LANES

    def body(d_ref, o_ref):
        j = _iota((TOEP_W, w_out), 0)
        rel = jnp.clip(LEFT_CHUNKS * CHUNK - j, -REL_CLIP, REL_CLIP) + REL_CLIP
        rel = jnp.where(j >= BAND_W, 2 * REL_CLIP, rel)
        onehot = (rel == _iota((TOEP_W, w_out), 1)).astype(BF16)
        d = d_ref[...]
        hi = d.astype(BF16)
        mid = (d - hi.astype(F32))
        mid_b = mid.astype(BF16)
        lo = (mid - mid_b.astype(F32)).astype(BF16)
        o_ref[...] = _dot(hi, onehot) + _dot(mid_b, onehot) + _dot(lo, onehot)

    return pl.pallas_call(
        body, name=name, out_shape=jax.ShapeDtypeStruct((C_H, w_out), F32),
        in_specs=[pl.BlockSpec((C_H, TOEP_W), lambda: (0, 0))], out_specs=pl.BlockSpec((C_H, w_out), lambda: (0, 0)),
        grid=(),
    )(dr0)


def _dense_weights(gw):
    w_in = jnp.moveaxis(gw["ev_w_in"], 0, 1).reshape(D, EVEN_IN)
    z = lambda n: jnp.zeros((D, n), BF16)
    w_in_p = jnp.concatenate(
        [w_in[:, 0:384], z(128), w_in[:, 384:640], w_in[:, 672:2208], z(KR_LANE), w_in[:, 640:672],
         z(LANES - KR_LANE - MLA_ROPE)], axis=1)
    w_uq = jnp.moveaxis(gw["ev_w_uq"], 0, 1).reshape(Q_LORA, MLA_H, MLA_NOPE + MLA_ROPE)
    w_uq_p = jnp.concatenate([w_uq, jnp.zeros((Q_LORA, MLA_H, LANES - MLA_NOPE - MLA_ROPE), BF16)], axis=2)
    w_ukv = jnp.moveaxis(gw["ev_w_ukv"], 0, 1).reshape(KV_LORA, MLA_H, MLA_NOPE + MLA_V)
    w_uk_p = jnp.concatenate([w_ukv[:, :, :MLA_NOPE], jnp.zeros((KV_LORA, MLA_H, LANES - MLA_NOPE), BF16)], axis=2)
    return dict(
        w_in=w_in_p, w_uq=w_uq_p.reshape(Q_LORA, MLA_H * LANES), w_uk=w_uk_p.reshape(KV_LORA, MLA_H * LANES),
        w_uv=w_ukv[:, :, MLA_NOPE:].reshape(KV_LORA, MLA_H * MLA_V),
        ev_w_out=gw["ev_w_out"].reshape(D, D), od_w_out=gw["od_w_out"].reshape(D, D),
        w_qkv=gw["od_w_qkv"], w_gate=gw["w_gate"], w_up=gw["w_up"], w_down=gw["w_down"])


def _proj_mm(name, u, w_in):
    return _mm(name, u, w_in, kind="nn", grid=(S // TM, 1, 1),
               a_spec=pl.BlockSpec((TM, D), lambda i, j, k: (i, 0)), b_spec=pl.BlockSpec((D, P_IN), lambda i, j, k: (0, 0)),
               o_spec=pl.BlockSpec((TM, P_IN), lambda i, j, k: (i, 0)), out_shape=(S, P_IN), out_dtype=F32, acc_shape=None)


def _out_proj(name, o, w, resid):
    return _mm(name, o, w, kind="nn", grid=(S // TM, 1, 1),
               a_spec=pl.BlockSpec((TM, D), lambda i, j, k: (i, 0)), b_spec=pl.BlockSpec((D, D), lambda i, j, k: (0, 0)),
               o_spec=pl.BlockSpec((TM, D), lambda i, j, k: (i, 0)), out_shape=(S, D), out_dtype=F32, acc_shape=None,
               resid=resid, r_spec=pl.BlockSpec((TM, D), lambda i, j, k: (i, 0)))


def _out_proj_bwd(name, dh, o, w):
    d_o = _mm(name + "_x", dh, w, kind="nt", grid=(S // TM, 1, 1),
              a_spec=pl.BlockSpec((TM, D), lambda i, j, k: (i, 0)), b_spec=pl.BlockSpec((D, D), lambda i, j, k: (0, 0)),
              o_spec=pl.BlockSpec((TM, D), lambda i, j, k: (i, 0)), out_shape=(S, D), out_dtype=F32, acc_shape=None)
    d_w = _mm(name + "_w", o, dh, kind="tn", grid=(2, S // TM),
              a_spec=pl.BlockSpec((TM, TM), lambda j, k: (k, j)), b_spec=pl.BlockSpec((TM, D), lambda j, k: (k, 0)),
              o_spec=pl.BlockSpec((TM, D), lambda j, k: (j, 0)), out_shape=(D, D), out_dtype=BF16, acc_shape=(TM, D))
    return d_o, d_w


def _local_step(x, tgt, gw, sm):
    w = _dense_weights(gw)
    cos_t, sin_t = _rope_tables()
    g_mix, g_ffn = sm["g_mix"], sm["g_ffn"]
    r0 = sm["od_rel_bias"][0][:, _band_row_index()].reshape(C_H // 2, 2, TOEP_W)
    nt = 3 * D // 256

    u0 = _rms_fwd("rms_mix0", x, g_mix[0:1])
    proj = _proj_mm("proj_in", u0, w["w_in"])
    qa, ka, va = _mla_prep_fwd("mla_prep", proj, sm["ev_g_cq"], sm["ev_g_ckv"], w["w_uq"], w["w_uk"], w["w_uv"], cos_t, sin_t)
    o_a, lse = _mla_fwd("mla_attn", qa, ka, va)
    o_b = _sb_fwd("sb_attn", proj)
    o_ev = jnp.concatenate([o_a.astype(BF16), o_b], axis=1)
    h1 = _out_proj("ev_out", o_ev, w["ev_w_out"], x)
    h2, gate0, up0 = _ffn_fwd("ffn0", h1, g_ffn[0:1], w["w_gate"], w["w_up"], w["w_down"], 0)
    u2 = _rms_fwd("rms_mix1", h2, g_mix[1:2])
    qkv = _mm("qkv", u2, w["w_qkv"], kind="nn", grid=(S // TM, nt, 1),
              a_spec=pl.BlockSpec((TM, D), lambda i, t, k: (i, 0)),
              b_spec=pl.BlockSpec((None, D, 256), lambda i, t, k: (t // 3, 0, t % 3)),
              o_spec=pl.BlockSpec((None, TM, 256), lambda i, t, k: (t // 4, i, t % 4)),
              out_shape=(3, S, D), out_dtype=BF16, acc_shape=None)
    o_od = _band_fwd("band_attn", qkv, r0)
    h3 = _out_proj("od_out", o_od, w["od_w_out"], h2)
    h4, gate1, up1 = _ffn_fwd("ffn1", h3, g_ffn[1:2], w["w_gate"], w["w_up"], w["w_down"], 1)

    loss, dh4, dg_final = _loss_bwd("loss", h4, sm["g_final"].reshape(1, D), tgt)

    dh3, dg_ffn1, u3, dgate, dup, act = _ffn_bwd("ffn1_bwd", dh4, h3, g_ffn[1:2], gate1, up1,
                                                 w["w_gate"], w["w_up"], w["w_down"], 1)
    d_wg1, d_wu1, d_wd1 = _ffn_wgrads("ffn1_dw", u3, dgate, dup, act, dh4)

    d_ood, d_w_od_out = _out_proj_bwd("od_out_bwd", dh3, o_od, w["od_w_out"])
    dqkv, dr0 = _band_bwd("band_attn_bwd", qkv, r0, d_ood)
    du2 = _mm("qkv_bwd_x", dqkv, w["w_qkv"], kind="nt", grid=(S // TM, nt),
              a_spec=pl.BlockSpec((None, TM, 256), lambda i, t: (t // 4, i, t % 4)),
              b_spec=pl.BlockSpec((None, D, 256), lambda i, t: (t // 3, 0, t % 3)),
              o_spec=pl.BlockSpec((TM, D), lambda i, t: (i, 0)), out_shape=(S, D), out_dtype=F32, acc_shape=(TM, D))
    d_w_qkv = _mm("qkv_bwd_w", u2, dqkv, kind="tn", grid=(nt, S // TM),
                  a_spec=pl.BlockSpec((TM, D), lambda t, k: (k, 0)),
                  b_spec=pl.BlockSpec((None, TM, 256), lambda t, k: (t // 4, k, t % 4)),
                  o_spec=pl.BlockSpec((None, D, 256), lambda t, k: (t // 3, 0, t % 3)),
                  out_shape=(N_CHIPS, D, 768), out_dtype=BF16, acc_shape=(D, 256))
    dh2, dg_mix1 = _rms_bwd("rms_mix1_bwd", du2, h2, g_mix[1:2], dh3)
    d_rel = _bias_table_grad("rel_bias_grad", dr0.reshape(C_H, TOEP_W))[:, :2 * REL_CLIP + 1]

    dh1, dg_ffn0, u1, dgate, dup, act = _ffn_bwd("ffn0_bwd", dh2, h1, g_ffn[0:1], gate0, up0,
                                                 w["w_gate"], w["w_up"], w["w_down"], 0)
    d_wg0, d_wu0, d_wd0 = _ffn_wgrads("ffn0_dw", u1, dgate, dup, act, dh2)

    d_oev, d_w_ev_out = _out_proj_bwd("ev_out_bwd", dh1, o_ev, w["ev_w_out"])
    dqa, dka, dva = _mla_bwd("mla_attn_bwd", qa, ka, va, o_a, lse, d_oev, 0)
    dqb, dkb, dvb = _sb_bwd("sb_attn_bwd", proj, d_oev, MLA_H * MLA_V // LANES)
    dcq, dckv, dkr, d_w_uq, d_w_uk, d_w_uv, dg_cq, dg_ckv = _mla_prep_bwd(
        "mla_prep_bwd", dqa, dka, dva, proj, sm["ev_g_cq"], sm["ev_g_ckv"], w["w_uq"], w["w_uk"], w["w_uv"], cos_t, sin_t)
    dproj = jnp.concatenate([dcq, jnp.zeros((S, LANES), BF16), dckv, dqb, dkb, dvb, dkr], axis=1)
    du0 = _mm("proj_in_bwd_x", dproj, w["w_in"], kind="nt", grid=(S // TM, 1, 1),
              a_spec=pl.BlockSpec((TM, P_IN), lambda i, j, k: (i, 0)), b_spec=pl.BlockSpec((D, P_IN), lambda i, j, k: (0, 0)),
              o_spec=pl.BlockSpec((TM, D), lambda i, j, k: (i, 0)), out_shape=(S, D), out_dtype=F32, acc_shape=None)
    d_w_in_p = _mm("proj_in_bwd_w", u0, dproj, kind="tn", grid=(1, S // TM),
                   a_spec=pl.BlockSpec((TM, D), lambda j, k: (k, 0)), b_spec=pl.BlockSpec((TM, P_IN), lambda j, k: (k, 0)),
                   o_spec=pl.BlockSpec((D, P_IN), lambda j, k: (0, 0)), out_shape=(D, P_IN), out_dtype=BF16,
                   acc_shape=(D, P_IN))
    grad_x, dg_mix0 = _rms_bwd("rms_mix0_bwd", du0, x, g_mix[0:1], dh1)

    d_w_in = jnp.concatenate([d_w_in_p[:, 0:384], d_w_in_p[:, 512:768],
                              d_w_in_p[:, P_KR + KR_LANE:P_KR + KR_LANE + MLA_ROPE], d_w_in_p[:, 768:2304]], axis=1)
    shard_cols = lambda a: jnp.moveaxis(a.reshape(a.shape[0], N_CHIPS, a.shape[1] // N_CHIPS), 1, 0)
    d_w_uq_std = d_w_uq.reshape(Q_LORA, MLA_H, LANES)[:, :, :MLA_NOPE + MLA_ROPE].reshape(Q_LORA, -1)
    d_w_ukv = jnp.concatenate([d_w_uk.reshape(KV_LORA, MLA_H, LANES)[:, :, :MLA_NOPE],
                               d_w_uv.reshape(KV_LORA, MLA_H, MLA_V)], axis=2).reshape(KV_LORA, -1)
    big = {
        "ev_w_in": shard_cols(d_w_in), "ev_w_uq": shard_cols(d_w_uq_std.astype(BF16)),
        "ev_w_ukv": shard_cols(d_w_ukv.astype(BF16)), "ev_w_out": d_w_ev_out.reshape(N_CHIPS, D // N_CHIPS, D),
        "od_w_qkv": d_w_qkv, "od_w_out": d_w_od_out.reshape(N_CHIPS, D // N_CHIPS, D),
        "w_gate0": d_wg0, "w_gate1": d_wg1, "w_up0": d_wu0, "w_up1": d_wu1, "w_down0": d_wd0, "w_down1": d_wd1,
    }
    small = {
        "ev_g_cq": dg_cq, "ev_g_ckv": dg_ckv, "od_rel_bias": d_rel.reshape(1, C_H, 2 * REL_CLIP + 1),
        "g_mix": jnp.concatenate([dg_mix0, dg_mix1], axis=0), "g_ffn": jnp.concatenate([dg_ffn0, dg_ffn1], axis=0),
        "g_final": dg_final.reshape(D),
    }
    return loss, grad_x, big, small


MESH = pl.DeviceIdType.MESH
ANY = pl.BlockSpec(memory_space=pl.ANY)
BIG = ("ev_w_in", "ev_w_uq", "ev_w_ukv", "ev_w_out", "od_w_qkv", "od_w_out", "w_gate", "w_up", "w_down")
SMALL = ("ev_g_cq", "ev_g_ckv", "od_rel_bias", "g_mix", "g_ffn", "g_final")
WEIGHTS = ("ev_w_in", "ev_g_cq", "ev_w_uq", "ev_g_ckv", "ev_w_ukv", "ev_w_out", "od_w_qkv", "od_rel_bias", "od_w_out",
           "g_mix", "g_ffn", "w_gate", "w_up", "w_down", "g_final")
GRAD_PARTS = (("ev_w_in", "ev_w_in", 0), ("ev_w_uq", "ev_w_uq", 0), ("ev_w_ukv", "ev_w_ukv", 0),
              ("ev_w_out", "ev_w_out", 0), ("od_w_qkv", "od_w_qkv", 0), ("od_w_out", "od_w_out", 0),
              ("w_gate0", "w_gate", 0), ("w_gate1", "w_gate", 1), ("w_up0", "w_up", 0), ("w_up1", "w_up", 1),
              ("w_down0", "w_down", 0), ("w_down1", "w_down", 1))
SMALL_ROWS = 112


def _row_tile(rows, cap=512):
    for t in range(min(rows, cap), 0, -1):
        if rows % t == 0 and t % 16 == 0:
            return t
    return rows


def _position():
    x, y, c = lax.axis_index("x"), lax.axis_index("y"), lax.axis_index("c")
    other_chips = [(1 - x, y), (x, 1 - y), (1 - x, 1 - y)]
    return x, y, c, other_chips


def _half_rows(c, half):
    return pl.ds(pl.multiple_of(c * half, 16), half)


def _cast_into_slot(name, w, pos):
    rows, cols = w.shape
    tr = _row_tile(rows)

    def body(pos_ref, w_ref, o_ref):
        o_ref[...] = w_ref[...].astype(BF16)

    return pl.pallas_call(
        body, name=name,
        grid_spec=pltpu.PrefetchScalarGridSpec(
            num_scalar_prefetch=1, grid=(rows // tr,),
            in_specs=[pl.BlockSpec((tr, cols), lambda i, p: (i, 0))],
            out_specs=pl.BlockSpec((None, tr, cols), lambda i, p: (p[0], i, 0))),
        out_shape=jax.ShapeDtypeStruct((N_CHIPS, rows, cols), BF16), compiler_params=_params("arbitrary"))(pos, w)


def _all_gather_weights(name, slots):
    n = len(slots)

    def body(*refs):
        g = refs[n:2 * n]
        send_sem, recv_sem = refs[2 * n:]
        x, y, c, chips = _position()
        me = 2 * x + y
        sibling = (x, y, 1 - c)

        def half(t, slot, cc):
            return g[t].at[slot, _half_rows(cc, slots[t].shape[1] // 2), :]

        def over_ici(t, j):
            return pltpu.make_async_remote_copy(
                src_ref=half(t, me, c), dst_ref=half(t, me, c), send_sem=send_sem.at[t, j],
                recv_sem=recv_sem.at[t, j], device_id=(*chips[j], c), device_id_type=MESH)

        def chip_slot(j):
            return 2 * chips[j][0] + chips[j][1]

        def to_sibling(t, j):
            return pltpu.make_async_remote_copy(
                src_ref=half(t, chip_slot(j), c), dst_ref=half(t, chip_slot(j), c), send_sem=send_sem.at[t, 3 + j],
                recv_sem=recv_sem.at[t, 3 + j], device_id=sibling, device_id_type=MESH)

        first = [[over_ici(t, j) for j in range(3)] for t in range(n)]
        for t in range(n):
            for j in range(3):
                first[t][j].start()
        passed = [[to_sibling(t, j) for j in range(3)] for t in range(n)]
        for t in range(n):
            for j in range(3):
                pltpu.make_async_remote_copy(
                    src_ref=half(t, chip_slot(j), c), dst_ref=half(t, chip_slot(j), c), send_sem=send_sem.at[t, j],
                    recv_sem=recv_sem.at[t, j], device_id=(*chips[j], c), device_id_type=MESH).wait_recv()
                passed[t][j].start()
        for t in range(n):
            for j in range(3):
                pltpu.make_async_remote_copy(
                    src_ref=half(t, chip_slot(j), 1 - c), dst_ref=half(t, chip_slot(j), 1 - c),
                    send_sem=send_sem.at[t, 3 + j], recv_sem=recv_sem.at[t, 3 + j], device_id=sibling,
                    device_id_type=MESH).wait_recv()
        for t in range(n):
            for j in range(3):
                first[t][j].wait_send()
                passed[t][j].wait_send()

    return pl.pallas_call(
        body, name=name, in_specs=[ANY] * n, out_specs=[ANY] * n,
        out_shape=[jax.ShapeDtypeStruct(s.shape, BF16) for s in slots],
        input_output_aliases={t: t for t in range(n)},
        scratch_shapes=[pltpu.SemaphoreType.DMA((n, 6)), pltpu.SemaphoreType.DMA((n, 6))],
    )(*slots)


def _pair_exchange(name, parts):
    n = len(parts)

    def body(*refs):
        f, theirs = refs[:n], refs[n:2 * n]
        send_sem, recv_sem = refs[2 * n:]
        x, y, c, _ = _position()
        out = [pltpu.make_async_remote_copy(
            src_ref=f[t].at[:, _half_rows(1 - c, parts[t].shape[1] // 2), :], dst_ref=theirs[t], send_sem=send_sem.at[t],
            recv_sem=recv_sem.at[t], device_id=(x, y, 1 - c), device_id_type=MESH) for t in range(n)]
        for cp in out:
            cp.start()
        for cp in out:
            cp.wait()

    return pl.pallas_call(
        body, name=name, in_specs=[ANY] * n, out_specs=[ANY] * n,
        out_shape=[jax.ShapeDtypeStruct((N_CHIPS, p.shape[1] // 2, p.shape[2]), BF16) for p in parts],
        scratch_shapes=[pltpu.SemaphoreType.DMA((n,)), pltpu.SemaphoreType.DMA((n,))],
    )(*parts)


def _pair_sum(name, part, theirs, pos):
    _, half, cols = theirs.shape
    tr = _row_tile(half)
    nb = half // tr

    def body(pos_ref, a_ref, b_ref, o_ref):
        o_ref[...] = (a_ref[...].astype(F32) + b_ref[...].astype(F32)).astype(BF16)

    return pl.pallas_call(
        body, name=name,
        grid_spec=pltpu.PrefetchScalarGridSpec(
            num_scalar_prefetch=1, grid=(N_CHIPS, nb),
            in_specs=[pl.BlockSpec((None, tr, cols), lambda s, i, p: (s, p[1] * nb + i, 0)),
                      pl.BlockSpec((None, tr, cols), lambda s, i, p: (s, i, 0))],
            out_specs=pl.BlockSpec((None, tr, cols), lambda s, i, p: (s, i, 0))),
        out_shape=jax.ShapeDtypeStruct(theirs.shape, BF16),
        compiler_params=_params("arbitrary", "arbitrary"))(pos, part, theirs)


def _chip_exchange(name, sums):
    n = len(sums)

    def body(*refs):
        r, got = refs[:n], refs[n:2 * n]
        send_sem, recv_sem = refs[2 * n:]
        x, y, c, chips = _position()
        out = [[pltpu.make_async_remote_copy(
            src_ref=r[t].at[2 * chips[j][0] + chips[j][1]], dst_ref=got[t].at[j], send_sem=send_sem.at[t, j],
            recv_sem=recv_sem.at[t, j], device_id=(*chips[j], c), device_id_type=MESH) for j in range(3)] for t in range(n)]
        for t in range(n):
            for j in range(3):
                out[t][j].start()
        for t in range(n):
            for j in range(3):
                out[t][j].wait()

    return pl.pallas_call(
        body, name=name, in_specs=[ANY] * n, out_specs=[ANY] * n,
        out_shape=[jax.ShapeDtypeStruct((3,) + s.shape[1:], BF16) for s in sums],
        scratch_shapes=[pltpu.SemaphoreType.DMA((n, 3)), pltpu.SemaphoreType.DMA((n, 3))],
    )(*sums)


def _chip_sum(name, sums, got, pos, layer, full_shape, full=None):
    _, half, cols = sums.shape
    tr = _row_tile(half)
    nb = half // tr

    def body(pos_ref, s_ref, g_ref, *rest):
        out_ref = rest[-1]
        out_ref[...] = ((s_ref[...].astype(F32) + g_ref[0].astype(F32)) + g_ref[1].astype(F32)) + g_ref[2].astype(F32)

    in_specs = [pl.BlockSpec((None, tr, cols), lambda i, p: (p[0], i, 0)),
                pl.BlockSpec((3, tr, cols), lambda i, p: (0, i, 0))]
    args = [pos, sums, got]
    if full is not None:
        in_specs.append(ANY)
        args.append(full)
    return pl.pallas_call(
        body, name=name,
        grid_spec=pltpu.PrefetchScalarGridSpec(
            num_scalar_prefetch=1, grid=(nb,), in_specs=in_specs,
            out_specs=pl.BlockSpec((None, tr, cols), lambda i, p: (layer, p[1] * nb + i, 0))),
        out_shape=jax.ShapeDtypeStruct(full_shape, F32),
        input_output_aliases={3: 0} if full is not None else {},
        compiler_params=_params("arbitrary"))(*args)


def _sibling_exchange(name, fulls):
    n = len(fulls)

    def body(*refs):
        g = refs[n:2 * n]
        send_sem, recv_sem = refs[2 * n:]
        x, y, c, _ = _position()

        def half(t, cc):
            return g[t].at[:, _half_rows(cc, fulls[t].shape[1] // 2), :]

        out = [pltpu.make_async_remote_copy(
            src_ref=half(t, c), dst_ref=half(t, c), send_sem=send_sem.at[t], recv_sem=recv_sem.at[t],
            device_id=(x, y, 1 - c), device_id_type=MESH) for t in range(n)]
        for cp in out:
            cp.start()
        for t in range(n):
            out[t].wait_send()
            pltpu.make_async_remote_copy(
                src_ref=half(t, 1 - c), dst_ref=half(t, 1 - c), send_sem=send_sem.at[t], recv_sem=recv_sem.at[t],
                device_id=(x, y, 1 - c), device_id_type=MESH).wait_recv()

    return pl.pallas_call(
        body, name=name, in_specs=[ANY] * n, out_specs=[ANY] * n,
        out_shape=[jax.ShapeDtypeStruct(f.shape, F32) for f in fulls],
        input_output_aliases={t: t for t in range(n)},
        scratch_shapes=[pltpu.SemaphoreType.DMA((n,)), pltpu.SemaphoreType.DMA((n,))],
    )(*fulls)


def _all_reduce_small(name, packed):
    n_dev = 8

    def body(p_ref, o_ref, slots, send_sem, recv_sem):
        x, y, c, _ = _position()
        me = 4 * x + 2 * y + c

        def peer(k):
            return (1 - x if k & 4 else x, 1 - y if k & 2 else y, 1 - c if k & 1 else c)

        def logical(k):
            px, py, pc = peer(k)
            return 4 * px + 2 * py + pc

        slots[me] = p_ref[...]
        sends = [pltpu.make_async_remote_copy(
            src_ref=p_ref, dst_ref=slots.at[me], send_sem=send_sem.at[k], recv_sem=recv_sem.at[k],
            device_id=peer(k), device_id_type=MESH) for k in range(1, n_dev)]
        for cp in sends:
            cp.start()
        for k in range(1, n_dev):
            pltpu.make_async_remote_copy(
                src_ref=p_ref, dst_ref=slots.at[logical(k)], send_sem=send_sem.at[k], recv_sem=recv_sem.at[k],
                device_id=peer(k), device_id_type=MESH).wait_recv()
        for cp in sends:
            cp.wait_send()
        total = slots[0]
        for d in range(1, n_dev):
            total = total + slots[d]
        o_ref[...] = total

    vm = pl.BlockSpec(memory_space=pltpu.VMEM)
    return pl.pallas_call(
        body, name=name, in_specs=[vm], out_specs=vm, out_shape=jax.ShapeDtypeStruct(packed.shape, F32),
        scratch_shapes=[pltpu.VMEM((n_dev,) + packed.shape, F32), pltpu.SemaphoreType.DMA((n_dev,)),
                        pltpu.SemaphoreType.DMA((n_dev,))],
    )(packed)


def _adamw(name, w, g, m, v):
    rows, cols = w.shape
    tr = _row_tile(rows)

    def body(w_ref, g_ref, m_ref, v_ref, d_ref, mo_ref, vo_ref):
        gv = g_ref[...]
        m_new = ADAM_B1 * m_ref[...] + (1.0 - ADAM_B1) * gv
        v_new = ADAM_B2 * v_ref[...] + (1.0 - ADAM_B2) * (gv * gv)
        m_hat = m_new / (1.0 - ADAM_B1 ** ADAM_STEP)
        v_hat = v_new / (1.0 - ADAM_B2 ** ADAM_STEP)
        d_ref[...] = -ADAM_LR * (m_hat / (jnp.sqrt(v_hat) + ADAM_EPS) + ADAM_WD * w_ref[...])
        mo_ref[...] = m_new
        vo_ref[...] = v_new

    spec = pl.BlockSpec((tr, cols), lambda i: (i, 0))
    shape = jax.ShapeDtypeStruct((rows, cols), F32)
    return pl.pallas_call(body, name=name, grid=(rows // tr,), in_specs=[spec] * 4, out_specs=[spec] * 3,
                          out_shape=[shape] * 3, compiler_params=_params("parallel"))(w, g, m, v)


def _pack_small(tree):
    flat = jnp.concatenate([tree[n].reshape(-1).astype(F32) for n in SMALL])
    return jnp.pad(flat, (0, SMALL_ROWS * LANES - flat.shape[0])).reshape(SMALL_ROWS, LANES)


def _unpack_small(packed, like):
    flat = packed.reshape(-1)
    out, off = {}, 0
    for n in SMALL:
        size = int(np.prod(like[n].shape))
        out[n] = flat[off:off + size].reshape(like[n].shape)
        off += size
    return out


def kernel(x, ev_w_in, ev_g_cq, ev_w_uq, ev_g_ckv, ev_w_ukv, ev_w_out, od_w_qkv, od_rel_bias, od_w_out, g_mix, g_ffn, w_gate, w_up, w_down, g_final, loss_target, m_ev_w_in, m_ev_g_cq, m_ev_w_uq, m_ev_g_ckv, m_ev_w_ukv, m_ev_w_out, m_od_w_qkv, m_od_rel_bias, m_od_w_out, m_g_mix, m_g_ffn, m_w_gate, m_w_up, m_w_down, m_g_final, v_ev_w_in, v_ev_g_cq, v_ev_w_uq, v_ev_g_ckv, v_ev_w_ukv, v_ev_w_out, v_od_w_qkv, v_od_rel_bias, v_od_w_out, v_g_mix, v_g_ffn, v_w_gate, v_w_up, v_w_down, v_g_final):
    w = dict(ev_w_in=ev_w_in, ev_g_cq=ev_g_cq, ev_w_uq=ev_w_uq, ev_g_ckv=ev_g_ckv, ev_w_ukv=ev_w_ukv, ev_w_out=ev_w_out,
             od_w_qkv=od_w_qkv, od_rel_bias=od_rel_bias, od_w_out=od_w_out, g_mix=g_mix, g_ffn=g_ffn, w_gate=w_gate,
             w_up=w_up, w_down=w_down, g_final=g_final)
    m = dict(ev_w_in=m_ev_w_in, ev_g_cq=m_ev_g_cq, ev_w_uq=m_ev_w_uq, ev_g_ckv=m_ev_g_ckv, ev_w_ukv=m_ev_w_ukv,
             ev_w_out=m_ev_w_out, od_w_qkv=m_od_w_qkv, od_rel_bias=m_od_rel_bias, od_w_out=m_od_w_out, g_mix=m_g_mix,
             g_ffn=m_g_ffn, w_gate=m_w_gate, w_up=m_w_up, w_down=m_w_down, g_final=m_g_final)
    v = dict(ev_w_in=v_ev_w_in, ev_g_cq=v_ev_g_cq, ev_w_uq=v_ev_w_uq, ev_g_ckv=v_ev_g_ckv, ev_w_ukv=v_ev_w_ukv,
             ev_w_out=v_ev_w_out, od_w_qkv=v_od_w_qkv, od_rel_bias=v_od_rel_bias, od_w_out=v_od_w_out, g_mix=v_g_mix,
             g_ffn=v_g_ffn, w_gate=v_w_gate, w_up=v_w_up, w_down=v_w_down, g_final=v_g_final)
    flat2d = lambda a: a.reshape(-1, a.shape[-1])

    pos = jnp.stack([2 * lax.axis_index("x") + lax.axis_index("y"), lax.axis_index("c")]).astype(jnp.int32)

    slots = [_cast_into_slot("cast_" + n, flat2d(w[n]), pos) for n in BIG]
    gw = dict(zip(BIG, _all_gather_weights("gather_weights", slots)))

    loss_local, grad_x, big, small = _local_step(x[0], loss_target[0], gw, {n: w[n] for n in SMALL})

    parts = [big[p] for p, _, _ in GRAD_PARTS]
    theirs = _pair_exchange("grads_pair", parts)
    sums = [_pair_sum("pair_sum_" + GRAD_PARTS[t][0], parts[t], theirs[t], pos) for t in range(len(parts))]
    arrived = _chip_exchange("grads_chips", sums)
    fulls = {}
    for t, (part_name, n, layer) in enumerate(GRAD_PARTS):
        fulls[n] = _chip_sum("chip_sum_" + part_name, sums[t], arrived[t], pos, layer, w[n].shape, fulls.get(n))
    grads = dict(zip(BIG, _sibling_exchange("grads_sibling", [fulls[n] for n in BIG])))
    small_sum = _all_reduce_small("small_sum", _pack_small(small))
    grads.update(_unpack_small(small_sum, w))

    delta, new_m, new_v = {}, {}, {}
    for n in BIG:
        d_, m_, v_ = _adamw("adamw_" + n, flat2d(w[n]), flat2d(grads[n]), flat2d(m[n]), flat2d(v[n]))
        delta[n], new_m[n], new_v[n] = d_.reshape(w[n].shape), m_.reshape(w[n].shape), v_.reshape(w[n].shape)
    d_, m_, v_ = _adamw("adamw_small", _pack_small(w), small_sum, _pack_small(m), _pack_small(v))
    delta.update(_unpack_small(d_, w))
    new_m.update(_unpack_small(m_, w))
    new_v.update(_unpack_small(v_, w))

    loss = lax.psum(loss_local[0, 0], ("x", "y", "c"))
    return (loss, grad_x[None], *[grads[n] for n in WEIGHTS], *[delta[n] for n in WEIGHTS],
            *[new_m[n] for n in WEIGHTS], *[new_v[n] for n in WEIGHTS])
```

```python
import functools

import jax
import jax.numpy as jnp
import numpy as np
from jax import lax
from jax.experimental import pallas as pl
from jax.experimental.pallas import tpu as pltpu

F32 = jnp.float32
BF16 = jnp.bfloat16

S = 2048
D = 1024
CHUNK = 64
MLA_H, MLA_NOPE, MLA_ROPE, MLA_V = 8, 64, 32, 64
Q_LORA, KV_LORA = 384, 256
ROPE_THETA = 10000.0
SB_H, SB_DIM = 8, 64
C_H, C_DIM = 16, 64
LEFT_CHUNKS = 8
REL_CLIP = 256
D_FF = 2816
EVEN_IN = 2208
RMS_EPS = 1e-6
ADAM_LR, ADAM_B1, ADAM_B2, ADAM_EPS, ADAM_WD, ADAM_STEP = 0.001, 0.9, 0.999, 1e-08, 0.01, 10

N_CHIPS = 4
FF_SHARD = D_FF // N_CHIPS
SCALE_A = (MLA_NOPE + MLA_ROPE) ** -0.5
SCALE_B = SB_DIM ** -0.5
SCALE_C = C_DIM ** -0.5
NEG = -1e30

LANES = 128
VMEM_LIMIT_BYTES = 56 * 1024 * 1024
TM = 512
QB = 256
BQ = 256

P_CQ, P_CKV, P_QB, P_KB, P_VB, P_KR = 0, 512, 768, 1280, 1792, 2304
P_IN = 2432
KR_LANE = 64
BAND_W = BQ + LEFT_CHUNKS * CHUNK
BAND_PAD = 512
TOEP_W = 1024


def _params(*sem):
    return pltpu.CompilerParams(dimension_semantics=sem, vmem_limit_bytes=VMEM_LIMIT_BYTES)


_DIMS = {"nn": (((1,), (0,)), ((), ())), "nt": (((1,), (1,)), ((), ())), "tn": (((0,), (0,)), ((), ()))}


def _dot(a, b, kind="nn"):
    return lax.dot_general(a, b, _DIMS[kind], preferred_element_type=F32)


def _iota(shape, dim):
    return lax.broadcasted_iota(jnp.int32, shape, dim)


def _sigmoid(x):
    return 1.0 / (1.0 + jnp.exp(-x))


def _softplus(x):
    return jnp.maximum(x, 0.0) + jnp.log(1.0 + jnp.exp(-jnp.abs(x)))


def _split_dot(x, tri):
    hi = x.astype(BF16)
    lo = (x - hi.astype(F32)).astype(BF16)
    return _dot(hi, tri) + _dot(lo, tri)


def _mm(name, a, b, *, kind, grid, a_spec, b_spec, o_spec, out_shape, out_dtype, acc_shape, resid=None, r_spec=None):
    nk = grid[-1]
    has_r = resid is not None

    def body(*refs):
        a_ref, b_ref = refs[0], refs[1]
        r_ref = refs[2] if has_r else None
        o_ref = refs[2 + has_r]
        part = _dot(a_ref[...].astype(BF16), b_ref[...].astype(BF16), kind)

        def finish(total):
            if has_r:
                total = total + r_ref[...].astype(F32)
            o_ref[...] = total.astype(out_dtype)

        if nk == 1:
            finish(part)
        else:
            acc_ref = refs[3 + has_r]
            k = pl.program_id(len(grid) - 1)

            @pl.when(k == 0)
            def _():
                acc_ref[...] = part

            @pl.when(k > 0)
            def _():
                acc_ref[...] += part

            @pl.when(k == nk - 1)
            def _():
                finish(acc_ref[...])

    in_specs = [a_spec, b_spec] + ([r_spec] if has_r else [])
    args = (a, b) + ((resid,) if has_r else ())
    sem = ("parallel",) * (len(grid) - 1) + ("arbitrary",)
    return pl.pallas_call(
        body, name=name, grid=grid, in_specs=in_specs, out_specs=o_spec,
        out_shape=jax.ShapeDtypeStruct(out_shape, out_dtype),
        scratch_shapes=[pltpu.VMEM(acc_shape, F32)] if nk > 1 else [],
        compiler_params=_params(*sem),
    )(*args)


def _rms_fwd(name, x, g, col_block=0):
    c = g.shape[1]

    def body(x_ref, g_ref, u_ref):
        xv = x_ref[...]
        r = lax.rsqrt(jnp.mean(xv * xv, axis=-1, keepdims=True) + RMS_EPS)
        u_ref[...] = (xv * r * g_ref[...]).astype(BF16)

    return pl.pallas_call(
        body, name=name, grid=(S // TM,),
        in_specs=[pl.BlockSpec((TM, c), lambda i: (i, col_block)), pl.BlockSpec((1, c), lambda i: (0, 0))],
        out_specs=pl.BlockSpec((TM, c), lambda i: (i, 0)),
        out_shape=jax.ShapeDtypeStruct((S, c), BF16),
        compiler_params=_params("parallel"),
    )(x, g)


def _rms_bwd(name, dy, x, g, resid):
    def body(dy_ref, x_ref, g_ref, r_ref, dx_ref, dg_ref):
        i = pl.program_id(0)
        xv = x_ref[...]
        r = lax.rsqrt(jnp.mean(xv * xv, axis=-1, keepdims=True) + RMS_EPS)
        xh = xv * r
        dyv = dy_ref[...]
        dxh = dyv * g_ref[...]
        dx_ref[...] = r_ref[...] + r * (dxh - xh * jnp.mean(dxh * xh, axis=-1, keepdims=True))
        part = jnp.sum(dyv * xh, axis=0, keepdims=True)

        @pl.when(i == 0)
        def _():
            dg_ref[...] = part

        @pl.when(i > 0)
        def _():
            dg_ref[...] += part

    row = pl.BlockSpec((TM, D), lambda i: (i, 0))
    vec = pl.BlockSpec((1, D), lambda i: (0, 0))
    return pl.pallas_call(
        body, name=name, grid=(S // TM,), in_specs=[row, row, vec, row], out_specs=[row, vec],
        out_shape=[jax.ShapeDtypeStruct((S, D), F32), jax.ShapeDtypeStruct((1, D), F32)],
        compiler_params=_params("arbitrary"),
    )(dy, x, g, resid)


def _loss_bwd(name, h, g, tgt):
    def body(h_ref, g_ref, t_ref, loss_ref, dh_ref, dg_ref):
        i = pl.program_id(0)
        xv = h_ref[...]
        gv = g_ref[...]
        r = lax.rsqrt(jnp.mean(xv * xv, axis=-1, keepdims=True) + RMS_EPS)
        xh = xv * r
        diff = xh * gv - t_ref[...]
        part_loss = 0.5 * jnp.sum(jnp.sum(diff * diff, axis=-1, keepdims=True) * (1.0 / D), axis=0, keepdims=True)
        dy = diff * (1.0 / D)
        dxh = dy * gv
        dh_ref[...] = r * (dxh - xh * jnp.mean(dxh * xh, axis=-1, keepdims=True))
        part_g = jnp.sum(dy * xh, axis=0, keepdims=True)

        @pl.when(i == 0)
        def _():
            dg_ref[...] = part_g
            loss_ref[...] = jnp.broadcast_to(part_loss, (1, LANES))

        @pl.when(i > 0)
        def _():
            dg_ref[...] += part_g
            loss_ref[...] += jnp.broadcast_to(part_loss, (1, LANES))

    row = pl.BlockSpec((TM, D), lambda i: (i, 0))
    vec = pl.BlockSpec((1, D), lambda i: (0, 0))
    return pl.pallas_call(
        body, name=name, grid=(S // TM,), in_specs=[row, vec, row],
        out_specs=[pl.BlockSpec((1, LANES), lambda i: (0, 0)), row, vec],
        out_shape=[jax.ShapeDtypeStruct((1, LANES), F32), jax.ShapeDtypeStruct((S, D), F32),
                   jax.ShapeDtypeStruct((1, D), F32)],
        compiler_params=_params("arbitrary"),
    )(h, g, tgt)


def _ffn_fwd(name, h, g, wg, wu, wd, layer):
    def body(h_ref, g_ref, wg_ref, wu_ref, wd_ref, o_ref, gate_ref, up_ref, u_scr):
        s = pl.program_id(1)

        @pl.when(s == 0)
        def _():
            xv = h_ref[...]
            r = lax.rsqrt(jnp.mean(xv * xv, axis=-1, keepdims=True) + RMS_EPS)
            u_scr[...] = (xv * r * g_ref[...]).astype(BF16)
            o_ref[...] = xv

        u = u_scr[...]
        gate = _dot(u, wg_ref[...])
        up = _dot(u, wu_ref[...])
        act = gate * _sigmoid(gate) * up
        o_ref[...] += _dot(act.astype(BF16), wd_ref[...])
        gate_ref[...] = gate.astype(BF16)
        up_ref[...] = up.astype(BF16)

    row = pl.BlockSpec((TM, D), lambda i, s: (i, 0))
    hid = pl.BlockSpec((None, TM, FF_SHARD), lambda i, s: (s, i, 0))
    return pl.pallas_call(
        body, name=name, grid=(S // TM, N_CHIPS),
        in_specs=[row, pl.BlockSpec((1, D), lambda i, s: (0, 0)),
                  pl.BlockSpec((None, D, FF_SHARD), lambda i, s: (s, layer, 0)),
                  pl.BlockSpec((None, D, FF_SHARD), lambda i, s: (s, layer, 0)),
                  pl.BlockSpec((None, FF_SHARD, D), lambda i, s: (s, layer, 0))],
        out_specs=[row, hid, hid],
        out_shape=[jax.ShapeDtypeStruct((S, D), F32), jax.ShapeDtypeStruct((N_CHIPS, S, FF_SHARD), BF16),
                   jax.ShapeDtypeStruct((N_CHIPS, S, FF_SHARD), BF16)],
        scratch_shapes=[pltpu.VMEM((TM, D), BF16)],
        compiler_params=_params("parallel", "arbitrary"),
    )(h, g, wg, wu, wd)


def _ffn_bwd(name, dh, h, g, gate, up, wg, wu, wd, layer):
    def body(dh_ref, h_ref, g_ref, gate_ref, up_ref, wg_ref, wu_ref, wd_ref,
             dhin_ref, dg_ref, u_ref, dgate_ref, dup_ref, act_ref, dhb_scr, du_scr):
        i = pl.program_id(0)
        s = pl.program_id(1)

        @pl.when(s == 0)
        def _():
            xv = h_ref[...]
            r = lax.rsqrt(jnp.mean(xv * xv, axis=-1, keepdims=True) + RMS_EPS)
            u_ref[...] = (xv * r * g_ref[...]).astype(BF16)
            dhb_scr[...] = dh_ref[...].astype(BF16)
            du_scr[...] = jnp.zeros_like(du_scr)

        dact = _dot(dhb_scr[...], wd_ref[...], "nt")
        gv = gate_ref[...].astype(F32)
        uv = up_ref[...].astype(F32)
        sig = _sigmoid(gv)
        sil = gv * sig
        dup = dact * sil
        dgate = dact * uv * (sig * (1.0 + gv * (1.0 - sig)))
        dgb = dgate.astype(BF16)
        dub = dup.astype(BF16)
        act_ref[...] = (sil * uv).astype(BF16)
        dgate_ref[...] = dgb
        dup_ref[...] = dub
        du_scr[...] += _dot(dgb, wg_ref[...], "nt") + _dot(dub, wu_ref[...], "nt")

        @pl.when(s == N_CHIPS - 1)
        def _():
            xv = h_ref[...]
            r = lax.rsqrt(jnp.mean(xv * xv, axis=-1, keepdims=True) + RMS_EPS)
            xh = xv * r
            du = du_scr[...]
            dxh = du * g_ref[...]
            dhin_ref[...] = dh_ref[...] + r * (dxh - xh * jnp.mean(dxh * xh, axis=-1, keepdims=True))
            part = jnp.sum(du * xh, axis=0, keepdims=True)

            @pl.when(i == 0)
            def _():
                dg_ref[...] = part

            @pl.when(i > 0)
            def _():
                dg_ref[...] += part

    row = pl.BlockSpec((TM, D), lambda i, s: (i, 0))
    vec = pl.BlockSpec((1, D), lambda i, s: (0, 0))
    hid = pl.BlockSpec((None, TM, FF_SHARD), lambda i, s: (s, i, 0))
    hid_shape = jax.ShapeDtypeStruct((N_CHIPS, S, FF_SHARD), BF16)
    return pl.pallas_call(
        body, name=name, grid=(S // TM, N_CHIPS),
        in_specs=[row, row, vec, hid, hid,
                  pl.BlockSpec((None, D, FF_SHARD), lambda i, s: (s, layer, 0)),
                  pl.BlockSpec((None, D, FF_SHARD), lambda i, s: (s, layer, 0)),
                  pl.BlockSpec((None, FF_SHARD, D), lambda i, s: (s, layer, 0))],
        out_specs=[row, vec, row, hid, hid, hid],
        out_shape=[jax.ShapeDtypeStruct((S, D), F32), jax.ShapeDtypeStruct((1, D), F32),
                   jax.ShapeDtypeStruct((S, D), BF16), hid_shape, hid_shape, hid_shape],
        scratch_shapes=[pltpu.VMEM((TM, D), BF16), pltpu.VMEM((TM, D), F32)],
        compiler_params=_params("arbitrary", "arbitrary"),
    )(dh, h, g, gate, up, wg, wu, wd)


def _ffn_wgrads(name, u, dgate, dup, act, dh):
    nk = S // TM
    tok = lambda s, k: (k, 0)
    hid = pl.BlockSpec((None, TM, FF_SHARD), lambda s, k: (s, k, 0))
    common = dict(kind="tn", grid=(N_CHIPS, nk), out_dtype=BF16)
    d_wg = _mm(name + "_g", u, dgate, a_spec=pl.BlockSpec((TM, D), tok), b_spec=hid,
               o_spec=pl.BlockSpec((None, D, FF_SHARD), lambda s, k: (s, 0, 0)),
               out_shape=(N_CHIPS, D, FF_SHARD), acc_shape=(D, FF_SHARD), **common)
    d_wu = _mm(name + "_u", u, dup, a_spec=pl.BlockSpec((TM, D), tok), b_spec=hid,
               o_spec=pl.BlockSpec((None, D, FF_SHARD), lambda s, k: (s, 0, 0)),
               out_shape=(N_CHIPS, D, FF_SHARD), acc_shape=(D, FF_SHARD), **common)
    d_wd = _mm(name + "_d", act, dh, a_spec=hid, b_spec=pl.BlockSpec((TM, D), tok),
               o_spec=pl.BlockSpec((None, FF_SHARD, D), lambda s, k: (s, 0, 0)),
               out_shape=(N_CHIPS, FF_SHARD, D), acc_shape=(FF_SHARD, D), **common)
    return d_wg, d_wu, d_wd


def _rope_tables():
    pos = jnp.arange(S, dtype=F32)
    inv = ROPE_THETA ** (-jnp.arange(0, MLA_ROPE, 2, dtype=F32) / MLA_ROPE)
    ang = pos[:, None] * inv[None, :]
    half = MLA_ROPE // 2
    cos = jnp.cos(ang)
    sin = jnp.sin(ang)
    one = jnp.ones((S, KR_LANE), F32)
    zero = jnp.zeros((S, KR_LANE), F32)
    tail_one = jnp.ones((S, LANES - KR_LANE - MLA_ROPE), F32)
    tail_zero = jnp.zeros((S, LANES - KR_LANE - MLA_ROPE), F32)
    cos_t = jnp.concatenate([one, cos, cos, tail_one], axis=1)
    sin_t = jnp.concatenate([zero, -sin, sin, tail_zero], axis=1)
    assert cos_t.shape == (S, LANES) and half * 2 == MLA_ROPE
    return cos_t, sin_t


def _rope(x, cos_t, sin_t, sign):
    n = x.shape[1] // LANES
    half = MLA_ROPE // 2
    lane = _iota(x.shape, 1) & (LANES - 1)
    first = (lane >= KR_LANE) & (lane < KR_LANE + half)
    swapped = jnp.where(first, pltpu.roll(x, x.shape[1] - half, 1), pltpu.roll(x, half, 1))
    c = jnp.tile(cos_t, (1, n)) if n > 1 else cos_t
    s = jnp.tile(sin_t, (1, n)) if n > 1 else sin_t
    return x * c + swapped * (s * sign)


def _mla_prep_fwd(name, proj, g_cq, g_ckv, w_uq, w_uk, w_uv, cos_t, sin_t):
    nh = MLA_H * LANES

    def body(cq_ref, ckv_ref, kr_ref, gq_ref, gkv_ref, wq_ref, wk_ref, wv_ref, cos_ref, sin_ref,
             qa_ref, ka_ref, va_ref):
        cos_v, sin_v = cos_ref[...], sin_ref[...]
        cq = cq_ref[...]
        r = lax.rsqrt(jnp.mean(cq * cq, axis=-1, keepdims=True) + RMS_EPS)
        cqn = (cq * r * gq_ref[...]).astype(BF16)
        qa_ref[...] = _rope(_dot(cqn, wq_ref[...]), cos_v, sin_v, 1.0).astype(BF16)
        ckv = ckv_ref[...]
        r = lax.rsqrt(jnp.mean(ckv * ckv, axis=-1, keepdims=True) + RMS_EPS)
        ckvn = (ckv * r * gkv_ref[...]).astype(BF16)
        lane = _iota((TM, LANES), 1)
        rot = (lane >= KR_LANE) & (lane < KR_LANE + MLA_ROPE)
        kr = jnp.where(rot, _rope(kr_ref[...], cos_v, sin_v, 1.0), 0.0)
        ka_ref[...] = (_dot(ckvn, wk_ref[...]) + jnp.tile(kr, (1, MLA_H))).astype(BF16)
        va_ref[...] = _dot(ckvn, wv_ref[...]).astype(BF16)

    full = lambda shape: pl.BlockSpec(shape, lambda i: (0, 0))
    return pl.pallas_call(
        body, name=name, grid=(S // TM,),
        in_specs=[pl.BlockSpec((TM, Q_LORA), lambda i: (i, P_CQ // Q_LORA)),
                  pl.BlockSpec((TM, KV_LORA), lambda i: (i, P_CKV // KV_LORA)),
                  pl.BlockSpec((TM, LANES), lambda i: (i, P_KR // LANES)),
                  full((1, Q_LORA)), full((1, KV_LORA)), full((Q_LORA, nh)), full((KV_LORA, nh)),
                  full((KV_LORA, MLA_H * MLA_V)),
                  pl.BlockSpec((TM, LANES), lambda i: (i, 0)), pl.BlockSpec((TM, LANES), lambda i: (i, 0))],
        out_specs=[pl.BlockSpec((TM, nh), lambda i: (i, 0)), pl.BlockSpec((TM, nh), lambda i: (i, 0)),
                   pl.BlockSpec((TM, MLA_H * MLA_V), lambda i: (i, 0))],
        out_shape=[jax.ShapeDtypeStruct((S, nh), BF16), jax.ShapeDtypeStruct((S, nh), BF16),
                   jax.ShapeDtypeStruct((S, MLA_H * MLA_V), BF16)],
        compiler_params=_params("parallel"),
    )(proj, proj, proj, g_cq, g_ckv, w_uq, w_uk, w_uv, cos_t, sin_t)


def _mla_prep_bwd(name, dqa, dka, dva, proj, g_cq, g_ckv, w_uq, w_uk, w_uv, cos_t, sin_t):
    nh = MLA_H * LANES

    def body(dqa_ref, dka_ref, dva_ref, cq_ref, ckv_ref, gq_ref, gkv_ref, wq_ref, wk_ref, wv_ref, cos_ref, sin_ref,
             dcq_ref, dckv_ref, dkr_ref, dwq_ref, dwk_ref, dwv_ref, dgq_ref, dgkv_ref):
        i = pl.program_id(0)
        cos_v, sin_v = cos_ref[...], sin_ref[...]

        def norm_bwd(x, g, dn):
            r = lax.rsqrt(jnp.mean(x * x, axis=-1, keepdims=True) + RMS_EPS)
            xh = x * r
            dxh = dn * g
            dx = r * (dxh - xh * jnp.mean(dxh * xh, axis=-1, keepdims=True))
            return dx, jnp.sum(dn * xh, axis=0, keepdims=True), (xh * g).astype(BF16)

        dq = _rope(dqa_ref[...], cos_v, sin_v, -1.0).astype(BF16)
        dcqn = _dot(dq, wq_ref[...], "nt")
        dcq, dgq, cqn = norm_bwd(cq_ref[...], gq_ref[...], dcqn)
        dcq_ref[...] = dcq.astype(BF16)
        dwq = _dot(cqn, dq, "tn")

        dka = dka_ref[...]
        dkab = dka.astype(BF16)
        dvab = dva_ref[...].astype(BF16)
        dckvn = _dot(dkab, wk_ref[...], "nt") + _dot(dvab, wv_ref[...], "nt")
        dckv, dgkv, ckvn = norm_bwd(ckv_ref[...], gkv_ref[...], dckvn)
        dckv_ref[...] = dckv.astype(BF16)
        dwk = _dot(ckvn, dkab, "tn")
        dwv = _dot(ckvn, dvab, "tn")

        fold = dka[:, 0:LANES]
        for hh in range(1, MLA_H):
            fold = fold + dka[:, hh * LANES:(hh + 1) * LANES]
        lane = _iota((TM, LANES), 1)
        rot = (lane >= KR_LANE) & (lane < KR_LANE + MLA_ROPE)
        dkr = _rope(jnp.where(rot, fold, 0.0), cos_v, sin_v, -1.0)
        dkr_ref[...] = jnp.where(rot, dkr, 0.0).astype(BF16)

        @pl.when(i == 0)
        def _():
            dwq_ref[...] = dwq
            dwk_ref[...] = dwk
            dwv_ref[...] = dwv
            dgq_ref[...] = dgq
            dgkv_ref[...] = dgkv

        @pl.when(i > 0)
        def _():
            dwq_ref[...] += dwq
            dwk_ref[...] += dwk
            dwv_ref[...] += dwv
            dgq_ref[...] += dgq
            dgkv_ref[...] += dgkv

    full = lambda shape: pl.BlockSpec(shape, lambda i: (0, 0))
    rows = lambda c: pl.BlockSpec((TM, c), lambda i: (i, 0))
    nv = MLA_H * MLA_V
    return pl.pallas_call(
        body, name=name, grid=(S // TM,),
        in_specs=[rows(nh), rows(nh), rows(nv),
                  pl.BlockSpec((TM, Q_LORA), lambda i: (i, P_CQ // Q_LORA)),
                  pl.BlockSpec((TM, KV_LORA), lambda i: (i, P_CKV // KV_LORA)),
                  full((1, Q_LORA)), full((1, KV_LORA)), full((Q_LORA, nh)), full((KV_LORA, nh)), full((KV_LORA, nv)),
                  rows(LANES), rows(LANES)],
        out_specs=[rows(Q_LORA), rows(KV_LORA), rows(LANES), full((Q_LORA, nh)), full((KV_LORA, nh)),
                   full((KV_LORA, nv)), full((1, Q_LORA)), full((1, KV_LORA))],
        out_shape=[jax.ShapeDtypeStruct((S, Q_LORA), BF16), jax.ShapeDtypeStruct((S, KV_LORA), BF16),
                   jax.ShapeDtypeStruct((S, LANES), BF16), jax.ShapeDtypeStruct((Q_LORA, nh), F32),
                   jax.ShapeDtypeStruct((KV_LORA, nh), F32), jax.ShapeDtypeStruct((KV_LORA, nv), F32),
                   jax.ShapeDtypeStruct((1, Q_LORA), F32), jax.ShapeDtypeStruct((1, KV_LORA), F32)],
        compiler_params=_params("arbitrary"),
    )(dqa, dka, dva, proj, proj, g_cq, g_ckv, w_uq, w_uk, w_uv, cos_t, sin_t)


def _head_masks(dtype):
    lane = _iota((1, LANES), 1)
    return (lane < 64).astype(dtype), (lane >= 64).astype(dtype)


def _mla_fwd(name, qa, ka, va):
    def body(q_ref, k_ref, v_ref, o_ref, lse_ref):
        m0b, m1b = _head_masks(BF16)
        lane = _iota((QB, LANES), 1)
        left = lane < 64

        def qblock(i, _):
            r0 = pl.multiple_of(i * QB, QB)
            qs = [q_ref[pl.ds(r0, QB), hh * LANES:(hh + 1) * LANES] for hh in range(2)]
            rowc = lax.shift_right_logical(r0 + _iota((QB, QB), 0), 6)

            def kv(kb, carry):
                ms, ls, acc = carry
                c0 = pl.multiple_of(kb * QB, QB)
                v = v_ref[pl.ds(c0, QB), :]
                ok = lax.shift_right_logical(c0 + _iota((QB, QB), 1), 6) <= rowc
                new_m, new_l, alphas = [], [], []
                pv = None
                for hh in range(2):
                    k = k_ref[pl.ds(c0, QB), hh * LANES:(hh + 1) * LANES]
                    s = jnp.where(ok, _dot(qs[hh], k, "nt") * SCALE_A, NEG)
                    mn = jnp.maximum(ms[hh], jnp.max(s, axis=-1, keepdims=True))
                    p = jnp.exp(s - mn)
                    a = jnp.exp(ms[hh] - mn)
                    new_m.append(mn)
                    new_l.append(a * ls[hh] + jnp.sum(p, axis=-1, keepdims=True))
                    alphas.append(a)
                    part = _dot(p.astype(BF16), v * (m0b if hh == 0 else m1b))
                    pv = part if pv is None else pv + part
                acc = acc * jnp.where(left, alphas[0], alphas[1]) + pv
                return tuple(new_m), tuple(new_l), acc

            init = ((jnp.full((QB, 1), NEG, F32),) * 2, (jnp.zeros((QB, 1), F32),) * 2, jnp.zeros((QB, LANES), F32))
            ms, ls, acc = lax.fori_loop(0, i + 1, kv, init)
            o_ref[pl.ds(r0, QB), :] = acc * jnp.where(left, 1.0 / ls[0], 1.0 / ls[1])
            lse_ref[pl.ds(r0, QB), :] = jnp.where(left, ms[0] + jnp.log(ls[0]), ms[1] + jnp.log(ls[1]))
            return 0

        lax.fori_loop(0, S // QB, qblock, 0)

    pair = lambda w: pl.BlockSpec((S, w), lambda p: (0, p))
    return pl.pallas_call(
        body, name=name, grid=(MLA_H // 2,), in_specs=[pair(2 * LANES), pair(2 * LANES), pair(LANES)],
        out_specs=[pair(LANES), pair(LANES)],
        out_shape=[jax.ShapeDtypeStruct((S, MLA_H * MLA_V), F32), jax.ShapeDtypeStruct((S, MLA_H * MLA_V), F32)],
        compiler_params=_params("parallel"),
    )(qa, ka, va)


def _mla_bwd(name, qa, ka, va, o, lse, do, do_block0):
    def body(q_ref, k_ref, v_ref, o_ref, lse_ref, do_ref, dq_ref, dk_ref, dv_ref):
        m0f, m1f = _head_masks(F32)
        m0b, m1b = _head_masks(BF16)
        dk_ref[...] = jnp.zeros_like(dk_ref)
        dv_ref[...] = jnp.zeros_like(dv_ref)

        def qblock(i, _):
            r0 = pl.multiple_of(i * QB, QB)
            rows = pl.ds(r0, QB)
            do_f = do_ref[rows, :]
            prod = do_f * o_ref[rows, :]
            deltas = [jnp.sum(prod * m0f, axis=-1, keepdims=True), jnp.sum(prod * m1f, axis=-1, keepdims=True)]
            lse_v = lse_ref[rows, :]
            lses = [lse_v[:, 0:1], lse_v[:, 64:65]]
            dob = do_f.astype(BF16)
            dos = [dob * m0b, dob * m1b]
            qs = [q_ref[rows, hh * LANES:(hh + 1) * LANES] for hh in range(2)]
            rowc = lax.shift_right_logical(r0 + _iota((QB, QB), 0), 6)

            def kv(kb, dqs):
                c0 = pl.multiple_of(kb * QB, QB)
                cols = pl.ds(c0, QB)
                v = v_ref[cols, :]
                ok = lax.shift_right_logical(c0 + _iota((QB, QB), 1), 6) <= rowc
                out = []
                dv = None
                for hh in range(2):
                    k = k_ref[cols, hh * LANES:(hh + 1) * LANES]
                    s = _dot(qs[hh], k, "nt") * SCALE_A
                    p = jnp.where(ok, jnp.exp(s - lses[hh]), 0.0)
                    dp = _dot(dos[hh], v, "nt")
                    ds = (p * (dp - deltas[hh]) * SCALE_A).astype(BF16)
                    out.append(dqs[hh] + _dot(ds, k))
                    dk_ref[cols, hh * LANES:(hh + 1) * LANES] += _dot(ds, qs[hh], "tn")
                    part = _dot(p.astype(BF16), dos[hh], "tn")
                    dv = part if dv is None else dv + part
                dv_ref[cols, :] += dv
                return tuple(out)

            dqs = lax.fori_loop(0, i + 1, kv, (jnp.zeros((QB, LANES), F32),) * 2)
            for hh in range(2):
                dq_ref[rows, hh * LANES:(hh + 1) * LANES] = dqs[hh]
            return 0

        lax.fori_loop(0, S // QB, qblock, 0)

    pair = lambda w: pl.BlockSpec((S, w), lambda p: (0, p))
    return pl.pallas_call(
        body, name=name, grid=(MLA_H // 2,),
        in_specs=[pair(2 * LANES), pair(2 * LANES), pair(LANES), pair(LANES), pair(LANES),
                  pl.BlockSpec((S, LANES), lambda p: (0, do_block0 + p))],
        out_specs=[pair(2 * LANES), pair(2 * LANES), pair(LANES)],
        out_shape=[jax.ShapeDtypeStruct((S, MLA_H * LANES), F32), jax.ShapeDtypeStruct((S, MLA_H * LANES), F32),
                   jax.ShapeDtypeStruct((S, MLA_H * MLA_V), F32)],
        compiler_params=_params("parallel"),
    )(qa, ka, va, o, lse, do)


def _sb_weights(q_h, k, c, before, tri_suffix):
    z = _dot(q_h, k, "nt") * SCALE_B
    sp = _softplus(z)
    log_keep = jnp.where(before, -sp, 0.0)
    log_between = _split_dot(log_keep, tri_suffix) + c
    w = jnp.where(before, jnp.exp(z - sp + log_between), 0.0)
    return w, jnp.exp(z - sp), jnp.sum(log_keep, axis=-1, keepdims=True)


def _sb_fwd(name, proj):
    def body(q_ref, k_ref, v_ref, o_ref):
        m0b, m1b = _head_masks(BF16)
        tri_suffix = (_iota((QB, QB), 0) > _iota((QB, QB), 1)).astype(BF16)

        def qblock(i, _):
            r0 = pl.multiple_of(i * QB, QB)
            q = q_ref[pl.ds(r0, QB), :].astype(BF16)
            qs = [q * m0b, q * m1b]
            rowg = r0 + _iota((QB, QB), 0)

            def kv(step, carry):
                cs, acc = carry
                c0 = pl.multiple_of((i - step) * QB, QB)
                k = k_ref[pl.ds(c0, QB), :].astype(BF16)
                v = v_ref[pl.ds(c0, QB), :].astype(BF16)
                before = (c0 + _iota((QB, QB), 1)) < rowg
                new_c = []
                for hh in range(2):
                    w, _, tot = _sb_weights(qs[hh], k, cs[hh], before, tri_suffix)
                    new_c.append(cs[hh] + tot)
                    acc = acc + _dot(w.astype(BF16), v * (m0b if hh == 0 else m1b))
                return tuple(new_c), acc

            init = ((jnp.zeros((QB, 1), F32),) * 2, jnp.zeros((QB, LANES), F32))
            _, acc = lax.fori_loop(0, i + 1, kv, init)
            o_ref[pl.ds(r0, QB), :] = acc.astype(BF16)
            return 0

        lax.fori_loop(0, S // QB, qblock, 0)

    col = lambda base: pl.BlockSpec((S, LANES), lambda p: (0, base // LANES + p))
    return pl.pallas_call(
        body, name=name, grid=(SB_H // 2,), in_specs=[col(P_QB), col(P_KB), col(P_VB)],
        out_specs=pl.BlockSpec((S, LANES), lambda p: (0, p)),
        out_shape=jax.ShapeDtypeStruct((S, SB_H * SB_DIM), BF16),
        compiler_params=_params("parallel"),
    )(proj, proj, proj)


def _sb_bwd(name, proj, do, do_block0):
    nb = S // QB

    def body(q_ref, k_ref, v_ref, do_ref, dq_ref, dk_ref, dv_ref, sig_scr, dl_scr, dk_acc, dv_acc):
        m0b, m1b = _head_masks(BF16)
        tri_suffix = (_iota((QB, QB), 0) > _iota((QB, QB), 1)).astype(BF16)
        tri_prefix = (_iota((QB, QB), 0) < _iota((QB, QB), 1)).astype(BF16)
        dk_acc[...] = jnp.zeros_like(dk_acc)
        dv_acc[...] = jnp.zeros_like(dv_acc)

        def qblock(i, _):
            r0 = pl.multiple_of(i * QB, QB)
            rows = pl.ds(r0, QB)
            q = q_ref[rows, :].astype(BF16)
            qs = [q * m0b, q * m1b]
            dob = do_ref[rows, :].astype(BF16)
            dos = [dob * m0b, dob * m1b]
            rowg = r0 + _iota((QB, QB), 0)

            def sweep_left(step, cs):
                kb = i - step
                c0 = pl.multiple_of(kb * QB, QB)
                cols = pl.ds(c0, QB)
                k = k_ref[cols, :].astype(BF16)
                v = v_ref[cols, :].astype(BF16)
                before = (c0 + _iota((QB, QB), 1)) < rowg
                new_c = []
                dv = None
                for hh in range(2):
                    w, sig, tot = _sb_weights(qs[hh], k, cs[hh], before, tri_suffix)
                    new_c.append(cs[hh] + tot)
                    sig_scr[hh, kb] = sig
                    dl_scr[hh, kb] = _dot(dos[hh], v, "nt") * w
                    part = _dot(w.astype(BF16), dos[hh], "tn")
                    dv = part if dv is None else dv + part
                dv_acc[cols, :] += dv
                return tuple(new_c)

            lax.fori_loop(0, i + 1, sweep_left, (jnp.zeros((QB, 1), F32),) * 2)

            def sweep_right(kb, carry):
                ps, dq = carry
                c0 = pl.multiple_of(kb * QB, QB)
                cols = pl.ds(c0, QB)
                k = k_ref[cols, :].astype(BF16)
                before = (c0 + _iota((QB, QB), 1)) < rowg
                new_p = []
                dk = None
                for hh in range(2):
                    dl = dl_scr[hh, kb]
                    sig = sig_scr[hh, kb]
                    earlier = _split_dot(dl, tri_prefix) + ps[hh]
                    new_p.append(ps[hh] + jnp.sum(dl, axis=-1, keepdims=True))
                    dz = (jnp.where(before, dl * (1.0 - sig) - earlier * sig, 0.0) * SCALE_B).astype(BF16)
                    dq = dq + _dot(dz, k * (m0b if hh == 0 else m1b))
                    part = _dot(dz, qs[hh], "tn")
                    dk = part if dk is None else dk + part
                dk_acc[cols, :] += dk
                return tuple(new_p), dq

            init = ((jnp.zeros((QB, 1), F32),) * 2, jnp.zeros((QB, LANES), F32))
            _, dq = lax.fori_loop(0, i + 1, sweep_right, init)
            dq_ref[rows, :] = dq.astype(BF16)
            return 0

        lax.fori_loop(0, nb, qblock, 0)
        dk_ref[...] = dk_acc[...].astype(BF16)
        dv_ref[...] = dv_acc[...].astype(BF16)

    col = lambda base: pl.BlockSpec((S, LANES), lambda p: (0, base // LANES + p))
    out = pl.BlockSpec((S, LANES), lambda p: (0, p))
    shape = jax.ShapeDtypeStruct((S, SB_H * SB_DIM), BF16)
    return pl.pallas_call(
        body, name=name, grid=(SB_H // 2,),
        in_specs=[col(P_QB), col(P_KB), col(P_VB), pl.BlockSpec((S, LANES), lambda p: (0, do_block0 + p))],
        out_specs=[out, out, out], out_shape=[shape, shape, shape],
        scratch_shapes=[pltpu.VMEM((2, nb, QB, QB), F32), pltpu.VMEM((2, nb, QB, QB), F32),
                        pltpu.VMEM((S, LANES), F32), pltpu.VMEM((S, LANES), F32)],
        compiler_params=_params("parallel"),
    )(proj, proj, proj, do)


def _band_row_index():
    j = np.arange(TOEP_W)
    rel = np.clip(LEFT_CHUNKS * CHUNK - j, -REL_CLIP, REL_CLIP) + REL_CLIP
    rel[BAND_W:] = 2 * REL_CLIP
    return rel.astype(np.int32)


def _band_tiles(r0_ref, q_ref, kpad, vpad, m, m0b, m1b, static_ok, bias):
    r0 = pl.multiple_of(m * BQ, BQ)
    q = q_ref[0, pl.ds(r0, BQ), :]
    kw = kpad[pl.ds(r0, BAND_W), :]
    vw = vpad[pl.ds(r0, BAND_W), :]
    ok = static_ok & ((r0 - BAND_PAD + _iota((BQ, BAND_W), 1)) >= 0)
    qs = [q * m0b, q * m1b]
    ps = []
    for hh in range(2):
        s = jnp.where(ok, _dot(qs[hh], kw, "nt") * SCALE_C + bias[hh], NEG)
        e = jnp.exp(s - jnp.max(s, axis=-1, keepdims=True))
        ps.append(e * (1.0 / jnp.sum(e, axis=-1, keepdims=True)))
    return r0, qs, kw, vw, ps


def _band_setup(qkv_ref, r0_ref, kpad, vpad):
    kpad[0:BAND_PAD, :] = jnp.zeros((BAND_PAD, LANES), BF16)
    vpad[0:BAND_PAD, :] = jnp.zeros((BAND_PAD, LANES), BF16)
    kpad[BAND_PAD:, :] = qkv_ref[1]
    vpad[BAND_PAD:, :] = qkv_ref[2]
    jc = lax.shift_right_logical(_iota((BQ, BAND_W), 1), 6)
    rc = lax.shift_right_logical(_iota((BQ, BAND_W), 0), 6)
    static_ok = (jc >= rc) & (jc <= rc + LEFT_CHUNKS)
    bias = []
    for hh in range(2):
        row = jnp.broadcast_to(r0_ref[hh:hh + 1, :], (BQ, TOEP_W))
        bias.append(pltpu.roll(row, 0, 1, stride=1, stride_axis=0)[:, :BAND_W])
    return static_ok, bias


def _band_fwd(name, qkv, r0):
    def body(qkv_ref, r0_ref, o_ref, kpad, vpad):
        m0b, m1b = _head_masks(BF16)
        static_ok, bias = _band_setup(qkv_ref, r0_ref, kpad, vpad)

        def qblock(m, _):
            r0_, _, _, vw, ps = _band_tiles(r0_ref, qkv_ref, kpad, vpad, m, m0b, m1b, static_ok, bias)
            o = _dot(ps[0].astype(BF16), vw * m0b) + _dot(ps[1].astype(BF16), vw * m1b)
            o_ref[pl.ds(r0_, BQ), :] = o.astype(BF16)
            return 0

        lax.fori_loop(0, S // BQ, qblock, 0)

    return pl.pallas_call(
        body, name=name, grid=(C_H // 2,),
        in_specs=[pl.BlockSpec((3, S, LANES), lambda p: (0, 0, p)), pl.BlockSpec((None, 2, TOEP_W), lambda p: (p, 0, 0))],
        out_specs=pl.BlockSpec((S, LANES), lambda p: (0, p)),
        out_shape=jax.ShapeDtypeStruct((S, C_H * C_DIM), BF16),
        scratch_shapes=[pltpu.VMEM((S + BAND_PAD, LANES), BF16), pltpu.VMEM((S + BAND_PAD, LANES), BF16)],
        compiler_params=_params("parallel"),
    )(qkv, r0)


def _band_bwd(name, qkv, r0, do):
    def body(qkv_ref, r0_ref, do_ref, dqkv_ref, dr0_ref, kpad, vpad, dkpad, dvpad, db_acc):
        m0b, m1b = _head_masks(BF16)
        static_ok, bias = _band_setup(qkv_ref, r0_ref, kpad, vpad)
        dkpad[...] = jnp.zeros_like(dkpad)
        dvpad[...] = jnp.zeros_like(dvpad)
        db_acc[...] = jnp.zeros_like(db_acc)

        def qblock(m, _):
            r0_, qs, kw, vw, ps = _band_tiles(r0_ref, qkv_ref, kpad, vpad, m, m0b, m1b, static_ok, bias)
            dob = do_ref[pl.ds(r0_, BQ), :].astype(BF16)
            dos = [dob * m0b, dob * m1b]
            dq = None
            dk = None
            dv = None
            for hh in range(2):
                p = ps[hh]
                dp = _dot(dos[hh], vw, "nt")
                ds = p * (dp - jnp.sum(dp * p, axis=-1, keepdims=True))
                db_acc[hh, :, 0:BAND_W] += ds
                dsb = (ds * SCALE_C).astype(BF16)
                t = _dot(dsb, kw * (m0b if hh == 0 else m1b))
                dq = t if dq is None else dq + t
                t = _dot(dsb, qs[hh], "tn")
                dk = t if dk is None else dk + t
                t = _dot(p.astype(BF16), dos[hh], "tn")
                dv = t if dv is None else dv + t
            dqkv_ref[0, pl.ds(r0_, BQ), :] = dq.astype(BF16)
            dkpad[pl.ds(r0_, BAND_W), :] += dk
            dvpad[pl.ds(r0_, BAND_W), :] += dv
            return 0

        lax.fori_loop(0, S // BQ, qblock, 0)
        dqkv_ref[1] = dkpad[BAND_PAD:, :].astype(BF16)
        dqkv_ref[2] = dvpad[BAND_PAD:, :].astype(BF16)
        sub = _iota((8, TOEP_W), 0)
        for hh in range(2):
            folded = db_acc[hh, 0:8, :]
            for a in range(1, BQ // 8):
                folded = folded + pltpu.roll(db_acc[hh, 8 * a:8 * a + 8, :], TOEP_W - 8 * a, 1)
            for bit in range(3):
                moved = pltpu.roll(folded, TOEP_W - (1 << bit), 1)
                folded = jnp.where((sub & (1 << bit)) != 0, moved, folded)
            dr0_ref[hh:hh + 1, :] = jnp.sum(folded, axis=0, keepdims=True)

    return pl.pallas_call(
        body, name=name, grid=(C_H // 2,),
        in_specs=[pl.BlockSpec((3, S, LANES), lambda p: (0, 0, p)), pl.BlockSpec((None, 2, TOEP_W), lambda p: (p, 0, 0)),
                  pl.BlockSpec((S, LANES), lambda p: (0, p))],
        out_specs=[pl.BlockSpec((3, S, LANES), lambda p: (0, 0, p)), pl.BlockSpec((None, 2, TOEP_W), lambda p: (p, 0, 0))],
        out_shape=[jax.ShapeDtypeStruct((3, S, C_H * C_DIM), BF16), jax.ShapeDtypeStruct((C_H // 2, 2, TOEP_W), F32)],
        scratch_shapes=[pltpu.VMEM((S + BAND_PAD, LANES), BF16), pltpu.VMEM((S + BAND_PAD, LANES), BF16),
                        pltpu.VMEM((S + BAND_PAD, LANES), F32), pltpu.VMEM((S + BAND_PAD, LANES), F32),
                        pltpu.VMEM((2, BQ, TOEP_W), F32)],
        compiler_params=_params("parallel"),
    )(qkv, r0, do)


def _bias_table_grad(name, dr0):
    w_out = 5 * LANES

    def body(d_ref, o_ref):
        j = _iota((TOEP_W, w_out), 0)
        rel = jnp.clip(LEFT_CHUNKS * CHUNK - j, -REL_CLIP, REL_CLIP) + REL_CLIP
        rel = jnp.where(j >= BAND_W, 2 * REL_CLIP, rel)
        onehot = (rel == _iota((TOEP_W, w_out), 1)).astype(BF16)
        d = d_ref[...]
        hi = d.astype(BF16)
        mid = (d - hi.astype(F32))
        mid_b = mid.astype(BF16)
        lo = (mid - mid_b.astype(F32)).astype(BF16)
        o_ref[...] = _dot(hi, onehot) + _dot(mid_b, onehot) + _dot(lo, onehot)

    return pl.pallas_call(
        body, name=name, out_shape=jax.ShapeDtypeStruct((C_H, w_out), F32),
        in_specs=[pl.BlockSpec((C_H, TOEP_W), lambda: (0, 0))], out_specs=pl.BlockSpec((C_H, w_out), lambda: (0, 0)),
        grid=(),
    )(dr0)


def _dense_weights(gw):
    w_in = jnp.moveaxis(gw["ev_w_in"], 0, 1).reshape(D, EVEN_IN)
    z = lambda n: jnp.zeros((D, n), BF16)
    w_in_p = jnp.concatenate(
        [w_in[:, 0:384], z(128), w_in[:, 384:640], w_in[:, 672:2208], z(KR_LANE), w_in[:, 640:672],
         z(LANES - KR_LANE - MLA_ROPE)], axis=1)
    w_uq = jnp.moveaxis(gw["ev_w_uq"], 0, 1).reshape(Q_LORA, MLA_H, MLA_NOPE + MLA_ROPE)
    w_uq_p = jnp.concatenate([w_uq, jnp.zeros((Q_LORA, MLA_H, LANES - MLA_NOPE - MLA_ROPE), BF16)], axis=2)
    w_ukv = jnp.moveaxis(gw["ev_w_ukv"], 0, 1).reshape(KV_LORA, MLA_H, MLA_NOPE + MLA_V)
    w_uk_p = jnp.concatenate([w_ukv[:, :, :MLA_NOPE], jnp.zeros((KV_LORA, MLA_H, LANES - MLA_NOPE), BF16)], axis=2)
    return dict(
        w_in=w_in_p, w_uq=w_uq_p.reshape(Q_LORA, MLA_H * LANES), w_uk=w_uk_p.reshape(KV_LORA, MLA_H * LANES),
        w_uv=w_ukv[:, :, MLA_NOPE:].reshape(KV_LORA, MLA_H * MLA_V),
        ev_w_out=gw["ev_w_out"].reshape(D, D), od_w_out=gw["od_w_out"].reshape(D, D),
        w_qkv=gw["od_w_qkv"], w_gate=gw["w_gate"], w_up=gw["w_up"], w_down=gw["w_down"])


def _proj_mm(name, u, w_in):
    return _mm(name, u, w_in, kind="nn", grid=(S // TM, 1, 1),
               a_spec=pl.BlockSpec((TM, D), lambda i, j, k: (i, 0)), b_spec=pl.BlockSpec((D, P_IN), lambda i, j, k: (0, 0)),
               o_spec=pl.BlockSpec((TM, P_IN), lambda i, j, k: (i, 0)), out_shape=(S, P_IN), out_dtype=F32, acc_shape=None)


def _out_proj(name, o, w, resid):
    return _mm(name, o, w, kind="nn", grid=(S // TM, 1, 1),
               a_spec=pl.BlockSpec((TM, D), lambda i, j, k: (i, 0)), b_spec=pl.BlockSpec((D, D), lambda i, j, k: (0, 0)),
               o_spec=pl.BlockSpec((TM, D), lambda i, j, k: (i, 0)), out_shape=(S, D), out_dtype=F32, acc_shape=None,
               resid=resid, r_spec=pl.BlockSpec((TM, D), lambda i, j, k: (i, 0)))


def _out_proj_bwd(name, dh, o, w):
    d_o = _mm(name + "_x", dh, w, kind="nt", grid=(S // TM, 1, 1),
              a_spec=pl.BlockSpec((TM, D), lambda i, j, k: (i, 0)), b_spec=pl.BlockSpec((D, D), lambda i, j, k: (0, 0)),
              o_spec=pl.BlockSpec((TM, D), lambda i, j, k: (i, 0)), out_shape=(S, D), out_dtype=F32, acc_shape=None)
    d_w = _mm(name + "_w", o, dh, kind="tn", grid=(2, S // TM),
              a_spec=pl.BlockSpec((TM, TM), lambda j, k: (k, j)), b_spec=pl.BlockSpec((TM, D), lambda j, k: (k, 0)),
              o_spec=pl.BlockSpec((TM, D), lambda j, k: (j, 0)), out_shape=(D, D), out_dtype=BF16, acc_shape=(TM, D))
    return d_o, d_w


def _local_step(x, tgt, gw, sm):
    w = _dense_weights(gw)
    cos_t, sin_t = _rope_tables()
    g_mix, g_ffn = sm["g_mix"], sm["g_ffn"]
    r0 = sm["od_rel_bias"][0][:, _band_row_index()].reshape(C_H // 2, 2, TOEP_W)
    nt = 3 * D // 256

    u0 = _rms_fwd("rms_mix0", x, g_mix[0:1])
    proj = _proj_mm("proj_in", u0, w["w_in"])
    qa, ka, va = _mla_prep_fwd("mla_prep", proj, sm["ev_g_cq"], sm["ev_g_ckv"], w["w_uq"], w["w_uk"], w["w_uv"], cos_t, sin_t)
    o_a, lse = _mla_fwd("mla_attn", qa, ka, va)
    o_b = _sb_fwd("sb_attn", proj)
    o_ev = jnp.concatenate([o_a.astype(BF16), o_b], axis=1)
    h1 = _out_proj("ev_out", o_ev, w["ev_w_out"], x)
    h2, gate0, up0 = _ffn_fwd("ffn0", h1, g_ffn[0:1], w["w_gate"], w["w_up"], w["w_down"], 0)
    u2 = _rms_fwd("rms_mix1", h2, g_mix[1:2])
    qkv = _mm("qkv", u2, w["w_qkv"], kind="nn", grid=(S // TM, nt, 1),
              a_spec=pl.BlockSpec((TM, D), lambda i, t, k: (i, 0)),
              b_spec=pl.BlockSpec((None, D, 256), lambda i, t, k: (t // 3, 0, t % 3)),
              o_spec=pl.BlockSpec((None, TM, 256), lambda i, t, k: (t // 4, i, t % 4)),
              out_shape=(3, S, D), out_dtype=BF16, acc_shape=None)
    o_od = _band_fwd("band_attn", qkv, r0)
    h3 = _out_proj("od_out", o_od, w["od_w_out"], h2)
    h4, gate1, up1 = _ffn_fwd("ffn1", h3, g_ffn[1:2], w["w_gate"], w["w_up"], w["w_down"], 1)

    loss, dh4, dg_final = _loss_bwd("loss", h4, sm["g_final"].reshape(1, D), tgt)

    dh3, dg_ffn1, u3, dgate, dup, act = _ffn_bwd("ffn1_bwd", dh4, h3, g_ffn[1:2], gate1, up1,
                                                 w["w_gate"], w["w_up"], w["w_down"], 1)
    d_wg1, d_wu1, d_wd1 = _ffn_wgrads("ffn1_dw", u3, dgate, dup, act, dh4)

    d_ood, d_w_od_out = _out_proj_bwd("od_out_bwd", dh3, o_od, w["od_w_out"])
    dqkv, dr0 = _band_bwd("band_attn_bwd", qkv, r0, d_ood)
    du2 = _mm("qkv_bwd_x", dqkv, w["w_qkv"], kind="nt", grid=(S // TM, nt),
              a_spec=pl.BlockSpec((None, TM, 256), lambda i, t: (t // 4, i, t % 4)),
              b_spec=pl.BlockSpec((None, D, 256), lambda i, t: (t // 3, 0, t % 3)),
              o_spec=pl.BlockSpec((TM, D), lambda i, t: (i, 0)), out_shape=(S, D), out_dtype=F32, acc_shape=(TM, D))
    d_w_qkv = _mm("qkv_bwd_w", u2, dqkv, kind="tn", grid=(nt, S // TM),
                  a_spec=pl.BlockSpec((TM, D), lambda t, k: (k, 0)),
                  b_spec=pl.BlockSpec((None, TM, 256), lambda t, k: (t // 4, k, t % 4)),
                  o_spec=pl.BlockSpec((None, D, 256), lambda t, k: (t // 3, 0, t % 3)),
                  out_shape=(N_CHIPS, D, 768), out_dtype=BF16, acc_shape=(D, 256))
    dh2, dg_mix1 = _rms_bwd("rms_mix1_bwd", du2, h2, g_mix[1:2], dh3)
    d_rel = _bias_table_grad("rel_bias_grad", dr0.reshape(C_H, TOEP_W))[:, :2 * REL_CLIP + 1]

    dh1, dg_ffn0, u1, dgate, dup, act = _ffn_bwd("ffn0_bwd", dh2, h1, g_ffn[0:1], gate0, up0,
                                                 w["w_gate"], w["w_up"], w["w_down"], 0)
    d_wg0, d_wu0, d_wd0 = _ffn_wgrads("ffn0_dw", u1, dgate, dup, act, dh2)

    d_oev, d_w_ev_out = _out_proj_bwd("ev_out_bwd", dh1, o_ev, w["ev_w_out"])
    dqa, dka, dva = _mla_bwd("mla_attn_bwd", qa, ka, va, o_a, lse, d_oev, 0)
    dqb, dkb, dvb = _sb_bwd("sb_attn_bwd", proj, d_oev, MLA_H * MLA_V // LANES)
    dcq, dckv, dkr, d_w_uq, d_w_uk, d_w_uv, dg_cq, dg_ckv = _mla_prep_bwd(
        "mla_prep_bwd", dqa, dka, dva, proj, sm["ev_g_cq"], sm["ev_g_ckv"], w["w_uq"], w["w_uk"], w["w_uv"], cos_t, sin_t)
    dproj = jnp.concatenate([dcq, jnp.zeros((S, LANES), BF16), dckv, dqb, dkb, dvb, dkr], axis=1)
    du0 = _mm("proj_in_bwd_x", dproj, w["w_in"], kind="nt", grid=(S // TM, 1, 1),
              a_spec=pl.BlockSpec((TM, P_IN), lambda i, j, k: (i, 0)), b_spec=pl.BlockSpec((D, P_IN), lambda i, j, k: (0, 0)),
              o_spec=pl.BlockSpec((TM, D), lambda i, j, k: (i, 0)), out_shape=(S, D), out_dtype=F32, acc_shape=None)
    d_w_in_p = _mm("proj_in_bwd_w", u0, dproj, kind="tn", grid=(1, S // TM),
                   a_spec=pl.BlockSpec((TM, D), lambda j, k: (k, 0)), b_spec=pl.BlockSpec((TM, P_IN), lambda j, k: (k, 0)),
                   o_spec=pl.BlockSpec((D, P_IN), lambda j, k: (0, 0)), out_shape=(D, P_IN), out_dtype=BF16,
                   acc_shape=(D, P_IN))
    grad_x, dg_mix0 = _rms_bwd("rms_mix0_bwd", du0, x, g_mix[0:1], dh1)

    d_w_in = jnp.concatenate([d_w_in_p[:, 0:384], d_w_in_p[:, 512:768],
                              d_w_in_p[:, P_KR + KR_LANE:P_KR + KR_LANE + MLA_ROPE], d_w_in_p[:, 768:2304]], axis=1)
    shard_cols = lambda a: jnp.moveaxis(a.reshape(a.shape[0], N_CHIPS, a.shape[1] // N_CHIPS), 1, 0)
    d_w_uq_std = d_w_uq.reshape(Q_LORA, MLA_H, LANES)[:, :, :MLA_NOPE + MLA_ROPE].reshape(Q_LORA, -1)
    d_w_ukv = jnp.concatenate([d_w_uk.reshape(KV_LORA, MLA_H, LANES)[:, :, :MLA_NOPE],
                               d_w_uv.reshape(KV_LORA, MLA_H, MLA_V)], axis=2).reshape(KV_LORA, -1)
    big = {
        "ev_w_in": shard_cols(d_w_in), "ev_w_uq": shard_cols(d_w_uq_std.astype(BF16)),
        "ev_w_ukv": shard_cols(d_w_ukv.astype(BF16)), "ev_w_out": d_w_ev_out.reshape(N_CHIPS, D // N_CHIPS, D),
        "od_w_qkv": d_w_qkv, "od_w_out": d_w_od_out.reshape(N_CHIPS, D // N_CHIPS, D),
        "w_gate0": d_wg0, "w_gate1": d_wg1, "w_up0": d_wu0, "w_up1": d_wu1, "w_down0": d_wd0, "w_down1": d_wd1,
    }
    small = {
        "ev_g_cq": dg_cq, "ev_g_ckv": dg_ckv, "od_rel_bias": d_rel.reshape(1, C_H, 2 * REL_CLIP + 1),
        "g_mix": jnp.concatenate([dg_mix0, dg_mix1], axis=0), "g_ffn": jnp.concatenate([dg_ffn0, dg_ffn1], axis=0),
        "g_final": dg_final.reshape(D),
    }
    return loss, grad_x, big, small


MESH = pl.DeviceIdType.MESH
ANY = pl.BlockSpec(memory_space=pl.ANY)
BIG = ("ev_w_in", "ev_w_uq", "ev_w_ukv", "ev_w_out", "od_w_qkv", "od_w_out", "w_gate", "w_up", "w_down")
SMALL = ("ev_g_cq", "ev_g_ckv", "od_rel_bias", "g_mix", "g_ffn", "g_final")
WEIGHTS = ("ev_w_in", "ev_g_cq", "ev_w_uq", "ev_g_ckv", "ev_w_ukv", "ev_w_out", "od_w_qkv", "od_rel_bias", "od_w_out",
           "g_mix", "g_ffn", "w_gate", "w_up", "w_down", "g_final")
GRAD_PARTS = (("ev_w_in", "ev_w_in", 0), ("ev_w_uq", "ev_w_uq", 0), ("ev_w_ukv", "ev_w_ukv", 0),
              ("ev_w_out", "ev_w_out", 0), ("od_w_qkv", "od_w_qkv", 0), ("od_w_out", "od_w_out", 0),
              ("w_gate0", "w_gate", 0), ("w_gate1", "w_gate", 1), ("w_up0", "w_up", 0), ("w_up1", "w_up", 1),
              ("w_down0", "w_down", 0), ("w_down1", "w_down", 1))
SMALL_ROWS = 112


def _row_tile(rows, cap=512):
    for t in range(min(rows, cap), 0, -1):
        if rows % t == 0 and t % 16 == 0:
            return t
    return rows


def _position():
    x, y, c = lax.axis_index("x"), lax.axis_index("y"), lax.axis_index("c")
    other_chips = [(1 - x, y), (x, 1 - y), (1 - x, 1 - y)]
    return x, y, c, other_chips


def _half_rows(c, half):
    return pl.ds(pl.multiple_of(c * half, 16), half)


def _cast_into_slot(name, w, pos):
    rows, cols = w.shape
    tr = _row_tile(rows)

    def body(pos_ref, w_ref, o_ref):
        o_ref[...] = w_ref[...].astype(BF16)

    return pl.pallas_call(
        body, name=name,
        grid_spec=pltpu.PrefetchScalarGridSpec(
            num_scalar_prefetch=1, grid=(rows // tr,),
            in_specs=[pl.BlockSpec((tr, cols), lambda i, p: (i, 0))],
            out_specs=pl.BlockSpec((None, tr, cols), lambda i, p: (p[0], i, 0))),
        out_shape=jax.ShapeDtypeStruct((N_CHIPS, rows, cols), BF16), compiler_params=_params("arbitrary"))(pos, w)


def _all_gather_weights(name, slots):
    n = len(slots)

    def body(*refs):
        g = refs[n:2 * n]
        send_sem, recv_sem = refs[2 * n:]
        x, y, c, chips = _position()
        me = 2 * x + y
        sibling = (x, y, 1 - c)

        def half(t, slot, cc):
            return g[t].at[slot, _half_rows(cc, slots[t].shape[1] // 2), :]

        def over_ici(t, j):
            return pltpu.make_async_remote_copy(
                src_ref=half(t, me, c), dst_ref=half(t, me, c), send_sem=send_sem.at[t, j],
                recv_sem=recv_sem.at[t, j], device_id=(*chips[j], c), device_id_type=MESH)

        def chip_slot(j):
            return 2 * chips[j][0] + chips[j][1]

        def to_sibling(t, j):
            return pltpu.make_async_remote_copy(
                src_ref=half(t, chip_slot(j), c), dst_ref=half(t, chip_slot(j), c), send_sem=send_sem.at[t, 3 + j],
                recv_sem=recv_sem.at[t, 3 + j], device_id=sibling, device_id_type=MESH)

        first = [[over_ici(t, j) for j in range(3)] for t in range(n)]
        for t in range(n):
            for j in range(3):
                first[t][j].start()
        passed = [[to_sibling(t, j) for j in range(3)] for t in range(n)]
        for t in range(n):
            for j in range(3):
                pltpu.make_async_remote_copy(
                    src_ref=half(t, chip_slot(j), c), dst_ref=half(t, chip_slot(j), c), send_sem=send_sem.at[t, j],
                    recv_sem=recv_sem.at[t, j], device_id=(*chips[j], c), device_id_type=MESH).wait_recv()
                passed[t][j].start()
        for t in range(n):
            for j in range(3):
                pltpu.make_async_remote_copy(
                    src_ref=half(t, chip_slot(j), 1 - c), dst_ref=half(t, chip_slot(j), 1 - c),
                    send_sem=send_sem.at[t, 3 + j], recv_sem=recv_sem.at[t, 3 + j], device_id=sibling,
                    device_id_type=MESH).wait_recv()
        for t in range(n):
            for j in range(3):
                first[t][j].wait_send()
                passed[t][j].wait_send()

    return pl.pallas_call(
        body, name=name, in_specs=[ANY] * n, out_specs=[ANY] * n,
        out_shape=[jax.ShapeDtypeStruct(s.shape, BF16) for s in slots],
        input_output_aliases={t: t for t in range(n)},
        scratch_shapes=[pltpu.SemaphoreType.DMA((n, 6)), pltpu.SemaphoreType.DMA((n, 6))],
    )(*slots)


def _pair_exchange(name, parts):
    n = len(parts)

    def body(*refs):
        f, theirs = refs[:n], refs[n:2 * n]
        send_sem, recv_sem = refs[2 * n:]
        x, y, c, _ = _position()
        out = [pltpu.make_async_remote_copy(
            src_ref=f[t].at[:, _half_rows(1 - c, parts[t].shape[1] // 2), :], dst_ref=theirs[t], send_sem=send_sem.at[t],
            recv_sem=recv_sem.at[t], device_id=(x, y, 1 - c), device_id_type=MESH) for t in range(n)]
        for cp in out:
            cp.start()
        for cp in out:
            cp.wait()

    return pl.pallas_call(
        body, name=name, in_specs=[ANY] * n, out_specs=[ANY] * n,
        out_shape=[jax.ShapeDtypeStruct((N_CHIPS, p.shape[1] // 2, p.shape[2]), BF16) for p in parts],
        scratch_shapes=[pltpu.SemaphoreType.DMA((n,)), pltpu.SemaphoreType.DMA((n,))],
    )(*parts)


def _pair_sum(name, part, theirs, pos):
    _, half, cols = theirs.shape
    tr = _row_tile(half)
    nb = half // tr

    def body(pos_ref, a_ref, b_ref, o_ref):
        o_ref[...] = (a_ref[...].astype(F32) + b_ref[...].astype(F32)).astype(BF16)

    return pl.pallas_call(
        body, name=name,
        grid_spec=pltpu.PrefetchScalarGridSpec(
            num_scalar_prefetch=1, grid=(N_CHIPS, nb),
            in_specs=[pl.BlockSpec((None, tr, cols), lambda s, i, p: (s, p[1] * nb + i, 0)),
                      pl.BlockSpec((None, tr, cols), lambda s, i, p: (s, i, 0))],
            out_specs=pl.BlockSpec((None, tr, cols), lambda s, i, p: (s, i, 0))),
        out_shape=jax.ShapeDtypeStruct(theirs.shape, BF16),
        compiler_params=_params("arbitrary", "arbitrary"))(pos, part, theirs)


def _chip_exchange(name, sums):
    n = len(sums)

    def body(*refs):
        r, got = refs[:n], refs[n:2 * n]
        send_sem, recv_sem = refs[2 * n:]
        x, y, c, chips = _position()
        out = [[pltpu.make_async_remote_copy(
            src_ref=r[t].at[2 * chips[j][0] + chips[j][1]], dst_ref=got[t].at[j], send_sem=send_sem.at[t, j],
            recv_sem=recv_sem.at[t, j], device_id=(*chips[j], c), device_id_type=MESH) for j in range(3)] for t in range(n)]
        for t in range(n):
            for j in range(3):
                out[t][j].start()
        for t in range(n):
            for j in range(3):
                out[t][j].wait()

    return pl.pallas_call(
        body, name=name, in_specs=[ANY] * n, out_specs=[ANY] * n,
        out_shape=[jax.ShapeDtypeStruct((3,) + s.shape[1:], BF16) for s in sums],
        scratch_shapes=[pltpu.SemaphoreType.DMA((n, 3)), pltpu.SemaphoreType.DMA((n, 3))],
    )(*sums)


def _chip_sum(name, sums, got, pos, layer, full_shape, full=None):
    _, half, cols = sums.shape
    tr = _row_tile(half)
    nb = half // tr

    def body(pos_ref, s_ref, g_ref, *rest):
        out_ref = rest[-1]
        out_ref[...] = ((s_ref[...].astype(F32) + g_ref[0].astype(F32)) + g_ref[1].astype(F32)) + g_ref[2].astype(F32)

    in_specs = [pl.BlockSpec((None, tr, cols), lambda i, p: (p[0], i, 0)),
                pl.BlockSpec((3, tr, cols), lambda i, p: (0, i, 0))]
    args = [pos, sums, got]
    if full is not None:
        in_specs.append(ANY)
        args.append(full)
    return pl.pallas_call(
        body, name=name,
        grid_spec=pltpu.PrefetchScalarGridSpec(
            num_scalar_prefetch=1, grid=(nb,), in_specs=in_specs,
            out_specs=pl.BlockSpec((None, tr, cols), lambda i, p: (layer, p[1] * nb + i, 0))),
        out_shape=jax.ShapeDtypeStruct(full_shape, F32),
        input_output_aliases={3: 0} if full is not None else {},
        compiler_params=_params("arbitrary"))(*args)


def _sibling_exchange(name, fulls):
    n = len(fulls)

    def body(*refs):
        g = refs[n:2 * n]
        send_sem, recv_sem = refs[2 * n:]
        x, y, c, _ = _position()

        def half(t, cc):
            return g[t].at[:, _half_rows(cc, fulls[t].shape[1] // 2), :]

        out = [pltpu.make_async_remote_copy(
            src_ref=half(t, c), dst_ref=half(t, c), send_sem=send_sem.at[t], recv_sem=recv_sem.at[t],
            device_id=(x, y, 1 - c), device_id_type=MESH) for t in range(n)]
        for cp in out:
            cp.start()
        for t in range(n):
            out[t].wait_send()
            pltpu.make_async_remote_copy(
                src_ref=half(t, 1 - c), dst_ref=half(t, 1 - c), send_sem=send_sem.at[t], recv_sem=recv_sem.at[t],
                device_id=(x, y, 1 - c), device_id_type=MESH).wait_recv()

    return pl.pallas_call(
        body, name=name, in_specs=[ANY] * n, out_specs=[ANY] * n,
        out_shape=[jax.ShapeDtypeStruct(f.shape, F32) for f in fulls],
        input_output_aliases={t: t for t in range(n)},
        scratch_shapes=[pltpu.SemaphoreType.DMA((n,)), pltpu.SemaphoreType.DMA((n,))],
    )(*fulls)


def _all_reduce_small(name, packed):
    n_dev = 8

    def body(p_ref, o_ref, slots, send_sem, recv_sem):
        x, y, c, _ = _position()
        me = 4 * x + 2 * y + c

        def peer(k):
            return (1 - x if k & 4 else x, 1 - y if k & 2 else y, 1 - c if k & 1 else c)

        def logical(k):
            px, py, pc = peer(k)
            return 4 * px + 2 * py + pc

        slots[me] = p_ref[...]
        sends = [pltpu.make_async_remote_copy(
            src_ref=p_ref, dst_ref=slots.at[me], send_sem=send_sem.at[k], recv_sem=recv_sem.at[k],
            device_id=peer(k), device_id_type=MESH) for k in range(1, n_dev)]
        for cp in sends:
            cp.start()
        for k in range(1, n_dev):
            pltpu.make_async_remote_copy(
                src_ref=p_ref, dst_ref=slots.at[logical(k)], send_sem=send_sem.at[k], recv_sem=recv_sem.at[k],
                device_id=peer(k), device_id_type=MESH).wait_recv()
        for cp in sends:
            cp.wait_send()
        total = slots[0]
        for d in range(1, n_dev):
            total = total + slots[d]
        o_ref[...] = total

    vm = pl.BlockSpec(memory_space=pltpu.VMEM)
    return pl.pallas_call(
        body, name=name, in_specs=[vm], out_specs=vm, out_shape=jax.ShapeDtypeStruct(packed.shape, F32),
        scratch_shapes=[pltpu.VMEM((n_dev,) + packed.shape, F32), pltpu.SemaphoreType.DMA((n_dev,)),
                        pltpu.SemaphoreType.DMA((n_dev,))],
    )(packed)


def _adamw(name, w, g, m, v):
    rows, cols = w.shape
    tr = _row_tile(rows)

    def body(w_ref, g_ref, m_ref, v_ref, d_ref, mo_ref, vo_ref):
        gv = g_ref[...]
        m_new = ADAM_B1 * m_ref[...] + (1.0 - ADAM_B1) * gv
        v_new = ADAM_B2 * v_ref[...] + (1.0 - ADAM_B2) * (gv * gv)
        m_hat = m_new / (1.0 - ADAM_B1 ** ADAM_STEP)
        v_hat = v_new / (1.0 - ADAM_B2 ** ADAM_STEP)
        d_ref[...] = -ADAM_LR * (m_hat / (jnp.sqrt(v_hat) + ADAM_EPS) + ADAM_WD * w_ref[...])
        mo_ref[...] = m_new
        vo_ref[...] = v_new

    spec = pl.BlockSpec((tr, cols), lambda i: (i, 0))
    shape = jax.ShapeDtypeStruct((rows, cols), F32)
    return pl.pallas_call(body, name=name, grid=(rows // tr,), in_specs=[spec] * 4, out_specs=[spec] * 3,
                          out_shape=[shape] * 3, compiler_params=_params("parallel"))(w, g, m, v)


def _pack_small(tree):
    flat = jnp.concatenate([tree[n].reshape(-1).astype(F32) for n in SMALL])
    return jnp.pad(flat, (0, SMALL_ROWS * LANES - flat.shape[0])).reshape(SMALL_ROWS, LANES)


def _unpack_small(packed, like):
    flat = packed.reshape(-1)
    out, off = {}, 0
    for n in SMALL:
        size = int(np.prod(like[n].shape))
        out[n] = flat[off:off + size].reshape(like[n].shape)
        off += size
    return out


def kernel(x, ev_w_in, ev_g_cq, ev_w_uq, ev_g_ckv, ev_w_ukv, ev_w_out, od_w_qkv, od_rel_bias, od_w_out, g_mix, g_ffn, w_gate, w_up, w_down, g_final, loss_target, m_ev_w_in, m_ev_g_cq, m_ev_w_uq, m_ev_g_ckv, m_ev_w_ukv, m_ev_w_out, m_od_w_qkv, m_od_rel_bias, m_od_w_out, m_g_mix, m_g_ffn, m_w_gate, m_w_up, m_w_down, m_g_final, v_ev_w_in, v_ev_g_cq, v_ev_w_uq, v_ev_g_ckv, v_ev_w_ukv, v_ev_w_out, v_od_w_qkv, v_od_rel_bias, v_od_w_out, v_g_mix, v_g_ffn, v_w_gate, v_w_up, v_w_down, v_g_final):
    w = dict(ev_w_in=ev_w_in, ev_g_cq=ev_g_cq, ev_w_uq=ev_w_uq, ev_g_ckv=ev_g_ckv, ev_w_ukv=ev_w_ukv, ev_w_out=ev_w_out,
             od_w_qkv=od_w_qkv, od_rel_bias=od_rel_bias, od_w_out=od_w_out, g_mix=g_mix, g_ffn=g_ffn, w_gate=w_gate,
             w_up=w_up, w_down=w_down, g_final=g_final)
    m = dict(ev_w_in=m_ev_w_in, ev_g_cq=m_ev_g_cq, ev_w_uq=m_ev_w_uq, ev_g_ckv=m_ev_g_ckv, ev_w_ukv=m_ev_w_ukv,
             ev_w_out=m_ev_w_out, od_w_qkv=m_od_w_qkv, od_rel_bias=m_od_rel_bias, od_w_out=m_od_w_out, g_mix=m_g_mix,
             g_ffn=m_g_ffn, w_gate=m_w_gate, w_up=m_w_up, w_down=m_w_down, g_final=m_g_final)
    v = dict(ev_w_in=v_ev_w_in, ev_g_cq=v_ev_g_cq, ev_w_uq=v_ev_w_uq, ev_g_ckv=v_ev_g_ckv, ev_w_ukv=v_ev_w_ukv,
             ev_w_out=v_ev_w_out, od_w_qkv=v_od_w_qkv, od_rel_bias=v_od_rel_bias, od_w_out=v_od_w_out, g_mix=v_g_mix,
             g_ffn=v_g_ffn, w_gate=v_w_gate, w_up=v_w_up, w_down=v_w_down, g_final=v_g_final)
    flat2d = lambda a: a.reshape(-1, a.shape[-1])

    pos = jnp.stack([2 * lax.axis_index("x") + lax.axis_index("y"), lax.axis_index("c")]).astype(jnp.int32)

    slots = [_cast_into_slot("cast_" + n, flat2d(w[n]), pos) for n in BIG]
    gw = dict(zip(BIG, _all_gather_weights("gather_weights", slots)))

    loss_local, grad_x, big, small = _local_step(x[0], loss_target[0], gw, {n: w[n] for n in SMALL})

    parts = [big[p] for p, _, _ in GRAD_PARTS]
    theirs = _pair_exchange("grads_pair", parts)
    sums = [_pair_sum("pair_sum_" + GRAD_PARTS[t][0], parts[t], theirs[t], pos) for t in range(len(parts))]
    arrived = _chip_exchange("grads_chips", sums)
    fulls = {}
    for t, (part_name, n, layer) in enumerate(GRAD_PARTS):
        fulls[n] = _chip_sum("chip_sum_" + part_name, sums[t], arrived[t], pos, layer, w[n].shape, fulls.get(n))
    grads = dict(zip(BIG, _sibling_exchange("grads_sibling", [fulls[n] for n in BIG])))
    small_sum = _all_reduce_small("small_sum", _pack_small(small))
    grads.update(_unpack_small(small_sum, w))

    delta, new_m, new_v = {}, {}, {}
    for n in BIG:
        d_, m_, v_ = _adamw("adamw_" + n, flat2d(w[n]), flat2d(grads[n]), flat2d(m[n]), flat2d(v[n]))
        delta[n], new_m[n], new_v[n] = d_.reshape(w[n].shape), m_.reshape(w[n].shape), v_.reshape(w[n].shape)
    d_, m_, v_ = _adamw("adamw_small", _pack_small(w), small_sum, _pack_small(m), _pack_small(v))
    delta.update(_unpack_small(d_, w))
    new_m.update(_unpack_small(m_, w))
    new_v.update(_unpack_small(v_, w))

    loss = lax.psum(loss_local[0, 0], ("x", "y", "c"))
    return (loss, grad_x[None], *[grads[n] for n in WEIGHTS], *[delta[n] for n in WEIGHTS],
            *[new_m[n] for n in WEIGHTS], *[new_v[n] for n in WEIGHTS])
```

```python
import functools

import jax
import jax.numpy as jnp
import numpy as np
from jax import lax
from jax.experimental import pallas as pl
from jax.experimental.pallas import tpu as pltpu

F32 = jnp.float32
BF16 = jnp.bfloat16

S = 2048
D = 1024
CHUNK = 64
MLA_H, MLA_NOPE, MLA_ROPE, MLA_V = 8, 64, 32, 64
Q_LORA, KV_LORA = 384, 256
ROPE_THETA = 10000.0
SB_H, SB_DIM = 8, 64
C_H, C_DIM = 16, 64
LEFT_CHUNKS = 8
REL_CLIP = 256
D_FF = 2816
EVEN_IN = 2208
RMS_EPS = 1e-6
ADAM_LR, ADAM_B1, ADAM_B2, ADAM_EPS, ADAM_WD, ADAM_STEP = 0.001, 0.9, 0.999, 1e-08, 0.01, 10

N_CHIPS = 4
FF_SHARD = D_FF // N_CHIPS
SCALE_A = (MLA_NOPE + MLA_ROPE) ** -0.5
SCALE_B = SB_DIM ** -0.5
SCALE_C = C_DIM ** -0.5
NEG = -1e30

LANES = 128
VMEM_LIMIT_BYTES = 56 * 1024 * 1024
TM = 512
QB = 256
BQ = 256

P_CQ, P_CKV, P_QB, P_KB, P_VB, P_KR = 0, 512, 768, 1280, 1792, 2304
P_IN = 2432
KR_LANE = 64
BAND_W = BQ + LEFT_CHUNKS * CHUNK
BAND_PAD = 512
TOEP_W = 1024


def _params(*sem):
    return pltpu.CompilerParams(dimension_semantics=sem, vmem_limit_bytes=VMEM_LIMIT_BYTES)


MESH = pl.DeviceIdType.MESH
ANY = pl.BlockSpec(memory_space=pl.ANY)


def _position():
    x, y, c = lax.axis_index("x"), lax.axis_index("y"), lax.axis_index("c")
    other_chips = [(1 - x, y), (x, 1 - y), (1 - x, 1 - y)]
    return x, y, c, other_chips


def _half_rows(c, half):
    return pl.ds(pl.multiple_of(c * half, 16), half)


def _remote(ref_src, ref_dst, send, recv, k, device):
    return pltpu.make_async_remote_copy(src_ref=ref_src, dst_ref=ref_dst, send_sem=send.at[k], recv_sem=recv.at[k],
                                        device_id=device, device_id_type=MESH)


class _Carry:
    def __init__(self):
        self.operands, self.aliased, self.fresh = [], [], []
        self.n_sems = 0
        self.starts, self.finishes, self.on_done = [], [], []

    def operand(self, arr, aliased):
        for i, a in enumerate(self.operands):
            if a is arr:
                return i
        self.operands.append(arr)
        self.aliased.append(aliased)
        return len(self.operands) - 1

    def result(self, shape, dtype):
        self.fresh.append(jax.ShapeDtypeStruct(shape, dtype))
        return len(self.fresh) - 1

    def sems(self, k):
        base = self.n_sems
        self.n_sems += k
        return base

    def done(self, results):
        aliased, fresh = results
        for f in self.on_done:
            f(aliased, fresh)


def _carrier_call(body, *, name, grid, in_specs, out_specs, out_shape, args, sem, scratch_shapes=(), carry=None):
    in_specs, out_specs, out_shape, scratch = list(in_specs), list(out_specs), list(out_shape), list(scratch_shapes)
    if carry is None:
        res = pl.pallas_call(body, name=name, grid=grid, in_specs=in_specs, out_specs=out_specs, out_shape=out_shape,
                             scratch_shapes=scratch, compiler_params=_params(*sem))(*args)
        return list(res), None
    ops = carry.operands
    alias_idx = [i for i, a in enumerate(carry.aliased) if a]
    c_shapes = [jax.ShapeDtypeStruct(ops[i].shape, ops[i].dtype) for i in alias_idx] + carry.fresh
    n_in, n_out, n_scr = len(args), len(out_shape), len(scratch)

    def wrapped(*refs):
        ins, c_ins = refs[:n_in], refs[n_in:n_in + len(ops)]
        o0 = n_in + len(ops)
        outs, c_outs = refs[o0:o0 + n_out], refs[o0 + n_out:o0 + n_out + len(c_shapes)]
        s0 = o0 + n_out + len(c_shapes)
        scr, send, recv = refs[s0:s0 + n_scr], refs[s0 + n_scr], refs[s0 + n_scr + 1]
        use = list(c_ins)
        for k, i in enumerate(alias_idx):
            use[i] = c_outs[k]
        fresh = c_outs[len(alias_idx):]

        def run(steps):
            for step in steps:
                step(use, fresh, send, recv)

        if not grid:
            run(carry.starts)
            if body is not None:
                body(*ins, *outs, *scr)
            run(carry.finishes)
            return
        ids = [pl.program_id(a) for a in range(len(grid))]
        first = functools.reduce(jnp.logical_and, [i == 0 for i in ids])
        last = functools.reduce(jnp.logical_and, [i == g - 1 for i, g in zip(ids, grid)])

        @pl.when(first)
        def _():
            run(carry.starts)

        body(*ins, *outs, *scr)

        @pl.when(last)
        def _():
            run(carry.finishes)

    res = pl.pallas_call(
        wrapped, name=name, grid=grid, in_specs=in_specs + [ANY] * len(ops), out_specs=out_specs + [ANY] * len(c_shapes),
        out_shape=out_shape + c_shapes,
        scratch_shapes=scratch + [pltpu.SemaphoreType.DMA((carry.n_sems,)), pltpu.SemaphoreType.DMA((carry.n_sems,))],
        input_output_aliases={n_in + i: n_out + k for k, i in enumerate(alias_idx)},
        compiler_params=_params(*(("arbitrary",) * len(grid))),
    )(*args, *ops)
    res = list(res)
    c_res = res[n_out:]
    return res[:n_out], ({i: c_res[k] for k, i in enumerate(alias_idx)}, c_res[len(alias_idx):])


_DIMS = {"nn": (((1,), (0,)), ((), ())), "nt": (((1,), (1,)), ((), ())), "tn": (((0,), (0,)), ((), ()))}


def _dot(a, b, kind="nn"):
    return lax.dot_general(a, b, _DIMS[kind], preferred_element_type=F32)


def _iota(shape, dim):
    return lax.broadcasted_iota(jnp.int32, shape, dim)


def _sigmoid(x):
    return 1.0 / (1.0 + jnp.exp(-x))


def _softplus(x):
    return jnp.maximum(x, 0.0) + jnp.log(1.0 + jnp.exp(-jnp.abs(x)))


def _split_dot(x, tri):
    hi = x.astype(BF16)
    lo = (x - hi.astype(F32)).astype(BF16)
    return _dot(hi, tri) + _dot(lo, tri)


def _mm(name, a, b, *, kind, grid, a_spec, b_spec, o_spec, out_shape, out_dtype, acc_shape, resid=None, r_spec=None):
    nk = grid[-1]
    has_r = resid is not None

    def body(*refs):
        a_ref, b_ref = refs[0], refs[1]
        r_ref = refs[2] if has_r else None
        o_ref = refs[2 + has_r]
        part = _dot(a_ref[...].astype(BF16), b_ref[...].astype(BF16), kind)

        def finish(total):
            if has_r:
                total = total + r_ref[...].astype(F32)
            o_ref[...] = total.astype(out_dtype)

        if nk == 1:
            finish(part)
        else:
            acc_ref = refs[3 + has_r]
            k = pl.program_id(len(grid) - 1)

            @pl.when(k == 0)
            def _():
                acc_ref[...] = part

            @pl.when(k > 0)
            def _():
                acc_ref[...] += part

            @pl.when(k == nk - 1)
            def _():
                finish(acc_ref[...])

    in_specs = [a_spec, b_spec] + ([r_spec] if has_r else [])
    args = (a, b) + ((resid,) if has_r else ())
    sem = ("parallel",) * (len(grid) - 1) + ("arbitrary",)
    return pl.pallas_call(
        body, name=name, grid=grid, in_specs=in_specs, out_specs=o_spec,
        out_shape=jax.ShapeDtypeStruct(out_shape, out_dtype),
        scratch_shapes=[pltpu.VMEM(acc_shape, F32)] if nk > 1 else [],
        compiler_params=_params(*sem),
    )(*args)


def _rms_fwd(name, x, g, col_block=0):
    c = g.shape[1]

    def body(x_ref, g_ref, u_ref):
        xv = x_ref[...]
        r = lax.rsqrt(jnp.mean(xv * xv, axis=-1, keepdims=True) + RMS_EPS)
        u_ref[...] = (xv * r * g_ref[...]).astype(BF16)

    return pl.pallas_call(
        body, name=name, grid=(S // TM,),
        in_specs=[pl.BlockSpec((TM, c), lambda i: (i, col_block)), pl.BlockSpec((1, c), lambda i: (0, 0))],
        out_specs=pl.BlockSpec((TM, c), lambda i: (i, 0)),
        out_shape=jax.ShapeDtypeStruct((S, c), BF16),
        compiler_params=_params("parallel"),
    )(x, g)


def _rms_bwd(name, dy, x, g, resid):
    def body(dy_ref, x_ref, g_ref, r_ref, dx_ref, dg_ref):
        i = pl.program_id(0)
        xv = x_ref[...]
        r = lax.rsqrt(jnp.mean(xv * xv, axis=-1, keepdims=True) + RMS_EPS)
        xh = xv * r
        dyv = dy_ref[...]
        dxh = dyv * g_ref[...]
        dx_ref[...] = r_ref[...] + r * (dxh - xh * jnp.mean(dxh * xh, axis=-1, keepdims=True))
        part = jnp.sum(dyv * xh, axis=0, keepdims=True)

        @pl.when(i == 0)
        def _():
            dg_ref[...] = part

        @pl.when(i > 0)
        def _():
            dg_ref[...] += part

    row = pl.BlockSpec((TM, D), lambda i: (i, 0))
    vec = pl.BlockSpec((1, D), lambda i: (0, 0))
    return pl.pallas_call(
        body, name=name, grid=(S // TM,), in_specs=[row, row, vec, row], out_specs=[row, vec],
        out_shape=[jax.ShapeDtypeStruct((S, D), F32), jax.ShapeDtypeStruct((1, D), F32)],
        compiler_params=_params("arbitrary"),
    )(dy, x, g, resid)


def _loss_bwd(name, h, g, tgt):
    def body(h_ref, g_ref, t_ref, loss_ref, dh_ref, dg_ref):
        i = pl.program_id(0)
        xv = h_ref[...]
        gv = g_ref[...]
        r = lax.rsqrt(jnp.mean(xv * xv, axis=-1, keepdims=True) + RMS_EPS)
        xh = xv * r
        diff = xh * gv - t_ref[...]
        part_loss = 0.5 * jnp.sum(jnp.sum(diff * diff, axis=-1, keepdims=True) * (1.0 / D), axis=0, keepdims=True)
        dy = diff * (1.0 / D)
        dxh = dy * gv
        dh_ref[...] = r * (dxh - xh * jnp.mean(dxh * xh, axis=-1, keepdims=True))
        part_g = jnp.sum(dy * xh, axis=0, keepdims=True)

        @pl.when(i == 0)
        def _():
            dg_ref[...] = part_g
            loss_ref[...] = jnp.broadcast_to(part_loss, (1, LANES))

        @pl.when(i > 0)
        def _():
            dg_ref[...] += part_g
            loss_ref[...] += jnp.broadcast_to(part_loss, (1, LANES))

    row = pl.BlockSpec((TM, D), lambda i: (i, 0))
    vec = pl.BlockSpec((1, D), lambda i: (0, 0))
    return pl.pallas_call(
        body, name=name, grid=(S // TM,), in_specs=[row, vec, row],
        out_specs=[pl.BlockSpec((1, LANES), lambda i: (0, 0)), row, vec],
        out_shape=[jax.ShapeDtypeStruct((1, LANES), F32), jax.ShapeDtypeStruct((S, D), F32),
                   jax.ShapeDtypeStruct((1, D), F32)],
        compiler_params=_params("arbitrary"),
    )(h, g, tgt)


def _ffn_fwd(name, h, g, wg, wu, wd, carry=None):
    def body(h_ref, g_ref, wg_ref, wu_ref, wd_ref, o_ref, gate_ref, up_ref, u_scr):
        s = pl.program_id(1)

        @pl.when(s == 0)
        def _():
            xv = h_ref[...]
            r = lax.rsqrt(jnp.mean(xv * xv, axis=-1, keepdims=True) + RMS_EPS)
            u_scr[...] = (xv * r * g_ref[...]).astype(BF16)
            o_ref[...] = xv

        u = u_scr[...]
        gate = _dot(u, wg_ref[...])
        up = _dot(u, wu_ref[...])
        act = gate * _sigmoid(gate) * up
        o_ref[...] += _dot(act.astype(BF16), wd_ref[...])
        gate_ref[...] = gate.astype(BF16)
        up_ref[...] = up.astype(BF16)

    row = pl.BlockSpec((TM, D), lambda i, s: (i, 0))
    hid = pl.BlockSpec((None, TM, FF_SHARD), lambda i, s: (s, i, 0))
    return _carrier_call(
        body, name=name, grid=(S // TM, N_CHIPS),
        in_specs=[row, pl.BlockSpec((1, D), lambda i, s: (0, 0)),
                  pl.BlockSpec((None, D, FF_SHARD), lambda i, s: (s, 0, 0)),
                  pl.BlockSpec((None, D, FF_SHARD), lambda i, s: (s, 0, 0)),
                  pl.BlockSpec((None, FF_SHARD, D), lambda i, s: (s, 0, 0))],
        out_specs=[row, hid, hid],
        out_shape=[jax.ShapeDtypeStruct((S, D), F32), jax.ShapeDtypeStruct((N_CHIPS, S, FF_SHARD), BF16),
                   jax.ShapeDtypeStruct((N_CHIPS, S, FF_SHARD), BF16)],
        scratch_shapes=[pltpu.VMEM((TM, D), BF16)], args=(h, g, wg, wu, wd), sem=("parallel", "arbitrary"), carry=carry)


def _ffn_bwd(name, dh, h, g, gate, up, wg, wu, wd):
    def body(dh_ref, h_ref, g_ref, gate_ref, up_ref, wg_ref, wu_ref, wd_ref,
             dhin_ref, dg_ref, u_ref, dgate_ref, dup_ref, act_ref, dhb_scr, du_scr):
        i = pl.program_id(0)
        s = pl.program_id(1)

        @pl.when(s == 0)
        def _():
            xv = h_ref[...]
            r = lax.rsqrt(jnp.mean(xv * xv, axis=-1, keepdims=True) + RMS_EPS)
            u_ref[...] = (xv * r * g_ref[...]).astype(BF16)
            dhb_scr[...] = dh_ref[...].astype(BF16)
            du_scr[...] = jnp.zeros_like(du_scr)

        dact = _dot(dhb_scr[...], wd_ref[...], "nt")
        gv = gate_ref[...].astype(F32)
        uv = up_ref[...].astype(F32)
        sig = _sigmoid(gv)
        sil = gv * sig
        dup = dact * sil
        dgate = dact * uv * (sig * (1.0 + gv * (1.0 - sig)))
        dgb = dgate.astype(BF16)
        dub = dup.astype(BF16)
        act_ref[...] = (sil * uv).astype(BF16)
        dgate_ref[...] = dgb
        dup_ref[...] = dub
        du_scr[...] += _dot(dgb, wg_ref[...], "nt") + _dot(dub, wu_ref[...], "nt")

        @pl.when(s == N_CHIPS - 1)
        def _():
            xv = h_ref[...]
            r = lax.rsqrt(jnp.mean(xv * xv, axis=-1, keepdims=True) + RMS_EPS)
            xh = xv * r
            du = du_scr[...]
            dxh = du * g_ref[...]
            dhin_ref[...] = dh_ref[...] + r * (dxh - xh * jnp.mean(dxh * xh, axis=-1, keepdims=True))
            part = jnp.sum(du * xh, axis=0, keepdims=True)

            @pl.when(i == 0)
            def _():
                dg_ref[...] = part

            @pl.when(i > 0)
            def _():
                dg_ref[...] += part

    row = pl.BlockSpec((TM, D), lambda i, s: (i, 0))
    vec = pl.BlockSpec((1, D), lambda i, s: (0, 0))
    hid = pl.BlockSpec((None, TM, FF_SHARD), lambda i, s: (s, i, 0))
    hid_shape = jax.ShapeDtypeStruct((N_CHIPS, S, FF_SHARD), BF16)
    return pl.pallas_call(
        body, name=name, grid=(S // TM, N_CHIPS),
        in_specs=[row, row, vec, hid, hid,
                  pl.BlockSpec((None, D, FF_SHARD), lambda i, s: (s, 0, 0)),
                  pl.BlockSpec((None, D, FF_SHARD), lambda i, s: (s, 0, 0)),
                  pl.BlockSpec((None, FF_SHARD, D), lambda i, s: (s, 0, 0))],
        out_specs=[row, vec, row, hid, hid, hid],
        out_shape=[jax.ShapeDtypeStruct((S, D), F32), jax.ShapeDtypeStruct((1, D), F32),
                   jax.ShapeDtypeStruct((S, D), BF16), hid_shape, hid_shape, hid_shape],
        scratch_shapes=[pltpu.VMEM((TM, D), BF16), pltpu.VMEM((TM, D), F32)],
        compiler_params=_params("arbitrary", "arbitrary"),
    )(dh, h, g, gate, up, wg, wu, wd)


def _ffn_wgrads(name, u, dgate, dup, act, dh):
    nk = S // TM
    tok = lambda s, k: (k, 0)
    hid = pl.BlockSpec((None, TM, FF_SHARD), lambda s, k: (s, k, 0))
    common = dict(kind="tn", grid=(N_CHIPS, nk), out_dtype=BF16)
    d_wg = _mm(name + "_g", u, dgate, a_spec=pl.BlockSpec((TM, D), tok), b_spec=hid,
               o_spec=pl.BlockSpec((None, D, FF_SHARD), lambda s, k: (s, 0, 0)),
               out_shape=(N_CHIPS, D, FF_SHARD), acc_shape=(D, FF_SHARD), **common)
    d_wu = _mm(name + "_u", u, dup, a_spec=pl.BlockSpec((TM, D), tok), b_spec=hid,
               o_spec=pl.BlockSpec((None, D, FF_SHARD), lambda s, k: (s, 0, 0)),
               out_shape=(N_CHIPS, D, FF_SHARD), acc_shape=(D, FF_SHARD), **common)
    d_wd = _mm(name + "_d", act, dh, a_spec=hid, b_spec=pl.BlockSpec((TM, D), tok),
               o_spec=pl.BlockSpec((None, FF_SHARD, D), lambda s, k: (s, 0, 0)),
               out_shape=(N_CHIPS, FF_SHARD, D), acc_shape=(FF_SHARD, D), **common)
    return d_wg, d_wu, d_wd


def _rope_tables():
    pos = jnp.arange(S, dtype=F32)
    inv = ROPE_THETA ** (-jnp.arange(0, MLA_ROPE, 2, dtype=F32) / MLA_ROPE)
    ang = pos[:, None] * inv[None, :]
    half = MLA_ROPE // 2
    cos = jnp.cos(ang)
    sin = jnp.sin(ang)
    one = jnp.ones((S, KR_LANE), F32)
    zero = jnp.zeros((S, KR_LANE), F32)
    tail_one = jnp.ones((S, LANES - KR_LANE - MLA_ROPE), F32)
    tail_zero = jnp.zeros((S, LANES - KR_LANE - MLA_ROPE), F32)
    cos_t = jnp.concatenate([one, cos, cos, tail_one], axis=1)
    sin_t = jnp.concatenate([zero, -sin, sin, tail_zero], axis=1)
    assert cos_t.shape == (S, LANES) and half * 2 == MLA_ROPE
    return cos_t, sin_t


def _rope(x, cos_t, sin_t, sign):
    n = x.shape[1] // LANES
    half = MLA_ROPE // 2
    lane = _iota(x.shape, 1) & (LANES - 1)
    first = (lane >= KR_LANE) & (lane < KR_LANE + half)
    swapped = jnp.where(first, pltpu.roll(x, x.shape[1] - half, 1), pltpu.roll(x, half, 1))
    c = jnp.tile(cos_t, (1, n)) if n > 1 else cos_t
    s = jnp.tile(sin_t, (1, n)) if n > 1 else sin_t
    return x * c + swapped * (s * sign)


def _mla_prep_fwd(name, proj, g_cq, g_ckv, w_uq, w_uk, w_uv, cos_t, sin_t):
    nh = MLA_H * LANES

    def body(cq_ref, ckv_ref, kr_ref, gq_ref, gkv_ref, wq_ref, wk_ref, wv_ref, cos_ref, sin_ref,
             qa_ref, ka_ref, va_ref):
        cos_v, sin_v = cos_ref[...], sin_ref[...]
        cq = cq_ref[...]
        r = lax.rsqrt(jnp.mean(cq * cq, axis=-1, keepdims=True) + RMS_EPS)
        cqn = (cq * r * gq_ref[...]).astype(BF16)
        qa_ref[...] = _rope(_dot(cqn, wq_ref[...]), cos_v, sin_v, 1.0).astype(BF16)
        ckv = ckv_ref[...]
        r = lax.rsqrt(jnp.mean(ckv * ckv, axis=-1, keepdims=True) + RMS_EPS)
        ckvn = (ckv * r * gkv_ref[...]).astype(BF16)
        lane = _iota((TM, LANES), 1)
        rot = (lane >= KR_LANE) & (lane < KR_LANE + MLA_ROPE)
        kr = jnp.where(rot, _rope(kr_ref[...], cos_v, sin_v, 1.0), 0.0)
        ka_ref[...] = (_dot(ckvn, wk_ref[...]) + jnp.tile(kr, (1, MLA_H))).astype(BF16)
        va_ref[...] = _dot(ckvn, wv_ref[...]).astype(BF16)

    full = lambda shape: pl.BlockSpec(shape, lambda i: (0, 0))
    return pl.pallas_call(
        body, name=name, grid=(S // TM,),
        in_specs=[pl.BlockSpec((TM, Q_LORA), lambda i: (i, P_CQ // Q_LORA)),
                  pl.BlockSpec((TM, KV_LORA), lambda i: (i, P_CKV // KV_LORA)),
                  pl.BlockSpec((TM, LANES), lambda i: (i, P_KR // LANES)),
                  full((1, Q_LORA)), full((1, KV_LORA)), full((Q_LORA, nh)), full((KV_LORA, nh)),
                  full((KV_LORA, MLA_H * MLA_V)),
                  pl.BlockSpec((TM, LANES), lambda i: (i, 0)), pl.BlockSpec((TM, LANES), lambda i: (i, 0))],
        out_specs=[pl.BlockSpec((TM, nh), lambda i: (i, 0)), pl.BlockSpec((TM, nh), lambda i: (i, 0)),
                   pl.BlockSpec((TM, MLA_H * MLA_V), lambda i: (i, 0))],
        out_shape=[jax.ShapeDtypeStruct((S, nh), BF16), jax.ShapeDtypeStruct((S, nh), BF16),
                   jax.ShapeDtypeStruct((S, MLA_H * MLA_V), BF16)],
        compiler_params=_params("parallel"),
    )(proj, proj, proj, g_cq, g_ckv, w_uq, w_uk, w_uv, cos_t, sin_t)


def _mla_prep_bwd(name, dqa, dka, dva, proj, g_cq, g_ckv, w_uq, w_uk, w_uv, cos_t, sin_t):
    nh = MLA_H * LANES

    def body(dqa_ref, dka_ref, dva_ref, cq_ref, ckv_ref, gq_ref, gkv_ref, wq_ref, wk_ref, wv_ref, cos_ref, sin_ref,
             dcq_ref, dckv_ref, dkr_ref, dwq_ref, dwk_ref, dwv_ref, dgq_ref, dgkv_ref):
        i = pl.program_id(0)
        cos_v, sin_v = cos_ref[...], sin_ref[...]

        def norm_bwd(x, g, dn):
            r = lax.rsqrt(jnp.mean(x * x, axis=-1, keepdims=True) + RMS_EPS)
            xh = x * r
            dxh = dn * g
            dx = r * (dxh - xh * jnp.mean(dxh * xh, axis=-1, keepdims=True))
            return dx, jnp.sum(dn * xh, axis=0, keepdims=True), (xh * g).astype(BF16)

        dq = _rope(dqa_ref[...], cos_v, sin_v, -1.0).astype(BF16)
        dcqn = _dot(dq, wq_ref[...], "nt")
        dcq, dgq, cqn = norm_bwd(cq_ref[...], gq_ref[...], dcqn)
        dcq_ref[...] = dcq.astype(BF16)
        dwq = _dot(cqn, dq, "tn")

        dka = dka_ref[...]
        dkab = dka.astype(BF16)
        dvab = dva_ref[...].astype(BF16)
        dckvn = _dot(dkab, wk_ref[...], "nt") + _dot(dvab, wv_ref[...], "nt")
        dckv, dgkv, ckvn = norm_bwd(ckv_ref[...], gkv_ref[...], dckvn)
        dckv_ref[...] = dckv.astype(BF16)
        dwk = _dot(ckvn, dkab, "tn")
        dwv = _dot(ckvn, dvab, "tn")

        fold = dka[:, 0:LANES]
        for hh in range(1, MLA_H):
            fold = fold + dka[:, hh * LANES:(hh + 1) * LANES]
        lane = _iota((TM, LANES), 1)
        rot = (lane >= KR_LANE) & (lane < KR_LANE + MLA_ROPE)
        dkr = _rope(jnp.where(rot, fold, 0.0), cos_v, sin_v, -1.0)
        dkr_ref[...] = jnp.where(rot, dkr, 0.0).astype(BF16)

        @pl.when(i == 0)
        def _():
            dwq_ref[...] = dwq
            dwk_ref[...] = dwk
            dwv_ref[...] = dwv
            dgq_ref[...] = dgq
            dgkv_ref[...] = dgkv

        @pl.when(i > 0)
        def _():
            dwq_ref[...] += dwq
            dwk_ref[...] += dwk
            dwv_ref[...] += dwv
            dgq_ref[...] += dgq
            dgkv_ref[...] += dgkv

    full = lambda shape: pl.BlockSpec(shape, lambda i: (0, 0))
    rows = lambda c: pl.BlockSpec((TM, c), lambda i: (i, 0))
    nv = MLA_H * MLA_V
    return pl.pallas_call(
        body, name=name, grid=(S // TM,),
        in_specs=[rows(nh), rows(nh), rows(nv),
                  pl.BlockSpec((TM, Q_LORA), lambda i: (i, P_CQ // Q_LORA)),
                  pl.BlockSpec((TM, KV_LORA), lambda i: (i, P_CKV // KV_LORA)),
                  full((1, Q_LORA)), full((1, KV_LORA)), full((Q_LORA, nh)), full((KV_LORA, nh)), full((KV_LORA, nv)),
                  rows(LANES), rows(LANES)],
        out_specs=[rows(Q_LORA), rows(KV_LORA), rows(LANES), full((Q_LORA, nh)), full((KV_LORA, nh)),
                   full((KV_LORA, nv)), full((1, Q_LORA)), full((1, KV_LORA))],
        out_shape=[jax.ShapeDtypeStruct((S, Q_LORA), BF16), jax.ShapeDtypeStruct((S, KV_LORA), BF16),
                   jax.ShapeDtypeStruct((S, LANES), BF16), jax.ShapeDtypeStruct((Q_LORA, nh), F32),
                   jax.ShapeDtypeStruct((KV_LORA, nh), F32), jax.ShapeDtypeStruct((KV_LORA, nv), F32),
                   jax.ShapeDtypeStruct((1, Q_LORA), F32), jax.ShapeDtypeStruct((1, KV_LORA), F32)],
        compiler_params=_params("arbitrary"),
    )(dqa, dka, dva, proj, proj, g_cq, g_ckv, w_uq, w_uk, w_uv, cos_t, sin_t)


def _head_masks(dtype):
    lane = _iota((1, LANES), 1)
    return (lane < 64).astype(dtype), (lane >= 64).astype(dtype)


def _mla_fwd(name, qa, ka, va, carry=None):
    def body(q_ref, k_ref, v_ref, o_ref, lse_ref):
        m0b, m1b = _head_masks(BF16)
        lane = _iota((QB, LANES), 1)
        left = lane < 64

        def qblock(i, _):
            r0 = pl.multiple_of(i * QB, QB)
            qs = [q_ref[pl.ds(r0, QB), hh * LANES:(hh + 1) * LANES] for hh in range(2)]
            rowc = lax.shift_right_logical(r0 + _iota((QB, QB), 0), 6)

            def kv(kb, carry):
                ms, ls, acc = carry
                c0 = pl.multiple_of(kb * QB, QB)
                v = v_ref[pl.ds(c0, QB), :]
                ok = lax.shift_right_logical(c0 + _iota((QB, QB), 1), 6) <= rowc
                new_m, new_l, alphas = [], [], []
                pv = None
                for hh in range(2):
                    k = k_ref[pl.ds(c0, QB), hh * LANES:(hh + 1) * LANES]
                    s = jnp.where(ok, _dot(qs[hh], k, "nt") * SCALE_A, NEG)
                    mn = jnp.maximum(ms[hh], jnp.max(s, axis=-1, keepdims=True))
                    p = jnp.exp(s - mn)
                    a = jnp.exp(ms[hh] - mn)
                    new_m.append(mn)
                    new_l.append(a * ls[hh] + jnp.sum(p, axis=-1, keepdims=True))
                    alphas.append(a)
                    part = _dot(p.astype(BF16), v * (m0b if hh == 0 else m1b))
                    pv = part if pv is None else pv + part
                acc = acc * jnp.where(left, alphas[0], alphas[1]) + pv
                return tuple(new_m), tuple(new_l), acc

            init = ((jnp.full((QB, 1), NEG, F32),) * 2, (jnp.zeros((QB, 1), F32),) * 2, jnp.zeros((QB, LANES), F32))
            ms, ls, acc = lax.fori_loop(0, i + 1, kv, init)
            o_ref[pl.ds(r0, QB), :] = acc * jnp.where(left, 1.0 / ls[0], 1.0 / ls[1])
            lse_ref[pl.ds(r0, QB), :] = jnp.where(left, ms[0] + jnp.log(ls[0]), ms[1] + jnp.log(ls[1]))
            return 0

        lax.fori_loop(0, S // QB, qblock, 0)

    pair = lambda w: pl.BlockSpec((S, w), lambda p: (0, p))
    return _carrier_call(
        body, name=name, grid=(MLA_H // 2,), in_specs=[pair(2 * LANES), pair(2 * LANES), pair(LANES)],
        out_specs=[pair(LANES), pair(LANES)],
        out_shape=[jax.ShapeDtypeStruct((S, MLA_H * MLA_V), F32), jax.ShapeDtypeStruct((S, MLA_H * MLA_V), F32)],
        args=(qa, ka, va), sem=("parallel",), carry=carry)


def _mla_bwd(name, qa, ka, va, o, lse, do, do_block0, carry=None):
    def body(q_ref, k_ref, v_ref, o_ref, lse_ref, do_ref, dq_ref, dk_ref, dv_ref):
        m0f, m1f = _head_masks(F32)
        m0b, m1b = _head_masks(BF16)
        dk_ref[...] = jnp.zeros_like(dk_ref)
        dv_ref[...] = jnp.zeros_like(dv_ref)

        def qblock(i, _):
            r0 = pl.multiple_of(i * QB, QB)
            rows = pl.ds(r0, QB)
            do_f = do_ref[rows, :]
            prod = do_f * o_ref[rows, :]
            deltas = [jnp.sum(prod * m0f, axis=-1, keepdims=True), jnp.sum(prod * m1f, axis=-1, keepdims=True)]
            lse_v = lse_ref[rows, :]
            lses = [lse_v[:, 0:1], lse_v[:, 64:65]]
            dob = do_f.astype(BF16)
            dos = [dob * m0b, dob * m1b]
            qs = [q_ref[rows, hh * LANES:(hh + 1) * LANES] for hh in range(2)]
            rowc = lax.shift_right_logical(r0 + _iota((QB, QB), 0), 6)

            def kv(kb, dqs):
                c0 = pl.multiple_of(kb * QB, QB)
                cols = pl.ds(c0, QB)
                v = v_ref[cols, :]
                ok = lax.shift_right_logical(c0 + _iota((QB, QB), 1), 6) <= rowc
                out = []
                dv = None
                for hh in range(2):
                    k = k_ref[cols, hh * LANES:(hh + 1) * LANES]
                    s = _dot(qs[hh], k, "nt") * SCALE_A
                    p = jnp.where(ok, jnp.exp(s - lses[hh]), 0.0)
                    dp = _dot(dos[hh], v, "nt")
                    ds = (p * (dp - deltas[hh]) * SCALE_A).astype(BF16)
                    out.append(dqs[hh] + _dot(ds, k))
                    dk_ref[cols, hh * LANES:(hh + 1) * LANES] += _dot(ds, qs[hh], "tn")
                    part = _dot(p.astype(BF16), dos[hh], "tn")
                    dv = part if dv is None else dv + part
                dv_ref[cols, :] += dv
                return tuple(out)

            dqs = lax.fori_loop(0, i + 1, kv, (jnp.zeros((QB, LANES), F32),) * 2)
            for hh in range(2):
                dq_ref[rows, hh * LANES:(hh + 1) * LANES] = dqs[hh]
            return 0

        lax.fori_loop(0, S // QB, qblock, 0)

    pair = lambda w: pl.BlockSpec((S, w), lambda p: (0, p))
    return _carrier_call(
        body, name=name, grid=(MLA_H // 2,),
        in_specs=[pair(2 * LANES), pair(2 * LANES), pair(LANES), pair(LANES), pair(LANES),
                  pl.BlockSpec((S, LANES), lambda p: (0, do_block0 + p))],
        out_specs=[pair(2 * LANES), pair(2 * LANES), pair(LANES)],
        out_shape=[jax.ShapeDtypeStruct((S, MLA_H * LANES), F32), jax.ShapeDtypeStruct((S, MLA_H * LANES), F32),
                   jax.ShapeDtypeStruct((S, MLA_H * MLA_V), F32)],
        args=(qa, ka, va, o, lse, do), sem=("parallel",), carry=carry)


def _sb_weights(q_h, k, c, before, tri_suffix):
    z = _dot(q_h, k, "nt") * SCALE_B
    sp = _softplus(z)
    log_keep = jnp.where(before, -sp, 0.0)
    log_between = _split_dot(log_keep, tri_suffix) + c
    w = jnp.where(before, jnp.exp(z - sp + log_between), 0.0)
    return w, jnp.exp(z - sp), jnp.sum(log_keep, axis=-1, keepdims=True)


def _sb_fwd(name, proj, carry=None):
    def body(q_ref, k_ref, v_ref, o_ref):
        m0b, m1b = _head_masks(BF16)
        tri_suffix = (_iota((QB, QB), 0) > _iota((QB, QB), 1)).astype(BF16)

        def qblock(i, _):
            r0 = pl.multiple_of(i * QB, QB)
            q = q_ref[pl.ds(r0, QB), :].astype(BF16)
            qs = [q * m0b, q * m1b]
            rowg = r0 + _iota((QB, QB), 0)

            def kv(step, carry):
                cs, acc = carry
                c0 = pl.multiple_of((i - step) * QB, QB)
                k = k_ref[pl.ds(c0, QB), :].astype(BF16)
                v = v_ref[pl.ds(c0, QB), :].astype(BF16)
                before = (c0 + _iota((QB, QB), 1)) < rowg
                new_c = []
                for hh in range(2):
                    w, _, tot = _sb_weights(qs[hh], k, cs[hh], before, tri_suffix)
                    new_c.append(cs[hh] + tot)
                    acc = acc + _dot(w.astype(BF16), v * (m0b if hh == 0 else m1b))
                return tuple(new_c), acc

            init = ((jnp.zeros((QB, 1), F32),) * 2, jnp.zeros((QB, LANES), F32))
            _, acc = lax.fori_loop(0, i + 1, kv, init)
            o_ref[pl.ds(r0, QB), :] = acc.astype(BF16)
            return 0

        lax.fori_loop(0, S // QB, qblock, 0)

    col = lambda base: pl.BlockSpec((S, LANES), lambda p: (0, base // LANES + p))
    return _carrier_call(
        body, name=name, grid=(SB_H // 2,), in_specs=[col(P_QB), col(P_KB), col(P_VB)],
        out_specs=[pl.BlockSpec((S, LANES), lambda p: (0, p))],
        out_shape=[jax.ShapeDtypeStruct((S, SB_H * SB_DIM), BF16)],
        args=(proj, proj, proj), sem=("parallel",), carry=carry)


def _sb_bwd(name, proj, do, do_block0, carry=None):
    nb = S // QB

    def body(q_ref, k_ref, v_ref, do_ref, dq_ref, dk_ref, dv_ref, sig_scr, dl_scr, dk_acc, dv_acc):
        m0b, m1b = _head_masks(BF16)
        tri_suffix = (_iota((QB, QB), 0) > _iota((QB, QB), 1)).astype(BF16)
        tri_prefix = (_iota((QB, QB), 0) < _iota((QB, QB), 1)).astype(BF16)
        dk_acc[...] = jnp.zeros_like(dk_acc)
        dv_acc[...] = jnp.zeros_like(dv_acc)

        def qblock(i, _):
            r0 = pl.multiple_of(i * QB, QB)
            rows = pl.ds(r0, QB)
            q = q_ref[rows, :].astype(BF16)
            qs = [q * m0b, q * m1b]
            dob = do_ref[rows, :].astype(BF16)
            dos = [dob * m0b, dob * m1b]
            rowg = r0 + _iota((QB, QB), 0)

            def sweep_left(step, cs):
                kb = i - step
                c0 = pl.multiple_of(kb * QB, QB)
                cols = pl.ds(c0, QB)
                k = k_ref[cols, :].astype(BF16)
                v = v_ref[cols, :].astype(BF16)
                before = (c0 + _iota((QB, QB), 1)) < rowg
                new_c = []
                dv = None
                for hh in range(2):
                    w, sig, tot = _sb_weights(qs[hh], k, cs[hh], before, tri_suffix)
                    new_c.append(cs[hh] + tot)
                    sig_scr[hh, kb] = sig
                    dl_scr[hh, kb] = _dot(dos[hh], v, "nt") * w
                    part = _dot(w.astype(BF16), dos[hh], "tn")
                    dv = part if dv is None else dv + part
                dv_acc[cols, :] += dv
                return tuple(new_c)

            lax.fori_loop(0, i + 1, sweep_left, (jnp.zeros((QB, 1), F32),) * 2)

            def sweep_right(kb, carry):
                ps, dq = carry
                c0 = pl.multiple_of(kb * QB, QB)
                cols = pl.ds(c0, QB)
                k = k_ref[cols, :].astype(BF16)
                before = (c0 + _iota((QB, QB), 1)) < rowg
                new_p = []
                dk = None
                for hh in range(2):
                    dl = dl_scr[hh, kb]
                    sig = sig_scr[hh, kb]
                    earlier = _split_dot(dl, tri_prefix) + ps[hh]
                    new_p.append(ps[hh] + jnp.sum(dl, axis=-1, keepdims=True))
                    dz = (jnp.where(before, dl * (1.0 - sig) - earlier * sig, 0.0) * SCALE_B).astype(BF16)
                    dq = dq + _dot(dz, k * (m0b if hh == 0 else m1b))
                    part = _dot(dz, qs[hh], "tn")
                    dk = part if dk is None else dk + part
                dk_acc[cols, :] += dk
                return tuple(new_p), dq

            init = ((jnp.zeros((QB, 1), F32),) * 2, jnp.zeros((QB, LANES), F32))
            _, dq = lax.fori_loop(0, i + 1, sweep_right, init)
            dq_ref[rows, :] = dq.astype(BF16)
            return 0

        lax.fori_loop(0, nb, qblock, 0)
        dk_ref[...] = dk_acc[...].astype(BF16)
        dv_ref[...] = dv_acc[...].astype(BF16)

    col = lambda base: pl.BlockSpec((S, LANES), lambda p: (0, base // LANES + p))
    out = pl.BlockSpec((S, LANES), lambda p: (0, p))
    shape = jax.ShapeDtypeStruct((S, SB_H * SB_DIM), BF16)
    return _carrier_call(
        body, name=name, grid=(SB_H // 2,),
        in_specs=[col(P_QB), col(P_KB), col(P_VB), pl.BlockSpec((S, LANES), lambda p: (0, do_block0 + p))],
        out_specs=[out, out, out], out_shape=[shape, shape, shape],
        scratch_shapes=[pltpu.VMEM((2, nb, QB, QB), F32), pltpu.VMEM((2, nb, QB, QB), F32),
                        pltpu.VMEM((S, LANES), F32), pltpu.VMEM((S, LANES), F32)],
        args=(proj, proj, proj, do), sem=("parallel",), carry=carry)


def _band_row_index():
    j = np.arange(TOEP_W)
    rel = np.clip(LEFT_CHUNKS * CHUNK - j, -REL_CLIP, REL_CLIP) + REL_CLIP
    rel[BAND_W:] = 2 * REL_CLIP
    return rel.astype(np.int32)


def _band_tiles(r0_ref, q_ref, kpad, vpad, m, m0b, m1b, static_ok, bias):
    r0 = pl.multiple_of(m * BQ, BQ)
    q = q_ref[0, pl.ds(r0, BQ), :]
    kw = kpad[pl.ds(r0, BAND_W), :]
    vw = vpad[pl.ds(r0, BAND_W), :]
    ok = static_ok & ((r0 - BAND_PAD + _iota((BQ, BAND_W), 1)) >= 0)
    qs = [q * m0b, q * m1b]
    ps = []
    for hh in range(2):
        s = jnp.where(ok, _dot(qs[hh], kw, "nt") * SCALE_C + bias[hh], NEG)
        e = jnp.exp(s - jnp.max(s, axis=-1, keepdims=True))
        ps.append(e * (1.0 / jnp.sum(e, axis=-1, keepdims=True)))
    return r0, qs, kw, vw, ps


def _band_setup(qkv_ref, r0_ref, kpad, vpad):
    kpad[0:BAND_PAD, :] = jnp.zeros((BAND_PAD, LANES), BF16)
    vpad[0:BAND_PAD, :] = jnp.zeros((BAND_PAD, LANES), BF16)
    kpad[BAND_PAD:, :] = qkv_ref[1]
    vpad[BAND_PAD:, :] = qkv_ref[2]
    jc = lax.shift_right_logical(_iota((BQ, BAND_W), 1), 6)
    rc = lax.shift_right_logical(_iota((BQ, BAND_W), 0), 6)
    static_ok = (jc >= rc) & (jc <= rc + LEFT_CHUNKS)
    bias = []
    for hh in range(2):
        row = jnp.broadcast_to(r0_ref[hh:hh + 1, :], (BQ, TOEP_W))
        bias.append(pltpu.roll(row, 0, 1, stride=1, stride_axis=0)[:, :BAND_W])
    return static_ok, bias


def _band_fwd(name, qkv, r0):
    def body(qkv_ref, r0_ref, o_ref, kpad, vpad):
        m0b, m1b = _head_masks(BF16)
        static_ok, bias = _band_setup(qkv_ref, r0_ref, kpad, vpad)

        def qblock(m, _):
            r0_, _, _, vw, ps = _band_tiles(r0_ref, qkv_ref, kpad, vpad, m, m0b, m1b, static_ok, bias)
            o = _dot(ps[0].astype(BF16), vw * m0b) + _dot(ps[1].astype(BF16), vw * m1b)
            o_ref[pl.ds(r0_, BQ), :] = o.astype(BF16)
            return 0

        lax.fori_loop(0, S // BQ, qblock, 0)

    return pl.pallas_call(
        body, name=name, grid=(C_H // 2,),
        in_specs=[pl.BlockSpec((3, S, LANES), lambda p: (0, 0, p)), pl.BlockSpec((None, 2, TOEP_W), lambda p: (p, 0, 0))],
        out_specs=pl.BlockSpec((S, LANES), lambda p: (0, p)),
        out_shape=jax.ShapeDtypeStruct((S, C_H * C_DIM), BF16),
        scratch_shapes=[pltpu.VMEM((S + BAND_PAD, LANES), BF16), pltpu.VMEM((S + BAND_PAD, LANES), BF16)],
        compiler_params=_params("parallel"),
    )(qkv, r0)


def _band_bwd(name, qkv, r0, do, carry=None):
    def body(qkv_ref, r0_ref, do_ref, dqkv_ref, dr0_ref, kpad, vpad, dkpad, dvpad, db_acc):
        m0b, m1b = _head_masks(BF16)
        static_ok, bias = _band_setup(qkv_ref, r0_ref, kpad, vpad)
        dkpad[...] = jnp.zeros_like(dkpad)
        dvpad[...] = jnp.zeros_like(dvpad)
        db_acc[...] = jnp.zeros_like(db_acc)

        def qblock(m, _):
            r0_, qs, kw, vw, ps = _band_tiles(r0_ref, qkv_ref, kpad, vpad, m, m0b, m1b, static_ok, bias)
            dob = do_ref[pl.ds(r0_, BQ), :].astype(BF16)
            dos = [dob * m0b, dob * m1b]
            dq = None
            dk = None
            dv = None
            for hh in range(2):
                p = ps[hh]
                dp = _dot(dos[hh], vw, "nt")
                ds = p * (dp - jnp.sum(dp * p, axis=-1, keepdims=True))
                db_acc[hh, :, 0:BAND_W] += ds
                dsb = (ds * SCALE_C).astype(BF16)
                t = _dot(dsb, kw * (m0b if hh == 0 else m1b))
                dq = t if dq is None else dq + t
                t = _dot(dsb, qs[hh], "tn")
                dk = t if dk is None else dk + t
                t = _dot(p.astype(BF16), dos[hh], "tn")
                dv = t if dv is None else dv + t
            dqkv_ref[0, pl.ds(r0_, BQ), :] = dq.astype(BF16)
            dkpad[pl.ds(r0_, BAND_W), :] += dk
            dvpad[pl.ds(r0_, BAND_W), :] += dv
            return 0

        lax.fori_loop(0, S // BQ, qblock, 0)
        dqkv_ref[1] = dkpad[BAND_PAD:, :].astype(BF16)
        dqkv_ref[2] = dvpad[BAND_PAD:, :].astype(BF16)
        sub = _iota((8, TOEP_W), 0)
        for hh in range(2):
            folded = db_acc[hh, 0:8, :]
            for a in range(1, BQ // 8):
                folded = folded + pltpu.roll(db_acc[hh, 8 * a:8 * a + 8, :], TOEP_W - 8 * a, 1)
            for bit in range(3):
                moved = pltpu.roll(folded, TOEP_W - (1 << bit), 1)
                folded = jnp.where((sub & (1 << bit)) != 0, moved, folded)
            dr0_ref[hh:hh + 1, :] = jnp.sum(folded, axis=0, keepdims=True)

    return _carrier_call(
        body, name=name, grid=(C_H // 2,),
        in_specs=[pl.BlockSpec((3, S, LANES), lambda p: (0, 0, p)), pl.BlockSpec((None, 2, TOEP_W), lambda p: (p, 0, 0)),
                  pl.BlockSpec((S, LANES), lambda p: (0, p))],
        out_specs=[pl.BlockSpec((3, S, LANES), lambda p: (0, 0, p)), pl.BlockSpec((None, 2, TOEP_W), lambda p: (p, 0, 0))],
        out_shape=[jax.ShapeDtypeStruct((3, S, C_H * C_DIM), BF16), jax.ShapeDtypeStruct((C_H // 2, 2, TOEP_W), F32)],
        scratch_shapes=[pltpu.VMEM((S + BAND_PAD, LANES), BF16), pltpu.VMEM((S + BAND_PAD, LANES), BF16),
                        pltpu.VMEM((S + BAND_PAD, LANES), F32), pltpu.VMEM((S + BAND_PAD, LANES), F32),
                        pltpu.VMEM((2, BQ, TOEP_W), F32)],
        args=(qkv, r0, do), sem=("parallel",), carry=carry)


def _bias_table_grad(name, dr0):
    w_out = 5 * LANES

    def body(d_ref, o_ref):
        j = _iota((TOEP_W, w_out), 0)
        rel = jnp.clip(LEFT_CHUNKS * CHUNK - j, -REL_CLIP, REL_CLIP) + REL_CLIP
        rel = jnp.where(j >= BAND_W, 2 * REL_CLIP, rel)
        onehot = (rel == _iota((TOEP_W, w_out), 1)).astype(BF16)
        d = d_ref[...]
        hi = d.astype(BF16)
        mid = (d - hi.astype(F32))
        mid_b = mid.astype(BF16)
        lo = (mid - mid_b.astype(F32)).astype(BF16)
        o_ref[...] = _dot(hi, onehot) + _dot(mid_b, onehot) + _dot(lo, onehot)

    return pl.pallas_call(
        body, name=name, out_shape=jax.ShapeDtypeStruct((C_H, w_out), F32),
        in_specs=[pl.BlockSpec((C_H, TOEP_W), lambda: (0, 0))], out_specs=pl.BlockSpec((C_H, w_out), lambda: (0, 0)),
        grid=(),
    )(dr0)


def _carry_gather(cy, slots, names, ici, d2d):
    idx = [cy.operand(slots[n], True) for n in names]
    n = len(names)
    base_i = cy.sems(3 * n) if ici else 0
    base_d = cy.sems(3 * n) if d2d else 0

    def piece(refs, t, slot, cc):
        return refs[idx[t]].at[slot, _half_rows(cc, slots[names[t]].shape[1] // 2), :]

    def over_ici(refs, send, recv, arriving):
        x, y, c, chips = _position()
        out = []
        for t in range(n):
            for j in range(3):
                r = piece(refs, t, 2 * chips[j][0] + chips[j][1] if arriving else 2 * x + y, c)
                out.append(_remote(r, r, send, recv, base_i + 3 * t + j, (*chips[j], c)))
        return out

    def over_d2d(refs, send, recv, arriving):
        x, y, c, chips = _position()
        out = []
        for t in range(n):
            for j in range(3):
                r = piece(refs, t, 2 * chips[j][0] + chips[j][1], 1 - c if arriving else c)
                out.append(_remote(r, r, send, recv, base_d + 3 * t + j, (x, y, 1 - c)))
        return out

    def start_ici(refs, fresh, send, recv):
        for cp in over_ici(refs, send, recv, False):
            cp.start()

    def wait_ici(refs, fresh, send, recv):
        for cp in over_ici(refs, send, recv, True):
            cp.wait_recv()
        for cp in over_ici(refs, send, recv, False):
            cp.wait_send()

    def start_d2d(refs, fresh, send, recv):
        for cp in over_d2d(refs, send, recv, False):
            cp.start()

    def wait_d2d(refs, fresh, send, recv):
        for cp in over_d2d(refs, send, recv, True):
            cp.wait_recv()
        for cp in over_d2d(refs, send, recv, False):
            cp.wait_send()

    if ici and d2d:
        cy.starts.append(start_ici)
        cy.finishes += [wait_ici, start_d2d, wait_d2d]
    elif ici:
        cy.starts.append(start_ici)
        cy.finishes.append(wait_ici)
    else:
        cy.starts.append(start_d2d)
        cy.finishes.append(wait_d2d)

    def done(aliased, fresh):
        for t, name in enumerate(names):
            slots[name] = aliased[idx[t]]

    cy.on_done.append(done)


def _carry_chip_exchange(cy, sums, got, names):
    idx = [cy.operand(sums[n], False) for n in names]
    out = [cy.result((3,) + sums[n].shape[1:], BF16) for n in names]
    base = cy.sems(3 * len(names))

    def copies(refs, fresh, send, recv):
        x, y, c, chips = _position()
        return [_remote(refs[idx[t]].at[2 * chips[j][0] + chips[j][1]], fresh[out[t]].at[j], send, recv, base + 3 * t + j,
                        (*chips[j], c)) for t in range(len(names)) for j in range(3)]

    def start(refs, fresh, send, recv):
        for cp in copies(refs, fresh, send, recv):
            cp.start()

    def wait(refs, fresh, send, recv):
        for cp in copies(refs, fresh, send, recv):
            cp.wait()

    cy.starts.append(start)
    cy.finishes.append(wait)

    def done(aliased, fresh):
        for t, name in enumerate(names):
            got[name] = fresh[out[t]]

    cy.on_done.append(done)


def _run_carry(name, cy):
    _, res = _carrier_call(None, name=name, grid=(), in_specs=[], out_specs=[], out_shape=[], args=(), sem=(), carry=cy)
    cy.done(res)


FIRST_WEIGHTS = ("ev_w_in", "ev_w_uq", "ev_w_ukv")
LAYER0_WEIGHTS = ("ev_w_out", "w_gate0", "w_up0", "w_down0")
LAYER1_WEIGHTS = ("od_w_qkv", "od_w_out", "w_gate1", "w_up1", "w_down1")
GRAD_GROUPS = {"ffn1": ("w_gate1", "w_up1", "w_down1"), "od": ("od_w_qkv", "od_w_out"),
               "ffn0": ("w_gate0", "w_up0", "w_down0"), "ev": ("ev_w_in", "ev_w_uq", "ev_w_ukv", "ev_w_out")}


class _Exchanges:
    def __init__(self, slots, pos):
        self.slots, self.pos = dict(slots), pos
        self.parts, self.sums, self.got = {}, {}, {}

    def begin(self):
        cy = _Carry()
        _carry_gather(cy, self.slots, FIRST_WEIGHTS, True, True)
        _run_carry("gather_first", cy)

    def weights(self, *names):
        return [self.slots[n] for n in names]

    def carry(self, stage):
        cy = _Carry()
        if stage == "mla_attn":
            _carry_gather(cy, self.slots, LAYER0_WEIGHTS, True, False)
        elif stage == "sb_attn":
            _carry_gather(cy, self.slots, LAYER0_WEIGHTS, False, True)
            _carry_gather(cy, self.slots, LAYER1_WEIGHTS, True, False)
        elif stage == "ffn0":
            _carry_gather(cy, self.slots, LAYER1_WEIGHTS, False, True)
        elif stage == "band_attn_bwd":
            _carry_chip_exchange(cy, self.sums, self.got, GRAD_GROUPS["ffn1"])
        elif stage == "mla_attn_bwd":
            _carry_chip_exchange(cy, self.sums, self.got, GRAD_GROUPS["od"])
        elif stage == "sb_attn_bwd":
            _carry_chip_exchange(cy, self.sums, self.got, GRAD_GROUPS["ffn0"])
        else:
            raise ValueError(stage)
        return cy

    def grads(self, group, parts):
        names = GRAD_GROUPS[group]
        self.parts.update(parts)
        theirs = _pair_exchange("grads_pair_" + group, [parts[n] for n in names])
        for n, t in zip(names, theirs):
            self.sums[n] = _pair_sum("pair_sum_" + n, parts[n], t, self.pos)

    def finish(self, shapes):
        cy = _Carry()
        _carry_chip_exchange(cy, self.sums, self.got, GRAD_GROUPS["ev"])
        _run_carry("grads_chips_ev", cy)
        fulls = {}
        for part_name, n, layer in GRAD_PARTS:
            fulls[n] = _chip_sum("chip_sum_" + part_name, self.sums[part_name], self.got[part_name], self.pos, layer,
                                 shapes[n], fulls.get(n))
        return dict(zip(BIG, _sibling_exchange("grads_sibling", [fulls[n] for n in BIG])))


class _NoExchanges:
    def __init__(self, slots):
        self.slots, self.parts = dict(slots), {}

    def begin(self):
        pass

    def weights(self, *names):
        return [self.slots[n] for n in names]

    def carry(self, stage):
        return None

    def grads(self, group, parts):
        self.parts.update(parts)


def _first_weights(w_in_s, w_uq_s, w_ukv_s):
    gw = {"ev_w_in": w_in_s, "ev_w_uq": w_uq_s, "ev_w_ukv": w_ukv_s}
    w_in = jnp.moveaxis(gw["ev_w_in"], 0, 1).reshape(D, EVEN_IN)
    z = lambda n: jnp.zeros((D, n), BF16)
    w_in_p = jnp.concatenate(
        [w_in[:, 0:384], z(128), w_in[:, 384:640], w_in[:, 672:2208], z(KR_LANE), w_in[:, 640:672],
         z(LANES - KR_LANE - MLA_ROPE)], axis=1)
    w_uq = jnp.moveaxis(gw["ev_w_uq"], 0, 1).reshape(Q_LORA, MLA_H, MLA_NOPE + MLA_ROPE)
    w_uq_p = jnp.concatenate([w_uq, jnp.zeros((Q_LORA, MLA_H, LANES - MLA_NOPE - MLA_ROPE), BF16)], axis=2)
    w_ukv = jnp.moveaxis(gw["ev_w_ukv"], 0, 1).reshape(KV_LORA, MLA_H, MLA_NOPE + MLA_V)
    w_uk_p = jnp.concatenate([w_ukv[:, :, :MLA_NOPE], jnp.zeros((KV_LORA, MLA_H, LANES - MLA_NOPE), BF16)], axis=2)
    return dict(
        w_in=w_in_p, w_uq=w_uq_p.reshape(Q_LORA, MLA_H * LANES), w_uk=w_uk_p.reshape(KV_LORA, MLA_H * LANES),
        w_uv=w_ukv[:, :, MLA_NOPE:].reshape(KV_LORA, MLA_H * MLA_V))


def _proj_mm(name, u, w_in):
    return _mm(name, u, w_in, kind="nn", grid=(S // TM, 1, 1),
               a_spec=pl.BlockSpec((TM, D), lambda i, j, k: (i, 0)), b_spec=pl.BlockSpec((D, P_IN), lambda i, j, k: (0, 0)),
               o_spec=pl.BlockSpec((TM, P_IN), lambda i, j, k: (i, 0)), out_shape=(S, P_IN), out_dtype=F32, acc_shape=None)


def _out_proj(name, o, w, resid):
    return _mm(name, o, w, kind="nn", grid=(S // TM, 1, 1),
               a_spec=pl.BlockSpec((TM, D), lambda i, j, k: (i, 0)), b_spec=pl.BlockSpec((D, D), lambda i, j, k: (0, 0)),
               o_spec=pl.BlockSpec((TM, D), lambda i, j, k: (i, 0)), out_shape=(S, D), out_dtype=F32, acc_shape=None,
               resid=resid, r_spec=pl.BlockSpec((TM, D), lambda i, j, k: (i, 0)))


def _out_proj_bwd(name, dh, o, w):
    d_o = _mm(name + "_x", dh, w, kind="nt", grid=(S // TM, 1, 1),
              a_spec=pl.BlockSpec((TM, D), lambda i, j, k: (i, 0)), b_spec=pl.BlockSpec((D, D), lambda i, j, k: (0, 0)),
              o_spec=pl.BlockSpec((TM, D), lambda i, j, k: (i, 0)), out_shape=(S, D), out_dtype=F32, acc_shape=None)
    d_w = _mm(name + "_w", o, dh, kind="tn", grid=(2, S // TM),
              a_spec=pl.BlockSpec((TM, TM), lambda j, k: (k, j)), b_spec=pl.BlockSpec((TM, D), lambda j, k: (k, 0)),
              o_spec=pl.BlockSpec((TM, D), lambda j, k: (j, 0)), out_shape=(D, D), out_dtype=BF16, acc_shape=(TM, D))
    return d_o, d_w


def _local_step(x, tgt, sm, ex):
    def riding(stage, fn, *args):
        cy = ex.carry(stage)
        res, copies = fn(stage, *args, carry=cy)
        if cy is not None:
            cy.done(copies)
        return res

    cos_t, sin_t = _rope_tables()
    g_mix, g_ffn = sm["g_mix"], sm["g_ffn"]
    r0 = sm["od_rel_bias"][0][:, _band_row_index()].reshape(C_H // 2, 2, TOEP_W)
    nt = 3 * D // 256

    ex.begin()
    w = _first_weights(*ex.weights(*FIRST_WEIGHTS))
    u0 = _rms_fwd("rms_mix0", x, g_mix[0:1])
    proj = _proj_mm("proj_in", u0, w["w_in"])
    qa, ka, va = _mla_prep_fwd("mla_prep", proj, sm["ev_g_cq"], sm["ev_g_ckv"], w["w_uq"], w["w_uk"], w["w_uv"], cos_t, sin_t)
    o_a, lse = riding("mla_attn", _mla_fwd, qa, ka, va)
    o_b, = riding("sb_attn", _sb_fwd, proj)
    o_ev = jnp.concatenate([o_a.astype(BF16), o_b], axis=1)
    w["ev_w_out"], w["w_gate0"], w["w_up0"], w["w_down0"] = ex.weights(*LAYER0_WEIGHTS)
    w["ev_w_out"] = w["ev_w_out"].reshape(D, D)
    h1 = _out_proj("ev_out", o_ev, w["ev_w_out"], x)
    h2, gate0, up0 = riding("ffn0", _ffn_fwd, h1, g_ffn[0:1], w["w_gate0"], w["w_up0"], w["w_down0"])
    w["w_qkv"], w["od_w_out"], w["w_gate1"], w["w_up1"], w["w_down1"] = ex.weights(*LAYER1_WEIGHTS)
    w["od_w_out"] = w["od_w_out"].reshape(D, D)
    u2 = _rms_fwd("rms_mix1", h2, g_mix[1:2])
    qkv = _mm("qkv", u2, w["w_qkv"], kind="nn", grid=(S // TM, nt, 1),
              a_spec=pl.BlockSpec((TM, D), lambda i, t, k: (i, 0)),
              b_spec=pl.BlockSpec((None, D, 256), lambda i, t, k: (t // 3, 0, t % 3)),
              o_spec=pl.BlockSpec((None, TM, 256), lambda i, t, k: (t // 4, i, t % 4)),
              out_shape=(3, S, D), out_dtype=BF16, acc_shape=None)
    o_od = _band_fwd("band_attn", qkv, r0)
    h3 = _out_proj("od_out", o_od, w["od_w_out"], h2)
    (h4, gate1, up1), _ = _ffn_fwd("ffn1", h3, g_ffn[1:2], w["w_gate1"], w["w_up1"], w["w_down1"])

    loss, dh4, dg_final = _loss_bwd("loss", h4, sm["g_final"].reshape(1, D), tgt)

    dh3, dg_ffn1, u3, dgate, dup, act = _ffn_bwd("ffn1_bwd", dh4, h3, g_ffn[1:2], gate1, up1,
                                                 w["w_gate1"], w["w_up1"], w["w_down1"])
    d_wg1, d_wu1, d_wd1 = _ffn_wgrads("ffn1_dw", u3, dgate, dup, act, dh4)
    ex.grads("ffn1", {"w_gate1": d_wg1, "w_up1": d_wu1, "w_down1": d_wd1})

    d_ood, d_w_od_out = _out_proj_bwd("od_out_bwd", dh3, o_od, w["od_w_out"])
    dqkv, dr0 = riding("band_attn_bwd", _band_bwd, qkv, r0, d_ood)
    du2 = _mm("qkv_bwd_x", dqkv, w["w_qkv"], kind="nt", grid=(S // TM, nt),
              a_spec=pl.BlockSpec((None, TM, 256), lambda i, t: (t // 4, i, t % 4)),
              b_spec=pl.BlockSpec((None, D, 256), lambda i, t: (t // 3, 0, t % 3)),
              o_spec=pl.BlockSpec((TM, D), lambda i, t: (i, 0)), out_shape=(S, D), out_dtype=F32, acc_shape=(TM, D))
    d_w_qkv = _mm("qkv_bwd_w", u2, dqkv, kind="tn", grid=(nt, S // TM),
                  a_spec=pl.BlockSpec((TM, D), lambda t, k: (k, 0)),
                  b_spec=pl.BlockSpec((None, TM, 256), lambda t, k: (t // 4, k, t % 4)),
                  o_spec=pl.BlockSpec((None, D, 256), lambda t, k: (t // 3, 0, t % 3)),
                  out_shape=(N_CHIPS, D, 768), out_dtype=BF16, acc_shape=(D, 256))
    ex.grads("od", {"od_w_qkv": d_w_qkv, "od_w_out": d_w_od_out.reshape(N_CHIPS, D // N_CHIPS, D)})
    dh2, dg_mix1 = _rms_bwd("rms_mix1_bwd", du2, h2, g_mix[1:2], dh3)
    d_rel = _bias_table_grad("rel_bias_grad", dr0.reshape(C_H, TOEP_W))[:, :2 * REL_CLIP + 1]

    dh1, dg_ffn0, u1, dgate, dup, act = _ffn_bwd("ffn0_bwd", dh2, h1, g_ffn[0:1], gate0, up0,
                                                 w["w_gate0"], w["w_up0"], w["w_down0"])
    d_wg0, d_wu0, d_wd0 = _ffn_wgrads("ffn0_dw", u1, dgate, dup, act, dh2)
    ex.grads("ffn0", {"w_gate0": d_wg0, "w_up0": d_wu0, "w_down0": d_wd0})

    d_oev, d_w_ev_out = _out_proj_bwd("ev_out_bwd", dh1, o_ev, w["ev_w_out"])
    dqa, dka, dva = riding("mla_attn_bwd", _mla_bwd, qa, ka, va, o_a, lse, d_oev, 0)
    dqb, dkb, dvb = riding("sb_attn_bwd", _sb_bwd, proj, d_oev, MLA_H * MLA_V // LANES)
    dcq, dckv, dkr, d_w_uq, d_w_uk, d_w_uv, dg_cq, dg_ckv = _mla_prep_bwd(
        "mla_prep_bwd", dqa, dka, dva, proj, sm["ev_g_cq"], sm["ev_g_ckv"], w["w_uq"], w["w_uk"], w["w_uv"], cos_t, sin_t)
    dproj = jnp.concatenate([dcq, jnp.zeros((S, LANES), BF16), dckv, dqb, dkb, dvb, dkr], axis=1)
    du0 = _mm("proj_in_bwd_x", dproj, w["w_in"], kind="nt", grid=(S // TM, 1, 1),
              a_spec=pl.BlockSpec((TM, P_IN), lambda i, j, k: (i, 0)), b_spec=pl.BlockSpec((D, P_IN), lambda i, j, k: (0, 0)),
              o_spec=pl.BlockSpec((TM, D), lambda i, j, k: (i, 0)), out_shape=(S, D), out_dtype=F32, acc_shape=None)
    d_w_in_p = _mm("proj_in_bwd_w", u0, dproj, kind="tn", grid=(1, S // TM),
                   a_spec=pl.BlockSpec((TM, D), lambda j, k: (k, 0)), b_spec=pl.BlockSpec((TM, P_IN), lambda j, k: (k, 0)),
                   o_spec=pl.BlockSpec((D, P_IN), lambda j, k: (0, 0)), out_shape=(D, P_IN), out_dtype=BF16,
                   acc_shape=(D, P_IN))
    grad_x, dg_mix0 = _rms_bwd("rms_mix0_bwd", du0, x, g_mix[0:1], dh1)

    d_w_in = jnp.concatenate([d_w_in_p[:, 0:384], d_w_in_p[:, 512:768],
                              d_w_in_p[:, P_KR + KR_LANE:P_KR + KR_LANE + MLA_ROPE], d_w_in_p[:, 768:2304]], axis=1)
    shard_cols = lambda a: jnp.moveaxis(a.reshape(a.shape[0], N_CHIPS, a.shape[1] // N_CHIPS), 1, 0)
    d_w_uq_std = d_w_uq.reshape(Q_LORA, MLA_H, LANES)[:, :, :MLA_NOPE + MLA_ROPE].reshape(Q_LORA, -1)
    d_w_ukv = jnp.concatenate([d_w_uk.reshape(KV_LORA, MLA_H, LANES)[:, :, :MLA_NOPE],
                               d_w_uv.reshape(KV_LORA, MLA_H, MLA_V)], axis=2).reshape(KV_LORA, -1)
    ex.grads("ev", {"ev_w_in": shard_cols(d_w_in), "ev_w_uq": shard_cols(d_w_uq_std.astype(BF16)),
                    "ev_w_ukv": shard_cols(d_w_ukv.astype(BF16)),
                    "ev_w_out": d_w_ev_out.reshape(N_CHIPS, D // N_CHIPS, D)})
    small = {
        "ev_g_cq": dg_cq, "ev_g_ckv": dg_ckv, "od_rel_bias": d_rel.reshape(1, C_H, 2 * REL_CLIP + 1),
        "g_mix": jnp.concatenate([dg_mix0, dg_mix1], axis=0), "g_ffn": jnp.concatenate([dg_ffn0, dg_ffn1], axis=0),
        "g_final": dg_final.reshape(D),
    }
    return loss, grad_x, small


BIG = ("ev_w_in", "ev_w_uq", "ev_w_ukv", "ev_w_out", "od_w_qkv", "od_w_out", "w_gate", "w_up", "w_down")
SMALL = ("ev_g_cq", "ev_g_ckv", "od_rel_bias", "g_mix", "g_ffn", "g_final")
WEIGHTS = ("ev_w_in", "ev_g_cq", "ev_w_uq", "ev_g_ckv", "ev_w_ukv", "ev_w_out", "od_w_qkv", "od_rel_bias", "od_w_out",
           "g_mix", "g_ffn", "w_gate", "w_up", "w_down", "g_final")
GRAD_PARTS = (("ev_w_in", "ev_w_in", 0), ("ev_w_uq", "ev_w_uq", 0), ("ev_w_ukv", "ev_w_ukv", 0),
              ("ev_w_out", "ev_w_out", 0), ("od_w_qkv", "od_w_qkv", 0), ("od_w_out", "od_w_out", 0),
              ("w_gate0", "w_gate", 0), ("w_gate1", "w_gate", 1), ("w_up0", "w_up", 0), ("w_up1", "w_up", 1),
              ("w_down0", "w_down", 0), ("w_down1", "w_down", 1))
SMALL_ROWS = 112


def _row_tile(rows, cap=512):
    for t in range(min(rows, cap), 0, -1):
        if rows % t == 0 and t % 16 == 0:
            return t
    return rows


def _cast_into_slot(name, w, layer, pos):
    _, rows, cols = w.shape
    tr = _row_tile(rows)

    def body(pos_ref, w_ref, o_ref):
        o_ref[...] = w_ref[...].astype(BF16)

    return pl.pallas_call(
        body, name=name,
        grid_spec=pltpu.PrefetchScalarGridSpec(
            num_scalar_prefetch=1, grid=(rows // tr,),
            in_specs=[pl.BlockSpec((None, tr, cols), lambda i, p: (layer, i, 0))],
            out_specs=pl.BlockSpec((None, tr, cols), lambda i, p: (p[0], i, 0))),
        out_shape=jax.ShapeDtypeStruct((N_CHIPS, rows, cols), BF16), compiler_params=_params("arbitrary"))(pos, w)


def _pair_exchange(name, parts):
    n = len(parts)

    def body(*refs):
        f, theirs = refs[:n], refs[n:2 * n]
        send_sem, recv_sem = refs[2 * n:]
        x, y, c, _ = _position()
        out = [pltpu.make_async_remote_copy(
            src_ref=f[t].at[:, _half_rows(1 - c, parts[t].shape[1] // 2), :], dst_ref=theirs[t], send_sem=send_sem.at[t],
            recv_sem=recv_sem.at[t], device_id=(x, y, 1 - c), device_id_type=MESH) for t in range(n)]
        for cp in out:
            cp.start()
        for cp in out:
            cp.wait()

    return pl.pallas_call(
        body, name=name, in_specs=[ANY] * n, out_specs=[ANY] * n,
        out_shape=[jax.ShapeDtypeStruct((N_CHIPS, p.shape[1] // 2, p.shape[2]), BF16) for p in parts],
        scratch_shapes=[pltpu.SemaphoreType.DMA((n,)), pltpu.SemaphoreType.DMA((n,))],
    )(*parts)


def _pair_sum(name, part, theirs, pos):
    _, half, cols = theirs.shape
    tr = _row_tile(half)
    nb = half // tr

    def body(pos_ref, a_ref, b_ref, o_ref):
        o_ref[...] = (a_ref[...].astype(F32) + b_ref[...].astype(F32)).astype(BF16)

    return pl.pallas_call(
        body, name=name,
        grid_spec=pltpu.PrefetchScalarGridSpec(
            num_scalar_prefetch=1, grid=(N_CHIPS, nb),
            in_specs=[pl.BlockSpec((None, tr, cols), lambda s, i, p: (s, p[1] * nb + i, 0)),
                      pl.BlockSpec((None, tr, cols), lambda s, i, p: (s, i, 0))],
            out_specs=pl.BlockSpec((None, tr, cols), lambda s, i, p: (s, i, 0))),
        out_shape=jax.ShapeDtypeStruct(theirs.shape, BF16),
        compiler_params=_params("arbitrary", "arbitrary"))(pos, part, theirs)


def _chip_sum(name, sums, got, pos, layer, full_shape, full=None):
    _, half, cols = sums.shape
    tr = _row_tile(half)
    nb = half // tr

    def body(pos_ref, s_ref, g_ref, *rest):
        out_ref = rest[-1]
        out_ref[...] = ((s_ref[...].astype(F32) + g_ref[0].astype(F32)) + g_ref[1].astype(F32)) + g_ref[2].astype(F32)

    in_specs = [pl.BlockSpec((None, tr, cols), lambda i, p: (p[0], i, 0)),
                pl.BlockSpec((3, tr, cols), lambda i, p: (0, i, 0))]
    args = [pos, sums, got]
    if full is not None:
        in_specs.append(ANY)
        args.append(full)
    return pl.pallas_call(
        body, name=name,
        grid_spec=pltpu.PrefetchScalarGridSpec(
            num_scalar_prefetch=1, grid=(nb,), in_specs=in_specs,
            out_specs=pl.BlockSpec((None, tr, cols), lambda i, p: (layer, p[1] * nb + i, 0))),
        out_shape=jax.ShapeDtypeStruct(full_shape, F32),
        input_output_aliases={3: 0} if full is not None else {},
        compiler_params=_params("arbitrary"))(*args)


def _sibling_exchange(name, fulls):
    n = len(fulls)

    def body(*refs):
        g = refs[n:2 * n]
        send_sem, recv_sem = refs[2 * n:]
        x, y, c, _ = _position()

        def half(t, cc):
            return g[t].at[:, _half_rows(cc, fulls[t].shape[1] // 2), :]

        out = [pltpu.make_async_remote_copy(
            src_ref=half(t, c), dst_ref=half(t, c), send_sem=send_sem.at[t], recv_sem=recv_sem.at[t],
            device_id=(x, y, 1 - c), device_id_type=MESH) for t in range(n)]
        for cp in out:
            cp.start()
        for t in range(n):
            out[t].wait_send()
            pltpu.make_async_remote_copy(
                src_ref=half(t, 1 - c), dst_ref=half(t, 1 - c), send_sem=send_sem.at[t], recv_sem=recv_sem.at[t],
                device_id=(x, y, 1 - c), device_id_type=MESH).wait_recv()

    return pl.pallas_call(
        body, name=name, in_specs=[ANY] * n, out_specs=[ANY] * n,
        out_shape=[jax.ShapeDtypeStruct(f.shape, F32) for f in fulls],
        input_output_aliases={t: t for t in range(n)},
        scratch_shapes=[pltpu.SemaphoreType.DMA((n,)), pltpu.SemaphoreType.DMA((n,))],
    )(*fulls)


def _all_reduce_small(name, packed):
    n_dev = 8

    def body(p_ref, o_ref, slots, send_sem, recv_sem):
        x, y, c, _ = _position()
        me = 4 * x + 2 * y + c

        def peer(k):
            return (1 - x if k & 4 else x, 1 - y if k & 2 else y, 1 - c if k & 1 else c)

        def logical(k):
            px, py, pc = peer(k)
            return 4 * px + 2 * py + pc

        slots[me] = p_ref[...]
        sends = [pltpu.make_async_remote_copy(
            src_ref=p_ref, dst_ref=slots.at[me], send_sem=send_sem.at[k], recv_sem=recv_sem.at[k],
            device_id=peer(k), device_id_type=MESH) for k in range(1, n_dev)]
        for cp in sends:
            cp.start()
        for k in range(1, n_dev):
            pltpu.make_async_remote_copy(
                src_ref=p_ref, dst_ref=slots.at[logical(k)], send_sem=send_sem.at[k], recv_sem=recv_sem.at[k],
                device_id=peer(k), device_id_type=MESH).wait_recv()
        for cp in sends:
            cp.wait_send()
        total = slots[0]
        for d in range(1, n_dev):
            total = total + slots[d]
        o_ref[...] = total

    vm = pl.BlockSpec(memory_space=pltpu.VMEM)
    return pl.pallas_call(
        body, name=name, in_specs=[vm], out_specs=vm, out_shape=jax.ShapeDtypeStruct(packed.shape, F32),
        scratch_shapes=[pltpu.VMEM((n_dev,) + packed.shape, F32), pltpu.SemaphoreType.DMA((n_dev,)),
                        pltpu.SemaphoreType.DMA((n_dev,))],
    )(packed)


def _adamw(name, w, g, m, v):
    rows, cols = w.shape
    tr = _row_tile(rows)

    def body(w_ref, g_ref, m_ref, v_ref, d_ref, mo_ref, vo_ref):
        gv = g_ref[...]
        m_new = ADAM_B1 * m_ref[...] + (1.0 - ADAM_B1) * gv
        v_new = ADAM_B2 * v_ref[...] + (1.0 - ADAM_B2) * (gv * gv)
        m_hat = m_new / (1.0 - ADAM_B1 ** ADAM_STEP)
        v_hat = v_new / (1.0 - ADAM_B2 ** ADAM_STEP)
        d_ref[...] = -ADAM_LR * (m_hat / (jnp.sqrt(v_hat) + ADAM_EPS) + ADAM_WD * w_ref[...])
        mo_ref[...] = m_new
        vo_ref[...] = v_new

    spec = pl.BlockSpec((tr, cols), lambda i: (i, 0))
    shape = jax.ShapeDtypeStruct((rows, cols), F32)
    return pl.pallas_call(body, name=name, grid=(rows // tr,), in_specs=[spec] * 4, out_specs=[spec] * 3,
                          out_shape=[shape] * 3, compiler_params=_params("parallel"))(w, g, m, v)


def _pack_small(tree):
    flat = jnp.concatenate([tree[n].reshape(-1).astype(F32) for n in SMALL])
    return jnp.pad(flat, (0, SMALL_ROWS * LANES - flat.shape[0])).reshape(SMALL_ROWS, LANES)


def _unpack_small(packed, like):
    flat = packed.reshape(-1)
    out, off = {}, 0
    for n in SMALL:
        size = int(np.prod(like[n].shape))
        out[n] = flat[off:off + size].reshape(like[n].shape)
        off += size
    return out


def kernel(x, ev_w_in, ev_g_cq, ev_w_uq, ev_g_ckv, ev_w_ukv, ev_w_out, od_w_qkv, od_rel_bias, od_w_out, g_mix, g_ffn, w_gate, w_up, w_down, g_final, loss_target, m_ev_w_in, m_ev_g_cq, m_ev_w_uq, m_ev_g_ckv, m_ev_w_ukv, m_ev_w_out, m_od_w_qkv, m_od_rel_bias, m_od_w_out, m_g_mix, m_g_ffn, m_w_gate, m_w_up, m_w_down, m_g_final, v_ev_w_in, v_ev_g_cq, v_ev_w_uq, v_ev_g_ckv, v_ev_w_ukv, v_ev_w_out, v_od_w_qkv, v_od_rel_bias, v_od_w_out, v_g_mix, v_g_ffn, v_w_gate, v_w_up, v_w_down, v_g_final):
    w = dict(ev_w_in=ev_w_in, ev_g_cq=ev_g_cq, ev_w_uq=ev_w_uq, ev_g_ckv=ev_g_ckv, ev_w_ukv=ev_w_ukv, ev_w_out=ev_w_out,
             od_w_qkv=od_w_qkv, od_rel_bias=od_rel_bias, od_w_out=od_w_out, g_mix=g_mix, g_ffn=g_ffn, w_gate=w_gate,
             w_up=w_up, w_down=w_down, g_final=g_final)
    m = dict(ev_w_in=m_ev_w_in, ev_g_cq=m_ev_g_cq, ev_w_uq=m_ev_w_uq, ev_g_ckv=m_ev_g_ckv, ev_w_ukv=m_ev_w_ukv,
             ev_w_out=m_ev_w_out, od_w_qkv=m_od_w_qkv, od_rel_bias=m_od_rel_bias, od_w_out=m_od_w_out, g_mix=m_g_mix,
             g_ffn=m_g_ffn, w_gate=m_w_gate, w_up=m_w_up, w_down=m_w_down, g_final=m_g_final)
    v = dict(ev_w_in=v_ev_w_in, ev_g_cq=v_ev_g_cq, ev_w_uq=v_ev_w_uq, ev_g_ckv=v_ev_g_ckv, ev_w_ukv=v_ev_w_ukv,
             ev_w_out=v_ev_w_out, od_w_qkv=v_od_w_qkv, od_rel_bias=v_od_rel_bias, od_w_out=v_od_w_out, g_mix=v_g_mix,
             g_ffn=v_g_ffn, w_gate=v_w_gate, w_up=v_w_up, w_down=v_w_down, g_final=v_g_final)
    flat2d = lambda a: a.reshape(-1, a.shape[-1])

    pos = jnp.stack([2 * lax.axis_index("x") + lax.axis_index("y"), lax.axis_index("c")]).astype(jnp.int32)

    slots = {part: _cast_into_slot("cast_" + part, w[n], layer, pos) for part, n, layer in GRAD_PARTS}
    ex = _Exchanges(slots, pos)

    loss_local, grad_x, small = _local_step(x[0], loss_target[0], {n: w[n] for n in SMALL}, ex)

    grads = ex.finish({n: w[n].shape for n in BIG})
    small_sum = _all_reduce_small("small_sum", _pack_small(small))
    grads.update(_unpack_small(small_sum, w))

    delta, new_m, new_v = {}, {}, {}
    for n in BIG:
        d_, m_, v_ = _adamw("adamw_" + n, flat2d(w[n]), flat2d(grads[n]), flat2d(m[n]), flat2d(v[n]))
        delta[n], new_m[n], new_v[n] = d_.reshape(w[n].shape), m_.reshape(w[n].shape), v_.reshape(w[n].shape)
    d_, m_, v_ = _adamw("adamw_small", _pack_small(w), small_sum, _pack_small(m), _pack_small(v))
    delta.update(_unpack_small(d_, w))
    new_m.update(_unpack_small(m_, w))
    new_v.update(_unpack_small(v_, w))

    loss = lax.psum(loss_local[0, 0], ("x", "y", "c"))
    return (loss, grad_x[None], *[grads[n] for n in WEIGHTS], *[delta[n] for n in WEIGHTS],
            *[new_m[n] for n in WEIGHTS], *[new_v[n] for n in WEIGHTS])
```

```python
import functools

import jax
import jax.numpy as jnp
import numpy as np
from jax import lax
from jax.experimental import pallas as pl
from jax.experimental.pallas import tpu as pltpu

F32 = jnp.float32
BF16 = jnp.bfloat16

S = 2048
D = 1024
CHUNK = 64
MLA_H, MLA_NOPE, MLA_ROPE, MLA_V = 8, 64, 32, 64
Q_LORA, KV_LORA = 384, 256
ROPE_THETA = 10000.0
SB_H, SB_DIM = 8, 64
C_H, C_DIM = 16, 64
LEFT_CHUNKS = 8
REL_CLIP = 256
D_FF = 2816
EVEN_IN = 2208
RMS_EPS = 1e-6
ADAM_LR, ADAM_B1, ADAM_B2, ADAM_EPS, ADAM_WD, ADAM_STEP = 0.001, 0.9, 0.999, 1e-08, 0.01, 10

N_CHIPS = 4
FF_SHARD = D_FF // N_CHIPS
SCALE_A = (MLA_NOPE + MLA_ROPE) ** -0.5
SCALE_B = SB_DIM ** -0.5
SCALE_C = C_DIM ** -0.5
NEG = -1e30

LANES = 128
VMEM_LIMIT_BYTES = 56 * 1024 * 1024
TM = 512
QB = 256
BQ = 256

P_CQ, P_CKV, P_QB, P_KB, P_VB, P_KR = 0, 512, 768, 1280, 1792, 2304
P_IN = 2432
KR_LANE = 64
BAND_W = BQ + LEFT_CHUNKS * CHUNK
BAND_PAD = 512
TOEP_W = 1024


def _params(*sem):
    return pltpu.CompilerParams(dimension_semantics=sem, vmem_limit_bytes=VMEM_LIMIT_BYTES)


MESH = pl.DeviceIdType.MESH
ANY = pl.BlockSpec(memory_space=pl.ANY)


def _position():
    x, y, c = lax.axis_index("x"), lax.axis_index("y"), lax.axis_index("c")
    other_chips = [(1 - x, y), (x, 1 - y), (1 - x, 1 - y)]
    return x, y, c, other_chips


def _half_rows(c, half):
    return pl.ds(pl.multiple_of(c * half, 16), half)


def _remote(ref_src, ref_dst, send, recv, k, device):
    return pltpu.make_async_remote_copy(src_ref=ref_src, dst_ref=ref_dst, send_sem=send.at[k], recv_sem=recv.at[k],
                                        device_id=device, device_id_type=MESH)


class _Carry:
    def __init__(self):
        self.operands, self.aliased, self.fresh = [], [], []
        self.n_sems = 0
        self.starts, self.finishes, self.on_done = [], [], []

    def operand(self, arr, aliased):
        for i, a in enumerate(self.operands):
            if a is arr:
                return i
        self.operands.append(arr)
        self.aliased.append(aliased)
        return len(self.operands) - 1

    def result(self, shape, dtype):
        self.fresh.append(jax.ShapeDtypeStruct(shape, dtype))
        return len(self.fresh) - 1

    def sems(self, k):
        base = self.n_sems
        self.n_sems += k
        return base

    def done(self, results):
        aliased, fresh = results
        for f in self.on_done:
            f(aliased, fresh)


def _carrier_call(body, *, name, grid, in_specs, out_specs, out_shape, args, sem, scratch_shapes=(), carry=None):
    in_specs, out_specs, out_shape, scratch = list(in_specs), list(out_specs), list(out_shape), list(scratch_shapes)
    if carry is None:
        res = pl.pallas_call(body, name=name, grid=grid, in_specs=in_specs, out_specs=out_specs, out_shape=out_shape,
                             scratch_shapes=scratch, compiler_params=_params(*sem))(*args)
        return list(res), None
    ops = carry.operands
    alias_idx = [i for i, a in enumerate(carry.aliased) if a]
    c_shapes = [jax.ShapeDtypeStruct(ops[i].shape, ops[i].dtype) for i in alias_idx] + carry.fresh
    n_in, n_out, n_scr = len(args), len(out_shape), len(scratch)

    def wrapped(*refs):
        ins, c_ins = refs[:n_in], refs[n_in:n_in + len(ops)]
        o0 = n_in + len(ops)
        outs, c_outs = refs[o0:o0 + n_out], refs[o0 + n_out:o0 + n_out + len(c_shapes)]
        s0 = o0 + n_out + len(c_shapes)
        scr, send, recv = refs[s0:s0 + n_scr], refs[s0 + n_scr], refs[s0 + n_scr + 1]
        use = list(c_ins)
        for k, i in enumerate(alias_idx):
            use[i] = c_outs[k]
        fresh = c_outs[len(alias_idx):]

        def run(steps):
            for step in steps:
                step(use, fresh, send, recv)

        if not grid:
            run(carry.starts)
            if body is not None:
                body(*ins, *outs, *scr)
            run(carry.finishes)
            return
        ids = [pl.program_id(a) for a in range(len(grid))]
        first = functools.reduce(jnp.logical_and, [i == 0 for i in ids])
        last = functools.reduce(jnp.logical_and, [i == g - 1 for i, g in zip(ids, grid)])

        @pl.when(first)
        def _():
            run(carry.starts)

        body(*ins, *outs, *scr)

        @pl.when(last)
        def _():
            run(carry.finishes)

    res = pl.pallas_call(
        wrapped, name=name, grid=grid, in_specs=in_specs + [ANY] * len(ops), out_specs=out_specs + [ANY] * len(c_shapes),
        out_shape=out_shape + c_shapes,
        scratch_shapes=scratch + [pltpu.SemaphoreType.DMA((carry.n_sems,)), pltpu.SemaphoreType.DMA((carry.n_sems,))],
        input_output_aliases={n_in + i: n_out + k for k, i in enumerate(alias_idx)},
        compiler_params=_params(*(("arbitrary",) * len(grid))),
    )(*args, *ops)
    res = list(res)
    c_res = res[n_out:]
    return res[:n_out], ({i: c_res[k] for k, i in enumerate(alias_idx)}, c_res[len(alias_idx):])


_DIMS = {"nn": (((1,), (0,)), ((), ())), "nt": (((1,), (1,)), ((), ())), "tn": (((0,), (0,)), ((), ()))}


def _dot(a, b, kind="nn"):
    return lax.dot_general(a, b, _DIMS[kind], preferred_element_type=F32)


def _iota(shape, dim):
    return lax.broadcasted_iota(jnp.int32, shape, dim)


def _sigmoid(x):
    return 1.0 / (1.0 + jnp.exp(-x))


def _softplus(x):
    return jnp.maximum(x, 0.0) + jnp.log(1.0 + jnp.exp(-jnp.abs(x)))


def _split_dot(x, tri):
    hi = x.astype(BF16)
    lo = (x - hi.astype(F32)).astype(BF16)
    return _dot(hi, tri) + _dot(lo, tri)


def _mm(name, a, b, *, kind, grid, a_spec, b_spec, o_spec, out_shape, out_dtype, acc_shape, resid=None, r_spec=None):
    nk = grid[-1]
    has_r = resid is not None

    def body(*refs):
        a_ref, b_ref = refs[0], refs[1]
        r_ref = refs[2] if has_r else None
        o_ref = refs[2 + has_r]
        part = _dot(a_ref[...].astype(BF16), b_ref[...].astype(BF16), kind)

        def finish(total):
            if has_r:
                total = total + r_ref[...].astype(F32)
            o_ref[...] = total.astype(out_dtype)

        if nk == 1:
            finish(part)
        else:
            acc_ref = refs[3 + has_r]
            k = pl.program_id(len(grid) - 1)

            @pl.when(k == 0)
            def _():
                acc_ref[...] = part

            @pl.when(k > 0)
            def _():
                acc_ref[...] += part

            @pl.when(k == nk - 1)
            def _():
                finish(acc_ref[...])

    in_specs = [a_spec, b_spec] + ([r_spec] if has_r else [])
    args = (a, b) + ((resid,) if has_r else ())
    sem = ("parallel",) * (len(grid) - 1) + ("arbitrary",)
    return pl.pallas_call(
        body, name=name, grid=grid, in_specs=in_specs, out_specs=o_spec,
        out_shape=jax.ShapeDtypeStruct(out_shape, out_dtype),
        scratch_shapes=[pltpu.VMEM(acc_shape, F32)] if nk > 1 else [],
        compiler_params=_params(*sem),
    )(*args)


def _rms_fwd(name, x, g, col_block=0):
    c = g.shape[1]

    def body(x_ref, g_ref, u_ref):
        xv = x_ref[...]
        r = lax.rsqrt(jnp.mean(xv * xv, axis=-1, keepdims=True) + RMS_EPS)
        u_ref[...] = (xv * r * g_ref[...]).astype(BF16)

    return pl.pallas_call(
        body, name=name, grid=(S // TM,),
        in_specs=[pl.BlockSpec((TM, c), lambda i: (i, col_block)), pl.BlockSpec((1, c), lambda i: (0, 0))],
        out_specs=pl.BlockSpec((TM, c), lambda i: (i, 0)),
        out_shape=jax.ShapeDtypeStruct((S, c), BF16),
        compiler_params=_params("parallel"),
    )(x, g)


def _rms_bwd(name, dy, x, g, resid):
    def body(dy_ref, x_ref, g_ref, r_ref, dx_ref, dg_ref):
        i = pl.program_id(0)
        xv = x_ref[...]
        r = lax.rsqrt(jnp.mean(xv * xv, axis=-1, keepdims=True) + RMS_EPS)
        xh = xv * r
        dyv = dy_ref[...]
        dxh = dyv * g_ref[...]
        dx_ref[...] = r_ref[...] + r * (dxh - xh * jnp.mean(dxh * xh, axis=-1, keepdims=True))
        part = jnp.sum(dyv * xh, axis=0, keepdims=True)

        @pl.when(i == 0)
        def _():
            dg_ref[...] = part

        @pl.when(i > 0)
        def _():
            dg_ref[...] += part

    row = pl.BlockSpec((TM, D), lambda i: (i, 0))
    vec = pl.BlockSpec((1, D), lambda i: (0, 0))
    return pl.pallas_call(
        body, name=name, grid=(S // TM,), in_specs=[row, row, vec, row], out_specs=[row, vec],
        out_shape=[jax.ShapeDtypeStruct((S, D), F32), jax.ShapeDtypeStruct((1, D), F32)],
        compiler_params=_params("arbitrary"),
    )(dy, x, g, resid)


def _loss_bwd(name, h, g, tgt):
    def body(h_ref, g_ref, t_ref, loss_ref, dh_ref, dg_ref):
        i = pl.program_id(0)
        xv = h_ref[...]
        gv = g_ref[...]
        r = lax.rsqrt(jnp.mean(xv * xv, axis=-1, keepdims=True) + RMS_EPS)
        xh = xv * r
        diff = xh * gv - t_ref[...]
        part_loss = 0.5 * jnp.sum(jnp.sum(diff * diff, axis=-1, keepdims=True) * (1.0 / D), axis=0, keepdims=True)
        dy = diff * (1.0 / D)
        dxh = dy * gv
        dh_ref[...] = r * (dxh - xh * jnp.mean(dxh * xh, axis=-1, keepdims=True))
        part_g = jnp.sum(dy * xh, axis=0, keepdims=True)

        @pl.when(i == 0)
        def _():
            dg_ref[...] = part_g
            loss_ref[...] = jnp.broadcast_to(part_loss, (1, LANES))

        @pl.when(i > 0)
        def _():
            dg_ref[...] += part_g
            loss_ref[...] += jnp.broadcast_to(part_loss, (1, LANES))

    row = pl.BlockSpec((TM, D), lambda i: (i, 0))
    vec = pl.BlockSpec((1, D), lambda i: (0, 0))
    return pl.pallas_call(
        body, name=name, grid=(S // TM,), in_specs=[row, vec, row],
        out_specs=[pl.BlockSpec((1, LANES), lambda i: (0, 0)), row, vec],
        out_shape=[jax.ShapeDtypeStruct((1, LANES), F32), jax.ShapeDtypeStruct((S, D), F32),
                   jax.ShapeDtypeStruct((1, D), F32)],
        compiler_params=_params("arbitrary"),
    )(h, g, tgt)


def _ffn_fwd(name, h, g, wg, wu, wd, carry=None):
    def body(h_ref, g_ref, wg_ref, wu_ref, wd_ref, o_ref, gate_ref, up_ref, u_scr):
        s = pl.program_id(1)

        @pl.when(s == 0)
        def _():
            xv = h_ref[...]
            r = lax.rsqrt(jnp.mean(xv * xv, axis=-1, keepdims=True) + RMS_EPS)
            u_scr[...] = (xv * r * g_ref[...]).astype(BF16)
            o_ref[...] = xv

        u = u_scr[...]
        gate = _dot(u, wg_ref[...], "nt")
        up = _dot(u, wu_ref[...], "nt")
        act = gate * _sigmoid(gate) * up
        o_ref[...] += _dot(act.astype(BF16), wd_ref[...])
        gate_ref[...] = gate.astype(BF16)
        up_ref[...] = up.astype(BF16)

    row = pl.BlockSpec((TM, D), lambda i, s: (i, 0))
    hid = pl.BlockSpec((None, TM, FF_SHARD), lambda i, s: (s, i, 0))
    return _carrier_call(
        body, name=name, grid=(S // TM, N_CHIPS),
        in_specs=[row, pl.BlockSpec((1, D), lambda i, s: (0, 0))]
        + [pl.BlockSpec((None, FF_SHARD, D), lambda i, s: (s, 0, 0))] * 3,
        out_specs=[row, hid, hid],
        out_shape=[jax.ShapeDtypeStruct((S, D), F32), jax.ShapeDtypeStruct((N_CHIPS, S, FF_SHARD), BF16),
                   jax.ShapeDtypeStruct((N_CHIPS, S, FF_SHARD), BF16)],
        scratch_shapes=[pltpu.VMEM((TM, D), BF16)], args=(h, g, wg, wu, wd), sem=("parallel", "arbitrary"), carry=carry)


def _ffn_bwd(name, dh, h, g, gate, up, wg, wu, wd):
    def body(dh_ref, h_ref, g_ref, gate_ref, up_ref, wg_ref, wu_ref, wd_ref,
             dhin_ref, dg_ref, u_ref, dgate_ref, dup_ref, act_ref, dhb_scr, du_scr):
        i = pl.program_id(0)
        s = pl.program_id(1)

        @pl.when(s == 0)
        def _():
            xv = h_ref[...]
            r = lax.rsqrt(jnp.mean(xv * xv, axis=-1, keepdims=True) + RMS_EPS)
            u_ref[...] = (xv * r * g_ref[...]).astype(BF16)
            dhb_scr[...] = dh_ref[...].astype(BF16)
            du_scr[...] = jnp.zeros_like(du_scr)

        dact = _dot(dhb_scr[...], wd_ref[...], "nt")
        gv = gate_ref[...].astype(F32)
        uv = up_ref[...].astype(F32)
        sig = _sigmoid(gv)
        sil = gv * sig
        dup = dact * sil
        dgate = dact * uv * (sig * (1.0 + gv * (1.0 - sig)))
        dgb = dgate.astype(BF16)
        dub = dup.astype(BF16)
        act_ref[...] = (sil * uv).astype(BF16)
        dgate_ref[...] = dgb
        dup_ref[...] = dub
        du_scr[...] += _dot(dgb, wg_ref[...]) + _dot(dub, wu_ref[...])

        @pl.when(s == N_CHIPS - 1)
        def _():
            xv = h_ref[...]
            r = lax.rsqrt(jnp.mean(xv * xv, axis=-1, keepdims=True) + RMS_EPS)
            xh = xv * r
            du = du_scr[...]
            dxh = du * g_ref[...]
            dhin_ref[...] = dh_ref[...] + r * (dxh - xh * jnp.mean(dxh * xh, axis=-1, keepdims=True))
            part = jnp.sum(du * xh, axis=0, keepdims=True)

            @pl.when(i == 0)
            def _():
                dg_ref[...] = part

            @pl.when(i > 0)
            def _():
                dg_ref[...] += part

    row = pl.BlockSpec((TM, D), lambda i, s: (i, 0))
    vec = pl.BlockSpec((1, D), lambda i, s: (0, 0))
    hid = pl.BlockSpec((None, TM, FF_SHARD), lambda i, s: (s, i, 0))
    hid_shape = jax.ShapeDtypeStruct((N_CHIPS, S, FF_SHARD), BF16)
    return pl.pallas_call(
        body, name=name, grid=(S // TM, N_CHIPS),
        in_specs=[row, row, vec, hid, hid] + [pl.BlockSpec((None, FF_SHARD, D), lambda i, s: (s, 0, 0))] * 3,
        out_specs=[row, vec, row, hid, hid, hid],
        out_shape=[jax.ShapeDtypeStruct((S, D), F32), jax.ShapeDtypeStruct((1, D), F32),
                   jax.ShapeDtypeStruct((S, D), BF16), hid_shape, hid_shape, hid_shape],
        scratch_shapes=[pltpu.VMEM((TM, D), BF16), pltpu.VMEM((TM, D), F32)],
        compiler_params=_params("arbitrary", "arbitrary"),
    )(dh, h, g, gate, up, wg, wu, wd)


def _ffn_wgrads(name, u, dgate, dup, act, dh):
    nk = S // TM

    def body(u_ref, dh_ref, dgate_ref, dup_ref, act_ref, dg_ref, du_ref, dd_ref, acc_g, acc_u, acc_d):
        k = pl.program_id(1)
        u = u_ref[...]
        parts = (_dot(dgate_ref[...], u, "tn"), _dot(dup_ref[...], u, "tn"),
                 _dot(act_ref[...], dh_ref[...].astype(BF16), "tn"))
        accs = (acc_g, acc_u, acc_d)

        @pl.when(k == 0)
        def _():
            for acc, part in zip(accs, parts):
                acc[...] = part

        @pl.when(k > 0)
        def _():
            for acc, part in zip(accs, parts):
                acc[...] += part

        @pl.when(k == nk - 1)
        def _():
            for out, acc in zip((dg_ref, du_ref, dd_ref), accs):
                out[...] = acc[...].astype(BF16)

    tok = pl.BlockSpec((TM, D), lambda s, k: (k, 0))
    hid = pl.BlockSpec((None, TM, FF_SHARD), lambda s, k: (s, k, 0))
    out = pl.BlockSpec((None, FF_SHARD, D), lambda s, k: (s, 0, 0))
    shape = jax.ShapeDtypeStruct((N_CHIPS, FF_SHARD, D), BF16)
    return pl.pallas_call(
        body, name=name, grid=(N_CHIPS, nk), in_specs=[tok, tok, hid, hid, hid], out_specs=[out, out, out],
        out_shape=[shape, shape, shape], scratch_shapes=[pltpu.VMEM((FF_SHARD, D), F32)] * 3,
        compiler_params=_params("parallel", "arbitrary"))(u, dh, dgate, dup, act)


def _rope_tables():
    pos = jnp.arange(S, dtype=F32)
    inv = ROPE_THETA ** (-jnp.arange(0, MLA_ROPE, 2, dtype=F32) / MLA_ROPE)
    ang = pos[:, None] * inv[None, :]
    half = MLA_ROPE // 2
    cos = jnp.cos(ang)
    sin = jnp.sin(ang)
    one = jnp.ones((S, KR_LANE), F32)
    zero = jnp.zeros((S, KR_LANE), F32)
    tail_one = jnp.ones((S, LANES - KR_LANE - MLA_ROPE), F32)
    tail_zero = jnp.zeros((S, LANES - KR_LANE - MLA_ROPE), F32)
    cos_t = jnp.concatenate([one, cos, cos, tail_one], axis=1)
    sin_t = jnp.concatenate([zero, -sin, sin, tail_zero], axis=1)
    assert cos_t.shape == (S, LANES) and half * 2 == MLA_ROPE
    return cos_t, sin_t


def _rope(x, cos_t, sin_t, sign):
    n = x.shape[1] // LANES
    half = MLA_ROPE // 2
    lane = _iota(x.shape, 1) & (LANES - 1)
    first = (lane >= KR_LANE) & (lane < KR_LANE + half)
    swapped = jnp.where(first, pltpu.roll(x, x.shape[1] - half, 1), pltpu.roll(x, half, 1))
    c = jnp.tile(cos_t, (1, n)) if n > 1 else cos_t
    s = jnp.tile(sin_t, (1, n)) if n > 1 else sin_t
    return x * c + swapped * (s * sign)


def _mla_prep_fwd(name, proj, g_cq, g_ckv, w_uq, w_uk, w_uv, cos_t, sin_t):
    nh = MLA_H * LANES

    def body(cq_ref, ckv_ref, kr_ref, gq_ref, gkv_ref, wq_ref, wk_ref, wv_ref, cos_ref, sin_ref,
             qa_ref, ka_ref, va_ref):
        cos_v, sin_v = cos_ref[...], sin_ref[...]
        cq = cq_ref[...]
        r = lax.rsqrt(jnp.mean(cq * cq, axis=-1, keepdims=True) + RMS_EPS)
        cqn = (cq * r * gq_ref[...]).astype(BF16)
        qa_ref[...] = _rope(_dot(cqn, wq_ref[...]), cos_v, sin_v, 1.0).astype(BF16)
        ckv = ckv_ref[...]
        r = lax.rsqrt(jnp.mean(ckv * ckv, axis=-1, keepdims=True) + RMS_EPS)
        ckvn = (ckv * r * gkv_ref[...]).astype(BF16)
        lane = _iota((TM, LANES), 1)
        rot = (lane >= KR_LANE) & (lane < KR_LANE + MLA_ROPE)
        kr = jnp.where(rot, _rope(kr_ref[...], cos_v, sin_v, 1.0), 0.0)
        ka_ref[...] = (_dot(ckvn, wk_ref[...]) + jnp.tile(kr, (1, MLA_H))).astype(BF16)
        va_ref[...] = _dot(ckvn, wv_ref[...]).astype(BF16)

    full = lambda shape: pl.BlockSpec(shape, lambda i: (0, 0))
    return pl.pallas_call(
        body, name=name, grid=(S // TM,),
        in_specs=[pl.BlockSpec((TM, Q_LORA), lambda i: (i, P_CQ // Q_LORA)),
                  pl.BlockSpec((TM, KV_LORA), lambda i: (i, P_CKV // KV_LORA)),
                  pl.BlockSpec((TM, LANES), lambda i: (i, P_KR // LANES)),
                  full((1, Q_LORA)), full((1, KV_LORA)), full((Q_LORA, nh)), full((KV_LORA, nh)),
                  full((KV_LORA, MLA_H * MLA_V)),
                  pl.BlockSpec((TM, LANES), lambda i: (i, 0)), pl.BlockSpec((TM, LANES), lambda i: (i, 0))],
        out_specs=[pl.BlockSpec((TM, nh), lambda i: (i, 0)), pl.BlockSpec((TM, nh), lambda i: (i, 0)),
                   pl.BlockSpec((TM, MLA_H * MLA_V), lambda i: (i, 0))],
        out_shape=[jax.ShapeDtypeStruct((S, nh), BF16), jax.ShapeDtypeStruct((S, nh), BF16),
                   jax.ShapeDtypeStruct((S, MLA_H * MLA_V), BF16)],
        compiler_params=_params("parallel"),
    )(proj, proj, proj, g_cq, g_ckv, w_uq, w_uk, w_uv, cos_t, sin_t)


def _mla_prep_bwd(name, dqa, dka, dva, proj, g_cq, g_ckv, w_uq, w_uk, w_uv, cos_t, sin_t):
    nh = MLA_H * LANES

    def body(dqa_ref, dka_ref, dva_ref, cq_ref, ckv_ref, gq_ref, gkv_ref, wq_ref, wk_ref, wv_ref, cos_ref, sin_ref,
             dcq_ref, dckv_ref, dkr_ref, dwq_ref, dwk_ref, dwv_ref, dgq_ref, dgkv_ref):
        i = pl.program_id(0)
        cos_v, sin_v = cos_ref[...], sin_ref[...]

        def norm_bwd(x, g, dn):
            r = lax.rsqrt(jnp.mean(x * x, axis=-1, keepdims=True) + RMS_EPS)
            xh = x * r
            dxh = dn * g
            dx = r * (dxh - xh * jnp.mean(dxh * xh, axis=-1, keepdims=True))
            return dx, jnp.sum(dn * xh, axis=0, keepdims=True), (xh * g).astype(BF16)

        dq = _rope(dqa_ref[...], cos_v, sin_v, -1.0).astype(BF16)
        dcqn = _dot(dq, wq_ref[...], "nt")
        dcq, dgq, cqn = norm_bwd(cq_ref[...], gq_ref[...], dcqn)
        dcq_ref[...] = dcq.astype(BF16)
        dwq = _dot(cqn, dq, "tn")

        dka = dka_ref[...]
        dkab = dka.astype(BF16)
        dvab = dva_ref[...].astype(BF16)
        dckvn = _dot(dkab, wk_ref[...], "nt") + _dot(dvab, wv_ref[...], "nt")
        dckv, dgkv, ckvn = norm_bwd(ckv_ref[...], gkv_ref[...], dckvn)
        dckv_ref[...] = dckv.astype(BF16)
        dwk = _dot(ckvn, dkab, "tn")
        dwv = _dot(ckvn, dvab, "tn")

        fold = dka[:, 0:LANES]
        for hh in range(1, MLA_H):
            fold = fold + dka[:, hh * LANES:(hh + 1) * LANES]
        lane = _iota((TM, LANES), 1)
        rot = (lane >= KR_LANE) & (lane < KR_LANE + MLA_ROPE)
        dkr = _rope(jnp.where(rot, fold, 0.0), cos_v, sin_v, -1.0)
        dkr_ref[...] = jnp.where(rot, dkr, 0.0).astype(BF16)

        @pl.when(i == 0)
        def _():
            dwq_ref[...] = dwq
            dwk_ref[...] = dwk
            dwv_ref[...] = dwv
            dgq_ref[...] = dgq
            dgkv_ref[...] = dgkv

        @pl.when(i > 0)
        def _():
            dwq_ref[...] += dwq
            dwk_ref[...] += dwk
            dwv_ref[...] += dwv
            dgq_ref[...] += dgq
            dgkv_ref[...] += dgkv

    full = lambda shape: pl.BlockSpec(shape, lambda i: (0, 0))
    rows = lambda c: pl.BlockSpec((TM, c), lambda i: (i, 0))
    nv = MLA_H * MLA_V
    return pl.pallas_call(
        body, name=name, grid=(S // TM,),
        in_specs=[rows(nh), rows(nh), rows(nv),
                  pl.BlockSpec((TM, Q_LORA), lambda i: (i, P_CQ // Q_LORA)),
                  pl.BlockSpec((TM, KV_LORA), lambda i: (i, P_CKV // KV_LORA)),
                  full((1, Q_LORA)), full((1, KV_LORA)), full((Q_LORA, nh)), full((KV_LORA, nh)), full((KV_LORA, nv)),
                  rows(LANES), rows(LANES)],
        out_specs=[rows(Q_LORA), rows(KV_LORA), rows(LANES), full((Q_LORA, nh)), full((KV_LORA, nh)),
                   full((KV_LORA, nv)), full((1, Q_LORA)), full((1, KV_LORA))],
        out_shape=[jax.ShapeDtypeStruct((S, Q_LORA), BF16), jax.ShapeDtypeStruct((S, KV_LORA), BF16),
                   jax.ShapeDtypeStruct((S, LANES), BF16), jax.ShapeDtypeStruct((Q_LORA, nh), F32),
                   jax.ShapeDtypeStruct((KV_LORA, nh), F32), jax.ShapeDtypeStruct((KV_LORA, nv), F32),
                   jax.ShapeDtypeStruct((1, Q_LORA), F32), jax.ShapeDtypeStruct((1, KV_LORA), F32)],
        compiler_params=_params("arbitrary"),
    )(dqa, dka, dva, proj, proj, g_cq, g_ckv, w_uq, w_uk, w_uv, cos_t, sin_t)


def _head_masks(dtype):
    lane = _iota((1, LANES), 1)
    return (lane < 64).astype(dtype), (lane >= 64).astype(dtype)


def _mla_fwd(name, qa, ka, va, carry=None):
    def body(q_ref, k_ref, v_ref, o_ref, lse_ref):
        m0b, m1b = _head_masks(BF16)
        lane = _iota((QB, LANES), 1)
        left = lane < 64

        def qblock(i, _):
            r0 = pl.multiple_of(i * QB, QB)
            qs = [q_ref[pl.ds(r0, QB), hh * LANES:(hh + 1) * LANES] for hh in range(2)]
            rowc = lax.shift_right_logical(r0 + _iota((QB, QB), 0), 6)

            def kv(kb, carry):
                ms, ls, acc = carry
                c0 = pl.multiple_of(kb * QB, QB)
                v = v_ref[pl.ds(c0, QB), :]
                ok = lax.shift_right_logical(c0 + _iota((QB, QB), 1), 6) <= rowc
                new_m, new_l, alphas = [], [], []
                pv = None
                for hh in range(2):
                    k = k_ref[pl.ds(c0, QB), hh * LANES:(hh + 1) * LANES]
                    s = jnp.where(ok, _dot(qs[hh], k, "nt") * SCALE_A, NEG)
                    mn = jnp.maximum(ms[hh], jnp.max(s, axis=-1, keepdims=True))
                    p = jnp.exp(s - mn)
                    a = jnp.exp(ms[hh] - mn)
                    new_m.append(mn)
                    new_l.append(a * ls[hh] + jnp.sum(p, axis=-1, keepdims=True))
                    alphas.append(a)
                    part = _dot(p.astype(BF16), v * (m0b if hh == 0 else m1b))
                    pv = part if pv is None else pv + part
                acc = acc * jnp.where(left, alphas[0], alphas[1]) + pv
                return tuple(new_m), tuple(new_l), acc

            init = ((jnp.full((QB, 1), NEG, F32),) * 2, (jnp.zeros((QB, 1), F32),) * 2, jnp.zeros((QB, LANES), F32))
            ms, ls, acc = lax.fori_loop(0, i + 1, kv, init)
            o_ref[pl.ds(r0, QB), :] = acc * jnp.where(left, 1.0 / ls[0], 1.0 / ls[1])
            lse_ref[pl.ds(r0, QB), :] = jnp.where(left, ms[0] + jnp.log(ls[0]), ms[1] + jnp.log(ls[1]))
            return 0

        lax.fori_loop(0, S // QB, qblock, 0)

    pair = lambda w: pl.BlockSpec((S, w), lambda p: (0, p))
    return _carrier_call(
        body, name=name, grid=(MLA_H // 2,), in_specs=[pair(2 * LANES), pair(2 * LANES), pair(LANES)],
        out_specs=[pair(LANES), pair(LANES)],
        out_shape=[jax.ShapeDtypeStruct((S, MLA_H * MLA_V), F32), jax.ShapeDtypeStruct((S, MLA_H * MLA_V), F32)],
        args=(qa, ka, va), sem=("parallel",), carry=carry)


def _mla_bwd(name, qa, ka, va, o, lse, do, do_block0, carry=None):
    def body(q_ref, k_ref, v_ref, o_ref, lse_ref, do_ref, dq_ref, dk_ref, dv_ref):
        m0f, m1f = _head_masks(F32)
        m0b, m1b = _head_masks(BF16)
        dk_ref[...] = jnp.zeros_like(dk_ref)
        dv_ref[...] = jnp.zeros_like(dv_ref)

        def qblock(i, _):
            r0 = pl.multiple_of(i * QB, QB)
            rows = pl.ds(r0, QB)
            do_f = do_ref[rows, :]
            prod = do_f * o_ref[rows, :]
            deltas = [jnp.sum(prod * m0f, axis=-1, keepdims=True), jnp.sum(prod * m1f, axis=-1, keepdims=True)]
            lse_v = lse_ref[rows, :]
            lses = [lse_v[:, 0:1], lse_v[:, 64:65]]
            dob = do_f.astype(BF16)
            dos = [dob * m0b, dob * m1b]
            qs = [q_ref[rows, hh * LANES:(hh + 1) * LANES] for hh in range(2)]
            rowc = lax.shift_right_logical(r0 + _iota((QB, QB), 0), 6)

            def kv(kb, dqs):
                c0 = pl.multiple_of(kb * QB, QB)
                cols = pl.ds(c0, QB)
                v = v_ref[cols, :]
                ok = lax.shift_right_logical(c0 + _iota((QB, QB), 1), 6) <= rowc
                out = []
                dv = None
                for hh in range(2):
                    k = k_ref[cols, hh * LANES:(hh + 1) * LANES]
                    s = _dot(qs[hh], k, "nt") * SCALE_A
                    p = jnp.where(ok, jnp.exp(s - lses[hh]), 0.0)
                    dp = _dot(dos[hh], v, "nt")
                    ds = (p * (dp - deltas[hh]) * SCALE_A).astype(BF16)
                    out.append(dqs[hh] + _dot(ds, k))
                    dk_ref[cols, hh * LANES:(hh + 1) * LANES] += _dot(ds, qs[hh], "tn")
                    part = _dot(p.astype(BF16), dos[hh], "tn")
                    dv = part if dv is None else dv + part
                dv_ref[cols, :] += dv
                return tuple(out)

            dqs = lax.fori_loop(0, i + 1, kv, (jnp.zeros((QB, LANES), F32),) * 2)
            for hh in range(2):
                dq_ref[rows, hh * LANES:(hh + 1) * LANES] = dqs[hh]
            return 0

        lax.fori_loop(0, S // QB, qblock, 0)

    pair = lambda w: pl.BlockSpec((S, w), lambda p: (0, p))
    return _carrier_call(
        body, name=name, grid=(MLA_H // 2,),
        in_specs=[pair(2 * LANES), pair(2 * LANES), pair(LANES), pair(LANES), pair(LANES),
                  pl.BlockSpec((S, LANES), lambda p: (0, do_block0 + p))],
        out_specs=[pair(2 * LANES), pair(2 * LANES), pair(LANES)],
        out_shape=[jax.ShapeDtypeStruct((S, MLA_H * LANES), F32), jax.ShapeDtypeStruct((S, MLA_H * LANES), F32),
                   jax.ShapeDtypeStruct((S, MLA_H * MLA_V), F32)],
        args=(qa, ka, va, o, lse, do), sem=("parallel",), carry=carry)


def _sb_weights(q_h, k, c, before, tri_suffix):
    z = _dot(q_h, k, "nt") * SCALE_B
    sp = _softplus(z)
    log_keep = jnp.where(before, -sp, 0.0)
    log_between = _split_dot(log_keep, tri_suffix) + c
    w = jnp.where(before, jnp.exp(z - sp + log_between), 0.0)
    return w, jnp.exp(z - sp), jnp.sum(log_keep, axis=-1, keepdims=True)


def _sb_fwd(name, proj, carry=None):
    def body(q_ref, k_ref, v_ref, o_ref):
        m0b, m1b = _head_masks(BF16)
        tri_suffix = (_iota((QB, QB), 0) > _iota((QB, QB), 1)).astype(BF16)

        def qblock(i, _):
            r0 = pl.multiple_of(i * QB, QB)
            q = q_ref[pl.ds(r0, QB), :].astype(BF16)
            qs = [q * m0b, q * m1b]
            rowg = r0 + _iota((QB, QB), 0)

            def kv(step, carry):
                cs, acc = carry
                c0 = pl.multiple_of((i - step) * QB, QB)
                k = k_ref[pl.ds(c0, QB), :].astype(BF16)
                v = v_ref[pl.ds(c0, QB), :].astype(BF16)
                before = (c0 + _iota((QB, QB), 1)) < rowg
                new_c = []
                for hh in range(2):
                    w, _, tot = _sb_weights(qs[hh], k, cs[hh], before, tri_suffix)
                    new_c.append(cs[hh] + tot)
                    acc = acc + _dot(w.astype(BF16), v * (m0b if hh == 0 else m1b))
                return tuple(new_c), acc

            init = ((jnp.zeros((QB, 1), F32),) * 2, jnp.zeros((QB, LANES), F32))
            _, acc = lax.fori_loop(0, i + 1, kv, init)
            o_ref[pl.ds(r0, QB), :] = acc.astype(BF16)
            return 0

        lax.fori_loop(0, S // QB, qblock, 0)

    col = lambda base: pl.BlockSpec((S, LANES), lambda p: (0, base // LANES + p))
    return _carrier_call(
        body, name=name, grid=(SB_H // 2,), in_specs=[col(P_QB), col(P_KB), col(P_VB)],
        out_specs=[pl.BlockSpec((S, LANES), lambda p: (0, p))],
        out_shape=[jax.ShapeDtypeStruct((S, SB_H * SB_DIM), BF16)],
        args=(proj, proj, proj), sem=("parallel",), carry=carry)


def _sb_bwd(name, proj, do, do_block0, carry=None):
    nb = S // QB

    def body(q_ref, k_ref, v_ref, do_ref, dq_ref, dk_ref, dv_ref, sig_scr, dl_scr, dk_acc, dv_acc):
        m0b, m1b = _head_masks(BF16)
        tri_suffix = (_iota((QB, QB), 0) > _iota((QB, QB), 1)).astype(BF16)
        tri_prefix = (_iota((QB, QB), 0) < _iota((QB, QB), 1)).astype(BF16)
        dk_acc[...] = jnp.zeros_like(dk_acc)
        dv_acc[...] = jnp.zeros_like(dv_acc)

        def qblock(i, _):
            r0 = pl.multiple_of(i * QB, QB)
            rows = pl.ds(r0, QB)
            q = q_ref[rows, :].astype(BF16)
            qs = [q * m0b, q * m1b]
            dob = do_ref[rows, :].astype(BF16)
            dos = [dob * m0b, dob * m1b]
            rowg = r0 + _iota((QB, QB), 0)

            def sweep_left(step, cs):
                kb = i - step
                c0 = pl.multiple_of(kb * QB, QB)
                cols = pl.ds(c0, QB)
                k = k_ref[cols, :].astype(BF16)
                v = v_ref[cols, :].astype(BF16)
                before = (c0 + _iota((QB, QB), 1)) < rowg
                new_c = []
                dv = None
                for hh in range(2):
                    w, sig, tot = _sb_weights(qs[hh], k, cs[hh], before, tri_suffix)
                    new_c.append(cs[hh] + tot)
                    sig_scr[hh, kb] = sig
                    dl_scr[hh, kb] = _dot(dos[hh], v, "nt") * w
                    part = _dot(w.astype(BF16), dos[hh], "tn")
                    dv = part if dv is None else dv + part
                dv_acc[cols, :] += dv
                return tuple(new_c)

            lax.fori_loop(0, i + 1, sweep_left, (jnp.zeros((QB, 1), F32),) * 2)

            def sweep_right(kb, carry):
                ps, dq = carry
                c0 = pl.multiple_of(kb * QB, QB)
                cols = pl.ds(c0, QB)
                k = k_ref[cols, :].astype(BF16)
                before = (c0 + _iota((QB, QB), 1)) < rowg
                new_p = []
                dk = None
                for hh in range(2):
                    dl = dl_scr[hh, kb]
                    sig = sig_scr[hh, kb]
                    earlier = _split_dot(dl, tri_prefix) + ps[hh]
                    new_p.append(ps[hh] + jnp.sum(dl, axis=-1, keepdims=True))
                    dz = (jnp.where(before, dl * (1.0 - sig) - earlier * sig, 0.0) * SCALE_B).astype(BF16)
                    dq = dq + _dot(dz, k * (m0b if hh == 0 else m1b))
                    part = _dot(dz, qs[hh], "tn")
                    dk = part if dk is None else dk + part
                dk_acc[cols, :] += dk
                return tuple(new_p), dq

            init = ((jnp.zeros((QB, 1), F32),) * 2, jnp.zeros((QB, LANES), F32))
            _, dq = lax.fori_loop(0, i + 1, sweep_right, init)
            dq_ref[rows, :] = dq.astype(BF16)
            return 0

        lax.fori_loop(0, nb, qblock, 0)
        dk_ref[...] = dk_acc[...].astype(BF16)
        dv_ref[...] = dv_acc[...].astype(BF16)

    col = lambda base: pl.BlockSpec((S, LANES), lambda p: (0, base // LANES + p))
    out = pl.BlockSpec((S, LANES), lambda p: (0, p))
    shape = jax.ShapeDtypeStruct((S, SB_H * SB_DIM), BF16)
    return _carrier_call(
        body, name=name, grid=(SB_H // 2,),
        in_specs=[col(P_QB), col(P_KB), col(P_VB), pl.BlockSpec((S, LANES), lambda p: (0, do_block0 + p))],
        out_specs=[out, out, out], out_shape=[shape, shape, shape],
        scratch_shapes=[pltpu.VMEM((2, nb, QB, QB), F32), pltpu.VMEM((2, nb, QB, QB), F32),
                        pltpu.VMEM((S, LANES), F32), pltpu.VMEM((S, LANES), F32)],
        args=(proj, proj, proj, do), sem=("parallel",), carry=carry)


def _band_row_index():
    j = np.arange(TOEP_W)
    rel = np.clip(LEFT_CHUNKS * CHUNK - j, -REL_CLIP, REL_CLIP) + REL_CLIP
    rel[BAND_W:] = 2 * REL_CLIP
    return rel.astype(np.int32)


def _band_tiles(r0_ref, q_ref, kpad, vpad, m, m0b, m1b, static_ok, bias):
    r0 = pl.multiple_of(m * BQ, BQ)
    q = q_ref[0, pl.ds(r0, BQ), :]
    kw = kpad[pl.ds(r0, BAND_W), :]
    vw = vpad[pl.ds(r0, BAND_W), :]
    ok = static_ok & ((r0 - BAND_PAD + _iota((BQ, BAND_W), 1)) >= 0)
    qs = [q * m0b, q * m1b]
    ps = []
    for hh in range(2):
        s = jnp.where(ok, _dot(qs[hh], kw, "nt") * SCALE_C + bias[hh], NEG)
        e = jnp.exp(s - jnp.max(s, axis=-1, keepdims=True))
        ps.append(e * (1.0 / jnp.sum(e, axis=-1, keepdims=True)))
    return r0, qs, kw, vw, ps


def _band_setup(qkv_ref, r0_ref, kpad, vpad):
    kpad[0:BAND_PAD, :] = jnp.zeros((BAND_PAD, LANES), BF16)
    vpad[0:BAND_PAD, :] = jnp.zeros((BAND_PAD, LANES), BF16)
    kpad[BAND_PAD:, :] = qkv_ref[1]
    vpad[BAND_PAD:, :] = qkv_ref[2]
    jc = lax.shift_right_logical(_iota((BQ, BAND_W), 1), 6)
    rc = lax.shift_right_logical(_iota((BQ, BAND_W), 0), 6)
    static_ok = (jc >= rc) & (jc <= rc + LEFT_CHUNKS)
    bias = []
    for hh in range(2):
        row = jnp.broadcast_to(r0_ref[hh:hh + 1, :], (BQ, TOEP_W))
        bias.append(pltpu.roll(row, 0, 1, stride=1, stride_axis=0)[:, :BAND_W])
    return static_ok, bias


def _band_fwd(name, qkv, r0):
    def body(qkv_ref, r0_ref, o_ref, kpad, vpad):
        m0b, m1b = _head_masks(BF16)
        static_ok, bias = _band_setup(qkv_ref, r0_ref, kpad, vpad)

        def qblock(m, _):
            r0_, _, _, vw, ps = _band_tiles(r0_ref, qkv_ref, kpad, vpad, m, m0b, m1b, static_ok, bias)
            o = _dot(ps[0].astype(BF16), vw * m0b) + _dot(ps[1].astype(BF16), vw * m1b)
            o_ref[pl.ds(r0_, BQ), :] = o.astype(BF16)
            return 0

        lax.fori_loop(0, S // BQ, qblock, 0)

    return pl.pallas_call(
        body, name=name, grid=(C_H // 2,),
        in_specs=[pl.BlockSpec((3, S, LANES), lambda p: (0, 0, p)), pl.BlockSpec((None, 2, TOEP_W), lambda p: (p, 0, 0))],
        out_specs=pl.BlockSpec((S, LANES), lambda p: (0, p)),
        out_shape=jax.ShapeDtypeStruct((S, C_H * C_DIM), BF16),
        scratch_shapes=[pltpu.VMEM((S + BAND_PAD, LANES), BF16), pltpu.VMEM((S + BAND_PAD, LANES), BF16)],
        compiler_params=_params("parallel"),
    )(qkv, r0)


def _band_bwd(name, qkv, r0, do, carry=None):
    def body(qkv_ref, r0_ref, do_ref, dqkv_ref, dr0_ref, kpad, vpad, dkpad, dvpad, db_acc):
        m0b, m1b = _head_masks(BF16)
        static_ok, bias = _band_setup(qkv_ref, r0_ref, kpad, vpad)
        dkpad[...] = jnp.zeros_like(dkpad)
        dvpad[...] = jnp.zeros_like(dvpad)
        db_acc[...] = jnp.zeros_like(db_acc)

        def qblock(m, _):
            r0_, qs, kw, vw, ps = _band_tiles(r0_ref, qkv_ref, kpad, vpad, m, m0b, m1b, static_ok, bias)
            dob = do_ref[pl.ds(r0_, BQ), :].astype(BF16)
            dos = [dob * m0b, dob * m1b]
            dq = None
            dk = None
            dv = None
            for hh in range(2):
                p = ps[hh]
                dp = _dot(dos[hh], vw, "nt")
                ds = p * (dp - jnp.sum(dp * p, axis=-1, keepdims=True))
                db_acc[hh, :, 0:BAND_W] += ds
                dsb = (ds * SCALE_C).astype(BF16)
                t = _dot(dsb, kw * (m0b if hh == 0 else m1b))
                dq = t if dq is None else dq + t
                t = _dot(dsb, qs[hh], "tn")
                dk = t if dk is None else dk + t
                t = _dot(p.astype(BF16), dos[hh], "tn")
                dv = t if dv is None else dv + t
            dqkv_ref[0, pl.ds(r0_, BQ), :] = dq.astype(BF16)
            dkpad[pl.ds(r0_, BAND_W), :] += dk
            dvpad[pl.ds(r0_, BAND_W), :] += dv
            return 0

        lax.fori_loop(0, S // BQ, qblock, 0)
        dqkv_ref[1] = dkpad[BAND_PAD:, :].astype(BF16)
        dqkv_ref[2] = dvpad[BAND_PAD:, :].astype(BF16)
        sub = _iota((8, TOEP_W), 0)
        for hh in range(2):
            folded = db_acc[hh, 0:8, :]
            for a in range(1, BQ // 8):
                folded = folded + pltpu.roll(db_acc[hh, 8 * a:8 * a + 8, :], TOEP_W - 8 * a, 1)
            for bit in range(3):
                moved = pltpu.roll(folded, TOEP_W - (1 << bit), 1)
                folded = jnp.where((sub & (1 << bit)) != 0, moved, folded)
            dr0_ref[hh:hh + 1, :] = jnp.sum(folded, axis=0, keepdims=True)

    return _carrier_call(
        body, name=name, grid=(C_H // 2,),
        in_specs=[pl.BlockSpec((3, S, LANES), lambda p: (0, 0, p)), pl.BlockSpec((None, 2, TOEP_W), lambda p: (p, 0, 0)),
                  pl.BlockSpec((S, LANES), lambda p: (0, p))],
        out_specs=[pl.BlockSpec((3, S, LANES), lambda p: (0, 0, p)), pl.BlockSpec((None, 2, TOEP_W), lambda p: (p, 0, 0))],
        out_shape=[jax.ShapeDtypeStruct((3, S, C_H * C_DIM), BF16), jax.ShapeDtypeStruct((C_H // 2, 2, TOEP_W), F32)],
        scratch_shapes=[pltpu.VMEM((S + BAND_PAD, LANES), BF16), pltpu.VMEM((S + BAND_PAD, LANES), BF16),
                        pltpu.VMEM((S + BAND_PAD, LANES), F32), pltpu.VMEM((S + BAND_PAD, LANES), F32),
                        pltpu.VMEM((2, BQ, TOEP_W), F32)],
        args=(qkv, r0, do), sem=("parallel",), carry=carry)


def _bias_table_grad(name, dr0):
    w_out = 5 * LANES

    def body(d_ref, o_ref):
        j = _iota((TOEP_W, w_out), 0)
        rel = jnp.clip(LEFT_CHUNKS * CHUNK - j, -REL_CLIP, REL_CLIP) + REL_CLIP
        rel = jnp.where(j >= BAND_W, 2 * REL_CLIP, rel)
        onehot = (rel == _iota((TOEP_W, w_out), 1)).astype(BF16)
        d = d_ref[...]
        hi = d.astype(BF16)
        mid = (d - hi.astype(F32))
        mid_b = mid.astype(BF16)
        lo = (mid - mid_b.astype(F32)).astype(BF16)
        o_ref[...] = _dot(hi, onehot) + _dot(mid_b, onehot) + _dot(lo, onehot)

    return pl.pallas_call(
        body, name=name, out_shape=jax.ShapeDtypeStruct((C_H, w_out), F32),
        in_specs=[pl.BlockSpec((C_H, TOEP_W), lambda: (0, 0))], out_specs=pl.BlockSpec((C_H, w_out), lambda: (0, 0)),
        grid=(),
    )(dr0)


def _carry_gather(cy, slots, names, ici, d2d):
    idx = [cy.operand(slots[n], True) for n in names]
    n = len(names)
    base_i = cy.sems(3 * n) if ici else 0
    base_d = cy.sems(3 * n) if d2d else 0

    def piece(refs, t, slot, cc):
        return refs[idx[t]].at[slot, _half_rows(cc, slots[names[t]].shape[1] // 2), :]

    def over_ici(refs, send, recv, arriving):
        x, y, c, chips = _position()
        out = []
        for t in range(n):
            for j in range(3):
                r = piece(refs, t, 2 * chips[j][0] + chips[j][1] if arriving else 2 * x + y, c)
                out.append(_remote(r, r, send, recv, base_i + 3 * t + j, (*chips[j], c)))
        return out

    def over_d2d(refs, send, recv, arriving):
        x, y, c, chips = _position()
        out = []
        for t in range(n):
            for j in range(3):
                r = piece(refs, t, 2 * chips[j][0] + chips[j][1], 1 - c if arriving else c)
                out.append(_remote(r, r, send, recv, base_d + 3 * t + j, (x, y, 1 - c)))
        return out

    def start_ici(refs, fresh, send, recv):
        for cp in over_ici(refs, send, recv, False):
            cp.start()

    def wait_ici(refs, fresh, send, recv):
        for cp in over_ici(refs, send, recv, True):
            cp.wait_recv()
        for cp in over_ici(refs, send, recv, False):
            cp.wait_send()

    def start_d2d(refs, fresh, send, recv):
        for cp in over_d2d(refs, send, recv, False):
            cp.start()

    def wait_d2d(refs, fresh, send, recv):
        for cp in over_d2d(refs, send, recv, True):
            cp.wait_recv()
        for cp in over_d2d(refs, send, recv, False):
            cp.wait_send()

    if ici and d2d:
        cy.starts.append(start_ici)
        cy.finishes += [wait_ici, start_d2d, wait_d2d]
    elif ici:
        cy.starts.append(start_ici)
        cy.finishes.append(wait_ici)
    else:
        cy.starts.append(start_d2d)
        cy.finishes.append(wait_d2d)

    def done(aliased, fresh):
        for t, name in enumerate(names):
            slots[name] = aliased[idx[t]]

    cy.on_done.append(done)


def _carry_chip_exchange(cy, sums, got, names):
    idx = [cy.operand(sums[n], False) for n in names]
    out = [cy.result((3,) + sums[n].shape[1:], BF16) for n in names]
    base = cy.sems(3 * len(names))

    def copies(refs, fresh, send, recv):
        x, y, c, chips = _position()
        return [_remote(refs[idx[t]].at[2 * chips[j][0] + chips[j][1]], fresh[out[t]].at[j], send, recv, base + 3 * t + j,
                        (*chips[j], c)) for t in range(len(names)) for j in range(3)]

    def start(refs, fresh, send, recv):
        for cp in copies(refs, fresh, send, recv):
            cp.start()

    def wait(refs, fresh, send, recv):
        for cp in copies(refs, fresh, send, recv):
            cp.wait()

    cy.starts.append(start)
    cy.finishes.append(wait)

    def done(aliased, fresh):
        for t, name in enumerate(names):
            got[name] = fresh[out[t]]

    cy.on_done.append(done)


def _run_carry(name, cy):
    _, res = _carrier_call(None, name=name, grid=(), in_specs=[], out_specs=[], out_shape=[], args=(), sem=(), carry=cy)
    cy.done(res)


FIRST_WEIGHTS = ("ev_w_in", "ev_w_uq", "ev_w_ukv")
LAYER0_WEIGHTS = ("ev_w_out", "w_gate0", "w_up0", "w_down0")
LAYER1_WEIGHTS = ("od_w_qkv", "od_w_out", "w_gate1", "w_up1", "w_down1")
GRAD_GROUPS = {"ffn1": ("w_gate1", "w_up1", "w_down1"), "od": ("od_w_qkv", "od_w_out"),
               "ffn0": ("w_gate0", "w_up0", "w_down0"), "ev": ("ev_w_in", "ev_w_uq", "ev_w_ukv", "ev_w_out")}


class _Exchanges:
    def __init__(self, slots, pos):
        self.slots, self.pos = dict(slots), pos
        self.parts, self.sums, self.got = {}, {}, {}

    def begin(self):
        cy = _Carry()
        _carry_gather(cy, self.slots, FIRST_WEIGHTS, True, True)
        _run_carry("gather_first", cy)

    def weights(self, *names):
        return [self.slots[n] for n in names]

    def carry(self, stage):
        cy = _Carry()
        if stage == "mla_attn":
            _carry_gather(cy, self.slots, LAYER0_WEIGHTS, True, False)
        elif stage == "sb_attn":
            _carry_gather(cy, self.slots, LAYER0_WEIGHTS, False, True)
            _carry_gather(cy, self.slots, LAYER1_WEIGHTS, True, False)
        elif stage == "ffn0":
            _carry_gather(cy, self.slots, LAYER1_WEIGHTS, False, True)
        elif stage == "band_attn_bwd":
            _carry_chip_exchange(cy, self.sums, self.got, GRAD_GROUPS["ffn1"])
        elif stage == "mla_attn_bwd":
            _carry_chip_exchange(cy, self.sums, self.got, GRAD_GROUPS["od"])
        elif stage == "sb_attn_bwd":
            _carry_chip_exchange(cy, self.sums, self.got, GRAD_GROUPS["ffn0"])
        else:
            raise ValueError(stage)
        return cy

    def grads(self, group, parts):
        names = GRAD_GROUPS[group]
        self.parts.update(parts)
        theirs = _pair_exchange("grads_pair_" + group, [parts[n] for n in names])
        for n, t in zip(names, theirs):
            self.sums[n] = _pair_sum("pair_sum_" + n, parts[n], t, self.pos)

    def finish(self, shapes):
        cy = _Carry()
        _carry_chip_exchange(cy, self.sums, self.got, GRAD_GROUPS["ev"])
        _run_carry("grads_chips_ev", cy)
        fulls = {}
        for part_name, n, layer in GRAD_PARTS:
            fulls[n] = _chip_sum("chip_sum_" + part_name, self.sums[part_name], self.got[part_name], self.pos, layer,
                                 shapes[n], fulls.get(n))
        return dict(zip(BIG, _sibling_exchange("grads_sibling", [fulls[n] for n in BIG])))


class _NoExchanges:
    def __init__(self, slots):
        self.slots, self.parts = dict(slots), {}

    def begin(self):
        pass

    def weights(self, *names):
        return [self.slots[n] for n in names]

    def carry(self, stage):
        return None

    def grads(self, group, parts):
        self.parts.update(parts)


def _first_weights(w_in_s, w_uq_s, w_ukv_s):
    gw = {"ev_w_in": w_in_s, "ev_w_uq": w_uq_s, "ev_w_ukv": w_ukv_s}
    w_in = jnp.moveaxis(gw["ev_w_in"], 0, 1).reshape(D, EVEN_IN)
    z = lambda n: jnp.zeros((D, n), BF16)
    w_in_p = jnp.concatenate(
        [w_in[:, 0:384], z(128), w_in[:, 384:640], w_in[:, 672:2208], z(KR_LANE), w_in[:, 640:672],
         z(LANES - KR_LANE - MLA_ROPE)], axis=1)
    w_uq = jnp.moveaxis(gw["ev_w_uq"], 0, 1).reshape(Q_LORA, MLA_H, MLA_NOPE + MLA_ROPE)
    w_uq_p = jnp.concatenate([w_uq, jnp.zeros((Q_LORA, MLA_H, LANES - MLA_NOPE - MLA_ROPE), BF16)], axis=2)
    w_ukv = jnp.moveaxis(gw["ev_w_ukv"], 0, 1).reshape(KV_LORA, MLA_H, MLA_NOPE + MLA_V)
    w_uk_p = jnp.concatenate([w_ukv[:, :, :MLA_NOPE], jnp.zeros((KV_LORA, MLA_H, LANES - MLA_NOPE), BF16)], axis=2)
    return dict(
        w_in=w_in_p, w_uq=w_uq_p.reshape(Q_LORA, MLA_H * LANES), w_uk=w_uk_p.reshape(KV_LORA, MLA_H * LANES),
        w_uv=w_ukv[:, :, MLA_NOPE:].reshape(KV_LORA, MLA_H * MLA_V))


def _proj_mm(name, u, w_in):
    return _mm(name, u, w_in, kind="nn", grid=(S // TM, 1, 1),
               a_spec=pl.BlockSpec((TM, D), lambda i, j, k: (i, 0)), b_spec=pl.BlockSpec((D, P_IN), lambda i, j, k: (0, 0)),
               o_spec=pl.BlockSpec((TM, P_IN), lambda i, j, k: (i, 0)), out_shape=(S, P_IN), out_dtype=F32, acc_shape=None)


def _out_proj(name, o, w, resid):
    return _mm(name, o, w, kind="nn", grid=(S // TM, 1, 1),
               a_spec=pl.BlockSpec((TM, D), lambda i, j, k: (i, 0)), b_spec=pl.BlockSpec((D, D), lambda i, j, k: (0, 0)),
               o_spec=pl.BlockSpec((TM, D), lambda i, j, k: (i, 0)), out_shape=(S, D), out_dtype=F32, acc_shape=None,
               resid=resid, r_spec=pl.BlockSpec((TM, D), lambda i, j, k: (i, 0)))


def _out_proj_bwd(name, dh, o, w):
    d_o = _mm(name + "_x", dh, w, kind="nt", grid=(S // TM, 1, 1),
              a_spec=pl.BlockSpec((TM, D), lambda i, j, k: (i, 0)), b_spec=pl.BlockSpec((D, D), lambda i, j, k: (0, 0)),
              o_spec=pl.BlockSpec((TM, D), lambda i, j, k: (i, 0)), out_shape=(S, D), out_dtype=F32, acc_shape=None)
    d_w = _mm(name + "_w", o, dh, kind="tn", grid=(2, S // TM),
              a_spec=pl.BlockSpec((TM, TM), lambda j, k: (k, j)), b_spec=pl.BlockSpec((TM, D), lambda j, k: (k, 0)),
              o_spec=pl.BlockSpec((TM, D), lambda j, k: (j, 0)), out_shape=(D, D), out_dtype=BF16, acc_shape=(TM, D))
    return d_o, d_w


def _local_step(x, tgt, sm, ex):
    def riding(stage, fn, *args):
        cy = ex.carry(stage)
        res, copies = fn(stage, *args, carry=cy)
        if cy is not None:
            cy.done(copies)
        return res

    cos_t, sin_t = _rope_tables()
    g_mix, g_ffn = sm["g_mix"], sm["g_ffn"]
    r0 = sm["od_rel_bias"][0][:, _band_row_index()].reshape(C_H // 2, 2, TOEP_W)
    nt = 3

    ex.begin()
    w = _first_weights(*ex.weights(*FIRST_WEIGHTS))
    u0 = _rms_fwd("rms_mix0", x, g_mix[0:1])
    proj = _proj_mm("proj_in", u0, w["w_in"])
    qa, ka, va = _mla_prep_fwd("mla_prep", proj, sm["ev_g_cq"], sm["ev_g_ckv"], w["w_uq"], w["w_uk"], w["w_uv"], cos_t, sin_t)
    o_a, lse = riding("mla_attn", _mla_fwd, qa, ka, va)
    o_b, = riding("sb_attn", _sb_fwd, proj)
    o_ev = jnp.concatenate([o_a.astype(BF16), o_b], axis=1)
    w["ev_w_out"], w["w_gate0"], w["w_up0"], w["w_down0"] = ex.weights(*LAYER0_WEIGHTS)
    w["ev_w_out"] = w["ev_w_out"].reshape(D, D)
    h1 = _out_proj("ev_out", o_ev, w["ev_w_out"], x)
    h2, gate0, up0 = riding("ffn0", _ffn_fwd, h1, g_ffn[0:1], w["w_gate0"], w["w_up0"], w["w_down0"])
    w["w_qkv"], w["od_w_out"], w["w_gate1"], w["w_up1"], w["w_down1"] = ex.weights(*LAYER1_WEIGHTS)
    w["od_w_out"] = w["od_w_out"].reshape(D, D)
    w["w_qkv"] = jnp.moveaxis(w["w_qkv"], 0, 1).reshape(D, nt * D)
    u2 = _rms_fwd("rms_mix1", h2, g_mix[1:2])
    qkv = _mm("qkv", u2, w["w_qkv"], kind="nn", grid=(S // TM, nt, 1),
              a_spec=pl.BlockSpec((TM, D), lambda i, t, k: (i, 0)), b_spec=pl.BlockSpec((D, D), lambda i, t, k: (0, t)),
              o_spec=pl.BlockSpec((None, TM, D), lambda i, t, k: (t, i, 0)),
              out_shape=(nt, S, D), out_dtype=BF16, acc_shape=None)
    o_od = _band_fwd("band_attn", qkv, r0)
    h3 = _out_proj("od_out", o_od, w["od_w_out"], h2)
    (h4, gate1, up1), _ = _ffn_fwd("ffn1", h3, g_ffn[1:2], w["w_gate1"], w["w_up1"], w["w_down1"])

    loss, dh4, dg_final = _loss_bwd("loss", h4, sm["g_final"].reshape(1, D), tgt)

    dh3, dg_ffn1, u3, dgate, dup, act = _ffn_bwd("ffn1_bwd", dh4, h3, g_ffn[1:2], gate1, up1,
                                                 w["w_gate1"], w["w_up1"], w["w_down1"])
    d_wg1, d_wu1, d_wd1 = _ffn_wgrads("ffn1_dw", u3, dgate, dup, act, dh4)
    ex.grads("ffn1", {"w_gate1": d_wg1, "w_up1": d_wu1, "w_down1": d_wd1})

    d_ood, d_w_od_out = _out_proj_bwd("od_out_bwd", dh3, o_od, w["od_w_out"])
    dqkv, dr0 = riding("band_attn_bwd", _band_bwd, qkv, r0, d_ood)
    du2 = _mm("qkv_bwd_x", dqkv, w["w_qkv"], kind="nt", grid=(S // TM, nt),
              a_spec=pl.BlockSpec((None, TM, D), lambda i, t: (t, i, 0)), b_spec=pl.BlockSpec((D, D), lambda i, t: (0, t)),
              o_spec=pl.BlockSpec((TM, D), lambda i, t: (i, 0)), out_shape=(S, D), out_dtype=F32, acc_shape=(TM, D))
    d_w_qkv = _mm("qkv_bwd_w", u2, dqkv, kind="tn", grid=(nt, S // TM),
                  a_spec=pl.BlockSpec((TM, D), lambda t, k: (k, 0)), b_spec=pl.BlockSpec((None, TM, D), lambda t, k: (t, k, 0)),
                  o_spec=pl.BlockSpec((D, D), lambda t, k: (0, t)), out_shape=(D, nt * D), out_dtype=BF16, acc_shape=(D, D))
    shard_cols = lambda a: jnp.moveaxis(a.reshape(a.shape[0], N_CHIPS, a.shape[1] // N_CHIPS), 1, 0)
    ex.grads("od", {"od_w_qkv": shard_cols(d_w_qkv), "od_w_out": d_w_od_out.reshape(N_CHIPS, D // N_CHIPS, D)})
    dh2, dg_mix1 = _rms_bwd("rms_mix1_bwd", du2, h2, g_mix[1:2], dh3)
    d_rel = _bias_table_grad("rel_bias_grad", dr0.reshape(C_H, TOEP_W))[:, :2 * REL_CLIP + 1]

    dh1, dg_ffn0, u1, dgate, dup, act = _ffn_bwd("ffn0_bwd", dh2, h1, g_ffn[0:1], gate0, up0,
                                                 w["w_gate0"], w["w_up0"], w["w_down0"])
    d_wg0, d_wu0, d_wd0 = _ffn_wgrads("ffn0_dw", u1, dgate, dup, act, dh2)
    ex.grads("ffn0", {"w_gate0": d_wg0, "w_up0": d_wu0, "w_down0": d_wd0})

    d_oev, d_w_ev_out = _out_proj_bwd("ev_out_bwd", dh1, o_ev, w["ev_w_out"])
    dqa, dka, dva = riding("mla_attn_bwd", _mla_bwd, qa, ka, va, o_a, lse, d_oev, 0)
    dqb, dkb, dvb = riding("sb_attn_bwd", _sb_bwd, proj, d_oev, MLA_H * MLA_V // LANES)
    dcq, dckv, dkr, d_w_uq, d_w_uk, d_w_uv, dg_cq, dg_ckv = _mla_prep_bwd(
        "mla_prep_bwd", dqa, dka, dva, proj, sm["ev_g_cq"], sm["ev_g_ckv"], w["w_uq"], w["w_uk"], w["w_uv"], cos_t, sin_t)
    dproj = jnp.concatenate([dcq, jnp.zeros((S, LANES), BF16), dckv, dqb, dkb, dvb, dkr], axis=1)
    du0 = _mm("proj_in_bwd_x", dproj, w["w_in"], kind="nt", grid=(S // TM, 1, 1),
              a_spec=pl.BlockSpec((TM, P_IN), lambda i, j, k: (i, 0)), b_spec=pl.BlockSpec((D, P_IN), lambda i, j, k: (0, 0)),
              o_spec=pl.BlockSpec((TM, D), lambda i, j, k: (i, 0)), out_shape=(S, D), out_dtype=F32, acc_shape=None)
    d_w_in_p = _mm("proj_in_bwd_w", u0, dproj, kind="tn", grid=(1, S // TM),
                   a_spec=pl.BlockSpec((TM, D), lambda j, k: (k, 0)), b_spec=pl.BlockSpec((TM, P_IN), lambda j, k: (k, 0)),
                   o_spec=pl.BlockSpec((D, P_IN), lambda j, k: (0, 0)), out_shape=(D, P_IN), out_dtype=BF16,
                   acc_shape=(D, P_IN))
    grad_x, dg_mix0 = _rms_bwd("rms_mix0_bwd", du0, x, g_mix[0:1], dh1)

    d_w_in = jnp.concatenate([d_w_in_p[:, 0:384], d_w_in_p[:, 512:768],
                              d_w_in_p[:, P_KR + KR_LANE:P_KR + KR_LANE + MLA_ROPE], d_w_in_p[:, 768:2304]], axis=1)
    d_w_uq_std = d_w_uq.reshape(Q_LORA, MLA_H, LANES)[:, :, :MLA_NOPE + MLA_ROPE].reshape(Q_LORA, -1)
    d_w_ukv = jnp.concatenate([d_w_uk.reshape(KV_LORA, MLA_H, LANES)[:, :, :MLA_NOPE],
                               d_w_uv.reshape(KV_LORA, MLA_H, MLA_V)], axis=2).reshape(KV_LORA, -1)
    ex.grads("ev", {"ev_w_in": shard_cols(d_w_in), "ev_w_uq": shard_cols(d_w_uq_std.astype(BF16)),
                    "ev_w_ukv": shard_cols(d_w_ukv.astype(BF16)),
                    "ev_w_out": d_w_ev_out.reshape(N_CHIPS, D // N_CHIPS, D)})
    small = {
        "ev_g_cq": dg_cq, "ev_g_ckv": dg_ckv, "od_rel_bias": d_rel.reshape(1, C_H, 2 * REL_CLIP + 1),
        "g_mix": jnp.concatenate([dg_mix0, dg_mix1], axis=0), "g_ffn": jnp.concatenate([dg_ffn0, dg_ffn1], axis=0),
        "g_final": dg_final.reshape(D),
    }
    return loss, grad_x, small


BIG = ("ev_w_in", "ev_w_uq", "ev_w_ukv", "ev_w_out", "od_w_qkv", "od_w_out", "w_gate", "w_up", "w_down")
SMALL = ("ev_g_cq", "ev_g_ckv", "od_rel_bias", "g_mix", "g_ffn", "g_final")
WEIGHTS = ("ev_w_in", "ev_g_cq", "ev_w_uq", "ev_g_ckv", "ev_w_ukv", "ev_w_out", "od_w_qkv", "od_rel_bias", "od_w_out",
           "g_mix", "g_ffn", "w_gate", "w_up", "w_down", "g_final")
GRAD_PARTS = (("ev_w_in", "ev_w_in", 0), ("ev_w_uq", "ev_w_uq", 0), ("ev_w_ukv", "ev_w_ukv", 0),
              ("ev_w_out", "ev_w_out", 0), ("od_w_qkv", "od_w_qkv", 0), ("od_w_out", "od_w_out", 0),
              ("w_gate0", "w_gate", 0), ("w_gate1", "w_gate", 1), ("w_up0", "w_up", 0), ("w_up1", "w_up", 1),
              ("w_down0", "w_down", 0), ("w_down1", "w_down", 1))
SMALL_ROWS = 112
TRANSPOSED = ("w_gate", "w_up")


def _row_tile(rows, cap=512):
    for t in range(min(rows, cap), 0, -1):
        if rows % t == 0 and t % 16 == 0:
            return t
    return rows


def _cast_into_slot(name, w, layer, pos):
    _, rows, cols = w.shape
    tr = _row_tile(rows)

    def body(pos_ref, w_ref, o_ref):
        o_ref[...] = w_ref[...].astype(BF16)

    return pl.pallas_call(
        body, name=name,
        grid_spec=pltpu.PrefetchScalarGridSpec(
            num_scalar_prefetch=1, grid=(rows // tr,),
            in_specs=[pl.BlockSpec((None, tr, cols), lambda i, p: (layer, i, 0))],
            out_specs=pl.BlockSpec((None, tr, cols), lambda i, p: (p[0], i, 0))),
        out_shape=jax.ShapeDtypeStruct((N_CHIPS, rows, cols), BF16), compiler_params=_params("arbitrary"))(pos, w)


def _pair_exchange(name, parts):
    n = len(parts)

    def body(*refs):
        f, theirs = refs[:n], refs[n:2 * n]
        send_sem, recv_sem = refs[2 * n:]
        x, y, c, _ = _position()
        out = [pltpu.make_async_remote_copy(
            src_ref=f[t].at[:, _half_rows(1 - c, parts[t].shape[1] // 2), :], dst_ref=theirs[t], send_sem=send_sem.at[t],
            recv_sem=recv_sem.at[t], device_id=(x, y, 1 - c), device_id_type=MESH) for t in range(n)]
        for cp in out:
            cp.start()
        for cp in out:
            cp.wait()

    return pl.pallas_call(
        body, name=name, in_specs=[ANY] * n, out_specs=[ANY] * n,
        out_shape=[jax.ShapeDtypeStruct((N_CHIPS, p.shape[1] // 2, p.shape[2]), BF16) for p in parts],
        scratch_shapes=[pltpu.SemaphoreType.DMA((n,)), pltpu.SemaphoreType.DMA((n,))],
    )(*parts)


def _pair_sum(name, part, theirs, pos):
    _, half, cols = theirs.shape
    tr = _row_tile(half)
    nb = half // tr

    def body(pos_ref, a_ref, b_ref, o_ref):
        o_ref[...] = (a_ref[...].astype(F32) + b_ref[...].astype(F32)).astype(BF16)

    return pl.pallas_call(
        body, name=name,
        grid_spec=pltpu.PrefetchScalarGridSpec(
            num_scalar_prefetch=1, grid=(N_CHIPS, nb),
            in_specs=[pl.BlockSpec((None, tr, cols), lambda s, i, p: (s, p[1] * nb + i, 0)),
                      pl.BlockSpec((None, tr, cols), lambda s, i, p: (s, i, 0))],
            out_specs=pl.BlockSpec((None, tr, cols), lambda s, i, p: (s, i, 0))),
        out_shape=jax.ShapeDtypeStruct(theirs.shape, BF16),
        compiler_params=_params("arbitrary", "arbitrary"))(pos, part, theirs)


def _chip_sum(name, sums, got, pos, layer, full_shape, full=None):
    _, half, cols = sums.shape
    tr = _row_tile(half)
    nb = half // tr

    def body(pos_ref, s_ref, g_ref, *rest):
        out_ref = rest[-1]
        out_ref[...] = ((s_ref[...].astype(F32) + g_ref[0].astype(F32)) + g_ref[1].astype(F32)) + g_ref[2].astype(F32)

    in_specs = [pl.BlockSpec((None, tr, cols), lambda i, p: (p[0], i, 0)),
                pl.BlockSpec((3, tr, cols), lambda i, p: (0, i, 0))]
    args = [pos, sums, got]
    if full is not None:
        in_specs.append(ANY)
        args.append(full)
    return pl.pallas_call(
        body, name=name,
        grid_spec=pltpu.PrefetchScalarGridSpec(
            num_scalar_prefetch=1, grid=(nb,), in_specs=in_specs,
            out_specs=pl.BlockSpec((None, tr, cols), lambda i, p: (layer, p[1] * nb + i, 0))),
        out_shape=jax.ShapeDtypeStruct(full_shape, F32),
        input_output_aliases={3: 0} if full is not None else {},
        compiler_params=_params("arbitrary"))(*args)


def _sibling_exchange(name, fulls):
    n = len(fulls)

    def body(*refs):
        g = refs[n:2 * n]
        send_sem, recv_sem = refs[2 * n:]
        x, y, c, _ = _position()

        def half(t, cc):
            return g[t].at[:, _half_rows(cc, fulls[t].shape[1] // 2), :]

        out = [pltpu.make_async_remote_copy(
            src_ref=half(t, c), dst_ref=half(t, c), send_sem=send_sem.at[t], recv_sem=recv_sem.at[t],
            device_id=(x, y, 1 - c), device_id_type=MESH) for t in range(n)]
        for cp in out:
            cp.start()
        for t in range(n):
            out[t].wait_send()
            pltpu.make_async_remote_copy(
                src_ref=half(t, 1 - c), dst_ref=half(t, 1 - c), send_sem=send_sem.at[t], recv_sem=recv_sem.at[t],
                device_id=(x, y, 1 - c), device_id_type=MESH).wait_recv()

    return pl.pallas_call(
        body, name=name, in_specs=[ANY] * n, out_specs=[ANY] * n,
        out_shape=[jax.ShapeDtypeStruct(f.shape, F32) for f in fulls],
        input_output_aliases={t: t for t in range(n)},
        scratch_shapes=[pltpu.SemaphoreType.DMA((n,)), pltpu.SemaphoreType.DMA((n,))],
    )(*fulls)


def _all_reduce_small(name, packed):
    n_dev = 8

    def body(p_ref, o_ref, slots, send_sem, recv_sem):
        x, y, c, _ = _position()
        me = 4 * x + 2 * y + c

        def peer(k):
            return (1 - x if k & 4 else x, 1 - y if k & 2 else y, 1 - c if k & 1 else c)

        def logical(k):
            px, py, pc = peer(k)
            return 4 * px + 2 * py + pc

        slots[me] = p_ref[...]
        sends = [pltpu.make_async_remote_copy(
            src_ref=p_ref, dst_ref=slots.at[me], send_sem=send_sem.at[k], recv_sem=recv_sem.at[k],
            device_id=peer(k), device_id_type=MESH) for k in range(1, n_dev)]
        for cp in sends:
            cp.start()
        for k in range(1, n_dev):
            pltpu.make_async_remote_copy(
                src_ref=p_ref, dst_ref=slots.at[logical(k)], send_sem=send_sem.at[k], recv_sem=recv_sem.at[k],
                device_id=peer(k), device_id_type=MESH).wait_recv()
        for cp in sends:
            cp.wait_send()
        total = slots[0]
        for d in range(1, n_dev):
            total = total + slots[d]
        o_ref[...] = total

    vm = pl.BlockSpec(memory_space=pltpu.VMEM)
    return pl.pallas_call(
        body, name=name, in_specs=[vm], out_specs=vm, out_shape=jax.ShapeDtypeStruct(packed.shape, F32),
        scratch_shapes=[pltpu.VMEM((n_dev,) + packed.shape, F32), pltpu.SemaphoreType.DMA((n_dev,)),
                        pltpu.SemaphoreType.DMA((n_dev,))],
    )(packed)


def _adamw(name, w, g, m, v):
    rows, cols = w.shape
    tr = _row_tile(rows)

    def body(w_ref, g_ref, m_ref, v_ref, d_ref, mo_ref, vo_ref):
        gv = g_ref[...]
        m_new = ADAM_B1 * m_ref[...] + (1.0 - ADAM_B1) * gv
        v_new = ADAM_B2 * v_ref[...] + (1.0 - ADAM_B2) * (gv * gv)
        m_hat = m_new / (1.0 - ADAM_B1 ** ADAM_STEP)
        v_hat = v_new / (1.0 - ADAM_B2 ** ADAM_STEP)
        d_ref[...] = -ADAM_LR * (m_hat / (jnp.sqrt(v_hat) + ADAM_EPS) + ADAM_WD * w_ref[...])
        mo_ref[...] = m_new
        vo_ref[...] = v_new

    spec = pl.BlockSpec((tr, cols), lambda i: (i, 0))
    shape = jax.ShapeDtypeStruct((rows, cols), F32)
    return pl.pallas_call(body, name=name, grid=(rows // tr,), in_specs=[spec] * 4, out_specs=[spec] * 3,
                          out_shape=[shape] * 3, compiler_params=_params("parallel"))(w, g, m, v)


def _pack_small(tree):
    flat = jnp.concatenate([tree[n].reshape(-1).astype(F32) for n in SMALL])
    return jnp.pad(flat, (0, SMALL_ROWS * LANES - flat.shape[0])).reshape(SMALL_ROWS, LANES)


def _unpack_small(packed, like):
    flat = packed.reshape(-1)
    out, off = {}, 0
    for n in SMALL:
        size = int(np.prod(like[n].shape))
        out[n] = flat[off:off + size].reshape(like[n].shape)
        off += size
    return out


def kernel(x, ev_w_in, ev_g_cq, ev_w_uq, ev_g_ckv, ev_w_ukv, ev_w_out, od_w_qkv, od_rel_bias, od_w_out, g_mix, g_ffn, w_gate, w_up, w_down, g_final, loss_target, m_ev_w_in, m_ev_g_cq, m_ev_w_uq, m_ev_g_ckv, m_ev_w_ukv, m_ev_w_out, m_od_w_qkv, m_od_rel_bias, m_od_w_out, m_g_mix, m_g_ffn, m_w_gate, m_w_up, m_w_down, m_g_final, v_ev_w_in, v_ev_g_cq, v_ev_w_uq, v_ev_g_ckv, v_ev_w_ukv, v_ev_w_out, v_od_w_qkv, v_od_rel_bias, v_od_w_out, v_g_mix, v_g_ffn, v_w_gate, v_w_up, v_w_down, v_g_final):
    w = dict(ev_w_in=ev_w_in, ev_g_cq=ev_g_cq, ev_w_uq=ev_w_uq, ev_g_ckv=ev_g_ckv, ev_w_ukv=ev_w_ukv, ev_w_out=ev_w_out,
             od_w_qkv=od_w_qkv, od_rel_bias=od_rel_bias, od_w_out=od_w_out, g_mix=g_mix, g_ffn=g_ffn, w_gate=w_gate,
             w_up=w_up, w_down=w_down, g_final=g_final)
    m = dict(ev_w_in=m_ev_w_in, ev_g_cq=m_ev_g_cq, ev_w_uq=m_ev_w_uq, ev_g_ckv=m_ev_g_ckv, ev_w_ukv=m_ev_w_ukv,
             ev_w_out=m_ev_w_out, od_w_qkv=m_od_w_qkv, od_rel_bias=m_od_rel_bias, od_w_out=m_od_w_out, g_mix=m_g_mix,
             g_ffn=m_g_ffn, w_gate=m_w_gate, w_up=m_w_up, w_down=m_w_down, g_final=m_g_final)
    v = dict(ev_w_in=v_ev_w_in, ev_g_cq=v_ev_g_cq, ev_w_uq=v_ev_w_uq, ev_g_ckv=v_ev_g_ckv, ev_w_ukv=v_ev_w_ukv,
             ev_w_out=v_ev_w_out, od_w_qkv=v_od_w_qkv, od_rel_bias=v_od_rel_bias, od_w_out=v_od_w_out, g_mix=v_g_mix,
             g_ffn=v_g_ffn, w_gate=v_w_gate, w_up=v_w_up, w_down=v_w_down, g_final=v_g_final)
    flat2d = lambda a: a.reshape(-1, a.shape[-1])
    for tree in (w, m, v):
        for n in TRANSPOSED:
            tree[n] = jnp.swapaxes(tree[n], 1, 2)

    pos = jnp.stack([2 * lax.axis_index("x") + lax.axis_index("y"), lax.axis_index("c")]).astype(jnp.int32)

    slots = {part: _cast_into_slot("cast_" + part, w[n], layer, pos) for part, n, layer in GRAD_PARTS}
    ex = _Exchanges(slots, pos)

    loss_local, grad_x, small = _local_step(x[0], loss_target[0], {n: w[n] for n in SMALL}, ex)

    grads = ex.finish({n: w[n].shape for n in BIG})
    small_sum = _all_reduce_small("small_sum", _pack_small(small))
    grads.update(_unpack_small(small_sum, w))

    delta, new_m, new_v = {}, {}, {}
    for n in BIG:
        d_, m_, v_ = _adamw("adamw_" + n, flat2d(w[n]), flat2d(grads[n]), flat2d(m[n]), flat2d(v[n]))
        delta[n], new_m[n], new_v[n] = d_.reshape(w[n].shape), m_.reshape(w[n].shape), v_.reshape(w[n].shape)
    d_, m_, v_ = _adamw("adamw_small", _pack_small(w), small_sum, _pack_small(m), _pack_small(v))
    delta.update(_unpack_small(d_, w))
    new_m.update(_unpack_small(m_, w))
    new_v.update(_unpack_small(v_, w))
    for tree in (grads, delta, new_m, new_v):
        for n in TRANSPOSED:
            tree[n] = jnp.swapaxes(tree[n], 1, 2)

    loss = lax.psum(loss_local[0, 0], ("x", "y", "c"))
    return (loss, grad_x[None], *[grads[n] for n in WEIGHTS], *[delta[n] for n in WEIGHTS],
            *[new_m[n] for n in WEIGHTS], *[new_v[n] for n in WEIGHTS])
```

```python
import functools

import jax
import jax.numpy as jnp
import numpy as np
from jax import lax
from jax.experimental import pallas as pl
from jax.experimental.pallas import tpu as pltpu

F32 = jnp.float32
BF16 = jnp.bfloat16

S = 2048
D = 1024
CHUNK = 64
MLA_H, MLA_NOPE, MLA_ROPE, MLA_V = 8, 64, 32, 64
Q_LORA, KV_LORA = 384, 256
ROPE_THETA = 10000.0
SB_H, SB_DIM = 8, 64
C_H, C_DIM = 16, 64
LEFT_CHUNKS = 8
REL_CLIP = 256
D_FF = 2816
EVEN_IN = 2208
RMS_EPS = 1e-6
ADAM_LR, ADAM_B1, ADAM_B2, ADAM_EPS, ADAM_WD, ADAM_STEP = 0.001, 0.9, 0.999, 1e-08, 0.01, 10

N_CHIPS = 4
FF_SHARD = D_FF // N_CHIPS
SCALE_A = (MLA_NOPE + MLA_ROPE) ** -0.5
SCALE_B = SB_DIM ** -0.5
SCALE_C = C_DIM ** -0.5
NEG = -1e30

LANES = 128
VMEM_LIMIT_BYTES = 56 * 1024 * 1024
TM = 512
QB = 256
BQ = 256
SB_PAIRS = 1

P_CQ, P_CKV, P_QB, P_KB, P_VB, P_KR = 0, 512, 768, 1280, 1792, 2304
P_IN = 2432
KR_LANE = 64
BAND_W = BQ + LEFT_CHUNKS * CHUNK
BAND_PAD = 512
TOEP_W = 1024


def _params(*sem):
    return pltpu.CompilerParams(dimension_semantics=sem, vmem_limit_bytes=VMEM_LIMIT_BYTES)


MESH = pl.DeviceIdType.MESH
ANY = pl.BlockSpec(memory_space=pl.ANY)


def _position():
    x, y, c = lax.axis_index("x"), lax.axis_index("y"), lax.axis_index("c")
    other_chips = [(1 - x, y), (x, 1 - y), (1 - x, 1 - y)]
    return x, y, c, other_chips


def _half_rows(c, half):
    return pl.ds(pl.multiple_of(c * half, 16), half)


def _remote(ref_src, ref_dst, send, recv, k, device):
    return pltpu.make_async_remote_copy(src_ref=ref_src, dst_ref=ref_dst, send_sem=send.at[k], recv_sem=recv.at[k],
                                        device_id=device, device_id_type=MESH)


class _Carry:
    def __init__(self):
        self.operands, self.aliased, self.fresh = [], [], []
        self.n_sems = 0
        self.starts, self.finishes, self.on_done = [], [], []

    def operand(self, arr, aliased):
        for i, a in enumerate(self.operands):
            if a is arr:
                return i
        self.operands.append(arr)
        self.aliased.append(aliased)
        return len(self.operands) - 1

    def result(self, shape, dtype):
        self.fresh.append(jax.ShapeDtypeStruct(shape, dtype))
        return len(self.fresh) - 1

    def sems(self, k):
        base = self.n_sems
        self.n_sems += k
        return base

    def done(self, results):
        aliased, fresh = results
        for f in self.on_done:
            f(aliased, fresh)


def _carrier_call(body, *, name, grid, in_specs, out_specs, out_shape, args, sem, scratch_shapes=(), carry=None):
    in_specs, out_specs, out_shape, scratch = list(in_specs), list(out_specs), list(out_shape), list(scratch_shapes)
    if carry is None:
        res = pl.pallas_call(body, name=name, grid=grid, in_specs=in_specs, out_specs=out_specs, out_shape=out_shape,
                             scratch_shapes=scratch, compiler_params=_params(*sem))(*args)
        return list(res), None
    ops = carry.operands
    alias_idx = [i for i, a in enumerate(carry.aliased) if a]
    c_shapes = [jax.ShapeDtypeStruct(ops[i].shape, ops[i].dtype) for i in alias_idx] + carry.fresh
    n_in, n_out, n_scr = len(args), len(out_shape), len(scratch)

    def wrapped(*refs):
        ins, c_ins = refs[:n_in], refs[n_in:n_in + len(ops)]
        o0 = n_in + len(ops)
        outs, c_outs = refs[o0:o0 + n_out], refs[o0 + n_out:o0 + n_out + len(c_shapes)]
        s0 = o0 + n_out + len(c_shapes)
        scr, send, recv = refs[s0:s0 + n_scr], refs[s0 + n_scr], refs[s0 + n_scr + 1]
        use = list(c_ins)
        for k, i in enumerate(alias_idx):
            use[i] = c_outs[k]
        fresh = c_outs[len(alias_idx):]

        def run(steps):
            for step in steps:
                step(use, fresh, send, recv)

        if not grid:
            run(carry.starts)
            if body is not None:
                body(*ins, *outs, *scr)
            run(carry.finishes)
            return
        ids = [pl.program_id(a) for a in range(len(grid))]
        first = functools.reduce(jnp.logical_and, [i == 0 for i in ids])
        last = functools.reduce(jnp.logical_and, [i == g - 1 for i, g in zip(ids, grid)])

        @pl.when(first)
        def _():
            run(carry.starts)

        body(*ins, *outs, *scr)

        @pl.when(last)
        def _():
            run(carry.finishes)

    res = pl.pallas_call(
        wrapped, name=name, grid=grid, in_specs=in_specs + [ANY] * len(ops), out_specs=out_specs + [ANY] * len(c_shapes),
        out_shape=out_shape + c_shapes,
        scratch_shapes=scratch + [pltpu.SemaphoreType.DMA((carry.n_sems,)), pltpu.SemaphoreType.DMA((carry.n_sems,))],
        input_output_aliases={n_in + i: n_out + k for k, i in enumerate(alias_idx)},
        compiler_params=_params(*(("arbitrary",) * len(grid))),
    )(*args, *ops)
    res = list(res)
    c_res = res[n_out:]
    return res[:n_out], ({i: c_res[k] for k, i in enumerate(alias_idx)}, c_res[len(alias_idx):])


_DIMS = {"nn": (((1,), (0,)), ((), ())), "nt": (((1,), (1,)), ((), ())), "tn": (((0,), (0,)), ((), ()))}


def _dot(a, b, kind="nn"):
    return lax.dot_general(a, b, _DIMS[kind], preferred_element_type=F32)


def _iota(shape, dim):
    return lax.broadcasted_iota(jnp.int32, shape, dim)


def _sigmoid(x):
    return 1.0 / (1.0 + jnp.exp(-x))


def _softplus(x):
    return jnp.maximum(x, 0.0) + jnp.log(1.0 + jnp.exp(-jnp.abs(x)))


def _split_dot(x, tri):
    hi = x.astype(BF16)
    lo = (x - hi.astype(F32)).astype(BF16)
    return _dot(hi, tri) + _dot(lo, tri)


def _mm(name, a, b, *, kind, grid, a_spec, b_spec, o_spec, out_shape, out_dtype, acc_shape, resid=None, r_spec=None,
        carry=None):
    nk = grid[-1]
    has_r = resid is not None

    def body(*refs):
        a_ref, b_ref = refs[0], refs[1]
        r_ref = refs[2] if has_r else None
        o_ref = refs[2 + has_r]
        part = _dot(a_ref[...].astype(BF16), b_ref[...].astype(BF16), kind)

        def finish(total):
            if has_r:
                total = total + r_ref[...].astype(F32)
            o_ref[...] = total.astype(out_dtype)

        if nk == 1:
            finish(part)
        else:
            acc_ref = refs[3 + has_r]
            k = pl.program_id(len(grid) - 1)

            @pl.when(k == 0)
            def _():
                acc_ref[...] = part

            @pl.when(k > 0)
            def _():
                acc_ref[...] += part

            @pl.when(k == nk - 1)
            def _():
                finish(acc_ref[...])

    in_specs = [a_spec, b_spec] + ([r_spec] if has_r else [])
    args = (a, b) + ((resid,) if has_r else ())
    sem = ("parallel",) * (len(grid) - 1) + ("arbitrary",)
    res, copies = _carrier_call(
        body, name=name, grid=grid, in_specs=in_specs, out_specs=[o_spec],
        out_shape=[jax.ShapeDtypeStruct(out_shape, out_dtype)],
        scratch_shapes=[pltpu.VMEM(acc_shape, F32)] if nk > 1 else [], args=args, sem=sem, carry=carry)
    if carry is not None:
        carry.done(copies)
    return res[0]


def _rms_fwd(name, x, g, col_block=0):
    c = g.shape[1]

    def body(x_ref, g_ref, u_ref):
        xv = x_ref[...]
        r = lax.rsqrt(jnp.mean(xv * xv, axis=-1, keepdims=True) + RMS_EPS)
        u_ref[...] = (xv * r * g_ref[...]).astype(BF16)

    return pl.pallas_call(
        body, name=name, grid=(S // TM,),
        in_specs=[pl.BlockSpec((TM, c), lambda i: (i, col_block)), pl.BlockSpec((1, c), lambda i: (0, 0))],
        out_specs=pl.BlockSpec((TM, c), lambda i: (i, 0)),
        out_shape=jax.ShapeDtypeStruct((S, c), BF16),
        compiler_params=_params("parallel"),
    )(x, g)


def _rms_bwd(name, dy, x, g, resid, carry=None):
    def body(dy_ref, x_ref, g_ref, r_ref, dx_ref, dg_ref):
        i = pl.program_id(0)
        xv = x_ref[...]
        r = lax.rsqrt(jnp.mean(xv * xv, axis=-1, keepdims=True) + RMS_EPS)
        xh = xv * r
        dyv = dy_ref[...]
        dxh = dyv * g_ref[...]
        dx_ref[...] = r_ref[...] + r * (dxh - xh * jnp.mean(dxh * xh, axis=-1, keepdims=True))
        part = jnp.sum(dyv * xh, axis=0, keepdims=True)

        @pl.when(i == 0)
        def _():
            dg_ref[...] = part

        @pl.when(i > 0)
        def _():
            dg_ref[...] += part

    row = pl.BlockSpec((TM, D), lambda i: (i, 0))
    vec = pl.BlockSpec((1, D), lambda i: (0, 0))
    res, copies = _carrier_call(
        body, name=name, grid=(S // TM,), in_specs=[row, row, vec, row], out_specs=[row, vec],
        out_shape=[jax.ShapeDtypeStruct((S, D), F32), jax.ShapeDtypeStruct((1, D), F32)],
        args=(dy, x, g, resid), sem=("arbitrary",), carry=carry)
    if carry is not None:
        carry.done(copies)
    return res


def _loss_bwd(name, h, g, tgt):
    def body(h_ref, g_ref, t_ref, loss_ref, dh_ref, dg_ref):
        i = pl.program_id(0)
        xv = h_ref[...]
        gv = g_ref[...]
        r = lax.rsqrt(jnp.mean(xv * xv, axis=-1, keepdims=True) + RMS_EPS)
        xh = xv * r
        diff = xh * gv - t_ref[...]
        part_loss = 0.5 * jnp.sum(jnp.sum(diff * diff, axis=-1, keepdims=True) * (1.0 / D), axis=0, keepdims=True)
        dy = diff * (1.0 / D)
        dxh = dy * gv
        dh_ref[...] = r * (dxh - xh * jnp.mean(dxh * xh, axis=-1, keepdims=True))
        part_g = jnp.sum(dy * xh, axis=0, keepdims=True)

        @pl.when(i == 0)
        def _():
            dg_ref[...] = part_g
            loss_ref[...] = jnp.broadcast_to(part_loss, (1, LANES))

        @pl.when(i > 0)
        def _():
            dg_ref[...] += part_g
            loss_ref[...] += jnp.broadcast_to(part_loss, (1, LANES))

    row = pl.BlockSpec((TM, D), lambda i: (i, 0))
    vec = pl.BlockSpec((1, D), lambda i: (0, 0))
    return pl.pallas_call(
        body, name=name, grid=(S // TM,), in_specs=[row, vec, row],
        out_specs=[pl.BlockSpec((1, LANES), lambda i: (0, 0)), row, vec],
        out_shape=[jax.ShapeDtypeStruct((1, LANES), F32), jax.ShapeDtypeStruct((S, D), F32),
                   jax.ShapeDtypeStruct((1, D), F32)],
        compiler_params=_params("arbitrary"),
    )(h, g, tgt)


def _ffn_fwd(name, h, g, wg, wu, wd, carry=None):
    def body(h_ref, g_ref, wg_ref, wu_ref, wd_ref, o_ref, gate_ref, up_ref, u_scr):
        s = pl.program_id(1)

        @pl.when(s == 0)
        def _():
            xv = h_ref[...]
            r = lax.rsqrt(jnp.mean(xv * xv, axis=-1, keepdims=True) + RMS_EPS)
            u_scr[...] = (xv * r * g_ref[...]).astype(BF16)
            o_ref[...] = xv

        u = u_scr[...]
        gate = _dot(u, wg_ref[...], "nt")
        up = _dot(u, wu_ref[...], "nt")
        act = gate * _sigmoid(gate) * up
        o_ref[...] += _dot(act.astype(BF16), wd_ref[...])
        gate_ref[...] = gate.astype(BF16)
        up_ref[...] = up.astype(BF16)

    row = pl.BlockSpec((TM, D), lambda i, s: (i, 0))
    hid = pl.BlockSpec((None, TM, FF_SHARD), lambda i, s: (s, i, 0))
    return _carrier_call(
        body, name=name, grid=(S // TM, N_CHIPS),
        in_specs=[row, pl.BlockSpec((1, D), lambda i, s: (0, 0))]
        + [pl.BlockSpec((None, FF_SHARD, D), lambda i, s: (s, 0, 0))] * 3,
        out_specs=[row, hid, hid],
        out_shape=[jax.ShapeDtypeStruct((S, D), F32), jax.ShapeDtypeStruct((N_CHIPS, S, FF_SHARD), BF16),
                   jax.ShapeDtypeStruct((N_CHIPS, S, FF_SHARD), BF16)],
        scratch_shapes=[pltpu.VMEM((TM, D), BF16)], args=(h, g, wg, wu, wd), sem=("parallel", "arbitrary"), carry=carry)


def _ffn_bwd(name, dh, h, g, gate, up, wg, wu, wd):
    def body(dh_ref, h_ref, g_ref, gate_ref, up_ref, wg_ref, wu_ref, wd_ref,
             dhin_ref, dg_ref, u_ref, dgate_ref, dup_ref, act_ref, dhb_scr, du_scr):
        i = pl.program_id(0)
        s = pl.program_id(1)

        @pl.when(s == 0)
        def _():
            xv = h_ref[...]
            r = lax.rsqrt(jnp.mean(xv * xv, axis=-1, keepdims=True) + RMS_EPS)
            u_ref[...] = (xv * r * g_ref[...]).astype(BF16)
            dhb_scr[...] = dh_ref[...].astype(BF16)
            du_scr[...] = jnp.zeros_like(du_scr)

        dact = _dot(dhb_scr[...], wd_ref[...], "nt")
        gv = gate_ref[...].astype(F32)
        uv = up_ref[...].astype(F32)
        sig = _sigmoid(gv)
        sil = gv * sig
        dup = dact * sil
        dgate = dact * uv * (sig * (1.0 + gv * (1.0 - sig)))
        dgb = dgate.astype(BF16)
        dub = dup.astype(BF16)
        act_ref[...] = (sil * uv).astype(BF16)
        dgate_ref[...] = dgb
        dup_ref[...] = dub
        du_scr[...] += _dot(dgb, wg_ref[...]) + _dot(dub, wu_ref[...])

        @pl.when(s == N_CHIPS - 1)
        def _():
            xv = h_ref[...]
            r = lax.rsqrt(jnp.mean(xv * xv, axis=-1, keepdims=True) + RMS_EPS)
            xh = xv * r
            du = du_scr[...]
            dxh = du * g_ref[...]
            dhin_ref[...] = dh_ref[...] + r * (dxh - xh * jnp.mean(dxh * xh, axis=-1, keepdims=True))
            part = jnp.sum(du * xh, axis=0, keepdims=True)

            @pl.when(i == 0)
            def _():
                dg_ref[...] = part

            @pl.when(i > 0)
            def _():
                dg_ref[...] += part

    row = pl.BlockSpec((TM, D), lambda i, s: (i, 0))
    vec = pl.BlockSpec((1, D), lambda i, s: (0, 0))
    hid = pl.BlockSpec((None, TM, FF_SHARD), lambda i, s: (s, i, 0))
    hid_shape = jax.ShapeDtypeStruct((N_CHIPS, S, FF_SHARD), BF16)
    return pl.pallas_call(
        body, name=name, grid=(S // TM, N_CHIPS),
        in_specs=[row, row, vec, hid, hid] + [pl.BlockSpec((None, FF_SHARD, D), lambda i, s: (s, 0, 0))] * 3,
        out_specs=[row, vec, row, hid, hid, hid],
        out_shape=[jax.ShapeDtypeStruct((S, D), F32), jax.ShapeDtypeStruct((1, D), F32),
                   jax.ShapeDtypeStruct((S, D), BF16), hid_shape, hid_shape, hid_shape],
        scratch_shapes=[pltpu.VMEM((TM, D), BF16), pltpu.VMEM((TM, D), F32)],
        compiler_params=_params("arbitrary", "arbitrary"),
    )(dh, h, g, gate, up, wg, wu, wd)


def _ffn_wgrads(name, u, dgate, dup, act, dh):
    nk = S // TM

    def body(u_ref, dh_ref, dgate_ref, dup_ref, act_ref, dg_ref, du_ref, dd_ref, acc_g, acc_u, acc_d):
        k = pl.program_id(1)
        u = u_ref[...]
        parts = (_dot(dgate_ref[...], u, "tn"), _dot(dup_ref[...], u, "tn"),
                 _dot(act_ref[...], dh_ref[...].astype(BF16), "tn"))
        accs = (acc_g, acc_u, acc_d)

        @pl.when(k == 0)
        def _():
            for acc, part in zip(accs, parts):
                acc[...] = part

        @pl.when(k > 0)
        def _():
            for acc, part in zip(accs, parts):
                acc[...] += part

        @pl.when(k == nk - 1)
        def _():
            for out, acc in zip((dg_ref, du_ref, dd_ref), accs):
                out[...] = acc[...].astype(BF16)

    tok = pl.BlockSpec((TM, D), lambda s, k: (k, 0))
    hid = pl.BlockSpec((None, TM, FF_SHARD), lambda s, k: (s, k, 0))
    out = pl.BlockSpec((None, FF_SHARD, D), lambda s, k: (s, 0, 0))
    shape = jax.ShapeDtypeStruct((N_CHIPS, FF_SHARD, D), BF16)
    return pl.pallas_call(
        body, name=name, grid=(N_CHIPS, nk), in_specs=[tok, tok, hid, hid, hid], out_specs=[out, out, out],
        out_shape=[shape, shape, shape], scratch_shapes=[pltpu.VMEM((FF_SHARD, D), F32)] * 3,
        compiler_params=_params("parallel", "arbitrary"))(u, dh, dgate, dup, act)


def _rope_tables():
    pos = jnp.arange(S, dtype=F32)
    inv = ROPE_THETA ** (-jnp.arange(0, MLA_ROPE, 2, dtype=F32) / MLA_ROPE)
    ang = pos[:, None] * inv[None, :]
    half = MLA_ROPE // 2
    cos = jnp.cos(ang)
    sin = jnp.sin(ang)
    one = jnp.ones((S, KR_LANE), F32)
    zero = jnp.zeros((S, KR_LANE), F32)
    tail_one = jnp.ones((S, LANES - KR_LANE - MLA_ROPE), F32)
    tail_zero = jnp.zeros((S, LANES - KR_LANE - MLA_ROPE), F32)
    cos_t = jnp.concatenate([one, cos, cos, tail_one], axis=1)
    sin_t = jnp.concatenate([zero, -sin, sin, tail_zero], axis=1)
    assert cos_t.shape == (S, LANES) and half * 2 == MLA_ROPE
    return cos_t, sin_t


def _rope(x, cos_t, sin_t, sign):
    n = x.shape[1] // LANES
    half = MLA_ROPE // 2
    lane = _iota(x.shape, 1) & (LANES - 1)
    first = (lane >= KR_LANE) & (lane < KR_LANE + half)
    swapped = jnp.where(first, pltpu.roll(x, x.shape[1] - half, 1), pltpu.roll(x, half, 1))
    c = jnp.tile(cos_t, (1, n)) if n > 1 else cos_t
    s = jnp.tile(sin_t, (1, n)) if n > 1 else sin_t
    return x * c + swapped * (s * sign)


def _mla_prep_fwd(name, proj, g_cq, g_ckv, w_uq, w_uk, w_uv, cos_t, sin_t):
    nh = MLA_H * LANES

    def body(cq_ref, ckv_ref, kr_ref, gq_ref, gkv_ref, wq_ref, wk_ref, wv_ref, cos_ref, sin_ref,
             qa_ref, ka_ref, va_ref):
        cos_v, sin_v = cos_ref[...], sin_ref[...]
        cq = cq_ref[...]
        r = lax.rsqrt(jnp.mean(cq * cq, axis=-1, keepdims=True) + RMS_EPS)
        cqn = (cq * r * gq_ref[...]).astype(BF16)
        qa_ref[...] = _rope(_dot(cqn, wq_ref[...]), cos_v, sin_v, 1.0).astype(BF16)
        ckv = ckv_ref[...]
        r = lax.rsqrt(jnp.mean(ckv * ckv, axis=-1, keepdims=True) + RMS_EPS)
        ckvn = (ckv * r * gkv_ref[...]).astype(BF16)
        lane = _iota((TM, LANES), 1)
        rot = (lane >= KR_LANE) & (lane < KR_LANE + MLA_ROPE)
        kr = jnp.where(rot, _rope(kr_ref[...], cos_v, sin_v, 1.0), 0.0)
        ka_ref[...] = (_dot(ckvn, wk_ref[...]) + jnp.tile(kr, (1, MLA_H))).astype(BF16)
        va_ref[...] = _dot(ckvn, wv_ref[...]).astype(BF16)

    full = lambda shape: pl.BlockSpec(shape, lambda i: (0, 0))
    return pl.pallas_call(
        body, name=name, grid=(S // TM,),
        in_specs=[pl.BlockSpec((TM, Q_LORA), lambda i: (i, P_CQ // Q_LORA)),
                  pl.BlockSpec((TM, KV_LORA), lambda i: (i, P_CKV // KV_LORA)),
                  pl.BlockSpec((TM, LANES), lambda i: (i, P_KR // LANES)),
                  full((1, Q_LORA)), full((1, KV_LORA)), full((Q_LORA, nh)), full((KV_LORA, nh)),
                  full((KV_LORA, MLA_H * MLA_V)),
                  pl.BlockSpec((TM, LANES), lambda i: (i, 0)), pl.BlockSpec((TM, LANES), lambda i: (i, 0))],
        out_specs=[pl.BlockSpec((TM, nh), lambda i: (i, 0)), pl.BlockSpec((TM, nh), lambda i: (i, 0)),
                   pl.BlockSpec((TM, MLA_H * MLA_V), lambda i: (i, 0))],
        out_shape=[jax.ShapeDtypeStruct((S, nh), BF16), jax.ShapeDtypeStruct((S, nh), BF16),
                   jax.ShapeDtypeStruct((S, MLA_H * MLA_V), BF16)],
        compiler_params=_params("parallel"),
    )(proj, proj, proj, g_cq, g_ckv, w_uq, w_uk, w_uv, cos_t, sin_t)


def _mla_prep_bwd(name, dqa, dka, dva, proj, g_cq, g_ckv, w_uq, w_uk, w_uv, cos_t, sin_t):
    nh = MLA_H * LANES

    def body(dqa_ref, dka_ref, dva_ref, cq_ref, ckv_ref, gq_ref, gkv_ref, wq_ref, wk_ref, wv_ref, cos_ref, sin_ref,
             dcq_ref, dckv_ref, dkr_ref, dwq_ref, dwk_ref, dwv_ref, dgq_ref, dgkv_ref):
        i = pl.program_id(0)
        cos_v, sin_v = cos_ref[...], sin_ref[...]

        def norm_bwd(x, g, dn):
            r = lax.rsqrt(jnp.mean(x * x, axis=-1, keepdims=True) + RMS_EPS)
            xh = x * r
            dxh = dn * g
            dx = r * (dxh - xh * jnp.mean(dxh * xh, axis=-1, keepdims=True))
            return dx, jnp.sum(dn * xh, axis=0, keepdims=True), (xh * g).astype(BF16)

        dq = _rope(dqa_ref[...], cos_v, sin_v, -1.0).astype(BF16)
        dcqn = _dot(dq, wq_ref[...], "nt")
        dcq, dgq, cqn = norm_bwd(cq_ref[...], gq_ref[...], dcqn)
        dcq_ref[...] = dcq.astype(BF16)
        dwq = _dot(cqn, dq, "tn")

        dka = dka_ref[...]
        dkab = dka.astype(BF16)
        dvab = dva_ref[...].astype(BF16)
        dckvn = _dot(dkab, wk_ref[...], "nt") + _dot(dvab, wv_ref[...], "nt")
        dckv, dgkv, ckvn = norm_bwd(ckv_ref[...], gkv_ref[...], dckvn)
        dckv_ref[...] = dckv.astype(BF16)
        dwk = _dot(ckvn, dkab, "tn")
        dwv = _dot(ckvn, dvab, "tn")

        fold = dka[:, 0:LANES]
        for hh in range(1, MLA_H):
            fold = fold + dka[:, hh * LANES:(hh + 1) * LANES]
        lane = _iota((TM, LANES), 1)
        rot = (lane >= KR_LANE) & (lane < KR_LANE + MLA_ROPE)
        dkr = _rope(jnp.where(rot, fold, 0.0), cos_v, sin_v, -1.0)
        dkr_ref[...] = jnp.where(rot, dkr, 0.0).astype(BF16)

        @pl.when(i == 0)
        def _():
            dwq_ref[...] = dwq
            dwk_ref[...] = dwk
            dwv_ref[...] = dwv
            dgq_ref[...] = dgq
            dgkv_ref[...] = dgkv

        @pl.when(i > 0)
        def _():
            dwq_ref[...] += dwq
            dwk_ref[...] += dwk
            dwv_ref[...] += dwv
            dgq_ref[...] += dgq
            dgkv_ref[...] += dgkv

    full = lambda shape: pl.BlockSpec(shape, lambda i: (0, 0))
    rows = lambda c: pl.BlockSpec((TM, c), lambda i: (i, 0))
    nv = MLA_H * MLA_V
    return pl.pallas_call(
        body, name=name, grid=(S // TM,),
        in_specs=[rows(nh), rows(nh), rows(nv),
                  pl.BlockSpec((TM, Q_LORA), lambda i: (i, P_CQ // Q_LORA)),
                  pl.BlockSpec((TM, KV_LORA), lambda i: (i, P_CKV // KV_LORA)),
                  full((1, Q_LORA)), full((1, KV_LORA)), full((Q_LORA, nh)), full((KV_LORA, nh)), full((KV_LORA, nv)),
                  rows(LANES), rows(LANES)],
        out_specs=[rows(Q_LORA), rows(KV_LORA), rows(LANES), full((Q_LORA, nh)), full((KV_LORA, nh)),
                   full((KV_LORA, nv)), full((1, Q_LORA)), full((1, KV_LORA))],
        out_shape=[jax.ShapeDtypeStruct((S, Q_LORA), BF16), jax.ShapeDtypeStruct((S, KV_LORA), BF16),
                   jax.ShapeDtypeStruct((S, LANES), BF16), jax.ShapeDtypeStruct((Q_LORA, nh), F32),
                   jax.ShapeDtypeStruct((KV_LORA, nh), F32), jax.ShapeDtypeStruct((KV_LORA, nv), F32),
                   jax.ShapeDtypeStruct((1, Q_LORA), F32), jax.ShapeDtypeStruct((1, KV_LORA), F32)],
        compiler_params=_params("arbitrary"),
    )(dqa, dka, dva, proj, proj, g_cq, g_ckv, w_uq, w_uk, w_uv, cos_t, sin_t)


def _head_masks(dtype):
    lane = _iota((1, LANES), 1)
    return (lane < 64).astype(dtype), (lane >= 64).astype(dtype)


def _mla_fwd(name, qa, ka, va, carry=None):
    def body(q_ref, k_ref, v_ref, o_ref, lse_ref):
        m0b, m1b = _head_masks(BF16)
        lane = _iota((QB, LANES), 1)
        left = lane < 64

        def qblock(i, _):
            r0 = pl.multiple_of(i * QB, QB)
            qs = [q_ref[pl.ds(r0, QB), hh * LANES:(hh + 1) * LANES] for hh in range(2)]
            rowc = lax.shift_right_logical(r0 + _iota((QB, QB), 0), 6)

            def kv(kb, carry):
                ms, ls, acc = carry
                c0 = pl.multiple_of(kb * QB, QB)
                v = v_ref[pl.ds(c0, QB), :]
                ok = lax.shift_right_logical(c0 + _iota((QB, QB), 1), 6) <= rowc
                new_m, new_l, alphas = [], [], []
                pv = None
                for hh in range(2):
                    k = k_ref[pl.ds(c0, QB), hh * LANES:(hh + 1) * LANES]
                    s = jnp.where(ok, _dot(qs[hh], k, "nt") * SCALE_A, NEG)
                    mn = jnp.maximum(ms[hh], jnp.max(s, axis=-1, keepdims=True))
                    p = jnp.exp(s - mn)
                    a = jnp.exp(ms[hh] - mn)
                    new_m.append(mn)
                    new_l.append(a * ls[hh] + jnp.sum(p, axis=-1, keepdims=True))
                    alphas.append(a)
                    part = _dot(p.astype(BF16), v * (m0b if hh == 0 else m1b))
                    pv = part if pv is None else pv + part
                acc = acc * jnp.where(left, alphas[0], alphas[1]) + pv
                return tuple(new_m), tuple(new_l), acc

            init = ((jnp.full((QB, 1), NEG, F32),) * 2, (jnp.zeros((QB, 1), F32),) * 2, jnp.zeros((QB, LANES), F32))
            ms, ls, acc = lax.fori_loop(0, i + 1, kv, init)
            o_ref[pl.ds(r0, QB), :] = acc * jnp.where(left, 1.0 / ls[0], 1.0 / ls[1])
            lse_ref[pl.ds(r0, QB), :] = jnp.where(left, ms[0] + jnp.log(ls[0]), ms[1] + jnp.log(ls[1]))
            return 0

        lax.fori_loop(0, S // QB, qblock, 0)

    pair = lambda w: pl.BlockSpec((S, w), lambda p: (0, p))
    return _carrier_call(
        body, name=name, grid=(MLA_H // 2,), in_specs=[pair(2 * LANES), pair(2 * LANES), pair(LANES)],
        out_specs=[pair(LANES), pair(LANES)],
        out_shape=[jax.ShapeDtypeStruct((S, MLA_H * MLA_V), F32), jax.ShapeDtypeStruct((S, MLA_H * MLA_V), F32)],
        args=(qa, ka, va), sem=("parallel",), carry=carry)


def _mla_bwd(name, qa, ka, va, o, lse, do, do_block0, carry=None):
    def body(q_ref, k_ref, v_ref, o_ref, lse_ref, do_ref, dq_ref, dk_ref, dv_ref):
        m0f, m1f = _head_masks(F32)
        m0b, m1b = _head_masks(BF16)
        dk_ref[...] = jnp.zeros_like(dk_ref)
        dv_ref[...] = jnp.zeros_like(dv_ref)

        def qblock(i, _):
            r0 = pl.multiple_of(i * QB, QB)
            rows = pl.ds(r0, QB)
            do_f = do_ref[rows, :]
            prod = do_f * o_ref[rows, :]
            deltas = [jnp.sum(prod * m0f, axis=-1, keepdims=True), jnp.sum(prod * m1f, axis=-1, keepdims=True)]
            lse_v = lse_ref[rows, :]
            lses = [lse_v[:, 0:1], lse_v[:, 64:65]]
            dob = do_f.astype(BF16)
            dos = [dob * m0b, dob * m1b]
            qs = [q_ref[rows, hh * LANES:(hh + 1) * LANES] for hh in range(2)]
            rowc = lax.shift_right_logical(r0 + _iota((QB, QB), 0), 6)

            def kv(kb, dqs):
                c0 = pl.multiple_of(kb * QB, QB)
                cols = pl.ds(c0, QB)
                v = v_ref[cols, :]
                ok = lax.shift_right_logical(c0 + _iota((QB, QB), 1), 6) <= rowc
                out = []
                dv = None
                for hh in range(2):
                    k = k_ref[cols, hh * LANES:(hh + 1) * LANES]
                    s = _dot(qs[hh], k, "nt") * SCALE_A
                    p = jnp.where(ok, jnp.exp(s - lses[hh]), 0.0)
                    dp = _dot(dos[hh], v, "nt")
                    ds = (p * (dp - deltas[hh]) * SCALE_A).astype(BF16)
                    out.append(dqs[hh] + _dot(ds, k))
                    dk_ref[cols, hh * LANES:(hh + 1) * LANES] += _dot(ds, qs[hh], "tn")
                    part = _dot(p.astype(BF16), dos[hh], "tn")
                    dv = part if dv is None else dv + part
                dv_ref[cols, :] += dv
                return tuple(out)

            dqs = lax.fori_loop(0, i + 1, kv, (jnp.zeros((QB, LANES), F32),) * 2)
            for hh in range(2):
                dq_ref[rows, hh * LANES:(hh + 1) * LANES] = dqs[hh]
            return 0

        lax.fori_loop(0, S // QB, qblock, 0)

    pair = lambda w: pl.BlockSpec((S, w), lambda p: (0, p))
    return _carrier_call(
        body, name=name, grid=(MLA_H // 2,),
        in_specs=[pair(2 * LANES), pair(2 * LANES), pair(LANES), pair(LANES), pair(LANES),
                  pl.BlockSpec((S, LANES), lambda p: (0, do_block0 + p))],
        out_specs=[pair(2 * LANES), pair(2 * LANES), pair(LANES)],
        out_shape=[jax.ShapeDtypeStruct((S, MLA_H * LANES), F32), jax.ShapeDtypeStruct((S, MLA_H * LANES), F32),
                   jax.ShapeDtypeStruct((S, MLA_H * MLA_V), F32)],
        args=(qa, ka, va, o, lse, do), sem=("parallel",), carry=carry)


def _sb_weights(q_h, k, c, before, tri_suffix):
    z = _dot(q_h, k, "nt") * SCALE_B
    sp = _softplus(z)
    log_keep = jnp.where(before, -sp, 0.0)
    log_between = _split_dot(log_keep, tri_suffix) + c
    w = jnp.where(before, jnp.exp(z - sp + log_between), 0.0)
    return w, jnp.exp(z - sp), jnp.sum(log_keep, axis=-1, keepdims=True)


def _sb_fwd(name, proj, carry=None):
    def body(q_ref, k_ref, v_ref, o_ref):
        m0b, m1b = _head_masks(BF16)
        tri_suffix = (_iota((QB, QB), 0) > _iota((QB, QB), 1)).astype(BF16)

        def qblock(i, _):
            r0 = pl.multiple_of(i * QB, QB)
            q = q_ref[pl.ds(r0, QB), :].astype(BF16)
            qs = [q[:, pp * LANES:(pp + 1) * LANES] * m for pp in range(SB_PAIRS) for m in (m0b, m1b)]
            rowg = r0 + _iota((QB, QB), 0)

            def kv(step, carry):
                cs, accs = carry
                c0 = pl.multiple_of((i - step) * QB, QB)
                k = k_ref[pl.ds(c0, QB), :].astype(BF16)
                v = v_ref[pl.ds(c0, QB), :].astype(BF16)
                before = (c0 + _iota((QB, QB), 1)) < rowg
                new_c, new_acc = [], []
                for pp in range(SB_PAIRS):
                    kp, vp = k[:, pp * LANES:(pp + 1) * LANES], v[:, pp * LANES:(pp + 1) * LANES]
                    acc = accs[pp]
                    for hh in range(2):
                        w, _, tot = _sb_weights(qs[2 * pp + hh], kp, cs[2 * pp + hh], before, tri_suffix)
                        new_c.append(cs[2 * pp + hh] + tot)
                        acc = acc + _dot(w.astype(BF16), vp * (m0b if hh == 0 else m1b))
                    new_acc.append(acc)
                return tuple(new_c), tuple(new_acc)

            init = ((jnp.zeros((QB, 1), F32),) * (2 * SB_PAIRS), (jnp.zeros((QB, LANES), F32),) * SB_PAIRS)
            _, accs = lax.fori_loop(0, i + 1, kv, init)
            for pp in range(SB_PAIRS):
                o_ref[pl.ds(r0, QB), pp * LANES:(pp + 1) * LANES] = accs[pp].astype(BF16)
            return 0

        lax.fori_loop(0, S // QB, qblock, 0)

    wide = SB_PAIRS * LANES
    col = lambda base: pl.BlockSpec((S, wide), lambda p: (0, base // wide + p))
    return _carrier_call(
        body, name=name, grid=(SB_H // 2 // SB_PAIRS,), in_specs=[col(P_QB), col(P_KB), col(P_VB)],
        out_specs=[pl.BlockSpec((S, wide), lambda p: (0, p))],
        out_shape=[jax.ShapeDtypeStruct((S, SB_H * SB_DIM), BF16)],
        args=(proj, proj, proj), sem=("parallel",), carry=carry)


def _sb_bwd(name, proj, do, do_block0, carry=None):
    nb = S // QB

    def body(q_ref, k_ref, v_ref, do_ref, dq_ref, dk_ref, dv_ref, sig_scr, dl_scr, dk_acc, dv_acc):
        m0b, m1b = _head_masks(BF16)
        tri_suffix = (_iota((QB, QB), 0) > _iota((QB, QB), 1)).astype(BF16)
        tri_prefix = (_iota((QB, QB), 0) < _iota((QB, QB), 1)).astype(BF16)
        dk_acc[...] = jnp.zeros_like(dk_acc)
        dv_acc[...] = jnp.zeros_like(dv_acc)

        def qblock(i, _):
            r0 = pl.multiple_of(i * QB, QB)
            rows = pl.ds(r0, QB)
            q = q_ref[rows, :].astype(BF16)
            qs = [q * m0b, q * m1b]
            dob = do_ref[rows, :].astype(BF16)
            dos = [dob * m0b, dob * m1b]
            rowg = r0 + _iota((QB, QB), 0)

            def sweep_left(step, cs):
                kb = i - step
                c0 = pl.multiple_of(kb * QB, QB)
                cols = pl.ds(c0, QB)
                k = k_ref[cols, :].astype(BF16)
                v = v_ref[cols, :].astype(BF16)
                before = (c0 + _iota((QB, QB), 1)) < rowg
                new_c = []
                dv = None
                for hh in range(2):
                    w, sig, tot = _sb_weights(qs[hh], k, cs[hh], before, tri_suffix)
                    new_c.append(cs[hh] + tot)
                    sig_scr[hh, kb] = sig
                    dl_scr[hh, kb] = _dot(dos[hh], v, "nt") * w
                    part = _dot(w.astype(BF16), dos[hh], "tn")
                    dv = part if dv is None else dv + part
                dv_acc[cols, :] += dv
                return tuple(new_c)

            lax.fori_loop(0, i + 1, sweep_left, (jnp.zeros((QB, 1), F32),) * 2)

            def sweep_right(kb, carry):
                ps, dq = carry
                c0 = pl.multiple_of(kb * QB, QB)
                cols = pl.ds(c0, QB)
                k = k_ref[cols, :].astype(BF16)
                before = (c0 + _iota((QB, QB), 1)) < rowg
                new_p = []
                dk = None
                for hh in range(2):
                    dl = dl_scr[hh, kb]
                    sig = sig_scr[hh, kb]
                    earlier = _split_dot(dl, tri_prefix) + ps[hh]
                    new_p.append(ps[hh] + jnp.sum(dl, axis=-1, keepdims=True))
                    dz = (jnp.where(before, dl * (1.0 - sig) - earlier * sig, 0.0) * SCALE_B).astype(BF16)
                    dq = dq + _dot(dz, k * (m0b if hh == 0 else m1b))
                    part = _dot(dz, qs[hh], "tn")
                    dk = part if dk is None else dk + part
                dk_acc[cols, :] += dk
                return tuple(new_p), dq

            init = ((jnp.zeros((QB, 1), F32),) * 2, jnp.zeros((QB, LANES), F32))
            _, dq = lax.fori_loop(0, i + 1, sweep_right, init)
            dq_ref[rows, :] = dq.astype(BF16)
            return 0

        lax.fori_loop(0, nb, qblock, 0)
        dk_ref[...] = dk_acc[...].astype(BF16)
        dv_ref[...] = dv_acc[...].astype(BF16)

    col = lambda base: pl.BlockSpec((S, LANES), lambda p: (0, base // LANES + p))
    out = pl.BlockSpec((S, LANES), lambda p: (0, p))
    shape = jax.ShapeDtypeStruct((S, SB_H * SB_DIM), BF16)
    return _carrier_call(
        body, name=name, grid=(SB_H // 2,),
        in_specs=[col(P_QB), col(P_KB), col(P_VB), pl.BlockSpec((S, LANES), lambda p: (0, do_block0 + p))],
        out_specs=[out, out, out], out_shape=[shape, shape, shape],
        scratch_shapes=[pltpu.VMEM((2, nb, QB, QB), F32), pltpu.VMEM((2, nb, QB, QB), F32),
                        pltpu.VMEM((S, LANES), F32), pltpu.VMEM((S, LANES), F32)],
        args=(proj, proj, proj, do), sem=("parallel",), carry=carry)


def _band_row_index():
    j = np.arange(TOEP_W)
    rel = np.clip(LEFT_CHUNKS * CHUNK - j, -REL_CLIP, REL_CLIP) + REL_CLIP
    rel[BAND_W:] = 2 * REL_CLIP
    return rel.astype(np.int32)


def _band_tiles(r0_ref, q_ref, kpad, vpad, m, m0b, m1b, static_ok, bias):
    r0 = pl.multiple_of(m * BQ, BQ)
    q = q_ref[0, pl.ds(r0, BQ), :]
    kw = kpad[pl.ds(r0, BAND_W), :]
    vw = vpad[pl.ds(r0, BAND_W), :]
    ok = static_ok & ((r0 - BAND_PAD + _iota((BQ, BAND_W), 1)) >= 0)
    qs = [q * m0b, q * m1b]
    ps = []
    for hh in range(2):
        s = jnp.where(ok, _dot(qs[hh], kw, "nt") * SCALE_C + bias[hh], NEG)
        e = jnp.exp(s - jnp.max(s, axis=-1, keepdims=True))
        ps.append(e * (1.0 / jnp.sum(e, axis=-1, keepdims=True)))
    return r0, qs, kw, vw, ps


def _band_setup(qkv_ref, r0_ref, kpad, vpad):
    kpad[0:BAND_PAD, :] = jnp.zeros((BAND_PAD, LANES), BF16)
    vpad[0:BAND_PAD, :] = jnp.zeros((BAND_PAD, LANES), BF16)
    kpad[BAND_PAD:, :] = qkv_ref[1]
    vpad[BAND_PAD:, :] = qkv_ref[2]
    jc = lax.shift_right_logical(_iota((BQ, BAND_W), 1), 6)
    rc = lax.shift_right_logical(_iota((BQ, BAND_W), 0), 6)
    static_ok = (jc >= rc) & (jc <= rc + LEFT_CHUNKS)
    bias = []
    for hh in range(2):
        row = jnp.broadcast_to(r0_ref[hh:hh + 1, :], (BQ, TOEP_W))
        bias.append(pltpu.roll(row, 0, 1, stride=1, stride_axis=0)[:, :BAND_W])
    return static_ok, bias


def _band_fwd(name, qkv, r0):
    def body(qkv_ref, r0_ref, o_ref, kpad, vpad):
        m0b, m1b = _head_masks(BF16)
        static_ok, bias = _band_setup(qkv_ref, r0_ref, kpad, vpad)

        def qblock(m, _):
            r0_, _, _, vw, ps = _band_tiles(r0_ref, qkv_ref, kpad, vpad, m, m0b, m1b, static_ok, bias)
            o = _dot(ps[0].astype(BF16), vw * m0b) + _dot(ps[1].astype(BF16), vw * m1b)
            o_ref[pl.ds(r0_, BQ), :] = o.astype(BF16)
            return 0

        lax.fori_loop(0, S // BQ, qblock, 0)

    return pl.pallas_call(
        body, name=name, grid=(C_H // 2,),
        in_specs=[pl.BlockSpec((3, S, LANES), lambda p: (0, 0, p)), pl.BlockSpec((None, 2, TOEP_W), lambda p: (p, 0, 0))],
        out_specs=pl.BlockSpec((S, LANES), lambda p: (0, p)),
        out_shape=jax.ShapeDtypeStruct((S, C_H * C_DIM), BF16),
        scratch_shapes=[pltpu.VMEM((S + BAND_PAD, LANES), BF16), pltpu.VMEM((S + BAND_PAD, LANES), BF16)],
        compiler_params=_params("parallel"),
    )(qkv, r0)


def _band_bwd(name, qkv, r0, do, carry=None):
    def body(qkv_ref, r0_ref, do_ref, dqkv_ref, dr0_ref, kpad, vpad, dkpad, dvpad, db_acc):
        m0b, m1b = _head_masks(BF16)
        static_ok, bias = _band_setup(qkv_ref, r0_ref, kpad, vpad)
        dkpad[...] = jnp.zeros_like(dkpad)
        dvpad[...] = jnp.zeros_like(dvpad)
        db_acc[...] = jnp.zeros_like(db_acc)

        def qblock(m, _):
            r0_, qs, kw, vw, ps = _band_tiles(r0_ref, qkv_ref, kpad, vpad, m, m0b, m1b, static_ok, bias)
            dob = do_ref[pl.ds(r0_, BQ), :].astype(BF16)
            dos = [dob * m0b, dob * m1b]
            dq = None
            dk = None
            dv = None
            for hh in range(2):
                p = ps[hh]
                dp = _dot(dos[hh], vw, "nt")
                ds = p * (dp - jnp.sum(dp * p, axis=-1, keepdims=True))
                db_acc[hh, :, 0:BAND_W] += ds
                dsb = (ds * SCALE_C).astype(BF16)
                t = _dot(dsb, kw * (m0b if hh == 0 else m1b))
                dq = t if dq is None else dq + t
                t = _dot(dsb, qs[hh], "tn")
                dk = t if dk is None else dk + t
                t = _dot(p.astype(BF16), dos[hh], "tn")
                dv = t if dv is None else dv + t
            dqkv_ref[0, pl.ds(r0_, BQ), :] = dq.astype(BF16)
            dkpad[pl.ds(r0_, BAND_W), :] += dk
            dvpad[pl.ds(r0_, BAND_W), :] += dv
            return 0

        lax.fori_loop(0, S // BQ, qblock, 0)
        dqkv_ref[1] = dkpad[BAND_PAD:, :].astype(BF16)
        dqkv_ref[2] = dvpad[BAND_PAD:, :].astype(BF16)
        sub = _iota((8, TOEP_W), 0)
        for hh in range(2):
            folded = db_acc[hh, 0:8, :]
            for a in range(1, BQ // 8):
                folded = folded + pltpu.roll(db_acc[hh, 8 * a:8 * a + 8, :], TOEP_W - 8 * a, 1)
            for bit in range(3):
                moved = pltpu.roll(folded, TOEP_W - (1 << bit), 1)
                folded = jnp.where((sub & (1 << bit)) != 0, moved, folded)
            dr0_ref[hh:hh + 1, :] = jnp.sum(folded, axis=0, keepdims=True)

    return _carrier_call(
        body, name=name, grid=(C_H // 2,),
        in_specs=[pl.BlockSpec((3, S, LANES), lambda p: (0, 0, p)), pl.BlockSpec((None, 2, TOEP_W), lambda p: (p, 0, 0)),
                  pl.BlockSpec((S, LANES), lambda p: (0, p))],
        out_specs=[pl.BlockSpec((3, S, LANES), lambda p: (0, 0, p)), pl.BlockSpec((None, 2, TOEP_W), lambda p: (p, 0, 0))],
        out_shape=[jax.ShapeDtypeStruct((3, S, C_H * C_DIM), BF16), jax.ShapeDtypeStruct((C_H // 2, 2, TOEP_W), F32)],
        scratch_shapes=[pltpu.VMEM((S + BAND_PAD, LANES), BF16), pltpu.VMEM((S + BAND_PAD, LANES), BF16),
                        pltpu.VMEM((S + BAND_PAD, LANES), F32), pltpu.VMEM((S + BAND_PAD, LANES), F32),
                        pltpu.VMEM((2, BQ, TOEP_W), F32)],
        args=(qkv, r0, do), sem=("parallel",), carry=carry)


def _bias_table_grad(name, dr0):
    w_out = 5 * LANES

    def body(d_ref, o_ref):
        j = _iota((TOEP_W, w_out), 0)
        rel = jnp.clip(LEFT_CHUNKS * CHUNK - j, -REL_CLIP, REL_CLIP) + REL_CLIP
        rel = jnp.where(j >= BAND_W, 2 * REL_CLIP, rel)
        onehot = (rel == _iota((TOEP_W, w_out), 1)).astype(BF16)
        d = d_ref[...]
        hi = d.astype(BF16)
        mid = (d - hi.astype(F32))
        mid_b = mid.astype(BF16)
        lo = (mid - mid_b.astype(F32)).astype(BF16)
        o_ref[...] = _dot(hi, onehot) + _dot(mid_b, onehot) + _dot(lo, onehot)

    return pl.pallas_call(
        body, name=name, out_shape=jax.ShapeDtypeStruct((C_H, w_out), F32),
        in_specs=[pl.BlockSpec((C_H, TOEP_W), lambda: (0, 0))], out_specs=pl.BlockSpec((C_H, w_out), lambda: (0, 0)),
        grid=(),
    )(dr0)


def _carry_gather(cy, slots, names, ici, d2d):
    idx = [cy.operand(slots[n], True) for n in names]
    n = len(names)
    base_i = cy.sems(3 * n) if ici else 0
    base_d = cy.sems(3 * n) if d2d else 0

    def piece(refs, t, slot, cc):
        return refs[idx[t]].at[slot, _half_rows(cc, slots[names[t]].shape[1] // 2), :]

    def over_ici(refs, send, recv, arriving):
        x, y, c, chips = _position()
        out = []
        for t in range(n):
            for j in range(3):
                r = piece(refs, t, 2 * chips[j][0] + chips[j][1] if arriving else 2 * x + y, c)
                out.append(_remote(r, r, send, recv, base_i + 3 * t + j, (*chips[j], c)))
        return out

    def over_d2d(refs, send, recv, arriving):
        x, y, c, chips = _position()
        out = []
        for t in range(n):
            for j in range(3):
                r = piece(refs, t, 2 * chips[j][0] + chips[j][1], 1 - c if arriving else c)
                out.append(_remote(r, r, send, recv, base_d + 3 * t + j, (x, y, 1 - c)))
        return out

    def start_ici(refs, fresh, send, recv):
        for cp in over_ici(refs, send, recv, False):
            cp.start()

    def wait_ici(refs, fresh, send, recv):
        for cp in over_ici(refs, send, recv, True):
            cp.wait_recv()
        for cp in over_ici(refs, send, recv, False):
            cp.wait_send()

    def start_d2d(refs, fresh, send, recv):
        for cp in over_d2d(refs, send, recv, False):
            cp.start()

    def wait_d2d(refs, fresh, send, recv):
        for cp in over_d2d(refs, send, recv, True):
            cp.wait_recv()
        for cp in over_d2d(refs, send, recv, False):
            cp.wait_send()

    if ici and d2d:
        cy.starts.append(start_ici)
        cy.finishes += [wait_ici, start_d2d, wait_d2d]
    elif ici:
        cy.starts.append(start_ici)
        cy.finishes.append(wait_ici)
    else:
        cy.starts.append(start_d2d)
        cy.finishes.append(wait_d2d)

    def done(aliased, fresh):
        for t, name in enumerate(names):
            slots[name] = aliased[idx[t]]

    cy.on_done.append(done)


def _carry_chip_exchange(cy, sums, got, names):
    idx = [cy.operand(sums[n], False) for n in names]
    out = [cy.result((3,) + sums[n].shape[1:], BF16) for n in names]
    base = cy.sems(3 * len(names))

    def copies(refs, fresh, send, recv):
        x, y, c, chips = _position()
        return [_remote(refs[idx[t]].at[2 * chips[j][0] + chips[j][1]], fresh[out[t]].at[j], send, recv, base + 3 * t + j,
                        (*chips[j], c)) for t in range(len(names)) for j in range(3)]

    def start(refs, fresh, send, recv):
        for cp in copies(refs, fresh, send, recv):
            cp.start()

    def wait(refs, fresh, send, recv):
        for cp in copies(refs, fresh, send, recv):
            cp.wait()

    cy.starts.append(start)
    cy.finishes.append(wait)

    def done(aliased, fresh):
        for t, name in enumerate(names):
            got[name] = fresh[out[t]]

    cy.on_done.append(done)


def _run_carry(name, cy):
    _, res = _carrier_call(None, name=name, grid=(), in_specs=[], out_specs=[], out_shape=[], args=(), sem=(), carry=cy)
    cy.done(res)


FIRST_WEIGHTS = ("ev_w_in", "ev_w_uq", "ev_w_ukv")
LAYER0_WEIGHTS = ("ev_w_out", "w_gate0", "w_up0", "w_down0")
LAYER1_WEIGHTS = ("od_w_qkv", "od_w_out", "w_gate1", "w_up1", "w_down1")
GRAD_GROUPS = {"ffn1": ("w_gate1", "w_up1", "w_down1"), "od": ("od_w_qkv", "od_w_out"),
               "ffn0": ("w_gate0", "w_up0", "w_down0"), "ev": ("ev_w_in", "ev_w_uq", "ev_w_ukv", "ev_w_out")}


def _carry_pair_exchange(cy, parts, theirs, names):
    idx = [cy.operand(parts[n], False) for n in names]
    out = [cy.result((N_CHIPS, parts[n].shape[1] // 2, parts[n].shape[2]), BF16) for n in names]
    base = cy.sems(len(names))

    def copies(refs, fresh, send, recv):
        x, y, c, _ = _position()
        return [_remote(refs[idx[t]].at[:, _half_rows(1 - c, parts[n].shape[1] // 2), :], fresh[out[t]], send, recv,
                        base + t, (x, y, 1 - c)) for t, n in enumerate(names)]

    cy.starts.append(lambda refs, fresh, send, recv: [cp.start() for cp in copies(refs, fresh, send, recv)])
    cy.finishes.append(lambda refs, fresh, send, recv: [cp.wait() for cp in copies(refs, fresh, send, recv)])

    def done(aliased, fresh):
        for t, name in enumerate(names):
            theirs[name] = fresh[out[t]]

    cy.on_done.append(done)


def _carry_sibling_exchange(cy, fulls, pieces):
    idx = [cy.operand(fulls[p], True) for p, _ in pieces]
    base = cy.sems(len(pieces))

    def copies(refs, send, recv, arriving):
        x, y, c, _ = _position()
        out = []
        for t, (p, layer) in enumerate(pieces):
            r = refs[idx[t]].at[layer, _half_rows(1 - c if arriving else c, fulls[p].shape[1] // 2), :]
            out.append(_remote(r, r, send, recv, base + t, (x, y, 1 - c)))
        return out

    def start(refs, fresh, send, recv):
        for cp in copies(refs, send, recv, False):
            cp.start()

    def wait(refs, fresh, send, recv):
        for cp in copies(refs, send, recv, True):
            cp.wait_recv()
        for cp in copies(refs, send, recv, False):
            cp.wait_send()

    cy.starts.append(start)
    cy.finishes.append(wait)

    def done(aliased, fresh):
        for t, (p, _) in enumerate(pieces):
            fulls[p] = aliased[idx[t]]

    cy.on_done.append(done)


RIDES = {
    "mla_attn": (("gather_ici", LAYER0_WEIGHTS),),
    "sb_attn": (("gather_d2d", LAYER0_WEIGHTS), ("gather_ici", LAYER1_WEIGHTS)),
    "ffn0": (("gather_d2d", LAYER1_WEIGHTS),),
    "od_out_bwd_w": (("pair", "ffn1"),),
    "band_attn_bwd": (("chips", "ffn1"),),
    "rms_mix1_bwd": (("pair", "od"),),
    "ev_out_bwd_w": (("pair", "ffn0"),),
    "mla_attn_bwd": (("chips", "od"), ("sibling", "ffn1")),
    "sb_attn_bwd": (("chips", "ffn0"), ("sibling", "od")),
    "proj_in_bwd_w": (("sibling", "ffn0"),),
    "proj_in_bwd_x": (("chips", "ev"),),
}


class _Exchanges:
    def __init__(self, slots, pos, shapes):
        self.slots, self.pos, self.shapes = dict(slots), pos, shapes
        self.parts, self.theirs, self.sums, self.got, self.fulls = {}, {}, {}, {}, {}

    def begin(self):
        cy = _Carry()
        _carry_gather(cy, self.slots, FIRST_WEIGHTS, True, True)
        _run_carry("gather_first", cy)

    def weights(self, *names):
        return [self.slots[n] for n in names]

    def _pair_sums(self, group):
        for n in GRAD_GROUPS[group]:
            if n not in self.sums:
                self.sums[n] = _pair_sum("pair_sum_" + n, self.parts[n], self.theirs[n], self.pos)

    def _chip_sums(self, group):
        for n in GRAD_GROUPS[group]:
            param, layer = PART_OF[n]
            self.fulls[param] = _chip_sum("chip_sum_" + n, self.sums[n], self.got[n], self.pos, layer,
                                          self.shapes[param], self.fulls.get(param))

    def carry(self, stage):
        cy = _Carry()
        for step, what in RIDES[stage]:
            if step == "gather_ici":
                _carry_gather(cy, self.slots, what, True, False)
            elif step == "gather_d2d":
                _carry_gather(cy, self.slots, what, False, True)
            elif step == "pair":
                _carry_pair_exchange(cy, self.parts, self.theirs, GRAD_GROUPS[what])
            elif step == "chips":
                self._pair_sums(what)
                _carry_chip_exchange(cy, self.sums, self.got, GRAD_GROUPS[what])
            elif step == "sibling":
                self._chip_sums(what)
                _carry_sibling_exchange(cy, self.fulls, [PART_OF[n] for n in GRAD_GROUPS[what]])
        return cy

    def grads(self, group, parts):
        self.parts.update(parts)
        if group == "ev":
            cy = _Carry()
            _carry_pair_exchange(cy, self.parts, self.theirs, GRAD_GROUPS[group])
            _run_carry("grads_pair_ev", cy)

    def finish(self):
        cy = _Carry()
        self._chip_sums("ev")
        _carry_sibling_exchange(cy, self.fulls, [PART_OF[n] for n in GRAD_GROUPS["ev"]])
        _run_carry("grads_sibling_ev", cy)
        return {n: self.fulls[n] for n in BIG}


class _NoExchanges:
    def __init__(self, slots):
        self.slots, self.parts = dict(slots), {}

    def begin(self):
        pass

    def weights(self, *names):
        return [self.slots[n] for n in names]

    def carry(self, stage):
        return None

    def grads(self, group, parts):
        self.parts.update(parts)


def _first_weights(w_in_s, w_uq_s, w_ukv_s):
    gw = {"ev_w_in": w_in_s, "ev_w_uq": w_uq_s, "ev_w_ukv": w_ukv_s}
    w_in = jnp.moveaxis(gw["ev_w_in"], 0, 1).reshape(D, EVEN_IN)
    z = lambda n: jnp.zeros((D, n), BF16)
    w_in_p = jnp.concatenate(
        [w_in[:, 0:384], z(128), w_in[:, 384:640], w_in[:, 672:2208], z(KR_LANE), w_in[:, 640:672],
         z(LANES - KR_LANE - MLA_ROPE)], axis=1)
    w_uq = jnp.moveaxis(gw["ev_w_uq"], 0, 1).reshape(Q_LORA, MLA_H, MLA_NOPE + MLA_ROPE)
    w_uq_p = jnp.concatenate([w_uq, jnp.zeros((Q_LORA, MLA_H, LANES - MLA_NOPE - MLA_ROPE), BF16)], axis=2)
    w_ukv = jnp.moveaxis(gw["ev_w_ukv"], 0, 1).reshape(KV_LORA, MLA_H, MLA_NOPE + MLA_V)
    w_uk_p = jnp.concatenate([w_ukv[:, :, :MLA_NOPE], jnp.zeros((KV_LORA, MLA_H, LANES - MLA_NOPE), BF16)], axis=2)
    return dict(
        w_in=w_in_p, w_uq=w_uq_p.reshape(Q_LORA, MLA_H * LANES), w_uk=w_uk_p.reshape(KV_LORA, MLA_H * LANES),
        w_uv=w_ukv[:, :, MLA_NOPE:].reshape(KV_LORA, MLA_H * MLA_V))


def _proj_mm(name, u, w_in):
    return _mm(name, u, w_in, kind="nn", grid=(S // TM, 1, 1),
               a_spec=pl.BlockSpec((TM, D), lambda i, j, k: (i, 0)), b_spec=pl.BlockSpec((D, P_IN), lambda i, j, k: (0, 0)),
               o_spec=pl.BlockSpec((TM, P_IN), lambda i, j, k: (i, 0)), out_shape=(S, P_IN), out_dtype=F32, acc_shape=None)


def _out_proj(name, o, w, resid):
    return _mm(name, o, w, kind="nn", grid=(S // TM, 1, 1),
               a_spec=pl.BlockSpec((TM, D), lambda i, j, k: (i, 0)), b_spec=pl.BlockSpec((D, D), lambda i, j, k: (0, 0)),
               o_spec=pl.BlockSpec((TM, D), lambda i, j, k: (i, 0)), out_shape=(S, D), out_dtype=F32, acc_shape=None,
               resid=resid, r_spec=pl.BlockSpec((TM, D), lambda i, j, k: (i, 0)))


def _out_proj_bwd(name, dh, o, w, ex):
    d_o = _mm(name + "_x", dh, w, kind="nt", grid=(S // TM, 1, 1),
              a_spec=pl.BlockSpec((TM, D), lambda i, j, k: (i, 0)), b_spec=pl.BlockSpec((D, D), lambda i, j, k: (0, 0)),
              o_spec=pl.BlockSpec((TM, D), lambda i, j, k: (i, 0)), out_shape=(S, D), out_dtype=F32, acc_shape=None)
    d_w = _mm(name + "_w", o, dh, kind="tn", grid=(2, S // TM),
              a_spec=pl.BlockSpec((TM, TM), lambda j, k: (k, j)), b_spec=pl.BlockSpec((TM, D), lambda j, k: (k, 0)),
              o_spec=pl.BlockSpec((TM, D), lambda j, k: (j, 0)), out_shape=(D, D), out_dtype=BF16, acc_shape=(TM, D),
              carry=ex.carry(name + "_w"))
    return d_o, d_w


def _local_step(x, tgt, sm, ex):
    def riding(stage, fn, *args):
        cy = ex.carry(stage)
        res, copies = fn(stage, *args, carry=cy)
        if cy is not None:
            cy.done(copies)
        return res

    cos_t, sin_t = _rope_tables()
    g_mix, g_ffn = sm["g_mix"], sm["g_ffn"]
    r0 = sm["od_rel_bias"][0][:, _band_row_index()].reshape(C_H // 2, 2, TOEP_W)
    nt = 3

    ex.begin()
    w = _first_weights(*ex.weights(*FIRST_WEIGHTS))
    u0 = _rms_fwd("rms_mix0", x, g_mix[0:1])
    proj = _proj_mm("proj_in", u0, w["w_in"])
    qa, ka, va = _mla_prep_fwd("mla_prep", proj, sm["ev_g_cq"], sm["ev_g_ckv"], w["w_uq"], w["w_uk"], w["w_uv"], cos_t, sin_t)
    o_a, lse = riding("mla_attn", _mla_fwd, qa, ka, va)
    o_b, = riding("sb_attn", _sb_fwd, proj)
    o_ev = jnp.concatenate([o_a.astype(BF16), o_b], axis=1)
    w["ev_w_out"], w["w_gate0"], w["w_up0"], w["w_down0"] = ex.weights(*LAYER0_WEIGHTS)
    w["ev_w_out"] = w["ev_w_out"].reshape(D, D)
    h1 = _out_proj("ev_out", o_ev, w["ev_w_out"], x)
    h2, gate0, up0 = riding("ffn0", _ffn_fwd, h1, g_ffn[0:1], w["w_gate0"], w["w_up0"], w["w_down0"])
    w["w_qkv"], w["od_w_out"], w["w_gate1"], w["w_up1"], w["w_down1"] = ex.weights(*LAYER1_WEIGHTS)
    w["od_w_out"] = w["od_w_out"].reshape(D, D)
    w["w_qkv"] = jnp.moveaxis(w["w_qkv"], 0, 1).reshape(D, nt * D)
    u2 = _rms_fwd("rms_mix1", h2, g_mix[1:2])
    qkv = _mm("qkv", u2, w["w_qkv"], kind="nn", grid=(S // TM, nt, 1),
              a_spec=pl.BlockSpec((TM, D), lambda i, t, k: (i, 0)), b_spec=pl.BlockSpec((D, D), lambda i, t, k: (0, t)),
              o_spec=pl.BlockSpec((None, TM, D), lambda i, t, k: (t, i, 0)),
              out_shape=(nt, S, D), out_dtype=BF16, acc_shape=None)
    o_od = _band_fwd("band_attn", qkv, r0)
    h3 = _out_proj("od_out", o_od, w["od_w_out"], h2)
    (h4, gate1, up1), _ = _ffn_fwd("ffn1", h3, g_ffn[1:2], w["w_gate1"], w["w_up1"], w["w_down1"])

    loss, dh4, dg_final = _loss_bwd("loss", h4, sm["g_final"].reshape(1, D), tgt)

    dh3, dg_ffn1, u3, dgate, dup, act = _ffn_bwd("ffn1_bwd", dh4, h3, g_ffn[1:2], gate1, up1,
                                                 w["w_gate1"], w["w_up1"], w["w_down1"])
    d_wg1, d_wu1, d_wd1 = _ffn_wgrads("ffn1_dw", u3, dgate, dup, act, dh4)
    ex.grads("ffn1", {"w_gate1": d_wg1, "w_up1": d_wu1, "w_down1": d_wd1})

    d_ood, d_w_od_out = _out_proj_bwd("od_out_bwd", dh3, o_od, w["od_w_out"], ex)
    dqkv, dr0 = riding("band_attn_bwd", _band_bwd, qkv, r0, d_ood)
    du2 = _mm("qkv_bwd_x", dqkv, w["w_qkv"], kind="nt", grid=(S // TM, nt),
              a_spec=pl.BlockSpec((None, TM, D), lambda i, t: (t, i, 0)), b_spec=pl.BlockSpec((D, D), lambda i, t: (0, t)),
              o_spec=pl.BlockSpec((TM, D), lambda i, t: (i, 0)), out_shape=(S, D), out_dtype=F32, acc_shape=(TM, D))
    d_w_qkv = _mm("qkv_bwd_w", u2, dqkv, kind="tn", grid=(nt, S // TM),
                  a_spec=pl.BlockSpec((TM, D), lambda t, k: (k, 0)), b_spec=pl.BlockSpec((None, TM, D), lambda t, k: (t, k, 0)),
                  o_spec=pl.BlockSpec((D, D), lambda t, k: (0, t)), out_shape=(D, nt * D), out_dtype=BF16, acc_shape=(D, D))
    shard_cols = lambda a: jnp.moveaxis(a.reshape(a.shape[0], N_CHIPS, a.shape[1] // N_CHIPS), 1, 0)
    ex.grads("od", {"od_w_qkv": shard_cols(d_w_qkv), "od_w_out": d_w_od_out.reshape(N_CHIPS, D // N_CHIPS, D)})
    dh2, dg_mix1 = _rms_bwd("rms_mix1_bwd", du2, h2, g_mix[1:2], dh3, carry=ex.carry("rms_mix1_bwd"))
    d_rel = _bias_table_grad("rel_bias_grad", dr0.reshape(C_H, TOEP_W))[:, :2 * REL_CLIP + 1]

    dh1, dg_ffn0, u1, dgate, dup, act = _ffn_bwd("ffn0_bwd", dh2, h1, g_ffn[0:1], gate0, up0,
                                                 w["w_gate0"], w["w_up0"], w["w_down0"])
    d_wg0, d_wu0, d_wd0 = _ffn_wgrads("ffn0_dw", u1, dgate, dup, act, dh2)
    ex.grads("ffn0", {"w_gate0": d_wg0, "w_up0": d_wu0, "w_down0": d_wd0})

    d_oev, d_w_ev_out = _out_proj_bwd("ev_out_bwd", dh1, o_ev, w["ev_w_out"], ex)
    dqa, dka, dva = riding("mla_attn_bwd", _mla_bwd, qa, ka, va, o_a, lse, d_oev, 0)
    dqb, dkb, dvb = riding("sb_attn_bwd", _sb_bwd, proj, d_oev, MLA_H * MLA_V // LANES)
    dcq, dckv, dkr, d_w_uq, d_w_uk, d_w_uv, dg_cq, dg_ckv = _mla_prep_bwd(
        "mla_prep_bwd", dqa, dka, dva, proj, sm["ev_g_cq"], sm["ev_g_ckv"], w["w_uq"], w["w_uk"], w["w_uv"], cos_t, sin_t)
    dproj = jnp.concatenate([dcq, jnp.zeros((S, LANES), BF16), dckv, dqb, dkb, dvb, dkr], axis=1)
    d_w_in_p = _mm("proj_in_bwd_w", u0, dproj, kind="tn", grid=(1, S // TM),
                   a_spec=pl.BlockSpec((TM, D), lambda j, k: (k, 0)), b_spec=pl.BlockSpec((TM, P_IN), lambda j, k: (k, 0)),
                   o_spec=pl.BlockSpec((D, P_IN), lambda j, k: (0, 0)), out_shape=(D, P_IN), out_dtype=BF16,
                   acc_shape=(D, P_IN), carry=ex.carry("proj_in_bwd_w"))
    d_w_in = jnp.concatenate([d_w_in_p[:, 0:384], d_w_in_p[:, 512:768],
                              d_w_in_p[:, P_KR + KR_LANE:P_KR + KR_LANE + MLA_ROPE], d_w_in_p[:, 768:2304]], axis=1)
    d_w_uq_std = d_w_uq.reshape(Q_LORA, MLA_H, LANES)[:, :, :MLA_NOPE + MLA_ROPE].reshape(Q_LORA, -1)
    d_w_ukv = jnp.concatenate([d_w_uk.reshape(KV_LORA, MLA_H, LANES)[:, :, :MLA_NOPE],
                               d_w_uv.reshape(KV_LORA, MLA_H, MLA_V)], axis=2).reshape(KV_LORA, -1)
    ex.grads("ev", {"ev_w_in": shard_cols(d_w_in), "ev_w_uq": shard_cols(d_w_uq_std.astype(BF16)),
                    "ev_w_ukv": shard_cols(d_w_ukv.astype(BF16)),
                    "ev_w_out": d_w_ev_out.reshape(N_CHIPS, D // N_CHIPS, D)})
    du0 = _mm("proj_in_bwd_x", dproj, w["w_in"], kind="nt", grid=(S // TM, 1, 1),
              a_spec=pl.BlockSpec((TM, P_IN), lambda i, j, k: (i, 0)), b_spec=pl.BlockSpec((D, P_IN), lambda i, j, k: (0, 0)),
              o_spec=pl.BlockSpec((TM, D), lambda i, j, k: (i, 0)), out_shape=(S, D), out_dtype=F32, acc_shape=None,
              carry=ex.carry("proj_in_bwd_x"))
    grad_x, dg_mix0 = _rms_bwd("rms_mix0_bwd", du0, x, g_mix[0:1], dh1)
    small = {
        "ev_g_cq": dg_cq, "ev_g_ckv": dg_ckv, "od_rel_bias": d_rel.reshape(1, C_H, 2 * REL_CLIP + 1),
        "g_mix": jnp.concatenate([dg_mix0, dg_mix1], axis=0), "g_ffn": jnp.concatenate([dg_ffn0, dg_ffn1], axis=0),
        "g_final": dg_final.reshape(D),
    }
    return loss, grad_x, small


BIG = ("ev_w_in", "ev_w_uq", "ev_w_ukv", "ev_w_out", "od_w_qkv", "od_w_out", "w_gate", "w_up", "w_down")
SMALL = ("ev_g_cq", "ev_g_ckv", "od_rel_bias", "g_mix", "g_ffn", "g_final")
WEIGHTS = ("ev_w_in", "ev_g_cq", "ev_w_uq", "ev_g_ckv", "ev_w_ukv", "ev_w_out", "od_w_qkv", "od_rel_bias", "od_w_out",
           "g_mix", "g_ffn", "w_gate", "w_up", "w_down", "g_final")
GRAD_PARTS = (("ev_w_in", "ev_w_in", 0), ("ev_w_uq", "ev_w_uq", 0), ("ev_w_ukv", "ev_w_ukv", 0),
              ("ev_w_out", "ev_w_out", 0), ("od_w_qkv", "od_w_qkv", 0), ("od_w_out", "od_w_out", 0),
              ("w_gate0", "w_gate", 0), ("w_gate1", "w_gate", 1), ("w_up0", "w_up", 0), ("w_up1", "w_up", 1),
              ("w_down0", "w_down", 0), ("w_down1", "w_down", 1))
PART_OF = {part: (param, layer) for part, param, layer in GRAD_PARTS}
SMALL_ROWS = 112
SMALL_SIZE = 384 + 256 + 16 * 513 + 2 * 1024 + 2 * 1024 + 1024
TRANSPOSED = ("w_gate", "w_up")


def _row_tile(rows, cap=512):
    for t in range(min(rows, cap), 0, -1):
        if rows % t == 0 and t % 16 == 0:
            return t
    return rows


def _cast_into_slot(name, w, layer, pos):
    _, rows, cols = w.shape
    tr = _row_tile(rows)

    def body(pos_ref, w_ref, o_ref):
        o_ref[...] = w_ref[...].astype(BF16)

    return pl.pallas_call(
        body, name=name,
        grid_spec=pltpu.PrefetchScalarGridSpec(
            num_scalar_prefetch=1, grid=(rows // tr,),
            in_specs=[pl.BlockSpec((None, tr, cols), lambda i, p: (layer, i, 0))],
            out_specs=pl.BlockSpec((None, tr, cols), lambda i, p: (p[0], i, 0))),
        out_shape=jax.ShapeDtypeStruct((N_CHIPS, rows, cols), BF16), compiler_params=_params("arbitrary"))(pos, w)


def _pair_sum(name, part, theirs, pos):
    _, half, cols = theirs.shape
    tr = _row_tile(half)
    nb = half // tr

    def body(pos_ref, a_ref, b_ref, o_ref):
        o_ref[...] = (a_ref[...].astype(F32) + b_ref[...].astype(F32)).astype(BF16)

    return pl.pallas_call(
        body, name=name,
        grid_spec=pltpu.PrefetchScalarGridSpec(
            num_scalar_prefetch=1, grid=(N_CHIPS, nb),
            in_specs=[pl.BlockSpec((None, tr, cols), lambda s, i, p: (s, p[1] * nb + i, 0)),
                      pl.BlockSpec((None, tr, cols), lambda s, i, p: (s, i, 0))],
            out_specs=pl.BlockSpec((None, tr, cols), lambda s, i, p: (s, i, 0))),
        out_shape=jax.ShapeDtypeStruct(theirs.shape, BF16),
        compiler_params=_params("arbitrary", "arbitrary"))(pos, part, theirs)


def _chip_sum(name, sums, got, pos, layer, full_shape, full=None):
    _, half, cols = sums.shape
    tr = _row_tile(half)
    nb = half // tr

    def body(pos_ref, s_ref, g_ref, *rest):
        out_ref = rest[-1]
        out_ref[...] = ((s_ref[...].astype(F32) + g_ref[0].astype(F32)) + g_ref[1].astype(F32)) + g_ref[2].astype(F32)

    in_specs = [pl.BlockSpec((None, tr, cols), lambda i, p: (p[0], i, 0)),
                pl.BlockSpec((3, tr, cols), lambda i, p: (0, i, 0))]
    args = [pos, sums, got]
    if full is not None:
        in_specs.append(ANY)
        args.append(full)
    return pl.pallas_call(
        body, name=name,
        grid_spec=pltpu.PrefetchScalarGridSpec(
            num_scalar_prefetch=1, grid=(nb,), in_specs=in_specs,
            out_specs=pl.BlockSpec((None, tr, cols), lambda i, p: (layer, p[1] * nb + i, 0))),
        out_shape=jax.ShapeDtypeStruct(full_shape, F32),
        input_output_aliases={3: 0} if full is not None else {},
        compiler_params=_params("arbitrary"))(*args)


def _all_reduce_small(name, packed):
    n_dev = 8

    def body(p_ref, o_ref, slots, send_sem, recv_sem):
        x, y, c, _ = _position()
        me = 4 * x + 2 * y + c

        def peer(k):
            return (1 - x if k & 4 else x, 1 - y if k & 2 else y, 1 - c if k & 1 else c)

        def logical(k):
            px, py, pc = peer(k)
            return 4 * px + 2 * py + pc

        slots[me] = p_ref[...]
        sends = [pltpu.make_async_remote_copy(
            src_ref=p_ref, dst_ref=slots.at[me], send_sem=send_sem.at[k], recv_sem=recv_sem.at[k],
            device_id=peer(k), device_id_type=MESH) for k in range(1, n_dev)]
        for cp in sends:
            cp.start()
        for k in range(1, n_dev):
            pltpu.make_async_remote_copy(
                src_ref=p_ref, dst_ref=slots.at[logical(k)], send_sem=send_sem.at[k], recv_sem=recv_sem.at[k],
                device_id=peer(k), device_id_type=MESH).wait_recv()
        for cp in sends:
            cp.wait_send()
        total = slots[0]
        for d in range(1, n_dev):
            total = total + slots[d]
        o_ref[...] = total

    vm = pl.BlockSpec(memory_space=pltpu.VMEM)
    return pl.pallas_call(
        body, name=name, in_specs=[vm], out_specs=vm, out_shape=jax.ShapeDtypeStruct(packed.shape, F32),
        scratch_shapes=[pltpu.VMEM((n_dev,) + packed.shape, F32), pltpu.SemaphoreType.DMA((n_dev,)),
                        pltpu.SemaphoreType.DMA((n_dev,))],
    )(packed)


def _adamw(name, w, g, m, v):
    rows, cols = w.shape
    tr = _row_tile(rows)

    def body(w_ref, g_ref, m_ref, v_ref, d_ref, mo_ref, vo_ref):
        gv = g_ref[...]
        m_new = ADAM_B1 * m_ref[...] + (1.0 - ADAM_B1) * gv
        v_new = ADAM_B2 * v_ref[...] + (1.0 - ADAM_B2) * (gv * gv)
        m_hat = m_new / (1.0 - ADAM_B1 ** ADAM_STEP)
        v_hat = v_new / (1.0 - ADAM_B2 ** ADAM_STEP)
        d_ref[...] = -ADAM_LR * (m_hat / (jnp.sqrt(v_hat) + ADAM_EPS) + ADAM_WD * w_ref[...])
        mo_ref[...] = m_new
        vo_ref[...] = v_new

    spec = pl.BlockSpec((tr, cols), lambda i: (i, 0))
    shape = jax.ShapeDtypeStruct((rows, cols), F32)
    return pl.pallas_call(body, name=name, grid=(rows // tr,), in_specs=[spec] * 4, out_specs=[spec] * 3,
                          out_shape=[shape] * 3, compiler_params=_params("parallel"))(w, g, m, v)


def _pack_small(tree, extra=None):
    pieces = [tree[n].reshape(-1).astype(F32) for n in SMALL]
    if extra is not None:
        pieces.append(extra.reshape(1).astype(F32))
    flat = jnp.concatenate(pieces)
    return jnp.pad(flat, (0, SMALL_ROWS * LANES - flat.shape[0])).reshape(SMALL_ROWS, LANES)


def _unpack_small(packed, like):
    flat = packed.reshape(-1)
    out, off = {}, 0
    for n in SMALL:
        size = int(np.prod(like[n].shape))
        out[n] = flat[off:off + size].reshape(like[n].shape)
        off += size
    return out


def kernel(x, ev_w_in, ev_g_cq, ev_w_uq, ev_g_ckv, ev_w_ukv, ev_w_out, od_w_qkv, od_rel_bias, od_w_out, g_mix, g_ffn, w_gate, w_up, w_down, g_final, loss_target, m_ev_w_in, m_ev_g_cq, m_ev_w_uq, m_ev_g_ckv, m_ev_w_ukv, m_ev_w_out, m_od_w_qkv, m_od_rel_bias, m_od_w_out, m_g_mix, m_g_ffn, m_w_gate, m_w_up, m_w_down, m_g_final, v_ev_w_in, v_ev_g_cq, v_ev_w_uq, v_ev_g_ckv, v_ev_w_ukv, v_ev_w_out, v_od_w_qkv, v_od_rel_bias, v_od_w_out, v_g_mix, v_g_ffn, v_w_gate, v_w_up, v_w_down, v_g_final):
    w = dict(ev_w_in=ev_w_in, ev_g_cq=ev_g_cq, ev_w_uq=ev_w_uq, ev_g_ckv=ev_g_ckv, ev_w_ukv=ev_w_ukv, ev_w_out=ev_w_out,
             od_w_qkv=od_w_qkv, od_rel_bias=od_rel_bias, od_w_out=od_w_out, g_mix=g_mix, g_ffn=g_ffn, w_gate=w_gate,
             w_up=w_up, w_down=w_down, g_final=g_final)
    m = dict(ev_w_in=m_ev_w_in, ev_g_cq=m_ev_g_cq, ev_w_uq=m_ev_w_uq, ev_g_ckv=m_ev_g_ckv, ev_w_ukv=m_ev_w_ukv,
             ev_w_out=m_ev_w_out, od_w_qkv=m_od_w_qkv, od_rel_bias=m_od_rel_bias, od_w_out=m_od_w_out, g_mix=m_g_mix,
             g_ffn=m_g_ffn, w_gate=m_w_gate, w_up=m_w_up, w_down=m_w_down, g_final=m_g_final)
    v = dict(ev_w_in=v_ev_w_in, ev_g_cq=v_ev_g_cq, ev_w_uq=v_ev_w_uq, ev_g_ckv=v_ev_g_ckv, ev_w_ukv=v_ev_w_ukv,
             ev_w_out=v_ev_w_out, od_w_qkv=v_od_w_qkv, od_rel_bias=v_od_rel_bias, od_w_out=v_od_w_out, g_mix=v_g_mix,
             g_ffn=v_g_ffn, w_gate=v_w_gate, w_up=v_w_up, w_down=v_w_down, g_final=v_g_final)
    flat2d = lambda a: a.reshape(-1, a.shape[-1])
    for tree in (w, m, v):
        for n in TRANSPOSED:
            tree[n] = jnp.swapaxes(tree[n], 1, 2)

    pos = jnp.stack([2 * lax.axis_index("x") + lax.axis_index("y"), lax.axis_index("c")]).astype(jnp.int32)

    slots = {part: _cast_into_slot("cast_" + part, w[n], layer, pos) for part, n, layer in GRAD_PARTS}
    ex = _Exchanges(slots, pos, {n: w[n].shape for n in BIG})

    loss_local, grad_x, small = _local_step(x[0], loss_target[0], {n: w[n] for n in SMALL}, ex)

    grads = ex.finish()
    small_sum = _all_reduce_small("small_sum", _pack_small(small, loss_local[0, 0]))
    grads.update(_unpack_small(small_sum, w))

    delta, new_m, new_v = {}, {}, {}
    for n in BIG:
        d_, m_, v_ = _adamw("adamw_" + n, flat2d(w[n]), flat2d(grads[n]), flat2d(m[n]), flat2d(v[n]))
        delta[n], new_m[n], new_v[n] = d_.reshape(w[n].shape), m_.reshape(w[n].shape), v_.reshape(w[n].shape)
    d_, m_, v_ = _adamw("adamw_small", _pack_small(w), small_sum, _pack_small(m), _pack_small(v))
    delta.update(_unpack_small(d_, w))
    new_m.update(_unpack_small(m_, w))
    new_v.update(_unpack_small(v_, w))
    for tree in (grads, delta, new_m, new_v):
        for n in TRANSPOSED:
            tree[n] = jnp.swapaxes(tree[n], 1, 2)

    loss = small_sum.reshape(-1)[SMALL_SIZE]
    return (loss, grad_x[None], *[grads[n] for n in WEIGHTS], *[delta[n] for n in WEIGHTS],
            *[new_m[n] for n in WEIGHTS], *[new_v[n] for n in WEIGHTS])
```

```python
import functools

import jax
import jax.numpy as jnp
import numpy as np
from jax import lax
from jax.experimental import pallas as pl
from jax.experimental.pallas import tpu as pltpu

F32 = jnp.float32
BF16 = jnp.bfloat16

S = 2048
D = 1024
CHUNK = 64
MLA_H, MLA_NOPE, MLA_ROPE, MLA_V = 8, 64, 32, 64
Q_LORA, KV_LORA = 384, 256
ROPE_THETA = 10000.0
SB_H, SB_DIM = 8, 64
C_H, C_DIM = 16, 64
LEFT_CHUNKS = 8
REL_CLIP = 256
D_FF = 2816
EVEN_IN = 2208
RMS_EPS = 1e-6
ADAM_LR, ADAM_B1, ADAM_B2, ADAM_EPS, ADAM_WD, ADAM_STEP = 0.001, 0.9, 0.999, 1e-08, 0.01, 10

N_CHIPS = 4
FF_SHARD = D_FF // N_CHIPS
SCALE_A = (MLA_NOPE + MLA_ROPE) ** -0.5
SCALE_B = SB_DIM ** -0.5
SCALE_C = C_DIM ** -0.5
NEG = -1e30

LANES = 128
VMEM_LIMIT_BYTES = 56 * 1024 * 1024
TM = 512
QB = 512
BQ = 256
SB_ROW_SPLIT = 1
SB_PAIRS = 1

P_CQ, P_CKV, P_QB, P_KB, P_VB, P_KR = 0, 512, 768, 1280, 1792, 2304
P_IN = 2432
KR_LANE = 64
BAND_W = BQ + LEFT_CHUNKS * CHUNK
BAND_PAD = 512
TOEP_W = 1024


def _params(*sem):
    return pltpu.CompilerParams(dimension_semantics=sem, vmem_limit_bytes=VMEM_LIMIT_BYTES)


MESH = pl.DeviceIdType.MESH
ANY = pl.BlockSpec(memory_space=pl.ANY)


def _position():
    x, y, c = lax.axis_index("x"), lax.axis_index("y"), lax.axis_index("c")
    other_chips = [(1 - x, y), (x, 1 - y), (1 - x, 1 - y)]
    return x, y, c, other_chips


def _half_rows(c, half):
    return pl.ds(pl.multiple_of(c * half, 16), half)


def _remote(ref_src, ref_dst, send, recv, k, device):
    return pltpu.make_async_remote_copy(src_ref=ref_src, dst_ref=ref_dst, send_sem=send.at[k], recv_sem=recv.at[k],
                                        device_id=device, device_id_type=MESH)


class _Carry:
    def __init__(self):
        self.operands, self.aliased, self.fresh = [], [], []
        self.n_sems = 0
        self.starts, self.finishes, self.on_done = [], [], []

    def operand(self, arr, aliased):
        for i, a in enumerate(self.operands):
            if a is arr:
                return i
        self.operands.append(arr)
        self.aliased.append(aliased)
        return len(self.operands) - 1

    def result(self, shape, dtype):
        self.fresh.append(jax.ShapeDtypeStruct(shape, dtype))
        return len(self.fresh) - 1

    def sems(self, k):
        base = self.n_sems
        self.n_sems += k
        return base

    def done(self, results):
        aliased, fresh = results
        for f in self.on_done:
            f(aliased, fresh)


def _carrier_call(body, *, name, grid, in_specs, out_specs, out_shape, args, sem, scratch_shapes=(), carry=None):
    in_specs, out_specs, out_shape, scratch = list(in_specs), list(out_specs), list(out_shape), list(scratch_shapes)
    if carry is None:
        res = pl.pallas_call(body, name=name, grid=grid, in_specs=in_specs, out_specs=out_specs, out_shape=out_shape,
                             scratch_shapes=scratch, compiler_params=_params(*sem))(*args)
        return list(res), None
    ops = carry.operands
    alias_idx = [i for i, a in enumerate(carry.aliased) if a]
    c_shapes = [jax.ShapeDtypeStruct(ops[i].shape, ops[i].dtype) for i in alias_idx] + carry.fresh
    n_in, n_out, n_scr = len(args), len(out_shape), len(scratch)

    def wrapped(*refs):
        ins, c_ins = refs[:n_in], refs[n_in:n_in + len(ops)]
        o0 = n_in + len(ops)
        outs, c_outs = refs[o0:o0 + n_out], refs[o0 + n_out:o0 + n_out + len(c_shapes)]
        s0 = o0 + n_out + len(c_shapes)
        scr, send, recv = refs[s0:s0 + n_scr], refs[s0 + n_scr], refs[s0 + n_scr + 1]
        use = list(c_ins)
        for k, i in enumerate(alias_idx):
            use[i] = c_outs[k]
        fresh = c_outs[len(alias_idx):]

        def run(steps):
            for step in steps:
                step(use, fresh, send, recv)

        if not grid:
            run(carry.starts)
            if body is not None:
                body(*ins, *outs, *scr)
            run(carry.finishes)
            return
        ids = [pl.program_id(a) for a in range(len(grid))]
        first = functools.reduce(jnp.logical_and, [i == 0 for i in ids])
        last = functools.reduce(jnp.logical_and, [i == g - 1 for i, g in zip(ids, grid)])

        @pl.when(first)
        def _():
            run(carry.starts)

        body(*ins, *outs, *scr)

        @pl.when(last)
        def _():
            run(carry.finishes)

    res = pl.pallas_call(
        wrapped, name=name, grid=grid, in_specs=in_specs + [ANY] * len(ops), out_specs=out_specs + [ANY] * len(c_shapes),
        out_shape=out_shape + c_shapes,
        scratch_shapes=scratch + [pltpu.SemaphoreType.DMA((carry.n_sems,)), pltpu.SemaphoreType.DMA((carry.n_sems,))],
        input_output_aliases={n_in + i: n_out + k for k, i in enumerate(alias_idx)},
        compiler_params=_params(*(("arbitrary",) * len(grid))),
    )(*args, *ops)
    res = list(res)
    c_res = res[n_out:]
    return res[:n_out], ({i: c_res[k] for k, i in enumerate(alias_idx)}, c_res[len(alias_idx):])


_DIMS = {"nn": (((1,), (0,)), ((), ())), "nt": (((1,), (1,)), ((), ())), "tn": (((0,), (0,)), ((), ()))}


def _dot(a, b, kind="nn"):
    return lax.dot_general(a, b, _DIMS[kind], preferred_element_type=F32)


def _iota(shape, dim):
    return lax.broadcasted_iota(jnp.int32, shape, dim)


def _sigmoid(x):
    return 1.0 / (1.0 + jnp.exp(-x))


def _softplus(x):
    return jnp.maximum(x, 0.0) + jnp.log(1.0 + jnp.exp(-jnp.abs(x)))


def _split_dot(x, tri):
    hi = x.astype(BF16)
    lo = (x - hi.astype(F32)).astype(BF16)
    return _dot(hi, tri) + _dot(lo, tri)


def _mm(name, a, b, *, kind, grid, a_spec, b_spec, o_spec, out_shape, out_dtype, acc_shape, resid=None, r_spec=None,
        carry=None):
    nk = grid[-1]
    has_r = resid is not None

    def body(*refs):
        a_ref, b_ref = refs[0], refs[1]
        r_ref = refs[2] if has_r else None
        o_ref = refs[2 + has_r]
        part = _dot(a_ref[...].astype(BF16), b_ref[...].astype(BF16), kind)

        def finish(total):
            if has_r:
                total = total + r_ref[...].astype(F32)
            o_ref[...] = total.astype(out_dtype)

        if nk == 1:
            finish(part)
        else:
            acc_ref = refs[3 + has_r]
            k = pl.program_id(len(grid) - 1)

            @pl.when(k == 0)
            def _():
                acc_ref[...] = part

            @pl.when(k > 0)
            def _():
                acc_ref[...] += part

            @pl.when(k == nk - 1)
            def _():
                finish(acc_ref[...])

    in_specs = [a_spec, b_spec] + ([r_spec] if has_r else [])
    args = (a, b) + ((resid,) if has_r else ())
    sem = ("parallel",) * (len(grid) - 1) + ("arbitrary",)
    res, copies = _carrier_call(
        body, name=name, grid=grid, in_specs=in_specs, out_specs=[o_spec],
        out_shape=[jax.ShapeDtypeStruct(out_shape, out_dtype)],
        scratch_shapes=[pltpu.VMEM(acc_shape, F32)] if nk > 1 else [], args=args, sem=sem, carry=carry)
    if carry is not None:
        carry.done(copies)
    return res[0]


def _rms_fwd(name, x, g, col_block=0):
    c = g.shape[1]

    def body(x_ref, g_ref, u_ref):
        xv = x_ref[...]
        r = lax.rsqrt(jnp.mean(xv * xv, axis=-1, keepdims=True) + RMS_EPS)
        u_ref[...] = (xv * r * g_ref[...]).astype(BF16)

    return pl.pallas_call(
        body, name=name, grid=(S // TM,),
        in_specs=[pl.BlockSpec((TM, c), lambda i: (i, col_block)), pl.BlockSpec((1, c), lambda i: (0, 0))],
        out_specs=pl.BlockSpec((TM, c), lambda i: (i, 0)),
        out_shape=jax.ShapeDtypeStruct((S, c), BF16),
        compiler_params=_params("parallel"),
    )(x, g)


def _rms_bwd(name, dy, x, g, resid, carry=None):
    def body(dy_ref, x_ref, g_ref, r_ref, dx_ref, dg_ref):
        i = pl.program_id(0)
        xv = x_ref[...]
        r = lax.rsqrt(jnp.mean(xv * xv, axis=-1, keepdims=True) + RMS_EPS)
        xh = xv * r
        dyv = dy_ref[...]
        dxh = dyv * g_ref[...]
        dx_ref[...] = r_ref[...] + r * (dxh - xh * jnp.mean(dxh * xh, axis=-1, keepdims=True))
        part = jnp.sum(dyv * xh, axis=0, keepdims=True)

        @pl.when(i == 0)
        def _():
            dg_ref[...] = part

        @pl.when(i > 0)
        def _():
            dg_ref[...] += part

    row = pl.BlockSpec((TM, D), lambda i: (i, 0))
    vec = pl.BlockSpec((1, D), lambda i: (0, 0))
    res, copies = _carrier_call(
        body, name=name, grid=(S // TM,), in_specs=[row, row, vec, row], out_specs=[row, vec],
        out_shape=[jax.ShapeDtypeStruct((S, D), F32), jax.ShapeDtypeStruct((1, D), F32)],
        args=(dy, x, g, resid), sem=("arbitrary",), carry=carry)
    if carry is not None:
        carry.done(copies)
    return res


def _loss_bwd(name, h, g, tgt):
    def body(h_ref, g_ref, t_ref, loss_ref, dh_ref, dg_ref):
        i = pl.program_id(0)
        xv = h_ref[...]
        gv = g_ref[...]
        r = lax.rsqrt(jnp.mean(xv * xv, axis=-1, keepdims=True) + RMS_EPS)
        xh = xv * r
        diff = xh * gv - t_ref[...]
        part_loss = 0.5 * jnp.sum(jnp.sum(diff * diff, axis=-1, keepdims=True) * (1.0 / D), axis=0, keepdims=True)
        dy = diff * (1.0 / D)
        dxh = dy * gv
        dh_ref[...] = r * (dxh - xh * jnp.mean(dxh * xh, axis=-1, keepdims=True))
        part_g = jnp.sum(dy * xh, axis=0, keepdims=True)

        @pl.when(i == 0)
        def _():
            dg_ref[...] = part_g
            loss_ref[...] = jnp.broadcast_to(part_loss, (1, LANES))

        @pl.when(i > 0)
        def _():
            dg_ref[...] += part_g
            loss_ref[...] += jnp.broadcast_to(part_loss, (1, LANES))

    row = pl.BlockSpec((TM, D), lambda i: (i, 0))
    vec = pl.BlockSpec((1, D), lambda i: (0, 0))
    return pl.pallas_call(
        body, name=name, grid=(S // TM,), in_specs=[row, vec, row],
        out_specs=[pl.BlockSpec((1, LANES), lambda i: (0, 0)), row, vec],
        out_shape=[jax.ShapeDtypeStruct((1, LANES), F32), jax.ShapeDtypeStruct((S, D), F32),
                   jax.ShapeDtypeStruct((1, D), F32)],
        compiler_params=_params("arbitrary"),
    )(h, g, tgt)


def _ffn_fwd(name, h, g, wg, wu, wd, carry=None):
    def body(h_ref, g_ref, wg_ref, wu_ref, wd_ref, o_ref, gate_ref, up_ref, u_scr):
        s = pl.program_id(1)

        @pl.when(s == 0)
        def _():
            xv = h_ref[...]
            r = lax.rsqrt(jnp.mean(xv * xv, axis=-1, keepdims=True) + RMS_EPS)
            u_scr[...] = (xv * r * g_ref[...]).astype(BF16)
            o_ref[...] = xv

        u = u_scr[...]
        gate = _dot(u, wg_ref[...], "nt")
        up = _dot(u, wu_ref[...], "nt")
        act = gate * _sigmoid(gate) * up
        o_ref[...] += _dot(act.astype(BF16), wd_ref[...])
        gate_ref[...] = gate.astype(BF16)
        up_ref[...] = up.astype(BF16)

    row = pl.BlockSpec((TM, D), lambda i, s: (i, 0))
    hid = pl.BlockSpec((None, TM, FF_SHARD), lambda i, s: (s, i, 0))
    return _carrier_call(
        body, name=name, grid=(S // TM, N_CHIPS),
        in_specs=[row, pl.BlockSpec((1, D), lambda i, s: (0, 0))]
        + [pl.BlockSpec((None, FF_SHARD, D), lambda i, s: (s, 0, 0))] * 3,
        out_specs=[row, hid, hid],
        out_shape=[jax.ShapeDtypeStruct((S, D), F32), jax.ShapeDtypeStruct((N_CHIPS, S, FF_SHARD), BF16),
                   jax.ShapeDtypeStruct((N_CHIPS, S, FF_SHARD), BF16)],
        scratch_shapes=[pltpu.VMEM((TM, D), BF16)], args=(h, g, wg, wu, wd), sem=("parallel", "arbitrary"), carry=carry)


def _ffn_bwd(name, dh, h, g, gate, up, wg, wu, wd):
    def body(dh_ref, h_ref, g_ref, gate_ref, up_ref, wg_ref, wu_ref, wd_ref,
             dhin_ref, dg_ref, u_ref, dgate_ref, dup_ref, act_ref, dhb_scr, du_scr):
        i = pl.program_id(0)
        s = pl.program_id(1)

        @pl.when(s == 0)
        def _():
            xv = h_ref[...]
            r = lax.rsqrt(jnp.mean(xv * xv, axis=-1, keepdims=True) + RMS_EPS)
            u_ref[...] = (xv * r * g_ref[...]).astype(BF16)
            dhb_scr[...] = dh_ref[...].astype(BF16)
            du_scr[...] = jnp.zeros_like(du_scr)

        dact = _dot(dhb_scr[...], wd_ref[...], "nt")
        gv = gate_ref[...].astype(F32)
        uv = up_ref[...].astype(F32)
        sig = _sigmoid(gv)
        sil = gv * sig
        dup = dact * sil
        dgate = dact * uv * (sig * (1.0 + gv * (1.0 - sig)))
        dgb = dgate.astype(BF16)
        dub = dup.astype(BF16)
        act_ref[...] = (sil * uv).astype(BF16)
        dgate_ref[...] = dgb
        dup_ref[...] = dub
        du_scr[...] += _dot(dgb, wg_ref[...]) + _dot(dub, wu_ref[...])

        @pl.when(s == N_CHIPS - 1)
        def _():
            xv = h_ref[...]
            r = lax.rsqrt(jnp.mean(xv * xv, axis=-1, keepdims=True) + RMS_EPS)
            xh = xv * r
            du = du_scr[...]
            dxh = du * g_ref[...]
            dhin_ref[...] = dh_ref[...] + r * (dxh - xh * jnp.mean(dxh * xh, axis=-1, keepdims=True))
            part = jnp.sum(du * xh, axis=0, keepdims=True)

            @pl.when(i == 0)
            def _():
                dg_ref[...] = part

            @pl.when(i > 0)
            def _():
                dg_ref[...] += part

    row = pl.BlockSpec((TM, D), lambda i, s: (i, 0))
    vec = pl.BlockSpec((1, D), lambda i, s: (0, 0))
    hid = pl.BlockSpec((None, TM, FF_SHARD), lambda i, s: (s, i, 0))
    hid_shape = jax.ShapeDtypeStruct((N_CHIPS, S, FF_SHARD), BF16)
    return pl.pallas_call(
        body, name=name, grid=(S // TM, N_CHIPS),
        in_specs=[row, row, vec, hid, hid] + [pl.BlockSpec((None, FF_SHARD, D), lambda i, s: (s, 0, 0))] * 3,
        out_specs=[row, vec, row, hid, hid, hid],
        out_shape=[jax.ShapeDtypeStruct((S, D), F32), jax.ShapeDtypeStruct((1, D), F32),
                   jax.ShapeDtypeStruct((S, D), BF16), hid_shape, hid_shape, hid_shape],
        scratch_shapes=[pltpu.VMEM((TM, D), BF16), pltpu.VMEM((TM, D), F32)],
        compiler_params=_params("arbitrary", "arbitrary"),
    )(dh, h, g, gate, up, wg, wu, wd)


def _ffn_wgrads(name, u, dgate, dup, act, dh):
    nk = S // TM

    def body(u_ref, dh_ref, dgate_ref, dup_ref, act_ref, dg_ref, du_ref, dd_ref, acc_g, acc_u, acc_d):
        k = pl.program_id(1)
        u = u_ref[...]
        parts = (_dot(dgate_ref[...], u, "tn"), _dot(dup_ref[...], u, "tn"),
                 _dot(act_ref[...], dh_ref[...].astype(BF16), "tn"))
        accs = (acc_g, acc_u, acc_d)

        @pl.when(k == 0)
        def _():
            for acc, part in zip(accs, parts):
                acc[...] = part

        @pl.when(k > 0)
        def _():
            for acc, part in zip(accs, parts):
                acc[...] += part

        @pl.when(k == nk - 1)
        def _():
            for out, acc in zip((dg_ref, du_ref, dd_ref), accs):
                out[...] = acc[...].astype(BF16)

    tok = pl.BlockSpec((TM, D), lambda s, k: (k, 0))
    hid = pl.BlockSpec((None, TM, FF_SHARD), lambda s, k: (s, k, 0))
    out = pl.BlockSpec((None, FF_SHARD, D), lambda s, k: (s, 0, 0))
    shape = jax.ShapeDtypeStruct((N_CHIPS, FF_SHARD, D), BF16)
    return pl.pallas_call(
        body, name=name, grid=(N_CHIPS, nk), in_specs=[tok, tok, hid, hid, hid], out_specs=[out, out, out],
        out_shape=[shape, shape, shape], scratch_shapes=[pltpu.VMEM((FF_SHARD, D), F32)] * 3,
        compiler_params=_params("parallel", "arbitrary"))(u, dh, dgate, dup, act)


def _rope_tables():
    pos = jnp.arange(S, dtype=F32)
    inv = ROPE_THETA ** (-jnp.arange(0, MLA_ROPE, 2, dtype=F32) / MLA_ROPE)
    ang = pos[:, None] * inv[None, :]
    half = MLA_ROPE // 2
    cos = jnp.cos(ang)
    sin = jnp.sin(ang)
    one = jnp.ones((S, KR_LANE), F32)
    zero = jnp.zeros((S, KR_LANE), F32)
    tail_one = jnp.ones((S, LANES - KR_LANE - MLA_ROPE), F32)
    tail_zero = jnp.zeros((S, LANES - KR_LANE - MLA_ROPE), F32)
    cos_t = jnp.concatenate([one, cos, cos, tail_one], axis=1)
    sin_t = jnp.concatenate([zero, -sin, sin, tail_zero], axis=1)
    assert cos_t.shape == (S, LANES) and half * 2 == MLA_ROPE
    return cos_t, sin_t


def _rope(x, cos_t, sin_t, sign):
    n = x.shape[1] // LANES
    half = MLA_ROPE // 2
    lane = _iota(x.shape, 1) & (LANES - 1)
    first = (lane >= KR_LANE) & (lane < KR_LANE + half)
    swapped = jnp.where(first, pltpu.roll(x, x.shape[1] - half, 1), pltpu.roll(x, half, 1))
    c = jnp.tile(cos_t, (1, n)) if n > 1 else cos_t
    s = jnp.tile(sin_t, (1, n)) if n > 1 else sin_t
    return x * c + swapped * (s * sign)


def _mla_prep_fwd(name, proj, g_cq, g_ckv, w_uq, w_uk, w_uv, cos_t, sin_t):
    nh = MLA_H * LANES

    def body(cq_ref, ckv_ref, kr_ref, gq_ref, gkv_ref, wq_ref, wk_ref, wv_ref, cos_ref, sin_ref,
             qa_ref, ka_ref, va_ref):
        cos_v, sin_v = cos_ref[...], sin_ref[...]
        cq = cq_ref[...]
        r = lax.rsqrt(jnp.mean(cq * cq, axis=-1, keepdims=True) + RMS_EPS)
        cqn = (cq * r * gq_ref[...]).astype(BF16)
        qa_ref[...] = _rope(_dot(cqn, wq_ref[...]), cos_v, sin_v, 1.0).astype(BF16)
        ckv = ckv_ref[...]
        r = lax.rsqrt(jnp.mean(ckv * ckv, axis=-1, keepdims=True) + RMS_EPS)
        ckvn = (ckv * r * gkv_ref[...]).astype(BF16)
        lane = _iota((TM, LANES), 1)
        rot = (lane >= KR_LANE) & (lane < KR_LANE + MLA_ROPE)
        kr = jnp.where(rot, _rope(kr_ref[...], cos_v, sin_v, 1.0), 0.0)
        ka_ref[...] = (_dot(ckvn, wk_ref[...]) + jnp.tile(kr, (1, MLA_H))).astype(BF16)
        va_ref[...] = _dot(ckvn, wv_ref[...]).astype(BF16)

    full = lambda shape: pl.BlockSpec(shape, lambda i: (0, 0))
    return pl.pallas_call(
        body, name=name, grid=(S // TM,),
        in_specs=[pl.BlockSpec((TM, Q_LORA), lambda i: (i, P_CQ // Q_LORA)),
                  pl.BlockSpec((TM, KV_LORA), lambda i: (i, P_CKV // KV_LORA)),
                  pl.BlockSpec((TM, LANES), lambda i: (i, P_KR // LANES)),
                  full((1, Q_LORA)), full((1, KV_LORA)), full((Q_LORA, nh)), full((KV_LORA, nh)),
                  full((KV_LORA, MLA_H * MLA_V)),
                  pl.BlockSpec((TM, LANES), lambda i: (i, 0)), pl.BlockSpec((TM, LANES), lambda i: (i, 0))],
        out_specs=[pl.BlockSpec((TM, nh), lambda i: (i, 0)), pl.BlockSpec((TM, nh), lambda i: (i, 0)),
                   pl.BlockSpec((TM, MLA_H * MLA_V), lambda i: (i, 0))],
        out_shape=[jax.ShapeDtypeStruct((S, nh), BF16), jax.ShapeDtypeStruct((S, nh), BF16),
                   jax.ShapeDtypeStruct((S, MLA_H * MLA_V), BF16)],
        compiler_params=_params("parallel"),
    )(proj, proj, proj, g_cq, g_ckv, w_uq, w_uk, w_uv, cos_t, sin_t)


def _mla_prep_bwd(name, dqa, dka, dva, proj, g_cq, g_ckv, w_uq, w_uk, w_uv, cos_t, sin_t):
    nh = MLA_H * LANES

    def body(dqa_ref, dka_ref, dva_ref, cq_ref, ckv_ref, gq_ref, gkv_ref, wq_ref, wk_ref, wv_ref, cos_ref, sin_ref,
             dcq_ref, dckv_ref, dkr_ref, dwq_ref, dwk_ref, dwv_ref, dgq_ref, dgkv_ref):
        i = pl.program_id(0)
        cos_v, sin_v = cos_ref[...], sin_ref[...]

        def norm_bwd(x, g, dn):
            r = lax.rsqrt(jnp.mean(x * x, axis=-1, keepdims=True) + RMS_EPS)
            xh = x * r
            dxh = dn * g
            dx = r * (dxh - xh * jnp.mean(dxh * xh, axis=-1, keepdims=True))
            return dx, jnp.sum(dn * xh, axis=0, keepdims=True), (xh * g).astype(BF16)

        dq = _rope(dqa_ref[...], cos_v, sin_v, -1.0).astype(BF16)
        dcqn = _dot(dq, wq_ref[...], "nt")
        dcq, dgq, cqn = norm_bwd(cq_ref[...], gq_ref[...], dcqn)
        dcq_ref[...] = dcq.astype(BF16)
        dwq = _dot(cqn, dq, "tn")

        dka = dka_ref[...]
        dkab = dka.astype(BF16)
        dvab = dva_ref[...].astype(BF16)
        dckvn = _dot(dkab, wk_ref[...], "nt") + _dot(dvab, wv_ref[...], "nt")
        dckv, dgkv, ckvn = norm_bwd(ckv_ref[...], gkv_ref[...], dckvn)
        dckv_ref[...] = dckv.astype(BF16)
        dwk = _dot(ckvn, dkab, "tn")
        dwv = _dot(ckvn, dvab, "tn")

        fold = dka[:, 0:LANES]
        for hh in range(1, MLA_H):
            fold = fold + dka[:, hh * LANES:(hh + 1) * LANES]
        lane = _iota((TM, LANES), 1)
        rot = (lane >= KR_LANE) & (lane < KR_LANE + MLA_ROPE)
        dkr = _rope(jnp.where(rot, fold, 0.0), cos_v, sin_v, -1.0)
        dkr_ref[...] = jnp.where(rot, dkr, 0.0).astype(BF16)

        @pl.when(i == 0)
        def _():
            dwq_ref[...] = dwq
            dwk_ref[...] = dwk
            dwv_ref[...] = dwv
            dgq_ref[...] = dgq
            dgkv_ref[...] = dgkv

        @pl.when(i > 0)
        def _():
            dwq_ref[...] += dwq
            dwk_ref[...] += dwk
            dwv_ref[...] += dwv
            dgq_ref[...] += dgq
            dgkv_ref[...] += dgkv

    full = lambda shape: pl.BlockSpec(shape, lambda i: (0, 0))
    rows = lambda c: pl.BlockSpec((TM, c), lambda i: (i, 0))
    nv = MLA_H * MLA_V
    return pl.pallas_call(
        body, name=name, grid=(S // TM,),
        in_specs=[rows(nh), rows(nh), rows(nv),
                  pl.BlockSpec((TM, Q_LORA), lambda i: (i, P_CQ // Q_LORA)),
                  pl.BlockSpec((TM, KV_LORA), lambda i: (i, P_CKV // KV_LORA)),
                  full((1, Q_LORA)), full((1, KV_LORA)), full((Q_LORA, nh)), full((KV_LORA, nh)), full((KV_LORA, nv)),
                  rows(LANES), rows(LANES)],
        out_specs=[rows(Q_LORA), rows(KV_LORA), rows(LANES), full((Q_LORA, nh)), full((KV_LORA, nh)),
                   full((KV_LORA, nv)), full((1, Q_LORA)), full((1, KV_LORA))],
        out_shape=[jax.ShapeDtypeStruct((S, Q_LORA), BF16), jax.ShapeDtypeStruct((S, KV_LORA), BF16),
                   jax.ShapeDtypeStruct((S, LANES), BF16), jax.ShapeDtypeStruct((Q_LORA, nh), F32),
                   jax.ShapeDtypeStruct((KV_LORA, nh), F32), jax.ShapeDtypeStruct((KV_LORA, nv), F32),
                   jax.ShapeDtypeStruct((1, Q_LORA), F32), jax.ShapeDtypeStruct((1, KV_LORA), F32)],
        compiler_params=_params("arbitrary"),
    )(dqa, dka, dva, proj, proj, g_cq, g_ckv, w_uq, w_uk, w_uv, cos_t, sin_t)


def _head_masks(dtype):
    lane = _iota((1, LANES), 1)
    return (lane < 64).astype(dtype), (lane >= 64).astype(dtype)


def _mla_fwd(name, qa, ka, va, carry=None):
    def body(q_ref, k_ref, v_ref, o_ref, lse_ref):
        m0b, m1b = _head_masks(BF16)
        lane = _iota((QB, LANES), 1)
        left = lane < 64

        def qblock(i, _):
            r0 = pl.multiple_of(i * QB, QB)
            qs = [q_ref[pl.ds(r0, QB), hh * LANES:(hh + 1) * LANES] for hh in range(2)]
            rowc = lax.shift_right_logical(r0 + _iota((QB, QB), 0), 6)

            def kv(kb, carry):
                ms, ls, acc = carry
                c0 = pl.multiple_of(kb * QB, QB)
                v = v_ref[pl.ds(c0, QB), :]
                ok = lax.shift_right_logical(c0 + _iota((QB, QB), 1), 6) <= rowc
                new_m, new_l, alphas = [], [], []
                pv = None
                for hh in range(2):
                    k = k_ref[pl.ds(c0, QB), hh * LANES:(hh + 1) * LANES]
                    s = jnp.where(ok, _dot(qs[hh], k, "nt") * SCALE_A, NEG)
                    mn = jnp.maximum(ms[hh], jnp.max(s, axis=-1, keepdims=True))
                    p = jnp.exp(s - mn)
                    a = jnp.exp(ms[hh] - mn)
                    new_m.append(mn)
                    new_l.append(a * ls[hh] + jnp.sum(p, axis=-1, keepdims=True))
                    alphas.append(a)
                    part = _dot(p.astype(BF16), v * (m0b if hh == 0 else m1b))
                    pv = part if pv is None else pv + part
                acc = acc * jnp.where(left, alphas[0], alphas[1]) + pv
                return tuple(new_m), tuple(new_l), acc

            init = ((jnp.full((QB, 1), NEG, F32),) * 2, (jnp.zeros((QB, 1), F32),) * 2, jnp.zeros((QB, LANES), F32))
            ms, ls, acc = lax.fori_loop(0, i + 1, kv, init)
            o_ref[pl.ds(r0, QB), :] = acc * jnp.where(left, 1.0 / ls[0], 1.0 / ls[1])
            lse_ref[pl.ds(r0, QB), :] = jnp.where(left, ms[0] + jnp.log(ls[0]), ms[1] + jnp.log(ls[1]))
            return 0

        lax.fori_loop(0, S // QB, qblock, 0)

    pair = lambda w: pl.BlockSpec((S, w), lambda p: (0, p))
    return _carrier_call(
        body, name=name, grid=(MLA_H // 2,), in_specs=[pair(2 * LANES), pair(2 * LANES), pair(LANES)],
        out_specs=[pair(LANES), pair(LANES)],
        out_shape=[jax.ShapeDtypeStruct((S, MLA_H * MLA_V), F32), jax.ShapeDtypeStruct((S, MLA_H * MLA_V), F32)],
        args=(qa, ka, va), sem=("parallel",), carry=carry)


def _mla_bwd(name, qa, ka, va, o, lse, do, do_block0, carry=None):
    def body(q_ref, k_ref, v_ref, o_ref, lse_ref, do_ref, dq_ref, dk_ref, dv_ref):
        m0f, m1f = _head_masks(F32)
        m0b, m1b = _head_masks(BF16)
        dk_ref[...] = jnp.zeros_like(dk_ref)
        dv_ref[...] = jnp.zeros_like(dv_ref)

        def qblock(i, _):
            r0 = pl.multiple_of(i * QB, QB)
            rows = pl.ds(r0, QB)
            do_f = do_ref[rows, :]
            prod = do_f * o_ref[rows, :]
            deltas = [jnp.sum(prod * m0f, axis=-1, keepdims=True), jnp.sum(prod * m1f, axis=-1, keepdims=True)]
            lse_v = lse_ref[rows, :]
            lses = [lse_v[:, 0:1], lse_v[:, 64:65]]
            dob = do_f.astype(BF16)
            dos = [dob * m0b, dob * m1b]
            qs = [q_ref[rows, hh * LANES:(hh + 1) * LANES] for hh in range(2)]
            rowc = lax.shift_right_logical(r0 + _iota((QB, QB), 0), 6)

            def kv(kb, dqs):
                c0 = pl.multiple_of(kb * QB, QB)
                cols = pl.ds(c0, QB)
                v = v_ref[cols, :]
                ok = lax.shift_right_logical(c0 + _iota((QB, QB), 1), 6) <= rowc
                out = []
                dv = None
                for hh in range(2):
                    k = k_ref[cols, hh * LANES:(hh + 1) * LANES]
                    s = _dot(qs[hh], k, "nt") * SCALE_A
                    p = jnp.where(ok, jnp.exp(s - lses[hh]), 0.0)
                    dp = _dot(dos[hh], v, "nt")
                    ds = (p * (dp - deltas[hh]) * SCALE_A).astype(BF16)
                    out.append(dqs[hh] + _dot(ds, k))
                    dk_ref[cols, hh * LANES:(hh + 1) * LANES] += _dot(ds, qs[hh], "tn")
                    part = _dot(p.astype(BF16), dos[hh], "tn")
                    dv = part if dv is None else dv + part
                dv_ref[cols, :] += dv
                return tuple(out)

            dqs = lax.fori_loop(0, i + 1, kv, (jnp.zeros((QB, LANES), F32),) * 2)
            for hh in range(2):
                dq_ref[rows, hh * LANES:(hh + 1) * LANES] = dqs[hh]
            return 0

        lax.fori_loop(0, S // QB, qblock, 0)

    pair = lambda w: pl.BlockSpec((S, w), lambda p: (0, p))
    return _carrier_call(
        body, name=name, grid=(MLA_H // 2,),
        in_specs=[pair(2 * LANES), pair(2 * LANES), pair(LANES), pair(LANES), pair(LANES),
                  pl.BlockSpec((S, LANES), lambda p: (0, do_block0 + p))],
        out_specs=[pair(2 * LANES), pair(2 * LANES), pair(LANES)],
        out_shape=[jax.ShapeDtypeStruct((S, MLA_H * LANES), F32), jax.ShapeDtypeStruct((S, MLA_H * LANES), F32),
                   jax.ShapeDtypeStruct((S, MLA_H * MLA_V), F32)],
        args=(qa, ka, va, o, lse, do), sem=("parallel",), carry=carry)


def _sb_weights(q_h, k, c, before, tri_suffix):
    z = _dot(q_h, k, "nt") * SCALE_B
    sp = _softplus(z)
    log_keep = jnp.where(before, -sp, 0.0)
    log_between = _split_dot(log_keep, tri_suffix) + c
    w = jnp.where(before, jnp.exp(z - sp + log_between), 0.0)
    return w, jnp.exp(z - sp), jnp.sum(log_keep, axis=-1, keepdims=True)


def _sb_fwd(name, proj, carry=None):
    def body(q_ref, k_ref, v_ref, o_ref):
        m0b, m1b = _head_masks(BF16)
        tri_suffix = (_iota((QB, QB), 0) > _iota((QB, QB), 1)).astype(BF16)

        def qblock(i, _):
            r0 = pl.multiple_of(i * QB, QB)
            q = q_ref[pl.ds(r0, QB), :].astype(BF16)
            qs = [q[:, pp * LANES:(pp + 1) * LANES] * m for pp in range(SB_PAIRS) for m in (m0b, m1b)]
            rowg = r0 + _iota((QB, QB), 0)

            def kv(step, carry):
                cs, accs = carry
                c0 = pl.multiple_of((i - step) * QB, QB)
                k = k_ref[pl.ds(c0, QB), :].astype(BF16)
                v = v_ref[pl.ds(c0, QB), :].astype(BF16)
                before = (c0 + _iota((QB, QB), 1)) < rowg
                new_c, new_acc = [], []
                rs = QB // SB_ROW_SPLIT
                for pp in range(SB_PAIRS):
                    kp, vp = k[:, pp * LANES:(pp + 1) * LANES], v[:, pp * LANES:(pp + 1) * LANES]
                    vms = [vp * m0b, vp * m1b]
                    acc_rows = []
                    c_rows = [[], []]
                    for r in range(SB_ROW_SPLIT):
                        rows = slice(r * rs, (r + 1) * rs)
                        acc = accs[pp][rows]
                        for hh in range(2):
                            w, _, tot = _sb_weights(qs[2 * pp + hh][rows], kp, cs[2 * pp + hh][rows], before[rows], tri_suffix)
                            c_rows[hh].append(cs[2 * pp + hh][rows] + tot)
                            acc = acc + _dot(w.astype(BF16), vms[hh])
                        acc_rows.append(acc)
                    for hh in range(2):
                        new_c.append(jnp.concatenate(c_rows[hh], axis=0) if SB_ROW_SPLIT > 1 else c_rows[hh][0])
                    new_acc.append(jnp.concatenate(acc_rows, axis=0) if SB_ROW_SPLIT > 1 else acc_rows[0])
                return tuple(new_c), tuple(new_acc)

            init = ((jnp.zeros((QB, 1), F32),) * (2 * SB_PAIRS), (jnp.zeros((QB, LANES), F32),) * SB_PAIRS)
            _, accs = lax.fori_loop(0, i + 1, kv, init)
            for pp in range(SB_PAIRS):
                o_ref[pl.ds(r0, QB), pp * LANES:(pp + 1) * LANES] = accs[pp].astype(BF16)
            return 0

        lax.fori_loop(0, S // QB, qblock, 0)

    wide = SB_PAIRS * LANES
    col = lambda base: pl.BlockSpec((S, wide), lambda p: (0, base // wide + p))
    return _carrier_call(
        body, name=name, grid=(SB_H // 2 // SB_PAIRS,), in_specs=[col(P_QB), col(P_KB), col(P_VB)],
        out_specs=[pl.BlockSpec((S, wide), lambda p: (0, p))],
        out_shape=[jax.ShapeDtypeStruct((S, SB_H * SB_DIM), BF16)],
        args=(proj, proj, proj), sem=("parallel",), carry=carry)


def _sb_bwd(name, proj, do, do_block0, carry=None):
    nb = S // QB

    def body(q_ref, k_ref, v_ref, do_ref, dq_ref, dk_ref, dv_ref, sig_scr, dl_scr, dk_acc, dv_acc):
        m0b, m1b = _head_masks(BF16)
        tri_suffix = (_iota((QB, QB), 0) > _iota((QB, QB), 1)).astype(BF16)
        tri_prefix = (_iota((QB, QB), 0) < _iota((QB, QB), 1)).astype(BF16)
        dk_acc[...] = jnp.zeros_like(dk_acc)
        dv_acc[...] = jnp.zeros_like(dv_acc)

        def qblock(i, _):
            r0 = pl.multiple_of(i * QB, QB)
            rows = pl.ds(r0, QB)
            q = q_ref[rows, :].astype(BF16)
            qs = [q * m0b, q * m1b]
            dob = do_ref[rows, :].astype(BF16)
            dos = [dob * m0b, dob * m1b]
            rowg = r0 + _iota((QB, QB), 0)

            def sweep_left(step, cs):
                kb = i - step
                c0 = pl.multiple_of(kb * QB, QB)
                cols = pl.ds(c0, QB)
                k = k_ref[cols, :].astype(BF16)
                v = v_ref[cols, :].astype(BF16)
                before = (c0 + _iota((QB, QB), 1)) < rowg
                new_c = []
                dv = None
                for hh in range(2):
                    w, sig, tot = _sb_weights(qs[hh], k, cs[hh], before, tri_suffix)
                    new_c.append(cs[hh] + tot)
                    sig_scr[hh, kb] = sig
                    dl_scr[hh, kb] = _dot(dos[hh], v, "nt") * w
                    part = _dot(w.astype(BF16), dos[hh], "tn")
                    dv = part if dv is None else dv + part
                dv_acc[cols, :] += dv
                return tuple(new_c)

            lax.fori_loop(0, i + 1, sweep_left, (jnp.zeros((QB, 1), F32),) * 2)

            def sweep_right(kb, carry):
                ps, dq = carry
                c0 = pl.multiple_of(kb * QB, QB)
                cols = pl.ds(c0, QB)
                k = k_ref[cols, :].astype(BF16)
                before = (c0 + _iota((QB, QB), 1)) < rowg
                new_p = []
                dk = None
                for hh in range(2):
                    dl = dl_scr[hh, kb]
                    sig = sig_scr[hh, kb]
                    earlier = _split_dot(dl, tri_prefix) + ps[hh]
                    new_p.append(ps[hh] + jnp.sum(dl, axis=-1, keepdims=True))
                    dz = (jnp.where(before, dl * (1.0 - sig) - earlier * sig, 0.0) * SCALE_B).astype(BF16)
                    dq = dq + _dot(dz, k * (m0b if hh == 0 else m1b))
                    part = _dot(dz, qs[hh], "tn")
                    dk = part if dk is None else dk + part
                dk_acc[cols, :] += dk
                return tuple(new_p), dq

            init = ((jnp.zeros((QB, 1), F32),) * 2, jnp.zeros((QB, LANES), F32))
            _, dq = lax.fori_loop(0, i + 1, sweep_right, init)
            dq_ref[rows, :] = dq.astype(BF16)
            return 0

        lax.fori_loop(0, nb, qblock, 0)
        dk_ref[...] = dk_acc[...].astype(BF16)
        dv_ref[...] = dv_acc[...].astype(BF16)

    col = lambda base: pl.BlockSpec((S, LANES), lambda p: (0, base // LANES + p))
    out = pl.BlockSpec((S, LANES), lambda p: (0, p))
    shape = jax.ShapeDtypeStruct((S, SB_H * SB_DIM), BF16)
    return _carrier_call(
        body, name=name, grid=(SB_H // 2,),
        in_specs=[col(P_QB), col(P_KB), col(P_VB), pl.BlockSpec((S, LANES), lambda p: (0, do_block0 + p))],
        out_specs=[out, out, out], out_shape=[shape, shape, shape],
        scratch_shapes=[pltpu.VMEM((2, nb, QB, QB), F32), pltpu.VMEM((2, nb, QB, QB), F32),
                        pltpu.VMEM((S, LANES), F32), pltpu.VMEM((S, LANES), F32)],
        args=(proj, proj, proj, do), sem=("parallel",), carry=carry)


def _band_row_index():
    j = np.arange(TOEP_W)
    rel = np.clip(LEFT_CHUNKS * CHUNK - j, -REL_CLIP, REL_CLIP) + REL_CLIP
    rel[BAND_W:] = 2 * REL_CLIP
    return rel.astype(np.int32)


def _band_tiles(r0_ref, q_ref, kpad, vpad, m, m0b, m1b, static_ok, bias):
    r0 = pl.multiple_of(m * BQ, BQ)
    q = q_ref[0, pl.ds(r0, BQ), :]
    kw = kpad[pl.ds(r0, BAND_W), :]
    vw = vpad[pl.ds(r0, BAND_W), :]
    ok = static_ok & ((r0 - BAND_PAD + _iota((BQ, BAND_W), 1)) >= 0)
    qs = [q * m0b, q * m1b]
    ps = []
    for hh in range(2):
        s = jnp.where(ok, _dot(qs[hh], kw, "nt") * SCALE_C + bias[hh], NEG)
        e = jnp.exp(s - jnp.max(s, axis=-1, keepdims=True))
        ps.append(e * (1.0 / jnp.sum(e, axis=-1, keepdims=True)))
    return r0, qs, kw, vw, ps


def _band_setup(qkv_ref, r0_ref, kpad, vpad):
    kpad[0:BAND_PAD, :] = jnp.zeros((BAND_PAD, LANES), BF16)
    vpad[0:BAND_PAD, :] = jnp.zeros((BAND_PAD, LANES), BF16)
    kpad[BAND_PAD:, :] = qkv_ref[1]
    vpad[BAND_PAD:, :] = qkv_ref[2]
    jc = lax.shift_right_logical(_iota((BQ, BAND_W), 1), 6)
    rc = lax.shift_right_logical(_iota((BQ, BAND_W), 0), 6)
    static_ok = (jc >= rc) & (jc <= rc + LEFT_CHUNKS)
    bias = []
    for hh in range(2):
        row = jnp.broadcast_to(r0_ref[hh:hh + 1, :], (BQ, TOEP_W))
        bias.append(pltpu.roll(row, 0, 1, stride=1, stride_axis=0)[:, :BAND_W])
    return static_ok, bias


def _band_fwd(name, qkv, r0):
    def body(qkv_ref, r0_ref, o_ref, kpad, vpad):
        m0b, m1b = _head_masks(BF16)
        static_ok, bias = _band_setup(qkv_ref, r0_ref, kpad, vpad)

        def qblock(m, _):
            r0_, _, _, vw, ps = _band_tiles(r0_ref, qkv_ref, kpad, vpad, m, m0b, m1b, static_ok, bias)
            o = _dot(ps[0].astype(BF16), vw * m0b) + _dot(ps[1].astype(BF16), vw * m1b)
            o_ref[pl.ds(r0_, BQ), :] = o.astype(BF16)
            return 0

        lax.fori_loop(0, S // BQ, qblock, 0)

    return pl.pallas_call(
        body, name=name, grid=(C_H // 2,),
        in_specs=[pl.BlockSpec((3, S, LANES), lambda p: (0, 0, p)), pl.BlockSpec((None, 2, TOEP_W), lambda p: (p, 0, 0))],
        out_specs=pl.BlockSpec((S, LANES), lambda p: (0, p)),
        out_shape=jax.ShapeDtypeStruct((S, C_H * C_DIM), BF16),
        scratch_shapes=[pltpu.VMEM((S + BAND_PAD, LANES), BF16), pltpu.VMEM((S + BAND_PAD, LANES), BF16)],
        compiler_params=_params("parallel"),
    )(qkv, r0)


def _band_bwd(name, qkv, r0, do, carry=None):
    def body(qkv_ref, r0_ref, do_ref, dqkv_ref, dr0_ref, kpad, vpad, dkpad, dvpad, db_acc):
        m0b, m1b = _head_masks(BF16)
        static_ok, bias = _band_setup(qkv_ref, r0_ref, kpad, vpad)
        dkpad[...] = jnp.zeros_like(dkpad)
        dvpad[...] = jnp.zeros_like(dvpad)
        db_acc[...] = jnp.zeros_like(db_acc)

        def qblock(m, _):
            r0_, qs, kw, vw, ps = _band_tiles(r0_ref, qkv_ref, kpad, vpad, m, m0b, m1b, static_ok, bias)
            dob = do_ref[pl.ds(r0_, BQ), :].astype(BF16)
            dos = [dob * m0b, dob * m1b]
            dq = None
            dk = None
            dv = None
            for hh in range(2):
                p = ps[hh]
                dp = _dot(dos[hh], vw, "nt")
                ds = p * (dp - jnp.sum(dp * p, axis=-1, keepdims=True))
                db_acc[hh, :, 0:BAND_W] += ds
                dsb = (ds * SCALE_C).astype(BF16)
                t = _dot(dsb, kw * (m0b if hh == 0 else m1b))
                dq = t if dq is None else dq + t
                t = _dot(dsb, qs[hh], "tn")
                dk = t if dk is None else dk + t
                t = _dot(p.astype(BF16), dos[hh], "tn")
                dv = t if dv is None else dv + t
            dqkv_ref[0, pl.ds(r0_, BQ), :] = dq.astype(BF16)
            dkpad[pl.ds(r0_, BAND_W), :] += dk
            dvpad[pl.ds(r0_, BAND_W), :] += dv
            return 0

        lax.fori_loop(0, S // BQ, qblock, 0)
        dqkv_ref[1] = dkpad[BAND_PAD:, :].astype(BF16)
        dqkv_ref[2] = dvpad[BAND_PAD:, :].astype(BF16)
        sub = _iota((8, TOEP_W), 0)
        for hh in range(2):
            folded = db_acc[hh, 0:8, :]
            for a in range(1, BQ // 8):
                folded = folded + pltpu.roll(db_acc[hh, 8 * a:8 * a + 8, :], TOEP_W - 8 * a, 1)
            for bit in range(3):
                moved = pltpu.roll(folded, TOEP_W - (1 << bit), 1)
                folded = jnp.where((sub & (1 << bit)) != 0, moved, folded)
            dr0_ref[hh:hh + 1, :] = jnp.sum(folded, axis=0, keepdims=True)

    return _carrier_call(
        body, name=name, grid=(C_H // 2,),
        in_specs=[pl.BlockSpec((3, S, LANES), lambda p: (0, 0, p)), pl.BlockSpec((None, 2, TOEP_W), lambda p: (p, 0, 0)),
                  pl.BlockSpec((S, LANES), lambda p: (0, p))],
        out_specs=[pl.BlockSpec((3, S, LANES), lambda p: (0, 0, p)), pl.BlockSpec((None, 2, TOEP_W), lambda p: (p, 0, 0))],
        out_shape=[jax.ShapeDtypeStruct((3, S, C_H * C_DIM), BF16), jax.ShapeDtypeStruct((C_H // 2, 2, TOEP_W), F32)],
        scratch_shapes=[pltpu.VMEM((S + BAND_PAD, LANES), BF16), pltpu.VMEM((S + BAND_PAD, LANES), BF16),
                        pltpu.VMEM((S + BAND_PAD, LANES), F32), pltpu.VMEM((S + BAND_PAD, LANES), F32),
                        pltpu.VMEM((2, BQ, TOEP_W), F32)],
        args=(qkv, r0, do), sem=("parallel",), carry=carry)


def _bias_table_grad(name, dr0):
    w_out = 5 * LANES

    def body(d_ref, o_ref):
        j = _iota((TOEP_W, w_out), 0)
        rel = jnp.clip(LEFT_CHUNKS * CHUNK - j, -REL_CLIP, REL_CLIP) + REL_CLIP
        rel = jnp.where(j >= BAND_W, 2 * REL_CLIP, rel)
        onehot = (rel == _iota((TOEP_W, w_out), 1)).astype(BF16)
        d = d_ref[...]
        hi = d.astype(BF16)
        mid = (d - hi.astype(F32))
        mid_b = mid.astype(BF16)
        lo = (mid - mid_b.astype(F32)).astype(BF16)
        o_ref[...] = _dot(hi, onehot) + _dot(mid_b, onehot) + _dot(lo, onehot)

    return pl.pallas_call(
        body, name=name, out_shape=jax.ShapeDtypeStruct((C_H, w_out), F32),
        in_specs=[pl.BlockSpec((C_H, TOEP_W), lambda: (0, 0))], out_specs=pl.BlockSpec((C_H, w_out), lambda: (0, 0)),
        grid=(),
    )(dr0)


def _carry_gather(cy, slots, names, ici, d2d):
    idx = [cy.operand(slots[n], True) for n in names]
    n = len(names)
    base_i = cy.sems(3 * n) if ici else 0
    base_d = cy.sems(3 * n) if d2d else 0

    def piece(refs, t, slot, cc):
        return refs[idx[t]].at[slot, _half_rows(cc, slots[names[t]].shape[1] // 2), :]

    def over_ici(refs, send, recv, arriving):
        x, y, c, chips = _position()
        out = []
        for t in range(n):
            for j in range(3):
                r = piece(refs, t, 2 * chips[j][0] + chips[j][1] if arriving else 2 * x + y, c)
                out.append(_remote(r, r, send, recv, base_i + 3 * t + j, (*chips[j], c)))
        return out

    def over_d2d(refs, send, recv, arriving):
        x, y, c, chips = _position()
        out = []
        for t in range(n):
            for j in range(3):
                r = piece(refs, t, 2 * chips[j][0] + chips[j][1], 1 - c if arriving else c)
                out.append(_remote(r, r, send, recv, base_d + 3 * t + j, (x, y, 1 - c)))
        return out

    def start_ici(refs, fresh, send, recv):
        for cp in over_ici(refs, send, recv, False):
            cp.start()

    def wait_ici(refs, fresh, send, recv):
        for cp in over_ici(refs, send, recv, True):
            cp.wait_recv()
        for cp in over_ici(refs, send, recv, False):
            cp.wait_send()

    def start_d2d(refs, fresh, send, recv):
        for cp in over_d2d(refs, send, recv, False):
            cp.start()

    def wait_d2d(refs, fresh, send, recv):
        for cp in over_d2d(refs, send, recv, True):
            cp.wait_recv()
        for cp in over_d2d(refs, send, recv, False):
            cp.wait_send()

    if ici and d2d:
        cy.starts.append(start_ici)
        cy.finishes += [wait_ici, start_d2d, wait_d2d]
    elif ici:
        cy.starts.append(start_ici)
        cy.finishes.append(wait_ici)
    else:
        cy.starts.append(start_d2d)
        cy.finishes.append(wait_d2d)

    def done(aliased, fresh):
        for t, name in enumerate(names):
            slots[name] = aliased[idx[t]]

    cy.on_done.append(done)


def _carry_chip_exchange(cy, sums, got, names):
    idx = [cy.operand(sums[n], False) for n in names]
    out = [cy.result((3,) + sums[n].shape[1:], BF16) for n in names]
    base = cy.sems(3 * len(names))

    def copies(refs, fresh, send, recv):
        x, y, c, chips = _position()
        return [_remote(refs[idx[t]].at[2 * chips[j][0] + chips[j][1]], fresh[out[t]].at[j], send, recv, base + 3 * t + j,
                        (*chips[j], c)) for t in range(len(names)) for j in range(3)]

    def start(refs, fresh, send, recv):
        for cp in copies(refs, fresh, send, recv):
            cp.start()

    def wait(refs, fresh, send, recv):
        for cp in copies(refs, fresh, send, recv):
            cp.wait()

    cy.starts.append(start)
    cy.finishes.append(wait)

    def done(aliased, fresh):
        for t, name in enumerate(names):
            got[name] = fresh[out[t]]

    cy.on_done.append(done)


def _run_carry(name, cy):
    _, res = _carrier_call(None, name=name, grid=(), in_specs=[], out_specs=[], out_shape=[], args=(), sem=(), carry=cy)
    cy.done(res)


FIRST_WEIGHTS = ("ev_w_in", "ev_w_uq", "ev_w_ukv")
LAYER0_WEIGHTS = ("ev_w_out", "w_gate0", "w_up0", "w_down0")
LAYER1_WEIGHTS = ("od_w_qkv", "od_w_out", "w_gate1", "w_up1", "w_down1")
GRAD_GROUPS = {"ffn1": ("w_gate1", "w_up1", "w_down1"), "od": ("od_w_qkv", "od_w_out"),
               "ffn0": ("w_gate0", "w_up0", "w_down0"), "ev": ("ev_w_in", "ev_w_uq", "ev_w_ukv", "ev_w_out")}


def _carry_pair_exchange(cy, parts, theirs, names):
    idx = [cy.operand(parts[n], False) for n in names]
    out = [cy.result((N_CHIPS, parts[n].shape[1] // 2, parts[n].shape[2]), BF16) for n in names]
    base = cy.sems(len(names))

    def copies(refs, fresh, send, recv):
        x, y, c, _ = _position()
        return [_remote(refs[idx[t]].at[:, _half_rows(1 - c, parts[n].shape[1] // 2), :], fresh[out[t]], send, recv,
                        base + t, (x, y, 1 - c)) for t, n in enumerate(names)]

    cy.starts.append(lambda refs, fresh, send, recv: [cp.start() for cp in copies(refs, fresh, send, recv)])
    cy.finishes.append(lambda refs, fresh, send, recv: [cp.wait() for cp in copies(refs, fresh, send, recv)])

    def done(aliased, fresh):
        for t, name in enumerate(names):
            theirs[name] = fresh[out[t]]

    cy.on_done.append(done)


def _carry_sibling_exchange(cy, fulls, pieces):
    idx = [cy.operand(fulls[p], True) for p, _ in pieces]
    base = cy.sems(len(pieces))

    def copies(refs, send, recv, arriving):
        x, y, c, _ = _position()
        out = []
        for t, (p, layer) in enumerate(pieces):
            r = refs[idx[t]].at[layer, _half_rows(1 - c if arriving else c, fulls[p].shape[1] // 2), :]
            out.append(_remote(r, r, send, recv, base + t, (x, y, 1 - c)))
        return out

    def start(refs, fresh, send, recv):
        for cp in copies(refs, send, recv, False):
            cp.start()

    def wait(refs, fresh, send, recv):
        for cp in copies(refs, send, recv, True):
            cp.wait_recv()
        for cp in copies(refs, send, recv, False):
            cp.wait_send()

    cy.starts.append(start)
    cy.finishes.append(wait)

    def done(aliased, fresh):
        for t, (p, _) in enumerate(pieces):
            fulls[p] = aliased[idx[t]]

    cy.on_done.append(done)


RIDES = {
    "mla_attn": (("gather_ici", LAYER0_WEIGHTS),),
    "sb_attn": (("gather_d2d", LAYER0_WEIGHTS), ("gather_ici", LAYER1_WEIGHTS)),
    "ffn0": (("gather_d2d", LAYER1_WEIGHTS),),
    "od_out_bwd_w": (("pair", "ffn1"),),
    "band_attn_bwd": (("chips", "ffn1"),),
    "rms_mix1_bwd": (("pair", "od"),),
    "ev_out_bwd_w": (("pair", "ffn0"),),
    "mla_attn_bwd": (("chips", "od"), ("sibling", "ffn1")),
    "sb_attn_bwd": (("chips", "ffn0"), ("sibling", "od")),
    "proj_in_bwd_w": (("sibling", "ffn0"),),
    "proj_in_bwd_x": (("chips", "ev"),),
}


class _Exchanges:
    def __init__(self, slots, pos, shapes):
        self.slots, self.pos, self.shapes = dict(slots), pos, shapes
        self.parts, self.theirs, self.sums, self.got, self.fulls = {}, {}, {}, {}, {}

    def begin(self):
        cy = _Carry()
        _carry_gather(cy, self.slots, FIRST_WEIGHTS, True, True)
        _run_carry("gather_first", cy)

    def weights(self, *names):
        return [self.slots[n] for n in names]

    def _pair_sums(self, group):
        for n in GRAD_GROUPS[group]:
            if n not in self.sums:
                self.sums[n] = _pair_sum("pair_sum_" + n, self.parts[n], self.theirs[n], self.pos)

    def _chip_sums(self, group):
        for n in GRAD_GROUPS[group]:
            param, layer = PART_OF[n]
            self.fulls[param] = _chip_sum("chip_sum_" + n, self.sums[n], self.got[n], self.pos, layer,
                                          self.shapes[param], self.fulls.get(param))

    def carry(self, stage):
        cy = _Carry()
        for step, what in RIDES[stage]:
            if step == "gather_ici":
                _carry_gather(cy, self.slots, what, True, False)
            elif step == "gather_d2d":
                _carry_gather(cy, self.slots, what, False, True)
            elif step == "pair":
                _carry_pair_exchange(cy, self.parts, self.theirs, GRAD_GROUPS[what])
            elif step == "chips":
                self._pair_sums(what)
                _carry_chip_exchange(cy, self.sums, self.got, GRAD_GROUPS[what])
            elif step == "sibling":
                self._chip_sums(what)
                _carry_sibling_exchange(cy, self.fulls, [PART_OF[n] for n in GRAD_GROUPS[what]])
        return cy

    def grads(self, group, parts):
        self.parts.update(parts)
        if group == "ev":
            cy = _Carry()
            _carry_pair_exchange(cy, self.parts, self.theirs, GRAD_GROUPS[group])
            _run_carry("grads_pair_ev", cy)

    def finish(self):
        cy = _Carry()
        self._chip_sums("ev")
        _carry_sibling_exchange(cy, self.fulls, [PART_OF[n] for n in GRAD_GROUPS["ev"]])
        _run_carry("grads_sibling_ev", cy)
        return {n: self.fulls[n] for n in BIG}


class _NoExchanges:
    def __init__(self, slots):
        self.slots, self.parts = dict(slots), {}

    def begin(self):
        pass

    def weights(self, *names):
        return [self.slots[n] for n in names]

    def carry(self, stage):
        return None

    def grads(self, group, parts):
        self.parts.update(parts)


def _first_weights(w_in_s, w_uq_s, w_ukv_s):
    gw = {"ev_w_in": w_in_s, "ev_w_uq": w_uq_s, "ev_w_ukv": w_ukv_s}
    w_in = jnp.moveaxis(gw["ev_w_in"], 0, 1).reshape(D, EVEN_IN)
    z = lambda n: jnp.zeros((D, n), BF16)
    w_in_p = jnp.concatenate(
        [w_in[:, 0:384], z(128), w_in[:, 384:640], w_in[:, 672:2208], z(KR_LANE), w_in[:, 640:672],
         z(LANES - KR_LANE - MLA_ROPE)], axis=1)
    w_uq = jnp.moveaxis(gw["ev_w_uq"], 0, 1).reshape(Q_LORA, MLA_H, MLA_NOPE + MLA_ROPE)
    w_uq_p = jnp.concatenate([w_uq, jnp.zeros((Q_LORA, MLA_H, LANES - MLA_NOPE - MLA_ROPE), BF16)], axis=2)
    w_ukv = jnp.moveaxis(gw["ev_w_ukv"], 0, 1).reshape(KV_LORA, MLA_H, MLA_NOPE + MLA_V)
    w_uk_p = jnp.concatenate([w_ukv[:, :, :MLA_NOPE], jnp.zeros((KV_LORA, MLA_H, LANES - MLA_NOPE), BF16)], axis=2)
    return dict(
        w_in=w_in_p, w_uq=w_uq_p.reshape(Q_LORA, MLA_H * LANES), w_uk=w_uk_p.reshape(KV_LORA, MLA_H * LANES),
        w_uv=w_ukv[:, :, MLA_NOPE:].reshape(KV_LORA, MLA_H * MLA_V))


def _proj_mm(name, u, w_in):
    return _mm(name, u, w_in, kind="nn", grid=(S // TM, 1, 1),
               a_spec=pl.BlockSpec((TM, D), lambda i, j, k: (i, 0)), b_spec=pl.BlockSpec((D, P_IN), lambda i, j, k: (0, 0)),
               o_spec=pl.BlockSpec((TM, P_IN), lambda i, j, k: (i, 0)), out_shape=(S, P_IN), out_dtype=F32, acc_shape=None)


def _out_proj(name, o, w, resid):
    return _mm(name, o, w, kind="nn", grid=(S // TM, 1, 1),
               a_spec=pl.BlockSpec((TM, D), lambda i, j, k: (i, 0)), b_spec=pl.BlockSpec((D, D), lambda i, j, k: (0, 0)),
               o_spec=pl.BlockSpec((TM, D), lambda i, j, k: (i, 0)), out_shape=(S, D), out_dtype=F32, acc_shape=None,
               resid=resid, r_spec=pl.BlockSpec((TM, D), lambda i, j, k: (i, 0)))


def _out_proj_bwd(name, dh, o, w, ex):
    d_o = _mm(name + "_x", dh, w, kind="nt", grid=(S // TM, 1, 1),
              a_spec=pl.BlockSpec((TM, D), lambda i, j, k: (i, 0)), b_spec=pl.BlockSpec((D, D), lambda i, j, k: (0, 0)),
              o_spec=pl.BlockSpec((TM, D), lambda i, j, k: (i, 0)), out_shape=(S, D), out_dtype=F32, acc_shape=None)
    d_w = _mm(name + "_w", o, dh, kind="tn", grid=(2, S // TM),
              a_spec=pl.BlockSpec((TM, TM), lambda j, k: (k, j)), b_spec=pl.BlockSpec((TM, D), lambda j, k: (k, 0)),
              o_spec=pl.BlockSpec((TM, D), lambda j, k: (j, 0)), out_shape=(D, D), out_dtype=BF16, acc_shape=(TM, D),
              carry=ex.carry(name + "_w"))
    return d_o, d_w


def _local_step(x, tgt, sm, ex):
    def riding(stage, fn, *args):
        cy = ex.carry(stage)
        res, copies = fn(stage, *args, carry=cy)
        if cy is not None:
            cy.done(copies)
        return res

    cos_t, sin_t = _rope_tables()
    g_mix, g_ffn = sm["g_mix"], sm["g_ffn"]
    r0 = sm["od_rel_bias"][0][:, _band_row_index()].reshape(C_H // 2, 2, TOEP_W)
    nt = 3

    ex.begin()
    w = _first_weights(*ex.weights(*FIRST_WEIGHTS))
    u0 = _rms_fwd("rms_mix0", x, g_mix[0:1])
    proj = _proj_mm("proj_in", u0, w["w_in"])
    qa, ka, va = _mla_prep_fwd("mla_prep", proj, sm["ev_g_cq"], sm["ev_g_ckv"], w["w_uq"], w["w_uk"], w["w_uv"], cos_t, sin_t)
    o_a, lse = riding("mla_attn", _mla_fwd, qa, ka, va)
    o_b, = riding("sb_attn", _sb_fwd, proj)
    o_ev = jnp.concatenate([o_a.astype(BF16), o_b], axis=1)
    w["ev_w_out"], w["w_gate0"], w["w_up0"], w["w_down0"] = ex.weights(*LAYER0_WEIGHTS)
    w["ev_w_out"] = w["ev_w_out"].reshape(D, D)
    h1 = _out_proj("ev_out", o_ev, w["ev_w_out"], x)
    h2, gate0, up0 = riding("ffn0", _ffn_fwd, h1, g_ffn[0:1], w["w_gate0"], w["w_up0"], w["w_down0"])
    w["w_qkv"], w["od_w_out"], w["w_gate1"], w["w_up1"], w["w_down1"] = ex.weights(*LAYER1_WEIGHTS)
    w["od_w_out"] = w["od_w_out"].reshape(D, D)
    w["w_qkv"] = jnp.moveaxis(w["w_qkv"], 0, 1).reshape(D, nt * D)
    u2 = _rms_fwd("rms_mix1", h2, g_mix[1:2])
    qkv = _mm("qkv", u2, w["w_qkv"], kind="nn", grid=(S // TM, nt, 1),
              a_spec=pl.BlockSpec((TM, D), lambda i, t, k: (i, 0)), b_spec=pl.BlockSpec((D, D), lambda i, t, k: (0, t)),
              o_spec=pl.BlockSpec((None, TM, D), lambda i, t, k: (t, i, 0)),
              out_shape=(nt, S, D), out_dtype=BF16, acc_shape=None)
    o_od = _band_fwd("band_attn", qkv, r0)
    h3 = _out_proj("od_out", o_od, w["od_w_out"], h2)
    (h4, gate1, up1), _ = _ffn_fwd("ffn1", h3, g_ffn[1:2], w["w_gate1"], w["w_up1"], w["w_down1"])

    loss, dh4, dg_final = _loss_bwd("loss", h4, sm["g_final"].reshape(1, D), tgt)

    dh3, dg_ffn1, u3, dgate, dup, act = _ffn_bwd("ffn1_bwd", dh4, h3, g_ffn[1:2], gate1, up1,
                                                 w["w_gate1"], w["w_up1"], w["w_down1"])
    d_wg1, d_wu1, d_wd1 = _ffn_wgrads("ffn1_dw", u3, dgate, dup, act, dh4)
    ex.grads("ffn1", {"w_gate1": d_wg1, "w_up1": d_wu1, "w_down1": d_wd1})

    d_ood, d_w_od_out = _out_proj_bwd("od_out_bwd", dh3, o_od, w["od_w_out"], ex)
    dqkv, dr0 = riding("band_attn_bwd", _band_bwd, qkv, r0, d_ood)
    du2 = _mm("qkv_bwd_x", dqkv, w["w_qkv"], kind="nt", grid=(S // TM, nt),
              a_spec=pl.BlockSpec((None, TM, D), lambda i, t: (t, i, 0)), b_spec=pl.BlockSpec((D, D), lambda i, t: (0, t)),
              o_spec=pl.BlockSpec((TM, D), lambda i, t: (i, 0)), out_shape=(S, D), out_dtype=F32, acc_shape=(TM, D))
    d_w_qkv = _mm("qkv_bwd_w", u2, dqkv, kind="tn", grid=(nt, S // TM),
                  a_spec=pl.BlockSpec((TM, D), lambda t, k: (k, 0)), b_spec=pl.BlockSpec((None, TM, D), lambda t, k: (t, k, 0)),
                  o_spec=pl.BlockSpec((D, D), lambda t, k: (0, t)), out_shape=(D, nt * D), out_dtype=BF16, acc_shape=(D, D))
    shard_cols = lambda a: jnp.moveaxis(a.reshape(a.shape[0], N_CHIPS, a.shape[1] // N_CHIPS), 1, 0)
    ex.grads("od", {"od_w_qkv": shard_cols(d_w_qkv), "od_w_out": d_w_od_out.reshape(N_CHIPS, D // N_CHIPS, D)})
    dh2, dg_mix1 = _rms_bwd("rms_mix1_bwd", du2, h2, g_mix[1:2], dh3, carry=ex.carry("rms_mix1_bwd"))
    d_rel = _bias_table_grad("rel_bias_grad", dr0.reshape(C_H, TOEP_W))[:, :2 * REL_CLIP + 1]

    dh1, dg_ffn0, u1, dgate, dup, act = _ffn_bwd("ffn0_bwd", dh2, h1, g_ffn[0:1], gate0, up0,
                                                 w["w_gate0"], w["w_up0"], w["w_down0"])
    d_wg0, d_wu0, d_wd0 = _ffn_wgrads("ffn0_dw", u1, dgate, dup, act, dh2)
    ex.grads("ffn0", {"w_gate0": d_wg0, "w_up0": d_wu0, "w_down0": d_wd0})

    d_oev, d_w_ev_out = _out_proj_bwd("ev_out_bwd", dh1, o_ev, w["ev_w_out"], ex)
    dqa, dka, dva = riding("mla_attn_bwd", _mla_bwd, qa, ka, va, o_a, lse, d_oev, 0)
    dqb, dkb, dvb = riding("sb_attn_bwd", _sb_bwd, proj, d_oev, MLA_H * MLA_V // LANES)
    dcq, dckv, dkr, d_w_uq, d_w_uk, d_w_uv, dg_cq, dg_ckv = _mla_prep_bwd(
        "mla_prep_bwd", dqa, dka, dva, proj, sm["ev_g_cq"], sm["ev_g_ckv"], w["w_uq"], w["w_uk"], w["w_uv"], cos_t, sin_t)
    dproj = jnp.concatenate([dcq, jnp.zeros((S, LANES), BF16), dckv, dqb, dkb, dvb, dkr], axis=1)
    d_w_in_p = _mm("proj_in_bwd_w", u0, dproj, kind="tn", grid=(1, S // TM),
                   a_spec=pl.BlockSpec((TM, D), lambda j, k: (k, 0)), b_spec=pl.BlockSpec((TM, P_IN), lambda j, k: (k, 0)),
                   o_spec=pl.BlockSpec((D, P_IN), lambda j, k: (0, 0)), out_shape=(D, P_IN), out_dtype=BF16,
                   acc_shape=(D, P_IN), carry=ex.carry("proj_in_bwd_w"))
    d_w_in = jnp.concatenate([d_w_in_p[:, 0:384], d_w_in_p[:, 512:768],
                              d_w_in_p[:, P_KR + KR_LANE:P_KR + KR_LANE + MLA_ROPE], d_w_in_p[:, 768:2304]], axis=1)
    d_w_uq_std = d_w_uq.reshape(Q_LORA, MLA_H, LANES)[:, :, :MLA_NOPE + MLA_ROPE].reshape(Q_LORA, -1)
    d_w_ukv = jnp.concatenate([d_w_uk.reshape(KV_LORA, MLA_H, LANES)[:, :, :MLA_NOPE],
                               d_w_uv.reshape(KV_LORA, MLA_H, MLA_V)], axis=2).reshape(KV_LORA, -1)
    ex.grads("ev", {"ev_w_in": shard_cols(d_w_in), "ev_w_uq": shard_cols(d_w_uq_std.astype(BF16)),
                    "ev_w_ukv": shard_cols(d_w_ukv.astype(BF16)),
                    "ev_w_out": d_w_ev_out.reshape(N_CHIPS, D // N_CHIPS, D)})
    du0 = _mm("proj_in_bwd_x", dproj, w["w_in"], kind="nt", grid=(S // TM, 1, 1),
              a_spec=pl.BlockSpec((TM, P_IN), lambda i, j, k: (i, 0)), b_spec=pl.BlockSpec((D, P_IN), lambda i, j, k: (0, 0)),
              o_spec=pl.BlockSpec((TM, D), lambda i, j, k: (i, 0)), out_shape=(S, D), out_dtype=F32, acc_shape=None,
              carry=ex.carry("proj_in_bwd_x"))
    grad_x, dg_mix0 = _rms_bwd("rms_mix0_bwd", du0, x, g_mix[0:1], dh1)
    small = {
        "ev_g_cq": dg_cq, "ev_g_ckv": dg_ckv, "od_rel_bias": d_rel.reshape(1, C_H, 2 * REL_CLIP + 1),
        "g_mix": jnp.concatenate([dg_mix0, dg_mix1], axis=0), "g_ffn": jnp.concatenate([dg_ffn0, dg_ffn1], axis=0),
        "g_final": dg_final.reshape(D),
    }
    return loss, grad_x, small


BIG = ("ev_w_in", "ev_w_uq", "ev_w_ukv", "ev_w_out", "od_w_qkv", "od_w_out", "w_gate", "w_up", "w_down")
SMALL = ("ev_g_cq", "ev_g_ckv", "od_rel_bias", "g_mix", "g_ffn", "g_final")
WEIGHTS = ("ev_w_in", "ev_g_cq", "ev_w_uq", "ev_g_ckv", "ev_w_ukv", "ev_w_out", "od_w_qkv", "od_rel_bias", "od_w_out",
           "g_mix", "g_ffn", "w_gate", "w_up", "w_down", "g_final")
GRAD_PARTS = (("ev_w_in", "ev_w_in", 0), ("ev_w_uq", "ev_w_uq", 0), ("ev_w_ukv", "ev_w_ukv", 0),
              ("ev_w_out", "ev_w_out", 0), ("od_w_qkv", "od_w_qkv", 0), ("od_w_out", "od_w_out", 0),
              ("w_gate0", "w_gate", 0), ("w_gate1", "w_gate", 1), ("w_up0", "w_up", 0), ("w_up1", "w_up", 1),
              ("w_down0", "w_down", 0), ("w_down1", "w_down", 1))
PART_OF = {part: (param, layer) for part, param, layer in GRAD_PARTS}
SMALL_ROWS = 112
SMALL_SIZE = 384 + 256 + 16 * 513 + 2 * 1024 + 2 * 1024 + 1024
TRANSPOSED = ("w_gate", "w_up")


def _row_tile(rows, cap=512):
    for t in range(min(rows, cap), 0, -1):
        if rows % t == 0 and t % 16 == 0:
            return t
    return rows


def _cast_into_slot(name, w, layer, pos):
    _, rows, cols = w.shape
    tr = _row_tile(rows)

    def body(pos_ref, w_ref, o_ref):
        o_ref[...] = w_ref[...].astype(BF16)

    return pl.pallas_call(
        body, name=name,
        grid_spec=pltpu.PrefetchScalarGridSpec(
            num_scalar_prefetch=1, grid=(rows // tr,),
            in_specs=[pl.BlockSpec((None, tr, cols), lambda i, p: (layer, i, 0))],
            out_specs=pl.BlockSpec((None, tr, cols), lambda i, p: (p[0], i, 0))),
        out_shape=jax.ShapeDtypeStruct((N_CHIPS, rows, cols), BF16), compiler_params=_params("arbitrary"))(pos, w)


def _pair_sum(name, part, theirs, pos):
    _, half, cols = theirs.shape
    tr = _row_tile(half)
    nb = half // tr

    def body(pos_ref, a_ref, b_ref, o_ref):
        o_ref[...] = (a_ref[...].astype(F32) + b_ref[...].astype(F32)).astype(BF16)

    return pl.pallas_call(
        body, name=name,
        grid_spec=pltpu.PrefetchScalarGridSpec(
            num_scalar_prefetch=1, grid=(N_CHIPS, nb),
            in_specs=[pl.BlockSpec((None, tr, cols), lambda s, i, p: (s, p[1] * nb + i, 0)),
                      pl.BlockSpec((None, tr, cols), lambda s, i, p: (s, i, 0))],
            out_specs=pl.BlockSpec((None, tr, cols), lambda s, i, p: (s, i, 0))),
        out_shape=jax.ShapeDtypeStruct(theirs.shape, BF16),
        compiler_params=_params("arbitrary", "arbitrary"))(pos, part, theirs)


def _chip_sum(name, sums, got, pos, layer, full_shape, full=None):
    _, half, cols = sums.shape
    tr = _row_tile(half)
    nb = half // tr

    def body(pos_ref, s_ref, g_ref, *rest):
        out_ref = rest[-1]
        out_ref[...] = ((s_ref[...].astype(F32) + g_ref[0].astype(F32)) + g_ref[1].astype(F32)) + g_ref[2].astype(F32)

    in_specs = [pl.BlockSpec((None, tr, cols), lambda i, p: (p[0], i, 0)),
                pl.BlockSpec((3, tr, cols), lambda i, p: (0, i, 0))]
    args = [pos, sums, got]
    if full is not None:
        in_specs.append(ANY)
        args.append(full)
    return pl.pallas_call(
        body, name=name,
        grid_spec=pltpu.PrefetchScalarGridSpec(
            num_scalar_prefetch=1, grid=(nb,), in_specs=in_specs,
            out_specs=pl.BlockSpec((None, tr, cols), lambda i, p: (layer, p[1] * nb + i, 0))),
        out_shape=jax.ShapeDtypeStruct(full_shape, F32),
        input_output_aliases={3: 0} if full is not None else {},
        compiler_params=_params("arbitrary"))(*args)


def _all_reduce_small(name, packed):
    n_dev = 8

    def body(p_ref, o_ref, slots, send_sem, recv_sem):
        x, y, c, _ = _position()
        me = 4 * x + 2 * y + c

        def peer(k):
            return (1 - x if k & 4 else x, 1 - y if k & 2 else y, 1 - c if k & 1 else c)

        def logical(k):
            px, py, pc = peer(k)
            return 4 * px + 2 * py + pc

        slots[me] = p_ref[...]
        sends = [pltpu.make_async_remote_copy(
            src_ref=p_ref, dst_ref=slots.at[me], send_sem=send_sem.at[k], recv_sem=recv_sem.at[k],
            device_id=peer(k), device_id_type=MESH) for k in range(1, n_dev)]
        for cp in sends:
            cp.start()
        for k in range(1, n_dev):
            pltpu.make_async_remote_copy(
                src_ref=p_ref, dst_ref=slots.at[logical(k)], send_sem=send_sem.at[k], recv_sem=recv_sem.at[k],
                device_id=peer(k), device_id_type=MESH).wait_recv()
        for cp in sends:
            cp.wait_send()
        total = slots[0]
        for d in range(1, n_dev):
            total = total + slots[d]
        o_ref[...] = total

    vm = pl.BlockSpec(memory_space=pltpu.VMEM)
    return pl.pallas_call(
        body, name=name, in_specs=[vm], out_specs=vm, out_shape=jax.ShapeDtypeStruct(packed.shape, F32),
        scratch_shapes=[pltpu.VMEM((n_dev,) + packed.shape, F32), pltpu.SemaphoreType.DMA((n_dev,)),
                        pltpu.SemaphoreType.DMA((n_dev,))],
    )(packed)


def _adamw(name, w, g, m, v):
    rows, cols = w.shape
    tr = _row_tile(rows)

    def body(w_ref, g_ref, m_ref, v_ref, d_ref, mo_ref, vo_ref):
        gv = g_ref[...]
        m_new = ADAM_B1 * m_ref[...] + (1.0 - ADAM_B1) * gv
        v_new = ADAM_B2 * v_ref[...] + (1.0 - ADAM_B2) * (gv * gv)
        m_hat = m_new / (1.0 - ADAM_B1 ** ADAM_STEP)
        v_hat = v_new / (1.0 - ADAM_B2 ** ADAM_STEP)
        d_ref[...] = -ADAM_LR * (m_hat / (jnp.sqrt(v_hat) + ADAM_EPS) + ADAM_WD * w_ref[...])
        mo_ref[...] = m_new
        vo_ref[...] = v_new

    spec = pl.BlockSpec((tr, cols), lambda i: (i, 0))
    shape = jax.ShapeDtypeStruct((rows, cols), F32)
    return pl.pallas_call(body, name=name, grid=(rows // tr,), in_specs=[spec] * 4, out_specs=[spec] * 3,
                          out_shape=[shape] * 3, compiler_params=_params("parallel"))(w, g, m, v)


def _pack_small(tree, extra=None):
    pieces = [tree[n].reshape(-1).astype(F32) for n in SMALL]
    if extra is not None:
        pieces.append(extra.reshape(1).astype(F32))
    flat = jnp.concatenate(pieces)
    return jnp.pad(flat, (0, SMALL_ROWS * LANES - flat.shape[0])).reshape(SMALL_ROWS, LANES)


def _unpack_small(packed, like):
    flat = packed.reshape(-1)
    out, off = {}, 0
    for n in SMALL:
        size = int(np.prod(like[n].shape))
        out[n] = flat[off:off + size].reshape(like[n].shape)
        off += size
    return out


def kernel(x, ev_w_in, ev_g_cq, ev_w_uq, ev_g_ckv, ev_w_ukv, ev_w_out, od_w_qkv, od_rel_bias, od_w_out, g_mix, g_ffn, w_gate, w_up, w_down, g_final, loss_target, m_ev_w_in, m_ev_g_cq, m_ev_w_uq, m_ev_g_ckv, m_ev_w_ukv, m_ev_w_out, m_od_w_qkv, m_od_rel_bias, m_od_w_out, m_g_mix, m_g_ffn, m_w_gate, m_w_up, m_w_down, m_g_final, v_ev_w_in, v_ev_g_cq, v_ev_w_uq, v_ev_g_ckv, v_ev_w_ukv, v_ev_w_out, v_od_w_qkv, v_od_rel_bias, v_od_w_out, v_g_mix, v_g_ffn, v_w_gate, v_w_up, v_w_down, v_g_final):
    w = dict(ev_w_in=ev_w_in, ev_g_cq=ev_g_cq, ev_w_uq=ev_w_uq, ev_g_ckv=ev_g_ckv, ev_w_ukv=ev_w_ukv, ev_w_out=ev_w_out,
             od_w_qkv=od_w_qkv, od_rel_bias=od_rel_bias, od_w_out=od_w_out, g_mix=g_mix, g_ffn=g_ffn, w_gate=w_gate,
             w_up=w_up, w_down=w_down, g_final=g_final)
    m = dict(ev_w_in=m_ev_w_in, ev_g_cq=m_ev_g_cq, ev_w_uq=m_ev_w_uq, ev_g_ckv=m_ev_g_ckv, ev_w_ukv=m_ev_w_ukv,
             ev_w_out=m_ev_w_out, od_w_qkv=m_od_w_qkv, od_rel_bias=m_od_rel_bias, od_w_out=m_od_w_out, g_mix=m_g_mix,
             g_ffn=m_g_ffn, w_gate=m_w_gate, w_up=m_w_up, w_down=m_w_down, g_final=m_g_final)
    v = dict(ev_w_in=v_ev_w_in, ev_g_cq=v_ev_g_cq, ev_w_uq=v_ev_w_uq, ev_g_ckv=v_ev_g_ckv, ev_w_ukv=v_ev_w_ukv,
             ev_w_out=v_ev_w_out, od_w_qkv=v_od_w_qkv, od_rel_bias=v_od_rel_bias, od_w_out=v_od_w_out, g_mix=v_g_mix,
             g_ffn=v_g_ffn, w_gate=v_w_gate, w_up=v_w_up, w_down=v_w_down, g_final=v_g_final)
    flat2d = lambda a: a.reshape(-1, a.shape[-1])
    for tree in (w, m, v):
        for n in TRANSPOSED:
            tree[n] = jnp.swapaxes(tree[n], 1, 2)

    pos = jnp.stack([2 * lax.axis_index("x") + lax.axis_index("y"), lax.axis_index("c")]).astype(jnp.int32)

    slots = {part: _cast_into_slot("cast_" + part, w[n], layer, pos) for part, n, layer in GRAD_PARTS}
    ex = _Exchanges(slots, pos, {n: w[n].shape for n in BIG})

    loss_local, grad_x, small = _local_step(x[0], loss_target[0], {n: w[n] for n in SMALL}, ex)

    grads = ex.finish()
    small_sum = _all_reduce_small("small_sum", _pack_small(small, loss_local[0, 0]))
    grads.update(_unpack_small(small_sum, w))

    delta, new_m, new_v = {}, {}, {}
    for n in BIG:
        d_, m_, v_ = _adamw("adamw_" + n, flat2d(w[n]), flat2d(grads[n]), flat2d(m[n]), flat2d(v[n]))
        delta[n], new_m[n], new_v[n] = d_.reshape(w[n].shape), m_.reshape(w[n].shape), v_.reshape(w[n].shape)
    d_, m_, v_ = _adamw("adamw_small", _pack_small(w), small_sum, _pack_small(m), _pack_small(v))
    delta.update(_unpack_small(d_, w))
    new_m.update(_unpack_small(m_, w))
    new_v.update(_unpack_small(v_, w))
    for tree in (grads, delta, new_m, new_v):
        for n in TRANSPOSED:
            tree[n] = jnp.swapaxes(tree[n], 1, 2)

    loss = small_sum.reshape(-1)[SMALL_SIZE]
    return (loss, grad_x[None], *[grads[n] for n in WEIGHTS], *[delta[n] for n in WEIGHTS],
            *[new_m[n] for n in WEIGHTS], *[new_v[n] for n in WEIGHTS])
```

```python
import functools

import jax
import jax.numpy as jnp
import numpy as np
from jax import lax
from jax.experimental import pallas as pl
from jax.experimental.pallas import tpu as pltpu

F32 = jnp.float32
BF16 = jnp.bfloat16

S = 2048
D = 1024
CHUNK = 64
MLA_H, MLA_NOPE, MLA_ROPE, MLA_V = 8, 64, 32, 64
Q_LORA, KV_LORA = 384, 256
ROPE_THETA = 10000.0
SB_H, SB_DIM = 8, 64
C_H, C_DIM = 16, 64
LEFT_CHUNKS = 8
REL_CLIP = 256
D_FF = 2816
EVEN_IN = 2208
RMS_EPS = 1e-6
ADAM_LR, ADAM_B1, ADAM_B2, ADAM_EPS, ADAM_WD, ADAM_STEP = 0.001, 0.9, 0.999, 1e-08, 0.01, 10

N_CHIPS = 4
FF_SHARD = D_FF // N_CHIPS
SCALE_A = (MLA_NOPE + MLA_ROPE) ** -0.5
SCALE_B = SB_DIM ** -0.5
SCALE_C = C_DIM ** -0.5
NEG = -1e30

LANES = 128
VMEM_LIMIT_BYTES = 56 * 1024 * 1024
TM = 512
FFN_ROWS = 128
QB = 512
BQ = 256

P_CQ, P_CKV, P_QB, P_KB, P_VB, P_KR = 0, 512, 768, 1280, 1792, 2304
P_IN = 2432
KR_LANE = 64
BAND_W = BQ + LEFT_CHUNKS * CHUNK
BAND_PAD = 512
TOEP_W = 1024


def _params(*sem):
    return pltpu.CompilerParams(dimension_semantics=sem, vmem_limit_bytes=VMEM_LIMIT_BYTES)


MESH = pl.DeviceIdType.MESH
ANY = pl.BlockSpec(memory_space=pl.ANY)


def _position():
    x, y, c = lax.axis_index("x"), lax.axis_index("y"), lax.axis_index("c")
    other_chips = [(1 - x, y), (x, 1 - y), (1 - x, 1 - y)]
    return x, y, c, other_chips


def _half_rows(c, half):
    return pl.ds(pl.multiple_of(c * half, 16), half)


def _remote(ref_src, ref_dst, send, recv, k, device):
    return pltpu.make_async_remote_copy(src_ref=ref_src, dst_ref=ref_dst, send_sem=send.at[k], recv_sem=recv.at[k],
                                        device_id=device, device_id_type=MESH)


class _Carry:
    def __init__(self):
        self.operands, self.aliased, self.fresh = [], [], []
        self.n_sems = 0
        self.starts, self.finishes, self.on_done = [], [], []

    def operand(self, arr, aliased):
        for i, a in enumerate(self.operands):
            if a is arr:
                return i
        self.operands.append(arr)
        self.aliased.append(aliased)
        return len(self.operands) - 1

    def result(self, shape, dtype):
        self.fresh.append(jax.ShapeDtypeStruct(shape, dtype))
        return len(self.fresh) - 1

    def sems(self, k):
        base = self.n_sems
        self.n_sems += k
        return base

    def done(self, results):
        aliased, fresh = results
        for f in self.on_done:
            f(aliased, fresh)


def _carrier_call(body, *, name, grid, in_specs, out_specs, out_shape, args, sem, scratch_shapes=(), carry=None):
    in_specs, out_specs, out_shape, scratch = list(in_specs), list(out_specs), list(out_shape), list(scratch_shapes)
    if carry is None:
        res = pl.pallas_call(body, name=name, grid=grid, in_specs=in_specs, out_specs=out_specs, out_shape=out_shape,
                             scratch_shapes=scratch, compiler_params=_params(*sem))(*args)
        return list(res), None
    ops = carry.operands
    alias_idx = [i for i, a in enumerate(carry.aliased) if a]
    c_shapes = [jax.ShapeDtypeStruct(ops[i].shape, ops[i].dtype) for i in alias_idx] + carry.fresh
    n_in, n_out, n_scr = len(args), len(out_shape), len(scratch)

    def wrapped(*refs):
        ins, c_ins = refs[:n_in], refs[n_in:n_in + len(ops)]
        o0 = n_in + len(ops)
        outs, c_outs = refs[o0:o0 + n_out], refs[o0 + n_out:o0 + n_out + len(c_shapes)]
        s0 = o0 + n_out + len(c_shapes)
        scr, send, recv = refs[s0:s0 + n_scr], refs[s0 + n_scr], refs[s0 + n_scr + 1]
        use = list(c_ins)
        for k, i in enumerate(alias_idx):
            use[i] = c_outs[k]
        fresh = c_outs[len(alias_idx):]

        def run(steps):
            for step in steps:
                step(use, fresh, send, recv)

        if not grid:
            run(carry.starts)
            if body is not None:
                body(*ins, *outs, *scr)
            run(carry.finishes)
            return
        ids = [pl.program_id(a) for a in range(len(grid))]
        first = functools.reduce(jnp.logical_and, [i == 0 for i in ids])
        last = functools.reduce(jnp.logical_and, [i == g - 1 for i, g in zip(ids, grid)])

        @pl.when(first)
        def _():
            run(carry.starts)

        body(*ins, *outs, *scr)

        @pl.when(last)
        def _():
            run(carry.finishes)

    res = pl.pallas_call(
        wrapped, name=name, grid=grid, in_specs=in_specs + [ANY] * len(ops), out_specs=out_specs + [ANY] * len(c_shapes),
        out_shape=out_shape + c_shapes,
        scratch_shapes=scratch + [pltpu.SemaphoreType.DMA((carry.n_sems,)), pltpu.SemaphoreType.DMA((carry.n_sems,))],
        input_output_aliases={n_in + i: n_out + k for k, i in enumerate(alias_idx)},
        compiler_params=_params(*(("arbitrary",) * len(grid))),
    )(*args, *ops)
    res = list(res)
    c_res = res[n_out:]
    return res[:n_out], ({i: c_res[k] for k, i in enumerate(alias_idx)}, c_res[len(alias_idx):])


_DIMS = {"nn": (((1,), (0,)), ((), ())), "nt": (((1,), (1,)), ((), ())), "tn": (((0,), (0,)), ((), ()))}


def _dot(a, b, kind="nn"):
    return lax.dot_general(a, b, _DIMS[kind], preferred_element_type=F32)


def _iota(shape, dim):
    return lax.broadcasted_iota(jnp.int32, shape, dim)


def _sigmoid(x):
    return 1.0 / (1.0 + jnp.exp(-x))


def _softplus(x):
    return jnp.maximum(x, 0.0) + jnp.log(1.0 + jnp.exp(-jnp.abs(x)))


def _split_dot(x, tri):
    hi = x.astype(BF16)
    lo = (x - hi.astype(F32)).astype(BF16)
    return _dot(hi, tri) + _dot(lo, tri)


def _mm(name, a, b, *, kind, grid, a_spec, b_spec, o_spec, out_shape, out_dtype, acc_shape, resid=None, r_spec=None,
        carry=None):
    nk = grid[-1]
    has_r = resid is not None

    def body(*refs):
        a_ref, b_ref = refs[0], refs[1]
        r_ref = refs[2] if has_r else None
        o_ref = refs[2 + has_r]
        part = _dot(a_ref[...].astype(BF16), b_ref[...].astype(BF16), kind)

        def finish(total):
            if has_r:
                total = total + r_ref[...].astype(F32)
            o_ref[...] = total.astype(out_dtype)

        if nk == 1:
            finish(part)
        else:
            acc_ref = refs[3 + has_r]
            k = pl.program_id(len(grid) - 1)

            @pl.when(k == 0)
            def _():
                acc_ref[...] = part

            @pl.when(k > 0)
            def _():
                acc_ref[...] += part

            @pl.when(k == nk - 1)
            def _():
                finish(acc_ref[...])

    in_specs = [a_spec, b_spec] + ([r_spec] if has_r else [])
    args = (a, b) + ((resid,) if has_r else ())
    sem = ("parallel",) * (len(grid) - 1) + ("arbitrary",)
    res, copies = _carrier_call(
        body, name=name, grid=grid, in_specs=in_specs, out_specs=[o_spec],
        out_shape=[jax.ShapeDtypeStruct(out_shape, out_dtype)],
        scratch_shapes=[pltpu.VMEM(acc_shape, F32)] if nk > 1 else [], args=args, sem=sem, carry=carry)
    if carry is not None:
        carry.done(copies)
    return res[0]


def _rms_fwd(name, x, g, col_block=0):
    c = g.shape[1]

    def body(x_ref, g_ref, u_ref):
        xv = x_ref[...]
        r = lax.rsqrt(jnp.mean(xv * xv, axis=-1, keepdims=True) + RMS_EPS)
        u_ref[...] = (xv * r * g_ref[...]).astype(BF16)

    return pl.pallas_call(
        body, name=name, grid=(S // TM,),
        in_specs=[pl.BlockSpec((TM, c), lambda i: (i, col_block)), pl.BlockSpec((1, c), lambda i: (0, 0))],
        out_specs=pl.BlockSpec((TM, c), lambda i: (i, 0)),
        out_shape=jax.ShapeDtypeStruct((S, c), BF16),
        compiler_params=_params("parallel"),
    )(x, g)


def _rms_bwd(name, dy, x, g, resid, carry=None):
    def body(dy_ref, x_ref, g_ref, r_ref, dx_ref, dg_ref):
        i = pl.program_id(0)
        xv = x_ref[...]
        r = lax.rsqrt(jnp.mean(xv * xv, axis=-1, keepdims=True) + RMS_EPS)
        xh = xv * r
        dyv = dy_ref[...]
        dxh = dyv * g_ref[...]
        dx_ref[...] = r_ref[...] + r * (dxh - xh * jnp.mean(dxh * xh, axis=-1, keepdims=True))
        part = jnp.sum(dyv * xh, axis=0, keepdims=True)

        @pl.when(i == 0)
        def _():
            dg_ref[...] = part

        @pl.when(i > 0)
        def _():
            dg_ref[...] += part

    row = pl.BlockSpec((TM, D), lambda i: (i, 0))
    vec = pl.BlockSpec((1, D), lambda i: (0, 0))
    res, copies = _carrier_call(
        body, name=name, grid=(S // TM,), in_specs=[row, row, vec, row], out_specs=[row, vec],
        out_shape=[jax.ShapeDtypeStruct((S, D), F32), jax.ShapeDtypeStruct((1, D), F32)],
        args=(dy, x, g, resid), sem=("arbitrary",), carry=carry)
    if carry is not None:
        carry.done(copies)
    return res


def _loss_bwd(name, h, g, tgt):
    def body(h_ref, g_ref, t_ref, loss_ref, dh_ref, dg_ref):
        i = pl.program_id(0)
        xv = h_ref[...]
        gv = g_ref[...]
        r = lax.rsqrt(jnp.mean(xv * xv, axis=-1, keepdims=True) + RMS_EPS)
        xh = xv * r
        diff = xh * gv - t_ref[...]
        part_loss = 0.5 * jnp.sum(jnp.sum(diff * diff, axis=-1, keepdims=True) * (1.0 / D), axis=0, keepdims=True)
        dy = diff * (1.0 / D)
        dxh = dy * gv
        dh_ref[...] = r * (dxh - xh * jnp.mean(dxh * xh, axis=-1, keepdims=True))
        part_g = jnp.sum(dy * xh, axis=0, keepdims=True)

        @pl.when(i == 0)
        def _():
            dg_ref[...] = part_g
            loss_ref[...] = jnp.broadcast_to(part_loss, (1, LANES))

        @pl.when(i > 0)
        def _():
            dg_ref[...] += part_g
            loss_ref[...] += jnp.broadcast_to(part_loss, (1, LANES))

    row = pl.BlockSpec((TM, D), lambda i: (i, 0))
    vec = pl.BlockSpec((1, D), lambda i: (0, 0))
    return pl.pallas_call(
        body, name=name, grid=(S // TM,), in_specs=[row, vec, row],
        out_specs=[pl.BlockSpec((1, LANES), lambda i: (0, 0)), row, vec],
        out_shape=[jax.ShapeDtypeStruct((1, LANES), F32), jax.ShapeDtypeStruct((S, D), F32),
                   jax.ShapeDtypeStruct((1, D), F32)],
        compiler_params=_params("arbitrary"),
    )(h, g, tgt)


def _ffn_fwd(name, h, g, wg, wu, wd, carry=None):
    def body(h_ref, g_ref, wg_ref, wu_ref, wd_ref, o_ref, gate_ref, up_ref, u_scr):
        s = pl.program_id(1)

        @pl.when(s == 0)
        def _():
            xv = h_ref[...]
            r = lax.rsqrt(jnp.mean(xv * xv, axis=-1, keepdims=True) + RMS_EPS)
            u_scr[...] = (xv * r * g_ref[...]).astype(BF16)
            o_ref[...] = xv

        u = u_scr[...]
        gate = _dot(u, wg_ref[...], "nt")
        up = _dot(u, wu_ref[...], "nt")
        act = gate * _sigmoid(gate) * up
        o_ref[...] += _dot(act.astype(BF16), wd_ref[...])
        gate_ref[...] = gate.astype(BF16)
        up_ref[...] = up.astype(BF16)

    row = pl.BlockSpec((TM, D), lambda i, s: (i, 0))
    hid = pl.BlockSpec((None, TM, FF_SHARD), lambda i, s: (s, i, 0))
    return _carrier_call(
        body, name=name, grid=(S // TM, N_CHIPS),
        in_specs=[row, pl.BlockSpec((1, D), lambda i, s: (0, 0))]
        + [pl.BlockSpec((None, FF_SHARD, D), lambda i, s: (s, 0, 0))] * 3,
        out_specs=[row, hid, hid],
        out_shape=[jax.ShapeDtypeStruct((S, D), F32), jax.ShapeDtypeStruct((N_CHIPS, S, FF_SHARD), BF16),
                   jax.ShapeDtypeStruct((N_CHIPS, S, FF_SHARD), BF16)],
        scratch_shapes=[pltpu.VMEM((TM, D), BF16)], args=(h, g, wg, wu, wd), sem=("parallel", "arbitrary"), carry=carry)


def _ffn_bwd(name, dh, h, g, gate, up, wg, wu, wd):
    def body(dh_ref, h_ref, g_ref, gate_ref, up_ref, wg_ref, wu_ref, wd_ref,
             dhin_ref, dg_ref, u_ref, dgate_ref, dup_ref, act_ref, dhb_scr, du_scr):
        i = pl.program_id(0)
        s = pl.program_id(1)

        @pl.when(s == 0)
        def _():
            xv = h_ref[...]
            r = lax.rsqrt(jnp.mean(xv * xv, axis=-1, keepdims=True) + RMS_EPS)
            u_ref[...] = (xv * r * g_ref[...]).astype(BF16)
            dhb_scr[...] = dh_ref[...].astype(BF16)
            du_scr[...] = jnp.zeros_like(du_scr)

        wd, wg, wu = wd_ref[...], wg_ref[...], wu_ref[...]
        for r in range(TM // FFN_ROWS):
            rows = pl.ds(r * FFN_ROWS, FFN_ROWS)
            dact = _dot(dhb_scr[rows, :], wd, "nt")
            gv = gate_ref[rows, :].astype(F32)
            uv = up_ref[rows, :].astype(F32)
            sig = _sigmoid(gv)
            sil = gv * sig
            dup = dact * sil
            dgate = dact * uv * (sig * (1.0 + gv * (1.0 - sig)))
            dgb = dgate.astype(BF16)
            dub = dup.astype(BF16)
            act_ref[rows, :] = (sil * uv).astype(BF16)
            dgate_ref[rows, :] = dgb
            dup_ref[rows, :] = dub
            du_scr[rows, :] += _dot(dgb, wg) + _dot(dub, wu)

        @pl.when(s == N_CHIPS - 1)
        def _():
            xv = h_ref[...]
            r = lax.rsqrt(jnp.mean(xv * xv, axis=-1, keepdims=True) + RMS_EPS)
            xh = xv * r
            du = du_scr[...]
            dxh = du * g_ref[...]
            dhin_ref[...] = dh_ref[...] + r * (dxh - xh * jnp.mean(dxh * xh, axis=-1, keepdims=True))
            part = jnp.sum(du * xh, axis=0, keepdims=True)

            @pl.when(i == 0)
            def _():
                dg_ref[...] = part

            @pl.when(i > 0)
            def _():
                dg_ref[...] += part

    row = pl.BlockSpec((TM, D), lambda i, s: (i, 0))
    vec = pl.BlockSpec((1, D), lambda i, s: (0, 0))
    hid = pl.BlockSpec((None, TM, FF_SHARD), lambda i, s: (s, i, 0))
    hid_shape = jax.ShapeDtypeStruct((N_CHIPS, S, FF_SHARD), BF16)
    return pl.pallas_call(
        body, name=name, grid=(S // TM, N_CHIPS),
        in_specs=[row, row, vec, hid, hid] + [pl.BlockSpec((None, FF_SHARD, D), lambda i, s: (s, 0, 0))] * 3,
        out_specs=[row, vec, row, hid, hid, hid],
        out_shape=[jax.ShapeDtypeStruct((S, D), F32), jax.ShapeDtypeStruct((1, D), F32),
                   jax.ShapeDtypeStruct((S, D), BF16), hid_shape, hid_shape, hid_shape],
        scratch_shapes=[pltpu.VMEM((TM, D), BF16), pltpu.VMEM((TM, D), F32)],
        compiler_params=_params("arbitrary", "arbitrary"),
    )(dh, h, g, gate, up, wg, wu, wd)


def _ffn_wgrads(name, u, dgate, dup, act, dh):
    nk = S // TM

    def body(u_ref, dh_ref, dgate_ref, dup_ref, act_ref, dg_ref, du_ref, dd_ref, acc_g, acc_u, acc_d):
        k = pl.program_id(1)
        u = u_ref[...]
        parts = (_dot(dgate_ref[...], u, "tn"), _dot(dup_ref[...], u, "tn"),
                 _dot(act_ref[...], dh_ref[...].astype(BF16), "tn"))
        accs = (acc_g, acc_u, acc_d)

        @pl.when(k == 0)
        def _():
            for acc, part in zip(accs, parts):
                acc[...] = part

        @pl.when(k > 0)
        def _():
            for acc, part in zip(accs, parts):
                acc[...] += part

        @pl.when(k == nk - 1)
        def _():
            for out, acc in zip((dg_ref, du_ref, dd_ref), accs):
                out[...] = acc[...].astype(BF16)

    tok = pl.BlockSpec((TM, D), lambda s, k: (k, 0))
    hid = pl.BlockSpec((None, TM, FF_SHARD), lambda s, k: (s, k, 0))
    out = pl.BlockSpec((None, FF_SHARD, D), lambda s, k: (s, 0, 0))
    shape = jax.ShapeDtypeStruct((N_CHIPS, FF_SHARD, D), BF16)
    return pl.pallas_call(
        body, name=name, grid=(N_CHIPS, nk), in_specs=[tok, tok, hid, hid, hid], out_specs=[out, out, out],
        out_shape=[shape, shape, shape], scratch_shapes=[pltpu.VMEM((FF_SHARD, D), F32)] * 3,
        compiler_params=_params("parallel", "arbitrary"))(u, dh, dgate, dup, act)


def _rope_tables():
    pos = jnp.arange(S, dtype=F32)
    inv = ROPE_THETA ** (-jnp.arange(0, MLA_ROPE, 2, dtype=F32) / MLA_ROPE)
    ang = pos[:, None] * inv[None, :]
    half = MLA_ROPE // 2
    cos = jnp.cos(ang)
    sin = jnp.sin(ang)
    one = jnp.ones((S, KR_LANE), F32)
    zero = jnp.zeros((S, KR_LANE), F32)
    tail_one = jnp.ones((S, LANES - KR_LANE - MLA_ROPE), F32)
    tail_zero = jnp.zeros((S, LANES - KR_LANE - MLA_ROPE), F32)
    cos_t = jnp.concatenate([one, cos, cos, tail_one], axis=1)
    sin_t = jnp.concatenate([zero, -sin, sin, tail_zero], axis=1)
    assert cos_t.shape == (S, LANES) and half * 2 == MLA_ROPE
    return cos_t, sin_t


def _rope(x, cos_t, sin_t, sign):
    n = x.shape[1] // LANES
    half = MLA_ROPE // 2
    lane = _iota(x.shape, 1) & (LANES - 1)
    first = (lane >= KR_LANE) & (lane < KR_LANE + half)
    swapped = jnp.where(first, pltpu.roll(x, x.shape[1] - half, 1), pltpu.roll(x, half, 1))
    c = jnp.tile(cos_t, (1, n)) if n > 1 else cos_t
    s = jnp.tile(sin_t, (1, n)) if n > 1 else sin_t
    return x * c + swapped * (s * sign)


def _mla_prep_fwd(name, proj, g_cq, g_ckv, w_uq, w_uk, w_uv, cos_t, sin_t):
    nh = MLA_H * LANES

    def body(cq_ref, ckv_ref, kr_ref, gq_ref, gkv_ref, wq_ref, wk_ref, wv_ref, cos_ref, sin_ref,
             qa_ref, ka_ref, va_ref):
        cos_v, sin_v = cos_ref[...], sin_ref[...]
        cq = cq_ref[...]
        r = lax.rsqrt(jnp.mean(cq * cq, axis=-1, keepdims=True) + RMS_EPS)
        cqn = (cq * r * gq_ref[...]).astype(BF16)
        qa_ref[...] = _rope(_dot(cqn, wq_ref[...]), cos_v, sin_v, 1.0).astype(BF16)
        ckv = ckv_ref[...]
        r = lax.rsqrt(jnp.mean(ckv * ckv, axis=-1, keepdims=True) + RMS_EPS)
        ckvn = (ckv * r * gkv_ref[...]).astype(BF16)
        lane = _iota((TM, LANES), 1)
        rot = (lane >= KR_LANE) & (lane < KR_LANE + MLA_ROPE)
        kr = jnp.where(rot, _rope(kr_ref[...], cos_v, sin_v, 1.0), 0.0)
        ka_ref[...] = (_dot(ckvn, wk_ref[...]) + jnp.tile(kr, (1, MLA_H))).astype(BF16)
        va_ref[...] = _dot(ckvn, wv_ref[...]).astype(BF16)

    full = lambda shape: pl.BlockSpec(shape, lambda i: (0, 0))
    return pl.pallas_call(
        body, name=name, grid=(S // TM,),
        in_specs=[pl.BlockSpec((TM, Q_LORA), lambda i: (i, P_CQ // Q_LORA)),
                  pl.BlockSpec((TM, KV_LORA), lambda i: (i, P_CKV // KV_LORA)),
                  pl.BlockSpec((TM, LANES), lambda i: (i, P_KR // LANES)),
                  full((1, Q_LORA)), full((1, KV_LORA)), full((Q_LORA, nh)), full((KV_LORA, nh)),
                  full((KV_LORA, MLA_H * MLA_V)),
                  pl.BlockSpec((TM, LANES), lambda i: (i, 0)), pl.BlockSpec((TM, LANES), lambda i: (i, 0))],
        out_specs=[pl.BlockSpec((TM, nh), lambda i: (i, 0)), pl.BlockSpec((TM, nh), lambda i: (i, 0)),
                   pl.BlockSpec((TM, MLA_H * MLA_V), lambda i: (i, 0))],
        out_shape=[jax.ShapeDtypeStruct((S, nh), BF16), jax.ShapeDtypeStruct((S, nh), BF16),
                   jax.ShapeDtypeStruct((S, MLA_H * MLA_V), BF16)],
        compiler_params=_params("parallel"),
    )(proj, proj, proj, g_cq, g_ckv, w_uq, w_uk, w_uv, cos_t, sin_t)


def _mla_prep_bwd(name, dqa, dka, dva, proj, g_cq, g_ckv, w_uq, w_uk, w_uv, cos_t, sin_t):
    nh = MLA_H * LANES

    def body(dqa_ref, dka_ref, dva_ref, cq_ref, ckv_ref, gq_ref, gkv_ref, wq_ref, wk_ref, wv_ref, cos_ref, sin_ref,
             dcq_ref, dckv_ref, dkr_ref, dwq_ref, dwk_ref, dwv_ref, dgq_ref, dgkv_ref):
        i = pl.program_id(0)
        cos_v, sin_v = cos_ref[...], sin_ref[...]

        def norm_bwd(x, g, dn):
            r = lax.rsqrt(jnp.mean(x * x, axis=-1, keepdims=True) + RMS_EPS)
            xh = x * r
            dxh = dn * g
            dx = r * (dxh - xh * jnp.mean(dxh * xh, axis=-1, keepdims=True))
            return dx, jnp.sum(dn * xh, axis=0, keepdims=True), (xh * g).astype(BF16)

        dq = _rope(dqa_ref[...], cos_v, sin_v, -1.0).astype(BF16)
        dcqn = _dot(dq, wq_ref[...], "nt")
        dcq, dgq, cqn = norm_bwd(cq_ref[...], gq_ref[...], dcqn)
        dcq_ref[...] = dcq.astype(BF16)
        dwq = _dot(cqn, dq, "tn")

        dka = dka_ref[...]
        dkab = dka.astype(BF16)
        dvab = dva_ref[...].astype(BF16)
        dckvn = _dot(dkab, wk_ref[...], "nt") + _dot(dvab, wv_ref[...], "nt")
        dckv, dgkv, ckvn = norm_bwd(ckv_ref[...], gkv_ref[...], dckvn)
        dckv_ref[...] = dckv.astype(BF16)
        dwk = _dot(ckvn, dkab, "tn")
        dwv = _dot(ckvn, dvab, "tn")

        fold = dka[:, 0:LANES]
        for hh in range(1, MLA_H):
            fold = fold + dka[:, hh * LANES:(hh + 1) * LANES]
        lane = _iota((TM, LANES), 1)
        rot = (lane >= KR_LANE) & (lane < KR_LANE + MLA_ROPE)
        dkr = _rope(jnp.where(rot, fold, 0.0), cos_v, sin_v, -1.0)
        dkr_ref[...] = jnp.where(rot, dkr, 0.0).astype(BF16)

        @pl.when(i == 0)
        def _():
            dwq_ref[...] = dwq
            dwk_ref[...] = dwk
            dwv_ref[...] = dwv
            dgq_ref[...] = dgq
            dgkv_ref[...] = dgkv

        @pl.when(i > 0)
        def _():
            dwq_ref[...] += dwq
            dwk_ref[...] += dwk
            dwv_ref[...] += dwv
            dgq_ref[...] += dgq
            dgkv_ref[...] += dgkv

    full = lambda shape: pl.BlockSpec(shape, lambda i: (0, 0))
    rows = lambda c: pl.BlockSpec((TM, c), lambda i: (i, 0))
    nv = MLA_H * MLA_V
    return pl.pallas_call(
        body, name=name, grid=(S // TM,),
        in_specs=[rows(nh), rows(nh), rows(nv),
                  pl.BlockSpec((TM, Q_LORA), lambda i: (i, P_CQ // Q_LORA)),
                  pl.BlockSpec((TM, KV_LORA), lambda i: (i, P_CKV // KV_LORA)),
                  full((1, Q_LORA)), full((1, KV_LORA)), full((Q_LORA, nh)), full((KV_LORA, nh)), full((KV_LORA, nv)),
                  rows(LANES), rows(LANES)],
        out_specs=[rows(Q_LORA), rows(KV_LORA), rows(LANES), full((Q_LORA, nh)), full((KV_LORA, nh)),
                   full((KV_LORA, nv)), full((1, Q_LORA)), full((1, KV_LORA))],
        out_shape=[jax.ShapeDtypeStruct((S, Q_LORA), BF16), jax.ShapeDtypeStruct((S, KV_LORA), BF16),
                   jax.ShapeDtypeStruct((S, LANES), BF16), jax.ShapeDtypeStruct((Q_LORA, nh), F32),
                   jax.ShapeDtypeStruct((KV_LORA, nh), F32), jax.ShapeDtypeStruct((KV_LORA, nv), F32),
                   jax.ShapeDtypeStruct((1, Q_LORA), F32), jax.ShapeDtypeStruct((1, KV_LORA), F32)],
        compiler_params=_params("arbitrary"),
    )(dqa, dka, dva, proj, proj, g_cq, g_ckv, w_uq, w_uk, w_uv, cos_t, sin_t)


def _head_masks(dtype):
    lane = _iota((1, LANES), 1)
    return (lane < 64).astype(dtype), (lane >= 64).astype(dtype)


def _mla_fwd(name, qa, ka, va, carry=None):
    def body(q_ref, k_ref, v_ref, o_ref, lse_ref):
        m0b, m1b = _head_masks(BF16)
        lane = _iota((QB, LANES), 1)
        left = lane < 64

        def qblock(i, _):
            r0 = pl.multiple_of(i * QB, QB)
            qs = [q_ref[pl.ds(r0, QB), hh * LANES:(hh + 1) * LANES] for hh in range(2)]
            rowc = lax.shift_right_logical(r0 + _iota((QB, QB), 0), 6)

            def kv(kb, carry):
                ms, ls, acc = carry
                c0 = pl.multiple_of(kb * QB, QB)
                v = v_ref[pl.ds(c0, QB), :]
                ok = lax.shift_right_logical(c0 + _iota((QB, QB), 1), 6) <= rowc
                new_m, new_l, alphas = [], [], []
                pv = None
                for hh in range(2):
                    k = k_ref[pl.ds(c0, QB), hh * LANES:(hh + 1) * LANES]
                    s = jnp.where(ok, _dot(qs[hh], k, "nt") * SCALE_A, NEG)
                    mn = jnp.maximum(ms[hh], jnp.max(s, axis=-1, keepdims=True))
                    p = jnp.exp(s - mn)
                    a = jnp.exp(ms[hh] - mn)
                    new_m.append(mn)
                    new_l.append(a * ls[hh] + jnp.sum(p, axis=-1, keepdims=True))
                    alphas.append(a)
                    part = _dot(p.astype(BF16), v * (m0b if hh == 0 else m1b))
                    pv = part if pv is None else pv + part
                acc = acc * jnp.where(left, alphas[0], alphas[1]) + pv
                return tuple(new_m), tuple(new_l), acc

            init = ((jnp.full((QB, 1), NEG, F32),) * 2, (jnp.zeros((QB, 1), F32),) * 2, jnp.zeros((QB, LANES), F32))
            ms, ls, acc = lax.fori_loop(0, i + 1, kv, init)
            o_ref[pl.ds(r0, QB), :] = acc * jnp.where(left, 1.0 / ls[0], 1.0 / ls[1])
            lse_ref[pl.ds(r0, QB), :] = jnp.where(left, ms[0] + jnp.log(ls[0]), ms[1] + jnp.log(ls[1]))
            return 0

        lax.fori_loop(0, S // QB, qblock, 0)

    pair = lambda w: pl.BlockSpec((S, w), lambda p: (0, p))
    return _carrier_call(
        body, name=name, grid=(MLA_H // 2,), in_specs=[pair(2 * LANES), pair(2 * LANES), pair(LANES)],
        out_specs=[pair(LANES), pair(LANES)],
        out_shape=[jax.ShapeDtypeStruct((S, MLA_H * MLA_V), F32), jax.ShapeDtypeStruct((S, MLA_H * MLA_V), F32)],
        args=(qa, ka, va), sem=("parallel",), carry=carry)


def _mla_bwd(name, qa, ka, va, o, lse, do, do_block0, carry=None):
    def body(q_ref, k_ref, v_ref, o_ref, lse_ref, do_ref, dq_ref, dk_ref, dv_ref):
        m0f, m1f = _head_masks(F32)
        m0b, m1b = _head_masks(BF16)
        dk_ref[...] = jnp.zeros_like(dk_ref)
        dv_ref[...] = jnp.zeros_like(dv_ref)

        def qblock(i, _):
            r0 = pl.multiple_of(i * QB, QB)
            rows = pl.ds(r0, QB)
            do_f = do_ref[rows, :]
            prod = do_f * o_ref[rows, :]
            deltas = [jnp.sum(prod * m0f, axis=-1, keepdims=True), jnp.sum(prod * m1f, axis=-1, keepdims=True)]
            lse_v = lse_ref[rows, :]
            lses = [lse_v[:, 0:1], lse_v[:, 64:65]]
            dob = do_f.astype(BF16)
            dos = [dob * m0b, dob * m1b]
            qs = [q_ref[rows, hh * LANES:(hh + 1) * LANES] for hh in range(2)]
            rowc = lax.shift_right_logical(r0 + _iota((QB, QB), 0), 6)

            def kv(kb, dqs):
                c0 = pl.multiple_of(kb * QB, QB)
                cols = pl.ds(c0, QB)
                v = v_ref[cols, :]
                ok = lax.shift_right_logical(c0 + _iota((QB, QB), 1), 6) <= rowc
                out = []
                dv = None
                for hh in range(2):
                    k = k_ref[cols, hh * LANES:(hh + 1) * LANES]
                    s = _dot(qs[hh], k, "nt") * SCALE_A
                    p = jnp.where(ok, jnp.exp(s - lses[hh]), 0.0)
                    dp = _dot(dos[hh], v, "nt")
                    ds = (p * (dp - deltas[hh]) * SCALE_A).astype(BF16)
                    out.append(dqs[hh] + _dot(ds, k))
                    dk_ref[cols, hh * LANES:(hh + 1) * LANES] += _dot(ds, qs[hh], "tn")
                    part = _dot(p.astype(BF16), dos[hh], "tn")
                    dv = part if dv is None else dv + part
                dv_ref[cols, :] += dv
                return tuple(out)

            dqs = lax.fori_loop(0, i + 1, kv, (jnp.zeros((QB, LANES), F32),) * 2)
            for hh in range(2):
                dq_ref[rows, hh * LANES:(hh + 1) * LANES] = dqs[hh]
            return 0

        lax.fori_loop(0, S // QB, qblock, 0)

    pair = lambda w: pl.BlockSpec((S, w), lambda p: (0, p))
    return _carrier_call(
        body, name=name, grid=(MLA_H // 2,),
        in_specs=[pair(2 * LANES), pair(2 * LANES), pair(LANES), pair(LANES), pair(LANES),
                  pl.BlockSpec((S, LANES), lambda p: (0, do_block0 + p))],
        out_specs=[pair(2 * LANES), pair(2 * LANES), pair(LANES)],
        out_shape=[jax.ShapeDtypeStruct((S, MLA_H * LANES), F32), jax.ShapeDtypeStruct((S, MLA_H * LANES), F32),
                   jax.ShapeDtypeStruct((S, MLA_H * MLA_V), F32)],
        args=(qa, ka, va, o, lse, do), sem=("parallel",), carry=carry)


def _sb_weights(q_h, k, c, before, tri_suffix):
    z = _dot(q_h, k, "nt") * SCALE_B
    sp = _softplus(z)
    log_keep = jnp.where(before, -sp, 0.0)
    log_between = _split_dot(log_keep, tri_suffix) + c
    w = jnp.where(before, jnp.exp(z - sp + log_between), 0.0)
    return w, jnp.exp(z - sp), jnp.sum(log_keep, axis=-1, keepdims=True)


def _sb_fwd(name, proj, carry=None):
    def body(q_ref, k_ref, v_ref, o_ref):
        m0b, m1b = _head_masks(BF16)
        tri_suffix = (_iota((QB, QB), 0) > _iota((QB, QB), 1)).astype(BF16)

        def qblock(i, _):
            r0 = pl.multiple_of(i * QB, QB)
            q = q_ref[pl.ds(r0, QB), :].astype(BF16)
            qs = [q * m0b, q * m1b]
            rowg = r0 + _iota((QB, QB), 0)

            def kv(step, carry):
                cs, acc = carry
                c0 = pl.multiple_of((i - step) * QB, QB)
                k = k_ref[pl.ds(c0, QB), :].astype(BF16)
                v = v_ref[pl.ds(c0, QB), :].astype(BF16)
                before = (c0 + _iota((QB, QB), 1)) < rowg
                new_c = []
                for hh in range(2):
                    w, _, tot = _sb_weights(qs[hh], k, cs[hh], before, tri_suffix)
                    new_c.append(cs[hh] + tot)
                    acc = acc + _dot(w.astype(BF16), v * (m0b if hh == 0 else m1b))
                return tuple(new_c), acc

            init = ((jnp.zeros((QB, 1), F32),) * 2, jnp.zeros((QB, LANES), F32))
            _, acc = lax.fori_loop(0, i + 1, kv, init)
            o_ref[pl.ds(r0, QB), :] = acc.astype(BF16)
            return 0

        lax.fori_loop(0, S // QB, qblock, 0)

    col = lambda base: pl.BlockSpec((S, LANES), lambda p: (0, base // LANES + p))
    return _carrier_call(
        body, name=name, grid=(SB_H // 2,), in_specs=[col(P_QB), col(P_KB), col(P_VB)],
        out_specs=[pl.BlockSpec((S, LANES), lambda p: (0, p))],
        out_shape=[jax.ShapeDtypeStruct((S, SB_H * SB_DIM), BF16)],
        args=(proj, proj, proj), sem=("parallel",), carry=carry)


def _sb_bwd(name, proj, do, do_block0, carry=None):
    nb = S // QB

    def body(q_ref, k_ref, v_ref, do_ref, dq_ref, dk_ref, dv_ref, sig_scr, dl_scr, dk_acc, dv_acc):
        m0b, m1b = _head_masks(BF16)
        tri_suffix = (_iota((QB, QB), 0) > _iota((QB, QB), 1)).astype(BF16)
        tri_prefix = (_iota((QB, QB), 0) < _iota((QB, QB), 1)).astype(BF16)
        dk_acc[...] = jnp.zeros_like(dk_acc)
        dv_acc[...] = jnp.zeros_like(dv_acc)

        def qblock(i, _):
            r0 = pl.multiple_of(i * QB, QB)
            rows = pl.ds(r0, QB)
            q = q_ref[rows, :].astype(BF16)
            qs = [q * m0b, q * m1b]
            dob = do_ref[rows, :].astype(BF16)
            dos = [dob * m0b, dob * m1b]
            rowg = r0 + _iota((QB, QB), 0)

            def sweep_left(step, cs):
                kb = i - step
                c0 = pl.multiple_of(kb * QB, QB)
                cols = pl.ds(c0, QB)
                k = k_ref[cols, :].astype(BF16)
                v = v_ref[cols, :].astype(BF16)
                before = (c0 + _iota((QB, QB), 1)) < rowg
                new_c = []
                dv = None
                for hh in range(2):
                    w, sig, tot = _sb_weights(qs[hh], k, cs[hh], before, tri_suffix)
                    new_c.append(cs[hh] + tot)
                    sig_scr[hh, kb] = sig
                    dl_scr[hh, kb] = _dot(dos[hh], v, "nt") * w
                    part = _dot(w.astype(BF16), dos[hh], "tn")
                    dv = part if dv is None else dv + part
                dv_acc[cols, :] += dv
                return tuple(new_c)

            lax.fori_loop(0, i + 1, sweep_left, (jnp.zeros((QB, 1), F32),) * 2)

            def sweep_right(kb, carry):
                ps, dq = carry
                c0 = pl.multiple_of(kb * QB, QB)
                cols = pl.ds(c0, QB)
                k = k_ref[cols, :].astype(BF16)
                before = (c0 + _iota((QB, QB), 1)) < rowg
                new_p = []
                dk = None
                for hh in range(2):
                    dl = dl_scr[hh, kb]
                    sig = sig_scr[hh, kb]
                    earlier = _split_dot(dl, tri_prefix) + ps[hh]
                    new_p.append(ps[hh] + jnp.sum(dl, axis=-1, keepdims=True))
                    dz = (jnp.where(before, dl * (1.0 - sig) - earlier * sig, 0.0) * SCALE_B).astype(BF16)
                    dq = dq + _dot(dz, k * (m0b if hh == 0 else m1b))
                    part = _dot(dz, qs[hh], "tn")
                    dk = part if dk is None else dk + part
                dk_acc[cols, :] += dk
                return tuple(new_p), dq

            init = ((jnp.zeros((QB, 1), F32),) * 2, jnp.zeros((QB, LANES), F32))
            _, dq = lax.fori_loop(0, i + 1, sweep_right, init)
            dq_ref[rows, :] = dq.astype(BF16)
            return 0

        lax.fori_loop(0, nb, qblock, 0)
        dk_ref[...] = dk_acc[...].astype(BF16)
        dv_ref[...] = dv_acc[...].astype(BF16)

    col = lambda base: pl.BlockSpec((S, LANES), lambda p: (0, base // LANES + p))
    out = pl.BlockSpec((S, LANES), lambda p: (0, p))
    shape = jax.ShapeDtypeStruct((S, SB_H * SB_DIM), BF16)
    return _carrier_call(
        body, name=name, grid=(SB_H // 2,),
        in_specs=[col(P_QB), col(P_KB), col(P_VB), pl.BlockSpec((S, LANES), lambda p: (0, do_block0 + p))],
        out_specs=[out, out, out], out_shape=[shape, shape, shape],
        scratch_shapes=[pltpu.VMEM((2, nb, QB, QB), F32), pltpu.VMEM((2, nb, QB, QB), F32),
                        pltpu.VMEM((S, LANES), F32), pltpu.VMEM((S, LANES), F32)],
        args=(proj, proj, proj, do), sem=("parallel",), carry=carry)


def _band_row_index():
    j = np.arange(TOEP_W)
    rel = np.clip(LEFT_CHUNKS * CHUNK - j, -REL_CLIP, REL_CLIP) + REL_CLIP
    rel[BAND_W:] = 2 * REL_CLIP
    return rel.astype(np.int32)


def _band_tiles(r0_ref, q_ref, kpad, vpad, m, m0b, m1b, static_ok, bias):
    r0 = pl.multiple_of(m * BQ, BQ)
    q = q_ref[0, pl.ds(r0, BQ), :]
    kw = kpad[pl.ds(r0, BAND_W), :]
    vw = vpad[pl.ds(r0, BAND_W), :]
    ok = static_ok & ((r0 - BAND_PAD + _iota((BQ, BAND_W), 1)) >= 0)
    qs = [q * m0b, q * m1b]
    ps = []
    for hh in range(2):
        s = jnp.where(ok, _dot(qs[hh], kw, "nt") * SCALE_C + bias[hh], NEG)
        e = jnp.exp(s - jnp.max(s, axis=-1, keepdims=True))
        ps.append(e * (1.0 / jnp.sum(e, axis=-1, keepdims=True)))
    return r0, qs, kw, vw, ps


def _band_setup(qkv_ref, r0_ref, kpad, vpad):
    kpad[0:BAND_PAD, :] = jnp.zeros((BAND_PAD, LANES), BF16)
    vpad[0:BAND_PAD, :] = jnp.zeros((BAND_PAD, LANES), BF16)
    kpad[BAND_PAD:, :] = qkv_ref[1]
    vpad[BAND_PAD:, :] = qkv_ref[2]
    jc = lax.shift_right_logical(_iota((BQ, BAND_W), 1), 6)
    rc = lax.shift_right_logical(_iota((BQ, BAND_W), 0), 6)
    static_ok = (jc >= rc) & (jc <= rc + LEFT_CHUNKS)
    bias = []
    for hh in range(2):
        row = jnp.broadcast_to(r0_ref[hh:hh + 1, :], (BQ, TOEP_W))
        bias.append(pltpu.roll(row, 0, 1, stride=1, stride_axis=0)[:, :BAND_W])
    return static_ok, bias


def _band_fwd(name, qkv, r0, carry=None):
    def body(qkv_ref, r0_ref, o_ref, kpad, vpad):
        m0b, m1b = _head_masks(BF16)
        static_ok, bias = _band_setup(qkv_ref, r0_ref, kpad, vpad)

        def qblock(m, _):
            r0_, _, _, vw, ps = _band_tiles(r0_ref, qkv_ref, kpad, vpad, m, m0b, m1b, static_ok, bias)
            o = _dot(ps[0].astype(BF16), vw * m0b) + _dot(ps[1].astype(BF16), vw * m1b)
            o_ref[pl.ds(r0_, BQ), :] = o.astype(BF16)
            return 0

        lax.fori_loop(0, S // BQ, qblock, 0)

    return _carrier_call(
        body, name=name, grid=(C_H // 2,),
        in_specs=[pl.BlockSpec((3, S, LANES), lambda p: (0, 0, p)), pl.BlockSpec((None, 2, TOEP_W), lambda p: (p, 0, 0))],
        out_specs=[pl.BlockSpec((S, LANES), lambda p: (0, p))],
        out_shape=[jax.ShapeDtypeStruct((S, C_H * C_DIM), BF16)],
        scratch_shapes=[pltpu.VMEM((S + BAND_PAD, LANES), BF16), pltpu.VMEM((S + BAND_PAD, LANES), BF16)],
        args=(qkv, r0), sem=("parallel",), carry=carry)


def _band_bwd(name, qkv, r0, do, carry=None):
    def body(qkv_ref, r0_ref, do_ref, dqkv_ref, dr0_ref, kpad, vpad, dkpad, dvpad, db_acc):
        m0b, m1b = _head_masks(BF16)
        static_ok, bias = _band_setup(qkv_ref, r0_ref, kpad, vpad)
        dkpad[...] = jnp.zeros_like(dkpad)
        dvpad[...] = jnp.zeros_like(dvpad)
        db_acc[...] = jnp.zeros_like(db_acc)

        def qblock(m, _):
            r0_, qs, kw, vw, ps = _band_tiles(r0_ref, qkv_ref, kpad, vpad, m, m0b, m1b, static_ok, bias)
            dob = do_ref[pl.ds(r0_, BQ), :].astype(BF16)
            dos = [dob * m0b, dob * m1b]
            dq = None
            dk = None
            dv = None
            for hh in range(2):
                p = ps[hh]
                dp = _dot(dos[hh], vw, "nt")
                ds = p * (dp - jnp.sum(dp * p, axis=-1, keepdims=True))
                db_acc[hh, :, 0:BAND_W] += ds
                dsb = (ds * SCALE_C).astype(BF16)
                t = _dot(dsb, kw * (m0b if hh == 0 else m1b))
                dq = t if dq is None else dq + t
                t = _dot(dsb, qs[hh], "tn")
                dk = t if dk is None else dk + t
                t = _dot(p.astype(BF16), dos[hh], "tn")
                dv = t if dv is None else dv + t
            dqkv_ref[0, pl.ds(r0_, BQ), :] = dq.astype(BF16)
            dkpad[pl.ds(r0_, BAND_W), :] += dk
            dvpad[pl.ds(r0_, BAND_W), :] += dv
            return 0

        lax.fori_loop(0, S // BQ, qblock, 0)
        dqkv_ref[1] = dkpad[BAND_PAD:, :].astype(BF16)
        dqkv_ref[2] = dvpad[BAND_PAD:, :].astype(BF16)
        sub = _iota((8, TOEP_W), 0)
        for hh in range(2):
            folded = db_acc[hh, 0:8, :]
            for a in range(1, BQ // 8):
                folded = folded + pltpu.roll(db_acc[hh, 8 * a:8 * a + 8, :], TOEP_W - 8 * a, 1)
            for bit in range(3):
                moved = pltpu.roll(folded, TOEP_W - (1 << bit), 1)
                folded = jnp.where((sub & (1 << bit)) != 0, moved, folded)
            dr0_ref[hh:hh + 1, :] = jnp.sum(folded, axis=0, keepdims=True)

    return _carrier_call(
        body, name=name, grid=(C_H // 2,),
        in_specs=[pl.BlockSpec((3, S, LANES), lambda p: (0, 0, p)), pl.BlockSpec((None, 2, TOEP_W), lambda p: (p, 0, 0)),
                  pl.BlockSpec((S, LANES), lambda p: (0, p))],
        out_specs=[pl.BlockSpec((3, S, LANES), lambda p: (0, 0, p)), pl.BlockSpec((None, 2, TOEP_W), lambda p: (p, 0, 0))],
        out_shape=[jax.ShapeDtypeStruct((3, S, C_H * C_DIM), BF16), jax.ShapeDtypeStruct((C_H // 2, 2, TOEP_W), F32)],
        scratch_shapes=[pltpu.VMEM((S + BAND_PAD, LANES), BF16), pltpu.VMEM((S + BAND_PAD, LANES), BF16),
                        pltpu.VMEM((S + BAND_PAD, LANES), F32), pltpu.VMEM((S + BAND_PAD, LANES), F32),
                        pltpu.VMEM((2, BQ, TOEP_W), F32)],
        args=(qkv, r0, do), sem=("parallel",), carry=carry)


def _bias_table_grad(name, dr0):
    w_out = 5 * LANES

    def body(d_ref, o_ref):
        j = _iota((TOEP_W, w_out), 0)
        rel = jnp.clip(LEFT_CHUNKS * CHUNK - j, -REL_CLIP, REL_CLIP) + REL_CLIP
        rel = jnp.where(j >= BAND_W, 2 * REL_CLIP, rel)
        onehot = (rel == _iota((TOEP_W, w_out), 1)).astype(BF16)
        d = d_ref[...]
        hi = d.astype(BF16)
        mid = (d - hi.astype(F32))
        mid_b = mid.astype(BF16)
        lo = (mid - mid_b.astype(F32)).astype(BF16)
        o_ref[...] = _dot(hi, onehot) + _dot(mid_b, onehot) + _dot(lo, onehot)

    return pl.pallas_call(
        body, name=name, out_shape=jax.ShapeDtypeStruct((C_H, w_out), F32),
        in_specs=[pl.BlockSpec((C_H, TOEP_W), lambda: (0, 0))], out_specs=pl.BlockSpec((C_H, w_out), lambda: (0, 0)),
        grid=(),
    )(dr0)


def _carry_gather(cy, slots, names, ici, d2d):
    idx = [cy.operand(slots[n], True) for n in names]
    n = len(names)
    base_i = cy.sems(3 * n) if ici else 0
    base_d = cy.sems(3 * n) if d2d else 0

    def piece(refs, t, slot, cc):
        return refs[idx[t]].at[slot, _half_rows(cc, slots[names[t]].shape[1] // 2), :]

    def over_ici(refs, send, recv, arriving):
        x, y, c, chips = _position()
        out = []
        for t in range(n):
            for j in range(3):
                r = piece(refs, t, 2 * chips[j][0] + chips[j][1] if arriving else 2 * x + y, c)
                out.append(_remote(r, r, send, recv, base_i + 3 * t + j, (*chips[j], c)))
        return out

    def over_d2d(refs, send, recv, arriving):
        x, y, c, chips = _position()
        out = []
        for t in range(n):
            for j in range(3):
                r = piece(refs, t, 2 * chips[j][0] + chips[j][1], 1 - c if arriving else c)
                out.append(_remote(r, r, send, recv, base_d + 3 * t + j, (x, y, 1 - c)))
        return out

    def start_ici(refs, fresh, send, recv):
        for cp in over_ici(refs, send, recv, False):
            cp.start()

    def wait_ici(refs, fresh, send, recv):
        for cp in over_ici(refs, send, recv, True):
            cp.wait_recv()
        for cp in over_ici(refs, send, recv, False):
            cp.wait_send()

    def start_d2d(refs, fresh, send, recv):
        for cp in over_d2d(refs, send, recv, False):
            cp.start()

    def wait_d2d(refs, fresh, send, recv):
        for cp in over_d2d(refs, send, recv, True):
            cp.wait_recv()
        for cp in over_d2d(refs, send, recv, False):
            cp.wait_send()

    if ici and d2d:
        cy.starts.append(start_ici)
        cy.finishes += [wait_ici, start_d2d, wait_d2d]
    elif ici:
        cy.starts.append(start_ici)
        cy.finishes.append(wait_ici)
    else:
        cy.starts.append(start_d2d)
        cy.finishes.append(wait_d2d)

    def done(aliased, fresh):
        for t, name in enumerate(names):
            slots[name] = aliased[idx[t]]

    cy.on_done.append(done)


def _carry_chip_exchange(cy, sums, got, names):
    idx = [cy.operand(sums[n], False) for n in names]
    out = [cy.result((3,) + sums[n].shape[1:], BF16) for n in names]
    base = cy.sems(3 * len(names))

    def copies(refs, fresh, send, recv):
        x, y, c, chips = _position()
        return [_remote(refs[idx[t]].at[2 * chips[j][0] + chips[j][1]], fresh[out[t]].at[j], send, recv, base + 3 * t + j,
                        (*chips[j], c)) for t in range(len(names)) for j in range(3)]

    def start(refs, fresh, send, recv):
        for cp in copies(refs, fresh, send, recv):
            cp.start()

    def wait(refs, fresh, send, recv):
        for cp in copies(refs, fresh, send, recv):
            cp.wait()

    cy.starts.append(start)
    cy.finishes.append(wait)

    def done(aliased, fresh):
        for t, name in enumerate(names):
            got[name] = fresh[out[t]]

    cy.on_done.append(done)


def _run_carry(name, cy):
    _, res = _carrier_call(None, name=name, grid=(), in_specs=[], out_specs=[], out_shape=[], args=(), sem=(), carry=cy)
    cy.done(res)


FIRST_WEIGHTS = ("ev_w_in", "ev_w_uq", "ev_w_ukv")
WEIGHTS_A = ("ev_w_out", "w_gate0", "w_up0")
WEIGHTS_B = ("w_down0", "od_w_qkv", "od_w_out")
WEIGHTS_C = ("w_gate1", "w_up1")
WEIGHTS_D = ("w_down1",)
GRAD_GROUPS = {"ffn1": ("w_gate1", "w_up1", "w_down1"), "od": ("od_w_qkv", "od_w_out"),
               "ffn0": ("w_gate0", "w_up0", "w_down0"), "ev": ("ev_w_in", "ev_w_uq", "ev_w_ukv", "ev_w_out")}


def _carry_pair_exchange(cy, parts, theirs, names):
    idx = [cy.operand(parts[n], False) for n in names]
    out = [cy.result((N_CHIPS, parts[n].shape[1] // 2, parts[n].shape[2]), BF16) for n in names]
    base = cy.sems(len(names))

    def copies(refs, fresh, send, recv):
        x, y, c, _ = _position()
        return [_remote(refs[idx[t]].at[:, _half_rows(1 - c, parts[n].shape[1] // 2), :], fresh[out[t]], send, recv,
                        base + t, (x, y, 1 - c)) for t, n in enumerate(names)]

    cy.starts.append(lambda refs, fresh, send, recv: [cp.start() for cp in copies(refs, fresh, send, recv)])
    cy.finishes.append(lambda refs, fresh, send, recv: [cp.wait() for cp in copies(refs, fresh, send, recv)])

    def done(aliased, fresh):
        for t, name in enumerate(names):
            theirs[name] = fresh[out[t]]

    cy.on_done.append(done)


def _carry_sibling_exchange(cy, fulls, pieces):
    idx = [cy.operand(fulls[p], True) for p, _ in pieces]
    base = cy.sems(len(pieces))

    def copies(refs, send, recv, arriving):
        x, y, c, _ = _position()
        out = []
        for t, (p, layer) in enumerate(pieces):
            r = refs[idx[t]].at[layer, _half_rows(1 - c if arriving else c, fulls[p].shape[1] // 2), :]
            out.append(_remote(r, r, send, recv, base + t, (x, y, 1 - c)))
        return out

    def start(refs, fresh, send, recv):
        for cp in copies(refs, send, recv, False):
            cp.start()

    def wait(refs, fresh, send, recv):
        for cp in copies(refs, send, recv, True):
            cp.wait_recv()
        for cp in copies(refs, send, recv, False):
            cp.wait_send()

    cy.starts.append(start)
    cy.finishes.append(wait)

    def done(aliased, fresh):
        for t, (p, _) in enumerate(pieces):
            fulls[p] = aliased[idx[t]]

    cy.on_done.append(done)


RIDES = {
    "mla_attn": (("gather_ici", WEIGHTS_A),),
    "sb_attn": (("gather_d2d", WEIGHTS_A), ("gather_ici", WEIGHTS_B)),
    "ev_out": (("gather_d2d", WEIGHTS_B),),
    "ffn0": (("gather_ici", WEIGHTS_C),),
    "qkv": (("gather_d2d", WEIGHTS_C),),
    "band_attn": (("gather_ici", WEIGHTS_D),),
    "od_out": (("gather_d2d", WEIGHTS_D),),
    "od_out_bwd_w": (("pair", "ffn1"),),
    "band_attn_bwd": (("chips", "ffn1"),),
    "rms_mix1_bwd": (("pair", "od"),),
    "ev_out_bwd_w": (("pair", "ffn0"),),
    "mla_attn_bwd": (("chips", "od"), ("sibling", "ffn1")),
    "sb_attn_bwd": (("chips", "ffn0"), ("sibling", "od")),
    "proj_in_bwd_w": (("sibling", "ffn0"),),
    "proj_in_bwd_x": (("chips", "ev"),),
}


class _Exchanges:
    def __init__(self, slots, pos, shapes):
        self.slots, self.pos, self.shapes = dict(slots), pos, shapes
        self.parts, self.theirs, self.sums, self.got, self.fulls = {}, {}, {}, {}, {}

    def begin(self):
        cy = _Carry()
        _carry_gather(cy, self.slots, FIRST_WEIGHTS, True, True)
        _run_carry("gather_first", cy)

    def weights(self, *names):
        return [self.slots[n] for n in names]

    def _pair_sums(self, group):
        for n in GRAD_GROUPS[group]:
            if n not in self.sums:
                self.sums[n] = _pair_sum("pair_sum_" + n, self.parts[n], self.theirs[n], self.pos)

    def _chip_sums(self, group):
        for n in GRAD_GROUPS[group]:
            param, layer = PART_OF[n]
            self.fulls[param] = _chip_sum("chip_sum_" + n, self.sums[n], self.got[n], self.pos, layer,
                                          self.shapes[param], self.fulls.get(param))

    def carry(self, stage):
        cy = _Carry()
        for step, what in RIDES[stage]:
            if step == "gather_ici":
                _carry_gather(cy, self.slots, what, True, False)
            elif step == "gather_d2d":
                _carry_gather(cy, self.slots, what, False, True)
            elif step == "pair":
                _carry_pair_exchange(cy, self.parts, self.theirs, GRAD_GROUPS[what])
            elif step == "chips":
                self._pair_sums(what)
                _carry_chip_exchange(cy, self.sums, self.got, GRAD_GROUPS[what])
            elif step == "sibling":
                self._chip_sums(what)
                _carry_sibling_exchange(cy, self.fulls, [PART_OF[n] for n in GRAD_GROUPS[what]])
        return cy

    def grads(self, group, parts):
        self.parts.update(parts)
        if group == "ev":
            cy = _Carry()
            _carry_pair_exchange(cy, self.parts, self.theirs, GRAD_GROUPS[group])
            _run_carry("grads_pair_ev", cy)

    def finish(self):
        cy = _Carry()
        self._chip_sums("ev")
        _carry_sibling_exchange(cy, self.fulls, [PART_OF[n] for n in GRAD_GROUPS["ev"]])
        _run_carry("grads_sibling_ev", cy)
        return {n: self.fulls[n] for n in BIG}


class _NoExchanges:
    def __init__(self, slots):
        self.slots, self.parts = dict(slots), {}

    def begin(self):
        pass

    def weights(self, *names):
        return [self.slots[n] for n in names]

    def carry(self, stage):
        return None

    def grads(self, group, parts):
        self.parts.update(parts)


def _first_weights(w_in_s, w_uq_s, w_ukv_s):
    gw = {"ev_w_in": w_in_s, "ev_w_uq": w_uq_s, "ev_w_ukv": w_ukv_s}
    w_in = jnp.moveaxis(gw["ev_w_in"], 0, 1).reshape(D, EVEN_IN)
    z = lambda n: jnp.zeros((D, n), BF16)
    w_in_p = jnp.concatenate(
        [w_in[:, 0:384], z(128), w_in[:, 384:640], w_in[:, 672:2208], z(KR_LANE), w_in[:, 640:672],
         z(LANES - KR_LANE - MLA_ROPE)], axis=1)
    w_uq = jnp.moveaxis(gw["ev_w_uq"], 0, 1).reshape(Q_LORA, MLA_H, MLA_NOPE + MLA_ROPE)
    w_uq_p = jnp.concatenate([w_uq, jnp.zeros((Q_LORA, MLA_H, LANES - MLA_NOPE - MLA_ROPE), BF16)], axis=2)
    w_ukv = jnp.moveaxis(gw["ev_w_ukv"], 0, 1).reshape(KV_LORA, MLA_H, MLA_NOPE + MLA_V)
    w_uk_p = jnp.concatenate([w_ukv[:, :, :MLA_NOPE], jnp.zeros((KV_LORA, MLA_H, LANES - MLA_NOPE), BF16)], axis=2)
    return dict(
        w_in=w_in_p, w_uq=w_uq_p.reshape(Q_LORA, MLA_H * LANES), w_uk=w_uk_p.reshape(KV_LORA, MLA_H * LANES),
        w_uv=w_ukv[:, :, MLA_NOPE:].reshape(KV_LORA, MLA_H * MLA_V))


def _proj_mm(name, u, w_in):
    return _mm(name, u, w_in, kind="nn", grid=(S // TM, 1, 1),
               a_spec=pl.BlockSpec((TM, D), lambda i, j, k: (i, 0)), b_spec=pl.BlockSpec((D, P_IN), lambda i, j, k: (0, 0)),
               o_spec=pl.BlockSpec((TM, P_IN), lambda i, j, k: (i, 0)), out_shape=(S, P_IN), out_dtype=F32, acc_shape=None)


def _out_proj(name, o, w, resid, carry=None):
    return _mm(name, o, w, kind="nn", grid=(S // TM, 1, 1),
               a_spec=pl.BlockSpec((TM, D), lambda i, j, k: (i, 0)), b_spec=pl.BlockSpec((D, D), lambda i, j, k: (0, 0)),
               o_spec=pl.BlockSpec((TM, D), lambda i, j, k: (i, 0)), out_shape=(S, D), out_dtype=F32, acc_shape=None,
               resid=resid, r_spec=pl.BlockSpec((TM, D), lambda i, j, k: (i, 0)), carry=carry)


def _out_proj_bwd(name, dh, o, w, ex):
    d_o = _mm(name + "_x", dh, w, kind="nt", grid=(S // TM, 1, 1),
              a_spec=pl.BlockSpec((TM, D), lambda i, j, k: (i, 0)), b_spec=pl.BlockSpec((D, D), lambda i, j, k: (0, 0)),
              o_spec=pl.BlockSpec((TM, D), lambda i, j, k: (i, 0)), out_shape=(S, D), out_dtype=F32, acc_shape=None)
    d_w = _mm(name + "_w", o, dh, kind="tn", grid=(2, S // TM),
              a_spec=pl.BlockSpec((TM, TM), lambda j, k: (k, j)), b_spec=pl.BlockSpec((TM, D), lambda j, k: (k, 0)),
              o_spec=pl.BlockSpec((TM, D), lambda j, k: (j, 0)), out_shape=(D, D), out_dtype=BF16, acc_shape=(TM, D),
              carry=ex.carry(name + "_w"))
    return d_o, d_w


def _local_step(x, tgt, sm, ex):
    def riding(stage, fn, *args):
        cy = ex.carry(stage)
        res, copies = fn(stage, *args, carry=cy)
        if cy is not None:
            cy.done(copies)
        return res

    cos_t, sin_t = _rope_tables()
    g_mix, g_ffn = sm["g_mix"], sm["g_ffn"]
    r0 = sm["od_rel_bias"][0][:, _band_row_index()].reshape(C_H // 2, 2, TOEP_W)
    nt = 3

    ex.begin()
    w = _first_weights(*ex.weights(*FIRST_WEIGHTS))
    u0 = _rms_fwd("rms_mix0", x, g_mix[0:1])
    proj = _proj_mm("proj_in", u0, w["w_in"])
    qa, ka, va = _mla_prep_fwd("mla_prep", proj, sm["ev_g_cq"], sm["ev_g_ckv"], w["w_uq"], w["w_uk"], w["w_uv"], cos_t, sin_t)
    o_a, lse = riding("mla_attn", _mla_fwd, qa, ka, va)
    o_b, = riding("sb_attn", _sb_fwd, proj)
    o_ev = jnp.concatenate([o_a.astype(BF16), o_b], axis=1)
    w["ev_w_out"] = ex.weights("ev_w_out")[0].reshape(D, D)
    h1 = _out_proj("ev_out", o_ev, w["ev_w_out"], x, ex.carry("ev_out"))
    w["w_gate0"], w["w_up0"], w["w_down0"] = ex.weights("w_gate0", "w_up0", "w_down0")
    h2, gate0, up0 = riding("ffn0", _ffn_fwd, h1, g_ffn[0:1], w["w_gate0"], w["w_up0"], w["w_down0"])
    w["w_qkv"] = jnp.moveaxis(ex.weights("od_w_qkv")[0], 0, 1).reshape(D, nt * D)
    u2 = _rms_fwd("rms_mix1", h2, g_mix[1:2])
    qkv = _mm("qkv", u2, w["w_qkv"], kind="nn", grid=(S // TM, nt, 1),
              a_spec=pl.BlockSpec((TM, D), lambda i, t, k: (i, 0)), b_spec=pl.BlockSpec((D, D), lambda i, t, k: (0, t)),
              o_spec=pl.BlockSpec((None, TM, D), lambda i, t, k: (t, i, 0)),
              out_shape=(nt, S, D), out_dtype=BF16, acc_shape=None, carry=ex.carry("qkv"))
    o_od, = riding("band_attn", _band_fwd, qkv, r0)
    w["od_w_out"] = ex.weights("od_w_out")[0].reshape(D, D)
    h3 = _out_proj("od_out", o_od, w["od_w_out"], h2, ex.carry("od_out"))
    w["w_gate1"], w["w_up1"], w["w_down1"] = ex.weights("w_gate1", "w_up1", "w_down1")
    (h4, gate1, up1), _ = _ffn_fwd("ffn1", h3, g_ffn[1:2], w["w_gate1"], w["w_up1"], w["w_down1"])

    loss, dh4, dg_final = _loss_bwd("loss", h4, sm["g_final"].reshape(1, D), tgt)

    dh3, dg_ffn1, u3, dgate, dup, act = _ffn_bwd("ffn1_bwd", dh4, h3, g_ffn[1:2], gate1, up1,
                                                 w["w_gate1"], w["w_up1"], w["w_down1"])
    d_wg1, d_wu1, d_wd1 = _ffn_wgrads("ffn1_dw", u3, dgate, dup, act, dh4)
    ex.grads("ffn1", {"w_gate1": d_wg1, "w_up1": d_wu1, "w_down1": d_wd1})

    d_ood, d_w_od_out = _out_proj_bwd("od_out_bwd", dh3, o_od, w["od_w_out"], ex)
    dqkv, dr0 = riding("band_attn_bwd", _band_bwd, qkv, r0, d_ood)
    du2 = _mm("qkv_bwd_x", dqkv, w["w_qkv"], kind="nt", grid=(S // TM, nt),
              a_spec=pl.BlockSpec((None, TM, D), lambda i, t: (t, i, 0)), b_spec=pl.BlockSpec((D, D), lambda i, t: (0, t)),
              o_spec=pl.BlockSpec((TM, D), lambda i, t: (i, 0)), out_shape=(S, D), out_dtype=F32, acc_shape=(TM, D))
    d_w_qkv = _mm("qkv_bwd_w", u2, dqkv, kind="tn", grid=(nt, S // TM),
                  a_spec=pl.BlockSpec((TM, D), lambda t, k: (k, 0)), b_spec=pl.BlockSpec((None, TM, D), lambda t, k: (t, k, 0)),
                  o_spec=pl.BlockSpec((D, D), lambda t, k: (0, t)), out_shape=(D, nt * D), out_dtype=BF16, acc_shape=(D, D))
    shard_cols = lambda a: jnp.moveaxis(a.reshape(a.shape[0], N_CHIPS, a.shape[1] // N_CHIPS), 1, 0)
    ex.grads("od", {"od_w_qkv": shard_cols(d_w_qkv), "od_w_out": d_w_od_out.reshape(N_CHIPS, D // N_CHIPS, D)})
    dh2, dg_mix1 = _rms_bwd("rms_mix1_bwd", du2, h2, g_mix[1:2], dh3, carry=ex.carry("rms_mix1_bwd"))
    d_rel = _bias_table_grad("rel_bias_grad", dr0.reshape(C_H, TOEP_W))[:, :2 * REL_CLIP + 1]

    dh1, dg_ffn0, u1, dgate, dup, act = _ffn_bwd("ffn0_bwd", dh2, h1, g_ffn[0:1], gate0, up0,
                                                 w["w_gate0"], w["w_up0"], w["w_down0"])
    d_wg0, d_wu0, d_wd0 = _ffn_wgrads("ffn0_dw", u1, dgate, dup, act, dh2)
    ex.grads("ffn0", {"w_gate0": d_wg0, "w_up0": d_wu0, "w_down0": d_wd0})

    d_oev, d_w_ev_out = _out_proj_bwd("ev_out_bwd", dh1, o_ev, w["ev_w_out"], ex)
    dqa, dka, dva = riding("mla_attn_bwd", _mla_bwd, qa, ka, va, o_a, lse, d_oev, 0)
    dqb, dkb, dvb = riding("sb_attn_bwd", _sb_bwd, proj, d_oev, MLA_H * MLA_V // LANES)
    dcq, dckv, dkr, d_w_uq, d_w_uk, d_w_uv, dg_cq, dg_ckv = _mla_prep_bwd(
        "mla_prep_bwd", dqa, dka, dva, proj, sm["ev_g_cq"], sm["ev_g_ckv"], w["w_uq"], w["w_uk"], w["w_uv"], cos_t, sin_t)
    dproj = jnp.concatenate([dcq, jnp.zeros((S, LANES), BF16), dckv, dqb, dkb, dvb, dkr], axis=1)
    d_w_in_p = _mm("proj_in_bwd_w", u0, dproj, kind="tn", grid=(1, S // TM),
                   a_spec=pl.BlockSpec((TM, D), lambda j, k: (k, 0)), b_spec=pl.BlockSpec((TM, P_IN), lambda j, k: (k, 0)),
                   o_spec=pl.BlockSpec((D, P_IN), lambda j, k: (0, 0)), out_shape=(D, P_IN), out_dtype=BF16,
                   acc_shape=(D, P_IN), carry=ex.carry("proj_in_bwd_w"))
    d_w_in = jnp.concatenate([d_w_in_p[:, 0:384], d_w_in_p[:, 512:768],
                              d_w_in_p[:, P_KR + KR_LANE:P_KR + KR_LANE + MLA_ROPE], d_w_in_p[:, 768:2304]], axis=1)
    d_w_uq_std = d_w_uq.reshape(Q_LORA, MLA_H, LANES)[:, :, :MLA_NOPE + MLA_ROPE].reshape(Q_LORA, -1)
    d_w_ukv = jnp.concatenate([d_w_uk.reshape(KV_LORA, MLA_H, LANES)[:, :, :MLA_NOPE],
                               d_w_uv.reshape(KV_LORA, MLA_H, MLA_V)], axis=2).reshape(KV_LORA, -1)
    ex.grads("ev", {"ev_w_in": shard_cols(d_w_in), "ev_w_uq": shard_cols(d_w_uq_std.astype(BF16)),
                    "ev_w_ukv": shard_cols(d_w_ukv.astype(BF16)),
                    "ev_w_out": d_w_ev_out.reshape(N_CHIPS, D // N_CHIPS, D)})
    du0 = _mm("proj_in_bwd_x", dproj, w["w_in"], kind="nt", grid=(S // TM, 1, 1),
              a_spec=pl.BlockSpec((TM, P_IN), lambda i, j, k: (i, 0)), b_spec=pl.BlockSpec((D, P_IN), lambda i, j, k: (0, 0)),
              o_spec=pl.BlockSpec((TM, D), lambda i, j, k: (i, 0)), out_shape=(S, D), out_dtype=F32, acc_shape=None,
              carry=ex.carry("proj_in_bwd_x"))
    grad_x, dg_mix0 = _rms_bwd("rms_mix0_bwd", du0, x, g_mix[0:1], dh1)
    small = {
        "ev_g_cq": dg_cq, "ev_g_ckv": dg_ckv, "od_rel_bias": d_rel.reshape(1, C_H, 2 * REL_CLIP + 1),
        "g_mix": jnp.concatenate([dg_mix0, dg_mix1], axis=0), "g_ffn": jnp.concatenate([dg_ffn0, dg_ffn1], axis=0),
        "g_final": dg_final.reshape(D),
    }
    return loss, grad_x, small


BIG = ("ev_w_in", "ev_w_uq", "ev_w_ukv", "ev_w_out", "od_w_qkv", "od_w_out", "w_gate", "w_up", "w_down")
SMALL = ("ev_g_cq", "ev_g_ckv", "od_rel_bias", "g_mix", "g_ffn", "g_final")
WEIGHTS = ("ev_w_in", "ev_g_cq", "ev_w_uq", "ev_g_ckv", "ev_w_ukv", "ev_w_out", "od_w_qkv", "od_rel_bias", "od_w_out",
           "g_mix", "g_ffn", "w_gate", "w_up", "w_down", "g_final")
GRAD_PARTS = (("ev_w_in", "ev_w_in", 0), ("ev_w_uq", "ev_w_uq", 0), ("ev_w_ukv", "ev_w_ukv", 0),
              ("ev_w_out", "ev_w_out", 0), ("od_w_qkv", "od_w_qkv", 0), ("od_w_out", "od_w_out", 0),
              ("w_gate0", "w_gate", 0), ("w_gate1", "w_gate", 1), ("w_up0", "w_up", 0), ("w_up1", "w_up", 1),
              ("w_down0", "w_down", 0), ("w_down1", "w_down", 1))
PART_OF = {part: (param, layer) for part, param, layer in GRAD_PARTS}
SMALL_ROWS = 112
SMALL_SIZE = 384 + 256 + 16 * 513 + 2 * 1024 + 2 * 1024 + 1024
TRANSPOSED = ("w_gate", "w_up")


def _row_tile(rows, cap=512):
    for t in range(min(rows, cap), 0, -1):
        if rows % t == 0 and t % 16 == 0:
            return t
    return rows


def _cast_into_slot(name, w, layer, pos):
    _, rows, cols = w.shape
    tr = _row_tile(rows)

    def body(pos_ref, w_ref, o_ref):
        o_ref[...] = w_ref[...].astype(BF16)

    return pl.pallas_call(
        body, name=name,
        grid_spec=pltpu.PrefetchScalarGridSpec(
            num_scalar_prefetch=1, grid=(rows // tr,),
            in_specs=[pl.BlockSpec((None, tr, cols), lambda i, p: (layer, i, 0))],
            out_specs=pl.BlockSpec((None, tr, cols), lambda i, p: (p[0], i, 0))),
        out_shape=jax.ShapeDtypeStruct((N_CHIPS, rows, cols), BF16), compiler_params=_params("arbitrary"))(pos, w)


def _pair_sum(name, part, theirs, pos):
    _, half, cols = theirs.shape
    tr = _row_tile(half)
    nb = half // tr

    def body(pos_ref, a_ref, b_ref, o_ref):
        o_ref[...] = (a_ref[...].astype(F32) + b_ref[...].astype(F32)).astype(BF16)

    return pl.pallas_call(
        body, name=name,
        grid_spec=pltpu.PrefetchScalarGridSpec(
            num_scalar_prefetch=1, grid=(N_CHIPS, nb),
            in_specs=[pl.BlockSpec((None, tr, cols), lambda s, i, p: (s, p[1] * nb + i, 0)),
                      pl.BlockSpec((None, tr, cols), lambda s, i, p: (s, i, 0))],
            out_specs=pl.BlockSpec((None, tr, cols), lambda s, i, p: (s, i, 0))),
        out_shape=jax.ShapeDtypeStruct(theirs.shape, BF16),
        compiler_params=_params("arbitrary", "arbitrary"))(pos, part, theirs)


def _chip_sum(name, sums, got, pos, layer, full_shape, full=None):
    _, half, cols = sums.shape
    tr = _row_tile(half)
    nb = half // tr

    def body(pos_ref, s_ref, g_ref, *rest):
        out_ref = rest[-1]
        out_ref[...] = ((s_ref[...].astype(F32) + g_ref[0].astype(F32)) + g_ref[1].astype(F32)) + g_ref[2].astype(F32)

    in_specs = [pl.BlockSpec((None, tr, cols), lambda i, p: (p[0], i, 0)),
                pl.BlockSpec((3, tr, cols), lambda i, p: (0, i, 0))]
    args = [pos, sums, got]
    if full is not None:
        in_specs.append(ANY)
        args.append(full)
    return pl.pallas_call(
        body, name=name,
        grid_spec=pltpu.PrefetchScalarGridSpec(
            num_scalar_prefetch=1, grid=(nb,), in_specs=in_specs,
            out_specs=pl.BlockSpec((None, tr, cols), lambda i, p: (layer, p[1] * nb + i, 0))),
        out_shape=jax.ShapeDtypeStruct(full_shape, F32),
        input_output_aliases={3: 0} if full is not None else {},
        compiler_params=_params("arbitrary"))(*args)


def _all_reduce_small(name, packed):
    n_dev = 8

    def body(p_ref, o_ref, slots, send_sem, recv_sem):
        x, y, c, _ = _position()
        me = 4 * x + 2 * y + c

        def peer(k):
            return (1 - x if k & 4 else x, 1 - y if k & 2 else y, 1 - c if k & 1 else c)

        def logical(k):
            px, py, pc = peer(k)
            return 4 * px + 2 * py + pc

        slots[me] = p_ref[...]
        sends = [pltpu.make_async_remote_copy(
            src_ref=p_ref, dst_ref=slots.at[me], send_sem=send_sem.at[k], recv_sem=recv_sem.at[k],
            device_id=peer(k), device_id_type=MESH) for k in range(1, n_dev)]
        for cp in sends:
            cp.start()
        for k in range(1, n_dev):
            pltpu.make_async_remote_copy(
                src_ref=p_ref, dst_ref=slots.at[logical(k)], send_sem=send_sem.at[k], recv_sem=recv_sem.at[k],
                device_id=peer(k), device_id_type=MESH).wait_recv()
        for cp in sends:
            cp.wait_send()
        total = slots[0]
        for d in range(1, n_dev):
            total = total + slots[d]
        o_ref[...] = total

    vm = pl.BlockSpec(memory_space=pltpu.VMEM)
    return pl.pallas_call(
        body, name=name, in_specs=[vm], out_specs=vm, out_shape=jax.ShapeDtypeStruct(packed.shape, F32),
        scratch_shapes=[pltpu.VMEM((n_dev,) + packed.shape, F32), pltpu.SemaphoreType.DMA((n_dev,)),
                        pltpu.SemaphoreType.DMA((n_dev,))],
    )(packed)


def _adamw(name, w, g, m, v):
    rows, cols = w.shape
    tr = _row_tile(rows)

    def body(w_ref, g_ref, m_ref, v_ref, d_ref, mo_ref, vo_ref):
        gv = g_ref[...]
        m_new = ADAM_B1 * m_ref[...] + (1.0 - ADAM_B1) * gv
        v_new = ADAM_B2 * v_ref[...] + (1.0 - ADAM_B2) * (gv * gv)
        m_hat = m_new / (1.0 - ADAM_B1 ** ADAM_STEP)
        v_hat = v_new / (1.0 - ADAM_B2 ** ADAM_STEP)
        d_ref[...] = -ADAM_LR * (m_hat / (jnp.sqrt(v_hat) + ADAM_EPS) + ADAM_WD * w_ref[...])
        mo_ref[...] = m_new
        vo_ref[...] = v_new

    spec = pl.BlockSpec((tr, cols), lambda i: (i, 0))
    shape = jax.ShapeDtypeStruct((rows, cols), F32)
    return pl.pallas_call(body, name=name, grid=(rows // tr,), in_specs=[spec] * 4, out_specs=[spec] * 3,
                          out_shape=[shape] * 3, compiler_params=_params("parallel"))(w, g, m, v)


def _pack_small(tree, extra=None):
    pieces = [tree[n].reshape(-1).astype(F32) for n in SMALL]
    if extra is not None:
        pieces.append(extra.reshape(1).astype(F32))
    flat = jnp.concatenate(pieces)
    return jnp.pad(flat, (0, SMALL_ROWS * LANES - flat.shape[0])).reshape(SMALL_ROWS, LANES)


def _unpack_small(packed, like):
    flat = packed.reshape(-1)
    out, off = {}, 0
    for n in SMALL:
        size = int(np.prod(like[n].shape))
        out[n] = flat[off:off + size].reshape(like[n].shape)
        off += size
    return out


def kernel(x, ev_w_in, ev_g_cq, ev_w_uq, ev_g_ckv, ev_w_ukv, ev_w_out, od_w_qkv, od_rel_bias, od_w_out, g_mix, g_ffn, w_gate, w_up, w_down, g_final, loss_target, m_ev_w_in, m_ev_g_cq, m_ev_w_uq, m_ev_g_ckv, m_ev_w_ukv, m_ev_w_out, m_od_w_qkv, m_od_rel_bias, m_od_w_out, m_g_mix, m_g_ffn, m_w_gate, m_w_up, m_w_down, m_g_final, v_ev_w_in, v_ev_g_cq, v_ev_w_uq, v_ev_g_ckv, v_ev_w_ukv, v_ev_w_out, v_od_w_qkv, v_od_rel_bias, v_od_w_out, v_g_mix, v_g_ffn, v_w_gate, v_w_up, v_w_down, v_g_final):
    w = dict(ev_w_in=ev_w_in, ev_g_cq=ev_g_cq, ev_w_uq=ev_w_uq, ev_g_ckv=ev_g_ckv, ev_w_ukv=ev_w_ukv, ev_w_out=ev_w_out,
             od_w_qkv=od_w_qkv, od_rel_bias=od_rel_bias, od_w_out=od_w_out, g_mix=g_mix, g_ffn=g_ffn, w_gate=w_gate,
             w_up=w_up, w_down=w_down, g_final=g_final)
    m = dict(ev_w_in=m_ev_w_in, ev_g_cq=m_ev_g_cq, ev_w_uq=m_ev_w_uq, ev_g_ckv=m_ev_g_ckv, ev_w_ukv=m_ev_w_ukv,
             ev_w_out=m_ev_w_out, od_w_qkv=m_od_w_qkv, od_rel_bias=m_od_rel_bias, od_w_out=m_od_w_out, g_mix=m_g_mix,
             g_ffn=m_g_ffn, w_gate=m_w_gate, w_up=m_w_up, w_down=m_w_down, g_final=m_g_final)
    v = dict(ev_w_in=v_ev_w_in, ev_g_cq=v_ev_g_cq, ev_w_uq=v_ev_w_uq, ev_g_ckv=v_ev_g_ckv, ev_w_ukv=v_ev_w_ukv,
             ev_w_out=v_ev_w_out, od_w_qkv=v_od_w_qkv, od_rel_bias=v_od_rel_bias, od_w_out=v_od_w_out, g_mix=v_g_mix,
             g_ffn=v_g_ffn, w_gate=v_w_gate, w_up=v_w_up, w_down=v_w_down, g_final=v_g_final)
    flat2d = lambda a: a.reshape(-1, a.shape[-1])
    for tree in (w, m, v):
        for n in TRANSPOSED:
            tree[n] = jnp.swapaxes(tree[n], 1, 2)

    pos = jnp.stack([2 * lax.axis_index("x") + lax.axis_index("y"), lax.axis_index("c")]).astype(jnp.int32)

    slots = {part: _cast_into_slot("cast_" + part, w[n], layer, pos) for part, n, layer in GRAD_PARTS}
    ex = _Exchanges(slots, pos, {n: w[n].shape for n in BIG})

    loss_local, grad_x, small = _local_step(x[0], loss_target[0], {n: w[n] for n in SMALL}, ex)

    grads = ex.finish()
    small_sum = _all_reduce_small("small_sum", _pack_small(small, loss_local[0, 0]))
    grads.update(_unpack_small(small_sum, w))

    delta, new_m, new_v = {}, {}, {}
    for n in BIG:
        d_, m_, v_ = _adamw("adamw_" + n, flat2d(w[n]), flat2d(grads[n]), flat2d(m[n]), flat2d(v[n]))
        delta[n], new_m[n], new_v[n] = d_.reshape(w[n].shape), m_.reshape(w[n].shape), v_.reshape(w[n].shape)
    d_, m_, v_ = _adamw("adamw_small", _pack_small(w), small_sum, _pack_small(m), _pack_small(v))
    delta.update(_unpack_small(d_, w))
    new_m.update(_unpack_small(m_, w))
    new_v.update(_unpack_small(v_, w))
    for tree in (grads, delta, new_m, new_v):
        for n in TRANSPOSED:
            tree[n] = jnp.swapaxes(tree[n], 1, 2)

    loss = small_sum.reshape(-1)[SMALL_SIZE]
    return (loss, grad_x[None], *[grads[n] for n in WEIGHTS], *[delta[n] for n in WEIGHTS],
            *[new_m[n] for n in WEIGHTS], *[new_v[n] for n in WEIGHTS])
```

```python
import functools

import jax
import jax.numpy as jnp
import numpy as np
from jax import lax
from jax.experimental import pallas as pl
from jax.experimental.pallas import tpu as pltpu

F32 = jnp.float32
BF16 = jnp.bfloat16

S = 2048
D = 1024
CHUNK = 64
MLA_H, MLA_NOPE, MLA_ROPE, MLA_V = 8, 64, 32, 64
Q_LORA, KV_LORA = 384, 256
ROPE_THETA = 10000.0
SB_H, SB_DIM = 8, 64
C_H, C_DIM = 16, 64
LEFT_CHUNKS = 8
REL_CLIP = 256
D_FF = 2816
EVEN_IN = 2208
RMS_EPS = 1e-6
ADAM_LR, ADAM_B1, ADAM_B2, ADAM_EPS, ADAM_WD, ADAM_STEP = 0.001, 0.9, 0.999, 1e-08, 0.01, 10

N_CHIPS = 4
FF_SHARD = D_FF // N_CHIPS
SCALE_A = (MLA_NOPE + MLA_ROPE) ** -0.5
SCALE_B = SB_DIM ** -0.5
SCALE_C = C_DIM ** -0.5
NEG = -1e30

LANES = 128
VMEM_LIMIT_BYTES = 56 * 1024 * 1024
TM = 512
QB = 512
BQ = 256

P_CQ, P_CKV, P_QB, P_KB, P_VB, P_KR = 0, 512, 768, 1280, 1792, 2304
P_IN = 2432
KR_LANE = 64
BAND_W = BQ + LEFT_CHUNKS * CHUNK
BAND_PAD = 512
TOEP_W = 1024


def _params(*sem):
    return pltpu.CompilerParams(dimension_semantics=sem, vmem_limit_bytes=VMEM_LIMIT_BYTES)


MESH = pl.DeviceIdType.MESH
ANY = pl.BlockSpec(memory_space=pl.ANY)


def _position():
    x, y, c = lax.axis_index("x"), lax.axis_index("y"), lax.axis_index("c")
    other_chips = [(1 - x, y), (x, 1 - y), (1 - x, 1 - y)]
    return x, y, c, other_chips


def _half_rows(c, half):
    return pl.ds(pl.multiple_of(c * half, 16), half)


def _remote(ref_src, ref_dst, send, recv, k, device):
    return pltpu.make_async_remote_copy(src_ref=ref_src, dst_ref=ref_dst, send_sem=send.at[k], recv_sem=recv.at[k],
                                        device_id=device, device_id_type=MESH)


class _Carry:
    def __init__(self):
        self.operands, self.aliased, self.fresh = [], [], []
        self.n_sems = 0
        self.starts, self.finishes, self.on_done = [], [], []

    def operand(self, arr, aliased):
        for i, a in enumerate(self.operands):
            if a is arr:
                return i
        self.operands.append(arr)
        self.aliased.append(aliased)
        return len(self.operands) - 1

    def result(self, shape, dtype):
        self.fresh.append(jax.ShapeDtypeStruct(shape, dtype))
        return len(self.fresh) - 1

    def sems(self, k):
        base = self.n_sems
        self.n_sems += k
        return base

    def done(self, results):
        aliased, fresh = results
        for f in self.on_done:
            f(aliased, fresh)


def _carrier_call(body, *, name, grid, in_specs, out_specs, out_shape, args, sem, scratch_shapes=(), carry=None):
    in_specs, out_specs, out_shape, scratch = list(in_specs), list(out_specs), list(out_shape), list(scratch_shapes)
    if carry is None:
        res = pl.pallas_call(body, name=name, grid=grid, in_specs=in_specs, out_specs=out_specs, out_shape=out_shape,
                             scratch_shapes=scratch, compiler_params=_params(*sem))(*args)
        return list(res), None
    ops = carry.operands
    alias_idx = [i for i, a in enumerate(carry.aliased) if a]
    c_shapes = [jax.ShapeDtypeStruct(ops[i].shape, ops[i].dtype) for i in alias_idx] + carry.fresh
    n_in, n_out, n_scr = len(args), len(out_shape), len(scratch)

    def wrapped(*refs):
        ins, c_ins = refs[:n_in], refs[n_in:n_in + len(ops)]
        o0 = n_in + len(ops)
        outs, c_outs = refs[o0:o0 + n_out], refs[o0 + n_out:o0 + n_out + len(c_shapes)]
        s0 = o0 + n_out + len(c_shapes)
        scr, send, recv = refs[s0:s0 + n_scr], refs[s0 + n_scr], refs[s0 + n_scr + 1]
        use = list(c_ins)
        for k, i in enumerate(alias_idx):
            use[i] = c_outs[k]
        fresh = c_outs[len(alias_idx):]

        def run(steps):
            for step in steps:
                step(use, fresh, send, recv)

        if not grid:
            run(carry.starts)
            if body is not None:
                body(*ins, *outs, *scr)
            run(carry.finishes)
            return
        ids = [pl.program_id(a) for a in range(len(grid))]
        first = functools.reduce(jnp.logical_and, [i == 0 for i in ids])
        last = functools.reduce(jnp.logical_and, [i == g - 1 for i, g in zip(ids, grid)])

        @pl.when(first)
        def _():
            run(carry.starts)

        body(*ins, *outs, *scr)

        @pl.when(last)
        def _():
            run(carry.finishes)

    res = pl.pallas_call(
        wrapped, name=name, grid=grid, in_specs=in_specs + [ANY] * len(ops), out_specs=out_specs + [ANY] * len(c_shapes),
        out_shape=out_shape + c_shapes,
        scratch_shapes=scratch + [pltpu.SemaphoreType.DMA((carry.n_sems,)), pltpu.SemaphoreType.DMA((carry.n_sems,))],
        input_output_aliases={n_in + i: n_out + k for k, i in enumerate(alias_idx)},
        compiler_params=_params(*(("arbitrary",) * len(grid))),
    )(*args, *ops)
    res = list(res)
    c_res = res[n_out:]
    return res[:n_out], ({i: c_res[k] for k, i in enumerate(alias_idx)}, c_res[len(alias_idx):])


_DIMS = {"nn": (((1,), (0,)), ((), ())), "nt": (((1,), (1,)), ((), ())), "tn": (((0,), (0,)), ((), ()))}


def _dot(a, b, kind="nn"):
    return lax.dot_general(a, b, _DIMS[kind], preferred_element_type=F32)


def _iota(shape, dim):
    return lax.broadcasted_iota(jnp.int32, shape, dim)


def _sigmoid(x):
    return 1.0 / (1.0 + jnp.exp(-x))


def _softplus(x):
    return jnp.maximum(x, 0.0) + jnp.log(1.0 + jnp.exp(-jnp.abs(x)))


def _split_dot(x, tri):
    hi = x.astype(BF16)
    lo = (x - hi.astype(F32)).astype(BF16)
    return _dot(hi, tri) + _dot(lo, tri)


def _mm(name, a, b, *, kind, grid, a_spec, b_spec, o_spec, out_shape, out_dtype, acc_shape, resid=None, r_spec=None,
        carry=None):
    nk = grid[-1]
    has_r = resid is not None

    def body(*refs):
        a_ref, b_ref = refs[0], refs[1]
        r_ref = refs[2] if has_r else None
        o_ref = refs[2 + has_r]
        part = _dot(a_ref[...].astype(BF16), b_ref[...].astype(BF16), kind)

        def finish(total):
            if has_r:
                total = total + r_ref[...].astype(F32)
            o_ref[...] = total.astype(out_dtype)

        if nk == 1:
            finish(part)
        else:
            acc_ref = refs[3 + has_r]
            k = pl.program_id(len(grid) - 1)

            @pl.when(k == 0)
            def _():
                acc_ref[...] = part

            @pl.when(k > 0)
            def _():
                acc_ref[...] += part

            @pl.when(k == nk - 1)
            def _():
                finish(acc_ref[...])

    in_specs = [a_spec, b_spec] + ([r_spec] if has_r else [])
    args = (a, b) + ((resid,) if has_r else ())
    sem = ("parallel",) * (len(grid) - 1) + ("arbitrary",)
    res, copies = _carrier_call(
        body, name=name, grid=grid, in_specs=in_specs, out_specs=[o_spec],
        out_shape=[jax.ShapeDtypeStruct(out_shape, out_dtype)],
        scratch_shapes=[pltpu.VMEM(acc_shape, F32)] if nk > 1 else [], args=args, sem=sem, carry=carry)
    if carry is not None:
        carry.done(copies)
    return res[0]


def _rms_fwd(name, x, g, col_block=0):
    c = g.shape[1]

    def body(x_ref, g_ref, u_ref):
        xv = x_ref[...]
        r = lax.rsqrt(jnp.mean(xv * xv, axis=-1, keepdims=True) + RMS_EPS)
        u_ref[...] = (xv * r * g_ref[...]).astype(BF16)

    return pl.pallas_call(
        body, name=name, grid=(S // TM,),
        in_specs=[pl.BlockSpec((TM, c), lambda i: (i, col_block)), pl.BlockSpec((1, c), lambda i: (0, 0))],
        out_specs=pl.BlockSpec((TM, c), lambda i: (i, 0)),
        out_shape=jax.ShapeDtypeStruct((S, c), BF16),
        compiler_params=_params("parallel"),
    )(x, g)


def _rms_bwd(name, dy, x, g, resid, carry=None):
    def body(dy_ref, x_ref, g_ref, r_ref, dx_ref, dg_ref):
        i = pl.program_id(0)
        xv = x_ref[...]
        r = lax.rsqrt(jnp.mean(xv * xv, axis=-1, keepdims=True) + RMS_EPS)
        xh = xv * r
        dyv = dy_ref[...]
        dxh = dyv * g_ref[...]
        dx_ref[...] = r_ref[...] + r * (dxh - xh * jnp.mean(dxh * xh, axis=-1, keepdims=True))
        part = jnp.sum(dyv * xh, axis=0, keepdims=True)

        @pl.when(i == 0)
        def _():
            dg_ref[...] = part

        @pl.when(i > 0)
        def _():
            dg_ref[...] += part

    row = pl.BlockSpec((TM, D), lambda i: (i, 0))
    vec = pl.BlockSpec((1, D), lambda i: (0, 0))
    res, copies = _carrier_call(
        body, name=name, grid=(S // TM,), in_specs=[row, row, vec, row], out_specs=[row, vec],
        out_shape=[jax.ShapeDtypeStruct((S, D), F32), jax.ShapeDtypeStruct((1, D), F32)],
        args=(dy, x, g, resid), sem=("arbitrary",), carry=carry)
    if carry is not None:
        carry.done(copies)
    return res


def _loss_bwd(name, h, g, tgt):
    def body(h_ref, g_ref, t_ref, loss_ref, dh_ref, dg_ref):
        i = pl.program_id(0)
        xv = h_ref[...]
        gv = g_ref[...]
        r = lax.rsqrt(jnp.mean(xv * xv, axis=-1, keepdims=True) + RMS_EPS)
        xh = xv * r
        diff = xh * gv - t_ref[...]
        part_loss = 0.5 * jnp.sum(jnp.sum(diff * diff, axis=-1, keepdims=True) * (1.0 / D), axis=0, keepdims=True)
        dy = diff * (1.0 / D)
        dxh = dy * gv
        dh_ref[...] = r * (dxh - xh * jnp.mean(dxh * xh, axis=-1, keepdims=True))
        part_g = jnp.sum(dy * xh, axis=0, keepdims=True)

        @pl.when(i == 0)
        def _():
            dg_ref[...] = part_g
            loss_ref[...] = jnp.broadcast_to(part_loss, (1, LANES))

        @pl.when(i > 0)
        def _():
            dg_ref[...] += part_g
            loss_ref[...] += jnp.broadcast_to(part_loss, (1, LANES))

    row = pl.BlockSpec((TM, D), lambda i: (i, 0))
    vec = pl.BlockSpec((1, D), lambda i: (0, 0))
    return pl.pallas_call(
        body, name=name, grid=(S // TM,), in_specs=[row, vec, row],
        out_specs=[pl.BlockSpec((1, LANES), lambda i: (0, 0)), row, vec],
        out_shape=[jax.ShapeDtypeStruct((1, LANES), F32), jax.ShapeDtypeStruct((S, D), F32),
                   jax.ShapeDtypeStruct((1, D), F32)],
        compiler_params=_params("arbitrary"),
    )(h, g, tgt)


def _ffn_fwd(name, h, g, wg, wu, wd, carry=None):
    def body(h_ref, g_ref, wg_ref, wu_ref, wd_ref, o_ref, gate_ref, up_ref, u_scr):
        s = pl.program_id(1)

        @pl.when(s == 0)
        def _():
            xv = h_ref[...]
            r = lax.rsqrt(jnp.mean(xv * xv, axis=-1, keepdims=True) + RMS_EPS)
            u_scr[...] = (xv * r * g_ref[...]).astype(BF16)
            o_ref[...] = xv

        u = u_scr[...]
        gate = _dot(u, wg_ref[...], "nt")
        up = _dot(u, wu_ref[...], "nt")
        act = gate * _sigmoid(gate) * up
        o_ref[...] += _dot(act.astype(BF16), wd_ref[...])
        gate_ref[...] = gate.astype(BF16)
        up_ref[...] = up.astype(BF16)

    row = pl.BlockSpec((TM, D), lambda i, s: (i, 0))
    hid = pl.BlockSpec((None, TM, FF_SHARD), lambda i, s: (s, i, 0))
    return _carrier_call(
        body, name=name, grid=(S // TM, N_CHIPS),
        in_specs=[row, pl.BlockSpec((1, D), lambda i, s: (0, 0))]
        + [pl.BlockSpec((None, FF_SHARD, D), lambda i, s: (s, 0, 0))] * 3,
        out_specs=[row, hid, hid],
        out_shape=[jax.ShapeDtypeStruct((S, D), F32), jax.ShapeDtypeStruct((N_CHIPS, S, FF_SHARD), BF16),
                   jax.ShapeDtypeStruct((N_CHIPS, S, FF_SHARD), BF16)],
        scratch_shapes=[pltpu.VMEM((TM, D), BF16)], args=(h, g, wg, wu, wd), sem=("parallel", "arbitrary"), carry=carry)


def _ffn_bwd_hidden(name, dh, gate, up, wd):
    def body(dh_ref, gate_ref, up_ref, wd_ref, dgate_ref, dup_ref, act_ref, dhb_scr):
        @pl.when(pl.program_id(1) == 0)
        def _():
            dhb_scr[...] = dh_ref[...].astype(BF16)

        dact = _dot(dhb_scr[...], wd_ref[...], "nt")
        gv = gate_ref[...].astype(F32)
        uv = up_ref[...].astype(F32)
        sig = _sigmoid(gv)
        sil = gv * sig
        act_ref[...] = (sil * uv).astype(BF16)
        dup_ref[...] = (dact * sil).astype(BF16)
        dgate_ref[...] = (dact * uv * (sig * (1.0 + gv * (1.0 - sig)))).astype(BF16)

    row = pl.BlockSpec((TM, D), lambda i, s: (i, 0))
    hid = pl.BlockSpec((None, TM, FF_SHARD), lambda i, s: (s, i, 0))
    hid_shape = jax.ShapeDtypeStruct((N_CHIPS, S, FF_SHARD), BF16)
    return pl.pallas_call(
        body, name=name, grid=(S // TM, N_CHIPS),
        in_specs=[row, hid, hid, pl.BlockSpec((None, FF_SHARD, D), lambda i, s: (s, 0, 0))],
        out_specs=[hid, hid, hid], out_shape=[hid_shape, hid_shape, hid_shape],
        scratch_shapes=[pltpu.VMEM((TM, D), BF16)], compiler_params=_params("parallel", "arbitrary"),
    )(dh, gate, up, wd)


def _ffn_bwd_input(name, dh, h, g, dgate, dup, wg, wu):
    def body(dh_ref, h_ref, g_ref, dgate_ref, dup_ref, wg_ref, wu_ref, dhin_ref, dg_ref, u_ref, du_scr):
        i = pl.program_id(0)
        s = pl.program_id(1)
        part = _dot(dgate_ref[...], wg_ref[...]) + _dot(dup_ref[...], wu_ref[...])

        @pl.when(s == 0)
        def _():
            xv = h_ref[...]
            r = lax.rsqrt(jnp.mean(xv * xv, axis=-1, keepdims=True) + RMS_EPS)
            u_ref[...] = (xv * r * g_ref[...]).astype(BF16)
            du_scr[...] = part

        @pl.when(s > 0)
        def _():
            du_scr[...] += part

        @pl.when(s == N_CHIPS - 1)
        def _():
            xv = h_ref[...]
            r = lax.rsqrt(jnp.mean(xv * xv, axis=-1, keepdims=True) + RMS_EPS)
            xh = xv * r
            du = du_scr[...]
            dxh = du * g_ref[...]
            dhin_ref[...] = dh_ref[...] + r * (dxh - xh * jnp.mean(dxh * xh, axis=-1, keepdims=True))
            part = jnp.sum(du * xh, axis=0, keepdims=True)

            @pl.when(i == 0)
            def _():
                dg_ref[...] = part

            @pl.when(i > 0)
            def _():
                dg_ref[...] += part

    row = pl.BlockSpec((TM, D), lambda i, s: (i, 0))
    vec = pl.BlockSpec((1, D), lambda i, s: (0, 0))
    hid = pl.BlockSpec((None, TM, FF_SHARD), lambda i, s: (s, i, 0))
    return pl.pallas_call(
        body, name=name, grid=(S // TM, N_CHIPS),
        in_specs=[row, row, vec, hid, hid] + [pl.BlockSpec((None, FF_SHARD, D), lambda i, s: (s, 0, 0))] * 2,
        out_specs=[row, vec, row],
        out_shape=[jax.ShapeDtypeStruct((S, D), F32), jax.ShapeDtypeStruct((1, D), F32),
                   jax.ShapeDtypeStruct((S, D), BF16)],
        scratch_shapes=[pltpu.VMEM((TM, D), F32)],
        compiler_params=_params("arbitrary", "arbitrary"),
    )(dh, h, g, dgate, dup, wg, wu)


def _ffn_wgrads(name, u, dgate, dup, act, dh):
    nk = S // TM

    def body(u_ref, dh_ref, dgate_ref, dup_ref, act_ref, dg_ref, du_ref, dd_ref, acc_g, acc_u, acc_d):
        k = pl.program_id(1)
        u = u_ref[...]
        parts = (_dot(dgate_ref[...], u, "tn"), _dot(dup_ref[...], u, "tn"),
                 _dot(act_ref[...], dh_ref[...].astype(BF16), "tn"))
        accs = (acc_g, acc_u, acc_d)

        @pl.when(k == 0)
        def _():
            for acc, part in zip(accs, parts):
                acc[...] = part

        @pl.when(k > 0)
        def _():
            for acc, part in zip(accs, parts):
                acc[...] += part

        @pl.when(k == nk - 1)
        def _():
            for out, acc in zip((dg_ref, du_ref, dd_ref), accs):
                out[...] = acc[...].astype(BF16)

    tok = pl.BlockSpec((TM, D), lambda s, k: (k, 0))
    hid = pl.BlockSpec((None, TM, FF_SHARD), lambda s, k: (s, k, 0))
    out = pl.BlockSpec((None, FF_SHARD, D), lambda s, k: (s, 0, 0))
    shape = jax.ShapeDtypeStruct((N_CHIPS, FF_SHARD, D), BF16)
    return pl.pallas_call(
        body, name=name, grid=(N_CHIPS, nk), in_specs=[tok, tok, hid, hid, hid], out_specs=[out, out, out],
        out_shape=[shape, shape, shape], scratch_shapes=[pltpu.VMEM((FF_SHARD, D), F32)] * 3,
        compiler_params=_params("parallel", "arbitrary"))(u, dh, dgate, dup, act)


def _rope_tables():
    pos = jnp.arange(S, dtype=F32)
    inv = ROPE_THETA ** (-jnp.arange(0, MLA_ROPE, 2, dtype=F32) / MLA_ROPE)
    ang = pos[:, None] * inv[None, :]
    half = MLA_ROPE // 2
    cos = jnp.cos(ang)
    sin = jnp.sin(ang)
    one = jnp.ones((S, KR_LANE), F32)
    zero = jnp.zeros((S, KR_LANE), F32)
    tail_one = jnp.ones((S, LANES - KR_LANE - MLA_ROPE), F32)
    tail_zero = jnp.zeros((S, LANES - KR_LANE - MLA_ROPE), F32)
    cos_t = jnp.concatenate([one, cos, cos, tail_one], axis=1)
    sin_t = jnp.concatenate([zero, -sin, sin, tail_zero], axis=1)
    assert cos_t.shape == (S, LANES) and half * 2 == MLA_ROPE
    return cos_t, sin_t


def _rope(x, cos_t, sin_t, sign):
    n = x.shape[1] // LANES
    half = MLA_ROPE // 2
    lane = _iota(x.shape, 1) & (LANES - 1)
    first = (lane >= KR_LANE) & (lane < KR_LANE + half)
    swapped = jnp.where(first, pltpu.roll(x, x.shape[1] - half, 1), pltpu.roll(x, half, 1))
    c = jnp.tile(cos_t, (1, n)) if n > 1 else cos_t
    s = jnp.tile(sin_t, (1, n)) if n > 1 else sin_t
    return x * c + swapped * (s * sign)


def _mla_prep_fwd(name, proj, g_cq, g_ckv, w_uq, w_uk, w_uv, cos_t, sin_t):
    nh = MLA_H * LANES

    def body(cq_ref, ckv_ref, kr_ref, gq_ref, gkv_ref, wq_ref, wk_ref, wv_ref, cos_ref, sin_ref,
             qa_ref, ka_ref, va_ref):
        cos_v, sin_v = cos_ref[...], sin_ref[...]
        cq = cq_ref[...]
        r = lax.rsqrt(jnp.mean(cq * cq, axis=-1, keepdims=True) + RMS_EPS)
        cqn = (cq * r * gq_ref[...]).astype(BF16)
        qa_ref[...] = _rope(_dot(cqn, wq_ref[...]), cos_v, sin_v, 1.0).astype(BF16)
        ckv = ckv_ref[...]
        r = lax.rsqrt(jnp.mean(ckv * ckv, axis=-1, keepdims=True) + RMS_EPS)
        ckvn = (ckv * r * gkv_ref[...]).astype(BF16)
        lane = _iota((TM, LANES), 1)
        rot = (lane >= KR_LANE) & (lane < KR_LANE + MLA_ROPE)
        kr = jnp.where(rot, _rope(kr_ref[...], cos_v, sin_v, 1.0), 0.0)
        ka_ref[...] = (_dot(ckvn, wk_ref[...]) + jnp.tile(kr, (1, MLA_H))).astype(BF16)
        va_ref[...] = _dot(ckvn, wv_ref[...]).astype(BF16)

    full = lambda shape: pl.BlockSpec(shape, lambda i: (0, 0))
    return pl.pallas_call(
        body, name=name, grid=(S // TM,),
        in_specs=[pl.BlockSpec((TM, Q_LORA), lambda i: (i, P_CQ // Q_LORA)),
                  pl.BlockSpec((TM, KV_LORA), lambda i: (i, P_CKV // KV_LORA)),
                  pl.BlockSpec((TM, LANES), lambda i: (i, P_KR // LANES)),
                  full((1, Q_LORA)), full((1, KV_LORA)), full((Q_LORA, nh)), full((KV_LORA, nh)),
                  full((KV_LORA, MLA_H * MLA_V)),
                  pl.BlockSpec((TM, LANES), lambda i: (i, 0)), pl.BlockSpec((TM, LANES), lambda i: (i, 0))],
        out_specs=[pl.BlockSpec((TM, nh), lambda i: (i, 0)), pl.BlockSpec((TM, nh), lambda i: (i, 0)),
                   pl.BlockSpec((TM, MLA_H * MLA_V), lambda i: (i, 0))],
        out_shape=[jax.ShapeDtypeStruct((S, nh), BF16), jax.ShapeDtypeStruct((S, nh), BF16),
                   jax.ShapeDtypeStruct((S, MLA_H * MLA_V), BF16)],
        compiler_params=_params("parallel"),
    )(proj, proj, proj, g_cq, g_ckv, w_uq, w_uk, w_uv, cos_t, sin_t)


def _mla_prep_bwd(name, dqa, dka, dva, proj, g_cq, g_ckv, w_uq, w_uk, w_uv, cos_t, sin_t):
    nh = MLA_H * LANES

    def body(dqa_ref, dka_ref, dva_ref, cq_ref, ckv_ref, gq_ref, gkv_ref, wq_ref, wk_ref, wv_ref, cos_ref, sin_ref,
             dcq_ref, dckv_ref, dkr_ref, dwq_ref, dwk_ref, dwv_ref, dgq_ref, dgkv_ref):
        i = pl.program_id(0)
        cos_v, sin_v = cos_ref[...], sin_ref[...]

        def norm_bwd(x, g, dn):
            r = lax.rsqrt(jnp.mean(x * x, axis=-1, keepdims=True) + RMS_EPS)
            xh = x * r
            dxh = dn * g
            dx = r * (dxh - xh * jnp.mean(dxh * xh, axis=-1, keepdims=True))
            return dx, jnp.sum(dn * xh, axis=0, keepdims=True), (xh * g).astype(BF16)

        dq = _rope(dqa_ref[...], cos_v, sin_v, -1.0).astype(BF16)
        dcqn = _dot(dq, wq_ref[...], "nt")
        dcq, dgq, cqn = norm_bwd(cq_ref[...], gq_ref[...], dcqn)
        dcq_ref[...] = dcq.astype(BF16)
        dwq = _dot(cqn, dq, "tn")

        dka = dka_ref[...]
        dkab = dka.astype(BF16)
        dvab = dva_ref[...].astype(BF16)
        dckvn = _dot(dkab, wk_ref[...], "nt") + _dot(dvab, wv_ref[...], "nt")
        dckv, dgkv, ckvn = norm_bwd(ckv_ref[...], gkv_ref[...], dckvn)
        dckv_ref[...] = dckv.astype(BF16)
        dwk = _dot(ckvn, dkab, "tn")
        dwv = _dot(ckvn, dvab, "tn")

        fold = dka[:, 0:LANES]
        for hh in range(1, MLA_H):
            fold = fold + dka[:, hh * LANES:(hh + 1) * LANES]
        lane = _iota((TM, LANES), 1)
        rot = (lane >= KR_LANE) & (lane < KR_LANE + MLA_ROPE)
        dkr = _rope(jnp.where(rot, fold, 0.0), cos_v, sin_v, -1.0)
        dkr_ref[...] = jnp.where(rot, dkr, 0.0).astype(BF16)

        @pl.when(i == 0)
        def _():
            dwq_ref[...] = dwq
            dwk_ref[...] = dwk
            dwv_ref[...] = dwv
            dgq_ref[...] = dgq
            dgkv_ref[...] = dgkv

        @pl.when(i > 0)
        def _():
            dwq_ref[...] += dwq
            dwk_ref[...] += dwk
            dwv_ref[...] += dwv
            dgq_ref[...] += dgq
            dgkv_ref[...] += dgkv

    full = lambda shape: pl.BlockSpec(shape, lambda i: (0, 0))
    rows = lambda c: pl.BlockSpec((TM, c), lambda i: (i, 0))
    nv = MLA_H * MLA_V
    return pl.pallas_call(
        body, name=name, grid=(S // TM,),
        in_specs=[rows(nh), rows(nh), rows(nv),
                  pl.BlockSpec((TM, Q_LORA), lambda i: (i, P_CQ // Q_LORA)),
                  pl.BlockSpec((TM, KV_LORA), lambda i: (i, P_CKV // KV_LORA)),
                  full((1, Q_LORA)), full((1, KV_LORA)), full((Q_LORA, nh)), full((KV_LORA, nh)), full((KV_LORA, nv)),
                  rows(LANES), rows(LANES)],
        out_specs=[rows(Q_LORA), rows(KV_LORA), rows(LANES), full((Q_LORA, nh)), full((KV_LORA, nh)),
                   full((KV_LORA, nv)), full((1, Q_LORA)), full((1, KV_LORA))],
        out_shape=[jax.ShapeDtypeStruct((S, Q_LORA), BF16), jax.ShapeDtypeStruct((S, KV_LORA), BF16),
                   jax.ShapeDtypeStruct((S, LANES), BF16), jax.ShapeDtypeStruct((Q_LORA, nh), F32),
                   jax.ShapeDtypeStruct((KV_LORA, nh), F32), jax.ShapeDtypeStruct((KV_LORA, nv), F32),
                   jax.ShapeDtypeStruct((1, Q_LORA), F32), jax.ShapeDtypeStruct((1, KV_LORA), F32)],
        compiler_params=_params("arbitrary"),
    )(dqa, dka, dva, proj, proj, g_cq, g_ckv, w_uq, w_uk, w_uv, cos_t, sin_t)


def _head_masks(dtype):
    lane = _iota((1, LANES), 1)
    return (lane < 64).astype(dtype), (lane >= 64).astype(dtype)


def _mla_fwd(name, qa, ka, va, carry=None):
    def body(q_ref, k_ref, v_ref, o_ref, lse_ref):
        m0b, m1b = _head_masks(BF16)
        lane = _iota((QB, LANES), 1)
        left = lane < 64

        def qblock(i, _):
            r0 = pl.multiple_of(i * QB, QB)
            qs = [q_ref[pl.ds(r0, QB), hh * LANES:(hh + 1) * LANES] for hh in range(2)]
            rowc = lax.shift_right_logical(r0 + _iota((QB, QB), 0), 6)

            def kv(kb, carry):
                ms, ls, acc = carry
                c0 = pl.multiple_of(kb * QB, QB)
                v = v_ref[pl.ds(c0, QB), :]
                ok = lax.shift_right_logical(c0 + _iota((QB, QB), 1), 6) <= rowc
                new_m, new_l, alphas = [], [], []
                pv = None
                for hh in range(2):
                    k = k_ref[pl.ds(c0, QB), hh * LANES:(hh + 1) * LANES]
                    s = jnp.where(ok, _dot(qs[hh], k, "nt") * SCALE_A, NEG)
                    mn = jnp.maximum(ms[hh], jnp.max(s, axis=-1, keepdims=True))
                    p = jnp.exp(s - mn)
                    a = jnp.exp(ms[hh] - mn)
                    new_m.append(mn)
                    new_l.append(a * ls[hh] + jnp.sum(p, axis=-1, keepdims=True))
                    alphas.append(a)
                    part = _dot(p.astype(BF16), v * (m0b if hh == 0 else m1b))
                    pv = part if pv is None else pv + part
                acc = acc * jnp.where(left, alphas[0], alphas[1]) + pv
                return tuple(new_m), tuple(new_l), acc

            init = ((jnp.full((QB, 1), NEG, F32),) * 2, (jnp.zeros((QB, 1), F32),) * 2, jnp.zeros((QB, LANES), F32))
            ms, ls, acc = lax.fori_loop(0, i + 1, kv, init)
            o_ref[pl.ds(r0, QB), :] = acc * jnp.where(left, 1.0 / ls[0], 1.0 / ls[1])
            lse_ref[pl.ds(r0, QB), :] = jnp.where(left, ms[0] + jnp.log(ls[0]), ms[1] + jnp.log(ls[1]))
            return 0

        lax.fori_loop(0, S // QB, qblock, 0)

    pair = lambda w: pl.BlockSpec((S, w), lambda p: (0, p))
    return _carrier_call(
        body, name=name, grid=(MLA_H // 2,), in_specs=[pair(2 * LANES), pair(2 * LANES), pair(LANES)],
        out_specs=[pair(LANES), pair(LANES)],
        out_shape=[jax.ShapeDtypeStruct((S, MLA_H * MLA_V), F32), jax.ShapeDtypeStruct((S, MLA_H * MLA_V), F32)],
        args=(qa, ka, va), sem=("parallel",), carry=carry)


def _mla_bwd(name, qa, ka, va, o, lse, do, do_block0, carry=None):
    def body(q_ref, k_ref, v_ref, o_ref, lse_ref, do_ref, dq_ref, dk_ref, dv_ref):
        m0f, m1f = _head_masks(F32)
        m0b, m1b = _head_masks(BF16)
        dk_ref[...] = jnp.zeros_like(dk_ref)
        dv_ref[...] = jnp.zeros_like(dv_ref)

        def qblock(i, _):
            r0 = pl.multiple_of(i * QB, QB)
            rows = pl.ds(r0, QB)
            do_f = do_ref[rows, :]
            prod = do_f * o_ref[rows, :]
            deltas = [jnp.sum(prod * m0f, axis=-1, keepdims=True), jnp.sum(prod * m1f, axis=-1, keepdims=True)]
            lse_v = lse_ref[rows, :]
            lses = [lse_v[:, 0:1], lse_v[:, 64:65]]
            dob = do_f.astype(BF16)
            dos = [dob * m0b, dob * m1b]
            qs = [q_ref[rows, hh * LANES:(hh + 1) * LANES] for hh in range(2)]
            rowc = lax.shift_right_logical(r0 + _iota((QB, QB), 0), 6)

            def kv(kb, dqs):
                c0 = pl.multiple_of(kb * QB, QB)
                cols = pl.ds(c0, QB)
                v = v_ref[cols, :]
                ok = lax.shift_right_logical(c0 + _iota((QB, QB), 1), 6) <= rowc
                out = []
                dv = None
                for hh in range(2):
                    k = k_ref[cols, hh * LANES:(hh + 1) * LANES]
                    s = _dot(qs[hh], k, "nt") * SCALE_A
                    p = jnp.where(ok, jnp.exp(s - lses[hh]), 0.0)
                    dp = _dot(dos[hh], v, "nt")
                    ds = (p * (dp - deltas[hh]) * SCALE_A).astype(BF16)
                    out.append(dqs[hh] + _dot(ds, k))
                    dk_ref[cols, hh * LANES:(hh + 1) * LANES] += _dot(ds, qs[hh], "tn")
                    part = _dot(p.astype(BF16), dos[hh], "tn")
                    dv = part if dv is None else dv + part
                dv_ref[cols, :] += dv
                return tuple(out)

            dqs = lax.fori_loop(0, i + 1, kv, (jnp.zeros((QB, LANES), F32),) * 2)
            for hh in range(2):
                dq_ref[rows, hh * LANES:(hh + 1) * LANES] = dqs[hh]
            return 0

        lax.fori_loop(0, S // QB, qblock, 0)

    pair = lambda w: pl.BlockSpec((S, w), lambda p: (0, p))
    return _carrier_call(
        body, name=name, grid=(MLA_H // 2,),
        in_specs=[pair(2 * LANES), pair(2 * LANES), pair(LANES), pair(LANES), pair(LANES),
                  pl.BlockSpec((S, LANES), lambda p: (0, do_block0 + p))],
        out_specs=[pair(2 * LANES), pair(2 * LANES), pair(LANES)],
        out_shape=[jax.ShapeDtypeStruct((S, MLA_H * LANES), F32), jax.ShapeDtypeStruct((S, MLA_H * LANES), F32),
                   jax.ShapeDtypeStruct((S, MLA_H * MLA_V), F32)],
        args=(qa, ka, va, o, lse, do), sem=("parallel",), carry=carry)


def _sb_weights(q_h, k, c, before, tri_suffix):
    z = _dot(q_h, k, "nt") * SCALE_B
    sp = _softplus(z)
    log_keep = jnp.where(before, -sp, 0.0)
    log_between = _split_dot(log_keep, tri_suffix) + c
    w = jnp.where(before, jnp.exp(z - sp + log_between), 0.0)
    return w, jnp.exp(z - sp), jnp.sum(log_keep, axis=-1, keepdims=True)


def _sb_fwd(name, proj, carry=None):
    def body(q_ref, k_ref, v_ref, o_ref):
        m0b, m1b = _head_masks(BF16)
        tri_suffix = (_iota((QB, QB), 0) > _iota((QB, QB), 1)).astype(BF16)

        def qblock(i, _):
            r0 = pl.multiple_of(i * QB, QB)
            q = q_ref[pl.ds(r0, QB), :].astype(BF16)
            qs = [q * m0b, q * m1b]
            rowg = r0 + _iota((QB, QB), 0)

            def kv(step, carry):
                cs, acc = carry
                c0 = pl.multiple_of((i - step) * QB, QB)
                k = k_ref[pl.ds(c0, QB), :].astype(BF16)
                v = v_ref[pl.ds(c0, QB), :].astype(BF16)
                before = (c0 + _iota((QB, QB), 1)) < rowg
                new_c = []
                for hh in range(2):
                    w, _, tot = _sb_weights(qs[hh], k, cs[hh], before, tri_suffix)
                    new_c.append(cs[hh] + tot)
                    acc = acc + _dot(w.astype(BF16), v * (m0b if hh == 0 else m1b))
                return tuple(new_c), acc

            init = ((jnp.zeros((QB, 1), F32),) * 2, jnp.zeros((QB, LANES), F32))
            _, acc = lax.fori_loop(0, i + 1, kv, init)
            o_ref[pl.ds(r0, QB), :] = acc.astype(BF16)
            return 0

        lax.fori_loop(0, S // QB, qblock, 0)

    col = lambda base: pl.BlockSpec((S, LANES), lambda p: (0, base // LANES + p))
    return _carrier_call(
        body, name=name, grid=(SB_H // 2,), in_specs=[col(P_QB), col(P_KB), col(P_VB)],
        out_specs=[pl.BlockSpec((S, LANES), lambda p: (0, p))],
        out_shape=[jax.ShapeDtypeStruct((S, SB_H * SB_DIM), BF16)],
        args=(proj, proj, proj), sem=("parallel",), carry=carry)


def _sb_bwd(name, proj, do, do_block0, carry=None):
    nb = S // QB

    def body(q_ref, k_ref, v_ref, do_ref, dq_ref, dk_ref, dv_ref, sig_scr, dl_scr, dk_acc, dv_acc):
        m0b, m1b = _head_masks(BF16)
        tri_suffix = (_iota((QB, QB), 0) > _iota((QB, QB), 1)).astype(BF16)
        tri_prefix = (_iota((QB, QB), 0) < _iota((QB, QB), 1)).astype(BF16)
        dk_acc[...] = jnp.zeros_like(dk_acc)
        dv_acc[...] = jnp.zeros_like(dv_acc)

        def qblock(i, _):
            r0 = pl.multiple_of(i * QB, QB)
            rows = pl.ds(r0, QB)
            q = q_ref[rows, :].astype(BF16)
            qs = [q * m0b, q * m1b]
            dob = do_ref[rows, :].astype(BF16)
            dos = [dob * m0b, dob * m1b]
            rowg = r0 + _iota((QB, QB), 0)

            def sweep_left(step, cs):
                kb = i - step
                c0 = pl.multiple_of(kb * QB, QB)
                cols = pl.ds(c0, QB)
                k = k_ref[cols, :].astype(BF16)
                v = v_ref[cols, :].astype(BF16)
                before = (c0 + _iota((QB, QB), 1)) < rowg
                new_c = []
                dv = None
                for hh in range(2):
                    w, sig, tot = _sb_weights(qs[hh], k, cs[hh], before, tri_suffix)
                    new_c.append(cs[hh] + tot)
                    sig_scr[hh, kb] = sig
                    dl_scr[hh, kb] = _dot(dos[hh], v, "nt") * w
                    part = _dot(w.astype(BF16), dos[hh], "tn")
                    dv = part if dv is None else dv + part
                dv_acc[cols, :] += dv
                return tuple(new_c)

            lax.fori_loop(0, i + 1, sweep_left, (jnp.zeros((QB, 1), F32),) * 2)

            def sweep_right(kb, carry):
                ps, dq = carry
                c0 = pl.multiple_of(kb * QB, QB)
                cols = pl.ds(c0, QB)
                k = k_ref[cols, :].astype(BF16)
                before = (c0 + _iota((QB, QB), 1)) < rowg
                new_p = []
                dk = None
                for hh in range(2):
                    dl = dl_scr[hh, kb]
                    sig = sig_scr[hh, kb]
                    earlier = _split_dot(dl, tri_prefix) + ps[hh]
                    new_p.append(ps[hh] + jnp.sum(dl, axis=-1, keepdims=True))
                    dz = (jnp.where(before, dl * (1.0 - sig) - earlier * sig, 0.0) * SCALE_B).astype(BF16)
                    dq = dq + _dot(dz, k * (m0b if hh == 0 else m1b))
                    part = _dot(dz, qs[hh], "tn")
                    dk = part if dk is None else dk + part
                dk_acc[cols, :] += dk
                return tuple(new_p), dq

            init = ((jnp.zeros((QB, 1), F32),) * 2, jnp.zeros((QB, LANES), F32))
            _, dq = lax.fori_loop(0, i + 1, sweep_right, init)
            dq_ref[rows, :] = dq.astype(BF16)
            return 0

        lax.fori_loop(0, nb, qblock, 0)
        dk_ref[...] = dk_acc[...].astype(BF16)
        dv_ref[...] = dv_acc[...].astype(BF16)

    col = lambda base: pl.BlockSpec((S, LANES), lambda p: (0, base // LANES + p))
    out = pl.BlockSpec((S, LANES), lambda p: (0, p))
    shape = jax.ShapeDtypeStruct((S, SB_H * SB_DIM), BF16)
    return _carrier_call(
        body, name=name, grid=(SB_H // 2,),
        in_specs=[col(P_QB), col(P_KB), col(P_VB), pl.BlockSpec((S, LANES), lambda p: (0, do_block0 + p))],
        out_specs=[out, out, out], out_shape=[shape, shape, shape],
        scratch_shapes=[pltpu.VMEM((2, nb, QB, QB), F32), pltpu.VMEM((2, nb, QB, QB), F32),
                        pltpu.VMEM((S, LANES), F32), pltpu.VMEM((S, LANES), F32)],
        args=(proj, proj, proj, do), sem=("parallel",), carry=carry)


def _band_row_index():
    j = np.arange(TOEP_W)
    rel = np.clip(LEFT_CHUNKS * CHUNK - j, -REL_CLIP, REL_CLIP) + REL_CLIP
    rel[BAND_W:] = 2 * REL_CLIP
    return rel.astype(np.int32)


def _band_tiles(r0_ref, q_ref, kpad, vpad, m, m0b, m1b, static_ok, bias):
    r0 = pl.multiple_of(m * BQ, BQ)
    q = q_ref[0, pl.ds(r0, BQ), :]
    kw = kpad[pl.ds(r0, BAND_W), :]
    vw = vpad[pl.ds(r0, BAND_W), :]
    ok = static_ok & ((r0 - BAND_PAD + _iota((BQ, BAND_W), 1)) >= 0)
    qs = [q * m0b, q * m1b]
    ps = []
    for hh in range(2):
        s = jnp.where(ok, _dot(qs[hh], kw, "nt") * SCALE_C + bias[hh], NEG)
        e = jnp.exp(s - jnp.max(s, axis=-1, keepdims=True))
        ps.append(e * (1.0 / jnp.sum(e, axis=-1, keepdims=True)))
    return r0, qs, kw, vw, ps


def _band_setup(qkv_ref, r0_ref, kpad, vpad):
    kpad[0:BAND_PAD, :] = jnp.zeros((BAND_PAD, LANES), BF16)
    vpad[0:BAND_PAD, :] = jnp.zeros((BAND_PAD, LANES), BF16)
    kpad[BAND_PAD:, :] = qkv_ref[1]
    vpad[BAND_PAD:, :] = qkv_ref[2]
    jc = lax.shift_right_logical(_iota((BQ, BAND_W), 1), 6)
    rc = lax.shift_right_logical(_iota((BQ, BAND_W), 0), 6)
    static_ok = (jc >= rc) & (jc <= rc + LEFT_CHUNKS)
    bias = []
    for hh in range(2):
        row = jnp.broadcast_to(r0_ref[hh:hh + 1, :], (BQ, TOEP_W))
        bias.append(pltpu.roll(row, 0, 1, stride=1, stride_axis=0)[:, :BAND_W])
    return static_ok, bias


def _band_fwd(name, qkv, r0, carry=None):
    def body(qkv_ref, r0_ref, o_ref, kpad, vpad):
        m0b, m1b = _head_masks(BF16)
        static_ok, bias = _band_setup(qkv_ref, r0_ref, kpad, vpad)

        def qblock(m, _):
            r0_, _, _, vw, ps = _band_tiles(r0_ref, qkv_ref, kpad, vpad, m, m0b, m1b, static_ok, bias)
            o = _dot(ps[0].astype(BF16), vw * m0b) + _dot(ps[1].astype(BF16), vw * m1b)
            o_ref[pl.ds(r0_, BQ), :] = o.astype(BF16)
            return 0

        lax.fori_loop(0, S // BQ, qblock, 0)

    return _carrier_call(
        body, name=name, grid=(C_H // 2,),
        in_specs=[pl.BlockSpec((3, S, LANES), lambda p: (0, 0, p)), pl.BlockSpec((None, 2, TOEP_W), lambda p: (p, 0, 0))],
        out_specs=[pl.BlockSpec((S, LANES), lambda p: (0, p))],
        out_shape=[jax.ShapeDtypeStruct((S, C_H * C_DIM), BF16)],
        scratch_shapes=[pltpu.VMEM((S + BAND_PAD, LANES), BF16), pltpu.VMEM((S + BAND_PAD, LANES), BF16)],
        args=(qkv, r0), sem=("parallel",), carry=carry)


def _band_bwd(name, qkv, r0, do, carry=None):
    def body(qkv_ref, r0_ref, do_ref, dqkv_ref, dr0_ref, kpad, vpad, dkpad, dvpad, db_acc):
        m0b, m1b = _head_masks(BF16)
        static_ok, bias = _band_setup(qkv_ref, r0_ref, kpad, vpad)
        dkpad[...] = jnp.zeros_like(dkpad)
        dvpad[...] = jnp.zeros_like(dvpad)
        db_acc[...] = jnp.zeros_like(db_acc)

        def qblock(m, _):
            r0_, qs, kw, vw, ps = _band_tiles(r0_ref, qkv_ref, kpad, vpad, m, m0b, m1b, static_ok, bias)
            dob = do_ref[pl.ds(r0_, BQ), :].astype(BF16)
            dos = [dob * m0b, dob * m1b]
            dq = None
            dk = None
            dv = None
            for hh in range(2):
                p = ps[hh]
                dp = _dot(dos[hh], vw, "nt")
                ds = p * (dp - jnp.sum(dp * p, axis=-1, keepdims=True))
                db_acc[hh, :, 0:BAND_W] += ds
                dsb = (ds * SCALE_C).astype(BF16)
                t = _dot(dsb, kw * (m0b if hh == 0 else m1b))
                dq = t if dq is None else dq + t
                t = _dot(dsb, qs[hh], "tn")
                dk = t if dk is None else dk + t
                t = _dot(p.astype(BF16), dos[hh], "tn")
                dv = t if dv is None else dv + t
            dqkv_ref[0, pl.ds(r0_, BQ), :] = dq.astype(BF16)
            dkpad[pl.ds(r0_, BAND_W), :] += dk
            dvpad[pl.ds(r0_, BAND_W), :] += dv
            return 0

        lax.fori_loop(0, S // BQ, qblock, 0)
        dqkv_ref[1] = dkpad[BAND_PAD:, :].astype(BF16)
        dqkv_ref[2] = dvpad[BAND_PAD:, :].astype(BF16)
        sub = _iota((8, TOEP_W), 0)
        for hh in range(2):
            folded = db_acc[hh, 0:8, :]
            for a in range(1, BQ // 8):
                folded = folded + pltpu.roll(db_acc[hh, 8 * a:8 * a + 8, :], TOEP_W - 8 * a, 1)
            for bit in range(3):
                moved = pltpu.roll(folded, TOEP_W - (1 << bit), 1)
                folded = jnp.where((sub & (1 << bit)) != 0, moved, folded)
            dr0_ref[hh:hh + 1, :] = jnp.sum(folded, axis=0, keepdims=True)

    return _carrier_call(
        body, name=name, grid=(C_H // 2,),
        in_specs=[pl.BlockSpec((3, S, LANES), lambda p: (0, 0, p)), pl.BlockSpec((None, 2, TOEP_W), lambda p: (p, 0, 0)),
                  pl.BlockSpec((S, LANES), lambda p: (0, p))],
        out_specs=[pl.BlockSpec((3, S, LANES), lambda p: (0, 0, p)), pl.BlockSpec((None, 2, TOEP_W), lambda p: (p, 0, 0))],
        out_shape=[jax.ShapeDtypeStruct((3, S, C_H * C_DIM), BF16), jax.ShapeDtypeStruct((C_H // 2, 2, TOEP_W), F32)],
        scratch_shapes=[pltpu.VMEM((S + BAND_PAD, LANES), BF16), pltpu.VMEM((S + BAND_PAD, LANES), BF16),
                        pltpu.VMEM((S + BAND_PAD, LANES), F32), pltpu.VMEM((S + BAND_PAD, LANES), F32),
                        pltpu.VMEM((2, BQ, TOEP_W), F32)],
        args=(qkv, r0, do), sem=("parallel",), carry=carry)


def _bias_table_grad(name, dr0):
    w_out = 5 * LANES

    def body(d_ref, o_ref):
        j = _iota((TOEP_W, w_out), 0)
        rel = jnp.clip(LEFT_CHUNKS * CHUNK - j, -REL_CLIP, REL_CLIP) + REL_CLIP
        rel = jnp.where(j >= BAND_W, 2 * REL_CLIP, rel)
        onehot = (rel == _iota((TOEP_W, w_out), 1)).astype(BF16)
        d = d_ref[...]
        hi = d.astype(BF16)
        mid = (d - hi.astype(F32))
        mid_b = mid.astype(BF16)
        lo = (mid - mid_b.astype(F32)).astype(BF16)
        o_ref[...] = _dot(hi, onehot) + _dot(mid_b, onehot) + _dot(lo, onehot)

    return pl.pallas_call(
        body, name=name, out_shape=jax.ShapeDtypeStruct((C_H, w_out), F32),
        in_specs=[pl.BlockSpec((C_H, TOEP_W), lambda: (0, 0))], out_specs=pl.BlockSpec((C_H, w_out), lambda: (0, 0)),
        grid=(),
    )(dr0)


def _carry_gather(cy, slots, names, ici, d2d):
    idx = [cy.operand(slots[n], True) for n in names]
    n = len(names)
    base_i = cy.sems(3 * n) if ici else 0
    base_d = cy.sems(3 * n) if d2d else 0

    def piece(refs, t, slot, cc):
        return refs[idx[t]].at[slot, _half_rows(cc, slots[names[t]].shape[1] // 2), :]

    def over_ici(refs, send, recv, arriving):
        x, y, c, chips = _position()
        out = []
        for t in range(n):
            for j in range(3):
                r = piece(refs, t, 2 * chips[j][0] + chips[j][1] if arriving else 2 * x + y, c)
                out.append(_remote(r, r, send, recv, base_i + 3 * t + j, (*chips[j], c)))
        return out

    def over_d2d(refs, send, recv, arriving):
        x, y, c, chips = _position()
        out = []
        for t in range(n):
            for j in range(3):
                r = piece(refs, t, 2 * chips[j][0] + chips[j][1], 1 - c if arriving else c)
                out.append(_remote(r, r, send, recv, base_d + 3 * t + j, (x, y, 1 - c)))
        return out

    def start_ici(refs, fresh, send, recv):
        for cp in over_ici(refs, send, recv, False):
            cp.start()

    def wait_ici(refs, fresh, send, recv):
        for cp in over_ici(refs, send, recv, True):
            cp.wait_recv()
        for cp in over_ici(refs, send, recv, False):
            cp.wait_send()

    def start_d2d(refs, fresh, send, recv):
        for cp in over_d2d(refs, send, recv, False):
            cp.start()

    def wait_d2d(refs, fresh, send, recv):
        for cp in over_d2d(refs, send, recv, True):
            cp.wait_recv()
        for cp in over_d2d(refs, send, recv, False):
            cp.wait_send()

    if ici and d2d:
        cy.starts.append(start_ici)
        cy.finishes += [wait_ici, start_d2d, wait_d2d]
    elif ici:
        cy.starts.append(start_ici)
        cy.finishes.append(wait_ici)
    else:
        cy.starts.append(start_d2d)
        cy.finishes.append(wait_d2d)

    def done(aliased, fresh):
        for t, name in enumerate(names):
            slots[name] = aliased[idx[t]]

    cy.on_done.append(done)


def _carry_chip_exchange(cy, sums, got, names):
    idx = [cy.operand(sums[n], False) for n in names]
    out = [cy.result((3,) + sums[n].shape[1:], BF16) for n in names]
    base = cy.sems(3 * len(names))

    def copies(refs, fresh, send, recv):
        x, y, c, chips = _position()
        return [_remote(refs[idx[t]].at[2 * chips[j][0] + chips[j][1]], fresh[out[t]].at[j], send, recv, base + 3 * t + j,
                        (*chips[j], c)) for t in range(len(names)) for j in range(3)]

    def start(refs, fresh, send, recv):
        for cp in copies(refs, fresh, send, recv):
            cp.start()

    def wait(refs, fresh, send, recv):
        for cp in copies(refs, fresh, send, recv):
            cp.wait()

    cy.starts.append(start)
    cy.finishes.append(wait)

    def done(aliased, fresh):
        for t, name in enumerate(names):
            got[name] = fresh[out[t]]

    cy.on_done.append(done)


def _run_carry(name, cy):
    _, res = _carrier_call(None, name=name, grid=(), in_specs=[], out_specs=[], out_shape=[], args=(), sem=(), carry=cy)
    cy.done(res)


FIRST_WEIGHTS = ("ev_w_in", "ev_w_uq", "ev_w_ukv")
WEIGHTS_A = ("ev_w_out", "w_gate0", "w_up0")
WEIGHTS_B = ("w_down0", "od_w_qkv", "od_w_out")
WEIGHTS_C = ("w_gate1", "w_up1")
WEIGHTS_D = ("w_down1",)
GRAD_GROUPS = {"ffn1": ("w_gate1", "w_up1", "w_down1"), "od": ("od_w_qkv", "od_w_out"),
               "ffn0": ("w_gate0", "w_up0", "w_down0"), "ev": ("ev_w_in", "ev_w_uq", "ev_w_ukv", "ev_w_out")}


def _carry_pair_exchange(cy, parts, theirs, names):
    idx = [cy.operand(parts[n], False) for n in names]
    out = [cy.result((N_CHIPS, parts[n].shape[1] // 2, parts[n].shape[2]), BF16) for n in names]
    base = cy.sems(len(names))

    def copies(refs, fresh, send, recv):
        x, y, c, _ = _position()
        return [_remote(refs[idx[t]].at[:, _half_rows(1 - c, parts[n].shape[1] // 2), :], fresh[out[t]], send, recv,
                        base + t, (x, y, 1 - c)) for t, n in enumerate(names)]

    cy.starts.append(lambda refs, fresh, send, recv: [cp.start() for cp in copies(refs, fresh, send, recv)])
    cy.finishes.append(lambda refs, fresh, send, recv: [cp.wait() for cp in copies(refs, fresh, send, recv)])

    def done(aliased, fresh):
        for t, name in enumerate(names):
            theirs[name] = fresh[out[t]]

    cy.on_done.append(done)


def _carry_sibling_exchange(cy, fulls, pieces):
    idx = [cy.operand(fulls[p], True) for p, _ in pieces]
    base = cy.sems(len(pieces))

    def copies(refs, send, recv, arriving):
        x, y, c, _ = _position()
        out = []
        for t, (p, layer) in enumerate(pieces):
            r = refs[idx[t]].at[layer, _half_rows(1 - c if arriving else c, fulls[p].shape[1] // 2), :]
            out.append(_remote(r, r, send, recv, base + t, (x, y, 1 - c)))
        return out

    def start(refs, fresh, send, recv):
        for cp in copies(refs, send, recv, False):
            cp.start()

    def wait(refs, fresh, send, recv):
        for cp in copies(refs, send, recv, True):
            cp.wait_recv()
        for cp in copies(refs, send, recv, False):
            cp.wait_send()

    cy.starts.append(start)
    cy.finishes.append(wait)

    def done(aliased, fresh):
        for t, (p, _) in enumerate(pieces):
            fulls[p] = aliased[idx[t]]

    cy.on_done.append(done)


RIDES = {
    "mla_attn": (("gather_ici", WEIGHTS_A),),
    "sb_attn": (("gather_d2d", WEIGHTS_A), ("gather_ici", WEIGHTS_B)),
    "ev_out": (("gather_d2d", WEIGHTS_B),),
    "ffn0": (("gather_ici", WEIGHTS_C),),
    "qkv": (("gather_d2d", WEIGHTS_C),),
    "band_attn": (("gather_ici", WEIGHTS_D),),
    "od_out": (("gather_d2d", WEIGHTS_D),),
    "od_out_bwd_w": (("pair", "ffn1"),),
    "band_attn_bwd": (("chips", "ffn1"),),
    "rms_mix1_bwd": (("pair", "od"),),
    "ev_out_bwd_w": (("pair", "ffn0"),),
    "mla_attn_bwd": (("chips", "od"), ("sibling", "ffn1")),
    "sb_attn_bwd": (("chips", "ffn0"), ("sibling", "od")),
    "proj_in_bwd_w": (("sibling", "ffn0"),),
    "proj_in_bwd_x": (("chips", "ev"),),
}


class _Exchanges:
    def __init__(self, slots, pos, shapes):
        self.slots, self.pos, self.shapes = dict(slots), pos, shapes
        self.parts, self.theirs, self.sums, self.got, self.fulls = {}, {}, {}, {}, {}

    def begin(self):
        cy = _Carry()
        _carry_gather(cy, self.slots, FIRST_WEIGHTS, True, True)
        _run_carry("gather_first", cy)

    def weights(self, *names):
        return [self.slots[n] for n in names]

    def _pair_sums(self, group):
        for n in GRAD_GROUPS[group]:
            if n not in self.sums:
                self.sums[n] = _pair_sum("pair_sum_" + n, self.parts[n], self.theirs[n], self.pos)

    def _chip_sums(self, group):
        for n in GRAD_GROUPS[group]:
            param, layer = PART_OF[n]
            self.fulls[param] = _chip_sum("chip_sum_" + n, self.sums[n], self.got[n], self.pos, layer,
                                          self.shapes[param], self.fulls.get(param))

    def carry(self, stage):
        cy = _Carry()
        for step, what in RIDES[stage]:
            if step == "gather_ici":
                _carry_gather(cy, self.slots, what, True, False)
            elif step == "gather_d2d":
                _carry_gather(cy, self.slots, what, False, True)
            elif step == "pair":
                _carry_pair_exchange(cy, self.parts, self.theirs, GRAD_GROUPS[what])
            elif step == "chips":
                self._pair_sums(what)
                _carry_chip_exchange(cy, self.sums, self.got, GRAD_GROUPS[what])
            elif step == "sibling":
                self._chip_sums(what)
                _carry_sibling_exchange(cy, self.fulls, [PART_OF[n] for n in GRAD_GROUPS[what]])
        return cy

    def grads(self, group, parts):
        self.parts.update(parts)
        if group == "ev":
            cy = _Carry()
            _carry_pair_exchange(cy, self.parts, self.theirs, GRAD_GROUPS[group])
            _run_carry("grads_pair_ev", cy)

    def finish(self):
        cy = _Carry()
        self._chip_sums("ev")
        _carry_sibling_exchange(cy, self.fulls, [PART_OF[n] for n in GRAD_GROUPS["ev"]])
        _run_carry("grads_sibling_ev", cy)
        return {n: self.fulls[n] for n in BIG}


class _NoExchanges:
    def __init__(self, slots):
        self.slots, self.parts = dict(slots), {}

    def begin(self):
        pass

    def weights(self, *names):
        return [self.slots[n] for n in names]

    def carry(self, stage):
        return None

    def grads(self, group, parts):
        self.parts.update(parts)


def _first_weights(w_in_s, w_uq_s, w_ukv_s):
    gw = {"ev_w_in": w_in_s, "ev_w_uq": w_uq_s, "ev_w_ukv": w_ukv_s}
    w_in = jnp.moveaxis(gw["ev_w_in"], 0, 1).reshape(D, EVEN_IN)
    z = lambda n: jnp.zeros((D, n), BF16)
    w_in_p = jnp.concatenate(
        [w_in[:, 0:384], z(128), w_in[:, 384:640], w_in[:, 672:2208], z(KR_LANE), w_in[:, 640:672],
         z(LANES - KR_LANE - MLA_ROPE)], axis=1)
    w_uq = jnp.moveaxis(gw["ev_w_uq"], 0, 1).reshape(Q_LORA, MLA_H, MLA_NOPE + MLA_ROPE)
    w_uq_p = jnp.concatenate([w_uq, jnp.zeros((Q_LORA, MLA_H, LANES - MLA_NOPE - MLA_ROPE), BF16)], axis=2)
    w_ukv = jnp.moveaxis(gw["ev_w_ukv"], 0, 1).reshape(KV_LORA, MLA_H, MLA_NOPE + MLA_V)
    w_uk_p = jnp.concatenate([w_ukv[:, :, :MLA_NOPE], jnp.zeros((KV_LORA, MLA_H, LANES - MLA_NOPE), BF16)], axis=2)
    return dict(
        w_in=w_in_p, w_uq=w_uq_p.reshape(Q_LORA, MLA_H * LANES), w_uk=w_uk_p.reshape(KV_LORA, MLA_H * LANES),
        w_uv=w_ukv[:, :, MLA_NOPE:].reshape(KV_LORA, MLA_H * MLA_V))


def _proj_mm(name, u, w_in):
    return _mm(name, u, w_in, kind="nn", grid=(S // TM, 1, 1),
               a_spec=pl.BlockSpec((TM, D), lambda i, j, k: (i, 0)), b_spec=pl.BlockSpec((D, P_IN), lambda i, j, k: (0, 0)),
               o_spec=pl.BlockSpec((TM, P_IN), lambda i, j, k: (i, 0)), out_shape=(S, P_IN), out_dtype=F32, acc_shape=None)


def _out_proj(name, o, w, resid, carry=None):
    return _mm(name, o, w, kind="nn", grid=(S // TM, 1, 1),
               a_spec=pl.BlockSpec((TM, D), lambda i, j, k: (i, 0)), b_spec=pl.BlockSpec((D, D), lambda i, j, k: (0, 0)),
               o_spec=pl.BlockSpec((TM, D), lambda i, j, k: (i, 0)), out_shape=(S, D), out_dtype=F32, acc_shape=None,
               resid=resid, r_spec=pl.BlockSpec((TM, D), lambda i, j, k: (i, 0)), carry=carry)


def _out_proj_bwd(name, dh, o, w, ex):
    d_o = _mm(name + "_x", dh, w, kind="nt", grid=(S // TM, 1, 1),
              a_spec=pl.BlockSpec((TM, D), lambda i, j, k: (i, 0)), b_spec=pl.BlockSpec((D, D), lambda i, j, k: (0, 0)),
              o_spec=pl.BlockSpec((TM, D), lambda i, j, k: (i, 0)), out_shape=(S, D), out_dtype=F32, acc_shape=None)
    d_w = _mm(name + "_w", o, dh, kind="tn", grid=(2, S // TM),
              a_spec=pl.BlockSpec((TM, TM), lambda j, k: (k, j)), b_spec=pl.BlockSpec((TM, D), lambda j, k: (k, 0)),
              o_spec=pl.BlockSpec((TM, D), lambda j, k: (j, 0)), out_shape=(D, D), out_dtype=BF16, acc_shape=(TM, D),
              carry=ex.carry(name + "_w"))
    return d_o, d_w


def _local_step(x, tgt, sm, ex):
    def riding(stage, fn, *args):
        cy = ex.carry(stage)
        res, copies = fn(stage, *args, carry=cy)
        if cy is not None:
            cy.done(copies)
        return res

    cos_t, sin_t = _rope_tables()
    g_mix, g_ffn = sm["g_mix"], sm["g_ffn"]
    r0 = sm["od_rel_bias"][0][:, _band_row_index()].reshape(C_H // 2, 2, TOEP_W)
    nt = 3

    ex.begin()
    w = _first_weights(*ex.weights(*FIRST_WEIGHTS))
    u0 = _rms_fwd("rms_mix0", x, g_mix[0:1])
    proj = _proj_mm("proj_in", u0, w["w_in"])
    qa, ka, va = _mla_prep_fwd("mla_prep", proj, sm["ev_g_cq"], sm["ev_g_ckv"], w["w_uq"], w["w_uk"], w["w_uv"], cos_t, sin_t)
    o_a, lse = riding("mla_attn", _mla_fwd, qa, ka, va)
    o_b, = riding("sb_attn", _sb_fwd, proj)
    o_ev = jnp.concatenate([o_a.astype(BF16), o_b], axis=1)
    w["ev_w_out"] = ex.weights("ev_w_out")[0].reshape(D, D)
    h1 = _out_proj("ev_out", o_ev, w["ev_w_out"], x, ex.carry("ev_out"))
    w["w_gate0"], w["w_up0"], w["w_down0"] = ex.weights("w_gate0", "w_up0", "w_down0")
    h2, gate0, up0 = riding("ffn0", _ffn_fwd, h1, g_ffn[0:1], w["w_gate0"], w["w_up0"], w["w_down0"])
    w["w_qkv"] = jnp.moveaxis(ex.weights("od_w_qkv")[0], 0, 1).reshape(D, nt * D)
    u2 = _rms_fwd("rms_mix1", h2, g_mix[1:2])
    qkv = _mm("qkv", u2, w["w_qkv"], kind="nn", grid=(S // TM, nt, 1),
              a_spec=pl.BlockSpec((TM, D), lambda i, t, k: (i, 0)), b_spec=pl.BlockSpec((D, D), lambda i, t, k: (0, t)),
              o_spec=pl.BlockSpec((None, TM, D), lambda i, t, k: (t, i, 0)),
              out_shape=(nt, S, D), out_dtype=BF16, acc_shape=None, carry=ex.carry("qkv"))
    o_od, = riding("band_attn", _band_fwd, qkv, r0)
    w["od_w_out"] = ex.weights("od_w_out")[0].reshape(D, D)
    h3 = _out_proj("od_out", o_od, w["od_w_out"], h2, ex.carry("od_out"))
    w["w_gate1"], w["w_up1"], w["w_down1"] = ex.weights("w_gate1", "w_up1", "w_down1")
    (h4, gate1, up1), _ = _ffn_fwd("ffn1", h3, g_ffn[1:2], w["w_gate1"], w["w_up1"], w["w_down1"])

    loss, dh4, dg_final = _loss_bwd("loss", h4, sm["g_final"].reshape(1, D), tgt)

    dgate, dup, act = _ffn_bwd_hidden("ffn1_bwd_h", dh4, gate1, up1, w["w_down1"])
    dh3, dg_ffn1, u3 = _ffn_bwd_input("ffn1_bwd_x", dh4, h3, g_ffn[1:2], dgate, dup, w["w_gate1"], w["w_up1"])
    d_wg1, d_wu1, d_wd1 = _ffn_wgrads("ffn1_dw", u3, dgate, dup, act, dh4)
    ex.grads("ffn1", {"w_gate1": d_wg1, "w_up1": d_wu1, "w_down1": d_wd1})

    d_ood, d_w_od_out = _out_proj_bwd("od_out_bwd", dh3, o_od, w["od_w_out"], ex)
    dqkv, dr0 = riding("band_attn_bwd", _band_bwd, qkv, r0, d_ood)
    du2 = _mm("qkv_bwd_x", dqkv, w["w_qkv"], kind="nt", grid=(S // TM, nt),
              a_spec=pl.BlockSpec((None, TM, D), lambda i, t: (t, i, 0)), b_spec=pl.BlockSpec((D, D), lambda i, t: (0, t)),
              o_spec=pl.BlockSpec((TM, D), lambda i, t: (i, 0)), out_shape=(S, D), out_dtype=F32, acc_shape=(TM, D))
    d_w_qkv = _mm("qkv_bwd_w", u2, dqkv, kind="tn", grid=(nt, S // TM),
                  a_spec=pl.BlockSpec((TM, D), lambda t, k: (k, 0)), b_spec=pl.BlockSpec((None, TM, D), lambda t, k: (t, k, 0)),
                  o_spec=pl.BlockSpec((D, D), lambda t, k: (0, t)), out_shape=(D, nt * D), out_dtype=BF16, acc_shape=(D, D))
    shard_cols = lambda a: jnp.moveaxis(a.reshape(a.shape[0], N_CHIPS, a.shape[1] // N_CHIPS), 1, 0)
    ex.grads("od", {"od_w_qkv": shard_cols(d_w_qkv), "od_w_out": d_w_od_out.reshape(N_CHIPS, D // N_CHIPS, D)})
    dh2, dg_mix1 = _rms_bwd("rms_mix1_bwd", du2, h2, g_mix[1:2], dh3, carry=ex.carry("rms_mix1_bwd"))
    d_rel = _bias_table_grad("rel_bias_grad", dr0.reshape(C_H, TOEP_W))[:, :2 * REL_CLIP + 1]

    dgate, dup, act = _ffn_bwd_hidden("ffn0_bwd_h", dh2, gate0, up0, w["w_down0"])
    dh1, dg_ffn0, u1 = _ffn_bwd_input("ffn0_bwd_x", dh2, h1, g_ffn[0:1], dgate, dup, w["w_gate0"], w["w_up0"])
    d_wg0, d_wu0, d_wd0 = _ffn_wgrads("ffn0_dw", u1, dgate, dup, act, dh2)
    ex.grads("ffn0", {"w_gate0": d_wg0, "w_up0": d_wu0, "w_down0": d_wd0})

    d_oev, d_w_ev_out = _out_proj_bwd("ev_out_bwd", dh1, o_ev, w["ev_w_out"], ex)
    dqa, dka, dva = riding("mla_attn_bwd", _mla_bwd, qa, ka, va, o_a, lse, d_oev, 0)
    dqb, dkb, dvb = riding("sb_attn_bwd", _sb_bwd, proj, d_oev, MLA_H * MLA_V // LANES)
    dcq, dckv, dkr, d_w_uq, d_w_uk, d_w_uv, dg_cq, dg_ckv = _mla_prep_bwd(
        "mla_prep_bwd", dqa, dka, dva, proj, sm["ev_g_cq"], sm["ev_g_ckv"], w["w_uq"], w["w_uk"], w["w_uv"], cos_t, sin_t)
    dproj = jnp.concatenate([dcq, jnp.zeros((S, LANES), BF16), dckv, dqb, dkb, dvb, dkr], axis=1)
    d_w_in_p = _mm("proj_in_bwd_w", u0, dproj, kind="tn", grid=(1, S // TM),
                   a_spec=pl.BlockSpec((TM, D), lambda j, k: (k, 0)), b_spec=pl.BlockSpec((TM, P_IN), lambda j, k: (k, 0)),
                   o_spec=pl.BlockSpec((D, P_IN), lambda j, k: (0, 0)), out_shape=(D, P_IN), out_dtype=BF16,
                   acc_shape=(D, P_IN), carry=ex.carry("proj_in_bwd_w"))
    d_w_in = jnp.concatenate([d_w_in_p[:, 0:384], d_w_in_p[:, 512:768],
                              d_w_in_p[:, P_KR + KR_LANE:P_KR + KR_LANE + MLA_ROPE], d_w_in_p[:, 768:2304]], axis=1)
    d_w_uq_std = d_w_uq.reshape(Q_LORA, MLA_H, LANES)[:, :, :MLA_NOPE + MLA_ROPE].reshape(Q_LORA, -1)
    d_w_ukv = jnp.concatenate([d_w_uk.reshape(KV_LORA, MLA_H, LANES)[:, :, :MLA_NOPE],
                               d_w_uv.reshape(KV_LORA, MLA_H, MLA_V)], axis=2).reshape(KV_LORA, -1)
    ex.grads("ev", {"ev_w_in": shard_cols(d_w_in), "ev_w_uq": shard_cols(d_w_uq_std.astype(BF16)),
                    "ev_w_ukv": shard_cols(d_w_ukv.astype(BF16)),
                    "ev_w_out": d_w_ev_out.reshape(N_CHIPS, D // N_CHIPS, D)})
    du0 = _mm("proj_in_bwd_x", dproj, w["w_in"], kind="nt", grid=(S // TM, 1, 1),
              a_spec=pl.BlockSpec((TM, P_IN), lambda i, j, k: (i, 0)), b_spec=pl.BlockSpec((D, P_IN), lambda i, j, k: (0, 0)),
              o_spec=pl.BlockSpec((TM, D), lambda i, j, k: (i, 0)), out_shape=(S, D), out_dtype=F32, acc_shape=None,
              carry=ex.carry("proj_in_bwd_x"))
    grad_x, dg_mix0 = _rms_bwd("rms_mix0_bwd", du0, x, g_mix[0:1], dh1)
    small = {
        "ev_g_cq": dg_cq, "ev_g_ckv": dg_ckv, "od_rel_bias": d_rel.reshape(1, C_H, 2 * REL_CLIP + 1),
        "g_mix": jnp.concatenate([dg_mix0, dg_mix1], axis=0), "g_ffn": jnp.concatenate([dg_ffn0, dg_ffn1], axis=0),
        "g_final": dg_final.reshape(D),
    }
    return loss, grad_x, small


BIG = ("ev_w_in", "ev_w_uq", "ev_w_ukv", "ev_w_out", "od_w_qkv", "od_w_out", "w_gate", "w_up", "w_down")
SMALL = ("ev_g_cq", "ev_g_ckv", "od_rel_bias", "g_mix", "g_ffn", "g_final")
WEIGHTS = ("ev_w_in", "ev_g_cq", "ev_w_uq", "ev_g_ckv", "ev_w_ukv", "ev_w_out", "od_w_qkv", "od_rel_bias", "od_w_out",
           "g_mix", "g_ffn", "w_gate", "w_up", "w_down", "g_final")
GRAD_PARTS = (("ev_w_in", "ev_w_in", 0), ("ev_w_uq", "ev_w_uq", 0), ("ev_w_ukv", "ev_w_ukv", 0),
              ("ev_w_out", "ev_w_out", 0), ("od_w_qkv", "od_w_qkv", 0), ("od_w_out", "od_w_out", 0),
              ("w_gate0", "w_gate", 0), ("w_gate1", "w_gate", 1), ("w_up0", "w_up", 0), ("w_up1", "w_up", 1),
              ("w_down0", "w_down", 0), ("w_down1", "w_down", 1))
PART_OF = {part: (param, layer) for part, param, layer in GRAD_PARTS}
SMALL_ROWS = 112
SMALL_SIZE = 384 + 256 + 16 * 513 + 2 * 1024 + 2 * 1024 + 1024
TRANSPOSED = ("w_gate", "w_up")


def _row_tile(rows, cap=512):
    for t in range(min(rows, cap), 0, -1):
        if rows % t == 0 and t % 16 == 0:
            return t
    return rows


def _cast_into_slot(name, w, layer, pos):
    _, rows, cols = w.shape
    tr = _row_tile(rows)

    def body(pos_ref, w_ref, o_ref):
        o_ref[...] = w_ref[...].astype(BF16)

    return pl.pallas_call(
        body, name=name,
        grid_spec=pltpu.PrefetchScalarGridSpec(
            num_scalar_prefetch=1, grid=(rows // tr,),
            in_specs=[pl.BlockSpec((None, tr, cols), lambda i, p: (layer, i, 0))],
            out_specs=pl.BlockSpec((None, tr, cols), lambda i, p: (p[0], i, 0))),
        out_shape=jax.ShapeDtypeStruct((N_CHIPS, rows, cols), BF16), compiler_params=_params("arbitrary"))(pos, w)


def _pair_sum(name, part, theirs, pos):
    _, half, cols = theirs.shape
    tr = _row_tile(half)
    nb = half // tr

    def body(pos_ref, a_ref, b_ref, o_ref):
        o_ref[...] = (a_ref[...].astype(F32) + b_ref[...].astype(F32)).astype(BF16)

    return pl.pallas_call(
        body, name=name,
        grid_spec=pltpu.PrefetchScalarGridSpec(
            num_scalar_prefetch=1, grid=(N_CHIPS, nb),
            in_specs=[pl.BlockSpec((None, tr, cols), lambda s, i, p: (s, p[1] * nb + i, 0)),
                      pl.BlockSpec((None, tr, cols), lambda s, i, p: (s, i, 0))],
            out_specs=pl.BlockSpec((None, tr, cols), lambda s, i, p: (s, i, 0))),
        out_shape=jax.ShapeDtypeStruct(theirs.shape, BF16),
        compiler_params=_params("arbitrary", "arbitrary"))(pos, part, theirs)


def _chip_sum(name, sums, got, pos, layer, full_shape, full=None):
    _, half, cols = sums.shape
    tr = _row_tile(half)
    nb = half // tr

    def body(pos_ref, s_ref, g_ref, *rest):
        out_ref = rest[-1]
        out_ref[...] = ((s_ref[...].astype(F32) + g_ref[0].astype(F32)) + g_ref[1].astype(F32)) + g_ref[2].astype(F32)

    in_specs = [pl.BlockSpec((None, tr, cols), lambda i, p: (p[0], i, 0)),
                pl.BlockSpec((3, tr, cols), lambda i, p: (0, i, 0))]
    args = [pos, sums, got]
    if full is not None:
        in_specs.append(ANY)
        args.append(full)
    return pl.pallas_call(
        body, name=name,
        grid_spec=pltpu.PrefetchScalarGridSpec(
            num_scalar_prefetch=1, grid=(nb,), in_specs=in_specs,
            out_specs=pl.BlockSpec((None, tr, cols), lambda i, p: (layer, p[1] * nb + i, 0))),
        out_shape=jax.ShapeDtypeStruct(full_shape, F32),
        input_output_aliases={3: 0} if full is not None else {},
        compiler_params=_params("arbitrary"))(*args)


def _all_reduce_small(name, packed):
    n_dev = 8

    def body(p_ref, o_ref, slots, send_sem, recv_sem):
        x, y, c, _ = _position()
        me = 4 * x + 2 * y + c

        def peer(k):
            return (1 - x if k & 4 else x, 1 - y if k & 2 else y, 1 - c if k & 1 else c)

        def logical(k):
            px, py, pc = peer(k)
            return 4 * px + 2 * py + pc

        slots[me] = p_ref[...]
        sends = [pltpu.make_async_remote_copy(
            src_ref=p_ref, dst_ref=slots.at[me], send_sem=send_sem.at[k], recv_sem=recv_sem.at[k],
            device_id=peer(k), device_id_type=MESH) for k in range(1, n_dev)]
        for cp in sends:
            cp.start()
        for k in range(1, n_dev):
            pltpu.make_async_remote_copy(
                src_ref=p_ref, dst_ref=slots.at[logical(k)], send_sem=send_sem.at[k], recv_sem=recv_sem.at[k],
                device_id=peer(k), device_id_type=MESH).wait_recv()
        for cp in sends:
            cp.wait_send()
        total = slots[0]
        for d in range(1, n_dev):
            total = total + slots[d]
        o_ref[...] = total

    vm = pl.BlockSpec(memory_space=pltpu.VMEM)
    return pl.pallas_call(
        body, name=name, in_specs=[vm], out_specs=vm, out_shape=jax.ShapeDtypeStruct(packed.shape, F32),
        scratch_shapes=[pltpu.VMEM((n_dev,) + packed.shape, F32), pltpu.SemaphoreType.DMA((n_dev,)),
                        pltpu.SemaphoreType.DMA((n_dev,))],
    )(packed)


def _adamw(name, w, g, m, v):
    rows, cols = w.shape
    tr = _row_tile(rows)

    def body(w_ref, g_ref, m_ref, v_ref, d_ref, mo_ref, vo_ref):
        gv = g_ref[...]
        m_new = ADAM_B1 * m_ref[...] + (1.0 - ADAM_B1) * gv
        v_new = ADAM_B2 * v_ref[...] + (1.0 - ADAM_B2) * (gv * gv)
        m_hat = m_new / (1.0 - ADAM_B1 ** ADAM_STEP)
        v_hat = v_new / (1.0 - ADAM_B2 ** ADAM_STEP)
        d_ref[...] = -ADAM_LR * (m_hat / (jnp.sqrt(v_hat) + ADAM_EPS) + ADAM_WD * w_ref[...])
        mo_ref[...] = m_new
        vo_ref[...] = v_new

    spec = pl.BlockSpec((tr, cols), lambda i: (i, 0))
    shape = jax.ShapeDtypeStruct((rows, cols), F32)
    return pl.pallas_call(body, name=name, grid=(rows // tr,), in_specs=[spec] * 4, out_specs=[spec] * 3,
                          out_shape=[shape] * 3, compiler_params=_params("parallel"))(w, g, m, v)


def _pack_small(tree, extra=None):
    pieces = [tree[n].reshape(-1).astype(F32) for n in SMALL]
    if extra is not None:
        pieces.append(extra.reshape(1).astype(F32))
    flat = jnp.concatenate(pieces)
    return jnp.pad(flat, (0, SMALL_ROWS * LANES - flat.shape[0])).reshape(SMALL_ROWS, LANES)


def _unpack_small(packed, like):
    flat = packed.reshape(-1)
    out, off = {}, 0
    for n in SMALL:
        size = int(np.prod(like[n].shape))
        out[n] = flat[off:off + size].reshape(like[n].shape)
        off += size
    return out


def kernel(x, ev_w_in, ev_g_cq, ev_w_uq, ev_g_ckv, ev_w_ukv, ev_w_out, od_w_qkv, od_rel_bias, od_w_out, g_mix, g_ffn, w_gate, w_up, w_down, g_final, loss_target, m_ev_w_in, m_ev_g_cq, m_ev_w_uq, m_ev_g_ckv, m_ev_w_ukv, m_ev_w_out, m_od_w_qkv, m_od_rel_bias, m_od_w_out, m_g_mix, m_g_ffn, m_w_gate, m_w_up, m_w_down, m_g_final, v_ev_w_in, v_ev_g_cq, v_ev_w_uq, v_ev_g_ckv, v_ev_w_ukv, v_ev_w_out, v_od_w_qkv, v_od_rel_bias, v_od_w_out, v_g_mix, v_g_ffn, v_w_gate, v_w_up, v_w_down, v_g_final):
    w = dict(ev_w_in=ev_w_in, ev_g_cq=ev_g_cq, ev_w_uq=ev_w_uq, ev_g_ckv=ev_g_ckv, ev_w_ukv=ev_w_ukv, ev_w_out=ev_w_out,
             od_w_qkv=od_w_qkv, od_rel_bias=od_rel_bias, od_w_out=od_w_out, g_mix=g_mix, g_ffn=g_ffn, w_gate=w_gate,
             w_up=w_up, w_down=w_down, g_final=g_final)
    m = dict(ev_w_in=m_ev_w_in, ev_g_cq=m_ev_g_cq, ev_w_uq=m_ev_w_uq, ev_g_ckv=m_ev_g_ckv, ev_w_ukv=m_ev_w_ukv,
             ev_w_out=m_ev_w_out, od_w_qkv=m_od_w_qkv, od_rel_bias=m_od_rel_bias, od_w_out=m_od_w_out, g_mix=m_g_mix,
             g_ffn=m_g_ffn, w_gate=m_w_gate, w_up=m_w_up, w_down=m_w_down, g_final=m_g_final)
    v = dict(ev_w_in=v_ev_w_in, ev_g_cq=v_ev_g_cq, ev_w_uq=v_ev_w_uq, ev_g_ckv=v_ev_g_ckv, ev_w_ukv=v_ev_w_ukv,
             ev_w_out=v_ev_w_out, od_w_qkv=v_od_w_qkv, od_rel_bias=v_od_rel_bias, od_w_out=v_od_w_out, g_mix=v_g_mix,
             g_ffn=v_g_ffn, w_gate=v_w_gate, w_up=v_w_up, w_down=v_w_down, g_final=v_g_final)
    flat2d = lambda a: a.reshape(-1, a.shape[-1])
    for tree in (w, m, v):
        for n in TRANSPOSED:
            tree[n] = jnp.swapaxes(tree[n], 1, 2)

    pos = jnp.stack([2 * lax.axis_index("x") + lax.axis_index("y"), lax.axis_index("c")]).astype(jnp.int32)

    slots = {part: _cast_into_slot("cast_" + part, w[n], layer, pos) for part, n, layer in GRAD_PARTS}
    ex = _Exchanges(slots, pos, {n: w[n].shape for n in BIG})

    loss_local, grad_x, small = _local_step(x[0], loss_target[0], {n: w[n] for n in SMALL}, ex)

    grads = ex.finish()
    small_sum = _all_reduce_small("small_sum", _pack_small(small, loss_local[0, 0]))
    grads.update(_unpack_small(small_sum, w))

    delta, new_m, new_v = {}, {}, {}
    for n in BIG:
        d_, m_, v_ = _adamw("adamw_" + n, flat2d(w[n]), flat2d(grads[n]), flat2d(m[n]), flat2d(v[n]))
        delta[n], new_m[n], new_v[n] = d_.reshape(w[n].shape), m_.reshape(w[n].shape), v_.reshape(w[n].shape)
    d_, m_, v_ = _adamw("adamw_small", _pack_small(w), small_sum, _pack_small(m), _pack_small(v))
    delta.update(_unpack_small(d_, w))
    new_m.update(_unpack_small(m_, w))
    new_v.update(_unpack_small(v_, w))
    for tree in (grads, delta, new_m, new_v):
        for n in TRANSPOSED:
            tree[n] = jnp.swapaxes(tree[n], 1, 2)

    loss = small_sum.reshape(-1)[SMALL_SIZE]
    return (loss, grad_x[None], *[grads[n] for n in WEIGHTS], *[delta[n] for n in WEIGHTS],
            *[new_m[n] for n in WEIGHTS], *[new_v[n] for n in WEIGHTS])
```

```python
import functools

import jax
import jax.numpy as jnp
import numpy as np
from jax import lax
from jax.experimental import pallas as pl
from jax.experimental.pallas import tpu as pltpu

F32 = jnp.float32
BF16 = jnp.bfloat16

S = 2048
D = 1024
CHUNK = 64
MLA_H, MLA_NOPE, MLA_ROPE, MLA_V = 8, 64, 32, 64
Q_LORA, KV_LORA = 384, 256
ROPE_THETA = 10000.0
SB_H, SB_DIM = 8, 64
C_H, C_DIM = 16, 64
LEFT_CHUNKS = 8
REL_CLIP = 256
D_FF = 2816
EVEN_IN = 2208
RMS_EPS = 1e-6
ADAM_LR, ADAM_B1, ADAM_B2, ADAM_EPS, ADAM_WD, ADAM_STEP = 0.001, 0.9, 0.999, 1e-08, 0.01, 10

N_CHIPS = 4
FF_SHARD = D_FF // N_CHIPS
SCALE_A = (MLA_NOPE + MLA_ROPE) ** -0.5
SCALE_B = SB_DIM ** -0.5
SCALE_C = C_DIM ** -0.5
NEG = -1e30

LANES = 128
VMEM_LIMIT_BYTES = 56 * 1024 * 1024
FFN_BWD_VMEM_BYTES = 60 * 1024 * 1024
FFN_BWD_ROWS = 256
TM = 512
QB = 512
BQ = 256

P_CQ, P_CKV, P_QB, P_KB, P_VB, P_KR = 0, 512, 768, 1280, 1792, 2304
P_IN = 2432
KR_LANE = 64
BAND_W = BQ + LEFT_CHUNKS * CHUNK
BAND_PAD = 512
TOEP_W = 1024


def _params(*sem):
    return pltpu.CompilerParams(dimension_semantics=sem, vmem_limit_bytes=VMEM_LIMIT_BYTES)


MESH = pl.DeviceIdType.MESH
ANY = pl.BlockSpec(memory_space=pl.ANY)


def _position():
    x, y, c = lax.axis_index("x"), lax.axis_index("y"), lax.axis_index("c")
    other_chips = [(1 - x, y), (x, 1 - y), (1 - x, 1 - y)]
    return x, y, c, other_chips


def _half_rows(c, half):
    return pl.ds(pl.multiple_of(c * half, 16), half)


def _remote(ref_src, ref_dst, send, recv, k, device):
    return pltpu.make_async_remote_copy(src_ref=ref_src, dst_ref=ref_dst, send_sem=send.at[k], recv_sem=recv.at[k],
                                        device_id=device, device_id_type=MESH)


class _Carry:
    def __init__(self):
        self.operands, self.aliased, self.fresh = [], [], []
        self.n_sems = 0
        self.starts, self.finishes, self.on_done = [], [], []

    def operand(self, arr, aliased):
        for i, a in enumerate(self.operands):
            if a is arr:
                return i
        self.operands.append(arr)
        self.aliased.append(aliased)
        return len(self.operands) - 1

    def result(self, shape, dtype):
        self.fresh.append(jax.ShapeDtypeStruct(shape, dtype))
        return len(self.fresh) - 1

    def sems(self, k):
        base = self.n_sems
        self.n_sems += k
        return base

    def done(self, results):
        aliased, fresh = results
        for f in self.on_done:
            f(aliased, fresh)


def _carrier_call(body, *, name, grid, in_specs, out_specs, out_shape, args, sem, scratch_shapes=(), carry=None):
    in_specs, out_specs, out_shape, scratch = list(in_specs), list(out_specs), list(out_shape), list(scratch_shapes)
    if carry is None:
        res = pl.pallas_call(body, name=name, grid=grid, in_specs=in_specs, out_specs=out_specs, out_shape=out_shape,
                             scratch_shapes=scratch, compiler_params=_params(*sem))(*args)
        return list(res), None
    ops = carry.operands
    alias_idx = [i for i, a in enumerate(carry.aliased) if a]
    c_shapes = [jax.ShapeDtypeStruct(ops[i].shape, ops[i].dtype) for i in alias_idx] + carry.fresh
    n_in, n_out, n_scr = len(args), len(out_shape), len(scratch)

    def wrapped(*refs):
        ins, c_ins = refs[:n_in], refs[n_in:n_in + len(ops)]
        o0 = n_in + len(ops)
        outs, c_outs = refs[o0:o0 + n_out], refs[o0 + n_out:o0 + n_out + len(c_shapes)]
        s0 = o0 + n_out + len(c_shapes)
        scr, send, recv = refs[s0:s0 + n_scr], refs[s0 + n_scr], refs[s0 + n_scr + 1]
        use = list(c_ins)
        for k, i in enumerate(alias_idx):
            use[i] = c_outs[k]
        fresh = c_outs[len(alias_idx):]

        def run(steps):
            for step in steps:
                step(use, fresh, send, recv)

        if not grid:
            run(carry.starts)
            if body is not None:
                body(*ins, *outs, *scr)
            run(carry.finishes)
            return
        ids = [pl.program_id(a) for a in range(len(grid))]
        first = functools.reduce(jnp.logical_and, [i == 0 for i in ids])
        last = functools.reduce(jnp.logical_and, [i == g - 1 for i, g in zip(ids, grid)])

        @pl.when(first)
        def _():
            run(carry.starts)

        body(*ins, *outs, *scr)

        @pl.when(last)
        def _():
            run(carry.finishes)

    res = pl.pallas_call(
        wrapped, name=name, grid=grid, in_specs=in_specs + [ANY] * len(ops), out_specs=out_specs + [ANY] * len(c_shapes),
        out_shape=out_shape + c_shapes,
        scratch_shapes=scratch + [pltpu.SemaphoreType.DMA((carry.n_sems,)), pltpu.SemaphoreType.DMA((carry.n_sems,))],
        input_output_aliases={n_in + i: n_out + k for k, i in enumerate(alias_idx)},
        compiler_params=_params(*(("arbitrary",) * len(grid))),
    )(*args, *ops)
    res = list(res)
    c_res = res[n_out:]
    return res[:n_out], ({i: c_res[k] for k, i in enumerate(alias_idx)}, c_res[len(alias_idx):])


_DIMS = {"nn": (((1,), (0,)), ((), ())), "nt": (((1,), (1,)), ((), ())), "tn": (((0,), (0,)), ((), ()))}


def _dot(a, b, kind="nn"):
    return lax.dot_general(a, b, _DIMS[kind], preferred_element_type=F32)


def _iota(shape, dim):
    return lax.broadcasted_iota(jnp.int32, shape, dim)


def _sigmoid(x):
    return 1.0 / (1.0 + jnp.exp(-x))


def _softplus(x):
    return jnp.maximum(x, 0.0) + jnp.log(1.0 + jnp.exp(-jnp.abs(x)))


def _split_dot(x, tri):
    hi = x.astype(BF16)
    lo = (x - hi.astype(F32)).astype(BF16)
    return _dot(hi, tri) + _dot(lo, tri)


def _mm(name, a, b, *, kind, grid, a_spec, b_spec, o_spec, out_shape, out_dtype, acc_shape, resid=None, r_spec=None,
        carry=None):
    nk = grid[-1]
    has_r = resid is not None

    def body(*refs):
        a_ref, b_ref = refs[0], refs[1]
        r_ref = refs[2] if has_r else None
        o_ref = refs[2 + has_r]
        part = _dot(a_ref[...].astype(BF16), b_ref[...].astype(BF16), kind)

        def finish(total):
            if has_r:
                total = total + r_ref[...].astype(F32)
            o_ref[...] = total.astype(out_dtype)

        if nk == 1:
            finish(part)
        else:
            acc_ref = refs[3 + has_r]
            k = pl.program_id(len(grid) - 1)

            @pl.when(k == 0)
            def _():
                acc_ref[...] = part

            @pl.when(k > 0)
            def _():
                acc_ref[...] += part

            @pl.when(k == nk - 1)
            def _():
                finish(acc_ref[...])

    in_specs = [a_spec, b_spec] + ([r_spec] if has_r else [])
    args = (a, b) + ((resid,) if has_r else ())
    sem = ("parallel",) * (len(grid) - 1) + ("arbitrary",)
    res, copies = _carrier_call(
        body, name=name, grid=grid, in_specs=in_specs, out_specs=[o_spec],
        out_shape=[jax.ShapeDtypeStruct(out_shape, out_dtype)],
        scratch_shapes=[pltpu.VMEM(acc_shape, F32)] if nk > 1 else [], args=args, sem=sem, carry=carry)
    if carry is not None:
        carry.done(copies)
    return res[0]


def _rms_fwd(name, x, g, col_block=0):
    c = g.shape[1]

    def body(x_ref, g_ref, u_ref):
        xv = x_ref[...]
        r = lax.rsqrt(jnp.mean(xv * xv, axis=-1, keepdims=True) + RMS_EPS)
        u_ref[...] = (xv * r * g_ref[...]).astype(BF16)

    return pl.pallas_call(
        body, name=name, grid=(S // TM,),
        in_specs=[pl.BlockSpec((TM, c), lambda i: (i, col_block)), pl.BlockSpec((1, c), lambda i: (0, 0))],
        out_specs=pl.BlockSpec((TM, c), lambda i: (i, 0)),
        out_shape=jax.ShapeDtypeStruct((S, c), BF16),
        compiler_params=_params("parallel"),
    )(x, g)


def _rms_bwd(name, dy, x, g, resid, carry=None):
    def body(dy_ref, x_ref, g_ref, r_ref, dx_ref, dg_ref):
        i = pl.program_id(0)
        xv = x_ref[...]
        r = lax.rsqrt(jnp.mean(xv * xv, axis=-1, keepdims=True) + RMS_EPS)
        xh = xv * r
        dyv = dy_ref[...]
        dxh = dyv * g_ref[...]
        dx_ref[...] = r_ref[...] + r * (dxh - xh * jnp.mean(dxh * xh, axis=-1, keepdims=True))
        part = jnp.sum(dyv * xh, axis=0, keepdims=True)

        @pl.when(i == 0)
        def _():
            dg_ref[...] = part

        @pl.when(i > 0)
        def _():
            dg_ref[...] += part

    row = pl.BlockSpec((TM, D), lambda i: (i, 0))
    vec = pl.BlockSpec((1, D), lambda i: (0, 0))
    res, copies = _carrier_call(
        body, name=name, grid=(S // TM,), in_specs=[row, row, vec, row], out_specs=[row, vec],
        out_shape=[jax.ShapeDtypeStruct((S, D), F32), jax.ShapeDtypeStruct((1, D), F32)],
        args=(dy, x, g, resid), sem=("arbitrary",), carry=carry)
    if carry is not None:
        carry.done(copies)
    return res


def _loss_bwd(name, h, g, tgt):
    def body(h_ref, g_ref, t_ref, loss_ref, dh_ref, dg_ref):
        i = pl.program_id(0)
        xv = h_ref[...]
        gv = g_ref[...]
        r = lax.rsqrt(jnp.mean(xv * xv, axis=-1, keepdims=True) + RMS_EPS)
        xh = xv * r
        diff = xh * gv - t_ref[...]
        part_loss = 0.5 * jnp.sum(jnp.sum(diff * diff, axis=-1, keepdims=True) * (1.0 / D), axis=0, keepdims=True)
        dy = diff * (1.0 / D)
        dxh = dy * gv
        dh_ref[...] = r * (dxh - xh * jnp.mean(dxh * xh, axis=-1, keepdims=True))
        part_g = jnp.sum(dy * xh, axis=0, keepdims=True)

        @pl.when(i == 0)
        def _():
            dg_ref[...] = part_g
            loss_ref[...] = jnp.broadcast_to(part_loss, (1, LANES))

        @pl.when(i > 0)
        def _():
            dg_ref[...] += part_g
            loss_ref[...] += jnp.broadcast_to(part_loss, (1, LANES))

    row = pl.BlockSpec((TM, D), lambda i: (i, 0))
    vec = pl.BlockSpec((1, D), lambda i: (0, 0))
    return pl.pallas_call(
        body, name=name, grid=(S // TM,), in_specs=[row, vec, row],
        out_specs=[pl.BlockSpec((1, LANES), lambda i: (0, 0)), row, vec],
        out_shape=[jax.ShapeDtypeStruct((1, LANES), F32), jax.ShapeDtypeStruct((S, D), F32),
                   jax.ShapeDtypeStruct((1, D), F32)],
        compiler_params=_params("arbitrary"),
    )(h, g, tgt)


def _ffn_fwd(name, h, g, wg, wu, wd, carry=None):
    def body(h_ref, g_ref, wg_ref, wu_ref, wd_ref, o_ref, gate_ref, up_ref, u_scr):
        s = pl.program_id(1)

        @pl.when(s == 0)
        def _():
            xv = h_ref[...]
            r = lax.rsqrt(jnp.mean(xv * xv, axis=-1, keepdims=True) + RMS_EPS)
            u_scr[...] = (xv * r * g_ref[...]).astype(BF16)
            o_ref[...] = xv

        u = u_scr[...]
        gate = _dot(u, wg_ref[...], "nt")
        up = _dot(u, wu_ref[...], "nt")
        act = gate * _sigmoid(gate) * up
        o_ref[...] += _dot(act.astype(BF16), wd_ref[...])
        gate_ref[...] = gate.astype(BF16)
        up_ref[...] = up.astype(BF16)

    row = pl.BlockSpec((TM, D), lambda i, s: (i, 0))
    hid = pl.BlockSpec((None, TM, FF_SHARD), lambda i, s: (s, i, 0))
    return _carrier_call(
        body, name=name, grid=(S // TM, N_CHIPS),
        in_specs=[row, pl.BlockSpec((1, D), lambda i, s: (0, 0))]
        + [pl.BlockSpec((None, FF_SHARD, D), lambda i, s: (s, 0, 0))] * 3,
        out_specs=[row, hid, hid],
        out_shape=[jax.ShapeDtypeStruct((S, D), F32), jax.ShapeDtypeStruct((N_CHIPS, S, FF_SHARD), BF16),
                   jax.ShapeDtypeStruct((N_CHIPS, S, FF_SHARD), BF16)],
        scratch_shapes=[pltpu.VMEM((TM, D), BF16)], args=(h, g, wg, wu, wd), sem=("parallel", "arbitrary"), carry=carry)


def _ffn_bwd(name, dh, h, g, gate, up, wg, wu, wd):
    def body(dh_ref, h_ref, g_ref, gate_ref, up_ref, wg_ref, wu_ref, wd_ref,
             dhin_ref, dg_ref, u_ref, dgate_ref, dup_ref, act_ref, dhb_scr, du_scr):
        i = pl.program_id(0)
        s = pl.program_id(1)

        @pl.when(s == 0)
        def _():
            xv = h_ref[...]
            r = lax.rsqrt(jnp.mean(xv * xv, axis=-1, keepdims=True) + RMS_EPS)
            u_ref[...] = (xv * r * g_ref[...]).astype(BF16)
            dhb_scr[...] = dh_ref[...].astype(BF16)
            du_scr[...] = jnp.zeros_like(du_scr)

        dact = _dot(dhb_scr[...], wd_ref[...], "nt")
        gv = gate_ref[...].astype(F32)
        uv = up_ref[...].astype(F32)
        sig = _sigmoid(gv)
        sil = gv * sig
        dup = dact * sil
        dgate = dact * uv * (sig * (1.0 + gv * (1.0 - sig)))
        dgb = dgate.astype(BF16)
        dub = dup.astype(BF16)
        act_ref[...] = (sil * uv).astype(BF16)
        dgate_ref[...] = dgb
        dup_ref[...] = dub
        du_scr[...] += _dot(dgb, wg_ref[...]) + _dot(dub, wu_ref[...])

        @pl.when(s == N_CHIPS - 1)
        def _():
            xv = h_ref[...]
            r = lax.rsqrt(jnp.mean(xv * xv, axis=-1, keepdims=True) + RMS_EPS)
            xh = xv * r
            du = du_scr[...]
            dxh = du * g_ref[...]
            dhin_ref[...] = dh_ref[...] + r * (dxh - xh * jnp.mean(dxh * xh, axis=-1, keepdims=True))
            part = jnp.sum(du * xh, axis=0, keepdims=True)

            @pl.when(i == 0)
            def _():
                dg_ref[...] = part

            @pl.when(i > 0)
            def _():
                dg_ref[...] += part

    row = pl.BlockSpec((TM, D), lambda i, s: (i, 0))
    vec = pl.BlockSpec((1, D), lambda i, s: (0, 0))
    hid = pl.BlockSpec((None, TM, FF_SHARD), lambda i, s: (s, i, 0))
    hid_shape = jax.ShapeDtypeStruct((N_CHIPS, S, FF_SHARD), BF16)
    return pl.pallas_call(
        body, name=name, grid=(S // TM, N_CHIPS),
        in_specs=[row, row, vec, hid, hid] + [pl.BlockSpec((None, FF_SHARD, D), lambda i, s: (s, 0, 0))] * 3,
        out_specs=[row, vec, row, hid, hid, hid],
        out_shape=[jax.ShapeDtypeStruct((S, D), F32), jax.ShapeDtypeStruct((1, D), F32),
                   jax.ShapeDtypeStruct((S, D), BF16), hid_shape, hid_shape, hid_shape],
        scratch_shapes=[pltpu.VMEM((TM, D), BF16), pltpu.VMEM((TM, D), F32)],
        compiler_params=_params("arbitrary", "arbitrary"),
    )(dh, h, g, gate, up, wg, wu, wd)


def _ffn_bwd_all(name, dh, h, g, gate, up, wg, wu, wd):
    tm = FFN_BWD_ROWS
    n_i = S // tm
    last_s = N_CHIPS - 1

    def body(dh_ref, h_ref, g_ref, gate_ref, up_ref, wg_ref, wu_ref, wd_ref,
             dhin_ref, dg_ref, dwg_ref, dwu_ref, dwd_ref, u_scr, dhb_scr, du_scr, acc_g, acc_u, acc_d):
        s = pl.program_id(0)
        i = pl.program_id(1)
        rows = pl.ds(pl.multiple_of(i * tm, tm), tm)

        @pl.when(s == 0)
        def _():
            xv = h_ref[...]
            r = lax.rsqrt(jnp.mean(xv * xv, axis=-1, keepdims=True) + RMS_EPS)
            u_scr[rows, :] = (xv * r * g_ref[...]).astype(BF16)
            dhb_scr[rows, :] = dh_ref[...].astype(BF16)

        u = u_scr[rows, :]
        dhb = dhb_scr[rows, :]
        dact = _dot(dhb, wd_ref[...], "nt")
        gv = gate_ref[...].astype(F32)
        uv = up_ref[...].astype(F32)
        sig = _sigmoid(gv)
        sil = gv * sig
        dgb = (dact * uv * (sig * (1.0 + gv * (1.0 - sig)))).astype(BF16)
        dub = (dact * sil).astype(BF16)
        actb = (sil * uv).astype(BF16)
        du_part = _dot(dgb, wg_ref[...]) + _dot(dub, wu_ref[...])

        @pl.when(s == 0)
        def _():
            du_scr[rows, :] = du_part

        @pl.when(s > 0)
        def _():
            du_scr[rows, :] += du_part

        parts = (_dot(dgb, u, "tn"), _dot(dub, u, "tn"), _dot(actb, dhb, "tn"))
        accs = (acc_g, acc_u, acc_d)

        @pl.when(i == 0)
        def _():
            for acc, part in zip(accs, parts):
                acc[...] = part

        @pl.when(i > 0)
        def _():
            for acc, part in zip(accs, parts):
                acc[...] += part

        @pl.when(i == n_i - 1)
        def _():
            for out, acc in zip((dwg_ref, dwu_ref, dwd_ref), accs):
                out[...] = acc[...].astype(BF16)

        @pl.when(s == last_s)
        def _():
            xv = h_ref[...]
            r = lax.rsqrt(jnp.mean(xv * xv, axis=-1, keepdims=True) + RMS_EPS)
            xh = xv * r
            du = du_scr[rows, :]
            dxh = du * g_ref[...]
            dhin_ref[...] = dh_ref[...] + r * (dxh - xh * jnp.mean(dxh * xh, axis=-1, keepdims=True))
            part = jnp.sum(du * xh, axis=0, keepdims=True)

            @pl.when(i == 0)
            def _():
                dg_ref[...] = part

            @pl.when(i > 0)
            def _():
                dg_ref[...] += part

    edge = lambda s, i: (jnp.where((s == 0) | (s == last_s), i, n_i - 1), 0)
    row_in = pl.BlockSpec((tm, D), edge)
    row_out = pl.BlockSpec((tm, D), lambda s, i: (jnp.where(s == last_s, i, 0), 0))
    vec = pl.BlockSpec((1, D), lambda s, i: (0, 0))
    hid = pl.BlockSpec((None, tm, FF_SHARD), lambda s, i: (s, i, 0))
    wq = pl.BlockSpec((None, FF_SHARD, D), lambda s, i: (s, 0, 0), pipeline_mode=pl.Buffered(1))
    w_shape = jax.ShapeDtypeStruct((N_CHIPS, FF_SHARD, D), BF16)
    return pl.pallas_call(
        body, name=name, grid=(N_CHIPS, n_i),
        in_specs=[row_in, row_in, vec, hid, hid, wq, wq, wq],
        out_specs=[row_out, vec, wq, wq, wq],
        out_shape=[jax.ShapeDtypeStruct((S, D), F32), jax.ShapeDtypeStruct((1, D), F32), w_shape, w_shape, w_shape],
        scratch_shapes=[pltpu.VMEM((S, D), BF16), pltpu.VMEM((S, D), BF16), pltpu.VMEM((S, D), F32)]
        + [pltpu.VMEM((FF_SHARD, D), F32)] * 3,
        compiler_params=pltpu.CompilerParams(dimension_semantics=("arbitrary", "arbitrary"),
                                             vmem_limit_bytes=FFN_BWD_VMEM_BYTES),
    )(dh, h, g, gate, up, wg, wu, wd)


def _ffn_wgrads(name, u, dgate, dup, act, dh):
    nk = S // TM

    def body(u_ref, dh_ref, dgate_ref, dup_ref, act_ref, dg_ref, du_ref, dd_ref, acc_g, acc_u, acc_d):
        k = pl.program_id(1)
        u = u_ref[...]
        parts = (_dot(dgate_ref[...], u, "tn"), _dot(dup_ref[...], u, "tn"),
                 _dot(act_ref[...], dh_ref[...].astype(BF16), "tn"))
        accs = (acc_g, acc_u, acc_d)

        @pl.when(k == 0)
        def _():
            for acc, part in zip(accs, parts):
                acc[...] = part

        @pl.when(k > 0)
        def _():
            for acc, part in zip(accs, parts):
                acc[...] += part

        @pl.when(k == nk - 1)
        def _():
            for out, acc in zip((dg_ref, du_ref, dd_ref), accs):
                out[...] = acc[...].astype(BF16)

    tok = pl.BlockSpec((TM, D), lambda s, k: (k, 0))
    hid = pl.BlockSpec((None, TM, FF_SHARD), lambda s, k: (s, k, 0))
    out = pl.BlockSpec((None, FF_SHARD, D), lambda s, k: (s, 0, 0))
    shape = jax.ShapeDtypeStruct((N_CHIPS, FF_SHARD, D), BF16)
    return pl.pallas_call(
        body, name=name, grid=(N_CHIPS, nk), in_specs=[tok, tok, hid, hid, hid], out_specs=[out, out, out],
        out_shape=[shape, shape, shape], scratch_shapes=[pltpu.VMEM((FF_SHARD, D), F32)] * 3,
        compiler_params=_params("parallel", "arbitrary"))(u, dh, dgate, dup, act)


def _rope_tables():
    pos = jnp.arange(S, dtype=F32)
    inv = ROPE_THETA ** (-jnp.arange(0, MLA_ROPE, 2, dtype=F32) / MLA_ROPE)
    ang = pos[:, None] * inv[None, :]
    half = MLA_ROPE // 2
    cos = jnp.cos(ang)
    sin = jnp.sin(ang)
    one = jnp.ones((S, KR_LANE), F32)
    zero = jnp.zeros((S, KR_LANE), F32)
    tail_one = jnp.ones((S, LANES - KR_LANE - MLA_ROPE), F32)
    tail_zero = jnp.zeros((S, LANES - KR_LANE - MLA_ROPE), F32)
    cos_t = jnp.concatenate([one, cos, cos, tail_one], axis=1)
    sin_t = jnp.concatenate([zero, -sin, sin, tail_zero], axis=1)
    assert cos_t.shape == (S, LANES) and half * 2 == MLA_ROPE
    return cos_t, sin_t


def _rope(x, cos_t, sin_t, sign):
    n = x.shape[1] // LANES
    half = MLA_ROPE // 2
    lane = _iota(x.shape, 1) & (LANES - 1)
    first = (lane >= KR_LANE) & (lane < KR_LANE + half)
    swapped = jnp.where(first, pltpu.roll(x, x.shape[1] - half, 1), pltpu.roll(x, half, 1))
    c = jnp.tile(cos_t, (1, n)) if n > 1 else cos_t
    s = jnp.tile(sin_t, (1, n)) if n > 1 else sin_t
    return x * c + swapped * (s * sign)


def _mla_prep_fwd(name, proj, g_cq, g_ckv, w_uq, w_uk, w_uv, cos_t, sin_t):
    nh = MLA_H * LANES

    def body(cq_ref, ckv_ref, kr_ref, gq_ref, gkv_ref, wq_ref, wk_ref, wv_ref, cos_ref, sin_ref,
             qa_ref, ka_ref, va_ref):
        cos_v, sin_v = cos_ref[...], sin_ref[...]
        cq = cq_ref[...]
        r = lax.rsqrt(jnp.mean(cq * cq, axis=-1, keepdims=True) + RMS_EPS)
        cqn = (cq * r * gq_ref[...]).astype(BF16)
        qa_ref[...] = _rope(_dot(cqn, wq_ref[...]), cos_v, sin_v, 1.0).astype(BF16)
        ckv = ckv_ref[...]
        r = lax.rsqrt(jnp.mean(ckv * ckv, axis=-1, keepdims=True) + RMS_EPS)
        ckvn = (ckv * r * gkv_ref[...]).astype(BF16)
        lane = _iota((TM, LANES), 1)
        rot = (lane >= KR_LANE) & (lane < KR_LANE + MLA_ROPE)
        kr = jnp.where(rot, _rope(kr_ref[...], cos_v, sin_v, 1.0), 0.0)
        ka_ref[...] = (_dot(ckvn, wk_ref[...]) + jnp.tile(kr, (1, MLA_H))).astype(BF16)
        va_ref[...] = _dot(ckvn, wv_ref[...]).astype(BF16)

    full = lambda shape: pl.BlockSpec(shape, lambda i: (0, 0))
    return pl.pallas_call(
        body, name=name, grid=(S // TM,),
        in_specs=[pl.BlockSpec((TM, Q_LORA), lambda i: (i, P_CQ // Q_LORA)),
                  pl.BlockSpec((TM, KV_LORA), lambda i: (i, P_CKV // KV_LORA)),
                  pl.BlockSpec((TM, LANES), lambda i: (i, P_KR // LANES)),
                  full((1, Q_LORA)), full((1, KV_LORA)), full((Q_LORA, nh)), full((KV_LORA, nh)),
                  full((KV_LORA, MLA_H * MLA_V)),
                  pl.BlockSpec((TM, LANES), lambda i: (i, 0)), pl.BlockSpec((TM, LANES), lambda i: (i, 0))],
        out_specs=[pl.BlockSpec((TM, nh), lambda i: (i, 0)), pl.BlockSpec((TM, nh), lambda i: (i, 0)),
                   pl.BlockSpec((TM, MLA_H * MLA_V), lambda i: (i, 0))],
        out_shape=[jax.ShapeDtypeStruct((S, nh), BF16), jax.ShapeDtypeStruct((S, nh), BF16),
                   jax.ShapeDtypeStruct((S, MLA_H * MLA_V), BF16)],
        compiler_params=_params("parallel"),
    )(proj, proj, proj, g_cq, g_ckv, w_uq, w_uk, w_uv, cos_t, sin_t)


def _mla_prep_bwd(name, dqa, dka, dva, proj, g_cq, g_ckv, w_uq, w_uk, w_uv, cos_t, sin_t):
    nh = MLA_H * LANES

    def body(dqa_ref, dka_ref, dva_ref, cq_ref, ckv_ref, gq_ref, gkv_ref, wq_ref, wk_ref, wv_ref, cos_ref, sin_ref,
             dcq_ref, dckv_ref, dkr_ref, dwq_ref, dwk_ref, dwv_ref, dgq_ref, dgkv_ref):
        i = pl.program_id(0)
        cos_v, sin_v = cos_ref[...], sin_ref[...]

        def norm_bwd(x, g, dn):
            r = lax.rsqrt(jnp.mean(x * x, axis=-1, keepdims=True) + RMS_EPS)
            xh = x * r
            dxh = dn * g
            dx = r * (dxh - xh * jnp.mean(dxh * xh, axis=-1, keepdims=True))
            return dx, jnp.sum(dn * xh, axis=0, keepdims=True), (xh * g).astype(BF16)

        dq = _rope(dqa_ref[...], cos_v, sin_v, -1.0).astype(BF16)
        dcqn = _dot(dq, wq_ref[...], "nt")
        dcq, dgq, cqn = norm_bwd(cq_ref[...], gq_ref[...], dcqn)
        dcq_ref[...] = dcq.astype(BF16)
        dwq = _dot(cqn, dq, "tn")

        dka = dka_ref[...]
        dkab = dka.astype(BF16)
        dvab = dva_ref[...].astype(BF16)
        dckvn = _dot(dkab, wk_ref[...], "nt") + _dot(dvab, wv_ref[...], "nt")
        dckv, dgkv, ckvn = norm_bwd(ckv_ref[...], gkv_ref[...], dckvn)
        dckv_ref[...] = dckv.astype(BF16)
        dwk = _dot(ckvn, dkab, "tn")
        dwv = _dot(ckvn, dvab, "tn")

        fold = dka[:, 0:LANES]
        for hh in range(1, MLA_H):
            fold = fold + dka[:, hh * LANES:(hh + 1) * LANES]
        lane = _iota((TM, LANES), 1)
        rot = (lane >= KR_LANE) & (lane < KR_LANE + MLA_ROPE)
        dkr = _rope(jnp.where(rot, fold, 0.0), cos_v, sin_v, -1.0)
        dkr_ref[...] = jnp.where(rot, dkr, 0.0).astype(BF16)

        @pl.when(i == 0)
        def _():
            dwq_ref[...] = dwq
            dwk_ref[...] = dwk
            dwv_ref[...] = dwv
            dgq_ref[...] = dgq
            dgkv_ref[...] = dgkv

        @pl.when(i > 0)
        def _():
            dwq_ref[...] += dwq
            dwk_ref[...] += dwk
            dwv_ref[...] += dwv
            dgq_ref[...] += dgq
            dgkv_ref[...] += dgkv

    full = lambda shape: pl.BlockSpec(shape, lambda i: (0, 0))
    rows = lambda c: pl.BlockSpec((TM, c), lambda i: (i, 0))
    nv = MLA_H * MLA_V
    return pl.pallas_call(
        body, name=name, grid=(S // TM,),
        in_specs=[rows(nh), rows(nh), rows(nv),
                  pl.BlockSpec((TM, Q_LORA), lambda i: (i, P_CQ // Q_LORA)),
                  pl.BlockSpec((TM, KV_LORA), lambda i: (i, P_CKV // KV_LORA)),
                  full((1, Q_LORA)), full((1, KV_LORA)), full((Q_LORA, nh)), full((KV_LORA, nh)), full((KV_LORA, nv)),
                  rows(LANES), rows(LANES)],
        out_specs=[rows(Q_LORA), rows(KV_LORA), rows(LANES), full((Q_LORA, nh)), full((KV_LORA, nh)),
                   full((KV_LORA, nv)), full((1, Q_LORA)), full((1, KV_LORA))],
        out_shape=[jax.ShapeDtypeStruct((S, Q_LORA), BF16), jax.ShapeDtypeStruct((S, KV_LORA), BF16),
                   jax.ShapeDtypeStruct((S, LANES), BF16), jax.ShapeDtypeStruct((Q_LORA, nh), F32),
                   jax.ShapeDtypeStruct((KV_LORA, nh), F32), jax.ShapeDtypeStruct((KV_LORA, nv), F32),
                   jax.ShapeDtypeStruct((1, Q_LORA), F32), jax.ShapeDtypeStruct((1, KV_LORA), F32)],
        compiler_params=_params("arbitrary"),
    )(dqa, dka, dva, proj, proj, g_cq, g_ckv, w_uq, w_uk, w_uv, cos_t, sin_t)


def _head_masks(dtype):
    lane = _iota((1, LANES), 1)
    return (lane < 64).astype(dtype), (lane >= 64).astype(dtype)


def _mla_fwd(name, qa, ka, va, carry=None):
    def body(q_ref, k_ref, v_ref, o_ref, lse_ref):
        m0b, m1b = _head_masks(BF16)
        lane = _iota((QB, LANES), 1)
        left = lane < 64

        def qblock(i, _):
            r0 = pl.multiple_of(i * QB, QB)
            qs = [q_ref[pl.ds(r0, QB), hh * LANES:(hh + 1) * LANES] for hh in range(2)]
            rowc = lax.shift_right_logical(r0 + _iota((QB, QB), 0), 6)

            def kv(kb, carry):
                ms, ls, acc = carry
                c0 = pl.multiple_of(kb * QB, QB)
                v = v_ref[pl.ds(c0, QB), :]
                ok = lax.shift_right_logical(c0 + _iota((QB, QB), 1), 6) <= rowc
                new_m, new_l, alphas = [], [], []
                pv = None
                for hh in range(2):
                    k = k_ref[pl.ds(c0, QB), hh * LANES:(hh + 1) * LANES]
                    s = jnp.where(ok, _dot(qs[hh], k, "nt") * SCALE_A, NEG)
                    mn = jnp.maximum(ms[hh], jnp.max(s, axis=-1, keepdims=True))
                    p = jnp.exp(s - mn)
                    a = jnp.exp(ms[hh] - mn)
                    new_m.append(mn)
                    new_l.append(a * ls[hh] + jnp.sum(p, axis=-1, keepdims=True))
                    alphas.append(a)
                    part = _dot(p.astype(BF16), v * (m0b if hh == 0 else m1b))
                    pv = part if pv is None else pv + part
                acc = acc * jnp.where(left, alphas[0], alphas[1]) + pv
                return tuple(new_m), tuple(new_l), acc

            init = ((jnp.full((QB, 1), NEG, F32),) * 2, (jnp.zeros((QB, 1), F32),) * 2, jnp.zeros((QB, LANES), F32))
            ms, ls, acc = lax.fori_loop(0, i + 1, kv, init)
            o_ref[pl.ds(r0, QB), :] = acc * jnp.where(left, 1.0 / ls[0], 1.0 / ls[1])
            lse_ref[pl.ds(r0, QB), :] = jnp.where(left, ms[0] + jnp.log(ls[0]), ms[1] + jnp.log(ls[1]))
            return 0

        lax.fori_loop(0, S // QB, qblock, 0)

    pair = lambda w: pl.BlockSpec((S, w), lambda p: (0, p))
    return _carrier_call(
        body, name=name, grid=(MLA_H // 2,), in_specs=[pair(2 * LANES), pair(2 * LANES), pair(LANES)],
        out_specs=[pair(LANES), pair(LANES)],
        out_shape=[jax.ShapeDtypeStruct((S, MLA_H * MLA_V), F32), jax.ShapeDtypeStruct((S, MLA_H * MLA_V), F32)],
        args=(qa, ka, va), sem=("parallel",), carry=carry)


def _mla_bwd(name, qa, ka, va, o, lse, do, do_block0, carry=None):
    def body(q_ref, k_ref, v_ref, o_ref, lse_ref, do_ref, dq_ref, dk_ref, dv_ref):
        m0f, m1f = _head_masks(F32)
        m0b, m1b = _head_masks(BF16)
        dk_ref[...] = jnp.zeros_like(dk_ref)
        dv_ref[...] = jnp.zeros_like(dv_ref)

        def qblock(i, _):
            r0 = pl.multiple_of(i * QB, QB)
            rows = pl.ds(r0, QB)
            do_f = do_ref[rows, :]
            prod = do_f * o_ref[rows, :]
            deltas = [jnp.sum(prod * m0f, axis=-1, keepdims=True), jnp.sum(prod * m1f, axis=-1, keepdims=True)]
            lse_v = lse_ref[rows, :]
            lses = [lse_v[:, 0:1], lse_v[:, 64:65]]
            dob = do_f.astype(BF16)
            dos = [dob * m0b, dob * m1b]
            qs = [q_ref[rows, hh * LANES:(hh + 1) * LANES] for hh in range(2)]
            rowc = lax.shift_right_logical(r0 + _iota((QB, QB), 0), 6)

            def kv(kb, dqs):
                c0 = pl.multiple_of(kb * QB, QB)
                cols = pl.ds(c0, QB)
                v = v_ref[cols, :]
                ok = lax.shift_right_logical(c0 + _iota((QB, QB), 1), 6) <= rowc
                out = []
                dv = None
                for hh in range(2):
                    k = k_ref[cols, hh * LANES:(hh + 1) * LANES]
                    s = _dot(qs[hh], k, "nt") * SCALE_A
                    p = jnp.where(ok, jnp.exp(s - lses[hh]), 0.0)
                    dp = _dot(dos[hh], v, "nt")
                    ds = (p * (dp - deltas[hh]) * SCALE_A).astype(BF16)
                    out.append(dqs[hh] + _dot(ds, k))
                    dk_ref[cols, hh * LANES:(hh + 1) * LANES] += _dot(ds, qs[hh], "tn")
                    part = _dot(p.astype(BF16), dos[hh], "tn")
                    dv = part if dv is None else dv + part
                dv_ref[cols, :] += dv
                return tuple(out)

            dqs = lax.fori_loop(0, i + 1, kv, (jnp.zeros((QB, LANES), F32),) * 2)
            for hh in range(2):
                dq_ref[rows, hh * LANES:(hh + 1) * LANES] = dqs[hh]
            return 0

        lax.fori_loop(0, S // QB, qblock, 0)

    pair = lambda w: pl.BlockSpec((S, w), lambda p: (0, p))
    return _carrier_call(
        body, name=name, grid=(MLA_H // 2,),
        in_specs=[pair(2 * LANES), pair(2 * LANES), pair(LANES), pair(LANES), pair(LANES),
                  pl.BlockSpec((S, LANES), lambda p: (0, do_block0 + p))],
        out_specs=[pair(2 * LANES), pair(2 * LANES), pair(LANES)],
        out_shape=[jax.ShapeDtypeStruct((S, MLA_H * LANES), F32), jax.ShapeDtypeStruct((S, MLA_H * LANES), F32),
                   jax.ShapeDtypeStruct((S, MLA_H * MLA_V), F32)],
        args=(qa, ka, va, o, lse, do), sem=("parallel",), carry=carry)


def _sb_weights(q_h, k, c, before, tri_suffix):
    z = _dot(q_h, k, "nt") * SCALE_B
    sp = _softplus(z)
    log_keep = jnp.where(before, -sp, 0.0)
    log_between = _split_dot(log_keep, tri_suffix) + c
    w = jnp.where(before, jnp.exp(z - sp + log_between), 0.0)
    return w, jnp.exp(z - sp), jnp.sum(log_keep, axis=-1, keepdims=True)


def _sb_fwd(name, proj, carry=None):
    def body(q_ref, k_ref, v_ref, o_ref):
        m0b, m1b = _head_masks(BF16)
        tri_suffix = (_iota((QB, QB), 0) > _iota((QB, QB), 1)).astype(BF16)

        def qblock(i, _):
            r0 = pl.multiple_of(i * QB, QB)
            q = q_ref[pl.ds(r0, QB), :].astype(BF16)
            qs = [q * m0b, q * m1b]
            rowg = r0 + _iota((QB, QB), 0)

            def kv(step, carry):
                cs, acc = carry
                c0 = pl.multiple_of((i - step) * QB, QB)
                k = k_ref[pl.ds(c0, QB), :].astype(BF16)
                v = v_ref[pl.ds(c0, QB), :].astype(BF16)
                before = (c0 + _iota((QB, QB), 1)) < rowg
                new_c = []
                for hh in range(2):
                    w, _, tot = _sb_weights(qs[hh], k, cs[hh], before, tri_suffix)
                    new_c.append(cs[hh] + tot)
                    acc = acc + _dot(w.astype(BF16), v * (m0b if hh == 0 else m1b))
                return tuple(new_c), acc

            init = ((jnp.zeros((QB, 1), F32),) * 2, jnp.zeros((QB, LANES), F32))
            _, acc = lax.fori_loop(0, i + 1, kv, init)
            o_ref[pl.ds(r0, QB), :] = acc.astype(BF16)
            return 0

        lax.fori_loop(0, S // QB, qblock, 0)

    col = lambda base: pl.BlockSpec((S, LANES), lambda p: (0, base // LANES + p))
    return _carrier_call(
        body, name=name, grid=(SB_H // 2,), in_specs=[col(P_QB), col(P_KB), col(P_VB)],
        out_specs=[pl.BlockSpec((S, LANES), lambda p: (0, p))],
        out_shape=[jax.ShapeDtypeStruct((S, SB_H * SB_DIM), BF16)],
        args=(proj, proj, proj), sem=("parallel",), carry=carry)


def _sb_bwd(name, proj, do, do_block0, carry=None):
    nb = S // QB

    def body(q_ref, k_ref, v_ref, do_ref, dq_ref, dk_ref, dv_ref, sig_scr, dl_scr, dk_acc, dv_acc):
        m0b, m1b = _head_masks(BF16)
        tri_suffix = (_iota((QB, QB), 0) > _iota((QB, QB), 1)).astype(BF16)
        tri_prefix = (_iota((QB, QB), 0) < _iota((QB, QB), 1)).astype(BF16)
        dk_acc[...] = jnp.zeros_like(dk_acc)
        dv_acc[...] = jnp.zeros_like(dv_acc)

        def qblock(i, _):
            r0 = pl.multiple_of(i * QB, QB)
            rows = pl.ds(r0, QB)
            q = q_ref[rows, :].astype(BF16)
            qs = [q * m0b, q * m1b]
            dob = do_ref[rows, :].astype(BF16)
            dos = [dob * m0b, dob * m1b]
            rowg = r0 + _iota((QB, QB), 0)

            def sweep_left(step, cs):
                kb = i - step
                c0 = pl.multiple_of(kb * QB, QB)
                cols = pl.ds(c0, QB)
                k = k_ref[cols, :].astype(BF16)
                v = v_ref[cols, :].astype(BF16)
                before = (c0 + _iota((QB, QB), 1)) < rowg
                new_c = []
                dv = None
                for hh in range(2):
                    w, sig, tot = _sb_weights(qs[hh], k, cs[hh], before, tri_suffix)
                    new_c.append(cs[hh] + tot)
                    sig_scr[hh, kb] = sig
                    dl_scr[hh, kb] = _dot(dos[hh], v, "nt") * w
                    part = _dot(w.astype(BF16), dos[hh], "tn")
                    dv = part if dv is None else dv + part
                dv_acc[cols, :] += dv
                return tuple(new_c)

            lax.fori_loop(0, i + 1, sweep_left, (jnp.zeros((QB, 1), F32),) * 2)

            def sweep_right(kb, carry):
                ps, dq = carry
                c0 = pl.multiple_of(kb * QB, QB)
                cols = pl.ds(c0, QB)
                k = k_ref[cols, :].astype(BF16)
                before = (c0 + _iota((QB, QB), 1)) < rowg
                new_p = []
                dk = None
                for hh in range(2):
                    dl = dl_scr[hh, kb]
                    sig = sig_scr[hh, kb]
                    earlier = _split_dot(dl, tri_prefix) + ps[hh]
                    new_p.append(ps[hh] + jnp.sum(dl, axis=-1, keepdims=True))
                    dz = (jnp.where(before, dl * (1.0 - sig) - earlier * sig, 0.0) * SCALE_B).astype(BF16)
                    dq = dq + _dot(dz, k * (m0b if hh == 0 else m1b))
                    part = _dot(dz, qs[hh], "tn")
                    dk = part if dk is None else dk + part
                dk_acc[cols, :] += dk
                return tuple(new_p), dq

            init = ((jnp.zeros((QB, 1), F32),) * 2, jnp.zeros((QB, LANES), F32))
            _, dq = lax.fori_loop(0, i + 1, sweep_right, init)
            dq_ref[rows, :] = dq.astype(BF16)
            return 0

        lax.fori_loop(0, nb, qblock, 0)
        dk_ref[...] = dk_acc[...].astype(BF16)
        dv_ref[...] = dv_acc[...].astype(BF16)

    col = lambda base: pl.BlockSpec((S, LANES), lambda p: (0, base // LANES + p))
    out = pl.BlockSpec((S, LANES), lambda p: (0, p))
    shape = jax.ShapeDtypeStruct((S, SB_H * SB_DIM), BF16)
    return _carrier_call(
        body, name=name, grid=(SB_H // 2,),
        in_specs=[col(P_QB), col(P_KB), col(P_VB), pl.BlockSpec((S, LANES), lambda p: (0, do_block0 + p))],
        out_specs=[out, out, out], out_shape=[shape, shape, shape],
        scratch_shapes=[pltpu.VMEM((2, nb, QB, QB), F32), pltpu.VMEM((2, nb, QB, QB), F32),
                        pltpu.VMEM((S, LANES), F32), pltpu.VMEM((S, LANES), F32)],
        args=(proj, proj, proj, do), sem=("parallel",), carry=carry)


def _band_row_index():
    j = np.arange(TOEP_W)
    rel = np.clip(LEFT_CHUNKS * CHUNK - j, -REL_CLIP, REL_CLIP) + REL_CLIP
    rel[BAND_W:] = 2 * REL_CLIP
    return rel.astype(np.int32)


def _band_tiles(r0_ref, q_ref, kpad, vpad, m, m0b, m1b, static_ok, bias):
    r0 = pl.multiple_of(m * BQ, BQ)
    q = q_ref[0, pl.ds(r0, BQ), :]
    kw = kpad[pl.ds(r0, BAND_W), :]
    vw = vpad[pl.ds(r0, BAND_W), :]
    ok = static_ok & ((r0 - BAND_PAD + _iota((BQ, BAND_W), 1)) >= 0)
    qs = [q * m0b, q * m1b]
    ps = []
    for hh in range(2):
        s = jnp.where(ok, _dot(qs[hh], kw, "nt") * SCALE_C + bias[hh], NEG)
        e = jnp.exp(s - jnp.max(s, axis=-1, keepdims=True))
        ps.append(e * (1.0 / jnp.sum(e, axis=-1, keepdims=True)))
    return r0, qs, kw, vw, ps


def _band_setup(qkv_ref, r0_ref, kpad, vpad):
    kpad[0:BAND_PAD, :] = jnp.zeros((BAND_PAD, LANES), BF16)
    vpad[0:BAND_PAD, :] = jnp.zeros((BAND_PAD, LANES), BF16)
    kpad[BAND_PAD:, :] = qkv_ref[1]
    vpad[BAND_PAD:, :] = qkv_ref[2]
    jc = lax.shift_right_logical(_iota((BQ, BAND_W), 1), 6)
    rc = lax.shift_right_logical(_iota((BQ, BAND_W), 0), 6)
    static_ok = (jc >= rc) & (jc <= rc + LEFT_CHUNKS)
    bias = []
    for hh in range(2):
        row = jnp.broadcast_to(r0_ref[hh:hh + 1, :], (BQ, TOEP_W))
        bias.append(pltpu.roll(row, 0, 1, stride=1, stride_axis=0)[:, :BAND_W])
    return static_ok, bias


def _band_fwd(name, qkv, r0, carry=None):
    def body(qkv_ref, r0_ref, o_ref, kpad, vpad):
        m0b, m1b = _head_masks(BF16)
        static_ok, bias = _band_setup(qkv_ref, r0_ref, kpad, vpad)

        def qblock(m, _):
            r0_, _, _, vw, ps = _band_tiles(r0_ref, qkv_ref, kpad, vpad, m, m0b, m1b, static_ok, bias)
            o = _dot(ps[0].astype(BF16), vw * m0b) + _dot(ps[1].astype(BF16), vw * m1b)
            o_ref[pl.ds(r0_, BQ), :] = o.astype(BF16)
            return 0

        lax.fori_loop(0, S // BQ, qblock, 0)

    return _carrier_call(
        body, name=name, grid=(C_H // 2,),
        in_specs=[pl.BlockSpec((3, S, LANES), lambda p: (0, 0, p)), pl.BlockSpec((None, 2, TOEP_W), lambda p: (p, 0, 0))],
        out_specs=[pl.BlockSpec((S, LANES), lambda p: (0, p))],
        out_shape=[jax.ShapeDtypeStruct((S, C_H * C_DIM), BF16)],
        scratch_shapes=[pltpu.VMEM((S + BAND_PAD, LANES), BF16), pltpu.VMEM((S + BAND_PAD, LANES), BF16)],
        args=(qkv, r0), sem=("parallel",), carry=carry)


def _band_bwd(name, qkv, r0, do, carry=None):
    def body(qkv_ref, r0_ref, do_ref, dqkv_ref, dr0_ref, kpad, vpad, dkpad, dvpad, db_acc):
        m0b, m1b = _head_masks(BF16)
        static_ok, bias = _band_setup(qkv_ref, r0_ref, kpad, vpad)
        dkpad[...] = jnp.zeros_like(dkpad)
        dvpad[...] = jnp.zeros_like(dvpad)
        db_acc[...] = jnp.zeros_like(db_acc)

        def qblock(m, _):
            r0_, qs, kw, vw, ps = _band_tiles(r0_ref, qkv_ref, kpad, vpad, m, m0b, m1b, static_ok, bias)
            dob = do_ref[pl.ds(r0_, BQ), :].astype(BF16)
            dos = [dob * m0b, dob * m1b]
            dq = None
            dk = None
            dv = None
            for hh in range(2):
                p = ps[hh]
                dp = _dot(dos[hh], vw, "nt")
                ds = p * (dp - jnp.sum(dp * p, axis=-1, keepdims=True))
                db_acc[hh, :, 0:BAND_W] += ds
                dsb = (ds * SCALE_C).astype(BF16)
                t = _dot(dsb, kw * (m0b if hh == 0 else m1b))
                dq = t if dq is None else dq + t
                t = _dot(dsb, qs[hh], "tn")
                dk = t if dk is None else dk + t
                t = _dot(p.astype(BF16), dos[hh], "tn")
                dv = t if dv is None else dv + t
            dqkv_ref[0, pl.ds(r0_, BQ), :] = dq.astype(BF16)
            dkpad[pl.ds(r0_, BAND_W), :] += dk
            dvpad[pl.ds(r0_, BAND_W), :] += dv
            return 0

        lax.fori_loop(0, S // BQ, qblock, 0)
        dqkv_ref[1] = dkpad[BAND_PAD:, :].astype(BF16)
        dqkv_ref[2] = dvpad[BAND_PAD:, :].astype(BF16)
        sub = _iota((8, TOEP_W), 0)
        for hh in range(2):
            folded = db_acc[hh, 0:8, :]
            for a in range(1, BQ // 8):
                folded = folded + pltpu.roll(db_acc[hh, 8 * a:8 * a + 8, :], TOEP_W - 8 * a, 1)
            for bit in range(3):
                moved = pltpu.roll(folded, TOEP_W - (1 << bit), 1)
                folded = jnp.where((sub & (1 << bit)) != 0, moved, folded)
            dr0_ref[hh:hh + 1, :] = jnp.sum(folded, axis=0, keepdims=True)

    return _carrier_call(
        body, name=name, grid=(C_H // 2,),
        in_specs=[pl.BlockSpec((3, S, LANES), lambda p: (0, 0, p)), pl.BlockSpec((None, 2, TOEP_W), lambda p: (p, 0, 0)),
                  pl.BlockSpec((S, LANES), lambda p: (0, p))],
        out_specs=[pl.BlockSpec((3, S, LANES), lambda p: (0, 0, p)), pl.BlockSpec((None, 2, TOEP_W), lambda p: (p, 0, 0))],
        out_shape=[jax.ShapeDtypeStruct((3, S, C_H * C_DIM), BF16), jax.ShapeDtypeStruct((C_H // 2, 2, TOEP_W), F32)],
        scratch_shapes=[pltpu.VMEM((S + BAND_PAD, LANES), BF16), pltpu.VMEM((S + BAND_PAD, LANES), BF16),
                        pltpu.VMEM((S + BAND_PAD, LANES), F32), pltpu.VMEM((S + BAND_PAD, LANES), F32),
                        pltpu.VMEM((2, BQ, TOEP_W), F32)],
        args=(qkv, r0, do), sem=("parallel",), carry=carry)


def _bias_table_grad(name, dr0):
    w_out = 5 * LANES

    def body(d_ref, o_ref):
        j = _iota((TOEP_W, w_out), 0)
        rel = jnp.clip(LEFT_CHUNKS * CHUNK - j, -REL_CLIP, REL_CLIP) + REL_CLIP
        rel = jnp.where(j >= BAND_W, 2 * REL_CLIP, rel)
        onehot = (rel == _iota((TOEP_W, w_out), 1)).astype(BF16)
        d = d_ref[...]
        hi = d.astype(BF16)
        mid = (d - hi.astype(F32))
        mid_b = mid.astype(BF16)
        lo = (mid - mid_b.astype(F32)).astype(BF16)
        o_ref[...] = _dot(hi, onehot) + _dot(mid_b, onehot) + _dot(lo, onehot)

    return pl.pallas_call(
        body, name=name, out_shape=jax.ShapeDtypeStruct((C_H, w_out), F32),
        in_specs=[pl.BlockSpec((C_H, TOEP_W), lambda: (0, 0))], out_specs=pl.BlockSpec((C_H, w_out), lambda: (0, 0)),
        grid=(),
    )(dr0)


def _carry_gather(cy, slots, names, ici, d2d):
    idx = [cy.operand(slots[n], True) for n in names]
    n = len(names)
    base_i = cy.sems(3 * n) if ici else 0
    base_d = cy.sems(3 * n) if d2d else 0

    def piece(refs, t, slot, cc):
        return refs[idx[t]].at[slot, _half_rows(cc, slots[names[t]].shape[1] // 2), :]

    def over_ici(refs, send, recv, arriving):
        x, y, c, chips = _position()
        out = []
        for t in range(n):
            for j in range(3):
                r = piece(refs, t, 2 * chips[j][0] + chips[j][1] if arriving else 2 * x + y, c)
                out.append(_remote(r, r, send, recv, base_i + 3 * t + j, (*chips[j], c)))
        return out

    def over_d2d(refs, send, recv, arriving):
        x, y, c, chips = _position()
        out = []
        for t in range(n):
            for j in range(3):
                r = piece(refs, t, 2 * chips[j][0] + chips[j][1], 1 - c if arriving else c)
                out.append(_remote(r, r, send, recv, base_d + 3 * t + j, (x, y, 1 - c)))
        return out

    def start_ici(refs, fresh, send, recv):
        for cp in over_ici(refs, send, recv, False):
            cp.start()

    def wait_ici(refs, fresh, send, recv):
        for cp in over_ici(refs, send, recv, True):
            cp.wait_recv()
        for cp in over_ici(refs, send, recv, False):
            cp.wait_send()

    def start_d2d(refs, fresh, send, recv):
        for cp in over_d2d(refs, send, recv, False):
            cp.start()

    def wait_d2d(refs, fresh, send, recv):
        for cp in over_d2d(refs, send, recv, True):
            cp.wait_recv()
        for cp in over_d2d(refs, send, recv, False):
            cp.wait_send()

    if ici and d2d:
        cy.starts.append(start_ici)
        cy.finishes += [wait_ici, start_d2d, wait_d2d]
    elif ici:
        cy.starts.append(start_ici)
        cy.finishes.append(wait_ici)
    else:
        cy.starts.append(start_d2d)
        cy.finishes.append(wait_d2d)

    def done(aliased, fresh):
        for t, name in enumerate(names):
            slots[name] = aliased[idx[t]]

    cy.on_done.append(done)


def _carry_chip_exchange(cy, sums, got, names):
    idx = [cy.operand(sums[n], False) for n in names]
    out = [cy.result((3,) + sums[n].shape[1:], BF16) for n in names]
    base = cy.sems(3 * len(names))

    def copies(refs, fresh, send, recv):
        x, y, c, chips = _position()
        return [_remote(refs[idx[t]].at[2 * chips[j][0] + chips[j][1]], fresh[out[t]].at[j], send, recv, base + 3 * t + j,
                        (*chips[j], c)) for t in range(len(names)) for j in range(3)]

    def start(refs, fresh, send, recv):
        for cp in copies(refs, fresh, send, recv):
            cp.start()

    def wait(refs, fresh, send, recv):
        for cp in copies(refs, fresh, send, recv):
            cp.wait()

    cy.starts.append(start)
    cy.finishes.append(wait)

    def done(aliased, fresh):
        for t, name in enumerate(names):
            got[name] = fresh[out[t]]

    cy.on_done.append(done)


def _run_carry(name, cy):
    _, res = _carrier_call(None, name=name, grid=(), in_specs=[], out_specs=[], out_shape=[], args=(), sem=(), carry=cy)
    cy.done(res)


FIRST_WEIGHTS = ("ev_w_in", "ev_w_uq", "ev_w_ukv")
WEIGHTS_A = ("ev_w_out", "w_gate0", "w_up0")
WEIGHTS_B = ("w_down0", "od_w_qkv", "od_w_out")
WEIGHTS_C = ("w_gate1", "w_up1")
WEIGHTS_D = ("w_down1",)
GRAD_GROUPS = {"ffn1": ("w_gate1", "w_up1", "w_down1"), "od": ("od_w_qkv", "od_w_out"),
               "ffn0": ("w_gate0", "w_up0", "w_down0"), "ev": ("ev_w_in", "ev_w_uq", "ev_w_ukv", "ev_w_out")}


def _carry_pair_exchange(cy, parts, theirs, names):
    idx = [cy.operand(parts[n], False) for n in names]
    out = [cy.result((N_CHIPS, parts[n].shape[1] // 2, parts[n].shape[2]), BF16) for n in names]
    base = cy.sems(len(names))

    def copies(refs, fresh, send, recv):
        x, y, c, _ = _position()
        return [_remote(refs[idx[t]].at[:, _half_rows(1 - c, parts[n].shape[1] // 2), :], fresh[out[t]], send, recv,
                        base + t, (x, y, 1 - c)) for t, n in enumerate(names)]

    cy.starts.append(lambda refs, fresh, send, recv: [cp.start() for cp in copies(refs, fresh, send, recv)])
    cy.finishes.append(lambda refs, fresh, send, recv: [cp.wait() for cp in copies(refs, fresh, send, recv)])

    def done(aliased, fresh):
        for t, name in enumerate(names):
            theirs[name] = fresh[out[t]]

    cy.on_done.append(done)


def _carry_sibling_exchange(cy, fulls, pieces):
    idx = [cy.operand(fulls[p], True) for p, _ in pieces]
    base = cy.sems(len(pieces))

    def copies(refs, send, recv, arriving):
        x, y, c, _ = _position()
        out = []
        for t, (p, layer) in enumerate(pieces):
            r = refs[idx[t]].at[layer, _half_rows(1 - c if arriving else c, fulls[p].shape[1] // 2), :]
            out.append(_remote(r, r, send, recv, base + t, (x, y, 1 - c)))
        return out

    def start(refs, fresh, send, recv):
        for cp in copies(refs, send, recv, False):
            cp.start()

    def wait(refs, fresh, send, recv):
        for cp in copies(refs, send, recv, True):
            cp.wait_recv()
        for cp in copies(refs, send, recv, False):
            cp.wait_send()

    cy.starts.append(start)
    cy.finishes.append(wait)

    def done(aliased, fresh):
        for t, (p, _) in enumerate(pieces):
            fulls[p] = aliased[idx[t]]

    cy.on_done.append(done)


RIDES = {
    "mla_attn": (("gather_ici", WEIGHTS_A),),
    "sb_attn": (("gather_d2d", WEIGHTS_A), ("gather_ici", WEIGHTS_B)),
    "ev_out": (("gather_d2d", WEIGHTS_B),),
    "ffn0": (("gather_ici", WEIGHTS_C),),
    "qkv": (("gather_d2d", WEIGHTS_C),),
    "band_attn": (("gather_ici", WEIGHTS_D),),
    "od_out": (("gather_d2d", WEIGHTS_D),),
    "od_out_bwd_w": (("pair", "ffn1"),),
    "band_attn_bwd": (("chips", "ffn1"),),
    "rms_mix1_bwd": (("pair", "od"),),
    "ev_out_bwd_w": (("pair", "ffn0"),),
    "mla_attn_bwd": (("chips", "od"), ("sibling", "ffn1")),
    "sb_attn_bwd": (("chips", "ffn0"), ("sibling", "od")),
    "proj_in_bwd_w": (("sibling", "ffn0"),),
    "proj_in_bwd_x": (("chips", "ev"),),
}


class _Exchanges:
    def __init__(self, slots, pos, shapes):
        self.slots, self.pos, self.shapes = dict(slots), pos, shapes
        self.parts, self.theirs, self.sums, self.got, self.fulls = {}, {}, {}, {}, {}

    def begin(self):
        cy = _Carry()
        _carry_gather(cy, self.slots, FIRST_WEIGHTS, True, True)
        _run_carry("gather_first", cy)

    def weights(self, *names):
        return [self.slots[n] for n in names]

    def _pair_sums(self, group):
        for n in GRAD_GROUPS[group]:
            if n not in self.sums:
                self.sums[n] = _pair_sum("pair_sum_" + n, self.parts[n], self.theirs[n], self.pos)

    def _chip_sums(self, group):
        for n in GRAD_GROUPS[group]:
            param, layer = PART_OF[n]
            self.fulls[param] = _chip_sum("chip_sum_" + n, self.sums[n], self.got[n], self.pos, layer,
                                          self.shapes[param], self.fulls.get(param))

    def carry(self, stage):
        cy = _Carry()
        for step, what in RIDES[stage]:
            if step == "gather_ici":
                _carry_gather(cy, self.slots, what, True, False)
            elif step == "gather_d2d":
                _carry_gather(cy, self.slots, what, False, True)
            elif step == "pair":
                _carry_pair_exchange(cy, self.parts, self.theirs, GRAD_GROUPS[what])
            elif step == "chips":
                self._pair_sums(what)
                _carry_chip_exchange(cy, self.sums, self.got, GRAD_GROUPS[what])
            elif step == "sibling":
                self._chip_sums(what)
                _carry_sibling_exchange(cy, self.fulls, [PART_OF[n] for n in GRAD_GROUPS[what]])
        return cy

    def grads(self, group, parts):
        self.parts.update(parts)
        if group == "ev":
            cy = _Carry()
            _carry_pair_exchange(cy, self.parts, self.theirs, GRAD_GROUPS[group])
            _run_carry("grads_pair_ev", cy)

    def finish(self):
        cy = _Carry()
        self._chip_sums("ev")
        _carry_sibling_exchange(cy, self.fulls, [PART_OF[n] for n in GRAD_GROUPS["ev"]])
        _run_carry("grads_sibling_ev", cy)
        return {n: self.fulls[n] for n in BIG}


class _NoExchanges:
    def __init__(self, slots):
        self.slots, self.parts = dict(slots), {}

    def begin(self):
        pass

    def weights(self, *names):
        return [self.slots[n] for n in names]

    def carry(self, stage):
        return None

    def grads(self, group, parts):
        self.parts.update(parts)


def _first_weights(w_in_s, w_uq_s, w_ukv_s):
    gw = {"ev_w_in": w_in_s, "ev_w_uq": w_uq_s, "ev_w_ukv": w_ukv_s}
    w_in = jnp.moveaxis(gw["ev_w_in"], 0, 1).reshape(D, EVEN_IN)
    z = lambda n: jnp.zeros((D, n), BF16)
    w_in_p = jnp.concatenate(
        [w_in[:, 0:384], z(128), w_in[:, 384:640], w_in[:, 672:2208], z(KR_LANE), w_in[:, 640:672],
         z(LANES - KR_LANE - MLA_ROPE)], axis=1)
    w_uq = jnp.moveaxis(gw["ev_w_uq"], 0, 1).reshape(Q_LORA, MLA_H, MLA_NOPE + MLA_ROPE)
    w_uq_p = jnp.concatenate([w_uq, jnp.zeros((Q_LORA, MLA_H, LANES - MLA_NOPE - MLA_ROPE), BF16)], axis=2)
    w_ukv = jnp.moveaxis(gw["ev_w_ukv"], 0, 1).reshape(KV_LORA, MLA_H, MLA_NOPE + MLA_V)
    w_uk_p = jnp.concatenate([w_ukv[:, :, :MLA_NOPE], jnp.zeros((KV_LORA, MLA_H, LANES - MLA_NOPE), BF16)], axis=2)
    return dict(
        w_in=w_in_p, w_uq=w_uq_p.reshape(Q_LORA, MLA_H * LANES), w_uk=w_uk_p.reshape(KV_LORA, MLA_H * LANES),
        w_uv=w_ukv[:, :, MLA_NOPE:].reshape(KV_LORA, MLA_H * MLA_V))


def _proj_mm(name, u, w_in):
    return _mm(name, u, w_in, kind="nn", grid=(S // TM, 1, 1),
               a_spec=pl.BlockSpec((TM, D), lambda i, j, k: (i, 0)), b_spec=pl.BlockSpec((D, P_IN), lambda i, j, k: (0, 0)),
               o_spec=pl.BlockSpec((TM, P_IN), lambda i, j, k: (i, 0)), out_shape=(S, P_IN), out_dtype=F32, acc_shape=None)


def _out_proj(name, o, w, resid, carry=None):
    return _mm(name, o, w, kind="nn", grid=(S // TM, 1, 1),
               a_spec=pl.BlockSpec((TM, D), lambda i, j, k: (i, 0)), b_spec=pl.BlockSpec((D, D), lambda i, j, k: (0, 0)),
               o_spec=pl.BlockSpec((TM, D), lambda i, j, k: (i, 0)), out_shape=(S, D), out_dtype=F32, acc_shape=None,
               resid=resid, r_spec=pl.BlockSpec((TM, D), lambda i, j, k: (i, 0)), carry=carry)


def _out_proj_bwd(name, dh, o, w, ex):
    d_o = _mm(name + "_x", dh, w, kind="nt", grid=(S // TM, 1, 1),
              a_spec=pl.BlockSpec((TM, D), lambda i, j, k: (i, 0)), b_spec=pl.BlockSpec((D, D), lambda i, j, k: (0, 0)),
              o_spec=pl.BlockSpec((TM, D), lambda i, j, k: (i, 0)), out_shape=(S, D), out_dtype=F32, acc_shape=None)
    d_w = _mm(name + "_w", o, dh, kind="tn", grid=(2, S // TM),
              a_spec=pl.BlockSpec((TM, TM), lambda j, k: (k, j)), b_spec=pl.BlockSpec((TM, D), lambda j, k: (k, 0)),
              o_spec=pl.BlockSpec((TM, D), lambda j, k: (j, 0)), out_shape=(D, D), out_dtype=BF16, acc_shape=(TM, D),
              carry=ex.carry(name + "_w"))
    return d_o, d_w


def _local_step(x, tgt, sm, ex):
    def riding(stage, fn, *args):
        cy = ex.carry(stage)
        res, copies = fn(stage, *args, carry=cy)
        if cy is not None:
            cy.done(copies)
        return res

    cos_t, sin_t = _rope_tables()
    g_mix, g_ffn = sm["g_mix"], sm["g_ffn"]
    r0 = sm["od_rel_bias"][0][:, _band_row_index()].reshape(C_H // 2, 2, TOEP_W)
    nt = 3

    ex.begin()
    w = _first_weights(*ex.weights(*FIRST_WEIGHTS))
    u0 = _rms_fwd("rms_mix0", x, g_mix[0:1])
    proj = _proj_mm("proj_in", u0, w["w_in"])
    qa, ka, va = _mla_prep_fwd("mla_prep", proj, sm["ev_g_cq"], sm["ev_g_ckv"], w["w_uq"], w["w_uk"], w["w_uv"], cos_t, sin_t)
    o_a, lse = riding("mla_attn", _mla_fwd, qa, ka, va)
    o_b, = riding("sb_attn", _sb_fwd, proj)
    o_ev = jnp.concatenate([o_a.astype(BF16), o_b], axis=1)
    w["ev_w_out"] = ex.weights("ev_w_out")[0].reshape(D, D)
    h1 = _out_proj("ev_out", o_ev, w["ev_w_out"], x, ex.carry("ev_out"))
    w["w_gate0"], w["w_up0"], w["w_down0"] = ex.weights("w_gate0", "w_up0", "w_down0")
    h2, gate0, up0 = riding("ffn0", _ffn_fwd, h1, g_ffn[0:1], w["w_gate0"], w["w_up0"], w["w_down0"])
    w["w_qkv"] = jnp.moveaxis(ex.weights("od_w_qkv")[0], 0, 1).reshape(D, nt * D)
    u2 = _rms_fwd("rms_mix1", h2, g_mix[1:2])
    qkv = _mm("qkv", u2, w["w_qkv"], kind="nn", grid=(S // TM, nt, 1),
              a_spec=pl.BlockSpec((TM, D), lambda i, t, k: (i, 0)), b_spec=pl.BlockSpec((D, D), lambda i, t, k: (0, t)),
              o_spec=pl.BlockSpec((None, TM, D), lambda i, t, k: (t, i, 0)),
              out_shape=(nt, S, D), out_dtype=BF16, acc_shape=None, carry=ex.carry("qkv"))
    o_od, = riding("band_attn", _band_fwd, qkv, r0)
    w["od_w_out"] = ex.weights("od_w_out")[0].reshape(D, D)
    h3 = _out_proj("od_out", o_od, w["od_w_out"], h2, ex.carry("od_out"))
    w["w_gate1"], w["w_up1"], w["w_down1"] = ex.weights("w_gate1", "w_up1", "w_down1")
    (h4, gate1, up1), _ = _ffn_fwd("ffn1", h3, g_ffn[1:2], w["w_gate1"], w["w_up1"], w["w_down1"])

    loss, dh4, dg_final = _loss_bwd("loss", h4, sm["g_final"].reshape(1, D), tgt)

    dh3, dg_ffn1, d_wg1, d_wu1, d_wd1 = _ffn_bwd_all("ffn1_bwd", dh4, h3, g_ffn[1:2], gate1, up1,
                                                     w["w_gate1"], w["w_up1"], w["w_down1"])
    ex.grads("ffn1", {"w_gate1": d_wg1, "w_up1": d_wu1, "w_down1": d_wd1})

    d_ood, d_w_od_out = _out_proj_bwd("od_out_bwd", dh3, o_od, w["od_w_out"], ex)
    dqkv, dr0 = riding("band_attn_bwd", _band_bwd, qkv, r0, d_ood)
    du2 = _mm("qkv_bwd_x", dqkv, w["w_qkv"], kind="nt", grid=(S // TM, nt),
              a_spec=pl.BlockSpec((None, TM, D), lambda i, t: (t, i, 0)), b_spec=pl.BlockSpec((D, D), lambda i, t: (0, t)),
              o_spec=pl.BlockSpec((TM, D), lambda i, t: (i, 0)), out_shape=(S, D), out_dtype=F32, acc_shape=(TM, D))
    d_w_qkv = _mm("qkv_bwd_w", u2, dqkv, kind="tn", grid=(nt, S // TM),
                  a_spec=pl.BlockSpec((TM, D), lambda t, k: (k, 0)), b_spec=pl.BlockSpec((None, TM, D), lambda t, k: (t, k, 0)),
                  o_spec=pl.BlockSpec((D, D), lambda t, k: (0, t)), out_shape=(D, nt * D), out_dtype=BF16, acc_shape=(D, D))
    shard_cols = lambda a: jnp.moveaxis(a.reshape(a.shape[0], N_CHIPS, a.shape[1] // N_CHIPS), 1, 0)
    ex.grads("od", {"od_w_qkv": shard_cols(d_w_qkv), "od_w_out": d_w_od_out.reshape(N_CHIPS, D // N_CHIPS, D)})
    dh2, dg_mix1 = _rms_bwd("rms_mix1_bwd", du2, h2, g_mix[1:2], dh3, carry=ex.carry("rms_mix1_bwd"))
    d_rel = _bias_table_grad("rel_bias_grad", dr0.reshape(C_H, TOEP_W))[:, :2 * REL_CLIP + 1]

    dh1, dg_ffn0, d_wg0, d_wu0, d_wd0 = _ffn_bwd_all("ffn0_bwd", dh2, h1, g_ffn[0:1], gate0, up0,
                                                     w["w_gate0"], w["w_up0"], w["w_down0"])
    ex.grads("ffn0", {"w_gate0": d_wg0, "w_up0": d_wu0, "w_down0": d_wd0})

    d_oev, d_w_ev_out = _out_proj_bwd("ev_out_bwd", dh1, o_ev, w["ev_w_out"], ex)
    dqa, dka, dva = riding("mla_attn_bwd", _mla_bwd, qa, ka, va, o_a, lse, d_oev, 0)
    dqb, dkb, dvb = riding("sb_attn_bwd", _sb_bwd, proj, d_oev, MLA_H * MLA_V // LANES)
    dcq, dckv, dkr, d_w_uq, d_w_uk, d_w_uv, dg_cq, dg_ckv = _mla_prep_bwd(
        "mla_prep_bwd", dqa, dka, dva, proj, sm["ev_g_cq"], sm["ev_g_ckv"], w["w_uq"], w["w_uk"], w["w_uv"], cos_t, sin_t)
    dproj = jnp.concatenate([dcq, jnp.zeros((S, LANES), BF16), dckv, dqb, dkb, dvb, dkr], axis=1)
    d_w_in_p = _mm("proj_in_bwd_w", u0, dproj, kind="tn", grid=(1, S // TM),
                   a_spec=pl.BlockSpec((TM, D), lambda j, k: (k, 0)), b_spec=pl.BlockSpec((TM, P_IN), lambda j, k: (k, 0)),
                   o_spec=pl.BlockSpec((D, P_IN), lambda j, k: (0, 0)), out_shape=(D, P_IN), out_dtype=BF16,
                   acc_shape=(D, P_IN), carry=ex.carry("proj_in_bwd_w"))
    d_w_in = jnp.concatenate([d_w_in_p[:, 0:384], d_w_in_p[:, 512:768],
                              d_w_in_p[:, P_KR + KR_LANE:P_KR + KR_LANE + MLA_ROPE], d_w_in_p[:, 768:2304]], axis=1)
    d_w_uq_std = d_w_uq.reshape(Q_LORA, MLA_H, LANES)[:, :, :MLA_NOPE + MLA_ROPE].reshape(Q_LORA, -1)
    d_w_ukv = jnp.concatenate([d_w_uk.reshape(KV_LORA, MLA_H, LANES)[:, :, :MLA_NOPE],
                               d_w_uv.reshape(KV_LORA, MLA_H, MLA_V)], axis=2).reshape(KV_LORA, -1)
    ex.grads("ev", {"ev_w_in": shard_cols(d_w_in), "ev_w_uq": shard_cols(d_w_uq_std.astype(BF16)),
                    "ev_w_ukv": shard_cols(d_w_ukv.astype(BF16)),
                    "ev_w_out": d_w_ev_out.reshape(N_CHIPS, D // N_CHIPS, D)})
    du0 = _mm("proj_in_bwd_x", dproj, w["w_in"], kind="nt", grid=(S // TM, 1, 1),
              a_spec=pl.BlockSpec((TM, P_IN), lambda i, j, k: (i, 0)), b_spec=pl.BlockSpec((D, P_IN), lambda i, j, k: (0, 0)),
              o_spec=pl.BlockSpec((TM, D), lambda i, j, k: (i, 0)), out_shape=(S, D), out_dtype=F32, acc_shape=None,
              carry=ex.carry("proj_in_bwd_x"))
    grad_x, dg_mix0 = _rms_bwd("rms_mix0_bwd", du0, x, g_mix[0:1], dh1)
    small = {
        "ev_g_cq": dg_cq, "ev_g_ckv": dg_ckv, "od_rel_bias": d_rel.reshape(1, C_H, 2 * REL_CLIP + 1),
        "g_mix": jnp.concatenate([dg_mix0, dg_mix1], axis=0), "g_ffn": jnp.concatenate([dg_ffn0, dg_ffn1], axis=0),
        "g_final": dg_final.reshape(D),
    }
    return loss, grad_x, small


BIG = ("ev_w_in", "ev_w_uq", "ev_w_ukv", "ev_w_out", "od_w_qkv", "od_w_out", "w_gate", "w_up", "w_down")
SMALL = ("ev_g_cq", "ev_g_ckv", "od_rel_bias", "g_mix", "g_ffn", "g_final")
WEIGHTS = ("ev_w_in", "ev_g_cq", "ev_w_uq", "ev_g_ckv", "ev_w_ukv", "ev_w_out", "od_w_qkv", "od_rel_bias", "od_w_out",
           "g_mix", "g_ffn", "w_gate", "w_up", "w_down", "g_final")
GRAD_PARTS = (("ev_w_in", "ev_w_in", 0), ("ev_w_uq", "ev_w_uq", 0), ("ev_w_ukv", "ev_w_ukv", 0),
              ("ev_w_out", "ev_w_out", 0), ("od_w_qkv", "od_w_qkv", 0), ("od_w_out", "od_w_out", 0),
              ("w_gate0", "w_gate", 0), ("w_gate1", "w_gate", 1), ("w_up0", "w_up", 0), ("w_up1", "w_up", 1),
              ("w_down0", "w_down", 0), ("w_down1", "w_down", 1))
PART_OF = {part: (param, layer) for part, param, layer in GRAD_PARTS}
SMALL_ROWS = 112
SMALL_SIZE = 384 + 256 + 16 * 513 + 2 * 1024 + 2 * 1024 + 1024
TRANSPOSED = ("w_gate", "w_up")


def _row_tile(rows, cap=512):
    for t in range(min(rows, cap), 0, -1):
        if rows % t == 0 and t % 16 == 0:
            return t
    return rows


def _cast_into_slot(name, w, layer, pos):
    _, rows, cols = w.shape
    tr = _row_tile(rows)

    def body(pos_ref, w_ref, o_ref):
        o_ref[...] = w_ref[...].astype(BF16)

    return pl.pallas_call(
        body, name=name,
        grid_spec=pltpu.PrefetchScalarGridSpec(
            num_scalar_prefetch=1, grid=(rows // tr,),
            in_specs=[pl.BlockSpec((None, tr, cols), lambda i, p: (layer, i, 0))],
            out_specs=pl.BlockSpec((None, tr, cols), lambda i, p: (p[0], i, 0))),
        out_shape=jax.ShapeDtypeStruct((N_CHIPS, rows, cols), BF16), compiler_params=_params("arbitrary"))(pos, w)


def _pair_sum(name, part, theirs, pos):
    _, half, cols = theirs.shape
    tr = _row_tile(half)
    nb = half // tr

    def body(pos_ref, a_ref, b_ref, o_ref):
        o_ref[...] = (a_ref[...].astype(F32) + b_ref[...].astype(F32)).astype(BF16)

    return pl.pallas_call(
        body, name=name,
        grid_spec=pltpu.PrefetchScalarGridSpec(
            num_scalar_prefetch=1, grid=(N_CHIPS, nb),
            in_specs=[pl.BlockSpec((None, tr, cols), lambda s, i, p: (s, p[1] * nb + i, 0)),
                      pl.BlockSpec((None, tr, cols), lambda s, i, p: (s, i, 0))],
            out_specs=pl.BlockSpec((None, tr, cols), lambda s, i, p: (s, i, 0))),
        out_shape=jax.ShapeDtypeStruct(theirs.shape, BF16),
        compiler_params=_params("arbitrary", "arbitrary"))(pos, part, theirs)


def _chip_sum(name, sums, got, pos, layer, full_shape, full=None):
    _, half, cols = sums.shape
    tr = _row_tile(half)
    nb = half // tr

    def body(pos_ref, s_ref, g_ref, *rest):
        out_ref = rest[-1]
        out_ref[...] = ((s_ref[...].astype(F32) + g_ref[0].astype(F32)) + g_ref[1].astype(F32)) + g_ref[2].astype(F32)

    in_specs = [pl.BlockSpec((None, tr, cols), lambda i, p: (p[0], i, 0)),
                pl.BlockSpec((3, tr, cols), lambda i, p: (0, i, 0))]
    args = [pos, sums, got]
    if full is not None:
        in_specs.append(ANY)
        args.append(full)
    return pl.pallas_call(
        body, name=name,
        grid_spec=pltpu.PrefetchScalarGridSpec(
            num_scalar_prefetch=1, grid=(nb,), in_specs=in_specs,
            out_specs=pl.BlockSpec((None, tr, cols), lambda i, p: (layer, p[1] * nb + i, 0))),
        out_shape=jax.ShapeDtypeStruct(full_shape, F32),
        input_output_aliases={3: 0} if full is not None else {},
        compiler_params=_params("arbitrary"))(*args)


def _all_reduce_small(name, packed):
    n_dev = 8

    def body(p_ref, o_ref, slots, send_sem, recv_sem):
        x, y, c, _ = _position()
        me = 4 * x + 2 * y + c

        def peer(k):
            return (1 - x if k & 4 else x, 1 - y if k & 2 else y, 1 - c if k & 1 else c)

        def logical(k):
            px, py, pc = peer(k)
            return 4 * px + 2 * py + pc

        slots[me] = p_ref[...]
        sends = [pltpu.make_async_remote_copy(
            src_ref=p_ref, dst_ref=slots.at[me], send_sem=send_sem.at[k], recv_sem=recv_sem.at[k],
            device_id=peer(k), device_id_type=MESH) for k in range(1, n_dev)]
        for cp in sends:
            cp.start()
        for k in range(1, n_dev):
            pltpu.make_async_remote_copy(
                src_ref=p_ref, dst_ref=slots.at[logical(k)], send_sem=send_sem.at[k], recv_sem=recv_sem.at[k],
                device_id=peer(k), device_id_type=MESH).wait_recv()
        for cp in sends:
            cp.wait_send()
        total = slots[0]
        for d in range(1, n_dev):
            total = total + slots[d]
        o_ref[...] = total

    vm = pl.BlockSpec(memory_space=pltpu.VMEM)
    return pl.pallas_call(
        body, name=name, in_specs=[vm], out_specs=vm, out_shape=jax.ShapeDtypeStruct(packed.shape, F32),
        scratch_shapes=[pltpu.VMEM((n_dev,) + packed.shape, F32), pltpu.SemaphoreType.DMA((n_dev,)),
                        pltpu.SemaphoreType.DMA((n_dev,))],
    )(packed)


def _adamw(name, w, g, m, v):
    rows, cols = w.shape
    tr = _row_tile(rows)

    def body(w_ref, g_ref, m_ref, v_ref, d_ref, mo_ref, vo_ref):
        gv = g_ref[...]
        m_new = ADAM_B1 * m_ref[...] + (1.0 - ADAM_B1) * gv
        v_new = ADAM_B2 * v_ref[...] + (1.0 - ADAM_B2) * (gv * gv)
        m_hat = m_new / (1.0 - ADAM_B1 ** ADAM_STEP)
        v_hat = v_new / (1.0 - ADAM_B2 ** ADAM_STEP)
        d_ref[...] = -ADAM_LR * (m_hat / (jnp.sqrt(v_hat) + ADAM_EPS) + ADAM_WD * w_ref[...])
        mo_ref[...] = m_new
        vo_ref[...] = v_new

    spec = pl.BlockSpec((tr, cols), lambda i: (i, 0))
    shape = jax.ShapeDtypeStruct((rows, cols), F32)
    return pl.pallas_call(body, name=name, grid=(rows // tr,), in_specs=[spec] * 4, out_specs=[spec] * 3,
                          out_shape=[shape] * 3, compiler_params=_params("parallel"))(w, g, m, v)


def _pack_small(tree, extra=None):
    pieces = [tree[n].reshape(-1).astype(F32) for n in SMALL]
    if extra is not None:
        pieces.append(extra.reshape(1).astype(F32))
    flat = jnp.concatenate(pieces)
    return jnp.pad(flat, (0, SMALL_ROWS * LANES - flat.shape[0])).reshape(SMALL_ROWS, LANES)


def _unpack_small(packed, like):
    flat = packed.reshape(-1)
    out, off = {}, 0
    for n in SMALL:
        size = int(np.prod(like[n].shape))
        out[n] = flat[off:off + size].reshape(like[n].shape)
        off += size
    return out


def kernel(x, ev_w_in, ev_g_cq, ev_w_uq, ev_g_ckv, ev_w_ukv, ev_w_out, od_w_qkv, od_rel_bias, od_w_out, g_mix, g_ffn, w_gate, w_up, w_down, g_final, loss_target, m_ev_w_in, m_ev_g_cq, m_ev_w_uq, m_ev_g_ckv, m_ev_w_ukv, m_ev_w_out, m_od_w_qkv, m_od_rel_bias, m_od_w_out, m_g_mix, m_g_ffn, m_w_gate, m_w_up, m_w_down, m_g_final, v_ev_w_in, v_ev_g_cq, v_ev_w_uq, v_ev_g_ckv, v_ev_w_ukv, v_ev_w_out, v_od_w_qkv, v_od_rel_bias, v_od_w_out, v_g_mix, v_g_ffn, v_w_gate, v_w_up, v_w_down, v_g_final):
    w = dict(ev_w_in=ev_w_in, ev_g_cq=ev_g_cq, ev_w_uq=ev_w_uq, ev_g_ckv=ev_g_ckv, ev_w_ukv=ev_w_ukv, ev_w_out=ev_w_out,
             od_w_qkv=od_w_qkv, od_rel_bias=od_rel_bias, od_w_out=od_w_out, g_mix=g_mix, g_ffn=g_ffn, w_gate=w_gate,
             w_up=w_up, w_down=w_down, g_final=g_final)
    m = dict(ev_w_in=m_ev_w_in, ev_g_cq=m_ev_g_cq, ev_w_uq=m_ev_w_uq, ev_g_ckv=m_ev_g_ckv, ev_w_ukv=m_ev_w_ukv,
             ev_w_out=m_ev_w_out, od_w_qkv=m_od_w_qkv, od_rel_bias=m_od_rel_bias, od_w_out=m_od_w_out, g_mix=m_g_mix,
             g_ffn=m_g_ffn, w_gate=m_w_gate, w_up=m_w_up, w_down=m_w_down, g_final=m_g_final)
    v = dict(ev_w_in=v_ev_w_in, ev_g_cq=v_ev_g_cq, ev_w_uq=v_ev_w_uq, ev_g_ckv=v_ev_g_ckv, ev_w_ukv=v_ev_w_ukv,
             ev_w_out=v_ev_w_out, od_w_qkv=v_od_w_qkv, od_rel_bias=v_od_rel_bias, od_w_out=v_od_w_out, g_mix=v_g_mix,
             g_ffn=v_g_ffn, w_gate=v_w_gate, w_up=v_w_up, w_down=v_w_down, g_final=v_g_final)
    flat2d = lambda a: a.reshape(-1, a.shape[-1])
    for tree in (w, m, v):
        for n in TRANSPOSED:
            tree[n] = jnp.swapaxes(tree[n], 1, 2)

    pos = jnp.stack([2 * lax.axis_index("x") + lax.axis_index("y"), lax.axis_index("c")]).astype(jnp.int32)

    slots = {part: _cast_into_slot("cast_" + part, w[n], layer, pos) for part, n, layer in GRAD_PARTS}
    ex = _Exchanges(slots, pos, {n: w[n].shape for n in BIG})

    loss_local, grad_x, small = _local_step(x[0], loss_target[0], {n: w[n] for n in SMALL}, ex)

    grads = ex.finish()
    small_sum = _all_reduce_small("small_sum", _pack_small(small, loss_local[0, 0]))
    grads.update(_unpack_small(small_sum, w))

    delta, new_m, new_v = {}, {}, {}
    for n in BIG:
        d_, m_, v_ = _adamw("adamw_" + n, flat2d(w[n]), flat2d(grads[n]), flat2d(m[n]), flat2d(v[n]))
        delta[n], new_m[n], new_v[n] = d_.reshape(w[n].shape), m_.reshape(w[n].shape), v_.reshape(w[n].shape)
    d_, m_, v_ = _adamw("adamw_small", _pack_small(w), small_sum, _pack_small(m), _pack_small(v))
    delta.update(_unpack_small(d_, w))
    new_m.update(_unpack_small(m_, w))
    new_v.update(_unpack_small(v_, w))
    for tree in (grads, delta, new_m, new_v):
        for n in TRANSPOSED:
            tree[n] = jnp.swapaxes(tree[n], 1, 2)

    loss = small_sum.reshape(-1)[SMALL_SIZE]
    return (loss, grad_x[None], *[grads[n] for n in WEIGHTS], *[delta[n] for n in WEIGHTS],
            *[new_m[n] for n in WEIGHTS], *[new_v[n] for n in WEIGHTS])
```

```python
import functools

import jax
import jax.numpy as jnp
import numpy as np
from jax import lax
from jax.experimental import pallas as pl
from jax.experimental.pallas import tpu as pltpu

F32 = jnp.float32
BF16 = jnp.bfloat16

S = 2048
D = 1024
CHUNK = 64
MLA_H, MLA_NOPE, MLA_ROPE, MLA_V = 8, 64, 32, 64
Q_LORA, KV_LORA = 384, 256
ROPE_THETA = 10000.0
SB_H, SB_DIM = 8, 64
C_H, C_DIM = 16, 64
LEFT_CHUNKS = 8
REL_CLIP = 256
D_FF = 2816
EVEN_IN = 2208
RMS_EPS = 1e-6
ADAM_LR, ADAM_B1, ADAM_B2, ADAM_EPS, ADAM_WD, ADAM_STEP = 0.001, 0.9, 0.999, 1e-08, 0.01, 10

N_CHIPS = 4
FF_SHARD = D_FF // N_CHIPS
SCALE_A = (MLA_NOPE + MLA_ROPE) ** -0.5
SCALE_B = SB_DIM ** -0.5
SCALE_C = C_DIM ** -0.5
NEG = -1e30
LOG2_E = 1.4426950408889634

LANES = 128
MXU_W = 256
VMEM_LIMIT_BYTES = 56 * 1024 * 1024
TM = 512
QB = 512
BQ = 256

P_CQ, P_CKV, P_QB, P_KB, P_VB, P_KR = 0, 512, 768, 1280, 1792, 2304
P_IN = 2432
KR_LANE = 64
BAND_W = BQ + LEFT_CHUNKS * CHUNK
BAND_PAD = 512
TOEP_W = 1024


def _params(*sem):
    return pltpu.CompilerParams(dimension_semantics=sem, vmem_limit_bytes=VMEM_LIMIT_BYTES)


MESH = pl.DeviceIdType.MESH
ANY = pl.BlockSpec(memory_space=pl.ANY)


def _position():
    x, y, c = lax.axis_index("x"), lax.axis_index("y"), lax.axis_index("c")
    other_chips = [(1 - x, y), (x, 1 - y), (1 - x, 1 - y)]
    return x, y, c, other_chips


def _half_rows(c, half):
    return pl.ds(pl.multiple_of(c * half, 16), half)


def _remote(ref_src, ref_dst, send, recv, k, device):
    return pltpu.make_async_remote_copy(src_ref=ref_src, dst_ref=ref_dst, send_sem=send.at[k], recv_sem=recv.at[k],
                                        device_id=device, device_id_type=MESH)


class _Carry:
    def __init__(self):
        self.operands, self.aliased, self.fresh = [], [], []
        self.n_sems = 0
        self.starts, self.finishes, self.on_done = [], [], []

    def operand(self, arr, aliased):
        for i, a in enumerate(self.operands):
            if a is arr:
                return i
        self.operands.append(arr)
        self.aliased.append(aliased)
        return len(self.operands) - 1

    def result(self, shape, dtype):
        self.fresh.append(jax.ShapeDtypeStruct(shape, dtype))
        return len(self.fresh) - 1

    def sems(self, k):
        base = self.n_sems
        self.n_sems += k
        return base

    def done(self, results):
        aliased, fresh = results
        for f in self.on_done:
            f(aliased, fresh)


def _carrier_call(body, *, name, grid, in_specs, out_specs, out_shape, args, sem, scratch_shapes=(), carry=None):
    in_specs, out_specs, out_shape, scratch = list(in_specs), list(out_specs), list(out_shape), list(scratch_shapes)
    if carry is None:
        res = pl.pallas_call(body, name=name, grid=grid, in_specs=in_specs, out_specs=out_specs, out_shape=out_shape,
                             scratch_shapes=scratch, compiler_params=_params(*sem))(*args)
        return list(res), None
    ops = carry.operands
    alias_idx = [i for i, a in enumerate(carry.aliased) if a]
    c_shapes = [jax.ShapeDtypeStruct(ops[i].shape, ops[i].dtype) for i in alias_idx] + carry.fresh
    n_in, n_out, n_scr = len(args), len(out_shape), len(scratch)

    def wrapped(*refs):
        ins, c_ins = refs[:n_in], refs[n_in:n_in + len(ops)]
        o0 = n_in + len(ops)
        outs, c_outs = refs[o0:o0 + n_out], refs[o0 + n_out:o0 + n_out + len(c_shapes)]
        s0 = o0 + n_out + len(c_shapes)
        scr, send, recv = refs[s0:s0 + n_scr], refs[s0 + n_scr], refs[s0 + n_scr + 1]
        use = list(c_ins)
        for k, i in enumerate(alias_idx):
            use[i] = c_outs[k]
        fresh = c_outs[len(alias_idx):]

        def run(steps):
            for step in steps:
                step(use, fresh, send, recv)

        if not grid:
            run(carry.starts)
            if body is not None:
                body(*ins, *outs, *scr)
            run(carry.finishes)
            return
        ids = [pl.program_id(a) for a in range(len(grid))]
        first = functools.reduce(jnp.logical_and, [i == 0 for i in ids])
        last = functools.reduce(jnp.logical_and, [i == g - 1 for i, g in zip(ids, grid)])

        @pl.when(first)
        def _():
            run(carry.starts)

        body(*ins, *outs, *scr)

        @pl.when(last)
        def _():
            run(carry.finishes)

    res = pl.pallas_call(
        wrapped, name=name, grid=grid, in_specs=in_specs + [ANY] * len(ops), out_specs=out_specs + [ANY] * len(c_shapes),
        out_shape=out_shape + c_shapes,
        scratch_shapes=scratch + [pltpu.SemaphoreType.DMA((carry.n_sems,)), pltpu.SemaphoreType.DMA((carry.n_sems,))],
        input_output_aliases={n_in + i: n_out + k for k, i in enumerate(alias_idx)},
        compiler_params=_params(*(("arbitrary",) * len(grid))),
    )(*args, *ops)
    res = list(res)
    c_res = res[n_out:]
    return res[:n_out], ({i: c_res[k] for k, i in enumerate(alias_idx)}, c_res[len(alias_idx):])


_DIMS = {"nn": (((1,), (0,)), ((), ())), "nt": (((1,), (1,)), ((), ())), "tn": (((0,), (0,)), ((), ()))}


def _dot(a, b, kind="nn"):
    return lax.dot_general(a, b, _DIMS[kind], preferred_element_type=F32)


def _iota(shape, dim):
    return lax.broadcasted_iota(jnp.int32, shape, dim)


def _sigmoid(x):
    return 1.0 / (1.0 + jnp.exp(-x))


def _split_dot(x, tri):
    hi = x.astype(BF16)
    lo = (x - hi.astype(F32)).astype(BF16)
    return _dot(hi, tri) + _dot(lo, tri)


def _running_sum(x, tri, reverse):
    n = x.shape[1] // MXU_W
    blocks = [x[:, b * MXU_W:(b + 1) * MXU_W] for b in range(n)]
    out = [None] * n
    carry = None
    for b in (range(n - 1, -1, -1) if reverse else range(n)):
        part = _split_dot(blocks[b], tri)
        out[b] = part if carry is None else part + carry
        total = jnp.sum(blocks[b], axis=-1, keepdims=True)
        carry = total if carry is None else carry + total
    return (jnp.concatenate(out, axis=1) if n > 1 else out[0]), carry


def _mm(name, a, b, *, kind, grid, a_spec, b_spec, o_spec, out_shape, out_dtype, acc_shape, resid=None, r_spec=None,
        carry=None):
    nk = grid[-1]
    has_r = resid is not None

    def body(*refs):
        a_ref, b_ref = refs[0], refs[1]
        r_ref = refs[2] if has_r else None
        o_ref = refs[2 + has_r]
        part = _dot(a_ref[...].astype(BF16), b_ref[...].astype(BF16), kind)

        def finish(total):
            if has_r:
                total = total + r_ref[...].astype(F32)
            o_ref[...] = total.astype(out_dtype)

        if nk == 1:
            finish(part)
        else:
            acc_ref = refs[3 + has_r]
            k = pl.program_id(len(grid) - 1)

            @pl.when(k == 0)
            def _():
                acc_ref[...] = part

            @pl.when(k > 0)
            def _():
                acc_ref[...] += part

            @pl.when(k == nk - 1)
            def _():
                finish(acc_ref[...])

    in_specs = [a_spec, b_spec] + ([r_spec] if has_r else [])
    args = (a, b) + ((resid,) if has_r else ())
    sem = ("parallel",) * (len(grid) - 1) + ("arbitrary",)
    res, copies = _carrier_call(
        body, name=name, grid=grid, in_specs=in_specs, out_specs=[o_spec],
        out_shape=[jax.ShapeDtypeStruct(out_shape, out_dtype)],
        scratch_shapes=[pltpu.VMEM(acc_shape, F32)] if nk > 1 else [], args=args, sem=sem, carry=carry)
    if carry is not None:
        carry.done(copies)
    return res[0]


def _rms_fwd(name, x, g, col_block=0):
    c = g.shape[1]

    def body(x_ref, g_ref, u_ref):
        xv = x_ref[...]
        r = lax.rsqrt(jnp.mean(xv * xv, axis=-1, keepdims=True) + RMS_EPS)
        u_ref[...] = (xv * r * g_ref[...]).astype(BF16)

    return pl.pallas_call(
        body, name=name, grid=(S // TM,),
        in_specs=[pl.BlockSpec((TM, c), lambda i: (i, col_block)), pl.BlockSpec((1, c), lambda i: (0, 0))],
        out_specs=pl.BlockSpec((TM, c), lambda i: (i, 0)),
        out_shape=jax.ShapeDtypeStruct((S, c), BF16),
        compiler_params=_params("parallel"),
    )(x, g)


def _rms_bwd(name, dy, x, g, resid, carry=None):
    def body(dy_ref, x_ref, g_ref, r_ref, dx_ref, dg_ref):
        i = pl.program_id(0)
        xv = x_ref[...]
        r = lax.rsqrt(jnp.mean(xv * xv, axis=-1, keepdims=True) + RMS_EPS)
        xh = xv * r
        dyv = dy_ref[...]
        dxh = dyv * g_ref[...]
        dx_ref[...] = r_ref[...] + r * (dxh - xh * jnp.mean(dxh * xh, axis=-1, keepdims=True))
        part = jnp.sum(dyv * xh, axis=0, keepdims=True)

        @pl.when(i == 0)
        def _():
            dg_ref[...] = part

        @pl.when(i > 0)
        def _():
            dg_ref[...] += part

    row = pl.BlockSpec((TM, D), lambda i: (i, 0))
    vec = pl.BlockSpec((1, D), lambda i: (0, 0))
    res, copies = _carrier_call(
        body, name=name, grid=(S // TM,), in_specs=[row, row, vec, row], out_specs=[row, vec],
        out_shape=[jax.ShapeDtypeStruct((S, D), F32), jax.ShapeDtypeStruct((1, D), F32)],
        args=(dy, x, g, resid), sem=("arbitrary",), carry=carry)
    if carry is not None:
        carry.done(copies)
    return res


def _loss_bwd(name, h, g, tgt):
    def body(h_ref, g_ref, t_ref, loss_ref, dh_ref, dg_ref):
        i = pl.program_id(0)
        xv = h_ref[...]
        gv = g_ref[...]
        r = lax.rsqrt(jnp.mean(xv * xv, axis=-1, keepdims=True) + RMS_EPS)
        xh = xv * r
        diff = xh * gv - t_ref[...]
        part_loss = 0.5 * jnp.sum(jnp.sum(diff * diff, axis=-1, keepdims=True) * (1.0 / D), axis=0, keepdims=True)
        dy = diff * (1.0 / D)
        dxh = dy * gv
        dh_ref[...] = r * (dxh - xh * jnp.mean(dxh * xh, axis=-1, keepdims=True))
        part_g = jnp.sum(dy * xh, axis=0, keepdims=True)

        @pl.when(i == 0)
        def _():
            dg_ref[...] = part_g
            loss_ref[...] = jnp.broadcast_to(part_loss, (1, LANES))

        @pl.when(i > 0)
        def _():
            dg_ref[...] += part_g
            loss_ref[...] += jnp.broadcast_to(part_loss, (1, LANES))

    row = pl.BlockSpec((TM, D), lambda i: (i, 0))
    vec = pl.BlockSpec((1, D), lambda i: (0, 0))
    return pl.pallas_call(
        body, name=name, grid=(S // TM,), in_specs=[row, vec, row],
        out_specs=[pl.BlockSpec((1, LANES), lambda i: (0, 0)), row, vec],
        out_shape=[jax.ShapeDtypeStruct((1, LANES), F32), jax.ShapeDtypeStruct((S, D), F32),
                   jax.ShapeDtypeStruct((1, D), F32)],
        compiler_params=_params("arbitrary"),
    )(h, g, tgt)


def _ffn_fwd(name, h, g, wg, wu, wd, carry=None):
    def body(h_ref, g_ref, wg_ref, wu_ref, wd_ref, o_ref, gate_ref, up_ref, u_scr):
        s = pl.program_id(1)

        @pl.when(s == 0)
        def _():
            xv = h_ref[...]
            r = lax.rsqrt(jnp.mean(xv * xv, axis=-1, keepdims=True) + RMS_EPS)
            u_scr[...] = (xv * r * g_ref[...]).astype(BF16)
            o_ref[...] = xv

        u = u_scr[...]
        gate = _dot(u, wg_ref[...], "nt")
        up = _dot(u, wu_ref[...], "nt")
        act = gate * _sigmoid(gate) * up
        o_ref[...] += _dot(act.astype(BF16), wd_ref[...])
        gate_ref[...] = gate.astype(BF16)
        up_ref[...] = up.astype(BF16)

    row = pl.BlockSpec((TM, D), lambda i, s: (i, 0))
    hid = pl.BlockSpec((None, TM, FF_SHARD), lambda i, s: (s, i, 0))
    return _carrier_call(
        body, name=name, grid=(S // TM, N_CHIPS),
        in_specs=[row, pl.BlockSpec((1, D), lambda i, s: (0, 0))]
        + [pl.BlockSpec((None, FF_SHARD, D), lambda i, s: (s, 0, 0))] * 3,
        out_specs=[row, hid, hid],
        out_shape=[jax.ShapeDtypeStruct((S, D), F32), jax.ShapeDtypeStruct((N_CHIPS, S, FF_SHARD), BF16),
                   jax.ShapeDtypeStruct((N_CHIPS, S, FF_SHARD), BF16)],
        scratch_shapes=[pltpu.VMEM((TM, D), BF16)], args=(h, g, wg, wu, wd), sem=("parallel", "arbitrary"), carry=carry)


def _ffn_bwd(name, dh, h, g, gate, up, wg, wu, wd):
    def body(dh_ref, h_ref, g_ref, gate_ref, up_ref, wg_ref, wu_ref, wd_ref,
             dhin_ref, dg_ref, u_ref, dgate_ref, dup_ref, act_ref, dhb_scr, du_scr):
        i = pl.program_id(0)
        s = pl.program_id(1)

        @pl.when(s == 0)
        def _():
            xv = h_ref[...]
            r = lax.rsqrt(jnp.mean(xv * xv, axis=-1, keepdims=True) + RMS_EPS)
            u_ref[...] = (xv * r * g_ref[...]).astype(BF16)
            dhb_scr[...] = dh_ref[...].astype(BF16)
            du_scr[...] = jnp.zeros_like(du_scr)

        dact = _dot(dhb_scr[...], wd_ref[...], "nt")
        gv = gate_ref[...].astype(F32)
        uv = up_ref[...].astype(F32)
        sig = _sigmoid(gv)
        sil = gv * sig
        dup = dact * sil
        dgate = dact * uv * (sig * (1.0 + gv * (1.0 - sig)))
        dgb = dgate.astype(BF16)
        dub = dup.astype(BF16)
        act_ref[...] = (sil * uv).astype(BF16)
        dgate_ref[...] = dgb
        dup_ref[...] = dub
        du_scr[...] += _dot(dgb, wg_ref[...]) + _dot(dub, wu_ref[...])

        @pl.when(s == N_CHIPS - 1)
        def _():
            xv = h_ref[...]
            r = lax.rsqrt(jnp.mean(xv * xv, axis=-1, keepdims=True) + RMS_EPS)
            xh = xv * r
            du = du_scr[...]
            dxh = du * g_ref[...]
            dhin_ref[...] = dh_ref[...] + r * (dxh - xh * jnp.mean(dxh * xh, axis=-1, keepdims=True))
            part = jnp.sum(du * xh, axis=0, keepdims=True)

            @pl.when(i == 0)
            def _():
                dg_ref[...] = part

            @pl.when(i > 0)
            def _():
                dg_ref[...] += part

    row = pl.BlockSpec((TM, D), lambda i, s: (i, 0))
    vec = pl.BlockSpec((1, D), lambda i, s: (0, 0))
    hid = pl.BlockSpec((None, TM, FF_SHARD), lambda i, s: (s, i, 0))
    hid_shape = jax.ShapeDtypeStruct((N_CHIPS, S, FF_SHARD), BF16)
    return pl.pallas_call(
        body, name=name, grid=(S // TM, N_CHIPS),
        in_specs=[row, row, vec, hid, hid] + [pl.BlockSpec((None, FF_SHARD, D), lambda i, s: (s, 0, 0))] * 3,
        out_specs=[row, vec, row, hid, hid, hid],
        out_shape=[jax.ShapeDtypeStruct((S, D), F32), jax.ShapeDtypeStruct((1, D), F32),
                   jax.ShapeDtypeStruct((S, D), BF16), hid_shape, hid_shape, hid_shape],
        scratch_shapes=[pltpu.VMEM((TM, D), BF16), pltpu.VMEM((TM, D), F32)],
        compiler_params=_params("arbitrary", "arbitrary"),
    )(dh, h, g, gate, up, wg, wu, wd)


def _ffn_wgrads(name, u, dgate, dup, act, dh):
    nk = S // TM

    def body(u_ref, dh_ref, dgate_ref, dup_ref, act_ref, dg_ref, du_ref, dd_ref, acc_g, acc_u, acc_d):
        k = pl.program_id(1)
        u = u_ref[...]
        parts = (_dot(dgate_ref[...], u, "tn"), _dot(dup_ref[...], u, "tn"),
                 _dot(act_ref[...], dh_ref[...].astype(BF16), "tn"))
        accs = (acc_g, acc_u, acc_d)

        @pl.when(k == 0)
        def _():
            for acc, part in zip(accs, parts):
                acc[...] = part

        @pl.when(k > 0)
        def _():
            for acc, part in zip(accs, parts):
                acc[...] += part

        @pl.when(k == nk - 1)
        def _():
            for out, acc in zip((dg_ref, du_ref, dd_ref), accs):
                out[...] = acc[...].astype(BF16)

    tok = pl.BlockSpec((TM, D), lambda s, k: (k, 0))
    hid = pl.BlockSpec((None, TM, FF_SHARD), lambda s, k: (s, k, 0))
    out = pl.BlockSpec((None, FF_SHARD, D), lambda s, k: (s, 0, 0))
    shape = jax.ShapeDtypeStruct((N_CHIPS, FF_SHARD, D), BF16)
    return pl.pallas_call(
        body, name=name, grid=(N_CHIPS, nk), in_specs=[tok, tok, hid, hid, hid], out_specs=[out, out, out],
        out_shape=[shape, shape, shape], scratch_shapes=[pltpu.VMEM((FF_SHARD, D), F32)] * 3,
        compiler_params=_params("parallel", "arbitrary"))(u, dh, dgate, dup, act)


def _rope_tables():
    pos = jnp.arange(S, dtype=F32)
    inv = ROPE_THETA ** (-jnp.arange(0, MLA_ROPE, 2, dtype=F32) / MLA_ROPE)
    ang = pos[:, None] * inv[None, :]
    half = MLA_ROPE // 2
    cos = jnp.cos(ang)
    sin = jnp.sin(ang)
    one = jnp.ones((S, KR_LANE), F32)
    zero = jnp.zeros((S, KR_LANE), F32)
    tail_one = jnp.ones((S, LANES - KR_LANE - MLA_ROPE), F32)
    tail_zero = jnp.zeros((S, LANES - KR_LANE - MLA_ROPE), F32)
    cos_t = jnp.concatenate([one, cos, cos, tail_one], axis=1)
    sin_t = jnp.concatenate([zero, -sin, sin, tail_zero], axis=1)
    assert cos_t.shape == (S, LANES) and half * 2 == MLA_ROPE
    return cos_t, sin_t


def _rope(x, cos_t, sin_t, sign):
    n = x.shape[1] // LANES
    half = MLA_ROPE // 2
    lane = _iota(x.shape, 1) & (LANES - 1)
    first = (lane >= KR_LANE) & (lane < KR_LANE + half)
    swapped = jnp.where(first, pltpu.roll(x, x.shape[1] - half, 1), pltpu.roll(x, half, 1))
    c = jnp.tile(cos_t, (1, n)) if n > 1 else cos_t
    s = jnp.tile(sin_t, (1, n)) if n > 1 else sin_t
    return x * c + swapped * (s * sign)


def _mla_prep_fwd(name, proj, g_cq, g_ckv, w_uq, w_uk, w_uv, cos_t, sin_t):
    nh = MLA_H * LANES

    def body(cq_ref, ckv_ref, kr_ref, gq_ref, gkv_ref, wq_ref, wk_ref, wv_ref, cos_ref, sin_ref,
             qa_ref, ka_ref, va_ref):
        cos_v, sin_v = cos_ref[...], sin_ref[...]
        cq = cq_ref[...]
        r = lax.rsqrt(jnp.mean(cq * cq, axis=-1, keepdims=True) + RMS_EPS)
        cqn = (cq * r * gq_ref[...]).astype(BF16)
        qa_ref[...] = _rope(_dot(cqn, wq_ref[...]), cos_v, sin_v, 1.0).astype(BF16)
        ckv = ckv_ref[...]
        r = lax.rsqrt(jnp.mean(ckv * ckv, axis=-1, keepdims=True) + RMS_EPS)
        ckvn = (ckv * r * gkv_ref[...]).astype(BF16)
        lane = _iota((TM, LANES), 1)
        rot = (lane >= KR_LANE) & (lane < KR_LANE + MLA_ROPE)
        kr = jnp.where(rot, _rope(kr_ref[...], cos_v, sin_v, 1.0), 0.0)
        ka_ref[...] = (_dot(ckvn, wk_ref[...]) + jnp.tile(kr, (1, MLA_H))).astype(BF16)
        va_ref[...] = _dot(ckvn, wv_ref[...]).astype(BF16)

    full = lambda shape: pl.BlockSpec(shape, lambda i: (0, 0))
    return pl.pallas_call(
        body, name=name, grid=(S // TM,),
        in_specs=[pl.BlockSpec((TM, Q_LORA), lambda i: (i, P_CQ // Q_LORA)),
                  pl.BlockSpec((TM, KV_LORA), lambda i: (i, P_CKV // KV_LORA)),
                  pl.BlockSpec((TM, LANES), lambda i: (i, P_KR // LANES)),
                  full((1, Q_LORA)), full((1, KV_LORA)), full((Q_LORA, nh)), full((KV_LORA, nh)),
                  full((KV_LORA, MLA_H * MLA_V)),
                  pl.BlockSpec((TM, LANES), lambda i: (i, 0)), pl.BlockSpec((TM, LANES), lambda i: (i, 0))],
        out_specs=[pl.BlockSpec((TM, nh), lambda i: (i, 0)), pl.BlockSpec((TM, nh), lambda i: (i, 0)),
                   pl.BlockSpec((TM, MLA_H * MLA_V), lambda i: (i, 0))],
        out_shape=[jax.ShapeDtypeStruct((S, nh), BF16), jax.ShapeDtypeStruct((S, nh), BF16),
                   jax.ShapeDtypeStruct((S, MLA_H * MLA_V), BF16)],
        compiler_params=_params("parallel"),
    )(proj, proj, proj, g_cq, g_ckv, w_uq, w_uk, w_uv, cos_t, sin_t)


def _mla_prep_bwd(name, dqa, dka, dva, proj, g_cq, g_ckv, w_uq, w_uk, w_uv, cos_t, sin_t):
    nh = MLA_H * LANES

    def body(dqa_ref, dka_ref, dva_ref, cq_ref, ckv_ref, gq_ref, gkv_ref, wq_ref, wk_ref, wv_ref, cos_ref, sin_ref,
             dcq_ref, dckv_ref, dkr_ref, dwq_ref, dwk_ref, dwv_ref, dgq_ref, dgkv_ref):
        i = pl.program_id(0)
        cos_v, sin_v = cos_ref[...], sin_ref[...]

        def norm_bwd(x, g, dn):
            r = lax.rsqrt(jnp.mean(x * x, axis=-1, keepdims=True) + RMS_EPS)
            xh = x * r
            dxh = dn * g
            dx = r * (dxh - xh * jnp.mean(dxh * xh, axis=-1, keepdims=True))
            return dx, jnp.sum(dn * xh, axis=0, keepdims=True), (xh * g).astype(BF16)

        dq = _rope(dqa_ref[...], cos_v, sin_v, -1.0).astype(BF16)
        dcqn = _dot(dq, wq_ref[...], "nt")
        dcq, dgq, cqn = norm_bwd(cq_ref[...], gq_ref[...], dcqn)
        dcq_ref[...] = dcq.astype(BF16)
        dwq = _dot(cqn, dq, "tn")

        dka = dka_ref[...]
        dkab = dka.astype(BF16)
        dvab = dva_ref[...].astype(BF16)
        dckvn = _dot(dkab, wk_ref[...], "nt") + _dot(dvab, wv_ref[...], "nt")
        dckv, dgkv, ckvn = norm_bwd(ckv_ref[...], gkv_ref[...], dckvn)
        dckv_ref[...] = dckv.astype(BF16)
        dwk = _dot(ckvn, dkab, "tn")
        dwv = _dot(ckvn, dvab, "tn")

        fold = dka[:, 0:LANES]
        for hh in range(1, MLA_H):
            fold = fold + dka[:, hh * LANES:(hh + 1) * LANES]
        lane = _iota((TM, LANES), 1)
        rot = (lane >= KR_LANE) & (lane < KR_LANE + MLA_ROPE)
        dkr = _rope(jnp.where(rot, fold, 0.0), cos_v, sin_v, -1.0)
        dkr_ref[...] = jnp.where(rot, dkr, 0.0).astype(BF16)

        @pl.when(i == 0)
        def _():
            dwq_ref[...] = dwq
            dwk_ref[...] = dwk
            dwv_ref[...] = dwv
            dgq_ref[...] = dgq
            dgkv_ref[...] = dgkv

        @pl.when(i > 0)
        def _():
            dwq_ref[...] += dwq
            dwk_ref[...] += dwk
            dwv_ref[...] += dwv
            dgq_ref[...] += dgq
            dgkv_ref[...] += dgkv

    full = lambda shape: pl.BlockSpec(shape, lambda i: (0, 0))
    rows = lambda c: pl.BlockSpec((TM, c), lambda i: (i, 0))
    nv = MLA_H * MLA_V
    return pl.pallas_call(
        body, name=name, grid=(S // TM,),
        in_specs=[rows(nh), rows(nh), rows(nv),
                  pl.BlockSpec((TM, Q_LORA), lambda i: (i, P_CQ // Q_LORA)),
                  pl.BlockSpec((TM, KV_LORA), lambda i: (i, P_CKV // KV_LORA)),
                  full((1, Q_LORA)), full((1, KV_LORA)), full((Q_LORA, nh)), full((KV_LORA, nh)), full((KV_LORA, nv)),
                  rows(LANES), rows(LANES)],
        out_specs=[rows(Q_LORA), rows(KV_LORA), rows(LANES), full((Q_LORA, nh)), full((KV_LORA, nh)),
                   full((KV_LORA, nv)), full((1, Q_LORA)), full((1, KV_LORA))],
        out_shape=[jax.ShapeDtypeStruct((S, Q_LORA), BF16), jax.ShapeDtypeStruct((S, KV_LORA), BF16),
                   jax.ShapeDtypeStruct((S, LANES), BF16), jax.ShapeDtypeStruct((Q_LORA, nh), F32),
                   jax.ShapeDtypeStruct((KV_LORA, nh), F32), jax.ShapeDtypeStruct((KV_LORA, nv), F32),
                   jax.ShapeDtypeStruct((1, Q_LORA), F32), jax.ShapeDtypeStruct((1, KV_LORA), F32)],
        compiler_params=_params("arbitrary"),
    )(dqa, dka, dva, proj, proj, g_cq, g_ckv, w_uq, w_uk, w_uv, cos_t, sin_t)


def _head_masks(dtype):
    lane = _iota((1, LANES), 1)
    return (lane < 64).astype(dtype), (lane >= 64).astype(dtype)


def _mla_fwd(name, qa, ka, va, carry=None):
    def body(q_ref, k_ref, v_ref, o_ref, lse_ref):
        m0b, m1b = _head_masks(BF16)
        lane = _iota((QB, LANES), 1)
        left = lane < 64

        def qblock(i, _):
            r0 = pl.multiple_of(i * QB, QB)
            qs = [q_ref[pl.ds(r0, QB), hh * LANES:(hh + 1) * LANES] for hh in range(2)]
            rowc = lax.shift_right_logical(r0 + _iota((QB, QB), 0), 6)

            def kv(kb, carry):
                ms, ls, acc = carry
                c0 = pl.multiple_of(kb * QB, QB)
                v = v_ref[pl.ds(c0, QB), :]
                ok = lax.shift_right_logical(c0 + _iota((QB, QB), 1), 6) <= rowc
                new_m, new_l, alphas = [], [], []
                pv = None
                for hh in range(2):
                    k = k_ref[pl.ds(c0, QB), hh * LANES:(hh + 1) * LANES]
                    s = jnp.where(ok, _dot(qs[hh], k, "nt") * SCALE_A, NEG)
                    mn = jnp.maximum(ms[hh], jnp.max(s, axis=-1, keepdims=True))
                    p = jnp.exp(s - mn)
                    a = jnp.exp(ms[hh] - mn)
                    new_m.append(mn)
                    new_l.append(a * ls[hh] + jnp.sum(p, axis=-1, keepdims=True))
                    alphas.append(a)
                    part = _dot(p.astype(BF16), v * (m0b if hh == 0 else m1b))
                    pv = part if pv is None else pv + part
                acc = acc * jnp.where(left, alphas[0], alphas[1]) + pv
                return tuple(new_m), tuple(new_l), acc

            init = ((jnp.full((QB, 1), NEG, F32),) * 2, (jnp.zeros((QB, 1), F32),) * 2, jnp.zeros((QB, LANES), F32))
            ms, ls, acc = lax.fori_loop(0, i + 1, kv, init)
            o_ref[pl.ds(r0, QB), :] = acc * jnp.where(left, 1.0 / ls[0], 1.0 / ls[1])
            lse_ref[pl.ds(r0, QB), :] = jnp.where(left, ms[0] + jnp.log(ls[0]), ms[1] + jnp.log(ls[1]))
            return 0

        lax.fori_loop(0, S // QB, qblock, 0)

    pair = lambda w: pl.BlockSpec((S, w), lambda p: (0, p))
    return _carrier_call(
        body, name=name, grid=(MLA_H // 2,), in_specs=[pair(2 * LANES), pair(2 * LANES), pair(LANES)],
        out_specs=[pair(LANES), pair(LANES)],
        out_shape=[jax.ShapeDtypeStruct((S, MLA_H * MLA_V), F32), jax.ShapeDtypeStruct((S, MLA_H * MLA_V), F32)],
        args=(qa, ka, va), sem=("parallel",), carry=carry)


def _mla_bwd(name, qa, ka, va, o, lse, do, do_block0, carry=None):
    def body(q_ref, k_ref, v_ref, o_ref, lse_ref, do_ref, dq_ref, dk_ref, dv_ref):
        m0f, m1f = _head_masks(F32)
        m0b, m1b = _head_masks(BF16)
        dk_ref[...] = jnp.zeros_like(dk_ref)
        dv_ref[...] = jnp.zeros_like(dv_ref)

        def qblock(i, _):
            r0 = pl.multiple_of(i * QB, QB)
            rows = pl.ds(r0, QB)
            do_f = do_ref[rows, :]
            prod = do_f * o_ref[rows, :]
            deltas = [jnp.sum(prod * m0f, axis=-1, keepdims=True), jnp.sum(prod * m1f, axis=-1, keepdims=True)]
            lse_v = lse_ref[rows, :]
            lses = [lse_v[:, 0:1], lse_v[:, 64:65]]
            dob = do_f.astype(BF16)
            dos = [dob * m0b, dob * m1b]
            qs = [q_ref[rows, hh * LANES:(hh + 1) * LANES] for hh in range(2)]
            rowc = lax.shift_right_logical(r0 + _iota((QB, QB), 0), 6)

            def kv(kb, dqs):
                c0 = pl.multiple_of(kb * QB, QB)
                cols = pl.ds(c0, QB)
                v = v_ref[cols, :]
                ok = lax.shift_right_logical(c0 + _iota((QB, QB), 1), 6) <= rowc
                out = []
                dv = None
                for hh in range(2):
                    k = k_ref[cols, hh * LANES:(hh + 1) * LANES]
                    s = _dot(qs[hh], k, "nt") * SCALE_A
                    p = jnp.where(ok, jnp.exp(s - lses[hh]), 0.0)
                    dp = _dot(dos[hh], v, "nt")
                    ds = (p * (dp - deltas[hh]) * SCALE_A).astype(BF16)
                    out.append(dqs[hh] + _dot(ds, k))
                    dk_ref[cols, hh * LANES:(hh + 1) * LANES] += _dot(ds, qs[hh], "tn")
                    part = _dot(p.astype(BF16), dos[hh], "tn")
                    dv = part if dv is None else dv + part
                dv_ref[cols, :] += dv
                return tuple(out)

            dqs = lax.fori_loop(0, i + 1, kv, (jnp.zeros((QB, LANES), F32),) * 2)
            for hh in range(2):
                dq_ref[rows, hh * LANES:(hh + 1) * LANES] = dqs[hh]
            return 0

        lax.fori_loop(0, S // QB, qblock, 0)

    pair = lambda w: pl.BlockSpec((S, w), lambda p: (0, p))
    return _carrier_call(
        body, name=name, grid=(MLA_H // 2,),
        in_specs=[pair(2 * LANES), pair(2 * LANES), pair(LANES), pair(LANES), pair(LANES),
                  pl.BlockSpec((S, LANES), lambda p: (0, do_block0 + p))],
        out_specs=[pair(2 * LANES), pair(2 * LANES), pair(LANES)],
        out_shape=[jax.ShapeDtypeStruct((S, MLA_H * LANES), F32), jax.ShapeDtypeStruct((S, MLA_H * LANES), F32),
                   jax.ShapeDtypeStruct((S, MLA_H * MLA_V), F32)],
        args=(qa, ka, va, o, lse, do), sem=("parallel",), carry=carry)


def _sb_weights(q_h, k, c, before, tri_suffix):
    z = _dot(q_h, k, "nt") * (SCALE_B * LOG2_E)
    sp = jnp.maximum(z, 0.0) + jnp.log(1.0 + jnp.exp2(-jnp.abs(z))) * LOG2_E
    log_keep = jnp.where(before, -sp, 0.0)
    to_the_right, total = _running_sum(log_keep, tri_suffix, True)
    w = jnp.where(before, jnp.exp2(z - sp + to_the_right + c), 0.0)
    return w, jnp.exp2(z - sp), total


def _sb_fwd(name, proj, carry=None):
    def body(q_ref, k_ref, v_ref, o_ref):
        m0b, m1b = _head_masks(BF16)
        tri_suffix = (_iota((MXU_W, MXU_W), 0) > _iota((MXU_W, MXU_W), 1)).astype(BF16)

        def qblock(i, _):
            r0 = pl.multiple_of(i * QB, QB)
            q = q_ref[pl.ds(r0, QB), :].astype(BF16)
            qs = [q * m0b, q * m1b]
            rowg = r0 + _iota((QB, QB), 0)

            def kv(step, carry):
                cs, acc = carry
                c0 = pl.multiple_of((i - step) * QB, QB)
                k = k_ref[pl.ds(c0, QB), :].astype(BF16)
                v = v_ref[pl.ds(c0, QB), :].astype(BF16)
                before = (c0 + _iota((QB, QB), 1)) < rowg
                new_c = []
                for hh in range(2):
                    w, _, tot = _sb_weights(qs[hh], k, cs[hh], before, tri_suffix)
                    new_c.append(cs[hh] + tot)
                    acc = acc + _dot(w.astype(BF16), v * (m0b if hh == 0 else m1b))
                return tuple(new_c), acc

            init = ((jnp.zeros((QB, 1), F32),) * 2, jnp.zeros((QB, LANES), F32))
            _, acc = lax.fori_loop(0, i + 1, kv, init)
            o_ref[pl.ds(r0, QB), :] = acc.astype(BF16)
            return 0

        lax.fori_loop(0, S // QB, qblock, 0)

    col = lambda base: pl.BlockSpec((S, LANES), lambda p: (0, base // LANES + p))
    return _carrier_call(
        body, name=name, grid=(SB_H // 2,), in_specs=[col(P_QB), col(P_KB), col(P_VB)],
        out_specs=[pl.BlockSpec((S, LANES), lambda p: (0, p))],
        out_shape=[jax.ShapeDtypeStruct((S, SB_H * SB_DIM), BF16)],
        args=(proj, proj, proj), sem=("parallel",), carry=carry)


def _sb_bwd(name, proj, do, do_block0, carry=None):
    nb = S // QB

    def body(q_ref, k_ref, v_ref, do_ref, dq_ref, dk_ref, dv_ref, sig_scr, dl_scr, dk_acc, dv_acc):
        m0b, m1b = _head_masks(BF16)
        tri_suffix = (_iota((MXU_W, MXU_W), 0) > _iota((MXU_W, MXU_W), 1)).astype(BF16)
        tri_prefix = (_iota((MXU_W, MXU_W), 0) < _iota((MXU_W, MXU_W), 1)).astype(BF16)
        dk_acc[...] = jnp.zeros_like(dk_acc)
        dv_acc[...] = jnp.zeros_like(dv_acc)

        def qblock(i, _):
            r0 = pl.multiple_of(i * QB, QB)
            rows = pl.ds(r0, QB)
            q = q_ref[rows, :].astype(BF16)
            qs = [q * m0b, q * m1b]
            dob = do_ref[rows, :].astype(BF16)
            dos = [dob * m0b, dob * m1b]
            rowg = r0 + _iota((QB, QB), 0)

            def sweep_left(step, cs):
                kb = i - step
                c0 = pl.multiple_of(kb * QB, QB)
                cols = pl.ds(c0, QB)
                k = k_ref[cols, :].astype(BF16)
                v = v_ref[cols, :].astype(BF16)
                before = (c0 + _iota((QB, QB), 1)) < rowg
                new_c = []
                dv = None
                for hh in range(2):
                    w, sig, tot = _sb_weights(qs[hh], k, cs[hh], before, tri_suffix)
                    new_c.append(cs[hh] + tot)
                    sig_scr[hh, kb] = sig
                    dl_scr[hh, kb] = _dot(dos[hh], v, "nt") * w
                    part = _dot(w.astype(BF16), dos[hh], "tn")
                    dv = part if dv is None else dv + part
                dv_acc[cols, :] += dv
                return tuple(new_c)

            lax.fori_loop(0, i + 1, sweep_left, (jnp.zeros((QB, 1), F32),) * 2)

            def sweep_right(kb, carry):
                ps, dq = carry
                c0 = pl.multiple_of(kb * QB, QB)
                cols = pl.ds(c0, QB)
                k = k_ref[cols, :].astype(BF16)
                before = (c0 + _iota((QB, QB), 1)) < rowg
                new_p = []
                dk = None
                for hh in range(2):
                    dl = dl_scr[hh, kb]
                    sig = sig_scr[hh, kb]
                    to_the_left, total = _running_sum(dl, tri_prefix, False)
                    earlier = to_the_left + ps[hh]
                    new_p.append(ps[hh] + total)
                    dz = (jnp.where(before, dl * (1.0 - sig) - earlier * sig, 0.0) * SCALE_B).astype(BF16)
                    dq = dq + _dot(dz, k * (m0b if hh == 0 else m1b))
                    part = _dot(dz, qs[hh], "tn")
                    dk = part if dk is None else dk + part
                dk_acc[cols, :] += dk
                return tuple(new_p), dq

            init = ((jnp.zeros((QB, 1), F32),) * 2, jnp.zeros((QB, LANES), F32))
            _, dq = lax.fori_loop(0, i + 1, sweep_right, init)
            dq_ref[rows, :] = dq.astype(BF16)
            return 0

        lax.fori_loop(0, nb, qblock, 0)
        dk_ref[...] = dk_acc[...].astype(BF16)
        dv_ref[...] = dv_acc[...].astype(BF16)

    col = lambda base: pl.BlockSpec((S, LANES), lambda p: (0, base // LANES + p))
    out = pl.BlockSpec((S, LANES), lambda p: (0, p))
    shape = jax.ShapeDtypeStruct((S, SB_H * SB_DIM), BF16)
    return _carrier_call(
        body, name=name, grid=(SB_H // 2,),
        in_specs=[col(P_QB), col(P_KB), col(P_VB), pl.BlockSpec((S, LANES), lambda p: (0, do_block0 + p))],
        out_specs=[out, out, out], out_shape=[shape, shape, shape],
        scratch_shapes=[pltpu.VMEM((2, nb, QB, QB), F32), pltpu.VMEM((2, nb, QB, QB), F32),
                        pltpu.VMEM((S, LANES), F32), pltpu.VMEM((S, LANES), F32)],
        args=(proj, proj, proj, do), sem=("parallel",), carry=carry)


def _band_row_index():
    j = np.arange(TOEP_W)
    rel = np.clip(LEFT_CHUNKS * CHUNK - j, -REL_CLIP, REL_CLIP) + REL_CLIP
    rel[BAND_W:] = 2 * REL_CLIP
    return rel.astype(np.int32)


def _band_tiles(r0_ref, q_ref, kpad, vpad, m, m0b, m1b, static_ok, bias):
    r0 = pl.multiple_of(m * BQ, BQ)
    q = q_ref[0, pl.ds(r0, BQ), :]
    kw = kpad[pl.ds(r0, BAND_W), :]
    vw = vpad[pl.ds(r0, BAND_W), :]
    ok = static_ok & ((r0 - BAND_PAD + _iota((BQ, BAND_W), 1)) >= 0)
    qs = [q * m0b, q * m1b]
    ps = []
    for hh in range(2):
        s = jnp.where(ok, _dot(qs[hh], kw, "nt") * SCALE_C + bias[hh], NEG)
        e = jnp.exp(s - jnp.max(s, axis=-1, keepdims=True))
        ps.append(e * (1.0 / jnp.sum(e, axis=-1, keepdims=True)))
    return r0, qs, kw, vw, ps


def _band_setup(qkv_ref, r0_ref, kpad, vpad):
    kpad[0:BAND_PAD, :] = jnp.zeros((BAND_PAD, LANES), BF16)
    vpad[0:BAND_PAD, :] = jnp.zeros((BAND_PAD, LANES), BF16)
    kpad[BAND_PAD:, :] = qkv_ref[1]
    vpad[BAND_PAD:, :] = qkv_ref[2]
    jc = lax.shift_right_logical(_iota((BQ, BAND_W), 1), 6)
    rc = lax.shift_right_logical(_iota((BQ, BAND_W), 0), 6)
    static_ok = (jc >= rc) & (jc <= rc + LEFT_CHUNKS)
    bias = []
    for hh in range(2):
        row = jnp.broadcast_to(r0_ref[hh:hh + 1, :], (BQ, TOEP_W))
        bias.append(pltpu.roll(row, 0, 1, stride=1, stride_axis=0)[:, :BAND_W])
    return static_ok, bias


def _band_fwd(name, qkv, r0, carry=None):
    def body(qkv_ref, r0_ref, o_ref, kpad, vpad):
        m0b, m1b = _head_masks(BF16)
        static_ok, bias = _band_setup(qkv_ref, r0_ref, kpad, vpad)

        def qblock(m, _):
            r0_, _, _, vw, ps = _band_tiles(r0_ref, qkv_ref, kpad, vpad, m, m0b, m1b, static_ok, bias)
            o = _dot(ps[0].astype(BF16), vw * m0b) + _dot(ps[1].astype(BF16), vw * m1b)
            o_ref[pl.ds(r0_, BQ), :] = o.astype(BF16)
            return 0

        lax.fori_loop(0, S // BQ, qblock, 0)

    return _carrier_call(
        body, name=name, grid=(C_H // 2,),
        in_specs=[pl.BlockSpec((3, S, LANES), lambda p: (0, 0, p)), pl.BlockSpec((None, 2, TOEP_W), lambda p: (p, 0, 0))],
        out_specs=[pl.BlockSpec((S, LANES), lambda p: (0, p))],
        out_shape=[jax.ShapeDtypeStruct((S, C_H * C_DIM), BF16)],
        scratch_shapes=[pltpu.VMEM((S + BAND_PAD, LANES), BF16), pltpu.VMEM((S + BAND_PAD, LANES), BF16)],
        args=(qkv, r0), sem=("parallel",), carry=carry)


def _band_bwd(name, qkv, r0, do, carry=None):
    def body(qkv_ref, r0_ref, do_ref, dqkv_ref, dr0_ref, kpad, vpad, dkpad, dvpad, db_acc):
        m0b, m1b = _head_masks(BF16)
        static_ok, bias = _band_setup(qkv_ref, r0_ref, kpad, vpad)
        dkpad[...] = jnp.zeros_like(dkpad)
        dvpad[...] = jnp.zeros_like(dvpad)
        db_acc[...] = jnp.zeros_like(db_acc)

        def qblock(m, _):
            r0_, qs, kw, vw, ps = _band_tiles(r0_ref, qkv_ref, kpad, vpad, m, m0b, m1b, static_ok, bias)
            dob = do_ref[pl.ds(r0_, BQ), :].astype(BF16)
            dos = [dob * m0b, dob * m1b]
            dq = None
            dk = None
            dv = None
            for hh in range(2):
                p = ps[hh]
                dp = _dot(dos[hh], vw, "nt")
                ds = p * (dp - jnp.sum(dp * p, axis=-1, keepdims=True))
                db_acc[hh, :, 0:BAND_W] += ds
                dsb = (ds * SCALE_C).astype(BF16)
                t = _dot(dsb, kw * (m0b if hh == 0 else m1b))
                dq = t if dq is None else dq + t
                t = _dot(dsb, qs[hh], "tn")
                dk = t if dk is None else dk + t
                t = _dot(p.astype(BF16), dos[hh], "tn")
                dv = t if dv is None else dv + t
            dqkv_ref[0, pl.ds(r0_, BQ), :] = dq.astype(BF16)
            dkpad[pl.ds(r0_, BAND_W), :] += dk
            dvpad[pl.ds(r0_, BAND_W), :] += dv
            return 0

        lax.fori_loop(0, S // BQ, qblock, 0)
        dqkv_ref[1] = dkpad[BAND_PAD:, :].astype(BF16)
        dqkv_ref[2] = dvpad[BAND_PAD:, :].astype(BF16)
        sub = _iota((8, TOEP_W), 0)
        for hh in range(2):
            folded = db_acc[hh, 0:8, :]
            for a in range(1, BQ // 8):
                folded = folded + pltpu.roll(db_acc[hh, 8 * a:8 * a + 8, :], TOEP_W - 8 * a, 1)
            for bit in range(3):
                moved = pltpu.roll(folded, TOEP_W - (1 << bit), 1)
                folded = jnp.where((sub & (1 << bit)) != 0, moved, folded)
            dr0_ref[hh:hh + 1, :] = jnp.sum(folded, axis=0, keepdims=True)

    return _carrier_call(
        body, name=name, grid=(C_H // 2,),
        in_specs=[pl.BlockSpec((3, S, LANES), lambda p: (0, 0, p)), pl.BlockSpec((None, 2, TOEP_W), lambda p: (p, 0, 0)),
                  pl.BlockSpec((S, LANES), lambda p: (0, p))],
        out_specs=[pl.BlockSpec((3, S, LANES), lambda p: (0, 0, p)), pl.BlockSpec((None, 2, TOEP_W), lambda p: (p, 0, 0))],
        out_shape=[jax.ShapeDtypeStruct((3, S, C_H * C_DIM), BF16), jax.ShapeDtypeStruct((C_H // 2, 2, TOEP_W), F32)],
        scratch_shapes=[pltpu.VMEM((S + BAND_PAD, LANES), BF16), pltpu.VMEM((S + BAND_PAD, LANES), BF16),
                        pltpu.VMEM((S + BAND_PAD, LANES), F32), pltpu.VMEM((S + BAND_PAD, LANES), F32),
                        pltpu.VMEM((2, BQ, TOEP_W), F32)],
        args=(qkv, r0, do), sem=("parallel",), carry=carry)


def _bias_table_grad(name, dr0):
    w_out = 5 * LANES

    def body(d_ref, o_ref):
        j = _iota((TOEP_W, w_out), 0)
        rel = jnp.clip(LEFT_CHUNKS * CHUNK - j, -REL_CLIP, REL_CLIP) + REL_CLIP
        rel = jnp.where(j >= BAND_W, 2 * REL_CLIP, rel)
        onehot = (rel == _iota((TOEP_W, w_out), 1)).astype(BF16)
        d = d_ref[...]
        hi = d.astype(BF16)
        mid = (d - hi.astype(F32))
        mid_b = mid.astype(BF16)
        lo = (mid - mid_b.astype(F32)).astype(BF16)
        o_ref[...] = _dot(hi, onehot) + _dot(mid_b, onehot) + _dot(lo, onehot)

    return pl.pallas_call(
        body, name=name, out_shape=jax.ShapeDtypeStruct((C_H, w_out), F32),
        in_specs=[pl.BlockSpec((C_H, TOEP_W), lambda: (0, 0))], out_specs=pl.BlockSpec((C_H, w_out), lambda: (0, 0)),
        grid=(),
    )(dr0)


def _carry_gather(cy, slots, names, ici, d2d):
    idx = [cy.operand(slots[n], True) for n in names]
    n = len(names)
    base_i = cy.sems(3 * n) if ici else 0
    base_d = cy.sems(3 * n) if d2d else 0

    def piece(refs, t, slot, cc):
        return refs[idx[t]].at[slot, _half_rows(cc, slots[names[t]].shape[1] // 2), :]

    def over_ici(refs, send, recv, arriving):
        x, y, c, chips = _position()
        out = []
        for t in range(n):
            for j in range(3):
                r = piece(refs, t, 2 * chips[j][0] + chips[j][1] if arriving else 2 * x + y, c)
                out.append(_remote(r, r, send, recv, base_i + 3 * t + j, (*chips[j], c)))
        return out

    def over_d2d(refs, send, recv, arriving):
        x, y, c, chips = _position()
        out = []
        for t in range(n):
            for j in range(3):
                r = piece(refs, t, 2 * chips[j][0] + chips[j][1], 1 - c if arriving else c)
                out.append(_remote(r, r, send, recv, base_d + 3 * t + j, (x, y, 1 - c)))
        return out

    def start_ici(refs, fresh, send, recv):
        for cp in over_ici(refs, send, recv, False):
            cp.start()

    def wait_ici(refs, fresh, send, recv):
        for cp in over_ici(refs, send, recv, True):
            cp.wait_recv()
        for cp in over_ici(refs, send, recv, False):
            cp.wait_send()

    def start_d2d(refs, fresh, send, recv):
        for cp in over_d2d(refs, send, recv, False):
            cp.start()

    def wait_d2d(refs, fresh, send, recv):
        for cp in over_d2d(refs, send, recv, True):
            cp.wait_recv()
        for cp in over_d2d(refs, send, recv, False):
            cp.wait_send()

    if ici and d2d:
        cy.starts.append(start_ici)
        cy.finishes += [wait_ici, start_d2d, wait_d2d]
    elif ici:
        cy.starts.append(start_ici)
        cy.finishes.append(wait_ici)
    else:
        cy.starts.append(start_d2d)
        cy.finishes.append(wait_d2d)

    def done(aliased, fresh):
        for t, name in enumerate(names):
            slots[name] = aliased[idx[t]]

    cy.on_done.append(done)


def _carry_chip_exchange(cy, sums, got, names):
    idx = [cy.operand(sums[n], False) for n in names]
    out = [cy.result((3,) + sums[n].shape[1:], BF16) for n in names]
    base = cy.sems(3 * len(names))

    def copies(refs, fresh, send, recv):
        x, y, c, chips = _position()
        return [_remote(refs[idx[t]].at[2 * chips[j][0] + chips[j][1]], fresh[out[t]].at[j], send, recv, base + 3 * t + j,
                        (*chips[j], c)) for t in range(len(names)) for j in range(3)]

    def start(refs, fresh, send, recv):
        for cp in copies(refs, fresh, send, recv):
            cp.start()

    def wait(refs, fresh, send, recv):
        for cp in copies(refs, fresh, send, recv):
            cp.wait()

    cy.starts.append(start)
    cy.finishes.append(wait)

    def done(aliased, fresh):
        for t, name in enumerate(names):
            got[name] = fresh[out[t]]

    cy.on_done.append(done)


def _run_carry(name, cy):
    _, res = _carrier_call(None, name=name, grid=(), in_specs=[], out_specs=[], out_shape=[], args=(), sem=(), carry=cy)
    cy.done(res)


FIRST_WEIGHTS = ("ev_w_in", "ev_w_uq", "ev_w_ukv")
WEIGHTS_A = ("ev_w_out", "w_gate0", "w_up0")
WEIGHTS_B = ("w_down0", "od_w_qkv", "od_w_out")
WEIGHTS_C = ("w_gate1", "w_up1")
WEIGHTS_D = ("w_down1",)
GRAD_GROUPS = {"ffn1": ("w_gate1", "w_up1", "w_down1"), "od": ("od_w_qkv", "od_w_out"),
               "ffn0": ("w_gate0", "w_up0", "w_down0"), "ev": ("ev_w_in", "ev_w_uq", "ev_w_ukv", "ev_w_out")}


def _carry_pair_exchange(cy, parts, theirs, names):
    idx = [cy.operand(parts[n], False) for n in names]
    out = [cy.result((N_CHIPS, parts[n].shape[1] // 2, parts[n].shape[2]), BF16) for n in names]
    base = cy.sems(len(names))

    def copies(refs, fresh, send, recv):
        x, y, c, _ = _position()
        return [_remote(refs[idx[t]].at[:, _half_rows(1 - c, parts[n].shape[1] // 2), :], fresh[out[t]], send, recv,
                        base + t, (x, y, 1 - c)) for t, n in enumerate(names)]

    cy.starts.append(lambda refs, fresh, send, recv: [cp.start() for cp in copies(refs, fresh, send, recv)])
    cy.finishes.append(lambda refs, fresh, send, recv: [cp.wait() for cp in copies(refs, fresh, send, recv)])

    def done(aliased, fresh):
        for t, name in enumerate(names):
            theirs[name] = fresh[out[t]]

    cy.on_done.append(done)


def _carry_sibling_exchange(cy, fulls, pieces):
    idx = [cy.operand(fulls[p], True) for p, _ in pieces]
    base = cy.sems(len(pieces))

    def copies(refs, send, recv, arriving):
        x, y, c, _ = _position()
        out = []
        for t, (p, layer) in enumerate(pieces):
            r = refs[idx[t]].at[layer, _half_rows(1 - c if arriving else c, fulls[p].shape[1] // 2), :]
            out.append(_remote(r, r, send, recv, base + t, (x, y, 1 - c)))
        return out

    def start(refs, fresh, send, recv):
        for cp in copies(refs, send, recv, False):
            cp.start()

    def wait(refs, fresh, send, recv):
        for cp in copies(refs, send, recv, True):
            cp.wait_recv()
        for cp in copies(refs, send, recv, False):
            cp.wait_send()

    cy.starts.append(start)
    cy.finishes.append(wait)

    def done(aliased, fresh):
        for t, (p, _) in enumerate(pieces):
            fulls[p] = aliased[idx[t]]

    cy.on_done.append(done)


RIDES = {
    "mla_attn": (("gather_ici", WEIGHTS_A),),
    "sb_attn": (("gather_d2d", WEIGHTS_A), ("gather_ici", WEIGHTS_B)),
    "ev_out": (("gather_d2d", WEIGHTS_B),),
    "ffn0": (("gather_ici", WEIGHTS_C),),
    "qkv": (("gather_d2d", WEIGHTS_C),),
    "band_attn": (("gather_ici", WEIGHTS_D),),
    "od_out": (("gather_d2d", WEIGHTS_D),),
    "od_out_bwd_w": (("pair", "ffn1"),),
    "band_attn_bwd": (("chips", "ffn1"),),
    "rms_mix1_bwd": (("pair", "od"),),
    "ev_out_bwd_w": (("pair", "ffn0"),),
    "mla_attn_bwd": (("chips", "od"), ("sibling", "ffn1")),
    "sb_attn_bwd": (("chips", "ffn0"), ("sibling", "od")),
    "proj_in_bwd_w": (("sibling", "ffn0"),),
    "proj_in_bwd_x": (("chips", "ev"),),
}


class _Exchanges:
    def __init__(self, slots, pos, shapes):
        self.slots, self.pos, self.shapes = dict(slots), pos, shapes
        self.parts, self.theirs, self.sums, self.got, self.fulls = {}, {}, {}, {}, {}

    def begin(self):
        cy = _Carry()
        _carry_gather(cy, self.slots, FIRST_WEIGHTS, True, True)
        _run_carry("gather_first", cy)

    def weights(self, *names):
        return [self.slots[n] for n in names]

    def _pair_sums(self, group):
        for n in GRAD_GROUPS[group]:
            if n not in self.sums:
                self.sums[n] = _pair_sum("pair_sum_" + n, self.parts[n], self.theirs[n], self.pos)

    def _chip_sums(self, group):
        for n in GRAD_GROUPS[group]:
            param, layer = PART_OF[n]
            self.fulls[param] = _chip_sum("chip_sum_" + n, self.sums[n], self.got[n], self.pos, layer,
                                          self.shapes[param], self.fulls.get(param))

    def carry(self, stage):
        cy = _Carry()
        for step, what in RIDES[stage]:
            if step == "gather_ici":
                _carry_gather(cy, self.slots, what, True, False)
            elif step == "gather_d2d":
                _carry_gather(cy, self.slots, what, False, True)
            elif step == "pair":
                _carry_pair_exchange(cy, self.parts, self.theirs, GRAD_GROUPS[what])
            elif step == "chips":
                self._pair_sums(what)
                _carry_chip_exchange(cy, self.sums, self.got, GRAD_GROUPS[what])
            elif step == "sibling":
                self._chip_sums(what)
                _carry_sibling_exchange(cy, self.fulls, [PART_OF[n] for n in GRAD_GROUPS[what]])
        return cy

    def grads(self, group, parts):
        self.parts.update(parts)
        if group == "ev":
            cy = _Carry()
            _carry_pair_exchange(cy, self.parts, self.theirs, GRAD_GROUPS[group])
            _run_carry("grads_pair_ev", cy)

    def finish(self):
        cy = _Carry()
        self._chip_sums("ev")
        _carry_sibling_exchange(cy, self.fulls, [PART_OF[n] for n in GRAD_GROUPS["ev"]])
        _run_carry("grads_sibling_ev", cy)
        return {n: self.fulls[n] for n in BIG}


class _NoExchanges:
    def __init__(self, slots):
        self.slots, self.parts = dict(slots), {}

    def begin(self):
        pass

    def weights(self, *names):
        return [self.slots[n] for n in names]

    def carry(self, stage):
        return None

    def grads(self, group, parts):
        self.parts.update(parts)


def _first_weights(w_in_s, w_uq_s, w_ukv_s):
    gw = {"ev_w_in": w_in_s, "ev_w_uq": w_uq_s, "ev_w_ukv": w_ukv_s}
    w_in = jnp.moveaxis(gw["ev_w_in"], 0, 1).reshape(D, EVEN_IN)
    z = lambda n: jnp.zeros((D, n), BF16)
    w_in_p = jnp.concatenate(
        [w_in[:, 0:384], z(128), w_in[:, 384:640], w_in[:, 672:2208], z(KR_LANE), w_in[:, 640:672],
         z(LANES - KR_LANE - MLA_ROPE)], axis=1)
    w_uq = jnp.moveaxis(gw["ev_w_uq"], 0, 1).reshape(Q_LORA, MLA_H, MLA_NOPE + MLA_ROPE)
    w_uq_p = jnp.concatenate([w_uq, jnp.zeros((Q_LORA, MLA_H, LANES - MLA_NOPE - MLA_ROPE), BF16)], axis=2)
    w_ukv = jnp.moveaxis(gw["ev_w_ukv"], 0, 1).reshape(KV_LORA, MLA_H, MLA_NOPE + MLA_V)
    w_uk_p = jnp.concatenate([w_ukv[:, :, :MLA_NOPE], jnp.zeros((KV_LORA, MLA_H, LANES - MLA_NOPE), BF16)], axis=2)
    return dict(
        w_in=w_in_p, w_uq=w_uq_p.reshape(Q_LORA, MLA_H * LANES), w_uk=w_uk_p.reshape(KV_LORA, MLA_H * LANES),
        w_uv=w_ukv[:, :, MLA_NOPE:].reshape(KV_LORA, MLA_H * MLA_V))


def _proj_mm(name, u, w_in):
    return _mm(name, u, w_in, kind="nn", grid=(S // TM, 1, 1),
               a_spec=pl.BlockSpec((TM, D), lambda i, j, k: (i, 0)), b_spec=pl.BlockSpec((D, P_IN), lambda i, j, k: (0, 0)),
               o_spec=pl.BlockSpec((TM, P_IN), lambda i, j, k: (i, 0)), out_shape=(S, P_IN), out_dtype=F32, acc_shape=None)


def _out_proj(name, o, w, resid, carry=None):
    return _mm(name, o, w, kind="nn", grid=(S // TM, 1, 1),
               a_spec=pl.BlockSpec((TM, D), lambda i, j, k: (i, 0)), b_spec=pl.BlockSpec((D, D), lambda i, j, k: (0, 0)),
               o_spec=pl.BlockSpec((TM, D), lambda i, j, k: (i, 0)), out_shape=(S, D), out_dtype=F32, acc_shape=None,
               resid=resid, r_spec=pl.BlockSpec((TM, D), lambda i, j, k: (i, 0)), carry=carry)


def _out_proj_bwd(name, dh, o, w, ex):
    d_o = _mm(name + "_x", dh, w, kind="nt", grid=(S // TM, 1, 1),
              a_spec=pl.BlockSpec((TM, D), lambda i, j, k: (i, 0)), b_spec=pl.BlockSpec((D, D), lambda i, j, k: (0, 0)),
              o_spec=pl.BlockSpec((TM, D), lambda i, j, k: (i, 0)), out_shape=(S, D), out_dtype=F32, acc_shape=None)
    d_w = _mm(name + "_w", o, dh, kind="tn", grid=(2, S // TM),
              a_spec=pl.BlockSpec((TM, TM), lambda j, k: (k, j)), b_spec=pl.BlockSpec((TM, D), lambda j, k: (k, 0)),
              o_spec=pl.BlockSpec((TM, D), lambda j, k: (j, 0)), out_shape=(D, D), out_dtype=BF16, acc_shape=(TM, D),
              carry=ex.carry(name + "_w"))
    return d_o, d_w


def _local_step(x, tgt, sm, ex):
    def riding(stage, fn, *args):
        cy = ex.carry(stage)
        res, copies = fn(stage, *args, carry=cy)
        if cy is not None:
            cy.done(copies)
        return res

    cos_t, sin_t = _rope_tables()
    g_mix, g_ffn = sm["g_mix"], sm["g_ffn"]
    r0 = sm["od_rel_bias"][0][:, _band_row_index()].reshape(C_H // 2, 2, TOEP_W)
    nt = 3

    ex.begin()
    w = _first_weights(*ex.weights(*FIRST_WEIGHTS))
    u0 = _rms_fwd("rms_mix0", x, g_mix[0:1])
    proj = _proj_mm("proj_in", u0, w["w_in"])
    qa, ka, va = _mla_prep_fwd("mla_prep", proj, sm["ev_g_cq"], sm["ev_g_ckv"], w["w_uq"], w["w_uk"], w["w_uv"], cos_t, sin_t)
    o_a, lse = riding("mla_attn", _mla_fwd, qa, ka, va)
    o_b, = riding("sb_attn", _sb_fwd, proj)
    o_ev = jnp.concatenate([o_a.astype(BF16), o_b], axis=1)
    w["ev_w_out"] = ex.weights("ev_w_out")[0].reshape(D, D)
    h1 = _out_proj("ev_out", o_ev, w["ev_w_out"], x, ex.carry("ev_out"))
    w["w_gate0"], w["w_up0"], w["w_down0"] = ex.weights("w_gate0", "w_up0", "w_down0")
    h2, gate0, up0 = riding("ffn0", _ffn_fwd, h1, g_ffn[0:1], w["w_gate0"], w["w_up0"], w["w_down0"])
    w["w_qkv"] = jnp.moveaxis(ex.weights("od_w_qkv")[0], 0, 1).reshape(D, nt * D)
    u2 = _rms_fwd("rms_mix1", h2, g_mix[1:2])
    qkv = _mm("qkv", u2, w["w_qkv"], kind="nn", grid=(S // TM, nt, 1),
              a_spec=pl.BlockSpec((TM, D), lambda i, t, k: (i, 0)), b_spec=pl.BlockSpec((D, D), lambda i, t, k: (0, t)),
              o_spec=pl.BlockSpec((None, TM, D), lambda i, t, k: (t, i, 0)),
              out_shape=(nt, S, D), out_dtype=BF16, acc_shape=None, carry=ex.carry("qkv"))
    o_od, = riding("band_attn", _band_fwd, qkv, r0)
    w["od_w_out"] = ex.weights("od_w_out")[0].reshape(D, D)
    h3 = _out_proj("od_out", o_od, w["od_w_out"], h2, ex.carry("od_out"))
    w["w_gate1"], w["w_up1"], w["w_down1"] = ex.weights("w_gate1", "w_up1", "w_down1")
    (h4, gate1, up1), _ = _ffn_fwd("ffn1", h3, g_ffn[1:2], w["w_gate1"], w["w_up1"], w["w_down1"])

    loss, dh4, dg_final = _loss_bwd("loss", h4, sm["g_final"].reshape(1, D), tgt)

    dh3, dg_ffn1, u3, dgate, dup, act = _ffn_bwd("ffn1_bwd", dh4, h3, g_ffn[1:2], gate1, up1,
                                                 w["w_gate1"], w["w_up1"], w["w_down1"])
    d_wg1, d_wu1, d_wd1 = _ffn_wgrads("ffn1_dw", u3, dgate, dup, act, dh4)
    ex.grads("ffn1", {"w_gate1": d_wg1, "w_up1": d_wu1, "w_down1": d_wd1})

    d_ood, d_w_od_out = _out_proj_bwd("od_out_bwd", dh3, o_od, w["od_w_out"], ex)
    dqkv, dr0 = riding("band_attn_bwd", _band_bwd, qkv, r0, d_ood)
    du2 = _mm("qkv_bwd_x", dqkv, w["w_qkv"], kind="nt", grid=(S // TM, nt),
              a_spec=pl.BlockSpec((None, TM, D), lambda i, t: (t, i, 0)), b_spec=pl.BlockSpec((D, D), lambda i, t: (0, t)),
              o_spec=pl.BlockSpec((TM, D), lambda i, t: (i, 0)), out_shape=(S, D), out_dtype=F32, acc_shape=(TM, D))
    d_w_qkv = _mm("qkv_bwd_w", u2, dqkv, kind="tn", grid=(nt, S // TM),
                  a_spec=pl.BlockSpec((TM, D), lambda t, k: (k, 0)), b_spec=pl.BlockSpec((None, TM, D), lambda t, k: (t, k, 0)),
                  o_spec=pl.BlockSpec((D, D), lambda t, k: (0, t)), out_shape=(D, nt * D), out_dtype=BF16, acc_shape=(D, D))
    shard_cols = lambda a: jnp.moveaxis(a.reshape(a.shape[0], N_CHIPS, a.shape[1] // N_CHIPS), 1, 0)
    ex.grads("od", {"od_w_qkv": shard_cols(d_w_qkv), "od_w_out": d_w_od_out.reshape(N_CHIPS, D // N_CHIPS, D)})
    dh2, dg_mix1 = _rms_bwd("rms_mix1_bwd", du2, h2, g_mix[1:2], dh3, carry=ex.carry("rms_mix1_bwd"))
    d_rel = _bias_table_grad("rel_bias_grad", dr0.reshape(C_H, TOEP_W))[:, :2 * REL_CLIP + 1]

    dh1, dg_ffn0, u1, dgate, dup, act = _ffn_bwd("ffn0_bwd", dh2, h1, g_ffn[0:1], gate0, up0,
                                                 w["w_gate0"], w["w_up0"], w["w_down0"])
    d_wg0, d_wu0, d_wd0 = _ffn_wgrads("ffn0_dw", u1, dgate, dup, act, dh2)
    ex.grads("ffn0", {"w_gate0": d_wg0, "w_up0": d_wu0, "w_down0": d_wd0})

    d_oev, d_w_ev_out = _out_proj_bwd("ev_out_bwd", dh1, o_ev, w["ev_w_out"], ex)
    dqa, dka, dva = riding("mla_attn_bwd", _mla_bwd, qa, ka, va, o_a, lse, d_oev, 0)
    dqb, dkb, dvb = riding("sb_attn_bwd", _sb_bwd, proj, d_oev, MLA_H * MLA_V // LANES)
    dcq, dckv, dkr, d_w_uq, d_w_uk, d_w_uv, dg_cq, dg_ckv = _mla_prep_bwd(
        "mla_prep_bwd", dqa, dka, dva, proj, sm["ev_g_cq"], sm["ev_g_ckv"], w["w_uq"], w["w_uk"], w["w_uv"], cos_t, sin_t)
    dproj = jnp.concatenate([dcq, jnp.zeros((S, LANES), BF16), dckv, dqb, dkb, dvb, dkr], axis=1)
    d_w_in_p = _mm("proj_in_bwd_w", u0, dproj, kind="tn", grid=(1, S // TM),
                   a_spec=pl.BlockSpec((TM, D), lambda j, k: (k, 0)), b_spec=pl.BlockSpec((TM, P_IN), lambda j, k: (k, 0)),
                   o_spec=pl.BlockSpec((D, P_IN), lambda j, k: (0, 0)), out_shape=(D, P_IN), out_dtype=BF16,
                   acc_shape=(D, P_IN), carry=ex.carry("proj_in_bwd_w"))
    d_w_in = jnp.concatenate([d_w_in_p[:, 0:384], d_w_in_p[:, 512:768],
                              d_w_in_p[:, P_KR + KR_LANE:P_KR + KR_LANE + MLA_ROPE], d_w_in_p[:, 768:2304]], axis=1)
    d_w_uq_std = d_w_uq.reshape(Q_LORA, MLA_H, LANES)[:, :, :MLA_NOPE + MLA_ROPE].reshape(Q_LORA, -1)
    d_w_ukv = jnp.concatenate([d_w_uk.reshape(KV_LORA, MLA_H, LANES)[:, :, :MLA_NOPE],
                               d_w_uv.reshape(KV_LORA, MLA_H, MLA_V)], axis=2).reshape(KV_LORA, -1)
    ex.grads("ev", {"ev_w_in": shard_cols(d_w_in), "ev_w_uq": shard_cols(d_w_uq_std.astype(BF16)),
                    "ev_w_ukv": shard_cols(d_w_ukv.astype(BF16)),
                    "ev_w_out": d_w_ev_out.reshape(N_CHIPS, D // N_CHIPS, D)})
    du0 = _mm("proj_in_bwd_x", dproj, w["w_in"], kind="nt", grid=(S // TM, 1, 1),
              a_spec=pl.BlockSpec((TM, P_IN), lambda i, j, k: (i, 0)), b_spec=pl.BlockSpec((D, P_IN), lambda i, j, k: (0, 0)),
              o_spec=pl.BlockSpec((TM, D), lambda i, j, k: (i, 0)), out_shape=(S, D), out_dtype=F32, acc_shape=None,
              carry=ex.carry("proj_in_bwd_x"))
    grad_x, dg_mix0 = _rms_bwd("rms_mix0_bwd", du0, x, g_mix[0:1], dh1)
    small = {
        "ev_g_cq": dg_cq, "ev_g_ckv": dg_ckv, "od_rel_bias": d_rel.reshape(1, C_H, 2 * REL_CLIP + 1),
        "g_mix": jnp.concatenate([dg_mix0, dg_mix1], axis=0), "g_ffn": jnp.concatenate([dg_ffn0, dg_ffn1], axis=0),
        "g_final": dg_final.reshape(D),
    }
    return loss, grad_x, small


BIG = ("ev_w_in", "ev_w_uq", "ev_w_ukv", "ev_w_out", "od_w_qkv", "od_w_out", "w_gate", "w_up", "w_down")
SMALL = ("ev_g_cq", "ev_g_ckv", "od_rel_bias", "g_mix", "g_ffn", "g_final")
WEIGHTS = ("ev_w_in", "ev_g_cq", "ev_w_uq", "ev_g_ckv", "ev_w_ukv", "ev_w_out", "od_w_qkv", "od_rel_bias", "od_w_out",
           "g_mix", "g_ffn", "w_gate", "w_up", "w_down", "g_final")
GRAD_PARTS = (("ev_w_in", "ev_w_in", 0), ("ev_w_uq", "ev_w_uq", 0), ("ev_w_ukv", "ev_w_ukv", 0),
              ("ev_w_out", "ev_w_out", 0), ("od_w_qkv", "od_w_qkv", 0), ("od_w_out", "od_w_out", 0),
              ("w_gate0", "w_gate", 0), ("w_gate1", "w_gate", 1), ("w_up0", "w_up", 0), ("w_up1", "w_up", 1),
              ("w_down0", "w_down", 0), ("w_down1", "w_down", 1))
PART_OF = {part: (param, layer) for part, param, layer in GRAD_PARTS}
SMALL_ROWS = 112
SMALL_SIZE = 384 + 256 + 16 * 513 + 2 * 1024 + 2 * 1024 + 1024
TRANSPOSED = ("w_gate", "w_up")


def _row_tile(rows, cap=512):
    for t in range(min(rows, cap), 0, -1):
        if rows % t == 0 and t % 16 == 0:
            return t
    return rows


def _cast_into_slot(name, w, layer, pos):
    _, rows, cols = w.shape
    tr = _row_tile(rows)

    def body(pos_ref, w_ref, o_ref):
        o_ref[...] = w_ref[...].astype(BF16)

    return pl.pallas_call(
        body, name=name,
        grid_spec=pltpu.PrefetchScalarGridSpec(
            num_scalar_prefetch=1, grid=(rows // tr,),
            in_specs=[pl.BlockSpec((None, tr, cols), lambda i, p: (layer, i, 0))],
            out_specs=pl.BlockSpec((None, tr, cols), lambda i, p: (p[0], i, 0))),
        out_shape=jax.ShapeDtypeStruct((N_CHIPS, rows, cols), BF16), compiler_params=_params("arbitrary"))(pos, w)


def _pair_sum(name, part, theirs, pos):
    _, half, cols = theirs.shape
    tr = _row_tile(half)
    nb = half // tr

    def body(pos_ref, a_ref, b_ref, o_ref):
        o_ref[...] = (a_ref[...].astype(F32) + b_ref[...].astype(F32)).astype(BF16)

    return pl.pallas_call(
        body, name=name,
        grid_spec=pltpu.PrefetchScalarGridSpec(
            num_scalar_prefetch=1, grid=(N_CHIPS, nb),
            in_specs=[pl.BlockSpec((None, tr, cols), lambda s, i, p: (s, p[1] * nb + i, 0)),
                      pl.BlockSpec((None, tr, cols), lambda s, i, p: (s, i, 0))],
            out_specs=pl.BlockSpec((None, tr, cols), lambda s, i, p: (s, i, 0))),
        out_shape=jax.ShapeDtypeStruct(theirs.shape, BF16),
        compiler_params=_params("arbitrary", "arbitrary"))(pos, part, theirs)


def _chip_sum(name, sums, got, pos, layer, full_shape, full=None):
    _, half, cols = sums.shape
    tr = _row_tile(half)
    nb = half // tr

    def body(pos_ref, s_ref, g_ref, *rest):
        out_ref = rest[-1]
        out_ref[...] = ((s_ref[...].astype(F32) + g_ref[0].astype(F32)) + g_ref[1].astype(F32)) + g_ref[2].astype(F32)

    in_specs = [pl.BlockSpec((None, tr, cols), lambda i, p: (p[0], i, 0)),
                pl.BlockSpec((3, tr, cols), lambda i, p: (0, i, 0))]
    args = [pos, sums, got]
    if full is not None:
        in_specs.append(ANY)
        args.append(full)
    return pl.pallas_call(
        body, name=name,
        grid_spec=pltpu.PrefetchScalarGridSpec(
            num_scalar_prefetch=1, grid=(nb,), in_specs=in_specs,
            out_specs=pl.BlockSpec((None, tr, cols), lambda i, p: (layer, p[1] * nb + i, 0))),
        out_shape=jax.ShapeDtypeStruct(full_shape, F32),
        input_output_aliases={3: 0} if full is not None else {},
        compiler_params=_params("arbitrary"))(*args)


def _all_reduce_small(name, packed):
    n_dev = 8

    def body(p_ref, o_ref, slots, send_sem, recv_sem):
        x, y, c, _ = _position()
        me = 4 * x + 2 * y + c

        def peer(k):
            return (1 - x if k & 4 else x, 1 - y if k & 2 else y, 1 - c if k & 1 else c)

        def logical(k):
            px, py, pc = peer(k)
            return 4 * px + 2 * py + pc

        slots[me] = p_ref[...]
        sends = [pltpu.make_async_remote_copy(
            src_ref=p_ref, dst_ref=slots.at[me], send_sem=send_sem.at[k], recv_sem=recv_sem.at[k],
            device_id=peer(k), device_id_type=MESH) for k in range(1, n_dev)]
        for cp in sends:
            cp.start()
        for k in range(1, n_dev):
            pltpu.make_async_remote_copy(
                src_ref=p_ref, dst_ref=slots.at[logical(k)], send_sem=send_sem.at[k], recv_sem=recv_sem.at[k],
                device_id=peer(k), device_id_type=MESH).wait_recv()
        for cp in sends:
            cp.wait_send()
        total = slots[0]
        for d in range(1, n_dev):
            total = total + slots[d]
        o_ref[...] = total

    vm = pl.BlockSpec(memory_space=pltpu.VMEM)
    return pl.pallas_call(
        body, name=name, in_specs=[vm], out_specs=vm, out_shape=jax.ShapeDtypeStruct(packed.shape, F32),
        scratch_shapes=[pltpu.VMEM((n_dev,) + packed.shape, F32), pltpu.SemaphoreType.DMA((n_dev,)),
                        pltpu.SemaphoreType.DMA((n_dev,))],
    )(packed)


def _adamw(name, w, g, m, v):
    rows, cols = w.shape
    tr = _row_tile(rows)

    def body(w_ref, g_ref, m_ref, v_ref, d_ref, mo_ref, vo_ref):
        gv = g_ref[...]
        m_new = ADAM_B1 * m_ref[...] + (1.0 - ADAM_B1) * gv
        v_new = ADAM_B2 * v_ref[...] + (1.0 - ADAM_B2) * (gv * gv)
        m_hat = m_new / (1.0 - ADAM_B1 ** ADAM_STEP)
        v_hat = v_new / (1.0 - ADAM_B2 ** ADAM_STEP)
        d_ref[...] = -ADAM_LR * (m_hat / (jnp.sqrt(v_hat) + ADAM_EPS) + ADAM_WD * w_ref[...])
        mo_ref[...] = m_new
        vo_ref[...] = v_new

    spec = pl.BlockSpec((tr, cols), lambda i: (i, 0))
    shape = jax.ShapeDtypeStruct((rows, cols), F32)
    return pl.pallas_call(body, name=name, grid=(rows // tr,), in_specs=[spec] * 4, out_specs=[spec] * 3,
                          out_shape=[shape] * 3, compiler_params=_params("parallel"))(w, g, m, v)


def _pack_small(tree, extra=None):
    pieces = [tree[n].reshape(-1).astype(F32) for n in SMALL]
    if extra is not None:
        pieces.append(extra.reshape(1).astype(F32))
    flat = jnp.concatenate(pieces)
    return jnp.pad(flat, (0, SMALL_ROWS * LANES - flat.shape[0])).reshape(SMALL_ROWS, LANES)


def _unpack_small(packed, like):
    flat = packed.reshape(-1)
    out, off = {}, 0
    for n in SMALL:
        size = int(np.prod(like[n].shape))
        out[n] = flat[off:off + size].reshape(like[n].shape)
        off += size
    return out


def kernel(x, ev_w_in, ev_g_cq, ev_w_uq, ev_g_ckv, ev_w_ukv, ev_w_out, od_w_qkv, od_rel_bias, od_w_out, g_mix, g_ffn, w_gate, w_up, w_down, g_final, loss_target, m_ev_w_in, m_ev_g_cq, m_ev_w_uq, m_ev_g_ckv, m_ev_w_ukv, m_ev_w_out, m_od_w_qkv, m_od_rel_bias, m_od_w_out, m_g_mix, m_g_ffn, m_w_gate, m_w_up, m_w_down, m_g_final, v_ev_w_in, v_ev_g_cq, v_ev_w_uq, v_ev_g_ckv, v_ev_w_ukv, v_ev_w_out, v_od_w_qkv, v_od_rel_bias, v_od_w_out, v_g_mix, v_g_ffn, v_w_gate, v_w_up, v_w_down, v_g_final):
    w = dict(ev_w_in=ev_w_in, ev_g_cq=ev_g_cq, ev_w_uq=ev_w_uq, ev_g_ckv=ev_g_ckv, ev_w_ukv=ev_w_ukv, ev_w_out=ev_w_out,
             od_w_qkv=od_w_qkv, od_rel_bias=od_rel_bias, od_w_out=od_w_out, g_mix=g_mix, g_ffn=g_ffn, w_gate=w_gate,
             w_up=w_up, w_down=w_down, g_final=g_final)
    m = dict(ev_w_in=m_ev_w_in, ev_g_cq=m_ev_g_cq, ev_w_uq=m_ev_w_uq, ev_g_ckv=m_ev_g_ckv, ev_w_ukv=m_ev_w_ukv,
             ev_w_out=m_ev_w_out, od_w_qkv=m_od_w_qkv, od_rel_bias=m_od_rel_bias, od_w_out=m_od_w_out, g_mix=m_g_mix,
             g_ffn=m_g_ffn, w_gate=m_w_gate, w_up=m_w_up, w_down=m_w_down, g_final=m_g_final)
    v = dict(ev_w_in=v_ev_w_in, ev_g_cq=v_ev_g_cq, ev_w_uq=v_ev_w_uq, ev_g_ckv=v_ev_g_ckv, ev_w_ukv=v_ev_w_ukv,
             ev_w_out=v_ev_w_out, od_w_qkv=v_od_w_qkv, od_rel_bias=v_od_rel_bias, od_w_out=v_od_w_out, g_mix=v_g_mix,
             g_ffn=v_g_ffn, w_gate=v_w_gate, w_up=v_w_up, w_down=v_w_down, g_final=v_g_final)
    flat2d = lambda a: a.reshape(-1, a.shape[-1])
    for tree in (w, m, v):
        for n in TRANSPOSED:
            tree[n] = jnp.swapaxes(tree[n], 1, 2)

    pos = jnp.stack([2 * lax.axis_index("x") + lax.axis_index("y"), lax.axis_index("c")]).astype(jnp.int32)

    slots = {part: _cast_into_slot("cast_" + part, w[n], layer, pos) for part, n, layer in GRAD_PARTS}
    ex = _Exchanges(slots, pos, {n: w[n].shape for n in BIG})

    loss_local, grad_x, small = _local_step(x[0], loss_target[0], {n: w[n] for n in SMALL}, ex)

    grads = ex.finish()
    small_sum = _all_reduce_small("small_sum", _pack_small(small, loss_local[0, 0]))
    grads.update(_unpack_small(small_sum, w))

    delta, new_m, new_v = {}, {}, {}
    for n in BIG:
        d_, m_, v_ = _adamw("adamw_" + n, flat2d(w[n]), flat2d(grads[n]), flat2d(m[n]), flat2d(v[n]))
        delta[n], new_m[n], new_v[n] = d_.reshape(w[n].shape), m_.reshape(w[n].shape), v_.reshape(w[n].shape)
    d_, m_, v_ = _adamw("adamw_small", _pack_small(w), small_sum, _pack_small(m), _pack_small(v))
    delta.update(_unpack_small(d_, w))
    new_m.update(_unpack_small(m_, w))
    new_v.update(_unpack_small(v_, w))
    for tree in (grads, delta, new_m, new_v):
        for n in TRANSPOSED:
            tree[n] = jnp.swapaxes(tree[n], 1, 2)

    loss = small_sum.reshape(-1)[SMALL_SIZE]
    return (loss, grad_x[None], *[grads[n] for n in WEIGHTS], *[delta[n] for n in WEIGHTS],
            *[new_m[n] for n in WEIGHTS], *[new_v[n] for n in WEIGHTS])
```

```python
import functools

import jax
import jax.numpy as jnp
import numpy as np
from jax import lax
from jax.experimental import pallas as pl
from jax.experimental.pallas import tpu as pltpu

F32 = jnp.float32
BF16 = jnp.bfloat16

S = 2048
D = 1024
CHUNK = 64
MLA_H, MLA_NOPE, MLA_ROPE, MLA_V = 8, 64, 32, 64
Q_LORA, KV_LORA = 384, 256
ROPE_THETA = 10000.0
SB_H, SB_DIM = 8, 64
C_H, C_DIM = 16, 64
LEFT_CHUNKS = 8
REL_CLIP = 256
D_FF = 2816
EVEN_IN = 2208
RMS_EPS = 1e-6
ADAM_LR, ADAM_B1, ADAM_B2, ADAM_EPS, ADAM_WD, ADAM_STEP = 0.001, 0.9, 0.999, 1e-08, 0.01, 10

N_CHIPS = 4
FF_SHARD = D_FF // N_CHIPS
SCALE_A = (MLA_NOPE + MLA_ROPE) ** -0.5
SCALE_B = SB_DIM ** -0.5
SCALE_C = C_DIM ** -0.5
NEG = -1e30
LOG2_E = 1.4426950408889634

LANES = 128
MXU_W = 256
VMEM_LIMIT_BYTES = 56 * 1024 * 1024
TM = 512
QB = 512
BQ = 256

P_CQ, P_CKV, P_QB, P_KB, P_VB, P_KR = 0, 512, 768, 1280, 1792, 2304
P_IN = 2432
KR_LANE = 64
BAND_W = BQ + LEFT_CHUNKS * CHUNK
BAND_PAD = 512
TOEP_W = 1024


def _params(*sem):
    return pltpu.CompilerParams(dimension_semantics=sem, vmem_limit_bytes=VMEM_LIMIT_BYTES)


MESH = pl.DeviceIdType.MESH
ANY = pl.BlockSpec(memory_space=pl.ANY)


def _position():
    x, y, c = lax.axis_index("x"), lax.axis_index("y"), lax.axis_index("c")
    other_chips = [(1 - x, y), (x, 1 - y), (1 - x, 1 - y)]
    return x, y, c, other_chips


def _half_rows(c, half):
    return pl.ds(pl.multiple_of(c * half, 16), half)


def _remote(ref_src, ref_dst, send, recv, k, device):
    return pltpu.make_async_remote_copy(src_ref=ref_src, dst_ref=ref_dst, send_sem=send.at[k], recv_sem=recv.at[k],
                                        device_id=device, device_id_type=MESH)


class _Carry:
    def __init__(self):
        self.operands, self.aliased, self.fresh = [], [], []
        self.n_sems = 0
        self.starts, self.finishes, self.on_done = [], [], []

    def operand(self, arr, aliased):
        for i, a in enumerate(self.operands):
            if a is arr:
                return i
        self.operands.append(arr)
        self.aliased.append(aliased)
        return len(self.operands) - 1

    def result(self, shape, dtype):
        self.fresh.append(jax.ShapeDtypeStruct(shape, dtype))
        return len(self.fresh) - 1

    def sems(self, k):
        base = self.n_sems
        self.n_sems += k
        return base

    def done(self, results):
        aliased, fresh = results
        for f in self.on_done:
            f(aliased, fresh)


def _carrier_call(body, *, name, grid, in_specs, out_specs, out_shape, args, sem, scratch_shapes=(), carry=None):
    in_specs, out_specs, out_shape, scratch = list(in_specs), list(out_specs), list(out_shape), list(scratch_shapes)
    if carry is None:
        res = pl.pallas_call(body, name=name, grid=grid, in_specs=in_specs, out_specs=out_specs, out_shape=out_shape,
                             scratch_shapes=scratch, compiler_params=_params(*sem))(*args)
        return list(res), None
    ops = carry.operands
    alias_idx = [i for i, a in enumerate(carry.aliased) if a]
    c_shapes = [jax.ShapeDtypeStruct(ops[i].shape, ops[i].dtype) for i in alias_idx] + carry.fresh
    n_in, n_out, n_scr = len(args), len(out_shape), len(scratch)

    def wrapped(*refs):
        ins, c_ins = refs[:n_in], refs[n_in:n_in + len(ops)]
        o0 = n_in + len(ops)
        outs, c_outs = refs[o0:o0 + n_out], refs[o0 + n_out:o0 + n_out + len(c_shapes)]
        s0 = o0 + n_out + len(c_shapes)
        scr, send, recv = refs[s0:s0 + n_scr], refs[s0 + n_scr], refs[s0 + n_scr + 1]
        use = list(c_ins)
        for k, i in enumerate(alias_idx):
            use[i] = c_outs[k]
        fresh = c_outs[len(alias_idx):]

        def run(steps):
            for step in steps:
                step(use, fresh, send, recv)

        if not grid:
            run(carry.starts)
            if body is not None:
                body(*ins, *outs, *scr)
            run(carry.finishes)
            return
        ids = [pl.program_id(a) for a in range(len(grid))]
        first = functools.reduce(jnp.logical_and, [i == 0 for i in ids])
        last = functools.reduce(jnp.logical_and, [i == g - 1 for i, g in zip(ids, grid)])

        @pl.when(first)
        def _():
            run(carry.starts)

        body(*ins, *outs, *scr)

        @pl.when(last)
        def _():
            run(carry.finishes)

    res = pl.pallas_call(
        wrapped, name=name, grid=grid, in_specs=in_specs + [ANY] * len(ops), out_specs=out_specs + [ANY] * len(c_shapes),
        out_shape=out_shape + c_shapes,
        scratch_shapes=scratch + [pltpu.SemaphoreType.DMA((carry.n_sems,)), pltpu.SemaphoreType.DMA((carry.n_sems,))],
        input_output_aliases={n_in + i: n_out + k for k, i in enumerate(alias_idx)},
        compiler_params=_params(*(("arbitrary",) * len(grid))),
    )(*args, *ops)
    res = list(res)
    c_res = res[n_out:]
    return res[:n_out], ({i: c_res[k] for k, i in enumerate(alias_idx)}, c_res[len(alias_idx):])


_DIMS = {"nn": (((1,), (0,)), ((), ())), "nt": (((1,), (1,)), ((), ())), "tn": (((0,), (0,)), ((), ()))}


def _dot(a, b, kind="nn"):
    return lax.dot_general(a, b, _DIMS[kind], preferred_element_type=F32)


def _iota(shape, dim):
    return lax.broadcasted_iota(jnp.int32, shape, dim)


def _sigmoid(x):
    return 1.0 / (1.0 + jnp.exp(-x))


def _split_dot(x, tri):
    hi = x.astype(BF16)
    lo = (x - hi.astype(F32)).astype(BF16)
    return _dot(hi, tri) + _dot(lo, tri)


def _running_sum(x, tri, reverse):
    n = x.shape[1] // MXU_W
    blocks = [x[:, b * MXU_W:(b + 1) * MXU_W] for b in range(n)]
    out = [None] * n
    carry = None
    for b in (range(n - 1, -1, -1) if reverse else range(n)):
        part = _split_dot(blocks[b], tri)
        out[b] = part if carry is None else part + carry
        total = jnp.sum(blocks[b], axis=-1, keepdims=True)
        carry = total if carry is None else carry + total
    return (jnp.concatenate(out, axis=1) if n > 1 else out[0]), carry


def _mm(name, a, b, *, kind, grid, a_spec, b_spec, o_spec, out_shape, out_dtype, acc_shape, resid=None, r_spec=None,
        carry=None):
    nk = grid[-1]
    has_r = resid is not None

    def body(*refs):
        a_ref, b_ref = refs[0], refs[1]
        r_ref = refs[2] if has_r else None
        o_ref = refs[2 + has_r]
        part = _dot(a_ref[...].astype(BF16), b_ref[...].astype(BF16), kind)

        def finish(total):
            if has_r:
                total = total + r_ref[...].astype(F32)
            o_ref[...] = total.astype(out_dtype)

        if nk == 1:
            finish(part)
        else:
            acc_ref = refs[3 + has_r]
            k = pl.program_id(len(grid) - 1)

            @pl.when(k == 0)
            def _():
                acc_ref[...] = part

            @pl.when(k > 0)
            def _():
                acc_ref[...] += part

            @pl.when(k == nk - 1)
            def _():
                finish(acc_ref[...])

    in_specs = [a_spec, b_spec] + ([r_spec] if has_r else [])
    args = (a, b) + ((resid,) if has_r else ())
    sem = ("parallel",) * (len(grid) - 1) + ("arbitrary",)
    res, copies = _carrier_call(
        body, name=name, grid=grid, in_specs=in_specs, out_specs=[o_spec],
        out_shape=[jax.ShapeDtypeStruct(out_shape, out_dtype)],
        scratch_shapes=[pltpu.VMEM(acc_shape, F32)] if nk > 1 else [], args=args, sem=sem, carry=carry)
    if carry is not None:
        carry.done(copies)
    return res[0]


def _rms_fwd(name, x, g, col_block=0):
    c = g.shape[1]

    def body(x_ref, g_ref, u_ref):
        xv = x_ref[...]
        r = lax.rsqrt(jnp.mean(xv * xv, axis=-1, keepdims=True) + RMS_EPS)
        u_ref[...] = (xv * r * g_ref[...]).astype(BF16)

    return pl.pallas_call(
        body, name=name, grid=(S // TM,),
        in_specs=[pl.BlockSpec((TM, c), lambda i: (i, col_block)), pl.BlockSpec((1, c), lambda i: (0, 0))],
        out_specs=pl.BlockSpec((TM, c), lambda i: (i, 0)),
        out_shape=jax.ShapeDtypeStruct((S, c), BF16),
        compiler_params=_params("parallel"),
    )(x, g)


def _rms_bwd(name, dy, x, g, resid, carry=None):
    def body(dy_ref, x_ref, g_ref, r_ref, dx_ref, dg_ref):
        i = pl.program_id(0)
        xv = x_ref[...]
        r = lax.rsqrt(jnp.mean(xv * xv, axis=-1, keepdims=True) + RMS_EPS)
        xh = xv * r
        dyv = dy_ref[...]
        dxh = dyv * g_ref[...]
        dx_ref[...] = r_ref[...] + r * (dxh - xh * jnp.mean(dxh * xh, axis=-1, keepdims=True))
        part = jnp.sum(dyv * xh, axis=0, keepdims=True)

        @pl.when(i == 0)
        def _():
            dg_ref[...] = part

        @pl.when(i > 0)
        def _():
            dg_ref[...] += part

    row = pl.BlockSpec((TM, D), lambda i: (i, 0))
    vec = pl.BlockSpec((1, D), lambda i: (0, 0))
    res, copies = _carrier_call(
        body, name=name, grid=(S // TM,), in_specs=[row, row, vec, row], out_specs=[row, vec],
        out_shape=[jax.ShapeDtypeStruct((S, D), F32), jax.ShapeDtypeStruct((1, D), F32)],
        args=(dy, x, g, resid), sem=("arbitrary",), carry=carry)
    if carry is not None:
        carry.done(copies)
    return res


def _loss_bwd(name, h, g, tgt):
    def body(h_ref, g_ref, t_ref, loss_ref, dh_ref, dg_ref):
        i = pl.program_id(0)
        xv = h_ref[...]
        gv = g_ref[...]
        r = lax.rsqrt(jnp.mean(xv * xv, axis=-1, keepdims=True) + RMS_EPS)
        xh = xv * r
        diff = xh * gv - t_ref[...]
        part_loss = 0.5 * jnp.sum(jnp.sum(diff * diff, axis=-1, keepdims=True) * (1.0 / D), axis=0, keepdims=True)
        dy = diff * (1.0 / D)
        dxh = dy * gv
        dh_ref[...] = r * (dxh - xh * jnp.mean(dxh * xh, axis=-1, keepdims=True))
        part_g = jnp.sum(dy * xh, axis=0, keepdims=True)

        @pl.when(i == 0)
        def _():
            dg_ref[...] = part_g
            loss_ref[...] = jnp.broadcast_to(part_loss, (1, LANES))

        @pl.when(i > 0)
        def _():
            dg_ref[...] += part_g
            loss_ref[...] += jnp.broadcast_to(part_loss, (1, LANES))

    row = pl.BlockSpec((TM, D), lambda i: (i, 0))
    vec = pl.BlockSpec((1, D), lambda i: (0, 0))
    return pl.pallas_call(
        body, name=name, grid=(S // TM,), in_specs=[row, vec, row],
        out_specs=[pl.BlockSpec((1, LANES), lambda i: (0, 0)), row, vec],
        out_shape=[jax.ShapeDtypeStruct((1, LANES), F32), jax.ShapeDtypeStruct((S, D), F32),
                   jax.ShapeDtypeStruct((1, D), F32)],
        compiler_params=_params("arbitrary"),
    )(h, g, tgt)


def _ffn_fwd(name, h, g, wg, wu, wd, carry=None):
    def body(h_ref, g_ref, wg_ref, wu_ref, wd_ref, o_ref, gate_ref, up_ref, u_scr):
        s = pl.program_id(1)

        @pl.when(s == 0)
        def _():
            xv = h_ref[...]
            r = lax.rsqrt(jnp.mean(xv * xv, axis=-1, keepdims=True) + RMS_EPS)
            u_scr[...] = (xv * r * g_ref[...]).astype(BF16)
            o_ref[...] = xv

        u = u_scr[...]
        gate = _dot(u, wg_ref[...], "nt")
        up = _dot(u, wu_ref[...], "nt")
        act = gate * _sigmoid(gate) * up
        o_ref[...] += _dot(act.astype(BF16), wd_ref[...])
        gate_ref[...] = gate.astype(BF16)
        up_ref[...] = up.astype(BF16)

    row = pl.BlockSpec((TM, D), lambda i, s: (i, 0))
    hid = pl.BlockSpec((None, TM, FF_SHARD), lambda i, s: (s, i, 0))
    return _carrier_call(
        body, name=name, grid=(S // TM, N_CHIPS),
        in_specs=[row, pl.BlockSpec((1, D), lambda i, s: (0, 0))]
        + [pl.BlockSpec((None, FF_SHARD, D), lambda i, s: (s, 0, 0))] * 3,
        out_specs=[row, hid, hid],
        out_shape=[jax.ShapeDtypeStruct((S, D), F32), jax.ShapeDtypeStruct((N_CHIPS, S, FF_SHARD), BF16),
                   jax.ShapeDtypeStruct((N_CHIPS, S, FF_SHARD), BF16)],
        scratch_shapes=[pltpu.VMEM((TM, D), BF16)], args=(h, g, wg, wu, wd), sem=("parallel", "arbitrary"), carry=carry)


def _ffn_bwd(name, dh, h, g, gate, up, wg, wu, wd):
    def body(dh_ref, h_ref, g_ref, gate_ref, up_ref, wg_ref, wu_ref, wd_ref,
             dhin_ref, dg_ref, u_ref, dgate_ref, dup_ref, act_ref, dhb_scr, du_scr):
        i = pl.program_id(0)
        s = pl.program_id(1)

        @pl.when(s == 0)
        def _():
            xv = h_ref[...]
            r = lax.rsqrt(jnp.mean(xv * xv, axis=-1, keepdims=True) + RMS_EPS)
            u_ref[...] = (xv * r * g_ref[...]).astype(BF16)
            dhb_scr[...] = dh_ref[...].astype(BF16)
            du_scr[...] = jnp.zeros_like(du_scr)

        dact = _dot(dhb_scr[...], wd_ref[...], "nt")
        gv = gate_ref[...].astype(F32)
        uv = up_ref[...].astype(F32)
        sig = _sigmoid(gv)
        sil = gv * sig
        dup = dact * sil
        dgate = dact * uv * (sig * (1.0 + gv * (1.0 - sig)))
        dgb = dgate.astype(BF16)
        dub = dup.astype(BF16)
        act_ref[...] = (sil * uv).astype(BF16)
        dgate_ref[...] = dgb
        dup_ref[...] = dub
        du_scr[...] += _dot(dgb, wg_ref[...]) + _dot(dub, wu_ref[...])

        @pl.when(s == N_CHIPS - 1)
        def _():
            xv = h_ref[...]
            r = lax.rsqrt(jnp.mean(xv * xv, axis=-1, keepdims=True) + RMS_EPS)
            xh = xv * r
            du = du_scr[...]
            dxh = du * g_ref[...]
            dhin_ref[...] = dh_ref[...] + r * (dxh - xh * jnp.mean(dxh * xh, axis=-1, keepdims=True))
            part = jnp.sum(du * xh, axis=0, keepdims=True)

            @pl.when(i == 0)
            def _():
                dg_ref[...] = part

            @pl.when(i > 0)
            def _():
                dg_ref[...] += part

    row = pl.BlockSpec((TM, D), lambda i, s: (i, 0))
    vec = pl.BlockSpec((1, D), lambda i, s: (0, 0))
    hid = pl.BlockSpec((None, TM, FF_SHARD), lambda i, s: (s, i, 0))
    hid_shape = jax.ShapeDtypeStruct((N_CHIPS, S, FF_SHARD), BF16)
    return pl.pallas_call(
        body, name=name, grid=(S // TM, N_CHIPS),
        in_specs=[row, row, vec, hid, hid] + [pl.BlockSpec((None, FF_SHARD, D), lambda i, s: (s, 0, 0))] * 3,
        out_specs=[row, vec, row, hid, hid, hid],
        out_shape=[jax.ShapeDtypeStruct((S, D), F32), jax.ShapeDtypeStruct((1, D), F32),
                   jax.ShapeDtypeStruct((S, D), BF16), hid_shape, hid_shape, hid_shape],
        scratch_shapes=[pltpu.VMEM((TM, D), BF16), pltpu.VMEM((TM, D), F32)],
        compiler_params=_params("arbitrary", "arbitrary"),
    )(dh, h, g, gate, up, wg, wu, wd)


def _ffn_wgrads(name, u, dgate, dup, act, dh):
    nk = S // TM

    def body(u_ref, dh_ref, dgate_ref, dup_ref, act_ref, dg_ref, du_ref, dd_ref, acc_g, acc_u, acc_d):
        k = pl.program_id(1)
        u = u_ref[...]
        parts = (_dot(dgate_ref[...], u, "tn"), _dot(dup_ref[...], u, "tn"),
                 _dot(act_ref[...], dh_ref[...].astype(BF16), "tn"))
        accs = (acc_g, acc_u, acc_d)

        @pl.when(k == 0)
        def _():
            for acc, part in zip(accs, parts):
                acc[...] = part

        @pl.when(k > 0)
        def _():
            for acc, part in zip(accs, parts):
                acc[...] += part

        @pl.when(k == nk - 1)
        def _():
            for out, acc in zip((dg_ref, du_ref, dd_ref), accs):
                out[...] = acc[...].astype(BF16)

    tok = pl.BlockSpec((TM, D), lambda s, k: (k, 0))
    hid = pl.BlockSpec((None, TM, FF_SHARD), lambda s, k: (s, k, 0))
    out = pl.BlockSpec((None, FF_SHARD, D), lambda s, k: (s, 0, 0))
    shape = jax.ShapeDtypeStruct((N_CHIPS, FF_SHARD, D), BF16)
    return pl.pallas_call(
        body, name=name, grid=(N_CHIPS, nk), in_specs=[tok, tok, hid, hid, hid], out_specs=[out, out, out],
        out_shape=[shape, shape, shape], scratch_shapes=[pltpu.VMEM((FF_SHARD, D), F32)] * 3,
        compiler_params=_params("parallel", "arbitrary"))(u, dh, dgate, dup, act)


def _rope_tables():
    pos = jnp.arange(S, dtype=F32)
    inv = ROPE_THETA ** (-jnp.arange(0, MLA_ROPE, 2, dtype=F32) / MLA_ROPE)
    ang = pos[:, None] * inv[None, :]
    half = MLA_ROPE // 2
    cos = jnp.cos(ang)
    sin = jnp.sin(ang)
    one = jnp.ones((S, KR_LANE), F32)
    zero = jnp.zeros((S, KR_LANE), F32)
    tail_one = jnp.ones((S, LANES - KR_LANE - MLA_ROPE), F32)
    tail_zero = jnp.zeros((S, LANES - KR_LANE - MLA_ROPE), F32)
    cos_t = jnp.concatenate([one, cos, cos, tail_one], axis=1)
    sin_t = jnp.concatenate([zero, -sin, sin, tail_zero], axis=1)
    assert cos_t.shape == (S, LANES) and half * 2 == MLA_ROPE
    return cos_t, sin_t


def _rope(x, cos_t, sin_t, sign):
    n = x.shape[1] // LANES
    half = MLA_ROPE // 2
    lane = _iota(x.shape, 1) & (LANES - 1)
    first = (lane >= KR_LANE) & (lane < KR_LANE + half)
    swapped = jnp.where(first, pltpu.roll(x, x.shape[1] - half, 1), pltpu.roll(x, half, 1))
    c = jnp.tile(cos_t, (1, n)) if n > 1 else cos_t
    s = jnp.tile(sin_t, (1, n)) if n > 1 else sin_t
    return x * c + swapped * (s * sign)


def _mla_prep_fwd(name, proj, g_cq, g_ckv, w_uq, w_uk, w_uv, cos_t, sin_t):
    nh = MLA_H * LANES

    def body(cq_ref, ckv_ref, kr_ref, gq_ref, gkv_ref, wq_ref, wk_ref, wv_ref, cos_ref, sin_ref,
             qa_ref, ka_ref, va_ref):
        cos_v, sin_v = cos_ref[...], sin_ref[...]
        cq = cq_ref[...]
        r = lax.rsqrt(jnp.mean(cq * cq, axis=-1, keepdims=True) + RMS_EPS)
        cqn = (cq * r * gq_ref[...]).astype(BF16)
        qa_ref[...] = _rope(_dot(cqn, wq_ref[...]), cos_v, sin_v, 1.0).astype(BF16)
        ckv = ckv_ref[...]
        r = lax.rsqrt(jnp.mean(ckv * ckv, axis=-1, keepdims=True) + RMS_EPS)
        ckvn = (ckv * r * gkv_ref[...]).astype(BF16)
        lane = _iota((TM, LANES), 1)
        rot = (lane >= KR_LANE) & (lane < KR_LANE + MLA_ROPE)
        kr = jnp.where(rot, _rope(kr_ref[...], cos_v, sin_v, 1.0), 0.0)
        ka_ref[...] = (_dot(ckvn, wk_ref[...]) + jnp.tile(kr, (1, MLA_H))).astype(BF16)
        va_ref[...] = _dot(ckvn, wv_ref[...]).astype(BF16)

    full = lambda shape: pl.BlockSpec(shape, lambda i: (0, 0))
    return pl.pallas_call(
        body, name=name, grid=(S // TM,),
        in_specs=[pl.BlockSpec((TM, Q_LORA), lambda i: (i, P_CQ // Q_LORA)),
                  pl.BlockSpec((TM, KV_LORA), lambda i: (i, P_CKV // KV_LORA)),
                  pl.BlockSpec((TM, LANES), lambda i: (i, P_KR // LANES)),
                  full((1, Q_LORA)), full((1, KV_LORA)), full((Q_LORA, nh)), full((KV_LORA, nh)),
                  full((KV_LORA, MLA_H * MLA_V)),
                  pl.BlockSpec((TM, LANES), lambda i: (i, 0)), pl.BlockSpec((TM, LANES), lambda i: (i, 0))],
        out_specs=[pl.BlockSpec((TM, nh), lambda i: (i, 0)), pl.BlockSpec((TM, nh), lambda i: (i, 0)),
                   pl.BlockSpec((TM, MLA_H * MLA_V), lambda i: (i, 0))],
        out_shape=[jax.ShapeDtypeStruct((S, nh), BF16), jax.ShapeDtypeStruct((S, nh), BF16),
                   jax.ShapeDtypeStruct((S, MLA_H * MLA_V), BF16)],
        compiler_params=_params("parallel"),
    )(proj, proj, proj, g_cq, g_ckv, w_uq, w_uk, w_uv, cos_t, sin_t)


def _mla_prep_bwd(name, dqa, dka, dva, proj, g_cq, g_ckv, w_uq, w_uk, w_uv, cos_t, sin_t):
    nh = MLA_H * LANES

    def body(dqa_ref, dka_ref, dva_ref, cq_ref, ckv_ref, gq_ref, gkv_ref, wq_ref, wk_ref, wv_ref, cos_ref, sin_ref,
             dcq_ref, dckv_ref, dkr_ref, dwq_ref, dwk_ref, dwv_ref, dgq_ref, dgkv_ref):
        i = pl.program_id(0)
        cos_v, sin_v = cos_ref[...], sin_ref[...]

        def norm_bwd(x, g, dn):
            r = lax.rsqrt(jnp.mean(x * x, axis=-1, keepdims=True) + RMS_EPS)
            xh = x * r
            dxh = dn * g
            dx = r * (dxh - xh * jnp.mean(dxh * xh, axis=-1, keepdims=True))
            return dx, jnp.sum(dn * xh, axis=0, keepdims=True), (xh * g).astype(BF16)

        dq = _rope(dqa_ref[...], cos_v, sin_v, -1.0).astype(BF16)
        dcqn = _dot(dq, wq_ref[...], "nt")
        dcq, dgq, cqn = norm_bwd(cq_ref[...], gq_ref[...], dcqn)
        dcq_ref[...] = dcq.astype(BF16)
        dwq = _dot(cqn, dq, "tn")

        dka = dka_ref[...]
        dkab = dka.astype(BF16)
        dvab = dva_ref[...].astype(BF16)
        dckvn = _dot(dkab, wk_ref[...], "nt") + _dot(dvab, wv_ref[...], "nt")
        dckv, dgkv, ckvn = norm_bwd(ckv_ref[...], gkv_ref[...], dckvn)
        dckv_ref[...] = dckv.astype(BF16)
        dwk = _dot(ckvn, dkab, "tn")
        dwv = _dot(ckvn, dvab, "tn")

        fold = dka[:, 0:LANES]
        for hh in range(1, MLA_H):
            fold = fold + dka[:, hh * LANES:(hh + 1) * LANES]
        lane = _iota((TM, LANES), 1)
        rot = (lane >= KR_LANE) & (lane < KR_LANE + MLA_ROPE)
        dkr = _rope(jnp.where(rot, fold, 0.0), cos_v, sin_v, -1.0)
        dkr_ref[...] = jnp.where(rot, dkr, 0.0).astype(BF16)

        @pl.when(i == 0)
        def _():
            dwq_ref[...] = dwq
            dwk_ref[...] = dwk
            dwv_ref[...] = dwv
            dgq_ref[...] = dgq
            dgkv_ref[...] = dgkv

        @pl.when(i > 0)
        def _():
            dwq_ref[...] += dwq
            dwk_ref[...] += dwk
            dwv_ref[...] += dwv
            dgq_ref[...] += dgq
            dgkv_ref[...] += dgkv

    full = lambda shape: pl.BlockSpec(shape, lambda i: (0, 0))
    rows = lambda c: pl.BlockSpec((TM, c), lambda i: (i, 0))
    nv = MLA_H * MLA_V
    return pl.pallas_call(
        body, name=name, grid=(S // TM,),
        in_specs=[rows(nh), rows(nh), rows(nv),
                  pl.BlockSpec((TM, Q_LORA), lambda i: (i, P_CQ // Q_LORA)),
                  pl.BlockSpec((TM, KV_LORA), lambda i: (i, P_CKV // KV_LORA)),
                  full((1, Q_LORA)), full((1, KV_LORA)), full((Q_LORA, nh)), full((KV_LORA, nh)), full((KV_LORA, nv)),
                  rows(LANES), rows(LANES)],
        out_specs=[rows(Q_LORA), rows(KV_LORA), rows(LANES), full((Q_LORA, nh)), full((KV_LORA, nh)),
                   full((KV_LORA, nv)), full((1, Q_LORA)), full((1, KV_LORA))],
        out_shape=[jax.ShapeDtypeStruct((S, Q_LORA), BF16), jax.ShapeDtypeStruct((S, KV_LORA), BF16),
                   jax.ShapeDtypeStruct((S, LANES), BF16), jax.ShapeDtypeStruct((Q_LORA, nh), F32),
                   jax.ShapeDtypeStruct((KV_LORA, nh), F32), jax.ShapeDtypeStruct((KV_LORA, nv), F32),
                   jax.ShapeDtypeStruct((1, Q_LORA), F32), jax.ShapeDtypeStruct((1, KV_LORA), F32)],
        compiler_params=_params("arbitrary"),
    )(dqa, dka, dva, proj, proj, g_cq, g_ckv, w_uq, w_uk, w_uv, cos_t, sin_t)


def _head_masks(dtype):
    lane = _iota((1, LANES), 1)
    return (lane < 64).astype(dtype), (lane >= 64).astype(dtype)


def _mla_fwd(name, qa, ka, va, carry=None):
    def body(q_ref, k_ref, v_ref, o_ref, lse_ref):
        m0b, m1b = _head_masks(BF16)
        lane = _iota((QB, LANES), 1)
        left = lane < 64

        def qblock(i, _):
            r0 = pl.multiple_of(i * QB, QB)
            qs = [q_ref[pl.ds(r0, QB), hh * LANES:(hh + 1) * LANES] for hh in range(2)]
            rowc = lax.shift_right_logical(r0 + _iota((QB, QB), 0), 6)

            def kv(kb, carry):
                ms, ls, acc = carry
                c0 = pl.multiple_of(kb * QB, QB)
                v = v_ref[pl.ds(c0, QB), :]
                ok = lax.shift_right_logical(c0 + _iota((QB, QB), 1), 6) <= rowc
                new_m, new_l, alphas = [], [], []
                pv = None
                for hh in range(2):
                    k = k_ref[pl.ds(c0, QB), hh * LANES:(hh + 1) * LANES]
                    s = jnp.where(ok, _dot(qs[hh], k, "nt") * (SCALE_A * LOG2_E), NEG)
                    mn = jnp.maximum(ms[hh], jnp.max(s, axis=-1, keepdims=True))
                    p = jnp.exp2(s - mn)
                    a = jnp.exp2(ms[hh] - mn)
                    new_m.append(mn)
                    new_l.append(a * ls[hh] + jnp.sum(p, axis=-1, keepdims=True))
                    alphas.append(a)
                    part = _dot(p.astype(BF16), v * (m0b if hh == 0 else m1b))
                    pv = part if pv is None else pv + part
                acc = acc * jnp.where(left, alphas[0], alphas[1]) + pv
                return tuple(new_m), tuple(new_l), acc

            init = ((jnp.full((QB, 1), NEG, F32),) * 2, (jnp.zeros((QB, 1), F32),) * 2, jnp.zeros((QB, LANES), F32))
            ms, ls, acc = lax.fori_loop(0, i + 1, kv, init)
            o_ref[pl.ds(r0, QB), :] = acc * jnp.where(left, 1.0 / ls[0], 1.0 / ls[1])
            lse_ref[pl.ds(r0, QB), :] = jnp.where(left, ms[0] + jnp.log(ls[0]) * LOG2_E, ms[1] + jnp.log(ls[1]) * LOG2_E)
            return 0

        lax.fori_loop(0, S // QB, qblock, 0)

    pair = lambda w: pl.BlockSpec((S, w), lambda p: (0, p))
    return _carrier_call(
        body, name=name, grid=(MLA_H // 2,), in_specs=[pair(2 * LANES), pair(2 * LANES), pair(LANES)],
        out_specs=[pair(LANES), pair(LANES)],
        out_shape=[jax.ShapeDtypeStruct((S, MLA_H * MLA_V), F32), jax.ShapeDtypeStruct((S, MLA_H * MLA_V), F32)],
        args=(qa, ka, va), sem=("parallel",), carry=carry)


def _mla_bwd(name, qa, ka, va, o, lse, do, do_block0, carry=None):
    def body(q_ref, k_ref, v_ref, o_ref, lse_ref, do_ref, dq_ref, dk_ref, dv_ref):
        m0f, m1f = _head_masks(F32)
        m0b, m1b = _head_masks(BF16)
        dk_ref[...] = jnp.zeros_like(dk_ref)
        dv_ref[...] = jnp.zeros_like(dv_ref)

        def qblock(i, _):
            r0 = pl.multiple_of(i * QB, QB)
            rows = pl.ds(r0, QB)
            do_f = do_ref[rows, :]
            prod = do_f * o_ref[rows, :]
            deltas = [jnp.sum(prod * m0f, axis=-1, keepdims=True), jnp.sum(prod * m1f, axis=-1, keepdims=True)]
            lse_v = lse_ref[rows, :]
            lses = [lse_v[:, 0:1], lse_v[:, 64:65]]
            dob = do_f.astype(BF16)
            dos = [dob * m0b, dob * m1b]
            qs = [q_ref[rows, hh * LANES:(hh + 1) * LANES] for hh in range(2)]
            rowc = lax.shift_right_logical(r0 + _iota((QB, QB), 0), 6)

            def kv(kb, dqs):
                c0 = pl.multiple_of(kb * QB, QB)
                cols = pl.ds(c0, QB)
                v = v_ref[cols, :]
                ok = lax.shift_right_logical(c0 + _iota((QB, QB), 1), 6) <= rowc
                out = []
                dv = None
                for hh in range(2):
                    k = k_ref[cols, hh * LANES:(hh + 1) * LANES]
                    s = _dot(qs[hh], k, "nt") * (SCALE_A * LOG2_E)
                    p = jnp.where(ok, jnp.exp2(s - lses[hh]), 0.0)
                    dp = _dot(dos[hh], v, "nt")
                    ds = (p * (dp - deltas[hh]) * SCALE_A).astype(BF16)
                    out.append(dqs[hh] + _dot(ds, k))
                    dk_ref[cols, hh * LANES:(hh + 1) * LANES] += _dot(ds, qs[hh], "tn")
                    part = _dot(p.astype(BF16), dos[hh], "tn")
                    dv = part if dv is None else dv + part
                dv_ref[cols, :] += dv
                return tuple(out)

            dqs = lax.fori_loop(0, i + 1, kv, (jnp.zeros((QB, LANES), F32),) * 2)
            for hh in range(2):
                dq_ref[rows, hh * LANES:(hh + 1) * LANES] = dqs[hh]
            return 0

        lax.fori_loop(0, S // QB, qblock, 0)

    pair = lambda w: pl.BlockSpec((S, w), lambda p: (0, p))
    return _carrier_call(
        body, name=name, grid=(MLA_H // 2,),
        in_specs=[pair(2 * LANES), pair(2 * LANES), pair(LANES), pair(LANES), pair(LANES),
                  pl.BlockSpec((S, LANES), lambda p: (0, do_block0 + p))],
        out_specs=[pair(2 * LANES), pair(2 * LANES), pair(LANES)],
        out_shape=[jax.ShapeDtypeStruct((S, MLA_H * LANES), F32), jax.ShapeDtypeStruct((S, MLA_H * LANES), F32),
                   jax.ShapeDtypeStruct((S, MLA_H * MLA_V), F32)],
        args=(qa, ka, va, o, lse, do), sem=("parallel",), carry=carry)


def _sb_weights(q_h, k, c, before, tri_suffix):
    z = _dot(q_h, k, "nt") * (SCALE_B * LOG2_E)
    sp = jnp.maximum(z, 0.0) + jnp.log(1.0 + jnp.exp2(-jnp.abs(z))) * LOG2_E
    log_keep = jnp.where(before, -sp, 0.0)
    to_the_right, total = _running_sum(log_keep, tri_suffix, True)
    w = jnp.where(before, jnp.exp2(z - sp + to_the_right + c), 0.0)
    return w, jnp.exp2(z - sp), total


def _sb_fwd(name, proj, carry=None):
    def body(q_ref, k_ref, v_ref, o_ref):
        m0b, m1b = _head_masks(BF16)
        tri_suffix = (_iota((MXU_W, MXU_W), 0) > _iota((MXU_W, MXU_W), 1)).astype(BF16)

        def qblock(i, _):
            r0 = pl.multiple_of(i * QB, QB)
            q = q_ref[pl.ds(r0, QB), :].astype(BF16)
            qs = [q * m0b, q * m1b]
            rowg = r0 + _iota((QB, QB), 0)

            def kv(step, carry):
                cs, acc = carry
                c0 = pl.multiple_of((i - step) * QB, QB)
                k = k_ref[pl.ds(c0, QB), :].astype(BF16)
                v = v_ref[pl.ds(c0, QB), :].astype(BF16)
                before = (c0 + _iota((QB, QB), 1)) < rowg
                new_c = []
                for hh in range(2):
                    w, _, tot = _sb_weights(qs[hh], k, cs[hh], before, tri_suffix)
                    new_c.append(cs[hh] + tot)
                    acc = acc + _dot(w.astype(BF16), v * (m0b if hh == 0 else m1b))
                return tuple(new_c), acc

            init = ((jnp.zeros((QB, 1), F32),) * 2, jnp.zeros((QB, LANES), F32))
            _, acc = lax.fori_loop(0, i + 1, kv, init)
            o_ref[pl.ds(r0, QB), :] = acc.astype(BF16)
            return 0

        lax.fori_loop(0, S // QB, qblock, 0)

    col = lambda base: pl.BlockSpec((S, LANES), lambda p: (0, base // LANES + p))
    return _carrier_call(
        body, name=name, grid=(SB_H // 2,), in_specs=[col(P_QB), col(P_KB), col(P_VB)],
        out_specs=[pl.BlockSpec((S, LANES), lambda p: (0, p))],
        out_shape=[jax.ShapeDtypeStruct((S, SB_H * SB_DIM), BF16)],
        args=(proj, proj, proj), sem=("parallel",), carry=carry)


def _sb_bwd(name, proj, do, do_block0, carry=None):
    nb = S // QB

    def body(q_ref, k_ref, v_ref, do_ref, dq_ref, dk_ref, dv_ref, sig_scr, dl_scr, dk_acc, dv_acc):
        m0b, m1b = _head_masks(BF16)
        tri_suffix = (_iota((MXU_W, MXU_W), 0) > _iota((MXU_W, MXU_W), 1)).astype(BF16)
        tri_prefix = (_iota((MXU_W, MXU_W), 0) < _iota((MXU_W, MXU_W), 1)).astype(BF16)
        dk_acc[...] = jnp.zeros_like(dk_acc)
        dv_acc[...] = jnp.zeros_like(dv_acc)

        def qblock(i, _):
            r0 = pl.multiple_of(i * QB, QB)
            rows = pl.ds(r0, QB)
            q = q_ref[rows, :].astype(BF16)
            qs = [q * m0b, q * m1b]
            dob = do_ref[rows, :].astype(BF16)
            dos = [dob * m0b, dob * m1b]
            rowg = r0 + _iota((QB, QB), 0)

            def sweep_left(step, cs):
                kb = i - step
                c0 = pl.multiple_of(kb * QB, QB)
                cols = pl.ds(c0, QB)
                k = k_ref[cols, :].astype(BF16)
                v = v_ref[cols, :].astype(BF16)
                before = (c0 + _iota((QB, QB), 1)) < rowg
                new_c = []
                dv = None
                for hh in range(2):
                    w, sig, tot = _sb_weights(qs[hh], k, cs[hh], before, tri_suffix)
                    new_c.append(cs[hh] + tot)
                    sig_scr[hh, kb] = sig
                    dl_scr[hh, kb] = _dot(dos[hh], v, "nt") * w
                    part = _dot(w.astype(BF16), dos[hh], "tn")
                    dv = part if dv is None else dv + part
                dv_acc[cols, :] += dv
                return tuple(new_c)

            lax.fori_loop(0, i + 1, sweep_left, (jnp.zeros((QB, 1), F32),) * 2)

            def sweep_right(kb, carry):
                ps, dq = carry
                c0 = pl.multiple_of(kb * QB, QB)
                cols = pl.ds(c0, QB)
                k = k_ref[cols, :].astype(BF16)
                before = (c0 + _iota((QB, QB), 1)) < rowg
                new_p = []
                dk = None
                for hh in range(2):
                    dl = dl_scr[hh, kb]
                    sig = sig_scr[hh, kb]
                    to_the_left, total = _running_sum(dl, tri_prefix, False)
                    earlier = to_the_left + ps[hh]
                    new_p.append(ps[hh] + total)
                    dz = (jnp.where(before, dl * (1.0 - sig) - earlier * sig, 0.0) * SCALE_B).astype(BF16)
                    dq = dq + _dot(dz, k * (m0b if hh == 0 else m1b))
                    part = _dot(dz, qs[hh], "tn")
                    dk = part if dk is None else dk + part
                dk_acc[cols, :] += dk
                return tuple(new_p), dq

            init = ((jnp.zeros((QB, 1), F32),) * 2, jnp.zeros((QB, LANES), F32))
            _, dq = lax.fori_loop(0, i + 1, sweep_right, init)
            dq_ref[rows, :] = dq.astype(BF16)
            return 0

        lax.fori_loop(0, nb, qblock, 0)
        dk_ref[...] = dk_acc[...].astype(BF16)
        dv_ref[...] = dv_acc[...].astype(BF16)

    col = lambda base: pl.BlockSpec((S, LANES), lambda p: (0, base // LANES + p))
    out = pl.BlockSpec((S, LANES), lambda p: (0, p))
    shape = jax.ShapeDtypeStruct((S, SB_H * SB_DIM), BF16)
    return _carrier_call(
        body, name=name, grid=(SB_H // 2,),
        in_specs=[col(P_QB), col(P_KB), col(P_VB), pl.BlockSpec((S, LANES), lambda p: (0, do_block0 + p))],
        out_specs=[out, out, out], out_shape=[shape, shape, shape],
        scratch_shapes=[pltpu.VMEM((2, nb, QB, QB), F32), pltpu.VMEM((2, nb, QB, QB), F32),
                        pltpu.VMEM((S, LANES), F32), pltpu.VMEM((S, LANES), F32)],
        args=(proj, proj, proj, do), sem=("parallel",), carry=carry)


def _band_row_index():
    j = np.arange(TOEP_W)
    rel = np.clip(LEFT_CHUNKS * CHUNK - j, -REL_CLIP, REL_CLIP) + REL_CLIP
    rel[BAND_W:] = 2 * REL_CLIP
    return rel.astype(np.int32)


def _band_tiles(r0_ref, q_ref, kpad, vpad, m, m0b, m1b, static_ok, bias):
    r0 = pl.multiple_of(m * BQ, BQ)
    q = q_ref[0, pl.ds(r0, BQ), :]
    kw = kpad[pl.ds(r0, BAND_W), :]
    vw = vpad[pl.ds(r0, BAND_W), :]
    ok = static_ok & ((r0 - BAND_PAD + _iota((BQ, BAND_W), 1)) >= 0)
    qs = [q * m0b, q * m1b]
    ps = []
    for hh in range(2):
        s = jnp.where(ok, _dot(qs[hh], kw, "nt") * (SCALE_C * LOG2_E) + bias[hh], NEG)
        e = jnp.exp2(s - jnp.max(s, axis=-1, keepdims=True))
        ps.append(e * (1.0 / jnp.sum(e, axis=-1, keepdims=True)))
    return r0, qs, kw, vw, ps


def _band_setup(qkv_ref, r0_ref, kpad, vpad):
    kpad[0:BAND_PAD, :] = jnp.zeros((BAND_PAD, LANES), BF16)
    vpad[0:BAND_PAD, :] = jnp.zeros((BAND_PAD, LANES), BF16)
    kpad[BAND_PAD:, :] = qkv_ref[1]
    vpad[BAND_PAD:, :] = qkv_ref[2]
    jc = lax.shift_right_logical(_iota((BQ, BAND_W), 1), 6)
    rc = lax.shift_right_logical(_iota((BQ, BAND_W), 0), 6)
    static_ok = (jc >= rc) & (jc <= rc + LEFT_CHUNKS)
    bias = []
    for hh in range(2):
        row = jnp.broadcast_to(r0_ref[hh:hh + 1, :] * LOG2_E, (BQ, TOEP_W))
        bias.append(pltpu.roll(row, 0, 1, stride=1, stride_axis=0)[:, :BAND_W])
    return static_ok, bias


def _band_fwd(name, qkv, r0, carry=None):
    def body(qkv_ref, r0_ref, o_ref, kpad, vpad):
        m0b, m1b = _head_masks(BF16)
        static_ok, bias = _band_setup(qkv_ref, r0_ref, kpad, vpad)

        def qblock(m, _):
            r0_, _, _, vw, ps = _band_tiles(r0_ref, qkv_ref, kpad, vpad, m, m0b, m1b, static_ok, bias)
            o = _dot(ps[0].astype(BF16), vw * m0b) + _dot(ps[1].astype(BF16), vw * m1b)
            o_ref[pl.ds(r0_, BQ), :] = o.astype(BF16)
            return 0

        lax.fori_loop(0, S // BQ, qblock, 0)

    return _carrier_call(
        body, name=name, grid=(C_H // 2,),
        in_specs=[pl.BlockSpec((3, S, LANES), lambda p: (0, 0, p)), pl.BlockSpec((None, 2, TOEP_W), lambda p: (p, 0, 0))],
        out_specs=[pl.BlockSpec((S, LANES), lambda p: (0, p))],
        out_shape=[jax.ShapeDtypeStruct((S, C_H * C_DIM), BF16)],
        scratch_shapes=[pltpu.VMEM((S + BAND_PAD, LANES), BF16), pltpu.VMEM((S + BAND_PAD, LANES), BF16)],
        args=(qkv, r0), sem=("parallel",), carry=carry)


def _band_bwd(name, qkv, r0, do, carry=None):
    def body(qkv_ref, r0_ref, do_ref, dqkv_ref, dr0_ref, kpad, vpad, dkpad, dvpad, db_acc):
        m0b, m1b = _head_masks(BF16)
        static_ok, bias = _band_setup(qkv_ref, r0_ref, kpad, vpad)
        dkpad[...] = jnp.zeros_like(dkpad)
        dvpad[...] = jnp.zeros_like(dvpad)
        db_acc[...] = jnp.zeros_like(db_acc)

        def qblock(m, _):
            r0_, qs, kw, vw, ps = _band_tiles(r0_ref, qkv_ref, kpad, vpad, m, m0b, m1b, static_ok, bias)
            dob = do_ref[pl.ds(r0_, BQ), :].astype(BF16)
            dos = [dob * m0b, dob * m1b]
            dq = None
            dk = None
            dv = None
            for hh in range(2):
                p = ps[hh]
                dp = _dot(dos[hh], vw, "nt")
                ds = p * (dp - jnp.sum(dp * p, axis=-1, keepdims=True))
                db_acc[hh, :, 0:BAND_W] += ds
                dsb = (ds * SCALE_C).astype(BF16)
                t = _dot(dsb, kw * (m0b if hh == 0 else m1b))
                dq = t if dq is None else dq + t
                t = _dot(dsb, qs[hh], "tn")
                dk = t if dk is None else dk + t
                t = _dot(p.astype(BF16), dos[hh], "tn")
                dv = t if dv is None else dv + t
            dqkv_ref[0, pl.ds(r0_, BQ), :] = dq.astype(BF16)
            dkpad[pl.ds(r0_, BAND_W), :] += dk
            dvpad[pl.ds(r0_, BAND_W), :] += dv
            return 0

        lax.fori_loop(0, S // BQ, qblock, 0)
        dqkv_ref[1] = dkpad[BAND_PAD:, :].astype(BF16)
        dqkv_ref[2] = dvpad[BAND_PAD:, :].astype(BF16)
        sub = _iota((8, TOEP_W), 0)
        for hh in range(2):
            folded = db_acc[hh, 0:8, :]
            for a in range(1, BQ // 8):
                folded = folded + pltpu.roll(db_acc[hh, 8 * a:8 * a + 8, :], TOEP_W - 8 * a, 1)
            for bit in range(3):
                moved = pltpu.roll(folded, TOEP_W - (1 << bit), 1)
                folded = jnp.where((sub & (1 << bit)) != 0, moved, folded)
            dr0_ref[hh:hh + 1, :] = jnp.sum(folded, axis=0, keepdims=True)

    return _carrier_call(
        body, name=name, grid=(C_H // 2,),
        in_specs=[pl.BlockSpec((3, S, LANES), lambda p: (0, 0, p)), pl.BlockSpec((None, 2, TOEP_W), lambda p: (p, 0, 0)),
                  pl.BlockSpec((S, LANES), lambda p: (0, p))],
        out_specs=[pl.BlockSpec((3, S, LANES), lambda p: (0, 0, p)), pl.BlockSpec((None, 2, TOEP_W), lambda p: (p, 0, 0))],
        out_shape=[jax.ShapeDtypeStruct((3, S, C_H * C_DIM), BF16), jax.ShapeDtypeStruct((C_H // 2, 2, TOEP_W), F32)],
        scratch_shapes=[pltpu.VMEM((S + BAND_PAD, LANES), BF16), pltpu.VMEM((S + BAND_PAD, LANES), BF16),
                        pltpu.VMEM((S + BAND_PAD, LANES), F32), pltpu.VMEM((S + BAND_PAD, LANES), F32),
                        pltpu.VMEM((2, BQ, TOEP_W), F32)],
        args=(qkv, r0, do), sem=("parallel",), carry=carry)


def _bias_table_grad(name, dr0):
    w_out = 5 * LANES

    def body(d_ref, o_ref):
        j = _iota((TOEP_W, w_out), 0)
        rel = jnp.clip(LEFT_CHUNKS * CHUNK - j, -REL_CLIP, REL_CLIP) + REL_CLIP
        rel = jnp.where(j >= BAND_W, 2 * REL_CLIP, rel)
        onehot = (rel == _iota((TOEP_W, w_out), 1)).astype(BF16)
        d = d_ref[...]
        hi = d.astype(BF16)
        mid = (d - hi.astype(F32))
        mid_b = mid.astype(BF16)
        lo = (mid - mid_b.astype(F32)).astype(BF16)
        o_ref[...] = _dot(hi, onehot) + _dot(mid_b, onehot) + _dot(lo, onehot)

    return pl.pallas_call(
        body, name=name, out_shape=jax.ShapeDtypeStruct((C_H, w_out), F32),
        in_specs=[pl.BlockSpec((C_H, TOEP_W), lambda: (0, 0))], out_specs=pl.BlockSpec((C_H, w_out), lambda: (0, 0)),
        grid=(),
    )(dr0)


def _carry_gather(cy, slots, names, ici, d2d):
    idx = [cy.operand(slots[n], True) for n in names]
    n = len(names)
    base_i = cy.sems(3 * n) if ici else 0
    base_d = cy.sems(3 * n) if d2d else 0

    def piece(refs, t, slot, cc):
        return refs[idx[t]].at[slot, _half_rows(cc, slots[names[t]].shape[1] // 2), :]

    def over_ici(refs, send, recv, arriving):
        x, y, c, chips = _position()
        out = []
        for t in range(n):
            for j in range(3):
                r = piece(refs, t, 2 * chips[j][0] + chips[j][1] if arriving else 2 * x + y, c)
                out.append(_remote(r, r, send, recv, base_i + 3 * t + j, (*chips[j], c)))
        return out

    def over_d2d(refs, send, recv, arriving):
        x, y, c, chips = _position()
        out = []
        for t in range(n):
            for j in range(3):
                r = piece(refs, t, 2 * chips[j][0] + chips[j][1], 1 - c if arriving else c)
                out.append(_remote(r, r, send, recv, base_d + 3 * t + j, (x, y, 1 - c)))
        return out

    def start_ici(refs, fresh, send, recv):
        for cp in over_ici(refs, send, recv, False):
            cp.start()

    def wait_ici(refs, fresh, send, recv):
        for cp in over_ici(refs, send, recv, True):
            cp.wait_recv()
        for cp in over_ici(refs, send, recv, False):
            cp.wait_send()

    def start_d2d(refs, fresh, send, recv):
        for cp in over_d2d(refs, send, recv, False):
            cp.start()

    def wait_d2d(refs, fresh, send, recv):
        for cp in over_d2d(refs, send, recv, True):
            cp.wait_recv()
        for cp in over_d2d(refs, send, recv, False):
            cp.wait_send()

    if ici and d2d:
        cy.starts.append(start_ici)
        cy.finishes += [wait_ici, start_d2d, wait_d2d]
    elif ici:
        cy.starts.append(start_ici)
        cy.finishes.append(wait_ici)
    else:
        cy.starts.append(start_d2d)
        cy.finishes.append(wait_d2d)

    def done(aliased, fresh):
        for t, name in enumerate(names):
            slots[name] = aliased[idx[t]]

    cy.on_done.append(done)


def _carry_chip_exchange(cy, sums, got, names):
    idx = [cy.operand(sums[n], False) for n in names]
    out = [cy.result((3,) + sums[n].shape[1:], BF16) for n in names]
    base = cy.sems(3 * len(names))

    def copies(refs, fresh, send, recv):
        x, y, c, chips = _position()
        return [_remote(refs[idx[t]].at[2 * chips[j][0] + chips[j][1]], fresh[out[t]].at[j], send, recv, base + 3 * t + j,
                        (*chips[j], c)) for t in range(len(names)) for j in range(3)]

    def start(refs, fresh, send, recv):
        for cp in copies(refs, fresh, send, recv):
            cp.start()

    def wait(refs, fresh, send, recv):
        for cp in copies(refs, fresh, send, recv):
            cp.wait()

    cy.starts.append(start)
    cy.finishes.append(wait)

    def done(aliased, fresh):
        for t, name in enumerate(names):
            got[name] = fresh[out[t]]

    cy.on_done.append(done)


def _run_carry(name, cy):
    _, res = _carrier_call(None, name=name, grid=(), in_specs=[], out_specs=[], out_shape=[], args=(), sem=(), carry=cy)
    cy.done(res)


FIRST_WEIGHTS = ("ev_w_in", "ev_w_uq", "ev_w_ukv")
WEIGHTS_A = ("ev_w_out", "w_gate0", "w_up0")
WEIGHTS_B = ("w_down0", "od_w_qkv", "od_w_out")
WEIGHTS_C = ("w_gate1",)
WEIGHTS_D = ("w_up1", "w_down1")
GRAD_GROUPS = {"ffn1": ("w_gate1", "w_up1", "w_down1"), "od": ("od_w_qkv", "od_w_out"),
               "ffn0": ("w_gate0", "w_up0", "w_down0"), "ev_out": ("ev_w_out",),
               "ev": ("ev_w_in", "ev_w_uq", "ev_w_ukv")}


def _carry_pair_exchange(cy, parts, theirs, names):
    idx = [cy.operand(parts[n], False) for n in names]
    out = [cy.result((N_CHIPS, parts[n].shape[1] // 2, parts[n].shape[2]), BF16) for n in names]
    base = cy.sems(len(names))

    def copies(refs, fresh, send, recv):
        x, y, c, _ = _position()
        return [_remote(refs[idx[t]].at[:, _half_rows(1 - c, parts[n].shape[1] // 2), :], fresh[out[t]], send, recv,
                        base + t, (x, y, 1 - c)) for t, n in enumerate(names)]

    cy.starts.append(lambda refs, fresh, send, recv: [cp.start() for cp in copies(refs, fresh, send, recv)])
    cy.finishes.append(lambda refs, fresh, send, recv: [cp.wait() for cp in copies(refs, fresh, send, recv)])

    def done(aliased, fresh):
        for t, name in enumerate(names):
            theirs[name] = fresh[out[t]]

    cy.on_done.append(done)


def _carry_sibling_exchange(cy, fulls, pieces):
    idx = [cy.operand(fulls[p], True) for p, _ in pieces]
    base = cy.sems(len(pieces))

    def copies(refs, send, recv, arriving):
        x, y, c, _ = _position()
        out = []
        for t, (p, layer) in enumerate(pieces):
            r = refs[idx[t]].at[layer, _half_rows(1 - c if arriving else c, fulls[p].shape[1] // 2), :]
            out.append(_remote(r, r, send, recv, base + t, (x, y, 1 - c)))
        return out

    def start(refs, fresh, send, recv):
        for cp in copies(refs, send, recv, False):
            cp.start()

    def wait(refs, fresh, send, recv):
        for cp in copies(refs, send, recv, True):
            cp.wait_recv()
        for cp in copies(refs, send, recv, False):
            cp.wait_send()

    cy.starts.append(start)
    cy.finishes.append(wait)

    def done(aliased, fresh):
        for t, (p, _) in enumerate(pieces):
            fulls[p] = aliased[idx[t]]

    cy.on_done.append(done)


RIDES = {
    "mla_attn": (("gather_ici", WEIGHTS_A),),
    "sb_attn": (("gather_d2d", WEIGHTS_A), ("gather_ici", WEIGHTS_B)),
    "ev_out": (("gather_d2d", WEIGHTS_B),),
    "ffn0": (("gather_ici", WEIGHTS_C),),
    "qkv": (("gather_d2d", WEIGHTS_C),),
    "band_attn": (("gather_ici", WEIGHTS_D),),
    "od_out": (("gather_d2d", WEIGHTS_D),),
    "od_out_bwd_w": (("pair", "ffn1"),),
    "band_attn_bwd": (("chips", "ffn1"),),
    "rms_mix1_bwd": (("pair", "od"),),
    "ev_out_bwd_w": (("pair", "ffn0"),),
    "mla_attn_bwd": (("chips", "od"), ("sibling", "ffn1"), ("pair", "ev_out")),
    "sb_attn_bwd": (("chips", "ffn0"), ("sibling", "od"), ("chips", "ev_out")),
    "proj_in_bwd_w": (("sibling", "ffn0"), ("sibling", "ev_out")),
    "proj_in_bwd_x": (("chips", "ev"),),
}


class _Exchanges:
    def __init__(self, slots, pos, shapes):
        self.slots, self.pos, self.shapes = dict(slots), pos, shapes
        self.parts, self.theirs, self.sums, self.got, self.fulls = {}, {}, {}, {}, {}

    def begin(self):
        cy = _Carry()
        _carry_gather(cy, self.slots, FIRST_WEIGHTS, True, True)
        _run_carry("gather_first", cy)

    def weights(self, *names):
        return [self.slots[n] for n in names]

    def _pair_sums(self, group):
        for n in GRAD_GROUPS[group]:
            if n not in self.sums:
                self.sums[n] = _pair_sum("pair_sum_" + n, self.parts[n], self.theirs[n], self.pos)

    def _chip_sums(self, group):
        for n in GRAD_GROUPS[group]:
            param, layer = PART_OF[n]
            self.fulls[param] = _chip_sum("chip_sum_" + n, self.sums[n], self.got[n], self.pos, layer,
                                          self.shapes[param], self.fulls.get(param))

    def carry(self, stage):
        cy = _Carry()
        for step, what in RIDES[stage]:
            if step == "gather_ici":
                _carry_gather(cy, self.slots, what, True, False)
            elif step == "gather_d2d":
                _carry_gather(cy, self.slots, what, False, True)
            elif step == "pair":
                _carry_pair_exchange(cy, self.parts, self.theirs, GRAD_GROUPS[what])
            elif step == "chips":
                self._pair_sums(what)
                _carry_chip_exchange(cy, self.sums, self.got, GRAD_GROUPS[what])
            elif step == "sibling":
                self._chip_sums(what)
                _carry_sibling_exchange(cy, self.fulls, [PART_OF[n] for n in GRAD_GROUPS[what]])
        return cy

    def grads(self, group, parts):
        self.parts.update(parts)
        if group == "ev":
            cy = _Carry()
            _carry_pair_exchange(cy, self.parts, self.theirs, GRAD_GROUPS[group])
            _run_carry("grads_pair_ev", cy)

    def finish(self):
        cy = _Carry()
        self._chip_sums("ev")
        _carry_sibling_exchange(cy, self.fulls, [PART_OF[n] for n in GRAD_GROUPS["ev"]])
        _run_carry("grads_sibling_ev", cy)
        return {n: self.fulls[n] for n in BIG}


class _NoExchanges:
    def __init__(self, slots):
        self.slots, self.parts = dict(slots), {}

    def begin(self):
        pass

    def weights(self, *names):
        return [self.slots[n] for n in names]

    def carry(self, stage):
        return None

    def grads(self, group, parts):
        self.parts.update(parts)


def _first_weights(w_in_s, w_uq_s, w_ukv_s):
    gw = {"ev_w_in": w_in_s, "ev_w_uq": w_uq_s, "ev_w_ukv": w_ukv_s}
    w_in = jnp.moveaxis(gw["ev_w_in"], 0, 1).reshape(D, EVEN_IN)
    z = lambda n: jnp.zeros((D, n), BF16)
    w_in_p = jnp.concatenate(
        [w_in[:, 0:384], z(128), w_in[:, 384:640], w_in[:, 672:2208], z(KR_LANE), w_in[:, 640:672],
         z(LANES - KR_LANE - MLA_ROPE)], axis=1)
    w_uq = jnp.moveaxis(gw["ev_w_uq"], 0, 1).reshape(Q_LORA, MLA_H, MLA_NOPE + MLA_ROPE)
    w_uq_p = jnp.concatenate([w_uq, jnp.zeros((Q_LORA, MLA_H, LANES - MLA_NOPE - MLA_ROPE), BF16)], axis=2)
    w_ukv = jnp.moveaxis(gw["ev_w_ukv"], 0, 1).reshape(KV_LORA, MLA_H, MLA_NOPE + MLA_V)
    w_uk_p = jnp.concatenate([w_ukv[:, :, :MLA_NOPE], jnp.zeros((KV_LORA, MLA_H, LANES - MLA_NOPE), BF16)], axis=2)
    return dict(
        w_in=w_in_p, w_uq=w_uq_p.reshape(Q_LORA, MLA_H * LANES), w_uk=w_uk_p.reshape(KV_LORA, MLA_H * LANES),
        w_uv=w_ukv[:, :, MLA_NOPE:].reshape(KV_LORA, MLA_H * MLA_V))


def _proj_mm(name, u, w_in):
    return _mm(name, u, w_in, kind="nn", grid=(S // TM, 1, 1),
               a_spec=pl.BlockSpec((TM, D), lambda i, j, k: (i, 0)), b_spec=pl.BlockSpec((D, P_IN), lambda i, j, k: (0, 0)),
               o_spec=pl.BlockSpec((TM, P_IN), lambda i, j, k: (i, 0)), out_shape=(S, P_IN), out_dtype=F32, acc_shape=None)


def _out_proj(name, o, w, resid, carry=None):
    return _mm(name, o, w, kind="nn", grid=(S // TM, 1, 1),
               a_spec=pl.BlockSpec((TM, D), lambda i, j, k: (i, 0)), b_spec=pl.BlockSpec((D, D), lambda i, j, k: (0, 0)),
               o_spec=pl.BlockSpec((TM, D), lambda i, j, k: (i, 0)), out_shape=(S, D), out_dtype=F32, acc_shape=None,
               resid=resid, r_spec=pl.BlockSpec((TM, D), lambda i, j, k: (i, 0)), carry=carry)


def _out_proj_bwd(name, dh, o, w, ex):
    d_o = _mm(name + "_x", dh, w, kind="nt", grid=(S // TM, 1, 1),
              a_spec=pl.BlockSpec((TM, D), lambda i, j, k: (i, 0)), b_spec=pl.BlockSpec((D, D), lambda i, j, k: (0, 0)),
              o_spec=pl.BlockSpec((TM, D), lambda i, j, k: (i, 0)), out_shape=(S, D), out_dtype=F32, acc_shape=None)
    d_w = _mm(name + "_w", o, dh, kind="tn", grid=(2, S // TM),
              a_spec=pl.BlockSpec((TM, TM), lambda j, k: (k, j)), b_spec=pl.BlockSpec((TM, D), lambda j, k: (k, 0)),
              o_spec=pl.BlockSpec((TM, D), lambda j, k: (j, 0)), out_shape=(D, D), out_dtype=BF16, acc_shape=(TM, D),
              carry=ex.carry(name + "_w"))
    return d_o, d_w


def _local_step(x, tgt, sm, ex):
    def riding(stage, fn, *args):
        cy = ex.carry(stage)
        res, copies = fn(stage, *args, carry=cy)
        if cy is not None:
            cy.done(copies)
        return res

    cos_t, sin_t = _rope_tables()
    g_mix, g_ffn = sm["g_mix"], sm["g_ffn"]
    r0 = sm["od_rel_bias"][0][:, _band_row_index()].reshape(C_H // 2, 2, TOEP_W)
    nt = 3

    ex.begin()
    w = _first_weights(*ex.weights(*FIRST_WEIGHTS))
    u0 = _rms_fwd("rms_mix0", x, g_mix[0:1])
    proj = _proj_mm("proj_in", u0, w["w_in"])
    qa, ka, va = _mla_prep_fwd("mla_prep", proj, sm["ev_g_cq"], sm["ev_g_ckv"], w["w_uq"], w["w_uk"], w["w_uv"], cos_t, sin_t)
    o_a, lse = riding("mla_attn", _mla_fwd, qa, ka, va)
    o_b, = riding("sb_attn", _sb_fwd, proj)
    o_ev = jnp.concatenate([o_a.astype(BF16), o_b], axis=1)
    w["ev_w_out"] = ex.weights("ev_w_out")[0].reshape(D, D)
    h1 = _out_proj("ev_out", o_ev, w["ev_w_out"], x, ex.carry("ev_out"))
    w["w_gate0"], w["w_up0"], w["w_down0"] = ex.weights("w_gate0", "w_up0", "w_down0")
    h2, gate0, up0 = riding("ffn0", _ffn_fwd, h1, g_ffn[0:1], w["w_gate0"], w["w_up0"], w["w_down0"])
    w["w_qkv"] = jnp.moveaxis(ex.weights("od_w_qkv")[0], 0, 1).reshape(D, nt * D)
    u2 = _rms_fwd("rms_mix1", h2, g_mix[1:2])
    qkv = _mm("qkv", u2, w["w_qkv"], kind="nn", grid=(S // TM, nt, 1),
              a_spec=pl.BlockSpec((TM, D), lambda i, t, k: (i, 0)), b_spec=pl.BlockSpec((D, D), lambda i, t, k: (0, t)),
              o_spec=pl.BlockSpec((None, TM, D), lambda i, t, k: (t, i, 0)),
              out_shape=(nt, S, D), out_dtype=BF16, acc_shape=None, carry=ex.carry("qkv"))
    o_od, = riding("band_attn", _band_fwd, qkv, r0)
    w["od_w_out"] = ex.weights("od_w_out")[0].reshape(D, D)
    h3 = _out_proj("od_out", o_od, w["od_w_out"], h2, ex.carry("od_out"))
    w["w_gate1"], w["w_up1"], w["w_down1"] = ex.weights("w_gate1", "w_up1", "w_down1")
    (h4, gate1, up1), _ = _ffn_fwd("ffn1", h3, g_ffn[1:2], w["w_gate1"], w["w_up1"], w["w_down1"])

    loss, dh4, dg_final = _loss_bwd("loss", h4, sm["g_final"].reshape(1, D), tgt)

    dh3, dg_ffn1, u3, dgate, dup, act = _ffn_bwd("ffn1_bwd", dh4, h3, g_ffn[1:2], gate1, up1,
                                                 w["w_gate1"], w["w_up1"], w["w_down1"])
    d_wg1, d_wu1, d_wd1 = _ffn_wgrads("ffn1_dw", u3, dgate, dup, act, dh4)
    ex.grads("ffn1", {"w_gate1": d_wg1, "w_up1": d_wu1, "w_down1": d_wd1})

    d_ood, d_w_od_out = _out_proj_bwd("od_out_bwd", dh3, o_od, w["od_w_out"], ex)
    dqkv, dr0 = riding("band_attn_bwd", _band_bwd, qkv, r0, d_ood)
    du2 = _mm("qkv_bwd_x", dqkv, w["w_qkv"], kind="nt", grid=(S // TM, nt),
              a_spec=pl.BlockSpec((None, TM, D), lambda i, t: (t, i, 0)), b_spec=pl.BlockSpec((D, D), lambda i, t: (0, t)),
              o_spec=pl.BlockSpec((TM, D), lambda i, t: (i, 0)), out_shape=(S, D), out_dtype=F32, acc_shape=(TM, D))
    d_w_qkv = _mm("qkv_bwd_w", u2, dqkv, kind="tn", grid=(nt, S // TM),
                  a_spec=pl.BlockSpec((TM, D), lambda t, k: (k, 0)), b_spec=pl.BlockSpec((None, TM, D), lambda t, k: (t, k, 0)),
                  o_spec=pl.BlockSpec((D, D), lambda t, k: (0, t)), out_shape=(D, nt * D), out_dtype=BF16, acc_shape=(D, D))
    shard_cols = lambda a: jnp.moveaxis(a.reshape(a.shape[0], N_CHIPS, a.shape[1] // N_CHIPS), 1, 0)
    ex.grads("od", {"od_w_qkv": shard_cols(d_w_qkv), "od_w_out": d_w_od_out.reshape(N_CHIPS, D // N_CHIPS, D)})
    dh2, dg_mix1 = _rms_bwd("rms_mix1_bwd", du2, h2, g_mix[1:2], dh3, carry=ex.carry("rms_mix1_bwd"))
    d_rel = _bias_table_grad("rel_bias_grad", dr0.reshape(C_H, TOEP_W))[:, :2 * REL_CLIP + 1]

    dh1, dg_ffn0, u1, dgate, dup, act = _ffn_bwd("ffn0_bwd", dh2, h1, g_ffn[0:1], gate0, up0,
                                                 w["w_gate0"], w["w_up0"], w["w_down0"])
    d_wg0, d_wu0, d_wd0 = _ffn_wgrads("ffn0_dw", u1, dgate, dup, act, dh2)
    ex.grads("ffn0", {"w_gate0": d_wg0, "w_up0": d_wu0, "w_down0": d_wd0})

    d_oev, d_w_ev_out = _out_proj_bwd("ev_out_bwd", dh1, o_ev, w["ev_w_out"], ex)
    ex.grads("ev_out", {"ev_w_out": d_w_ev_out.reshape(N_CHIPS, D // N_CHIPS, D)})
    dqa, dka, dva = riding("mla_attn_bwd", _mla_bwd, qa, ka, va, o_a, lse, d_oev, 0)
    dqb, dkb, dvb = riding("sb_attn_bwd", _sb_bwd, proj, d_oev, MLA_H * MLA_V // LANES)
    dcq, dckv, dkr, d_w_uq, d_w_uk, d_w_uv, dg_cq, dg_ckv = _mla_prep_bwd(
        "mla_prep_bwd", dqa, dka, dva, proj, sm["ev_g_cq"], sm["ev_g_ckv"], w["w_uq"], w["w_uk"], w["w_uv"], cos_t, sin_t)
    dproj = jnp.concatenate([dcq, jnp.zeros((S, LANES), BF16), dckv, dqb, dkb, dvb, dkr], axis=1)
    d_w_in_p = _mm("proj_in_bwd_w", u0, dproj, kind="tn", grid=(1, S // TM),
                   a_spec=pl.BlockSpec((TM, D), lambda j, k: (k, 0)), b_spec=pl.BlockSpec((TM, P_IN), lambda j, k: (k, 0)),
                   o_spec=pl.BlockSpec((D, P_IN), lambda j, k: (0, 0)), out_shape=(D, P_IN), out_dtype=BF16,
                   acc_shape=(D, P_IN), carry=ex.carry("proj_in_bwd_w"))
    d_w_in = jnp.concatenate([d_w_in_p[:, 0:384], d_w_in_p[:, 512:768],
                              d_w_in_p[:, P_KR + KR_LANE:P_KR + KR_LANE + MLA_ROPE], d_w_in_p[:, 768:2304]], axis=1)
    d_w_uq_std = d_w_uq.reshape(Q_LORA, MLA_H, LANES)[:, :, :MLA_NOPE + MLA_ROPE].reshape(Q_LORA, -1)
    d_w_ukv = jnp.concatenate([d_w_uk.reshape(KV_LORA, MLA_H, LANES)[:, :, :MLA_NOPE],
                               d_w_uv.reshape(KV_LORA, MLA_H, MLA_V)], axis=2).reshape(KV_LORA, -1)
    ex.grads("ev", {"ev_w_in": shard_cols(d_w_in), "ev_w_uq": shard_cols(d_w_uq_std.astype(BF16)),
                    "ev_w_ukv": shard_cols(d_w_ukv.astype(BF16))})
    du0 = _mm("proj_in_bwd_x", dproj, w["w_in"], kind="nt", grid=(S // TM, 1, 1),
              a_spec=pl.BlockSpec((TM, P_IN), lambda i, j, k: (i, 0)), b_spec=pl.BlockSpec((D, P_IN), lambda i, j, k: (0, 0)),
              o_spec=pl.BlockSpec((TM, D), lambda i, j, k: (i, 0)), out_shape=(S, D), out_dtype=F32, acc_shape=None,
              carry=ex.carry("proj_in_bwd_x"))
    grad_x, dg_mix0 = _rms_bwd("rms_mix0_bwd", du0, x, g_mix[0:1], dh1)
    small = {
        "ev_g_cq": dg_cq, "ev_g_ckv": dg_ckv, "od_rel_bias": d_rel.reshape(1, C_H, 2 * REL_CLIP + 1),
        "g_mix": jnp.concatenate([dg_mix0, dg_mix1], axis=0), "g_ffn": jnp.concatenate([dg_ffn0, dg_ffn1], axis=0),
        "g_final": dg_final.reshape(D),
    }
    return loss, grad_x, small


BIG = ("ev_w_in", "ev_w_uq", "ev_w_ukv", "ev_w_out", "od_w_qkv", "od_w_out", "w_gate", "w_up", "w_down")
SMALL = ("ev_g_cq", "ev_g_ckv", "od_rel_bias", "g_mix", "g_ffn", "g_final")
WEIGHTS = ("ev_w_in", "ev_g_cq", "ev_w_uq", "ev_g_ckv", "ev_w_ukv", "ev_w_out", "od_w_qkv", "od_rel_bias", "od_w_out",
           "g_mix", "g_ffn", "w_gate", "w_up", "w_down", "g_final")
GRAD_PARTS = (("ev_w_in", "ev_w_in", 0), ("ev_w_uq", "ev_w_uq", 0), ("ev_w_ukv", "ev_w_ukv", 0),
              ("ev_w_out", "ev_w_out", 0), ("od_w_qkv", "od_w_qkv", 0), ("od_w_out", "od_w_out", 0),
              ("w_gate0", "w_gate", 0), ("w_gate1", "w_gate", 1), ("w_up0", "w_up", 0), ("w_up1", "w_up", 1),
              ("w_down0", "w_down", 0), ("w_down1", "w_down", 1))
PART_OF = {part: (param, layer) for part, param, layer in GRAD_PARTS}
SMALL_ROWS = 112
SMALL_SIZE = 384 + 256 + 16 * 513 + 2 * 1024 + 2 * 1024 + 1024
TRANSPOSED = ("w_gate", "w_up")


def _row_tile(rows, cap=512):
    for t in range(min(rows, cap), 0, -1):
        if rows % t == 0 and t % 16 == 0:
            return t
    return rows


def _cast_into_slot(name, w, layer, pos):
    _, rows, cols = w.shape
    tr = _row_tile(rows)

    def body(pos_ref, w_ref, o_ref):
        o_ref[...] = w_ref[...].astype(BF16)

    return pl.pallas_call(
        body, name=name,
        grid_spec=pltpu.PrefetchScalarGridSpec(
            num_scalar_prefetch=1, grid=(rows // tr,),
            in_specs=[pl.BlockSpec((None, tr, cols), lambda i, p: (layer, i, 0))],
            out_specs=pl.BlockSpec((None, tr, cols), lambda i, p: (p[0], i, 0))),
        out_shape=jax.ShapeDtypeStruct((N_CHIPS, rows, cols), BF16), compiler_params=_params("arbitrary"))(pos, w)


def _pair_sum(name, part, theirs, pos):
    _, half, cols = theirs.shape
    tr = _row_tile(half)
    nb = half // tr

    def body(pos_ref, a_ref, b_ref, o_ref):
        o_ref[...] = (a_ref[...].astype(F32) + b_ref[...].astype(F32)).astype(BF16)

    return pl.pallas_call(
        body, name=name,
        grid_spec=pltpu.PrefetchScalarGridSpec(
            num_scalar_prefetch=1, grid=(N_CHIPS, nb),
            in_specs=[pl.BlockSpec((None, tr, cols), lambda s, i, p: (s, p[1] * nb + i, 0)),
                      pl.BlockSpec((None, tr, cols), lambda s, i, p: (s, i, 0))],
            out_specs=pl.BlockSpec((None, tr, cols), lambda s, i, p: (s, i, 0))),
        out_shape=jax.ShapeDtypeStruct(theirs.shape, BF16),
        compiler_params=_params("arbitrary", "arbitrary"))(pos, part, theirs)


def _chip_sum(name, sums, got, pos, layer, full_shape, full=None):
    _, half, cols = sums.shape
    tr = _row_tile(half)
    nb = half // tr

    def body(pos_ref, s_ref, g_ref, *rest):
        out_ref = rest[-1]
        out_ref[...] = ((s_ref[...].astype(F32) + g_ref[0].astype(F32)) + g_ref[1].astype(F32)) + g_ref[2].astype(F32)

    in_specs = [pl.BlockSpec((None, tr, cols), lambda i, p: (p[0], i, 0)),
                pl.BlockSpec((3, tr, cols), lambda i, p: (0, i, 0))]
    args = [pos, sums, got]
    if full is not None:
        in_specs.append(ANY)
        args.append(full)
    return pl.pallas_call(
        body, name=name,
        grid_spec=pltpu.PrefetchScalarGridSpec(
            num_scalar_prefetch=1, grid=(nb,), in_specs=in_specs,
            out_specs=pl.BlockSpec((None, tr, cols), lambda i, p: (layer, p[1] * nb + i, 0))),
        out_shape=jax.ShapeDtypeStruct(full_shape, F32),
        input_output_aliases={3: 0} if full is not None else {},
        compiler_params=_params("arbitrary"))(*args)


def _all_reduce_small(name, packed):
    n_dev = 8

    def body(p_ref, o_ref, slots, send_sem, recv_sem):
        x, y, c, _ = _position()
        me = 4 * x + 2 * y + c

        def peer(k):
            return (1 - x if k & 4 else x, 1 - y if k & 2 else y, 1 - c if k & 1 else c)

        def logical(k):
            px, py, pc = peer(k)
            return 4 * px + 2 * py + pc

        slots[me] = p_ref[...]
        sends = [pltpu.make_async_remote_copy(
            src_ref=p_ref, dst_ref=slots.at[me], send_sem=send_sem.at[k], recv_sem=recv_sem.at[k],
            device_id=peer(k), device_id_type=MESH) for k in range(1, n_dev)]
        for cp in sends:
            cp.start()
        for k in range(1, n_dev):
            pltpu.make_async_remote_copy(
                src_ref=p_ref, dst_ref=slots.at[logical(k)], send_sem=send_sem.at[k], recv_sem=recv_sem.at[k],
                device_id=peer(k), device_id_type=MESH).wait_recv()
        for cp in sends:
            cp.wait_send()
        total = slots[0]
        for d in range(1, n_dev):
            total = total + slots[d]
        o_ref[...] = total

    vm = pl.BlockSpec(memory_space=pltpu.VMEM)
    return pl.pallas_call(
        body, name=name, in_specs=[vm], out_specs=vm, out_shape=jax.ShapeDtypeStruct(packed.shape, F32),
        scratch_shapes=[pltpu.VMEM((n_dev,) + packed.shape, F32), pltpu.SemaphoreType.DMA((n_dev,)),
                        pltpu.SemaphoreType.DMA((n_dev,))],
    )(packed)


def _adamw(name, w, g, m, v):
    rows, cols = w.shape
    tr = _row_tile(rows)

    def body(w_ref, g_ref, m_ref, v_ref, d_ref, mo_ref, vo_ref):
        gv = g_ref[...]
        m_new = ADAM_B1 * m_ref[...] + (1.0 - ADAM_B1) * gv
        v_new = ADAM_B2 * v_ref[...] + (1.0 - ADAM_B2) * (gv * gv)
        m_hat = m_new / (1.0 - ADAM_B1 ** ADAM_STEP)
        v_hat = v_new / (1.0 - ADAM_B2 ** ADAM_STEP)
        d_ref[...] = -ADAM_LR * (m_hat / (jnp.sqrt(v_hat) + ADAM_EPS) + ADAM_WD * w_ref[...])
        mo_ref[...] = m_new
        vo_ref[...] = v_new

    spec = pl.BlockSpec((tr, cols), lambda i: (i, 0))
    shape = jax.ShapeDtypeStruct((rows, cols), F32)
    return pl.pallas_call(body, name=name, grid=(rows // tr,), in_specs=[spec] * 4, out_specs=[spec] * 3,
                          out_shape=[shape] * 3, compiler_params=_params("parallel"))(w, g, m, v)


def _pack_small(tree, extra=None):
    pieces = [tree[n].reshape(-1).astype(F32) for n in SMALL]
    if extra is not None:
        pieces.append(extra.reshape(1).astype(F32))
    flat = jnp.concatenate(pieces)
    return jnp.pad(flat, (0, SMALL_ROWS * LANES - flat.shape[0])).reshape(SMALL_ROWS, LANES)


def _unpack_small(packed, like):
    flat = packed.reshape(-1)
    out, off = {}, 0
    for n in SMALL:
        size = int(np.prod(like[n].shape))
        out[n] = flat[off:off + size].reshape(like[n].shape)
        off += size
    return out


def kernel(x, ev_w_in, ev_g_cq, ev_w_uq, ev_g_ckv, ev_w_ukv, ev_w_out, od_w_qkv, od_rel_bias, od_w_out, g_mix, g_ffn, w_gate, w_up, w_down, g_final, loss_target, m_ev_w_in, m_ev_g_cq, m_ev_w_uq, m_ev_g_ckv, m_ev_w_ukv, m_ev_w_out, m_od_w_qkv, m_od_rel_bias, m_od_w_out, m_g_mix, m_g_ffn, m_w_gate, m_w_up, m_w_down, m_g_final, v_ev_w_in, v_ev_g_cq, v_ev_w_uq, v_ev_g_ckv, v_ev_w_ukv, v_ev_w_out, v_od_w_qkv, v_od_rel_bias, v_od_w_out, v_g_mix, v_g_ffn, v_w_gate, v_w_up, v_w_down, v_g_final):
    w = dict(ev_w_in=ev_w_in, ev_g_cq=ev_g_cq, ev_w_uq=ev_w_uq, ev_g_ckv=ev_g_ckv, ev_w_ukv=ev_w_ukv, ev_w_out=ev_w_out,
             od_w_qkv=od_w_qkv, od_rel_bias=od_rel_bias, od_w_out=od_w_out, g_mix=g_mix, g_ffn=g_ffn, w_gate=w_gate,
             w_up=w_up, w_down=w_down, g_final=g_final)
    m = dict(ev_w_in=m_ev_w_in, ev_g_cq=m_ev_g_cq, ev_w_uq=m_ev_w_uq, ev_g_ckv=m_ev_g_ckv, ev_w_ukv=m_ev_w_ukv,
             ev_w_out=m_ev_w_out, od_w_qkv=m_od_w_qkv, od_rel_bias=m_od_rel_bias, od_w_out=m_od_w_out, g_mix=m_g_mix,
             g_ffn=m_g_ffn, w_gate=m_w_gate, w_up=m_w_up, w_down=m_w_down, g_final=m_g_final)
    v = dict(ev_w_in=v_ev_w_in, ev_g_cq=v_ev_g_cq, ev_w_uq=v_ev_w_uq, ev_g_ckv=v_ev_g_ckv, ev_w_ukv=v_ev_w_ukv,
             ev_w_out=v_ev_w_out, od_w_qkv=v_od_w_qkv, od_rel_bias=v_od_rel_bias, od_w_out=v_od_w_out, g_mix=v_g_mix,
             g_ffn=v_g_ffn, w_gate=v_w_gate, w_up=v_w_up, w_down=v_w_down, g_final=v_g_final)
    flat2d = lambda a: a.reshape(-1, a.shape[-1])
    for tree in (w, m, v):
        for n in TRANSPOSED:
            tree[n] = jnp.swapaxes(tree[n], 1, 2)

    pos = jnp.stack([2 * lax.axis_index("x") + lax.axis_index("y"), lax.axis_index("c")]).astype(jnp.int32)

    slots = {part: _cast_into_slot("cast_" + part, w[n], layer, pos) for part, n, layer in GRAD_PARTS}
    ex = _Exchanges(slots, pos, {n: w[n].shape for n in BIG})

    loss_local, grad_x, small = _local_step(x[0], loss_target[0], {n: w[n] for n in SMALL}, ex)

    grads = ex.finish()
    small_sum = _all_reduce_small("small_sum", _pack_small(small, loss_local[0, 0]))
    grads.update(_unpack_small(small_sum, w))

    delta, new_m, new_v = {}, {}, {}
    for n in BIG:
        d_, m_, v_ = _adamw("adamw_" + n, flat2d(w[n]), flat2d(grads[n]), flat2d(m[n]), flat2d(v[n]))
        delta[n], new_m[n], new_v[n] = d_.reshape(w[n].shape), m_.reshape(w[n].shape), v_.reshape(w[n].shape)
    d_, m_, v_ = _adamw("adamw_small", _pack_small(w), small_sum, _pack_small(m), _pack_small(v))
    delta.update(_unpack_small(d_, w))
    new_m.update(_unpack_small(m_, w))
    new_v.update(_unpack_small(v_, w))
    for tree in (grads, delta, new_m, new_v):
        for n in TRANSPOSED:
            tree[n] = jnp.swapaxes(tree[n], 1, 2)

    loss = small_sum.reshape(-1)[SMALL_SIZE]
    return (loss, grad_x[None], *[grads[n] for n in WEIGHTS], *[delta[n] for n in WEIGHTS],
            *[new_m[n] for n in WEIGHTS], *[new_v[n] for n in WEIGHTS])
```

```python
import functools

import jax
import jax.numpy as jnp
import numpy as np
from jax import lax
from jax.experimental import pallas as pl
from jax.experimental.pallas import tpu as pltpu

F32 = jnp.float32
BF16 = jnp.bfloat16

S = 2048
D = 1024
CHUNK = 64
MLA_H, MLA_NOPE, MLA_ROPE, MLA_V = 8, 64, 32, 64
Q_LORA, KV_LORA = 384, 256
ROPE_THETA = 10000.0
SB_H, SB_DIM = 8, 64
C_H, C_DIM = 16, 64
LEFT_CHUNKS = 8
REL_CLIP = 256
D_FF = 2816
EVEN_IN = 2208
RMS_EPS = 1e-6
ADAM_LR, ADAM_B1, ADAM_B2, ADAM_EPS, ADAM_WD, ADAM_STEP = 0.001, 0.9, 0.999, 1e-08, 0.01, 10

N_CHIPS = 4
FF_SHARD = D_FF // N_CHIPS
SCALE_A = (MLA_NOPE + MLA_ROPE) ** -0.5
SCALE_B = SB_DIM ** -0.5
SCALE_C = C_DIM ** -0.5
NEG = -1e30
LOG2_E = 1.4426950408889634

LANES = 128
MXU_W = 256
VMEM_LIMIT_BYTES = 56 * 1024 * 1024
TM = 512
QB = 512
BQ = 256

P_CQ, P_CKV, P_QB, P_KB, P_VB, P_KR = 0, 512, 768, 1280, 1792, 2304
P_IN = 2432
KR_LANE = 64
BAND_W = BQ + LEFT_CHUNKS * CHUNK
BAND_PAD = 512
TOEP_W = 1024


def _params(*sem):
    return pltpu.CompilerParams(dimension_semantics=sem, vmem_limit_bytes=VMEM_LIMIT_BYTES)


MESH = pl.DeviceIdType.MESH
ANY = pl.BlockSpec(memory_space=pl.ANY)


def _position():
    x, y, c = lax.axis_index("x"), lax.axis_index("y"), lax.axis_index("c")
    other_chips = [(1 - x, y), (x, 1 - y), (1 - x, 1 - y)]
    return x, y, c, other_chips


def _half_rows(c, half):
    return pl.ds(pl.multiple_of(c * half, 16), half)


def _remote(ref_src, ref_dst, send, recv, k, device):
    return pltpu.make_async_remote_copy(src_ref=ref_src, dst_ref=ref_dst, send_sem=send.at[k], recv_sem=recv.at[k],
                                        device_id=device, device_id_type=MESH)


class _Carry:
    def __init__(self):
        self.operands, self.aliased, self.fresh = [], [], []
        self.n_sems = 0
        self.starts, self.finishes, self.on_done = [], [], []

    def operand(self, arr, aliased):
        for i, a in enumerate(self.operands):
            if a is arr:
                return i
        self.operands.append(arr)
        self.aliased.append(aliased)
        return len(self.operands) - 1

    def result(self, shape, dtype):
        self.fresh.append(jax.ShapeDtypeStruct(shape, dtype))
        return len(self.fresh) - 1

    def sems(self, k):
        base = self.n_sems
        self.n_sems += k
        return base

    def done(self, results):
        aliased, fresh = results
        for f in self.on_done:
            f(aliased, fresh)


def _carrier_call(body, *, name, grid, in_specs, out_specs, out_shape, args, sem, scratch_shapes=(), carry=None):
    in_specs, out_specs, out_shape, scratch = list(in_specs), list(out_specs), list(out_shape), list(scratch_shapes)
    if carry is None:
        res = pl.pallas_call(body, name=name, grid=grid, in_specs=in_specs, out_specs=out_specs, out_shape=out_shape,
                             scratch_shapes=scratch, compiler_params=_params(*sem))(*args)
        return list(res), None
    ops = carry.operands
    alias_idx = [i for i, a in enumerate(carry.aliased) if a]
    c_shapes = [jax.ShapeDtypeStruct(ops[i].shape, ops[i].dtype) for i in alias_idx] + carry.fresh
    n_in, n_out, n_scr = len(args), len(out_shape), len(scratch)

    def wrapped(*refs):
        ins, c_ins = refs[:n_in], refs[n_in:n_in + len(ops)]
        o0 = n_in + len(ops)
        outs, c_outs = refs[o0:o0 + n_out], refs[o0 + n_out:o0 + n_out + len(c_shapes)]
        s0 = o0 + n_out + len(c_shapes)
        scr, send, recv = refs[s0:s0 + n_scr], refs[s0 + n_scr], refs[s0 + n_scr + 1]
        use = list(c_ins)
        for k, i in enumerate(alias_idx):
            use[i] = c_outs[k]
        fresh = c_outs[len(alias_idx):]

        def run(steps):
            for step in steps:
                step(use, fresh, send, recv)

        if not grid:
            run(carry.starts)
            if body is not None:
                body(*ins, *outs, *scr)
            run(carry.finishes)
            return
        ids = [pl.program_id(a) for a in range(len(grid))]
        first = functools.reduce(jnp.logical_and, [i == 0 for i in ids])
        last = functools.reduce(jnp.logical_and, [i == g - 1 for i, g in zip(ids, grid)])

        @pl.when(first)
        def _():
            run(carry.starts)

        body(*ins, *outs, *scr)

        @pl.when(last)
        def _():
            run(carry.finishes)

    res = pl.pallas_call(
        wrapped, name=name, grid=grid, in_specs=in_specs + [ANY] * len(ops), out_specs=out_specs + [ANY] * len(c_shapes),
        out_shape=out_shape + c_shapes,
        scratch_shapes=scratch + [pltpu.SemaphoreType.DMA((carry.n_sems,)), pltpu.SemaphoreType.DMA((carry.n_sems,))],
        input_output_aliases={n_in + i: n_out + k for k, i in enumerate(alias_idx)},
        compiler_params=_params(*(("arbitrary",) * len(grid))),
    )(*args, *ops)
    res = list(res)
    c_res = res[n_out:]
    return res[:n_out], ({i: c_res[k] for k, i in enumerate(alias_idx)}, c_res[len(alias_idx):])


_DIMS = {"nn": (((1,), (0,)), ((), ())), "nt": (((1,), (1,)), ((), ())), "tn": (((0,), (0,)), ((), ()))}


def _dot(a, b, kind="nn"):
    return lax.dot_general(a, b, _DIMS[kind], preferred_element_type=F32)


def _iota(shape, dim):
    return lax.broadcasted_iota(jnp.int32, shape, dim)


def _sigmoid(x):
    return 1.0 / (1.0 + jnp.exp(-x))


def _split_dot(x, tri):
    hi = x.astype(BF16)
    lo = (x - hi.astype(F32)).astype(BF16)
    both = _dot(jnp.concatenate([hi, lo], axis=0), tri)
    return both[:x.shape[0]] + both[x.shape[0]:]


def _running_sum(x, tri, reverse):
    n = x.shape[1] // MXU_W
    blocks = [x[:, b * MXU_W:(b + 1) * MXU_W] for b in range(n)]
    out = [None] * n
    carry = None
    for b in (range(n - 1, -1, -1) if reverse else range(n)):
        part = _split_dot(blocks[b], tri)
        out[b] = part if carry is None else part + carry
        total = jnp.sum(blocks[b], axis=-1, keepdims=True)
        carry = total if carry is None else carry + total
    return (jnp.concatenate(out, axis=1) if n > 1 else out[0]), carry


def _mm(name, a, b, *, kind, grid, a_spec, b_spec, o_spec, out_shape, out_dtype, acc_shape, resid=None, r_spec=None,
        carry=None):
    nk = grid[-1]
    has_r = resid is not None

    def body(*refs):
        a_ref, b_ref = refs[0], refs[1]
        r_ref = refs[2] if has_r else None
        o_ref = refs[2 + has_r]
        part = _dot(a_ref[...].astype(BF16), b_ref[...].astype(BF16), kind)

        def finish(total):
            if has_r:
                total = total + r_ref[...].astype(F32)
            o_ref[...] = total.astype(out_dtype)

        if nk == 1:
            finish(part)
        else:
            acc_ref = refs[3 + has_r]
            k = pl.program_id(len(grid) - 1)

            @pl.when(k == 0)
            def _():
                acc_ref[...] = part

            @pl.when(k > 0)
            def _():
                acc_ref[...] += part

            @pl.when(k == nk - 1)
            def _():
                finish(acc_ref[...])

    in_specs = [a_spec, b_spec] + ([r_spec] if has_r else [])
    args = (a, b) + ((resid,) if has_r else ())
    sem = ("parallel",) * (len(grid) - 1) + ("arbitrary",)
    res, copies = _carrier_call(
        body, name=name, grid=grid, in_specs=in_specs, out_specs=[o_spec],
        out_shape=[jax.ShapeDtypeStruct(out_shape, out_dtype)],
        scratch_shapes=[pltpu.VMEM(acc_shape, F32)] if nk > 1 else [], args=args, sem=sem, carry=carry)
    if carry is not None:
        carry.done(copies)
    return res[0]


def _rms_fwd(name, x, g, col_block=0):
    c = g.shape[1]

    def body(x_ref, g_ref, u_ref):
        xv = x_ref[...]
        r = lax.rsqrt(jnp.mean(xv * xv, axis=-1, keepdims=True) + RMS_EPS)
        u_ref[...] = (xv * r * g_ref[...]).astype(BF16)

    return pl.pallas_call(
        body, name=name, grid=(S // TM,),
        in_specs=[pl.BlockSpec((TM, c), lambda i: (i, col_block)), pl.BlockSpec((1, c), lambda i: (0, 0))],
        out_specs=pl.BlockSpec((TM, c), lambda i: (i, 0)),
        out_shape=jax.ShapeDtypeStruct((S, c), BF16),
        compiler_params=_params("parallel"),
    )(x, g)


def _rms_bwd(name, dy, x, g, resid, carry=None):
    def body(dy_ref, x_ref, g_ref, r_ref, dx_ref, dg_ref):
        i = pl.program_id(0)
        xv = x_ref[...]
        r = lax.rsqrt(jnp.mean(xv * xv, axis=-1, keepdims=True) + RMS_EPS)
        xh = xv * r
        dyv = dy_ref[...]
        dxh = dyv * g_ref[...]
        dx_ref[...] = r_ref[...] + r * (dxh - xh * jnp.mean(dxh * xh, axis=-1, keepdims=True))
        part = jnp.sum(dyv * xh, axis=0, keepdims=True)

        @pl.when(i == 0)
        def _():
            dg_ref[...] = part

        @pl.when(i > 0)
        def _():
            dg_ref[...] += part

    row = pl.BlockSpec((TM, D), lambda i: (i, 0))
    vec = pl.BlockSpec((1, D), lambda i: (0, 0))
    res, copies = _carrier_call(
        body, name=name, grid=(S // TM,), in_specs=[row, row, vec, row], out_specs=[row, vec],
        out_shape=[jax.ShapeDtypeStruct((S, D), F32), jax.ShapeDtypeStruct((1, D), F32)],
        args=(dy, x, g, resid), sem=("arbitrary",), carry=carry)
    if carry is not None:
        carry.done(copies)
    return res


def _loss_bwd(name, h, g, tgt):
    def body(h_ref, g_ref, t_ref, loss_ref, dh_ref, dg_ref):
        i = pl.program_id(0)
        xv = h_ref[...]
        gv = g_ref[...]
        r = lax.rsqrt(jnp.mean(xv * xv, axis=-1, keepdims=True) + RMS_EPS)
        xh = xv * r
        diff = xh * gv - t_ref[...]
        part_loss = 0.5 * jnp.sum(jnp.sum(diff * diff, axis=-1, keepdims=True) * (1.0 / D), axis=0, keepdims=True)
        dy = diff * (1.0 / D)
        dxh = dy * gv
        dh_ref[...] = r * (dxh - xh * jnp.mean(dxh * xh, axis=-1, keepdims=True))
        part_g = jnp.sum(dy * xh, axis=0, keepdims=True)

        @pl.when(i == 0)
        def _():
            dg_ref[...] = part_g
            loss_ref[...] = jnp.broadcast_to(part_loss, (1, LANES))

        @pl.when(i > 0)
        def _():
            dg_ref[...] += part_g
            loss_ref[...] += jnp.broadcast_to(part_loss, (1, LANES))

    row = pl.BlockSpec((TM, D), lambda i: (i, 0))
    vec = pl.BlockSpec((1, D), lambda i: (0, 0))
    return pl.pallas_call(
        body, name=name, grid=(S // TM,), in_specs=[row, vec, row],
        out_specs=[pl.BlockSpec((1, LANES), lambda i: (0, 0)), row, vec],
        out_shape=[jax.ShapeDtypeStruct((1, LANES), F32), jax.ShapeDtypeStruct((S, D), F32),
                   jax.ShapeDtypeStruct((1, D), F32)],
        compiler_params=_params("arbitrary"),
    )(h, g, tgt)


def _ffn_fwd(name, h, g, wg, wu, wd, carry=None):
    def body(h_ref, g_ref, wg_ref, wu_ref, wd_ref, o_ref, gate_ref, up_ref, u_scr):
        s = pl.program_id(1)

        @pl.when(s == 0)
        def _():
            xv = h_ref[...]
            r = lax.rsqrt(jnp.mean(xv * xv, axis=-1, keepdims=True) + RMS_EPS)
            u_scr[...] = (xv * r * g_ref[...]).astype(BF16)
            o_ref[...] = xv

        u = u_scr[...]
        gate = _dot(u, wg_ref[...], "nt")
        up = _dot(u, wu_ref[...], "nt")
        act = gate * _sigmoid(gate) * up
        o_ref[...] += _dot(act.astype(BF16), wd_ref[...])
        gate_ref[...] = gate.astype(BF16)
        up_ref[...] = up.astype(BF16)

    row = pl.BlockSpec((TM, D), lambda i, s: (i, 0))
    hid = pl.BlockSpec((None, TM, FF_SHARD), lambda i, s: (s, i, 0))
    return _carrier_call(
        body, name=name, grid=(S // TM, N_CHIPS),
        in_specs=[row, pl.BlockSpec((1, D), lambda i, s: (0, 0))]
        + [pl.BlockSpec((None, FF_SHARD, D), lambda i, s: (s, 0, 0))] * 3,
        out_specs=[row, hid, hid],
        out_shape=[jax.ShapeDtypeStruct((S, D), F32), jax.ShapeDtypeStruct((N_CHIPS, S, FF_SHARD), BF16),
                   jax.ShapeDtypeStruct((N_CHIPS, S, FF_SHARD), BF16)],
        scratch_shapes=[pltpu.VMEM((TM, D), BF16)], args=(h, g, wg, wu, wd), sem=("parallel", "arbitrary"), carry=carry)


def _ffn_bwd(name, dh, h, g, gate, up, wg, wu, wd):
    def body(dh_ref, h_ref, g_ref, gate_ref, up_ref, wg_ref, wu_ref, wd_ref,
             dhin_ref, dg_ref, u_ref, dgate_ref, dup_ref, act_ref, dhb_scr, du_scr):
        i = pl.program_id(0)
        s = pl.program_id(1)

        @pl.when(s == 0)
        def _():
            xv = h_ref[...]
            r = lax.rsqrt(jnp.mean(xv * xv, axis=-1, keepdims=True) + RMS_EPS)
            u_ref[...] = (xv * r * g_ref[...]).astype(BF16)
            dhb_scr[...] = dh_ref[...].astype(BF16)
            du_scr[...] = jnp.zeros_like(du_scr)

        dact = _dot(dhb_scr[...], wd_ref[...], "nt")
        gv = gate_ref[...].astype(F32)
        uv = up_ref[...].astype(F32)
        sig = _sigmoid(gv)
        sil = gv * sig
        dup = dact * sil
        dgate = dact * uv * (sig * (1.0 + gv * (1.0 - sig)))
        dgb = dgate.astype(BF16)
        dub = dup.astype(BF16)
        act_ref[...] = (sil * uv).astype(BF16)
        dgate_ref[...] = dgb
        dup_ref[...] = dub
        du_scr[...] += _dot(dgb, wg_ref[...]) + _dot(dub, wu_ref[...])

        @pl.when(s == N_CHIPS - 1)
        def _():
            xv = h_ref[...]
            r = lax.rsqrt(jnp.mean(xv * xv, axis=-1, keepdims=True) + RMS_EPS)
            xh = xv * r
            du = du_scr[...]
            dxh = du * g_ref[...]
            dhin_ref[...] = dh_ref[...] + r * (dxh - xh * jnp.mean(dxh * xh, axis=-1, keepdims=True))
            part = jnp.sum(du * xh, axis=0, keepdims=True)

            @pl.when(i == 0)
            def _():
                dg_ref[...] = part

            @pl.when(i > 0)
            def _():
                dg_ref[...] += part

    row = pl.BlockSpec((TM, D), lambda i, s: (i, 0))
    vec = pl.BlockSpec((1, D), lambda i, s: (0, 0))
    hid = pl.BlockSpec((None, TM, FF_SHARD), lambda i, s: (s, i, 0))
    hid_shape = jax.ShapeDtypeStruct((N_CHIPS, S, FF_SHARD), BF16)
    return pl.pallas_call(
        body, name=name, grid=(S // TM, N_CHIPS),
        in_specs=[row, row, vec, hid, hid] + [pl.BlockSpec((None, FF_SHARD, D), lambda i, s: (s, 0, 0))] * 3,
        out_specs=[row, vec, row, hid, hid, hid],
        out_shape=[jax.ShapeDtypeStruct((S, D), F32), jax.ShapeDtypeStruct((1, D), F32),
                   jax.ShapeDtypeStruct((S, D), BF16), hid_shape, hid_shape, hid_shape],
        scratch_shapes=[pltpu.VMEM((TM, D), BF16), pltpu.VMEM((TM, D), F32)],
        compiler_params=_params("arbitrary", "arbitrary"),
    )(dh, h, g, gate, up, wg, wu, wd)


def _ffn_wgrads(name, u, dgate, dup, act, dh):
    nk = S // TM

    def body(u_ref, dh_ref, dgate_ref, dup_ref, act_ref, dg_ref, du_ref, dd_ref, acc_g, acc_u, acc_d):
        k = pl.program_id(1)
        u = u_ref[...]
        parts = (_dot(dgate_ref[...], u, "tn"), _dot(dup_ref[...], u, "tn"),
                 _dot(act_ref[...], dh_ref[...].astype(BF16), "tn"))
        accs = (acc_g, acc_u, acc_d)

        @pl.when(k == 0)
        def _():
            for acc, part in zip(accs, parts):
                acc[...] = part

        @pl.when(k > 0)
        def _():
            for acc, part in zip(accs, parts):
                acc[...] += part

        @pl.when(k == nk - 1)
        def _():
            for out, acc in zip((dg_ref, du_ref, dd_ref), accs):
                out[...] = acc[...].astype(BF16)

    tok = pl.BlockSpec((TM, D), lambda s, k: (k, 0))
    hid = pl.BlockSpec((None, TM, FF_SHARD), lambda s, k: (s, k, 0))
    out = pl.BlockSpec((None, FF_SHARD, D), lambda s, k: (s, 0, 0))
    shape = jax.ShapeDtypeStruct((N_CHIPS, FF_SHARD, D), BF16)
    return pl.pallas_call(
        body, name=name, grid=(N_CHIPS, nk), in_specs=[tok, tok, hid, hid, hid], out_specs=[out, out, out],
        out_shape=[shape, shape, shape], scratch_shapes=[pltpu.VMEM((FF_SHARD, D), F32)] * 3,
        compiler_params=_params("parallel", "arbitrary"))(u, dh, dgate, dup, act)


def _rope_tables():
    pos = jnp.arange(S, dtype=F32)
    inv = ROPE_THETA ** (-jnp.arange(0, MLA_ROPE, 2, dtype=F32) / MLA_ROPE)
    ang = pos[:, None] * inv[None, :]
    half = MLA_ROPE // 2
    cos = jnp.cos(ang)
    sin = jnp.sin(ang)
    one = jnp.ones((S, KR_LANE), F32)
    zero = jnp.zeros((S, KR_LANE), F32)
    tail_one = jnp.ones((S, LANES - KR_LANE - MLA_ROPE), F32)
    tail_zero = jnp.zeros((S, LANES - KR_LANE - MLA_ROPE), F32)
    cos_t = jnp.concatenate([one, cos, cos, tail_one], axis=1)
    sin_t = jnp.concatenate([zero, -sin, sin, tail_zero], axis=1)
    assert cos_t.shape == (S, LANES) and half * 2 == MLA_ROPE
    return cos_t, sin_t


def _rope(x, cos_t, sin_t, sign):
    n = x.shape[1] // LANES
    half = MLA_ROPE // 2
    lane = _iota(x.shape, 1) & (LANES - 1)
    first = (lane >= KR_LANE) & (lane < KR_LANE + half)
    swapped = jnp.where(first, pltpu.roll(x, x.shape[1] - half, 1), pltpu.roll(x, half, 1))
    c = jnp.tile(cos_t, (1, n)) if n > 1 else cos_t
    s = jnp.tile(sin_t, (1, n)) if n > 1 else sin_t
    return x * c + swapped * (s * sign)


def _mla_prep_fwd(name, proj, g_cq, g_ckv, w_uq, w_uk, w_uv, cos_t, sin_t):
    nh = MLA_H * LANES

    def body(cq_ref, ckv_ref, kr_ref, gq_ref, gkv_ref, wq_ref, wk_ref, wv_ref, cos_ref, sin_ref,
             qa_ref, ka_ref, va_ref):
        cos_v, sin_v = cos_ref[...], sin_ref[...]
        cq = cq_ref[...]
        r = lax.rsqrt(jnp.mean(cq * cq, axis=-1, keepdims=True) + RMS_EPS)
        cqn = (cq * r * gq_ref[...]).astype(BF16)
        qa_ref[...] = _rope(_dot(cqn, wq_ref[...]), cos_v, sin_v, 1.0).astype(BF16)
        ckv = ckv_ref[...]
        r = lax.rsqrt(jnp.mean(ckv * ckv, axis=-1, keepdims=True) + RMS_EPS)
        ckvn = (ckv * r * gkv_ref[...]).astype(BF16)
        lane = _iota((TM, LANES), 1)
        rot = (lane >= KR_LANE) & (lane < KR_LANE + MLA_ROPE)
        kr = jnp.where(rot, _rope(kr_ref[...], cos_v, sin_v, 1.0), 0.0)
        ka_ref[...] = (_dot(ckvn, wk_ref[...]) + jnp.tile(kr, (1, MLA_H))).astype(BF16)
        va_ref[...] = _dot(ckvn, wv_ref[...]).astype(BF16)

    full = lambda shape: pl.BlockSpec(shape, lambda i: (0, 0))
    return pl.pallas_call(
        body, name=name, grid=(S // TM,),
        in_specs=[pl.BlockSpec((TM, Q_LORA), lambda i: (i, P_CQ // Q_LORA)),
                  pl.BlockSpec((TM, KV_LORA), lambda i: (i, P_CKV // KV_LORA)),
                  pl.BlockSpec((TM, LANES), lambda i: (i, P_KR // LANES)),
                  full((1, Q_LORA)), full((1, KV_LORA)), full((Q_LORA, nh)), full((KV_LORA, nh)),
                  full((KV_LORA, MLA_H * MLA_V)),
                  pl.BlockSpec((TM, LANES), lambda i: (i, 0)), pl.BlockSpec((TM, LANES), lambda i: (i, 0))],
        out_specs=[pl.BlockSpec((TM, nh), lambda i: (i, 0)), pl.BlockSpec((TM, nh), lambda i: (i, 0)),
                   pl.BlockSpec((TM, MLA_H * MLA_V), lambda i: (i, 0))],
        out_shape=[jax.ShapeDtypeStruct((S, nh), BF16), jax.ShapeDtypeStruct((S, nh), BF16),
                   jax.ShapeDtypeStruct((S, MLA_H * MLA_V), BF16)],
        compiler_params=_params("parallel"),
    )(proj, proj, proj, g_cq, g_ckv, w_uq, w_uk, w_uv, cos_t, sin_t)


def _mla_prep_bwd(name, dqa, dka, dva, proj, g_cq, g_ckv, w_uq, w_uk, w_uv, cos_t, sin_t):
    nh = MLA_H * LANES

    def body(dqa_ref, dka_ref, dva_ref, cq_ref, ckv_ref, gq_ref, gkv_ref, wq_ref, wk_ref, wv_ref, cos_ref, sin_ref,
             dcq_ref, dckv_ref, dkr_ref, dwq_ref, dwk_ref, dwv_ref, dgq_ref, dgkv_ref):
        i = pl.program_id(0)
        cos_v, sin_v = cos_ref[...], sin_ref[...]

        def norm_bwd(x, g, dn):
            r = lax.rsqrt(jnp.mean(x * x, axis=-1, keepdims=True) + RMS_EPS)
            xh = x * r
            dxh = dn * g
            dx = r * (dxh - xh * jnp.mean(dxh * xh, axis=-1, keepdims=True))
            return dx, jnp.sum(dn * xh, axis=0, keepdims=True), (xh * g).astype(BF16)

        dq = _rope(dqa_ref[...], cos_v, sin_v, -1.0).astype(BF16)
        dcqn = _dot(dq, wq_ref[...], "nt")
        dcq, dgq, cqn = norm_bwd(cq_ref[...], gq_ref[...], dcqn)
        dcq_ref[...] = dcq.astype(BF16)
        dwq = _dot(cqn, dq, "tn")

        dka = dka_ref[...]
        dkab = dka.astype(BF16)
        dvab = dva_ref[...].astype(BF16)
        dckvn = _dot(dkab, wk_ref[...], "nt") + _dot(dvab, wv_ref[...], "nt")
        dckv, dgkv, ckvn = norm_bwd(ckv_ref[...], gkv_ref[...], dckvn)
        dckv_ref[...] = dckv.astype(BF16)
        dwk = _dot(ckvn, dkab, "tn")
        dwv = _dot(ckvn, dvab, "tn")

        fold = dka[:, 0:LANES]
        for hh in range(1, MLA_H):
            fold = fold + dka[:, hh * LANES:(hh + 1) * LANES]
        lane = _iota((TM, LANES), 1)
        rot = (lane >= KR_LANE) & (lane < KR_LANE + MLA_ROPE)
        dkr = _rope(jnp.where(rot, fold, 0.0), cos_v, sin_v, -1.0)
        dkr_ref[...] = jnp.where(rot, dkr, 0.0).astype(BF16)

        @pl.when(i == 0)
        def _():
            dwq_ref[...] = dwq
            dwk_ref[...] = dwk
            dwv_ref[...] = dwv
            dgq_ref[...] = dgq
            dgkv_ref[...] = dgkv

        @pl.when(i > 0)
        def _():
            dwq_ref[...] += dwq
            dwk_ref[...] += dwk
            dwv_ref[...] += dwv
            dgq_ref[...] += dgq
            dgkv_ref[...] += dgkv

    full = lambda shape: pl.BlockSpec(shape, lambda i: (0, 0))
    rows = lambda c: pl.BlockSpec((TM, c), lambda i: (i, 0))
    nv = MLA_H * MLA_V
    return pl.pallas_call(
        body, name=name, grid=(S // TM,),
        in_specs=[rows(nh), rows(nh), rows(nv),
                  pl.BlockSpec((TM, Q_LORA), lambda i: (i, P_CQ // Q_LORA)),
                  pl.BlockSpec((TM, KV_LORA), lambda i: (i, P_CKV // KV_LORA)),
                  full((1, Q_LORA)), full((1, KV_LORA)), full((Q_LORA, nh)), full((KV_LORA, nh)), full((KV_LORA, nv)),
                  rows(LANES), rows(LANES)],
        out_specs=[rows(Q_LORA), rows(KV_LORA), rows(LANES), full((Q_LORA, nh)), full((KV_LORA, nh)),
                   full((KV_LORA, nv)), full((1, Q_LORA)), full((1, KV_LORA))],
        out_shape=[jax.ShapeDtypeStruct((S, Q_LORA), BF16), jax.ShapeDtypeStruct((S, KV_LORA), BF16),
                   jax.ShapeDtypeStruct((S, LANES), BF16), jax.ShapeDtypeStruct((Q_LORA, nh), F32),
                   jax.ShapeDtypeStruct((KV_LORA, nh), F32), jax.ShapeDtypeStruct((KV_LORA, nv), F32),
                   jax.ShapeDtypeStruct((1, Q_LORA), F32), jax.ShapeDtypeStruct((1, KV_LORA), F32)],
        compiler_params=_params("arbitrary"),
    )(dqa, dka, dva, proj, proj, g_cq, g_ckv, w_uq, w_uk, w_uv, cos_t, sin_t)


def _head_masks(dtype):
    lane = _iota((1, LANES), 1)
    return (lane < 64).astype(dtype), (lane >= 64).astype(dtype)


def _mla_fwd(name, qa, ka, va, carry=None):
    def body(q_ref, k_ref, v_ref, o_ref, lse_ref):
        m0b, m1b = _head_masks(BF16)
        lane = _iota((QB, LANES), 1)
        left = lane < 64

        def qblock(i, _):
            r0 = pl.multiple_of(i * QB, QB)
            qs = [q_ref[pl.ds(r0, QB), hh * LANES:(hh + 1) * LANES] for hh in range(2)]
            rowc = lax.shift_right_logical(r0 + _iota((QB, QB), 0), 6)

            def kv(kb, carry):
                ms, ls, acc = carry
                c0 = pl.multiple_of(kb * QB, QB)
                v = v_ref[pl.ds(c0, QB), :]
                ok = lax.shift_right_logical(c0 + _iota((QB, QB), 1), 6) <= rowc
                new_m, new_l, alphas = [], [], []
                pv = None
                for hh in range(2):
                    k = k_ref[pl.ds(c0, QB), hh * LANES:(hh + 1) * LANES]
                    s = jnp.where(ok, _dot(qs[hh], k, "nt") * (SCALE_A * LOG2_E), NEG)
                    mn = jnp.maximum(ms[hh], jnp.max(s, axis=-1, keepdims=True))
                    p = jnp.exp2(s - mn)
                    a = jnp.exp2(ms[hh] - mn)
                    new_m.append(mn)
                    new_l.append(a * ls[hh] + jnp.sum(p, axis=-1, keepdims=True))
                    alphas.append(a)
                    part = _dot(p.astype(BF16), v * (m0b if hh == 0 else m1b))
                    pv = part if pv is None else pv + part
                acc = acc * jnp.where(left, alphas[0], alphas[1]) + pv
                return tuple(new_m), tuple(new_l), acc

            init = ((jnp.full((QB, 1), NEG, F32),) * 2, (jnp.zeros((QB, 1), F32),) * 2, jnp.zeros((QB, LANES), F32))
            ms, ls, acc = lax.fori_loop(0, i + 1, kv, init)
            o_ref[pl.ds(r0, QB), :] = acc * jnp.where(left, 1.0 / ls[0], 1.0 / ls[1])
            lse_ref[pl.ds(r0, QB), :] = jnp.where(left, ms[0] + jnp.log(ls[0]) * LOG2_E, ms[1] + jnp.log(ls[1]) * LOG2_E)
            return 0

        lax.fori_loop(0, S // QB, qblock, 0)

    pair = lambda w: pl.BlockSpec((S, w), lambda p: (0, p))
    return _carrier_call(
        body, name=name, grid=(MLA_H // 2,), in_specs=[pair(2 * LANES), pair(2 * LANES), pair(LANES)],
        out_specs=[pair(LANES), pair(LANES)],
        out_shape=[jax.ShapeDtypeStruct((S, MLA_H * MLA_V), F32), jax.ShapeDtypeStruct((S, MLA_H * MLA_V), F32)],
        args=(qa, ka, va), sem=("parallel",), carry=carry)


def _mla_bwd(name, qa, ka, va, o, lse, do, do_block0, carry=None):
    def body(q_ref, k_ref, v_ref, o_ref, lse_ref, do_ref, dq_ref, dk_ref, dv_ref):
        m0f, m1f = _head_masks(F32)
        m0b, m1b = _head_masks(BF16)
        dk_ref[...] = jnp.zeros_like(dk_ref)
        dv_ref[...] = jnp.zeros_like(dv_ref)

        def qblock(i, _):
            r0 = pl.multiple_of(i * QB, QB)
            rows = pl.ds(r0, QB)
            do_f = do_ref[rows, :]
            prod = do_f * o_ref[rows, :]
            deltas = [jnp.sum(prod * m0f, axis=-1, keepdims=True), jnp.sum(prod * m1f, axis=-1, keepdims=True)]
            lse_v = lse_ref[rows, :]
            lses = [lse_v[:, 0:1], lse_v[:, 64:65]]
            dob = do_f.astype(BF16)
            dos = [dob * m0b, dob * m1b]
            qs = [q_ref[rows, hh * LANES:(hh + 1) * LANES] for hh in range(2)]
            rowc = lax.shift_right_logical(r0 + _iota((QB, QB), 0), 6)

            def kv(kb, dqs):
                c0 = pl.multiple_of(kb * QB, QB)
                cols = pl.ds(c0, QB)
                v = v_ref[cols, :]
                ok = lax.shift_right_logical(c0 + _iota((QB, QB), 1), 6) <= rowc
                out = []
                dv = None
                for hh in range(2):
                    k = k_ref[cols, hh * LANES:(hh + 1) * LANES]
                    s = _dot(qs[hh], k, "nt") * (SCALE_A * LOG2_E)
                    p = jnp.where(ok, jnp.exp2(s - lses[hh]), 0.0)
                    dp = _dot(dos[hh], v, "nt")
                    ds = (p * (dp - deltas[hh]) * SCALE_A).astype(BF16)
                    out.append(dqs[hh] + _dot(ds, k))
                    dk_ref[cols, hh * LANES:(hh + 1) * LANES] += _dot(ds, qs[hh], "tn")
                    part = _dot(p.astype(BF16), dos[hh], "tn")
                    dv = part if dv is None else dv + part
                dv_ref[cols, :] += dv
                return tuple(out)

            dqs = lax.fori_loop(0, i + 1, kv, (jnp.zeros((QB, LANES), F32),) * 2)
            for hh in range(2):
                dq_ref[rows, hh * LANES:(hh + 1) * LANES] = dqs[hh]
            return 0

        lax.fori_loop(0, S // QB, qblock, 0)

    pair = lambda w: pl.BlockSpec((S, w), lambda p: (0, p))
    return _carrier_call(
        body, name=name, grid=(MLA_H // 2,),
        in_specs=[pair(2 * LANES), pair(2 * LANES), pair(LANES), pair(LANES), pair(LANES),
                  pl.BlockSpec((S, LANES), lambda p: (0, do_block0 + p))],
        out_specs=[pair(2 * LANES), pair(2 * LANES), pair(LANES)],
        out_shape=[jax.ShapeDtypeStruct((S, MLA_H * LANES), F32), jax.ShapeDtypeStruct((S, MLA_H * LANES), F32),
                   jax.ShapeDtypeStruct((S, MLA_H * MLA_V), F32)],
        args=(qa, ka, va, o, lse, do), sem=("parallel",), carry=carry)


def _sb_weights(q_h, k, c, before, tri_suffix):
    z = _dot(q_h, k, "nt") * (SCALE_B * LOG2_E)
    sp = jnp.maximum(z, 0.0) + jnp.log(1.0 + jnp.exp2(-jnp.abs(z))) * LOG2_E
    log_keep = jnp.where(before, -sp, 0.0)
    to_the_right, total = _running_sum(log_keep, tri_suffix, True)
    w = jnp.where(before, jnp.exp2(z - sp + to_the_right + c), 0.0)
    return w, jnp.exp2(z - sp), total


def _sb_fwd(name, proj, carry=None):
    def body(q_ref, k_ref, v_ref, o_ref):
        m0b, m1b = _head_masks(BF16)
        tri_suffix = (_iota((MXU_W, MXU_W), 0) > _iota((MXU_W, MXU_W), 1)).astype(BF16)

        def qblock(i, _):
            r0 = pl.multiple_of(i * QB, QB)
            q = q_ref[pl.ds(r0, QB), :].astype(BF16)
            qs = [q * m0b, q * m1b]
            rowg = r0 + _iota((QB, QB), 0)

            def kv(step, carry):
                cs, acc = carry
                c0 = pl.multiple_of((i - step) * QB, QB)
                k = k_ref[pl.ds(c0, QB), :].astype(BF16)
                v = v_ref[pl.ds(c0, QB), :].astype(BF16)
                before = (c0 + _iota((QB, QB), 1)) < rowg
                new_c = []
                for hh in range(2):
                    w, _, tot = _sb_weights(qs[hh], k, cs[hh], before, tri_suffix)
                    new_c.append(cs[hh] + tot)
                    acc = acc + _dot(w.astype(BF16), v * (m0b if hh == 0 else m1b))
                return tuple(new_c), acc

            init = ((jnp.zeros((QB, 1), F32),) * 2, jnp.zeros((QB, LANES), F32))
            _, acc = lax.fori_loop(0, i + 1, kv, init)
            o_ref[pl.ds(r0, QB), :] = acc.astype(BF16)
            return 0

        lax.fori_loop(0, S // QB, qblock, 0)

    col = lambda base: pl.BlockSpec((S, LANES), lambda p: (0, base // LANES + p))
    return _carrier_call(
        body, name=name, grid=(SB_H // 2,), in_specs=[col(P_QB), col(P_KB), col(P_VB)],
        out_specs=[pl.BlockSpec((S, LANES), lambda p: (0, p))],
        out_shape=[jax.ShapeDtypeStruct((S, SB_H * SB_DIM), BF16)],
        args=(proj, proj, proj), sem=("parallel",), carry=carry)


def _sb_bwd(name, proj, do, do_block0, carry=None):
    nb = S // QB

    def body(q_ref, k_ref, v_ref, do_ref, dq_ref, dk_ref, dv_ref, sig_scr, dl_scr, dk_acc, dv_acc):
        m0b, m1b = _head_masks(BF16)
        tri_suffix = (_iota((MXU_W, MXU_W), 0) > _iota((MXU_W, MXU_W), 1)).astype(BF16)
        tri_prefix = (_iota((MXU_W, MXU_W), 0) < _iota((MXU_W, MXU_W), 1)).astype(BF16)
        dk_acc[...] = jnp.zeros_like(dk_acc)
        dv_acc[...] = jnp.zeros_like(dv_acc)

        def qblock(i, _):
            r0 = pl.multiple_of(i * QB, QB)
            rows = pl.ds(r0, QB)
            q = q_ref[rows, :].astype(BF16)
            qs = [q * m0b, q * m1b]
            dob = do_ref[rows, :].astype(BF16)
            dos = [dob * m0b, dob * m1b]
            rowg = r0 + _iota((QB, QB), 0)

            def sweep_left(step, cs):
                kb = i - step
                c0 = pl.multiple_of(kb * QB, QB)
                cols = pl.ds(c0, QB)
                k = k_ref[cols, :].astype(BF16)
                v = v_ref[cols, :].astype(BF16)
                before = (c0 + _iota((QB, QB), 1)) < rowg
                new_c = []
                dv = None
                for hh in range(2):
                    w, sig, tot = _sb_weights(qs[hh], k, cs[hh], before, tri_suffix)
                    new_c.append(cs[hh] + tot)
                    sig_scr[hh, kb] = sig
                    dl_scr[hh, kb] = _dot(dos[hh], v, "nt") * w
                    part = _dot(w.astype(BF16), dos[hh], "tn")
                    dv = part if dv is None else dv + part
                dv_acc[cols, :] += dv
                return tuple(new_c)

            lax.fori_loop(0, i + 1, sweep_left, (jnp.zeros((QB, 1), F32),) * 2)

            def sweep_right(kb, carry):
                ps, dq = carry
                c0 = pl.multiple_of(kb * QB, QB)
                cols = pl.ds(c0, QB)
                k = k_ref[cols, :].astype(BF16)
                before = (c0 + _iota((QB, QB), 1)) < rowg
                new_p = []
                dk = None
                for hh in range(2):
                    dl = dl_scr[hh, kb]
                    sig = sig_scr[hh, kb]
                    to_the_left, total = _running_sum(dl, tri_prefix, False)
                    earlier = to_the_left + ps[hh]
                    new_p.append(ps[hh] + total)
                    dz = (jnp.where(before, dl * (1.0 - sig) - earlier * sig, 0.0) * SCALE_B).astype(BF16)
                    dq = dq + _dot(dz, k * (m0b if hh == 0 else m1b))
                    part = _dot(dz, qs[hh], "tn")
                    dk = part if dk is None else dk + part
                dk_acc[cols, :] += dk
                return tuple(new_p), dq

            init = ((jnp.zeros((QB, 1), F32),) * 2, jnp.zeros((QB, LANES), F32))
            _, dq = lax.fori_loop(0, i + 1, sweep_right, init)
            dq_ref[rows, :] = dq.astype(BF16)
            return 0

        lax.fori_loop(0, nb, qblock, 0)
        dk_ref[...] = dk_acc[...].astype(BF16)
        dv_ref[...] = dv_acc[...].astype(BF16)

    col = lambda base: pl.BlockSpec((S, LANES), lambda p: (0, base // LANES + p))
    out = pl.BlockSpec((S, LANES), lambda p: (0, p))
    shape = jax.ShapeDtypeStruct((S, SB_H * SB_DIM), BF16)
    return _carrier_call(
        body, name=name, grid=(SB_H // 2,),
        in_specs=[col(P_QB), col(P_KB), col(P_VB), pl.BlockSpec((S, LANES), lambda p: (0, do_block0 + p))],
        out_specs=[out, out, out], out_shape=[shape, shape, shape],
        scratch_shapes=[pltpu.VMEM((2, nb, QB, QB), F32), pltpu.VMEM((2, nb, QB, QB), F32),
                        pltpu.VMEM((S, LANES), F32), pltpu.VMEM((S, LANES), F32)],
        args=(proj, proj, proj, do), sem=("parallel",), carry=carry)


def _band_row_index():
    j = np.arange(TOEP_W)
    rel = np.clip(LEFT_CHUNKS * CHUNK - j, -REL_CLIP, REL_CLIP) + REL_CLIP
    rel[BAND_W:] = 2 * REL_CLIP
    return rel.astype(np.int32)


def _band_tiles(r0_ref, q_ref, kpad, vpad, m, m0b, m1b, static_ok, bias):
    r0 = pl.multiple_of(m * BQ, BQ)
    q = q_ref[0, pl.ds(r0, BQ), :]
    kw = kpad[pl.ds(r0, BAND_W), :]
    vw = vpad[pl.ds(r0, BAND_W), :]
    ok = static_ok & ((r0 - BAND_PAD + _iota((BQ, BAND_W), 1)) >= 0)
    qs = [q * m0b, q * m1b]
    ps = []
    for hh in range(2):
        s = jnp.where(ok, _dot(qs[hh], kw, "nt") * (SCALE_C * LOG2_E) + bias[hh], NEG)
        e = jnp.exp2(s - jnp.max(s, axis=-1, keepdims=True))
        ps.append(e * (1.0 / jnp.sum(e, axis=-1, keepdims=True)))
    return r0, qs, kw, vw, ps


def _band_setup(qkv_ref, r0_ref, kpad, vpad):
    kpad[0:BAND_PAD, :] = jnp.zeros((BAND_PAD, LANES), BF16)
    vpad[0:BAND_PAD, :] = jnp.zeros((BAND_PAD, LANES), BF16)
    kpad[BAND_PAD:, :] = qkv_ref[1]
    vpad[BAND_PAD:, :] = qkv_ref[2]
    jc = lax.shift_right_logical(_iota((BQ, BAND_W), 1), 6)
    rc = lax.shift_right_logical(_iota((BQ, BAND_W), 0), 6)
    static_ok = (jc >= rc) & (jc <= rc + LEFT_CHUNKS)
    bias = []
    for hh in range(2):
        row = jnp.broadcast_to(r0_ref[hh:hh + 1, :] * LOG2_E, (BQ, TOEP_W))
        bias.append(pltpu.roll(row, 0, 1, stride=1, stride_axis=0)[:, :BAND_W])
    return static_ok, bias


def _band_fwd(name, qkv, r0, carry=None):
    def body(qkv_ref, r0_ref, o_ref, kpad, vpad):
        m0b, m1b = _head_masks(BF16)
        static_ok, bias = _band_setup(qkv_ref, r0_ref, kpad, vpad)

        def qblock(m, _):
            r0_, _, _, vw, ps = _band_tiles(r0_ref, qkv_ref, kpad, vpad, m, m0b, m1b, static_ok, bias)
            o = _dot(ps[0].astype(BF16), vw * m0b) + _dot(ps[1].astype(BF16), vw * m1b)
            o_ref[pl.ds(r0_, BQ), :] = o.astype(BF16)
            return 0

        lax.fori_loop(0, S // BQ, qblock, 0)

    return _carrier_call(
        body, name=name, grid=(C_H // 2,),
        in_specs=[pl.BlockSpec((3, S, LANES), lambda p: (0, 0, p)), pl.BlockSpec((None, 2, TOEP_W), lambda p: (p, 0, 0))],
        out_specs=[pl.BlockSpec((S, LANES), lambda p: (0, p))],
        out_shape=[jax.ShapeDtypeStruct((S, C_H * C_DIM), BF16)],
        scratch_shapes=[pltpu.VMEM((S + BAND_PAD, LANES), BF16), pltpu.VMEM((S + BAND_PAD, LANES), BF16)],
        args=(qkv, r0), sem=("parallel",), carry=carry)


def _band_bwd(name, qkv, r0, do, carry=None):
    def body(qkv_ref, r0_ref, do_ref, dqkv_ref, dr0_ref, kpad, vpad, dkpad, dvpad, db_acc):
        m0b, m1b = _head_masks(BF16)
        static_ok, bias = _band_setup(qkv_ref, r0_ref, kpad, vpad)
        dkpad[...] = jnp.zeros_like(dkpad)
        dvpad[...] = jnp.zeros_like(dvpad)
        db_acc[...] = jnp.zeros_like(db_acc)

        def qblock(m, _):
            r0_, qs, kw, vw, ps = _band_tiles(r0_ref, qkv_ref, kpad, vpad, m, m0b, m1b, static_ok, bias)
            dob = do_ref[pl.ds(r0_, BQ), :].astype(BF16)
            dos = [dob * m0b, dob * m1b]
            dq = None
            dk = None
            dv = None
            for hh in range(2):
                p = ps[hh]
                dp = _dot(dos[hh], vw, "nt")
                ds = p * (dp - jnp.sum(dp * p, axis=-1, keepdims=True))
                db_acc[hh, :, 0:BAND_W] += ds
                dsb = (ds * SCALE_C).astype(BF16)
                t = _dot(dsb, kw * (m0b if hh == 0 else m1b))
                dq = t if dq is None else dq + t
                t = _dot(dsb, qs[hh], "tn")
                dk = t if dk is None else dk + t
                t = _dot(p.astype(BF16), dos[hh], "tn")
                dv = t if dv is None else dv + t
            dqkv_ref[0, pl.ds(r0_, BQ), :] = dq.astype(BF16)
            dkpad[pl.ds(r0_, BAND_W), :] += dk
            dvpad[pl.ds(r0_, BAND_W), :] += dv
            return 0

        lax.fori_loop(0, S // BQ, qblock, 0)
        dqkv_ref[1] = dkpad[BAND_PAD:, :].astype(BF16)
        dqkv_ref[2] = dvpad[BAND_PAD:, :].astype(BF16)
        sub = _iota((8, TOEP_W), 0)
        for hh in range(2):
            folded = db_acc[hh, 0:8, :]
            for a in range(1, BQ // 8):
                folded = folded + pltpu.roll(db_acc[hh, 8 * a:8 * a + 8, :], TOEP_W - 8 * a, 1)
            for bit in range(3):
                moved = pltpu.roll(folded, TOEP_W - (1 << bit), 1)
                folded = jnp.where((sub & (1 << bit)) != 0, moved, folded)
            dr0_ref[hh:hh + 1, :] = jnp.sum(folded, axis=0, keepdims=True)

    return _carrier_call(
        body, name=name, grid=(C_H // 2,),
        in_specs=[pl.BlockSpec((3, S, LANES), lambda p: (0, 0, p)), pl.BlockSpec((None, 2, TOEP_W), lambda p: (p, 0, 0)),
                  pl.BlockSpec((S, LANES), lambda p: (0, p))],
        out_specs=[pl.BlockSpec((3, S, LANES), lambda p: (0, 0, p)), pl.BlockSpec((None, 2, TOEP_W), lambda p: (p, 0, 0))],
        out_shape=[jax.ShapeDtypeStruct((3, S, C_H * C_DIM), BF16), jax.ShapeDtypeStruct((C_H // 2, 2, TOEP_W), F32)],
        scratch_shapes=[pltpu.VMEM((S + BAND_PAD, LANES), BF16), pltpu.VMEM((S + BAND_PAD, LANES), BF16),
                        pltpu.VMEM((S + BAND_PAD, LANES), F32), pltpu.VMEM((S + BAND_PAD, LANES), F32),
                        pltpu.VMEM((2, BQ, TOEP_W), F32)],
        args=(qkv, r0, do), sem=("parallel",), carry=carry)


def _bias_table_grad(name, dr0):
    w_out = 5 * LANES

    def body(d_ref, o_ref):
        j = _iota((TOEP_W, w_out), 0)
        rel = jnp.clip(LEFT_CHUNKS * CHUNK - j, -REL_CLIP, REL_CLIP) + REL_CLIP
        rel = jnp.where(j >= BAND_W, 2 * REL_CLIP, rel)
        onehot = (rel == _iota((TOEP_W, w_out), 1)).astype(BF16)
        d = d_ref[...]
        hi = d.astype(BF16)
        mid = (d - hi.astype(F32))
        mid_b = mid.astype(BF16)
        lo = (mid - mid_b.astype(F32)).astype(BF16)
        o_ref[...] = _dot(hi, onehot) + _dot(mid_b, onehot) + _dot(lo, onehot)

    return pl.pallas_call(
        body, name=name, out_shape=jax.ShapeDtypeStruct((C_H, w_out), F32),
        in_specs=[pl.BlockSpec((C_H, TOEP_W), lambda: (0, 0))], out_specs=pl.BlockSpec((C_H, w_out), lambda: (0, 0)),
        grid=(),
    )(dr0)


def _carry_gather(cy, slots, names, ici, d2d):
    idx = [cy.operand(slots[n], True) for n in names]
    n = len(names)
    base_i = cy.sems(3 * n) if ici else 0
    base_d = cy.sems(3 * n) if d2d else 0

    def piece(refs, t, slot, cc):
        return refs[idx[t]].at[slot, _half_rows(cc, slots[names[t]].shape[1] // 2), :]

    def over_ici(refs, send, recv, arriving):
        x, y, c, chips = _position()
        out = []
        for t in range(n):
            for j in range(3):
                r = piece(refs, t, 2 * chips[j][0] + chips[j][1] if arriving else 2 * x + y, c)
                out.append(_remote(r, r, send, recv, base_i + 3 * t + j, (*chips[j], c)))
        return out

    def over_d2d(refs, send, recv, arriving):
        x, y, c, chips = _position()
        out = []
        for t in range(n):
            for j in range(3):
                r = piece(refs, t, 2 * chips[j][0] + chips[j][1], 1 - c if arriving else c)
                out.append(_remote(r, r, send, recv, base_d + 3 * t + j, (x, y, 1 - c)))
        return out

    def start_ici(refs, fresh, send, recv):
        for cp in over_ici(refs, send, recv, False):
            cp.start()

    def wait_ici(refs, fresh, send, recv):
        for cp in over_ici(refs, send, recv, True):
            cp.wait_recv()
        for cp in over_ici(refs, send, recv, False):
            cp.wait_send()

    def start_d2d(refs, fresh, send, recv):
        for cp in over_d2d(refs, send, recv, False):
            cp.start()

    def wait_d2d(refs, fresh, send, recv):
        for cp in over_d2d(refs, send, recv, True):
            cp.wait_recv()
        for cp in over_d2d(refs, send, recv, False):
            cp.wait_send()

    if ici and d2d:
        cy.starts.append(start_ici)
        cy.finishes += [wait_ici, start_d2d, wait_d2d]
    elif ici:
        cy.starts.append(start_ici)
        cy.finishes.append(wait_ici)
    else:
        cy.starts.append(start_d2d)
        cy.finishes.append(wait_d2d)

    def done(aliased, fresh):
        for t, name in enumerate(names):
            slots[name] = aliased[idx[t]]

    cy.on_done.append(done)


def _carry_chip_exchange(cy, sums, got, names):
    idx = [cy.operand(sums[n], False) for n in names]
    out = [cy.result((3,) + sums[n].shape[1:], BF16) for n in names]
    base = cy.sems(3 * len(names))

    def copies(refs, fresh, send, recv):
        x, y, c, chips = _position()
        return [_remote(refs[idx[t]].at[2 * chips[j][0] + chips[j][1]], fresh[out[t]].at[j], send, recv, base + 3 * t + j,
                        (*chips[j], c)) for t in range(len(names)) for j in range(3)]

    def start(refs, fresh, send, recv):
        for cp in copies(refs, fresh, send, recv):
            cp.start()

    def wait(refs, fresh, send, recv):
        for cp in copies(refs, fresh, send, recv):
            cp.wait()

    cy.starts.append(start)
    cy.finishes.append(wait)

    def done(aliased, fresh):
        for t, name in enumerate(names):
            got[name] = fresh[out[t]]

    cy.on_done.append(done)


def _run_carry(name, cy):
    _, res = _carrier_call(None, name=name, grid=(), in_specs=[], out_specs=[], out_shape=[], args=(), sem=(), carry=cy)
    cy.done(res)


FIRST_WEIGHTS = ("ev_w_in", "ev_w_uq", "ev_w_ukv")
WEIGHTS_A = ("ev_w_out", "w_gate0", "w_up0")
WEIGHTS_B = ("w_down0", "od_w_qkv", "od_w_out")
WEIGHTS_C = ("w_gate1",)
WEIGHTS_D = ("w_up1", "w_down1")
GRAD_GROUPS = {"ffn1": ("w_gate1", "w_up1", "w_down1"), "od": ("od_w_qkv", "od_w_out"),
               "ffn0": ("w_gate0", "w_up0", "w_down0"), "ev_out": ("ev_w_out",),
               "ev": ("ev_w_in", "ev_w_uq", "ev_w_ukv")}


def _carry_pair_exchange(cy, parts, theirs, names):
    idx = [cy.operand(parts[n], False) for n in names]
    out = [cy.result((N_CHIPS, parts[n].shape[1] // 2, parts[n].shape[2]), BF16) for n in names]
    base = cy.sems(len(names))

    def copies(refs, fresh, send, recv):
        x, y, c, _ = _position()
        return [_remote(refs[idx[t]].at[:, _half_rows(1 - c, parts[n].shape[1] // 2), :], fresh[out[t]], send, recv,
                        base + t, (x, y, 1 - c)) for t, n in enumerate(names)]

    cy.starts.append(lambda refs, fresh, send, recv: [cp.start() for cp in copies(refs, fresh, send, recv)])
    cy.finishes.append(lambda refs, fresh, send, recv: [cp.wait() for cp in copies(refs, fresh, send, recv)])

    def done(aliased, fresh):
        for t, name in enumerate(names):
            theirs[name] = fresh[out[t]]

    cy.on_done.append(done)


def _carry_sibling_exchange(cy, fulls, pieces):
    idx = [cy.operand(fulls[p], True) for p, _ in pieces]
    base = cy.sems(len(pieces))

    def copies(refs, send, recv, arriving):
        x, y, c, _ = _position()
        out = []
        for t, (p, layer) in enumerate(pieces):
            r = refs[idx[t]].at[layer, _half_rows(1 - c if arriving else c, fulls[p].shape[1] // 2), :]
            out.append(_remote(r, r, send, recv, base + t, (x, y, 1 - c)))
        return out

    def start(refs, fresh, send, recv):
        for cp in copies(refs, send, recv, False):
            cp.start()

    def wait(refs, fresh, send, recv):
        for cp in copies(refs, send, recv, True):
            cp.wait_recv()
        for cp in copies(refs, send, recv, False):
            cp.wait_send()

    cy.starts.append(start)
    cy.finishes.append(wait)

    def done(aliased, fresh):
        for t, (p, _) in enumerate(pieces):
            fulls[p] = aliased[idx[t]]

    cy.on_done.append(done)


RIDES = {
    "mla_attn": (("gather_ici", WEIGHTS_A),),
    "sb_attn": (("gather_d2d", WEIGHTS_A), ("gather_ici", WEIGHTS_B)),
    "ev_out": (("gather_d2d", WEIGHTS_B),),
    "ffn0": (("gather_ici", WEIGHTS_C),),
    "qkv": (("gather_d2d", WEIGHTS_C),),
    "band_attn": (("gather_ici", WEIGHTS_D),),
    "od_out": (("gather_d2d", WEIGHTS_D),),
    "od_out_bwd_w": (("pair", "ffn1"),),
    "band_attn_bwd": (("chips", "ffn1"),),
    "rms_mix1_bwd": (("pair", "od"),),
    "ev_out_bwd_w": (("pair", "ffn0"),),
    "mla_attn_bwd": (("chips", "od"), ("sibling", "ffn1"), ("pair", "ev_out")),
    "sb_attn_bwd": (("chips", "ffn0"), ("sibling", "od"), ("chips", "ev_out")),
    "proj_in_bwd_w": (("sibling", "ffn0"), ("sibling", "ev_out")),
    "adamw_w_gate": (("pair", "ev"),),
    "adamw_w_up": (("chips", "ev"),),
    "rms_mix0_bwd": (("sibling", "ev"),),
}


class _Exchanges:
    def __init__(self, slots, pos, shapes):
        self.slots, self.pos, self.shapes = dict(slots), pos, shapes
        self.parts, self.theirs, self.sums, self.got, self.fulls, self.fillers = {}, {}, {}, {}, {}, {}

    def begin(self):
        cy = _Carry()
        _carry_gather(cy, self.slots, FIRST_WEIGHTS, True, True)
        _run_carry("gather_first", cy)

    def weights(self, *names):
        return [self.slots[n] for n in names]

    def _pair_sums(self, group):
        for n in GRAD_GROUPS[group]:
            if n not in self.sums:
                self.sums[n] = _pair_sum("pair_sum_" + n, self.parts[n], self.theirs[n], self.pos)

    def _chip_sums(self, group):
        for n in GRAD_GROUPS[group]:
            param, layer = PART_OF[n]
            self.fulls[param] = _chip_sum("chip_sum_" + n, self.sums[n], self.got[n], self.pos, layer,
                                          self.shapes[param], self.fulls.get(param))

    def carry(self, stage):
        cy = _Carry()
        for step, what in RIDES[stage]:
            if step == "gather_ici":
                _carry_gather(cy, self.slots, what, True, False)
            elif step == "gather_d2d":
                _carry_gather(cy, self.slots, what, False, True)
            elif step == "pair":
                _carry_pair_exchange(cy, self.parts, self.theirs, GRAD_GROUPS[what])
            elif step == "chips":
                self._pair_sums(what)
                _carry_chip_exchange(cy, self.sums, self.got, GRAD_GROUPS[what])
            elif step == "sibling":
                self._chip_sums(what)
                _carry_sibling_exchange(cy, self.fulls, [PART_OF[n] for n in GRAD_GROUPS[what]])
        return cy

    def grads(self, group, parts):
        self.parts.update(parts)

    def during(self, stage):
        self.fillers[stage](self.carry(stage))

    def finish(self):
        return {n: self.fulls[n] for n in BIG}


class _NoExchanges:
    def __init__(self, slots):
        self.slots, self.parts = dict(slots), {}

    def begin(self):
        pass

    def weights(self, *names):
        return [self.slots[n] for n in names]

    def carry(self, stage):
        return None

    def grads(self, group, parts):
        self.parts.update(parts)

    def during(self, stage):
        pass


def _first_weights(w_in_s, w_uq_s, w_ukv_s):
    gw = {"ev_w_in": w_in_s, "ev_w_uq": w_uq_s, "ev_w_ukv": w_ukv_s}
    w_in = jnp.moveaxis(gw["ev_w_in"], 0, 1).reshape(D, EVEN_IN)
    z = lambda n: jnp.zeros((D, n), BF16)
    w_in_p = jnp.concatenate(
        [w_in[:, 0:384], z(128), w_in[:, 384:640], w_in[:, 672:2208], z(KR_LANE), w_in[:, 640:672],
         z(LANES - KR_LANE - MLA_ROPE)], axis=1)
    w_uq = jnp.moveaxis(gw["ev_w_uq"], 0, 1).reshape(Q_LORA, MLA_H, MLA_NOPE + MLA_ROPE)
    w_uq_p = jnp.concatenate([w_uq, jnp.zeros((Q_LORA, MLA_H, LANES - MLA_NOPE - MLA_ROPE), BF16)], axis=2)
    w_ukv = jnp.moveaxis(gw["ev_w_ukv"], 0, 1).reshape(KV_LORA, MLA_H, MLA_NOPE + MLA_V)
    w_uk_p = jnp.concatenate([w_ukv[:, :, :MLA_NOPE], jnp.zeros((KV_LORA, MLA_H, LANES - MLA_NOPE), BF16)], axis=2)
    return dict(
        w_in=w_in_p, w_uq=w_uq_p.reshape(Q_LORA, MLA_H * LANES), w_uk=w_uk_p.reshape(KV_LORA, MLA_H * LANES),
        w_uv=w_ukv[:, :, MLA_NOPE:].reshape(KV_LORA, MLA_H * MLA_V))


def _proj_mm(name, u, w_in):
    return _mm(name, u, w_in, kind="nn", grid=(S // TM, 1, 1),
               a_spec=pl.BlockSpec((TM, D), lambda i, j, k: (i, 0)), b_spec=pl.BlockSpec((D, P_IN), lambda i, j, k: (0, 0)),
               o_spec=pl.BlockSpec((TM, P_IN), lambda i, j, k: (i, 0)), out_shape=(S, P_IN), out_dtype=F32, acc_shape=None)


def _out_proj(name, o, w, resid, carry=None):
    return _mm(name, o, w, kind="nn", grid=(S // TM, 1, 1),
               a_spec=pl.BlockSpec((TM, D), lambda i, j, k: (i, 0)), b_spec=pl.BlockSpec((D, D), lambda i, j, k: (0, 0)),
               o_spec=pl.BlockSpec((TM, D), lambda i, j, k: (i, 0)), out_shape=(S, D), out_dtype=F32, acc_shape=None,
               resid=resid, r_spec=pl.BlockSpec((TM, D), lambda i, j, k: (i, 0)), carry=carry)


def _out_proj_bwd(name, dh, o, w, ex):
    d_o = _mm(name + "_x", dh, w, kind="nt", grid=(S // TM, 1, 1),
              a_spec=pl.BlockSpec((TM, D), lambda i, j, k: (i, 0)), b_spec=pl.BlockSpec((D, D), lambda i, j, k: (0, 0)),
              o_spec=pl.BlockSpec((TM, D), lambda i, j, k: (i, 0)), out_shape=(S, D), out_dtype=F32, acc_shape=None)
    d_w = _mm(name + "_w", o, dh, kind="tn", grid=(2, S // TM),
              a_spec=pl.BlockSpec((TM, TM), lambda j, k: (k, j)), b_spec=pl.BlockSpec((TM, D), lambda j, k: (k, 0)),
              o_spec=pl.BlockSpec((TM, D), lambda j, k: (j, 0)), out_shape=(D, D), out_dtype=BF16, acc_shape=(TM, D),
              carry=ex.carry(name + "_w"))
    return d_o, d_w


def _local_step(x, tgt, sm, ex):
    def riding(stage, fn, *args):
        cy = ex.carry(stage)
        res, copies = fn(stage, *args, carry=cy)
        if cy is not None:
            cy.done(copies)
        return res

    cos_t, sin_t = _rope_tables()
    g_mix, g_ffn = sm["g_mix"], sm["g_ffn"]
    r0 = sm["od_rel_bias"][0][:, _band_row_index()].reshape(C_H // 2, 2, TOEP_W)
    nt = 3

    ex.begin()
    w = _first_weights(*ex.weights(*FIRST_WEIGHTS))
    u0 = _rms_fwd("rms_mix0", x, g_mix[0:1])
    proj = _proj_mm("proj_in", u0, w["w_in"])
    qa, ka, va = _mla_prep_fwd("mla_prep", proj, sm["ev_g_cq"], sm["ev_g_ckv"], w["w_uq"], w["w_uk"], w["w_uv"], cos_t, sin_t)
    o_a, lse = riding("mla_attn", _mla_fwd, qa, ka, va)
    o_b, = riding("sb_attn", _sb_fwd, proj)
    o_ev = jnp.concatenate([o_a.astype(BF16), o_b], axis=1)
    w["ev_w_out"] = ex.weights("ev_w_out")[0].reshape(D, D)
    h1 = _out_proj("ev_out", o_ev, w["ev_w_out"], x, ex.carry("ev_out"))
    w["w_gate0"], w["w_up0"], w["w_down0"] = ex.weights("w_gate0", "w_up0", "w_down0")
    h2, gate0, up0 = riding("ffn0", _ffn_fwd, h1, g_ffn[0:1], w["w_gate0"], w["w_up0"], w["w_down0"])
    w["w_qkv"] = jnp.moveaxis(ex.weights("od_w_qkv")[0], 0, 1).reshape(D, nt * D)
    u2 = _rms_fwd("rms_mix1", h2, g_mix[1:2])
    qkv = _mm("qkv", u2, w["w_qkv"], kind="nn", grid=(S // TM, nt, 1),
              a_spec=pl.BlockSpec((TM, D), lambda i, t, k: (i, 0)), b_spec=pl.BlockSpec((D, D), lambda i, t, k: (0, t)),
              o_spec=pl.BlockSpec((None, TM, D), lambda i, t, k: (t, i, 0)),
              out_shape=(nt, S, D), out_dtype=BF16, acc_shape=None, carry=ex.carry("qkv"))
    o_od, = riding("band_attn", _band_fwd, qkv, r0)
    w["od_w_out"] = ex.weights("od_w_out")[0].reshape(D, D)
    h3 = _out_proj("od_out", o_od, w["od_w_out"], h2, ex.carry("od_out"))
    w["w_gate1"], w["w_up1"], w["w_down1"] = ex.weights("w_gate1", "w_up1", "w_down1")
    (h4, gate1, up1), _ = _ffn_fwd("ffn1", h3, g_ffn[1:2], w["w_gate1"], w["w_up1"], w["w_down1"])

    loss, dh4, dg_final = _loss_bwd("loss", h4, sm["g_final"].reshape(1, D), tgt)

    dh3, dg_ffn1, u3, dgate, dup, act = _ffn_bwd("ffn1_bwd", dh4, h3, g_ffn[1:2], gate1, up1,
                                                 w["w_gate1"], w["w_up1"], w["w_down1"])
    d_wg1, d_wu1, d_wd1 = _ffn_wgrads("ffn1_dw", u3, dgate, dup, act, dh4)
    ex.grads("ffn1", {"w_gate1": d_wg1, "w_up1": d_wu1, "w_down1": d_wd1})

    d_ood, d_w_od_out = _out_proj_bwd("od_out_bwd", dh3, o_od, w["od_w_out"], ex)
    dqkv, dr0 = riding("band_attn_bwd", _band_bwd, qkv, r0, d_ood)
    du2 = _mm("qkv_bwd_x", dqkv, w["w_qkv"], kind="nt", grid=(S // TM, nt),
              a_spec=pl.BlockSpec((None, TM, D), lambda i, t: (t, i, 0)), b_spec=pl.BlockSpec((D, D), lambda i, t: (0, t)),
              o_spec=pl.BlockSpec((TM, D), lambda i, t: (i, 0)), out_shape=(S, D), out_dtype=F32, acc_shape=(TM, D))
    d_w_qkv = _mm("qkv_bwd_w", u2, dqkv, kind="tn", grid=(nt, S // TM),
                  a_spec=pl.BlockSpec((TM, D), lambda t, k: (k, 0)), b_spec=pl.BlockSpec((None, TM, D), lambda t, k: (t, k, 0)),
                  o_spec=pl.BlockSpec((D, D), lambda t, k: (0, t)), out_shape=(D, nt * D), out_dtype=BF16, acc_shape=(D, D))
    shard_cols = lambda a: jnp.moveaxis(a.reshape(a.shape[0], N_CHIPS, a.shape[1] // N_CHIPS), 1, 0)
    ex.grads("od", {"od_w_qkv": shard_cols(d_w_qkv), "od_w_out": d_w_od_out.reshape(N_CHIPS, D // N_CHIPS, D)})
    dh2, dg_mix1 = _rms_bwd("rms_mix1_bwd", du2, h2, g_mix[1:2], dh3, carry=ex.carry("rms_mix1_bwd"))
    d_rel = _bias_table_grad("rel_bias_grad", dr0.reshape(C_H, TOEP_W))[:, :2 * REL_CLIP + 1]

    dh1, dg_ffn0, u1, dgate, dup, act = _ffn_bwd("ffn0_bwd", dh2, h1, g_ffn[0:1], gate0, up0,
                                                 w["w_gate0"], w["w_up0"], w["w_down0"])
    d_wg0, d_wu0, d_wd0 = _ffn_wgrads("ffn0_dw", u1, dgate, dup, act, dh2)
    ex.grads("ffn0", {"w_gate0": d_wg0, "w_up0": d_wu0, "w_down0": d_wd0})

    d_oev, d_w_ev_out = _out_proj_bwd("ev_out_bwd", dh1, o_ev, w["ev_w_out"], ex)
    ex.grads("ev_out", {"ev_w_out": d_w_ev_out.reshape(N_CHIPS, D // N_CHIPS, D)})
    dqa, dka, dva = riding("mla_attn_bwd", _mla_bwd, qa, ka, va, o_a, lse, d_oev, 0)
    dqb, dkb, dvb = riding("sb_attn_bwd", _sb_bwd, proj, d_oev, MLA_H * MLA_V // LANES)
    dcq, dckv, dkr, d_w_uq, d_w_uk, d_w_uv, dg_cq, dg_ckv = _mla_prep_bwd(
        "mla_prep_bwd", dqa, dka, dva, proj, sm["ev_g_cq"], sm["ev_g_ckv"], w["w_uq"], w["w_uk"], w["w_uv"], cos_t, sin_t)
    dproj = jnp.concatenate([dcq, jnp.zeros((S, LANES), BF16), dckv, dqb, dkb, dvb, dkr], axis=1)
    d_w_in_p = _mm("proj_in_bwd_w", u0, dproj, kind="tn", grid=(1, S // TM),
                   a_spec=pl.BlockSpec((TM, D), lambda j, k: (k, 0)), b_spec=pl.BlockSpec((TM, P_IN), lambda j, k: (k, 0)),
                   o_spec=pl.BlockSpec((D, P_IN), lambda j, k: (0, 0)), out_shape=(D, P_IN), out_dtype=BF16,
                   acc_shape=(D, P_IN), carry=ex.carry("proj_in_bwd_w"))
    d_w_in = jnp.concatenate([d_w_in_p[:, 0:384], d_w_in_p[:, 512:768],
                              d_w_in_p[:, P_KR + KR_LANE:P_KR + KR_LANE + MLA_ROPE], d_w_in_p[:, 768:2304]], axis=1)
    d_w_uq_std = d_w_uq.reshape(Q_LORA, MLA_H, LANES)[:, :, :MLA_NOPE + MLA_ROPE].reshape(Q_LORA, -1)
    d_w_ukv = jnp.concatenate([d_w_uk.reshape(KV_LORA, MLA_H, LANES)[:, :, :MLA_NOPE],
                               d_w_uv.reshape(KV_LORA, MLA_H, MLA_V)], axis=2).reshape(KV_LORA, -1)
    ex.grads("ev", {"ev_w_in": shard_cols(d_w_in), "ev_w_uq": shard_cols(d_w_uq_std.astype(BF16)),
                    "ev_w_ukv": shard_cols(d_w_ukv.astype(BF16))})
    ex.during("adamw_w_gate")
    du0 = _mm("proj_in_bwd_x", dproj, w["w_in"], kind="nt", grid=(S // TM, 1, 1),
              a_spec=pl.BlockSpec((TM, P_IN), lambda i, j, k: (i, 0)), b_spec=pl.BlockSpec((D, P_IN), lambda i, j, k: (0, 0)),
              o_spec=pl.BlockSpec((TM, D), lambda i, j, k: (i, 0)), out_shape=(S, D), out_dtype=F32, acc_shape=None)
    ex.during("adamw_w_up")
    grad_x, dg_mix0 = _rms_bwd("rms_mix0_bwd", du0, x, g_mix[0:1], dh1, carry=ex.carry("rms_mix0_bwd"))
    small = {
        "ev_g_cq": dg_cq, "ev_g_ckv": dg_ckv, "od_rel_bias": d_rel.reshape(1, C_H, 2 * REL_CLIP + 1),
        "g_mix": jnp.concatenate([dg_mix0, dg_mix1], axis=0), "g_ffn": jnp.concatenate([dg_ffn0, dg_ffn1], axis=0),
        "g_final": dg_final.reshape(D),
    }
    return loss, grad_x, small


BIG = ("ev_w_in", "ev_w_uq", "ev_w_ukv", "ev_w_out", "od_w_qkv", "od_w_out", "w_gate", "w_up", "w_down")
SMALL = ("ev_g_cq", "ev_g_ckv", "od_rel_bias", "g_mix", "g_ffn", "g_final")
WEIGHTS = ("ev_w_in", "ev_g_cq", "ev_w_uq", "ev_g_ckv", "ev_w_ukv", "ev_w_out", "od_w_qkv", "od_rel_bias", "od_w_out",
           "g_mix", "g_ffn", "w_gate", "w_up", "w_down", "g_final")
GRAD_PARTS = (("ev_w_in", "ev_w_in", 0), ("ev_w_uq", "ev_w_uq", 0), ("ev_w_ukv", "ev_w_ukv", 0),
              ("ev_w_out", "ev_w_out", 0), ("od_w_qkv", "od_w_qkv", 0), ("od_w_out", "od_w_out", 0),
              ("w_gate0", "w_gate", 0), ("w_gate1", "w_gate", 1), ("w_up0", "w_up", 0), ("w_up1", "w_up", 1),
              ("w_down0", "w_down", 0), ("w_down1", "w_down", 1))
PART_OF = {part: (param, layer) for part, param, layer in GRAD_PARTS}
SMALL_ROWS = 112
SMALL_SIZE = 384 + 256 + 16 * 513 + 2 * 1024 + 2 * 1024 + 1024
TRANSPOSED = ("w_gate", "w_up")


def _row_tile(rows, cap=512):
    for t in range(min(rows, cap), 0, -1):
        if rows % t == 0 and t % 16 == 0:
            return t
    return rows


def _cast_into_slot(name, w, layer, pos):
    _, rows, cols = w.shape
    tr = _row_tile(rows)

    def body(pos_ref, w_ref, o_ref):
        o_ref[...] = w_ref[...].astype(BF16)

    return pl.pallas_call(
        body, name=name,
        grid_spec=pltpu.PrefetchScalarGridSpec(
            num_scalar_prefetch=1, grid=(rows // tr,),
            in_specs=[pl.BlockSpec((None, tr, cols), lambda i, p: (layer, i, 0))],
            out_specs=pl.BlockSpec((None, tr, cols), lambda i, p: (p[0], i, 0))),
        out_shape=jax.ShapeDtypeStruct((N_CHIPS, rows, cols), BF16), compiler_params=_params("arbitrary"))(pos, w)


def _pair_sum(name, part, theirs, pos):
    _, half, cols = theirs.shape
    tr = _row_tile(half)
    nb = half // tr

    def body(pos_ref, a_ref, b_ref, o_ref):
        o_ref[...] = (a_ref[...].astype(F32) + b_ref[...].astype(F32)).astype(BF16)

    return pl.pallas_call(
        body, name=name,
        grid_spec=pltpu.PrefetchScalarGridSpec(
            num_scalar_prefetch=1, grid=(N_CHIPS, nb),
            in_specs=[pl.BlockSpec((None, tr, cols), lambda s, i, p: (s, p[1] * nb + i, 0)),
                      pl.BlockSpec((None, tr, cols), lambda s, i, p: (s, i, 0))],
            out_specs=pl.BlockSpec((None, tr, cols), lambda s, i, p: (s, i, 0))),
        out_shape=jax.ShapeDtypeStruct(theirs.shape, BF16),
        compiler_params=_params("arbitrary", "arbitrary"))(pos, part, theirs)


def _chip_sum(name, sums, got, pos, layer, full_shape, full=None):
    _, half, cols = sums.shape
    tr = _row_tile(half)
    nb = half // tr

    def body(pos_ref, s_ref, g_ref, *rest):
        out_ref = rest[-1]
        out_ref[...] = ((s_ref[...].astype(F32) + g_ref[0].astype(F32)) + g_ref[1].astype(F32)) + g_ref[2].astype(F32)

    in_specs = [pl.BlockSpec((None, tr, cols), lambda i, p: (p[0], i, 0)),
                pl.BlockSpec((3, tr, cols), lambda i, p: (0, i, 0))]
    args = [pos, sums, got]
    if full is not None:
        in_specs.append(ANY)
        args.append(full)
    return pl.pallas_call(
        body, name=name,
        grid_spec=pltpu.PrefetchScalarGridSpec(
            num_scalar_prefetch=1, grid=(nb,), in_specs=in_specs,
            out_specs=pl.BlockSpec((None, tr, cols), lambda i, p: (layer, p[1] * nb + i, 0))),
        out_shape=jax.ShapeDtypeStruct(full_shape, F32),
        input_output_aliases={3: 0} if full is not None else {},
        compiler_params=_params("arbitrary"))(*args)


def _all_reduce_small(name, packed):
    n_dev = 8

    def body(p_ref, o_ref, slots, send_sem, recv_sem):
        x, y, c, _ = _position()
        me = 4 * x + 2 * y + c

        def peer(k):
            return (1 - x if k & 4 else x, 1 - y if k & 2 else y, 1 - c if k & 1 else c)

        def logical(k):
            px, py, pc = peer(k)
            return 4 * px + 2 * py + pc

        slots[me] = p_ref[...]
        sends = [pltpu.make_async_remote_copy(
            src_ref=p_ref, dst_ref=slots.at[me], send_sem=send_sem.at[k], recv_sem=recv_sem.at[k],
            device_id=peer(k), device_id_type=MESH) for k in range(1, n_dev)]
        for cp in sends:
            cp.start()
        for k in range(1, n_dev):
            pltpu.make_async_remote_copy(
                src_ref=p_ref, dst_ref=slots.at[logical(k)], send_sem=send_sem.at[k], recv_sem=recv_sem.at[k],
                device_id=peer(k), device_id_type=MESH).wait_recv()
        for cp in sends:
            cp.wait_send()
        total = slots[0]
        for d in range(1, n_dev):
            total = total + slots[d]
        o_ref[...] = total

    vm = pl.BlockSpec(memory_space=pltpu.VMEM)
    return pl.pallas_call(
        body, name=name, in_specs=[vm], out_specs=vm, out_shape=jax.ShapeDtypeStruct(packed.shape, F32),
        scratch_shapes=[pltpu.VMEM((n_dev,) + packed.shape, F32), pltpu.SemaphoreType.DMA((n_dev,)),
                        pltpu.SemaphoreType.DMA((n_dev,))],
    )(packed)


def _adamw(name, w, g, m, v, carry=None):
    rows, cols = w.shape
    tr = _row_tile(rows)

    def body(w_ref, g_ref, m_ref, v_ref, d_ref, mo_ref, vo_ref):
        gv = g_ref[...]
        m_new = ADAM_B1 * m_ref[...] + (1.0 - ADAM_B1) * gv
        v_new = ADAM_B2 * v_ref[...] + (1.0 - ADAM_B2) * (gv * gv)
        m_hat = m_new / (1.0 - ADAM_B1 ** ADAM_STEP)
        v_hat = v_new / (1.0 - ADAM_B2 ** ADAM_STEP)
        d_ref[...] = -ADAM_LR * (m_hat / (jnp.sqrt(v_hat) + ADAM_EPS) + ADAM_WD * w_ref[...])
        mo_ref[...] = m_new
        vo_ref[...] = v_new

    spec = pl.BlockSpec((tr, cols), lambda i: (i, 0))
    shape = jax.ShapeDtypeStruct((rows, cols), F32)
    res, copies = _carrier_call(body, name=name, grid=(rows // tr,), in_specs=[spec] * 4, out_specs=[spec] * 3,
                                out_shape=[shape] * 3, args=(w, g, m, v), sem=("parallel",), carry=carry)
    if carry is not None:
        carry.done(copies)
    return res


def _pack_small(tree, extra=None):
    pieces = [tree[n].reshape(-1).astype(F32) for n in SMALL]
    if extra is not None:
        pieces.append(extra.reshape(1).astype(F32))
    flat = jnp.concatenate(pieces)
    return jnp.pad(flat, (0, SMALL_ROWS * LANES - flat.shape[0])).reshape(SMALL_ROWS, LANES)


def _unpack_small(packed, like):
    flat = packed.reshape(-1)
    out, off = {}, 0
    for n in SMALL:
        size = int(np.prod(like[n].shape))
        out[n] = flat[off:off + size].reshape(like[n].shape)
        off += size
    return out


def kernel(x, ev_w_in, ev_g_cq, ev_w_uq, ev_g_ckv, ev_w_ukv, ev_w_out, od_w_qkv, od_rel_bias, od_w_out, g_mix, g_ffn, w_gate, w_up, w_down, g_final, loss_target, m_ev_w_in, m_ev_g_cq, m_ev_w_uq, m_ev_g_ckv, m_ev_w_ukv, m_ev_w_out, m_od_w_qkv, m_od_rel_bias, m_od_w_out, m_g_mix, m_g_ffn, m_w_gate, m_w_up, m_w_down, m_g_final, v_ev_w_in, v_ev_g_cq, v_ev_w_uq, v_ev_g_ckv, v_ev_w_ukv, v_ev_w_out, v_od_w_qkv, v_od_rel_bias, v_od_w_out, v_g_mix, v_g_ffn, v_w_gate, v_w_up, v_w_down, v_g_final):
    w = dict(ev_w_in=ev_w_in, ev_g_cq=ev_g_cq, ev_w_uq=ev_w_uq, ev_g_ckv=ev_g_ckv, ev_w_ukv=ev_w_ukv, ev_w_out=ev_w_out,
             od_w_qkv=od_w_qkv, od_rel_bias=od_rel_bias, od_w_out=od_w_out, g_mix=g_mix, g_ffn=g_ffn, w_gate=w_gate,
             w_up=w_up, w_down=w_down, g_final=g_final)
    m = dict(ev_w_in=m_ev_w_in, ev_g_cq=m_ev_g_cq, ev_w_uq=m_ev_w_uq, ev_g_ckv=m_ev_g_ckv, ev_w_ukv=m_ev_w_ukv,
             ev_w_out=m_ev_w_out, od_w_qkv=m_od_w_qkv, od_rel_bias=m_od_rel_bias, od_w_out=m_od_w_out, g_mix=m_g_mix,
             g_ffn=m_g_ffn, w_gate=m_w_gate, w_up=m_w_up, w_down=m_w_down, g_final=m_g_final)
    v = dict(ev_w_in=v_ev_w_in, ev_g_cq=v_ev_g_cq, ev_w_uq=v_ev_w_uq, ev_g_ckv=v_ev_g_ckv, ev_w_ukv=v_ev_w_ukv,
             ev_w_out=v_ev_w_out, od_w_qkv=v_od_w_qkv, od_rel_bias=v_od_rel_bias, od_w_out=v_od_w_out, g_mix=v_g_mix,
             g_ffn=v_g_ffn, w_gate=v_w_gate, w_up=v_w_up, w_down=v_w_down, g_final=v_g_final)
    flat2d = lambda a: a.reshape(-1, a.shape[-1])
    for tree in (w, m, v):
        for n in TRANSPOSED:
            tree[n] = jnp.swapaxes(tree[n], 1, 2)

    pos = jnp.stack([2 * lax.axis_index("x") + lax.axis_index("y"), lax.axis_index("c")]).astype(jnp.int32)

    slots = {part: _cast_into_slot("cast_" + part, w[n], layer, pos) for part, n, layer in GRAD_PARTS}
    ex = _Exchanges(slots, pos, {n: w[n].shape for n in BIG})

    delta, new_m, new_v = {}, {}, {}

    def adamw(n, grad, carry=None):
        d_, m_, v_ = _adamw("adamw_" + n, flat2d(w[n]), flat2d(grad), flat2d(m[n]), flat2d(v[n]), carry)
        delta[n], new_m[n], new_v[n] = d_.reshape(w[n].shape), m_.reshape(w[n].shape), v_.reshape(w[n].shape)

    ex.fillers = {"adamw_" + n: functools.partial(lambda n, cy: adamw(n, ex.fulls[n], cy), n) for n in ("w_gate", "w_up")}

    loss_local, grad_x, small = _local_step(x[0], loss_target[0], {n: w[n] for n in SMALL}, ex)

    grads = ex.finish()
    small_sum = _all_reduce_small("small_sum", _pack_small(small, loss_local[0, 0]))
    grads.update(_unpack_small(small_sum, w))

    for n in BIG:
        if n not in delta:
            adamw(n, grads[n])
    d_, m_, v_ = _adamw("adamw_small", _pack_small(w), small_sum, _pack_small(m), _pack_small(v))
    delta.update(_unpack_small(d_, w))
    new_m.update(_unpack_small(m_, w))
    new_v.update(_unpack_small(v_, w))
    for tree in (grads, delta, new_m, new_v):
        for n in TRANSPOSED:
            tree[n] = jnp.swapaxes(tree[n], 1, 2)

    loss = small_sum.reshape(-1)[SMALL_SIZE]
    return (loss, grad_x[None], *[grads[n] for n in WEIGHTS], *[delta[n] for n in WEIGHTS],
            *[new_m[n] for n in WEIGHTS], *[new_v[n] for n in WEIGHTS])
```

```python
import functools

import jax
import jax.numpy as jnp
import numpy as np
from jax import lax
from jax.experimental import pallas as pl
from jax.experimental.pallas import tpu as pltpu

F32 = jnp.float32
BF16 = jnp.bfloat16

S = 2048
D = 1024
CHUNK = 64
MLA_H, MLA_NOPE, MLA_ROPE, MLA_V = 8, 64, 32, 64
Q_LORA, KV_LORA = 384, 256
ROPE_THETA = 10000.0
SB_H, SB_DIM = 8, 64
C_H, C_DIM = 16, 64
LEFT_CHUNKS = 8
REL_CLIP = 256
D_FF = 2816
EVEN_IN = 2208
RMS_EPS = 1e-6
ADAM_LR, ADAM_B1, ADAM_B2, ADAM_EPS, ADAM_WD, ADAM_STEP = 0.001, 0.9, 0.999, 1e-08, 0.01, 10

N_CHIPS = 4
FF_SHARD = D_FF // N_CHIPS
SCALE_A = (MLA_NOPE + MLA_ROPE) ** -0.5
SCALE_B = SB_DIM ** -0.5
SCALE_C = C_DIM ** -0.5
NEG = -1e30
LOG2_E = 1.4426950408889634

LANES = 128
MXU_W = 256
VMEM_LIMIT_BYTES = 56 * 1024 * 1024
TM = 512
QB = 512
BQ = 256

P_CQ, P_CKV, P_QB, P_KB, P_VB, P_KR = 0, 512, 768, 1280, 1792, 2304
P_IN = 2432
KR_LANE = 64
BAND_W = BQ + LEFT_CHUNKS * CHUNK
BAND_PAD = 512
TOEP_W = 1024


def _params(*sem):
    return pltpu.CompilerParams(dimension_semantics=sem, vmem_limit_bytes=VMEM_LIMIT_BYTES)


MESH = pl.DeviceIdType.MESH
ANY = pl.BlockSpec(memory_space=pl.ANY)


def _position():
    x, y, c = lax.axis_index("x"), lax.axis_index("y"), lax.axis_index("c")
    other_chips = [(1 - x, y), (x, 1 - y), (1 - x, 1 - y)]
    return x, y, c, other_chips


def _half_rows(c, half):
    return pl.ds(pl.multiple_of(c * half, 16), half)


def _remote(ref_src, ref_dst, send, recv, k, device):
    return pltpu.make_async_remote_copy(src_ref=ref_src, dst_ref=ref_dst, send_sem=send.at[k], recv_sem=recv.at[k],
                                        device_id=device, device_id_type=MESH)


class _Carry:
    def __init__(self):
        self.operands, self.aliased, self.fresh = [], [], []
        self.n_sems = 0
        self.starts, self.finishes, self.on_done = [], [], []

    def operand(self, arr, aliased):
        for i, a in enumerate(self.operands):
            if a is arr:
                return i
        self.operands.append(arr)
        self.aliased.append(aliased)
        return len(self.operands) - 1

    def result(self, shape, dtype):
        self.fresh.append(jax.ShapeDtypeStruct(shape, dtype))
        return len(self.fresh) - 1

    def sems(self, k):
        base = self.n_sems
        self.n_sems += k
        return base

    def done(self, results):
        aliased, fresh = results
        for f in self.on_done:
            f(aliased, fresh)


def _carrier_call(body, *, name, grid, in_specs, out_specs, out_shape, args, sem, scratch_shapes=(), carry=None,
                  prefetch=()):
    in_specs, out_specs, out_shape, scratch = list(in_specs), list(out_specs), list(out_shape), list(scratch_shapes)
    n_pre = len(prefetch)

    def call(kernel, in_specs, out_specs, out_shape, scratch, aliases, sem):
        return pl.pallas_call(
            kernel, name=name, out_shape=out_shape, input_output_aliases=aliases, compiler_params=_params(*sem),
            grid_spec=pltpu.PrefetchScalarGridSpec(num_scalar_prefetch=n_pre, grid=grid, in_specs=in_specs,
                                                   out_specs=out_specs, scratch_shapes=scratch))

    if carry is None:
        return list(call(body, in_specs, out_specs, out_shape, scratch, {}, sem)(*prefetch, *args)), None
    ops = carry.operands
    alias_idx = [i for i, a in enumerate(carry.aliased) if a]
    c_shapes = [jax.ShapeDtypeStruct(ops[i].shape, ops[i].dtype) for i in alias_idx] + carry.fresh
    n_in, n_out, n_scr = len(args), len(out_shape), len(scratch)

    def wrapped(*refs):
        pre, refs = refs[:n_pre], refs[n_pre:]
        ins, c_ins = refs[:n_in], refs[n_in:n_in + len(ops)]
        o0 = n_in + len(ops)
        outs, c_outs = refs[o0:o0 + n_out], refs[o0 + n_out:o0 + n_out + len(c_shapes)]
        s0 = o0 + n_out + len(c_shapes)
        scr, send, recv = refs[s0:s0 + n_scr], refs[s0 + n_scr], refs[s0 + n_scr + 1]
        use = list(c_ins)
        for k, i in enumerate(alias_idx):
            use[i] = c_outs[k]
        fresh = c_outs[len(alias_idx):]

        def run(steps):
            for step in steps:
                step(use, fresh, send, recv)

        if not grid:
            run(carry.starts)
            if body is not None:
                body(*pre, *ins, *outs, *scr)
            run(carry.finishes)
            return
        ids = [pl.program_id(a) for a in range(len(grid))]
        first = functools.reduce(jnp.logical_and, [i == 0 for i in ids])
        last = functools.reduce(jnp.logical_and, [i == g - 1 for i, g in zip(ids, grid)])

        @pl.when(first)
        def _():
            run(carry.starts)

        body(*pre, *ins, *outs, *scr)

        @pl.when(last)
        def _():
            run(carry.finishes)

    res = call(wrapped, in_specs + [ANY] * len(ops), out_specs + [ANY] * len(c_shapes), out_shape + c_shapes,
               scratch + [pltpu.SemaphoreType.DMA((carry.n_sems,)), pltpu.SemaphoreType.DMA((carry.n_sems,))],
               {n_pre + n_in + i: n_out + k for k, i in enumerate(alias_idx)},
               ("arbitrary",) * len(grid))(*prefetch, *args, *ops)
    res = list(res)
    c_res = res[n_out:]
    return res[:n_out], ({i: c_res[k] for k, i in enumerate(alias_idx)}, c_res[len(alias_idx):])


_DIMS = {"nn": (((1,), (0,)), ((), ())), "nt": (((1,), (1,)), ((), ())), "tn": (((0,), (0,)), ((), ()))}


def _dot(a, b, kind="nn"):
    return lax.dot_general(a, b, _DIMS[kind], preferred_element_type=F32)


def _iota(shape, dim):
    return lax.broadcasted_iota(jnp.int32, shape, dim)


def _sigmoid(x):
    return 1.0 / (1.0 + jnp.exp(-x))


def _split_dot(x, tri):
    hi = x.astype(BF16)
    lo = (x - hi.astype(F32)).astype(BF16)
    both = _dot(jnp.concatenate([hi, lo], axis=0), tri)
    return both[:x.shape[0]] + both[x.shape[0]:]


def _running_sum(x, tri, reverse):
    n = x.shape[1] // MXU_W
    blocks = [x[:, b * MXU_W:(b + 1) * MXU_W] for b in range(n)]
    out = [None] * n
    carry = None
    for b in (range(n - 1, -1, -1) if reverse else range(n)):
        part = _split_dot(blocks[b], tri)
        out[b] = part if carry is None else part + carry
        total = jnp.sum(blocks[b], axis=-1, keepdims=True)
        carry = total if carry is None else carry + total
    return (jnp.concatenate(out, axis=1) if n > 1 else out[0]), carry


def _mm(name, a, b, *, kind, grid, a_spec, b_spec, o_spec, out_shape, out_dtype, acc_shape, resid=None, r_spec=None,
        carry=None):
    nk = grid[-1]
    has_r = resid is not None

    def body(*refs):
        a_ref, b_ref = refs[0], refs[1]
        r_ref = refs[2] if has_r else None
        o_ref = refs[2 + has_r]
        part = _dot(a_ref[...].astype(BF16), b_ref[...].astype(BF16), kind)

        def finish(total):
            if has_r:
                total = total + r_ref[...].astype(F32)
            o_ref[...] = total.astype(out_dtype)

        if nk == 1:
            finish(part)
        else:
            acc_ref = refs[3 + has_r]
            k = pl.program_id(len(grid) - 1)

            @pl.when(k == 0)
            def _():
                acc_ref[...] = part

            @pl.when(k > 0)
            def _():
                acc_ref[...] += part

            @pl.when(k == nk - 1)
            def _():
                finish(acc_ref[...])

    in_specs = [a_spec, b_spec] + ([r_spec] if has_r else [])
    args = (a, b) + ((resid,) if has_r else ())
    sem = ("parallel",) * (len(grid) - 1) + ("arbitrary",)
    res, copies = _carrier_call(
        body, name=name, grid=grid, in_specs=in_specs, out_specs=[o_spec],
        out_shape=[jax.ShapeDtypeStruct(out_shape, out_dtype)],
        scratch_shapes=[pltpu.VMEM(acc_shape, F32)] if nk > 1 else [], args=args, sem=sem, carry=carry)
    if carry is not None:
        carry.done(copies)
    return res[0]


def _rms_fwd(name, x, g, col_block=0):
    c = g.shape[1]

    def body(x_ref, g_ref, u_ref):
        xv = x_ref[...]
        r = lax.rsqrt(jnp.mean(xv * xv, axis=-1, keepdims=True) + RMS_EPS)
        u_ref[...] = (xv * r * g_ref[...]).astype(BF16)

    return pl.pallas_call(
        body, name=name, grid=(S // TM,),
        in_specs=[pl.BlockSpec((TM, c), lambda i: (i, col_block)), pl.BlockSpec((1, c), lambda i: (0, 0))],
        out_specs=pl.BlockSpec((TM, c), lambda i: (i, 0)),
        out_shape=jax.ShapeDtypeStruct((S, c), BF16),
        compiler_params=_params("parallel"),
    )(x, g)


def _rms_bwd(name, dy, x, g, resid, carry=None):
    def body(dy_ref, x_ref, g_ref, r_ref, dx_ref, dg_ref):
        i = pl.program_id(0)
        xv = x_ref[...]
        r = lax.rsqrt(jnp.mean(xv * xv, axis=-1, keepdims=True) + RMS_EPS)
        xh = xv * r
        dyv = dy_ref[...]
        dxh = dyv * g_ref[...]
        dx_ref[...] = r_ref[...] + r * (dxh - xh * jnp.mean(dxh * xh, axis=-1, keepdims=True))
        part = jnp.sum(dyv * xh, axis=0, keepdims=True)

        @pl.when(i == 0)
        def _():
            dg_ref[...] = part

        @pl.when(i > 0)
        def _():
            dg_ref[...] += part

    row = pl.BlockSpec((TM, D), lambda i: (i, 0))
    vec = pl.BlockSpec((1, D), lambda i: (0, 0))
    res, copies = _carrier_call(
        body, name=name, grid=(S // TM,), in_specs=[row, row, vec, row], out_specs=[row, vec],
        out_shape=[jax.ShapeDtypeStruct((S, D), F32), jax.ShapeDtypeStruct((1, D), F32)],
        args=(dy, x, g, resid), sem=("arbitrary",), carry=carry)
    if carry is not None:
        carry.done(copies)
    return res


def _loss_bwd(name, h, g, tgt):
    def body(h_ref, g_ref, t_ref, loss_ref, dh_ref, dg_ref):
        i = pl.program_id(0)
        xv = h_ref[...]
        gv = g_ref[...]
        r = lax.rsqrt(jnp.mean(xv * xv, axis=-1, keepdims=True) + RMS_EPS)
        xh = xv * r
        diff = xh * gv - t_ref[...]
        part_loss = 0.5 * jnp.sum(jnp.sum(diff * diff, axis=-1, keepdims=True) * (1.0 / D), axis=0, keepdims=True)
        dy = diff * (1.0 / D)
        dxh = dy * gv
        dh_ref[...] = r * (dxh - xh * jnp.mean(dxh * xh, axis=-1, keepdims=True))
        part_g = jnp.sum(dy * xh, axis=0, keepdims=True)

        @pl.when(i == 0)
        def _():
            dg_ref[...] = part_g
            loss_ref[...] = jnp.broadcast_to(part_loss, (1, LANES))

        @pl.when(i > 0)
        def _():
            dg_ref[...] += part_g
            loss_ref[...] += jnp.broadcast_to(part_loss, (1, LANES))

    row = pl.BlockSpec((TM, D), lambda i: (i, 0))
    vec = pl.BlockSpec((1, D), lambda i: (0, 0))
    return pl.pallas_call(
        body, name=name, grid=(S // TM,), in_specs=[row, vec, row],
        out_specs=[pl.BlockSpec((1, LANES), lambda i: (0, 0)), row, vec],
        out_shape=[jax.ShapeDtypeStruct((1, LANES), F32), jax.ShapeDtypeStruct((S, D), F32),
                   jax.ShapeDtypeStruct((1, D), F32)],
        compiler_params=_params("arbitrary"),
    )(h, g, tgt)


def _ffn_fwd(name, h, g, wg, wu, wd, carry=None):
    def body(h_ref, g_ref, wg_ref, wu_ref, wd_ref, o_ref, gate_ref, up_ref, u_scr):
        s = pl.program_id(1)

        @pl.when(s == 0)
        def _():
            xv = h_ref[...]
            r = lax.rsqrt(jnp.mean(xv * xv, axis=-1, keepdims=True) + RMS_EPS)
            u_scr[...] = (xv * r * g_ref[...]).astype(BF16)
            o_ref[...] = xv

        u = u_scr[...]
        gate = _dot(u, wg_ref[...], "nt")
        up = _dot(u, wu_ref[...], "nt")
        act = gate * _sigmoid(gate) * up
        o_ref[...] += _dot(act.astype(BF16), wd_ref[...])
        gate_ref[...] = gate.astype(BF16)
        up_ref[...] = up.astype(BF16)

    row = pl.BlockSpec((TM, D), lambda i, s: (i, 0))
    hid = pl.BlockSpec((None, TM, FF_SHARD), lambda i, s: (s, i, 0))
    return _carrier_call(
        body, name=name, grid=(S // TM, N_CHIPS),
        in_specs=[row, pl.BlockSpec((1, D), lambda i, s: (0, 0))]
        + [pl.BlockSpec((None, FF_SHARD, D), lambda i, s: (s, 0, 0))] * 3,
        out_specs=[row, hid, hid],
        out_shape=[jax.ShapeDtypeStruct((S, D), F32), jax.ShapeDtypeStruct((N_CHIPS, S, FF_SHARD), BF16),
                   jax.ShapeDtypeStruct((N_CHIPS, S, FF_SHARD), BF16)],
        scratch_shapes=[pltpu.VMEM((TM, D), BF16)], args=(h, g, wg, wu, wd), sem=("parallel", "arbitrary"), carry=carry)


def _ffn_bwd(name, dh, h, g, gate, up, wg, wu, wd):
    def body(dh_ref, h_ref, g_ref, gate_ref, up_ref, wg_ref, wu_ref, wd_ref,
             dhin_ref, dg_ref, u_ref, dgate_ref, dup_ref, act_ref, dhb_scr, du_scr):
        i = pl.program_id(0)
        s = pl.program_id(1)

        @pl.when(s == 0)
        def _():
            xv = h_ref[...]
            r = lax.rsqrt(jnp.mean(xv * xv, axis=-1, keepdims=True) + RMS_EPS)
            u_ref[...] = (xv * r * g_ref[...]).astype(BF16)
            dhb_scr[...] = dh_ref[...].astype(BF16)
            du_scr[...] = jnp.zeros_like(du_scr)

        dact = _dot(dhb_scr[...], wd_ref[...], "nt")
        gv = gate_ref[...].astype(F32)
        uv = up_ref[...].astype(F32)
        sig = _sigmoid(gv)
        sil = gv * sig
        dup = dact * sil
        dgate = dact * uv * (sig * (1.0 + gv * (1.0 - sig)))
        dgb = dgate.astype(BF16)
        dub = dup.astype(BF16)
        act_ref[...] = (sil * uv).astype(BF16)
        dgate_ref[...] = dgb
        dup_ref[...] = dub
        du_scr[...] += _dot(dgb, wg_ref[...]) + _dot(dub, wu_ref[...])

        @pl.when(s == N_CHIPS - 1)
        def _():
            xv = h_ref[...]
            r = lax.rsqrt(jnp.mean(xv * xv, axis=-1, keepdims=True) + RMS_EPS)
            xh = xv * r
            du = du_scr[...]
            dxh = du * g_ref[...]
            dhin_ref[...] = dh_ref[...] + r * (dxh - xh * jnp.mean(dxh * xh, axis=-1, keepdims=True))
            part = jnp.sum(du * xh, axis=0, keepdims=True)

            @pl.when(i == 0)
            def _():
                dg_ref[...] = part

            @pl.when(i > 0)
            def _():
                dg_ref[...] += part

    row = pl.BlockSpec((TM, D), lambda i, s: (i, 0))
    vec = pl.BlockSpec((1, D), lambda i, s: (0, 0))
    hid = pl.BlockSpec((None, TM, FF_SHARD), lambda i, s: (s, i, 0))
    hid_shape = jax.ShapeDtypeStruct((N_CHIPS, S, FF_SHARD), BF16)
    return pl.pallas_call(
        body, name=name, grid=(S // TM, N_CHIPS),
        in_specs=[row, row, vec, hid, hid] + [pl.BlockSpec((None, FF_SHARD, D), lambda i, s: (s, 0, 0))] * 3,
        out_specs=[row, vec, row, hid, hid, hid],
        out_shape=[jax.ShapeDtypeStruct((S, D), F32), jax.ShapeDtypeStruct((1, D), F32),
                   jax.ShapeDtypeStruct((S, D), BF16), hid_shape, hid_shape, hid_shape],
        scratch_shapes=[pltpu.VMEM((TM, D), BF16), pltpu.VMEM((TM, D), F32)],
        compiler_params=_params("arbitrary", "arbitrary"),
    )(dh, h, g, gate, up, wg, wu, wd)


def _ffn_wgrads(name, u, dgate, dup, act, dh):
    nk = S // TM

    def body(u_ref, dh_ref, dgate_ref, dup_ref, act_ref, dg_ref, du_ref, dd_ref, acc_g, acc_u, acc_d):
        k = pl.program_id(1)
        u = u_ref[...]
        parts = (_dot(dgate_ref[...], u, "tn"), _dot(dup_ref[...], u, "tn"),
                 _dot(act_ref[...], dh_ref[...].astype(BF16), "tn"))
        accs = (acc_g, acc_u, acc_d)

        @pl.when(k == 0)
        def _():
            for acc, part in zip(accs, parts):
                acc[...] = part

        @pl.when(k > 0)
        def _():
            for acc, part in zip(accs, parts):
                acc[...] += part

        @pl.when(k == nk - 1)
        def _():
            for out, acc in zip((dg_ref, du_ref, dd_ref), accs):
                out[...] = acc[...].astype(BF16)

    tok = pl.BlockSpec((TM, D), lambda s, k: (k, 0))
    hid = pl.BlockSpec((None, TM, FF_SHARD), lambda s, k: (s, k, 0))
    out = pl.BlockSpec((None, FF_SHARD, D), lambda s, k: (s, 0, 0))
    shape = jax.ShapeDtypeStruct((N_CHIPS, FF_SHARD, D), BF16)
    return pl.pallas_call(
        body, name=name, grid=(N_CHIPS, nk), in_specs=[tok, tok, hid, hid, hid], out_specs=[out, out, out],
        out_shape=[shape, shape, shape], scratch_shapes=[pltpu.VMEM((FF_SHARD, D), F32)] * 3,
        compiler_params=_params("parallel", "arbitrary"))(u, dh, dgate, dup, act)


def _rope_tables():
    pos = jnp.arange(S, dtype=F32)
    inv = ROPE_THETA ** (-jnp.arange(0, MLA_ROPE, 2, dtype=F32) / MLA_ROPE)
    ang = pos[:, None] * inv[None, :]
    half = MLA_ROPE // 2
    cos = jnp.cos(ang)
    sin = jnp.sin(ang)
    one = jnp.ones((S, KR_LANE), F32)
    zero = jnp.zeros((S, KR_LANE), F32)
    tail_one = jnp.ones((S, LANES - KR_LANE - MLA_ROPE), F32)
    tail_zero = jnp.zeros((S, LANES - KR_LANE - MLA_ROPE), F32)
    cos_t = jnp.concatenate([one, cos, cos, tail_one], axis=1)
    sin_t = jnp.concatenate([zero, -sin, sin, tail_zero], axis=1)
    assert cos_t.shape == (S, LANES) and half * 2 == MLA_ROPE
    return cos_t, sin_t


def _rope(x, cos_t, sin_t, sign):
    n = x.shape[1] // LANES
    half = MLA_ROPE // 2
    lane = _iota(x.shape, 1) & (LANES - 1)
    first = (lane >= KR_LANE) & (lane < KR_LANE + half)
    swapped = jnp.where(first, pltpu.roll(x, x.shape[1] - half, 1), pltpu.roll(x, half, 1))
    c = jnp.tile(cos_t, (1, n)) if n > 1 else cos_t
    s = jnp.tile(sin_t, (1, n)) if n > 1 else sin_t
    return x * c + swapped * (s * sign)


def _mla_prep_fwd(name, proj, g_cq, g_ckv, w_uq, w_uk, w_uv, cos_t, sin_t):
    nh = MLA_H * LANES

    def body(cq_ref, ckv_ref, kr_ref, gq_ref, gkv_ref, wq_ref, wk_ref, wv_ref, cos_ref, sin_ref,
             qa_ref, ka_ref, va_ref):
        cos_v, sin_v = cos_ref[...], sin_ref[...]
        cq = cq_ref[...]
        r = lax.rsqrt(jnp.mean(cq * cq, axis=-1, keepdims=True) + RMS_EPS)
        cqn = (cq * r * gq_ref[...]).astype(BF16)
        qa_ref[...] = _rope(_dot(cqn, wq_ref[...]), cos_v, sin_v, 1.0).astype(BF16)
        ckv = ckv_ref[...]
        r = lax.rsqrt(jnp.mean(ckv * ckv, axis=-1, keepdims=True) + RMS_EPS)
        ckvn = (ckv * r * gkv_ref[...]).astype(BF16)
        lane = _iota((TM, LANES), 1)
        rot = (lane >= KR_LANE) & (lane < KR_LANE + MLA_ROPE)
        kr = jnp.where(rot, _rope(kr_ref[...], cos_v, sin_v, 1.0), 0.0)
        ka_ref[...] = (_dot(ckvn, wk_ref[...]) + jnp.tile(kr, (1, MLA_H))).astype(BF16)
        va_ref[...] = _dot(ckvn, wv_ref[...]).astype(BF16)

    full = lambda shape: pl.BlockSpec(shape, lambda i: (0, 0))
    return pl.pallas_call(
        body, name=name, grid=(S // TM,),
        in_specs=[pl.BlockSpec((TM, Q_LORA), lambda i: (i, P_CQ // Q_LORA)),
                  pl.BlockSpec((TM, KV_LORA), lambda i: (i, P_CKV // KV_LORA)),
                  pl.BlockSpec((TM, LANES), lambda i: (i, P_KR // LANES)),
                  full((1, Q_LORA)), full((1, KV_LORA)), full((Q_LORA, nh)), full((KV_LORA, nh)),
                  full((KV_LORA, MLA_H * MLA_V)),
                  pl.BlockSpec((TM, LANES), lambda i: (i, 0)), pl.BlockSpec((TM, LANES), lambda i: (i, 0))],
        out_specs=[pl.BlockSpec((TM, nh), lambda i: (i, 0)), pl.BlockSpec((TM, nh), lambda i: (i, 0)),
                   pl.BlockSpec((TM, MLA_H * MLA_V), lambda i: (i, 0))],
        out_shape=[jax.ShapeDtypeStruct((S, nh), BF16), jax.ShapeDtypeStruct((S, nh), BF16),
                   jax.ShapeDtypeStruct((S, MLA_H * MLA_V), BF16)],
        compiler_params=_params("parallel"),
    )(proj, proj, proj, g_cq, g_ckv, w_uq, w_uk, w_uv, cos_t, sin_t)


def _mla_prep_bwd(name, dqa, dka, dva, proj, g_cq, g_ckv, w_uq, w_uk, w_uv, cos_t, sin_t):
    nh = MLA_H * LANES

    def body(dqa_ref, dka_ref, dva_ref, cq_ref, ckv_ref, gq_ref, gkv_ref, wq_ref, wk_ref, wv_ref, cos_ref, sin_ref,
             dcq_ref, dckv_ref, dkr_ref, dwq_ref, dwk_ref, dwv_ref, dgq_ref, dgkv_ref):
        i = pl.program_id(0)
        cos_v, sin_v = cos_ref[...], sin_ref[...]

        def norm_bwd(x, g, dn):
            r = lax.rsqrt(jnp.mean(x * x, axis=-1, keepdims=True) + RMS_EPS)
            xh = x * r
            dxh = dn * g
            dx = r * (dxh - xh * jnp.mean(dxh * xh, axis=-1, keepdims=True))
            return dx, jnp.sum(dn * xh, axis=0, keepdims=True), (xh * g).astype(BF16)

        dq = _rope(dqa_ref[...], cos_v, sin_v, -1.0).astype(BF16)
        dcqn = _dot(dq, wq_ref[...], "nt")
        dcq, dgq, cqn = norm_bwd(cq_ref[...], gq_ref[...], dcqn)
        dcq_ref[...] = dcq.astype(BF16)
        dwq = _dot(cqn, dq, "tn")

        dka = dka_ref[...]
        dkab = dka.astype(BF16)
        dvab = dva_ref[...].astype(BF16)
        dckvn = _dot(dkab, wk_ref[...], "nt") + _dot(dvab, wv_ref[...], "nt")
        dckv, dgkv, ckvn = norm_bwd(ckv_ref[...], gkv_ref[...], dckvn)
        dckv_ref[...] = dckv.astype(BF16)
        dwk = _dot(ckvn, dkab, "tn")
        dwv = _dot(ckvn, dvab, "tn")

        fold = dka[:, 0:LANES]
        for hh in range(1, MLA_H):
            fold = fold + dka[:, hh * LANES:(hh + 1) * LANES]
        lane = _iota((TM, LANES), 1)
        rot = (lane >= KR_LANE) & (lane < KR_LANE + MLA_ROPE)
        dkr = _rope(jnp.where(rot, fold, 0.0), cos_v, sin_v, -1.0)
        dkr_ref[...] = jnp.where(rot, dkr, 0.0).astype(BF16)

        @pl.when(i == 0)
        def _():
            dwq_ref[...] = dwq
            dwk_ref[...] = dwk
            dwv_ref[...] = dwv
            dgq_ref[...] = dgq
            dgkv_ref[...] = dgkv

        @pl.when(i > 0)
        def _():
            dwq_ref[...] += dwq
            dwk_ref[...] += dwk
            dwv_ref[...] += dwv
            dgq_ref[...] += dgq
            dgkv_ref[...] += dgkv

    full = lambda shape: pl.BlockSpec(shape, lambda i: (0, 0))
    rows = lambda c: pl.BlockSpec((TM, c), lambda i: (i, 0))
    nv = MLA_H * MLA_V
    return pl.pallas_call(
        body, name=name, grid=(S // TM,),
        in_specs=[rows(nh), rows(nh), rows(nv),
                  pl.BlockSpec((TM, Q_LORA), lambda i: (i, P_CQ // Q_LORA)),
                  pl.BlockSpec((TM, KV_LORA), lambda i: (i, P_CKV // KV_LORA)),
                  full((1, Q_LORA)), full((1, KV_LORA)), full((Q_LORA, nh)), full((KV_LORA, nh)), full((KV_LORA, nv)),
                  rows(LANES), rows(LANES)],
        out_specs=[rows(Q_LORA), rows(KV_LORA), rows(LANES), full((Q_LORA, nh)), full((KV_LORA, nh)),
                   full((KV_LORA, nv)), full((1, Q_LORA)), full((1, KV_LORA))],
        out_shape=[jax.ShapeDtypeStruct((S, Q_LORA), BF16), jax.ShapeDtypeStruct((S, KV_LORA), BF16),
                   jax.ShapeDtypeStruct((S, LANES), BF16), jax.ShapeDtypeStruct((Q_LORA, nh), F32),
                   jax.ShapeDtypeStruct((KV_LORA, nh), F32), jax.ShapeDtypeStruct((KV_LORA, nv), F32),
                   jax.ShapeDtypeStruct((1, Q_LORA), F32), jax.ShapeDtypeStruct((1, KV_LORA), F32)],
        compiler_params=_params("arbitrary"),
    )(dqa, dka, dva, proj, proj, g_cq, g_ckv, w_uq, w_uk, w_uv, cos_t, sin_t)


def _head_masks(dtype):
    lane = _iota((1, LANES), 1)
    return (lane < 64).astype(dtype), (lane >= 64).astype(dtype)


def _mla_fwd(name, qa, ka, va, carry=None):
    def body(q_ref, k_ref, v_ref, o_ref, lse_ref):
        m0b, m1b = _head_masks(BF16)
        lane = _iota((QB, LANES), 1)
        left = lane < 64

        def qblock(i, _):
            r0 = pl.multiple_of(i * QB, QB)
            qs = [q_ref[pl.ds(r0, QB), hh * LANES:(hh + 1) * LANES] for hh in range(2)]
            rowc = lax.shift_right_logical(r0 + _iota((QB, QB), 0), 6)

            def kv(kb, carry):
                ms, ls, acc = carry
                c0 = pl.multiple_of(kb * QB, QB)
                v = v_ref[pl.ds(c0, QB), :]
                ok = lax.shift_right_logical(c0 + _iota((QB, QB), 1), 6) <= rowc
                new_m, new_l, alphas = [], [], []
                pv = None
                for hh in range(2):
                    k = k_ref[pl.ds(c0, QB), hh * LANES:(hh + 1) * LANES]
                    s = jnp.where(ok, _dot(qs[hh], k, "nt") * (SCALE_A * LOG2_E), NEG)
                    mn = jnp.maximum(ms[hh], jnp.max(s, axis=-1, keepdims=True))
                    p = jnp.exp2(s - mn)
                    a = jnp.exp2(ms[hh] - mn)
                    new_m.append(mn)
                    new_l.append(a * ls[hh] + jnp.sum(p, axis=-1, keepdims=True))
                    alphas.append(a)
                    part = _dot(p.astype(BF16), v * (m0b if hh == 0 else m1b))
                    pv = part if pv is None else pv + part
                acc = acc * jnp.where(left, alphas[0], alphas[1]) + pv
                return tuple(new_m), tuple(new_l), acc

            init = ((jnp.full((QB, 1), NEG, F32),) * 2, (jnp.zeros((QB, 1), F32),) * 2, jnp.zeros((QB, LANES), F32))
            ms, ls, acc = lax.fori_loop(0, i + 1, kv, init)
            o_ref[pl.ds(r0, QB), :] = acc * jnp.where(left, 1.0 / ls[0], 1.0 / ls[1])
            lse_ref[pl.ds(r0, QB), :] = jnp.where(left, ms[0] + jnp.log(ls[0]) * LOG2_E, ms[1] + jnp.log(ls[1]) * LOG2_E)
            return 0

        lax.fori_loop(0, S // QB, qblock, 0)

    pair = lambda w: pl.BlockSpec((S, w), lambda p: (0, p))
    return _carrier_call(
        body, name=name, grid=(MLA_H // 2,), in_specs=[pair(2 * LANES), pair(2 * LANES), pair(LANES)],
        out_specs=[pair(LANES), pair(LANES)],
        out_shape=[jax.ShapeDtypeStruct((S, MLA_H * MLA_V), F32), jax.ShapeDtypeStruct((S, MLA_H * MLA_V), F32)],
        args=(qa, ka, va), sem=("parallel",), carry=carry)


def _mla_bwd(name, qa, ka, va, o, lse, do, do_block0, carry=None):
    def body(q_ref, k_ref, v_ref, o_ref, lse_ref, do_ref, dq_ref, dk_ref, dv_ref):
        m0f, m1f = _head_masks(F32)
        m0b, m1b = _head_masks(BF16)
        dk_ref[...] = jnp.zeros_like(dk_ref)
        dv_ref[...] = jnp.zeros_like(dv_ref)

        def qblock(i, _):
            r0 = pl.multiple_of(i * QB, QB)
            rows = pl.ds(r0, QB)
            do_f = do_ref[rows, :]
            prod = do_f * o_ref[rows, :]
            deltas = [jnp.sum(prod * m0f, axis=-1, keepdims=True), jnp.sum(prod * m1f, axis=-1, keepdims=True)]
            lse_v = lse_ref[rows, :]
            lses = [lse_v[:, 0:1], lse_v[:, 64:65]]
            dob = do_f.astype(BF16)
            dos = [dob * m0b, dob * m1b]
            qs = [q_ref[rows, hh * LANES:(hh + 1) * LANES] for hh in range(2)]
            rowc = lax.shift_right_logical(r0 + _iota((QB, QB), 0), 6)

            def kv(kb, dqs):
                c0 = pl.multiple_of(kb * QB, QB)
                cols = pl.ds(c0, QB)
                v = v_ref[cols, :]
                ok = lax.shift_right_logical(c0 + _iota((QB, QB), 1), 6) <= rowc
                out = []
                dv = None
                for hh in range(2):
                    k = k_ref[cols, hh * LANES:(hh + 1) * LANES]
                    s = _dot(qs[hh], k, "nt") * (SCALE_A * LOG2_E)
                    p = jnp.where(ok, jnp.exp2(s - lses[hh]), 0.0)
                    dp = _dot(dos[hh], v, "nt")
                    ds = (p * (dp - deltas[hh]) * SCALE_A).astype(BF16)
                    out.append(dqs[hh] + _dot(ds, k))
                    dk_ref[cols, hh * LANES:(hh + 1) * LANES] += _dot(ds, qs[hh], "tn")
                    part = _dot(p.astype(BF16), dos[hh], "tn")
                    dv = part if dv is None else dv + part
                dv_ref[cols, :] += dv
                return tuple(out)

            dqs = lax.fori_loop(0, i + 1, kv, (jnp.zeros((QB, LANES), F32),) * 2)
            for hh in range(2):
                dq_ref[rows, hh * LANES:(hh + 1) * LANES] = dqs[hh]
            return 0

        lax.fori_loop(0, S // QB, qblock, 0)

    pair = lambda w: pl.BlockSpec((S, w), lambda p: (0, p))
    return _carrier_call(
        body, name=name, grid=(MLA_H // 2,),
        in_specs=[pair(2 * LANES), pair(2 * LANES), pair(LANES), pair(LANES), pair(LANES),
                  pl.BlockSpec((S, LANES), lambda p: (0, do_block0 + p))],
        out_specs=[pair(2 * LANES), pair(2 * LANES), pair(LANES)],
        out_shape=[jax.ShapeDtypeStruct((S, MLA_H * LANES), F32), jax.ShapeDtypeStruct((S, MLA_H * LANES), F32),
                   jax.ShapeDtypeStruct((S, MLA_H * MLA_V), F32)],
        args=(qa, ka, va, o, lse, do), sem=("parallel",), carry=carry)


def _sb_weights(q_h, k, c, before, tri_suffix):
    z = _dot(q_h, k, "nt") * (SCALE_B * LOG2_E)
    sp = jnp.maximum(z, 0.0) + jnp.log(1.0 + jnp.exp2(-jnp.abs(z))) * LOG2_E
    log_keep = jnp.where(before, -sp, 0.0)
    to_the_right, total = _running_sum(log_keep, tri_suffix, True)
    w = jnp.where(before, jnp.exp2(z - sp + to_the_right + c), 0.0)
    return w, jnp.exp2(z - sp), total


def _sb_fwd(name, proj, carry=None):
    def body(q_ref, k_ref, v_ref, o_ref):
        m0b, m1b = _head_masks(BF16)
        tri_suffix = (_iota((MXU_W, MXU_W), 0) > _iota((MXU_W, MXU_W), 1)).astype(BF16)

        def qblock(i, _):
            r0 = pl.multiple_of(i * QB, QB)
            q = q_ref[pl.ds(r0, QB), :].astype(BF16)
            qs = [q * m0b, q * m1b]
            rowg = r0 + _iota((QB, QB), 0)

            def kv(step, carry):
                cs, acc = carry
                c0 = pl.multiple_of((i - step) * QB, QB)
                k = k_ref[pl.ds(c0, QB), :].astype(BF16)
                v = v_ref[pl.ds(c0, QB), :].astype(BF16)
                before = (c0 + _iota((QB, QB), 1)) < rowg
                new_c = []
                for hh in range(2):
                    w, _, tot = _sb_weights(qs[hh], k, cs[hh], before, tri_suffix)
                    new_c.append(cs[hh] + tot)
                    acc = acc + _dot(w.astype(BF16), v * (m0b if hh == 0 else m1b))
                return tuple(new_c), acc

            init = ((jnp.zeros((QB, 1), F32),) * 2, jnp.zeros((QB, LANES), F32))
            _, acc = lax.fori_loop(0, i + 1, kv, init)
            o_ref[pl.ds(r0, QB), :] = acc.astype(BF16)
            return 0

        lax.fori_loop(0, S // QB, qblock, 0)

    col = lambda base: pl.BlockSpec((S, LANES), lambda p: (0, base // LANES + p))
    return _carrier_call(
        body, name=name, grid=(SB_H // 2,), in_specs=[col(P_QB), col(P_KB), col(P_VB)],
        out_specs=[pl.BlockSpec((S, LANES), lambda p: (0, p))],
        out_shape=[jax.ShapeDtypeStruct((S, SB_H * SB_DIM), BF16)],
        args=(proj, proj, proj), sem=("parallel",), carry=carry)


def _sb_bwd(name, proj, do, do_block0, carry=None):
    nb = S // QB

    def body(q_ref, k_ref, v_ref, do_ref, dq_ref, dk_ref, dv_ref, sig_scr, dl_scr, dk_acc, dv_acc):
        m0b, m1b = _head_masks(BF16)
        tri_suffix = (_iota((MXU_W, MXU_W), 0) > _iota((MXU_W, MXU_W), 1)).astype(BF16)
        tri_prefix = (_iota((MXU_W, MXU_W), 0) < _iota((MXU_W, MXU_W), 1)).astype(BF16)
        dk_acc[...] = jnp.zeros_like(dk_acc)
        dv_acc[...] = jnp.zeros_like(dv_acc)

        def qblock(i, _):
            r0 = pl.multiple_of(i * QB, QB)
            rows = pl.ds(r0, QB)
            q = q_ref[rows, :].astype(BF16)
            qs = [q * m0b, q * m1b]
            dob = do_ref[rows, :].astype(BF16)
            dos = [dob * m0b, dob * m1b]
            rowg = r0 + _iota((QB, QB), 0)

            def sweep_left(step, cs):
                kb = i - step
                c0 = pl.multiple_of(kb * QB, QB)
                cols = pl.ds(c0, QB)
                k = k_ref[cols, :].astype(BF16)
                v = v_ref[cols, :].astype(BF16)
                before = (c0 + _iota((QB, QB), 1)) < rowg
                new_c = []
                dv = None
                for hh in range(2):
                    w, sig, tot = _sb_weights(qs[hh], k, cs[hh], before, tri_suffix)
                    new_c.append(cs[hh] + tot)
                    sig_scr[hh, kb] = sig
                    dl_scr[hh, kb] = _dot(dos[hh], v, "nt") * w
                    part = _dot(w.astype(BF16), dos[hh], "tn")
                    dv = part if dv is None else dv + part
                dv_acc[cols, :] += dv
                return tuple(new_c)

            lax.fori_loop(0, i + 1, sweep_left, (jnp.zeros((QB, 1), F32),) * 2)

            def sweep_right(kb, carry):
                ps, dq = carry
                c0 = pl.multiple_of(kb * QB, QB)
                cols = pl.ds(c0, QB)
                k = k_ref[cols, :].astype(BF16)
                before = (c0 + _iota((QB, QB), 1)) < rowg
                new_p = []
                dk = None
                for hh in range(2):
                    dl = dl_scr[hh, kb]
                    sig = sig_scr[hh, kb]
                    to_the_left, total = _running_sum(dl, tri_prefix, False)
                    earlier = to_the_left + ps[hh]
                    new_p.append(ps[hh] + total)
                    dz = (jnp.where(before, dl * (1.0 - sig) - earlier * sig, 0.0) * SCALE_B).astype(BF16)
                    dq = dq + _dot(dz, k * (m0b if hh == 0 else m1b))
                    part = _dot(dz, qs[hh], "tn")
                    dk = part if dk is None else dk + part
                dk_acc[cols, :] += dk
                return tuple(new_p), dq

            init = ((jnp.zeros((QB, 1), F32),) * 2, jnp.zeros((QB, LANES), F32))
            _, dq = lax.fori_loop(0, i + 1, sweep_right, init)
            dq_ref[rows, :] = dq.astype(BF16)
            return 0

        lax.fori_loop(0, nb, qblock, 0)
        dk_ref[...] = dk_acc[...].astype(BF16)
        dv_ref[...] = dv_acc[...].astype(BF16)

    col = lambda base: pl.BlockSpec((S, LANES), lambda p: (0, base // LANES + p))
    out = pl.BlockSpec((S, LANES), lambda p: (0, p))
    shape = jax.ShapeDtypeStruct((S, SB_H * SB_DIM), BF16)
    return _carrier_call(
        body, name=name, grid=(SB_H // 2,),
        in_specs=[col(P_QB), col(P_KB), col(P_VB), pl.BlockSpec((S, LANES), lambda p: (0, do_block0 + p))],
        out_specs=[out, out, out], out_shape=[shape, shape, shape],
        scratch_shapes=[pltpu.VMEM((2, nb, QB, QB), F32), pltpu.VMEM((2, nb, QB, QB), F32),
                        pltpu.VMEM((S, LANES), F32), pltpu.VMEM((S, LANES), F32)],
        args=(proj, proj, proj, do), sem=("parallel",), carry=carry)


def _band_row_index():
    j = np.arange(TOEP_W)
    rel = np.clip(LEFT_CHUNKS * CHUNK - j, -REL_CLIP, REL_CLIP) + REL_CLIP
    rel[BAND_W:] = 2 * REL_CLIP
    return rel.astype(np.int32)


def _band_tiles(r0_ref, q_ref, kpad, vpad, m, m0b, m1b, static_ok, bias):
    r0 = pl.multiple_of(m * BQ, BQ)
    q = q_ref[0, pl.ds(r0, BQ), :]
    kw = kpad[pl.ds(r0, BAND_W), :]
    vw = vpad[pl.ds(r0, BAND_W), :]
    ok = static_ok & ((r0 - BAND_PAD + _iota((BQ, BAND_W), 1)) >= 0)
    qs = [q * m0b, q * m1b]
    ps = []
    for hh in range(2):
        s = jnp.where(ok, _dot(qs[hh], kw, "nt") * (SCALE_C * LOG2_E) + bias[hh], NEG)
        e = jnp.exp2(s - jnp.max(s, axis=-1, keepdims=True))
        ps.append(e * (1.0 / jnp.sum(e, axis=-1, keepdims=True)))
    return r0, qs, kw, vw, ps


def _band_setup(qkv_ref, r0_ref, kpad, vpad):
    kpad[0:BAND_PAD, :] = jnp.zeros((BAND_PAD, LANES), BF16)
    vpad[0:BAND_PAD, :] = jnp.zeros((BAND_PAD, LANES), BF16)
    kpad[BAND_PAD:, :] = qkv_ref[1]
    vpad[BAND_PAD:, :] = qkv_ref[2]
    jc = lax.shift_right_logical(_iota((BQ, BAND_W), 1), 6)
    rc = lax.shift_right_logical(_iota((BQ, BAND_W), 0), 6)
    static_ok = (jc >= rc) & (jc <= rc + LEFT_CHUNKS)
    bias = []
    for hh in range(2):
        row = jnp.broadcast_to(r0_ref[hh:hh + 1, :] * LOG2_E, (BQ, TOEP_W))
        bias.append(pltpu.roll(row, 0, 1, stride=1, stride_axis=0)[:, :BAND_W])
    return static_ok, bias


def _band_fwd(name, qkv, r0, carry=None):
    def body(qkv_ref, r0_ref, o_ref, kpad, vpad):
        m0b, m1b = _head_masks(BF16)
        static_ok, bias = _band_setup(qkv_ref, r0_ref, kpad, vpad)

        def qblock(m, _):
            r0_, _, _, vw, ps = _band_tiles(r0_ref, qkv_ref, kpad, vpad, m, m0b, m1b, static_ok, bias)
            o = _dot(ps[0].astype(BF16), vw * m0b) + _dot(ps[1].astype(BF16), vw * m1b)
            o_ref[pl.ds(r0_, BQ), :] = o.astype(BF16)
            return 0

        lax.fori_loop(0, S // BQ, qblock, 0)

    return _carrier_call(
        body, name=name, grid=(C_H // 2,),
        in_specs=[pl.BlockSpec((3, S, LANES), lambda p: (0, 0, p)), pl.BlockSpec((None, 2, TOEP_W), lambda p: (p, 0, 0))],
        out_specs=[pl.BlockSpec((S, LANES), lambda p: (0, p))],
        out_shape=[jax.ShapeDtypeStruct((S, C_H * C_DIM), BF16)],
        scratch_shapes=[pltpu.VMEM((S + BAND_PAD, LANES), BF16), pltpu.VMEM((S + BAND_PAD, LANES), BF16)],
        args=(qkv, r0), sem=("parallel",), carry=carry)


def _band_bwd(name, qkv, r0, do, carry=None):
    def body(qkv_ref, r0_ref, do_ref, dqkv_ref, dr0_ref, kpad, vpad, dkpad, dvpad, db_acc):
        m0b, m1b = _head_masks(BF16)
        static_ok, bias = _band_setup(qkv_ref, r0_ref, kpad, vpad)
        dkpad[...] = jnp.zeros_like(dkpad)
        dvpad[...] = jnp.zeros_like(dvpad)
        db_acc[...] = jnp.zeros_like(db_acc)

        def qblock(m, _):
            r0_, qs, kw, vw, ps = _band_tiles(r0_ref, qkv_ref, kpad, vpad, m, m0b, m1b, static_ok, bias)
            dob = do_ref[pl.ds(r0_, BQ), :].astype(BF16)
            dos = [dob * m0b, dob * m1b]
            dq = None
            dk = None
            dv = None
            for hh in range(2):
                p = ps[hh]
                dp = _dot(dos[hh], vw, "nt")
                ds = p * (dp - jnp.sum(dp * p, axis=-1, keepdims=True))
                db_acc[hh, :, 0:BAND_W] += ds
                dsb = (ds * SCALE_C).astype(BF16)
                t = _dot(dsb, kw * (m0b if hh == 0 else m1b))
                dq = t if dq is None else dq + t
                t = _dot(dsb, qs[hh], "tn")
                dk = t if dk is None else dk + t
                t = _dot(p.astype(BF16), dos[hh], "tn")
                dv = t if dv is None else dv + t
            dqkv_ref[0, pl.ds(r0_, BQ), :] = dq.astype(BF16)
            dkpad[pl.ds(r0_, BAND_W), :] += dk
            dvpad[pl.ds(r0_, BAND_W), :] += dv
            return 0

        lax.fori_loop(0, S // BQ, qblock, 0)
        dqkv_ref[1] = dkpad[BAND_PAD:, :].astype(BF16)
        dqkv_ref[2] = dvpad[BAND_PAD:, :].astype(BF16)
        sub = _iota((8, TOEP_W), 0)
        for hh in range(2):
            folded = db_acc[hh, 0:8, :]
            for a in range(1, BQ // 8):
                folded = folded + pltpu.roll(db_acc[hh, 8 * a:8 * a + 8, :], TOEP_W - 8 * a, 1)
            for bit in range(3):
                moved = pltpu.roll(folded, TOEP_W - (1 << bit), 1)
                folded = jnp.where((sub & (1 << bit)) != 0, moved, folded)
            dr0_ref[hh:hh + 1, :] = jnp.sum(folded, axis=0, keepdims=True)

    return _carrier_call(
        body, name=name, grid=(C_H // 2,),
        in_specs=[pl.BlockSpec((3, S, LANES), lambda p: (0, 0, p)), pl.BlockSpec((None, 2, TOEP_W), lambda p: (p, 0, 0)),
                  pl.BlockSpec((S, LANES), lambda p: (0, p))],
        out_specs=[pl.BlockSpec((3, S, LANES), lambda p: (0, 0, p)), pl.BlockSpec((None, 2, TOEP_W), lambda p: (p, 0, 0))],
        out_shape=[jax.ShapeDtypeStruct((3, S, C_H * C_DIM), BF16), jax.ShapeDtypeStruct((C_H // 2, 2, TOEP_W), F32)],
        scratch_shapes=[pltpu.VMEM((S + BAND_PAD, LANES), BF16), pltpu.VMEM((S + BAND_PAD, LANES), BF16),
                        pltpu.VMEM((S + BAND_PAD, LANES), F32), pltpu.VMEM((S + BAND_PAD, LANES), F32),
                        pltpu.VMEM((2, BQ, TOEP_W), F32)],
        args=(qkv, r0, do), sem=("parallel",), carry=carry)


def _bias_table_grad(name, dr0):
    w_out = 5 * LANES

    def body(d_ref, o_ref):
        j = _iota((TOEP_W, w_out), 0)
        rel = jnp.clip(LEFT_CHUNKS * CHUNK - j, -REL_CLIP, REL_CLIP) + REL_CLIP
        rel = jnp.where(j >= BAND_W, 2 * REL_CLIP, rel)
        onehot = (rel == _iota((TOEP_W, w_out), 1)).astype(BF16)
        d = d_ref[...]
        hi = d.astype(BF16)
        mid = (d - hi.astype(F32))
        mid_b = mid.astype(BF16)
        lo = (mid - mid_b.astype(F32)).astype(BF16)
        o_ref[...] = _dot(hi, onehot) + _dot(mid_b, onehot) + _dot(lo, onehot)

    return pl.pallas_call(
        body, name=name, out_shape=jax.ShapeDtypeStruct((C_H, w_out), F32),
        in_specs=[pl.BlockSpec((C_H, TOEP_W), lambda: (0, 0))], out_specs=pl.BlockSpec((C_H, w_out), lambda: (0, 0)),
        grid=(),
    )(dr0)


def _carry_gather(cy, slots, names, ici, d2d):
    idx = [cy.operand(slots[n], True) for n in names]
    n = len(names)
    base_i = cy.sems(3 * n) if ici else 0
    base_d = cy.sems(3 * n) if d2d else 0

    def piece(refs, t, slot, cc):
        return refs[idx[t]].at[slot, _half_rows(cc, slots[names[t]].shape[1] // 2), :]

    def over_ici(refs, send, recv, arriving):
        x, y, c, chips = _position()
        out = []
        for t in range(n):
            for j in range(3):
                r = piece(refs, t, 2 * chips[j][0] + chips[j][1] if arriving else 2 * x + y, c)
                out.append(_remote(r, r, send, recv, base_i + 3 * t + j, (*chips[j], c)))
        return out

    def over_d2d(refs, send, recv, arriving):
        x, y, c, chips = _position()
        out = []
        for t in range(n):
            for j in range(3):
                r = piece(refs, t, 2 * chips[j][0] + chips[j][1], 1 - c if arriving else c)
                out.append(_remote(r, r, send, recv, base_d + 3 * t + j, (x, y, 1 - c)))
        return out

    def start_ici(refs, fresh, send, recv):
        for cp in over_ici(refs, send, recv, False):
            cp.start()

    def wait_ici(refs, fresh, send, recv):
        for cp in over_ici(refs, send, recv, True):
            cp.wait_recv()
        for cp in over_ici(refs, send, recv, False):
            cp.wait_send()

    def start_d2d(refs, fresh, send, recv):
        for cp in over_d2d(refs, send, recv, False):
            cp.start()

    def wait_d2d(refs, fresh, send, recv):
        for cp in over_d2d(refs, send, recv, True):
            cp.wait_recv()
        for cp in over_d2d(refs, send, recv, False):
            cp.wait_send()

    if ici and d2d:
        cy.starts.append(start_ici)
        cy.finishes += [wait_ici, start_d2d, wait_d2d]
    elif ici:
        cy.starts.append(start_ici)
        cy.finishes.append(wait_ici)
    else:
        cy.starts.append(start_d2d)
        cy.finishes.append(wait_d2d)

    def done(aliased, fresh):
        for t, name in enumerate(names):
            slots[name] = aliased[idx[t]]

    cy.on_done.append(done)


def _carry_chip_exchange(cy, sums, got, names):
    idx = [cy.operand(sums[n], False) for n in names]
    out = [cy.result((3,) + sums[n].shape[1:], BF16) for n in names]
    base = cy.sems(3 * len(names))

    def copies(refs, fresh, send, recv):
        x, y, c, chips = _position()
        return [_remote(refs[idx[t]].at[2 * chips[j][0] + chips[j][1]], fresh[out[t]].at[j], send, recv, base + 3 * t + j,
                        (*chips[j], c)) for t in range(len(names)) for j in range(3)]

    def start(refs, fresh, send, recv):
        for cp in copies(refs, fresh, send, recv):
            cp.start()

    def wait(refs, fresh, send, recv):
        for cp in copies(refs, fresh, send, recv):
            cp.wait()

    cy.starts.append(start)
    cy.finishes.append(wait)

    def done(aliased, fresh):
        for t, name in enumerate(names):
            got[name] = fresh[out[t]]

    cy.on_done.append(done)


def _run_carry(name, cy):
    _, res = _carrier_call(None, name=name, grid=(), in_specs=[], out_specs=[], out_shape=[], args=(), sem=(), carry=cy)
    cy.done(res)


FIRST_WEIGHTS = ("ev_w_in", "ev_w_uq", "ev_w_ukv")
WEIGHTS_A = ("ev_w_out", "w_gate0", "w_up0")
WEIGHTS_B = ("w_down0", "od_w_qkv", "od_w_out")
WEIGHTS_C = ("w_gate1",)
WEIGHTS_D = ("w_up1", "w_down1")
GRAD_GROUPS = {"ffn1": ("w_gate1", "w_up1", "w_down1"), "od": ("od_w_qkv", "od_w_out"),
               "ffn0": ("w_gate0", "w_up0", "w_down0"), "ev_out": ("ev_w_out",),
               "ev": ("ev_w_in", "ev_w_uq", "ev_w_ukv")}


def _carry_pair_exchange(cy, parts, theirs, names):
    idx = [cy.operand(parts[n], False) for n in names]
    out = [cy.result((N_CHIPS, parts[n].shape[1] // 2, parts[n].shape[2]), BF16) for n in names]
    base = cy.sems(len(names))

    def copies(refs, fresh, send, recv):
        x, y, c, _ = _position()
        return [_remote(refs[idx[t]].at[:, _half_rows(1 - c, parts[n].shape[1] // 2), :], fresh[out[t]], send, recv,
                        base + t, (x, y, 1 - c)) for t, n in enumerate(names)]

    cy.starts.append(lambda refs, fresh, send, recv: [cp.start() for cp in copies(refs, fresh, send, recv)])
    cy.finishes.append(lambda refs, fresh, send, recv: [cp.wait() for cp in copies(refs, fresh, send, recv)])

    def done(aliased, fresh):
        for t, name in enumerate(names):
            theirs[name] = fresh[out[t]]

    cy.on_done.append(done)


def _carry_sibling_exchange(cy, fulls, pieces):
    idx = [cy.operand(fulls[p], True) for p, _ in pieces]
    base = cy.sems(len(pieces))

    def copies(refs, send, recv, arriving):
        x, y, c, _ = _position()
        out = []
        for t, (p, layer) in enumerate(pieces):
            r = refs[idx[t]].at[layer, _half_rows(1 - c if arriving else c, fulls[p].shape[1] // 2), :]
            out.append(_remote(r, r, send, recv, base + t, (x, y, 1 - c)))
        return out

    def start(refs, fresh, send, recv):
        for cp in copies(refs, send, recv, False):
            cp.start()

    def wait(refs, fresh, send, recv):
        for cp in copies(refs, send, recv, True):
            cp.wait_recv()
        for cp in copies(refs, send, recv, False):
            cp.wait_send()

    cy.starts.append(start)
    cy.finishes.append(wait)

    def done(aliased, fresh):
        for t, (p, _) in enumerate(pieces):
            fulls[p] = aliased[idx[t]]

    cy.on_done.append(done)


RIDES = {
    "cast_rest": (("gather", FIRST_WEIGHTS),),
    "mla_attn": (("gather_ici", WEIGHTS_A),),
    "sb_attn": (("gather_d2d", WEIGHTS_A), ("gather_ici", WEIGHTS_B)),
    "ev_out": (("gather_d2d", WEIGHTS_B),),
    "ffn0": (("gather_ici", WEIGHTS_C),),
    "qkv": (("gather_d2d", WEIGHTS_C),),
    "band_attn": (("gather_ici", WEIGHTS_D),),
    "od_out": (("gather_d2d", WEIGHTS_D),),
    "od_out_bwd_w": (("pair", "ffn1"),),
    "band_attn_bwd": (("chips", "ffn1"),),
    "rms_mix1_bwd": (("pair", "od"),),
    "ev_out_bwd_w": (("pair", "ffn0"),),
    "mla_attn_bwd": (("chips", "od"), ("sibling", "ffn1"), ("pair", "ev_out")),
    "sb_attn_bwd": (("chips", "ffn0"), ("sibling", "od"), ("chips", "ev_out")),
    "proj_in_bwd_w": (("sibling", "ffn0"), ("sibling", "ev_out")),
    "grads_pair_ev": (("pair", "ev"),),
    "proj_in_bwd_x": (("chips", "ev"),),
    "grads_sibling_ev": (("sibling", "ev"),),
}


class _Exchanges:
    def __init__(self, slots, pos, shapes, cast_rest):
        self.slots, self.pos, self.shapes, self.cast_rest = dict(slots), pos, shapes, cast_rest
        self.parts, self.theirs, self.sums, self.got, self.fulls = {}, {}, {}, {}, {}

    def begin(self):
        self.slots.update(self.cast_rest(self.carry("cast_rest")))

    def weights(self, *names):
        return [self.slots[n] for n in names]

    def _pair_sums(self, group):
        for n in GRAD_GROUPS[group]:
            if n not in self.sums:
                self.sums[n] = _pair_sum("pair_sum_" + n, self.parts[n], self.theirs[n], self.pos)

    def _chip_sums(self, group):
        for n in GRAD_GROUPS[group]:
            param, layer = PART_OF[n]
            self.fulls[param] = _chip_sum("chip_sum_" + n, self.sums[n], self.got[n], self.pos, layer,
                                          self.shapes[param], self.fulls.get(param))

    def carry(self, stage):
        cy = _Carry()
        for step, what in RIDES[stage]:
            if step == "gather":
                _carry_gather(cy, self.slots, what, True, True)
            elif step == "gather_ici":
                _carry_gather(cy, self.slots, what, True, False)
            elif step == "gather_d2d":
                _carry_gather(cy, self.slots, what, False, True)
            elif step == "pair":
                _carry_pair_exchange(cy, self.parts, self.theirs, GRAD_GROUPS[what])
            elif step == "chips":
                self._pair_sums(what)
                _carry_chip_exchange(cy, self.sums, self.got, GRAD_GROUPS[what])
            elif step == "sibling":
                self._chip_sums(what)
                _carry_sibling_exchange(cy, self.fulls, [PART_OF[n] for n in GRAD_GROUPS[what]])
        return cy

    def grads(self, group, parts):
        self.parts.update(parts)
        if group == "ev":
            _run_carry("grads_pair_ev", self.carry("grads_pair_ev"))

    def finish(self):
        _run_carry("grads_sibling_ev", self.carry("grads_sibling_ev"))
        return {n: self.fulls[n] for n in BIG}


class _NoExchanges:
    def __init__(self, slots):
        self.slots, self.parts = dict(slots), {}

    def begin(self):
        pass

    def weights(self, *names):
        return [self.slots[n] for n in names]

    def carry(self, stage):
        return None

    def grads(self, group, parts):
        self.parts.update(parts)


def _w_in_pieces():
    segments = ((0, Q_LORA, P_CQ), (Q_LORA, Q_LORA + KV_LORA, P_CKV),
                (Q_LORA + KV_LORA, Q_LORA + KV_LORA + MLA_ROPE, P_KR + KR_LANE),
                (Q_LORA + KV_LORA + MLA_ROPE, EVEN_IN, P_QB))
    width = EVEN_IN // N_CHIPS
    pieces = []
    for lo, hi, at in segments:
        for k in range(N_CHIPS):
            a, b = max(lo, k * width), min(hi, (k + 1) * width)
            if a < b:
                pieces.append((k, a - k * width, b - a, at + a - lo))
    return pieces


def _w_in_padded(name, w_in_s):
    tr = MXU_W

    def body(s_ref, o_ref):
        o_ref[...] = jnp.zeros(o_ref.shape, BF16)
        for k, a, n, at in _w_in_pieces():
            o_ref[:, at:at + n] = s_ref[k, :, a:a + n]

    return pl.pallas_call(
        body, name=name, grid=(D // tr,),
        in_specs=[pl.BlockSpec((N_CHIPS, tr, EVEN_IN // N_CHIPS), lambda i: (0, i, 0))],
        out_specs=pl.BlockSpec((tr, P_IN), lambda i: (i, 0)), out_shape=jax.ShapeDtypeStruct((D, P_IN), BF16),
        compiler_params=_params("parallel"))(w_in_s)


def _w_in_sharded(name, d_w_in_p):
    tr = MXU_W

    def body(p_ref, o_ref):
        for k, a, n, at in _w_in_pieces():
            o_ref[k, :, a:a + n] = p_ref[:, at:at + n]

    return pl.pallas_call(
        body, name=name, grid=(D // tr,),
        in_specs=[pl.BlockSpec((tr, P_IN), lambda i: (i, 0))],
        out_specs=pl.BlockSpec((N_CHIPS, tr, EVEN_IN // N_CHIPS), lambda i: (0, i, 0)),
        out_shape=jax.ShapeDtypeStruct((N_CHIPS, D, EVEN_IN // N_CHIPS), BF16),
        compiler_params=_params("parallel"))(d_w_in_p)


def _first_weights(w_in_s, w_uq_s, w_ukv_s):
    gw = {"ev_w_in": w_in_s, "ev_w_uq": w_uq_s, "ev_w_ukv": w_ukv_s}
    w_in_p = _w_in_padded("w_in_padded", w_in_s)
    w_uq = jnp.moveaxis(gw["ev_w_uq"], 0, 1).reshape(Q_LORA, MLA_H, MLA_NOPE + MLA_ROPE)
    w_uq_p = jnp.concatenate([w_uq, jnp.zeros((Q_LORA, MLA_H, LANES - MLA_NOPE - MLA_ROPE), BF16)], axis=2)
    w_ukv = jnp.moveaxis(gw["ev_w_ukv"], 0, 1).reshape(KV_LORA, MLA_H, MLA_NOPE + MLA_V)
    w_uk_p = jnp.concatenate([w_ukv[:, :, :MLA_NOPE], jnp.zeros((KV_LORA, MLA_H, LANES - MLA_NOPE), BF16)], axis=2)
    return dict(
        w_in=w_in_p, w_uq=w_uq_p.reshape(Q_LORA, MLA_H * LANES), w_uk=w_uk_p.reshape(KV_LORA, MLA_H * LANES),
        w_uv=w_ukv[:, :, MLA_NOPE:].reshape(KV_LORA, MLA_H * MLA_V))


def _proj_mm(name, u, w_in):
    return _mm(name, u, w_in, kind="nn", grid=(S // TM, 1, 1),
               a_spec=pl.BlockSpec((TM, D), lambda i, j, k: (i, 0)), b_spec=pl.BlockSpec((D, P_IN), lambda i, j, k: (0, 0)),
               o_spec=pl.BlockSpec((TM, P_IN), lambda i, j, k: (i, 0)), out_shape=(S, P_IN), out_dtype=F32, acc_shape=None)


def _out_proj(name, o, w, resid, carry=None):
    return _mm(name, o, w, kind="nn", grid=(S // TM, 1, 1),
               a_spec=pl.BlockSpec((TM, D), lambda i, j, k: (i, 0)), b_spec=pl.BlockSpec((D, D), lambda i, j, k: (0, 0)),
               o_spec=pl.BlockSpec((TM, D), lambda i, j, k: (i, 0)), out_shape=(S, D), out_dtype=F32, acc_shape=None,
               resid=resid, r_spec=pl.BlockSpec((TM, D), lambda i, j, k: (i, 0)), carry=carry)


def _out_proj_bwd(name, dh, o, w, ex):
    d_o = _mm(name + "_x", dh, w, kind="nt", grid=(S // TM, 1, 1),
              a_spec=pl.BlockSpec((TM, D), lambda i, j, k: (i, 0)), b_spec=pl.BlockSpec((D, D), lambda i, j, k: (0, 0)),
              o_spec=pl.BlockSpec((TM, D), lambda i, j, k: (i, 0)), out_shape=(S, D), out_dtype=F32, acc_shape=None)
    d_w = _mm(name + "_w", o, dh, kind="tn", grid=(2, S // TM),
              a_spec=pl.BlockSpec((TM, TM), lambda j, k: (k, j)), b_spec=pl.BlockSpec((TM, D), lambda j, k: (k, 0)),
              o_spec=pl.BlockSpec((TM, D), lambda j, k: (j, 0)), out_shape=(D, D), out_dtype=BF16, acc_shape=(TM, D),
              carry=ex.carry(name + "_w"))
    return d_o, d_w


def _local_step(x, tgt, sm, ex):
    def riding(stage, fn, *args):
        cy = ex.carry(stage)
        res, copies = fn(stage, *args, carry=cy)
        if cy is not None:
            cy.done(copies)
        return res

    cos_t, sin_t = _rope_tables()
    g_mix, g_ffn = sm["g_mix"], sm["g_ffn"]
    r0 = sm["od_rel_bias"][0][:, _band_row_index()].reshape(C_H // 2, 2, TOEP_W)
    nt = 3

    ex.begin()
    w = _first_weights(*ex.weights(*FIRST_WEIGHTS))
    u0 = _rms_fwd("rms_mix0", x, g_mix[0:1])
    proj = _proj_mm("proj_in", u0, w["w_in"])
    qa, ka, va = _mla_prep_fwd("mla_prep", proj, sm["ev_g_cq"], sm["ev_g_ckv"], w["w_uq"], w["w_uk"], w["w_uv"], cos_t, sin_t)
    o_a, lse = riding("mla_attn", _mla_fwd, qa, ka, va)
    o_b, = riding("sb_attn", _sb_fwd, proj)
    o_ev = jnp.concatenate([o_a.astype(BF16), o_b], axis=1)
    w["ev_w_out"] = ex.weights("ev_w_out")[0].reshape(D, D)
    h1 = _out_proj("ev_out", o_ev, w["ev_w_out"], x, ex.carry("ev_out"))
    w["w_gate0"], w["w_up0"], w["w_down0"] = ex.weights("w_gate0", "w_up0", "w_down0")
    h2, gate0, up0 = riding("ffn0", _ffn_fwd, h1, g_ffn[0:1], w["w_gate0"], w["w_up0"], w["w_down0"])
    w["w_qkv"] = jnp.moveaxis(ex.weights("od_w_qkv")[0], 0, 1).reshape(D, nt * D)
    u2 = _rms_fwd("rms_mix1", h2, g_mix[1:2])
    qkv = _mm("qkv", u2, w["w_qkv"], kind="nn", grid=(S // TM, nt, 1),
              a_spec=pl.BlockSpec((TM, D), lambda i, t, k: (i, 0)), b_spec=pl.BlockSpec((D, D), lambda i, t, k: (0, t)),
              o_spec=pl.BlockSpec((None, TM, D), lambda i, t, k: (t, i, 0)),
              out_shape=(nt, S, D), out_dtype=BF16, acc_shape=None, carry=ex.carry("qkv"))
    o_od, = riding("band_attn", _band_fwd, qkv, r0)
    w["od_w_out"] = ex.weights("od_w_out")[0].reshape(D, D)
    h3 = _out_proj("od_out", o_od, w["od_w_out"], h2, ex.carry("od_out"))
    w["w_gate1"], w["w_up1"], w["w_down1"] = ex.weights("w_gate1", "w_up1", "w_down1")
    (h4, gate1, up1), _ = _ffn_fwd("ffn1", h3, g_ffn[1:2], w["w_gate1"], w["w_up1"], w["w_down1"])

    loss, dh4, dg_final = _loss_bwd("loss", h4, sm["g_final"].reshape(1, D), tgt)

    dh3, dg_ffn1, u3, dgate, dup, act = _ffn_bwd("ffn1_bwd", dh4, h3, g_ffn[1:2], gate1, up1,
                                                 w["w_gate1"], w["w_up1"], w["w_down1"])
    d_wg1, d_wu1, d_wd1 = _ffn_wgrads("ffn1_dw", u3, dgate, dup, act, dh4)
    ex.grads("ffn1", {"w_gate1": d_wg1, "w_up1": d_wu1, "w_down1": d_wd1})

    d_ood, d_w_od_out = _out_proj_bwd("od_out_bwd", dh3, o_od, w["od_w_out"], ex)
    dqkv, dr0 = riding("band_attn_bwd", _band_bwd, qkv, r0, d_ood)
    du2 = _mm("qkv_bwd_x", dqkv, w["w_qkv"], kind="nt", grid=(S // TM, nt),
              a_spec=pl.BlockSpec((None, TM, D), lambda i, t: (t, i, 0)), b_spec=pl.BlockSpec((D, D), lambda i, t: (0, t)),
              o_spec=pl.BlockSpec((TM, D), lambda i, t: (i, 0)), out_shape=(S, D), out_dtype=F32, acc_shape=(TM, D))
    d_w_qkv = _mm("qkv_bwd_w", u2, dqkv, kind="tn", grid=(nt, S // TM),
                  a_spec=pl.BlockSpec((TM, D), lambda t, k: (k, 0)), b_spec=pl.BlockSpec((None, TM, D), lambda t, k: (t, k, 0)),
                  o_spec=pl.BlockSpec((D, D), lambda t, k: (0, t)), out_shape=(D, nt * D), out_dtype=BF16, acc_shape=(D, D))
    shard_cols = lambda a: jnp.moveaxis(a.reshape(a.shape[0], N_CHIPS, a.shape[1] // N_CHIPS), 1, 0)
    ex.grads("od", {"od_w_qkv": shard_cols(d_w_qkv), "od_w_out": d_w_od_out.reshape(N_CHIPS, D // N_CHIPS, D)})
    dh2, dg_mix1 = _rms_bwd("rms_mix1_bwd", du2, h2, g_mix[1:2], dh3, carry=ex.carry("rms_mix1_bwd"))
    d_rel = _bias_table_grad("rel_bias_grad", dr0.reshape(C_H, TOEP_W))[:, :2 * REL_CLIP + 1]

    dh1, dg_ffn0, u1, dgate, dup, act = _ffn_bwd("ffn0_bwd", dh2, h1, g_ffn[0:1], gate0, up0,
                                                 w["w_gate0"], w["w_up0"], w["w_down0"])
    d_wg0, d_wu0, d_wd0 = _ffn_wgrads("ffn0_dw", u1, dgate, dup, act, dh2)
    ex.grads("ffn0", {"w_gate0": d_wg0, "w_up0": d_wu0, "w_down0": d_wd0})

    d_oev, d_w_ev_out = _out_proj_bwd("ev_out_bwd", dh1, o_ev, w["ev_w_out"], ex)
    ex.grads("ev_out", {"ev_w_out": d_w_ev_out.reshape(N_CHIPS, D // N_CHIPS, D)})
    dqa, dka, dva = riding("mla_attn_bwd", _mla_bwd, qa, ka, va, o_a, lse, d_oev, 0)
    dqb, dkb, dvb = riding("sb_attn_bwd", _sb_bwd, proj, d_oev, MLA_H * MLA_V // LANES)
    dcq, dckv, dkr, d_w_uq, d_w_uk, d_w_uv, dg_cq, dg_ckv = _mla_prep_bwd(
        "mla_prep_bwd", dqa, dka, dva, proj, sm["ev_g_cq"], sm["ev_g_ckv"], w["w_uq"], w["w_uk"], w["w_uv"], cos_t, sin_t)
    dproj = jnp.concatenate([dcq, jnp.zeros((S, LANES), BF16), dckv, dqb, dkb, dvb, dkr], axis=1)
    d_w_in_p = _mm("proj_in_bwd_w", u0, dproj, kind="tn", grid=(1, S // TM),
                   a_spec=pl.BlockSpec((TM, D), lambda j, k: (k, 0)), b_spec=pl.BlockSpec((TM, P_IN), lambda j, k: (k, 0)),
                   o_spec=pl.BlockSpec((D, P_IN), lambda j, k: (0, 0)), out_shape=(D, P_IN), out_dtype=BF16,
                   acc_shape=(D, P_IN), carry=ex.carry("proj_in_bwd_w"))
    d_w_uq_std = d_w_uq.reshape(Q_LORA, MLA_H, LANES)[:, :, :MLA_NOPE + MLA_ROPE].reshape(Q_LORA, -1)
    d_w_ukv = jnp.concatenate([d_w_uk.reshape(KV_LORA, MLA_H, LANES)[:, :, :MLA_NOPE],
                               d_w_uv.reshape(KV_LORA, MLA_H, MLA_V)], axis=2).reshape(KV_LORA, -1)
    ex.grads("ev", {"ev_w_in": _w_in_sharded("w_in_sharded", d_w_in_p), "ev_w_uq": shard_cols(d_w_uq_std.astype(BF16)),
                    "ev_w_ukv": shard_cols(d_w_ukv.astype(BF16))})
    du0 = _mm("proj_in_bwd_x", dproj, w["w_in"], kind="nt", grid=(S // TM, 1, 1),
              a_spec=pl.BlockSpec((TM, P_IN), lambda i, j, k: (i, 0)), b_spec=pl.BlockSpec((D, P_IN), lambda i, j, k: (0, 0)),
              o_spec=pl.BlockSpec((TM, D), lambda i, j, k: (i, 0)), out_shape=(S, D), out_dtype=F32, acc_shape=None,
              carry=ex.carry("proj_in_bwd_x"))
    grad_x, dg_mix0 = _rms_bwd("rms_mix0_bwd", du0, x, g_mix[0:1], dh1)
    small = {
        "ev_g_cq": dg_cq, "ev_g_ckv": dg_ckv, "od_rel_bias": d_rel.reshape(1, C_H, 2 * REL_CLIP + 1),
        "g_mix": jnp.concatenate([dg_mix0, dg_mix1], axis=0), "g_ffn": jnp.concatenate([dg_ffn0, dg_ffn1], axis=0),
        "g_final": dg_final.reshape(D),
    }
    return loss, grad_x, small


BIG = ("ev_w_in", "ev_w_uq", "ev_w_ukv", "ev_w_out", "od_w_qkv", "od_w_out", "w_gate", "w_up", "w_down")
SMALL = ("ev_g_cq", "ev_g_ckv", "od_rel_bias", "g_mix", "g_ffn", "g_final")
WEIGHTS = ("ev_w_in", "ev_g_cq", "ev_w_uq", "ev_g_ckv", "ev_w_ukv", "ev_w_out", "od_w_qkv", "od_rel_bias", "od_w_out",
           "g_mix", "g_ffn", "w_gate", "w_up", "w_down", "g_final")
GRAD_PARTS = (("ev_w_in", "ev_w_in", 0), ("ev_w_uq", "ev_w_uq", 0), ("ev_w_ukv", "ev_w_ukv", 0),
              ("ev_w_out", "ev_w_out", 0), ("od_w_qkv", "od_w_qkv", 0), ("od_w_out", "od_w_out", 0),
              ("w_gate0", "w_gate", 0), ("w_gate1", "w_gate", 1), ("w_up0", "w_up", 0), ("w_up1", "w_up", 1),
              ("w_down0", "w_down", 0), ("w_down1", "w_down", 1))
PART_OF = {part: (param, layer) for part, param, layer in GRAD_PARTS}
SMALL_ROWS = 112
SMALL_SIZE = 384 + 256 + 16 * 513 + 2 * 1024 + 2 * 1024 + 1024
TRANSPOSED = ("w_gate", "w_up")


def _row_tile(rows, cap=512):
    for t in range(min(rows, cap), 0, -1):
        if rows % t == 0 and t % 16 == 0:
            return t
    return rows


def _cast_into_slot(name, w, layer, pos):
    _, rows, cols = w.shape
    tr = _row_tile(rows)

    def body(pos_ref, w_ref, o_ref):
        o_ref[...] = w_ref[...].astype(BF16)

    return pl.pallas_call(
        body, name=name,
        grid_spec=pltpu.PrefetchScalarGridSpec(
            num_scalar_prefetch=1, grid=(rows // tr,),
            in_specs=[pl.BlockSpec((None, tr, cols), lambda i, p: (layer, i, 0))],
            out_specs=pl.BlockSpec((None, tr, cols), lambda i, p: (p[0], i, 0))),
        out_shape=jax.ShapeDtypeStruct((N_CHIPS, rows, cols), BF16), compiler_params=_params("arbitrary"))(pos, w)


def _cast_many_into_slots(name, items, pos, carry):
    tiles = [_row_tile(w.shape[1]) for w, _ in items]
    counts = [w.shape[1] // tr for (w, _), tr in zip(items, tiles)]
    starts = [sum(counts[:t]) for t in range(len(items))]

    def body(pos_ref, *refs):
        i = pl.program_id(0)
        for t, (start, nb) in enumerate(zip(starts, counts)):
            @pl.when((i >= start) & (i < start + nb))
            def _(w_ref=refs[t], o_ref=refs[len(items) + t]):
                o_ref[...] = w_ref[...].astype(BF16)

    def block(start, nb):
        return lambda i: jnp.clip(i - start, 0, nb - 1)

    in_specs, out_specs, out_shape = [], [], []
    for (w, layer), tr, start, nb in zip(items, tiles, starts, counts):
        _, rows, cols = w.shape
        at = block(start, nb)
        in_specs.append(pl.BlockSpec((None, tr, cols), lambda i, p, at=at, layer=layer: (layer, at(i), 0)))
        out_specs.append(pl.BlockSpec((None, tr, cols), lambda i, p, at=at: (p[0], at(i), 0)))
        out_shape.append(jax.ShapeDtypeStruct((N_CHIPS, rows, cols), BF16))
    res, copies = _carrier_call(body, name=name, grid=(sum(counts),), in_specs=in_specs, out_specs=out_specs,
                                out_shape=out_shape, args=[w for w, _ in items], sem=("arbitrary",), carry=carry,
                                prefetch=(pos,))
    if carry is not None:
        carry.done(copies)
    return res


def _pair_sum(name, part, theirs, pos):
    _, half, cols = theirs.shape
    tr = _row_tile(half)
    nb = half // tr

    def body(pos_ref, a_ref, b_ref, o_ref):
        o_ref[...] = (a_ref[...].astype(F32) + b_ref[...].astype(F32)).astype(BF16)

    return pl.pallas_call(
        body, name=name,
        grid_spec=pltpu.PrefetchScalarGridSpec(
            num_scalar_prefetch=1, grid=(N_CHIPS, nb),
            in_specs=[pl.BlockSpec((None, tr, cols), lambda s, i, p: (s, p[1] * nb + i, 0)),
                      pl.BlockSpec((None, tr, cols), lambda s, i, p: (s, i, 0))],
            out_specs=pl.BlockSpec((None, tr, cols), lambda s, i, p: (s, i, 0))),
        out_shape=jax.ShapeDtypeStruct(theirs.shape, BF16),
        compiler_params=_params("arbitrary", "arbitrary"))(pos, part, theirs)


def _chip_sum(name, sums, got, pos, layer, full_shape, full=None):
    _, half, cols = sums.shape
    tr = _row_tile(half)
    nb = half // tr

    def body(pos_ref, s_ref, g_ref, *rest):
        out_ref = rest[-1]
        out_ref[...] = ((s_ref[...].astype(F32) + g_ref[0].astype(F32)) + g_ref[1].astype(F32)) + g_ref[2].astype(F32)

    in_specs = [pl.BlockSpec((None, tr, cols), lambda i, p: (p[0], i, 0)),
                pl.BlockSpec((3, tr, cols), lambda i, p: (0, i, 0))]
    args = [pos, sums, got]
    if full is not None:
        in_specs.append(ANY)
        args.append(full)
    return pl.pallas_call(
        body, name=name,
        grid_spec=pltpu.PrefetchScalarGridSpec(
            num_scalar_prefetch=1, grid=(nb,), in_specs=in_specs,
            out_specs=pl.BlockSpec((None, tr, cols), lambda i, p: (layer, p[1] * nb + i, 0))),
        out_shape=jax.ShapeDtypeStruct(full_shape, F32),
        input_output_aliases={3: 0} if full is not None else {},
        compiler_params=_params("arbitrary"))(*args)


def _all_reduce_small(name, packed):
    n_dev = 8

    def body(p_ref, o_ref, slots, send_sem, recv_sem):
        x, y, c, _ = _position()
        me = 4 * x + 2 * y + c

        def peer(k):
            return (1 - x if k & 4 else x, 1 - y if k & 2 else y, 1 - c if k & 1 else c)

        def logical(k):
            px, py, pc = peer(k)
            return 4 * px + 2 * py + pc

        slots[me] = p_ref[...]
        sends = [pltpu.make_async_remote_copy(
            src_ref=p_ref, dst_ref=slots.at[me], send_sem=send_sem.at[k], recv_sem=recv_sem.at[k],
            device_id=peer(k), device_id_type=MESH) for k in range(1, n_dev)]
        for cp in sends:
            cp.start()
        for k in range(1, n_dev):
            pltpu.make_async_remote_copy(
                src_ref=p_ref, dst_ref=slots.at[logical(k)], send_sem=send_sem.at[k], recv_sem=recv_sem.at[k],
                device_id=peer(k), device_id_type=MESH).wait_recv()
        for cp in sends:
            cp.wait_send()
        total = slots[0]
        for d in range(1, n_dev):
            total = total + slots[d]
        o_ref[...] = total

    vm = pl.BlockSpec(memory_space=pltpu.VMEM)
    return pl.pallas_call(
        body, name=name, in_specs=[vm], out_specs=vm, out_shape=jax.ShapeDtypeStruct(packed.shape, F32),
        scratch_shapes=[pltpu.VMEM((n_dev,) + packed.shape, F32), pltpu.SemaphoreType.DMA((n_dev,)),
                        pltpu.SemaphoreType.DMA((n_dev,))],
    )(packed)


def _adamw(name, w, g, m, v):
    rows, cols = w.shape
    tr = _row_tile(rows)

    def body(w_ref, g_ref, m_ref, v_ref, d_ref, mo_ref, vo_ref):
        gv = g_ref[...]
        m_new = ADAM_B1 * m_ref[...] + (1.0 - ADAM_B1) * gv
        v_new = ADAM_B2 * v_ref[...] + (1.0 - ADAM_B2) * (gv * gv)
        m_hat = m_new / (1.0 - ADAM_B1 ** ADAM_STEP)
        v_hat = v_new / (1.0 - ADAM_B2 ** ADAM_STEP)
        d_ref[...] = -ADAM_LR * (m_hat / (jnp.sqrt(v_hat) + ADAM_EPS) + ADAM_WD * w_ref[...])
        mo_ref[...] = m_new
        vo_ref[...] = v_new

    spec = pl.BlockSpec((tr, cols), lambda i: (i, 0))
    shape = jax.ShapeDtypeStruct((rows, cols), F32)
    return pl.pallas_call(body, name=name, grid=(rows // tr,), in_specs=[spec] * 4, out_specs=[spec] * 3,
                          out_shape=[shape] * 3, compiler_params=_params("parallel"))(w, g, m, v)


def _pack_small(tree, extra=None):
    pieces = [tree[n].reshape(-1).astype(F32) for n in SMALL]
    if extra is not None:
        pieces.append(extra.reshape(1).astype(F32))
    flat = jnp.concatenate(pieces)
    return jnp.pad(flat, (0, SMALL_ROWS * LANES - flat.shape[0])).reshape(SMALL_ROWS, LANES)


def _unpack_small(packed, like):
    flat = packed.reshape(-1)
    out, off = {}, 0
    for n in SMALL:
        size = int(np.prod(like[n].shape))
        out[n] = flat[off:off + size].reshape(like[n].shape)
        off += size
    return out


def kernel(x, ev_w_in, ev_g_cq, ev_w_uq, ev_g_ckv, ev_w_ukv, ev_w_out, od_w_qkv, od_rel_bias, od_w_out, g_mix, g_ffn, w_gate, w_up, w_down, g_final, loss_target, m_ev_w_in, m_ev_g_cq, m_ev_w_uq, m_ev_g_ckv, m_ev_w_ukv, m_ev_w_out, m_od_w_qkv, m_od_rel_bias, m_od_w_out, m_g_mix, m_g_ffn, m_w_gate, m_w_up, m_w_down, m_g_final, v_ev_w_in, v_ev_g_cq, v_ev_w_uq, v_ev_g_ckv, v_ev_w_ukv, v_ev_w_out, v_od_w_qkv, v_od_rel_bias, v_od_w_out, v_g_mix, v_g_ffn, v_w_gate, v_w_up, v_w_down, v_g_final):
    w = dict(ev_w_in=ev_w_in, ev_g_cq=ev_g_cq, ev_w_uq=ev_w_uq, ev_g_ckv=ev_g_ckv, ev_w_ukv=ev_w_ukv, ev_w_out=ev_w_out,
             od_w_qkv=od_w_qkv, od_rel_bias=od_rel_bias, od_w_out=od_w_out, g_mix=g_mix, g_ffn=g_ffn, w_gate=w_gate,
             w_up=w_up, w_down=w_down, g_final=g_final)
    m = dict(ev_w_in=m_ev_w_in, ev_g_cq=m_ev_g_cq, ev_w_uq=m_ev_w_uq, ev_g_ckv=m_ev_g_ckv, ev_w_ukv=m_ev_w_ukv,
             ev_w_out=m_ev_w_out, od_w_qkv=m_od_w_qkv, od_rel_bias=m_od_rel_bias, od_w_out=m_od_w_out, g_mix=m_g_mix,
             g_ffn=m_g_ffn, w_gate=m_w_gate, w_up=m_w_up, w_down=m_w_down, g_final=m_g_final)
    v = dict(ev_w_in=v_ev_w_in, ev_g_cq=v_ev_g_cq, ev_w_uq=v_ev_w_uq, ev_g_ckv=v_ev_g_ckv, ev_w_ukv=v_ev_w_ukv,
             ev_w_out=v_ev_w_out, od_w_qkv=v_od_w_qkv, od_rel_bias=v_od_rel_bias, od_w_out=v_od_w_out, g_mix=v_g_mix,
             g_ffn=v_g_ffn, w_gate=v_w_gate, w_up=v_w_up, w_down=v_w_down, g_final=v_g_final)
    flat2d = lambda a: a.reshape(-1, a.shape[-1])
    for tree in (w, m, v):
        for n in TRANSPOSED:
            tree[n] = jnp.swapaxes(tree[n], 1, 2)

    pos = jnp.stack([2 * lax.axis_index("x") + lax.axis_index("y"), lax.axis_index("c")]).astype(jnp.int32)

    slots = {part: _cast_into_slot("cast_" + part, w[n], layer, pos) for part, n, layer in GRAD_PARTS
             if part in FIRST_WEIGHTS}
    rest = [(part, n, layer) for part, n, layer in GRAD_PARTS if part not in FIRST_WEIGHTS]

    def cast_rest(carry):
        return dict(zip([part for part, _, _ in rest],
                        _cast_many_into_slots("cast_rest", [(w[n], layer) for _, n, layer in rest], pos, carry)))

    ex = _Exchanges(slots, pos, {n: w[n].shape for n in BIG}, cast_rest)

    loss_local, grad_x, small = _local_step(x[0], loss_target[0], {n: w[n] for n in SMALL}, ex)

    grads = ex.finish()
    small_sum = _all_reduce_small("small_sum", _pack_small(small, loss_local[0, 0]))
    grads.update(_unpack_small(small_sum, w))

    delta, new_m, new_v = {}, {}, {}
    for n in BIG:
        d_, m_, v_ = _adamw("adamw_" + n, flat2d(w[n]), flat2d(grads[n]), flat2d(m[n]), flat2d(v[n]))
        delta[n], new_m[n], new_v[n] = d_.reshape(w[n].shape), m_.reshape(w[n].shape), v_.reshape(w[n].shape)
    d_, m_, v_ = _adamw("adamw_small", _pack_small(w), small_sum, _pack_small(m), _pack_small(v))
    delta.update(_unpack_small(d_, w))
    new_m.update(_unpack_small(m_, w))
    new_v.update(_unpack_small(v_, w))
    for tree in (grads, delta, new_m, new_v):
        for n in TRANSPOSED:
            tree[n] = jnp.swapaxes(tree[n], 1, 2)

    loss = small_sum.reshape(-1)[SMALL_SIZE]
    return (loss, grad_x[None], *[grads[n] for n in WEIGHTS], *[delta[n] for n in WEIGHTS],
            *[new_m[n] for n in WEIGHTS], *[new_v[n] for n in WEIGHTS])
```

```python
import functools

import jax
import jax.numpy as jnp
import numpy as np
from jax import lax
from jax.experimental import pallas as pl
from jax.experimental.pallas import tpu as pltpu

F32 = jnp.float32
BF16 = jnp.bfloat16

S = 2048
D = 1024
CHUNK = 64
MLA_H, MLA_NOPE, MLA_ROPE, MLA_V = 8, 64, 32, 64
Q_LORA, KV_LORA = 384, 256
ROPE_THETA = 10000.0
SB_H, SB_DIM = 8, 64
C_H, C_DIM = 16, 64
LEFT_CHUNKS = 8
REL_CLIP = 256
D_FF = 2816
EVEN_IN = 2208
RMS_EPS = 1e-6
ADAM_LR, ADAM_B1, ADAM_B2, ADAM_EPS, ADAM_WD, ADAM_STEP = 0.001, 0.9, 0.999, 1e-08, 0.01, 10

N_CHIPS = 4
FF_SHARD = D_FF // N_CHIPS
SCALE_A = (MLA_NOPE + MLA_ROPE) ** -0.5
SCALE_B = SB_DIM ** -0.5
SCALE_C = C_DIM ** -0.5
NEG = -1e30
LOG2_E = 1.4426950408889634

LANES = 128
MXU_W = 256
VMEM_LIMIT_BYTES = 56 * 1024 * 1024
TM = 512
QB = 512
BQ = 256

P_CQ, P_CKV, P_QB, P_KB, P_VB, P_KR = 0, 512, 768, 1280, 1792, 2304
P_IN = 2432
KR_LANE = 64
BAND_W = BQ + LEFT_CHUNKS * CHUNK
BAND_PAD = 512
TOEP_W = 1024


def _params(*sem):
    return pltpu.CompilerParams(dimension_semantics=sem, vmem_limit_bytes=VMEM_LIMIT_BYTES)


MESH = pl.DeviceIdType.MESH
ANY = pl.BlockSpec(memory_space=pl.ANY)


def _position():
    x, y, c = lax.axis_index("x"), lax.axis_index("y"), lax.axis_index("c")
    other_chips = [(1 - x, y), (x, 1 - y), (1 - x, 1 - y)]
    return x, y, c, other_chips


def _half_rows(c, half):
    return pl.ds(pl.multiple_of(c * half, 16), half)


def _remote(ref_src, ref_dst, send, recv, k, device):
    return pltpu.make_async_remote_copy(src_ref=ref_src, dst_ref=ref_dst, send_sem=send.at[k], recv_sem=recv.at[k],
                                        device_id=device, device_id_type=MESH)


class _Carry:
    def __init__(self):
        self.operands, self.aliased, self.fresh = [], [], []
        self.n_sems = 0
        self.starts, self.finishes, self.on_done = [], [], []

    def operand(self, arr, aliased):
        for i, a in enumerate(self.operands):
            if a is arr:
                return i
        self.operands.append(arr)
        self.aliased.append(aliased)
        return len(self.operands) - 1

    def result(self, shape, dtype):
        self.fresh.append(jax.ShapeDtypeStruct(shape, dtype))
        return len(self.fresh) - 1

    def sems(self, k):
        base = self.n_sems
        self.n_sems += k
        return base

    def done(self, results):
        aliased, fresh = results
        for f in self.on_done:
            f(aliased, fresh)


def _carrier_call(body, *, name, grid, in_specs, out_specs, out_shape, args, sem, scratch_shapes=(), carry=None,
                  prefetch=()):
    in_specs, out_specs, out_shape, scratch = list(in_specs), list(out_specs), list(out_shape), list(scratch_shapes)
    n_pre = len(prefetch)

    def call(kernel, in_specs, out_specs, out_shape, scratch, aliases, sem):
        return pl.pallas_call(
            kernel, name=name, out_shape=out_shape, input_output_aliases=aliases, compiler_params=_params(*sem),
            grid_spec=pltpu.PrefetchScalarGridSpec(num_scalar_prefetch=n_pre, grid=grid, in_specs=in_specs,
                                                   out_specs=out_specs, scratch_shapes=scratch))

    if carry is None:
        return list(call(body, in_specs, out_specs, out_shape, scratch, {}, sem)(*prefetch, *args)), None
    ops = carry.operands
    alias_idx = [i for i, a in enumerate(carry.aliased) if a]
    c_shapes = [jax.ShapeDtypeStruct(ops[i].shape, ops[i].dtype) for i in alias_idx] + carry.fresh
    n_in, n_out, n_scr = len(args), len(out_shape), len(scratch)

    def wrapped(*refs):
        pre, refs = refs[:n_pre], refs[n_pre:]
        ins, c_ins = refs[:n_in], refs[n_in:n_in + len(ops)]
        o0 = n_in + len(ops)
        outs, c_outs = refs[o0:o0 + n_out], refs[o0 + n_out:o0 + n_out + len(c_shapes)]
        s0 = o0 + n_out + len(c_shapes)
        scr, send, recv = refs[s0:s0 + n_scr], refs[s0 + n_scr], refs[s0 + n_scr + 1]
        use = list(c_ins)
        for k, i in enumerate(alias_idx):
            use[i] = c_outs[k]
        fresh = c_outs[len(alias_idx):]

        def run(steps):
            for step in steps:
                step(use, fresh, send, recv)

        if not grid:
            run(carry.starts)
            if body is not None:
                body(*pre, *ins, *outs, *scr)
            run(carry.finishes)
            return
        ids = [pl.program_id(a) for a in range(len(grid))]
        first = functools.reduce(jnp.logical_and, [i == 0 for i in ids])
        last = functools.reduce(jnp.logical_and, [i == g - 1 for i, g in zip(ids, grid)])

        @pl.when(first)
        def _():
            run(carry.starts)

        body(*pre, *ins, *outs, *scr)

        @pl.when(last)
        def _():
            run(carry.finishes)

    res = call(wrapped, in_specs + [ANY] * len(ops), out_specs + [ANY] * len(c_shapes), out_shape + c_shapes,
               scratch + [pltpu.SemaphoreType.DMA((carry.n_sems,)), pltpu.SemaphoreType.DMA((carry.n_sems,))],
               {n_pre + n_in + i: n_out + k for k, i in enumerate(alias_idx)},
               ("arbitrary",) * len(grid))(*prefetch, *args, *ops)
    res = list(res)
    c_res = res[n_out:]
    return res[:n_out], ({i: c_res[k] for k, i in enumerate(alias_idx)}, c_res[len(alias_idx):])


_DIMS = {"nn": (((1,), (0,)), ((), ())), "nt": (((1,), (1,)), ((), ())), "tn": (((0,), (0,)), ((), ()))}


def _dot(a, b, kind="nn"):
    return lax.dot_general(a, b, _DIMS[kind], preferred_element_type=F32)


def _iota(shape, dim):
    return lax.broadcasted_iota(jnp.int32, shape, dim)


def _sigmoid(x):
    return 1.0 / (1.0 + jnp.exp(-x))


def _split_dot(x, tri):
    hi = x.astype(BF16)
    lo = (x - hi.astype(F32)).astype(BF16)
    both = _dot(jnp.concatenate([hi, lo], axis=0), tri)
    return both[:x.shape[0]] + both[x.shape[0]:]


def _running_sum(x, tri, reverse):
    n = x.shape[1] // MXU_W
    blocks = [x[:, b * MXU_W:(b + 1) * MXU_W] for b in range(n)]
    out = [None] * n
    carry = None
    for b in (range(n - 1, -1, -1) if reverse else range(n)):
        part = _split_dot(blocks[b], tri)
        out[b] = part if carry is None else part + carry
        total = jnp.sum(blocks[b], axis=-1, keepdims=True)
        carry = total if carry is None else carry + total
    return (jnp.concatenate(out, axis=1) if n > 1 else out[0]), carry


def _mm(name, a, b, *, kind, grid, a_spec, b_spec, o_spec, out_shape, out_dtype, acc_shape, resid=None, r_spec=None,
        carry=None):
    nk = grid[-1]
    has_r = resid is not None
    b_specs = list(b_spec) if isinstance(b_spec, (tuple, list)) else [b_spec]
    nb = len(b_specs)

    def body(*refs):
        a_ref, b_refs = refs[0], refs[1:1 + nb]
        r_ref = refs[1 + nb] if has_r else None
        o_ref = refs[1 + nb + has_r]
        b_val = b_refs[0][...] if nb == 1 else jnp.concatenate([r[...] for r in b_refs], axis=1)
        part = _dot(a_ref[...].astype(BF16), b_val.astype(BF16), kind)

        def finish(total):
            if has_r:
                total = total + r_ref[...].astype(F32)
            o_ref[...] = total.astype(out_dtype)

        if nk == 1:
            finish(part)
        else:
            acc_ref = refs[2 + nb + has_r]
            k = pl.program_id(len(grid) - 1)

            @pl.when(k == 0)
            def _():
                acc_ref[...] = part

            @pl.when(k > 0)
            def _():
                acc_ref[...] += part

            @pl.when(k == nk - 1)
            def _():
                finish(acc_ref[...])

    in_specs = [a_spec] + b_specs + ([r_spec] if has_r else [])
    args = (a,) + (b,) * nb + ((resid,) if has_r else ())
    sem = ("parallel",) * (len(grid) - 1) + ("arbitrary",)
    res, copies = _carrier_call(
        body, name=name, grid=grid, in_specs=in_specs, out_specs=[o_spec],
        out_shape=[jax.ShapeDtypeStruct(out_shape, out_dtype)],
        scratch_shapes=[pltpu.VMEM(acc_shape, F32)] if nk > 1 else [], args=args, sem=sem, carry=carry)
    if carry is not None:
        carry.done(copies)
    return res[0]


def _rms_fwd(name, x, g, col_block=0):
    c = g.shape[1]

    def body(x_ref, g_ref, u_ref):
        xv = x_ref[...]
        r = lax.rsqrt(jnp.mean(xv * xv, axis=-1, keepdims=True) + RMS_EPS)
        u_ref[...] = (xv * r * g_ref[...]).astype(BF16)

    return pl.pallas_call(
        body, name=name, grid=(S // TM,),
        in_specs=[pl.BlockSpec((TM, c), lambda i: (i, col_block)), pl.BlockSpec((1, c), lambda i: (0, 0))],
        out_specs=pl.BlockSpec((TM, c), lambda i: (i, 0)),
        out_shape=jax.ShapeDtypeStruct((S, c), BF16),
        compiler_params=_params("parallel"),
    )(x, g)


def _rms_bwd(name, dy, x, g, resid, carry=None):
    def body(dy_ref, x_ref, g_ref, r_ref, dx_ref, dg_ref):
        i = pl.program_id(0)
        xv = x_ref[...]
        r = lax.rsqrt(jnp.mean(xv * xv, axis=-1, keepdims=True) + RMS_EPS)
        xh = xv * r
        dyv = dy_ref[...]
        dxh = dyv * g_ref[...]
        dx_ref[...] = r_ref[...] + r * (dxh - xh * jnp.mean(dxh * xh, axis=-1, keepdims=True))
        part = jnp.sum(dyv * xh, axis=0, keepdims=True)

        @pl.when(i == 0)
        def _():
            dg_ref[...] = part

        @pl.when(i > 0)
        def _():
            dg_ref[...] += part

    row = pl.BlockSpec((TM, D), lambda i: (i, 0))
    vec = pl.BlockSpec((1, D), lambda i: (0, 0))
    res, copies = _carrier_call(
        body, name=name, grid=(S // TM,), in_specs=[row, row, vec, row], out_specs=[row, vec],
        out_shape=[jax.ShapeDtypeStruct((S, D), F32), jax.ShapeDtypeStruct((1, D), F32)],
        args=(dy, x, g, resid), sem=("arbitrary",), carry=carry)
    if carry is not None:
        carry.done(copies)
    return res


def _loss_bwd(name, h, g, tgt):
    def body(h_ref, g_ref, t_ref, loss_ref, dh_ref, dg_ref):
        i = pl.program_id(0)
        xv = h_ref[...]
        gv = g_ref[...]
        r = lax.rsqrt(jnp.mean(xv * xv, axis=-1, keepdims=True) + RMS_EPS)
        xh = xv * r
        diff = xh * gv - t_ref[...]
        part_loss = 0.5 * jnp.sum(jnp.sum(diff * diff, axis=-1, keepdims=True) * (1.0 / D), axis=0, keepdims=True)
        dy = diff * (1.0 / D)
        dxh = dy * gv
        dh_ref[...] = r * (dxh - xh * jnp.mean(dxh * xh, axis=-1, keepdims=True))
        part_g = jnp.sum(dy * xh, axis=0, keepdims=True)

        @pl.when(i == 0)
        def _():
            dg_ref[...] = part_g
            loss_ref[...] = jnp.broadcast_to(part_loss, (1, LANES))

        @pl.when(i > 0)
        def _():
            dg_ref[...] += part_g
            loss_ref[...] += jnp.broadcast_to(part_loss, (1, LANES))

    row = pl.BlockSpec((TM, D), lambda i: (i, 0))
    vec = pl.BlockSpec((1, D), lambda i: (0, 0))
    return pl.pallas_call(
        body, name=name, grid=(S // TM,), in_specs=[row, vec, row],
        out_specs=[pl.BlockSpec((1, LANES), lambda i: (0, 0)), row, vec],
        out_shape=[jax.ShapeDtypeStruct((1, LANES), F32), jax.ShapeDtypeStruct((S, D), F32),
                   jax.ShapeDtypeStruct((1, D), F32)],
        compiler_params=_params("arbitrary"),
    )(h, g, tgt)


def _ffn_fwd(name, h, g, wg, wu, wd, carry=None):
    def body(h_ref, g_ref, wg_ref, wu_ref, wd_ref, o_ref, gate_ref, up_ref, u_scr):
        s = pl.program_id(1)

        @pl.when(s == 0)
        def _():
            xv = h_ref[...]
            r = lax.rsqrt(jnp.mean(xv * xv, axis=-1, keepdims=True) + RMS_EPS)
            u_scr[...] = (xv * r * g_ref[...]).astype(BF16)
            o_ref[...] = xv

        u = u_scr[...]
        gate = _dot(u, wg_ref[...], "nt")
        up = _dot(u, wu_ref[...], "nt")
        act = gate * _sigmoid(gate) * up
        o_ref[...] += _dot(act.astype(BF16), wd_ref[...])
        gate_ref[...] = gate.astype(BF16)
        up_ref[...] = up.astype(BF16)

    row = pl.BlockSpec((TM, D), lambda i, s: (i, 0))
    hid = pl.BlockSpec((None, TM, FF_SHARD), lambda i, s: (s, i, 0))
    return _carrier_call(
        body, name=name, grid=(S // TM, N_CHIPS),
        in_specs=[row, pl.BlockSpec((1, D), lambda i, s: (0, 0))]
        + [pl.BlockSpec((None, FF_SHARD, D), lambda i, s: (s, 0, 0))] * 3,
        out_specs=[row, hid, hid],
        out_shape=[jax.ShapeDtypeStruct((S, D), F32), jax.ShapeDtypeStruct((N_CHIPS, S, FF_SHARD), BF16),
                   jax.ShapeDtypeStruct((N_CHIPS, S, FF_SHARD), BF16)],
        scratch_shapes=[pltpu.VMEM((TM, D), BF16)], args=(h, g, wg, wu, wd), sem=("parallel", "arbitrary"), carry=carry)


def _ffn_bwd(name, dh, h, g, gate, up, wg, wu, wd):
    def body(dh_ref, h_ref, g_ref, gate_ref, up_ref, wg_ref, wu_ref, wd_ref,
             dhin_ref, dg_ref, u_ref, dgate_ref, dup_ref, act_ref, dhb_scr, du_scr):
        i = pl.program_id(0)
        s = pl.program_id(1)

        @pl.when(s == 0)
        def _():
            xv = h_ref[...]
            r = lax.rsqrt(jnp.mean(xv * xv, axis=-1, keepdims=True) + RMS_EPS)
            u_ref[...] = (xv * r * g_ref[...]).astype(BF16)
            dhb_scr[...] = dh_ref[...].astype(BF16)
            du_scr[...] = jnp.zeros_like(du_scr)

        dact = _dot(dhb_scr[...], wd_ref[...], "nt")
        gv = gate_ref[...].astype(F32)
        uv = up_ref[...].astype(F32)
        sig = _sigmoid(gv)
        sil = gv * sig
        dup = dact * sil
        dgate = dact * uv * (sig * (1.0 + gv * (1.0 - sig)))
        dgb = dgate.astype(BF16)
        dub = dup.astype(BF16)
        act_ref[...] = (sil * uv).astype(BF16)
        dgate_ref[...] = dgb
        dup_ref[...] = dub
        du_scr[...] += _dot(dgb, wg_ref[...]) + _dot(dub, wu_ref[...])

        @pl.when(s == N_CHIPS - 1)
        def _():
            xv = h_ref[...]
            r = lax.rsqrt(jnp.mean(xv * xv, axis=-1, keepdims=True) + RMS_EPS)
            xh = xv * r
            du = du_scr[...]
            dxh = du * g_ref[...]
            dhin_ref[...] = dh_ref[...] + r * (dxh - xh * jnp.mean(dxh * xh, axis=-1, keepdims=True))
            part = jnp.sum(du * xh, axis=0, keepdims=True)

            @pl.when(i == 0)
            def _():
                dg_ref[...] = part

            @pl.when(i > 0)
            def _():
                dg_ref[...] += part

    row = pl.BlockSpec((TM, D), lambda i, s: (i, 0))
    vec = pl.BlockSpec((1, D), lambda i, s: (0, 0))
    hid = pl.BlockSpec((None, TM, FF_SHARD), lambda i, s: (s, i, 0))
    hid_shape = jax.ShapeDtypeStruct((N_CHIPS, S, FF_SHARD), BF16)
    return pl.pallas_call(
        body, name=name, grid=(S // TM, N_CHIPS),
        in_specs=[row, row, vec, hid, hid] + [pl.BlockSpec((None, FF_SHARD, D), lambda i, s: (s, 0, 0))] * 3,
        out_specs=[row, vec, row, hid, hid, hid],
        out_shape=[jax.ShapeDtypeStruct((S, D), F32), jax.ShapeDtypeStruct((1, D), F32),
                   jax.ShapeDtypeStruct((S, D), BF16), hid_shape, hid_shape, hid_shape],
        scratch_shapes=[pltpu.VMEM((TM, D), BF16), pltpu.VMEM((TM, D), F32)],
        compiler_params=_params("arbitrary", "arbitrary"),
    )(dh, h, g, gate, up, wg, wu, wd)


def _ffn_wgrads(name, u, dgate, dup, act, dh):
    nk = S // TM

    def body(u_ref, dh_ref, dgate_ref, dup_ref, act_ref, dg_ref, du_ref, dd_ref, acc_g, acc_u, acc_d):
        k = pl.program_id(1)
        u = u_ref[...]
        parts = (_dot(dgate_ref[...], u, "tn"), _dot(dup_ref[...], u, "tn"),
                 _dot(act_ref[...], dh_ref[...].astype(BF16), "tn"))
        accs = (acc_g, acc_u, acc_d)

        @pl.when(k == 0)
        def _():
            for acc, part in zip(accs, parts):
                acc[...] = part

        @pl.when(k > 0)
        def _():
            for acc, part in zip(accs, parts):
                acc[...] += part

        @pl.when(k == nk - 1)
        def _():
            for out, acc in zip((dg_ref, du_ref, dd_ref), accs):
                out[...] = acc[...].astype(BF16)

    tok = pl.BlockSpec((TM, D), lambda s, k: (k, 0))
    hid = pl.BlockSpec((None, TM, FF_SHARD), lambda s, k: (s, k, 0))
    out = pl.BlockSpec((None, FF_SHARD, D), lambda s, k: (s, 0, 0))
    shape = jax.ShapeDtypeStruct((N_CHIPS, FF_SHARD, D), BF16)
    return pl.pallas_call(
        body, name=name, grid=(N_CHIPS, nk), in_specs=[tok, tok, hid, hid, hid], out_specs=[out, out, out],
        out_shape=[shape, shape, shape], scratch_shapes=[pltpu.VMEM((FF_SHARD, D), F32)] * 3,
        compiler_params=_params("parallel", "arbitrary"))(u, dh, dgate, dup, act)


def _rope_tables():
    pos = jnp.arange(S, dtype=F32)
    inv = ROPE_THETA ** (-jnp.arange(0, MLA_ROPE, 2, dtype=F32) / MLA_ROPE)
    ang = pos[:, None] * inv[None, :]
    half = MLA_ROPE // 2
    cos = jnp.cos(ang)
    sin = jnp.sin(ang)
    one = jnp.ones((S, KR_LANE), F32)
    zero = jnp.zeros((S, KR_LANE), F32)
    tail_one = jnp.ones((S, LANES - KR_LANE - MLA_ROPE), F32)
    tail_zero = jnp.zeros((S, LANES - KR_LANE - MLA_ROPE), F32)
    cos_t = jnp.concatenate([one, cos, cos, tail_one], axis=1)
    sin_t = jnp.concatenate([zero, -sin, sin, tail_zero], axis=1)
    assert cos_t.shape == (S, LANES) and half * 2 == MLA_ROPE
    return cos_t, sin_t


def _rope(x, cos_t, sin_t, sign):
    n = x.shape[1] // LANES
    half = MLA_ROPE // 2
    lane = _iota(x.shape, 1) & (LANES - 1)
    first = (lane >= KR_LANE) & (lane < KR_LANE + half)
    swapped = jnp.where(first, pltpu.roll(x, x.shape[1] - half, 1), pltpu.roll(x, half, 1))
    c = jnp.tile(cos_t, (1, n)) if n > 1 else cos_t
    s = jnp.tile(sin_t, (1, n)) if n > 1 else sin_t
    return x * c + swapped * (s * sign)


def _mla_prep_fwd(name, proj, g_cq, g_ckv, w_uq, w_uk, w_uv, cos_t, sin_t):
    nh = MLA_H * LANES

    def body(cq_ref, ckv_ref, kr_ref, gq_ref, gkv_ref, wq_ref, wk_ref, wv_ref, cos_ref, sin_ref,
             qa_ref, ka_ref, va_ref):
        cos_v, sin_v = cos_ref[...], sin_ref[...]
        cq = cq_ref[...]
        r = lax.rsqrt(jnp.mean(cq * cq, axis=-1, keepdims=True) + RMS_EPS)
        cqn = (cq * r * gq_ref[...]).astype(BF16)
        qa_ref[...] = _rope(_dot(cqn, wq_ref[...]), cos_v, sin_v, 1.0).astype(BF16)
        ckv = ckv_ref[...]
        r = lax.rsqrt(jnp.mean(ckv * ckv, axis=-1, keepdims=True) + RMS_EPS)
        ckvn = (ckv * r * gkv_ref[...]).astype(BF16)
        lane = _iota((TM, LANES), 1)
        rot = (lane >= KR_LANE) & (lane < KR_LANE + MLA_ROPE)
        kr = jnp.where(rot, _rope(kr_ref[...], cos_v, sin_v, 1.0), 0.0)
        ka_ref[...] = (_dot(ckvn, wk_ref[...]) + jnp.tile(kr, (1, MLA_H))).astype(BF16)
        va_ref[...] = _dot(ckvn, wv_ref[...]).astype(BF16)

    full = lambda shape: pl.BlockSpec(shape, lambda i: (0, 0))
    return pl.pallas_call(
        body, name=name, grid=(S // TM,),
        in_specs=[pl.BlockSpec((TM, Q_LORA), lambda i: (i, P_CQ // Q_LORA)),
                  pl.BlockSpec((TM, KV_LORA), lambda i: (i, P_CKV // KV_LORA)),
                  pl.BlockSpec((TM, LANES), lambda i: (i, P_KR // LANES)),
                  full((1, Q_LORA)), full((1, KV_LORA)), full((Q_LORA, nh)), full((KV_LORA, nh)),
                  full((KV_LORA, MLA_H * MLA_V)),
                  pl.BlockSpec((TM, LANES), lambda i: (i, 0)), pl.BlockSpec((TM, LANES), lambda i: (i, 0))],
        out_specs=[pl.BlockSpec((TM, nh), lambda i: (i, 0)), pl.BlockSpec((TM, nh), lambda i: (i, 0)),
                   pl.BlockSpec((TM, MLA_H * MLA_V), lambda i: (i, 0))],
        out_shape=[jax.ShapeDtypeStruct((S, nh), BF16), jax.ShapeDtypeStruct((S, nh), BF16),
                   jax.ShapeDtypeStruct((S, MLA_H * MLA_V), BF16)],
        compiler_params=_params("parallel"),
    )(proj, proj, proj, g_cq, g_ckv, w_uq, w_uk, w_uv, cos_t, sin_t)


def _mla_prep_bwd(name, dqa, dka, dva, proj, g_cq, g_ckv, w_uq, w_uk, w_uv, cos_t, sin_t):
    nh = MLA_H * LANES

    def body(dqa_ref, dka_ref, dva_ref, cq_ref, ckv_ref, gq_ref, gkv_ref, wq_ref, wk_ref, wv_ref, cos_ref, sin_ref,
             dcq_ref, dckv_ref, dkr_ref, dwq_ref, dwk_ref, dwv_ref, dgq_ref, dgkv_ref):
        i = pl.program_id(0)
        cos_v, sin_v = cos_ref[...], sin_ref[...]

        def norm_bwd(x, g, dn):
            r = lax.rsqrt(jnp.mean(x * x, axis=-1, keepdims=True) + RMS_EPS)
            xh = x * r
            dxh = dn * g
            dx = r * (dxh - xh * jnp.mean(dxh * xh, axis=-1, keepdims=True))
            return dx, jnp.sum(dn * xh, axis=0, keepdims=True), (xh * g).astype(BF16)

        dq = _rope(dqa_ref[...], cos_v, sin_v, -1.0).astype(BF16)
        dcqn = _dot(dq, wq_ref[...], "nt")
        dcq, dgq, cqn = norm_bwd(cq_ref[...], gq_ref[...], dcqn)
        dcq_ref[...] = dcq.astype(BF16)
        dwq = _dot(cqn, dq, "tn")

        dka = dka_ref[...]
        dkab = dka.astype(BF16)
        dvab = dva_ref[...].astype(BF16)
        dckvn = _dot(dkab, wk_ref[...], "nt") + _dot(dvab, wv_ref[...], "nt")
        dckv, dgkv, ckvn = norm_bwd(ckv_ref[...], gkv_ref[...], dckvn)
        dckv_ref[...] = dckv.astype(BF16)
        dwk = _dot(ckvn, dkab, "tn")
        dwv = _dot(ckvn, dvab, "tn")

        fold = dka[:, 0:LANES]
        for hh in range(1, MLA_H):
            fold = fold + dka[:, hh * LANES:(hh + 1) * LANES]
        lane = _iota((TM, LANES), 1)
        rot = (lane >= KR_LANE) & (lane < KR_LANE + MLA_ROPE)
        dkr = _rope(jnp.where(rot, fold, 0.0), cos_v, sin_v, -1.0)
        dkr_ref[...] = jnp.where(rot, dkr, 0.0).astype(BF16)

        @pl.when(i == 0)
        def _():
            dwq_ref[...] = dwq
            dwk_ref[...] = dwk
            dwv_ref[...] = dwv
            dgq_ref[...] = dgq
            dgkv_ref[...] = dgkv

        @pl.when(i > 0)
        def _():
            dwq_ref[...] += dwq
            dwk_ref[...] += dwk
            dwv_ref[...] += dwv
            dgq_ref[...] += dgq
            dgkv_ref[...] += dgkv

    full = lambda shape: pl.BlockSpec(shape, lambda i: (0, 0))
    rows = lambda c: pl.BlockSpec((TM, c), lambda i: (i, 0))
    nv = MLA_H * MLA_V
    return pl.pallas_call(
        body, name=name, grid=(S // TM,),
        in_specs=[rows(nh), rows(nh), rows(nv),
                  pl.BlockSpec((TM, Q_LORA), lambda i: (i, P_CQ // Q_LORA)),
                  pl.BlockSpec((TM, KV_LORA), lambda i: (i, P_CKV // KV_LORA)),
                  full((1, Q_LORA)), full((1, KV_LORA)), full((Q_LORA, nh)), full((KV_LORA, nh)), full((KV_LORA, nv)),
                  rows(LANES), rows(LANES)],
        out_specs=[rows(Q_LORA), rows(KV_LORA), rows(LANES), full((Q_LORA, nh)), full((KV_LORA, nh)),
                   full((KV_LORA, nv)), full((1, Q_LORA)), full((1, KV_LORA))],
        out_shape=[jax.ShapeDtypeStruct((S, Q_LORA), BF16), jax.ShapeDtypeStruct((S, KV_LORA), BF16),
                   jax.ShapeDtypeStruct((S, LANES), BF16), jax.ShapeDtypeStruct((Q_LORA, nh), F32),
                   jax.ShapeDtypeStruct((KV_LORA, nh), F32), jax.ShapeDtypeStruct((KV_LORA, nv), F32),
                   jax.ShapeDtypeStruct((1, Q_LORA), F32), jax.ShapeDtypeStruct((1, KV_LORA), F32)],
        compiler_params=_params("arbitrary"),
    )(dqa, dka, dva, proj, proj, g_cq, g_ckv, w_uq, w_uk, w_uv, cos_t, sin_t)


def _head_masks(dtype):
    lane = _iota((1, LANES), 1)
    return (lane < 64).astype(dtype), (lane >= 64).astype(dtype)


def _mla_fwd(name, qa, ka, va, carry=None):
    def body(q_ref, k_ref, v_ref, o_ref, lse_ref):
        m0b, m1b = _head_masks(BF16)
        lane = _iota((QB, LANES), 1)
        left = lane < 64

        def qblock(i, _):
            r0 = pl.multiple_of(i * QB, QB)
            qs = [q_ref[pl.ds(r0, QB), hh * LANES:(hh + 1) * LANES] for hh in range(2)]
            rowc = lax.shift_right_logical(r0 + _iota((QB, QB), 0), 6)

            def kv(kb, carry):
                ms, ls, acc = carry
                c0 = pl.multiple_of(kb * QB, QB)
                v = v_ref[pl.ds(c0, QB), :]
                ok = lax.shift_right_logical(c0 + _iota((QB, QB), 1), 6) <= rowc
                new_m, new_l, alphas = [], [], []
                pv = None
                for hh in range(2):
                    k = k_ref[pl.ds(c0, QB), hh * LANES:(hh + 1) * LANES]
                    s = jnp.where(ok, _dot(qs[hh], k, "nt") * (SCALE_A * LOG2_E), NEG)
                    mn = jnp.maximum(ms[hh], jnp.max(s, axis=-1, keepdims=True))
                    p = jnp.exp2(s - mn)
                    a = jnp.exp2(ms[hh] - mn)
                    new_m.append(mn)
                    new_l.append(a * ls[hh] + jnp.sum(p, axis=-1, keepdims=True))
                    alphas.append(a)
                    part = _dot(p.astype(BF16), v * (m0b if hh == 0 else m1b))
                    pv = part if pv is None else pv + part
                acc = acc * jnp.where(left, alphas[0], alphas[1]) + pv
                return tuple(new_m), tuple(new_l), acc

            init = ((jnp.full((QB, 1), NEG, F32),) * 2, (jnp.zeros((QB, 1), F32),) * 2, jnp.zeros((QB, LANES), F32))
            ms, ls, acc = lax.fori_loop(0, i + 1, kv, init)
            o_ref[pl.ds(r0, QB), :] = acc * jnp.where(left, 1.0 / ls[0], 1.0 / ls[1])
            lse_ref[pl.ds(r0, QB), :] = jnp.where(left, ms[0] + jnp.log(ls[0]) * LOG2_E, ms[1] + jnp.log(ls[1]) * LOG2_E)
            return 0

        lax.fori_loop(0, S // QB, qblock, 0)

    pair = lambda w: pl.BlockSpec((S, w), lambda p: (0, p))
    return _carrier_call(
        body, name=name, grid=(MLA_H // 2,), in_specs=[pair(2 * LANES), pair(2 * LANES), pair(LANES)],
        out_specs=[pair(LANES), pair(LANES)],
        out_shape=[jax.ShapeDtypeStruct((S, MLA_H * MLA_V), F32), jax.ShapeDtypeStruct((S, MLA_H * MLA_V), F32)],
        args=(qa, ka, va), sem=("parallel",), carry=carry)


def _mla_bwd(name, qa, ka, va, o, lse, do, do_block0, carry=None):
    def body(q_ref, k_ref, v_ref, o_ref, lse_ref, do_ref, dq_ref, dk_ref, dv_ref):
        m0f, m1f = _head_masks(F32)
        m0b, m1b = _head_masks(BF16)
        dk_ref[...] = jnp.zeros_like(dk_ref)
        dv_ref[...] = jnp.zeros_like(dv_ref)

        def qblock(i, _):
            r0 = pl.multiple_of(i * QB, QB)
            rows = pl.ds(r0, QB)
            do_f = do_ref[rows, :]
            prod = do_f * o_ref[rows, :]
            deltas = [jnp.sum(prod * m0f, axis=-1, keepdims=True), jnp.sum(prod * m1f, axis=-1, keepdims=True)]
            lse_v = lse_ref[rows, :]
            lses = [lse_v[:, 0:1], lse_v[:, 64:65]]
            dob = do_f.astype(BF16)
            dos = [dob * m0b, dob * m1b]
            qs = [q_ref[rows, hh * LANES:(hh + 1) * LANES] for hh in range(2)]
            rowc = lax.shift_right_logical(r0 + _iota((QB, QB), 0), 6)

            def kv(kb, dqs):
                c0 = pl.multiple_of(kb * QB, QB)
                cols = pl.ds(c0, QB)
                v = v_ref[cols, :]
                ok = lax.shift_right_logical(c0 + _iota((QB, QB), 1), 6) <= rowc
                out = []
                dv = None
                for hh in range(2):
                    k = k_ref[cols, hh * LANES:(hh + 1) * LANES]
                    s = _dot(qs[hh], k, "nt") * (SCALE_A * LOG2_E)
                    p = jnp.where(ok, jnp.exp2(s - lses[hh]), 0.0)
                    dp = _dot(dos[hh], v, "nt")
                    ds = (p * (dp - deltas[hh]) * SCALE_A).astype(BF16)
                    out.append(dqs[hh] + _dot(ds, k))
                    dk_ref[cols, hh * LANES:(hh + 1) * LANES] += _dot(ds, qs[hh], "tn")
                    part = _dot(p.astype(BF16), dos[hh], "tn")
                    dv = part if dv is None else dv + part
                dv_ref[cols, :] += dv
                return tuple(out)

            dqs = lax.fori_loop(0, i + 1, kv, (jnp.zeros((QB, LANES), F32),) * 2)
            for hh in range(2):
                dq_ref[rows, hh * LANES:(hh + 1) * LANES] = dqs[hh]
            return 0

        lax.fori_loop(0, S // QB, qblock, 0)

    pair = lambda w: pl.BlockSpec((S, w), lambda p: (0, p))
    return _carrier_call(
        body, name=name, grid=(MLA_H // 2,),
        in_specs=[pair(2 * LANES), pair(2 * LANES), pair(LANES), pair(LANES), pair(LANES),
                  pl.BlockSpec((S, LANES), lambda p: (0, do_block0 + p))],
        out_specs=[pair(2 * LANES), pair(2 * LANES), pair(LANES)],
        out_shape=[jax.ShapeDtypeStruct((S, MLA_H * LANES), F32), jax.ShapeDtypeStruct((S, MLA_H * LANES), F32),
                   jax.ShapeDtypeStruct((S, MLA_H * MLA_V), F32)],
        args=(qa, ka, va, o, lse, do), sem=("parallel",), carry=carry)


def _sb_weights(q_h, k, c, before, tri_suffix):
    z = _dot(q_h, k, "nt") * (SCALE_B * LOG2_E)
    sp = jnp.maximum(z, 0.0) + jnp.log(1.0 + jnp.exp2(-jnp.abs(z))) * LOG2_E
    log_keep = jnp.where(before, -sp, 0.0)
    to_the_right, total = _running_sum(log_keep, tri_suffix, True)
    w = jnp.where(before, jnp.exp2(z - sp + to_the_right + c), 0.0)
    return w, jnp.exp2(z - sp), total


def _sb_fwd(name, proj, carry=None):
    def body(q_ref, k_ref, v_ref, o_ref):
        m0b, m1b = _head_masks(BF16)
        tri_suffix = (_iota((MXU_W, MXU_W), 0) > _iota((MXU_W, MXU_W), 1)).astype(BF16)

        def qblock(i, _):
            r0 = pl.multiple_of(i * QB, QB)
            q = q_ref[pl.ds(r0, QB), :].astype(BF16)
            qs = [q * m0b, q * m1b]
            rowg = r0 + _iota((QB, QB), 0)

            def kv(step, carry):
                cs, acc = carry
                c0 = pl.multiple_of((i - step) * QB, QB)
                k = k_ref[pl.ds(c0, QB), :].astype(BF16)
                v = v_ref[pl.ds(c0, QB), :].astype(BF16)
                before = (c0 + _iota((QB, QB), 1)) < rowg
                new_c = []
                for hh in range(2):
                    w, _, tot = _sb_weights(qs[hh], k, cs[hh], before, tri_suffix)
                    new_c.append(cs[hh] + tot)
                    acc = acc + _dot(w.astype(BF16), v * (m0b if hh == 0 else m1b))
                return tuple(new_c), acc

            init = ((jnp.zeros((QB, 1), F32),) * 2, jnp.zeros((QB, LANES), F32))
            _, acc = lax.fori_loop(0, i + 1, kv, init)
            o_ref[pl.ds(r0, QB), :] = acc.astype(BF16)
            return 0

        lax.fori_loop(0, S // QB, qblock, 0)

    col = lambda base: pl.BlockSpec((S, LANES), lambda p: (0, base // LANES + p))
    return _carrier_call(
        body, name=name, grid=(SB_H // 2,), in_specs=[col(P_QB), col(P_KB), col(P_VB)],
        out_specs=[pl.BlockSpec((S, LANES), lambda p: (0, p))],
        out_shape=[jax.ShapeDtypeStruct((S, SB_H * SB_DIM), BF16)],
        args=(proj, proj, proj), sem=("parallel",), carry=carry)


def _sb_bwd(name, proj, do, do_block0, carry=None):
    nb = S // QB

    def body(q_ref, k_ref, v_ref, do_ref, dq_ref, dk_ref, dv_ref, sig_scr, dl_scr, dk_acc, dv_acc):
        m0b, m1b = _head_masks(BF16)
        tri_suffix = (_iota((MXU_W, MXU_W), 0) > _iota((MXU_W, MXU_W), 1)).astype(BF16)
        tri_prefix = (_iota((MXU_W, MXU_W), 0) < _iota((MXU_W, MXU_W), 1)).astype(BF16)
        dk_acc[...] = jnp.zeros_like(dk_acc)
        dv_acc[...] = jnp.zeros_like(dv_acc)

        def qblock(i, _):
            r0 = pl.multiple_of(i * QB, QB)
            rows = pl.ds(r0, QB)
            q = q_ref[rows, :].astype(BF16)
            qs = [q * m0b, q * m1b]
            dob = do_ref[rows, :].astype(BF16)
            dos = [dob * m0b, dob * m1b]
            rowg = r0 + _iota((QB, QB), 0)

            def sweep_left(step, cs):
                kb = i - step
                c0 = pl.multiple_of(kb * QB, QB)
                cols = pl.ds(c0, QB)
                k = k_ref[cols, :].astype(BF16)
                v = v_ref[cols, :].astype(BF16)
                before = (c0 + _iota((QB, QB), 1)) < rowg
                new_c = []
                dv = None
                for hh in range(2):
                    w, sig, tot = _sb_weights(qs[hh], k, cs[hh], before, tri_suffix)
                    new_c.append(cs[hh] + tot)
                    sig_scr[hh, kb] = sig
                    dl_scr[hh, kb] = _dot(dos[hh], v, "nt") * w
                    part = _dot(w.astype(BF16), dos[hh], "tn")
                    dv = part if dv is None else dv + part
                dv_acc[cols, :] += dv
                return tuple(new_c)

            lax.fori_loop(0, i + 1, sweep_left, (jnp.zeros((QB, 1), F32),) * 2)

            def sweep_right(kb, carry):
                ps, dq = carry
                c0 = pl.multiple_of(kb * QB, QB)
                cols = pl.ds(c0, QB)
                k = k_ref[cols, :].astype(BF16)
                before = (c0 + _iota((QB, QB), 1)) < rowg
                new_p = []
                dk = None
                for hh in range(2):
                    dl = dl_scr[hh, kb]
                    sig = sig_scr[hh, kb]
                    to_the_left, total = _running_sum(dl, tri_prefix, False)
                    earlier = to_the_left + ps[hh]
                    new_p.append(ps[hh] + total)
                    dz = (jnp.where(before, dl * (1.0 - sig) - earlier * sig, 0.0) * SCALE_B).astype(BF16)
                    dq = dq + _dot(dz, k * (m0b if hh == 0 else m1b))
                    part = _dot(dz, qs[hh], "tn")
                    dk = part if dk is None else dk + part
                dk_acc[cols, :] += dk
                return tuple(new_p), dq

            init = ((jnp.zeros((QB, 1), F32),) * 2, jnp.zeros((QB, LANES), F32))
            _, dq = lax.fori_loop(0, i + 1, sweep_right, init)
            dq_ref[rows, :] = dq.astype(BF16)
            return 0

        lax.fori_loop(0, nb, qblock, 0)
        dk_ref[...] = dk_acc[...].astype(BF16)
        dv_ref[...] = dv_acc[...].astype(BF16)

    col = lambda base: pl.BlockSpec((S, LANES), lambda p: (0, base // LANES + p))
    out = pl.BlockSpec((S, LANES), lambda p: (0, p))
    shape = jax.ShapeDtypeStruct((S, SB_H * SB_DIM), BF16)
    return _carrier_call(
        body, name=name, grid=(SB_H // 2,),
        in_specs=[col(P_QB), col(P_KB), col(P_VB), pl.BlockSpec((S, LANES), lambda p: (0, do_block0 + p))],
        out_specs=[out, out, out], out_shape=[shape, shape, shape],
        scratch_shapes=[pltpu.VMEM((2, nb, QB, QB), F32), pltpu.VMEM((2, nb, QB, QB), F32),
                        pltpu.VMEM((S, LANES), F32), pltpu.VMEM((S, LANES), F32)],
        args=(proj, proj, proj, do), sem=("parallel",), carry=carry)


def _band_row_index():
    j = np.arange(TOEP_W)
    rel = np.clip(LEFT_CHUNKS * CHUNK - j, -REL_CLIP, REL_CLIP) + REL_CLIP
    rel[BAND_W:] = 2 * REL_CLIP
    return rel.astype(np.int32)


def _band_tiles(r0_ref, q_ref, kpad, vpad, m, m0b, m1b, static_ok, bias):
    r0 = pl.multiple_of(m * BQ, BQ)
    q = q_ref[0, pl.ds(r0, BQ), :]
    kw = kpad[pl.ds(r0, BAND_W), :]
    vw = vpad[pl.ds(r0, BAND_W), :]
    ok = static_ok & ((r0 - BAND_PAD + _iota((BQ, BAND_W), 1)) >= 0)
    qs = [q * m0b, q * m1b]
    ps = []
    for hh in range(2):
        s = jnp.where(ok, _dot(qs[hh], kw, "nt") * (SCALE_C * LOG2_E) + bias[hh], NEG)
        e = jnp.exp2(s - jnp.max(s, axis=-1, keepdims=True))
        ps.append(e * (1.0 / jnp.sum(e, axis=-1, keepdims=True)))
    return r0, qs, kw, vw, ps


def _band_setup(qkv_ref, r0_ref, kpad, vpad):
    kpad[0:BAND_PAD, :] = jnp.zeros((BAND_PAD, LANES), BF16)
    vpad[0:BAND_PAD, :] = jnp.zeros((BAND_PAD, LANES), BF16)
    kpad[BAND_PAD:, :] = qkv_ref[1]
    vpad[BAND_PAD:, :] = qkv_ref[2]
    jc = lax.shift_right_logical(_iota((BQ, BAND_W), 1), 6)
    rc = lax.shift_right_logical(_iota((BQ, BAND_W), 0), 6)
    static_ok = (jc >= rc) & (jc <= rc + LEFT_CHUNKS)
    bias = []
    for hh in range(2):
        row = jnp.broadcast_to(r0_ref[hh:hh + 1, :] * LOG2_E, (BQ, TOEP_W))
        bias.append(pltpu.roll(row, 0, 1, stride=1, stride_axis=0)[:, :BAND_W])
    return static_ok, bias


def _band_fwd(name, qkv, r0, carry=None):
    def body(qkv_ref, r0_ref, o_ref, kpad, vpad):
        m0b, m1b = _head_masks(BF16)
        static_ok, bias = _band_setup(qkv_ref, r0_ref, kpad, vpad)

        def qblock(m, _):
            r0_, _, _, vw, ps = _band_tiles(r0_ref, qkv_ref, kpad, vpad, m, m0b, m1b, static_ok, bias)
            o = _dot(ps[0].astype(BF16), vw * m0b) + _dot(ps[1].astype(BF16), vw * m1b)
            o_ref[pl.ds(r0_, BQ), :] = o.astype(BF16)
            return 0

        lax.fori_loop(0, S // BQ, qblock, 0)

    return _carrier_call(
        body, name=name, grid=(C_H // 2,),
        in_specs=[pl.BlockSpec((3, S, LANES), lambda p: (0, 0, p)), pl.BlockSpec((None, 2, TOEP_W), lambda p: (p, 0, 0))],
        out_specs=[pl.BlockSpec((S, LANES), lambda p: (0, p))],
        out_shape=[jax.ShapeDtypeStruct((S, C_H * C_DIM), BF16)],
        scratch_shapes=[pltpu.VMEM((S + BAND_PAD, LANES), BF16), pltpu.VMEM((S + BAND_PAD, LANES), BF16)],
        args=(qkv, r0), sem=("parallel",), carry=carry)


def _band_bwd(name, qkv, r0, do, carry=None):
    def body(qkv_ref, r0_ref, do_ref, dqkv_ref, dr0_ref, kpad, vpad, dkpad, dvpad, db_acc):
        m0b, m1b = _head_masks(BF16)
        static_ok, bias = _band_setup(qkv_ref, r0_ref, kpad, vpad)
        dkpad[...] = jnp.zeros_like(dkpad)
        dvpad[...] = jnp.zeros_like(dvpad)
        db_acc[...] = jnp.zeros_like(db_acc)

        def qblock(m, _):
            r0_, qs, kw, vw, ps = _band_tiles(r0_ref, qkv_ref, kpad, vpad, m, m0b, m1b, static_ok, bias)
            dob = do_ref[pl.ds(r0_, BQ), :].astype(BF16)
            dos = [dob * m0b, dob * m1b]
            dq = None
            dk = None
            dv = None
            for hh in range(2):
                p = ps[hh]
                dp = _dot(dos[hh], vw, "nt")
                ds = p * (dp - jnp.sum(dp * p, axis=-1, keepdims=True))
                db_acc[hh, :, 0:BAND_W] += ds
                dsb = (ds * SCALE_C).astype(BF16)
                t = _dot(dsb, kw * (m0b if hh == 0 else m1b))
                dq = t if dq is None else dq + t
                t = _dot(dsb, qs[hh], "tn")
                dk = t if dk is None else dk + t
                t = _dot(p.astype(BF16), dos[hh], "tn")
                dv = t if dv is None else dv + t
            dqkv_ref[0, pl.ds(r0_, BQ), :] = dq.astype(BF16)
            dkpad[pl.ds(r0_, BAND_W), :] += dk
            dvpad[pl.ds(r0_, BAND_W), :] += dv
            return 0

        lax.fori_loop(0, S // BQ, qblock, 0)
        dqkv_ref[1] = dkpad[BAND_PAD:, :].astype(BF16)
        dqkv_ref[2] = dvpad[BAND_PAD:, :].astype(BF16)
        sub = _iota((8, TOEP_W), 0)
        for hh in range(2):
            folded = db_acc[hh, 0:8, :]
            for a in range(1, BQ // 8):
                folded = folded + pltpu.roll(db_acc[hh, 8 * a:8 * a + 8, :], TOEP_W - 8 * a, 1)
            for bit in range(3):
                moved = pltpu.roll(folded, TOEP_W - (1 << bit), 1)
                folded = jnp.where((sub & (1 << bit)) != 0, moved, folded)
            dr0_ref[hh:hh + 1, :] = jnp.sum(folded, axis=0, keepdims=True)

    return _carrier_call(
        body, name=name, grid=(C_H // 2,),
        in_specs=[pl.BlockSpec((3, S, LANES), lambda p: (0, 0, p)), pl.BlockSpec((None, 2, TOEP_W), lambda p: (p, 0, 0)),
                  pl.BlockSpec((S, LANES), lambda p: (0, p))],
        out_specs=[pl.BlockSpec((3, S, LANES), lambda p: (0, 0, p)), pl.BlockSpec((None, 2, TOEP_W), lambda p: (p, 0, 0))],
        out_shape=[jax.ShapeDtypeStruct((3, S, C_H * C_DIM), BF16), jax.ShapeDtypeStruct((C_H // 2, 2, TOEP_W), F32)],
        scratch_shapes=[pltpu.VMEM((S + BAND_PAD, LANES), BF16), pltpu.VMEM((S + BAND_PAD, LANES), BF16),
                        pltpu.VMEM((S + BAND_PAD, LANES), F32), pltpu.VMEM((S + BAND_PAD, LANES), F32),
                        pltpu.VMEM((2, BQ, TOEP_W), F32)],
        args=(qkv, r0, do), sem=("parallel",), carry=carry)


def _bias_table_grad(name, dr0):
    w_out = 5 * LANES

    def body(d_ref, o_ref):
        j = _iota((TOEP_W, w_out), 0)
        rel = jnp.clip(LEFT_CHUNKS * CHUNK - j, -REL_CLIP, REL_CLIP) + REL_CLIP
        rel = jnp.where(j >= BAND_W, 2 * REL_CLIP, rel)
        onehot = (rel == _iota((TOEP_W, w_out), 1)).astype(BF16)
        d = d_ref[...]
        hi = d.astype(BF16)
        mid = (d - hi.astype(F32))
        mid_b = mid.astype(BF16)
        lo = (mid - mid_b.astype(F32)).astype(BF16)
        o_ref[...] = _dot(hi, onehot) + _dot(mid_b, onehot) + _dot(lo, onehot)

    return pl.pallas_call(
        body, name=name, out_shape=jax.ShapeDtypeStruct((C_H, w_out), F32),
        in_specs=[pl.BlockSpec((C_H, TOEP_W), lambda: (0, 0))], out_specs=pl.BlockSpec((C_H, w_out), lambda: (0, 0)),
        grid=(),
    )(dr0)


def _carry_gather(cy, slots, names, ici, d2d):
    idx = [cy.operand(slots[n], True) for n in names]
    n = len(names)
    base_i = cy.sems(3 * n) if ici else 0
    base_d = cy.sems(3 * n) if d2d else 0

    def piece(refs, t, slot, cc):
        return refs[idx[t]].at[slot, _half_rows(cc, slots[names[t]].shape[1] // 2), :]

    def over_ici(refs, send, recv, arriving):
        x, y, c, chips = _position()
        out = []
        for t in range(n):
            for j in range(3):
                r = piece(refs, t, 2 * chips[j][0] + chips[j][1] if arriving else 2 * x + y, c)
                out.append(_remote(r, r, send, recv, base_i + 3 * t + j, (*chips[j], c)))
        return out

    def over_d2d(refs, send, recv, arriving):
        x, y, c, chips = _position()
        out = []
        for t in range(n):
            for j in range(3):
                r = piece(refs, t, 2 * chips[j][0] + chips[j][1], 1 - c if arriving else c)
                out.append(_remote(r, r, send, recv, base_d + 3 * t + j, (x, y, 1 - c)))
        return out

    def start_ici(refs, fresh, send, recv):
        for cp in over_ici(refs, send, recv, False):
            cp.start()

    def wait_ici(refs, fresh, send, recv):
        for cp in over_ici(refs, send, recv, True):
            cp.wait_recv()
        for cp in over_ici(refs, send, recv, False):
            cp.wait_send()

    def start_d2d(refs, fresh, send, recv):
        for cp in over_d2d(refs, send, recv, False):
            cp.start()

    def wait_d2d(refs, fresh, send, recv):
        for cp in over_d2d(refs, send, recv, True):
            cp.wait_recv()
        for cp in over_d2d(refs, send, recv, False):
            cp.wait_send()

    if ici and d2d:
        cy.starts.append(start_ici)
        cy.finishes += [wait_ici, start_d2d, wait_d2d]
    elif ici:
        cy.starts.append(start_ici)
        cy.finishes.append(wait_ici)
    else:
        cy.starts.append(start_d2d)
        cy.finishes.append(wait_d2d)

    def done(aliased, fresh):
        for t, name in enumerate(names):
            slots[name] = aliased[idx[t]]

    cy.on_done.append(done)


def _carry_chip_exchange(cy, sums, got, names):
    idx = [cy.operand(sums[n], False) for n in names]
    out = [cy.result((3,) + sums[n].shape[1:], BF16) for n in names]
    base = cy.sems(3 * len(names))

    def copies(refs, fresh, send, recv):
        x, y, c, chips = _position()
        return [_remote(refs[idx[t]].at[2 * chips[j][0] + chips[j][1]], fresh[out[t]].at[j], send, recv, base + 3 * t + j,
                        (*chips[j], c)) for t in range(len(names)) for j in range(3)]

    def start(refs, fresh, send, recv):
        for cp in copies(refs, fresh, send, recv):
            cp.start()

    def wait(refs, fresh, send, recv):
        for cp in copies(refs, fresh, send, recv):
            cp.wait()

    cy.starts.append(start)
    cy.finishes.append(wait)

    def done(aliased, fresh):
        for t, name in enumerate(names):
            got[name] = fresh[out[t]]

    cy.on_done.append(done)


def _run_carry(name, cy):
    _, res = _carrier_call(None, name=name, grid=(), in_specs=[], out_specs=[], out_shape=[], args=(), sem=(), carry=cy)
    cy.done(res)


FIRST_WEIGHTS = ("ev_w_in", "ev_w_uq", "ev_w_ukv")
WEIGHTS_A = ("ev_w_out", "w_gate0", "w_up0")
WEIGHTS_B = ("w_down0", "od_w_qkv", "od_w_out")
WEIGHTS_C = ("w_gate1",)
WEIGHTS_D = ("w_up1", "w_down1")
GRAD_GROUPS = {"ffn1": ("w_gate1", "w_up1", "w_down1"), "od": ("od_w_qkv", "od_w_out"),
               "ffn0": ("w_gate0", "w_up0", "w_down0"), "ev_out": ("ev_w_out",),
               "ev": ("ev_w_in", "ev_w_uq", "ev_w_ukv")}


def _carry_pair_exchange(cy, parts, theirs, names):
    idx = [cy.operand(parts[n], False) for n in names]
    out = [cy.result((N_CHIPS, parts[n].shape[1] // 2, parts[n].shape[2]), BF16) for n in names]
    base = cy.sems(len(names))

    def copies(refs, fresh, send, recv):
        x, y, c, _ = _position()
        return [_remote(refs[idx[t]].at[:, _half_rows(1 - c, parts[n].shape[1] // 2), :], fresh[out[t]], send, recv,
                        base + t, (x, y, 1 - c)) for t, n in enumerate(names)]

    cy.starts.append(lambda refs, fresh, send, recv: [cp.start() for cp in copies(refs, fresh, send, recv)])
    cy.finishes.append(lambda refs, fresh, send, recv: [cp.wait() for cp in copies(refs, fresh, send, recv)])

    def done(aliased, fresh):
        for t, name in enumerate(names):
            theirs[name] = fresh[out[t]]

    cy.on_done.append(done)


def _carry_sibling_exchange(cy, fulls, pieces):
    idx = [cy.operand(fulls[p], True) for p, _ in pieces]
    base = cy.sems(len(pieces))

    def copies(refs, send, recv, arriving):
        x, y, c, _ = _position()
        out = []
        for t, (p, layer) in enumerate(pieces):
            r = refs[idx[t]].at[layer, _half_rows(1 - c if arriving else c, fulls[p].shape[1] // 2), :]
            out.append(_remote(r, r, send, recv, base + t, (x, y, 1 - c)))
        return out

    def start(refs, fresh, send, recv):
        for cp in copies(refs, send, recv, False):
            cp.start()

    def wait(refs, fresh, send, recv):
        for cp in copies(refs, send, recv, True):
            cp.wait_recv()
        for cp in copies(refs, send, recv, False):
            cp.wait_send()

    cy.starts.append(start)
    cy.finishes.append(wait)

    def done(aliased, fresh):
        for t, (p, _) in enumerate(pieces):
            fulls[p] = aliased[idx[t]]

    cy.on_done.append(done)


RIDES = {
    "cast_rest": (("gather", FIRST_WEIGHTS),),
    "mla_attn": (("gather_ici", WEIGHTS_A),),
    "sb_attn": (("gather_d2d", WEIGHTS_A), ("gather_ici", WEIGHTS_B)),
    "ev_out": (("gather_d2d", WEIGHTS_B),),
    "ffn0": (("gather_ici", WEIGHTS_C),),
    "qkv": (("gather_d2d", WEIGHTS_C),),
    "band_attn": (("gather_ici", WEIGHTS_D),),
    "od_out": (("gather_d2d", WEIGHTS_D),),
    "od_out_bwd_w": (("pair", "ffn1"),),
    "band_attn_bwd": (("chips", "ffn1"),),
    "rms_mix1_bwd": (("pair", "od"),),
    "ev_out_bwd_w": (("pair", "ffn0"),),
    "mla_attn_bwd": (("chips", "od"), ("sibling", "ffn1"), ("pair", "ev_out")),
    "sb_attn_bwd": (("chips", "ffn0"), ("sibling", "od"), ("chips", "ev_out")),
    "proj_in_bwd_w": (("sibling", "ffn0"), ("sibling", "ev_out")),
    "grads_pair_ev": (("pair", "ev"),),
    "proj_in_bwd_x": (("chips", "ev"),),
    "grads_sibling_ev": (("sibling", "ev"),),
}


class _Exchanges:
    def __init__(self, slots, pos, shapes, cast_rest):
        self.slots, self.pos, self.shapes, self.cast_rest = dict(slots), pos, shapes, cast_rest
        self.parts, self.theirs, self.sums, self.got, self.fulls = {}, {}, {}, {}, {}

    def begin(self):
        self.slots.update(self.cast_rest(self.carry("cast_rest")))

    def weights(self, *names):
        return [self.slots[n] for n in names]

    def _pair_sums(self, group):
        for n in GRAD_GROUPS[group]:
            if n not in self.sums:
                self.sums[n] = _pair_sum("pair_sum_" + n, self.parts[n], self.theirs[n], self.pos)

    def _chip_sums(self, group):
        for n in GRAD_GROUPS[group]:
            param, layer = PART_OF[n]
            self.fulls[param] = _chip_sum("chip_sum_" + n, self.sums[n], self.got[n], self.pos, layer,
                                          self.shapes[param], self.fulls.get(param))

    def carry(self, stage):
        cy = _Carry()
        for step, what in RIDES[stage]:
            if step == "gather":
                _carry_gather(cy, self.slots, what, True, True)
            elif step == "gather_ici":
                _carry_gather(cy, self.slots, what, True, False)
            elif step == "gather_d2d":
                _carry_gather(cy, self.slots, what, False, True)
            elif step == "pair":
                _carry_pair_exchange(cy, self.parts, self.theirs, GRAD_GROUPS[what])
            elif step == "chips":
                self._pair_sums(what)
                _carry_chip_exchange(cy, self.sums, self.got, GRAD_GROUPS[what])
            elif step == "sibling":
                self._chip_sums(what)
                _carry_sibling_exchange(cy, self.fulls, [PART_OF[n] for n in GRAD_GROUPS[what]])
        return cy

    def grads(self, group, parts):
        self.parts.update(parts)
        if group == "ev":
            _run_carry("grads_pair_ev", self.carry("grads_pair_ev"))

    def finish(self):
        _run_carry("grads_sibling_ev", self.carry("grads_sibling_ev"))
        return {n: self.fulls[n] for n in BIG}


class _NoExchanges:
    def __init__(self, slots):
        self.slots, self.parts = dict(slots), {}

    def begin(self):
        pass

    def weights(self, *names):
        return [self.slots[n] for n in names]

    def carry(self, stage):
        return None

    def grads(self, group, parts):
        self.parts.update(parts)


def _w_in_pieces():
    segments = ((0, Q_LORA, P_CQ), (Q_LORA, Q_LORA + KV_LORA, P_CKV),
                (Q_LORA + KV_LORA, Q_LORA + KV_LORA + MLA_ROPE, P_KR + KR_LANE),
                (Q_LORA + KV_LORA + MLA_ROPE, EVEN_IN, P_QB))
    width = EVEN_IN // N_CHIPS
    pieces = []
    for lo, hi, at in segments:
        for k in range(N_CHIPS):
            a, b = max(lo, k * width), min(hi, (k + 1) * width)
            if a < b:
                pieces.append((k, a - k * width, b - a, at + a - lo))
    return pieces


def _w_in_padded(name, w_in_s):
    tr = MXU_W

    def body(s_ref, o_ref):
        o_ref[...] = jnp.zeros(o_ref.shape, BF16)
        for k, a, n, at in _w_in_pieces():
            o_ref[:, at:at + n] = s_ref[k, :, a:a + n]

    return pl.pallas_call(
        body, name=name, grid=(D // tr,),
        in_specs=[pl.BlockSpec((N_CHIPS, tr, EVEN_IN // N_CHIPS), lambda i: (0, i, 0))],
        out_specs=pl.BlockSpec((tr, P_IN), lambda i: (i, 0)), out_shape=jax.ShapeDtypeStruct((D, P_IN), BF16),
        compiler_params=_params("parallel"))(w_in_s)


def _w_in_sharded(name, d_w_in_p):
    tr = MXU_W

    def body(p_ref, o_ref):
        for k, a, n, at in _w_in_pieces():
            o_ref[k, :, a:a + n] = p_ref[:, at:at + n]

    return pl.pallas_call(
        body, name=name, grid=(D // tr,),
        in_specs=[pl.BlockSpec((tr, P_IN), lambda i: (i, 0))],
        out_specs=pl.BlockSpec((N_CHIPS, tr, EVEN_IN // N_CHIPS), lambda i: (0, i, 0)),
        out_shape=jax.ShapeDtypeStruct((N_CHIPS, D, EVEN_IN // N_CHIPS), BF16),
        compiler_params=_params("parallel"))(d_w_in_p)


def _first_weights(w_in_s, w_uq_s, w_ukv_s):
    gw = {"ev_w_in": w_in_s, "ev_w_uq": w_uq_s, "ev_w_ukv": w_ukv_s}
    w_in_p = _w_in_padded("w_in_padded", w_in_s)
    w_uq = jnp.moveaxis(gw["ev_w_uq"], 0, 1).reshape(Q_LORA, MLA_H, MLA_NOPE + MLA_ROPE)
    w_uq_p = jnp.concatenate([w_uq, jnp.zeros((Q_LORA, MLA_H, LANES - MLA_NOPE - MLA_ROPE), BF16)], axis=2)
    w_ukv = jnp.moveaxis(gw["ev_w_ukv"], 0, 1).reshape(KV_LORA, MLA_H, MLA_NOPE + MLA_V)
    w_uk_p = jnp.concatenate([w_ukv[:, :, :MLA_NOPE], jnp.zeros((KV_LORA, MLA_H, LANES - MLA_NOPE), BF16)], axis=2)
    return dict(
        w_in=w_in_p, w_uq=w_uq_p.reshape(Q_LORA, MLA_H * LANES), w_uk=w_uk_p.reshape(KV_LORA, MLA_H * LANES),
        w_uv=w_ukv[:, :, MLA_NOPE:].reshape(KV_LORA, MLA_H * MLA_V))


def _proj_mm(name, u, w_in):
    return _mm(name, u, w_in, kind="nn", grid=(S // TM, 1, 1),
               a_spec=pl.BlockSpec((TM, D), lambda i, j, k: (i, 0)), b_spec=pl.BlockSpec((D, P_IN), lambda i, j, k: (0, 0)),
               o_spec=pl.BlockSpec((TM, P_IN), lambda i, j, k: (i, 0)), out_shape=(S, P_IN), out_dtype=F32, acc_shape=None)


def _out_proj(name, o, w, resid, carry=None):
    return _mm(name, o, w, kind="nn", grid=(S // TM, 1, 1),
               a_spec=pl.BlockSpec((TM, D), lambda i, j, k: (i, 0)), b_spec=pl.BlockSpec((D, D), lambda i, j, k: (0, 0)),
               o_spec=pl.BlockSpec((TM, D), lambda i, j, k: (i, 0)), out_shape=(S, D), out_dtype=F32, acc_shape=None,
               resid=resid, r_spec=pl.BlockSpec((TM, D), lambda i, j, k: (i, 0)), carry=carry)


def _out_proj_bwd(name, dh, o, w, ex):
    d_o = _mm(name + "_x", dh, w, kind="nt", grid=(S // TM, 1, 1),
              a_spec=pl.BlockSpec((TM, D), lambda i, j, k: (i, 0)), b_spec=pl.BlockSpec((D, D), lambda i, j, k: (0, 0)),
              o_spec=pl.BlockSpec((TM, D), lambda i, j, k: (i, 0)), out_shape=(S, D), out_dtype=F32, acc_shape=None)
    d_w = _mm(name + "_w", o, dh, kind="tn", grid=(2, S // TM),
              a_spec=pl.BlockSpec((TM, TM), lambda j, k: (k, j)), b_spec=pl.BlockSpec((TM, D), lambda j, k: (k, 0)),
              o_spec=pl.BlockSpec((TM, D), lambda j, k: (j, 0)), out_shape=(D, D), out_dtype=BF16, acc_shape=(TM, D),
              carry=ex.carry(name + "_w"))
    return d_o, d_w


def _local_step(x, tgt, sm, ex):
    def riding(stage, fn, *args):
        cy = ex.carry(stage)
        res, copies = fn(stage, *args, carry=cy)
        if cy is not None:
            cy.done(copies)
        return res

    cos_t, sin_t = _rope_tables()
    g_mix, g_ffn = sm["g_mix"], sm["g_ffn"]
    r0 = sm["od_rel_bias"][0][:, _band_row_index()].reshape(C_H // 2, 2, TOEP_W)
    nt = 3

    ex.begin()
    w = _first_weights(*ex.weights(*FIRST_WEIGHTS))
    u0 = _rms_fwd("rms_mix0", x, g_mix[0:1])
    proj = _proj_mm("proj_in", u0, w["w_in"])
    qa, ka, va = _mla_prep_fwd("mla_prep", proj, sm["ev_g_cq"], sm["ev_g_ckv"], w["w_uq"], w["w_uk"], w["w_uv"], cos_t, sin_t)
    o_a, lse = riding("mla_attn", _mla_fwd, qa, ka, va)
    o_b, = riding("sb_attn", _sb_fwd, proj)
    o_ev = jnp.concatenate([o_a.astype(BF16), o_b], axis=1)
    w["ev_w_out"] = ex.weights("ev_w_out")[0].reshape(D, D)
    h1 = _out_proj("ev_out", o_ev, w["ev_w_out"], x, ex.carry("ev_out"))
    w["w_gate0"], w["w_up0"], w["w_down0"] = ex.weights("w_gate0", "w_up0", "w_down0")
    h2, gate0, up0 = riding("ffn0", _ffn_fwd, h1, g_ffn[0:1], w["w_gate0"], w["w_up0"], w["w_down0"])
    w["w_qkv"] = jnp.moveaxis(ex.weights("od_w_qkv")[0], 0, 1).reshape(D, nt * D)
    u2 = _rms_fwd("rms_mix1", h2, g_mix[1:2])
    qkv = _mm("qkv", u2, w["w_qkv"], kind="nn", grid=(S // TM, nt, 1),
              a_spec=pl.BlockSpec((TM, D), lambda i, t, k: (i, 0)), b_spec=pl.BlockSpec((D, D), lambda i, t, k: (0, t)),
              o_spec=pl.BlockSpec((None, TM, D), lambda i, t, k: (t, i, 0)),
              out_shape=(nt, S, D), out_dtype=BF16, acc_shape=None, carry=ex.carry("qkv"))
    o_od, = riding("band_attn", _band_fwd, qkv, r0)
    w["od_w_out"] = ex.weights("od_w_out")[0].reshape(D, D)
    h3 = _out_proj("od_out", o_od, w["od_w_out"], h2, ex.carry("od_out"))
    w["w_gate1"], w["w_up1"], w["w_down1"] = ex.weights("w_gate1", "w_up1", "w_down1")
    (h4, gate1, up1), _ = _ffn_fwd("ffn1", h3, g_ffn[1:2], w["w_gate1"], w["w_up1"], w["w_down1"])

    loss, dh4, dg_final = _loss_bwd("loss", h4, sm["g_final"].reshape(1, D), tgt)

    dh3, dg_ffn1, u3, dgate, dup, act = _ffn_bwd("ffn1_bwd", dh4, h3, g_ffn[1:2], gate1, up1,
                                                 w["w_gate1"], w["w_up1"], w["w_down1"])
    d_wg1, d_wu1, d_wd1 = _ffn_wgrads("ffn1_dw", u3, dgate, dup, act, dh4)
    ex.grads("ffn1", {"w_gate1": d_wg1, "w_up1": d_wu1, "w_down1": d_wd1})

    d_ood, d_w_od_out = _out_proj_bwd("od_out_bwd", dh3, o_od, w["od_w_out"], ex)
    dqkv, dr0 = riding("band_attn_bwd", _band_bwd, qkv, r0, d_ood)
    du2 = _mm("qkv_bwd_x", dqkv, w["w_qkv"], kind="nt", grid=(S // TM, nt),
              a_spec=pl.BlockSpec((None, TM, D), lambda i, t: (t, i, 0)), b_spec=pl.BlockSpec((D, D), lambda i, t: (0, t)),
              o_spec=pl.BlockSpec((TM, D), lambda i, t: (i, 0)), out_shape=(S, D), out_dtype=F32, acc_shape=(TM, D))
    wide, per = D // MXU_W, nt * D // N_CHIPS // MXU_W
    piece = lambda r: pl.BlockSpec((None, TM, MXU_W), lambda j, k: ((per * j + r) // wide, k, (per * j + r) % wide))
    d_w_qkv = _mm("qkv_bwd_w", u2, dqkv, kind="tn", grid=(N_CHIPS, S // TM),
                  a_spec=pl.BlockSpec((TM, D), lambda j, k: (k, 0)), b_spec=[piece(r) for r in range(per)],
                  o_spec=pl.BlockSpec((None, D, per * MXU_W), lambda j, k: (j, 0, 0)),
                  out_shape=(N_CHIPS, D, per * MXU_W), out_dtype=BF16, acc_shape=(D, per * MXU_W))
    shard_cols = lambda a: jnp.moveaxis(a.reshape(a.shape[0], N_CHIPS, a.shape[1] // N_CHIPS), 1, 0)
    ex.grads("od", {"od_w_qkv": d_w_qkv, "od_w_out": d_w_od_out.reshape(N_CHIPS, D // N_CHIPS, D)})
    dh2, dg_mix1 = _rms_bwd("rms_mix1_bwd", du2, h2, g_mix[1:2], dh3, carry=ex.carry("rms_mix1_bwd"))
    d_rel = _bias_table_grad("rel_bias_grad", dr0.reshape(C_H, TOEP_W))[:, :2 * REL_CLIP + 1]

    dh1, dg_ffn0, u1, dgate, dup, act = _ffn_bwd("ffn0_bwd", dh2, h1, g_ffn[0:1], gate0, up0,
                                                 w["w_gate0"], w["w_up0"], w["w_down0"])
    d_wg0, d_wu0, d_wd0 = _ffn_wgrads("ffn0_dw", u1, dgate, dup, act, dh2)
    ex.grads("ffn0", {"w_gate0": d_wg0, "w_up0": d_wu0, "w_down0": d_wd0})

    d_oev, d_w_ev_out = _out_proj_bwd("ev_out_bwd", dh1, o_ev, w["ev_w_out"], ex)
    ex.grads("ev_out", {"ev_w_out": d_w_ev_out.reshape(N_CHIPS, D // N_CHIPS, D)})
    dqa, dka, dva = riding("mla_attn_bwd", _mla_bwd, qa, ka, va, o_a, lse, d_oev, 0)
    dqb, dkb, dvb = riding("sb_attn_bwd", _sb_bwd, proj, d_oev, MLA_H * MLA_V // LANES)
    dcq, dckv, dkr, d_w_uq, d_w_uk, d_w_uv, dg_cq, dg_ckv = _mla_prep_bwd(
        "mla_prep_bwd", dqa, dka, dva, proj, sm["ev_g_cq"], sm["ev_g_ckv"], w["w_uq"], w["w_uk"], w["w_uv"], cos_t, sin_t)
    dproj = jnp.concatenate([dcq, jnp.zeros((S, LANES), BF16), dckv, dqb, dkb, dvb, dkr], axis=1)
    d_w_in_p = _mm("proj_in_bwd_w", u0, dproj, kind="tn", grid=(1, S // TM),
                   a_spec=pl.BlockSpec((TM, D), lambda j, k: (k, 0)), b_spec=pl.BlockSpec((TM, P_IN), lambda j, k: (k, 0)),
                   o_spec=pl.BlockSpec((D, P_IN), lambda j, k: (0, 0)), out_shape=(D, P_IN), out_dtype=BF16,
                   acc_shape=(D, P_IN), carry=ex.carry("proj_in_bwd_w"))
    d_w_uq_std = d_w_uq.reshape(Q_LORA, MLA_H, LANES)[:, :, :MLA_NOPE + MLA_ROPE].reshape(Q_LORA, -1)
    d_w_ukv = jnp.concatenate([d_w_uk.reshape(KV_LORA, MLA_H, LANES)[:, :, :MLA_NOPE],
                               d_w_uv.reshape(KV_LORA, MLA_H, MLA_V)], axis=2).reshape(KV_LORA, -1)
    ex.grads("ev", {"ev_w_in": _w_in_sharded("w_in_sharded", d_w_in_p), "ev_w_uq": shard_cols(d_w_uq_std.astype(BF16)),
                    "ev_w_ukv": shard_cols(d_w_ukv.astype(BF16))})
    du0 = _mm("proj_in_bwd_x", dproj, w["w_in"], kind="nt", grid=(S // TM, 1, 1),
              a_spec=pl.BlockSpec((TM, P_IN), lambda i, j, k: (i, 0)), b_spec=pl.BlockSpec((D, P_IN), lambda i, j, k: (0, 0)),
              o_spec=pl.BlockSpec((TM, D), lambda i, j, k: (i, 0)), out_shape=(S, D), out_dtype=F32, acc_shape=None,
              carry=ex.carry("proj_in_bwd_x"))
    grad_x, dg_mix0 = _rms_bwd("rms_mix0_bwd", du0, x, g_mix[0:1], dh1)
    small = {
        "ev_g_cq": dg_cq, "ev_g_ckv": dg_ckv, "od_rel_bias": d_rel.reshape(1, C_H, 2 * REL_CLIP + 1),
        "g_mix": jnp.concatenate([dg_mix0, dg_mix1], axis=0), "g_ffn": jnp.concatenate([dg_ffn0, dg_ffn1], axis=0),
        "g_final": dg_final.reshape(D),
    }
    return loss, grad_x, small


BIG = ("ev_w_in", "ev_w_uq", "ev_w_ukv", "ev_w_out", "od_w_qkv", "od_w_out", "w_gate", "w_up", "w_down")
SMALL = ("ev_g_cq", "ev_g_ckv", "od_rel_bias", "g_mix", "g_ffn", "g_final")
WEIGHTS = ("ev_w_in", "ev_g_cq", "ev_w_uq", "ev_g_ckv", "ev_w_ukv", "ev_w_out", "od_w_qkv", "od_rel_bias", "od_w_out",
           "g_mix", "g_ffn", "w_gate", "w_up", "w_down", "g_final")
GRAD_PARTS = (("ev_w_in", "ev_w_in", 0), ("ev_w_uq", "ev_w_uq", 0), ("ev_w_ukv", "ev_w_ukv", 0),
              ("ev_w_out", "ev_w_out", 0), ("od_w_qkv", "od_w_qkv", 0), ("od_w_out", "od_w_out", 0),
              ("w_gate0", "w_gate", 0), ("w_gate1", "w_gate", 1), ("w_up0", "w_up", 0), ("w_up1", "w_up", 1),
              ("w_down0", "w_down", 0), ("w_down1", "w_down", 1))
PART_OF = {part: (param, layer) for part, param, layer in GRAD_PARTS}
SMALL_ROWS = 112
SMALL_SIZE = 384 + 256 + 16 * 513 + 2 * 1024 + 2 * 1024 + 1024
TRANSPOSED = ("w_gate", "w_up")
ADAMW_TRANSPOSED = ("ev_w_in", "ev_w_uq")


def _row_tile(rows, cap=512, sublanes=16):
    for t in range(min(rows, cap), 0, -1):
        if rows % t == 0 and t % sublanes == 0:
            return t
    return rows


def _cast_into_slot(name, w, layer, pos):
    _, rows, cols = w.shape
    tr = _row_tile(rows)

    def body(pos_ref, w_ref, o_ref):
        o_ref[...] = w_ref[...].astype(BF16)

    return pl.pallas_call(
        body, name=name,
        grid_spec=pltpu.PrefetchScalarGridSpec(
            num_scalar_prefetch=1, grid=(rows // tr,),
            in_specs=[pl.BlockSpec((None, tr, cols), lambda i, p: (layer, i, 0))],
            out_specs=pl.BlockSpec((None, tr, cols), lambda i, p: (p[0], i, 0))),
        out_shape=jax.ShapeDtypeStruct((N_CHIPS, rows, cols), BF16), compiler_params=_params("arbitrary"))(pos, w)


def _cast_many_into_slots(name, items, pos, carry):
    tiles = [_row_tile(w.shape[1]) for w, _ in items]
    counts = [w.shape[1] // tr for (w, _), tr in zip(items, tiles)]
    starts = [sum(counts[:t]) for t in range(len(items))]

    def body(pos_ref, *refs):
        i = pl.program_id(0)
        for t, (start, nb) in enumerate(zip(starts, counts)):
            @pl.when((i >= start) & (i < start + nb))
            def _(w_ref=refs[t], o_ref=refs[len(items) + t]):
                o_ref[...] = w_ref[...].astype(BF16)

    def block(start, nb):
        return lambda i: jnp.clip(i - start, 0, nb - 1)

    in_specs, out_specs, out_shape = [], [], []
    for (w, layer), tr, start, nb in zip(items, tiles, starts, counts):
        _, rows, cols = w.shape
        at = block(start, nb)
        in_specs.append(pl.BlockSpec((None, tr, cols), lambda i, p, at=at, layer=layer: (layer, at(i), 0)))
        out_specs.append(pl.BlockSpec((None, tr, cols), lambda i, p, at=at: (p[0], at(i), 0)))
        out_shape.append(jax.ShapeDtypeStruct((N_CHIPS, rows, cols), BF16))
    res, copies = _carrier_call(body, name=name, grid=(sum(counts),), in_specs=in_specs, out_specs=out_specs,
                                out_shape=out_shape, args=[w for w, _ in items], sem=("arbitrary",), carry=carry,
                                prefetch=(pos,))
    if carry is not None:
        carry.done(copies)
    return res


def _pair_sum(name, part, theirs, pos):
    _, half, cols = theirs.shape
    tr = _row_tile(half)
    nb = half // tr

    def body(pos_ref, a_ref, b_ref, o_ref):
        o_ref[...] = (a_ref[...].astype(F32) + b_ref[...].astype(F32)).astype(BF16)

    return pl.pallas_call(
        body, name=name,
        grid_spec=pltpu.PrefetchScalarGridSpec(
            num_scalar_prefetch=1, grid=(N_CHIPS, nb),
            in_specs=[pl.BlockSpec((None, tr, cols), lambda s, i, p: (s, p[1] * nb + i, 0)),
                      pl.BlockSpec((None, tr, cols), lambda s, i, p: (s, i, 0))],
            out_specs=pl.BlockSpec((None, tr, cols), lambda s, i, p: (s, i, 0))),
        out_shape=jax.ShapeDtypeStruct(theirs.shape, BF16),
        compiler_params=_params("arbitrary", "arbitrary"))(pos, part, theirs)


def _chip_sum(name, sums, got, pos, layer, full_shape, full=None):
    _, half, cols = sums.shape
    tr = _row_tile(half)
    nb = half // tr

    def body(pos_ref, s_ref, g_ref, *rest):
        out_ref = rest[-1]
        out_ref[...] = ((s_ref[...].astype(F32) + g_ref[0].astype(F32)) + g_ref[1].astype(F32)) + g_ref[2].astype(F32)

    in_specs = [pl.BlockSpec((None, tr, cols), lambda i, p: (p[0], i, 0)),
                pl.BlockSpec((3, tr, cols), lambda i, p: (0, i, 0))]
    args = [pos, sums, got]
    if full is not None:
        in_specs.append(ANY)
        args.append(full)
    return pl.pallas_call(
        body, name=name,
        grid_spec=pltpu.PrefetchScalarGridSpec(
            num_scalar_prefetch=1, grid=(nb,), in_specs=in_specs,
            out_specs=pl.BlockSpec((None, tr, cols), lambda i, p: (layer, p[1] * nb + i, 0))),
        out_shape=jax.ShapeDtypeStruct(full_shape, F32),
        input_output_aliases={3: 0} if full is not None else {},
        compiler_params=_params("arbitrary"))(*args)


def _all_reduce_small(name, packed):
    n_dev = 8

    def body(p_ref, o_ref, slots, send_sem, recv_sem):
        x, y, c, _ = _position()
        me = 4 * x + 2 * y + c

        def peer(k):
            return (1 - x if k & 4 else x, 1 - y if k & 2 else y, 1 - c if k & 1 else c)

        def logical(k):
            px, py, pc = peer(k)
            return 4 * px + 2 * py + pc

        slots[me] = p_ref[...]
        sends = [pltpu.make_async_remote_copy(
            src_ref=p_ref, dst_ref=slots.at[me], send_sem=send_sem.at[k], recv_sem=recv_sem.at[k],
            device_id=peer(k), device_id_type=MESH) for k in range(1, n_dev)]
        for cp in sends:
            cp.start()
        for k in range(1, n_dev):
            pltpu.make_async_remote_copy(
                src_ref=p_ref, dst_ref=slots.at[logical(k)], send_sem=send_sem.at[k], recv_sem=recv_sem.at[k],
                device_id=peer(k), device_id_type=MESH).wait_recv()
        for cp in sends:
            cp.wait_send()
        total = slots[0]
        for d in range(1, n_dev):
            total = total + slots[d]
        o_ref[...] = total

    vm = pl.BlockSpec(memory_space=pltpu.VMEM)
    return pl.pallas_call(
        body, name=name, in_specs=[vm], out_specs=vm, out_shape=jax.ShapeDtypeStruct(packed.shape, F32),
        scratch_shapes=[pltpu.VMEM((n_dev,) + packed.shape, F32), pltpu.SemaphoreType.DMA((n_dev,)),
                        pltpu.SemaphoreType.DMA((n_dev,))],
    )(packed)


def _adamw(name, w, g, m, v):
    rows, cols = w.shape
    tr = _row_tile(rows, sublanes=8)

    def body(w_ref, g_ref, m_ref, v_ref, d_ref, mo_ref, vo_ref):
        gv = g_ref[...]
        m_new = ADAM_B1 * m_ref[...] + (1.0 - ADAM_B1) * gv
        v_new = ADAM_B2 * v_ref[...] + (1.0 - ADAM_B2) * (gv * gv)
        m_hat = m_new / (1.0 - ADAM_B1 ** ADAM_STEP)
        v_hat = v_new / (1.0 - ADAM_B2 ** ADAM_STEP)
        d_ref[...] = -ADAM_LR * (m_hat / (jnp.sqrt(v_hat) + ADAM_EPS) + ADAM_WD * w_ref[...])
        mo_ref[...] = m_new
        vo_ref[...] = v_new

    spec = pl.BlockSpec((tr, cols), lambda i: (i, 0))
    shape = jax.ShapeDtypeStruct((rows, cols), F32)
    return pl.pallas_call(body, name=name, grid=(rows // tr,), in_specs=[spec] * 4, out_specs=[spec] * 3,
                          out_shape=[shape] * 3, compiler_params=_params("parallel"))(w, g, m, v)


def _pack_small(tree, extra=None):
    pieces = [tree[n].reshape(-1).astype(F32) for n in SMALL]
    if extra is not None:
        pieces.append(extra.reshape(1).astype(F32))
    flat = jnp.concatenate(pieces)
    return jnp.pad(flat, (0, SMALL_ROWS * LANES - flat.shape[0])).reshape(SMALL_ROWS, LANES)


def _unpack_small(packed, like):
    flat = packed.reshape(-1)
    out, off = {}, 0
    for n in SMALL:
        size = int(np.prod(like[n].shape))
        out[n] = flat[off:off + size].reshape(like[n].shape)
        off += size
    return out


def kernel(x, ev_w_in, ev_g_cq, ev_w_uq, ev_g_ckv, ev_w_ukv, ev_w_out, od_w_qkv, od_rel_bias, od_w_out, g_mix, g_ffn, w_gate, w_up, w_down, g_final, loss_target, m_ev_w_in, m_ev_g_cq, m_ev_w_uq, m_ev_g_ckv, m_ev_w_ukv, m_ev_w_out, m_od_w_qkv, m_od_rel_bias, m_od_w_out, m_g_mix, m_g_ffn, m_w_gate, m_w_up, m_w_down, m_g_final, v_ev_w_in, v_ev_g_cq, v_ev_w_uq, v_ev_g_ckv, v_ev_w_ukv, v_ev_w_out, v_od_w_qkv, v_od_rel_bias, v_od_w_out, v_g_mix, v_g_ffn, v_w_gate, v_w_up, v_w_down, v_g_final):
    w = dict(ev_w_in=ev_w_in, ev_g_cq=ev_g_cq, ev_w_uq=ev_w_uq, ev_g_ckv=ev_g_ckv, ev_w_ukv=ev_w_ukv, ev_w_out=ev_w_out,
             od_w_qkv=od_w_qkv, od_rel_bias=od_rel_bias, od_w_out=od_w_out, g_mix=g_mix, g_ffn=g_ffn, w_gate=w_gate,
             w_up=w_up, w_down=w_down, g_final=g_final)
    m = dict(ev_w_in=m_ev_w_in, ev_g_cq=m_ev_g_cq, ev_w_uq=m_ev_w_uq, ev_g_ckv=m_ev_g_ckv, ev_w_ukv=m_ev_w_ukv,
             ev_w_out=m_ev_w_out, od_w_qkv=m_od_w_qkv, od_rel_bias=m_od_rel_bias, od_w_out=m_od_w_out, g_mix=m_g_mix,
             g_ffn=m_g_ffn, w_gate=m_w_gate, w_up=m_w_up, w_down=m_w_down, g_final=m_g_final)
    v = dict(ev_w_in=v_ev_w_in, ev_g_cq=v_ev_g_cq, ev_w_uq=v_ev_w_uq, ev_g_ckv=v_ev_g_ckv, ev_w_ukv=v_ev_w_ukv,
             ev_w_out=v_ev_w_out, od_w_qkv=v_od_w_qkv, od_rel_bias=v_od_rel_bias, od_w_out=v_od_w_out, g_mix=v_g_mix,
             g_ffn=v_g_ffn, w_gate=v_w_gate, w_up=v_w_up, w_down=v_w_down, g_final=v_g_final)
    flat2d = lambda a: a.reshape(-1, a.shape[-1])
    for tree in (w, m, v):
        for n in TRANSPOSED:
            tree[n] = jnp.swapaxes(tree[n], 1, 2)

    pos = jnp.stack([2 * lax.axis_index("x") + lax.axis_index("y"), lax.axis_index("c")]).astype(jnp.int32)

    slots = {part: _cast_into_slot("cast_" + part, w[n], layer, pos) for part, n, layer in GRAD_PARTS
             if part in FIRST_WEIGHTS}
    rest = [(part, n, layer) for part, n, layer in GRAD_PARTS if part not in FIRST_WEIGHTS]

    def cast_rest(carry):
        return dict(zip([part for part, _, _ in rest],
                        _cast_many_into_slots("cast_rest", [(w[n], layer) for _, n, layer in rest], pos, carry)))

    ex = _Exchanges(slots, pos, {n: w[n].shape for n in BIG}, cast_rest)

    loss_local, grad_x, small = _local_step(x[0], loss_target[0], {n: w[n] for n in SMALL}, ex)

    grads = ex.finish()
    small_sum = _all_reduce_small("small_sum", _pack_small(small, loss_local[0, 0]))
    grads.update(_unpack_small(small_sum, w))

    delta, new_m, new_v = {}, {}, {}
    for n in BIG:
        turn = (lambda a: jnp.swapaxes(a, 1, 2)) if n in ADAMW_TRANSPOSED else (lambda a: a)
        shape = turn(w[n]).shape
        outs = _adamw("adamw_" + n, *(flat2d(turn(a)) for a in (w[n], grads[n], m[n], v[n])))
        delta[n], new_m[n], new_v[n] = (turn(o.reshape(shape)) for o in outs)
    d_, m_, v_ = _adamw("adamw_small", _pack_small(w), small_sum, _pack_small(m), _pack_small(v))
    delta.update(_unpack_small(d_, w))
    new_m.update(_unpack_small(m_, w))
    new_v.update(_unpack_small(v_, w))
    for tree in (grads, delta, new_m, new_v):
        for n in TRANSPOSED:
            tree[n] = jnp.swapaxes(tree[n], 1, 2)

    loss = small_sum.reshape(-1)[SMALL_SIZE]
    return (loss, grad_x[None], *[grads[n] for n in WEIGHTS], *[delta[n] for n in WEIGHTS],
            *[new_m[n] for n in WEIGHTS], *[new_v[n] for n in WEIGHTS])
```

```python
import functools

import jax
import jax.numpy as jnp
import numpy as np
from jax import lax
from jax.experimental import pallas as pl
from jax.experimental.pallas import tpu as pltpu

F32 = jnp.float32
BF16 = jnp.bfloat16

S = 2048
D = 1024
CHUNK = 64
MLA_H, MLA_NOPE, MLA_ROPE, MLA_V = 8, 64, 32, 64
Q_LORA, KV_LORA = 384, 256
ROPE_THETA = 10000.0
SB_H, SB_DIM = 8, 64
C_H, C_DIM = 16, 64
LEFT_CHUNKS = 8
REL_CLIP = 256
D_FF = 2816
EVEN_IN = 2208
RMS_EPS = 1e-6
ADAM_LR, ADAM_B1, ADAM_B2, ADAM_EPS, ADAM_WD, ADAM_STEP = 0.001, 0.9, 0.999, 1e-08, 0.01, 10

N_CHIPS = 4
FF_SHARD = D_FF // N_CHIPS
SCALE_A = (MLA_NOPE + MLA_ROPE) ** -0.5
SCALE_B = SB_DIM ** -0.5
SCALE_C = C_DIM ** -0.5
NEG = -1e30
LOG2_E = 1.4426950408889634

LANES = 128
MXU_W = 256
VMEM_LIMIT_BYTES = 56 * 1024 * 1024
TM = 512
QB = 512
BQ = 256

P_CQ, P_CKV, P_QB, P_KB, P_VB, P_KR = 0, 512, 768, 1280, 1792, 2304
P_IN = 2432
KR_LANE = 64
BAND_W = BQ + LEFT_CHUNKS * CHUNK
BAND_PAD = 512
TOEP_W = 1024


def _params(*sem):
    return pltpu.CompilerParams(dimension_semantics=sem, vmem_limit_bytes=VMEM_LIMIT_BYTES)


MESH = pl.DeviceIdType.MESH
ANY = pl.BlockSpec(memory_space=pl.ANY)


def _position():
    x, y, c = lax.axis_index("x"), lax.axis_index("y"), lax.axis_index("c")
    other_chips = [(1 - x, y), (x, 1 - y), (1 - x, 1 - y)]
    return x, y, c, other_chips


def _half_rows(c, half):
    return pl.ds(pl.multiple_of(c * half, 16), half)


def _remote(ref_src, ref_dst, send, recv, k, device):
    return pltpu.make_async_remote_copy(src_ref=ref_src, dst_ref=ref_dst, send_sem=send.at[k], recv_sem=recv.at[k],
                                        device_id=device, device_id_type=MESH)


class _Carry:
    def __init__(self):
        self.operands, self.aliased, self.fresh = [], [], []
        self.n_sems = 0
        self.starts, self.finishes, self.on_done = [], [], []

    def operand(self, arr, aliased):
        for i, a in enumerate(self.operands):
            if a is arr:
                return i
        self.operands.append(arr)
        self.aliased.append(aliased)
        return len(self.operands) - 1

    def result(self, shape, dtype):
        self.fresh.append(jax.ShapeDtypeStruct(shape, dtype))
        return len(self.fresh) - 1

    def sems(self, k):
        base = self.n_sems
        self.n_sems += k
        return base

    def done(self, results):
        aliased, fresh = results
        for f in self.on_done:
            f(aliased, fresh)


def _carrier_call(body, *, name, grid, in_specs, out_specs, out_shape, args, sem, scratch_shapes=(), carry=None,
                  prefetch=()):
    in_specs, out_specs, out_shape, scratch = list(in_specs), list(out_specs), list(out_shape), list(scratch_shapes)
    n_pre = len(prefetch)

    def call(kernel, in_specs, out_specs, out_shape, scratch, aliases, sem):
        return pl.pallas_call(
            kernel, name=name, out_shape=out_shape, input_output_aliases=aliases, compiler_params=_params(*sem),
            grid_spec=pltpu.PrefetchScalarGridSpec(num_scalar_prefetch=n_pre, grid=grid, in_specs=in_specs,
                                                   out_specs=out_specs, scratch_shapes=scratch))

    if carry is None:
        return list(call(body, in_specs, out_specs, out_shape, scratch, {}, sem)(*prefetch, *args)), None
    ops = carry.operands
    alias_idx = [i for i, a in enumerate(carry.aliased) if a]
    c_shapes = [jax.ShapeDtypeStruct(ops[i].shape, ops[i].dtype) for i in alias_idx] + carry.fresh
    n_in, n_out, n_scr = len(args), len(out_shape), len(scratch)

    def wrapped(*refs):
        pre, refs = refs[:n_pre], refs[n_pre:]
        ins, c_ins = refs[:n_in], refs[n_in:n_in + len(ops)]
        o0 = n_in + len(ops)
        outs, c_outs = refs[o0:o0 + n_out], refs[o0 + n_out:o0 + n_out + len(c_shapes)]
        s0 = o0 + n_out + len(c_shapes)
        scr, send, recv = refs[s0:s0 + n_scr], refs[s0 + n_scr], refs[s0 + n_scr + 1]
        use = list(c_ins)
        for k, i in enumerate(alias_idx):
            use[i] = c_outs[k]
        fresh = c_outs[len(alias_idx):]

        def run(steps):
            for step in steps:
                step(use, fresh, send, recv)

        if not grid:
            run(carry.starts)
            if body is not None:
                body(*pre, *ins, *outs, *scr)
            run(carry.finishes)
            return
        ids = [pl.program_id(a) for a in range(len(grid))]
        first = functools.reduce(jnp.logical_and, [i == 0 for i in ids])
        last = functools.reduce(jnp.logical_and, [i == g - 1 for i, g in zip(ids, grid)])

        @pl.when(first)
        def _():
            run(carry.starts)

        body(*pre, *ins, *outs, *scr)

        @pl.when(last)
        def _():
            run(carry.finishes)

    res = call(wrapped, in_specs + [ANY] * len(ops), out_specs + [ANY] * len(c_shapes), out_shape + c_shapes,
               scratch + [pltpu.SemaphoreType.DMA((carry.n_sems,)), pltpu.SemaphoreType.DMA((carry.n_sems,))],
               {n_pre + n_in + i: n_out + k for k, i in enumerate(alias_idx)},
               ("arbitrary",) * len(grid))(*prefetch, *args, *ops)
    res = list(res)
    c_res = res[n_out:]
    return res[:n_out], ({i: c_res[k] for k, i in enumerate(alias_idx)}, c_res[len(alias_idx):])


_DIMS = {"nn": (((1,), (0,)), ((), ())), "nt": (((1,), (1,)), ((), ())), "tn": (((0,), (0,)), ((), ()))}


def _dot(a, b, kind="nn"):
    return lax.dot_general(a, b, _DIMS[kind], preferred_element_type=F32)


def _iota(shape, dim):
    return lax.broadcasted_iota(jnp.int32, shape, dim)


def _sigmoid(x):
    return 1.0 / (1.0 + jnp.exp(-x))


def _split_dot(x, tri):
    hi = x.astype(BF16)
    lo = (x - hi.astype(F32)).astype(BF16)
    both = _dot(jnp.concatenate([hi, lo], axis=0), tri)
    return both[:x.shape[0]] + both[x.shape[0]:]


def _running_sum(x, tri, reverse):
    n = x.shape[1] // MXU_W
    blocks = [x[:, b * MXU_W:(b + 1) * MXU_W] for b in range(n)]
    out = [None] * n
    carry = None
    for b in (range(n - 1, -1, -1) if reverse else range(n)):
        part = _split_dot(blocks[b], tri)
        out[b] = part if carry is None else part + carry
        total = jnp.sum(blocks[b], axis=-1, keepdims=True)
        carry = total if carry is None else carry + total
    return (jnp.concatenate(out, axis=1) if n > 1 else out[0]), carry


def _mm(name, a, b, *, kind, grid, a_spec, b_spec, o_spec, out_shape, out_dtype, acc_shape, resid=None, r_spec=None,
        carry=None):
    nk = grid[-1]
    has_r = resid is not None
    b_specs = list(b_spec) if isinstance(b_spec, (tuple, list)) else [b_spec]
    nb = len(b_specs)

    def body(*refs):
        a_ref, b_refs = refs[0], refs[1:1 + nb]
        r_ref = refs[1 + nb] if has_r else None
        o_ref = refs[1 + nb + has_r]
        b_val = b_refs[0][...] if nb == 1 else jnp.concatenate([r[...] for r in b_refs], axis=1)
        part = _dot(a_ref[...].astype(BF16), b_val.astype(BF16), kind)

        def finish(total):
            if has_r:
                total = total + r_ref[...].astype(F32)
            o_ref[...] = total.astype(out_dtype)

        if nk == 1:
            finish(part)
        else:
            acc_ref = refs[2 + nb + has_r]
            k = pl.program_id(len(grid) - 1)

            @pl.when(k == 0)
            def _():
                acc_ref[...] = part

            @pl.when(k > 0)
            def _():
                acc_ref[...] += part

            @pl.when(k == nk - 1)
            def _():
                finish(acc_ref[...])

    in_specs = [a_spec] + b_specs + ([r_spec] if has_r else [])
    args = (a,) + (b,) * nb + ((resid,) if has_r else ())
    sem = ("parallel",) * (len(grid) - 1) + ("arbitrary",)
    res, copies = _carrier_call(
        body, name=name, grid=grid, in_specs=in_specs, out_specs=[o_spec],
        out_shape=[jax.ShapeDtypeStruct(out_shape, out_dtype)],
        scratch_shapes=[pltpu.VMEM(acc_shape, F32)] if nk > 1 else [], args=args, sem=sem, carry=carry)
    if carry is not None:
        carry.done(copies)
    return res[0]


def _rms_fwd(name, x, g, col_block=0):
    c = g.shape[1]

    def body(x_ref, g_ref, u_ref):
        xv = x_ref[...]
        r = lax.rsqrt(jnp.mean(xv * xv, axis=-1, keepdims=True) + RMS_EPS)
        u_ref[...] = (xv * r * g_ref[...]).astype(BF16)

    return pl.pallas_call(
        body, name=name, grid=(S // TM,),
        in_specs=[pl.BlockSpec((TM, c), lambda i: (i, col_block)), pl.BlockSpec((1, c), lambda i: (0, 0))],
        out_specs=pl.BlockSpec((TM, c), lambda i: (i, 0)),
        out_shape=jax.ShapeDtypeStruct((S, c), BF16),
        compiler_params=_params("parallel"),
    )(x, g)


def _rms_bwd(name, dy, x, g, resid, carry=None):
    def body(dy_ref, x_ref, g_ref, r_ref, dx_ref, dg_ref):
        i = pl.program_id(0)
        xv = x_ref[...]
        r = lax.rsqrt(jnp.mean(xv * xv, axis=-1, keepdims=True) + RMS_EPS)
        xh = xv * r
        dyv = dy_ref[...]
        dxh = dyv * g_ref[...]
        dx_ref[...] = r_ref[...] + r * (dxh - xh * jnp.mean(dxh * xh, axis=-1, keepdims=True))
        part = jnp.sum(dyv * xh, axis=0, keepdims=True)

        @pl.when(i == 0)
        def _():
            dg_ref[...] = part

        @pl.when(i > 0)
        def _():
            dg_ref[...] += part

    row = pl.BlockSpec((TM, D), lambda i: (i, 0))
    vec = pl.BlockSpec((1, D), lambda i: (0, 0))
    res, copies = _carrier_call(
        body, name=name, grid=(S // TM,), in_specs=[row, row, vec, row], out_specs=[row, vec],
        out_shape=[jax.ShapeDtypeStruct((S, D), F32), jax.ShapeDtypeStruct((1, D), F32)],
        args=(dy, x, g, resid), sem=("arbitrary",), carry=carry)
    if carry is not None:
        carry.done(copies)
    return res


def _loss_bwd(name, h, g, tgt):
    def body(h_ref, g_ref, t_ref, loss_ref, dh_ref, dg_ref):
        i = pl.program_id(0)
        xv = h_ref[...]
        gv = g_ref[...]
        r = lax.rsqrt(jnp.mean(xv * xv, axis=-1, keepdims=True) + RMS_EPS)
        xh = xv * r
        diff = xh * gv - t_ref[...]
        part_loss = 0.5 * jnp.sum(jnp.sum(diff * diff, axis=-1, keepdims=True) * (1.0 / D), axis=0, keepdims=True)
        dy = diff * (1.0 / D)
        dxh = dy * gv
        dh_ref[...] = r * (dxh - xh * jnp.mean(dxh * xh, axis=-1, keepdims=True))
        part_g = jnp.sum(dy * xh, axis=0, keepdims=True)

        @pl.when(i == 0)
        def _():
            dg_ref[...] = part_g
            loss_ref[...] = jnp.broadcast_to(part_loss, (1, LANES))

        @pl.when(i > 0)
        def _():
            dg_ref[...] += part_g
            loss_ref[...] += jnp.broadcast_to(part_loss, (1, LANES))

    row = pl.BlockSpec((TM, D), lambda i: (i, 0))
    vec = pl.BlockSpec((1, D), lambda i: (0, 0))
    return pl.pallas_call(
        body, name=name, grid=(S // TM,), in_specs=[row, vec, row],
        out_specs=[pl.BlockSpec((1, LANES), lambda i: (0, 0)), row, vec],
        out_shape=[jax.ShapeDtypeStruct((1, LANES), F32), jax.ShapeDtypeStruct((S, D), F32),
                   jax.ShapeDtypeStruct((1, D), F32)],
        compiler_params=_params("arbitrary"),
    )(h, g, tgt)


def _ffn_fwd(name, h, g, wg, wu, wd, carry=None):
    def body(h_ref, g_ref, wg_ref, wu_ref, wd_ref, o_ref, gate_ref, up_ref, u_scr):
        s = pl.program_id(1)

        @pl.when(s == 0)
        def _():
            xv = h_ref[...]
            r = lax.rsqrt(jnp.mean(xv * xv, axis=-1, keepdims=True) + RMS_EPS)
            u_scr[...] = (xv * r * g_ref[...]).astype(BF16)
            o_ref[...] = xv

        u = u_scr[...]
        gate = _dot(u, wg_ref[...], "nt")
        up = _dot(u, wu_ref[...], "nt")
        act = gate * _sigmoid(gate) * up
        o_ref[...] += _dot(act.astype(BF16), wd_ref[...])
        gate_ref[...] = gate.astype(BF16)
        up_ref[...] = up.astype(BF16)

    row = pl.BlockSpec((TM, D), lambda i, s: (i, 0))
    hid = pl.BlockSpec((None, TM, FF_SHARD), lambda i, s: (s, i, 0))
    return _carrier_call(
        body, name=name, grid=(S // TM, N_CHIPS),
        in_specs=[row, pl.BlockSpec((1, D), lambda i, s: (0, 0))]
        + [pl.BlockSpec((None, FF_SHARD, D), lambda i, s: (s, 0, 0))] * 3,
        out_specs=[row, hid, hid],
        out_shape=[jax.ShapeDtypeStruct((S, D), F32), jax.ShapeDtypeStruct((N_CHIPS, S, FF_SHARD), BF16),
                   jax.ShapeDtypeStruct((N_CHIPS, S, FF_SHARD), BF16)],
        scratch_shapes=[pltpu.VMEM((TM, D), BF16)], args=(h, g, wg, wu, wd), sem=("parallel", "arbitrary"), carry=carry)


def _ffn_bwd(name, dh, h, g, gate, up, wg, wu, wd):
    def body(dh_ref, h_ref, g_ref, gate_ref, up_ref, wg_ref, wu_ref, wd_ref,
             dhin_ref, dg_ref, u_ref, dgate_ref, dup_ref, act_ref, dhb_scr, du_scr):
        i = pl.program_id(0)
        s = pl.program_id(1)

        @pl.when(s == 0)
        def _():
            xv = h_ref[...]
            r = lax.rsqrt(jnp.mean(xv * xv, axis=-1, keepdims=True) + RMS_EPS)
            u_ref[...] = (xv * r * g_ref[...]).astype(BF16)
            dhb_scr[...] = dh_ref[...].astype(BF16)
            du_scr[...] = jnp.zeros_like(du_scr)

        dact = _dot(dhb_scr[...], wd_ref[...], "nt")
        gv = gate_ref[...].astype(F32)
        uv = up_ref[...].astype(F32)
        sig = _sigmoid(gv)
        sil = gv * sig
        dup = dact * sil
        dgate = dact * uv * (sig * (1.0 + gv * (1.0 - sig)))
        dgb = dgate.astype(BF16)
        dub = dup.astype(BF16)
        act_ref[...] = (sil * uv).astype(BF16)
        dgate_ref[...] = dgb
        dup_ref[...] = dub
        du_scr[...] += _dot(dgb, wg_ref[...]) + _dot(dub, wu_ref[...])

        @pl.when(s == N_CHIPS - 1)
        def _():
            xv = h_ref[...]
            r = lax.rsqrt(jnp.mean(xv * xv, axis=-1, keepdims=True) + RMS_EPS)
            xh = xv * r
            du = du_scr[...]
            dxh = du * g_ref[...]
            dhin_ref[...] = dh_ref[...] + r * (dxh - xh * jnp.mean(dxh * xh, axis=-1, keepdims=True))
            part = jnp.sum(du * xh, axis=0, keepdims=True)

            @pl.when(i == 0)
            def _():
                dg_ref[...] = part

            @pl.when(i > 0)
            def _():
                dg_ref[...] += part

    row = pl.BlockSpec((TM, D), lambda i, s: (i, 0))
    vec = pl.BlockSpec((1, D), lambda i, s: (0, 0))
    hid = pl.BlockSpec((None, TM, FF_SHARD), lambda i, s: (s, i, 0))
    hid_shape = jax.ShapeDtypeStruct((N_CHIPS, S, FF_SHARD), BF16)
    return pl.pallas_call(
        body, name=name, grid=(S // TM, N_CHIPS),
        in_specs=[row, row, vec, hid, hid] + [pl.BlockSpec((None, FF_SHARD, D), lambda i, s: (s, 0, 0))] * 3,
        out_specs=[row, vec, row, hid, hid, hid],
        out_shape=[jax.ShapeDtypeStruct((S, D), F32), jax.ShapeDtypeStruct((1, D), F32),
                   jax.ShapeDtypeStruct((S, D), BF16), hid_shape, hid_shape, hid_shape],
        scratch_shapes=[pltpu.VMEM((TM, D), BF16), pltpu.VMEM((TM, D), F32)],
        compiler_params=_params("arbitrary", "arbitrary"),
    )(dh, h, g, gate, up, wg, wu, wd)


def _ffn_wgrads(name, u, dgate, dup, act, dh):
    nk = S // TM

    def body(u_ref, dh_ref, dgate_ref, dup_ref, act_ref, dg_ref, du_ref, dd_ref, acc_g, acc_u, acc_d):
        k = pl.program_id(1)
        u = u_ref[...]
        parts = (_dot(dgate_ref[...], u, "tn"), _dot(dup_ref[...], u, "tn"),
                 _dot(act_ref[...], dh_ref[...].astype(BF16), "tn"))
        accs = (acc_g, acc_u, acc_d)

        @pl.when(k == 0)
        def _():
            for acc, part in zip(accs, parts):
                acc[...] = part

        @pl.when(k > 0)
        def _():
            for acc, part in zip(accs, parts):
                acc[...] += part

        @pl.when(k == nk - 1)
        def _():
            for out, acc in zip((dg_ref, du_ref, dd_ref), accs):
                out[...] = acc[...].astype(BF16)

    tok = pl.BlockSpec((TM, D), lambda s, k: (k, 0))
    hid = pl.BlockSpec((None, TM, FF_SHARD), lambda s, k: (s, k, 0))
    out = pl.BlockSpec((None, FF_SHARD, D), lambda s, k: (s, 0, 0))
    shape = jax.ShapeDtypeStruct((N_CHIPS, FF_SHARD, D), BF16)
    return pl.pallas_call(
        body, name=name, grid=(N_CHIPS, nk), in_specs=[tok, tok, hid, hid, hid], out_specs=[out, out, out],
        out_shape=[shape, shape, shape], scratch_shapes=[pltpu.VMEM((FF_SHARD, D), F32)] * 3,
        compiler_params=_params("parallel", "arbitrary"))(u, dh, dgate, dup, act)


def _rope_tables():
    pos = jnp.arange(S, dtype=F32)
    inv = ROPE_THETA ** (-jnp.arange(0, MLA_ROPE, 2, dtype=F32) / MLA_ROPE)
    ang = pos[:, None] * inv[None, :]
    half = MLA_ROPE // 2
    cos = jnp.cos(ang)
    sin = jnp.sin(ang)
    one = jnp.ones((S, KR_LANE), F32)
    zero = jnp.zeros((S, KR_LANE), F32)
    tail_one = jnp.ones((S, LANES - KR_LANE - MLA_ROPE), F32)
    tail_zero = jnp.zeros((S, LANES - KR_LANE - MLA_ROPE), F32)
    cos_t = jnp.concatenate([one, cos, cos, tail_one], axis=1)
    sin_t = jnp.concatenate([zero, -sin, sin, tail_zero], axis=1)
    assert cos_t.shape == (S, LANES) and half * 2 == MLA_ROPE
    return cos_t, sin_t


def _rope(x, cos_t, sin_t, sign):
    n = x.shape[1] // LANES
    half = MLA_ROPE // 2
    lane = _iota(x.shape, 1) & (LANES - 1)
    first = (lane >= KR_LANE) & (lane < KR_LANE + half)
    swapped = jnp.where(first, pltpu.roll(x, x.shape[1] - half, 1), pltpu.roll(x, half, 1))
    c = jnp.tile(cos_t, (1, n)) if n > 1 else cos_t
    s = jnp.tile(sin_t, (1, n)) if n > 1 else sin_t
    return x * c + swapped * (s * sign)


def _mla_prep_fwd(name, proj, g_cq, g_ckv, w_uq, w_uk, w_uv, cos_t, sin_t):
    nh = MLA_H * LANES

    def body(cq_ref, ckv_ref, kr_ref, gq_ref, gkv_ref, wq_ref, wk_ref, wv_ref, cos_ref, sin_ref,
             qa_ref, ka_ref, va_ref):
        cos_v, sin_v = cos_ref[...], sin_ref[...]
        cq = cq_ref[...]
        r = lax.rsqrt(jnp.mean(cq * cq, axis=-1, keepdims=True) + RMS_EPS)
        cqn = (cq * r * gq_ref[...]).astype(BF16)
        qa_ref[...] = _rope(_dot(cqn, wq_ref[...]), cos_v, sin_v, 1.0).astype(BF16)
        ckv = ckv_ref[...]
        r = lax.rsqrt(jnp.mean(ckv * ckv, axis=-1, keepdims=True) + RMS_EPS)
        ckvn = (ckv * r * gkv_ref[...]).astype(BF16)
        lane = _iota((TM, LANES), 1)
        rot = (lane >= KR_LANE) & (lane < KR_LANE + MLA_ROPE)
        kr = jnp.where(rot, _rope(kr_ref[...], cos_v, sin_v, 1.0), 0.0)
        ka_ref[...] = (_dot(ckvn, wk_ref[...]) + jnp.tile(kr, (1, MLA_H))).astype(BF16)
        va_ref[...] = _dot(ckvn, wv_ref[...]).astype(BF16)

    full = lambda shape: pl.BlockSpec(shape, lambda i: (0, 0))
    return pl.pallas_call(
        body, name=name, grid=(S // TM,),
        in_specs=[pl.BlockSpec((TM, Q_LORA), lambda i: (i, P_CQ // Q_LORA)),
                  pl.BlockSpec((TM, KV_LORA), lambda i: (i, P_CKV // KV_LORA)),
                  pl.BlockSpec((TM, LANES), lambda i: (i, P_KR // LANES)),
                  full((1, Q_LORA)), full((1, KV_LORA)), full((Q_LORA, nh)), full((KV_LORA, nh)),
                  full((KV_LORA, MLA_H * MLA_V)),
                  pl.BlockSpec((TM, LANES), lambda i: (i, 0)), pl.BlockSpec((TM, LANES), lambda i: (i, 0))],
        out_specs=[pl.BlockSpec((TM, nh), lambda i: (i, 0)), pl.BlockSpec((TM, nh), lambda i: (i, 0)),
                   pl.BlockSpec((TM, MLA_H * MLA_V), lambda i: (i, 0))],
        out_shape=[jax.ShapeDtypeStruct((S, nh), BF16), jax.ShapeDtypeStruct((S, nh), BF16),
                   jax.ShapeDtypeStruct((S, MLA_H * MLA_V), BF16)],
        compiler_params=_params("parallel"),
    )(proj, proj, proj, g_cq, g_ckv, w_uq, w_uk, w_uv, cos_t, sin_t)


def _mla_prep_bwd(name, dqa, dka, dva, proj, g_cq, g_ckv, w_uq, w_uk, w_uv, cos_t, sin_t):
    nh = MLA_H * LANES

    def body(dqa_ref, dka_ref, dva_ref, cq_ref, ckv_ref, gq_ref, gkv_ref, wq_ref, wk_ref, wv_ref, cos_ref, sin_ref,
             dcq_ref, dckv_ref, dkr_ref, dwq_ref, dwk_ref, dwv_ref, dgq_ref, dgkv_ref):
        i = pl.program_id(0)
        cos_v, sin_v = cos_ref[...], sin_ref[...]

        def norm_bwd(x, g, dn):
            r = lax.rsqrt(jnp.mean(x * x, axis=-1, keepdims=True) + RMS_EPS)
            xh = x * r
            dxh = dn * g
            dx = r * (dxh - xh * jnp.mean(dxh * xh, axis=-1, keepdims=True))
            return dx, jnp.sum(dn * xh, axis=0, keepdims=True), (xh * g).astype(BF16)

        dq = _rope(dqa_ref[...], cos_v, sin_v, -1.0).astype(BF16)
        dcqn = _dot(dq, wq_ref[...], "nt")
        dcq, dgq, cqn = norm_bwd(cq_ref[...], gq_ref[...], dcqn)
        dcq_ref[...] = dcq.astype(BF16)
        dwq = _dot(cqn, dq, "tn")

        dka = dka_ref[...]
        dkab = dka.astype(BF16)
        dvab = dva_ref[...].astype(BF16)
        dckvn = _dot(dkab, wk_ref[...], "nt") + _dot(dvab, wv_ref[...], "nt")
        dckv, dgkv, ckvn = norm_bwd(ckv_ref[...], gkv_ref[...], dckvn)
        dckv_ref[...] = dckv.astype(BF16)
        dwk = _dot(ckvn, dkab, "tn")
        dwv = _dot(ckvn, dvab, "tn")

        fold = dka[:, 0:LANES]
        for hh in range(1, MLA_H):
            fold = fold + dka[:, hh * LANES:(hh + 1) * LANES]
        lane = _iota((TM, LANES), 1)
        rot = (lane >= KR_LANE) & (lane < KR_LANE + MLA_ROPE)
        dkr = _rope(jnp.where(rot, fold, 0.0), cos_v, sin_v, -1.0)
        dkr_ref[...] = jnp.where(rot, dkr, 0.0).astype(BF16)

        @pl.when(i == 0)
        def _():
            dwq_ref[...] = dwq
            dwk_ref[...] = dwk
            dwv_ref[...] = dwv
            dgq_ref[...] = dgq
            dgkv_ref[...] = dgkv

        @pl.when(i > 0)
        def _():
            dwq_ref[...] += dwq
            dwk_ref[...] += dwk
            dwv_ref[...] += dwv
            dgq_ref[...] += dgq
            dgkv_ref[...] += dgkv

    full = lambda shape: pl.BlockSpec(shape, lambda i: (0, 0))
    rows = lambda c: pl.BlockSpec((TM, c), lambda i: (i, 0))
    nv = MLA_H * MLA_V
    return pl.pallas_call(
        body, name=name, grid=(S // TM,),
        in_specs=[rows(nh), rows(nh), rows(nv),
                  pl.BlockSpec((TM, Q_LORA), lambda i: (i, P_CQ // Q_LORA)),
                  pl.BlockSpec((TM, KV_LORA), lambda i: (i, P_CKV // KV_LORA)),
                  full((1, Q_LORA)), full((1, KV_LORA)), full((Q_LORA, nh)), full((KV_LORA, nh)), full((KV_LORA, nv)),
                  rows(LANES), rows(LANES)],
        out_specs=[rows(Q_LORA), rows(KV_LORA), rows(LANES), full((Q_LORA, nh)), full((KV_LORA, nh)),
                   full((KV_LORA, nv)), full((1, Q_LORA)), full((1, KV_LORA))],
        out_shape=[jax.ShapeDtypeStruct((S, Q_LORA), BF16), jax.ShapeDtypeStruct((S, KV_LORA), BF16),
                   jax.ShapeDtypeStruct((S, LANES), BF16), jax.ShapeDtypeStruct((Q_LORA, nh), F32),
                   jax.ShapeDtypeStruct((KV_LORA, nh), F32), jax.ShapeDtypeStruct((KV_LORA, nv), F32),
                   jax.ShapeDtypeStruct((1, Q_LORA), F32), jax.ShapeDtypeStruct((1, KV_LORA), F32)],
        compiler_params=_params("arbitrary"),
    )(dqa, dka, dva, proj, proj, g_cq, g_ckv, w_uq, w_uk, w_uv, cos_t, sin_t)


def _head_masks(dtype):
    lane = _iota((1, LANES), 1)
    return (lane < 64).astype(dtype), (lane >= 64).astype(dtype)


def _mla_fwd(name, qa, ka, va, carry=None):
    def body(q_ref, k_ref, v_ref, o_ref, lse_ref):
        m0b, m1b = _head_masks(BF16)
        lane = _iota((QB, LANES), 1)
        left = lane < 64

        def qblock(i, _):
            r0 = pl.multiple_of(i * QB, QB)
            qs = [q_ref[pl.ds(r0, QB), hh * LANES:(hh + 1) * LANES] for hh in range(2)]
            rowc = lax.shift_right_logical(r0 + _iota((QB, QB), 0), 6)

            def kv(kb, carry):
                ms, ls, acc = carry
                c0 = pl.multiple_of(kb * QB, QB)
                v = v_ref[pl.ds(c0, QB), :]
                ok = lax.shift_right_logical(c0 + _iota((QB, QB), 1), 6) <= rowc
                new_m, new_l, alphas = [], [], []
                pv = None
                for hh in range(2):
                    k = k_ref[pl.ds(c0, QB), hh * LANES:(hh + 1) * LANES]
                    s = jnp.where(ok, _dot(qs[hh], k, "nt") * (SCALE_A * LOG2_E), NEG)
                    mn = jnp.maximum(ms[hh], jnp.max(s, axis=-1, keepdims=True))
                    p = jnp.exp2(s - mn)
                    a = jnp.exp2(ms[hh] - mn)
                    new_m.append(mn)
                    new_l.append(a * ls[hh] + jnp.sum(p, axis=-1, keepdims=True))
                    alphas.append(a)
                    part = _dot(p.astype(BF16), v * (m0b if hh == 0 else m1b))
                    pv = part if pv is None else pv + part
                acc = acc * jnp.where(left, alphas[0], alphas[1]) + pv
                return tuple(new_m), tuple(new_l), acc

            init = ((jnp.full((QB, 1), NEG, F32),) * 2, (jnp.zeros((QB, 1), F32),) * 2, jnp.zeros((QB, LANES), F32))
            ms, ls, acc = lax.fori_loop(0, i + 1, kv, init)
            o_ref[pl.ds(r0, QB), :] = acc * jnp.where(left, 1.0 / ls[0], 1.0 / ls[1])
            lse_ref[pl.ds(r0, QB), :] = jnp.where(left, ms[0] + jnp.log(ls[0]) * LOG2_E, ms[1] + jnp.log(ls[1]) * LOG2_E)
            return 0

        lax.fori_loop(0, S // QB, qblock, 0)

    pair = lambda w: pl.BlockSpec((S, w), lambda p: (0, p))
    return _carrier_call(
        body, name=name, grid=(MLA_H // 2,), in_specs=[pair(2 * LANES), pair(2 * LANES), pair(LANES)],
        out_specs=[pair(LANES), pair(LANES)],
        out_shape=[jax.ShapeDtypeStruct((S, MLA_H * MLA_V), F32), jax.ShapeDtypeStruct((S, MLA_H * MLA_V), F32)],
        args=(qa, ka, va), sem=("parallel",), carry=carry)


def _mla_bwd(name, qa, ka, va, o, lse, do, do_block0, carry=None):
    def body(q_ref, k_ref, v_ref, o_ref, lse_ref, do_ref, dq_ref, dk_ref, dv_ref):
        m0f, m1f = _head_masks(F32)
        m0b, m1b = _head_masks(BF16)
        dk_ref[...] = jnp.zeros_like(dk_ref)
        dv_ref[...] = jnp.zeros_like(dv_ref)

        def qblock(i, _):
            r0 = pl.multiple_of(i * QB, QB)
            rows = pl.ds(r0, QB)
            do_f = do_ref[rows, :]
            prod = do_f * o_ref[rows, :]
            deltas = [jnp.sum(prod * m0f, axis=-1, keepdims=True), jnp.sum(prod * m1f, axis=-1, keepdims=True)]
            lse_v = lse_ref[rows, :]
            lses = [lse_v[:, 0:1], lse_v[:, 64:65]]
            dob = do_f.astype(BF16)
            dos = [dob * m0b, dob * m1b]
            qs = [q_ref[rows, hh * LANES:(hh + 1) * LANES] for hh in range(2)]
            rowc = lax.shift_right_logical(r0 + _iota((QB, QB), 0), 6)

            def kv(kb, dqs):
                c0 = pl.multiple_of(kb * QB, QB)
                cols = pl.ds(c0, QB)
                v = v_ref[cols, :]
                ok = lax.shift_right_logical(c0 + _iota((QB, QB), 1), 6) <= rowc
                out = []
                dv = None
                for hh in range(2):
                    k = k_ref[cols, hh * LANES:(hh + 1) * LANES]
                    s = _dot(qs[hh], k, "nt") * (SCALE_A * LOG2_E)
                    p = jnp.where(ok, jnp.exp2(s - lses[hh]), 0.0)
                    dp = _dot(dos[hh], v, "nt")
                    ds = (p * (dp - deltas[hh]) * SCALE_A).astype(BF16)
                    out.append(dqs[hh] + _dot(ds, k))
                    dk_ref[cols, hh * LANES:(hh + 1) * LANES] += _dot(ds, qs[hh], "tn")
                    part = _dot(p.astype(BF16), dos[hh], "tn")
                    dv = part if dv is None else dv + part
                dv_ref[cols, :] += dv
                return tuple(out)

            dqs = lax.fori_loop(0, i + 1, kv, (jnp.zeros((QB, LANES), F32),) * 2)
            for hh in range(2):
                dq_ref[rows, hh * LANES:(hh + 1) * LANES] = dqs[hh]
            return 0

        lax.fori_loop(0, S // QB, qblock, 0)

    pair = lambda w: pl.BlockSpec((S, w), lambda p: (0, p))
    return _carrier_call(
        body, name=name, grid=(MLA_H // 2,),
        in_specs=[pair(2 * LANES), pair(2 * LANES), pair(LANES), pair(LANES), pair(LANES),
                  pl.BlockSpec((S, LANES), lambda p: (0, do_block0 + p))],
        out_specs=[pair(2 * LANES), pair(2 * LANES), pair(LANES)],
        out_shape=[jax.ShapeDtypeStruct((S, MLA_H * LANES), F32), jax.ShapeDtypeStruct((S, MLA_H * LANES), F32),
                   jax.ShapeDtypeStruct((S, MLA_H * MLA_V), F32)],
        args=(qa, ka, va, o, lse, do), sem=("parallel",), carry=carry)


def _sb_weights(q_h, k, c, before, tri_suffix):
    z = _dot(q_h, k, "nt") * (SCALE_B * LOG2_E)
    sp = jnp.maximum(z, 0.0) + jnp.log(1.0 + jnp.exp2(-jnp.abs(z))) * LOG2_E
    log_keep = jnp.where(before, -sp, 0.0)
    to_the_right, total = _running_sum(log_keep, tri_suffix, True)
    w = jnp.where(before, jnp.exp2(z - sp + to_the_right + c), 0.0)
    return w, jnp.exp2(z - sp), total


def _sb_fwd(name, proj, carry=None):
    def body(q_ref, k_ref, v_ref, o_ref):
        m0b, m1b = _head_masks(BF16)
        tri_suffix = (_iota((MXU_W, MXU_W), 0) > _iota((MXU_W, MXU_W), 1)).astype(BF16)

        def qblock(i, _):
            r0 = pl.multiple_of(i * QB, QB)
            q = q_ref[pl.ds(r0, QB), :].astype(BF16)
            qs = [q * m0b, q * m1b]
            rowg = r0 + _iota((QB, QB), 0)

            def kv(step, carry):
                cs, acc = carry
                c0 = pl.multiple_of((i - step) * QB, QB)
                k = k_ref[pl.ds(c0, QB), :].astype(BF16)
                v = v_ref[pl.ds(c0, QB), :].astype(BF16)
                before = (c0 + _iota((QB, QB), 1)) < rowg
                new_c = []
                for hh in range(2):
                    w, _, tot = _sb_weights(qs[hh], k, cs[hh], before, tri_suffix)
                    new_c.append(cs[hh] + tot)
                    acc = acc + _dot(w.astype(BF16), v * (m0b if hh == 0 else m1b))
                return tuple(new_c), acc

            init = ((jnp.zeros((QB, 1), F32),) * 2, jnp.zeros((QB, LANES), F32))
            _, acc = lax.fori_loop(0, i + 1, kv, init)
            o_ref[pl.ds(r0, QB), :] = acc.astype(BF16)
            return 0

        lax.fori_loop(0, S // QB, qblock, 0)

    col = lambda base: pl.BlockSpec((S, LANES), lambda p: (0, base // LANES + p))
    return _carrier_call(
        body, name=name, grid=(SB_H // 2,), in_specs=[col(P_QB), col(P_KB), col(P_VB)],
        out_specs=[pl.BlockSpec((S, LANES), lambda p: (0, p))],
        out_shape=[jax.ShapeDtypeStruct((S, SB_H * SB_DIM), BF16)],
        args=(proj, proj, proj), sem=("parallel",), carry=carry)


def _sb_bwd(name, proj, do, do_block0, carry=None):
    nb = S // QB

    def body(q_ref, k_ref, v_ref, do_ref, dq_ref, dk_ref, dv_ref, sig_scr, dl_scr, dk_acc, dv_acc):
        m0b, m1b = _head_masks(BF16)
        tri_suffix = (_iota((MXU_W, MXU_W), 0) > _iota((MXU_W, MXU_W), 1)).astype(BF16)
        tri_prefix = (_iota((MXU_W, MXU_W), 0) < _iota((MXU_W, MXU_W), 1)).astype(BF16)
        dk_acc[...] = jnp.zeros_like(dk_acc)
        dv_acc[...] = jnp.zeros_like(dv_acc)

        def qblock(i, _):
            r0 = pl.multiple_of(i * QB, QB)
            rows = pl.ds(r0, QB)
            q = q_ref[rows, :].astype(BF16)
            qs = [q * m0b, q * m1b]
            dob = do_ref[rows, :].astype(BF16)
            dos = [dob * m0b, dob * m1b]
            rowg = r0 + _iota((QB, QB), 0)

            def sweep_left(step, cs):
                kb = i - step
                c0 = pl.multiple_of(kb * QB, QB)
                cols = pl.ds(c0, QB)
                k = k_ref[cols, :].astype(BF16)
                v = v_ref[cols, :].astype(BF16)
                before = (c0 + _iota((QB, QB), 1)) < rowg
                new_c = []
                dv = None
                for hh in range(2):
                    w, sig, tot = _sb_weights(qs[hh], k, cs[hh], before, tri_suffix)
                    new_c.append(cs[hh] + tot)
                    sig_scr[hh, kb] = sig
                    dl_scr[hh, kb] = _dot(dos[hh], v, "nt") * w
                    part = _dot(w.astype(BF16), dos[hh], "tn")
                    dv = part if dv is None else dv + part
                dv_acc[cols, :] += dv
                return tuple(new_c)

            lax.fori_loop(0, i + 1, sweep_left, (jnp.zeros((QB, 1), F32),) * 2)

            def sweep_right(kb, carry):
                ps, dq = carry
                c0 = pl.multiple_of(kb * QB, QB)
                cols = pl.ds(c0, QB)
                k = k_ref[cols, :].astype(BF16)
                before = (c0 + _iota((QB, QB), 1)) < rowg
                new_p = []
                dk = None
                for hh in range(2):
                    dl = dl_scr[hh, kb]
                    sig = sig_scr[hh, kb]
                    to_the_left, total = _running_sum(dl, tri_prefix, False)
                    earlier = to_the_left + ps[hh]
                    new_p.append(ps[hh] + total)
                    dz = (jnp.where(before, dl * (1.0 - sig) - earlier * sig, 0.0) * SCALE_B).astype(BF16)
                    dq = dq + _dot(dz, k * (m0b if hh == 0 else m1b))
                    part = _dot(dz, qs[hh], "tn")
                    dk = part if dk is None else dk + part
                dk_acc[cols, :] += dk
                return tuple(new_p), dq

            init = ((jnp.zeros((QB, 1), F32),) * 2, jnp.zeros((QB, LANES), F32))
            _, dq = lax.fori_loop(0, i + 1, sweep_right, init)
            dq_ref[rows, :] = dq.astype(BF16)
            return 0

        lax.fori_loop(0, nb, qblock, 0)
        dk_ref[...] = dk_acc[...].astype(BF16)
        dv_ref[...] = dv_acc[...].astype(BF16)

    col = lambda base: pl.BlockSpec((S, LANES), lambda p: (0, base // LANES + p))
    out = pl.BlockSpec((S, LANES), lambda p: (0, p))
    shape = jax.ShapeDtypeStruct((S, SB_H * SB_DIM), BF16)
    return _carrier_call(
        body, name=name, grid=(SB_H // 2,),
        in_specs=[col(P_QB), col(P_KB), col(P_VB), pl.BlockSpec((S, LANES), lambda p: (0, do_block0 + p))],
        out_specs=[out, out, out], out_shape=[shape, shape, shape],
        scratch_shapes=[pltpu.VMEM((2, nb, QB, QB), F32), pltpu.VMEM((2, nb, QB, QB), F32),
                        pltpu.VMEM((S, LANES), F32), pltpu.VMEM((S, LANES), F32)],
        args=(proj, proj, proj, do), sem=("parallel",), carry=carry)


def _band_row_index():
    j = np.arange(TOEP_W)
    rel = np.clip(LEFT_CHUNKS * CHUNK - j, -REL_CLIP, REL_CLIP) + REL_CLIP
    rel[BAND_W:] = 2 * REL_CLIP
    return rel.astype(np.int32)


def _band_tiles(r0_ref, q_ref, kpad, vpad, m, m0b, m1b, static_ok, bias):
    r0 = pl.multiple_of(m * BQ, BQ)
    q = q_ref[0, pl.ds(r0, BQ), :]
    kw = kpad[pl.ds(r0, BAND_W), :]
    vw = vpad[pl.ds(r0, BAND_W), :]
    ok = static_ok & ((r0 - BAND_PAD + _iota((BQ, BAND_W), 1)) >= 0)
    qs = [q * m0b, q * m1b]
    ps = []
    for hh in range(2):
        s = jnp.where(ok, _dot(qs[hh], kw, "nt") * (SCALE_C * LOG2_E) + bias[hh], NEG)
        e = jnp.exp2(s - jnp.max(s, axis=-1, keepdims=True))
        ps.append(e * (1.0 / jnp.sum(e, axis=-1, keepdims=True)))
    return r0, qs, kw, vw, ps


def _band_setup(qkv_ref, r0_ref, kpad, vpad):
    kpad[0:BAND_PAD, :] = jnp.zeros((BAND_PAD, LANES), BF16)
    vpad[0:BAND_PAD, :] = jnp.zeros((BAND_PAD, LANES), BF16)
    kpad[BAND_PAD:, :] = qkv_ref[1]
    vpad[BAND_PAD:, :] = qkv_ref[2]
    jc = lax.shift_right_logical(_iota((BQ, BAND_W), 1), 6)
    rc = lax.shift_right_logical(_iota((BQ, BAND_W), 0), 6)
    static_ok = (jc >= rc) & (jc <= rc + LEFT_CHUNKS)
    bias = []
    for hh in range(2):
        row = jnp.broadcast_to(r0_ref[hh:hh + 1, :] * LOG2_E, (BQ, TOEP_W))
        bias.append(pltpu.roll(row, 0, 1, stride=1, stride_axis=0)[:, :BAND_W])
    return static_ok, bias


def _band_fwd(name, qkv, r0, carry=None):
    def body(qkv_ref, r0_ref, o_ref, kpad, vpad):
        m0b, m1b = _head_masks(BF16)
        static_ok, bias = _band_setup(qkv_ref, r0_ref, kpad, vpad)

        def qblock(m, _):
            r0_, _, _, vw, ps = _band_tiles(r0_ref, qkv_ref, kpad, vpad, m, m0b, m1b, static_ok, bias)
            o = _dot(ps[0].astype(BF16), vw * m0b) + _dot(ps[1].astype(BF16), vw * m1b)
            o_ref[pl.ds(r0_, BQ), :] = o.astype(BF16)
            return 0

        lax.fori_loop(0, S // BQ, qblock, 0)

    return _carrier_call(
        body, name=name, grid=(C_H // 2,),
        in_specs=[pl.BlockSpec((3, S, LANES), lambda p: (0, 0, p)), pl.BlockSpec((None, 2, TOEP_W), lambda p: (p, 0, 0))],
        out_specs=[pl.BlockSpec((S, LANES), lambda p: (0, p))],
        out_shape=[jax.ShapeDtypeStruct((S, C_H * C_DIM), BF16)],
        scratch_shapes=[pltpu.VMEM((S + BAND_PAD, LANES), BF16), pltpu.VMEM((S + BAND_PAD, LANES), BF16)],
        args=(qkv, r0), sem=("parallel",), carry=carry)


def _band_bwd(name, qkv, r0, do, carry=None):
    def body(qkv_ref, r0_ref, do_ref, dqkv_ref, dr0_ref, kpad, vpad, dkpad, dvpad, db_acc):
        m0b, m1b = _head_masks(BF16)
        static_ok, bias = _band_setup(qkv_ref, r0_ref, kpad, vpad)
        dkpad[...] = jnp.zeros_like(dkpad)
        dvpad[...] = jnp.zeros_like(dvpad)
        db_acc[...] = jnp.zeros_like(db_acc)

        def qblock(m, _):
            r0_, qs, kw, vw, ps = _band_tiles(r0_ref, qkv_ref, kpad, vpad, m, m0b, m1b, static_ok, bias)
            dob = do_ref[pl.ds(r0_, BQ), :].astype(BF16)
            dos = [dob * m0b, dob * m1b]
            dq = None
            dk = None
            dv = None
            for hh in range(2):
                p = ps[hh]
                dp = _dot(dos[hh], vw, "nt")
                ds = p * (dp - jnp.sum(dp * p, axis=-1, keepdims=True))
                db_acc[hh, :, 0:BAND_W] += ds
                dsb = (ds * SCALE_C).astype(BF16)
                t = _dot(dsb, kw * (m0b if hh == 0 else m1b))
                dq = t if dq is None else dq + t
                t = _dot(dsb, qs[hh], "tn")
                dk = t if dk is None else dk + t
                t = _dot(p.astype(BF16), dos[hh], "tn")
                dv = t if dv is None else dv + t
            dqkv_ref[0, pl.ds(r0_, BQ), :] = dq.astype(BF16)
            dkpad[pl.ds(r0_, BAND_W), :] += dk
            dvpad[pl.ds(r0_, BAND_W), :] += dv
            return 0

        lax.fori_loop(0, S // BQ, qblock, 0)
        dqkv_ref[1] = dkpad[BAND_PAD:, :].astype(BF16)
        dqkv_ref[2] = dvpad[BAND_PAD:, :].astype(BF16)
        sub = _iota((8, TOEP_W), 0)
        for hh in range(2):
            folded = db_acc[hh, 0:8, :]
            for a in range(1, BQ // 8):
                folded = folded + pltpu.roll(db_acc[hh, 8 * a:8 * a + 8, :], TOEP_W - 8 * a, 1)
            for bit in range(3):
                moved = pltpu.roll(folded, TOEP_W - (1 << bit), 1)
                folded = jnp.where((sub & (1 << bit)) != 0, moved, folded)
            dr0_ref[hh:hh + 1, :] = jnp.sum(folded, axis=0, keepdims=True)

    return _carrier_call(
        body, name=name, grid=(C_H // 2,),
        in_specs=[pl.BlockSpec((3, S, LANES), lambda p: (0, 0, p)), pl.BlockSpec((None, 2, TOEP_W), lambda p: (p, 0, 0)),
                  pl.BlockSpec((S, LANES), lambda p: (0, p))],
        out_specs=[pl.BlockSpec((3, S, LANES), lambda p: (0, 0, p)), pl.BlockSpec((None, 2, TOEP_W), lambda p: (p, 0, 0))],
        out_shape=[jax.ShapeDtypeStruct((3, S, C_H * C_DIM), BF16), jax.ShapeDtypeStruct((C_H // 2, 2, TOEP_W), F32)],
        scratch_shapes=[pltpu.VMEM((S + BAND_PAD, LANES), BF16), pltpu.VMEM((S + BAND_PAD, LANES), BF16),
                        pltpu.VMEM((S + BAND_PAD, LANES), F32), pltpu.VMEM((S + BAND_PAD, LANES), F32),
                        pltpu.VMEM((2, BQ, TOEP_W), F32)],
        args=(qkv, r0, do), sem=("parallel",), carry=carry)


def _bias_table_grad(name, dr0):
    w_out = 5 * LANES

    def body(d_ref, o_ref):
        j = _iota((TOEP_W, w_out), 0)
        rel = jnp.clip(LEFT_CHUNKS * CHUNK - j, -REL_CLIP, REL_CLIP) + REL_CLIP
        rel = jnp.where(j >= BAND_W, 2 * REL_CLIP, rel)
        onehot = (rel == _iota((TOEP_W, w_out), 1)).astype(BF16)
        d = d_ref[...]
        hi = d.astype(BF16)
        mid = (d - hi.astype(F32))
        mid_b = mid.astype(BF16)
        lo = (mid - mid_b.astype(F32)).astype(BF16)
        o_ref[...] = _dot(hi, onehot) + _dot(mid_b, onehot) + _dot(lo, onehot)

    return pl.pallas_call(
        body, name=name, out_shape=jax.ShapeDtypeStruct((C_H, w_out), F32),
        in_specs=[pl.BlockSpec((C_H, TOEP_W), lambda: (0, 0))], out_specs=pl.BlockSpec((C_H, w_out), lambda: (0, 0)),
        grid=(),
    )(dr0)


def _carry_gather(cy, slots, names, ici, d2d):
    idx = [cy.operand(slots[n], True) for n in names]
    n = len(names)
    base_i = cy.sems(3 * n) if ici else 0
    base_d = cy.sems(3 * n) if d2d else 0

    def piece(refs, t, slot, cc):
        return refs[idx[t]].at[slot, _half_rows(cc, slots[names[t]].shape[1] // 2), :]

    def over_ici(refs, send, recv, arriving):
        x, y, c, chips = _position()
        out = []
        for t in range(n):
            for j in range(3):
                r = piece(refs, t, 2 * chips[j][0] + chips[j][1] if arriving else 2 * x + y, c)
                out.append(_remote(r, r, send, recv, base_i + 3 * t + j, (*chips[j], c)))
        return out

    def over_d2d(refs, send, recv, arriving):
        x, y, c, chips = _position()
        out = []
        for t in range(n):
            for j in range(3):
                r = piece(refs, t, 2 * chips[j][0] + chips[j][1], 1 - c if arriving else c)
                out.append(_remote(r, r, send, recv, base_d + 3 * t + j, (x, y, 1 - c)))
        return out

    def start_ici(refs, fresh, send, recv):
        for cp in over_ici(refs, send, recv, False):
            cp.start()

    def wait_ici(refs, fresh, send, recv):
        for cp in over_ici(refs, send, recv, True):
            cp.wait_recv()
        for cp in over_ici(refs, send, recv, False):
            cp.wait_send()

    def start_d2d(refs, fresh, send, recv):
        for cp in over_d2d(refs, send, recv, False):
            cp.start()

    def wait_d2d(refs, fresh, send, recv):
        for cp in over_d2d(refs, send, recv, True):
            cp.wait_recv()
        for cp in over_d2d(refs, send, recv, False):
            cp.wait_send()

    if ici and d2d:
        cy.starts.append(start_ici)
        cy.finishes += [wait_ici, start_d2d, wait_d2d]
    elif ici:
        cy.starts.append(start_ici)
        cy.finishes.append(wait_ici)
    else:
        cy.starts.append(start_d2d)
        cy.finishes.append(wait_d2d)

    def done(aliased, fresh):
        for t, name in enumerate(names):
            slots[name] = aliased[idx[t]]

    cy.on_done.append(done)


def _carry_chip_exchange(cy, sums, got, names):
    idx = [cy.operand(sums[n], False) for n in names]
    out = [cy.result((3,) + sums[n].shape[1:], BF16) for n in names]
    base = cy.sems(3 * len(names))

    def copies(refs, fresh, send, recv):
        x, y, c, chips = _position()
        return [_remote(refs[idx[t]].at[2 * chips[j][0] + chips[j][1]], fresh[out[t]].at[j], send, recv, base + 3 * t + j,
                        (*chips[j], c)) for t in range(len(names)) for j in range(3)]

    def start(refs, fresh, send, recv):
        for cp in copies(refs, fresh, send, recv):
            cp.start()

    def wait(refs, fresh, send, recv):
        for cp in copies(refs, fresh, send, recv):
            cp.wait()

    cy.starts.append(start)
    cy.finishes.append(wait)

    def done(aliased, fresh):
        for t, name in enumerate(names):
            got[name] = fresh[out[t]]

    cy.on_done.append(done)


def _run_carry(name, cy):
    _, res = _carrier_call(None, name=name, grid=(), in_specs=[], out_specs=[], out_shape=[], args=(), sem=(), carry=cy)
    cy.done(res)


FIRST_WEIGHTS = ("ev_w_in", "ev_w_uq", "ev_w_ukv")
WEIGHTS_A = ("ev_w_out", "w_gate0", "w_up0")
WEIGHTS_B = ("w_down0", "od_w_qkv", "od_w_out")
WEIGHTS_C = ("w_gate1",)
WEIGHTS_D = ("w_up1", "w_down1")
GRAD_GROUPS = {"ffn1": ("w_gate1", "w_up1", "w_down1"), "od": ("od_w_qkv", "od_w_out"),
               "ffn0": ("w_gate0", "w_up0", "w_down0"), "ev_out": ("ev_w_out",),
               "ev": ("ev_w_in", "ev_w_uq", "ev_w_ukv")}


def _carry_pair_exchange(cy, parts, theirs, names):
    idx = [cy.operand(parts[n], False) for n in names]
    out = [cy.result((N_CHIPS, parts[n].shape[1] // 2, parts[n].shape[2]), BF16) for n in names]
    base = cy.sems(len(names))

    def copies(refs, fresh, send, recv):
        x, y, c, _ = _position()
        return [_remote(refs[idx[t]].at[:, _half_rows(1 - c, parts[n].shape[1] // 2), :], fresh[out[t]], send, recv,
                        base + t, (x, y, 1 - c)) for t, n in enumerate(names)]

    cy.starts.append(lambda refs, fresh, send, recv: [cp.start() for cp in copies(refs, fresh, send, recv)])
    cy.finishes.append(lambda refs, fresh, send, recv: [cp.wait() for cp in copies(refs, fresh, send, recv)])

    def done(aliased, fresh):
        for t, name in enumerate(names):
            theirs[name] = fresh[out[t]]

    cy.on_done.append(done)


def _carry_sibling_exchange(cy, fulls, pieces):
    idx = [cy.operand(fulls[p], True) for p, _ in pieces]
    base = cy.sems(len(pieces))

    def copies(refs, send, recv, arriving):
        x, y, c, _ = _position()
        out = []
        for t, (p, layer) in enumerate(pieces):
            r = refs[idx[t]].at[layer, _half_rows(1 - c if arriving else c, fulls[p].shape[1] // 2), :]
            out.append(_remote(r, r, send, recv, base + t, (x, y, 1 - c)))
        return out

    def start(refs, fresh, send, recv):
        for cp in copies(refs, send, recv, False):
            cp.start()

    def wait(refs, fresh, send, recv):
        for cp in copies(refs, send, recv, True):
            cp.wait_recv()
        for cp in copies(refs, send, recv, False):
            cp.wait_send()

    cy.starts.append(start)
    cy.finishes.append(wait)

    def done(aliased, fresh):
        for t, (p, _) in enumerate(pieces):
            fulls[p] = aliased[idx[t]]

    cy.on_done.append(done)


RIDES = {
    "cast_rest": (("gather", FIRST_WEIGHTS),),
    "mla_attn": (("gather_ici", WEIGHTS_A),),
    "sb_attn": (("gather_d2d", WEIGHTS_A), ("gather_ici", WEIGHTS_B)),
    "ev_out": (("gather_d2d", WEIGHTS_B),),
    "ffn0": (("gather_ici", WEIGHTS_C),),
    "qkv": (("gather_d2d", WEIGHTS_C),),
    "band_attn": (("gather_ici", WEIGHTS_D),),
    "od_out": (("gather_d2d", WEIGHTS_D),),
    "od_out_bwd_w": (("pair", "ffn1"),),
    "band_attn_bwd": (("chips", "ffn1"),),
    "rms_mix1_bwd": (("pair", "od"),),
    "ev_out_bwd_w": (("pair", "ffn0"),),
    "mla_attn_bwd": (("chips", "od"), ("sibling", "ffn1"), ("pair", "ev_out")),
    "sb_attn_bwd": (("chips", "ffn0"), ("sibling", "od"), ("chips", "ev_out")),
    "proj_in_bwd_w": (("sibling", "ffn0"), ("sibling", "ev_out")),
    "grads_pair_ev": (("pair", "ev"),),
    "proj_in_bwd_x": (("chips", "ev"),),
    "grads_sibling_ev": (("sibling", "ev"),),
}


class _Exchanges:
    def __init__(self, slots, pos, shapes, cast_rest):
        self.slots, self.pos, self.shapes, self.cast_rest = dict(slots), pos, shapes, cast_rest
        self.parts, self.theirs, self.sums, self.got, self.fulls = {}, {}, {}, {}, {}

    def begin(self):
        self.slots.update(self.cast_rest(self.carry("cast_rest")))

    def weights(self, *names):
        return [self.slots[n] for n in names]

    def _pair_sums(self, group):
        names = GRAD_GROUPS[group]
        self.sums.update(zip(names, _pair_sums("pair_sums_" + group, [self.parts[n] for n in names],
                                               [self.theirs[n] for n in names], self.pos)))

    def _chip_sums(self, group):
        names = GRAD_GROUPS[group]
        items = [(self.sums[n], self.got[n], PART_OF[n][1], self.shapes[PART_OF[n][0]], self.fulls.get(PART_OF[n][0]))
                 for n in names]
        self.fulls.update(zip([PART_OF[n][0] for n in names], _chip_sums("chip_sums_" + group, items, self.pos)))

    def carry(self, stage):
        cy = _Carry()
        for step, what in RIDES[stage]:
            if step == "gather":
                _carry_gather(cy, self.slots, what, True, True)
            elif step == "gather_ici":
                _carry_gather(cy, self.slots, what, True, False)
            elif step == "gather_d2d":
                _carry_gather(cy, self.slots, what, False, True)
            elif step == "pair":
                _carry_pair_exchange(cy, self.parts, self.theirs, GRAD_GROUPS[what])
            elif step == "chips":
                self._pair_sums(what)
                _carry_chip_exchange(cy, self.sums, self.got, GRAD_GROUPS[what])
            elif step == "sibling":
                self._chip_sums(what)
                _carry_sibling_exchange(cy, self.fulls, [PART_OF[n] for n in GRAD_GROUPS[what]])
        return cy

    def grads(self, group, parts):
        self.parts.update(parts)
        if group == "ev":
            _run_carry("grads_pair_ev", self.carry("grads_pair_ev"))

    def finish(self):
        _run_carry("grads_sibling_ev", self.carry("grads_sibling_ev"))
        return {n: self.fulls[n] for n in BIG}


class _NoExchanges:
    def __init__(self, slots):
        self.slots, self.parts = dict(slots), {}

    def begin(self):
        pass

    def weights(self, *names):
        return [self.slots[n] for n in names]

    def carry(self, stage):
        return None

    def grads(self, group, parts):
        self.parts.update(parts)


def _w_in_pieces():
    segments = ((0, Q_LORA, P_CQ), (Q_LORA, Q_LORA + KV_LORA, P_CKV),
                (Q_LORA + KV_LORA, Q_LORA + KV_LORA + MLA_ROPE, P_KR + KR_LANE),
                (Q_LORA + KV_LORA + MLA_ROPE, EVEN_IN, P_QB))
    width = EVEN_IN // N_CHIPS
    pieces = []
    for lo, hi, at in segments:
        for k in range(N_CHIPS):
            a, b = max(lo, k * width), min(hi, (k + 1) * width)
            if a < b:
                pieces.append((k, a - k * width, b - a, at + a - lo))
    return pieces


def _w_in_padded(name, w_in_s):
    tr = MXU_W

    def body(s_ref, o_ref):
        o_ref[...] = jnp.zeros(o_ref.shape, BF16)
        for k, a, n, at in _w_in_pieces():
            o_ref[:, at:at + n] = s_ref[k, :, a:a + n]

    return pl.pallas_call(
        body, name=name, grid=(D // tr,),
        in_specs=[pl.BlockSpec((N_CHIPS, tr, EVEN_IN // N_CHIPS), lambda i: (0, i, 0))],
        out_specs=pl.BlockSpec((tr, P_IN), lambda i: (i, 0)), out_shape=jax.ShapeDtypeStruct((D, P_IN), BF16),
        compiler_params=_params("parallel"))(w_in_s)


def _w_in_sharded(name, d_w_in_p):
    tr = MXU_W

    def body(p_ref, o_ref):
        for k, a, n, at in _w_in_pieces():
            o_ref[k, :, a:a + n] = p_ref[:, at:at + n]

    return pl.pallas_call(
        body, name=name, grid=(D // tr,),
        in_specs=[pl.BlockSpec((tr, P_IN), lambda i: (i, 0))],
        out_specs=pl.BlockSpec((N_CHIPS, tr, EVEN_IN // N_CHIPS), lambda i: (0, i, 0)),
        out_shape=jax.ShapeDtypeStruct((N_CHIPS, D, EVEN_IN // N_CHIPS), BF16),
        compiler_params=_params("parallel"))(d_w_in_p)


def _first_weights(w_in_s, w_uq_s, w_ukv_s):
    gw = {"ev_w_in": w_in_s, "ev_w_uq": w_uq_s, "ev_w_ukv": w_ukv_s}
    w_in_p = _w_in_padded("w_in_padded", w_in_s)
    w_uq = jnp.moveaxis(gw["ev_w_uq"], 0, 1).reshape(Q_LORA, MLA_H, MLA_NOPE + MLA_ROPE)
    w_uq_p = jnp.concatenate([w_uq, jnp.zeros((Q_LORA, MLA_H, LANES - MLA_NOPE - MLA_ROPE), BF16)], axis=2)
    w_ukv = jnp.moveaxis(gw["ev_w_ukv"], 0, 1).reshape(KV_LORA, MLA_H, MLA_NOPE + MLA_V)
    w_uk_p = jnp.concatenate([w_ukv[:, :, :MLA_NOPE], jnp.zeros((KV_LORA, MLA_H, LANES - MLA_NOPE), BF16)], axis=2)
    return dict(
        w_in=w_in_p, w_uq=w_uq_p.reshape(Q_LORA, MLA_H * LANES), w_uk=w_uk_p.reshape(KV_LORA, MLA_H * LANES),
        w_uv=w_ukv[:, :, MLA_NOPE:].reshape(KV_LORA, MLA_H * MLA_V))


def _proj_mm(name, u, w_in):
    return _mm(name, u, w_in, kind="nn", grid=(S // TM, 1, 1),
               a_spec=pl.BlockSpec((TM, D), lambda i, j, k: (i, 0)), b_spec=pl.BlockSpec((D, P_IN), lambda i, j, k: (0, 0)),
               o_spec=pl.BlockSpec((TM, P_IN), lambda i, j, k: (i, 0)), out_shape=(S, P_IN), out_dtype=F32, acc_shape=None)


def _out_proj(name, o, w, resid, carry=None):
    return _mm(name, o, w, kind="nn", grid=(S // TM, 1, 1),
               a_spec=pl.BlockSpec((TM, D), lambda i, j, k: (i, 0)), b_spec=pl.BlockSpec((D, D), lambda i, j, k: (0, 0)),
               o_spec=pl.BlockSpec((TM, D), lambda i, j, k: (i, 0)), out_shape=(S, D), out_dtype=F32, acc_shape=None,
               resid=resid, r_spec=pl.BlockSpec((TM, D), lambda i, j, k: (i, 0)), carry=carry)


def _out_proj_bwd(name, dh, o, w, ex):
    d_o = _mm(name + "_x", dh, w, kind="nt", grid=(S // TM, 1, 1),
              a_spec=pl.BlockSpec((TM, D), lambda i, j, k: (i, 0)), b_spec=pl.BlockSpec((D, D), lambda i, j, k: (0, 0)),
              o_spec=pl.BlockSpec((TM, D), lambda i, j, k: (i, 0)), out_shape=(S, D), out_dtype=F32, acc_shape=None)
    d_w = _mm(name + "_w", o, dh, kind="tn", grid=(2, S // TM),
              a_spec=pl.BlockSpec((TM, TM), lambda j, k: (k, j)), b_spec=pl.BlockSpec((TM, D), lambda j, k: (k, 0)),
              o_spec=pl.BlockSpec((TM, D), lambda j, k: (j, 0)), out_shape=(D, D), out_dtype=BF16, acc_shape=(TM, D),
              carry=ex.carry(name + "_w"))
    return d_o, d_w


def _local_step(x, tgt, sm, ex):
    def riding(stage, fn, *args):
        cy = ex.carry(stage)
        res, copies = fn(stage, *args, carry=cy)
        if cy is not None:
            cy.done(copies)
        return res

    cos_t, sin_t = _rope_tables()
    g_mix, g_ffn = sm["g_mix"], sm["g_ffn"]
    r0 = sm["od_rel_bias"][0][:, _band_row_index()].reshape(C_H // 2, 2, TOEP_W)
    nt = 3

    ex.begin()
    w = _first_weights(*ex.weights(*FIRST_WEIGHTS))
    u0 = _rms_fwd("rms_mix0", x, g_mix[0:1])
    proj = _proj_mm("proj_in", u0, w["w_in"])
    qa, ka, va = _mla_prep_fwd("mla_prep", proj, sm["ev_g_cq"], sm["ev_g_ckv"], w["w_uq"], w["w_uk"], w["w_uv"], cos_t, sin_t)
    o_a, lse = riding("mla_attn", _mla_fwd, qa, ka, va)
    o_b, = riding("sb_attn", _sb_fwd, proj)
    o_ev = jnp.concatenate([o_a.astype(BF16), o_b], axis=1)
    w["ev_w_out"] = ex.weights("ev_w_out")[0].reshape(D, D)
    h1 = _out_proj("ev_out", o_ev, w["ev_w_out"], x, ex.carry("ev_out"))
    w["w_gate0"], w["w_up0"], w["w_down0"] = ex.weights("w_gate0", "w_up0", "w_down0")
    h2, gate0, up0 = riding("ffn0", _ffn_fwd, h1, g_ffn[0:1], w["w_gate0"], w["w_up0"], w["w_down0"])
    w["w_qkv"] = jnp.moveaxis(ex.weights("od_w_qkv")[0], 0, 1).reshape(D, nt * D)
    u2 = _rms_fwd("rms_mix1", h2, g_mix[1:2])
    qkv = _mm("qkv", u2, w["w_qkv"], kind="nn", grid=(S // TM, nt, 1),
              a_spec=pl.BlockSpec((TM, D), lambda i, t, k: (i, 0)), b_spec=pl.BlockSpec((D, D), lambda i, t, k: (0, t)),
              o_spec=pl.BlockSpec((None, TM, D), lambda i, t, k: (t, i, 0)),
              out_shape=(nt, S, D), out_dtype=BF16, acc_shape=None, carry=ex.carry("qkv"))
    o_od, = riding("band_attn", _band_fwd, qkv, r0)
    w["od_w_out"] = ex.weights("od_w_out")[0].reshape(D, D)
    h3 = _out_proj("od_out", o_od, w["od_w_out"], h2, ex.carry("od_out"))
    w["w_gate1"], w["w_up1"], w["w_down1"] = ex.weights("w_gate1", "w_up1", "w_down1")
    (h4, gate1, up1), _ = _ffn_fwd("ffn1", h3, g_ffn[1:2], w["w_gate1"], w["w_up1"], w["w_down1"])

    loss, dh4, dg_final = _loss_bwd("loss", h4, sm["g_final"].reshape(1, D), tgt)

    dh3, dg_ffn1, u3, dgate, dup, act = _ffn_bwd("ffn1_bwd", dh4, h3, g_ffn[1:2], gate1, up1,
                                                 w["w_gate1"], w["w_up1"], w["w_down1"])
    d_wg1, d_wu1, d_wd1 = _ffn_wgrads("ffn1_dw", u3, dgate, dup, act, dh4)
    ex.grads("ffn1", {"w_gate1": d_wg1, "w_up1": d_wu1, "w_down1": d_wd1})

    d_ood, d_w_od_out = _out_proj_bwd("od_out_bwd", dh3, o_od, w["od_w_out"], ex)
    dqkv, dr0 = riding("band_attn_bwd", _band_bwd, qkv, r0, d_ood)
    du2 = _mm("qkv_bwd_x", dqkv, w["w_qkv"], kind="nt", grid=(S // TM, nt),
              a_spec=pl.BlockSpec((None, TM, D), lambda i, t: (t, i, 0)), b_spec=pl.BlockSpec((D, D), lambda i, t: (0, t)),
              o_spec=pl.BlockSpec((TM, D), lambda i, t: (i, 0)), out_shape=(S, D), out_dtype=F32, acc_shape=(TM, D))
    wide, per = D // MXU_W, nt * D // N_CHIPS // MXU_W
    piece = lambda r: pl.BlockSpec((None, TM, MXU_W), lambda j, k: ((per * j + r) // wide, k, (per * j + r) % wide))
    d_w_qkv = _mm("qkv_bwd_w", u2, dqkv, kind="tn", grid=(N_CHIPS, S // TM),
                  a_spec=pl.BlockSpec((TM, D), lambda j, k: (k, 0)), b_spec=[piece(r) for r in range(per)],
                  o_spec=pl.BlockSpec((None, D, per * MXU_W), lambda j, k: (j, 0, 0)),
                  out_shape=(N_CHIPS, D, per * MXU_W), out_dtype=BF16, acc_shape=(D, per * MXU_W))
    shard_cols = lambda a: jnp.moveaxis(a.reshape(a.shape[0], N_CHIPS, a.shape[1] // N_CHIPS), 1, 0)
    ex.grads("od", {"od_w_qkv": d_w_qkv, "od_w_out": d_w_od_out.reshape(N_CHIPS, D // N_CHIPS, D)})
    dh2, dg_mix1 = _rms_bwd("rms_mix1_bwd", du2, h2, g_mix[1:2], dh3, carry=ex.carry("rms_mix1_bwd"))
    d_rel = _bias_table_grad("rel_bias_grad", dr0.reshape(C_H, TOEP_W))[:, :2 * REL_CLIP + 1]

    dh1, dg_ffn0, u1, dgate, dup, act = _ffn_bwd("ffn0_bwd", dh2, h1, g_ffn[0:1], gate0, up0,
                                                 w["w_gate0"], w["w_up0"], w["w_down0"])
    d_wg0, d_wu0, d_wd0 = _ffn_wgrads("ffn0_dw", u1, dgate, dup, act, dh2)
    ex.grads("ffn0", {"w_gate0": d_wg0, "w_up0": d_wu0, "w_down0": d_wd0})

    d_oev, d_w_ev_out = _out_proj_bwd("ev_out_bwd", dh1, o_ev, w["ev_w_out"], ex)
    ex.grads("ev_out", {"ev_w_out": d_w_ev_out.reshape(N_CHIPS, D // N_CHIPS, D)})
    dqa, dka, dva = riding("mla_attn_bwd", _mla_bwd, qa, ka, va, o_a, lse, d_oev, 0)
    dqb, dkb, dvb = riding("sb_attn_bwd", _sb_bwd, proj, d_oev, MLA_H * MLA_V // LANES)
    dcq, dckv, dkr, d_w_uq, d_w_uk, d_w_uv, dg_cq, dg_ckv = _mla_prep_bwd(
        "mla_prep_bwd", dqa, dka, dva, proj, sm["ev_g_cq"], sm["ev_g_ckv"], w["w_uq"], w["w_uk"], w["w_uv"], cos_t, sin_t)
    dproj = jnp.concatenate([dcq, jnp.zeros((S, LANES), BF16), dckv, dqb, dkb, dvb, dkr], axis=1)
    d_w_in_p = _mm("proj_in_bwd_w", u0, dproj, kind="tn", grid=(1, S // TM),
                   a_spec=pl.BlockSpec((TM, D), lambda j, k: (k, 0)), b_spec=pl.BlockSpec((TM, P_IN), lambda j, k: (k, 0)),
                   o_spec=pl.BlockSpec((D, P_IN), lambda j, k: (0, 0)), out_shape=(D, P_IN), out_dtype=BF16,
                   acc_shape=(D, P_IN), carry=ex.carry("proj_in_bwd_w"))
    d_w_uq_std = d_w_uq.reshape(Q_LORA, MLA_H, LANES)[:, :, :MLA_NOPE + MLA_ROPE].reshape(Q_LORA, -1)
    d_w_ukv = jnp.concatenate([d_w_uk.reshape(KV_LORA, MLA_H, LANES)[:, :, :MLA_NOPE],
                               d_w_uv.reshape(KV_LORA, MLA_H, MLA_V)], axis=2).reshape(KV_LORA, -1)
    ex.grads("ev", {"ev_w_in": _w_in_sharded("w_in_sharded", d_w_in_p), "ev_w_uq": shard_cols(d_w_uq_std.astype(BF16)),
                    "ev_w_ukv": shard_cols(d_w_ukv.astype(BF16))})
    du0 = _mm("proj_in_bwd_x", dproj, w["w_in"], kind="nt", grid=(S // TM, 1, 1),
              a_spec=pl.BlockSpec((TM, P_IN), lambda i, j, k: (i, 0)), b_spec=pl.BlockSpec((D, P_IN), lambda i, j, k: (0, 0)),
              o_spec=pl.BlockSpec((TM, D), lambda i, j, k: (i, 0)), out_shape=(S, D), out_dtype=F32, acc_shape=None,
              carry=ex.carry("proj_in_bwd_x"))
    grad_x, dg_mix0 = _rms_bwd("rms_mix0_bwd", du0, x, g_mix[0:1], dh1)
    small = {
        "ev_g_cq": dg_cq, "ev_g_ckv": dg_ckv, "od_rel_bias": d_rel.reshape(1, C_H, 2 * REL_CLIP + 1),
        "g_mix": jnp.concatenate([dg_mix0, dg_mix1], axis=0), "g_ffn": jnp.concatenate([dg_ffn0, dg_ffn1], axis=0),
        "g_final": dg_final.reshape(D),
    }
    return loss, grad_x, small


BIG = ("ev_w_in", "ev_w_uq", "ev_w_ukv", "ev_w_out", "od_w_qkv", "od_w_out", "w_gate", "w_up", "w_down")
SMALL = ("ev_g_cq", "ev_g_ckv", "od_rel_bias", "g_mix", "g_ffn", "g_final")
WEIGHTS = ("ev_w_in", "ev_g_cq", "ev_w_uq", "ev_g_ckv", "ev_w_ukv", "ev_w_out", "od_w_qkv", "od_rel_bias", "od_w_out",
           "g_mix", "g_ffn", "w_gate", "w_up", "w_down", "g_final")
GRAD_PARTS = (("ev_w_in", "ev_w_in", 0), ("ev_w_uq", "ev_w_uq", 0), ("ev_w_ukv", "ev_w_ukv", 0),
              ("ev_w_out", "ev_w_out", 0), ("od_w_qkv", "od_w_qkv", 0), ("od_w_out", "od_w_out", 0),
              ("w_gate0", "w_gate", 0), ("w_gate1", "w_gate", 1), ("w_up0", "w_up", 0), ("w_up1", "w_up", 1),
              ("w_down0", "w_down", 0), ("w_down1", "w_down", 1))
PART_OF = {part: (param, layer) for part, param, layer in GRAD_PARTS}
TRANSPOSED = ("w_gate", "w_up")
ADAMW_TRANSPOSED = ("ev_w_in", "ev_w_uq")


def _row_tile(rows, cap=512, sublanes=16):
    for t in range(min(rows, cap), 0, -1):
        if rows % t == 0 and t % sublanes == 0:
            return t
    return rows


def _cast_into_slot(name, w, layer, pos):
    _, rows, cols = w.shape
    tr = _row_tile(rows)

    def body(pos_ref, w_ref, o_ref):
        o_ref[...] = w_ref[...].astype(BF16)

    return pl.pallas_call(
        body, name=name,
        grid_spec=pltpu.PrefetchScalarGridSpec(
            num_scalar_prefetch=1, grid=(rows // tr,),
            in_specs=[pl.BlockSpec((None, tr, cols), lambda i, p: (layer, i, 0))],
            out_specs=pl.BlockSpec((None, tr, cols), lambda i, p: (p[0], i, 0))),
        out_shape=jax.ShapeDtypeStruct((N_CHIPS, rows, cols), BF16), compiler_params=_params("arbitrary"))(pos, w)


def _cast_many_into_slots(name, items, pos, carry):
    tiles = [_row_tile(w.shape[1]) for w, _ in items]
    turns = _Turns([w.shape[1] // tr for (w, _), tr in zip(items, tiles)])

    def body(pos_ref, *refs):
        i = pl.program_id(0)
        for t in range(len(items)):
            @pl.when(turns.mine(t, i))
            def _(w_ref=refs[t], o_ref=refs[len(items) + t]):
                o_ref[...] = w_ref[...].astype(BF16)

    in_specs, out_specs, out_shape = [], [], []
    for t, ((w, layer), tr) in enumerate(zip(items, tiles)):
        _, rows, cols = w.shape
        at = turns.step(t)
        in_specs.append(pl.BlockSpec((None, tr, cols), lambda i, p, at=at, layer=layer: (layer, at(i), 0)))
        out_specs.append(pl.BlockSpec((None, tr, cols), lambda i, p, at=at: (p[0], at(i), 0)))
        out_shape.append(jax.ShapeDtypeStruct((N_CHIPS, rows, cols), BF16))
    res, copies = _carrier_call(body, name=name, grid=(turns.total,), in_specs=in_specs, out_specs=out_specs,
                                out_shape=out_shape, args=[w for w, _ in items], sem=("arbitrary",), carry=carry,
                                prefetch=(pos,))
    if carry is not None:
        carry.done(copies)
    return res


class _Turns:
    def __init__(self, counts):
        self.counts = list(counts)
        self.starts = [sum(self.counts[:t]) for t in range(len(self.counts))]
        self.total = sum(self.counts)

    def step(self, t):
        start, n = self.starts[t], self.counts[t]
        return lambda i: jnp.clip(i - start, 0, n - 1)

    def mine(self, t, i):
        return (i >= self.starts[t]) & (i < self.starts[t] + self.counts[t])


def _pair_sums(name, parts, theirs, pos):
    n = len(parts)
    tiles = [_row_tile(b.shape[1]) for b in theirs]
    blocks = [b.shape[1] // tr for b, tr in zip(theirs, tiles)]
    turns = _Turns([N_CHIPS * nb for nb in blocks])

    def body(pos_ref, *refs):
        i = pl.program_id(0)
        for t in range(n):
            @pl.when(turns.mine(t, i))
            def _(a_ref=refs[2 * t], b_ref=refs[2 * t + 1], o_ref=refs[2 * n + t]):
                o_ref[...] = (a_ref[...].astype(F32) + b_ref[...].astype(F32)).astype(BF16)

    in_specs, out_specs = [], []
    for t, (b, tr, nb) in enumerate(zip(theirs, tiles, blocks)):
        at, block = turns.step(t), (None, tr, b.shape[2])
        in_specs.append(pl.BlockSpec(block, lambda i, p, at=at, nb=nb: (at(i) // nb, p[1] * nb + at(i) % nb, 0)))
        in_specs.append(pl.BlockSpec(block, lambda i, p, at=at, nb=nb: (at(i) // nb, at(i) % nb, 0)))
        out_specs.append(pl.BlockSpec(block, lambda i, p, at=at, nb=nb: (at(i) // nb, at(i) % nb, 0)))
    return pl.pallas_call(
        body, name=name,
        grid_spec=pltpu.PrefetchScalarGridSpec(num_scalar_prefetch=1, grid=(turns.total,), in_specs=in_specs,
                                               out_specs=out_specs),
        out_shape=[jax.ShapeDtypeStruct(b.shape, BF16) for b in theirs],
        compiler_params=_params("arbitrary"))(pos, *[a for pair in zip(parts, theirs) for a in pair])


def _chip_sums(name, items, pos):
    n = len(items)
    tiles = [_row_tile(s.shape[1]) for s, *_ in items]
    turns = _Turns([s.shape[1] // tr for (s, *_), tr in zip(items, tiles)])
    carried = [t for t, item in enumerate(items) if item[4] is not None]

    def body(pos_ref, *refs):
        i = pl.program_id(0)
        for t in range(n):
            @pl.when(turns.mine(t, i))
            def _(s_ref=refs[2 * t], g_ref=refs[2 * t + 1], o_ref=refs[2 * n + len(carried) + t]):
                o_ref[...] = ((s_ref[...].astype(F32) + g_ref[0].astype(F32)) + g_ref[1].astype(F32)) + g_ref[2].astype(F32)

    in_specs, out_specs = [], []
    for t, ((s, got, layer, full_shape, full), tr) in enumerate(zip(items, tiles)):
        at, cols, nb = turns.step(t), s.shape[2], turns.counts[t]
        in_specs.append(pl.BlockSpec((None, tr, cols), lambda i, p, at=at: (p[0], at(i), 0)))
        in_specs.append(pl.BlockSpec((3, tr, cols), lambda i, p, at=at: (0, at(i), 0)))
        out_specs.append(pl.BlockSpec((None, tr, cols), lambda i, p, at=at, nb=nb, layer=layer: (layer, p[1] * nb + at(i), 0)))
    return pl.pallas_call(
        body, name=name,
        grid_spec=pltpu.PrefetchScalarGridSpec(num_scalar_prefetch=1, grid=(turns.total,),
                                               in_specs=in_specs + [ANY] * len(carried), out_specs=out_specs),
        out_shape=[jax.ShapeDtypeStruct(item[3], F32) for item in items],
        input_output_aliases={1 + 2 * n + k: t for k, t in enumerate(carried)},
        compiler_params=_params("arbitrary"))(
            pos, *[a for item in items for a in item[:2]], *[items[t][4] for t in carried])


def _adamw_update(w, g, m, v):
    m_new = ADAM_B1 * m + (1.0 - ADAM_B1) * g
    v_new = ADAM_B2 * v + (1.0 - ADAM_B2) * (g * g)
    m_hat = m_new / (1.0 - ADAM_B1 ** ADAM_STEP)
    v_hat = v_new / (1.0 - ADAM_B2 ** ADAM_STEP)
    return -ADAM_LR * (m_hat / (jnp.sqrt(v_hat) + ADAM_EPS) + ADAM_WD * w), m_new, v_new


def _small_step(name, grads, loss, w, m, v):
    n, n_dev = len(grads), 8
    offs = [sum(g.shape[0] for g in grads[:t]) for t in range(n + 1)]
    rows = -(-(offs[n] + 1) // 8) * 8
    width = max(g.shape[1] for g in grads)

    def body(*refs):
        g_refs, loss_ref = refs[:n], refs[n]
        w_refs, m_refs, v_refs = (refs[1 + k * n:1 + (k + 1) * n] for k in (1, 2, 3))
        outs = refs[4 * n + 1:8 * n + 2]
        mine, slots, send_sem, recv_sem = refs[8 * n + 2:]
        x, y, c, _ = _position()
        me = 4 * x + 2 * y + c

        def peer(k):
            return (1 - x if k & 4 else x, 1 - y if k & 2 else y, 1 - c if k & 1 else c)

        def logical(k):
            px, py, pc = peer(k)
            return 4 * px + 2 * py + pc

        mine[...] = jnp.zeros(mine.shape, F32)
        for t in range(n):
            mine[offs[t]:offs[t + 1], 0:grads[t].shape[1]] = g_refs[t][...]
        mine[offs[n]:offs[n] + 1, 0:LANES] = loss_ref[...]
        slots[me] = mine[...]
        sends = [pltpu.make_async_remote_copy(
            src_ref=mine, dst_ref=slots.at[me], send_sem=send_sem.at[k], recv_sem=recv_sem.at[k],
            device_id=peer(k), device_id_type=MESH) for k in range(1, n_dev)]
        for cp in sends:
            cp.start()
        for k in range(1, n_dev):
            pltpu.make_async_remote_copy(
                src_ref=mine, dst_ref=slots.at[logical(k)], send_sem=send_sem.at[k], recv_sem=recv_sem.at[k],
                device_id=peer(k), device_id_type=MESH).wait_recv()
        for cp in sends:
            cp.wait_send()
        total = slots[0]
        for d in range(1, n_dev):
            total = total + slots[d]
        for t in range(n):
            gv = total[offs[t]:offs[t + 1], 0:grads[t].shape[1]]
            outs[t][...] = gv
            outs[n + t][...], outs[2 * n + t][...], outs[3 * n + t][...] = _adamw_update(
                w_refs[t][...], gv, m_refs[t][...], v_refs[t][...])
        outs[4 * n][...] = total[offs[n]:offs[n] + 1, 0:LANES]

    vm = pl.BlockSpec(memory_space=pltpu.VMEM)
    shapes = [jax.ShapeDtypeStruct(g.shape, F32) for g in grads]
    res = pl.pallas_call(
        body, name=name, in_specs=[vm] * (4 * n + 1), out_specs=[vm] * (4 * n + 1),
        out_shape=shapes * 4 + [jax.ShapeDtypeStruct(loss.shape, F32)],
        scratch_shapes=[pltpu.VMEM((rows, width), F32), pltpu.VMEM((n_dev, rows, width), F32),
                        pltpu.SemaphoreType.DMA((n_dev,)), pltpu.SemaphoreType.DMA((n_dev,))],
    )(*grads, loss, *w, *m, *v)
    return [res[k * n:(k + 1) * n] for k in range(4)], res[4 * n]


def _adamw(name, w, g, m, v):
    rows, cols = w.shape
    tr = _row_tile(rows, sublanes=8)

    def body(w_ref, g_ref, m_ref, v_ref, d_ref, mo_ref, vo_ref):
        d_ref[...], mo_ref[...], vo_ref[...] = _adamw_update(w_ref[...], g_ref[...], m_ref[...], v_ref[...])

    spec = pl.BlockSpec((tr, cols), lambda i: (i, 0))
    shape = jax.ShapeDtypeStruct((rows, cols), F32)
    return pl.pallas_call(body, name=name, grid=(rows // tr,), in_specs=[spec] * 4, out_specs=[spec] * 3,
                          out_shape=[shape] * 3, compiler_params=_params("parallel"))(w, g, m, v)


def kernel(x, ev_w_in, ev_g_cq, ev_w_uq, ev_g_ckv, ev_w_ukv, ev_w_out, od_w_qkv, od_rel_bias, od_w_out, g_mix, g_ffn, w_gate, w_up, w_down, g_final, loss_target, m_ev_w_in, m_ev_g_cq, m_ev_w_uq, m_ev_g_ckv, m_ev_w_ukv, m_ev_w_out, m_od_w_qkv, m_od_rel_bias, m_od_w_out, m_g_mix, m_g_ffn, m_w_gate, m_w_up, m_w_down, m_g_final, v_ev_w_in, v_ev_g_cq, v_ev_w_uq, v_ev_g_ckv, v_ev_w_ukv, v_ev_w_out, v_od_w_qkv, v_od_rel_bias, v_od_w_out, v_g_mix, v_g_ffn, v_w_gate, v_w_up, v_w_down, v_g_final):
    w = dict(ev_w_in=ev_w_in, ev_g_cq=ev_g_cq, ev_w_uq=ev_w_uq, ev_g_ckv=ev_g_ckv, ev_w_ukv=ev_w_ukv, ev_w_out=ev_w_out,
             od_w_qkv=od_w_qkv, od_rel_bias=od_rel_bias, od_w_out=od_w_out, g_mix=g_mix, g_ffn=g_ffn, w_gate=w_gate,
             w_up=w_up, w_down=w_down, g_final=g_final)
    m = dict(ev_w_in=m_ev_w_in, ev_g_cq=m_ev_g_cq, ev_w_uq=m_ev_w_uq, ev_g_ckv=m_ev_g_ckv, ev_w_ukv=m_ev_w_ukv,
             ev_w_out=m_ev_w_out, od_w_qkv=m_od_w_qkv, od_rel_bias=m_od_rel_bias, od_w_out=m_od_w_out, g_mix=m_g_mix,
             g_ffn=m_g_ffn, w_gate=m_w_gate, w_up=m_w_up, w_down=m_w_down, g_final=m_g_final)
    v = dict(ev_w_in=v_ev_w_in, ev_g_cq=v_ev_g_cq, ev_w_uq=v_ev_w_uq, ev_g_ckv=v_ev_g_ckv, ev_w_ukv=v_ev_w_ukv,
             ev_w_out=v_ev_w_out, od_w_qkv=v_od_w_qkv, od_rel_bias=v_od_rel_bias, od_w_out=v_od_w_out, g_mix=v_g_mix,
             g_ffn=v_g_ffn, w_gate=v_w_gate, w_up=v_w_up, w_down=v_w_down, g_final=v_g_final)
    flat2d = lambda a: a.reshape(-1, a.shape[-1])
    for tree in (w, m, v):
        for n in TRANSPOSED:
            tree[n] = jnp.swapaxes(tree[n], 1, 2)

    pos = jnp.stack([2 * lax.axis_index("x") + lax.axis_index("y"), lax.axis_index("c")]).astype(jnp.int32)

    slots = {part: _cast_into_slot("cast_" + part, w[n], layer, pos) for part, n, layer in GRAD_PARTS
             if part in FIRST_WEIGHTS}
    rest = [(part, n, layer) for part, n, layer in GRAD_PARTS if part not in FIRST_WEIGHTS]

    def cast_rest(carry):
        return dict(zip([part for part, _, _ in rest],
                        _cast_many_into_slots("cast_rest", [(w[n], layer) for _, n, layer in rest], pos, carry)))

    ex = _Exchanges(slots, pos, {n: w[n].shape for n in BIG}, cast_rest)

    loss_local, grad_x, small = _local_step(x[0], loss_target[0], {n: w[n] for n in SMALL}, ex)

    grads = ex.finish()
    delta, new_m, new_v = {}, {}, {}
    small_out, loss = _small_step("small_step", [flat2d(small[n]) for n in SMALL], loss_local,
                                  *([flat2d(t[n]) for n in SMALL] for t in (w, m, v)))
    for tree, outs in zip((grads, delta, new_m, new_v), small_out):
        tree.update({n: o.reshape(w[n].shape) for n, o in zip(SMALL, outs)})

    for n in BIG:
        turn = (lambda a: jnp.swapaxes(a, 1, 2)) if n in ADAMW_TRANSPOSED else (lambda a: a)
        shape = turn(w[n]).shape
        outs = _adamw("adamw_" + n, *(flat2d(turn(a)) for a in (w[n], grads[n], m[n], v[n])))
        delta[n], new_m[n], new_v[n] = (turn(o.reshape(shape)) for o in outs)
    for tree in (grads, delta, new_m, new_v):
        for n in TRANSPOSED:
            tree[n] = jnp.swapaxes(tree[n], 1, 2)

    return (loss[0, 0], grad_x[None], *[grads[n] for n in WEIGHTS], *[delta[n] for n in WEIGHTS],
            *[new_m[n] for n in WEIGHTS], *[new_v[n] for n in WEIGHTS])
```

```python
import functools

import jax
import jax.numpy as jnp
import numpy as np
from jax import lax
from jax.experimental import pallas as pl
from jax.experimental.pallas import tpu as pltpu

F32 = jnp.float32
BF16 = jnp.bfloat16

S = 2048
D = 1024
CHUNK = 64
MLA_H, MLA_NOPE, MLA_ROPE, MLA_V = 8, 64, 32, 64
Q_LORA, KV_LORA = 384, 256
ROPE_THETA = 10000.0
SB_H, SB_DIM = 8, 64
C_H, C_DIM = 16, 64
LEFT_CHUNKS = 8
REL_CLIP = 256
D_FF = 2816
EVEN_IN = 2208
RMS_EPS = 1e-6
ADAM_LR, ADAM_B1, ADAM_B2, ADAM_EPS, ADAM_WD, ADAM_STEP = 0.001, 0.9, 0.999, 1e-08, 0.01, 10

N_CHIPS = 4
FF_SHARD = D_FF // N_CHIPS
SCALE_A = (MLA_NOPE + MLA_ROPE) ** -0.5
SCALE_B = SB_DIM ** -0.5
SCALE_C = C_DIM ** -0.5
NEG = -1e30
LOG2_E = 1.4426950408889634

LANES = 128
MXU_W = 256
VMEM_LIMIT_BYTES = 56 * 1024 * 1024
TM = 512
QB = 512
BQ = 256

P_CQ, P_CKV, P_QB, P_KB, P_VB, P_KR = 0, 512, 768, 1280, 1792, 2304
P_IN = 2432
KR_LANE = 64
BAND_W = BQ + LEFT_CHUNKS * CHUNK
BAND_PAD = 512
TOEP_W = 1024


def _params(*sem):
    return pltpu.CompilerParams(dimension_semantics=sem, vmem_limit_bytes=VMEM_LIMIT_BYTES)


MESH = pl.DeviceIdType.MESH
ANY = pl.BlockSpec(memory_space=pl.ANY)


def _position():
    x, y, c = lax.axis_index("x"), lax.axis_index("y"), lax.axis_index("c")
    other_chips = [(1 - x, y), (x, 1 - y), (1 - x, 1 - y)]
    return x, y, c, other_chips


def _half_rows(c, half):
    return pl.ds(pl.multiple_of(c * half, 16), half)


def _remote(ref_src, ref_dst, send, recv, k, device):
    return pltpu.make_async_remote_copy(src_ref=ref_src, dst_ref=ref_dst, send_sem=send.at[k], recv_sem=recv.at[k],
                                        device_id=device, device_id_type=MESH)


class _Carry:
    def __init__(self):
        self.operands, self.aliased, self.fresh = [], [], []
        self.n_sems = 0
        self.starts, self.finishes, self.on_done = [], [], []

    def operand(self, arr, aliased):
        for i, a in enumerate(self.operands):
            if a is arr:
                return i
        self.operands.append(arr)
        self.aliased.append(aliased)
        return len(self.operands) - 1

    def result(self, shape, dtype):
        self.fresh.append(jax.ShapeDtypeStruct(shape, dtype))
        return len(self.fresh) - 1

    def sems(self, k):
        base = self.n_sems
        self.n_sems += k
        return base

    def done(self, results):
        aliased, fresh = results
        for f in self.on_done:
            f(aliased, fresh)


def _carrier_call(body, *, name, grid, in_specs, out_specs, out_shape, args, sem, scratch_shapes=(), carry=None,
                  prefetch=()):
    in_specs, out_specs, out_shape, scratch = list(in_specs), list(out_specs), list(out_shape), list(scratch_shapes)
    n_pre = len(prefetch)

    def call(kernel, in_specs, out_specs, out_shape, scratch, aliases, sem):
        return pl.pallas_call(
            kernel, name=name, out_shape=out_shape, input_output_aliases=aliases, compiler_params=_params(*sem),
            grid_spec=pltpu.PrefetchScalarGridSpec(num_scalar_prefetch=n_pre, grid=grid, in_specs=in_specs,
                                                   out_specs=out_specs, scratch_shapes=scratch))

    if carry is None:
        return list(call(body, in_specs, out_specs, out_shape, scratch, {}, sem)(*prefetch, *args)), None
    ops = carry.operands
    alias_idx = [i for i, a in enumerate(carry.aliased) if a]
    c_shapes = [jax.ShapeDtypeStruct(ops[i].shape, ops[i].dtype) for i in alias_idx] + carry.fresh
    n_in, n_out, n_scr = len(args), len(out_shape), len(scratch)

    def wrapped(*refs):
        pre, refs = refs[:n_pre], refs[n_pre:]
        ins, c_ins = refs[:n_in], refs[n_in:n_in + len(ops)]
        o0 = n_in + len(ops)
        outs, c_outs = refs[o0:o0 + n_out], refs[o0 + n_out:o0 + n_out + len(c_shapes)]
        s0 = o0 + n_out + len(c_shapes)
        scr, send, recv = refs[s0:s0 + n_scr], refs[s0 + n_scr], refs[s0 + n_scr + 1]
        use = list(c_ins)
        for k, i in enumerate(alias_idx):
            use[i] = c_outs[k]
        fresh = c_outs[len(alias_idx):]

        def run(steps):
            for step in steps:
                step(use, fresh, send, recv)

        if not grid:
            run(carry.starts)
            if body is not None:
                body(*pre, *ins, *outs, *scr)
            run(carry.finishes)
            return
        ids = [pl.program_id(a) for a in range(len(grid))]
        first = functools.reduce(jnp.logical_and, [i == 0 for i in ids])
        last = functools.reduce(jnp.logical_and, [i == g - 1 for i, g in zip(ids, grid)])

        @pl.when(first)
        def _():
            run(carry.starts)

        body(*pre, *ins, *outs, *scr)

        @pl.when(last)
        def _():
            run(carry.finishes)

    res = call(wrapped, in_specs + [ANY] * len(ops), out_specs + [ANY] * len(c_shapes), out_shape + c_shapes,
               scratch + [pltpu.SemaphoreType.DMA((carry.n_sems,)), pltpu.SemaphoreType.DMA((carry.n_sems,))],
               {n_pre + n_in + i: n_out + k for k, i in enumerate(alias_idx)},
               ("arbitrary",) * len(grid))(*prefetch, *args, *ops)
    res = list(res)
    c_res = res[n_out:]
    return res[:n_out], ({i: c_res[k] for k, i in enumerate(alias_idx)}, c_res[len(alias_idx):])


_DIMS = {"nn": (((1,), (0,)), ((), ())), "nt": (((1,), (1,)), ((), ())), "tn": (((0,), (0,)), ((), ()))}


def _dot(a, b, kind="nn"):
    return lax.dot_general(a, b, _DIMS[kind], preferred_element_type=F32)


def _iota(shape, dim):
    return lax.broadcasted_iota(jnp.int32, shape, dim)


def _sigmoid(x):
    return 1.0 / (1.0 + jnp.exp(-x))


def _split_dot(x, tri):
    hi = x.astype(BF16)
    lo = (x - hi.astype(F32)).astype(BF16)
    both = _dot(jnp.concatenate([hi, lo], axis=0), tri)
    return both[:x.shape[0]] + both[x.shape[0]:]


def _running_sum(x, tri, reverse):
    n = x.shape[1] // MXU_W
    blocks = [x[:, b * MXU_W:(b + 1) * MXU_W] for b in range(n)]
    out = [None] * n
    carry = None
    for b in (range(n - 1, -1, -1) if reverse else range(n)):
        part = _split_dot(blocks[b], tri)
        out[b] = part if carry is None else part + carry
        total = jnp.sum(blocks[b], axis=-1, keepdims=True)
        carry = total if carry is None else carry + total
    return (jnp.concatenate(out, axis=1) if n > 1 else out[0]), carry


def _mm(name, a, b, *, kind, grid, a_spec, b_spec, o_spec, out_shape, out_dtype, acc_shape, resid=None, r_spec=None,
        carry=None):
    nk = grid[-1]
    has_r = resid is not None
    several = lambda x: list(x) if isinstance(x, (tuple, list)) else [x]
    a_specs, b_specs = several(a_spec), several(b_spec)
    na, nb = len(a_specs), len(b_specs)
    a_args = several(a) if isinstance(a, (tuple, list)) else [a] * na
    b_args = several(b) if isinstance(b, (tuple, list)) else [b] * nb

    def body(*refs):
        r_ref = refs[na + nb] if has_r else None
        o_ref = refs[na + nb + has_r]
        side_by_side = lambda rs: rs[0][...] if len(rs) == 1 else jnp.concatenate([r[...].astype(BF16) for r in rs], axis=1)
        part = _dot(side_by_side(refs[:na]).astype(BF16), side_by_side(refs[na:na + nb]).astype(BF16), kind)

        def finish(total):
            if has_r:
                total = total + r_ref[...].astype(F32)
            o_ref[...] = total.astype(out_dtype)

        if nk == 1:
            finish(part)
        else:
            acc_ref = refs[na + nb + has_r + 1]
            k = pl.program_id(len(grid) - 1)

            @pl.when(k == 0)
            def _():
                acc_ref[...] = part

            @pl.when(k > 0)
            def _():
                acc_ref[...] += part

            @pl.when(k == nk - 1)
            def _():
                finish(acc_ref[...])

    in_specs = a_specs + b_specs + ([r_spec] if has_r else [])
    args = (*a_args, *b_args) + ((resid,) if has_r else ())
    sem = ("parallel",) * (len(grid) - 1) + ("arbitrary",)
    res, copies = _carrier_call(
        body, name=name, grid=grid, in_specs=in_specs, out_specs=[o_spec],
        out_shape=[jax.ShapeDtypeStruct(out_shape, out_dtype)],
        scratch_shapes=[pltpu.VMEM(acc_shape, F32)] if nk > 1 else [], args=args, sem=sem, carry=carry)
    if carry is not None:
        carry.done(copies)
    return res[0]


def _rms_fwd(name, x, g, col_block=0):
    c = g.shape[1]

    def body(x_ref, g_ref, u_ref):
        xv = x_ref[...]
        r = lax.rsqrt(jnp.mean(xv * xv, axis=-1, keepdims=True) + RMS_EPS)
        u_ref[...] = (xv * r * g_ref[...]).astype(BF16)

    return pl.pallas_call(
        body, name=name, grid=(S // TM,),
        in_specs=[pl.BlockSpec((TM, c), lambda i: (i, col_block)), pl.BlockSpec((1, c), lambda i: (0, 0))],
        out_specs=pl.BlockSpec((TM, c), lambda i: (i, 0)),
        out_shape=jax.ShapeDtypeStruct((S, c), BF16),
        compiler_params=_params("parallel"),
    )(x, g)


def _rms_bwd(name, dy, x, g, resid, carry=None):
    def body(dy_ref, x_ref, g_ref, r_ref, dx_ref, dg_ref):
        i = pl.program_id(0)
        xv = x_ref[...]
        r = lax.rsqrt(jnp.mean(xv * xv, axis=-1, keepdims=True) + RMS_EPS)
        xh = xv * r
        dyv = dy_ref[...]
        dxh = dyv * g_ref[...]
        dx_ref[...] = r_ref[...] + r * (dxh - xh * jnp.mean(dxh * xh, axis=-1, keepdims=True))
        part = jnp.sum(dyv * xh, axis=0, keepdims=True)

        @pl.when(i == 0)
        def _():
            dg_ref[...] = part

        @pl.when(i > 0)
        def _():
            dg_ref[...] += part

    row = pl.BlockSpec((TM, D), lambda i: (i, 0))
    vec = pl.BlockSpec((1, D), lambda i: (0, 0))
    res, copies = _carrier_call(
        body, name=name, grid=(S // TM,), in_specs=[row, row, vec, row], out_specs=[row, vec],
        out_shape=[jax.ShapeDtypeStruct((S, D), F32), jax.ShapeDtypeStruct((1, D), F32)],
        args=(dy, x, g, resid), sem=("arbitrary",), carry=carry)
    if carry is not None:
        carry.done(copies)
    return res


def _loss_bwd(name, h, g, tgt):
    def body(h_ref, g_ref, t_ref, loss_ref, dh_ref, dg_ref):
        i = pl.program_id(0)
        xv = h_ref[...]
        gv = g_ref[...]
        r = lax.rsqrt(jnp.mean(xv * xv, axis=-1, keepdims=True) + RMS_EPS)
        xh = xv * r
        diff = xh * gv - t_ref[...]
        part_loss = 0.5 * jnp.sum(jnp.sum(diff * diff, axis=-1, keepdims=True) * (1.0 / D), axis=0, keepdims=True)
        dy = diff * (1.0 / D)
        dxh = dy * gv
        dh_ref[...] = r * (dxh - xh * jnp.mean(dxh * xh, axis=-1, keepdims=True))
        part_g = jnp.sum(dy * xh, axis=0, keepdims=True)

        @pl.when(i == 0)
        def _():
            dg_ref[...] = part_g
            loss_ref[...] = jnp.broadcast_to(part_loss, (1, LANES))

        @pl.when(i > 0)
        def _():
            dg_ref[...] += part_g
            loss_ref[...] += jnp.broadcast_to(part_loss, (1, LANES))

    row = pl.BlockSpec((TM, D), lambda i: (i, 0))
    vec = pl.BlockSpec((1, D), lambda i: (0, 0))
    return pl.pallas_call(
        body, name=name, grid=(S // TM,), in_specs=[row, vec, row],
        out_specs=[pl.BlockSpec((1, LANES), lambda i: (0, 0)), row, vec],
        out_shape=[jax.ShapeDtypeStruct((1, LANES), F32), jax.ShapeDtypeStruct((S, D), F32),
                   jax.ShapeDtypeStruct((1, D), F32)],
        compiler_params=_params("arbitrary"),
    )(h, g, tgt)


def _ffn_fwd(name, h, g, wg, wu, wd, carry=None):
    def body(h_ref, g_ref, wg_ref, wu_ref, wd_ref, o_ref, gate_ref, up_ref, u_scr):
        s = pl.program_id(1)

        @pl.when(s == 0)
        def _():
            xv = h_ref[...]
            r = lax.rsqrt(jnp.mean(xv * xv, axis=-1, keepdims=True) + RMS_EPS)
            u_scr[...] = (xv * r * g_ref[...]).astype(BF16)
            o_ref[...] = xv

        u = u_scr[...]
        gate = _dot(u, wg_ref[...], "nt")
        up = _dot(u, wu_ref[...], "nt")
        act = gate * _sigmoid(gate) * up
        o_ref[...] += _dot(act.astype(BF16), wd_ref[...])
        gate_ref[...] = gate.astype(BF16)
        up_ref[...] = up.astype(BF16)

    row = pl.BlockSpec((TM, D), lambda i, s: (i, 0))
    hid = pl.BlockSpec((None, TM, FF_SHARD), lambda i, s: (s, i, 0))
    return _carrier_call(
        body, name=name, grid=(S // TM, N_CHIPS),
        in_specs=[row, pl.BlockSpec((1, D), lambda i, s: (0, 0))]
        + [pl.BlockSpec((None, FF_SHARD, D), lambda i, s: (s, 0, 0))] * 3,
        out_specs=[row, hid, hid],
        out_shape=[jax.ShapeDtypeStruct((S, D), F32), jax.ShapeDtypeStruct((N_CHIPS, S, FF_SHARD), BF16),
                   jax.ShapeDtypeStruct((N_CHIPS, S, FF_SHARD), BF16)],
        scratch_shapes=[pltpu.VMEM((TM, D), BF16)], args=(h, g, wg, wu, wd), sem=("parallel", "arbitrary"), carry=carry)


def _ffn_bwd(name, dh, h, g, gate, up, wg, wu, wd):
    def body(dh_ref, h_ref, g_ref, gate_ref, up_ref, wg_ref, wu_ref, wd_ref,
             dhin_ref, dg_ref, u_ref, dgate_ref, dup_ref, act_ref, dhb_scr, du_scr):
        i = pl.program_id(0)
        s = pl.program_id(1)

        @pl.when(s == 0)
        def _():
            xv = h_ref[...]
            r = lax.rsqrt(jnp.mean(xv * xv, axis=-1, keepdims=True) + RMS_EPS)
            u_ref[...] = (xv * r * g_ref[...]).astype(BF16)
            dhb_scr[...] = dh_ref[...].astype(BF16)
            du_scr[...] = jnp.zeros_like(du_scr)

        dact = _dot(dhb_scr[...], wd_ref[...], "nt")
        gv = gate_ref[...].astype(F32)
        uv = up_ref[...].astype(F32)
        sig = _sigmoid(gv)
        sil = gv * sig
        dup = dact * sil
        dgate = dact * uv * (sig * (1.0 + gv * (1.0 - sig)))
        dgb = dgate.astype(BF16)
        dub = dup.astype(BF16)
        act_ref[...] = (sil * uv).astype(BF16)
        dgate_ref[...] = dgb
        dup_ref[...] = dub
        du_scr[...] += _dot(dgb, wg_ref[...]) + _dot(dub, wu_ref[...])

        @pl.when(s == N_CHIPS - 1)
        def _():
            xv = h_ref[...]
            r = lax.rsqrt(jnp.mean(xv * xv, axis=-1, keepdims=True) + RMS_EPS)
            xh = xv * r
            du = du_scr[...]
            dxh = du * g_ref[...]
            dhin_ref[...] = dh_ref[...] + r * (dxh - xh * jnp.mean(dxh * xh, axis=-1, keepdims=True))
            part = jnp.sum(du * xh, axis=0, keepdims=True)

            @pl.when(i == 0)
            def _():
                dg_ref[...] = part

            @pl.when(i > 0)
            def _():
                dg_ref[...] += part

    row = pl.BlockSpec((TM, D), lambda i, s: (i, 0))
    vec = pl.BlockSpec((1, D), lambda i, s: (0, 0))
    hid = pl.BlockSpec((None, TM, FF_SHARD), lambda i, s: (s, i, 0))
    hid_shape = jax.ShapeDtypeStruct((N_CHIPS, S, FF_SHARD), BF16)
    return pl.pallas_call(
        body, name=name, grid=(S // TM, N_CHIPS),
        in_specs=[row, row, vec, hid, hid] + [pl.BlockSpec((None, FF_SHARD, D), lambda i, s: (s, 0, 0))] * 3,
        out_specs=[row, vec, row, hid, hid, hid],
        out_shape=[jax.ShapeDtypeStruct((S, D), F32), jax.ShapeDtypeStruct((1, D), F32),
                   jax.ShapeDtypeStruct((S, D), BF16), hid_shape, hid_shape, hid_shape],
        scratch_shapes=[pltpu.VMEM((TM, D), BF16), pltpu.VMEM((TM, D), F32)],
        compiler_params=_params("arbitrary", "arbitrary"),
    )(dh, h, g, gate, up, wg, wu, wd)


def _ffn_wgrads(name, u, dgate, dup, act, dh):
    nk = S // TM

    def body(u_ref, dh_ref, dgate_ref, dup_ref, act_ref, dg_ref, du_ref, dd_ref, acc_g, acc_u, acc_d):
        k = pl.program_id(1)
        u = u_ref[...]
        parts = (_dot(dgate_ref[...], u, "tn"), _dot(dup_ref[...], u, "tn"),
                 _dot(act_ref[...], dh_ref[...].astype(BF16), "tn"))
        accs = (acc_g, acc_u, acc_d)

        @pl.when(k == 0)
        def _():
            for acc, part in zip(accs, parts):
                acc[...] = part

        @pl.when(k > 0)
        def _():
            for acc, part in zip(accs, parts):
                acc[...] += part

        @pl.when(k == nk - 1)
        def _():
            for out, acc in zip((dg_ref, du_ref, dd_ref), accs):
                out[...] = acc[...].astype(BF16)

    tok = pl.BlockSpec((TM, D), lambda s, k: (k, 0))
    hid = pl.BlockSpec((None, TM, FF_SHARD), lambda s, k: (s, k, 0))
    out = pl.BlockSpec((None, FF_SHARD, D), lambda s, k: (s, 0, 0))
    shape = jax.ShapeDtypeStruct((N_CHIPS, FF_SHARD, D), BF16)
    return pl.pallas_call(
        body, name=name, grid=(N_CHIPS, nk), in_specs=[tok, tok, hid, hid, hid], out_specs=[out, out, out],
        out_shape=[shape, shape, shape], scratch_shapes=[pltpu.VMEM((FF_SHARD, D), F32)] * 3,
        compiler_params=_params("parallel", "arbitrary"))(u, dh, dgate, dup, act)


def _rope_tables():
    pos = jnp.arange(S, dtype=F32)
    inv = ROPE_THETA ** (-jnp.arange(0, MLA_ROPE, 2, dtype=F32) / MLA_ROPE)
    ang = pos[:, None] * inv[None, :]
    half = MLA_ROPE // 2
    cos = jnp.cos(ang)
    sin = jnp.sin(ang)
    one = jnp.ones((S, KR_LANE), F32)
    zero = jnp.zeros((S, KR_LANE), F32)
    tail_one = jnp.ones((S, LANES - KR_LANE - MLA_ROPE), F32)
    tail_zero = jnp.zeros((S, LANES - KR_LANE - MLA_ROPE), F32)
    cos_t = jnp.concatenate([one, cos, cos, tail_one], axis=1)
    sin_t = jnp.concatenate([zero, -sin, sin, tail_zero], axis=1)
    assert cos_t.shape == (S, LANES) and half * 2 == MLA_ROPE
    return cos_t, sin_t


def _rope(x, cos_t, sin_t, sign):
    n = x.shape[1] // LANES
    half = MLA_ROPE // 2
    lane = _iota(x.shape, 1) & (LANES - 1)
    first = (lane >= KR_LANE) & (lane < KR_LANE + half)
    swapped = jnp.where(first, pltpu.roll(x, x.shape[1] - half, 1), pltpu.roll(x, half, 1))
    c = jnp.tile(cos_t, (1, n)) if n > 1 else cos_t
    s = jnp.tile(sin_t, (1, n)) if n > 1 else sin_t
    return x * c + swapped * (s * sign)


def _mla_prep_fwd(name, proj, g_cq, g_ckv, w_uq, w_uk, w_uv, cos_t, sin_t):
    nh = MLA_H * LANES

    def body(cq_ref, ckv_ref, kr_ref, gq_ref, gkv_ref, wq_ref, wk_ref, wv_ref, cos_ref, sin_ref,
             qa_ref, ka_ref, va_ref):
        cos_v, sin_v = cos_ref[...], sin_ref[...]
        cq = cq_ref[...]
        r = lax.rsqrt(jnp.mean(cq * cq, axis=-1, keepdims=True) + RMS_EPS)
        cqn = (cq * r * gq_ref[...]).astype(BF16)
        qa_ref[...] = _rope(_dot(cqn, wq_ref[...]), cos_v, sin_v, 1.0).astype(BF16)
        ckv = ckv_ref[...]
        r = lax.rsqrt(jnp.mean(ckv * ckv, axis=-1, keepdims=True) + RMS_EPS)
        ckvn = (ckv * r * gkv_ref[...]).astype(BF16)
        lane = _iota((TM, LANES), 1)
        rot = (lane >= KR_LANE) & (lane < KR_LANE + MLA_ROPE)
        kr = jnp.where(rot, _rope(kr_ref[...], cos_v, sin_v, 1.0), 0.0)
        ka_ref[...] = (_dot(ckvn, wk_ref[...]) + jnp.tile(kr, (1, MLA_H))).astype(BF16)
        va_ref[...] = _dot(ckvn, wv_ref[...]).astype(BF16)

    full = lambda shape: pl.BlockSpec(shape, lambda i: (0, 0))
    return pl.pallas_call(
        body, name=name, grid=(S // TM,),
        in_specs=[pl.BlockSpec((TM, Q_LORA), lambda i: (i, P_CQ // Q_LORA)),
                  pl.BlockSpec((TM, KV_LORA), lambda i: (i, P_CKV // KV_LORA)),
                  pl.BlockSpec((TM, LANES), lambda i: (i, P_KR // LANES)),
                  full((1, Q_LORA)), full((1, KV_LORA)), full((Q_LORA, nh)), full((KV_LORA, nh)),
                  full((KV_LORA, MLA_H * MLA_V)),
                  pl.BlockSpec((TM, LANES), lambda i: (i, 0)), pl.BlockSpec((TM, LANES), lambda i: (i, 0))],
        out_specs=[pl.BlockSpec((TM, nh), lambda i: (i, 0)), pl.BlockSpec((TM, nh), lambda i: (i, 0)),
                   pl.BlockSpec((TM, MLA_H * MLA_V), lambda i: (i, 0))],
        out_shape=[jax.ShapeDtypeStruct((S, nh), BF16), jax.ShapeDtypeStruct((S, nh), BF16),
                   jax.ShapeDtypeStruct((S, MLA_H * MLA_V), BF16)],
        compiler_params=_params("parallel"),
    )(proj, proj, proj, g_cq, g_ckv, w_uq, w_uk, w_uv, cos_t, sin_t)


def _mla_prep_bwd(name, dqa, dka, dva, proj, g_cq, g_ckv, w_uq, w_uk, w_uv, cos_t, sin_t):
    nh = MLA_H * LANES

    def body(dqa_ref, dka_ref, dva_ref, cq_ref, ckv_ref, gq_ref, gkv_ref, wq_ref, wk_ref, wv_ref, cos_ref, sin_ref,
             dcq_ref, dckv_ref, dkr_ref, dwq_ref, dwk_ref, dwv_ref, dgq_ref, dgkv_ref):
        i = pl.program_id(0)
        cos_v, sin_v = cos_ref[...], sin_ref[...]

        def norm_bwd(x, g, dn):
            r = lax.rsqrt(jnp.mean(x * x, axis=-1, keepdims=True) + RMS_EPS)
            xh = x * r
            dxh = dn * g
            dx = r * (dxh - xh * jnp.mean(dxh * xh, axis=-1, keepdims=True))
            return dx, jnp.sum(dn * xh, axis=0, keepdims=True), (xh * g).astype(BF16)

        dq = _rope(dqa_ref[...], cos_v, sin_v, -1.0).astype(BF16)
        dcqn = _dot(dq, wq_ref[...], "nt")
        dcq, dgq, cqn = norm_bwd(cq_ref[...], gq_ref[...], dcqn)
        dcq_ref[...] = dcq.astype(BF16)
        dwq = _dot(cqn, dq, "tn")

        dka = dka_ref[...]
        dkab = dka.astype(BF16)
        dvab = dva_ref[...].astype(BF16)
        dckvn = _dot(dkab, wk_ref[...], "nt") + _dot(dvab, wv_ref[...], "nt")
        dckv, dgkv, ckvn = norm_bwd(ckv_ref[...], gkv_ref[...], dckvn)
        dckv_ref[...] = dckv.astype(BF16)
        dwk = _dot(ckvn, dkab, "tn")
        dwv = _dot(ckvn, dvab, "tn")

        fold = dka[:, 0:LANES]
        for hh in range(1, MLA_H):
            fold = fold + dka[:, hh * LANES:(hh + 1) * LANES]
        lane = _iota((TM, LANES), 1)
        rot = (lane >= KR_LANE) & (lane < KR_LANE + MLA_ROPE)
        dkr = _rope(jnp.where(rot, fold, 0.0), cos_v, sin_v, -1.0)
        dkr_ref[...] = jnp.where(rot, dkr, 0.0).astype(BF16)

        @pl.when(i == 0)
        def _():
            dwq_ref[...] = dwq
            dwk_ref[...] = dwk
            dwv_ref[...] = dwv
            dgq_ref[...] = dgq
            dgkv_ref[...] = dgkv

        @pl.when(i > 0)
        def _():
            dwq_ref[...] += dwq
            dwk_ref[...] += dwk
            dwv_ref[...] += dwv
            dgq_ref[...] += dgq
            dgkv_ref[...] += dgkv

    full = lambda shape: pl.BlockSpec(shape, lambda i: (0, 0))
    rows = lambda c: pl.BlockSpec((TM, c), lambda i: (i, 0))
    nv = MLA_H * MLA_V
    return pl.pallas_call(
        body, name=name, grid=(S // TM,),
        in_specs=[rows(nh), rows(nh), rows(nv),
                  pl.BlockSpec((TM, Q_LORA), lambda i: (i, P_CQ // Q_LORA)),
                  pl.BlockSpec((TM, KV_LORA), lambda i: (i, P_CKV // KV_LORA)),
                  full((1, Q_LORA)), full((1, KV_LORA)), full((Q_LORA, nh)), full((KV_LORA, nh)), full((KV_LORA, nv)),
                  rows(LANES), rows(LANES)],
        out_specs=[rows(Q_LORA), rows(KV_LORA), rows(LANES), full((Q_LORA, nh)), full((KV_LORA, nh)),
                   full((KV_LORA, nv)), full((1, Q_LORA)), full((1, KV_LORA))],
        out_shape=[jax.ShapeDtypeStruct((S, Q_LORA), BF16), jax.ShapeDtypeStruct((S, KV_LORA), BF16),
                   jax.ShapeDtypeStruct((S, LANES), BF16), jax.ShapeDtypeStruct((Q_LORA, nh), F32),
                   jax.ShapeDtypeStruct((KV_LORA, nh), F32), jax.ShapeDtypeStruct((KV_LORA, nv), F32),
                   jax.ShapeDtypeStruct((1, Q_LORA), F32), jax.ShapeDtypeStruct((1, KV_LORA), F32)],
        compiler_params=_params("arbitrary"),
    )(dqa, dka, dva, proj, proj, g_cq, g_ckv, w_uq, w_uk, w_uv, cos_t, sin_t)


def _head_masks(dtype):
    lane = _iota((1, LANES), 1)
    return (lane < 64).astype(dtype), (lane >= 64).astype(dtype)


def _mla_fwd(name, qa, ka, va, carry=None):
    def body(q_ref, k_ref, v_ref, o_ref, lse_ref):
        m0b, m1b = _head_masks(BF16)
        lane = _iota((QB, LANES), 1)
        left = lane < 64

        def qblock(i, _):
            r0 = pl.multiple_of(i * QB, QB)
            qs = [q_ref[pl.ds(r0, QB), hh * LANES:(hh + 1) * LANES] for hh in range(2)]
            rowc = lax.shift_right_logical(r0 + _iota((QB, QB), 0), 6)

            def kv(kb, carry):
                ms, ls, acc = carry
                c0 = pl.multiple_of(kb * QB, QB)
                v = v_ref[pl.ds(c0, QB), :]
                ok = lax.shift_right_logical(c0 + _iota((QB, QB), 1), 6) <= rowc
                new_m, new_l, alphas = [], [], []
                pv = None
                for hh in range(2):
                    k = k_ref[pl.ds(c0, QB), hh * LANES:(hh + 1) * LANES]
                    s = jnp.where(ok, _dot(qs[hh], k, "nt") * (SCALE_A * LOG2_E), NEG)
                    mn = jnp.maximum(ms[hh], jnp.max(s, axis=-1, keepdims=True))
                    p = jnp.exp2(s - mn)
                    a = jnp.exp2(ms[hh] - mn)
                    new_m.append(mn)
                    new_l.append(a * ls[hh] + jnp.sum(p, axis=-1, keepdims=True))
                    alphas.append(a)
                    part = _dot(p.astype(BF16), v * (m0b if hh == 0 else m1b))
                    pv = part if pv is None else pv + part
                acc = acc * jnp.where(left, alphas[0], alphas[1]) + pv
                return tuple(new_m), tuple(new_l), acc

            init = ((jnp.full((QB, 1), NEG, F32),) * 2, (jnp.zeros((QB, 1), F32),) * 2, jnp.zeros((QB, LANES), F32))
            ms, ls, acc = lax.fori_loop(0, i + 1, kv, init)
            o_ref[pl.ds(r0, QB), :] = acc * jnp.where(left, 1.0 / ls[0], 1.0 / ls[1])
            lse_ref[pl.ds(r0, QB), :] = jnp.where(left, ms[0] + jnp.log(ls[0]) * LOG2_E, ms[1] + jnp.log(ls[1]) * LOG2_E)
            return 0

        lax.fori_loop(0, S // QB, qblock, 0)

    pair = lambda w: pl.BlockSpec((S, w), lambda p: (0, p))
    return _carrier_call(
        body, name=name, grid=(MLA_H // 2,), in_specs=[pair(2 * LANES), pair(2 * LANES), pair(LANES)],
        out_specs=[pair(LANES), pair(LANES)],
        out_shape=[jax.ShapeDtypeStruct((S, MLA_H * MLA_V), F32), jax.ShapeDtypeStruct((S, MLA_H * MLA_V), F32)],
        args=(qa, ka, va), sem=("parallel",), carry=carry)


def _mla_bwd(name, qa, ka, va, o, lse, do, do_block0, carry=None):
    def body(q_ref, k_ref, v_ref, o_ref, lse_ref, do_ref, dq_ref, dk_ref, dv_ref):
        m0f, m1f = _head_masks(F32)
        m0b, m1b = _head_masks(BF16)
        dk_ref[...] = jnp.zeros_like(dk_ref)
        dv_ref[...] = jnp.zeros_like(dv_ref)

        def qblock(i, _):
            r0 = pl.multiple_of(i * QB, QB)
            rows = pl.ds(r0, QB)
            do_f = do_ref[rows, :]
            prod = do_f * o_ref[rows, :]
            deltas = [jnp.sum(prod * m0f, axis=-1, keepdims=True), jnp.sum(prod * m1f, axis=-1, keepdims=True)]
            lse_v = lse_ref[rows, :]
            lses = [lse_v[:, 0:1], lse_v[:, 64:65]]
            dob = do_f.astype(BF16)
            dos = [dob * m0b, dob * m1b]
            qs = [q_ref[rows, hh * LANES:(hh + 1) * LANES] for hh in range(2)]
            rowc = lax.shift_right_logical(r0 + _iota((QB, QB), 0), 6)

            def kv(kb, dqs):
                c0 = pl.multiple_of(kb * QB, QB)
                cols = pl.ds(c0, QB)
                v = v_ref[cols, :]
                ok = lax.shift_right_logical(c0 + _iota((QB, QB), 1), 6) <= rowc
                out = []
                dv = None
                for hh in range(2):
                    k = k_ref[cols, hh * LANES:(hh + 1) * LANES]
                    s = _dot(qs[hh], k, "nt") * (SCALE_A * LOG2_E)
                    p = jnp.where(ok, jnp.exp2(s - lses[hh]), 0.0)
                    dp = _dot(dos[hh], v, "nt")
                    ds = (p * (dp - deltas[hh]) * SCALE_A).astype(BF16)
                    out.append(dqs[hh] + _dot(ds, k))
                    dk_ref[cols, hh * LANES:(hh + 1) * LANES] += _dot(ds, qs[hh], "tn")
                    part = _dot(p.astype(BF16), dos[hh], "tn")
                    dv = part if dv is None else dv + part
                dv_ref[cols, :] += dv
                return tuple(out)

            dqs = lax.fori_loop(0, i + 1, kv, (jnp.zeros((QB, LANES), F32),) * 2)
            for hh in range(2):
                dq_ref[rows, hh * LANES:(hh + 1) * LANES] = dqs[hh]
            return 0

        lax.fori_loop(0, S // QB, qblock, 0)

    pair = lambda w: pl.BlockSpec((S, w), lambda p: (0, p))
    return _carrier_call(
        body, name=name, grid=(MLA_H // 2,),
        in_specs=[pair(2 * LANES), pair(2 * LANES), pair(LANES), pair(LANES), pair(LANES),
                  pl.BlockSpec((S, LANES), lambda p: (0, do_block0 + p))],
        out_specs=[pair(2 * LANES), pair(2 * LANES), pair(LANES)],
        out_shape=[jax.ShapeDtypeStruct((S, MLA_H * LANES), F32), jax.ShapeDtypeStruct((S, MLA_H * LANES), F32),
                   jax.ShapeDtypeStruct((S, MLA_H * MLA_V), F32)],
        args=(qa, ka, va, o, lse, do), sem=("parallel",), carry=carry)


def _sb_weights(q_h, k, c, before, tri_suffix):
    z = _dot(q_h, k, "nt") * (SCALE_B * LOG2_E)
    sp = jnp.maximum(z, 0.0) + jnp.log(1.0 + jnp.exp2(-jnp.abs(z))) * LOG2_E
    log_keep = jnp.where(before, -sp, 0.0)
    to_the_right, total = _running_sum(log_keep, tri_suffix, True)
    w = jnp.where(before, jnp.exp2(z - sp + to_the_right + c), 0.0)
    return w, jnp.exp2(z - sp), total


def _sb_fwd(name, proj, carry=None):
    def body(q_ref, k_ref, v_ref, o_ref):
        m0b, m1b = _head_masks(BF16)
        tri_suffix = (_iota((MXU_W, MXU_W), 0) > _iota((MXU_W, MXU_W), 1)).astype(BF16)

        def qblock(i, _):
            r0 = pl.multiple_of(i * QB, QB)
            q = q_ref[pl.ds(r0, QB), :].astype(BF16)
            qs = [q * m0b, q * m1b]
            rowg = r0 + _iota((QB, QB), 0)

            def kv(step, carry):
                cs, acc = carry
                c0 = pl.multiple_of((i - step) * QB, QB)
                k = k_ref[pl.ds(c0, QB), :].astype(BF16)
                v = v_ref[pl.ds(c0, QB), :].astype(BF16)
                before = (c0 + _iota((QB, QB), 1)) < rowg
                new_c = []
                for hh in range(2):
                    w, _, tot = _sb_weights(qs[hh], k, cs[hh], before, tri_suffix)
                    new_c.append(cs[hh] + tot)
                    acc = acc + _dot(w.astype(BF16), v * (m0b if hh == 0 else m1b))
                return tuple(new_c), acc

            init = ((jnp.zeros((QB, 1), F32),) * 2, jnp.zeros((QB, LANES), F32))
            _, acc = lax.fori_loop(0, i + 1, kv, init)
            o_ref[pl.ds(r0, QB), :] = acc.astype(BF16)
            return 0

        lax.fori_loop(0, S // QB, qblock, 0)

    col = lambda base: pl.BlockSpec((S, LANES), lambda p: (0, base // LANES + p))
    return _carrier_call(
        body, name=name, grid=(SB_H // 2,), in_specs=[col(P_QB), col(P_KB), col(P_VB)],
        out_specs=[pl.BlockSpec((S, LANES), lambda p: (0, p))],
        out_shape=[jax.ShapeDtypeStruct((S, SB_H * SB_DIM), BF16)],
        args=(proj, proj, proj), sem=("parallel",), carry=carry)


def _sb_bwd(name, proj, do, do_block0, carry=None):
    nb = S // QB

    def body(q_ref, k_ref, v_ref, do_ref, dq_ref, dk_ref, dv_ref, sig_scr, dl_scr, dk_acc, dv_acc):
        m0b, m1b = _head_masks(BF16)
        tri_suffix = (_iota((MXU_W, MXU_W), 0) > _iota((MXU_W, MXU_W), 1)).astype(BF16)
        tri_prefix = (_iota((MXU_W, MXU_W), 0) < _iota((MXU_W, MXU_W), 1)).astype(BF16)
        dk_acc[...] = jnp.zeros_like(dk_acc)
        dv_acc[...] = jnp.zeros_like(dv_acc)

        def qblock(i, _):
            r0 = pl.multiple_of(i * QB, QB)
            rows = pl.ds(r0, QB)
            q = q_ref[rows, :].astype(BF16)
            qs = [q * m0b, q * m1b]
            dob = do_ref[rows, :].astype(BF16)
            dos = [dob * m0b, dob * m1b]
            rowg = r0 + _iota((QB, QB), 0)

            def sweep_left(step, cs):
                kb = i - step
                c0 = pl.multiple_of(kb * QB, QB)
                cols = pl.ds(c0, QB)
                k = k_ref[cols, :].astype(BF16)
                v = v_ref[cols, :].astype(BF16)
                before = (c0 + _iota((QB, QB), 1)) < rowg
                new_c = []
                dv = None
                for hh in range(2):
                    w, sig, tot = _sb_weights(qs[hh], k, cs[hh], before, tri_suffix)
                    new_c.append(cs[hh] + tot)
                    sig_scr[hh, kb] = sig
                    dl_scr[hh, kb] = _dot(dos[hh], v, "nt") * w
                    part = _dot(w.astype(BF16), dos[hh], "tn")
                    dv = part if dv is None else dv + part
                dv_acc[cols, :] += dv
                return tuple(new_c)

            lax.fori_loop(0, i + 1, sweep_left, (jnp.zeros((QB, 1), F32),) * 2)

            def sweep_right(kb, carry):
                ps, dq = carry
                c0 = pl.multiple_of(kb * QB, QB)
                cols = pl.ds(c0, QB)
                k = k_ref[cols, :].astype(BF16)
                before = (c0 + _iota((QB, QB), 1)) < rowg
                new_p = []
                dk = None
                for hh in range(2):
                    dl = dl_scr[hh, kb]
                    sig = sig_scr[hh, kb]
                    to_the_left, total = _running_sum(dl, tri_prefix, False)
                    earlier = to_the_left + ps[hh]
                    new_p.append(ps[hh] + total)
                    dz = (jnp.where(before, dl * (1.0 - sig) - earlier * sig, 0.0) * SCALE_B).astype(BF16)
                    dq = dq + _dot(dz, k * (m0b if hh == 0 else m1b))
                    part = _dot(dz, qs[hh], "tn")
                    dk = part if dk is None else dk + part
                dk_acc[cols, :] += dk
                return tuple(new_p), dq

            init = ((jnp.zeros((QB, 1), F32),) * 2, jnp.zeros((QB, LANES), F32))
            _, dq = lax.fori_loop(0, i + 1, sweep_right, init)
            dq_ref[rows, :] = dq.astype(BF16)
            return 0

        lax.fori_loop(0, nb, qblock, 0)
        dk_ref[...] = dk_acc[...].astype(BF16)
        dv_ref[...] = dv_acc[...].astype(BF16)

    col = lambda base: pl.BlockSpec((S, LANES), lambda p: (0, base // LANES + p))
    out = pl.BlockSpec((S, LANES), lambda p: (0, p))
    shape = jax.ShapeDtypeStruct((S, SB_H * SB_DIM), BF16)
    return _carrier_call(
        body, name=name, grid=(SB_H // 2,),
        in_specs=[col(P_QB), col(P_KB), col(P_VB), pl.BlockSpec((S, LANES), lambda p: (0, do_block0 + p))],
        out_specs=[out, out, out], out_shape=[shape, shape, shape],
        scratch_shapes=[pltpu.VMEM((2, nb, QB, QB), F32), pltpu.VMEM((2, nb, QB, QB), F32),
                        pltpu.VMEM((S, LANES), F32), pltpu.VMEM((S, LANES), F32)],
        args=(proj, proj, proj, do), sem=("parallel",), carry=carry)


def _band_row_index():
    j = np.arange(TOEP_W)
    rel = np.clip(LEFT_CHUNKS * CHUNK - j, -REL_CLIP, REL_CLIP) + REL_CLIP
    rel[BAND_W:] = 2 * REL_CLIP
    return rel.astype(np.int32)


def _band_tiles(r0_ref, q_ref, kpad, vpad, m, m0b, m1b, static_ok, bias):
    r0 = pl.multiple_of(m * BQ, BQ)
    q = q_ref[0, pl.ds(r0, BQ), :]
    kw = kpad[pl.ds(r0, BAND_W), :]
    vw = vpad[pl.ds(r0, BAND_W), :]
    ok = static_ok & ((r0 - BAND_PAD + _iota((BQ, BAND_W), 1)) >= 0)
    qs = [q * m0b, q * m1b]
    ps = []
    for hh in range(2):
        s = jnp.where(ok, _dot(qs[hh], kw, "nt") * (SCALE_C * LOG2_E) + bias[hh], NEG)
        e = jnp.exp2(s - jnp.max(s, axis=-1, keepdims=True))
        ps.append(e * (1.0 / jnp.sum(e, axis=-1, keepdims=True)))
    return r0, qs, kw, vw, ps


def _band_setup(qkv_ref, r0_ref, kpad, vpad):
    kpad[0:BAND_PAD, :] = jnp.zeros((BAND_PAD, LANES), BF16)
    vpad[0:BAND_PAD, :] = jnp.zeros((BAND_PAD, LANES), BF16)
    kpad[BAND_PAD:, :] = qkv_ref[1]
    vpad[BAND_PAD:, :] = qkv_ref[2]
    jc = lax.shift_right_logical(_iota((BQ, BAND_W), 1), 6)
    rc = lax.shift_right_logical(_iota((BQ, BAND_W), 0), 6)
    static_ok = (jc >= rc) & (jc <= rc + LEFT_CHUNKS)
    bias = []
    for hh in range(2):
        row = jnp.broadcast_to(r0_ref[hh:hh + 1, :] * LOG2_E, (BQ, TOEP_W))
        bias.append(pltpu.roll(row, 0, 1, stride=1, stride_axis=0)[:, :BAND_W])
    return static_ok, bias


def _band_fwd(name, qkv, r0, carry=None):
    def body(qkv_ref, r0_ref, o_ref, kpad, vpad):
        m0b, m1b = _head_masks(BF16)
        static_ok, bias = _band_setup(qkv_ref, r0_ref, kpad, vpad)

        def qblock(m, _):
            r0_, _, _, vw, ps = _band_tiles(r0_ref, qkv_ref, kpad, vpad, m, m0b, m1b, static_ok, bias)
            o = _dot(ps[0].astype(BF16), vw * m0b) + _dot(ps[1].astype(BF16), vw * m1b)
            o_ref[pl.ds(r0_, BQ), :] = o.astype(BF16)
            return 0

        lax.fori_loop(0, S // BQ, qblock, 0)

    return _carrier_call(
        body, name=name, grid=(C_H // 2,),
        in_specs=[pl.BlockSpec((3, S, LANES), lambda p: (0, 0, p)), pl.BlockSpec((None, 2, TOEP_W), lambda p: (p, 0, 0))],
        out_specs=[pl.BlockSpec((S, LANES), lambda p: (0, p))],
        out_shape=[jax.ShapeDtypeStruct((S, C_H * C_DIM), BF16)],
        scratch_shapes=[pltpu.VMEM((S + BAND_PAD, LANES), BF16), pltpu.VMEM((S + BAND_PAD, LANES), BF16)],
        args=(qkv, r0), sem=("parallel",), carry=carry)


def _band_bwd(name, qkv, r0, do, carry=None):
    def body(qkv_ref, r0_ref, do_ref, dqkv_ref, dr0_ref, kpad, vpad, dkpad, dvpad, db_acc):
        m0b, m1b = _head_masks(BF16)
        static_ok, bias = _band_setup(qkv_ref, r0_ref, kpad, vpad)
        dkpad[...] = jnp.zeros_like(dkpad)
        dvpad[...] = jnp.zeros_like(dvpad)
        db_acc[...] = jnp.zeros_like(db_acc)

        def qblock(m, _):
            r0_, qs, kw, vw, ps = _band_tiles(r0_ref, qkv_ref, kpad, vpad, m, m0b, m1b, static_ok, bias)
            dob = do_ref[pl.ds(r0_, BQ), :].astype(BF16)
            dos = [dob * m0b, dob * m1b]
            dq = None
            dk = None
            dv = None
            for hh in range(2):
                p = ps[hh]
                dp = _dot(dos[hh], vw, "nt")
                ds = p * (dp - jnp.sum(dp * p, axis=-1, keepdims=True))
                db_acc[hh, :, 0:BAND_W] += ds
                dsb = (ds * SCALE_C).astype(BF16)
                t = _dot(dsb, kw * (m0b if hh == 0 else m1b))
                dq = t if dq is None else dq + t
                t = _dot(dsb, qs[hh], "tn")
                dk = t if dk is None else dk + t
                t = _dot(p.astype(BF16), dos[hh], "tn")
                dv = t if dv is None else dv + t
            dqkv_ref[0, pl.ds(r0_, BQ), :] = dq.astype(BF16)
            dkpad[pl.ds(r0_, BAND_W), :] += dk
            dvpad[pl.ds(r0_, BAND_W), :] += dv
            return 0

        lax.fori_loop(0, S // BQ, qblock, 0)
        dqkv_ref[1] = dkpad[BAND_PAD:, :].astype(BF16)
        dqkv_ref[2] = dvpad[BAND_PAD:, :].astype(BF16)
        sub = _iota((8, TOEP_W), 0)
        for hh in range(2):
            folded = db_acc[hh, 0:8, :]
            for a in range(1, BQ // 8):
                folded = folded + pltpu.roll(db_acc[hh, 8 * a:8 * a + 8, :], TOEP_W - 8 * a, 1)
            for bit in range(3):
                moved = pltpu.roll(folded, TOEP_W - (1 << bit), 1)
                folded = jnp.where((sub & (1 << bit)) != 0, moved, folded)
            dr0_ref[hh:hh + 1, :] = jnp.sum(folded, axis=0, keepdims=True)

    return _carrier_call(
        body, name=name, grid=(C_H // 2,),
        in_specs=[pl.BlockSpec((3, S, LANES), lambda p: (0, 0, p)), pl.BlockSpec((None, 2, TOEP_W), lambda p: (p, 0, 0)),
                  pl.BlockSpec((S, LANES), lambda p: (0, p))],
        out_specs=[pl.BlockSpec((3, S, LANES), lambda p: (0, 0, p)), pl.BlockSpec((None, 2, TOEP_W), lambda p: (p, 0, 0))],
        out_shape=[jax.ShapeDtypeStruct((3, S, C_H * C_DIM), BF16), jax.ShapeDtypeStruct((C_H // 2, 2, TOEP_W), F32)],
        scratch_shapes=[pltpu.VMEM((S + BAND_PAD, LANES), BF16), pltpu.VMEM((S + BAND_PAD, LANES), BF16),
                        pltpu.VMEM((S + BAND_PAD, LANES), F32), pltpu.VMEM((S + BAND_PAD, LANES), F32),
                        pltpu.VMEM((2, BQ, TOEP_W), F32)],
        args=(qkv, r0, do), sem=("parallel",), carry=carry)


def _bias_table_grad(name, dr0):
    w_out = 5 * LANES

    def body(d_ref, o_ref):
        j = _iota((TOEP_W, w_out), 0)
        rel = jnp.clip(LEFT_CHUNKS * CHUNK - j, -REL_CLIP, REL_CLIP) + REL_CLIP
        rel = jnp.where(j >= BAND_W, 2 * REL_CLIP, rel)
        onehot = (rel == _iota((TOEP_W, w_out), 1)).astype(BF16)
        d = d_ref[...]
        hi = d.astype(BF16)
        mid = (d - hi.astype(F32))
        mid_b = mid.astype(BF16)
        lo = (mid - mid_b.astype(F32)).astype(BF16)
        o_ref[...] = _dot(hi, onehot) + _dot(mid_b, onehot) + _dot(lo, onehot)

    return pl.pallas_call(
        body, name=name, out_shape=jax.ShapeDtypeStruct((C_H, w_out), F32),
        in_specs=[pl.BlockSpec((C_H, TOEP_W), lambda: (0, 0))], out_specs=pl.BlockSpec((C_H, w_out), lambda: (0, 0)),
        grid=(),
    )(dr0)


def _carry_gather(cy, slots, names, ici, d2d):
    idx = [cy.operand(slots[n], True) for n in names]
    n = len(names)
    base_i = cy.sems(3 * n) if ici else 0
    base_d = cy.sems(3 * n) if d2d else 0

    def piece(refs, t, slot, cc):
        return refs[idx[t]].at[slot, _half_rows(cc, slots[names[t]].shape[1] // 2), :]

    def over_ici(refs, send, recv, arriving):
        x, y, c, chips = _position()
        out = []
        for t in range(n):
            for j in range(3):
                r = piece(refs, t, 2 * chips[j][0] + chips[j][1] if arriving else 2 * x + y, c)
                out.append(_remote(r, r, send, recv, base_i + 3 * t + j, (*chips[j], c)))
        return out

    def over_d2d(refs, send, recv, arriving):
        x, y, c, chips = _position()
        out = []
        for t in range(n):
            for j in range(3):
                r = piece(refs, t, 2 * chips[j][0] + chips[j][1], 1 - c if arriving else c)
                out.append(_remote(r, r, send, recv, base_d + 3 * t + j, (x, y, 1 - c)))
        return out

    def start_ici(refs, fresh, send, recv):
        for cp in over_ici(refs, send, recv, False):
            cp.start()

    def wait_ici(refs, fresh, send, recv):
        for cp in over_ici(refs, send, recv, True):
            cp.wait_recv()
        for cp in over_ici(refs, send, recv, False):
            cp.wait_send()

    def start_d2d(refs, fresh, send, recv):
        for cp in over_d2d(refs, send, recv, False):
            cp.start()

    def wait_d2d(refs, fresh, send, recv):
        for cp in over_d2d(refs, send, recv, True):
            cp.wait_recv()
        for cp in over_d2d(refs, send, recv, False):
            cp.wait_send()

    if ici and d2d:
        cy.starts.append(start_ici)
        cy.finishes += [wait_ici, start_d2d, wait_d2d]
    elif ici:
        cy.starts.append(start_ici)
        cy.finishes.append(wait_ici)
    else:
        cy.starts.append(start_d2d)
        cy.finishes.append(wait_d2d)

    def done(aliased, fresh):
        for t, name in enumerate(names):
            slots[name] = aliased[idx[t]]

    cy.on_done.append(done)


def _carry_chip_exchange(cy, sums, got, names):
    idx = [cy.operand(sums[n], False) for n in names]
    out = [cy.result((3,) + sums[n].shape[1:], BF16) for n in names]
    base = cy.sems(3 * len(names))

    def copies(refs, fresh, send, recv):
        x, y, c, chips = _position()
        return [_remote(refs[idx[t]].at[2 * chips[j][0] + chips[j][1]], fresh[out[t]].at[j], send, recv, base + 3 * t + j,
                        (*chips[j], c)) for t in range(len(names)) for j in range(3)]

    def start(refs, fresh, send, recv):
        for cp in copies(refs, fresh, send, recv):
            cp.start()

    def wait(refs, fresh, send, recv):
        for cp in copies(refs, fresh, send, recv):
            cp.wait()

    cy.starts.append(start)
    cy.finishes.append(wait)

    def done(aliased, fresh):
        for t, name in enumerate(names):
            got[name] = fresh[out[t]]

    cy.on_done.append(done)


def _run_carry(name, cy):
    _, res = _carrier_call(None, name=name, grid=(), in_specs=[], out_specs=[], out_shape=[], args=(), sem=(), carry=cy)
    cy.done(res)


FIRST_WEIGHTS = ("ev_w_in", "ev_w_uq", "ev_w_ukv")
WEIGHTS_A = ("ev_w_out", "w_gate0", "w_up0")
WEIGHTS_B = ("w_down0", "od_w_qkv", "od_w_out")
WEIGHTS_C = ("w_gate1",)
WEIGHTS_D = ("w_up1", "w_down1")
GRAD_GROUPS = {"ffn1": ("w_gate1", "w_up1", "w_down1"), "od": ("od_w_qkv", "od_w_out"),
               "ffn0": ("w_gate0", "w_up0", "w_down0"), "ev_out": ("ev_w_out",),
               "ev": ("ev_w_in", "ev_w_uq", "ev_w_ukv")}


def _carry_pair_exchange(cy, parts, theirs, names):
    idx = [cy.operand(parts[n], False) for n in names]
    out = [cy.result((N_CHIPS, parts[n].shape[1] // 2, parts[n].shape[2]), BF16) for n in names]
    base = cy.sems(len(names))

    def copies(refs, fresh, send, recv):
        x, y, c, _ = _position()
        return [_remote(refs[idx[t]].at[:, _half_rows(1 - c, parts[n].shape[1] // 2), :], fresh[out[t]], send, recv,
                        base + t, (x, y, 1 - c)) for t, n in enumerate(names)]

    cy.starts.append(lambda refs, fresh, send, recv: [cp.start() for cp in copies(refs, fresh, send, recv)])
    cy.finishes.append(lambda refs, fresh, send, recv: [cp.wait() for cp in copies(refs, fresh, send, recv)])

    def done(aliased, fresh):
        for t, name in enumerate(names):
            theirs[name] = fresh[out[t]]

    cy.on_done.append(done)


def _carry_sibling_exchange(cy, fulls, pieces):
    idx = [cy.operand(fulls[p], True) for p, _ in pieces]
    base = cy.sems(len(pieces))

    def copies(refs, send, recv, arriving):
        x, y, c, _ = _position()
        out = []
        for t, (p, layer) in enumerate(pieces):
            r = refs[idx[t]].at[layer, _half_rows(1 - c if arriving else c, fulls[p].shape[1] // 2), :]
            out.append(_remote(r, r, send, recv, base + t, (x, y, 1 - c)))
        return out

    def start(refs, fresh, send, recv):
        for cp in copies(refs, send, recv, False):
            cp.start()

    def wait(refs, fresh, send, recv):
        for cp in copies(refs, send, recv, True):
            cp.wait_recv()
        for cp in copies(refs, send, recv, False):
            cp.wait_send()

    cy.starts.append(start)
    cy.finishes.append(wait)

    def done(aliased, fresh):
        for t, (p, _) in enumerate(pieces):
            fulls[p] = aliased[idx[t]]

    cy.on_done.append(done)


RIDES = {
    "cast_rest": (("gather", FIRST_WEIGHTS),),
    "mla_attn": (("gather_ici", WEIGHTS_A),),
    "sb_attn": (("gather_d2d", WEIGHTS_A), ("gather_ici", WEIGHTS_B)),
    "ev_out": (("gather_d2d", WEIGHTS_B),),
    "ffn0": (("gather_ici", WEIGHTS_C),),
    "qkv": (("gather_d2d", WEIGHTS_C),),
    "band_attn": (("gather_ici", WEIGHTS_D),),
    "od_out": (("gather_d2d", WEIGHTS_D),),
    "od_out_bwd_w": (("pair", "ffn1"),),
    "band_attn_bwd": (("chips", "ffn1"),),
    "rms_mix1_bwd": (("pair", "od"),),
    "ev_out_bwd_w": (("pair", "ffn0"),),
    "mla_attn_bwd": (("chips", "od"), ("sibling", "ffn1"), ("pair", "ev_out")),
    "sb_attn_bwd": (("chips", "ffn0"), ("sibling", "od"), ("chips", "ev_out")),
    "proj_in_bwd_w": (("sibling", "ffn0"), ("sibling", "ev_out")),
    "grads_pair_ev": (("pair", "ev"),),
    "proj_in_bwd_x": (("chips", "ev"),),
    "grads_sibling_ev": (("sibling", "ev"),),
}


class _Exchanges:
    def __init__(self, slots, pos, shapes, cast_rest):
        self.slots, self.pos, self.shapes, self.cast_rest = dict(slots), pos, shapes, cast_rest
        self.parts, self.theirs, self.sums, self.got, self.fulls = {}, {}, {}, {}, {}

    def begin(self):
        self.slots.update(self.cast_rest(self.carry("cast_rest")))

    def weights(self, *names):
        return [self.slots[n] for n in names]

    def _pair_sums(self, group):
        names = GRAD_GROUPS[group]
        self.sums.update(zip(names, _pair_sums("pair_sums_" + group, [self.parts[n] for n in names],
                                               [self.theirs[n] for n in names], self.pos)))

    def _chip_sums(self, group):
        names = GRAD_GROUPS[group]
        items = [(self.sums[n], self.got[n], PART_OF[n][1], self.shapes[PART_OF[n][0]], self.fulls.get(PART_OF[n][0]))
                 for n in names]
        self.fulls.update(zip([PART_OF[n][0] for n in names], _chip_sums("chip_sums_" + group, items, self.pos)))

    def carry(self, stage):
        cy = _Carry()
        for step, what in RIDES[stage]:
            if step == "gather":
                _carry_gather(cy, self.slots, what, True, True)
            elif step == "gather_ici":
                _carry_gather(cy, self.slots, what, True, False)
            elif step == "gather_d2d":
                _carry_gather(cy, self.slots, what, False, True)
            elif step == "pair":
                _carry_pair_exchange(cy, self.parts, self.theirs, GRAD_GROUPS[what])
            elif step == "chips":
                self._pair_sums(what)
                _carry_chip_exchange(cy, self.sums, self.got, GRAD_GROUPS[what])
            elif step == "sibling":
                self._chip_sums(what)
                _carry_sibling_exchange(cy, self.fulls, [PART_OF[n] for n in GRAD_GROUPS[what]])
        return cy

    def grads(self, group, parts):
        self.parts.update(parts)
        if group == "ev":
            _run_carry("grads_pair_ev", self.carry("grads_pair_ev"))

    def finish(self):
        _run_carry("grads_sibling_ev", self.carry("grads_sibling_ev"))
        return {n: self.fulls[n] for n in BIG}


class _NoExchanges:
    def __init__(self, slots):
        self.slots, self.parts = dict(slots), {}

    def begin(self):
        pass

    def weights(self, *names):
        return [self.slots[n] for n in names]

    def carry(self, stage):
        return None

    def grads(self, group, parts):
        self.parts.update(parts)


def _w_in_pieces():
    segments = ((0, Q_LORA, P_CQ), (Q_LORA, Q_LORA + KV_LORA, P_CKV),
                (Q_LORA + KV_LORA, Q_LORA + KV_LORA + MLA_ROPE, P_KR + KR_LANE),
                (Q_LORA + KV_LORA + MLA_ROPE, EVEN_IN, P_QB))
    width = EVEN_IN // N_CHIPS
    pieces = []
    for lo, hi, at in segments:
        for k in range(N_CHIPS):
            a, b = max(lo, k * width), min(hi, (k + 1) * width)
            if a < b:
                pieces.append((k, a - k * width, b - a, at + a - lo))
    return pieces


def _w_in_padded(name, w_in_s):
    tr = MXU_W

    def body(s_ref, o_ref):
        o_ref[...] = jnp.zeros(o_ref.shape, BF16)
        for k, a, n, at in _w_in_pieces():
            o_ref[:, at:at + n] = s_ref[k, :, a:a + n]

    return pl.pallas_call(
        body, name=name, grid=(D // tr,),
        in_specs=[pl.BlockSpec((N_CHIPS, tr, EVEN_IN // N_CHIPS), lambda i: (0, i, 0))],
        out_specs=pl.BlockSpec((tr, P_IN), lambda i: (i, 0)), out_shape=jax.ShapeDtypeStruct((D, P_IN), BF16),
        compiler_params=_params("parallel"))(w_in_s)


def _w_in_sharded(name, d_w_in_p):
    tr = MXU_W

    def body(p_ref, o_ref):
        for k, a, n, at in _w_in_pieces():
            o_ref[k, :, a:a + n] = p_ref[:, at:at + n]

    return pl.pallas_call(
        body, name=name, grid=(D // tr,),
        in_specs=[pl.BlockSpec((tr, P_IN), lambda i: (i, 0))],
        out_specs=pl.BlockSpec((N_CHIPS, tr, EVEN_IN // N_CHIPS), lambda i: (0, i, 0)),
        out_shape=jax.ShapeDtypeStruct((N_CHIPS, D, EVEN_IN // N_CHIPS), BF16),
        compiler_params=_params("parallel"))(d_w_in_p)


def _first_weights(w_in_s, w_uq_s, w_ukv_s):
    gw = {"ev_w_in": w_in_s, "ev_w_uq": w_uq_s, "ev_w_ukv": w_ukv_s}
    w_in_p = _w_in_padded("w_in_padded", w_in_s)
    w_uq = jnp.moveaxis(gw["ev_w_uq"], 0, 1).reshape(Q_LORA, MLA_H, MLA_NOPE + MLA_ROPE)
    w_uq_p = jnp.concatenate([w_uq, jnp.zeros((Q_LORA, MLA_H, LANES - MLA_NOPE - MLA_ROPE), BF16)], axis=2)
    w_ukv = jnp.moveaxis(gw["ev_w_ukv"], 0, 1).reshape(KV_LORA, MLA_H, MLA_NOPE + MLA_V)
    w_uk_p = jnp.concatenate([w_ukv[:, :, :MLA_NOPE], jnp.zeros((KV_LORA, MLA_H, LANES - MLA_NOPE), BF16)], axis=2)
    return dict(
        w_in=w_in_p, w_uq=w_uq_p.reshape(Q_LORA, MLA_H * LANES), w_uk=w_uk_p.reshape(KV_LORA, MLA_H * LANES),
        w_uv=w_ukv[:, :, MLA_NOPE:].reshape(KV_LORA, MLA_H * MLA_V))


def _proj_mm(name, u, w_in):
    return _mm(name, u, w_in, kind="nn", grid=(S // TM, 1, 1),
               a_spec=pl.BlockSpec((TM, D), lambda i, j, k: (i, 0)), b_spec=pl.BlockSpec((D, P_IN), lambda i, j, k: (0, 0)),
               o_spec=pl.BlockSpec((TM, P_IN), lambda i, j, k: (i, 0)), out_shape=(S, P_IN), out_dtype=F32, acc_shape=None)


def _out_proj(name, o, w, resid, carry=None):
    return _mm(name, o, w, kind="nn", grid=(S // TM, 1, 1),
               a_spec=pl.BlockSpec((TM, D), lambda i, j, k: (i, 0)), b_spec=pl.BlockSpec((D, D), lambda i, j, k: (0, 0)),
               o_spec=pl.BlockSpec((TM, D), lambda i, j, k: (i, 0)), out_shape=(S, D), out_dtype=F32, acc_shape=None,
               resid=resid, r_spec=pl.BlockSpec((TM, D), lambda i, j, k: (i, 0)), carry=carry)


def _out_proj_bwd(name, dh, o, w, ex):
    d_o = _mm(name + "_x", dh, w, kind="nt", grid=(S // TM, 1, 1),
              a_spec=pl.BlockSpec((TM, D), lambda i, j, k: (i, 0)), b_spec=pl.BlockSpec((D, D), lambda i, j, k: (0, 0)),
              o_spec=pl.BlockSpec((TM, D), lambda i, j, k: (i, 0)), out_shape=(S, D), out_dtype=F32, acc_shape=None)
    d_w = _mm(name + "_w", o, dh, kind="tn", grid=(2, S // TM),
              a_spec=pl.BlockSpec((TM, TM), lambda j, k: (k, j)), b_spec=pl.BlockSpec((TM, D), lambda j, k: (k, 0)),
              o_spec=pl.BlockSpec((TM, D), lambda j, k: (j, 0)), out_shape=(D, D), out_dtype=BF16, acc_shape=(TM, D),
              carry=ex.carry(name + "_w"))
    return d_o, d_w


def _local_step(x, tgt, sm, ex):
    def riding(stage, fn, *args):
        cy = ex.carry(stage)
        res, copies = fn(stage, *args, carry=cy)
        if cy is not None:
            cy.done(copies)
        return res

    cos_t, sin_t = _rope_tables()
    g_mix, g_ffn = sm["g_mix"], sm["g_ffn"]
    r0 = sm["od_rel_bias"][0][:, _band_row_index()].reshape(C_H // 2, 2, TOEP_W)
    nt = 3

    ex.begin()
    w = _first_weights(*ex.weights(*FIRST_WEIGHTS))
    u0 = _rms_fwd("rms_mix0", x, g_mix[0:1])
    proj = _proj_mm("proj_in", u0, w["w_in"])
    qa, ka, va = _mla_prep_fwd("mla_prep", proj, sm["ev_g_cq"], sm["ev_g_ckv"], w["w_uq"], w["w_uk"], w["w_uv"], cos_t, sin_t)
    o_a, lse = riding("mla_attn", _mla_fwd, qa, ka, va)
    o_b, = riding("sb_attn", _sb_fwd, proj)
    o_ev = jnp.concatenate([o_a.astype(BF16), o_b], axis=1)
    w["ev_w_out"] = ex.weights("ev_w_out")[0].reshape(D, D)
    h1 = _out_proj("ev_out", o_ev, w["ev_w_out"], x, ex.carry("ev_out"))
    w["w_gate0"], w["w_up0"], w["w_down0"] = ex.weights("w_gate0", "w_up0", "w_down0")
    h2, gate0, up0 = riding("ffn0", _ffn_fwd, h1, g_ffn[0:1], w["w_gate0"], w["w_up0"], w["w_down0"])
    w["w_qkv"] = jnp.moveaxis(ex.weights("od_w_qkv")[0], 0, 1).reshape(D, nt * D)
    u2 = _rms_fwd("rms_mix1", h2, g_mix[1:2])
    qkv = _mm("qkv", u2, w["w_qkv"], kind="nn", grid=(S // TM, nt, 1),
              a_spec=pl.BlockSpec((TM, D), lambda i, t, k: (i, 0)), b_spec=pl.BlockSpec((D, D), lambda i, t, k: (0, t)),
              o_spec=pl.BlockSpec((None, TM, D), lambda i, t, k: (t, i, 0)),
              out_shape=(nt, S, D), out_dtype=BF16, acc_shape=None, carry=ex.carry("qkv"))
    o_od, = riding("band_attn", _band_fwd, qkv, r0)
    w["od_w_out"] = ex.weights("od_w_out")[0].reshape(D, D)
    h3 = _out_proj("od_out", o_od, w["od_w_out"], h2, ex.carry("od_out"))
    w["w_gate1"], w["w_up1"], w["w_down1"] = ex.weights("w_gate1", "w_up1", "w_down1")
    (h4, gate1, up1), _ = _ffn_fwd("ffn1", h3, g_ffn[1:2], w["w_gate1"], w["w_up1"], w["w_down1"])

    loss, dh4, dg_final = _loss_bwd("loss", h4, sm["g_final"].reshape(1, D), tgt)

    dh3, dg_ffn1, u3, dgate, dup, act = _ffn_bwd("ffn1_bwd", dh4, h3, g_ffn[1:2], gate1, up1,
                                                 w["w_gate1"], w["w_up1"], w["w_down1"])
    d_wg1, d_wu1, d_wd1 = _ffn_wgrads("ffn1_dw", u3, dgate, dup, act, dh4)
    ex.grads("ffn1", {"w_gate1": d_wg1, "w_up1": d_wu1, "w_down1": d_wd1})

    d_ood, d_w_od_out = _out_proj_bwd("od_out_bwd", dh3, o_od, w["od_w_out"], ex)
    dqkv, dr0 = riding("band_attn_bwd", _band_bwd, qkv, r0, d_ood)
    du2 = _mm("qkv_bwd_x", dqkv, w["w_qkv"], kind="nt", grid=(S // TM, nt),
              a_spec=pl.BlockSpec((None, TM, D), lambda i, t: (t, i, 0)), b_spec=pl.BlockSpec((D, D), lambda i, t: (0, t)),
              o_spec=pl.BlockSpec((TM, D), lambda i, t: (i, 0)), out_shape=(S, D), out_dtype=F32, acc_shape=(TM, D))
    wide, per = D // MXU_W, nt * D // N_CHIPS // MXU_W
    piece = lambda r: pl.BlockSpec((None, TM, MXU_W), lambda j, k: ((per * j + r) // wide, k, (per * j + r) % wide))
    d_w_qkv = _mm("qkv_bwd_w", u2, dqkv, kind="tn", grid=(N_CHIPS, S // TM),
                  a_spec=pl.BlockSpec((TM, D), lambda j, k: (k, 0)), b_spec=[piece(r) for r in range(per)],
                  o_spec=pl.BlockSpec((None, D, per * MXU_W), lambda j, k: (j, 0, 0)),
                  out_shape=(N_CHIPS, D, per * MXU_W), out_dtype=BF16, acc_shape=(D, per * MXU_W))
    shard_cols = lambda a: jnp.moveaxis(a.reshape(a.shape[0], N_CHIPS, a.shape[1] // N_CHIPS), 1, 0)
    ex.grads("od", {"od_w_qkv": d_w_qkv, "od_w_out": d_w_od_out.reshape(N_CHIPS, D // N_CHIPS, D)})
    dh2, dg_mix1 = _rms_bwd("rms_mix1_bwd", du2, h2, g_mix[1:2], dh3, carry=ex.carry("rms_mix1_bwd"))
    d_rel = _bias_table_grad("rel_bias_grad", dr0.reshape(C_H, TOEP_W))[:, :2 * REL_CLIP + 1]

    dh1, dg_ffn0, u1, dgate, dup, act = _ffn_bwd("ffn0_bwd", dh2, h1, g_ffn[0:1], gate0, up0,
                                                 w["w_gate0"], w["w_up0"], w["w_down0"])
    d_wg0, d_wu0, d_wd0 = _ffn_wgrads("ffn0_dw", u1, dgate, dup, act, dh2)
    ex.grads("ffn0", {"w_gate0": d_wg0, "w_up0": d_wu0, "w_down0": d_wd0})

    d_oev, d_w_ev_out = _out_proj_bwd("ev_out_bwd", dh1, o_ev, w["ev_w_out"], ex)
    ex.grads("ev_out", {"ev_w_out": d_w_ev_out.reshape(N_CHIPS, D // N_CHIPS, D)})
    dqa, dka, dva = riding("mla_attn_bwd", _mla_bwd, qa, ka, va, o_a, lse, d_oev, 0)
    dqb, dkb, dvb = riding("sb_attn_bwd", _sb_bwd, proj, d_oev, MLA_H * MLA_V // LANES)
    dcq, dckv, dkr, d_w_uq, d_w_uk, d_w_uv, dg_cq, dg_ckv = _mla_prep_bwd(
        "mla_prep_bwd", dqa, dka, dva, proj, sm["ev_g_cq"], sm["ev_g_ckv"], w["w_uq"], w["w_uk"], w["w_uv"], cos_t, sin_t)
    dproj = [dcq, jnp.zeros((S, LANES), BF16), dckv, dqb, dkb, dvb, dkr]
    d_w_in_p = _mm("proj_in_bwd_w", u0, dproj, kind="tn", grid=(1, S // TM),
                   a_spec=pl.BlockSpec((TM, D), lambda j, k: (k, 0)),
                   b_spec=[pl.BlockSpec((TM, p.shape[1]), lambda j, k: (k, 0)) for p in dproj],
                   o_spec=pl.BlockSpec((D, P_IN), lambda j, k: (0, 0)), out_shape=(D, P_IN), out_dtype=BF16,
                   acc_shape=(D, P_IN), carry=ex.carry("proj_in_bwd_w"))
    d_w_uq_std = d_w_uq.reshape(Q_LORA, MLA_H, LANES)[:, :, :MLA_NOPE + MLA_ROPE].reshape(Q_LORA, -1)
    d_w_ukv = jnp.concatenate([d_w_uk.reshape(KV_LORA, MLA_H, LANES)[:, :, :MLA_NOPE],
                               d_w_uv.reshape(KV_LORA, MLA_H, MLA_V)], axis=2).reshape(KV_LORA, -1)
    ex.grads("ev", {"ev_w_in": _w_in_sharded("w_in_sharded", d_w_in_p), "ev_w_uq": shard_cols(d_w_uq_std.astype(BF16)),
                    "ev_w_ukv": shard_cols(d_w_ukv.astype(BF16))})
    du0 = _mm("proj_in_bwd_x", dproj, w["w_in"], kind="nt", grid=(S // TM, 1, 1),
              a_spec=[pl.BlockSpec((TM, p.shape[1]), lambda i, j, k: (i, 0)) for p in dproj],
              b_spec=pl.BlockSpec((D, P_IN), lambda i, j, k: (0, 0)),
              o_spec=pl.BlockSpec((TM, D), lambda i, j, k: (i, 0)), out_shape=(S, D), out_dtype=F32, acc_shape=None,
              carry=ex.carry("proj_in_bwd_x"))
    grad_x, dg_mix0 = _rms_bwd("rms_mix0_bwd", du0, x, g_mix[0:1], dh1)
    small = {
        "ev_g_cq": dg_cq, "ev_g_ckv": dg_ckv, "od_rel_bias": d_rel.reshape(1, C_H, 2 * REL_CLIP + 1),
        "g_mix": jnp.concatenate([dg_mix0, dg_mix1], axis=0), "g_ffn": jnp.concatenate([dg_ffn0, dg_ffn1], axis=0),
        "g_final": dg_final.reshape(D),
    }
    return loss, grad_x, small


BIG = ("ev_w_in", "ev_w_uq", "ev_w_ukv", "ev_w_out", "od_w_qkv", "od_w_out", "w_gate", "w_up", "w_down")
SMALL = ("ev_g_cq", "ev_g_ckv", "od_rel_bias", "g_mix", "g_ffn", "g_final")
WEIGHTS = ("ev_w_in", "ev_g_cq", "ev_w_uq", "ev_g_ckv", "ev_w_ukv", "ev_w_out", "od_w_qkv", "od_rel_bias", "od_w_out",
           "g_mix", "g_ffn", "w_gate", "w_up", "w_down", "g_final")
GRAD_PARTS = (("ev_w_in", "ev_w_in", 0), ("ev_w_uq", "ev_w_uq", 0), ("ev_w_ukv", "ev_w_ukv", 0),
              ("ev_w_out", "ev_w_out", 0), ("od_w_qkv", "od_w_qkv", 0), ("od_w_out", "od_w_out", 0),
              ("w_gate0", "w_gate", 0), ("w_gate1", "w_gate", 1), ("w_up0", "w_up", 0), ("w_up1", "w_up", 1),
              ("w_down0", "w_down", 0), ("w_down1", "w_down", 1))
PART_OF = {part: (param, layer) for part, param, layer in GRAD_PARTS}
TRANSPOSED = ("w_gate", "w_up")
ADAMW_TRANSPOSED = ("ev_w_in", "ev_w_uq")


def _row_tile(rows, cap=512, sublanes=16):
    for t in range(min(rows, cap), 0, -1):
        if rows % t == 0 and t % sublanes == 0:
            return t
    return rows


def _cast_into_slot(name, w, layer, pos):
    _, rows, cols = w.shape
    tr = _row_tile(rows)

    def body(pos_ref, w_ref, o_ref):
        o_ref[...] = w_ref[...].astype(BF16)

    return pl.pallas_call(
        body, name=name,
        grid_spec=pltpu.PrefetchScalarGridSpec(
            num_scalar_prefetch=1, grid=(rows // tr,),
            in_specs=[pl.BlockSpec((None, tr, cols), lambda i, p: (layer, i, 0))],
            out_specs=pl.BlockSpec((None, tr, cols), lambda i, p: (p[0], i, 0))),
        out_shape=jax.ShapeDtypeStruct((N_CHIPS, rows, cols), BF16), compiler_params=_params("arbitrary"))(pos, w)


def _cast_many_into_slots(name, items, pos, carry):
    tiles = [_row_tile(w.shape[1]) for w, _ in items]
    turns = _Turns([w.shape[1] // tr for (w, _), tr in zip(items, tiles)])

    def body(pos_ref, *refs):
        i = pl.program_id(0)
        for t in range(len(items)):
            @pl.when(turns.mine(t, i))
            def _(w_ref=refs[t], o_ref=refs[len(items) + t]):
                o_ref[...] = w_ref[...].astype(BF16)

    in_specs, out_specs, out_shape = [], [], []
    for t, ((w, layer), tr) in enumerate(zip(items, tiles)):
        _, rows, cols = w.shape
        at = turns.step(t)
        in_specs.append(pl.BlockSpec((None, tr, cols), lambda i, p, at=at, layer=layer: (layer, at(i), 0)))
        out_specs.append(pl.BlockSpec((None, tr, cols), lambda i, p, at=at: (p[0], at(i), 0)))
        out_shape.append(jax.ShapeDtypeStruct((N_CHIPS, rows, cols), BF16))
    res, copies = _carrier_call(body, name=name, grid=(turns.total,), in_specs=in_specs, out_specs=out_specs,
                                out_shape=out_shape, args=[w for w, _ in items], sem=("arbitrary",), carry=carry,
                                prefetch=(pos,))
    if carry is not None:
        carry.done(copies)
    return res


class _Turns:
    def __init__(self, counts):
        self.counts = list(counts)
        self.starts = [sum(self.counts[:t]) for t in range(len(self.counts))]
        self.total = sum(self.counts)

    def step(self, t):
        start, n = self.starts[t], self.counts[t]
        return lambda i: jnp.clip(i - start, 0, n - 1)

    def mine(self, t, i):
        return (i >= self.starts[t]) & (i < self.starts[t] + self.counts[t])


def _pair_sums(name, parts, theirs, pos):
    n, pair = len(parts), 2
    tiles = [_row_tile(b.shape[1]) for b in theirs]
    blocks = [b.shape[1] // tr for b, tr in zip(theirs, tiles)]
    turns = _Turns([N_CHIPS // pair * nb for nb in blocks])

    def body(pos_ref, *refs):
        i = pl.program_id(0)
        for t in range(n):
            @pl.when(turns.mine(t, i))
            def _(a_ref=refs[2 * t], b_ref=refs[2 * t + 1], o_ref=refs[2 * n + t]):
                o_ref[...] = (a_ref[...].astype(F32) + b_ref[...].astype(F32)).astype(BF16)

    in_specs, out_specs = [], []
    for t, (b, tr, nb) in enumerate(zip(theirs, tiles, blocks)):
        at, block = turns.step(t), (pair, tr, b.shape[2])
        in_specs.append(pl.BlockSpec(block, lambda i, p, at=at, nb=nb: (at(i) // nb, p[1] * nb + at(i) % nb, 0)))
        in_specs.append(pl.BlockSpec(block, lambda i, p, at=at, nb=nb: (at(i) // nb, at(i) % nb, 0)))
        out_specs.append(pl.BlockSpec(block, lambda i, p, at=at, nb=nb: (at(i) // nb, at(i) % nb, 0)))
    return pl.pallas_call(
        body, name=name,
        grid_spec=pltpu.PrefetchScalarGridSpec(num_scalar_prefetch=1, grid=(turns.total,), in_specs=in_specs,
                                               out_specs=out_specs),
        out_shape=[jax.ShapeDtypeStruct(b.shape, BF16) for b in theirs],
        compiler_params=_params("arbitrary"))(pos, *[a for pair in zip(parts, theirs) for a in pair])


def _chip_sums(name, items, pos):
    n = len(items)
    tiles = [_row_tile(s.shape[1]) for s, *_ in items]
    turns = _Turns([s.shape[1] // tr for (s, *_), tr in zip(items, tiles)])
    carried = [t for t, item in enumerate(items) if item[4] is not None]

    def body(pos_ref, *refs):
        i = pl.program_id(0)
        for t in range(n):
            @pl.when(turns.mine(t, i))
            def _(s_ref=refs[2 * t], g_ref=refs[2 * t + 1], o_ref=refs[2 * n + len(carried) + t]):
                o_ref[...] = ((s_ref[...].astype(F32) + g_ref[0].astype(F32)) + g_ref[1].astype(F32)) + g_ref[2].astype(F32)

    in_specs, out_specs = [], []
    for t, ((s, got, layer, full_shape, full), tr) in enumerate(zip(items, tiles)):
        at, cols, nb = turns.step(t), s.shape[2], turns.counts[t]
        in_specs.append(pl.BlockSpec((None, tr, cols), lambda i, p, at=at: (p[0], at(i), 0)))
        in_specs.append(pl.BlockSpec((3, tr, cols), lambda i, p, at=at: (0, at(i), 0)))
        out_specs.append(pl.BlockSpec((None, tr, cols), lambda i, p, at=at, nb=nb, layer=layer: (layer, p[1] * nb + at(i), 0)))
    return pl.pallas_call(
        body, name=name,
        grid_spec=pltpu.PrefetchScalarGridSpec(num_scalar_prefetch=1, grid=(turns.total,),
                                               in_specs=in_specs + [ANY] * len(carried), out_specs=out_specs),
        out_shape=[jax.ShapeDtypeStruct(item[3], F32) for item in items],
        input_output_aliases={1 + 2 * n + k: t for k, t in enumerate(carried)},
        compiler_params=_params("arbitrary"))(
            pos, *[a for item in items for a in item[:2]], *[items[t][4] for t in carried])


def _adamw_update(w, g, m, v):
    m_new = ADAM_B1 * m + (1.0 - ADAM_B1) * g
    v_new = ADAM_B2 * v + (1.0 - ADAM_B2) * (g * g)
    m_hat = m_new / (1.0 - ADAM_B1 ** ADAM_STEP)
    v_hat = v_new / (1.0 - ADAM_B2 ** ADAM_STEP)
    return -ADAM_LR * (m_hat / (jnp.sqrt(v_hat) + ADAM_EPS) + ADAM_WD * w), m_new, v_new


def _small_step(name, grads, loss, w, m, v):
    n, n_dev = len(grads), 8
    offs = [sum(g.shape[0] for g in grads[:t]) for t in range(n + 1)]
    rows = -(-(offs[n] + 1) // 8) * 8
    width = max(g.shape[1] for g in grads)

    def body(*refs):
        g_refs, loss_ref = refs[:n], refs[n]
        w_refs, m_refs, v_refs = (refs[1 + k * n:1 + (k + 1) * n] for k in (1, 2, 3))
        outs = refs[4 * n + 1:8 * n + 2]
        mine, slots, send_sem, recv_sem = refs[8 * n + 2:]
        x, y, c, _ = _position()
        me = 4 * x + 2 * y + c

        def peer(k):
            return (1 - x if k & 4 else x, 1 - y if k & 2 else y, 1 - c if k & 1 else c)

        def logical(k):
            px, py, pc = peer(k)
            return 4 * px + 2 * py + pc

        mine[...] = jnp.zeros(mine.shape, F32)
        for t in range(n):
            mine[offs[t]:offs[t + 1], 0:grads[t].shape[1]] = g_refs[t][...]
        mine[offs[n]:offs[n] + 1, 0:LANES] = loss_ref[...]
        slots[me] = mine[...]
        sends = [pltpu.make_async_remote_copy(
            src_ref=mine, dst_ref=slots.at[me], send_sem=send_sem.at[k], recv_sem=recv_sem.at[k],
            device_id=peer(k), device_id_type=MESH) for k in range(1, n_dev)]
        for cp in sends:
            cp.start()
        for k in range(1, n_dev):
            pltpu.make_async_remote_copy(
                src_ref=mine, dst_ref=slots.at[logical(k)], send_sem=send_sem.at[k], recv_sem=recv_sem.at[k],
                device_id=peer(k), device_id_type=MESH).wait_recv()
        for cp in sends:
            cp.wait_send()
        total = slots[0]
        for d in range(1, n_dev):
            total = total + slots[d]
        for t in range(n):
            gv = total[offs[t]:offs[t + 1], 0:grads[t].shape[1]]
            outs[t][...] = gv
            outs[n + t][...], outs[2 * n + t][...], outs[3 * n + t][...] = _adamw_update(
                w_refs[t][...], gv, m_refs[t][...], v_refs[t][...])
        outs[4 * n][...] = total[offs[n]:offs[n] + 1, 0:LANES]

    vm = pl.BlockSpec(memory_space=pltpu.VMEM)
    shapes = [jax.ShapeDtypeStruct(g.shape, F32) for g in grads]
    res = pl.pallas_call(
        body, name=name, in_specs=[vm] * (4 * n + 1), out_specs=[vm] * (4 * n + 1),
        out_shape=shapes * 4 + [jax.ShapeDtypeStruct(loss.shape, F32)],
        scratch_shapes=[pltpu.VMEM((rows, width), F32), pltpu.VMEM((n_dev, rows, width), F32),
                        pltpu.SemaphoreType.DMA((n_dev,)), pltpu.SemaphoreType.DMA((n_dev,))],
    )(*grads, loss, *w, *m, *v)
    return [res[k * n:(k + 1) * n] for k in range(4)], res[4 * n]


def _adamw(name, w, g, m, v):
    rows, cols = w.shape
    tr = _row_tile(rows, sublanes=8)

    def body(w_ref, g_ref, m_ref, v_ref, d_ref, mo_ref, vo_ref):
        d_ref[...], mo_ref[...], vo_ref[...] = _adamw_update(w_ref[...], g_ref[...], m_ref[...], v_ref[...])

    spec = pl.BlockSpec((tr, cols), lambda i: (i, 0))
    shape = jax.ShapeDtypeStruct((rows, cols), F32)
    return pl.pallas_call(body, name=name, grid=(rows // tr,), in_specs=[spec] * 4, out_specs=[spec] * 3,
                          out_shape=[shape] * 3, compiler_params=_params("parallel"))(w, g, m, v)


def kernel(x, ev_w_in, ev_g_cq, ev_w_uq, ev_g_ckv, ev_w_ukv, ev_w_out, od_w_qkv, od_rel_bias, od_w_out, g_mix, g_ffn, w_gate, w_up, w_down, g_final, loss_target, m_ev_w_in, m_ev_g_cq, m_ev_w_uq, m_ev_g_ckv, m_ev_w_ukv, m_ev_w_out, m_od_w_qkv, m_od_rel_bias, m_od_w_out, m_g_mix, m_g_ffn, m_w_gate, m_w_up, m_w_down, m_g_final, v_ev_w_in, v_ev_g_cq, v_ev_w_uq, v_ev_g_ckv, v_ev_w_ukv, v_ev_w_out, v_od_w_qkv, v_od_rel_bias, v_od_w_out, v_g_mix, v_g_ffn, v_w_gate, v_w_up, v_w_down, v_g_final):
    w = dict(ev_w_in=ev_w_in, ev_g_cq=ev_g_cq, ev_w_uq=ev_w_uq, ev_g_ckv=ev_g_ckv, ev_w_ukv=ev_w_ukv, ev_w_out=ev_w_out,
             od_w_qkv=od_w_qkv, od_rel_bias=od_rel_bias, od_w_out=od_w_out, g_mix=g_mix, g_ffn=g_ffn, w_gate=w_gate,
             w_up=w_up, w_down=w_down, g_final=g_final)
    m = dict(ev_w_in=m_ev_w_in, ev_g_cq=m_ev_g_cq, ev_w_uq=m_ev_w_uq, ev_g_ckv=m_ev_g_ckv, ev_w_ukv=m_ev_w_ukv,
             ev_w_out=m_ev_w_out, od_w_qkv=m_od_w_qkv, od_rel_bias=m_od_rel_bias, od_w_out=m_od_w_out, g_mix=m_g_mix,
             g_ffn=m_g_ffn, w_gate=m_w_gate, w_up=m_w_up, w_down=m_w_down, g_final=m_g_final)
    v = dict(ev_w_in=v_ev_w_in, ev_g_cq=v_ev_g_cq, ev_w_uq=v_ev_w_uq, ev_g_ckv=v_ev_g_ckv, ev_w_ukv=v_ev_w_ukv,
             ev_w_out=v_ev_w_out, od_w_qkv=v_od_w_qkv, od_rel_bias=v_od_rel_bias, od_w_out=v_od_w_out, g_mix=v_g_mix,
             g_ffn=v_g_ffn, w_gate=v_w_gate, w_up=v_w_up, w_down=v_w_down, g_final=v_g_final)
    flat2d = lambda a: a.reshape(-1, a.shape[-1])
    for tree in (w, m, v):
        for n in TRANSPOSED:
            tree[n] = jnp.swapaxes(tree[n], 1, 2)

    pos = jnp.stack([2 * lax.axis_index("x") + lax.axis_index("y"), lax.axis_index("c")]).astype(jnp.int32)

    slots = {part: _cast_into_slot("cast_" + part, w[n], layer, pos) for part, n, layer in GRAD_PARTS
             if part in FIRST_WEIGHTS}
    rest = [(part, n, layer) for part, n, layer in GRAD_PARTS if part not in FIRST_WEIGHTS]

    def cast_rest(carry):
        return dict(zip([part for part, _, _ in rest],
                        _cast_many_into_slots("cast_rest", [(w[n], layer) for _, n, layer in rest], pos, carry)))

    ex = _Exchanges(slots, pos, {n: w[n].shape for n in BIG}, cast_rest)

    loss_local, grad_x, small = _local_step(x[0], loss_target[0], {n: w[n] for n in SMALL}, ex)

    grads = ex.finish()
    delta, new_m, new_v = {}, {}, {}
    small_out, loss = _small_step("small_step", [flat2d(small[n]) for n in SMALL], loss_local,
                                  *([flat2d(t[n]) for n in SMALL] for t in (w, m, v)))
    for tree, outs in zip((grads, delta, new_m, new_v), small_out):
        tree.update({n: o.reshape(w[n].shape) for n, o in zip(SMALL, outs)})

    for n in BIG:
        turn = (lambda a: jnp.swapaxes(a, 1, 2)) if n in ADAMW_TRANSPOSED else (lambda a: a)
        shape = turn(w[n]).shape
        outs = _adamw("adamw_" + n, *(flat2d(turn(a)) for a in (w[n], grads[n], m[n], v[n])))
        delta[n], new_m[n], new_v[n] = (turn(o.reshape(shape)) for o in outs)
    for tree in (grads, delta, new_m, new_v):
        for n in TRANSPOSED:
            tree[n] = jnp.swapaxes(tree[n], 1, 2)

    return (loss[0, 0], grad_x[None], *[grads[n] for n in WEIGHTS], *[delta[n] for n in WEIGHTS],
            *[new_m[n] for n in WEIGHTS], *[new_v[n] for n in WEIGHTS])
```

```python
import functools

import jax
import jax.numpy as jnp
import numpy as np
from jax import lax
from jax.experimental import pallas as pl
from jax.experimental.pallas import tpu as pltpu

F32 = jnp.float32
BF16 = jnp.bfloat16

S = 2048
D = 1024
CHUNK = 64
MLA_H, MLA_NOPE, MLA_ROPE, MLA_V = 8, 64, 32, 64
Q_LORA, KV_LORA = 384, 256
ROPE_THETA = 10000.0
SB_H, SB_DIM = 8, 64
C_H, C_DIM = 16, 64
LEFT_CHUNKS = 8
REL_CLIP = 256
D_FF = 2816
EVEN_IN = 2208
RMS_EPS = 1e-6
ADAM_LR, ADAM_B1, ADAM_B2, ADAM_EPS, ADAM_WD, ADAM_STEP = 0.001, 0.9, 0.999, 1e-08, 0.01, 10

N_CHIPS = 4
FF_SHARD = D_FF // N_CHIPS
SCALE_A = (MLA_NOPE + MLA_ROPE) ** -0.5
SCALE_B = SB_DIM ** -0.5
SCALE_C = C_DIM ** -0.5
NEG = -1e30
LOG2_E = 1.4426950408889634

LANES = 128
MXU_W = 256
VMEM_LIMIT_BYTES = 56 * 1024 * 1024
TM = 512
QB = 512
BQ = 256

P_CQ, P_CKV, P_QB, P_KB, P_VB, P_KR = 0, 512, 768, 1280, 1792, 2304
P_IN = 2432
KR_LANE = 64
BAND_W = BQ + LEFT_CHUNKS * CHUNK
BAND_PAD = 512
TOEP_W = 1024


def _params(*sem):
    return pltpu.CompilerParams(dimension_semantics=sem, vmem_limit_bytes=VMEM_LIMIT_BYTES)


MESH = pl.DeviceIdType.MESH
ANY = pl.BlockSpec(memory_space=pl.ANY)


def _position():
    x, y, c = lax.axis_index("x"), lax.axis_index("y"), lax.axis_index("c")
    other_chips = [(1 - x, y), (x, 1 - y), (1 - x, 1 - y)]
    return x, y, c, other_chips


def _half_rows(c, half):
    return pl.ds(pl.multiple_of(c * half, 16), half)


def _remote(ref_src, ref_dst, send, recv, k, device):
    return pltpu.make_async_remote_copy(src_ref=ref_src, dst_ref=ref_dst, send_sem=send.at[k], recv_sem=recv.at[k],
                                        device_id=device, device_id_type=MESH)


class _Carry:
    def __init__(self):
        self.operands, self.aliased, self.fresh = [], [], []
        self.n_sems = 0
        self.starts, self.finishes, self.on_done = [], [], []

    def operand(self, arr, aliased):
        for i, a in enumerate(self.operands):
            if a is arr:
                return i
        self.operands.append(arr)
        self.aliased.append(aliased)
        return len(self.operands) - 1

    def result(self, shape, dtype):
        self.fresh.append(jax.ShapeDtypeStruct(shape, dtype))
        return len(self.fresh) - 1

    def sems(self, k):
        base = self.n_sems
        self.n_sems += k
        return base

    def done(self, results):
        aliased, fresh = results
        for f in self.on_done:
            f(aliased, fresh)


def _carrier_call(body, *, name, grid, in_specs, out_specs, out_shape, args, sem, scratch_shapes=(), carry=None,
                  prefetch=()):
    in_specs, out_specs, out_shape, scratch = list(in_specs), list(out_specs), list(out_shape), list(scratch_shapes)
    n_pre = len(prefetch)

    def call(kernel, in_specs, out_specs, out_shape, scratch, aliases, sem):
        return pl.pallas_call(
            kernel, name=name, out_shape=out_shape, input_output_aliases=aliases, compiler_params=_params(*sem),
            grid_spec=pltpu.PrefetchScalarGridSpec(num_scalar_prefetch=n_pre, grid=grid, in_specs=in_specs,
                                                   out_specs=out_specs, scratch_shapes=scratch))

    if carry is None:
        return list(call(body, in_specs, out_specs, out_shape, scratch, {}, sem)(*prefetch, *args)), None
    ops = carry.operands
    alias_idx = [i for i, a in enumerate(carry.aliased) if a]
    c_shapes = [jax.ShapeDtypeStruct(ops[i].shape, ops[i].dtype) for i in alias_idx] + carry.fresh
    n_in, n_out, n_scr = len(args), len(out_shape), len(scratch)

    def wrapped(*refs):
        pre, refs = refs[:n_pre], refs[n_pre:]
        ins, c_ins = refs[:n_in], refs[n_in:n_in + len(ops)]
        o0 = n_in + len(ops)
        outs, c_outs = refs[o0:o0 + n_out], refs[o0 + n_out:o0 + n_out + len(c_shapes)]
        s0 = o0 + n_out + len(c_shapes)
        scr, send, recv = refs[s0:s0 + n_scr], refs[s0 + n_scr], refs[s0 + n_scr + 1]
        use = list(c_ins)
        for k, i in enumerate(alias_idx):
            use[i] = c_outs[k]
        fresh = c_outs[len(alias_idx):]

        def run(steps):
            for step in steps:
                step(use, fresh, send, recv)

        if not grid:
            run(carry.starts)
            if body is not None:
                body(*pre, *ins, *outs, *scr)
            run(carry.finishes)
            return
        ids = [pl.program_id(a) for a in range(len(grid))]
        first = functools.reduce(jnp.logical_and, [i == 0 for i in ids])
        last = functools.reduce(jnp.logical_and, [i == g - 1 for i, g in zip(ids, grid)])

        @pl.when(first)
        def _():
            run(carry.starts)

        body(*pre, *ins, *outs, *scr)

        @pl.when(last)
        def _():
            run(carry.finishes)

    res = call(wrapped, in_specs + [ANY] * len(ops), out_specs + [ANY] * len(c_shapes), out_shape + c_shapes,
               scratch + [pltpu.SemaphoreType.DMA((carry.n_sems,)), pltpu.SemaphoreType.DMA((carry.n_sems,))],
               {n_pre + n_in + i: n_out + k for k, i in enumerate(alias_idx)},
               ("arbitrary",) * len(grid))(*prefetch, *args, *ops)
    res = list(res)
    c_res = res[n_out:]
    return res[:n_out], ({i: c_res[k] for k, i in enumerate(alias_idx)}, c_res[len(alias_idx):])


_DIMS = {"nn": (((1,), (0,)), ((), ())), "nt": (((1,), (1,)), ((), ())), "tn": (((0,), (0,)), ((), ()))}


def _dot(a, b, kind="nn"):
    return lax.dot_general(a, b, _DIMS[kind], preferred_element_type=F32)


def _iota(shape, dim):
    return lax.broadcasted_iota(jnp.int32, shape, dim)


def _sigmoid(x):
    return 1.0 / (1.0 + jnp.exp(-x))


def _split_dot(x, tri):
    hi = x.astype(BF16)
    lo = (x - hi.astype(F32)).astype(BF16)
    both = _dot(jnp.concatenate([hi, lo], axis=0), tri)
    return both[:x.shape[0]] + both[x.shape[0]:]


def _running_sum(x, tri, reverse):
    n = x.shape[1] // MXU_W
    blocks = [x[:, b * MXU_W:(b + 1) * MXU_W] for b in range(n)]
    out = [None] * n
    carry = None
    for b in (range(n - 1, -1, -1) if reverse else range(n)):
        part = _split_dot(blocks[b], tri)
        out[b] = part if carry is None else part + carry
        total = jnp.sum(blocks[b], axis=-1, keepdims=True)
        carry = total if carry is None else carry + total
    return (jnp.concatenate(out, axis=1) if n > 1 else out[0]), carry


def _mm(name, a, b, *, kind, grid, a_spec, b_spec, o_spec, out_shape, out_dtype, acc_shape, resid=None, r_spec=None,
        carry=None):
    nk = grid[-1]
    has_r = resid is not None
    several = lambda x: list(x) if isinstance(x, (tuple, list)) else [x]
    a_specs, b_specs = several(a_spec), several(b_spec)
    na, nb = len(a_specs), len(b_specs)
    a_args = several(a) if isinstance(a, (tuple, list)) else [a] * na
    b_args = several(b) if isinstance(b, (tuple, list)) else [b] * nb

    def body(*refs):
        r_ref = refs[na + nb] if has_r else None
        o_ref = refs[na + nb + has_r]
        side_by_side = lambda rs: rs[0][...] if len(rs) == 1 else jnp.concatenate([r[...].astype(BF16) for r in rs], axis=1)
        part = _dot(side_by_side(refs[:na]).astype(BF16), side_by_side(refs[na:na + nb]).astype(BF16), kind)

        def finish(total):
            if has_r:
                total = total + r_ref[...].astype(F32)
            o_ref[...] = total.astype(out_dtype)

        if nk == 1:
            finish(part)
        else:
            acc_ref = refs[na + nb + has_r + 1]
            k = pl.program_id(len(grid) - 1)

            @pl.when(k == 0)
            def _():
                acc_ref[...] = part

            @pl.when(k > 0)
            def _():
                acc_ref[...] += part

            @pl.when(k == nk - 1)
            def _():
                finish(acc_ref[...])

    in_specs = a_specs + b_specs + ([r_spec] if has_r else [])
    args = (*a_args, *b_args) + ((resid,) if has_r else ())
    sem = ("parallel",) * (len(grid) - 1) + ("arbitrary",)
    res, copies = _carrier_call(
        body, name=name, grid=grid, in_specs=in_specs, out_specs=[o_spec],
        out_shape=[jax.ShapeDtypeStruct(out_shape, out_dtype)],
        scratch_shapes=[pltpu.VMEM(acc_shape, F32)] if nk > 1 else [], args=args, sem=sem, carry=carry)
    if carry is not None:
        carry.done(copies)
    return res[0]


def _rms_fwd(name, x, g, col_block=0):
    c = g.shape[1]

    def body(x_ref, g_ref, u_ref):
        xv = x_ref[...]
        r = lax.rsqrt(jnp.mean(xv * xv, axis=-1, keepdims=True) + RMS_EPS)
        u_ref[...] = (xv * r * g_ref[...]).astype(BF16)

    return pl.pallas_call(
        body, name=name, grid=(S // TM,),
        in_specs=[pl.BlockSpec((TM, c), lambda i: (i, col_block)), pl.BlockSpec((1, c), lambda i: (0, 0))],
        out_specs=pl.BlockSpec((TM, c), lambda i: (i, 0)),
        out_shape=jax.ShapeDtypeStruct((S, c), BF16),
        compiler_params=_params("parallel"),
    )(x, g)


def _rms_bwd(name, dy, x, g, resid, carry=None):
    def body(dy_ref, x_ref, g_ref, r_ref, dx_ref, dg_ref):
        i = pl.program_id(0)
        xv = x_ref[...]
        r = lax.rsqrt(jnp.mean(xv * xv, axis=-1, keepdims=True) + RMS_EPS)
        xh = xv * r
        dyv = dy_ref[...]
        dxh = dyv * g_ref[...]
        dx_ref[...] = r_ref[...] + r * (dxh - xh * jnp.mean(dxh * xh, axis=-1, keepdims=True))
        part = jnp.sum(dyv * xh, axis=0, keepdims=True)

        @pl.when(i == 0)
        def _():
            dg_ref[...] = part

        @pl.when(i > 0)
        def _():
            dg_ref[...] += part

    row = pl.BlockSpec((TM, D), lambda i: (i, 0))
    vec = pl.BlockSpec((1, D), lambda i: (0, 0))
    res, copies = _carrier_call(
        body, name=name, grid=(S // TM,), in_specs=[row, row, vec, row], out_specs=[row, vec],
        out_shape=[jax.ShapeDtypeStruct((S, D), F32), jax.ShapeDtypeStruct((1, D), F32)],
        args=(dy, x, g, resid), sem=("arbitrary",), carry=carry)
    if carry is not None:
        carry.done(copies)
    return res


def _loss_bwd(name, h, g, tgt):
    def body(h_ref, g_ref, t_ref, loss_ref, dh_ref, dg_ref):
        i = pl.program_id(0)
        xv = h_ref[...]
        gv = g_ref[...]
        r = lax.rsqrt(jnp.mean(xv * xv, axis=-1, keepdims=True) + RMS_EPS)
        xh = xv * r
        diff = xh * gv - t_ref[...]
        part_loss = 0.5 * jnp.sum(jnp.sum(diff * diff, axis=-1, keepdims=True) * (1.0 / D), axis=0, keepdims=True)
        dy = diff * (1.0 / D)
        dxh = dy * gv
        dh_ref[...] = r * (dxh - xh * jnp.mean(dxh * xh, axis=-1, keepdims=True))
        part_g = jnp.sum(dy * xh, axis=0, keepdims=True)

        @pl.when(i == 0)
        def _():
            dg_ref[...] = part_g
            loss_ref[...] = jnp.broadcast_to(part_loss, (1, LANES))

        @pl.when(i > 0)
        def _():
            dg_ref[...] += part_g
            loss_ref[...] += jnp.broadcast_to(part_loss, (1, LANES))

    row = pl.BlockSpec((TM, D), lambda i: (i, 0))
    vec = pl.BlockSpec((1, D), lambda i: (0, 0))
    return pl.pallas_call(
        body, name=name, grid=(S // TM,), in_specs=[row, vec, row],
        out_specs=[pl.BlockSpec((1, LANES), lambda i: (0, 0)), row, vec],
        out_shape=[jax.ShapeDtypeStruct((1, LANES), F32), jax.ShapeDtypeStruct((S, D), F32),
                   jax.ShapeDtypeStruct((1, D), F32)],
        compiler_params=_params("arbitrary"),
    )(h, g, tgt)


def _ffn_fwd(name, h, g, wg, wu, wd, carry=None):
    def body(h_ref, g_ref, wg_ref, wu_ref, wd_ref, o_ref, gate_ref, up_ref, u_scr):
        s = pl.program_id(1)

        @pl.when(s == 0)
        def _():
            xv = h_ref[...]
            r = lax.rsqrt(jnp.mean(xv * xv, axis=-1, keepdims=True) + RMS_EPS)
            u_scr[...] = (xv * r * g_ref[...]).astype(BF16)
            o_ref[...] = xv

        u = u_scr[...]
        gate = _dot(u, wg_ref[...], "nt")
        up = _dot(u, wu_ref[...], "nt")
        act = gate * _sigmoid(gate) * up
        o_ref[...] += _dot(act.astype(BF16), wd_ref[...])
        gate_ref[...] = gate.astype(BF16)
        up_ref[...] = up.astype(BF16)

    row = pl.BlockSpec((TM, D), lambda i, s: (i, 0))
    hid = pl.BlockSpec((None, TM, FF_SHARD), lambda i, s: (s, i, 0))
    return _carrier_call(
        body, name=name, grid=(S // TM, N_CHIPS),
        in_specs=[row, pl.BlockSpec((1, D), lambda i, s: (0, 0))]
        + [pl.BlockSpec((None, FF_SHARD, D), lambda i, s: (s, 0, 0))] * 3,
        out_specs=[row, hid, hid],
        out_shape=[jax.ShapeDtypeStruct((S, D), F32), jax.ShapeDtypeStruct((N_CHIPS, S, FF_SHARD), BF16),
                   jax.ShapeDtypeStruct((N_CHIPS, S, FF_SHARD), BF16)],
        scratch_shapes=[pltpu.VMEM((TM, D), BF16)], args=(h, g, wg, wu, wd), sem=("parallel", "arbitrary"), carry=carry)


def _ffn_bwd(name, dh, h, g, gate, up, wg, wu, wd):
    def body(dh_ref, h_ref, g_ref, gate_ref, up_ref, wg_ref, wu_ref, wd_ref,
             dhin_ref, dg_ref, u_ref, dgate_ref, dup_ref, act_ref, dhb_scr, du_scr):
        i = pl.program_id(0)
        s = pl.program_id(1)

        @pl.when(s == 0)
        def _():
            xv = h_ref[...]
            r = lax.rsqrt(jnp.mean(xv * xv, axis=-1, keepdims=True) + RMS_EPS)
            u_ref[...] = (xv * r * g_ref[...]).astype(BF16)
            dhb_scr[...] = dh_ref[...].astype(BF16)
            du_scr[...] = jnp.zeros_like(du_scr)

        dact = _dot(dhb_scr[...], wd_ref[...], "nt")
        gv = gate_ref[...].astype(F32)
        uv = up_ref[...].astype(F32)
        sig = _sigmoid(gv)
        sil = gv * sig
        dup = dact * sil
        dgate = dact * uv * (sig * (1.0 + gv * (1.0 - sig)))
        dgb = dgate.astype(BF16)
        dub = dup.astype(BF16)
        act_ref[...] = (sil * uv).astype(BF16)
        dgate_ref[...] = dgb
        dup_ref[...] = dub
        du_scr[...] += _dot(dgb, wg_ref[...]) + _dot(dub, wu_ref[...])

        @pl.when(s == N_CHIPS - 1)
        def _():
            xv = h_ref[...]
            r = lax.rsqrt(jnp.mean(xv * xv, axis=-1, keepdims=True) + RMS_EPS)
            xh = xv * r
            du = du_scr[...]
            dxh = du * g_ref[...]
            dhin_ref[...] = dh_ref[...] + r * (dxh - xh * jnp.mean(dxh * xh, axis=-1, keepdims=True))
            part = jnp.sum(du * xh, axis=0, keepdims=True)

            @pl.when(i == 0)
            def _():
                dg_ref[...] = part

            @pl.when(i > 0)
            def _():
                dg_ref[...] += part

    row = pl.BlockSpec((TM, D), lambda i, s: (i, 0))
    vec = pl.BlockSpec((1, D), lambda i, s: (0, 0))
    hid = pl.BlockSpec((None, TM, FF_SHARD), lambda i, s: (s, i, 0))
    hid_shape = jax.ShapeDtypeStruct((N_CHIPS, S, FF_SHARD), BF16)
    return pl.pallas_call(
        body, name=name, grid=(S // TM, N_CHIPS),
        in_specs=[row, row, vec, hid, hid] + [pl.BlockSpec((None, FF_SHARD, D), lambda i, s: (s, 0, 0))] * 3,
        out_specs=[row, vec, row, hid, hid, hid],
        out_shape=[jax.ShapeDtypeStruct((S, D), F32), jax.ShapeDtypeStruct((1, D), F32),
                   jax.ShapeDtypeStruct((S, D), BF16), hid_shape, hid_shape, hid_shape],
        scratch_shapes=[pltpu.VMEM((TM, D), BF16), pltpu.VMEM((TM, D), F32)],
        compiler_params=_params("arbitrary", "arbitrary"),
    )(dh, h, g, gate, up, wg, wu, wd)


def _ffn_wgrads(name, u, dgate, dup, act, dh):
    nk = S // TM

    def body(u_ref, dh_ref, dgate_ref, dup_ref, act_ref, dg_ref, du_ref, dd_ref, acc_g, acc_u, acc_d):
        k = pl.program_id(1)
        u = u_ref[...]
        parts = (_dot(dgate_ref[...], u, "tn"), _dot(dup_ref[...], u, "tn"),
                 _dot(act_ref[...], dh_ref[...].astype(BF16), "tn"))
        accs = (acc_g, acc_u, acc_d)

        @pl.when(k == 0)
        def _():
            for acc, part in zip(accs, parts):
                acc[...] = part

        @pl.when(k > 0)
        def _():
            for acc, part in zip(accs, parts):
                acc[...] += part

        @pl.when(k == nk - 1)
        def _():
            for out, acc in zip((dg_ref, du_ref, dd_ref), accs):
                out[...] = acc[...].astype(BF16)

    tok = pl.BlockSpec((TM, D), lambda s, k: (k, 0))
    hid = pl.BlockSpec((None, TM, FF_SHARD), lambda s, k: (s, k, 0))
    out = pl.BlockSpec((None, FF_SHARD, D), lambda s, k: (s, 0, 0))
    shape = jax.ShapeDtypeStruct((N_CHIPS, FF_SHARD, D), BF16)
    return pl.pallas_call(
        body, name=name, grid=(N_CHIPS, nk), in_specs=[tok, tok, hid, hid, hid], out_specs=[out, out, out],
        out_shape=[shape, shape, shape], scratch_shapes=[pltpu.VMEM((FF_SHARD, D), F32)] * 3,
        compiler_params=_params("parallel", "arbitrary"))(u, dh, dgate, dup, act)


def _rope_tables():
    pos = jnp.arange(S, dtype=F32)
    inv = ROPE_THETA ** (-jnp.arange(0, MLA_ROPE, 2, dtype=F32) / MLA_ROPE)
    ang = pos[:, None] * inv[None, :]
    half = MLA_ROPE // 2
    cos = jnp.cos(ang)
    sin = jnp.sin(ang)
    one = jnp.ones((S, KR_LANE), F32)
    zero = jnp.zeros((S, KR_LANE), F32)
    tail_one = jnp.ones((S, LANES - KR_LANE - MLA_ROPE), F32)
    tail_zero = jnp.zeros((S, LANES - KR_LANE - MLA_ROPE), F32)
    cos_t = jnp.concatenate([one, cos, cos, tail_one], axis=1)
    sin_t = jnp.concatenate([zero, -sin, sin, tail_zero], axis=1)
    assert cos_t.shape == (S, LANES) and half * 2 == MLA_ROPE
    return cos_t, sin_t


def _rope(x, cos_t, sin_t, sign):
    n = x.shape[1] // LANES
    half = MLA_ROPE // 2
    lane = _iota(x.shape, 1) & (LANES - 1)
    first = (lane >= KR_LANE) & (lane < KR_LANE + half)
    swapped = jnp.where(first, pltpu.roll(x, x.shape[1] - half, 1), pltpu.roll(x, half, 1))
    c = jnp.tile(cos_t, (1, n)) if n > 1 else cos_t
    s = jnp.tile(sin_t, (1, n)) if n > 1 else sin_t
    return x * c + swapped * (s * sign)


def _mla_prep_fwd(name, proj, g_cq, g_ckv, w_uq, w_uk, w_uv, cos_t, sin_t):
    nh = MLA_H * LANES

    def body(cq_ref, ckv_ref, kr_ref, gq_ref, gkv_ref, wq_ref, wk_ref, wv_ref, cos_ref, sin_ref,
             qa_ref, ka_ref, va_ref):
        cos_v, sin_v = cos_ref[...], sin_ref[...]
        cq = cq_ref[...]
        r = lax.rsqrt(jnp.mean(cq * cq, axis=-1, keepdims=True) + RMS_EPS)
        cqn = (cq * r * gq_ref[...]).astype(BF16)
        qa_ref[...] = _rope(_dot(cqn, wq_ref[...]), cos_v, sin_v, 1.0).astype(BF16)
        ckv = ckv_ref[...]
        r = lax.rsqrt(jnp.mean(ckv * ckv, axis=-1, keepdims=True) + RMS_EPS)
        ckvn = (ckv * r * gkv_ref[...]).astype(BF16)
        lane = _iota((TM, LANES), 1)
        rot = (lane >= KR_LANE) & (lane < KR_LANE + MLA_ROPE)
        kr = jnp.where(rot, _rope(kr_ref[...], cos_v, sin_v, 1.0), 0.0)
        ka_ref[...] = (_dot(ckvn, wk_ref[...]) + jnp.tile(kr, (1, MLA_H))).astype(BF16)
        va_ref[...] = _dot(ckvn, wv_ref[...]).astype(BF16)

    full = lambda shape: pl.BlockSpec(shape, lambda i: (0, 0))
    return pl.pallas_call(
        body, name=name, grid=(S // TM,),
        in_specs=[pl.BlockSpec((TM, Q_LORA), lambda i: (i, P_CQ // Q_LORA)),
                  pl.BlockSpec((TM, KV_LORA), lambda i: (i, P_CKV // KV_LORA)),
                  pl.BlockSpec((TM, LANES), lambda i: (i, P_KR // LANES)),
                  full((1, Q_LORA)), full((1, KV_LORA)), full((Q_LORA, nh)), full((KV_LORA, nh)),
                  full((KV_LORA, MLA_H * MLA_V)),
                  pl.BlockSpec((TM, LANES), lambda i: (i, 0)), pl.BlockSpec((TM, LANES), lambda i: (i, 0))],
        out_specs=[pl.BlockSpec((TM, nh), lambda i: (i, 0)), pl.BlockSpec((TM, nh), lambda i: (i, 0)),
                   pl.BlockSpec((TM, MLA_H * MLA_V), lambda i: (i, 0))],
        out_shape=[jax.ShapeDtypeStruct((S, nh), BF16), jax.ShapeDtypeStruct((S, nh), BF16),
                   jax.ShapeDtypeStruct((S, MLA_H * MLA_V), BF16)],
        compiler_params=_params("parallel"),
    )(proj, proj, proj, g_cq, g_ckv, w_uq, w_uk, w_uv, cos_t, sin_t)


def _mla_prep_bwd(name, dqa, dka, dva, proj, g_cq, g_ckv, w_uq, w_uk, w_uv, cos_t, sin_t):
    nh = MLA_H * LANES

    def body(dqa_ref, dka_ref, dva_ref, cq_ref, ckv_ref, gq_ref, gkv_ref, wq_ref, wk_ref, wv_ref, cos_ref, sin_ref,
             dcq_ref, dckv_ref, dkr_ref, dwq_ref, dwk_ref, dwv_ref, dgq_ref, dgkv_ref):
        i = pl.program_id(0)
        cos_v, sin_v = cos_ref[...], sin_ref[...]

        def norm_bwd(x, g, dn):
            r = lax.rsqrt(jnp.mean(x * x, axis=-1, keepdims=True) + RMS_EPS)
            xh = x * r
            dxh = dn * g
            dx = r * (dxh - xh * jnp.mean(dxh * xh, axis=-1, keepdims=True))
            return dx, jnp.sum(dn * xh, axis=0, keepdims=True), (xh * g).astype(BF16)

        dq = _rope(dqa_ref[...], cos_v, sin_v, -1.0).astype(BF16)
        dcqn = _dot(dq, wq_ref[...], "nt")
        dcq, dgq, cqn = norm_bwd(cq_ref[...], gq_ref[...], dcqn)
        dcq_ref[...] = dcq.astype(BF16)
        dwq = _dot(cqn, dq, "tn")

        dka = dka_ref[...]
        dkab = dka.astype(BF16)
        dvab = dva_ref[...].astype(BF16)
        dckvn = _dot(dkab, wk_ref[...], "nt") + _dot(dvab, wv_ref[...], "nt")
        dckv, dgkv, ckvn = norm_bwd(ckv_ref[...], gkv_ref[...], dckvn)
        dckv_ref[...] = dckv.astype(BF16)
        dwk = _dot(ckvn, dkab, "tn")
        dwv = _dot(ckvn, dvab, "tn")

        fold = dka[:, 0:LANES]
        for hh in range(1, MLA_H):
            fold = fold + dka[:, hh * LANES:(hh + 1) * LANES]
        lane = _iota((TM, LANES), 1)
        rot = (lane >= KR_LANE) & (lane < KR_LANE + MLA_ROPE)
        dkr = _rope(jnp.where(rot, fold, 0.0), cos_v, sin_v, -1.0)
        dkr_ref[...] = jnp.where(rot, dkr, 0.0).astype(BF16)

        @pl.when(i == 0)
        def _():
            dwq_ref[...] = dwq
            dwk_ref[...] = dwk
            dwv_ref[...] = dwv
            dgq_ref[...] = dgq
            dgkv_ref[...] = dgkv

        @pl.when(i > 0)
        def _():
            dwq_ref[...] += dwq
            dwk_ref[...] += dwk
            dwv_ref[...] += dwv
            dgq_ref[...] += dgq
            dgkv_ref[...] += dgkv

    full = lambda shape: pl.BlockSpec(shape, lambda i: (0, 0))
    rows = lambda c: pl.BlockSpec((TM, c), lambda i: (i, 0))
    nv = MLA_H * MLA_V
    return pl.pallas_call(
        body, name=name, grid=(S // TM,),
        in_specs=[rows(nh), rows(nh), rows(nv),
                  pl.BlockSpec((TM, Q_LORA), lambda i: (i, P_CQ // Q_LORA)),
                  pl.BlockSpec((TM, KV_LORA), lambda i: (i, P_CKV // KV_LORA)),
                  full((1, Q_LORA)), full((1, KV_LORA)), full((Q_LORA, nh)), full((KV_LORA, nh)), full((KV_LORA, nv)),
                  rows(LANES), rows(LANES)],
        out_specs=[rows(Q_LORA), rows(KV_LORA), rows(LANES), full((Q_LORA, nh)), full((KV_LORA, nh)),
                   full((KV_LORA, nv)), full((1, Q_LORA)), full((1, KV_LORA))],
        out_shape=[jax.ShapeDtypeStruct((S, Q_LORA), BF16), jax.ShapeDtypeStruct((S, KV_LORA), BF16),
                   jax.ShapeDtypeStruct((S, LANES), BF16), jax.ShapeDtypeStruct((Q_LORA, nh), F32),
                   jax.ShapeDtypeStruct((KV_LORA, nh), F32), jax.ShapeDtypeStruct((KV_LORA, nv), F32),
                   jax.ShapeDtypeStruct((1, Q_LORA), F32), jax.ShapeDtypeStruct((1, KV_LORA), F32)],
        compiler_params=_params("arbitrary"),
    )(dqa, dka, dva, proj, proj, g_cq, g_ckv, w_uq, w_uk, w_uv, cos_t, sin_t)


def _head_masks(dtype):
    lane = _iota((1, LANES), 1)
    return (lane < 64).astype(dtype), (lane >= 64).astype(dtype)


def _mla_fwd(name, qa, ka, va, carry=None):
    def body(q_ref, k_ref, v_ref, o_ref, lse_ref):
        m0b, m1b = _head_masks(BF16)
        lane = _iota((QB, LANES), 1)
        left = lane < 64

        def qblock(i, _):
            r0 = pl.multiple_of(i * QB, QB)
            qs = [q_ref[pl.ds(r0, QB), hh * LANES:(hh + 1) * LANES] for hh in range(2)]
            rowc = lax.shift_right_logical(r0 + _iota((QB, QB), 0), 6)

            def kv(kb, carry):
                ms, ls, acc = carry
                c0 = pl.multiple_of(kb * QB, QB)
                v = v_ref[pl.ds(c0, QB), :]
                ok = lax.shift_right_logical(c0 + _iota((QB, QB), 1), 6) <= rowc
                new_m, new_l, alphas = [], [], []
                pv = None
                for hh in range(2):
                    k = k_ref[pl.ds(c0, QB), hh * LANES:(hh + 1) * LANES]
                    s = jnp.where(ok, _dot(qs[hh], k, "nt") * (SCALE_A * LOG2_E), NEG)
                    mn = jnp.maximum(ms[hh], jnp.max(s, axis=-1, keepdims=True))
                    p = jnp.exp2(s - mn)
                    a = jnp.exp2(ms[hh] - mn)
                    new_m.append(mn)
                    new_l.append(a * ls[hh] + jnp.sum(p, axis=-1, keepdims=True))
                    alphas.append(a)
                    part = _dot(p.astype(BF16), v * (m0b if hh == 0 else m1b))
                    pv = part if pv is None else pv + part
                acc = acc * jnp.where(left, alphas[0], alphas[1]) + pv
                return tuple(new_m), tuple(new_l), acc

            init = ((jnp.full((QB, 1), NEG, F32),) * 2, (jnp.zeros((QB, 1), F32),) * 2, jnp.zeros((QB, LANES), F32))
            ms, ls, acc = lax.fori_loop(0, i + 1, kv, init)
            o_ref[pl.ds(r0, QB), :] = acc * jnp.where(left, 1.0 / ls[0], 1.0 / ls[1])
            lse_ref[pl.ds(r0, QB), :] = jnp.where(left, ms[0] + jnp.log(ls[0]) * LOG2_E, ms[1] + jnp.log(ls[1]) * LOG2_E)
            return 0

        lax.fori_loop(0, S // QB, qblock, 0)

    pair = lambda w: pl.BlockSpec((S, w), lambda p: (0, p))
    return _carrier_call(
        body, name=name, grid=(MLA_H // 2,), in_specs=[pair(2 * LANES), pair(2 * LANES), pair(LANES)],
        out_specs=[pair(LANES), pair(LANES)],
        out_shape=[jax.ShapeDtypeStruct((S, MLA_H * MLA_V), F32), jax.ShapeDtypeStruct((S, MLA_H * MLA_V), F32)],
        args=(qa, ka, va), sem=("parallel",), carry=carry)


def _mla_bwd(name, qa, ka, va, o, lse, do, do_block0, carry=None):
    def body(q_ref, k_ref, v_ref, o_ref, lse_ref, do_ref, dq_ref, dk_ref, dv_ref):
        m0f, m1f = _head_masks(F32)
        m0b, m1b = _head_masks(BF16)
        dk_ref[...] = jnp.zeros_like(dk_ref)
        dv_ref[...] = jnp.zeros_like(dv_ref)

        def qblock(i, _):
            r0 = pl.multiple_of(i * QB, QB)
            rows = pl.ds(r0, QB)
            do_f = do_ref[rows, :]
            prod = do_f * o_ref[rows, :]
            deltas = [jnp.sum(prod * m0f, axis=-1, keepdims=True), jnp.sum(prod * m1f, axis=-1, keepdims=True)]
            lse_v = lse_ref[rows, :]
            lses = [lse_v[:, 0:1], lse_v[:, 64:65]]
            dob = do_f.astype(BF16)
            dos = [dob * m0b, dob * m1b]
            qs = [q_ref[rows, hh * LANES:(hh + 1) * LANES] for hh in range(2)]
            rowc = lax.shift_right_logical(r0 + _iota((QB, QB), 0), 6)

            def kv(kb, dqs):
                c0 = pl.multiple_of(kb * QB, QB)
                cols = pl.ds(c0, QB)
                v = v_ref[cols, :]
                ok = lax.shift_right_logical(c0 + _iota((QB, QB), 1), 6) <= rowc
                out = []
                dv = None
                for hh in range(2):
                    k = k_ref[cols, hh * LANES:(hh + 1) * LANES]
                    s = _dot(qs[hh], k, "nt") * (SCALE_A * LOG2_E)
                    p = jnp.where(ok, jnp.exp2(s - lses[hh]), 0.0)
                    dp = _dot(dos[hh], v, "nt")
                    ds = (p * (dp - deltas[hh]) * SCALE_A).astype(BF16)
                    out.append(dqs[hh] + _dot(ds, k))
                    dk_ref[cols, hh * LANES:(hh + 1) * LANES] += _dot(ds, qs[hh], "tn")
                    part = _dot(p.astype(BF16), dos[hh], "tn")
                    dv = part if dv is None else dv + part
                dv_ref[cols, :] += dv
                return tuple(out)

            dqs = lax.fori_loop(0, i + 1, kv, (jnp.zeros((QB, LANES), F32),) * 2)
            for hh in range(2):
                dq_ref[rows, hh * LANES:(hh + 1) * LANES] = dqs[hh]
            return 0

        lax.fori_loop(0, S // QB, qblock, 0)

    pair = lambda w: pl.BlockSpec((S, w), lambda p: (0, p))
    return _carrier_call(
        body, name=name, grid=(MLA_H // 2,),
        in_specs=[pair(2 * LANES), pair(2 * LANES), pair(LANES), pair(LANES), pair(LANES),
                  pl.BlockSpec((S, LANES), lambda p: (0, do_block0 + p))],
        out_specs=[pair(2 * LANES), pair(2 * LANES), pair(LANES)],
        out_shape=[jax.ShapeDtypeStruct((S, MLA_H * LANES), F32), jax.ShapeDtypeStruct((S, MLA_H * LANES), F32),
                   jax.ShapeDtypeStruct((S, MLA_H * MLA_V), F32)],
        args=(qa, ka, va, o, lse, do), sem=("parallel",), carry=carry)


def _sb_weights(q_h, k, c, before, tri_suffix):
    z = _dot(q_h, k, "nt") * (SCALE_B * LOG2_E)
    sp = jnp.maximum(z, 0.0) + jnp.log(1.0 + jnp.exp2(-jnp.abs(z))) * LOG2_E
    log_keep = jnp.where(before, -sp, 0.0)
    to_the_right, total = _running_sum(log_keep, tri_suffix, True)
    w = jnp.where(before, jnp.exp2(z - sp + to_the_right + c), 0.0)
    return w, jnp.exp2(z - sp), total


def _sb_fwd(name, proj, carry=None):
    def body(q_ref, k_ref, v_ref, o_ref):
        m0b, m1b = _head_masks(BF16)
        tri_suffix = (_iota((MXU_W, MXU_W), 0) > _iota((MXU_W, MXU_W), 1)).astype(BF16)

        def qblock(i, _):
            r0 = pl.multiple_of(i * QB, QB)
            q = q_ref[pl.ds(r0, QB), :].astype(BF16)
            qs = [q * m0b, q * m1b]
            rowg = r0 + _iota((QB, QB), 0)

            def kv(step, carry):
                cs, acc = carry
                c0 = pl.multiple_of((i - step) * QB, QB)
                k = k_ref[pl.ds(c0, QB), :].astype(BF16)
                v = v_ref[pl.ds(c0, QB), :].astype(BF16)
                before = (c0 + _iota((QB, QB), 1)) < rowg
                new_c = []
                for hh in range(2):
                    w, _, tot = _sb_weights(qs[hh], k, cs[hh], before, tri_suffix)
                    new_c.append(cs[hh] + tot)
                    acc = acc + _dot(w.astype(BF16), v * (m0b if hh == 0 else m1b))
                return tuple(new_c), acc

            init = ((jnp.zeros((QB, 1), F32),) * 2, jnp.zeros((QB, LANES), F32))
            _, acc = lax.fori_loop(0, i + 1, kv, init)
            o_ref[pl.ds(r0, QB), :] = acc.astype(BF16)
            return 0

        lax.fori_loop(0, S // QB, qblock, 0)

    col = lambda base: pl.BlockSpec((S, LANES), lambda p: (0, base // LANES + p))
    return _carrier_call(
        body, name=name, grid=(SB_H // 2,), in_specs=[col(P_QB), col(P_KB), col(P_VB)],
        out_specs=[pl.BlockSpec((S, LANES), lambda p: (0, p))],
        out_shape=[jax.ShapeDtypeStruct((S, SB_H * SB_DIM), BF16)],
        args=(proj, proj, proj), sem=("parallel",), carry=carry)


def _sb_bwd(name, proj, do, do_block0, carry=None):
    nb = S // QB

    def body(q_ref, k_ref, v_ref, do_ref, dq_ref, dk_ref, dv_ref, sig_scr, dl_scr, dk_acc, dv_acc):
        m0b, m1b = _head_masks(BF16)
        tri_suffix = (_iota((MXU_W, MXU_W), 0) > _iota((MXU_W, MXU_W), 1)).astype(BF16)
        tri_prefix = (_iota((MXU_W, MXU_W), 0) < _iota((MXU_W, MXU_W), 1)).astype(BF16)
        dk_acc[...] = jnp.zeros_like(dk_acc)
        dv_acc[...] = jnp.zeros_like(dv_acc)

        def qblock(i, _):
            r0 = pl.multiple_of(i * QB, QB)
            rows = pl.ds(r0, QB)
            q = q_ref[rows, :].astype(BF16)
            qs = [q * m0b, q * m1b]
            dob = do_ref[rows, :].astype(BF16)
            dos = [dob * m0b, dob * m1b]
            rowg = r0 + _iota((QB, QB), 0)

            def sweep_left(step, cs):
                kb = i - step
                c0 = pl.multiple_of(kb * QB, QB)
                cols = pl.ds(c0, QB)
                k = k_ref[cols, :].astype(BF16)
                v = v_ref[cols, :].astype(BF16)
                before = (c0 + _iota((QB, QB), 1)) < rowg
                new_c = []
                dv = None
                for hh in range(2):
                    w, sig, tot = _sb_weights(qs[hh], k, cs[hh], before, tri_suffix)
                    new_c.append(cs[hh] + tot)
                    sig_scr[hh, kb] = sig
                    dl_scr[hh, kb] = _dot(dos[hh], v, "nt") * w
                    part = _dot(w.astype(BF16), dos[hh], "tn")
                    dv = part if dv is None else dv + part
                dv_acc[cols, :] += dv
                return tuple(new_c)

            lax.fori_loop(0, i + 1, sweep_left, (jnp.zeros((QB, 1), F32),) * 2)

            def sweep_right(kb, carry):
                ps, dq = carry
                c0 = pl.multiple_of(kb * QB, QB)
                cols = pl.ds(c0, QB)
                k = k_ref[cols, :].astype(BF16)
                before = (c0 + _iota((QB, QB), 1)) < rowg
                new_p = []
                dk = None
                for hh in range(2):
                    dl = dl_scr[hh, kb]
                    sig = sig_scr[hh, kb]
                    to_the_left, total = _running_sum(dl, tri_prefix, False)
                    earlier = to_the_left + ps[hh]
                    new_p.append(ps[hh] + total)
                    dz = (jnp.where(before, dl * (1.0 - sig) - earlier * sig, 0.0) * SCALE_B).astype(BF16)
                    dq = dq + _dot(dz, k * (m0b if hh == 0 else m1b))
                    part = _dot(dz, qs[hh], "tn")
                    dk = part if dk is None else dk + part
                dk_acc[cols, :] += dk
                return tuple(new_p), dq

            init = ((jnp.zeros((QB, 1), F32),) * 2, jnp.zeros((QB, LANES), F32))
            _, dq = lax.fori_loop(0, i + 1, sweep_right, init)
            dq_ref[rows, :] = dq.astype(BF16)
            return 0

        lax.fori_loop(0, nb, qblock, 0)
        dk_ref[...] = dk_acc[...].astype(BF16)
        dv_ref[...] = dv_acc[...].astype(BF16)

    col = lambda base: pl.BlockSpec((S, LANES), lambda p: (0, base // LANES + p))
    out = pl.BlockSpec((S, LANES), lambda p: (0, p))
    shape = jax.ShapeDtypeStruct((S, SB_H * SB_DIM), BF16)
    return _carrier_call(
        body, name=name, grid=(SB_H // 2,),
        in_specs=[col(P_QB), col(P_KB), col(P_VB), pl.BlockSpec((S, LANES), lambda p: (0, do_block0 + p))],
        out_specs=[out, out, out], out_shape=[shape, shape, shape],
        scratch_shapes=[pltpu.VMEM((2, nb, QB, QB), F32), pltpu.VMEM((2, nb, QB, QB), F32),
                        pltpu.VMEM((S, LANES), F32), pltpu.VMEM((S, LANES), F32)],
        args=(proj, proj, proj, do), sem=("parallel",), carry=carry)


def _band_row_index():
    j = np.arange(TOEP_W)
    rel = np.clip(LEFT_CHUNKS * CHUNK - j, -REL_CLIP, REL_CLIP) + REL_CLIP
    rel[BAND_W:] = 2 * REL_CLIP
    return rel.astype(np.int32)


def _band_tiles(r0_ref, q_ref, kpad, vpad, m, m0b, m1b, static_ok, bias):
    r0 = pl.multiple_of(m * BQ, BQ)
    q = q_ref[0, pl.ds(r0, BQ), :]
    kw = kpad[pl.ds(r0, BAND_W), :]
    vw = vpad[pl.ds(r0, BAND_W), :]
    ok = static_ok & ((r0 - BAND_PAD + _iota((BQ, BAND_W), 1)) >= 0)
    qs = [q * m0b, q * m1b]
    ps = []
    for hh in range(2):
        s = jnp.where(ok, _dot(qs[hh], kw, "nt") * (SCALE_C * LOG2_E) + bias[hh], NEG)
        e = jnp.exp2(s - jnp.max(s, axis=-1, keepdims=True))
        ps.append(e * (1.0 / jnp.sum(e, axis=-1, keepdims=True)))
    return r0, qs, kw, vw, ps


def _band_setup(qkv_ref, r0_ref, kpad, vpad):
    kpad[0:BAND_PAD, :] = jnp.zeros((BAND_PAD, LANES), BF16)
    vpad[0:BAND_PAD, :] = jnp.zeros((BAND_PAD, LANES), BF16)
    kpad[BAND_PAD:, :] = qkv_ref[1]
    vpad[BAND_PAD:, :] = qkv_ref[2]
    jc = lax.shift_right_logical(_iota((BQ, BAND_W), 1), 6)
    rc = lax.shift_right_logical(_iota((BQ, BAND_W), 0), 6)
    static_ok = (jc >= rc) & (jc <= rc + LEFT_CHUNKS)
    bias = []
    for hh in range(2):
        row = jnp.broadcast_to(r0_ref[hh:hh + 1, :] * LOG2_E, (BQ, TOEP_W))
        bias.append(pltpu.roll(row, 0, 1, stride=1, stride_axis=0)[:, :BAND_W])
    return static_ok, bias


def _band_fwd(name, qkv, r0, carry=None):
    def body(qkv_ref, r0_ref, o_ref, kpad, vpad):
        m0b, m1b = _head_masks(BF16)
        static_ok, bias = _band_setup(qkv_ref, r0_ref, kpad, vpad)

        def qblock(m, _):
            r0_, _, _, vw, ps = _band_tiles(r0_ref, qkv_ref, kpad, vpad, m, m0b, m1b, static_ok, bias)
            o = _dot(ps[0].astype(BF16), vw * m0b) + _dot(ps[1].astype(BF16), vw * m1b)
            o_ref[pl.ds(r0_, BQ), :] = o.astype(BF16)
            return 0

        lax.fori_loop(0, S // BQ, qblock, 0)

    return _carrier_call(
        body, name=name, grid=(C_H // 2,),
        in_specs=[pl.BlockSpec((3, S, LANES), lambda p: (0, 0, p)), pl.BlockSpec((None, 2, TOEP_W), lambda p: (p, 0, 0))],
        out_specs=[pl.BlockSpec((S, LANES), lambda p: (0, p))],
        out_shape=[jax.ShapeDtypeStruct((S, C_H * C_DIM), BF16)],
        scratch_shapes=[pltpu.VMEM((S + BAND_PAD, LANES), BF16), pltpu.VMEM((S + BAND_PAD, LANES), BF16)],
        args=(qkv, r0), sem=("parallel",), carry=carry)


def _band_bwd(name, qkv, r0, do, carry=None):
    def body(qkv_ref, r0_ref, do_ref, dqkv_ref, dr0_ref, kpad, vpad, dkpad, dvpad, db_acc):
        m0b, m1b = _head_masks(BF16)
        static_ok, bias = _band_setup(qkv_ref, r0_ref, kpad, vpad)
        dkpad[...] = jnp.zeros_like(dkpad)
        dvpad[...] = jnp.zeros_like(dvpad)
        db_acc[...] = jnp.zeros_like(db_acc)

        def qblock(m, _):
            r0_, qs, kw, vw, ps = _band_tiles(r0_ref, qkv_ref, kpad, vpad, m, m0b, m1b, static_ok, bias)
            dob = do_ref[pl.ds(r0_, BQ), :].astype(BF16)
            dos = [dob * m0b, dob * m1b]
            dq = None
            dk = None
            dv = None
            for hh in range(2):
                p = ps[hh]
                dp = _dot(dos[hh], vw, "nt")
                ds = p * (dp - jnp.sum(dp * p, axis=-1, keepdims=True))
                db_acc[hh, :, 0:BAND_W] += ds
                dsb = (ds * SCALE_C).astype(BF16)
                t = _dot(dsb, kw * (m0b if hh == 0 else m1b))
                dq = t if dq is None else dq + t
                t = _dot(dsb, qs[hh], "tn")
                dk = t if dk is None else dk + t
                t = _dot(p.astype(BF16), dos[hh], "tn")
                dv = t if dv is None else dv + t
            dqkv_ref[0, pl.ds(r0_, BQ), :] = dq.astype(BF16)
            dkpad[pl.ds(r0_, BAND_W), :] += dk
            dvpad[pl.ds(r0_, BAND_W), :] += dv
            return 0

        lax.fori_loop(0, S // BQ, qblock, 0)
        dqkv_ref[1] = dkpad[BAND_PAD:, :].astype(BF16)
        dqkv_ref[2] = dvpad[BAND_PAD:, :].astype(BF16)
        sub = _iota((8, TOEP_W), 0)
        for hh in range(2):
            folded = db_acc[hh, 0:8, :]
            for a in range(1, BQ // 8):
                folded = folded + pltpu.roll(db_acc[hh, 8 * a:8 * a + 8, :], TOEP_W - 8 * a, 1)
            for bit in range(3):
                moved = pltpu.roll(folded, TOEP_W - (1 << bit), 1)
                folded = jnp.where((sub & (1 << bit)) != 0, moved, folded)
            dr0_ref[hh:hh + 1, :] = jnp.sum(folded, axis=0, keepdims=True)

    return _carrier_call(
        body, name=name, grid=(C_H // 2,),
        in_specs=[pl.BlockSpec((3, S, LANES), lambda p: (0, 0, p)), pl.BlockSpec((None, 2, TOEP_W), lambda p: (p, 0, 0)),
                  pl.BlockSpec((S, LANES), lambda p: (0, p))],
        out_specs=[pl.BlockSpec((3, S, LANES), lambda p: (0, 0, p)), pl.BlockSpec((None, 2, TOEP_W), lambda p: (p, 0, 0))],
        out_shape=[jax.ShapeDtypeStruct((3, S, C_H * C_DIM), BF16), jax.ShapeDtypeStruct((C_H // 2, 2, TOEP_W), F32)],
        scratch_shapes=[pltpu.VMEM((S + BAND_PAD, LANES), BF16), pltpu.VMEM((S + BAND_PAD, LANES), BF16),
                        pltpu.VMEM((S + BAND_PAD, LANES), F32), pltpu.VMEM((S + BAND_PAD, LANES), F32),
                        pltpu.VMEM((2, BQ, TOEP_W), F32)],
        args=(qkv, r0, do), sem=("parallel",), carry=carry)


def _bias_table_grad(name, dr0):
    w_out = 5 * LANES

    def body(d_ref, o_ref):
        j = _iota((TOEP_W, w_out), 0)
        rel = jnp.clip(LEFT_CHUNKS * CHUNK - j, -REL_CLIP, REL_CLIP) + REL_CLIP
        rel = jnp.where(j >= BAND_W, 2 * REL_CLIP, rel)
        onehot = (rel == _iota((TOEP_W, w_out), 1)).astype(BF16)
        d = d_ref[...]
        hi = d.astype(BF16)
        mid = (d - hi.astype(F32))
        mid_b = mid.astype(BF16)
        lo = (mid - mid_b.astype(F32)).astype(BF16)
        o_ref[...] = _dot(hi, onehot) + _dot(mid_b, onehot) + _dot(lo, onehot)

    return pl.pallas_call(
        body, name=name, out_shape=jax.ShapeDtypeStruct((C_H, w_out), F32),
        in_specs=[pl.BlockSpec((C_H, TOEP_W), lambda: (0, 0))], out_specs=pl.BlockSpec((C_H, w_out), lambda: (0, 0)),
        grid=(),
    )(dr0)


def _carry_gather(cy, slots, names, ici, d2d):
    idx = [cy.operand(slots[n], True) for n in names]
    n = len(names)
    base_i = cy.sems(3 * n) if ici else 0
    base_d = cy.sems(3 * n) if d2d else 0

    def piece(refs, t, slot, cc):
        return refs[idx[t]].at[slot, _half_rows(cc, slots[names[t]].shape[1] // 2), :]

    def over_ici(refs, send, recv, arriving):
        x, y, c, chips = _position()
        out = []
        for t in range(n):
            for j in range(3):
                r = piece(refs, t, 2 * chips[j][0] + chips[j][1] if arriving else 2 * x + y, c)
                out.append(_remote(r, r, send, recv, base_i + 3 * t + j, (*chips[j], c)))
        return out

    def over_d2d(refs, send, recv, arriving):
        x, y, c, chips = _position()
        out = []
        for t in range(n):
            for j in range(3):
                r = piece(refs, t, 2 * chips[j][0] + chips[j][1], 1 - c if arriving else c)
                out.append(_remote(r, r, send, recv, base_d + 3 * t + j, (x, y, 1 - c)))
        return out

    def start_ici(refs, fresh, send, recv):
        for cp in over_ici(refs, send, recv, False):
            cp.start()

    def wait_ici(refs, fresh, send, recv):
        for cp in over_ici(refs, send, recv, True):
            cp.wait_recv()
        for cp in over_ici(refs, send, recv, False):
            cp.wait_send()

    def start_d2d(refs, fresh, send, recv):
        for cp in over_d2d(refs, send, recv, False):
            cp.start()

    def wait_d2d(refs, fresh, send, recv):
        for cp in over_d2d(refs, send, recv, True):
            cp.wait_recv()
        for cp in over_d2d(refs, send, recv, False):
            cp.wait_send()

    if ici and d2d:
        cy.starts.append(start_ici)
        cy.finishes += [wait_ici, start_d2d, wait_d2d]
    elif ici:
        cy.starts.append(start_ici)
        cy.finishes.append(wait_ici)
    else:
        cy.starts.append(start_d2d)
        cy.finishes.append(wait_d2d)

    def done(aliased, fresh):
        for t, name in enumerate(names):
            slots[name] = aliased[idx[t]]

    cy.on_done.append(done)


def _carry_chip_exchange(cy, sums, got, names):
    idx = [cy.operand(sums[n], False) for n in names]
    out = [cy.result((3,) + sums[n].shape[1:], BF16) for n in names]
    base = cy.sems(3 * len(names))

    def copies(refs, fresh, send, recv):
        x, y, c, chips = _position()
        return [_remote(refs[idx[t]].at[2 * chips[j][0] + chips[j][1]], fresh[out[t]].at[j], send, recv, base + 3 * t + j,
                        (*chips[j], c)) for t in range(len(names)) for j in range(3)]

    def start(refs, fresh, send, recv):
        for cp in copies(refs, fresh, send, recv):
            cp.start()

    def wait(refs, fresh, send, recv):
        for cp in copies(refs, fresh, send, recv):
            cp.wait()

    cy.starts.append(start)
    cy.finishes.append(wait)

    def done(aliased, fresh):
        for t, name in enumerate(names):
            got[name] = fresh[out[t]]

    cy.on_done.append(done)


def _run_carry(name, cy):
    _, res = _carrier_call(None, name=name, grid=(), in_specs=[], out_specs=[], out_shape=[], args=(), sem=(), carry=cy)
    cy.done(res)


FIRST_WEIGHTS = ("ev_w_in",)
NEXT_WEIGHTS = ("ev_w_uq", "ev_w_ukv")
WEIGHTS_A = ("ev_w_out", "w_gate0", "w_up0")
WEIGHTS_B = ("w_down0", "od_w_qkv", "od_w_out")
WEIGHTS_C = ("w_gate1",)
WEIGHTS_D = ("w_up1", "w_down1")
GRAD_GROUPS = {"ffn1": ("w_gate1", "w_up1", "w_down1"), "od": ("od_w_qkv", "od_w_out"),
               "ffn0": ("w_gate0", "w_up0", "w_down0"), "ev_out": ("ev_w_out",),
               "ev": ("ev_w_in", "ev_w_uq", "ev_w_ukv")}


def _carry_pair_exchange(cy, parts, theirs, names):
    idx = [cy.operand(parts[n], False) for n in names]
    out = [cy.result((N_CHIPS, parts[n].shape[1] // 2, parts[n].shape[2]), BF16) for n in names]
    base = cy.sems(len(names))

    def copies(refs, fresh, send, recv):
        x, y, c, _ = _position()
        return [_remote(refs[idx[t]].at[:, _half_rows(1 - c, parts[n].shape[1] // 2), :], fresh[out[t]], send, recv,
                        base + t, (x, y, 1 - c)) for t, n in enumerate(names)]

    cy.starts.append(lambda refs, fresh, send, recv: [cp.start() for cp in copies(refs, fresh, send, recv)])
    cy.finishes.append(lambda refs, fresh, send, recv: [cp.wait() for cp in copies(refs, fresh, send, recv)])

    def done(aliased, fresh):
        for t, name in enumerate(names):
            theirs[name] = fresh[out[t]]

    cy.on_done.append(done)


def _carry_sibling_exchange(cy, fulls, pieces):
    idx = [cy.operand(fulls[p], True) for p, _ in pieces]
    base = cy.sems(len(pieces))

    def copies(refs, send, recv, arriving):
        x, y, c, _ = _position()
        out = []
        for t, (p, layer) in enumerate(pieces):
            r = refs[idx[t]].at[layer, _half_rows(1 - c if arriving else c, fulls[p].shape[1] // 2), :]
            out.append(_remote(r, r, send, recv, base + t, (x, y, 1 - c)))
        return out

    def start(refs, fresh, send, recv):
        for cp in copies(refs, send, recv, False):
            cp.start()

    def wait(refs, fresh, send, recv):
        for cp in copies(refs, send, recv, True):
            cp.wait_recv()
        for cp in copies(refs, send, recv, False):
            cp.wait_send()

    cy.starts.append(start)
    cy.finishes.append(wait)

    def done(aliased, fresh):
        for t, (p, _) in enumerate(pieces):
            fulls[p] = aliased[idx[t]]

    cy.on_done.append(done)


RIDES = {
    "cast_rest": (("gather", FIRST_WEIGHTS),),
    "proj_in": (("gather", NEXT_WEIGHTS),),
    "mla_attn": (("gather_ici", WEIGHTS_A),),
    "sb_attn": (("gather_d2d", WEIGHTS_A), ("gather_ici", WEIGHTS_B)),
    "ev_out": (("gather_d2d", WEIGHTS_B),),
    "ffn0": (("gather_ici", WEIGHTS_C),),
    "qkv": (("gather_d2d", WEIGHTS_C),),
    "band_attn": (("gather_ici", WEIGHTS_D),),
    "od_out": (("gather_d2d", WEIGHTS_D),),
    "od_out_bwd_w": (("pair", "ffn1"),),
    "band_attn_bwd": (("chips", "ffn1"),),
    "rms_mix1_bwd": (("pair", "od"),),
    "ev_out_bwd_w": (("pair", "ffn0"),),
    "mla_attn_bwd": (("chips", "od"), ("sibling", "ffn1"), ("pair", "ev_out")),
    "sb_attn_bwd": (("chips", "ffn0"), ("sibling", "od"), ("chips", "ev_out")),
    "proj_in_bwd_w": (("sibling", "ffn0"), ("sibling", "ev_out")),
    "grads_pair_ev": (("pair", "ev"),),
    "proj_in_bwd_x": (("chips", "ev"),),
    "grads_sibling_ev": (("sibling", "ev"),),
}


class _Exchanges:
    def __init__(self, slots, pos, shapes, cast_rest):
        self.slots, self.pos, self.shapes, self.cast_rest = dict(slots), pos, shapes, cast_rest
        self.parts, self.theirs, self.sums, self.got, self.fulls = {}, {}, {}, {}, {}

    def begin(self):
        self.slots.update(self.cast_rest(self.carry("cast_rest")))

    def weights(self, *names):
        return [self.slots[n] for n in names]

    def _pair_sums(self, group):
        names = GRAD_GROUPS[group]
        self.sums.update(zip(names, _pair_sums("pair_sums_" + group, [self.parts[n] for n in names],
                                               [self.theirs[n] for n in names], self.pos)))

    def _chip_sums(self, group):
        names = GRAD_GROUPS[group]
        items = [(self.sums[n], self.got[n], PART_OF[n][1], self.shapes[PART_OF[n][0]], self.fulls.get(PART_OF[n][0]))
                 for n in names]
        self.fulls.update(zip([PART_OF[n][0] for n in names], _chip_sums("chip_sums_" + group, items, self.pos)))

    def carry(self, stage):
        cy = _Carry()
        for step, what in RIDES[stage]:
            if step == "gather":
                _carry_gather(cy, self.slots, what, True, True)
            elif step == "gather_ici":
                _carry_gather(cy, self.slots, what, True, False)
            elif step == "gather_d2d":
                _carry_gather(cy, self.slots, what, False, True)
            elif step == "pair":
                _carry_pair_exchange(cy, self.parts, self.theirs, GRAD_GROUPS[what])
            elif step == "chips":
                self._pair_sums(what)
                _carry_chip_exchange(cy, self.sums, self.got, GRAD_GROUPS[what])
            elif step == "sibling":
                self._chip_sums(what)
                _carry_sibling_exchange(cy, self.fulls, [PART_OF[n] for n in GRAD_GROUPS[what]])
        return cy

    def grads(self, group, parts):
        self.parts.update(parts)
        if group == "ev":
            _run_carry("grads_pair_ev", self.carry("grads_pair_ev"))

    def finish(self):
        _run_carry("grads_sibling_ev", self.carry("grads_sibling_ev"))
        return {n: self.fulls[n] for n in BIG}


class _NoExchanges:
    def __init__(self, slots):
        self.slots, self.parts = dict(slots), {}

    def begin(self):
        pass

    def weights(self, *names):
        return [self.slots[n] for n in names]

    def carry(self, stage):
        return None

    def grads(self, group, parts):
        self.parts.update(parts)


def _w_in_pieces():
    segments = ((0, Q_LORA, P_CQ), (Q_LORA, Q_LORA + KV_LORA, P_CKV),
                (Q_LORA + KV_LORA, Q_LORA + KV_LORA + MLA_ROPE, P_KR + KR_LANE),
                (Q_LORA + KV_LORA + MLA_ROPE, EVEN_IN, P_QB))
    width = EVEN_IN // N_CHIPS
    pieces = []
    for lo, hi, at in segments:
        for k in range(N_CHIPS):
            a, b = max(lo, k * width), min(hi, (k + 1) * width)
            if a < b:
                pieces.append((k, a - k * width, b - a, at + a - lo))
    return pieces


def _w_in_padded(name, w_in_s):
    tr = MXU_W

    def body(s_ref, o_ref):
        o_ref[...] = jnp.zeros(o_ref.shape, BF16)
        for k, a, n, at in _w_in_pieces():
            o_ref[:, at:at + n] = s_ref[k, :, a:a + n]

    return pl.pallas_call(
        body, name=name, grid=(D // tr,),
        in_specs=[pl.BlockSpec((N_CHIPS, tr, EVEN_IN // N_CHIPS), lambda i: (0, i, 0))],
        out_specs=pl.BlockSpec((tr, P_IN), lambda i: (i, 0)), out_shape=jax.ShapeDtypeStruct((D, P_IN), BF16),
        compiler_params=_params("parallel"))(w_in_s)


def _w_in_sharded(name, d_w_in_p):
    tr = MXU_W

    def body(p_ref, o_ref):
        for k, a, n, at in _w_in_pieces():
            o_ref[k, :, a:a + n] = p_ref[:, at:at + n]

    return pl.pallas_call(
        body, name=name, grid=(D // tr,),
        in_specs=[pl.BlockSpec((tr, P_IN), lambda i: (i, 0))],
        out_specs=pl.BlockSpec((N_CHIPS, tr, EVEN_IN // N_CHIPS), lambda i: (0, i, 0)),
        out_shape=jax.ShapeDtypeStruct((N_CHIPS, D, EVEN_IN // N_CHIPS), BF16),
        compiler_params=_params("parallel"))(d_w_in_p)


def _mla_weights(w_uq_s, w_ukv_s):
    w_uq = jnp.moveaxis(w_uq_s, 0, 1).reshape(Q_LORA, MLA_H, MLA_NOPE + MLA_ROPE)
    w_uq_p = jnp.concatenate([w_uq, jnp.zeros((Q_LORA, MLA_H, LANES - MLA_NOPE - MLA_ROPE), BF16)], axis=2)
    w_ukv = jnp.moveaxis(w_ukv_s, 0, 1).reshape(KV_LORA, MLA_H, MLA_NOPE + MLA_V)
    w_uk_p = jnp.concatenate([w_ukv[:, :, :MLA_NOPE], jnp.zeros((KV_LORA, MLA_H, LANES - MLA_NOPE), BF16)], axis=2)
    return dict(
        w_uq=w_uq_p.reshape(Q_LORA, MLA_H * LANES), w_uk=w_uk_p.reshape(KV_LORA, MLA_H * LANES),
        w_uv=w_ukv[:, :, MLA_NOPE:].reshape(KV_LORA, MLA_H * MLA_V))


def _proj_mm(name, u, w_in, carry=None):
    return _mm(name, u, w_in, kind="nn", grid=(S // TM, 1, 1),
               a_spec=pl.BlockSpec((TM, D), lambda i, j, k: (i, 0)), b_spec=pl.BlockSpec((D, P_IN), lambda i, j, k: (0, 0)),
               o_spec=pl.BlockSpec((TM, P_IN), lambda i, j, k: (i, 0)), out_shape=(S, P_IN), out_dtype=F32, acc_shape=None,
               carry=carry)


def _out_proj(name, o, w, resid, carry=None):
    return _mm(name, o, w, kind="nn", grid=(S // TM, 1, 1),
               a_spec=pl.BlockSpec((TM, D), lambda i, j, k: (i, 0)), b_spec=pl.BlockSpec((D, D), lambda i, j, k: (0, 0)),
               o_spec=pl.BlockSpec((TM, D), lambda i, j, k: (i, 0)), out_shape=(S, D), out_dtype=F32, acc_shape=None,
               resid=resid, r_spec=pl.BlockSpec((TM, D), lambda i, j, k: (i, 0)), carry=carry)


def _out_proj_bwd(name, dh, o, w, ex):
    d_o = _mm(name + "_x", dh, w, kind="nt", grid=(S // TM, 1, 1),
              a_spec=pl.BlockSpec((TM, D), lambda i, j, k: (i, 0)), b_spec=pl.BlockSpec((D, D), lambda i, j, k: (0, 0)),
              o_spec=pl.BlockSpec((TM, D), lambda i, j, k: (i, 0)), out_shape=(S, D), out_dtype=F32, acc_shape=None)
    d_w = _mm(name + "_w", o, dh, kind="tn", grid=(2, S // TM),
              a_spec=pl.BlockSpec((TM, TM), lambda j, k: (k, j)), b_spec=pl.BlockSpec((TM, D), lambda j, k: (k, 0)),
              o_spec=pl.BlockSpec((TM, D), lambda j, k: (j, 0)), out_shape=(D, D), out_dtype=BF16, acc_shape=(TM, D),
              carry=ex.carry(name + "_w"))
    return d_o, d_w


def _local_step(x, tgt, sm, ex):
    def riding(stage, fn, *args):
        cy = ex.carry(stage)
        res, copies = fn(stage, *args, carry=cy)
        if cy is not None:
            cy.done(copies)
        return res

    cos_t, sin_t = _rope_tables()
    g_mix, g_ffn = sm["g_mix"], sm["g_ffn"]
    r0 = sm["od_rel_bias"][0][:, _band_row_index()].reshape(C_H // 2, 2, TOEP_W)
    nt = 3

    ex.begin()
    w = {"w_in": _w_in_padded("w_in_padded", *ex.weights(*FIRST_WEIGHTS))}
    u0 = _rms_fwd("rms_mix0", x, g_mix[0:1])
    proj = _proj_mm("proj_in", u0, w["w_in"], ex.carry("proj_in"))
    w.update(_mla_weights(*ex.weights(*NEXT_WEIGHTS)))
    qa, ka, va = _mla_prep_fwd("mla_prep", proj, sm["ev_g_cq"], sm["ev_g_ckv"], w["w_uq"], w["w_uk"], w["w_uv"], cos_t, sin_t)
    o_a, lse = riding("mla_attn", _mla_fwd, qa, ka, va)
    o_b, = riding("sb_attn", _sb_fwd, proj)
    o_ev = jnp.concatenate([o_a.astype(BF16), o_b], axis=1)
    w["ev_w_out"] = ex.weights("ev_w_out")[0].reshape(D, D)
    h1 = _out_proj("ev_out", o_ev, w["ev_w_out"], x, ex.carry("ev_out"))
    w["w_gate0"], w["w_up0"], w["w_down0"] = ex.weights("w_gate0", "w_up0", "w_down0")
    h2, gate0, up0 = riding("ffn0", _ffn_fwd, h1, g_ffn[0:1], w["w_gate0"], w["w_up0"], w["w_down0"])
    w["w_qkv"] = jnp.moveaxis(ex.weights("od_w_qkv")[0], 0, 1).reshape(D, nt * D)
    u2 = _rms_fwd("rms_mix1", h2, g_mix[1:2])
    qkv = _mm("qkv", u2, w["w_qkv"], kind="nn", grid=(S // TM, nt, 1),
              a_spec=pl.BlockSpec((TM, D), lambda i, t, k: (i, 0)), b_spec=pl.BlockSpec((D, D), lambda i, t, k: (0, t)),
              o_spec=pl.BlockSpec((None, TM, D), lambda i, t, k: (t, i, 0)),
              out_shape=(nt, S, D), out_dtype=BF16, acc_shape=None, carry=ex.carry("qkv"))
    o_od, = riding("band_attn", _band_fwd, qkv, r0)
    w["od_w_out"] = ex.weights("od_w_out")[0].reshape(D, D)
    h3 = _out_proj("od_out", o_od, w["od_w_out"], h2, ex.carry("od_out"))
    w["w_gate1"], w["w_up1"], w["w_down1"] = ex.weights("w_gate1", "w_up1", "w_down1")
    (h4, gate1, up1), _ = _ffn_fwd("ffn1", h3, g_ffn[1:2], w["w_gate1"], w["w_up1"], w["w_down1"])

    loss, dh4, dg_final = _loss_bwd("loss", h4, sm["g_final"].reshape(1, D), tgt)

    dh3, dg_ffn1, u3, dgate, dup, act = _ffn_bwd("ffn1_bwd", dh4, h3, g_ffn[1:2], gate1, up1,
                                                 w["w_gate1"], w["w_up1"], w["w_down1"])
    d_wg1, d_wu1, d_wd1 = _ffn_wgrads("ffn1_dw", u3, dgate, dup, act, dh4)
    ex.grads("ffn1", {"w_gate1": d_wg1, "w_up1": d_wu1, "w_down1": d_wd1})

    d_ood, d_w_od_out = _out_proj_bwd("od_out_bwd", dh3, o_od, w["od_w_out"], ex)
    dqkv, dr0 = riding("band_attn_bwd", _band_bwd, qkv, r0, d_ood)
    du2 = _mm("qkv_bwd_x", dqkv, w["w_qkv"], kind="nt", grid=(S // TM, nt),
              a_spec=pl.BlockSpec((None, TM, D), lambda i, t: (t, i, 0)), b_spec=pl.BlockSpec((D, D), lambda i, t: (0, t)),
              o_spec=pl.BlockSpec((TM, D), lambda i, t: (i, 0)), out_shape=(S, D), out_dtype=F32, acc_shape=(TM, D))
    wide, per = D // MXU_W, nt * D // N_CHIPS // MXU_W
    piece = lambda r: pl.BlockSpec((None, TM, MXU_W), lambda j, k: ((per * j + r) // wide, k, (per * j + r) % wide))
    d_w_qkv = _mm("qkv_bwd_w", u2, dqkv, kind="tn", grid=(N_CHIPS, S // TM),
                  a_spec=pl.BlockSpec((TM, D), lambda j, k: (k, 0)), b_spec=[piece(r) for r in range(per)],
                  o_spec=pl.BlockSpec((None, D, per * MXU_W), lambda j, k: (j, 0, 0)),
                  out_shape=(N_CHIPS, D, per * MXU_W), out_dtype=BF16, acc_shape=(D, per * MXU_W))
    shard_cols = lambda a: jnp.moveaxis(a.reshape(a.shape[0], N_CHIPS, a.shape[1] // N_CHIPS), 1, 0)
    ex.grads("od", {"od_w_qkv": d_w_qkv, "od_w_out": d_w_od_out.reshape(N_CHIPS, D // N_CHIPS, D)})
    dh2, dg_mix1 = _rms_bwd("rms_mix1_bwd", du2, h2, g_mix[1:2], dh3, carry=ex.carry("rms_mix1_bwd"))
    d_rel = _bias_table_grad("rel_bias_grad", dr0.reshape(C_H, TOEP_W))[:, :2 * REL_CLIP + 1]

    dh1, dg_ffn0, u1, dgate, dup, act = _ffn_bwd("ffn0_bwd", dh2, h1, g_ffn[0:1], gate0, up0,
                                                 w["w_gate0"], w["w_up0"], w["w_down0"])
    d_wg0, d_wu0, d_wd0 = _ffn_wgrads("ffn0_dw", u1, dgate, dup, act, dh2)
    ex.grads("ffn0", {"w_gate0": d_wg0, "w_up0": d_wu0, "w_down0": d_wd0})

    d_oev, d_w_ev_out = _out_proj_bwd("ev_out_bwd", dh1, o_ev, w["ev_w_out"], ex)
    ex.grads("ev_out", {"ev_w_out": d_w_ev_out.reshape(N_CHIPS, D // N_CHIPS, D)})
    dqa, dka, dva = riding("mla_attn_bwd", _mla_bwd, qa, ka, va, o_a, lse, d_oev, 0)
    dqb, dkb, dvb = riding("sb_attn_bwd", _sb_bwd, proj, d_oev, MLA_H * MLA_V // LANES)
    dcq, dckv, dkr, d_w_uq, d_w_uk, d_w_uv, dg_cq, dg_ckv = _mla_prep_bwd(
        "mla_prep_bwd", dqa, dka, dva, proj, sm["ev_g_cq"], sm["ev_g_ckv"], w["w_uq"], w["w_uk"], w["w_uv"], cos_t, sin_t)
    dproj = [dcq, jnp.zeros((S, LANES), BF16), dckv, dqb, dkb, dvb, dkr]
    d_w_in_p = _mm("proj_in_bwd_w", u0, dproj, kind="tn", grid=(1, S // TM),
                   a_spec=pl.BlockSpec((TM, D), lambda j, k: (k, 0)),
                   b_spec=[pl.BlockSpec((TM, p.shape[1]), lambda j, k: (k, 0)) for p in dproj],
                   o_spec=pl.BlockSpec((D, P_IN), lambda j, k: (0, 0)), out_shape=(D, P_IN), out_dtype=BF16,
                   acc_shape=(D, P_IN), carry=ex.carry("proj_in_bwd_w"))
    d_w_uq_std = d_w_uq.reshape(Q_LORA, MLA_H, LANES)[:, :, :MLA_NOPE + MLA_ROPE].reshape(Q_LORA, -1)
    d_w_ukv = jnp.concatenate([d_w_uk.reshape(KV_LORA, MLA_H, LANES)[:, :, :MLA_NOPE],
                               d_w_uv.reshape(KV_LORA, MLA_H, MLA_V)], axis=2).reshape(KV_LORA, -1)
    ex.grads("ev", {"ev_w_in": _w_in_sharded("w_in_sharded", d_w_in_p), "ev_w_uq": shard_cols(d_w_uq_std.astype(BF16)),
                    "ev_w_ukv": shard_cols(d_w_ukv.astype(BF16))})
    du0 = _mm("proj_in_bwd_x", dproj, w["w_in"], kind="nt", grid=(S // TM, 1, 1),
              a_spec=[pl.BlockSpec((TM, p.shape[1]), lambda i, j, k: (i, 0)) for p in dproj],
              b_spec=pl.BlockSpec((D, P_IN), lambda i, j, k: (0, 0)),
              o_spec=pl.BlockSpec((TM, D), lambda i, j, k: (i, 0)), out_shape=(S, D), out_dtype=F32, acc_shape=None,
              carry=ex.carry("proj_in_bwd_x"))
    grad_x, dg_mix0 = _rms_bwd("rms_mix0_bwd", du0, x, g_mix[0:1], dh1)
    small = {
        "ev_g_cq": dg_cq, "ev_g_ckv": dg_ckv, "od_rel_bias": d_rel.reshape(1, C_H, 2 * REL_CLIP + 1),
        "g_mix": jnp.concatenate([dg_mix0, dg_mix1], axis=0), "g_ffn": jnp.concatenate([dg_ffn0, dg_ffn1], axis=0),
        "g_final": dg_final.reshape(D),
    }
    return loss, grad_x, small


BIG = ("ev_w_in", "ev_w_uq", "ev_w_ukv", "ev_w_out", "od_w_qkv", "od_w_out", "w_gate", "w_up", "w_down")
SMALL = ("ev_g_cq", "ev_g_ckv", "od_rel_bias", "g_mix", "g_ffn", "g_final")
WEIGHTS = ("ev_w_in", "ev_g_cq", "ev_w_uq", "ev_g_ckv", "ev_w_ukv", "ev_w_out", "od_w_qkv", "od_rel_bias", "od_w_out",
           "g_mix", "g_ffn", "w_gate", "w_up", "w_down", "g_final")
GRAD_PARTS = (("ev_w_in", "ev_w_in", 0), ("ev_w_uq", "ev_w_uq", 0), ("ev_w_ukv", "ev_w_ukv", 0),
              ("ev_w_out", "ev_w_out", 0), ("od_w_qkv", "od_w_qkv", 0), ("od_w_out", "od_w_out", 0),
              ("w_gate0", "w_gate", 0), ("w_gate1", "w_gate", 1), ("w_up0", "w_up", 0), ("w_up1", "w_up", 1),
              ("w_down0", "w_down", 0), ("w_down1", "w_down", 1))
PART_OF = {part: (param, layer) for part, param, layer in GRAD_PARTS}
TRANSPOSED = ("w_gate", "w_up")
ADAMW_TRANSPOSED = ("ev_w_in", "ev_w_uq")


def _row_tile(rows, cap=512, sublanes=16):
    for t in range(min(rows, cap), 0, -1):
        if rows % t == 0 and t % sublanes == 0:
            return t
    return rows


def _cast_into_slot(name, w, layer, pos):
    _, rows, cols = w.shape
    tr = _row_tile(rows)

    def body(pos_ref, w_ref, o_ref):
        o_ref[...] = w_ref[...].astype(BF16)

    return pl.pallas_call(
        body, name=name,
        grid_spec=pltpu.PrefetchScalarGridSpec(
            num_scalar_prefetch=1, grid=(rows // tr,),
            in_specs=[pl.BlockSpec((None, tr, cols), lambda i, p: (layer, i, 0))],
            out_specs=pl.BlockSpec((None, tr, cols), lambda i, p: (p[0], i, 0))),
        out_shape=jax.ShapeDtypeStruct((N_CHIPS, rows, cols), BF16), compiler_params=_params("arbitrary"))(pos, w)


def _cast_many_into_slots(name, items, pos, carry):
    tiles = [_row_tile(w.shape[1]) for w, _ in items]
    turns = _Turns([w.shape[1] // tr for (w, _), tr in zip(items, tiles)])

    def body(pos_ref, *refs):
        i = pl.program_id(0)
        for t in range(len(items)):
            @pl.when(turns.mine(t, i))
            def _(w_ref=refs[t], o_ref=refs[len(items) + t]):
                o_ref[...] = w_ref[...].astype(BF16)

    in_specs, out_specs, out_shape = [], [], []
    for t, ((w, layer), tr) in enumerate(zip(items, tiles)):
        _, rows, cols = w.shape
        at = turns.step(t)
        in_specs.append(pl.BlockSpec((None, tr, cols), lambda i, p, at=at, layer=layer: (layer, at(i), 0)))
        out_specs.append(pl.BlockSpec((None, tr, cols), lambda i, p, at=at: (p[0], at(i), 0)))
        out_shape.append(jax.ShapeDtypeStruct((N_CHIPS, rows, cols), BF16))
    res, copies = _carrier_call(body, name=name, grid=(turns.total,), in_specs=in_specs, out_specs=out_specs,
                                out_shape=out_shape, args=[w for w, _ in items], sem=("arbitrary",), carry=carry,
                                prefetch=(pos,))
    if carry is not None:
        carry.done(copies)
    return res


class _Turns:
    def __init__(self, counts):
        self.counts = list(counts)
        self.starts = [sum(self.counts[:t]) for t in range(len(self.counts))]
        self.total = sum(self.counts)

    def step(self, t):
        start, n = self.starts[t], self.counts[t]
        return lambda i: jnp.clip(i - start, 0, n - 1)

    def mine(self, t, i):
        return (i >= self.starts[t]) & (i < self.starts[t] + self.counts[t])


def _pair_sums(name, parts, theirs, pos):
    n, pair = len(parts), 2
    tiles = [_row_tile(b.shape[1]) for b in theirs]
    blocks = [b.shape[1] // tr for b, tr in zip(theirs, tiles)]
    turns = _Turns([N_CHIPS // pair * nb for nb in blocks])

    def body(pos_ref, *refs):
        i = pl.program_id(0)
        for t in range(n):
            @pl.when(turns.mine(t, i))
            def _(a_ref=refs[2 * t], b_ref=refs[2 * t + 1], o_ref=refs[2 * n + t]):
                o_ref[...] = (a_ref[...].astype(F32) + b_ref[...].astype(F32)).astype(BF16)

    in_specs, out_specs = [], []
    for t, (b, tr, nb) in enumerate(zip(theirs, tiles, blocks)):
        at, block = turns.step(t), (pair, tr, b.shape[2])
        in_specs.append(pl.BlockSpec(block, lambda i, p, at=at, nb=nb: (at(i) // nb, p[1] * nb + at(i) % nb, 0)))
        in_specs.append(pl.BlockSpec(block, lambda i, p, at=at, nb=nb: (at(i) // nb, at(i) % nb, 0)))
        out_specs.append(pl.BlockSpec(block, lambda i, p, at=at, nb=nb: (at(i) // nb, at(i) % nb, 0)))
    return pl.pallas_call(
        body, name=name,
        grid_spec=pltpu.PrefetchScalarGridSpec(num_scalar_prefetch=1, grid=(turns.total,), in_specs=in_specs,
                                               out_specs=out_specs),
        out_shape=[jax.ShapeDtypeStruct(b.shape, BF16) for b in theirs],
        compiler_params=_params("arbitrary"))(pos, *[a for pair in zip(parts, theirs) for a in pair])


def _chip_sums(name, items, pos):
    n = len(items)
    tiles = [_row_tile(s.shape[1]) for s, *_ in items]
    turns = _Turns([s.shape[1] // tr for (s, *_), tr in zip(items, tiles)])
    carried = [t for t, item in enumerate(items) if item[4] is not None]

    def body(pos_ref, *refs):
        i = pl.program_id(0)
        for t in range(n):
            @pl.when(turns.mine(t, i))
            def _(s_ref=refs[2 * t], g_ref=refs[2 * t + 1], o_ref=refs[2 * n + len(carried) + t]):
                o_ref[...] = ((s_ref[...].astype(F32) + g_ref[0].astype(F32)) + g_ref[1].astype(F32)) + g_ref[2].astype(F32)

    in_specs, out_specs = [], []
    for t, ((s, got, layer, full_shape, full), tr) in enumerate(zip(items, tiles)):
        at, cols, nb = turns.step(t), s.shape[2], turns.counts[t]
        in_specs.append(pl.BlockSpec((None, tr, cols), lambda i, p, at=at: (p[0], at(i), 0)))
        in_specs.append(pl.BlockSpec((3, tr, cols), lambda i, p, at=at: (0, at(i), 0)))
        out_specs.append(pl.BlockSpec((None, tr, cols), lambda i, p, at=at, nb=nb, layer=layer: (layer, p[1] * nb + at(i), 0)))
    return pl.pallas_call(
        body, name=name,
        grid_spec=pltpu.PrefetchScalarGridSpec(num_scalar_prefetch=1, grid=(turns.total,),
                                               in_specs=in_specs + [ANY] * len(carried), out_specs=out_specs),
        out_shape=[jax.ShapeDtypeStruct(item[3], F32) for item in items],
        input_output_aliases={1 + 2 * n + k: t for k, t in enumerate(carried)},
        compiler_params=_params("arbitrary"))(
            pos, *[a for item in items for a in item[:2]], *[items[t][4] for t in carried])


def _adamw_update(w, g, m, v):
    m_new = ADAM_B1 * m + (1.0 - ADAM_B1) * g
    v_new = ADAM_B2 * v + (1.0 - ADAM_B2) * (g * g)
    m_hat = m_new / (1.0 - ADAM_B1 ** ADAM_STEP)
    v_hat = v_new / (1.0 - ADAM_B2 ** ADAM_STEP)
    return -ADAM_LR * (m_hat / (jnp.sqrt(v_hat) + ADAM_EPS) + ADAM_WD * w), m_new, v_new


def _small_step(name, grads, loss, w, m, v):
    n, n_dev = len(grads), 8
    offs = [sum(g.shape[0] for g in grads[:t]) for t in range(n + 1)]
    rows = -(-(offs[n] + 1) // 8) * 8
    width = max(g.shape[1] for g in grads)

    def body(*refs):
        g_refs, loss_ref = refs[:n], refs[n]
        w_refs, m_refs, v_refs = (refs[1 + k * n:1 + (k + 1) * n] for k in (1, 2, 3))
        outs = refs[4 * n + 1:8 * n + 2]
        mine, slots, send_sem, recv_sem = refs[8 * n + 2:]
        x, y, c, _ = _position()
        me = 4 * x + 2 * y + c

        def peer(k):
            return (1 - x if k & 4 else x, 1 - y if k & 2 else y, 1 - c if k & 1 else c)

        def logical(k):
            px, py, pc = peer(k)
            return 4 * px + 2 * py + pc

        mine[...] = jnp.zeros(mine.shape, F32)
        for t in range(n):
            mine[offs[t]:offs[t + 1], 0:grads[t].shape[1]] = g_refs[t][...]
        mine[offs[n]:offs[n] + 1, 0:LANES] = loss_ref[...]
        slots[me] = mine[...]
        sends = [pltpu.make_async_remote_copy(
            src_ref=mine, dst_ref=slots.at[me], send_sem=send_sem.at[k], recv_sem=recv_sem.at[k],
            device_id=peer(k), device_id_type=MESH) for k in range(1, n_dev)]
        for cp in sends:
            cp.start()
        for k in range(1, n_dev):
            pltpu.make_async_remote_copy(
                src_ref=mine, dst_ref=slots.at[logical(k)], send_sem=send_sem.at[k], recv_sem=recv_sem.at[k],
                device_id=peer(k), device_id_type=MESH).wait_recv()
        for cp in sends:
            cp.wait_send()
        total = slots[0]
        for d in range(1, n_dev):
            total = total + slots[d]
        for t in range(n):
            gv = total[offs[t]:offs[t + 1], 0:grads[t].shape[1]]
            outs[t][...] = gv
            outs[n + t][...], outs[2 * n + t][...], outs[3 * n + t][...] = _adamw_update(
                w_refs[t][...], gv, m_refs[t][...], v_refs[t][...])
        outs[4 * n][...] = total[offs[n]:offs[n] + 1, 0:LANES]

    vm = pl.BlockSpec(memory_space=pltpu.VMEM)
    shapes = [jax.ShapeDtypeStruct(g.shape, F32) for g in grads]
    res = pl.pallas_call(
        body, name=name, in_specs=[vm] * (4 * n + 1), out_specs=[vm] * (4 * n + 1),
        out_shape=shapes * 4 + [jax.ShapeDtypeStruct(loss.shape, F32)],
        scratch_shapes=[pltpu.VMEM((rows, width), F32), pltpu.VMEM((n_dev, rows, width), F32),
                        pltpu.SemaphoreType.DMA((n_dev,)), pltpu.SemaphoreType.DMA((n_dev,))],
    )(*grads, loss, *w, *m, *v)
    return [res[k * n:(k + 1) * n] for k in range(4)], res[4 * n]


def _adamw(name, w, g, m, v):
    rows, cols = w.shape
    tr = _row_tile(rows, sublanes=8)

    def body(w_ref, g_ref, m_ref, v_ref, d_ref, mo_ref, vo_ref):
        d_ref[...], mo_ref[...], vo_ref[...] = _adamw_update(w_ref[...], g_ref[...], m_ref[...], v_ref[...])

    spec = pl.BlockSpec((tr, cols), lambda i: (i, 0))
    shape = jax.ShapeDtypeStruct((rows, cols), F32)
    return pl.pallas_call(body, name=name, grid=(rows // tr,), in_specs=[spec] * 4, out_specs=[spec] * 3,
                          out_shape=[shape] * 3, compiler_params=_params("parallel"))(w, g, m, v)


def kernel(x, ev_w_in, ev_g_cq, ev_w_uq, ev_g_ckv, ev_w_ukv, ev_w_out, od_w_qkv, od_rel_bias, od_w_out, g_mix, g_ffn, w_gate, w_up, w_down, g_final, loss_target, m_ev_w_in, m_ev_g_cq, m_ev_w_uq, m_ev_g_ckv, m_ev_w_ukv, m_ev_w_out, m_od_w_qkv, m_od_rel_bias, m_od_w_out, m_g_mix, m_g_ffn, m_w_gate, m_w_up, m_w_down, m_g_final, v_ev_w_in, v_ev_g_cq, v_ev_w_uq, v_ev_g_ckv, v_ev_w_ukv, v_ev_w_out, v_od_w_qkv, v_od_rel_bias, v_od_w_out, v_g_mix, v_g_ffn, v_w_gate, v_w_up, v_w_down, v_g_final):
    w = dict(ev_w_in=ev_w_in, ev_g_cq=ev_g_cq, ev_w_uq=ev_w_uq, ev_g_ckv=ev_g_ckv, ev_w_ukv=ev_w_ukv, ev_w_out=ev_w_out,
             od_w_qkv=od_w_qkv, od_rel_bias=od_rel_bias, od_w_out=od_w_out, g_mix=g_mix, g_ffn=g_ffn, w_gate=w_gate,
             w_up=w_up, w_down=w_down, g_final=g_final)
    m = dict(ev_w_in=m_ev_w_in, ev_g_cq=m_ev_g_cq, ev_w_uq=m_ev_w_uq, ev_g_ckv=m_ev_g_ckv, ev_w_ukv=m_ev_w_ukv,
             ev_w_out=m_ev_w_out, od_w_qkv=m_od_w_qkv, od_rel_bias=m_od_rel_bias, od_w_out=m_od_w_out, g_mix=m_g_mix,
             g_ffn=m_g_ffn, w_gate=m_w_gate, w_up=m_w_up, w_down=m_w_down, g_final=m_g_final)
    v = dict(ev_w_in=v_ev_w_in, ev_g_cq=v_ev_g_cq, ev_w_uq=v_ev_w_uq, ev_g_ckv=v_ev_g_ckv, ev_w_ukv=v_ev_w_ukv,
             ev_w_out=v_ev_w_out, od_w_qkv=v_od_w_qkv, od_rel_bias=v_od_rel_bias, od_w_out=v_od_w_out, g_mix=v_g_mix,
             g_ffn=v_g_ffn, w_gate=v_w_gate, w_up=v_w_up, w_down=v_w_down, g_final=v_g_final)
    flat2d = lambda a: a.reshape(-1, a.shape[-1])
    for tree in (w, m, v):
        for n in TRANSPOSED:
            tree[n] = jnp.swapaxes(tree[n], 1, 2)

    pos = jnp.stack([2 * lax.axis_index("x") + lax.axis_index("y"), lax.axis_index("c")]).astype(jnp.int32)

    slots = {part: _cast_into_slot("cast_" + part, w[n], layer, pos) for part, n, layer in GRAD_PARTS
             if part in FIRST_WEIGHTS + NEXT_WEIGHTS}
    rest = [(part, n, layer) for part, n, layer in GRAD_PARTS if part not in slots]

    def cast_rest(carry):
        return dict(zip([part for part, _, _ in rest],
                        _cast_many_into_slots("cast_rest", [(w[n], layer) for _, n, layer in rest], pos, carry)))

    ex = _Exchanges(slots, pos, {n: w[n].shape for n in BIG}, cast_rest)

    loss_local, grad_x, small = _local_step(x[0], loss_target[0], {n: w[n] for n in SMALL}, ex)

    grads = ex.finish()
    delta, new_m, new_v = {}, {}, {}
    small_out, loss = _small_step("small_step", [flat2d(small[n]) for n in SMALL], loss_local,
                                  *([flat2d(t[n]) for n in SMALL] for t in (w, m, v)))
    for tree, outs in zip((grads, delta, new_m, new_v), small_out):
        tree.update({n: o.reshape(w[n].shape) for n, o in zip(SMALL, outs)})

    for n in BIG:
        turn = (lambda a: jnp.swapaxes(a, 1, 2)) if n in ADAMW_TRANSPOSED else (lambda a: a)
        shape = turn(w[n]).shape
        outs = _adamw("adamw_" + n, *(flat2d(turn(a)) for a in (w[n], grads[n], m[n], v[n])))
        delta[n], new_m[n], new_v[n] = (turn(o.reshape(shape)) for o in outs)
    for tree in (grads, delta, new_m, new_v):
        for n in TRANSPOSED:
            tree[n] = jnp.swapaxes(tree[n], 1, 2)

    return (loss[0, 0], grad_x[None], *[grads[n] for n in WEIGHTS], *[delta[n] for n in WEIGHTS],
            *[new_m[n] for n in WEIGHTS], *[new_v[n] for n in WEIGHTS])
```

```python
import functools

import jax
import jax.numpy as jnp
import numpy as np
from jax import lax
from jax.experimental import pallas as pl
from jax.experimental.pallas import tpu as pltpu

F32 = jnp.float32
BF16 = jnp.bfloat16

S = 2048
D = 1024
CHUNK = 64
MLA_H, MLA_NOPE, MLA_ROPE, MLA_V = 8, 64, 32, 64
Q_LORA, KV_LORA = 384, 256
ROPE_THETA = 10000.0
SB_H, SB_DIM = 8, 64
C_H, C_DIM = 16, 64
LEFT_CHUNKS = 8
REL_CLIP = 256
D_FF = 2816
EVEN_IN = 2208
RMS_EPS = 1e-6
ADAM_LR, ADAM_B1, ADAM_B2, ADAM_EPS, ADAM_WD, ADAM_STEP = 0.001, 0.9, 0.999, 1e-08, 0.01, 10

N_CHIPS = 4
FF_SHARD = D_FF // N_CHIPS
SCALE_A = (MLA_NOPE + MLA_ROPE) ** -0.5
SCALE_B = SB_DIM ** -0.5
SCALE_C = C_DIM ** -0.5
NEG = -1e30
LOG2_E = 1.4426950408889634

LANES = 128
MXU_W = 256
VMEM_LIMIT_BYTES = 56 * 1024 * 1024
TM = 512
TQ = 1024
QB = 512
BQ = 256

P_CQ, P_CKV, P_QB, P_KB, P_VB, P_KR = 0, 512, 768, 1280, 1792, 2304
P_IN = 2432
KR_LANE = 64
BAND_W = BQ + LEFT_CHUNKS * CHUNK
BAND_PAD = 512
TOEP_W = 1024


def _params(*sem):
    return pltpu.CompilerParams(dimension_semantics=sem, vmem_limit_bytes=VMEM_LIMIT_BYTES)


MESH = pl.DeviceIdType.MESH
ANY = pl.BlockSpec(memory_space=pl.ANY)


def _position():
    x, y, c = lax.axis_index("x"), lax.axis_index("y"), lax.axis_index("c")
    other_chips = [(1 - x, y), (x, 1 - y), (1 - x, 1 - y)]
    return x, y, c, other_chips


def _half_rows(c, half):
    return pl.ds(pl.multiple_of(c * half, 16), half)


def _remote(ref_src, ref_dst, send, recv, k, device):
    return pltpu.make_async_remote_copy(src_ref=ref_src, dst_ref=ref_dst, send_sem=send.at[k], recv_sem=recv.at[k],
                                        device_id=device, device_id_type=MESH)


class _Carry:
    def __init__(self):
        self.operands, self.aliased, self.fresh = [], [], []
        self.n_sems = 0
        self.starts, self.finishes, self.on_done = [], [], []

    def operand(self, arr, aliased):
        for i, a in enumerate(self.operands):
            if a is arr:
                return i
        self.operands.append(arr)
        self.aliased.append(aliased)
        return len(self.operands) - 1

    def result(self, shape, dtype):
        self.fresh.append(jax.ShapeDtypeStruct(shape, dtype))
        return len(self.fresh) - 1

    def sems(self, k):
        base = self.n_sems
        self.n_sems += k
        return base

    def done(self, results):
        aliased, fresh = results
        for f in self.on_done:
            f(aliased, fresh)


def _carrier_call(body, *, name, grid, in_specs, out_specs, out_shape, args, sem, scratch_shapes=(), carry=None,
                  prefetch=()):
    in_specs, out_specs, out_shape, scratch = list(in_specs), list(out_specs), list(out_shape), list(scratch_shapes)
    n_pre = len(prefetch)

    def call(kernel, in_specs, out_specs, out_shape, scratch, aliases, sem):
        return pl.pallas_call(
            kernel, name=name, out_shape=out_shape, input_output_aliases=aliases, compiler_params=_params(*sem),
            grid_spec=pltpu.PrefetchScalarGridSpec(num_scalar_prefetch=n_pre, grid=grid, in_specs=in_specs,
                                                   out_specs=out_specs, scratch_shapes=scratch))

    if carry is None:
        return list(call(body, in_specs, out_specs, out_shape, scratch, {}, sem)(*prefetch, *args)), None
    ops = carry.operands
    alias_idx = [i for i, a in enumerate(carry.aliased) if a]
    c_shapes = [jax.ShapeDtypeStruct(ops[i].shape, ops[i].dtype) for i in alias_idx] + carry.fresh
    n_in, n_out, n_scr = len(args), len(out_shape), len(scratch)

    def wrapped(*refs):
        pre, refs = refs[:n_pre], refs[n_pre:]
        ins, c_ins = refs[:n_in], refs[n_in:n_in + len(ops)]
        o0 = n_in + len(ops)
        outs, c_outs = refs[o0:o0 + n_out], refs[o0 + n_out:o0 + n_out + len(c_shapes)]
        s0 = o0 + n_out + len(c_shapes)
        scr, send, recv = refs[s0:s0 + n_scr], refs[s0 + n_scr], refs[s0 + n_scr + 1]
        use = list(c_ins)
        for k, i in enumerate(alias_idx):
            use[i] = c_outs[k]
        fresh = c_outs[len(alias_idx):]

        def run(steps):
            for step in steps:
                step(use, fresh, send, recv)

        if not grid:
            run(carry.starts)
            if body is not None:
                body(*pre, *ins, *outs, *scr)
            run(carry.finishes)
            return
        ids = [pl.program_id(a) for a in range(len(grid))]
        first = functools.reduce(jnp.logical_and, [i == 0 for i in ids])
        last = functools.reduce(jnp.logical_and, [i == g - 1 for i, g in zip(ids, grid)])

        @pl.when(first)
        def _():
            run(carry.starts)

        body(*pre, *ins, *outs, *scr)

        @pl.when(last)
        def _():
            run(carry.finishes)

    res = call(wrapped, in_specs + [ANY] * len(ops), out_specs + [ANY] * len(c_shapes), out_shape + c_shapes,
               scratch + [pltpu.SemaphoreType.DMA((carry.n_sems,)), pltpu.SemaphoreType.DMA((carry.n_sems,))],
               {n_pre + n_in + i: n_out + k for k, i in enumerate(alias_idx)},
               ("arbitrary",) * len(grid))(*prefetch, *args, *ops)
    res = list(res)
    c_res = res[n_out:]
    return res[:n_out], ({i: c_res[k] for k, i in enumerate(alias_idx)}, c_res[len(alias_idx):])


_DIMS = {"nn": (((1,), (0,)), ((), ())), "nt": (((1,), (1,)), ((), ())), "tn": (((0,), (0,)), ((), ()))}


def _dot(a, b, kind="nn"):
    return lax.dot_general(a, b, _DIMS[kind], preferred_element_type=F32)


def _iota(shape, dim):
    return lax.broadcasted_iota(jnp.int32, shape, dim)


def _sigmoid(x):
    return 1.0 / (1.0 + jnp.exp(-x))


def _split_dot(x, tri):
    hi = x.astype(BF16)
    lo = (x - hi.astype(F32)).astype(BF16)
    both = _dot(jnp.concatenate([hi, lo], axis=0), tri)
    return both[:x.shape[0]] + both[x.shape[0]:]


def _running_sum(x, tri, reverse):
    n = x.shape[1] // MXU_W
    blocks = [x[:, b * MXU_W:(b + 1) * MXU_W] for b in range(n)]
    out = [None] * n
    carry = None
    for b in (range(n - 1, -1, -1) if reverse else range(n)):
        part = _split_dot(blocks[b], tri)
        out[b] = part if carry is None else part + carry
        total = jnp.sum(blocks[b], axis=-1, keepdims=True)
        carry = total if carry is None else carry + total
    return (jnp.concatenate(out, axis=1) if n > 1 else out[0]), carry


def _mm(name, a, b, *, kind, grid, a_spec, b_spec, o_spec, out_shape, out_dtype, acc_shape, resid=None, r_spec=None,
        carry=None):
    nk = grid[-1]
    has_r = resid is not None
    several = lambda x: list(x) if isinstance(x, (tuple, list)) else [x]
    a_specs, b_specs = several(a_spec), several(b_spec)
    na, nb = len(a_specs), len(b_specs)
    a_args = several(a) if isinstance(a, (tuple, list)) else [a] * na
    b_args = several(b) if isinstance(b, (tuple, list)) else [b] * nb

    def body(*refs):
        r_ref = refs[na + nb] if has_r else None
        o_ref = refs[na + nb + has_r]
        side_by_side = lambda rs: rs[0][...] if len(rs) == 1 else jnp.concatenate([r[...].astype(BF16) for r in rs], axis=1)
        part = _dot(side_by_side(refs[:na]).astype(BF16), side_by_side(refs[na:na + nb]).astype(BF16), kind)

        def finish(total):
            if has_r:
                total = total + r_ref[...].astype(F32)
            o_ref[...] = total.astype(out_dtype)

        if nk == 1:
            finish(part)
        else:
            acc_ref = refs[na + nb + has_r + 1]
            k = pl.program_id(len(grid) - 1)

            @pl.when(k == 0)
            def _():
                acc_ref[...] = part

            @pl.when(k > 0)
            def _():
                acc_ref[...] += part

            @pl.when(k == nk - 1)
            def _():
                finish(acc_ref[...])

    in_specs = a_specs + b_specs + ([r_spec] if has_r else [])
    args = (*a_args, *b_args) + ((resid,) if has_r else ())
    sem = ("parallel",) * (len(grid) - 1) + ("arbitrary",)
    res, copies = _carrier_call(
        body, name=name, grid=grid, in_specs=in_specs, out_specs=[o_spec],
        out_shape=[jax.ShapeDtypeStruct(out_shape, out_dtype)],
        scratch_shapes=[pltpu.VMEM(acc_shape, F32)] if nk > 1 else [], args=args, sem=sem, carry=carry)
    if carry is not None:
        carry.done(copies)
    return res[0]


def _rms_fwd(name, x, g, col_block=0):
    c = g.shape[1]

    def body(x_ref, g_ref, u_ref):
        xv = x_ref[...]
        r = lax.rsqrt(jnp.mean(xv * xv, axis=-1, keepdims=True) + RMS_EPS)
        u_ref[...] = (xv * r * g_ref[...]).astype(BF16)

    return pl.pallas_call(
        body, name=name, grid=(S // TM,),
        in_specs=[pl.BlockSpec((TM, c), lambda i: (i, col_block)), pl.BlockSpec((1, c), lambda i: (0, 0))],
        out_specs=pl.BlockSpec((TM, c), lambda i: (i, 0)),
        out_shape=jax.ShapeDtypeStruct((S, c), BF16),
        compiler_params=_params("parallel"),
    )(x, g)


def _rms_bwd(name, dy, x, g, resid, carry=None):
    def body(dy_ref, x_ref, g_ref, r_ref, dx_ref, dg_ref):
        i = pl.program_id(0)
        xv = x_ref[...]
        r = lax.rsqrt(jnp.mean(xv * xv, axis=-1, keepdims=True) + RMS_EPS)
        xh = xv * r
        dyv = dy_ref[...]
        dxh = dyv * g_ref[...]
        dx_ref[...] = r_ref[...] + r * (dxh - xh * jnp.mean(dxh * xh, axis=-1, keepdims=True))
        part = jnp.sum(dyv * xh, axis=0, keepdims=True)

        @pl.when(i == 0)
        def _():
            dg_ref[...] = part

        @pl.when(i > 0)
        def _():
            dg_ref[...] += part

    row = pl.BlockSpec((TM, D), lambda i: (i, 0))
    vec = pl.BlockSpec((1, D), lambda i: (0, 0))
    res, copies = _carrier_call(
        body, name=name, grid=(S // TM,), in_specs=[row, row, vec, row], out_specs=[row, vec],
        out_shape=[jax.ShapeDtypeStruct((S, D), F32), jax.ShapeDtypeStruct((1, D), F32)],
        args=(dy, x, g, resid), sem=("arbitrary",), carry=carry)
    if carry is not None:
        carry.done(copies)
    return res


def _loss_bwd(name, h, g, tgt):
    def body(h_ref, g_ref, t_ref, loss_ref, dh_ref, dg_ref):
        i = pl.program_id(0)
        xv = h_ref[...]
        gv = g_ref[...]
        r = lax.rsqrt(jnp.mean(xv * xv, axis=-1, keepdims=True) + RMS_EPS)
        xh = xv * r
        diff = xh * gv - t_ref[...]
        part_loss = 0.5 * jnp.sum(jnp.sum(diff * diff, axis=-1, keepdims=True) * (1.0 / D), axis=0, keepdims=True)
        dy = diff * (1.0 / D)
        dxh = dy * gv
        dh_ref[...] = r * (dxh - xh * jnp.mean(dxh * xh, axis=-1, keepdims=True))
        part_g = jnp.sum(dy * xh, axis=0, keepdims=True)

        @pl.when(i == 0)
        def _():
            dg_ref[...] = part_g
            loss_ref[...] = jnp.broadcast_to(part_loss, (1, LANES))

        @pl.when(i > 0)
        def _():
            dg_ref[...] += part_g
            loss_ref[...] += jnp.broadcast_to(part_loss, (1, LANES))

    row = pl.BlockSpec((TM, D), lambda i: (i, 0))
    vec = pl.BlockSpec((1, D), lambda i: (0, 0))
    return pl.pallas_call(
        body, name=name, grid=(S // TM,), in_specs=[row, vec, row],
        out_specs=[pl.BlockSpec((1, LANES), lambda i: (0, 0)), row, vec],
        out_shape=[jax.ShapeDtypeStruct((1, LANES), F32), jax.ShapeDtypeStruct((S, D), F32),
                   jax.ShapeDtypeStruct((1, D), F32)],
        compiler_params=_params("arbitrary"),
    )(h, g, tgt)


def _ffn_fwd(name, h, g, wg, wu, wd, carry=None):
    def body(h_ref, g_ref, wg_ref, wu_ref, wd_ref, o_ref, gate_ref, up_ref, u_scr):
        s = pl.program_id(1)

        @pl.when(s == 0)
        def _():
            xv = h_ref[...]
            r = lax.rsqrt(jnp.mean(xv * xv, axis=-1, keepdims=True) + RMS_EPS)
            u_scr[...] = (xv * r * g_ref[...]).astype(BF16)
            o_ref[...] = xv

        u = u_scr[...]
        gate = _dot(u, wg_ref[...], "nt")
        up = _dot(u, wu_ref[...], "nt")
        act = gate * _sigmoid(gate) * up
        o_ref[...] += _dot(act.astype(BF16), wd_ref[...])
        gate_ref[...] = gate.astype(BF16)
        up_ref[...] = up.astype(BF16)

    row = pl.BlockSpec((TM, D), lambda i, s: (i, 0))
    hid = pl.BlockSpec((None, TM, FF_SHARD), lambda i, s: (s, i, 0))
    return _carrier_call(
        body, name=name, grid=(S // TM, N_CHIPS),
        in_specs=[row, pl.BlockSpec((1, D), lambda i, s: (0, 0))]
        + [pl.BlockSpec((None, FF_SHARD, D), lambda i, s: (s, 0, 0))] * 3,
        out_specs=[row, hid, hid],
        out_shape=[jax.ShapeDtypeStruct((S, D), F32), jax.ShapeDtypeStruct((N_CHIPS, S, FF_SHARD), BF16),
                   jax.ShapeDtypeStruct((N_CHIPS, S, FF_SHARD), BF16)],
        scratch_shapes=[pltpu.VMEM((TM, D), BF16)], args=(h, g, wg, wu, wd), sem=("parallel", "arbitrary"), carry=carry)


def _ffn_bwd(name, dh, h, g, gate, up, wg, wu, wd):
    def body(dh_ref, h_ref, g_ref, gate_ref, up_ref, wg_ref, wu_ref, wd_ref,
             dhin_ref, dg_ref, u_ref, dgate_ref, dup_ref, act_ref, dhb_scr, du_scr):
        i = pl.program_id(0)
        s = pl.program_id(1)

        @pl.when(s == 0)
        def _():
            xv = h_ref[...]
            r = lax.rsqrt(jnp.mean(xv * xv, axis=-1, keepdims=True) + RMS_EPS)
            u_ref[...] = (xv * r * g_ref[...]).astype(BF16)
            dhb_scr[...] = dh_ref[...].astype(BF16)
            du_scr[...] = jnp.zeros_like(du_scr)

        dact = _dot(dhb_scr[...], wd_ref[...], "nt")
        gv = gate_ref[...].astype(F32)
        uv = up_ref[...].astype(F32)
        sig = _sigmoid(gv)
        sil = gv * sig
        dup = dact * sil
        dgate = dact * uv * (sig * (1.0 + gv * (1.0 - sig)))
        dgb = dgate.astype(BF16)
        dub = dup.astype(BF16)
        act_ref[...] = (sil * uv).astype(BF16)
        dgate_ref[...] = dgb
        dup_ref[...] = dub
        du_scr[...] += _dot(dgb, wg_ref[...]) + _dot(dub, wu_ref[...])

        @pl.when(s == N_CHIPS - 1)
        def _():
            xv = h_ref[...]
            r = lax.rsqrt(jnp.mean(xv * xv, axis=-1, keepdims=True) + RMS_EPS)
            xh = xv * r
            du = du_scr[...]
            dxh = du * g_ref[...]
            dhin_ref[...] = dh_ref[...] + r * (dxh - xh * jnp.mean(dxh * xh, axis=-1, keepdims=True))
            part = jnp.sum(du * xh, axis=0, keepdims=True)

            @pl.when(i == 0)
            def _():
                dg_ref[...] = part

            @pl.when(i > 0)
            def _():
                dg_ref[...] += part

    row = pl.BlockSpec((TM, D), lambda i, s: (i, 0))
    vec = pl.BlockSpec((1, D), lambda i, s: (0, 0))
    hid = pl.BlockSpec((None, TM, FF_SHARD), lambda i, s: (s, i, 0))
    hid_shape = jax.ShapeDtypeStruct((N_CHIPS, S, FF_SHARD), BF16)
    return pl.pallas_call(
        body, name=name, grid=(S // TM, N_CHIPS),
        in_specs=[row, row, vec, hid, hid] + [pl.BlockSpec((None, FF_SHARD, D), lambda i, s: (s, 0, 0))] * 3,
        out_specs=[row, vec, row, hid, hid, hid],
        out_shape=[jax.ShapeDtypeStruct((S, D), F32), jax.ShapeDtypeStruct((1, D), F32),
                   jax.ShapeDtypeStruct((S, D), BF16), hid_shape, hid_shape, hid_shape],
        scratch_shapes=[pltpu.VMEM((TM, D), BF16), pltpu.VMEM((TM, D), F32)],
        compiler_params=_params("arbitrary", "arbitrary"),
    )(dh, h, g, gate, up, wg, wu, wd)


def _ffn_wgrads(name, u, dgate, dup, act, dh):
    nk = S // TM

    def body(u_ref, dh_ref, dgate_ref, dup_ref, act_ref, dg_ref, du_ref, dd_ref, acc_g, acc_u, acc_d):
        k = pl.program_id(1)
        u = u_ref[...]
        parts = (_dot(dgate_ref[...], u, "tn"), _dot(dup_ref[...], u, "tn"),
                 _dot(act_ref[...], dh_ref[...].astype(BF16), "tn"))
        accs = (acc_g, acc_u, acc_d)

        @pl.when(k == 0)
        def _():
            for acc, part in zip(accs, parts):
                acc[...] = part

        @pl.when(k > 0)
        def _():
            for acc, part in zip(accs, parts):
                acc[...] += part

        @pl.when(k == nk - 1)
        def _():
            for out, acc in zip((dg_ref, du_ref, dd_ref), accs):
                out[...] = acc[...].astype(BF16)

    tok = pl.BlockSpec((TM, D), lambda s, k: (k, 0))
    hid = pl.BlockSpec((None, TM, FF_SHARD), lambda s, k: (s, k, 0))
    out = pl.BlockSpec((None, FF_SHARD, D), lambda s, k: (s, 0, 0))
    shape = jax.ShapeDtypeStruct((N_CHIPS, FF_SHARD, D), BF16)
    return pl.pallas_call(
        body, name=name, grid=(N_CHIPS, nk), in_specs=[tok, tok, hid, hid, hid], out_specs=[out, out, out],
        out_shape=[shape, shape, shape], scratch_shapes=[pltpu.VMEM((FF_SHARD, D), F32)] * 3,
        compiler_params=_params("parallel", "arbitrary"))(u, dh, dgate, dup, act)


def _rope_tables():
    pos = jnp.arange(S, dtype=F32)
    inv = ROPE_THETA ** (-jnp.arange(0, MLA_ROPE, 2, dtype=F32) / MLA_ROPE)
    ang = pos[:, None] * inv[None, :]
    half = MLA_ROPE // 2
    cos = jnp.cos(ang)
    sin = jnp.sin(ang)
    one = jnp.ones((S, KR_LANE), F32)
    zero = jnp.zeros((S, KR_LANE), F32)
    tail_one = jnp.ones((S, LANES - KR_LANE - MLA_ROPE), F32)
    tail_zero = jnp.zeros((S, LANES - KR_LANE - MLA_ROPE), F32)
    cos_t = jnp.concatenate([one, cos, cos, tail_one], axis=1)
    sin_t = jnp.concatenate([zero, -sin, sin, tail_zero], axis=1)
    assert cos_t.shape == (S, LANES) and half * 2 == MLA_ROPE
    return cos_t, sin_t


def _rope(x, cos_t, sin_t, sign):
    n = x.shape[1] // LANES
    half = MLA_ROPE // 2
    lane = _iota(x.shape, 1) & (LANES - 1)
    first = (lane >= KR_LANE) & (lane < KR_LANE + half)
    swapped = jnp.where(first, pltpu.roll(x, x.shape[1] - half, 1), pltpu.roll(x, half, 1))
    c = jnp.tile(cos_t, (1, n)) if n > 1 else cos_t
    s = jnp.tile(sin_t, (1, n)) if n > 1 else sin_t
    return x * c + swapped * (s * sign)


def _mla_prep_fwd(name, proj, g_cq, g_ckv, w_uq, w_uk, w_uv, cos_t, sin_t):
    nh = MLA_H * LANES

    def body(cq_ref, ckv_ref, kr_ref, gq_ref, gkv_ref, wq_ref, wk_ref, wv_ref, cos_ref, sin_ref,
             qa_ref, ka_ref, va_ref):
        cos_v, sin_v = cos_ref[...], sin_ref[...]
        cq = cq_ref[...]
        r = lax.rsqrt(jnp.mean(cq * cq, axis=-1, keepdims=True) + RMS_EPS)
        cqn = (cq * r * gq_ref[...]).astype(BF16)
        qa_ref[...] = _rope(_dot(cqn, wq_ref[...]), cos_v, sin_v, 1.0).astype(BF16)
        ckv = ckv_ref[...]
        r = lax.rsqrt(jnp.mean(ckv * ckv, axis=-1, keepdims=True) + RMS_EPS)
        ckvn = (ckv * r * gkv_ref[...]).astype(BF16)
        lane = _iota((TM, LANES), 1)
        rot = (lane >= KR_LANE) & (lane < KR_LANE + MLA_ROPE)
        kr = jnp.where(rot, _rope(kr_ref[...], cos_v, sin_v, 1.0), 0.0)
        ka_ref[...] = (_dot(ckvn, wk_ref[...]) + jnp.tile(kr, (1, MLA_H))).astype(BF16)
        va_ref[...] = _dot(ckvn, wv_ref[...]).astype(BF16)

    full = lambda shape: pl.BlockSpec(shape, lambda i: (0, 0))
    return pl.pallas_call(
        body, name=name, grid=(S // TM,),
        in_specs=[pl.BlockSpec((TM, Q_LORA), lambda i: (i, P_CQ // Q_LORA)),
                  pl.BlockSpec((TM, KV_LORA), lambda i: (i, P_CKV // KV_LORA)),
                  pl.BlockSpec((TM, LANES), lambda i: (i, P_KR // LANES)),
                  full((1, Q_LORA)), full((1, KV_LORA)), full((Q_LORA, nh)), full((KV_LORA, nh)),
                  full((KV_LORA, MLA_H * MLA_V)),
                  pl.BlockSpec((TM, LANES), lambda i: (i, 0)), pl.BlockSpec((TM, LANES), lambda i: (i, 0))],
        out_specs=[pl.BlockSpec((TM, nh), lambda i: (i, 0)), pl.BlockSpec((TM, nh), lambda i: (i, 0)),
                   pl.BlockSpec((TM, MLA_H * MLA_V), lambda i: (i, 0))],
        out_shape=[jax.ShapeDtypeStruct((S, nh), BF16), jax.ShapeDtypeStruct((S, nh), BF16),
                   jax.ShapeDtypeStruct((S, MLA_H * MLA_V), BF16)],
        compiler_params=_params("parallel"),
    )(proj, proj, proj, g_cq, g_ckv, w_uq, w_uk, w_uv, cos_t, sin_t)


def _mla_prep_bwd(name, dqa, dka, dva, proj, g_cq, g_ckv, w_uq, w_uk, w_uv, cos_t, sin_t):
    nh = MLA_H * LANES

    def body(dqa_ref, dka_ref, dva_ref, cq_ref, ckv_ref, gq_ref, gkv_ref, wq_ref, wk_ref, wv_ref, cos_ref, sin_ref,
             dcq_ref, dckv_ref, dkr_ref, dwq_ref, dwk_ref, dwv_ref, dgq_ref, dgkv_ref):
        i = pl.program_id(0)
        cos_v, sin_v = cos_ref[...], sin_ref[...]

        def norm_bwd(x, g, dn):
            r = lax.rsqrt(jnp.mean(x * x, axis=-1, keepdims=True) + RMS_EPS)
            xh = x * r
            dxh = dn * g
            dx = r * (dxh - xh * jnp.mean(dxh * xh, axis=-1, keepdims=True))
            return dx, jnp.sum(dn * xh, axis=0, keepdims=True), (xh * g).astype(BF16)

        dq = _rope(dqa_ref[...], cos_v, sin_v, -1.0).astype(BF16)
        dcqn = _dot(dq, wq_ref[...], "nt")
        dcq, dgq, cqn = norm_bwd(cq_ref[...], gq_ref[...], dcqn)
        dcq_ref[...] = dcq.astype(BF16)
        dwq = _dot(cqn, dq, "tn")

        dka = dka_ref[...]
        dkab = dka.astype(BF16)
        dvab = dva_ref[...].astype(BF16)
        dckvn = _dot(dkab, wk_ref[...], "nt") + _dot(dvab, wv_ref[...], "nt")
        dckv, dgkv, ckvn = norm_bwd(ckv_ref[...], gkv_ref[...], dckvn)
        dckv_ref[...] = dckv.astype(BF16)
        dwk = _dot(ckvn, dkab, "tn")
        dwv = _dot(ckvn, dvab, "tn")

        fold = dka[:, 0:LANES]
        for hh in range(1, MLA_H):
            fold = fold + dka[:, hh * LANES:(hh + 1) * LANES]
        lane = _iota((TM, LANES), 1)
        rot = (lane >= KR_LANE) & (lane < KR_LANE + MLA_ROPE)
        dkr = _rope(jnp.where(rot, fold, 0.0), cos_v, sin_v, -1.0)
        dkr_ref[...] = jnp.where(rot, dkr, 0.0).astype(BF16)

        @pl.when(i == 0)
        def _():
            dwq_ref[...] = dwq
            dwk_ref[...] = dwk
            dwv_ref[...] = dwv
            dgq_ref[...] = dgq
            dgkv_ref[...] = dgkv

        @pl.when(i > 0)
        def _():
            dwq_ref[...] += dwq
            dwk_ref[...] += dwk
            dwv_ref[...] += dwv
            dgq_ref[...] += dgq
            dgkv_ref[...] += dgkv

    full = lambda shape: pl.BlockSpec(shape, lambda i: (0, 0))
    rows = lambda c: pl.BlockSpec((TM, c), lambda i: (i, 0))
    nv = MLA_H * MLA_V
    return pl.pallas_call(
        body, name=name, grid=(S // TM,),
        in_specs=[rows(nh), rows(nh), rows(nv),
                  pl.BlockSpec((TM, Q_LORA), lambda i: (i, P_CQ // Q_LORA)),
                  pl.BlockSpec((TM, KV_LORA), lambda i: (i, P_CKV // KV_LORA)),
                  full((1, Q_LORA)), full((1, KV_LORA)), full((Q_LORA, nh)), full((KV_LORA, nh)), full((KV_LORA, nv)),
                  rows(LANES), rows(LANES)],
        out_specs=[rows(Q_LORA), rows(KV_LORA), rows(LANES), full((Q_LORA, nh)), full((KV_LORA, nh)),
                   full((KV_LORA, nv)), full((1, Q_LORA)), full((1, KV_LORA))],
        out_shape=[jax.ShapeDtypeStruct((S, Q_LORA), BF16), jax.ShapeDtypeStruct((S, KV_LORA), BF16),
                   jax.ShapeDtypeStruct((S, LANES), BF16), jax.ShapeDtypeStruct((Q_LORA, nh), F32),
                   jax.ShapeDtypeStruct((KV_LORA, nh), F32), jax.ShapeDtypeStruct((KV_LORA, nv), F32),
                   jax.ShapeDtypeStruct((1, Q_LORA), F32), jax.ShapeDtypeStruct((1, KV_LORA), F32)],
        compiler_params=_params("arbitrary"),
    )(dqa, dka, dva, proj, proj, g_cq, g_ckv, w_uq, w_uk, w_uv, cos_t, sin_t)


def _head_masks(dtype):
    lane = _iota((1, LANES), 1)
    return (lane < 64).astype(dtype), (lane >= 64).astype(dtype)


def _mla_fwd(name, qa, ka, va, carry=None):
    def body(q_ref, k_ref, v_ref, o_ref, lse_ref):
        m0b, m1b = _head_masks(BF16)
        lane = _iota((QB, LANES), 1)
        left = lane < 64

        def qblock(i, _):
            r0 = pl.multiple_of(i * QB, QB)
            qs = [q_ref[pl.ds(r0, QB), hh * LANES:(hh + 1) * LANES] for hh in range(2)]
            rowc = lax.shift_right_logical(r0 + _iota((QB, QB), 0), 6)

            def kv(kb, carry):
                ms, ls, acc = carry
                c0 = pl.multiple_of(kb * QB, QB)
                v = v_ref[pl.ds(c0, QB), :]
                ok = lax.shift_right_logical(c0 + _iota((QB, QB), 1), 6) <= rowc
                new_m, new_l, alphas = [], [], []
                pv = None
                for hh in range(2):
                    k = k_ref[pl.ds(c0, QB), hh * LANES:(hh + 1) * LANES]
                    s = jnp.where(ok, _dot(qs[hh], k, "nt") * (SCALE_A * LOG2_E), NEG)
                    mn = jnp.maximum(ms[hh], jnp.max(s, axis=-1, keepdims=True))
                    p = jnp.exp2(s - mn)
                    a = jnp.exp2(ms[hh] - mn)
                    new_m.append(mn)
                    new_l.append(a * ls[hh] + jnp.sum(p, axis=-1, keepdims=True))
                    alphas.append(a)
                    part = _dot(p.astype(BF16), v * (m0b if hh == 0 else m1b))
                    pv = part if pv is None else pv + part
                acc = acc * jnp.where(left, alphas[0], alphas[1]) + pv
                return tuple(new_m), tuple(new_l), acc

            init = ((jnp.full((QB, 1), NEG, F32),) * 2, (jnp.zeros((QB, 1), F32),) * 2, jnp.zeros((QB, LANES), F32))
            ms, ls, acc = lax.fori_loop(0, i + 1, kv, init)
            o_ref[pl.ds(r0, QB), :] = acc * jnp.where(left, 1.0 / ls[0], 1.0 / ls[1])
            lse_ref[pl.ds(r0, QB), :] = jnp.where(left, ms[0] + jnp.log(ls[0]) * LOG2_E, ms[1] + jnp.log(ls[1]) * LOG2_E)
            return 0

        lax.fori_loop(0, S // QB, qblock, 0)

    pair = lambda w: pl.BlockSpec((S, w), lambda p: (0, p))
    return _carrier_call(
        body, name=name, grid=(MLA_H // 2,), in_specs=[pair(2 * LANES), pair(2 * LANES), pair(LANES)],
        out_specs=[pair(LANES), pair(LANES)],
        out_shape=[jax.ShapeDtypeStruct((S, MLA_H * MLA_V), F32), jax.ShapeDtypeStruct((S, MLA_H * MLA_V), F32)],
        args=(qa, ka, va), sem=("parallel",), carry=carry)


def _mla_bwd(name, qa, ka, va, o, lse, do, do_block0, carry=None):
    def body(q_ref, k_ref, v_ref, o_ref, lse_ref, do_ref, dq_ref, dk_ref, dv_ref):
        m0f, m1f = _head_masks(F32)
        m0b, m1b = _head_masks(BF16)
        dk_ref[...] = jnp.zeros_like(dk_ref)
        dv_ref[...] = jnp.zeros_like(dv_ref)

        def qblock(i, _):
            r0 = pl.multiple_of(i * QB, QB)
            rows = pl.ds(r0, QB)
            do_f = do_ref[rows, :]
            prod = do_f * o_ref[rows, :]
            deltas = [jnp.sum(prod * m0f, axis=-1, keepdims=True), jnp.sum(prod * m1f, axis=-1, keepdims=True)]
            lse_v = lse_ref[rows, :]
            lses = [lse_v[:, 0:1], lse_v[:, 64:65]]
            dob = do_f.astype(BF16)
            dos = [dob * m0b, dob * m1b]
            qs = [q_ref[rows, hh * LANES:(hh + 1) * LANES] for hh in range(2)]
            rowc = lax.shift_right_logical(r0 + _iota((QB, QB), 0), 6)

            def kv(kb, dqs):
                c0 = pl.multiple_of(kb * QB, QB)
                cols = pl.ds(c0, QB)
                v = v_ref[cols, :]
                ok = lax.shift_right_logical(c0 + _iota((QB, QB), 1), 6) <= rowc
                out = []
                dv = None
                for hh in range(2):
                    k = k_ref[cols, hh * LANES:(hh + 1) * LANES]
                    s = _dot(qs[hh], k, "nt") * (SCALE_A * LOG2_E)
                    p = jnp.where(ok, jnp.exp2(s - lses[hh]), 0.0)
                    dp = _dot(dos[hh], v, "nt")
                    ds = (p * (dp - deltas[hh]) * SCALE_A).astype(BF16)
                    out.append(dqs[hh] + _dot(ds, k))
                    dk_ref[cols, hh * LANES:(hh + 1) * LANES] += _dot(ds, qs[hh], "tn")
                    part = _dot(p.astype(BF16), dos[hh], "tn")
                    dv = part if dv is None else dv + part
                dv_ref[cols, :] += dv
                return tuple(out)

            dqs = lax.fori_loop(0, i + 1, kv, (jnp.zeros((QB, LANES), F32),) * 2)
            for hh in range(2):
                dq_ref[rows, hh * LANES:(hh + 1) * LANES] = dqs[hh]
            return 0

        lax.fori_loop(0, S // QB, qblock, 0)

    pair = lambda w: pl.BlockSpec((S, w), lambda p: (0, p))
    return _carrier_call(
        body, name=name, grid=(MLA_H // 2,),
        in_specs=[pair(2 * LANES), pair(2 * LANES), pair(LANES), pair(LANES), pair(LANES),
                  pl.BlockSpec((S, LANES), lambda p: (0, do_block0 + p))],
        out_specs=[pair(2 * LANES), pair(2 * LANES), pair(LANES)],
        out_shape=[jax.ShapeDtypeStruct((S, MLA_H * LANES), F32), jax.ShapeDtypeStruct((S, MLA_H * LANES), F32),
                   jax.ShapeDtypeStruct((S, MLA_H * MLA_V), F32)],
        args=(qa, ka, va, o, lse, do), sem=("parallel",), carry=carry)


def _sb_weights(q_h, k, c, before, tri_suffix):
    z = _dot(q_h, k, "nt") * (SCALE_B * LOG2_E)
    sp = jnp.maximum(z, 0.0) + jnp.log(1.0 + jnp.exp2(-jnp.abs(z))) * LOG2_E
    log_keep = jnp.where(before, -sp, 0.0)
    to_the_right, total = _running_sum(log_keep, tri_suffix, True)
    w = jnp.where(before, jnp.exp2(z - sp + to_the_right + c), 0.0)
    return w, jnp.exp2(z - sp), total


def _sb_fwd(name, proj, carry=None):
    def body(q_ref, k_ref, v_ref, o_ref):
        m0b, m1b = _head_masks(BF16)
        tri_suffix = (_iota((MXU_W, MXU_W), 0) > _iota((MXU_W, MXU_W), 1)).astype(BF16)

        def qblock(i, _):
            r0 = pl.multiple_of(i * QB, QB)
            q = q_ref[pl.ds(r0, QB), :].astype(BF16)
            qs = [q * m0b, q * m1b]
            rowg = r0 + _iota((QB, QB), 0)

            def kv(step, carry):
                cs, acc = carry
                c0 = pl.multiple_of((i - step) * QB, QB)
                k = k_ref[pl.ds(c0, QB), :].astype(BF16)
                v = v_ref[pl.ds(c0, QB), :].astype(BF16)
                before = (c0 + _iota((QB, QB), 1)) < rowg
                new_c = []
                for hh in range(2):
                    w, _, tot = _sb_weights(qs[hh], k, cs[hh], before, tri_suffix)
                    new_c.append(cs[hh] + tot)
                    acc = acc + _dot(w.astype(BF16), v * (m0b if hh == 0 else m1b))
                return tuple(new_c), acc

            init = ((jnp.zeros((QB, 1), F32),) * 2, jnp.zeros((QB, LANES), F32))
            _, acc = lax.fori_loop(0, i + 1, kv, init)
            o_ref[pl.ds(r0, QB), :] = acc.astype(BF16)
            return 0

        lax.fori_loop(0, S // QB, qblock, 0)

    col = lambda base: pl.BlockSpec((S, LANES), lambda p: (0, base // LANES + p))
    return _carrier_call(
        body, name=name, grid=(SB_H // 2,), in_specs=[col(P_QB), col(P_KB), col(P_VB)],
        out_specs=[pl.BlockSpec((S, LANES), lambda p: (0, p))],
        out_shape=[jax.ShapeDtypeStruct((S, SB_H * SB_DIM), BF16)],
        args=(proj, proj, proj), sem=("parallel",), carry=carry)


def _sb_bwd(name, proj, do, do_block0, carry=None):
    nb = S // QB

    def body(q_ref, k_ref, v_ref, do_ref, dq_ref, dk_ref, dv_ref, sig_scr, dl_scr, dk_acc, dv_acc):
        m0b, m1b = _head_masks(BF16)
        tri_suffix = (_iota((MXU_W, MXU_W), 0) > _iota((MXU_W, MXU_W), 1)).astype(BF16)
        tri_prefix = (_iota((MXU_W, MXU_W), 0) < _iota((MXU_W, MXU_W), 1)).astype(BF16)
        dk_acc[...] = jnp.zeros_like(dk_acc)
        dv_acc[...] = jnp.zeros_like(dv_acc)

        def qblock(i, _):
            r0 = pl.multiple_of(i * QB, QB)
            rows = pl.ds(r0, QB)
            q = q_ref[rows, :].astype(BF16)
            qs = [q * m0b, q * m1b]
            dob = do_ref[rows, :].astype(BF16)
            dos = [dob * m0b, dob * m1b]
            rowg = r0 + _iota((QB, QB), 0)

            def sweep_left(step, cs):
                kb = i - step
                c0 = pl.multiple_of(kb * QB, QB)
                cols = pl.ds(c0, QB)
                k = k_ref[cols, :].astype(BF16)
                v = v_ref[cols, :].astype(BF16)
                before = (c0 + _iota((QB, QB), 1)) < rowg
                new_c = []
                dv = None
                for hh in range(2):
                    w, sig, tot = _sb_weights(qs[hh], k, cs[hh], before, tri_suffix)
                    new_c.append(cs[hh] + tot)
                    sig_scr[hh, kb] = sig
                    dl_scr[hh, kb] = _dot(dos[hh], v, "nt") * w
                    part = _dot(w.astype(BF16), dos[hh], "tn")
                    dv = part if dv is None else dv + part
                dv_acc[cols, :] += dv
                return tuple(new_c)

            lax.fori_loop(0, i + 1, sweep_left, (jnp.zeros((QB, 1), F32),) * 2)

            def sweep_right(kb, carry):
                ps, dq = carry
                c0 = pl.multiple_of(kb * QB, QB)
                cols = pl.ds(c0, QB)
                k = k_ref[cols, :].astype(BF16)
                before = (c0 + _iota((QB, QB), 1)) < rowg
                new_p = []
                dk = None
                for hh in range(2):
                    dl = dl_scr[hh, kb]
                    sig = sig_scr[hh, kb]
                    to_the_left, total = _running_sum(dl, tri_prefix, False)
                    earlier = to_the_left + ps[hh]
                    new_p.append(ps[hh] + total)
                    dz = (jnp.where(before, dl * (1.0 - sig) - earlier * sig, 0.0) * SCALE_B).astype(BF16)
                    dq = dq + _dot(dz, k * (m0b if hh == 0 else m1b))
                    part = _dot(dz, qs[hh], "tn")
                    dk = part if dk is None else dk + part
                dk_acc[cols, :] += dk
                return tuple(new_p), dq

            init = ((jnp.zeros((QB, 1), F32),) * 2, jnp.zeros((QB, LANES), F32))
            _, dq = lax.fori_loop(0, i + 1, sweep_right, init)
            dq_ref[rows, :] = dq.astype(BF16)
            return 0

        lax.fori_loop(0, nb, qblock, 0)
        dk_ref[...] = dk_acc[...].astype(BF16)
        dv_ref[...] = dv_acc[...].astype(BF16)

    col = lambda base: pl.BlockSpec((S, LANES), lambda p: (0, base // LANES + p))
    out = pl.BlockSpec((S, LANES), lambda p: (0, p))
    shape = jax.ShapeDtypeStruct((S, SB_H * SB_DIM), BF16)
    return _carrier_call(
        body, name=name, grid=(SB_H // 2,),
        in_specs=[col(P_QB), col(P_KB), col(P_VB), pl.BlockSpec((S, LANES), lambda p: (0, do_block0 + p))],
        out_specs=[out, out, out], out_shape=[shape, shape, shape],
        scratch_shapes=[pltpu.VMEM((2, nb, QB, QB), F32), pltpu.VMEM((2, nb, QB, QB), F32),
                        pltpu.VMEM((S, LANES), F32), pltpu.VMEM((S, LANES), F32)],
        args=(proj, proj, proj, do), sem=("parallel",), carry=carry)


def _band_row_index():
    j = np.arange(TOEP_W)
    rel = np.clip(LEFT_CHUNKS * CHUNK - j, -REL_CLIP, REL_CLIP) + REL_CLIP
    rel[BAND_W:] = 2 * REL_CLIP
    return rel.astype(np.int32)


def _band_tiles(r0_ref, q_ref, kpad, vpad, m, m0b, m1b, static_ok, bias):
    r0 = pl.multiple_of(m * BQ, BQ)
    q = q_ref[0, pl.ds(r0, BQ), :]
    kw = kpad[pl.ds(r0, BAND_W), :]
    vw = vpad[pl.ds(r0, BAND_W), :]
    ok = static_ok & ((r0 - BAND_PAD + _iota((BQ, BAND_W), 1)) >= 0)
    qs = [q * m0b, q * m1b]
    ps = []
    for hh in range(2):
        s = jnp.where(ok, _dot(qs[hh], kw, "nt") * (SCALE_C * LOG2_E) + bias[hh], NEG)
        e = jnp.exp2(s - jnp.max(s, axis=-1, keepdims=True))
        ps.append(e * (1.0 / jnp.sum(e, axis=-1, keepdims=True)))
    return r0, qs, kw, vw, ps


def _band_setup(qkv_ref, r0_ref, kpad, vpad):
    kpad[0:BAND_PAD, :] = jnp.zeros((BAND_PAD, LANES), BF16)
    vpad[0:BAND_PAD, :] = jnp.zeros((BAND_PAD, LANES), BF16)
    kpad[BAND_PAD:, :] = qkv_ref[1]
    vpad[BAND_PAD:, :] = qkv_ref[2]
    jc = lax.shift_right_logical(_iota((BQ, BAND_W), 1), 6)
    rc = lax.shift_right_logical(_iota((BQ, BAND_W), 0), 6)
    static_ok = (jc >= rc) & (jc <= rc + LEFT_CHUNKS)
    bias = []
    for hh in range(2):
        row = jnp.broadcast_to(r0_ref[hh:hh + 1, :] * LOG2_E, (BQ, TOEP_W))
        bias.append(pltpu.roll(row, 0, 1, stride=1, stride_axis=0)[:, :BAND_W])
    return static_ok, bias


def _band_fwd(name, qkv, r0, carry=None):
    def body(qkv_ref, r0_ref, o_ref, kpad, vpad):
        m0b, m1b = _head_masks(BF16)
        static_ok, bias = _band_setup(qkv_ref, r0_ref, kpad, vpad)

        def qblock(m, _):
            r0_, _, _, vw, ps = _band_tiles(r0_ref, qkv_ref, kpad, vpad, m, m0b, m1b, static_ok, bias)
            o = _dot(ps[0].astype(BF16), vw * m0b) + _dot(ps[1].astype(BF16), vw * m1b)
            o_ref[pl.ds(r0_, BQ), :] = o.astype(BF16)
            return 0

        lax.fori_loop(0, S // BQ, qblock, 0)

    return _carrier_call(
        body, name=name, grid=(C_H // 2,),
        in_specs=[pl.BlockSpec((3, S, LANES), lambda p: (0, 0, p)), pl.BlockSpec((None, 2, TOEP_W), lambda p: (p, 0, 0))],
        out_specs=[pl.BlockSpec((S, LANES), lambda p: (0, p))],
        out_shape=[jax.ShapeDtypeStruct((S, C_H * C_DIM), BF16)],
        scratch_shapes=[pltpu.VMEM((S + BAND_PAD, LANES), BF16), pltpu.VMEM((S + BAND_PAD, LANES), BF16)],
        args=(qkv, r0), sem=("parallel",), carry=carry)


def _band_bwd(name, qkv, r0, do, carry=None):
    def body(qkv_ref, r0_ref, do_ref, dqkv_ref, dr0_ref, kpad, vpad, dkpad, dvpad, db_acc):
        m0b, m1b = _head_masks(BF16)
        static_ok, bias = _band_setup(qkv_ref, r0_ref, kpad, vpad)
        dkpad[...] = jnp.zeros_like(dkpad)
        dvpad[...] = jnp.zeros_like(dvpad)
        db_acc[...] = jnp.zeros_like(db_acc)

        def qblock(m, _):
            r0_, qs, kw, vw, ps = _band_tiles(r0_ref, qkv_ref, kpad, vpad, m, m0b, m1b, static_ok, bias)
            dob = do_ref[pl.ds(r0_, BQ), :].astype(BF16)
            dos = [dob * m0b, dob * m1b]
            dq = None
            dk = None
            dv = None
            for hh in range(2):
                p = ps[hh]
                dp = _dot(dos[hh], vw, "nt")
                ds = p * (dp - jnp.sum(dp * p, axis=-1, keepdims=True))
                db_acc[hh, :, 0:BAND_W] += ds
                dsb = (ds * SCALE_C).astype(BF16)
                t = _dot(dsb, kw * (m0b if hh == 0 else m1b))
                dq = t if dq is None else dq + t
                t = _dot(dsb, qs[hh], "tn")
                dk = t if dk is None else dk + t
                t = _dot(p.astype(BF16), dos[hh], "tn")
                dv = t if dv is None else dv + t
            dqkv_ref[0, pl.ds(r0_, BQ), :] = dq.astype(BF16)
            dkpad[pl.ds(r0_, BAND_W), :] += dk
            dvpad[pl.ds(r0_, BAND_W), :] += dv
            return 0

        lax.fori_loop(0, S // BQ, qblock, 0)
        dqkv_ref[1] = dkpad[BAND_PAD:, :].astype(BF16)
        dqkv_ref[2] = dvpad[BAND_PAD:, :].astype(BF16)
        sub = _iota((8, TOEP_W), 0)
        for hh in range(2):
            folded = db_acc[hh, 0:8, :]
            for a in range(1, BQ // 8):
                folded = folded + pltpu.roll(db_acc[hh, 8 * a:8 * a + 8, :], TOEP_W - 8 * a, 1)
            for bit in range(3):
                moved = pltpu.roll(folded, TOEP_W - (1 << bit), 1)
                folded = jnp.where((sub & (1 << bit)) != 0, moved, folded)
            dr0_ref[hh:hh + 1, :] = jnp.sum(folded, axis=0, keepdims=True)

    return _carrier_call(
        body, name=name, grid=(C_H // 2,),
        in_specs=[pl.BlockSpec((3, S, LANES), lambda p: (0, 0, p)), pl.BlockSpec((None, 2, TOEP_W), lambda p: (p, 0, 0)),
                  pl.BlockSpec((S, LANES), lambda p: (0, p))],
        out_specs=[pl.BlockSpec((3, S, LANES), lambda p: (0, 0, p)), pl.BlockSpec((None, 2, TOEP_W), lambda p: (p, 0, 0))],
        out_shape=[jax.ShapeDtypeStruct((3, S, C_H * C_DIM), BF16), jax.ShapeDtypeStruct((C_H // 2, 2, TOEP_W), F32)],
        scratch_shapes=[pltpu.VMEM((S + BAND_PAD, LANES), BF16), pltpu.VMEM((S + BAND_PAD, LANES), BF16),
                        pltpu.VMEM((S + BAND_PAD, LANES), F32), pltpu.VMEM((S + BAND_PAD, LANES), F32),
                        pltpu.VMEM((2, BQ, TOEP_W), F32)],
        args=(qkv, r0, do), sem=("parallel",), carry=carry)


def _bias_table_grad(name, dr0):
    w_out = 5 * LANES

    def body(d_ref, o_ref):
        j = _iota((TOEP_W, w_out), 0)
        rel = jnp.clip(LEFT_CHUNKS * CHUNK - j, -REL_CLIP, REL_CLIP) + REL_CLIP
        rel = jnp.where(j >= BAND_W, 2 * REL_CLIP, rel)
        onehot = (rel == _iota((TOEP_W, w_out), 1)).astype(BF16)
        d = d_ref[...]
        hi = d.astype(BF16)
        mid = (d - hi.astype(F32))
        mid_b = mid.astype(BF16)
        lo = (mid - mid_b.astype(F32)).astype(BF16)
        o_ref[...] = _dot(hi, onehot) + _dot(mid_b, onehot) + _dot(lo, onehot)

    return pl.pallas_call(
        body, name=name, out_shape=jax.ShapeDtypeStruct((C_H, w_out), F32),
        in_specs=[pl.BlockSpec((C_H, TOEP_W), lambda: (0, 0))], out_specs=pl.BlockSpec((C_H, w_out), lambda: (0, 0)),
        grid=(),
    )(dr0)


def _carry_gather(cy, slots, names, ici, d2d):
    idx = [cy.operand(slots[n], True) for n in names]
    n = len(names)
    base_i = cy.sems(3 * n) if ici else 0
    base_d = cy.sems(3 * n) if d2d else 0

    def piece(refs, t, slot, cc):
        return refs[idx[t]].at[slot, _half_rows(cc, slots[names[t]].shape[1] // 2), :]

    def over_ici(refs, send, recv, arriving):
        x, y, c, chips = _position()
        out = []
        for t in range(n):
            for j in range(3):
                r = piece(refs, t, 2 * chips[j][0] + chips[j][1] if arriving else 2 * x + y, c)
                out.append(_remote(r, r, send, recv, base_i + 3 * t + j, (*chips[j], c)))
        return out

    def over_d2d(refs, send, recv, arriving):
        x, y, c, chips = _position()
        out = []
        for t in range(n):
            for j in range(3):
                r = piece(refs, t, 2 * chips[j][0] + chips[j][1], 1 - c if arriving else c)
                out.append(_remote(r, r, send, recv, base_d + 3 * t + j, (x, y, 1 - c)))
        return out

    def start_ici(refs, fresh, send, recv):
        for cp in over_ici(refs, send, recv, False):
            cp.start()

    def wait_ici(refs, fresh, send, recv):
        for cp in over_ici(refs, send, recv, True):
            cp.wait_recv()
        for cp in over_ici(refs, send, recv, False):
            cp.wait_send()

    def start_d2d(refs, fresh, send, recv):
        for cp in over_d2d(refs, send, recv, False):
            cp.start()

    def wait_d2d(refs, fresh, send, recv):
        for cp in over_d2d(refs, send, recv, True):
            cp.wait_recv()
        for cp in over_d2d(refs, send, recv, False):
            cp.wait_send()

    def wait_ici_and_forward(refs, fresh, send, recv):
        forwards = over_d2d(refs, send, recv, False)
        for k, cp in enumerate(over_ici(refs, send, recv, True)):
            cp.wait_recv()
            forwards[k].start()
        for cp in over_ici(refs, send, recv, False):
            cp.wait_send()

    if ici and d2d:
        cy.starts.append(start_ici)
        cy.finishes += [wait_ici_and_forward, wait_d2d]
    elif ici:
        cy.starts.append(start_ici)
        cy.finishes.append(wait_ici)
    else:
        cy.starts.append(start_d2d)
        cy.finishes.append(wait_d2d)

    def done(aliased, fresh):
        for t, name in enumerate(names):
            slots[name] = aliased[idx[t]]

    cy.on_done.append(done)


def _carry_chip_exchange(cy, sums, got, names):
    idx = [cy.operand(sums[n], False) for n in names]
    out = [cy.result((3,) + sums[n].shape[1:], BF16) for n in names]
    base = cy.sems(3 * len(names))

    def copies(refs, fresh, send, recv):
        x, y, c, chips = _position()
        return [_remote(refs[idx[t]].at[2 * chips[j][0] + chips[j][1]], fresh[out[t]].at[j], send, recv, base + 3 * t + j,
                        (*chips[j], c)) for t in range(len(names)) for j in range(3)]

    def start(refs, fresh, send, recv):
        for cp in copies(refs, fresh, send, recv):
            cp.start()

    def wait(refs, fresh, send, recv):
        for cp in copies(refs, fresh, send, recv):
            cp.wait()

    cy.starts.append(start)
    cy.finishes.append(wait)

    def done(aliased, fresh):
        for t, name in enumerate(names):
            got[name] = fresh[out[t]]

    cy.on_done.append(done)


def _run_carry(name, cy):
    _, res = _carrier_call(None, name=name, grid=(), in_specs=[], out_specs=[], out_shape=[], args=(), sem=(), carry=cy)
    cy.done(res)


FIRST_WEIGHTS = ("ev_w_in",)
NEXT_WEIGHTS = ("ev_w_uq", "ev_w_ukv")
WEIGHTS_A = ("ev_w_out", "w_gate0", "w_up0")
WEIGHTS_B = ("w_down0", "od_w_qkv", "od_w_out")
WEIGHTS_C = ("w_gate1",)
WEIGHTS_D = ("w_up1", "w_down1")
GRAD_GROUPS = {"ffn1": ("w_gate1", "w_up1", "w_down1"), "od": ("od_w_qkv", "od_w_out"),
               "ffn0": ("w_gate0", "w_up0", "w_down0"), "ev_out": ("ev_w_out",),
               "ev": ("ev_w_in", "ev_w_uq", "ev_w_ukv")}


def _carry_pair_exchange(cy, parts, theirs, names):
    idx = [cy.operand(parts[n], False) for n in names]
    out = [cy.result((N_CHIPS, parts[n].shape[1] // 2, parts[n].shape[2]), BF16) for n in names]
    base = cy.sems(len(names))

    def copies(refs, fresh, send, recv):
        x, y, c, _ = _position()
        return [_remote(refs[idx[t]].at[:, _half_rows(1 - c, parts[n].shape[1] // 2), :], fresh[out[t]], send, recv,
                        base + t, (x, y, 1 - c)) for t, n in enumerate(names)]

    cy.starts.append(lambda refs, fresh, send, recv: [cp.start() for cp in copies(refs, fresh, send, recv)])
    cy.finishes.append(lambda refs, fresh, send, recv: [cp.wait() for cp in copies(refs, fresh, send, recv)])

    def done(aliased, fresh):
        for t, name in enumerate(names):
            theirs[name] = fresh[out[t]]

    cy.on_done.append(done)


def _carry_sibling_exchange(cy, fulls, pieces):
    idx = [cy.operand(fulls[p], True) for p, _ in pieces]
    base = cy.sems(len(pieces))

    def copies(refs, send, recv, arriving):
        x, y, c, _ = _position()
        out = []
        for t, (p, layer) in enumerate(pieces):
            r = refs[idx[t]].at[layer, _half_rows(1 - c if arriving else c, fulls[p].shape[1] // 2), :]
            out.append(_remote(r, r, send, recv, base + t, (x, y, 1 - c)))
        return out

    def start(refs, fresh, send, recv):
        for cp in copies(refs, send, recv, False):
            cp.start()

    def wait(refs, fresh, send, recv):
        for cp in copies(refs, send, recv, True):
            cp.wait_recv()
        for cp in copies(refs, send, recv, False):
            cp.wait_send()

    cy.starts.append(start)
    cy.finishes.append(wait)

    def done(aliased, fresh):
        for t, (p, _) in enumerate(pieces):
            fulls[p] = aliased[idx[t]]

    cy.on_done.append(done)


RIDES = {
    "cast_rest": (("gather", FIRST_WEIGHTS),),
    "proj_in": (("gather", NEXT_WEIGHTS),),
    "mla_attn": (("gather_ici", WEIGHTS_A),),
    "sb_attn": (("gather_d2d", WEIGHTS_A), ("gather_ici", WEIGHTS_B)),
    "ev_out": (("gather_d2d", WEIGHTS_B),),
    "ffn0": (("gather_ici", WEIGHTS_C),),
    "qkv": (("gather_d2d", WEIGHTS_C),),
    "band_attn": (("gather_ici", WEIGHTS_D),),
    "od_out": (("gather_d2d", WEIGHTS_D),),
    "od_out_bwd_w": (("pair", "ffn1"),),
    "band_attn_bwd": (("chips", "ffn1"),),
    "rms_mix1_bwd": (("pair", "od"),),
    "ev_out_bwd_w": (("pair", "ffn0"),),
    "mla_attn_bwd": (("chips", "od"), ("sibling", "ffn1"), ("pair", "ev_out")),
    "sb_attn_bwd": (("chips", "ffn0"), ("sibling", "od"), ("chips", "ev_out")),
    "proj_in_bwd_w": (("sibling", "ffn0"), ("sibling", "ev_out")),
    "grads_pair_ev": (("pair", "ev"),),
    "proj_in_bwd_x": (("chips", "ev"),),
    "grads_sibling_ev": (("sibling", "ev"),),
}


class _Exchanges:
    def __init__(self, slots, pos, shapes, cast_rest):
        self.slots, self.pos, self.shapes, self.cast_rest = dict(slots), pos, shapes, cast_rest
        self.parts, self.theirs, self.sums, self.got, self.fulls = {}, {}, {}, {}, {}

    def begin(self):
        self.slots.update(self.cast_rest(self.carry("cast_rest")))

    def weights(self, *names):
        return [self.slots[n] for n in names]

    def _pair_sums(self, group):
        names = GRAD_GROUPS[group]
        self.sums.update(zip(names, _pair_sums("pair_sums_" + group, [self.parts[n] for n in names],
                                               [self.theirs[n] for n in names], self.pos)))

    def _chip_sums(self, group):
        names = GRAD_GROUPS[group]
        items = [(self.sums[n], self.got[n], PART_OF[n][1], self.shapes[PART_OF[n][0]], self.fulls.get(PART_OF[n][0]))
                 for n in names]
        self.fulls.update(zip([PART_OF[n][0] for n in names], _chip_sums("chip_sums_" + group, items, self.pos)))

    def carry(self, stage):
        cy = _Carry()
        for step, what in RIDES[stage]:
            if step == "gather":
                _carry_gather(cy, self.slots, what, True, True)
            elif step == "gather_ici":
                _carry_gather(cy, self.slots, what, True, False)
            elif step == "gather_d2d":
                _carry_gather(cy, self.slots, what, False, True)
            elif step == "pair":
                _carry_pair_exchange(cy, self.parts, self.theirs, GRAD_GROUPS[what])
            elif step == "chips":
                self._pair_sums(what)
                _carry_chip_exchange(cy, self.sums, self.got, GRAD_GROUPS[what])
            elif step == "sibling":
                self._chip_sums(what)
                _carry_sibling_exchange(cy, self.fulls, [PART_OF[n] for n in GRAD_GROUPS[what]])
        return cy

    def grads(self, group, parts):
        self.parts.update(parts)
        if group == "ev":
            _run_carry("grads_pair_ev", self.carry("grads_pair_ev"))

    def finish(self):
        _run_carry("grads_sibling_ev", self.carry("grads_sibling_ev"))
        return {n: self.fulls[n] for n in BIG}


class _NoExchanges:
    def __init__(self, slots):
        self.slots, self.parts = dict(slots), {}

    def begin(self):
        pass

    def weights(self, *names):
        return [self.slots[n] for n in names]

    def carry(self, stage):
        return None

    def grads(self, group, parts):
        self.parts.update(parts)


def _w_in_pieces():
    segments = ((0, Q_LORA, P_CQ), (Q_LORA, Q_LORA + KV_LORA, P_CKV),
                (Q_LORA + KV_LORA, Q_LORA + KV_LORA + MLA_ROPE, P_KR + KR_LANE),
                (Q_LORA + KV_LORA + MLA_ROPE, EVEN_IN, P_QB))
    width = EVEN_IN // N_CHIPS
    pieces = []
    for lo, hi, at in segments:
        for k in range(N_CHIPS):
            a, b = max(lo, k * width), min(hi, (k + 1) * width)
            if a < b:
                pieces.append((k, a - k * width, b - a, at + a - lo))
    return pieces


def _w_in_padded(name, w_in_s):
    tr = MXU_W

    def body(s_ref, o_ref):
        o_ref[...] = jnp.zeros(o_ref.shape, BF16)
        for k, a, n, at in _w_in_pieces():
            o_ref[:, at:at + n] = s_ref[k, :, a:a + n]

    return pl.pallas_call(
        body, name=name, grid=(D // tr,),
        in_specs=[pl.BlockSpec((N_CHIPS, tr, EVEN_IN // N_CHIPS), lambda i: (0, i, 0))],
        out_specs=pl.BlockSpec((tr, P_IN), lambda i: (i, 0)), out_shape=jax.ShapeDtypeStruct((D, P_IN), BF16),
        compiler_params=_params("parallel"))(w_in_s)


def _w_in_sharded(name, d_w_in_p):
    tr = MXU_W

    def body(p_ref, o_ref):
        for k, a, n, at in _w_in_pieces():
            o_ref[k, :, a:a + n] = p_ref[:, at:at + n]

    return pl.pallas_call(
        body, name=name, grid=(D // tr,),
        in_specs=[pl.BlockSpec((tr, P_IN), lambda i: (i, 0))],
        out_specs=pl.BlockSpec((N_CHIPS, tr, EVEN_IN // N_CHIPS), lambda i: (0, i, 0)),
        out_shape=jax.ShapeDtypeStruct((N_CHIPS, D, EVEN_IN // N_CHIPS), BF16),
        compiler_params=_params("parallel"))(d_w_in_p)


def _mla_weights(w_uq_s, w_ukv_s):
    w_uq = jnp.moveaxis(w_uq_s, 0, 1).reshape(Q_LORA, MLA_H, MLA_NOPE + MLA_ROPE)
    w_uq_p = jnp.concatenate([w_uq, jnp.zeros((Q_LORA, MLA_H, LANES - MLA_NOPE - MLA_ROPE), BF16)], axis=2)
    w_ukv = jnp.moveaxis(w_ukv_s, 0, 1).reshape(KV_LORA, MLA_H, MLA_NOPE + MLA_V)
    w_uk_p = jnp.concatenate([w_ukv[:, :, :MLA_NOPE], jnp.zeros((KV_LORA, MLA_H, LANES - MLA_NOPE), BF16)], axis=2)
    return dict(
        w_uq=w_uq_p.reshape(Q_LORA, MLA_H * LANES), w_uk=w_uk_p.reshape(KV_LORA, MLA_H * LANES),
        w_uv=w_ukv[:, :, MLA_NOPE:].reshape(KV_LORA, MLA_H * MLA_V))


def _proj_mm(name, u, w_in, carry=None):
    return _mm(name, u, w_in, kind="nn", grid=(S // TM, 1, 1),
               a_spec=pl.BlockSpec((TM, D), lambda i, j, k: (i, 0)), b_spec=pl.BlockSpec((D, P_IN), lambda i, j, k: (0, 0)),
               o_spec=pl.BlockSpec((TM, P_IN), lambda i, j, k: (i, 0)), out_shape=(S, P_IN), out_dtype=F32, acc_shape=None,
               carry=carry)


def _out_proj(name, o, w, resid, carry=None):
    return _mm(name, o, w, kind="nn", grid=(S // TM, 1, 1),
               a_spec=pl.BlockSpec((TM, D), lambda i, j, k: (i, 0)), b_spec=pl.BlockSpec((D, D), lambda i, j, k: (0, 0)),
               o_spec=pl.BlockSpec((TM, D), lambda i, j, k: (i, 0)), out_shape=(S, D), out_dtype=F32, acc_shape=None,
               resid=resid, r_spec=pl.BlockSpec((TM, D), lambda i, j, k: (i, 0)), carry=carry)


def _out_proj_bwd(name, dh, o, w, ex):
    d_o = _mm(name + "_x", dh, w, kind="nt", grid=(S // TM, 1, 1),
              a_spec=pl.BlockSpec((TM, D), lambda i, j, k: (i, 0)), b_spec=pl.BlockSpec((D, D), lambda i, j, k: (0, 0)),
              o_spec=pl.BlockSpec((TM, D), lambda i, j, k: (i, 0)), out_shape=(S, D), out_dtype=F32, acc_shape=None)
    d_w = _mm(name + "_w", o, dh, kind="tn", grid=(2, S // TM),
              a_spec=pl.BlockSpec((TM, TM), lambda j, k: (k, j)), b_spec=pl.BlockSpec((TM, D), lambda j, k: (k, 0)),
              o_spec=pl.BlockSpec((TM, D), lambda j, k: (j, 0)), out_shape=(D, D), out_dtype=BF16, acc_shape=(TM, D),
              carry=ex.carry(name + "_w"))
    return d_o, d_w


def _local_step(x, tgt, sm, ex):
    def riding(stage, fn, *args):
        cy = ex.carry(stage)
        res, copies = fn(stage, *args, carry=cy)
        if cy is not None:
            cy.done(copies)
        return res

    cos_t, sin_t = _rope_tables()
    g_mix, g_ffn = sm["g_mix"], sm["g_ffn"]
    r0 = sm["od_rel_bias"][0][:, _band_row_index()].reshape(C_H // 2, 2, TOEP_W)
    nt = 3

    ex.begin()
    w = {"w_in": _w_in_padded("w_in_padded", *ex.weights(*FIRST_WEIGHTS))}
    u0 = _rms_fwd("rms_mix0", x, g_mix[0:1])
    proj = _proj_mm("proj_in", u0, w["w_in"], ex.carry("proj_in"))
    w.update(_mla_weights(*ex.weights(*NEXT_WEIGHTS)))
    qa, ka, va = _mla_prep_fwd("mla_prep", proj, sm["ev_g_cq"], sm["ev_g_ckv"], w["w_uq"], w["w_uk"], w["w_uv"], cos_t, sin_t)
    o_a, lse = riding("mla_attn", _mla_fwd, qa, ka, va)
    o_b, = riding("sb_attn", _sb_fwd, proj)
    o_ev = jnp.concatenate([o_a.astype(BF16), o_b], axis=1)
    w["ev_w_out"] = ex.weights("ev_w_out")[0].reshape(D, D)
    h1 = _out_proj("ev_out", o_ev, w["ev_w_out"], x, ex.carry("ev_out"))
    w["w_gate0"], w["w_up0"], w["w_down0"] = ex.weights("w_gate0", "w_up0", "w_down0")
    h2, gate0, up0 = riding("ffn0", _ffn_fwd, h1, g_ffn[0:1], w["w_gate0"], w["w_up0"], w["w_down0"])
    w["w_qkv"] = jnp.moveaxis(ex.weights("od_w_qkv")[0], 0, 1).reshape(D, nt * D)
    u2 = _rms_fwd("rms_mix1", h2, g_mix[1:2])
    qkv = _mm("qkv", u2, w["w_qkv"], kind="nn", grid=(S // TQ, nt, 1),
              a_spec=pl.BlockSpec((TQ, D), lambda i, t, k: (i, 0)), b_spec=pl.BlockSpec((D, D), lambda i, t, k: (0, t)),
              o_spec=pl.BlockSpec((None, TQ, D), lambda i, t, k: (t, i, 0)),
              out_shape=(nt, S, D), out_dtype=BF16, acc_shape=None, carry=ex.carry("qkv"))
    o_od, = riding("band_attn", _band_fwd, qkv, r0)
    w["od_w_out"] = ex.weights("od_w_out")[0].reshape(D, D)
    h3 = _out_proj("od_out", o_od, w["od_w_out"], h2, ex.carry("od_out"))
    w["w_gate1"], w["w_up1"], w["w_down1"] = ex.weights("w_gate1", "w_up1", "w_down1")
    (h4, gate1, up1), _ = _ffn_fwd("ffn1", h3, g_ffn[1:2], w["w_gate1"], w["w_up1"], w["w_down1"])

    loss, dh4, dg_final = _loss_bwd("loss", h4, sm["g_final"].reshape(1, D), tgt)

    dh3, dg_ffn1, u3, dgate, dup, act = _ffn_bwd("ffn1_bwd", dh4, h3, g_ffn[1:2], gate1, up1,
                                                 w["w_gate1"], w["w_up1"], w["w_down1"])
    d_wg1, d_wu1, d_wd1 = _ffn_wgrads("ffn1_dw", u3, dgate, dup, act, dh4)
    ex.grads("ffn1", {"w_gate1": d_wg1, "w_up1": d_wu1, "w_down1": d_wd1})

    d_ood, d_w_od_out = _out_proj_bwd("od_out_bwd", dh3, o_od, w["od_w_out"], ex)
    dqkv, dr0 = riding("band_attn_bwd", _band_bwd, qkv, r0, d_ood)
    du2 = _mm("qkv_bwd_x", dqkv, w["w_qkv"], kind="nt", grid=(S // TQ, nt),
              a_spec=pl.BlockSpec((None, TQ, D), lambda i, t: (t, i, 0)), b_spec=pl.BlockSpec((D, D), lambda i, t: (0, t)),
              o_spec=pl.BlockSpec((TQ, D), lambda i, t: (i, 0)), out_shape=(S, D), out_dtype=F32, acc_shape=(TQ, D))
    wide, per = D // MXU_W, nt * D // N_CHIPS // MXU_W
    piece = lambda r: pl.BlockSpec((None, TQ, MXU_W), lambda j, k: ((per * j + r) // wide, k, (per * j + r) % wide))
    d_w_qkv = _mm("qkv_bwd_w", u2, dqkv, kind="tn", grid=(N_CHIPS, S // TQ),
                  a_spec=pl.BlockSpec((TQ, D), lambda j, k: (k, 0)), b_spec=[piece(r) for r in range(per)],
                  o_spec=pl.BlockSpec((None, D, per * MXU_W), lambda j, k: (j, 0, 0)),
                  out_shape=(N_CHIPS, D, per * MXU_W), out_dtype=BF16, acc_shape=(D, per * MXU_W))
    shard_cols = lambda a: jnp.moveaxis(a.reshape(a.shape[0], N_CHIPS, a.shape[1] // N_CHIPS), 1, 0)
    ex.grads("od", {"od_w_qkv": d_w_qkv, "od_w_out": d_w_od_out.reshape(N_CHIPS, D // N_CHIPS, D)})
    dh2, dg_mix1 = _rms_bwd("rms_mix1_bwd", du2, h2, g_mix[1:2], dh3, carry=ex.carry("rms_mix1_bwd"))
    d_rel = _bias_table_grad("rel_bias_grad", dr0.reshape(C_H, TOEP_W))[:, :2 * REL_CLIP + 1]

    dh1, dg_ffn0, u1, dgate, dup, act = _ffn_bwd("ffn0_bwd", dh2, h1, g_ffn[0:1], gate0, up0,
                                                 w["w_gate0"], w["w_up0"], w["w_down0"])
    d_wg0, d_wu0, d_wd0 = _ffn_wgrads("ffn0_dw", u1, dgate, dup, act, dh2)
    ex.grads("ffn0", {"w_gate0": d_wg0, "w_up0": d_wu0, "w_down0": d_wd0})

    d_oev, d_w_ev_out = _out_proj_bwd("ev_out_bwd", dh1, o_ev, w["ev_w_out"], ex)
    ex.grads("ev_out", {"ev_w_out": d_w_ev_out.reshape(N_CHIPS, D // N_CHIPS, D)})
    dqa, dka, dva = riding("mla_attn_bwd", _mla_bwd, qa, ka, va, o_a, lse, d_oev, 0)
    dqb, dkb, dvb = riding("sb_attn_bwd", _sb_bwd, proj, d_oev, MLA_H * MLA_V // LANES)
    dcq, dckv, dkr, d_w_uq, d_w_uk, d_w_uv, dg_cq, dg_ckv = _mla_prep_bwd(
        "mla_prep_bwd", dqa, dka, dva, proj, sm["ev_g_cq"], sm["ev_g_ckv"], w["w_uq"], w["w_uk"], w["w_uv"], cos_t, sin_t)
    dproj = [dcq, jnp.zeros((S, LANES), BF16), dckv, dqb, dkb, dvb, dkr]
    d_w_in_p = _mm("proj_in_bwd_w", u0, dproj, kind="tn", grid=(1, S // TM),
                   a_spec=pl.BlockSpec((TM, D), lambda j, k: (k, 0)),
                   b_spec=[pl.BlockSpec((TM, p.shape[1]), lambda j, k: (k, 0)) for p in dproj],
                   o_spec=pl.BlockSpec((D, P_IN), lambda j, k: (0, 0)), out_shape=(D, P_IN), out_dtype=BF16,
                   acc_shape=(D, P_IN), carry=ex.carry("proj_in_bwd_w"))
    d_w_uq_std = d_w_uq.reshape(Q_LORA, MLA_H, LANES)[:, :, :MLA_NOPE + MLA_ROPE].reshape(Q_LORA, -1)
    d_w_ukv = jnp.concatenate([d_w_uk.reshape(KV_LORA, MLA_H, LANES)[:, :, :MLA_NOPE],
                               d_w_uv.reshape(KV_LORA, MLA_H, MLA_V)], axis=2).reshape(KV_LORA, -1)
    ex.grads("ev", {"ev_w_in": _w_in_sharded("w_in_sharded", d_w_in_p), "ev_w_uq": shard_cols(d_w_uq_std.astype(BF16)),
                    "ev_w_ukv": shard_cols(d_w_ukv.astype(BF16))})
    du0 = _mm("proj_in_bwd_x", dproj, w["w_in"], kind="nt", grid=(S // TM, 1, 1),
              a_spec=[pl.BlockSpec((TM, p.shape[1]), lambda i, j, k: (i, 0)) for p in dproj],
              b_spec=pl.BlockSpec((D, P_IN), lambda i, j, k: (0, 0)),
              o_spec=pl.BlockSpec((TM, D), lambda i, j, k: (i, 0)), out_shape=(S, D), out_dtype=F32, acc_shape=None,
              carry=ex.carry("proj_in_bwd_x"))
    grad_x, dg_mix0 = _rms_bwd("rms_mix0_bwd", du0, x, g_mix[0:1], dh1)
    small = {
        "ev_g_cq": dg_cq, "ev_g_ckv": dg_ckv, "od_rel_bias": d_rel.reshape(1, C_H, 2 * REL_CLIP + 1),
        "g_mix": jnp.concatenate([dg_mix0, dg_mix1], axis=0), "g_ffn": jnp.concatenate([dg_ffn0, dg_ffn1], axis=0),
        "g_final": dg_final.reshape(D),
    }
    return loss, grad_x, small


BIG = ("ev_w_in", "ev_w_uq", "ev_w_ukv", "ev_w_out", "od_w_qkv", "od_w_out", "w_gate", "w_up", "w_down")
SMALL = ("ev_g_cq", "ev_g_ckv", "od_rel_bias", "g_mix", "g_ffn", "g_final")
WEIGHTS = ("ev_w_in", "ev_g_cq", "ev_w_uq", "ev_g_ckv", "ev_w_ukv", "ev_w_out", "od_w_qkv", "od_rel_bias", "od_w_out",
           "g_mix", "g_ffn", "w_gate", "w_up", "w_down", "g_final")
GRAD_PARTS = (("ev_w_in", "ev_w_in", 0), ("ev_w_uq", "ev_w_uq", 0), ("ev_w_ukv", "ev_w_ukv", 0),
              ("ev_w_out", "ev_w_out", 0), ("od_w_qkv", "od_w_qkv", 0), ("od_w_out", "od_w_out", 0),
              ("w_gate0", "w_gate", 0), ("w_gate1", "w_gate", 1), ("w_up0", "w_up", 0), ("w_up1", "w_up", 1),
              ("w_down0", "w_down", 0), ("w_down1", "w_down", 1))
PART_OF = {part: (param, layer) for part, param, layer in GRAD_PARTS}
TRANSPOSED = ("w_gate", "w_up")
ADAMW_TRANSPOSED = ("ev_w_in", "ev_w_uq")


def _row_tile(rows, cap=512, sublanes=16):
    for t in range(min(rows, cap), 0, -1):
        if rows % t == 0 and t % sublanes == 0:
            return t
    return rows


def _cast_into_slot(name, w, layer, pos):
    _, rows, cols = w.shape
    tr = _row_tile(rows)

    def body(pos_ref, w_ref, o_ref):
        o_ref[...] = w_ref[...].astype(BF16)

    return pl.pallas_call(
        body, name=name,
        grid_spec=pltpu.PrefetchScalarGridSpec(
            num_scalar_prefetch=1, grid=(rows // tr,),
            in_specs=[pl.BlockSpec((None, tr, cols), lambda i, p: (layer, i, 0))],
            out_specs=pl.BlockSpec((None, tr, cols), lambda i, p: (p[0], i, 0))),
        out_shape=jax.ShapeDtypeStruct((N_CHIPS, rows, cols), BF16), compiler_params=_params("arbitrary"))(pos, w)


def _cast_many_into_slots(name, items, pos, carry):
    tiles = [_row_tile(w.shape[1]) for w, _ in items]
    turns = _Turns([w.shape[1] // tr for (w, _), tr in zip(items, tiles)])

    def body(pos_ref, *refs):
        i = pl.program_id(0)
        for t in range(len(items)):
            @pl.when(turns.mine(t, i))
            def _(w_ref=refs[t], o_ref=refs[len(items) + t]):
                o_ref[...] = w_ref[...].astype(BF16)

    in_specs, out_specs, out_shape = [], [], []
    for t, ((w, layer), tr) in enumerate(zip(items, tiles)):
        _, rows, cols = w.shape
        at = turns.step(t)
        in_specs.append(pl.BlockSpec((None, tr, cols), lambda i, p, at=at, layer=layer: (layer, at(i), 0)))
        out_specs.append(pl.BlockSpec((None, tr, cols), lambda i, p, at=at: (p[0], at(i), 0)))
        out_shape.append(jax.ShapeDtypeStruct((N_CHIPS, rows, cols), BF16))
    res, copies = _carrier_call(body, name=name, grid=(turns.total,), in_specs=in_specs, out_specs=out_specs,
                                out_shape=out_shape, args=[w for w, _ in items], sem=("arbitrary",), carry=carry,
                                prefetch=(pos,))
    if carry is not None:
        carry.done(copies)
    return res


class _Turns:
    def __init__(self, counts):
        self.counts = list(counts)
        self.starts = [sum(self.counts[:t]) for t in range(len(self.counts))]
        self.total = sum(self.counts)

    def step(self, t):
        start, n = self.starts[t], self.counts[t]
        return lambda i: jnp.clip(i - start, 0, n - 1)

    def mine(self, t, i):
        return (i >= self.starts[t]) & (i < self.starts[t] + self.counts[t])


def _pair_sums(name, parts, theirs, pos):
    n, pair = len(parts), 2
    tiles = [_row_tile(b.shape[1]) for b in theirs]
    blocks = [b.shape[1] // tr for b, tr in zip(theirs, tiles)]
    turns = _Turns([N_CHIPS // pair * nb for nb in blocks])

    def body(pos_ref, *refs):
        i = pl.program_id(0)
        for t in range(n):
            @pl.when(turns.mine(t, i))
            def _(a_ref=refs[2 * t], b_ref=refs[2 * t + 1], o_ref=refs[2 * n + t]):
                o_ref[...] = (a_ref[...].astype(F32) + b_ref[...].astype(F32)).astype(BF16)

    in_specs, out_specs = [], []
    for t, (b, tr, nb) in enumerate(zip(theirs, tiles, blocks)):
        at, block = turns.step(t), (pair, tr, b.shape[2])
        in_specs.append(pl.BlockSpec(block, lambda i, p, at=at, nb=nb: (at(i) // nb, p[1] * nb + at(i) % nb, 0)))
        in_specs.append(pl.BlockSpec(block, lambda i, p, at=at, nb=nb: (at(i) // nb, at(i) % nb, 0)))
        out_specs.append(pl.BlockSpec(block, lambda i, p, at=at, nb=nb: (at(i) // nb, at(i) % nb, 0)))
    return pl.pallas_call(
        body, name=name,
        grid_spec=pltpu.PrefetchScalarGridSpec(num_scalar_prefetch=1, grid=(turns.total,), in_specs=in_specs,
                                               out_specs=out_specs),
        out_shape=[jax.ShapeDtypeStruct(b.shape, BF16) for b in theirs],
        compiler_params=_params("arbitrary"))(pos, *[a for pair in zip(parts, theirs) for a in pair])


def _chip_sums(name, items, pos):
    n = len(items)
    tiles = [_row_tile(s.shape[1]) for s, *_ in items]
    turns = _Turns([s.shape[1] // tr for (s, *_), tr in zip(items, tiles)])
    carried = [t for t, item in enumerate(items) if item[4] is not None]

    def body(pos_ref, *refs):
        i = pl.program_id(0)
        for t in range(n):
            @pl.when(turns.mine(t, i))
            def _(s_ref=refs[2 * t], g_ref=refs[2 * t + 1], o_ref=refs[2 * n + len(carried) + t]):
                o_ref[...] = ((s_ref[...].astype(F32) + g_ref[0].astype(F32)) + g_ref[1].astype(F32)) + g_ref[2].astype(F32)

    in_specs, out_specs = [], []
    for t, ((s, got, layer, full_shape, full), tr) in enumerate(zip(items, tiles)):
        at, cols, nb = turns.step(t), s.shape[2], turns.counts[t]
        in_specs.append(pl.BlockSpec((None, tr, cols), lambda i, p, at=at: (p[0], at(i), 0)))
        in_specs.append(pl.BlockSpec((3, tr, cols), lambda i, p, at=at: (0, at(i), 0)))
        out_specs.append(pl.BlockSpec((None, tr, cols), lambda i, p, at=at, nb=nb, layer=layer: (layer, p[1] * nb + at(i), 0)))
    return pl.pallas_call(
        body, name=name,
        grid_spec=pltpu.PrefetchScalarGridSpec(num_scalar_prefetch=1, grid=(turns.total,),
                                               in_specs=in_specs + [ANY] * len(carried), out_specs=out_specs),
        out_shape=[jax.ShapeDtypeStruct(item[3], F32) for item in items],
        input_output_aliases={1 + 2 * n + k: t for k, t in enumerate(carried)},
        compiler_params=_params("arbitrary"))(
            pos, *[a for item in items for a in item[:2]], *[items[t][4] for t in carried])


def _adamw_update(w, g, m, v):
    m_new = ADAM_B1 * m + (1.0 - ADAM_B1) * g
    v_new = ADAM_B2 * v + (1.0 - ADAM_B2) * (g * g)
    m_hat = m_new / (1.0 - ADAM_B1 ** ADAM_STEP)
    v_hat = v_new / (1.0 - ADAM_B2 ** ADAM_STEP)
    return -ADAM_LR * (m_hat / (jnp.sqrt(v_hat) + ADAM_EPS) + ADAM_WD * w), m_new, v_new


def _small_step(name, grads, loss, w, m, v):
    n, n_dev = len(grads), 8
    offs = [sum(g.shape[0] for g in grads[:t]) for t in range(n + 1)]
    rows = -(-(offs[n] + 1) // 8) * 8
    width = max(g.shape[1] for g in grads)

    def body(*refs):
        g_refs, loss_ref = refs[:n], refs[n]
        w_refs, m_refs, v_refs = (refs[1 + k * n:1 + (k + 1) * n] for k in (1, 2, 3))
        outs = refs[4 * n + 1:8 * n + 2]
        mine, slots, send_sem, recv_sem = refs[8 * n + 2:]
        x, y, c, _ = _position()
        me = 4 * x + 2 * y + c

        def peer(k):
            return (1 - x if k & 4 else x, 1 - y if k & 2 else y, 1 - c if k & 1 else c)

        def logical(k):
            px, py, pc = peer(k)
            return 4 * px + 2 * py + pc

        mine[...] = jnp.zeros(mine.shape, F32)
        for t in range(n):
            mine[offs[t]:offs[t + 1], 0:grads[t].shape[1]] = g_refs[t][...]
        mine[offs[n]:offs[n] + 1, 0:LANES] = loss_ref[...]
        slots[me] = mine[...]
        sends = [pltpu.make_async_remote_copy(
            src_ref=mine, dst_ref=slots.at[me], send_sem=send_sem.at[k], recv_sem=recv_sem.at[k],
            device_id=peer(k), device_id_type=MESH) for k in range(1, n_dev)]
        for cp in sends:
            cp.start()
        for k in range(1, n_dev):
            pltpu.make_async_remote_copy(
                src_ref=mine, dst_ref=slots.at[logical(k)], send_sem=send_sem.at[k], recv_sem=recv_sem.at[k],
                device_id=peer(k), device_id_type=MESH).wait_recv()
        for cp in sends:
            cp.wait_send()
        total = slots[0]
        for d in range(1, n_dev):
            total = total + slots[d]
        for t in range(n):
            gv = total[offs[t]:offs[t + 1], 0:grads[t].shape[1]]
            outs[t][...] = gv
            outs[n + t][...], outs[2 * n + t][...], outs[3 * n + t][...] = _adamw_update(
                w_refs[t][...], gv, m_refs[t][...], v_refs[t][...])
        outs[4 * n][...] = total[offs[n]:offs[n] + 1, 0:LANES]

    vm = pl.BlockSpec(memory_space=pltpu.VMEM)
    shapes = [jax.ShapeDtypeStruct(g.shape, F32) for g in grads]
    res = pl.pallas_call(
        body, name=name, in_specs=[vm] * (4 * n + 1), out_specs=[vm] * (4 * n + 1),
        out_shape=shapes * 4 + [jax.ShapeDtypeStruct(loss.shape, F32)],
        scratch_shapes=[pltpu.VMEM((rows, width), F32), pltpu.VMEM((n_dev, rows, width), F32),
                        pltpu.SemaphoreType.DMA((n_dev,)), pltpu.SemaphoreType.DMA((n_dev,))],
    )(*grads, loss, *w, *m, *v)
    return [res[k * n:(k + 1) * n] for k in range(4)], res[4 * n]


def _adamw(name, w, g, m, v):
    rows, cols = w.shape
    tr = _row_tile(rows, sublanes=8)

    def body(w_ref, g_ref, m_ref, v_ref, d_ref, mo_ref, vo_ref):
        d_ref[...], mo_ref[...], vo_ref[...] = _adamw_update(w_ref[...], g_ref[...], m_ref[...], v_ref[...])

    spec = pl.BlockSpec((tr, cols), lambda i: (i, 0))
    shape = jax.ShapeDtypeStruct((rows, cols), F32)
    return pl.pallas_call(body, name=name, grid=(rows // tr,), in_specs=[spec] * 4, out_specs=[spec] * 3,
                          out_shape=[shape] * 3, compiler_params=_params("parallel"))(w, g, m, v)


def kernel(x, ev_w_in, ev_g_cq, ev_w_uq, ev_g_ckv, ev_w_ukv, ev_w_out, od_w_qkv, od_rel_bias, od_w_out, g_mix, g_ffn, w_gate, w_up, w_down, g_final, loss_target, m_ev_w_in, m_ev_g_cq, m_ev_w_uq, m_ev_g_ckv, m_ev_w_ukv, m_ev_w_out, m_od_w_qkv, m_od_rel_bias, m_od_w_out, m_g_mix, m_g_ffn, m_w_gate, m_w_up, m_w_down, m_g_final, v_ev_w_in, v_ev_g_cq, v_ev_w_uq, v_ev_g_ckv, v_ev_w_ukv, v_ev_w_out, v_od_w_qkv, v_od_rel_bias, v_od_w_out, v_g_mix, v_g_ffn, v_w_gate, v_w_up, v_w_down, v_g_final):
    w = dict(ev_w_in=ev_w_in, ev_g_cq=ev_g_cq, ev_w_uq=ev_w_uq, ev_g_ckv=ev_g_ckv, ev_w_ukv=ev_w_ukv, ev_w_out=ev_w_out,
             od_w_qkv=od_w_qkv, od_rel_bias=od_rel_bias, od_w_out=od_w_out, g_mix=g_mix, g_ffn=g_ffn, w_gate=w_gate,
             w_up=w_up, w_down=w_down, g_final=g_final)
    m = dict(ev_w_in=m_ev_w_in, ev_g_cq=m_ev_g_cq, ev_w_uq=m_ev_w_uq, ev_g_ckv=m_ev_g_ckv, ev_w_ukv=m_ev_w_ukv,
             ev_w_out=m_ev_w_out, od_w_qkv=m_od_w_qkv, od_rel_bias=m_od_rel_bias, od_w_out=m_od_w_out, g_mix=m_g_mix,
             g_ffn=m_g_ffn, w_gate=m_w_gate, w_up=m_w_up, w_down=m_w_down, g_final=m_g_final)
    v = dict(ev_w_in=v_ev_w_in, ev_g_cq=v_ev_g_cq, ev_w_uq=v_ev_w_uq, ev_g_ckv=v_ev_g_ckv, ev_w_ukv=v_ev_w_ukv,
             ev_w_out=v_ev_w_out, od_w_qkv=v_od_w_qkv, od_rel_bias=v_od_rel_bias, od_w_out=v_od_w_out, g_mix=v_g_mix,
             g_ffn=v_g_ffn, w_gate=v_w_gate, w_up=v_w_up, w_down=v_w_down, g_final=v_g_final)
    flat2d = lambda a: a.reshape(-1, a.shape[-1])
    for tree in (w, m, v):
        for n in TRANSPOSED:
            tree[n] = jnp.swapaxes(tree[n], 1, 2)

    pos = jnp.stack([2 * lax.axis_index("x") + lax.axis_index("y"), lax.axis_index("c")]).astype(jnp.int32)

    slots = {part: _cast_into_slot("cast_" + part, w[n], layer, pos) for part, n, layer in GRAD_PARTS
             if part in FIRST_WEIGHTS + NEXT_WEIGHTS}
    rest = [(part, n, layer) for part, n, layer in GRAD_PARTS if part not in slots]

    def cast_rest(carry):
        return dict(zip([part for part, _, _ in rest],
                        _cast_many_into_slots("cast_rest", [(w[n], layer) for _, n, layer in rest], pos, carry)))

    ex = _Exchanges(slots, pos, {n: w[n].shape for n in BIG}, cast_rest)

    loss_local, grad_x, small = _local_step(x[0], loss_target[0], {n: w[n] for n in SMALL}, ex)

    grads = ex.finish()
    delta, new_m, new_v = {}, {}, {}
    small_out, loss = _small_step("small_step", [flat2d(small[n]) for n in SMALL], loss_local,
                                  *([flat2d(t[n]) for n in SMALL] for t in (w, m, v)))
    for tree, outs in zip((grads, delta, new_m, new_v), small_out):
        tree.update({n: o.reshape(w[n].shape) for n, o in zip(SMALL, outs)})

    for n in BIG:
        turn = (lambda a: jnp.swapaxes(a, 1, 2)) if n in ADAMW_TRANSPOSED else (lambda a: a)
        shape = turn(w[n]).shape
        outs = _adamw("adamw_" + n, *(flat2d(turn(a)) for a in (w[n], grads[n], m[n], v[n])))
        delta[n], new_m[n], new_v[n] = (turn(o.reshape(shape)) for o in outs)
    for tree in (grads, delta, new_m, new_v):
        for n in TRANSPOSED:
            tree[n] = jnp.swapaxes(tree[n], 1, 2)

    return (loss[0, 0], grad_x[None], *[grads[n] for n in WEIGHTS], *[delta[n] for n in WEIGHTS],
            *[new_m[n] for n in WEIGHTS], *[new_v[n] for n in WEIGHTS])
```

```python
import functools

import jax
import jax.numpy as jnp
import numpy as np
from jax import lax
from jax.experimental import pallas as pl
from jax.experimental.pallas import tpu as pltpu

F32 = jnp.float32
BF16 = jnp.bfloat16

S = 2048
D = 1024
CHUNK = 64
MLA_H, MLA_NOPE, MLA_ROPE, MLA_V = 8, 64, 32, 64
Q_LORA, KV_LORA = 384, 256
ROPE_THETA = 10000.0
SB_H, SB_DIM = 8, 64
C_H, C_DIM = 16, 64
LEFT_CHUNKS = 8
REL_CLIP = 256
D_FF = 2816
EVEN_IN = 2208
RMS_EPS = 1e-6
ADAM_LR, ADAM_B1, ADAM_B2, ADAM_EPS, ADAM_WD, ADAM_STEP = 0.001, 0.9, 0.999, 1e-08, 0.01, 10

N_CHIPS = 4
FF_SHARD = D_FF // N_CHIPS
SCALE_A = (MLA_NOPE + MLA_ROPE) ** -0.5
SCALE_B = SB_DIM ** -0.5
SCALE_C = C_DIM ** -0.5
NEG = -1e30
LOG2_E = 1.4426950408889634

LANES = 128
MXU_W = 256
VMEM_LIMIT_BYTES = 56 * 1024 * 1024
TM = 512
TQ = 1024
QB = 512
BQ = 256

P_CQ, P_CKV, P_QB, P_KB, P_VB, P_KR = 0, 512, 768, 1280, 1792, 2304
P_IN = 2432
KR_LANE = 64
BAND_W = BQ + LEFT_CHUNKS * CHUNK
BAND_PAD = 512
TOEP_W = 1024


def _params(*sem):
    return pltpu.CompilerParams(dimension_semantics=sem, vmem_limit_bytes=VMEM_LIMIT_BYTES)


MESH = pl.DeviceIdType.MESH
ANY = pl.BlockSpec(memory_space=pl.ANY)


def _position():
    x, y, c = lax.axis_index("x"), lax.axis_index("y"), lax.axis_index("c")
    other_chips = [(1 - x, y), (x, 1 - y), (1 - x, 1 - y)]
    return x, y, c, other_chips


def _half_rows(c, half):
    return pl.ds(pl.multiple_of(c * half, 16), half)


def _remote(ref_src, ref_dst, send, recv, k, device):
    return pltpu.make_async_remote_copy(src_ref=ref_src, dst_ref=ref_dst, send_sem=send.at[k], recv_sem=recv.at[k],
                                        device_id=device, device_id_type=MESH)


class _Carry:
    def __init__(self):
        self.operands, self.aliased, self.fresh = [], [], []
        self.n_sems = 0
        self.starts, self.finishes, self.on_done = [], [], []

    def operand(self, arr, aliased):
        for i, a in enumerate(self.operands):
            if a is arr:
                return i
        self.operands.append(arr)
        self.aliased.append(aliased)
        return len(self.operands) - 1

    def result(self, shape, dtype):
        self.fresh.append(jax.ShapeDtypeStruct(shape, dtype))
        return len(self.fresh) - 1

    def sems(self, k):
        base = self.n_sems
        self.n_sems += k
        return base

    def done(self, results):
        aliased, fresh = results
        for f in self.on_done:
            f(aliased, fresh)


def _carrier_call(body, *, name, grid, in_specs, out_specs, out_shape, args, sem, scratch_shapes=(), carry=None,
                  prefetch=()):
    in_specs, out_specs, out_shape, scratch = list(in_specs), list(out_specs), list(out_shape), list(scratch_shapes)
    n_pre = len(prefetch)

    def call(kernel, in_specs, out_specs, out_shape, scratch, aliases, sem):
        return pl.pallas_call(
            kernel, name=name, out_shape=out_shape, input_output_aliases=aliases, compiler_params=_params(*sem),
            grid_spec=pltpu.PrefetchScalarGridSpec(num_scalar_prefetch=n_pre, grid=grid, in_specs=in_specs,
                                                   out_specs=out_specs, scratch_shapes=scratch))

    if carry is None:
        return list(call(body, in_specs, out_specs, out_shape, scratch, {}, sem)(*prefetch, *args)), None
    ops = carry.operands
    alias_idx = [i for i, a in enumerate(carry.aliased) if a]
    c_shapes = [jax.ShapeDtypeStruct(ops[i].shape, ops[i].dtype) for i in alias_idx] + carry.fresh
    n_in, n_out, n_scr = len(args), len(out_shape), len(scratch)

    def wrapped(*refs):
        pre, refs = refs[:n_pre], refs[n_pre:]
        ins, c_ins = refs[:n_in], refs[n_in:n_in + len(ops)]
        o0 = n_in + len(ops)
        outs, c_outs = refs[o0:o0 + n_out], refs[o0 + n_out:o0 + n_out + len(c_shapes)]
        s0 = o0 + n_out + len(c_shapes)
        scr, send, recv = refs[s0:s0 + n_scr], refs[s0 + n_scr], refs[s0 + n_scr + 1]
        use = list(c_ins)
        for k, i in enumerate(alias_idx):
            use[i] = c_outs[k]
        fresh = c_outs[len(alias_idx):]

        def run(steps):
            for step in steps:
                step(use, fresh, send, recv)

        if not grid:
            run(carry.starts)
            if body is not None:
                body(*pre, *ins, *outs, *scr)
            run(carry.finishes)
            return
        ids = [pl.program_id(a) for a in range(len(grid))]
        first = functools.reduce(jnp.logical_and, [i == 0 for i in ids])
        last = functools.reduce(jnp.logical_and, [i == g - 1 for i, g in zip(ids, grid)])

        @pl.when(first)
        def _():
            run(carry.starts)

        body(*pre, *ins, *outs, *scr)

        @pl.when(last)
        def _():
            run(carry.finishes)

    res = call(wrapped, in_specs + [ANY] * len(ops), out_specs + [ANY] * len(c_shapes), out_shape + c_shapes,
               scratch + [pltpu.SemaphoreType.DMA((carry.n_sems,)), pltpu.SemaphoreType.DMA((carry.n_sems,))],
               {n_pre + n_in + i: n_out + k for k, i in enumerate(alias_idx)},
               ("arbitrary",) * len(grid))(*prefetch, *args, *ops)
    res = list(res)
    c_res = res[n_out:]
    return res[:n_out], ({i: c_res[k] for k, i in enumerate(alias_idx)}, c_res[len(alias_idx):])


_DIMS = {"nn": (((1,), (0,)), ((), ())), "nt": (((1,), (1,)), ((), ())), "tn": (((0,), (0,)), ((), ()))}


def _dot(a, b, kind="nn"):
    return lax.dot_general(a, b, _DIMS[kind], preferred_element_type=F32)


def _iota(shape, dim):
    return lax.broadcasted_iota(jnp.int32, shape, dim)


def _sigmoid(x):
    return 1.0 / (1.0 + jnp.exp(-x))


def _split_dot(x, tri):
    hi = x.astype(BF16)
    lo = (x - hi.astype(F32)).astype(BF16)
    both = _dot(jnp.concatenate([hi, lo], axis=0), tri)
    return both[:x.shape[0]] + both[x.shape[0]:]


def _running_sum(x, tri, reverse):
    n = x.shape[1] // MXU_W
    blocks = [x[:, b * MXU_W:(b + 1) * MXU_W] for b in range(n)]
    out = [None] * n
    carry = None
    for b in (range(n - 1, -1, -1) if reverse else range(n)):
        part = _split_dot(blocks[b], tri)
        out[b] = part if carry is None else part + carry
        total = jnp.sum(blocks[b], axis=-1, keepdims=True)
        carry = total if carry is None else carry + total
    return (jnp.concatenate(out, axis=1) if n > 1 else out[0]), carry


def _mm(name, a, b, *, kind, grid, a_spec, b_spec, o_spec, out_shape, out_dtype, acc_shape, resid=None, r_spec=None,
        carry=None):
    nk = grid[-1]
    has_r = resid is not None
    several = lambda x: list(x) if isinstance(x, (tuple, list)) else [x]
    a_specs, b_specs = several(a_spec), several(b_spec)
    na, nb = len(a_specs), len(b_specs)
    a_args = several(a) if isinstance(a, (tuple, list)) else [a] * na
    b_args = several(b) if isinstance(b, (tuple, list)) else [b] * nb

    def body(*refs):
        r_ref = refs[na + nb] if has_r else None
        o_ref = refs[na + nb + has_r]
        side_by_side = lambda rs: rs[0][...] if len(rs) == 1 else jnp.concatenate([r[...].astype(BF16) for r in rs], axis=1)
        part = _dot(side_by_side(refs[:na]).astype(BF16), side_by_side(refs[na:na + nb]).astype(BF16), kind)

        def finish(total):
            if has_r:
                total = total + r_ref[...].astype(F32)
            o_ref[...] = total.astype(out_dtype)

        if nk == 1:
            finish(part)
        else:
            acc_ref = refs[na + nb + has_r + 1]
            k = pl.program_id(len(grid) - 1)

            @pl.when(k == 0)
            def _():
                acc_ref[...] = part

            @pl.when(k > 0)
            def _():
                acc_ref[...] += part

            @pl.when(k == nk - 1)
            def _():
                finish(acc_ref[...])

    in_specs = a_specs + b_specs + ([r_spec] if has_r else [])
    args = (*a_args, *b_args) + ((resid,) if has_r else ())
    sem = ("parallel",) * (len(grid) - 1) + ("arbitrary",)
    res, copies = _carrier_call(
        body, name=name, grid=grid, in_specs=in_specs, out_specs=[o_spec],
        out_shape=[jax.ShapeDtypeStruct(out_shape, out_dtype)],
        scratch_shapes=[pltpu.VMEM(acc_shape, F32)] if nk > 1 else [], args=args, sem=sem, carry=carry)
    if carry is not None:
        carry.done(copies)
    return res[0]


def _rms_fwd(name, x, g, col_block=0):
    c = g.shape[1]

    def body(x_ref, g_ref, u_ref):
        xv = x_ref[...]
        r = lax.rsqrt(jnp.mean(xv * xv, axis=-1, keepdims=True) + RMS_EPS)
        u_ref[...] = (xv * r * g_ref[...]).astype(BF16)

    return pl.pallas_call(
        body, name=name, grid=(S // TM,),
        in_specs=[pl.BlockSpec((TM, c), lambda i: (i, col_block)), pl.BlockSpec((1, c), lambda i: (0, 0))],
        out_specs=pl.BlockSpec((TM, c), lambda i: (i, 0)),
        out_shape=jax.ShapeDtypeStruct((S, c), BF16),
        compiler_params=_params("parallel"),
    )(x, g)


def _rms_bwd(name, dy, x, g, resid, carry=None):
    def body(dy_ref, x_ref, g_ref, r_ref, dx_ref, dg_ref):
        i = pl.program_id(0)
        xv = x_ref[...]
        r = lax.rsqrt(jnp.mean(xv * xv, axis=-1, keepdims=True) + RMS_EPS)
        xh = xv * r
        dyv = dy_ref[...]
        dxh = dyv * g_ref[...]
        dx_ref[...] = r_ref[...] + r * (dxh - xh * jnp.mean(dxh * xh, axis=-1, keepdims=True))
        part = jnp.sum(dyv * xh, axis=0, keepdims=True)

        @pl.when(i == 0)
        def _():
            dg_ref[...] = part

        @pl.when(i > 0)
        def _():
            dg_ref[...] += part

    row = pl.BlockSpec((TM, D), lambda i: (i, 0))
    vec = pl.BlockSpec((1, D), lambda i: (0, 0))
    res, copies = _carrier_call(
        body, name=name, grid=(S // TM,), in_specs=[row, row, vec, row], out_specs=[row, vec],
        out_shape=[jax.ShapeDtypeStruct((S, D), F32), jax.ShapeDtypeStruct((1, D), F32)],
        args=(dy, x, g, resid), sem=("arbitrary",), carry=carry)
    if carry is not None:
        carry.done(copies)
    return res


def _loss_bwd(name, h, g, tgt):
    def body(h_ref, g_ref, t_ref, loss_ref, dh_ref, dg_ref):
        i = pl.program_id(0)
        xv = h_ref[...]
        gv = g_ref[...]
        r = lax.rsqrt(jnp.mean(xv * xv, axis=-1, keepdims=True) + RMS_EPS)
        xh = xv * r
        diff = xh * gv - t_ref[...]
        part_loss = 0.5 * jnp.sum(jnp.sum(diff * diff, axis=-1, keepdims=True) * (1.0 / D), axis=0, keepdims=True)
        dy = diff * (1.0 / D)
        dxh = dy * gv
        dh_ref[...] = r * (dxh - xh * jnp.mean(dxh * xh, axis=-1, keepdims=True))
        part_g = jnp.sum(dy * xh, axis=0, keepdims=True)

        @pl.when(i == 0)
        def _():
            dg_ref[...] = part_g
            loss_ref[...] = jnp.broadcast_to(part_loss, (1, LANES))

        @pl.when(i > 0)
        def _():
            dg_ref[...] += part_g
            loss_ref[...] += jnp.broadcast_to(part_loss, (1, LANES))

    row = pl.BlockSpec((TM, D), lambda i: (i, 0))
    vec = pl.BlockSpec((1, D), lambda i: (0, 0))
    return pl.pallas_call(
        body, name=name, grid=(S // TM,), in_specs=[row, vec, row],
        out_specs=[pl.BlockSpec((1, LANES), lambda i: (0, 0)), row, vec],
        out_shape=[jax.ShapeDtypeStruct((1, LANES), F32), jax.ShapeDtypeStruct((S, D), F32),
                   jax.ShapeDtypeStruct((1, D), F32)],
        compiler_params=_params("arbitrary"),
    )(h, g, tgt)


def _ffn_fwd(name, h, g, wg, wu, wd, carry=None):
    def body(h_ref, g_ref, wg_ref, wu_ref, wd_ref, o_ref, gate_ref, up_ref, u_scr):
        s = pl.program_id(1)

        @pl.when(s == 0)
        def _():
            xv = h_ref[...]
            r = lax.rsqrt(jnp.mean(xv * xv, axis=-1, keepdims=True) + RMS_EPS)
            u_scr[...] = (xv * r * g_ref[...]).astype(BF16)
            o_ref[...] = xv

        u = u_scr[...]
        gate = _dot(u, wg_ref[...], "nt")
        up = _dot(u, wu_ref[...], "nt")
        act = gate * _sigmoid(gate) * up
        o_ref[...] += _dot(act.astype(BF16), wd_ref[...])
        gate_ref[...] = gate.astype(BF16)
        up_ref[...] = up.astype(BF16)

    row = pl.BlockSpec((TM, D), lambda i, s: (i, 0))
    hid = pl.BlockSpec((None, TM, FF_SHARD), lambda i, s: (s, i, 0))
    return _carrier_call(
        body, name=name, grid=(S // TM, N_CHIPS),
        in_specs=[row, pl.BlockSpec((1, D), lambda i, s: (0, 0))]
        + [pl.BlockSpec((None, FF_SHARD, D), lambda i, s: (s, 0, 0))] * 3,
        out_specs=[row, hid, hid],
        out_shape=[jax.ShapeDtypeStruct((S, D), F32), jax.ShapeDtypeStruct((N_CHIPS, S, FF_SHARD), BF16),
                   jax.ShapeDtypeStruct((N_CHIPS, S, FF_SHARD), BF16)],
        scratch_shapes=[pltpu.VMEM((TM, D), BF16)], args=(h, g, wg, wu, wd), sem=("parallel", "arbitrary"), carry=carry)


def _ffn_bwd(name, dh, h, g, gate, up, wg, wu, wd):
    def body(dh_ref, h_ref, g_ref, gate_ref, up_ref, wg_ref, wu_ref, wd_ref,
             dhin_ref, dg_ref, u_ref, dgate_ref, dup_ref, act_ref, dhb_scr, du_scr):
        i = pl.program_id(0)
        s = pl.program_id(1)

        @pl.when(s == 0)
        def _():
            xv = h_ref[...]
            r = lax.rsqrt(jnp.mean(xv * xv, axis=-1, keepdims=True) + RMS_EPS)
            u_ref[...] = (xv * r * g_ref[...]).astype(BF16)
            dhb_scr[...] = dh_ref[...].astype(BF16)
            du_scr[...] = jnp.zeros_like(du_scr)

        dact = _dot(dhb_scr[...], wd_ref[...], "nt")
        gv = gate_ref[...].astype(F32)
        uv = up_ref[...].astype(F32)
        sig = _sigmoid(gv)
        sil = gv * sig
        dup = dact * sil
        dgate = dact * uv * (sig * (1.0 + gv * (1.0 - sig)))
        dgb = dgate.astype(BF16)
        dub = dup.astype(BF16)
        act_ref[...] = (sil * uv).astype(BF16)
        dgate_ref[...] = dgb
        dup_ref[...] = dub
        du_scr[...] += _dot(dgb, wg_ref[...]) + _dot(dub, wu_ref[...])

        @pl.when(s == N_CHIPS - 1)
        def _():
            xv = h_ref[...]
            r = lax.rsqrt(jnp.mean(xv * xv, axis=-1, keepdims=True) + RMS_EPS)
            xh = xv * r
            du = du_scr[...]
            dxh = du * g_ref[...]
            dhin_ref[...] = dh_ref[...] + r * (dxh - xh * jnp.mean(dxh * xh, axis=-1, keepdims=True))
            part = jnp.sum(du * xh, axis=0, keepdims=True)

            @pl.when(i == 0)
            def _():
                dg_ref[...] = part

            @pl.when(i > 0)
            def _():
                dg_ref[...] += part

    row = pl.BlockSpec((TM, D), lambda i, s: (i, 0))
    vec = pl.BlockSpec((1, D), lambda i, s: (0, 0))
    hid = pl.BlockSpec((None, TM, FF_SHARD), lambda i, s: (s, i, 0))
    hid_shape = jax.ShapeDtypeStruct((N_CHIPS, S, FF_SHARD), BF16)
    return pl.pallas_call(
        body, name=name, grid=(S // TM, N_CHIPS),
        in_specs=[row, row, vec, hid, hid] + [pl.BlockSpec((None, FF_SHARD, D), lambda i, s: (s, 0, 0))] * 3,
        out_specs=[row, vec, row, hid, hid, hid],
        out_shape=[jax.ShapeDtypeStruct((S, D), F32), jax.ShapeDtypeStruct((1, D), F32),
                   jax.ShapeDtypeStruct((S, D), BF16), hid_shape, hid_shape, hid_shape],
        scratch_shapes=[pltpu.VMEM((TM, D), BF16), pltpu.VMEM((TM, D), F32)],
        compiler_params=_params("arbitrary", "arbitrary"),
    )(dh, h, g, gate, up, wg, wu, wd)


def _ffn_wgrads(name, u, dgate, dup, act, dh):
    nk = S // TM

    def body(u_ref, dh_ref, dgate_ref, dup_ref, act_ref, dg_ref, du_ref, dd_ref, acc_g, acc_u, acc_d):
        k = pl.program_id(1)
        u = u_ref[...]
        parts = (_dot(dgate_ref[...], u, "tn"), _dot(dup_ref[...], u, "tn"),
                 _dot(act_ref[...], dh_ref[...].astype(BF16), "tn"))
        accs = (acc_g, acc_u, acc_d)

        @pl.when(k == 0)
        def _():
            for acc, part in zip(accs, parts):
                acc[...] = part

        @pl.when(k > 0)
        def _():
            for acc, part in zip(accs, parts):
                acc[...] += part

        @pl.when(k == nk - 1)
        def _():
            for out, acc in zip((dg_ref, du_ref, dd_ref), accs):
                out[...] = acc[...].astype(BF16)

    tok = pl.BlockSpec((TM, D), lambda s, k: (k, 0))
    hid = pl.BlockSpec((None, TM, FF_SHARD), lambda s, k: (s, k, 0))
    out = pl.BlockSpec((None, FF_SHARD, D), lambda s, k: (s, 0, 0))
    shape = jax.ShapeDtypeStruct((N_CHIPS, FF_SHARD, D), BF16)
    return pl.pallas_call(
        body, name=name, grid=(N_CHIPS, nk), in_specs=[tok, tok, hid, hid, hid], out_specs=[out, out, out],
        out_shape=[shape, shape, shape], scratch_shapes=[pltpu.VMEM((FF_SHARD, D), F32)] * 3,
        compiler_params=_params("parallel", "arbitrary"))(u, dh, dgate, dup, act)


def _rope_tables():
    pos = jnp.arange(S, dtype=F32)
    inv = ROPE_THETA ** (-jnp.arange(0, MLA_ROPE, 2, dtype=F32) / MLA_ROPE)
    ang = pos[:, None] * inv[None, :]
    half = MLA_ROPE // 2
    cos = jnp.cos(ang)
    sin = jnp.sin(ang)
    one = jnp.ones((S, KR_LANE), F32)
    zero = jnp.zeros((S, KR_LANE), F32)
    tail_one = jnp.ones((S, LANES - KR_LANE - MLA_ROPE), F32)
    tail_zero = jnp.zeros((S, LANES - KR_LANE - MLA_ROPE), F32)
    cos_t = jnp.concatenate([one, cos, cos, tail_one], axis=1)
    sin_t = jnp.concatenate([zero, -sin, sin, tail_zero], axis=1)
    assert cos_t.shape == (S, LANES) and half * 2 == MLA_ROPE
    return cos_t, sin_t


def _rope(x, cos_t, sin_t, sign):
    n = x.shape[1] // LANES
    half = MLA_ROPE // 2
    lane = _iota(x.shape, 1) & (LANES - 1)
    first = (lane >= KR_LANE) & (lane < KR_LANE + half)
    swapped = jnp.where(first, pltpu.roll(x, x.shape[1] - half, 1), pltpu.roll(x, half, 1))
    c = jnp.tile(cos_t, (1, n)) if n > 1 else cos_t
    s = jnp.tile(sin_t, (1, n)) if n > 1 else sin_t
    return x * c + swapped * (s * sign)


def _mla_prep_fwd(name, proj, g_cq, g_ckv, w_uq, w_uk, w_uv, cos_t, sin_t):
    nh = MLA_H * LANES

    def body(cq_ref, ckv_ref, kr_ref, gq_ref, gkv_ref, wq_ref, wk_ref, wv_ref, cos_ref, sin_ref,
             qa_ref, ka_ref, va_ref):
        cos_v, sin_v = cos_ref[...], sin_ref[...]
        cq = cq_ref[...]
        r = lax.rsqrt(jnp.mean(cq * cq, axis=-1, keepdims=True) + RMS_EPS)
        cqn = (cq * r * gq_ref[...]).astype(BF16)
        qa_ref[...] = _rope(_dot(cqn, wq_ref[...]), cos_v, sin_v, 1.0).astype(BF16)
        ckv = ckv_ref[...]
        r = lax.rsqrt(jnp.mean(ckv * ckv, axis=-1, keepdims=True) + RMS_EPS)
        ckvn = (ckv * r * gkv_ref[...]).astype(BF16)
        lane = _iota((TM, LANES), 1)
        rot = (lane >= KR_LANE) & (lane < KR_LANE + MLA_ROPE)
        kr = jnp.where(rot, _rope(kr_ref[...], cos_v, sin_v, 1.0), 0.0)
        ka_ref[...] = (_dot(ckvn, wk_ref[...]) + jnp.tile(kr, (1, MLA_H))).astype(BF16)
        va_ref[...] = _dot(ckvn, wv_ref[...]).astype(BF16)

    full = lambda shape: pl.BlockSpec(shape, lambda i: (0, 0))
    return pl.pallas_call(
        body, name=name, grid=(S // TM,),
        in_specs=[pl.BlockSpec((TM, Q_LORA), lambda i: (i, P_CQ // Q_LORA)),
                  pl.BlockSpec((TM, KV_LORA), lambda i: (i, P_CKV // KV_LORA)),
                  pl.BlockSpec((TM, LANES), lambda i: (i, P_KR // LANES)),
                  full((1, Q_LORA)), full((1, KV_LORA)), full((Q_LORA, nh)), full((KV_LORA, nh)),
                  full((KV_LORA, MLA_H * MLA_V)),
                  pl.BlockSpec((TM, LANES), lambda i: (i, 0)), pl.BlockSpec((TM, LANES), lambda i: (i, 0))],
        out_specs=[pl.BlockSpec((TM, nh), lambda i: (i, 0)), pl.BlockSpec((TM, nh), lambda i: (i, 0)),
                   pl.BlockSpec((TM, MLA_H * MLA_V), lambda i: (i, 0))],
        out_shape=[jax.ShapeDtypeStruct((S, nh), BF16), jax.ShapeDtypeStruct((S, nh), BF16),
                   jax.ShapeDtypeStruct((S, MLA_H * MLA_V), BF16)],
        compiler_params=_params("parallel"),
    )(proj, proj, proj, g_cq, g_ckv, w_uq, w_uk, w_uv, cos_t, sin_t)


def _mla_prep_bwd(name, dqa, dka, dva, proj, g_cq, g_ckv, w_uq, w_uk, w_uv, cos_t, sin_t):
    nh = MLA_H * LANES

    def body(dqa_ref, dka_ref, dva_ref, cq_ref, ckv_ref, gq_ref, gkv_ref, wq_ref, wk_ref, wv_ref, cos_ref, sin_ref,
             dcq_ref, dckv_ref, dkr_ref, dwq_ref, dwk_ref, dwv_ref, dgq_ref, dgkv_ref):
        i = pl.program_id(0)
        cos_v, sin_v = cos_ref[...], sin_ref[...]

        def norm_bwd(x, g, dn):
            r = lax.rsqrt(jnp.mean(x * x, axis=-1, keepdims=True) + RMS_EPS)
            xh = x * r
            dxh = dn * g
            dx = r * (dxh - xh * jnp.mean(dxh * xh, axis=-1, keepdims=True))
            return dx, jnp.sum(dn * xh, axis=0, keepdims=True), (xh * g).astype(BF16)

        dq = _rope(dqa_ref[...], cos_v, sin_v, -1.0).astype(BF16)
        dcqn = _dot(dq, wq_ref[...], "nt")
        dcq, dgq, cqn = norm_bwd(cq_ref[...], gq_ref[...], dcqn)
        dcq_ref[...] = dcq.astype(BF16)
        dwq = _dot(cqn, dq, "tn")

        dka = dka_ref[...]
        dkab = dka.astype(BF16)
        dvab = dva_ref[...].astype(BF16)
        dckvn = _dot(dkab, wk_ref[...], "nt") + _dot(dvab, wv_ref[...], "nt")
        dckv, dgkv, ckvn = norm_bwd(ckv_ref[...], gkv_ref[...], dckvn)
        dckv_ref[...] = dckv.astype(BF16)
        dwk = _dot(ckvn, dkab, "tn")
        dwv = _dot(ckvn, dvab, "tn")

        fold = dka[:, 0:LANES]
        for hh in range(1, MLA_H):
            fold = fold + dka[:, hh * LANES:(hh + 1) * LANES]
        lane = _iota((TM, LANES), 1)
        rot = (lane >= KR_LANE) & (lane < KR_LANE + MLA_ROPE)
        dkr = _rope(jnp.where(rot, fold, 0.0), cos_v, sin_v, -1.0)
        dkr_ref[...] = jnp.where(rot, dkr, 0.0).astype(BF16)

        @pl.when(i == 0)
        def _():
            dwq_ref[...] = dwq
            dwk_ref[...] = dwk
            dwv_ref[...] = dwv
            dgq_ref[...] = dgq
            dgkv_ref[...] = dgkv

        @pl.when(i > 0)
        def _():
            dwq_ref[...] += dwq
            dwk_ref[...] += dwk
            dwv_ref[...] += dwv
            dgq_ref[...] += dgq
            dgkv_ref[...] += dgkv

    full = lambda shape: pl.BlockSpec(shape, lambda i: (0, 0))
    rows = lambda c: pl.BlockSpec((TM, c), lambda i: (i, 0))
    nv = MLA_H * MLA_V
    return pl.pallas_call(
        body, name=name, grid=(S // TM,),
        in_specs=[rows(nh), rows(nh), rows(nv),
                  pl.BlockSpec((TM, Q_LORA), lambda i: (i, P_CQ // Q_LORA)),
                  pl.BlockSpec((TM, KV_LORA), lambda i: (i, P_CKV // KV_LORA)),
                  full((1, Q_LORA)), full((1, KV_LORA)), full((Q_LORA, nh)), full((KV_LORA, nh)), full((KV_LORA, nv)),
                  rows(LANES), rows(LANES)],
        out_specs=[rows(Q_LORA), rows(KV_LORA), rows(LANES), full((Q_LORA, nh)), full((KV_LORA, nh)),
                   full((KV_LORA, nv)), full((1, Q_LORA)), full((1, KV_LORA))],
        out_shape=[jax.ShapeDtypeStruct((S, Q_LORA), BF16), jax.ShapeDtypeStruct((S, KV_LORA), BF16),
                   jax.ShapeDtypeStruct((S, LANES), BF16), jax.ShapeDtypeStruct((Q_LORA, nh), F32),
                   jax.ShapeDtypeStruct((KV_LORA, nh), F32), jax.ShapeDtypeStruct((KV_LORA, nv), F32),
                   jax.ShapeDtypeStruct((1, Q_LORA), F32), jax.ShapeDtypeStruct((1, KV_LORA), F32)],
        compiler_params=_params("arbitrary"),
    )(dqa, dka, dva, proj, proj, g_cq, g_ckv, w_uq, w_uk, w_uv, cos_t, sin_t)


def _head_masks(dtype):
    lane = _iota((1, LANES), 1)
    return (lane < 64).astype(dtype), (lane >= 64).astype(dtype)


def _mla_fwd(name, qa, ka, va, carry=None):
    def body(q_ref, k_ref, v_ref, o_ref, lse_ref):
        m0b, m1b = _head_masks(BF16)
        lane = _iota((QB, LANES), 1)
        left = lane < 64

        def qblock(i, _):
            r0 = pl.multiple_of(i * QB, QB)
            qs = [q_ref[pl.ds(r0, QB), hh * LANES:(hh + 1) * LANES] for hh in range(2)]
            rowc = lax.shift_right_logical(r0 + _iota((QB, QB), 0), 6)

            def kv(kb, carry):
                ms, ls, acc = carry
                c0 = pl.multiple_of(kb * QB, QB)
                v = v_ref[pl.ds(c0, QB), :]
                ok = lax.shift_right_logical(c0 + _iota((QB, QB), 1), 6) <= rowc
                new_m, new_l, alphas = [], [], []
                pv = None
                for hh in range(2):
                    k = k_ref[pl.ds(c0, QB), hh * LANES:(hh + 1) * LANES]
                    s = jnp.where(ok, _dot(qs[hh], k, "nt") * (SCALE_A * LOG2_E), NEG)
                    mn = jnp.maximum(ms[hh], jnp.max(s, axis=-1, keepdims=True))
                    p = jnp.exp2(s - mn)
                    a = jnp.exp2(ms[hh] - mn)
                    new_m.append(mn)
                    new_l.append(a * ls[hh] + jnp.sum(p, axis=-1, keepdims=True))
                    alphas.append(a)
                    part = _dot(p.astype(BF16), v * (m0b if hh == 0 else m1b))
                    pv = part if pv is None else pv + part
                acc = acc * jnp.where(left, alphas[0], alphas[1]) + pv
                return tuple(new_m), tuple(new_l), acc

            init = ((jnp.full((QB, 1), NEG, F32),) * 2, (jnp.zeros((QB, 1), F32),) * 2, jnp.zeros((QB, LANES), F32))
            ms, ls, acc = lax.fori_loop(0, i + 1, kv, init)
            o_ref[pl.ds(r0, QB), :] = acc * jnp.where(left, 1.0 / ls[0], 1.0 / ls[1])
            lse_ref[pl.ds(r0, QB), :] = jnp.where(left, ms[0] + jnp.log(ls[0]) * LOG2_E, ms[1] + jnp.log(ls[1]) * LOG2_E)
            return 0

        lax.fori_loop(0, S // QB, qblock, 0)

    pair = lambda w: pl.BlockSpec((S, w), lambda p: (0, p))
    return _carrier_call(
        body, name=name, grid=(MLA_H // 2,), in_specs=[pair(2 * LANES), pair(2 * LANES), pair(LANES)],
        out_specs=[pair(LANES), pair(LANES)],
        out_shape=[jax.ShapeDtypeStruct((S, MLA_H * MLA_V), F32), jax.ShapeDtypeStruct((S, MLA_H * MLA_V), F32)],
        args=(qa, ka, va), sem=("parallel",), carry=carry)


def _mla_bwd(name, qa, ka, va, o, lse, do, do_block0, carry=None):
    def body(q_ref, k_ref, v_ref, o_ref, lse_ref, do_ref, dq_ref, dk_ref, dv_ref):
        m0f, m1f = _head_masks(F32)
        m0b, m1b = _head_masks(BF16)
        dk_ref[...] = jnp.zeros_like(dk_ref)
        dv_ref[...] = jnp.zeros_like(dv_ref)

        def qblock(i, _):
            r0 = pl.multiple_of(i * QB, QB)
            rows = pl.ds(r0, QB)
            do_f = do_ref[rows, :]
            prod = do_f * o_ref[rows, :]
            deltas = [jnp.sum(prod * m0f, axis=-1, keepdims=True), jnp.sum(prod * m1f, axis=-1, keepdims=True)]
            lse_v = lse_ref[rows, :]
            lses = [lse_v[:, 0:1], lse_v[:, 64:65]]
            dob = do_f.astype(BF16)
            dos = [dob * m0b, dob * m1b]
            qs = [q_ref[rows, hh * LANES:(hh + 1) * LANES] for hh in range(2)]
            rowc = lax.shift_right_logical(r0 + _iota((QB, QB), 0), 6)

            def kv(kb, dqs):
                c0 = pl.multiple_of(kb * QB, QB)
                cols = pl.ds(c0, QB)
                v = v_ref[cols, :]
                ok = lax.shift_right_logical(c0 + _iota((QB, QB), 1), 6) <= rowc
                out = []
                dv = None
                for hh in range(2):
                    k = k_ref[cols, hh * LANES:(hh + 1) * LANES]
                    s = _dot(qs[hh], k, "nt") * (SCALE_A * LOG2_E)
                    p = jnp.where(ok, jnp.exp2(s - lses[hh]), 0.0)
                    dp = _dot(dos[hh], v, "nt")
                    ds = (p * (dp - deltas[hh]) * SCALE_A).astype(BF16)
                    out.append(dqs[hh] + _dot(ds, k))
                    dk_ref[cols, hh * LANES:(hh + 1) * LANES] += _dot(ds, qs[hh], "tn")
                    part = _dot(p.astype(BF16), dos[hh], "tn")
                    dv = part if dv is None else dv + part
                dv_ref[cols, :] += dv
                return tuple(out)

            dqs = lax.fori_loop(0, i + 1, kv, (jnp.zeros((QB, LANES), F32),) * 2)
            for hh in range(2):
                dq_ref[rows, hh * LANES:(hh + 1) * LANES] = dqs[hh]
            return 0

        lax.fori_loop(0, S // QB, qblock, 0)

    pair = lambda w: pl.BlockSpec((S, w), lambda p: (0, p))
    return _carrier_call(
        body, name=name, grid=(MLA_H // 2,),
        in_specs=[pair(2 * LANES), pair(2 * LANES), pair(LANES), pair(LANES), pair(LANES),
                  pl.BlockSpec((S, LANES), lambda p: (0, do_block0 + p))],
        out_specs=[pair(2 * LANES), pair(2 * LANES), pair(LANES)],
        out_shape=[jax.ShapeDtypeStruct((S, MLA_H * LANES), F32), jax.ShapeDtypeStruct((S, MLA_H * LANES), F32),
                   jax.ShapeDtypeStruct((S, MLA_H * MLA_V), F32)],
        args=(qa, ka, va, o, lse, do), sem=("parallel",), carry=carry)


def _sb_weights(q_h, k, c, before, tri_suffix):
    z = _dot(q_h, k, "nt") * (SCALE_B * LOG2_E)
    sp = jnp.maximum(z, 0.0) + jnp.log(1.0 + jnp.exp2(-jnp.abs(z))) * LOG2_E
    log_keep = jnp.where(before, -sp, 0.0)
    to_the_right, total = _running_sum(log_keep, tri_suffix, True)
    w = jnp.where(before, jnp.exp2(z - sp + to_the_right + c), 0.0)
    return w, jnp.exp2(z - sp), total


def _sb_fwd(name, proj, carry=None):
    def body(q_ref, k_ref, v_ref, o_ref):
        m0b, m1b = _head_masks(BF16)
        tri_suffix = (_iota((MXU_W, MXU_W), 0) > _iota((MXU_W, MXU_W), 1)).astype(BF16)

        def qblock(i, _):
            r0 = pl.multiple_of(i * QB, QB)
            q = q_ref[pl.ds(r0, QB), :].astype(BF16)
            qs = [q * m0b, q * m1b]
            rowg = r0 + _iota((QB, QB), 0)

            def kv(step, carry):
                cs, acc = carry
                c0 = pl.multiple_of((i - step) * QB, QB)
                k = k_ref[pl.ds(c0, QB), :].astype(BF16)
                v = v_ref[pl.ds(c0, QB), :].astype(BF16)
                before = (c0 + _iota((QB, QB), 1)) < rowg
                new_c = []
                for hh in range(2):
                    w, _, tot = _sb_weights(qs[hh], k, cs[hh], before, tri_suffix)
                    new_c.append(cs[hh] + tot)
                    acc = acc + _dot(w.astype(BF16), v * (m0b if hh == 0 else m1b))
                return tuple(new_c), acc

            init = ((jnp.zeros((QB, 1), F32),) * 2, jnp.zeros((QB, LANES), F32))
            _, acc = lax.fori_loop(0, i + 1, kv, init)
            o_ref[pl.ds(r0, QB), :] = acc.astype(BF16)
            return 0

        lax.fori_loop(0, S // QB, qblock, 0)

    col = lambda base: pl.BlockSpec((S, LANES), lambda p: (0, base // LANES + p))
    return _carrier_call(
        body, name=name, grid=(SB_H // 2,), in_specs=[col(P_QB), col(P_KB), col(P_VB)],
        out_specs=[pl.BlockSpec((S, LANES), lambda p: (0, p))],
        out_shape=[jax.ShapeDtypeStruct((S, SB_H * SB_DIM), BF16)],
        args=(proj, proj, proj), sem=("parallel",), carry=carry)


def _sb_bwd(name, proj, do, do_block0, carry=None):
    nb = S // QB

    def body(q_ref, k_ref, v_ref, do_ref, dq_ref, dk_ref, dv_ref, sig_scr, dl_scr, dk_acc, dv_acc):
        m0b, m1b = _head_masks(BF16)
        tri_suffix = (_iota((MXU_W, MXU_W), 0) > _iota((MXU_W, MXU_W), 1)).astype(BF16)
        tri_prefix = (_iota((MXU_W, MXU_W), 0) < _iota((MXU_W, MXU_W), 1)).astype(BF16)
        dk_acc[...] = jnp.zeros_like(dk_acc)
        dv_acc[...] = jnp.zeros_like(dv_acc)

        def qblock(i, _):
            r0 = pl.multiple_of(i * QB, QB)
            rows = pl.ds(r0, QB)
            q = q_ref[rows, :].astype(BF16)
            qs = [q * m0b, q * m1b]
            dob = do_ref[rows, :].astype(BF16)
            dos = [dob * m0b, dob * m1b]
            rowg = r0 + _iota((QB, QB), 0)

            def sweep_left(step, cs):
                kb = i - step
                c0 = pl.multiple_of(kb * QB, QB)
                cols = pl.ds(c0, QB)
                k = k_ref[cols, :].astype(BF16)
                v = v_ref[cols, :].astype(BF16)
                before = (c0 + _iota((QB, QB), 1)) < rowg
                new_c = []
                dv = None
                for hh in range(2):
                    w, sig, tot = _sb_weights(qs[hh], k, cs[hh], before, tri_suffix)
                    new_c.append(cs[hh] + tot)
                    sig_scr[hh, kb] = sig
                    dl_scr[hh, kb] = _dot(dos[hh], v, "nt") * w
                    part = _dot(w.astype(BF16), dos[hh], "tn")
                    dv = part if dv is None else dv + part
                dv_acc[cols, :] += dv
                return tuple(new_c)

            lax.fori_loop(0, i + 1, sweep_left, (jnp.zeros((QB, 1), F32),) * 2)

            def sweep_right(kb, carry):
                ps, dq = carry
                c0 = pl.multiple_of(kb * QB, QB)
                cols = pl.ds(c0, QB)
                k = k_ref[cols, :].astype(BF16)
                before = (c0 + _iota((QB, QB), 1)) < rowg
                new_p = []
                dk = None
                for hh in range(2):
                    dl = dl_scr[hh, kb]
                    sig = sig_scr[hh, kb]
                    to_the_left, total = _running_sum(dl, tri_prefix, False)
                    earlier = to_the_left + ps[hh]
                    new_p.append(ps[hh] + total)
                    dz = (jnp.where(before, dl * (1.0 - sig) - earlier * sig, 0.0) * SCALE_B).astype(BF16)
                    dq = dq + _dot(dz, k * (m0b if hh == 0 else m1b))
                    part = _dot(dz, qs[hh], "tn")
                    dk = part if dk is None else dk + part
                dk_acc[cols, :] += dk
                return tuple(new_p), dq

            init = ((jnp.zeros((QB, 1), F32),) * 2, jnp.zeros((QB, LANES), F32))
            _, dq = lax.fori_loop(0, i + 1, sweep_right, init)
            dq_ref[rows, :] = dq.astype(BF16)
            return 0

        lax.fori_loop(0, nb, qblock, 0)
        dk_ref[...] = dk_acc[...].astype(BF16)
        dv_ref[...] = dv_acc[...].astype(BF16)

    col = lambda base: pl.BlockSpec((S, LANES), lambda p: (0, base // LANES + p))
    out = pl.BlockSpec((S, LANES), lambda p: (0, p))
    shape = jax.ShapeDtypeStruct((S, SB_H * SB_DIM), BF16)
    return _carrier_call(
        body, name=name, grid=(SB_H // 2,),
        in_specs=[col(P_QB), col(P_KB), col(P_VB), pl.BlockSpec((S, LANES), lambda p: (0, do_block0 + p))],
        out_specs=[out, out, out], out_shape=[shape, shape, shape],
        scratch_shapes=[pltpu.VMEM((2, nb, QB, QB), F32), pltpu.VMEM((2, nb, QB, QB), F32),
                        pltpu.VMEM((S, LANES), F32), pltpu.VMEM((S, LANES), F32)],
        args=(proj, proj, proj, do), sem=("parallel",), carry=carry)


def _band_row_index():
    j = np.arange(TOEP_W)
    rel = np.clip(LEFT_CHUNKS * CHUNK - j, -REL_CLIP, REL_CLIP) + REL_CLIP
    rel[BAND_W:] = 2 * REL_CLIP
    return rel.astype(np.int32)


def _band_tiles(r0_ref, q_ref, kpad, vpad, m, m0b, m1b, static_ok, bias):
    r0 = pl.multiple_of(m * BQ, BQ)
    q = q_ref[0, pl.ds(r0, BQ), :]
    kw = kpad[pl.ds(r0, BAND_W), :]
    vw = vpad[pl.ds(r0, BAND_W), :]
    ok = static_ok & ((r0 - BAND_PAD + _iota((BQ, BAND_W), 1)) >= 0)
    qs = [q * m0b, q * m1b]
    ps = []
    for hh in range(2):
        s = jnp.where(ok, _dot(qs[hh], kw, "nt") * (SCALE_C * LOG2_E) + bias[hh], NEG)
        e = jnp.exp2(s - jnp.max(s, axis=-1, keepdims=True))
        ps.append(e * (1.0 / jnp.sum(e, axis=-1, keepdims=True)))
    return r0, qs, kw, vw, ps


def _band_setup(qkv_ref, r0_ref, kpad, vpad):
    kpad[0:BAND_PAD, :] = jnp.zeros((BAND_PAD, LANES), BF16)
    vpad[0:BAND_PAD, :] = jnp.zeros((BAND_PAD, LANES), BF16)
    kpad[BAND_PAD:, :] = qkv_ref[1]
    vpad[BAND_PAD:, :] = qkv_ref[2]
    jc = lax.shift_right_logical(_iota((BQ, BAND_W), 1), 6)
    rc = lax.shift_right_logical(_iota((BQ, BAND_W), 0), 6)
    static_ok = (jc >= rc) & (jc <= rc + LEFT_CHUNKS)
    bias = []
    for hh in range(2):
        row = jnp.broadcast_to(r0_ref[hh:hh + 1, :] * LOG2_E, (BQ, TOEP_W))
        bias.append(pltpu.roll(row, 0, 1, stride=1, stride_axis=0)[:, :BAND_W])
    return static_ok, bias


def _band_fwd(name, qkv, r0, carry=None):
    def body(qkv_ref, r0_ref, o_ref, kpad, vpad):
        m0b, m1b = _head_masks(BF16)
        static_ok, bias = _band_setup(qkv_ref, r0_ref, kpad, vpad)

        def qblock(m, _):
            r0_, _, _, vw, ps = _band_tiles(r0_ref, qkv_ref, kpad, vpad, m, m0b, m1b, static_ok, bias)
            o = _dot(ps[0].astype(BF16), vw * m0b) + _dot(ps[1].astype(BF16), vw * m1b)
            o_ref[pl.ds(r0_, BQ), :] = o.astype(BF16)
            return 0

        lax.fori_loop(0, S // BQ, qblock, 0)

    return _carrier_call(
        body, name=name, grid=(C_H // 2,),
        in_specs=[pl.BlockSpec((3, S, LANES), lambda p: (0, 0, p)), pl.BlockSpec((None, 2, TOEP_W), lambda p: (p, 0, 0))],
        out_specs=[pl.BlockSpec((S, LANES), lambda p: (0, p))],
        out_shape=[jax.ShapeDtypeStruct((S, C_H * C_DIM), BF16)],
        scratch_shapes=[pltpu.VMEM((S + BAND_PAD, LANES), BF16), pltpu.VMEM((S + BAND_PAD, LANES), BF16)],
        args=(qkv, r0), sem=("parallel",), carry=carry)


def _band_bwd(name, qkv, r0, do, carry=None):
    def body(qkv_ref, r0_ref, do_ref, dqkv_ref, dr0_ref, kpad, vpad, dkpad, dvpad, db_acc):
        m0b, m1b = _head_masks(BF16)
        static_ok, bias = _band_setup(qkv_ref, r0_ref, kpad, vpad)
        dkpad[...] = jnp.zeros_like(dkpad)
        dvpad[...] = jnp.zeros_like(dvpad)
        db_acc[...] = jnp.zeros_like(db_acc)

        def qblock(m, _):
            r0_, qs, kw, vw, ps = _band_tiles(r0_ref, qkv_ref, kpad, vpad, m, m0b, m1b, static_ok, bias)
            dob = do_ref[pl.ds(r0_, BQ), :].astype(BF16)
            dos = [dob * m0b, dob * m1b]
            dq = None
            dk = None
            dv = None
            for hh in range(2):
                p = ps[hh]
                dp = _dot(dos[hh], vw, "nt")
                ds = p * (dp - jnp.sum(dp * p, axis=-1, keepdims=True))
                db_acc[hh, :, 0:BAND_W] += ds
                dsb = (ds * SCALE_C).astype(BF16)
                t = _dot(dsb, kw * (m0b if hh == 0 else m1b))
                dq = t if dq is None else dq + t
                t = _dot(dsb, qs[hh], "tn")
                dk = t if dk is None else dk + t
                t = _dot(p.astype(BF16), dos[hh], "tn")
                dv = t if dv is None else dv + t
            dqkv_ref[0, pl.ds(r0_, BQ), :] = dq.astype(BF16)
            dkpad[pl.ds(r0_, BAND_W), :] += dk
            dvpad[pl.ds(r0_, BAND_W), :] += dv
            return 0

        lax.fori_loop(0, S // BQ, qblock, 0)
        dqkv_ref[1] = dkpad[BAND_PAD:, :].astype(BF16)
        dqkv_ref[2] = dvpad[BAND_PAD:, :].astype(BF16)
        sub = _iota((8, TOEP_W), 0)
        for hh in range(2):
            folded = db_acc[hh, 0:8, :]
            for a in range(1, BQ // 8):
                folded = folded + pltpu.roll(db_acc[hh, 8 * a:8 * a + 8, :], TOEP_W - 8 * a, 1)
            for bit in range(3):
                moved = pltpu.roll(folded, TOEP_W - (1 << bit), 1)
                folded = jnp.where((sub & (1 << bit)) != 0, moved, folded)
            dr0_ref[hh:hh + 1, :] = jnp.sum(folded, axis=0, keepdims=True)

    return _carrier_call(
        body, name=name, grid=(C_H // 2,),
        in_specs=[pl.BlockSpec((3, S, LANES), lambda p: (0, 0, p)), pl.BlockSpec((None, 2, TOEP_W), lambda p: (p, 0, 0)),
                  pl.BlockSpec((S, LANES), lambda p: (0, p))],
        out_specs=[pl.BlockSpec((3, S, LANES), lambda p: (0, 0, p)), pl.BlockSpec((None, 2, TOEP_W), lambda p: (p, 0, 0))],
        out_shape=[jax.ShapeDtypeStruct((3, S, C_H * C_DIM), BF16), jax.ShapeDtypeStruct((C_H // 2, 2, TOEP_W), F32)],
        scratch_shapes=[pltpu.VMEM((S + BAND_PAD, LANES), BF16), pltpu.VMEM((S + BAND_PAD, LANES), BF16),
                        pltpu.VMEM((S + BAND_PAD, LANES), F32), pltpu.VMEM((S + BAND_PAD, LANES), F32),
                        pltpu.VMEM((2, BQ, TOEP_W), F32)],
        args=(qkv, r0, do), sem=("parallel",), carry=carry)


def _bias_table_grad(name, dr0):
    w_out = 5 * LANES

    def body(d_ref, o_ref):
        j = _iota((TOEP_W, w_out), 0)
        rel = jnp.clip(LEFT_CHUNKS * CHUNK - j, -REL_CLIP, REL_CLIP) + REL_CLIP
        rel = jnp.where(j >= BAND_W, 2 * REL_CLIP, rel)
        onehot = (rel == _iota((TOEP_W, w_out), 1)).astype(BF16)
        d = d_ref[...]
        hi = d.astype(BF16)
        mid = (d - hi.astype(F32))
        mid_b = mid.astype(BF16)
        lo = (mid - mid_b.astype(F32)).astype(BF16)
        o_ref[...] = _dot(hi, onehot) + _dot(mid_b, onehot) + _dot(lo, onehot)

    return pl.pallas_call(
        body, name=name, out_shape=jax.ShapeDtypeStruct((C_H, w_out), F32),
        in_specs=[pl.BlockSpec((C_H, TOEP_W), lambda: (0, 0))], out_specs=pl.BlockSpec((C_H, w_out), lambda: (0, 0)),
        grid=(),
    )(dr0)


def _carry_gather(cy, slots, names, ici, d2d):
    idx = [cy.operand(slots[n], True) for n in names]
    n = len(names)
    base_i = cy.sems(3 * n) if ici else 0
    base_d = cy.sems(3 * n) if d2d else 0

    def piece(refs, t, slot, cc):
        return refs[idx[t]].at[slot, _half_rows(cc, slots[names[t]].shape[1] // 2), :]

    def over_ici(refs, send, recv, arriving):
        x, y, c, chips = _position()
        out = []
        for t in range(n):
            for j in range(3):
                r = piece(refs, t, 2 * chips[j][0] + chips[j][1] if arriving else 2 * x + y, c)
                out.append(_remote(r, r, send, recv, base_i + 3 * t + j, (*chips[j], c)))
        return out

    def over_d2d(refs, send, recv, arriving):
        x, y, c, chips = _position()
        out = []
        for t in range(n):
            for j in range(3):
                r = piece(refs, t, 2 * chips[j][0] + chips[j][1], 1 - c if arriving else c)
                out.append(_remote(r, r, send, recv, base_d + 3 * t + j, (x, y, 1 - c)))
        return out

    def start_ici(refs, fresh, send, recv):
        for cp in over_ici(refs, send, recv, False):
            cp.start()

    def wait_ici(refs, fresh, send, recv):
        for cp in over_ici(refs, send, recv, True):
            cp.wait_recv()
        for cp in over_ici(refs, send, recv, False):
            cp.wait_send()

    def start_d2d(refs, fresh, send, recv):
        for cp in over_d2d(refs, send, recv, False):
            cp.start()

    def wait_d2d(refs, fresh, send, recv):
        for cp in over_d2d(refs, send, recv, True):
            cp.wait_recv()
        for cp in over_d2d(refs, send, recv, False):
            cp.wait_send()

    def wait_ici_and_forward(refs, fresh, send, recv):
        forwards = over_d2d(refs, send, recv, False)
        for k, cp in enumerate(over_ici(refs, send, recv, True)):
            cp.wait_recv()
            forwards[k].start()
        for cp in over_ici(refs, send, recv, False):
            cp.wait_send()

    if ici and d2d:
        cy.starts.append(start_ici)
        cy.finishes += [wait_ici_and_forward, wait_d2d]
    elif ici:
        cy.starts.append(start_ici)
        cy.finishes.append(wait_ici)
    else:
        cy.starts.append(start_d2d)
        cy.finishes.append(wait_d2d)

    def done(aliased, fresh):
        for t, name in enumerate(names):
            slots[name] = aliased[idx[t]]

    cy.on_done.append(done)


def _carry_chip_exchange(cy, sums, got, names):
    idx = [cy.operand(sums[n], False) for n in names]
    out = [cy.result((3,) + sums[n].shape[1:], BF16) for n in names]
    base = cy.sems(3 * len(names))

    def copies(refs, fresh, send, recv):
        x, y, c, chips = _position()
        return [_remote(refs[idx[t]].at[2 * chips[j][0] + chips[j][1]], fresh[out[t]].at[j], send, recv, base + 3 * t + j,
                        (*chips[j], c)) for t in range(len(names)) for j in range(3)]

    def start(refs, fresh, send, recv):
        for cp in copies(refs, fresh, send, recv):
            cp.start()

    def wait(refs, fresh, send, recv):
        for cp in copies(refs, fresh, send, recv):
            cp.wait()

    cy.starts.append(start)
    cy.finishes.append(wait)

    def done(aliased, fresh):
        for t, name in enumerate(names):
            got[name] = fresh[out[t]]

    cy.on_done.append(done)


def _run_carry(name, cy):
    _, res = _carrier_call(None, name=name, grid=(), in_specs=[], out_specs=[], out_shape=[], args=(), sem=(), carry=cy)
    cy.done(res)


FIRST_WEIGHTS = ("ev_w_in",)
NEXT_WEIGHTS = ("ev_w_uq", "ev_w_ukv")
WEIGHTS_A = ("ev_w_out", "w_gate0", "w_up0")
WEIGHTS_B = ("w_down0", "od_w_qkv", "od_w_out")
WEIGHTS_C = ("w_gate1",)
WEIGHTS_D = ("w_up1", "w_down1")
GRAD_GROUPS = {"ffn1": ("w_gate1", "w_up1", "w_down1"), "od": ("od_w_qkv", "od_w_out"),
               "ffn0": ("w_gate0", "w_up0", "w_down0"), "ev_out": ("ev_w_out",),
               "ev": ("ev_w_in", "ev_w_uq", "ev_w_ukv")}


def _carry_pair_exchange(cy, parts, theirs, names):
    idx = [cy.operand(parts[n], False) for n in names]
    out = [cy.result((N_CHIPS, parts[n].shape[1] // 2, parts[n].shape[2]), BF16) for n in names]
    base = cy.sems(len(names))

    def copies(refs, fresh, send, recv):
        x, y, c, _ = _position()
        return [_remote(refs[idx[t]].at[:, _half_rows(1 - c, parts[n].shape[1] // 2), :], fresh[out[t]], send, recv,
                        base + t, (x, y, 1 - c)) for t, n in enumerate(names)]

    cy.starts.append(lambda refs, fresh, send, recv: [cp.start() for cp in copies(refs, fresh, send, recv)])
    cy.finishes.append(lambda refs, fresh, send, recv: [cp.wait() for cp in copies(refs, fresh, send, recv)])

    def done(aliased, fresh):
        for t, name in enumerate(names):
            theirs[name] = fresh[out[t]]

    cy.on_done.append(done)


def _carry_sibling_exchange(cy, fulls, pieces):
    idx = [cy.operand(fulls[p], True) for p, _ in pieces]
    base = cy.sems(len(pieces))

    def copies(refs, send, recv, arriving):
        x, y, c, _ = _position()
        out = []
        for t, (p, layer) in enumerate(pieces):
            r = refs[idx[t]].at[layer, _half_rows(1 - c if arriving else c, fulls[p].shape[1] // 2), :]
            out.append(_remote(r, r, send, recv, base + t, (x, y, 1 - c)))
        return out

    def start(refs, fresh, send, recv):
        for cp in copies(refs, send, recv, False):
            cp.start()

    def wait(refs, fresh, send, recv):
        for cp in copies(refs, send, recv, True):
            cp.wait_recv()
        for cp in copies(refs, send, recv, False):
            cp.wait_send()

    cy.starts.append(start)
    cy.finishes.append(wait)

    def done(aliased, fresh):
        for t, (p, _) in enumerate(pieces):
            fulls[p] = aliased[idx[t]]

    cy.on_done.append(done)


RIDES = {
    "cast_rest": (("gather", FIRST_WEIGHTS),),
    "proj_in": (("gather", NEXT_WEIGHTS),),
    "mla_attn": (("gather_ici", WEIGHTS_A),),
    "sb_attn": (("gather_d2d", WEIGHTS_A), ("gather_ici", WEIGHTS_B)),
    "ev_out": (("gather_d2d", WEIGHTS_B),),
    "ffn0": (("gather_ici", WEIGHTS_C),),
    "qkv": (("gather_d2d", WEIGHTS_C),),
    "band_attn": (("gather_ici", WEIGHTS_D),),
    "od_out": (("gather_d2d", WEIGHTS_D),),
    "od_out_bwd_w": (("pair", "ffn1"),),
    "band_attn_bwd": (("chips", "ffn1"),),
    "rms_mix1_bwd": (("pair", "od"),),
    "ev_out_bwd_w": (("pair", "ffn0"),),
    "mla_attn_bwd": (("chips", "od"), ("sibling", "ffn1"), ("pair", "ev_out")),
    "sb_attn_bwd": (("chips", "ffn0"), ("sibling", "od"), ("chips", "ev_out")),
    "proj_in_bwd_w": (("sibling", "ffn0"), ("sibling", "ev_out")),
    "grads_pair_ev": (("pair", "ev"),),
    "proj_in_bwd_x": (("chips", "ev"),),
    "grads_sibling_ev": (("sibling", "ev"),),
}


class _Exchanges:
    def __init__(self, slots, pos, shapes, cast_rest):
        self.slots, self.pos, self.shapes, self.cast_rest = dict(slots), pos, shapes, cast_rest
        self.parts, self.theirs, self.sums, self.got, self.fulls = {}, {}, {}, {}, {}

    def begin(self):
        self.slots.update(self.cast_rest(self.carry("cast_rest")))

    def weights(self, *names):
        return [self.slots[n] for n in names]

    def _pair_sums(self, group):
        names = GRAD_GROUPS[group]
        self.sums.update(zip(names, _pair_sums("pair_sums_" + group, [self.parts[n] for n in names],
                                               [self.theirs[n] for n in names], self.pos)))

    def _chip_sums(self, group):
        names = GRAD_GROUPS[group]
        items = [(self.sums[n], self.got[n], PART_OF[n][1], self.shapes[PART_OF[n][0]], self.fulls.get(PART_OF[n][0]))
                 for n in names]
        self.fulls.update(zip([PART_OF[n][0] for n in names], _chip_sums("chip_sums_" + group, items, self.pos)))

    def carry(self, stage):
        cy = _Carry()
        for step, what in RIDES[stage]:
            if step == "gather":
                _carry_gather(cy, self.slots, what, True, True)
            elif step == "gather_ici":
                _carry_gather(cy, self.slots, what, True, False)
            elif step == "gather_d2d":
                _carry_gather(cy, self.slots, what, False, True)
            elif step == "pair":
                _carry_pair_exchange(cy, self.parts, self.theirs, GRAD_GROUPS[what])
            elif step == "chips":
                self._pair_sums(what)
                _carry_chip_exchange(cy, self.sums, self.got, GRAD_GROUPS[what])
            elif step == "sibling":
                self._chip_sums(what)
                _carry_sibling_exchange(cy, self.fulls, [PART_OF[n] for n in GRAD_GROUPS[what]])
        return cy

    def grads(self, group, parts):
        self.parts.update(parts)
        if group == "ev":
            _run_carry("grads_pair_ev", self.carry("grads_pair_ev"))

    def finish(self):
        _run_carry("grads_sibling_ev", self.carry("grads_sibling_ev"))
        return {n: self.fulls[n] for n in BIG}


class _NoExchanges:
    def __init__(self, slots):
        self.slots, self.parts = dict(slots), {}

    def begin(self):
        pass

    def weights(self, *names):
        return [self.slots[n] for n in names]

    def carry(self, stage):
        return None

    def grads(self, group, parts):
        self.parts.update(parts)


def _w_in_pieces():
    segments = ((0, Q_LORA, P_CQ), (Q_LORA, Q_LORA + KV_LORA, P_CKV),
                (Q_LORA + KV_LORA, Q_LORA + KV_LORA + MLA_ROPE, P_KR + KR_LANE),
                (Q_LORA + KV_LORA + MLA_ROPE, EVEN_IN, P_QB))
    width = EVEN_IN // N_CHIPS
    pieces = []
    for lo, hi, at in segments:
        for k in range(N_CHIPS):
            a, b = max(lo, k * width), min(hi, (k + 1) * width)
            if a < b:
                pieces.append((k, a - k * width, b - a, at + a - lo))
    return pieces


def _w_in_padded(name, w_in_s):
    tr = MXU_W

    def body(s_ref, o_ref):
        o_ref[...] = jnp.zeros(o_ref.shape, BF16)
        for k, a, n, at in _w_in_pieces():
            o_ref[:, at:at + n] = s_ref[k, :, a:a + n]

    return pl.pallas_call(
        body, name=name, grid=(D // tr,),
        in_specs=[pl.BlockSpec((N_CHIPS, tr, EVEN_IN // N_CHIPS), lambda i: (0, i, 0))],
        out_specs=pl.BlockSpec((tr, P_IN), lambda i: (i, 0)), out_shape=jax.ShapeDtypeStruct((D, P_IN), BF16),
        compiler_params=_params("parallel"))(w_in_s)


def _w_in_sharded(name, d_w_in_p):
    tr = MXU_W

    def body(p_ref, o_ref):
        for k, a, n, at in _w_in_pieces():
            o_ref[k, :, a:a + n] = p_ref[:, at:at + n]

    return pl.pallas_call(
        body, name=name, grid=(D // tr,),
        in_specs=[pl.BlockSpec((tr, P_IN), lambda i: (i, 0))],
        out_specs=pl.BlockSpec((N_CHIPS, tr, EVEN_IN // N_CHIPS), lambda i: (0, i, 0)),
        out_shape=jax.ShapeDtypeStruct((N_CHIPS, D, EVEN_IN // N_CHIPS), BF16),
        compiler_params=_params("parallel"))(d_w_in_p)


def _mla_weights(w_uq_s, w_ukv_s):
    w_uq = jnp.moveaxis(w_uq_s, 0, 1).reshape(Q_LORA, MLA_H, MLA_NOPE + MLA_ROPE)
    w_uq_p = jnp.concatenate([w_uq, jnp.zeros((Q_LORA, MLA_H, LANES - MLA_NOPE - MLA_ROPE), BF16)], axis=2)
    w_ukv = jnp.moveaxis(w_ukv_s, 0, 1).reshape(KV_LORA, MLA_H, MLA_NOPE + MLA_V)
    w_uk_p = jnp.concatenate([w_ukv[:, :, :MLA_NOPE], jnp.zeros((KV_LORA, MLA_H, LANES - MLA_NOPE), BF16)], axis=2)
    return dict(
        w_uq=w_uq_p.reshape(Q_LORA, MLA_H * LANES), w_uk=w_uk_p.reshape(KV_LORA, MLA_H * LANES),
        w_uv=w_ukv[:, :, MLA_NOPE:].reshape(KV_LORA, MLA_H * MLA_V))


def _proj_mm(name, u, w_in, carry=None):
    return _mm(name, u, w_in, kind="nn", grid=(S // TM, 1, 1),
               a_spec=pl.BlockSpec((TM, D), lambda i, j, k: (i, 0)), b_spec=pl.BlockSpec((D, P_IN), lambda i, j, k: (0, 0)),
               o_spec=pl.BlockSpec((TM, P_IN), lambda i, j, k: (i, 0)), out_shape=(S, P_IN), out_dtype=F32, acc_shape=None,
               carry=carry)


def _out_proj(name, o, w, resid, carry=None):
    return _mm(name, o, w, kind="nn", grid=(S // TQ, 1, 1),
               a_spec=pl.BlockSpec((TQ, D), lambda i, j, k: (i, 0)), b_spec=pl.BlockSpec((D, D), lambda i, j, k: (0, 0)),
               o_spec=pl.BlockSpec((TQ, D), lambda i, j, k: (i, 0)), out_shape=(S, D), out_dtype=F32, acc_shape=None,
               resid=resid, r_spec=pl.BlockSpec((TQ, D), lambda i, j, k: (i, 0)), carry=carry)


def _out_proj_bwd(name, dh, o, w, ex):
    d_o = _mm(name + "_x", dh, w, kind="nt", grid=(S // TQ, 1, 1),
              a_spec=pl.BlockSpec((TQ, D), lambda i, j, k: (i, 0)), b_spec=pl.BlockSpec((D, D), lambda i, j, k: (0, 0)),
              o_spec=pl.BlockSpec((TQ, D), lambda i, j, k: (i, 0)), out_shape=(S, D), out_dtype=F32, acc_shape=None)
    d_w = _mm(name + "_w", o, dh, kind="tn", grid=(2, S // TQ),
              a_spec=pl.BlockSpec((TQ, TM), lambda j, k: (k, j)), b_spec=pl.BlockSpec((TQ, D), lambda j, k: (k, 0)),
              o_spec=pl.BlockSpec((TM, D), lambda j, k: (j, 0)), out_shape=(D, D), out_dtype=BF16, acc_shape=(TM, D),
              carry=ex.carry(name + "_w"))
    return d_o, d_w


def _local_step(x, tgt, sm, ex):
    def riding(stage, fn, *args):
        cy = ex.carry(stage)
        res, copies = fn(stage, *args, carry=cy)
        if cy is not None:
            cy.done(copies)
        return res

    cos_t, sin_t = _rope_tables()
    g_mix, g_ffn = sm["g_mix"], sm["g_ffn"]
    r0 = sm["od_rel_bias"][0][:, _band_row_index()].reshape(C_H // 2, 2, TOEP_W)
    nt = 3

    ex.begin()
    w = {"w_in": _w_in_padded("w_in_padded", *ex.weights(*FIRST_WEIGHTS))}
    u0 = _rms_fwd("rms_mix0", x, g_mix[0:1])
    proj = _proj_mm("proj_in", u0, w["w_in"], ex.carry("proj_in"))
    w.update(_mla_weights(*ex.weights(*NEXT_WEIGHTS)))
    qa, ka, va = _mla_prep_fwd("mla_prep", proj, sm["ev_g_cq"], sm["ev_g_ckv"], w["w_uq"], w["w_uk"], w["w_uv"], cos_t, sin_t)
    o_a, lse = riding("mla_attn", _mla_fwd, qa, ka, va)
    o_b, = riding("sb_attn", _sb_fwd, proj)
    o_ev = jnp.concatenate([o_a.astype(BF16), o_b], axis=1)
    w["ev_w_out"] = ex.weights("ev_w_out")[0].reshape(D, D)
    h1 = _out_proj("ev_out", o_ev, w["ev_w_out"], x, ex.carry("ev_out"))
    w["w_gate0"], w["w_up0"], w["w_down0"] = ex.weights("w_gate0", "w_up0", "w_down0")
    h2, gate0, up0 = riding("ffn0", _ffn_fwd, h1, g_ffn[0:1], w["w_gate0"], w["w_up0"], w["w_down0"])
    w["w_qkv"] = jnp.moveaxis(ex.weights("od_w_qkv")[0], 0, 1).reshape(D, nt * D)
    u2 = _rms_fwd("rms_mix1", h2, g_mix[1:2])
    qkv = _mm("qkv", u2, w["w_qkv"], kind="nn", grid=(S // TQ, nt, 1),
              a_spec=pl.BlockSpec((TQ, D), lambda i, t, k: (i, 0)), b_spec=pl.BlockSpec((D, D), lambda i, t, k: (0, t)),
              o_spec=pl.BlockSpec((None, TQ, D), lambda i, t, k: (t, i, 0)),
              out_shape=(nt, S, D), out_dtype=BF16, acc_shape=None, carry=ex.carry("qkv"))
    o_od, = riding("band_attn", _band_fwd, qkv, r0)
    w["od_w_out"] = ex.weights("od_w_out")[0].reshape(D, D)
    h3 = _out_proj("od_out", o_od, w["od_w_out"], h2, ex.carry("od_out"))
    w["w_gate1"], w["w_up1"], w["w_down1"] = ex.weights("w_gate1", "w_up1", "w_down1")
    (h4, gate1, up1), _ = _ffn_fwd("ffn1", h3, g_ffn[1:2], w["w_gate1"], w["w_up1"], w["w_down1"])

    loss, dh4, dg_final = _loss_bwd("loss", h4, sm["g_final"].reshape(1, D), tgt)

    dh3, dg_ffn1, u3, dgate, dup, act = _ffn_bwd("ffn1_bwd", dh4, h3, g_ffn[1:2], gate1, up1,
                                                 w["w_gate1"], w["w_up1"], w["w_down1"])
    d_wg1, d_wu1, d_wd1 = _ffn_wgrads("ffn1_dw", u3, dgate, dup, act, dh4)
    ex.grads("ffn1", {"w_gate1": d_wg1, "w_up1": d_wu1, "w_down1": d_wd1})

    d_ood, d_w_od_out = _out_proj_bwd("od_out_bwd", dh3, o_od, w["od_w_out"], ex)
    dqkv, dr0 = riding("band_attn_bwd", _band_bwd, qkv, r0, d_ood)
    du2 = _mm("qkv_bwd_x", dqkv, w["w_qkv"], kind="nt", grid=(S // TQ, nt),
              a_spec=pl.BlockSpec((None, TQ, D), lambda i, t: (t, i, 0)), b_spec=pl.BlockSpec((D, D), lambda i, t: (0, t)),
              o_spec=pl.BlockSpec((TQ, D), lambda i, t: (i, 0)), out_shape=(S, D), out_dtype=F32, acc_shape=(TQ, D))
    wide, per = D // MXU_W, nt * D // N_CHIPS // MXU_W
    piece = lambda r: pl.BlockSpec((None, TQ, MXU_W), lambda j, k: ((per * j + r) // wide, k, (per * j + r) % wide))
    d_w_qkv = _mm("qkv_bwd_w", u2, dqkv, kind="tn", grid=(N_CHIPS, S // TQ),
                  a_spec=pl.BlockSpec((TQ, D), lambda j, k: (k, 0)), b_spec=[piece(r) for r in range(per)],
                  o_spec=pl.BlockSpec((None, D, per * MXU_W), lambda j, k: (j, 0, 0)),
                  out_shape=(N_CHIPS, D, per * MXU_W), out_dtype=BF16, acc_shape=(D, per * MXU_W))
    shard_cols = lambda a: jnp.moveaxis(a.reshape(a.shape[0], N_CHIPS, a.shape[1] // N_CHIPS), 1, 0)
    ex.grads("od", {"od_w_qkv": d_w_qkv, "od_w_out": d_w_od_out.reshape(N_CHIPS, D // N_CHIPS, D)})
    dh2, dg_mix1 = _rms_bwd("rms_mix1_bwd", du2, h2, g_mix[1:2], dh3, carry=ex.carry("rms_mix1_bwd"))
    d_rel = _bias_table_grad("rel_bias_grad", dr0.reshape(C_H, TOEP_W))[:, :2 * REL_CLIP + 1]

    dh1, dg_ffn0, u1, dgate, dup, act = _ffn_bwd("ffn0_bwd", dh2, h1, g_ffn[0:1], gate0, up0,
                                                 w["w_gate0"], w["w_up0"], w["w_down0"])
    d_wg0, d_wu0, d_wd0 = _ffn_wgrads("ffn0_dw", u1, dgate, dup, act, dh2)
    ex.grads("ffn0", {"w_gate0": d_wg0, "w_up0": d_wu0, "w_down0": d_wd0})

    d_oev, d_w_ev_out = _out_proj_bwd("ev_out_bwd", dh1, o_ev, w["ev_w_out"], ex)
    ex.grads("ev_out", {"ev_w_out": d_w_ev_out.reshape(N_CHIPS, D // N_CHIPS, D)})
    dqa, dka, dva = riding("mla_attn_bwd", _mla_bwd, qa, ka, va, o_a, lse, d_oev, 0)
    dqb, dkb, dvb = riding("sb_attn_bwd", _sb_bwd, proj, d_oev, MLA_H * MLA_V // LANES)
    dcq, dckv, dkr, d_w_uq, d_w_uk, d_w_uv, dg_cq, dg_ckv = _mla_prep_bwd(
        "mla_prep_bwd", dqa, dka, dva, proj, sm["ev_g_cq"], sm["ev_g_ckv"], w["w_uq"], w["w_uk"], w["w_uv"], cos_t, sin_t)
    dproj = [dcq, jnp.zeros((S, LANES), BF16), dckv, dqb, dkb, dvb, dkr]
    d_w_in_p = _mm("proj_in_bwd_w", u0, dproj, kind="tn", grid=(1, S // TQ),
                   a_spec=pl.BlockSpec((TQ, D), lambda j, k: (k, 0)),
                   b_spec=[pl.BlockSpec((TQ, p.shape[1]), lambda j, k: (k, 0)) for p in dproj],
                   o_spec=pl.BlockSpec((D, P_IN), lambda j, k: (0, 0)), out_shape=(D, P_IN), out_dtype=BF16,
                   acc_shape=(D, P_IN), carry=ex.carry("proj_in_bwd_w"))
    d_w_uq_std = d_w_uq.reshape(Q_LORA, MLA_H, LANES)[:, :, :MLA_NOPE + MLA_ROPE].reshape(Q_LORA, -1)
    d_w_ukv = jnp.concatenate([d_w_uk.reshape(KV_LORA, MLA_H, LANES)[:, :, :MLA_NOPE],
                               d_w_uv.reshape(KV_LORA, MLA_H, MLA_V)], axis=2).reshape(KV_LORA, -1)
    ex.grads("ev", {"ev_w_in": _w_in_sharded("w_in_sharded", d_w_in_p), "ev_w_uq": shard_cols(d_w_uq_std.astype(BF16)),
                    "ev_w_ukv": shard_cols(d_w_ukv.astype(BF16))})
    du0 = _mm("proj_in_bwd_x", dproj, w["w_in"], kind="nt", grid=(S // TM, 1, 1),
              a_spec=[pl.BlockSpec((TM, p.shape[1]), lambda i, j, k: (i, 0)) for p in dproj],
              b_spec=pl.BlockSpec((D, P_IN), lambda i, j, k: (0, 0)),
              o_spec=pl.BlockSpec((TM, D), lambda i, j, k: (i, 0)), out_shape=(S, D), out_dtype=F32, acc_shape=None,
              carry=ex.carry("proj_in_bwd_x"))
    grad_x, dg_mix0 = _rms_bwd("rms_mix0_bwd", du0, x, g_mix[0:1], dh1)
    small = {
        "ev_g_cq": dg_cq, "ev_g_ckv": dg_ckv, "od_rel_bias": d_rel.reshape(1, C_H, 2 * REL_CLIP + 1),
        "g_mix": jnp.concatenate([dg_mix0, dg_mix1], axis=0), "g_ffn": jnp.concatenate([dg_ffn0, dg_ffn1], axis=0),
        "g_final": dg_final.reshape(D),
    }
    return loss, grad_x, small


BIG = ("ev_w_in", "ev_w_uq", "ev_w_ukv", "ev_w_out", "od_w_qkv", "od_w_out", "w_gate", "w_up", "w_down")
SMALL = ("ev_g_cq", "ev_g_ckv", "od_rel_bias", "g_mix", "g_ffn", "g_final")
WEIGHTS = ("ev_w_in", "ev_g_cq", "ev_w_uq", "ev_g_ckv", "ev_w_ukv", "ev_w_out", "od_w_qkv", "od_rel_bias", "od_w_out",
           "g_mix", "g_ffn", "w_gate", "w_up", "w_down", "g_final")
GRAD_PARTS = (("ev_w_in", "ev_w_in", 0), ("ev_w_uq", "ev_w_uq", 0), ("ev_w_ukv", "ev_w_ukv", 0),
              ("ev_w_out", "ev_w_out", 0), ("od_w_qkv", "od_w_qkv", 0), ("od_w_out", "od_w_out", 0),
              ("w_gate0", "w_gate", 0), ("w_gate1", "w_gate", 1), ("w_up0", "w_up", 0), ("w_up1", "w_up", 1),
              ("w_down0", "w_down", 0), ("w_down1", "w_down", 1))
PART_OF = {part: (param, layer) for part, param, layer in GRAD_PARTS}
TRANSPOSED = ("w_gate", "w_up")
ADAMW_TRANSPOSED = ("ev_w_in", "ev_w_uq")


def _row_tile(rows, cap=512, sublanes=16):
    for t in range(min(rows, cap), 0, -1):
        if rows % t == 0 and t % sublanes == 0:
            return t
    return rows


def _cast_into_slot(name, w, layer, pos):
    _, rows, cols = w.shape
    tr = _row_tile(rows)

    def body(pos_ref, w_ref, o_ref):
        o_ref[...] = w_ref[...].astype(BF16)

    return pl.pallas_call(
        body, name=name,
        grid_spec=pltpu.PrefetchScalarGridSpec(
            num_scalar_prefetch=1, grid=(rows // tr,),
            in_specs=[pl.BlockSpec((None, tr, cols), lambda i, p: (layer, i, 0))],
            out_specs=pl.BlockSpec((None, tr, cols), lambda i, p: (p[0], i, 0))),
        out_shape=jax.ShapeDtypeStruct((N_CHIPS, rows, cols), BF16), compiler_params=_params("arbitrary"))(pos, w)


def _cast_many_into_slots(name, items, pos, carry):
    tiles = [_row_tile(w.shape[1]) for w, _ in items]
    turns = _Turns([w.shape[1] // tr for (w, _), tr in zip(items, tiles)])

    def body(pos_ref, *refs):
        i = pl.program_id(0)
        for t in range(len(items)):
            @pl.when(turns.mine(t, i))
            def _(w_ref=refs[t], o_ref=refs[len(items) + t]):
                o_ref[...] = w_ref[...].astype(BF16)

    in_specs, out_specs, out_shape = [], [], []
    for t, ((w, layer), tr) in enumerate(zip(items, tiles)):
        _, rows, cols = w.shape
        at = turns.step(t)
        in_specs.append(pl.BlockSpec((None, tr, cols), lambda i, p, at=at, layer=layer: (layer, at(i), 0)))
        out_specs.append(pl.BlockSpec((None, tr, cols), lambda i, p, at=at: (p[0], at(i), 0)))
        out_shape.append(jax.ShapeDtypeStruct((N_CHIPS, rows, cols), BF16))
    res, copies = _carrier_call(body, name=name, grid=(turns.total,), in_specs=in_specs, out_specs=out_specs,
                                out_shape=out_shape, args=[w for w, _ in items], sem=("arbitrary",), carry=carry,
                                prefetch=(pos,))
    if carry is not None:
        carry.done(copies)
    return res


class _Turns:
    def __init__(self, counts):
        self.counts = list(counts)
        self.starts = [sum(self.counts[:t]) for t in range(len(self.counts))]
        self.total = sum(self.counts)

    def step(self, t):
        start, n = self.starts[t], self.counts[t]
        return lambda i: jnp.clip(i - start, 0, n - 1)

    def mine(self, t, i):
        return (i >= self.starts[t]) & (i < self.starts[t] + self.counts[t])


def _pair_sums(name, parts, theirs, pos):
    n, pair = len(parts), 2
    tiles = [_row_tile(b.shape[1]) for b in theirs]
    blocks = [b.shape[1] // tr for b, tr in zip(theirs, tiles)]
    turns = _Turns([N_CHIPS // pair * nb for nb in blocks])

    def body(pos_ref, *refs):
        i = pl.program_id(0)
        for t in range(n):
            @pl.when(turns.mine(t, i))
            def _(a_ref=refs[2 * t], b_ref=refs[2 * t + 1], o_ref=refs[2 * n + t]):
                o_ref[...] = (a_ref[...].astype(F32) + b_ref[...].astype(F32)).astype(BF16)

    in_specs, out_specs = [], []
    for t, (b, tr, nb) in enumerate(zip(theirs, tiles, blocks)):
        at, block = turns.step(t), (pair, tr, b.shape[2])
        in_specs.append(pl.BlockSpec(block, lambda i, p, at=at, nb=nb: (at(i) // nb, p[1] * nb + at(i) % nb, 0)))
        in_specs.append(pl.BlockSpec(block, lambda i, p, at=at, nb=nb: (at(i) // nb, at(i) % nb, 0)))
        out_specs.append(pl.BlockSpec(block, lambda i, p, at=at, nb=nb: (at(i) // nb, at(i) % nb, 0)))
    return pl.pallas_call(
        body, name=name,
        grid_spec=pltpu.PrefetchScalarGridSpec(num_scalar_prefetch=1, grid=(turns.total,), in_specs=in_specs,
                                               out_specs=out_specs),
        out_shape=[jax.ShapeDtypeStruct(b.shape, BF16) for b in theirs],
        compiler_params=_params("arbitrary"))(pos, *[a for pair in zip(parts, theirs) for a in pair])


def _chip_sums(name, items, pos):
    n = len(items)
    tiles = [_row_tile(s.shape[1]) for s, *_ in items]
    turns = _Turns([s.shape[1] // tr for (s, *_), tr in zip(items, tiles)])
    carried = [t for t, item in enumerate(items) if item[4] is not None]

    def body(pos_ref, *refs):
        i = pl.program_id(0)
        for t in range(n):
            @pl.when(turns.mine(t, i))
            def _(s_ref=refs[2 * t], g_ref=refs[2 * t + 1], o_ref=refs[2 * n + len(carried) + t]):
                o_ref[...] = ((s_ref[...].astype(F32) + g_ref[0].astype(F32)) + g_ref[1].astype(F32)) + g_ref[2].astype(F32)

    in_specs, out_specs = [], []
    for t, ((s, got, layer, full_shape, full), tr) in enumerate(zip(items, tiles)):
        at, cols, nb = turns.step(t), s.shape[2], turns.counts[t]
        in_specs.append(pl.BlockSpec((None, tr, cols), lambda i, p, at=at: (p[0], at(i), 0)))
        in_specs.append(pl.BlockSpec((3, tr, cols), lambda i, p, at=at: (0, at(i), 0)))
        out_specs.append(pl.BlockSpec((None, tr, cols), lambda i, p, at=at, nb=nb, layer=layer: (layer, p[1] * nb + at(i), 0)))
    return pl.pallas_call(
        body, name=name,
        grid_spec=pltpu.PrefetchScalarGridSpec(num_scalar_prefetch=1, grid=(turns.total,),
                                               in_specs=in_specs + [ANY] * len(carried), out_specs=out_specs),
        out_shape=[jax.ShapeDtypeStruct(item[3], F32) for item in items],
        input_output_aliases={1 + 2 * n + k: t for k, t in enumerate(carried)},
        compiler_params=_params("arbitrary"))(
            pos, *[a for item in items for a in item[:2]], *[items[t][4] for t in carried])


def _adamw_update(w, g, m, v):
    m_new = ADAM_B1 * m + (1.0 - ADAM_B1) * g
    v_new = ADAM_B2 * v + (1.0 - ADAM_B2) * (g * g)
    m_hat = m_new / (1.0 - ADAM_B1 ** ADAM_STEP)
    v_hat = v_new / (1.0 - ADAM_B2 ** ADAM_STEP)
    return -ADAM_LR * (m_hat / (jnp.sqrt(v_hat) + ADAM_EPS) + ADAM_WD * w), m_new, v_new


def _small_step(name, grads, loss, w, m, v):
    n, n_dev = len(grads), 8
    offs = [sum(g.shape[0] for g in grads[:t]) for t in range(n + 1)]
    rows = -(-(offs[n] + 1) // 8) * 8
    width = max(g.shape[1] for g in grads)

    def body(*refs):
        g_refs, loss_ref = refs[:n], refs[n]
        w_refs, m_refs, v_refs = (refs[1 + k * n:1 + (k + 1) * n] for k in (1, 2, 3))
        outs = refs[4 * n + 1:8 * n + 2]
        mine, slots, send_sem, recv_sem = refs[8 * n + 2:]
        x, y, c, _ = _position()
        me = 4 * x + 2 * y + c

        def peer(k):
            return (1 - x if k & 4 else x, 1 - y if k & 2 else y, 1 - c if k & 1 else c)

        def logical(k):
            px, py, pc = peer(k)
            return 4 * px + 2 * py + pc

        mine[...] = jnp.zeros(mine.shape, F32)
        for t in range(n):
            mine[offs[t]:offs[t + 1], 0:grads[t].shape[1]] = g_refs[t][...]
        mine[offs[n]:offs[n] + 1, 0:LANES] = loss_ref[...]
        slots[me] = mine[...]
        sends = [pltpu.make_async_remote_copy(
            src_ref=mine, dst_ref=slots.at[me], send_sem=send_sem.at[k], recv_sem=recv_sem.at[k],
            device_id=peer(k), device_id_type=MESH) for k in range(1, n_dev)]
        for cp in sends:
            cp.start()
        for k in range(1, n_dev):
            pltpu.make_async_remote_copy(
                src_ref=mine, dst_ref=slots.at[logical(k)], send_sem=send_sem.at[k], recv_sem=recv_sem.at[k],
                device_id=peer(k), device_id_type=MESH).wait_recv()
        for cp in sends:
            cp.wait_send()
        total = slots[0]
        for d in range(1, n_dev):
            total = total + slots[d]
        for t in range(n):
            gv = total[offs[t]:offs[t + 1], 0:grads[t].shape[1]]
            outs[t][...] = gv
            outs[n + t][...], outs[2 * n + t][...], outs[3 * n + t][...] = _adamw_update(
                w_refs[t][...], gv, m_refs[t][...], v_refs[t][...])
        outs[4 * n][...] = total[offs[n]:offs[n] + 1, 0:LANES]

    vm = pl.BlockSpec(memory_space=pltpu.VMEM)
    shapes = [jax.ShapeDtypeStruct(g.shape, F32) for g in grads]
    res = pl.pallas_call(
        body, name=name, in_specs=[vm] * (4 * n + 1), out_specs=[vm] * (4 * n + 1),
        out_shape=shapes * 4 + [jax.ShapeDtypeStruct(loss.shape, F32)],
        scratch_shapes=[pltpu.VMEM((rows, width), F32), pltpu.VMEM((n_dev, rows, width), F32),
                        pltpu.SemaphoreType.DMA((n_dev,)), pltpu.SemaphoreType.DMA((n_dev,))],
    )(*grads, loss, *w, *m, *v)
    return [res[k * n:(k + 1) * n] for k in range(4)], res[4 * n]


def _adamw(name, w, g, m, v):
    rows, cols = w.shape
    tr = _row_tile(rows, sublanes=8)

    def body(w_ref, g_ref, m_ref, v_ref, d_ref, mo_ref, vo_ref):
        d_ref[...], mo_ref[...], vo_ref[...] = _adamw_update(w_ref[...], g_ref[...], m_ref[...], v_ref[...])

    spec = pl.BlockSpec((tr, cols), lambda i: (i, 0))
    shape = jax.ShapeDtypeStruct((rows, cols), F32)
    return pl.pallas_call(body, name=name, grid=(rows // tr,), in_specs=[spec] * 4, out_specs=[spec] * 3,
                          out_shape=[shape] * 3, compiler_params=_params("parallel"))(w, g, m, v)


def kernel(x, ev_w_in, ev_g_cq, ev_w_uq, ev_g_ckv, ev_w_ukv, ev_w_out, od_w_qkv, od_rel_bias, od_w_out, g_mix, g_ffn, w_gate, w_up, w_down, g_final, loss_target, m_ev_w_in, m_ev_g_cq, m_ev_w_uq, m_ev_g_ckv, m_ev_w_ukv, m_ev_w_out, m_od_w_qkv, m_od_rel_bias, m_od_w_out, m_g_mix, m_g_ffn, m_w_gate, m_w_up, m_w_down, m_g_final, v_ev_w_in, v_ev_g_cq, v_ev_w_uq, v_ev_g_ckv, v_ev_w_ukv, v_ev_w_out, v_od_w_qkv, v_od_rel_bias, v_od_w_out, v_g_mix, v_g_ffn, v_w_gate, v_w_up, v_w_down, v_g_final):
    w = dict(ev_w_in=ev_w_in, ev_g_cq=ev_g_cq, ev_w_uq=ev_w_uq, ev_g_ckv=ev_g_ckv, ev_w_ukv=ev_w_ukv, ev_w_out=ev_w_out,
             od_w_qkv=od_w_qkv, od_rel_bias=od_rel_bias, od_w_out=od_w_out, g_mix=g_mix, g_ffn=g_ffn, w_gate=w_gate,
             w_up=w_up, w_down=w_down, g_final=g_final)
    m = dict(ev_w_in=m_ev_w_in, ev_g_cq=m_ev_g_cq, ev_w_uq=m_ev_w_uq, ev_g_ckv=m_ev_g_ckv, ev_w_ukv=m_ev_w_ukv,
             ev_w_out=m_ev_w_out, od_w_qkv=m_od_w_qkv, od_rel_bias=m_od_rel_bias, od_w_out=m_od_w_out, g_mix=m_g_mix,
             g_ffn=m_g_ffn, w_gate=m_w_gate, w_up=m_w_up, w_down=m_w_down, g_final=m_g_final)
    v = dict(ev_w_in=v_ev_w_in, ev_g_cq=v_ev_g_cq, ev_w_uq=v_ev_w_uq, ev_g_ckv=v_ev_g_ckv, ev_w_ukv=v_ev_w_ukv,
             ev_w_out=v_ev_w_out, od_w_qkv=v_od_w_qkv, od_rel_bias=v_od_rel_bias, od_w_out=v_od_w_out, g_mix=v_g_mix,
             g_ffn=v_g_ffn, w_gate=v_w_gate, w_up=v_w_up, w_down=v_w_down, g_final=v_g_final)
    flat2d = lambda a: a.reshape(-1, a.shape[-1])
    for tree in (w, m, v):
        for n in TRANSPOSED:
            tree[n] = jnp.swapaxes(tree[n], 1, 2)

    pos = jnp.stack([2 * lax.axis_index("x") + lax.axis_index("y"), lax.axis_index("c")]).astype(jnp.int32)

    slots = {part: _cast_into_slot("cast_" + part, w[n], layer, pos) for part, n, layer in GRAD_PARTS
             if part in FIRST_WEIGHTS + NEXT_WEIGHTS}
    rest = [(part, n, layer) for part, n, layer in GRAD_PARTS if part not in slots]

    def cast_rest(carry):
        return dict(zip([part for part, _, _ in rest],
                        _cast_many_into_slots("cast_rest", [(w[n], layer) for _, n, layer in rest], pos, carry)))

    ex = _Exchanges(slots, pos, {n: w[n].shape for n in BIG}, cast_rest)

    loss_local, grad_x, small = _local_step(x[0], loss_target[0], {n: w[n] for n in SMALL}, ex)

    grads = ex.finish()
    delta, new_m, new_v = {}, {}, {}
    small_out, loss = _small_step("small_step", [flat2d(small[n]) for n in SMALL], loss_local,
                                  *([flat2d(t[n]) for n in SMALL] for t in (w, m, v)))
    for tree, outs in zip((grads, delta, new_m, new_v), small_out):
        tree.update({n: o.reshape(w[n].shape) for n, o in zip(SMALL, outs)})

    for n in BIG:
        turn = (lambda a: jnp.swapaxes(a, 1, 2)) if n in ADAMW_TRANSPOSED else (lambda a: a)
        shape = turn(w[n]).shape
        outs = _adamw("adamw_" + n, *(flat2d(turn(a)) for a in (w[n], grads[n], m[n], v[n])))
        delta[n], new_m[n], new_v[n] = (turn(o.reshape(shape)) for o in outs)
    for tree in (grads, delta, new_m, new_v):
        for n in TRANSPOSED:
            tree[n] = jnp.swapaxes(tree[n], 1, 2)

    return (loss[0, 0], grad_x[None], *[grads[n] for n in WEIGHTS], *[delta[n] for n in WEIGHTS],
            *[new_m[n] for n in WEIGHTS], *[new_v[n] for n in WEIGHTS])
```

```python
import functools

import jax
import jax.numpy as jnp
import numpy as np
from jax import lax
from jax.experimental import pallas as pl
from jax.experimental.pallas import tpu as pltpu

F32 = jnp.float32
BF16 = jnp.bfloat16

S = 2048
D = 1024
CHUNK = 64
MLA_H, MLA_NOPE, MLA_ROPE, MLA_V = 8, 64, 32, 64
Q_LORA, KV_LORA = 384, 256
ROPE_THETA = 10000.0
SB_H, SB_DIM = 8, 64
C_H, C_DIM = 16, 64
LEFT_CHUNKS = 8
REL_CLIP = 256
D_FF = 2816
EVEN_IN = 2208
RMS_EPS = 1e-6
ADAM_LR, ADAM_B1, ADAM_B2, ADAM_EPS, ADAM_WD, ADAM_STEP = 0.001, 0.9, 0.999, 1e-08, 0.01, 10

N_CHIPS = 4
FF_SHARD = D_FF // N_CHIPS
SCALE_A = (MLA_NOPE + MLA_ROPE) ** -0.5
SCALE_B = SB_DIM ** -0.5
SCALE_C = C_DIM ** -0.5
NEG = -1e30
LOG2_E = 1.4426950408889634

LANES = 128
MXU_W = 256
VMEM_LIMIT_BYTES = 56 * 1024 * 1024
TM = 512
TQ = 1024
QB = 512
BQ = 256

P_CQ, P_CKV, P_QB, P_KB, P_VB, P_KR = 0, 512, 768, 1280, 1792, 2304
P_IN = 2432
KR_LANE = 64
BAND_W = BQ + LEFT_CHUNKS * CHUNK
BAND_PAD = 512
TOEP_W = 1024


def _params(*sem):
    return pltpu.CompilerParams(dimension_semantics=sem, vmem_limit_bytes=VMEM_LIMIT_BYTES)


MESH = pl.DeviceIdType.MESH
ANY = pl.BlockSpec(memory_space=pl.ANY)


def _position():
    x, y, c = lax.axis_index("x"), lax.axis_index("y"), lax.axis_index("c")
    other_chips = [(1 - x, y), (x, 1 - y), (1 - x, 1 - y)]
    return x, y, c, other_chips


def _half_rows(c, half):
    return pl.ds(pl.multiple_of(c * half, 16), half)


def _remote(ref_src, ref_dst, send, recv, k, device):
    return pltpu.make_async_remote_copy(src_ref=ref_src, dst_ref=ref_dst, send_sem=send.at[k], recv_sem=recv.at[k],
                                        device_id=device, device_id_type=MESH)


class _Carry:
    def __init__(self):
        self.operands, self.aliased, self.fresh = [], [], []
        self.n_sems = 0
        self.starts, self.finishes, self.on_done = [], [], []

    def operand(self, arr, aliased):
        for i, a in enumerate(self.operands):
            if a is arr:
                return i
        self.operands.append(arr)
        self.aliased.append(aliased)
        return len(self.operands) - 1

    def result(self, shape, dtype):
        self.fresh.append(jax.ShapeDtypeStruct(shape, dtype))
        return len(self.fresh) - 1

    def sems(self, k):
        base = self.n_sems
        self.n_sems += k
        return base

    def done(self, results):
        aliased, fresh = results
        for f in self.on_done:
            f(aliased, fresh)


def _carrier_call(body, *, name, grid, in_specs, out_specs, out_shape, args, sem, scratch_shapes=(), carry=None,
                  prefetch=()):
    in_specs, out_specs, out_shape, scratch = list(in_specs), list(out_specs), list(out_shape), list(scratch_shapes)
    n_pre = len(prefetch)

    def call(kernel, in_specs, out_specs, out_shape, scratch, aliases, sem):
        return pl.pallas_call(
            kernel, name=name, out_shape=out_shape, input_output_aliases=aliases, compiler_params=_params(*sem),
            grid_spec=pltpu.PrefetchScalarGridSpec(num_scalar_prefetch=n_pre, grid=grid, in_specs=in_specs,
                                                   out_specs=out_specs, scratch_shapes=scratch))

    if carry is None:
        return list(call(body, in_specs, out_specs, out_shape, scratch, {}, sem)(*prefetch, *args)), None
    ops = carry.operands
    alias_idx = [i for i, a in enumerate(carry.aliased) if a]
    c_shapes = [jax.ShapeDtypeStruct(ops[i].shape, ops[i].dtype) for i in alias_idx] + carry.fresh
    n_in, n_out, n_scr = len(args), len(out_shape), len(scratch)

    def wrapped(*refs):
        pre, refs = refs[:n_pre], refs[n_pre:]
        ins, c_ins = refs[:n_in], refs[n_in:n_in + len(ops)]
        o0 = n_in + len(ops)
        outs, c_outs = refs[o0:o0 + n_out], refs[o0 + n_out:o0 + n_out + len(c_shapes)]
        s0 = o0 + n_out + len(c_shapes)
        scr, send, recv = refs[s0:s0 + n_scr], refs[s0 + n_scr], refs[s0 + n_scr + 1]
        use = list(c_ins)
        for k, i in enumerate(alias_idx):
            use[i] = c_outs[k]
        fresh = c_outs[len(alias_idx):]

        def run(steps):
            for step in steps:
                step(use, fresh, send, recv)

        if not grid:
            run(carry.starts)
            if body is not None:
                body(*pre, *ins, *outs, *scr)
            run(carry.finishes)
            return
        ids = [pl.program_id(a) for a in range(len(grid))]
        first = functools.reduce(jnp.logical_and, [i == 0 for i in ids])
        last = functools.reduce(jnp.logical_and, [i == g - 1 for i, g in zip(ids, grid)])

        @pl.when(first)
        def _():
            run(carry.starts)

        body(*pre, *ins, *outs, *scr)

        @pl.when(last)
        def _():
            run(carry.finishes)

    res = call(wrapped, in_specs + [ANY] * len(ops), out_specs + [ANY] * len(c_shapes), out_shape + c_shapes,
               scratch + [pltpu.SemaphoreType.DMA((carry.n_sems,)), pltpu.SemaphoreType.DMA((carry.n_sems,))],
               {n_pre + n_in + i: n_out + k for k, i in enumerate(alias_idx)},
               ("arbitrary",) * len(grid))(*prefetch, *args, *ops)
    res = list(res)
    c_res = res[n_out:]
    return res[:n_out], ({i: c_res[k] for k, i in enumerate(alias_idx)}, c_res[len(alias_idx):])


_DIMS = {"nn": (((1,), (0,)), ((), ())), "nt": (((1,), (1,)), ((), ())), "tn": (((0,), (0,)), ((), ()))}


def _dot(a, b, kind="nn"):
    return lax.dot_general(a, b, _DIMS[kind], preferred_element_type=F32)


def _iota(shape, dim):
    return lax.broadcasted_iota(jnp.int32, shape, dim)


def _sigmoid(x):
    return 1.0 / (1.0 + jnp.exp(-x))


def _split_dot(x, tri):
    hi = x.astype(BF16)
    lo = (x - hi.astype(F32)).astype(BF16)
    both = _dot(jnp.concatenate([hi, lo], axis=0), tri)
    return both[:x.shape[0]] + both[x.shape[0]:]


def _running_sum(x, tri, reverse):
    n = x.shape[1] // MXU_W
    blocks = [x[:, b * MXU_W:(b + 1) * MXU_W] for b in range(n)]
    out = [None] * n
    carry = None
    for b in (range(n - 1, -1, -1) if reverse else range(n)):
        part = _split_dot(blocks[b], tri)
        out[b] = part if carry is None else part + carry
        total = jnp.sum(blocks[b], axis=-1, keepdims=True)
        carry = total if carry is None else carry + total
    return (jnp.concatenate(out, axis=1) if n > 1 else out[0]), carry


def _mm(name, a, b, *, kind, grid, a_spec, b_spec, o_spec, out_shape, out_dtype, acc_shape, resid=None, r_spec=None,
        carry=None):
    nk = grid[-1]
    has_r = resid is not None
    several = lambda x: list(x) if isinstance(x, (tuple, list)) else [x]
    a_specs, b_specs = several(a_spec), several(b_spec)
    na, nb = len(a_specs), len(b_specs)
    a_args = several(a) if isinstance(a, (tuple, list)) else [a] * na
    b_args = several(b) if isinstance(b, (tuple, list)) else [b] * nb

    def body(*refs):
        r_ref = refs[na + nb] if has_r else None
        o_ref = refs[na + nb + has_r]
        side_by_side = lambda rs: rs[0][...] if len(rs) == 1 else jnp.concatenate([r[...].astype(BF16) for r in rs], axis=1)
        part = _dot(side_by_side(refs[:na]).astype(BF16), side_by_side(refs[na:na + nb]).astype(BF16), kind)

        def finish(total):
            if has_r:
                total = total + r_ref[...].astype(F32)
            o_ref[...] = total.astype(out_dtype)

        if nk == 1:
            finish(part)
        else:
            acc_ref = refs[na + nb + has_r + 1]
            k = pl.program_id(len(grid) - 1)

            @pl.when(k == 0)
            def _():
                acc_ref[...] = part

            @pl.when(k > 0)
            def _():
                acc_ref[...] += part

            @pl.when(k == nk - 1)
            def _():
                finish(acc_ref[...])

    in_specs = a_specs + b_specs + ([r_spec] if has_r else [])
    args = (*a_args, *b_args) + ((resid,) if has_r else ())
    sem = ("parallel",) * (len(grid) - 1) + ("arbitrary",)
    res, copies = _carrier_call(
        body, name=name, grid=grid, in_specs=in_specs, out_specs=[o_spec],
        out_shape=[jax.ShapeDtypeStruct(out_shape, out_dtype)],
        scratch_shapes=[pltpu.VMEM(acc_shape, F32)] if nk > 1 else [], args=args, sem=sem, carry=carry)
    if carry is not None:
        carry.done(copies)
    return res[0]


def _rms_fwd(name, x, g, col_block=0):
    c = g.shape[1]

    def body(x_ref, g_ref, u_ref):
        xv = x_ref[...]
        r = lax.rsqrt(jnp.mean(xv * xv, axis=-1, keepdims=True) + RMS_EPS)
        u_ref[...] = (xv * r * g_ref[...]).astype(BF16)

    return pl.pallas_call(
        body, name=name, grid=(S // TM,),
        in_specs=[pl.BlockSpec((TM, c), lambda i: (i, col_block)), pl.BlockSpec((1, c), lambda i: (0, 0))],
        out_specs=pl.BlockSpec((TM, c), lambda i: (i, 0)),
        out_shape=jax.ShapeDtypeStruct((S, c), BF16),
        compiler_params=_params("parallel"),
    )(x, g)


def _rms_bwd(name, dy, x, g, resid, carry=None):
    def body(dy_ref, x_ref, g_ref, r_ref, dx_ref, dg_ref):
        i = pl.program_id(0)
        xv = x_ref[...]
        r = lax.rsqrt(jnp.mean(xv * xv, axis=-1, keepdims=True) + RMS_EPS)
        xh = xv * r
        dyv = dy_ref[...]
        dxh = dyv * g_ref[...]
        dx_ref[...] = r_ref[...] + r * (dxh - xh * jnp.mean(dxh * xh, axis=-1, keepdims=True))
        part = jnp.sum(dyv * xh, axis=0, keepdims=True)

        @pl.when(i == 0)
        def _():
            dg_ref[...] = part

        @pl.when(i > 0)
        def _():
            dg_ref[...] += part

    row = pl.BlockSpec((TM, D), lambda i: (i, 0))
    vec = pl.BlockSpec((1, D), lambda i: (0, 0))
    res, copies = _carrier_call(
        body, name=name, grid=(S // TM,), in_specs=[row, row, vec, row], out_specs=[row, vec],
        out_shape=[jax.ShapeDtypeStruct((S, D), F32), jax.ShapeDtypeStruct((1, D), F32)],
        args=(dy, x, g, resid), sem=("arbitrary",), carry=carry)
    if carry is not None:
        carry.done(copies)
    return res


def _loss_bwd(name, h, g, tgt):
    def body(h_ref, g_ref, t_ref, loss_ref, dh_ref, dg_ref):
        i = pl.program_id(0)
        xv = h_ref[...]
        gv = g_ref[...]
        r = lax.rsqrt(jnp.mean(xv * xv, axis=-1, keepdims=True) + RMS_EPS)
        xh = xv * r
        diff = xh * gv - t_ref[...]
        part_loss = 0.5 * jnp.sum(jnp.sum(diff * diff, axis=-1, keepdims=True) * (1.0 / D), axis=0, keepdims=True)
        dy = diff * (1.0 / D)
        dxh = dy * gv
        dh_ref[...] = r * (dxh - xh * jnp.mean(dxh * xh, axis=-1, keepdims=True))
        part_g = jnp.sum(dy * xh, axis=0, keepdims=True)

        @pl.when(i == 0)
        def _():
            dg_ref[...] = part_g
            loss_ref[...] = jnp.broadcast_to(part_loss, (1, LANES))

        @pl.when(i > 0)
        def _():
            dg_ref[...] += part_g
            loss_ref[...] += jnp.broadcast_to(part_loss, (1, LANES))

    row = pl.BlockSpec((TM, D), lambda i: (i, 0))
    vec = pl.BlockSpec((1, D), lambda i: (0, 0))
    return pl.pallas_call(
        body, name=name, grid=(S // TM,), in_specs=[row, vec, row],
        out_specs=[pl.BlockSpec((1, LANES), lambda i: (0, 0)), row, vec],
        out_shape=[jax.ShapeDtypeStruct((1, LANES), F32), jax.ShapeDtypeStruct((S, D), F32),
                   jax.ShapeDtypeStruct((1, D), F32)],
        compiler_params=_params("arbitrary"),
    )(h, g, tgt)


def _ffn_fwd(name, h, g, wg, wu, wd, carry=None):
    def body(h_ref, g_ref, wg_ref, wu_ref, wd_ref, o_ref, gate_ref, up_ref, u_scr):
        s = pl.program_id(1)

        @pl.when(s == 0)
        def _():
            xv = h_ref[...]
            r = lax.rsqrt(jnp.mean(xv * xv, axis=-1, keepdims=True) + RMS_EPS)
            u_scr[...] = (xv * r * g_ref[...]).astype(BF16)
            o_ref[...] = xv

        u = u_scr[...]
        gate = _dot(u, wg_ref[...], "nt")
        up = _dot(u, wu_ref[...], "nt")
        act = gate * _sigmoid(gate) * up
        o_ref[...] += _dot(act.astype(BF16), wd_ref[...])
        gate_ref[...] = gate.astype(BF16)
        up_ref[...] = up.astype(BF16)

    row = pl.BlockSpec((TM, D), lambda i, s: (i, 0))
    hid = pl.BlockSpec((None, TM, FF_SHARD), lambda i, s: (s, i, 0))
    return _carrier_call(
        body, name=name, grid=(S // TM, N_CHIPS),
        in_specs=[row, pl.BlockSpec((1, D), lambda i, s: (0, 0))]
        + [pl.BlockSpec((None, FF_SHARD, D), lambda i, s: (s, 0, 0))] * 3,
        out_specs=[row, hid, hid],
        out_shape=[jax.ShapeDtypeStruct((S, D), F32), jax.ShapeDtypeStruct((N_CHIPS, S, FF_SHARD), BF16),
                   jax.ShapeDtypeStruct((N_CHIPS, S, FF_SHARD), BF16)],
        scratch_shapes=[pltpu.VMEM((TM, D), BF16)], args=(h, g, wg, wu, wd), sem=("parallel", "arbitrary"), carry=carry)


def _ffn_bwd(name, dh, h, g, gate, up, wg, wu, wd):
    def body(dh_ref, h_ref, g_ref, gate_ref, up_ref, wg_ref, wu_ref, wd_ref,
             dhin_ref, dg_ref, u_ref, dgate_ref, dup_ref, act_ref, dhb_scr, du_scr):
        i = pl.program_id(0)
        s = pl.program_id(1)

        @pl.when(s == 0)
        def _():
            xv = h_ref[...]
            r = lax.rsqrt(jnp.mean(xv * xv, axis=-1, keepdims=True) + RMS_EPS)
            u_ref[...] = (xv * r * g_ref[...]).astype(BF16)
            dhb_scr[...] = dh_ref[...].astype(BF16)
            du_scr[...] = jnp.zeros_like(du_scr)

        dact = _dot(dhb_scr[...], wd_ref[...], "nt")
        gv = gate_ref[...].astype(F32)
        uv = up_ref[...].astype(F32)
        sig = _sigmoid(gv)
        sil = gv * sig
        dup = dact * sil
        dgate = dact * uv * (sig * (1.0 + gv * (1.0 - sig)))
        dgb = dgate.astype(BF16)
        dub = dup.astype(BF16)
        act_ref[...] = (sil * uv).astype(BF16)
        dgate_ref[...] = dgb
        dup_ref[...] = dub
        du_scr[...] += _dot(dgb, wg_ref[...]) + _dot(dub, wu_ref[...])

        @pl.when(s == N_CHIPS - 1)
        def _():
            xv = h_ref[...]
            r = lax.rsqrt(jnp.mean(xv * xv, axis=-1, keepdims=True) + RMS_EPS)
            xh = xv * r
            du = du_scr[...]
            dxh = du * g_ref[...]
            dhin_ref[...] = dh_ref[...] + r * (dxh - xh * jnp.mean(dxh * xh, axis=-1, keepdims=True))
            part = jnp.sum(du * xh, axis=0, keepdims=True)

            @pl.when(i == 0)
            def _():
                dg_ref[...] = part

            @pl.when(i > 0)
            def _():
                dg_ref[...] += part

    row = pl.BlockSpec((TM, D), lambda i, s: (i, 0))
    vec = pl.BlockSpec((1, D), lambda i, s: (0, 0))
    hid = pl.BlockSpec((None, TM, FF_SHARD), lambda i, s: (s, i, 0))
    hid_shape = jax.ShapeDtypeStruct((N_CHIPS, S, FF_SHARD), BF16)
    return pl.pallas_call(
        body, name=name, grid=(S // TM, N_CHIPS),
        in_specs=[row, row, vec, hid, hid] + [pl.BlockSpec((None, FF_SHARD, D), lambda i, s: (s, 0, 0))] * 3,
        out_specs=[row, vec, row, hid, hid, hid],
        out_shape=[jax.ShapeDtypeStruct((S, D), F32), jax.ShapeDtypeStruct((1, D), F32),
                   jax.ShapeDtypeStruct((S, D), BF16), hid_shape, hid_shape, hid_shape],
        scratch_shapes=[pltpu.VMEM((TM, D), BF16), pltpu.VMEM((TM, D), F32)],
        compiler_params=_params("arbitrary", "arbitrary"),
    )(dh, h, g, gate, up, wg, wu, wd)


def _ffn_wgrads(name, u, dgate, dup, act, dh):
    nk = S // TQ

    def body(u_ref, dh_ref, dgate_ref, dup_ref, act_ref, dg_ref, du_ref, dd_ref, acc_g, acc_u, acc_d):
        k = pl.program_id(1)
        u = u_ref[...]
        parts = (_dot(dgate_ref[...], u, "tn"), _dot(dup_ref[...], u, "tn"),
                 _dot(act_ref[...], dh_ref[...].astype(BF16), "tn"))
        accs = (acc_g, acc_u, acc_d)

        @pl.when(k == 0)
        def _():
            for acc, part in zip(accs, parts):
                acc[...] = part

        @pl.when(k > 0)
        def _():
            for acc, part in zip(accs, parts):
                acc[...] += part

        @pl.when(k == nk - 1)
        def _():
            for out, acc in zip((dg_ref, du_ref, dd_ref), accs):
                out[...] = acc[...].astype(BF16)

    tok = pl.BlockSpec((TQ, D), lambda s, k: (k, 0))
    hid = pl.BlockSpec((None, TQ, FF_SHARD), lambda s, k: (s, k, 0))
    out = pl.BlockSpec((None, FF_SHARD, D), lambda s, k: (s, 0, 0))
    shape = jax.ShapeDtypeStruct((N_CHIPS, FF_SHARD, D), BF16)
    return pl.pallas_call(
        body, name=name, grid=(N_CHIPS, nk), in_specs=[tok, tok, hid, hid, hid], out_specs=[out, out, out],
        out_shape=[shape, shape, shape], scratch_shapes=[pltpu.VMEM((FF_SHARD, D), F32)] * 3,
        compiler_params=_params("parallel", "arbitrary"))(u, dh, dgate, dup, act)


def _rope_tables():
    pos = jnp.arange(S, dtype=F32)
    inv = ROPE_THETA ** (-jnp.arange(0, MLA_ROPE, 2, dtype=F32) / MLA_ROPE)
    ang = pos[:, None] * inv[None, :]
    half = MLA_ROPE // 2
    cos = jnp.cos(ang)
    sin = jnp.sin(ang)
    one = jnp.ones((S, KR_LANE), F32)
    zero = jnp.zeros((S, KR_LANE), F32)
    tail_one = jnp.ones((S, LANES - KR_LANE - MLA_ROPE), F32)
    tail_zero = jnp.zeros((S, LANES - KR_LANE - MLA_ROPE), F32)
    cos_t = jnp.concatenate([one, cos, cos, tail_one], axis=1)
    sin_t = jnp.concatenate([zero, -sin, sin, tail_zero], axis=1)
    assert cos_t.shape == (S, LANES) and half * 2 == MLA_ROPE
    return cos_t, sin_t


def _rope(x, cos_t, sin_t, sign):
    n = x.shape[1] // LANES
    half = MLA_ROPE // 2
    lane = _iota(x.shape, 1) & (LANES - 1)
    first = (lane >= KR_LANE) & (lane < KR_LANE + half)
    swapped = jnp.where(first, pltpu.roll(x, x.shape[1] - half, 1), pltpu.roll(x, half, 1))
    c = jnp.tile(cos_t, (1, n)) if n > 1 else cos_t
    s = jnp.tile(sin_t, (1, n)) if n > 1 else sin_t
    return x * c + swapped * (s * sign)


def _mla_prep_fwd(name, proj, g_cq, g_ckv, w_uq, w_uk, w_uv, cos_t, sin_t):
    nh = MLA_H * LANES

    def body(cq_ref, ckv_ref, kr_ref, gq_ref, gkv_ref, wq_ref, wk_ref, wv_ref, cos_ref, sin_ref,
             qa_ref, ka_ref, va_ref):
        cos_v, sin_v = cos_ref[...], sin_ref[...]
        cq = cq_ref[...]
        r = lax.rsqrt(jnp.mean(cq * cq, axis=-1, keepdims=True) + RMS_EPS)
        cqn = (cq * r * gq_ref[...]).astype(BF16)
        qa_ref[...] = _rope(_dot(cqn, wq_ref[...]), cos_v, sin_v, 1.0).astype(BF16)
        ckv = ckv_ref[...]
        r = lax.rsqrt(jnp.mean(ckv * ckv, axis=-1, keepdims=True) + RMS_EPS)
        ckvn = (ckv * r * gkv_ref[...]).astype(BF16)
        lane = _iota((TM, LANES), 1)
        rot = (lane >= KR_LANE) & (lane < KR_LANE + MLA_ROPE)
        kr = jnp.where(rot, _rope(kr_ref[...], cos_v, sin_v, 1.0), 0.0)
        ka_ref[...] = (_dot(ckvn, wk_ref[...]) + jnp.tile(kr, (1, MLA_H))).astype(BF16)
        va_ref[...] = _dot(ckvn, wv_ref[...]).astype(BF16)

    full = lambda shape: pl.BlockSpec(shape, lambda i: (0, 0))
    return pl.pallas_call(
        body, name=name, grid=(S // TM,),
        in_specs=[pl.BlockSpec((TM, Q_LORA), lambda i: (i, P_CQ // Q_LORA)),
                  pl.BlockSpec((TM, KV_LORA), lambda i: (i, P_CKV // KV_LORA)),
                  pl.BlockSpec((TM, LANES), lambda i: (i, P_KR // LANES)),
                  full((1, Q_LORA)), full((1, KV_LORA)), full((Q_LORA, nh)), full((KV_LORA, nh)),
                  full((KV_LORA, MLA_H * MLA_V)),
                  pl.BlockSpec((TM, LANES), lambda i: (i, 0)), pl.BlockSpec((TM, LANES), lambda i: (i, 0))],
        out_specs=[pl.BlockSpec((TM, nh), lambda i: (i, 0)), pl.BlockSpec((TM, nh), lambda i: (i, 0)),
                   pl.BlockSpec((TM, MLA_H * MLA_V), lambda i: (i, 0))],
        out_shape=[jax.ShapeDtypeStruct((S, nh), BF16), jax.ShapeDtypeStruct((S, nh), BF16),
                   jax.ShapeDtypeStruct((S, MLA_H * MLA_V), BF16)],
        compiler_params=_params("parallel"),
    )(proj, proj, proj, g_cq, g_ckv, w_uq, w_uk, w_uv, cos_t, sin_t)


def _mla_prep_bwd(name, dqa, dka, dva, proj, g_cq, g_ckv, w_uq, w_uk, w_uv, cos_t, sin_t):
    nh = MLA_H * LANES

    def body(dqa_ref, dka_ref, dva_ref, cq_ref, ckv_ref, gq_ref, gkv_ref, wq_ref, wk_ref, wv_ref, cos_ref, sin_ref,
             dcq_ref, dckv_ref, dkr_ref, dwq_ref, dwk_ref, dwv_ref, dgq_ref, dgkv_ref):
        i = pl.program_id(0)
        cos_v, sin_v = cos_ref[...], sin_ref[...]

        def norm_bwd(x, g, dn):
            r = lax.rsqrt(jnp.mean(x * x, axis=-1, keepdims=True) + RMS_EPS)
            xh = x * r
            dxh = dn * g
            dx = r * (dxh - xh * jnp.mean(dxh * xh, axis=-1, keepdims=True))
            return dx, jnp.sum(dn * xh, axis=0, keepdims=True), (xh * g).astype(BF16)

        dq = _rope(dqa_ref[...], cos_v, sin_v, -1.0).astype(BF16)
        dcqn = _dot(dq, wq_ref[...], "nt")
        dcq, dgq, cqn = norm_bwd(cq_ref[...], gq_ref[...], dcqn)
        dcq_ref[...] = dcq.astype(BF16)
        dwq = _dot(cqn, dq, "tn")

        dka = dka_ref[...]
        dkab = dka.astype(BF16)
        dvab = dva_ref[...].astype(BF16)
        dckvn = _dot(dkab, wk_ref[...], "nt") + _dot(dvab, wv_ref[...], "nt")
        dckv, dgkv, ckvn = norm_bwd(ckv_ref[...], gkv_ref[...], dckvn)
        dckv_ref[...] = dckv.astype(BF16)
        dwk = _dot(ckvn, dkab, "tn")
        dwv = _dot(ckvn, dvab, "tn")

        fold = dka[:, 0:LANES]
        for hh in range(1, MLA_H):
            fold = fold + dka[:, hh * LANES:(hh + 1) * LANES]
        lane = _iota((TM, LANES), 1)
        rot = (lane >= KR_LANE) & (lane < KR_LANE + MLA_ROPE)
        dkr = _rope(jnp.where(rot, fold, 0.0), cos_v, sin_v, -1.0)
        dkr_ref[...] = jnp.where(rot, dkr, 0.0).astype(BF16)

        @pl.when(i == 0)
        def _():
            dwq_ref[...] = dwq
            dwk_ref[...] = dwk
            dwv_ref[...] = dwv
            dgq_ref[...] = dgq
            dgkv_ref[...] = dgkv

        @pl.when(i > 0)
        def _():
            dwq_ref[...] += dwq
            dwk_ref[...] += dwk
            dwv_ref[...] += dwv
            dgq_ref[...] += dgq
            dgkv_ref[...] += dgkv

    full = lambda shape: pl.BlockSpec(shape, lambda i: (0, 0))
    rows = lambda c: pl.BlockSpec((TM, c), lambda i: (i, 0))
    nv = MLA_H * MLA_V
    return pl.pallas_call(
        body, name=name, grid=(S // TM,),
        in_specs=[rows(nh), rows(nh), rows(nv),
                  pl.BlockSpec((TM, Q_LORA), lambda i: (i, P_CQ // Q_LORA)),
                  pl.BlockSpec((TM, KV_LORA), lambda i: (i, P_CKV // KV_LORA)),
                  full((1, Q_LORA)), full((1, KV_LORA)), full((Q_LORA, nh)), full((KV_LORA, nh)), full((KV_LORA, nv)),
                  rows(LANES), rows(LANES)],
        out_specs=[rows(Q_LORA), rows(KV_LORA), rows(LANES), full((Q_LORA, nh)), full((KV_LORA, nh)),
                   full((KV_LORA, nv)), full((1, Q_LORA)), full((1, KV_LORA))],
        out_shape=[jax.ShapeDtypeStruct((S, Q_LORA), BF16), jax.ShapeDtypeStruct((S, KV_LORA), BF16),
                   jax.ShapeDtypeStruct((S, LANES), BF16), jax.ShapeDtypeStruct((Q_LORA, nh), F32),
                   jax.ShapeDtypeStruct((KV_LORA, nh), F32), jax.ShapeDtypeStruct((KV_LORA, nv), F32),
                   jax.ShapeDtypeStruct((1, Q_LORA), F32), jax.ShapeDtypeStruct((1, KV_LORA), F32)],
        compiler_params=_params("arbitrary"),
    )(dqa, dka, dva, proj, proj, g_cq, g_ckv, w_uq, w_uk, w_uv, cos_t, sin_t)


def _head_masks(dtype):
    lane = _iota((1, LANES), 1)
    return (lane < 64).astype(dtype), (lane >= 64).astype(dtype)


def _mla_fwd(name, qa, ka, va, carry=None):
    def body(q_ref, k_ref, v_ref, o_ref, lse_ref):
        m0b, m1b = _head_masks(BF16)
        lane = _iota((QB, LANES), 1)
        left = lane < 64

        def qblock(i, _):
            r0 = pl.multiple_of(i * QB, QB)
            qs = [q_ref[pl.ds(r0, QB), hh * LANES:(hh + 1) * LANES] for hh in range(2)]
            rowc = lax.shift_right_logical(r0 + _iota((QB, QB), 0), 6)

            def kv(kb, carry):
                ms, ls, acc = carry
                c0 = pl.multiple_of(kb * QB, QB)
                v = v_ref[pl.ds(c0, QB), :]
                ok = lax.shift_right_logical(c0 + _iota((QB, QB), 1), 6) <= rowc
                new_m, new_l, alphas = [], [], []
                pv = None
                for hh in range(2):
                    k = k_ref[pl.ds(c0, QB), hh * LANES:(hh + 1) * LANES]
                    s = jnp.where(ok, _dot(qs[hh], k, "nt") * (SCALE_A * LOG2_E), NEG)
                    mn = jnp.maximum(ms[hh], jnp.max(s, axis=-1, keepdims=True))
                    p = jnp.exp2(s - mn)
                    a = jnp.exp2(ms[hh] - mn)
                    new_m.append(mn)
                    new_l.append(a * ls[hh] + jnp.sum(p, axis=-1, keepdims=True))
                    alphas.append(a)
                    part = _dot(p.astype(BF16), v * (m0b if hh == 0 else m1b))
                    pv = part if pv is None else pv + part
                acc = acc * jnp.where(left, alphas[0], alphas[1]) + pv
                return tuple(new_m), tuple(new_l), acc

            init = ((jnp.full((QB, 1), NEG, F32),) * 2, (jnp.zeros((QB, 1), F32),) * 2, jnp.zeros((QB, LANES), F32))
            ms, ls, acc = lax.fori_loop(0, i + 1, kv, init)
            o_ref[pl.ds(r0, QB), :] = acc * jnp.where(left, 1.0 / ls[0], 1.0 / ls[1])
            lse_ref[pl.ds(r0, QB), :] = jnp.where(left, ms[0] + jnp.log(ls[0]) * LOG2_E, ms[1] + jnp.log(ls[1]) * LOG2_E)
            return 0

        lax.fori_loop(0, S // QB, qblock, 0)

    pair = lambda w: pl.BlockSpec((S, w), lambda p: (0, p))
    return _carrier_call(
        body, name=name, grid=(MLA_H // 2,), in_specs=[pair(2 * LANES), pair(2 * LANES), pair(LANES)],
        out_specs=[pair(LANES), pair(LANES)],
        out_shape=[jax.ShapeDtypeStruct((S, MLA_H * MLA_V), F32), jax.ShapeDtypeStruct((S, MLA_H * MLA_V), F32)],
        args=(qa, ka, va), sem=("parallel",), carry=carry)


def _mla_bwd(name, qa, ka, va, o, lse, do, do_block0, carry=None):
    def body(q_ref, k_ref, v_ref, o_ref, lse_ref, do_ref, dq_ref, dk_ref, dv_ref):
        m0f, m1f = _head_masks(F32)
        m0b, m1b = _head_masks(BF16)
        dk_ref[...] = jnp.zeros_like(dk_ref)
        dv_ref[...] = jnp.zeros_like(dv_ref)

        def qblock(i, _):
            r0 = pl.multiple_of(i * QB, QB)
            rows = pl.ds(r0, QB)
            do_f = do_ref[rows, :]
            prod = do_f * o_ref[rows, :]
            deltas = [jnp.sum(prod * m0f, axis=-1, keepdims=True), jnp.sum(prod * m1f, axis=-1, keepdims=True)]
            lse_v = lse_ref[rows, :]
            lses = [lse_v[:, 0:1], lse_v[:, 64:65]]
            dob = do_f.astype(BF16)
            dos = [dob * m0b, dob * m1b]
            qs = [q_ref[rows, hh * LANES:(hh + 1) * LANES] for hh in range(2)]
            rowc = lax.shift_right_logical(r0 + _iota((QB, QB), 0), 6)

            def kv(kb, dqs):
                c0 = pl.multiple_of(kb * QB, QB)
                cols = pl.ds(c0, QB)
                v = v_ref[cols, :]
                ok = lax.shift_right_logical(c0 + _iota((QB, QB), 1), 6) <= rowc
                out = []
                dv = None
                for hh in range(2):
                    k = k_ref[cols, hh * LANES:(hh + 1) * LANES]
                    s = _dot(qs[hh], k, "nt") * (SCALE_A * LOG2_E)
                    p = jnp.where(ok, jnp.exp2(s - lses[hh]), 0.0)
                    dp = _dot(dos[hh], v, "nt")
                    ds = (p * (dp - deltas[hh]) * SCALE_A).astype(BF16)
                    out.append(dqs[hh] + _dot(ds, k))
                    dk_ref[cols, hh * LANES:(hh + 1) * LANES] += _dot(ds, qs[hh], "tn")
                    part = _dot(p.astype(BF16), dos[hh], "tn")
                    dv = part if dv is None else dv + part
                dv_ref[cols, :] += dv
                return tuple(out)

            dqs = lax.fori_loop(0, i + 1, kv, (jnp.zeros((QB, LANES), F32),) * 2)
            for hh in range(2):
                dq_ref[rows, hh * LANES:(hh + 1) * LANES] = dqs[hh]
            return 0

        lax.fori_loop(0, S // QB, qblock, 0)

    pair = lambda w: pl.BlockSpec((S, w), lambda p: (0, p))
    return _carrier_call(
        body, name=name, grid=(MLA_H // 2,),
        in_specs=[pair(2 * LANES), pair(2 * LANES), pair(LANES), pair(LANES), pair(LANES),
                  pl.BlockSpec((S, LANES), lambda p: (0, do_block0 + p))],
        out_specs=[pair(2 * LANES), pair(2 * LANES), pair(LANES)],
        out_shape=[jax.ShapeDtypeStruct((S, MLA_H * LANES), F32), jax.ShapeDtypeStruct((S, MLA_H * LANES), F32),
                   jax.ShapeDtypeStruct((S, MLA_H * MLA_V), F32)],
        args=(qa, ka, va, o, lse, do), sem=("parallel",), carry=carry)


def _sb_weights(q_h, k, c, before, tri_suffix):
    z = _dot(q_h, k, "nt") * (SCALE_B * LOG2_E)
    sp = jnp.maximum(z, 0.0) + jnp.log(1.0 + jnp.exp2(-jnp.abs(z))) * LOG2_E
    log_keep = jnp.where(before, -sp, 0.0)
    to_the_right, total = _running_sum(log_keep, tri_suffix, True)
    w = jnp.where(before, jnp.exp2(z - sp + to_the_right + c), 0.0)
    return w, jnp.exp2(z - sp), total


def _sb_fwd(name, proj, carry=None):
    def body(q_ref, k_ref, v_ref, o_ref):
        m0b, m1b = _head_masks(BF16)
        tri_suffix = (_iota((MXU_W, MXU_W), 0) > _iota((MXU_W, MXU_W), 1)).astype(BF16)

        def qblock(i, _):
            r0 = pl.multiple_of(i * QB, QB)
            q = q_ref[pl.ds(r0, QB), :].astype(BF16)
            qs = [q * m0b, q * m1b]
            rowg = r0 + _iota((QB, QB), 0)

            def kv(step, carry):
                cs, acc = carry
                c0 = pl.multiple_of((i - step) * QB, QB)
                k = k_ref[pl.ds(c0, QB), :].astype(BF16)
                v = v_ref[pl.ds(c0, QB), :].astype(BF16)
                before = (c0 + _iota((QB, QB), 1)) < rowg
                new_c = []
                for hh in range(2):
                    w, _, tot = _sb_weights(qs[hh], k, cs[hh], before, tri_suffix)
                    new_c.append(cs[hh] + tot)
                    acc = acc + _dot(w.astype(BF16), v * (m0b if hh == 0 else m1b))
                return tuple(new_c), acc

            init = ((jnp.zeros((QB, 1), F32),) * 2, jnp.zeros((QB, LANES), F32))
            _, acc = lax.fori_loop(0, i + 1, kv, init)
            o_ref[pl.ds(r0, QB), :] = acc.astype(BF16)
            return 0

        lax.fori_loop(0, S // QB, qblock, 0)

    col = lambda base: pl.BlockSpec((S, LANES), lambda p: (0, base // LANES + p))
    return _carrier_call(
        body, name=name, grid=(SB_H // 2,), in_specs=[col(P_QB), col(P_KB), col(P_VB)],
        out_specs=[pl.BlockSpec((S, LANES), lambda p: (0, p))],
        out_shape=[jax.ShapeDtypeStruct((S, SB_H * SB_DIM), BF16)],
        args=(proj, proj, proj), sem=("parallel",), carry=carry)


def _sb_bwd(name, proj, do, do_block0, carry=None):
    nb = S // QB

    def body(q_ref, k_ref, v_ref, do_ref, dq_ref, dk_ref, dv_ref, sig_scr, dl_scr, dk_acc, dv_acc):
        m0b, m1b = _head_masks(BF16)
        tri_suffix = (_iota((MXU_W, MXU_W), 0) > _iota((MXU_W, MXU_W), 1)).astype(BF16)
        tri_prefix = (_iota((MXU_W, MXU_W), 0) < _iota((MXU_W, MXU_W), 1)).astype(BF16)
        dk_acc[...] = jnp.zeros_like(dk_acc)
        dv_acc[...] = jnp.zeros_like(dv_acc)

        def qblock(i, _):
            r0 = pl.multiple_of(i * QB, QB)
            rows = pl.ds(r0, QB)
            q = q_ref[rows, :].astype(BF16)
            qs = [q * m0b, q * m1b]
            dob = do_ref[rows, :].astype(BF16)
            dos = [dob * m0b, dob * m1b]
            rowg = r0 + _iota((QB, QB), 0)

            def sweep_left(step, cs):
                kb = i - step
                c0 = pl.multiple_of(kb * QB, QB)
                cols = pl.ds(c0, QB)
                k = k_ref[cols, :].astype(BF16)
                v = v_ref[cols, :].astype(BF16)
                before = (c0 + _iota((QB, QB), 1)) < rowg
                new_c = []
                dv = None
                for hh in range(2):
                    w, sig, tot = _sb_weights(qs[hh], k, cs[hh], before, tri_suffix)
                    new_c.append(cs[hh] + tot)
                    sig_scr[hh, kb] = sig
                    dl_scr[hh, kb] = _dot(dos[hh], v, "nt") * w
                    part = _dot(w.astype(BF16), dos[hh], "tn")
                    dv = part if dv is None else dv + part
                dv_acc[cols, :] += dv
                return tuple(new_c)

            lax.fori_loop(0, i + 1, sweep_left, (jnp.zeros((QB, 1), F32),) * 2)

            def sweep_right(kb, carry):
                ps, dq = carry
                c0 = pl.multiple_of(kb * QB, QB)
                cols = pl.ds(c0, QB)
                k = k_ref[cols, :].astype(BF16)
                before = (c0 + _iota((QB, QB), 1)) < rowg
                new_p = []
                dk = None
                for hh in range(2):
                    dl = dl_scr[hh, kb]
                    sig = sig_scr[hh, kb]
                    to_the_left, total = _running_sum(dl, tri_prefix, False)
                    earlier = to_the_left + ps[hh]
                    new_p.append(ps[hh] + total)
                    dz = (jnp.where(before, dl * (1.0 - sig) - earlier * sig, 0.0) * SCALE_B).astype(BF16)
                    dq = dq + _dot(dz, k * (m0b if hh == 0 else m1b))
                    part = _dot(dz, qs[hh], "tn")
                    dk = part if dk is None else dk + part
                dk_acc[cols, :] += dk
                return tuple(new_p), dq

            init = ((jnp.zeros((QB, 1), F32),) * 2, jnp.zeros((QB, LANES), F32))
            _, dq = lax.fori_loop(0, i + 1, sweep_right, init)
            dq_ref[rows, :] = dq.astype(BF16)
            return 0

        lax.fori_loop(0, nb, qblock, 0)
        dk_ref[...] = dk_acc[...].astype(BF16)
        dv_ref[...] = dv_acc[...].astype(BF16)

    col = lambda base: pl.BlockSpec((S, LANES), lambda p: (0, base // LANES + p))
    out = pl.BlockSpec((S, LANES), lambda p: (0, p))
    shape = jax.ShapeDtypeStruct((S, SB_H * SB_DIM), BF16)
    return _carrier_call(
        body, name=name, grid=(SB_H // 2,),
        in_specs=[col(P_QB), col(P_KB), col(P_VB), pl.BlockSpec((S, LANES), lambda p: (0, do_block0 + p))],
        out_specs=[out, out, out], out_shape=[shape, shape, shape],
        scratch_shapes=[pltpu.VMEM((2, nb, QB, QB), F32), pltpu.VMEM((2, nb, QB, QB), F32),
                        pltpu.VMEM((S, LANES), F32), pltpu.VMEM((S, LANES), F32)],
        args=(proj, proj, proj, do), sem=("parallel",), carry=carry)


def _band_row_index():
    j = np.arange(TOEP_W)
    rel = np.clip(LEFT_CHUNKS * CHUNK - j, -REL_CLIP, REL_CLIP) + REL_CLIP
    rel[BAND_W:] = 2 * REL_CLIP
    return rel.astype(np.int32)


def _band_tiles(r0_ref, q_ref, kpad, vpad, m, m0b, m1b, static_ok, bias):
    r0 = pl.multiple_of(m * BQ, BQ)
    q = q_ref[0, pl.ds(r0, BQ), :]
    kw = kpad[pl.ds(r0, BAND_W), :]
    vw = vpad[pl.ds(r0, BAND_W), :]
    ok = static_ok & ((r0 - BAND_PAD + _iota((BQ, BAND_W), 1)) >= 0)
    qs = [q * m0b, q * m1b]
    ps = []
    for hh in range(2):
        s = jnp.where(ok, _dot(qs[hh], kw, "nt") * (SCALE_C * LOG2_E) + bias[hh], NEG)
        e = jnp.exp2(s - jnp.max(s, axis=-1, keepdims=True))
        ps.append(e * (1.0 / jnp.sum(e, axis=-1, keepdims=True)))
    return r0, qs, kw, vw, ps


def _band_setup(qkv_ref, r0_ref, kpad, vpad):
    kpad[0:BAND_PAD, :] = jnp.zeros((BAND_PAD, LANES), BF16)
    vpad[0:BAND_PAD, :] = jnp.zeros((BAND_PAD, LANES), BF16)
    kpad[BAND_PAD:, :] = qkv_ref[1]
    vpad[BAND_PAD:, :] = qkv_ref[2]
    jc = lax.shift_right_logical(_iota((BQ, BAND_W), 1), 6)
    rc = lax.shift_right_logical(_iota((BQ, BAND_W), 0), 6)
    static_ok = (jc >= rc) & (jc <= rc + LEFT_CHUNKS)
    bias = []
    for hh in range(2):
        row = jnp.broadcast_to(r0_ref[hh:hh + 1, :] * LOG2_E, (BQ, TOEP_W))
        bias.append(pltpu.roll(row, 0, 1, stride=1, stride_axis=0)[:, :BAND_W])
    return static_ok, bias


def _band_fwd(name, qkv, r0, carry=None):
    def body(qkv_ref, r0_ref, o_ref, kpad, vpad):
        m0b, m1b = _head_masks(BF16)
        static_ok, bias = _band_setup(qkv_ref, r0_ref, kpad, vpad)

        def qblock(m, _):
            r0_, _, _, vw, ps = _band_tiles(r0_ref, qkv_ref, kpad, vpad, m, m0b, m1b, static_ok, bias)
            o = _dot(ps[0].astype(BF16), vw * m0b) + _dot(ps[1].astype(BF16), vw * m1b)
            o_ref[pl.ds(r0_, BQ), :] = o.astype(BF16)
            return 0

        lax.fori_loop(0, S // BQ, qblock, 0)

    return _carrier_call(
        body, name=name, grid=(C_H // 2,),
        in_specs=[pl.BlockSpec((3, S, LANES), lambda p: (0, 0, p)), pl.BlockSpec((None, 2, TOEP_W), lambda p: (p, 0, 0))],
        out_specs=[pl.BlockSpec((S, LANES), lambda p: (0, p))],
        out_shape=[jax.ShapeDtypeStruct((S, C_H * C_DIM), BF16)],
        scratch_shapes=[pltpu.VMEM((S + BAND_PAD, LANES), BF16), pltpu.VMEM((S + BAND_PAD, LANES), BF16)],
        args=(qkv, r0), sem=("parallel",), carry=carry)


def _band_bwd(name, qkv, r0, do, carry=None):
    def body(qkv_ref, r0_ref, do_ref, dqkv_ref, dr0_ref, kpad, vpad, dkpad, dvpad, db_acc):
        m0b, m1b = _head_masks(BF16)
        static_ok, bias = _band_setup(qkv_ref, r0_ref, kpad, vpad)
        dkpad[...] = jnp.zeros_like(dkpad)
        dvpad[...] = jnp.zeros_like(dvpad)
        db_acc[...] = jnp.zeros_like(db_acc)

        def qblock(m, _):
            r0_, qs, kw, vw, ps = _band_tiles(r0_ref, qkv_ref, kpad, vpad, m, m0b, m1b, static_ok, bias)
            dob = do_ref[pl.ds(r0_, BQ), :].astype(BF16)
            dos = [dob * m0b, dob * m1b]
            dq = None
            dk = None
            dv = None
            for hh in range(2):
                p = ps[hh]
                dp = _dot(dos[hh], vw, "nt")
                ds = p * (dp - jnp.sum(dp * p, axis=-1, keepdims=True))
                db_acc[hh, :, 0:BAND_W] += ds
                dsb = (ds * SCALE_C).astype(BF16)
                t = _dot(dsb, kw * (m0b if hh == 0 else m1b))
                dq = t if dq is None else dq + t
                t = _dot(dsb, qs[hh], "tn")
                dk = t if dk is None else dk + t
                t = _dot(p.astype(BF16), dos[hh], "tn")
                dv = t if dv is None else dv + t
            dqkv_ref[0, pl.ds(r0_, BQ), :] = dq.astype(BF16)
            dkpad[pl.ds(r0_, BAND_W), :] += dk
            dvpad[pl.ds(r0_, BAND_W), :] += dv
            return 0

        lax.fori_loop(0, S // BQ, qblock, 0)
        dqkv_ref[1] = dkpad[BAND_PAD:, :].astype(BF16)
        dqkv_ref[2] = dvpad[BAND_PAD:, :].astype(BF16)
        sub = _iota((8, TOEP_W), 0)
        for hh in range(2):
            folded = db_acc[hh, 0:8, :]
            for a in range(1, BQ // 8):
                folded = folded + pltpu.roll(db_acc[hh, 8 * a:8 * a + 8, :], TOEP_W - 8 * a, 1)
            for bit in range(3):
                moved = pltpu.roll(folded, TOEP_W - (1 << bit), 1)
                folded = jnp.where((sub & (1 << bit)) != 0, moved, folded)
            dr0_ref[hh:hh + 1, :] = jnp.sum(folded, axis=0, keepdims=True)

    return _carrier_call(
        body, name=name, grid=(C_H // 2,),
        in_specs=[pl.BlockSpec((3, S, LANES), lambda p: (0, 0, p)), pl.BlockSpec((None, 2, TOEP_W), lambda p: (p, 0, 0)),
                  pl.BlockSpec((S, LANES), lambda p: (0, p))],
        out_specs=[pl.BlockSpec((3, S, LANES), lambda p: (0, 0, p)), pl.BlockSpec((None, 2, TOEP_W), lambda p: (p, 0, 0))],
        out_shape=[jax.ShapeDtypeStruct((3, S, C_H * C_DIM), BF16), jax.ShapeDtypeStruct((C_H // 2, 2, TOEP_W), F32)],
        scratch_shapes=[pltpu.VMEM((S + BAND_PAD, LANES), BF16), pltpu.VMEM((S + BAND_PAD, LANES), BF16),
                        pltpu.VMEM((S + BAND_PAD, LANES), F32), pltpu.VMEM((S + BAND_PAD, LANES), F32),
                        pltpu.VMEM((2, BQ, TOEP_W), F32)],
        args=(qkv, r0, do), sem=("parallel",), carry=carry)


def _bias_table_grad(name, dr0):
    w_out = 5 * LANES

    def body(d_ref, o_ref):
        j = _iota((TOEP_W, w_out), 0)
        rel = jnp.clip(LEFT_CHUNKS * CHUNK - j, -REL_CLIP, REL_CLIP) + REL_CLIP
        rel = jnp.where(j >= BAND_W, 2 * REL_CLIP, rel)
        onehot = (rel == _iota((TOEP_W, w_out), 1)).astype(BF16)
        d = d_ref[...]
        hi = d.astype(BF16)
        mid = (d - hi.astype(F32))
        mid_b = mid.astype(BF16)
        lo = (mid - mid_b.astype(F32)).astype(BF16)
        o_ref[...] = _dot(hi, onehot) + _dot(mid_b, onehot) + _dot(lo, onehot)

    return pl.pallas_call(
        body, name=name, out_shape=jax.ShapeDtypeStruct((C_H, w_out), F32),
        in_specs=[pl.BlockSpec((C_H, TOEP_W), lambda: (0, 0))], out_specs=pl.BlockSpec((C_H, w_out), lambda: (0, 0)),
        grid=(),
    )(dr0)


def _carry_gather(cy, slots, names, ici, d2d):
    idx = [cy.operand(slots[n], True) for n in names]
    n = len(names)
    base_i = cy.sems(3 * n) if ici else 0
    base_d = cy.sems(3 * n) if d2d else 0

    def piece(refs, t, slot, cc):
        return refs[idx[t]].at[slot, _half_rows(cc, slots[names[t]].shape[1] // 2), :]

    def over_ici(refs, send, recv, arriving):
        x, y, c, chips = _position()
        out = []
        for t in range(n):
            for j in range(3):
                r = piece(refs, t, 2 * chips[j][0] + chips[j][1] if arriving else 2 * x + y, c)
                out.append(_remote(r, r, send, recv, base_i + 3 * t + j, (*chips[j], c)))
        return out

    def over_d2d(refs, send, recv, arriving):
        x, y, c, chips = _position()
        out = []
        for t in range(n):
            for j in range(3):
                r = piece(refs, t, 2 * chips[j][0] + chips[j][1], 1 - c if arriving else c)
                out.append(_remote(r, r, send, recv, base_d + 3 * t + j, (x, y, 1 - c)))
        return out

    def start_ici(refs, fresh, send, recv):
        for cp in over_ici(refs, send, recv, False):
            cp.start()

    def wait_ici(refs, fresh, send, recv):
        for cp in over_ici(refs, send, recv, True):
            cp.wait_recv()
        for cp in over_ici(refs, send, recv, False):
            cp.wait_send()

    def start_d2d(refs, fresh, send, recv):
        for cp in over_d2d(refs, send, recv, False):
            cp.start()

    def wait_d2d(refs, fresh, send, recv):
        for cp in over_d2d(refs, send, recv, True):
            cp.wait_recv()
        for cp in over_d2d(refs, send, recv, False):
            cp.wait_send()

    def wait_ici_and_forward(refs, fresh, send, recv):
        forwards = over_d2d(refs, send, recv, False)
        for k, cp in enumerate(over_ici(refs, send, recv, True)):
            cp.wait_recv()
            forwards[k].start()
        for cp in over_ici(refs, send, recv, False):
            cp.wait_send()

    if ici and d2d:
        cy.starts.append(start_ici)
        cy.finishes += [wait_ici_and_forward, wait_d2d]
    elif ici:
        cy.starts.append(start_ici)
        cy.finishes.append(wait_ici)
    else:
        cy.starts.append(start_d2d)
        cy.finishes.append(wait_d2d)

    def done(aliased, fresh):
        for t, name in enumerate(names):
            slots[name] = aliased[idx[t]]

    cy.on_done.append(done)


def _carry_chip_exchange(cy, sums, got, names):
    idx = [cy.operand(sums[n], False) for n in names]
    out = [cy.result((3,) + sums[n].shape[1:], BF16) for n in names]
    base = cy.sems(3 * len(names))

    def copies(refs, fresh, send, recv):
        x, y, c, chips = _position()
        return [_remote(refs[idx[t]].at[2 * chips[j][0] + chips[j][1]], fresh[out[t]].at[j], send, recv, base + 3 * t + j,
                        (*chips[j], c)) for t in range(len(names)) for j in range(3)]

    def start(refs, fresh, send, recv):
        for cp in copies(refs, fresh, send, recv):
            cp.start()

    def wait(refs, fresh, send, recv):
        for cp in copies(refs, fresh, send, recv):
            cp.wait()

    cy.starts.append(start)
    cy.finishes.append(wait)

    def done(aliased, fresh):
        for t, name in enumerate(names):
            got[name] = fresh[out[t]]

    cy.on_done.append(done)


def _run_carry(name, cy):
    _, res = _carrier_call(None, name=name, grid=(), in_specs=[], out_specs=[], out_shape=[], args=(), sem=(), carry=cy)
    cy.done(res)


FIRST_WEIGHTS = ("ev_w_in",)
NEXT_WEIGHTS = ("ev_w_uq", "ev_w_ukv")
WEIGHTS_A = ("ev_w_out", "w_gate0", "w_up0")
WEIGHTS_B = ("w_down0", "od_w_qkv", "od_w_out")
WEIGHTS_C = ("w_gate1",)
WEIGHTS_D = ("w_up1", "w_down1")
GRAD_GROUPS = {"ffn1": ("w_gate1", "w_up1", "w_down1"), "od": ("od_w_qkv", "od_w_out"),
               "ffn0": ("w_gate0", "w_up0", "w_down0"), "ev_out": ("ev_w_out",),
               "ev": ("ev_w_in", "ev_w_uq", "ev_w_ukv")}


def _carry_pair_exchange(cy, parts, theirs, names):
    idx = [cy.operand(parts[n], False) for n in names]
    out = [cy.result((N_CHIPS, parts[n].shape[1] // 2, parts[n].shape[2]), BF16) for n in names]
    base = cy.sems(len(names))

    def copies(refs, fresh, send, recv):
        x, y, c, _ = _position()
        return [_remote(refs[idx[t]].at[:, _half_rows(1 - c, parts[n].shape[1] // 2), :], fresh[out[t]], send, recv,
                        base + t, (x, y, 1 - c)) for t, n in enumerate(names)]

    cy.starts.append(lambda refs, fresh, send, recv: [cp.start() for cp in copies(refs, fresh, send, recv)])
    cy.finishes.append(lambda refs, fresh, send, recv: [cp.wait() for cp in copies(refs, fresh, send, recv)])

    def done(aliased, fresh):
        for t, name in enumerate(names):
            theirs[name] = fresh[out[t]]

    cy.on_done.append(done)


def _carry_sibling_exchange(cy, fulls, pieces):
    idx = [cy.operand(fulls[p], True) for p, _ in pieces]
    base = cy.sems(len(pieces))

    def copies(refs, send, recv, arriving):
        x, y, c, _ = _position()
        out = []
        for t, (p, layer) in enumerate(pieces):
            r = refs[idx[t]].at[layer, _half_rows(1 - c if arriving else c, fulls[p].shape[1] // 2), :]
            out.append(_remote(r, r, send, recv, base + t, (x, y, 1 - c)))
        return out

    def start(refs, fresh, send, recv):
        for cp in copies(refs, send, recv, False):
            cp.start()

    def wait(refs, fresh, send, recv):
        for cp in copies(refs, send, recv, True):
            cp.wait_recv()
        for cp in copies(refs, send, recv, False):
            cp.wait_send()

    cy.starts.append(start)
    cy.finishes.append(wait)

    def done(aliased, fresh):
        for t, (p, _) in enumerate(pieces):
            fulls[p] = aliased[idx[t]]

    cy.on_done.append(done)


RIDES = {
    "cast_rest": (("gather", FIRST_WEIGHTS),),
    "proj_in": (("gather", NEXT_WEIGHTS),),
    "mla_attn": (("gather_ici", WEIGHTS_A),),
    "sb_attn": (("gather_d2d", WEIGHTS_A), ("gather_ici", WEIGHTS_B)),
    "ev_out": (("gather_d2d", WEIGHTS_B),),
    "ffn0": (("gather_ici", WEIGHTS_C),),
    "qkv": (("gather_d2d", WEIGHTS_C),),
    "band_attn": (("gather_ici", WEIGHTS_D),),
    "od_out": (("gather_d2d", WEIGHTS_D),),
    "od_out_bwd_w": (("pair", "ffn1"),),
    "band_attn_bwd": (("chips", "ffn1"),),
    "rms_mix1_bwd": (("pair", "od"),),
    "ev_out_bwd_w": (("pair", "ffn0"),),
    "mla_attn_bwd": (("chips", "od"), ("sibling", "ffn1"), ("pair", "ev_out")),
    "sb_attn_bwd": (("chips", "ffn0"), ("sibling", "od"), ("chips", "ev_out")),
    "proj_in_bwd_w": (("sibling", "ffn0"), ("sibling", "ev_out")),
    "grads_pair_ev": (("pair", "ev"),),
    "proj_in_bwd_x": (("chips", "ev"),),
    "grads_sibling_ev": (("sibling", "ev"),),
}


class _Exchanges:
    def __init__(self, slots, pos, shapes, cast_rest):
        self.slots, self.pos, self.shapes, self.cast_rest = dict(slots), pos, shapes, cast_rest
        self.parts, self.theirs, self.sums, self.got, self.fulls = {}, {}, {}, {}, {}

    def begin(self):
        self.slots.update(self.cast_rest(self.carry("cast_rest")))

    def weights(self, *names):
        return [self.slots[n] for n in names]

    def _pair_sums(self, group):
        names = GRAD_GROUPS[group]
        self.sums.update(zip(names, _pair_sums("pair_sums_" + group, [self.parts[n] for n in names],
                                               [self.theirs[n] for n in names], self.pos)))

    def _chip_sums(self, group):
        names = GRAD_GROUPS[group]
        items = [(self.sums[n], self.got[n], PART_OF[n][1], self.shapes[PART_OF[n][0]], self.fulls.get(PART_OF[n][0]))
                 for n in names]
        self.fulls.update(zip([PART_OF[n][0] for n in names], _chip_sums("chip_sums_" + group, items, self.pos)))

    def carry(self, stage):
        cy = _Carry()
        for step, what in RIDES[stage]:
            if step == "gather":
                _carry_gather(cy, self.slots, what, True, True)
            elif step == "gather_ici":
                _carry_gather(cy, self.slots, what, True, False)
            elif step == "gather_d2d":
                _carry_gather(cy, self.slots, what, False, True)
            elif step == "pair":
                _carry_pair_exchange(cy, self.parts, self.theirs, GRAD_GROUPS[what])
            elif step == "chips":
                self._pair_sums(what)
                _carry_chip_exchange(cy, self.sums, self.got, GRAD_GROUPS[what])
            elif step == "sibling":
                self._chip_sums(what)
                _carry_sibling_exchange(cy, self.fulls, [PART_OF[n] for n in GRAD_GROUPS[what]])
        return cy

    def grads(self, group, parts):
        self.parts.update(parts)
        if group == "ev":
            _run_carry("grads_pair_ev", self.carry("grads_pair_ev"))

    def finish(self):
        _run_carry("grads_sibling_ev", self.carry("grads_sibling_ev"))
        return {n: self.fulls[n] for n in BIG}


class _NoExchanges:
    def __init__(self, slots):
        self.slots, self.parts = dict(slots), {}

    def begin(self):
        pass

    def weights(self, *names):
        return [self.slots[n] for n in names]

    def carry(self, stage):
        return None

    def grads(self, group, parts):
        self.parts.update(parts)


def _w_in_pieces():
    segments = ((0, Q_LORA, P_CQ), (Q_LORA, Q_LORA + KV_LORA, P_CKV),
                (Q_LORA + KV_LORA, Q_LORA + KV_LORA + MLA_ROPE, P_KR + KR_LANE),
                (Q_LORA + KV_LORA + MLA_ROPE, EVEN_IN, P_QB))
    width = EVEN_IN // N_CHIPS
    pieces = []
    for lo, hi, at in segments:
        for k in range(N_CHIPS):
            a, b = max(lo, k * width), min(hi, (k + 1) * width)
            if a < b:
                pieces.append((k, a - k * width, b - a, at + a - lo))
    return pieces


def _w_in_padded(name, w_in_s):
    tr = MXU_W

    def body(s_ref, o_ref):
        o_ref[...] = jnp.zeros(o_ref.shape, BF16)
        for k, a, n, at in _w_in_pieces():
            o_ref[:, at:at + n] = s_ref[k, :, a:a + n]

    return pl.pallas_call(
        body, name=name, grid=(D // tr,),
        in_specs=[pl.BlockSpec((N_CHIPS, tr, EVEN_IN // N_CHIPS), lambda i: (0, i, 0))],
        out_specs=pl.BlockSpec((tr, P_IN), lambda i: (i, 0)), out_shape=jax.ShapeDtypeStruct((D, P_IN), BF16),
        compiler_params=_params("parallel"))(w_in_s)


def _w_in_sharded(name, d_w_in_p):
    tr = MXU_W

    def body(p_ref, o_ref):
        for k, a, n, at in _w_in_pieces():
            o_ref[k, :, a:a + n] = p_ref[:, at:at + n]

    return pl.pallas_call(
        body, name=name, grid=(D // tr,),
        in_specs=[pl.BlockSpec((tr, P_IN), lambda i: (i, 0))],
        out_specs=pl.BlockSpec((N_CHIPS, tr, EVEN_IN // N_CHIPS), lambda i: (0, i, 0)),
        out_shape=jax.ShapeDtypeStruct((N_CHIPS, D, EVEN_IN // N_CHIPS), BF16),
        compiler_params=_params("parallel"))(d_w_in_p)


def _mla_weights(w_uq_s, w_ukv_s):
    w_uq = jnp.moveaxis(w_uq_s, 0, 1).reshape(Q_LORA, MLA_H, MLA_NOPE + MLA_ROPE)
    w_uq_p = jnp.concatenate([w_uq, jnp.zeros((Q_LORA, MLA_H, LANES - MLA_NOPE - MLA_ROPE), BF16)], axis=2)
    w_ukv = jnp.moveaxis(w_ukv_s, 0, 1).reshape(KV_LORA, MLA_H, MLA_NOPE + MLA_V)
    w_uk_p = jnp.concatenate([w_ukv[:, :, :MLA_NOPE], jnp.zeros((KV_LORA, MLA_H, LANES - MLA_NOPE), BF16)], axis=2)
    return dict(
        w_uq=w_uq_p.reshape(Q_LORA, MLA_H * LANES), w_uk=w_uk_p.reshape(KV_LORA, MLA_H * LANES),
        w_uv=w_ukv[:, :, MLA_NOPE:].reshape(KV_LORA, MLA_H * MLA_V))


def _proj_mm(name, u, w_in, carry=None):
    return _mm(name, u, w_in, kind="nn", grid=(S // TM, 1, 1),
               a_spec=pl.BlockSpec((TM, D), lambda i, j, k: (i, 0)), b_spec=pl.BlockSpec((D, P_IN), lambda i, j, k: (0, 0)),
               o_spec=pl.BlockSpec((TM, P_IN), lambda i, j, k: (i, 0)), out_shape=(S, P_IN), out_dtype=F32, acc_shape=None,
               carry=carry)


def _out_proj(name, o, w, resid, carry=None):
    return _mm(name, o, w, kind="nn", grid=(S // TQ, 1, 1),
               a_spec=pl.BlockSpec((TQ, D), lambda i, j, k: (i, 0)), b_spec=pl.BlockSpec((D, D), lambda i, j, k: (0, 0)),
               o_spec=pl.BlockSpec((TQ, D), lambda i, j, k: (i, 0)), out_shape=(S, D), out_dtype=F32, acc_shape=None,
               resid=resid, r_spec=pl.BlockSpec((TQ, D), lambda i, j, k: (i, 0)), carry=carry)


def _out_proj_bwd(name, dh, o, w, ex):
    d_o = _mm(name + "_x", dh, w, kind="nt", grid=(S // TQ, 1, 1),
              a_spec=pl.BlockSpec((TQ, D), lambda i, j, k: (i, 0)), b_spec=pl.BlockSpec((D, D), lambda i, j, k: (0, 0)),
              o_spec=pl.BlockSpec((TQ, D), lambda i, j, k: (i, 0)), out_shape=(S, D), out_dtype=F32, acc_shape=None)
    d_w = _mm(name + "_w", o, dh, kind="tn", grid=(2, S // TQ),
              a_spec=pl.BlockSpec((TQ, TM), lambda j, k: (k, j)), b_spec=pl.BlockSpec((TQ, D), lambda j, k: (k, 0)),
              o_spec=pl.BlockSpec((TM, D), lambda j, k: (j, 0)), out_shape=(D, D), out_dtype=BF16, acc_shape=(TM, D),
              carry=ex.carry(name + "_w"))
    return d_o, d_w


def _local_step(x, tgt, sm, ex):
    def riding(stage, fn, *args):
        cy = ex.carry(stage)
        res, copies = fn(stage, *args, carry=cy)
        if cy is not None:
            cy.done(copies)
        return res

    cos_t, sin_t = _rope_tables()
    g_mix, g_ffn = sm["g_mix"], sm["g_ffn"]
    r0 = sm["od_rel_bias"][0][:, _band_row_index()].reshape(C_H // 2, 2, TOEP_W)
    nt = 3

    ex.begin()
    w = {"w_in": _w_in_padded("w_in_padded", *ex.weights(*FIRST_WEIGHTS))}
    u0 = _rms_fwd("rms_mix0", x, g_mix[0:1])
    proj = _proj_mm("proj_in", u0, w["w_in"], ex.carry("proj_in"))
    w.update(_mla_weights(*ex.weights(*NEXT_WEIGHTS)))
    qa, ka, va = _mla_prep_fwd("mla_prep", proj, sm["ev_g_cq"], sm["ev_g_ckv"], w["w_uq"], w["w_uk"], w["w_uv"], cos_t, sin_t)
    o_a, lse = riding("mla_attn", _mla_fwd, qa, ka, va)
    o_b, = riding("sb_attn", _sb_fwd, proj)
    o_ev = jnp.concatenate([o_a.astype(BF16), o_b], axis=1)
    w["ev_w_out"] = ex.weights("ev_w_out")[0].reshape(D, D)
    h1 = _out_proj("ev_out", o_ev, w["ev_w_out"], x, ex.carry("ev_out"))
    w["w_gate0"], w["w_up0"], w["w_down0"] = ex.weights("w_gate0", "w_up0", "w_down0")
    h2, gate0, up0 = riding("ffn0", _ffn_fwd, h1, g_ffn[0:1], w["w_gate0"], w["w_up0"], w["w_down0"])
    w["w_qkv"] = jnp.moveaxis(ex.weights("od_w_qkv")[0], 0, 1).reshape(D, nt * D)
    u2 = _rms_fwd("rms_mix1", h2, g_mix[1:2])
    qkv = _mm("qkv", u2, w["w_qkv"], kind="nn", grid=(S // TQ, nt, 1),
              a_spec=pl.BlockSpec((TQ, D), lambda i, t, k: (i, 0)), b_spec=pl.BlockSpec((D, D), lambda i, t, k: (0, t)),
              o_spec=pl.BlockSpec((None, TQ, D), lambda i, t, k: (t, i, 0)),
              out_shape=(nt, S, D), out_dtype=BF16, acc_shape=None, carry=ex.carry("qkv"))
    o_od, = riding("band_attn", _band_fwd, qkv, r0)
    w["od_w_out"] = ex.weights("od_w_out")[0].reshape(D, D)
    h3 = _out_proj("od_out", o_od, w["od_w_out"], h2, ex.carry("od_out"))
    w["w_gate1"], w["w_up1"], w["w_down1"] = ex.weights("w_gate1", "w_up1", "w_down1")
    (h4, gate1, up1), _ = _ffn_fwd("ffn1", h3, g_ffn[1:2], w["w_gate1"], w["w_up1"], w["w_down1"])

    loss, dh4, dg_final = _loss_bwd("loss", h4, sm["g_final"].reshape(1, D), tgt)

    dh3, dg_ffn1, u3, dgate, dup, act = _ffn_bwd("ffn1_bwd", dh4, h3, g_ffn[1:2], gate1, up1,
                                                 w["w_gate1"], w["w_up1"], w["w_down1"])
    d_wg1, d_wu1, d_wd1 = _ffn_wgrads("ffn1_dw", u3, dgate, dup, act, dh4)
    ex.grads("ffn1", {"w_gate1": d_wg1, "w_up1": d_wu1, "w_down1": d_wd1})

    d_ood, d_w_od_out = _out_proj_bwd("od_out_bwd", dh3, o_od, w["od_w_out"], ex)
    dqkv, dr0 = riding("band_attn_bwd", _band_bwd, qkv, r0, d_ood)
    du2 = _mm("qkv_bwd_x", dqkv, w["w_qkv"], kind="nt", grid=(S // TQ, nt),
              a_spec=pl.BlockSpec((None, TQ, D), lambda i, t: (t, i, 0)), b_spec=pl.BlockSpec((D, D), lambda i, t: (0, t)),
              o_spec=pl.BlockSpec((TQ, D), lambda i, t: (i, 0)), out_shape=(S, D), out_dtype=F32, acc_shape=(TQ, D))
    wide, per = D // MXU_W, nt * D // N_CHIPS // MXU_W
    piece = lambda r: pl.BlockSpec((None, TQ, MXU_W), lambda j, k: ((per * j + r) // wide, k, (per * j + r) % wide))
    d_w_qkv = _mm("qkv_bwd_w", u2, dqkv, kind="tn", grid=(N_CHIPS, S // TQ),
                  a_spec=pl.BlockSpec((TQ, D), lambda j, k: (k, 0)), b_spec=[piece(r) for r in range(per)],
                  o_spec=pl.BlockSpec((None, D, per * MXU_W), lambda j, k: (j, 0, 0)),
                  out_shape=(N_CHIPS, D, per * MXU_W), out_dtype=BF16, acc_shape=(D, per * MXU_W))
    shard_cols = lambda a: jnp.moveaxis(a.reshape(a.shape[0], N_CHIPS, a.shape[1] // N_CHIPS), 1, 0)
    ex.grads("od", {"od_w_qkv": d_w_qkv, "od_w_out": d_w_od_out.reshape(N_CHIPS, D // N_CHIPS, D)})
    dh2, dg_mix1 = _rms_bwd("rms_mix1_bwd", du2, h2, g_mix[1:2], dh3, carry=ex.carry("rms_mix1_bwd"))
    d_rel = _bias_table_grad("rel_bias_grad", dr0.reshape(C_H, TOEP_W))[:, :2 * REL_CLIP + 1]

    dh1, dg_ffn0, u1, dgate, dup, act = _ffn_bwd("ffn0_bwd", dh2, h1, g_ffn[0:1], gate0, up0,
                                                 w["w_gate0"], w["w_up0"], w["w_down0"])
    d_wg0, d_wu0, d_wd0 = _ffn_wgrads("ffn0_dw", u1, dgate, dup, act, dh2)
    ex.grads("ffn0", {"w_gate0": d_wg0, "w_up0": d_wu0, "w_down0": d_wd0})

    d_oev, d_w_ev_out = _out_proj_bwd("ev_out_bwd", dh1, o_ev, w["ev_w_out"], ex)
    ex.grads("ev_out", {"ev_w_out": d_w_ev_out.reshape(N_CHIPS, D // N_CHIPS, D)})
    dqa, dka, dva = riding("mla_attn_bwd", _mla_bwd, qa, ka, va, o_a, lse, d_oev, 0)
    dqb, dkb, dvb = riding("sb_attn_bwd", _sb_bwd, proj, d_oev, MLA_H * MLA_V // LANES)
    dcq, dckv, dkr, d_w_uq, d_w_uk, d_w_uv, dg_cq, dg_ckv = _mla_prep_bwd(
        "mla_prep_bwd", dqa, dka, dva, proj, sm["ev_g_cq"], sm["ev_g_ckv"], w["w_uq"], w["w_uk"], w["w_uv"], cos_t, sin_t)
    dproj = [dcq, jnp.zeros((S, LANES), BF16), dckv, dqb, dkb, dvb, dkr]
    d_w_in_p = _mm("proj_in_bwd_w", u0, dproj, kind="tn", grid=(1, S // TQ),
                   a_spec=pl.BlockSpec((TQ, D), lambda j, k: (k, 0)),
                   b_spec=[pl.BlockSpec((TQ, p.shape[1]), lambda j, k: (k, 0)) for p in dproj],
                   o_spec=pl.BlockSpec((D, P_IN), lambda j, k: (0, 0)), out_shape=(D, P_IN), out_dtype=BF16,
                   acc_shape=(D, P_IN), carry=ex.carry("proj_in_bwd_w"))
    d_w_uq_std = d_w_uq.reshape(Q_LORA, MLA_H, LANES)[:, :, :MLA_NOPE + MLA_ROPE].reshape(Q_LORA, -1)
    d_w_ukv = jnp.concatenate([d_w_uk.reshape(KV_LORA, MLA_H, LANES)[:, :, :MLA_NOPE],
                               d_w_uv.reshape(KV_LORA, MLA_H, MLA_V)], axis=2).reshape(KV_LORA, -1)
    ex.grads("ev", {"ev_w_in": _w_in_sharded("w_in_sharded", d_w_in_p), "ev_w_uq": shard_cols(d_w_uq_std.astype(BF16)),
                    "ev_w_ukv": shard_cols(d_w_ukv.astype(BF16))})
    du0 = _mm("proj_in_bwd_x", dproj, w["w_in"], kind="nt", grid=(S // TM, 1, 1),
              a_spec=[pl.BlockSpec((TM, p.shape[1]), lambda i, j, k: (i, 0)) for p in dproj],
              b_spec=pl.BlockSpec((D, P_IN), lambda i, j, k: (0, 0)),
              o_spec=pl.BlockSpec((TM, D), lambda i, j, k: (i, 0)), out_shape=(S, D), out_dtype=F32, acc_shape=None,
              carry=ex.carry("proj_in_bwd_x"))
    grad_x, dg_mix0 = _rms_bwd("rms_mix0_bwd", du0, x, g_mix[0:1], dh1)
    small = {
        "ev_g_cq": dg_cq, "ev_g_ckv": dg_ckv, "od_rel_bias": d_rel.reshape(1, C_H, 2 * REL_CLIP + 1),
        "g_mix": jnp.concatenate([dg_mix0, dg_mix1], axis=0), "g_ffn": jnp.concatenate([dg_ffn0, dg_ffn1], axis=0),
        "g_final": dg_final.reshape(D),
    }
    return loss, grad_x, small


BIG = ("ev_w_in", "ev_w_uq", "ev_w_ukv", "ev_w_out", "od_w_qkv", "od_w_out", "w_gate", "w_up", "w_down")
SMALL = ("ev_g_cq", "ev_g_ckv", "od_rel_bias", "g_mix", "g_ffn", "g_final")
WEIGHTS = ("ev_w_in", "ev_g_cq", "ev_w_uq", "ev_g_ckv", "ev_w_ukv", "ev_w_out", "od_w_qkv", "od_rel_bias", "od_w_out",
           "g_mix", "g_ffn", "w_gate", "w_up", "w_down", "g_final")
GRAD_PARTS = (("ev_w_in", "ev_w_in", 0), ("ev_w_uq", "ev_w_uq", 0), ("ev_w_ukv", "ev_w_ukv", 0),
              ("ev_w_out", "ev_w_out", 0), ("od_w_qkv", "od_w_qkv", 0), ("od_w_out", "od_w_out", 0),
              ("w_gate0", "w_gate", 0), ("w_gate1", "w_gate", 1), ("w_up0", "w_up", 0), ("w_up1", "w_up", 1),
              ("w_down0", "w_down", 0), ("w_down1", "w_down", 1))
PART_OF = {part: (param, layer) for part, param, layer in GRAD_PARTS}
TRANSPOSED = ("w_gate", "w_up")
ADAMW_TRANSPOSED = ("ev_w_in", "ev_w_uq")


def _row_tile(rows, cap=512, sublanes=16):
    for t in range(min(rows, cap), 0, -1):
        if rows % t == 0 and t % sublanes == 0:
            return t
    return rows


def _cast_into_slot(name, w, layer, pos):
    _, rows, cols = w.shape
    tr = _row_tile(rows)

    def body(pos_ref, w_ref, o_ref):
        o_ref[...] = w_ref[...].astype(BF16)

    return pl.pallas_call(
        body, name=name,
        grid_spec=pltpu.PrefetchScalarGridSpec(
            num_scalar_prefetch=1, grid=(rows // tr,),
            in_specs=[pl.BlockSpec((None, tr, cols), lambda i, p: (layer, i, 0))],
            out_specs=pl.BlockSpec((None, tr, cols), lambda i, p: (p[0], i, 0))),
        out_shape=jax.ShapeDtypeStruct((N_CHIPS, rows, cols), BF16), compiler_params=_params("arbitrary"))(pos, w)


def _cast_many_into_slots(name, items, pos, carry):
    tiles = [_row_tile(w.shape[1]) for w, _ in items]
    turns = _Turns([w.shape[1] // tr for (w, _), tr in zip(items, tiles)])

    def body(pos_ref, *refs):
        i = pl.program_id(0)
        for t in range(len(items)):
            @pl.when(turns.mine(t, i))
            def _(w_ref=refs[t], o_ref=refs[len(items) + t]):
                o_ref[...] = w_ref[...].astype(BF16)

    in_specs, out_specs, out_shape = [], [], []
    for t, ((w, layer), tr) in enumerate(zip(items, tiles)):
        _, rows, cols = w.shape
        at = turns.step(t)
        in_specs.append(pl.BlockSpec((None, tr, cols), lambda i, p, at=at, layer=layer: (layer, at(i), 0)))
        out_specs.append(pl.BlockSpec((None, tr, cols), lambda i, p, at=at: (p[0], at(i), 0)))
        out_shape.append(jax.ShapeDtypeStruct((N_CHIPS, rows, cols), BF16))
    res, copies = _carrier_call(body, name=name, grid=(turns.total,), in_specs=in_specs, out_specs=out_specs,
                                out_shape=out_shape, args=[w for w, _ in items], sem=("arbitrary",), carry=carry,
                                prefetch=(pos,))
    if carry is not None:
        carry.done(copies)
    return res


class _Turns:
    def __init__(self, counts):
        self.counts = list(counts)
        self.starts = [sum(self.counts[:t]) for t in range(len(self.counts))]
        self.total = sum(self.counts)

    def step(self, t):
        start, n = self.starts[t], self.counts[t]
        return lambda i: jnp.clip(i - start, 0, n - 1)

    def mine(self, t, i):
        return (i >= self.starts[t]) & (i < self.starts[t] + self.counts[t])


def _pair_sums(name, parts, theirs, pos):
    n, pair = len(parts), 2
    tiles = [_row_tile(b.shape[1]) for b in theirs]
    blocks = [b.shape[1] // tr for b, tr in zip(theirs, tiles)]
    turns = _Turns([N_CHIPS // pair * nb for nb in blocks])

    def body(pos_ref, *refs):
        i = pl.program_id(0)
        for t in range(n):
            @pl.when(turns.mine(t, i))
            def _(a_ref=refs[2 * t], b_ref=refs[2 * t + 1], o_ref=refs[2 * n + t]):
                o_ref[...] = (a_ref[...].astype(F32) + b_ref[...].astype(F32)).astype(BF16)

    in_specs, out_specs = [], []
    for t, (b, tr, nb) in enumerate(zip(theirs, tiles, blocks)):
        at, block = turns.step(t), (pair, tr, b.shape[2])
        in_specs.append(pl.BlockSpec(block, lambda i, p, at=at, nb=nb: (at(i) // nb, p[1] * nb + at(i) % nb, 0)))
        in_specs.append(pl.BlockSpec(block, lambda i, p, at=at, nb=nb: (at(i) // nb, at(i) % nb, 0)))
        out_specs.append(pl.BlockSpec(block, lambda i, p, at=at, nb=nb: (at(i) // nb, at(i) % nb, 0)))
    return pl.pallas_call(
        body, name=name,
        grid_spec=pltpu.PrefetchScalarGridSpec(num_scalar_prefetch=1, grid=(turns.total,), in_specs=in_specs,
                                               out_specs=out_specs),
        out_shape=[jax.ShapeDtypeStruct(b.shape, BF16) for b in theirs],
        compiler_params=_params("arbitrary"))(pos, *[a for pair in zip(parts, theirs) for a in pair])


def _chip_sums(name, items, pos):
    n = len(items)
    tiles = [_row_tile(s.shape[1]) for s, *_ in items]
    turns = _Turns([s.shape[1] // tr for (s, *_), tr in zip(items, tiles)])
    carried = [t for t, item in enumerate(items) if item[4] is not None]

    def body(pos_ref, *refs):
        i = pl.program_id(0)
        for t in range(n):
            @pl.when(turns.mine(t, i))
            def _(s_ref=refs[2 * t], g_ref=refs[2 * t + 1], o_ref=refs[2 * n + len(carried) + t]):
                o_ref[...] = ((s_ref[...].astype(F32) + g_ref[0].astype(F32)) + g_ref[1].astype(F32)) + g_ref[2].astype(F32)

    in_specs, out_specs = [], []
    for t, ((s, got, layer, full_shape, full), tr) in enumerate(zip(items, tiles)):
        at, cols, nb = turns.step(t), s.shape[2], turns.counts[t]
        in_specs.append(pl.BlockSpec((None, tr, cols), lambda i, p, at=at: (p[0], at(i), 0)))
        in_specs.append(pl.BlockSpec((3, tr, cols), lambda i, p, at=at: (0, at(i), 0)))
        out_specs.append(pl.BlockSpec((None, tr, cols), lambda i, p, at=at, nb=nb, layer=layer: (layer, p[1] * nb + at(i), 0)))
    return pl.pallas_call(
        body, name=name,
        grid_spec=pltpu.PrefetchScalarGridSpec(num_scalar_prefetch=1, grid=(turns.total,),
                                               in_specs=in_specs + [ANY] * len(carried), out_specs=out_specs),
        out_shape=[jax.ShapeDtypeStruct(item[3], F32) for item in items],
        input_output_aliases={1 + 2 * n + k: t for k, t in enumerate(carried)},
        compiler_params=_params("arbitrary"))(
            pos, *[a for item in items for a in item[:2]], *[items[t][4] for t in carried])


def _adamw_update(w, g, m, v):
    m_new = ADAM_B1 * m + (1.0 - ADAM_B1) * g
    v_new = ADAM_B2 * v + (1.0 - ADAM_B2) * (g * g)
    m_hat = m_new / (1.0 - ADAM_B1 ** ADAM_STEP)
    v_hat = v_new / (1.0 - ADAM_B2 ** ADAM_STEP)
    return -ADAM_LR * (m_hat / (jnp.sqrt(v_hat) + ADAM_EPS) + ADAM_WD * w), m_new, v_new


def _small_step(name, grads, loss, w, m, v):
    n, n_dev = len(grads), 8
    offs = [sum(g.shape[0] for g in grads[:t]) for t in range(n + 1)]
    rows = -(-(offs[n] + 1) // 8) * 8
    width = max(g.shape[1] for g in grads)

    def body(*refs):
        g_refs, loss_ref = refs[:n], refs[n]
        w_refs, m_refs, v_refs = (refs[1 + k * n:1 + (k + 1) * n] for k in (1, 2, 3))
        outs = refs[4 * n + 1:8 * n + 2]
        mine, slots, send_sem, recv_sem = refs[8 * n + 2:]
        x, y, c, _ = _position()
        me = 4 * x + 2 * y + c

        def peer(k):
            return (1 - x if k & 4 else x, 1 - y if k & 2 else y, 1 - c if k & 1 else c)

        def logical(k):
            px, py, pc = peer(k)
            return 4 * px + 2 * py + pc

        mine[...] = jnp.zeros(mine.shape, F32)
        for t in range(n):
            mine[offs[t]:offs[t + 1], 0:grads[t].shape[1]] = g_refs[t][...]
        mine[offs[n]:offs[n] + 1, 0:LANES] = loss_ref[...]
        slots[me] = mine[...]
        sends = [pltpu.make_async_remote_copy(
            src_ref=mine, dst_ref=slots.at[me], send_sem=send_sem.at[k], recv_sem=recv_sem.at[k],
            device_id=peer(k), device_id_type=MESH) for k in range(1, n_dev)]
        for cp in sends:
            cp.start()
        for k in range(1, n_dev):
            pltpu.make_async_remote_copy(
                src_ref=mine, dst_ref=slots.at[logical(k)], send_sem=send_sem.at[k], recv_sem=recv_sem.at[k],
                device_id=peer(k), device_id_type=MESH).wait_recv()
        for cp in sends:
            cp.wait_send()
        total = slots[0]
        for d in range(1, n_dev):
            total = total + slots[d]
        for t in range(n):
            gv = total[offs[t]:offs[t + 1], 0:grads[t].shape[1]]
            outs[t][...] = gv
            outs[n + t][...], outs[2 * n + t][...], outs[3 * n + t][...] = _adamw_update(
                w_refs[t][...], gv, m_refs[t][...], v_refs[t][...])
        outs[4 * n][...] = total[offs[n]:offs[n] + 1, 0:LANES]

    vm = pl.BlockSpec(memory_space=pltpu.VMEM)
    shapes = [jax.ShapeDtypeStruct(g.shape, F32) for g in grads]
    res = pl.pallas_call(
        body, name=name, in_specs=[vm] * (4 * n + 1), out_specs=[vm] * (4 * n + 1),
        out_shape=shapes * 4 + [jax.ShapeDtypeStruct(loss.shape, F32)],
        scratch_shapes=[pltpu.VMEM((rows, width), F32), pltpu.VMEM((n_dev, rows, width), F32),
                        pltpu.SemaphoreType.DMA((n_dev,)), pltpu.SemaphoreType.DMA((n_dev,))],
    )(*grads, loss, *w, *m, *v)
    return [res[k * n:(k + 1) * n] for k in range(4)], res[4 * n]


def _adamw(name, w, g, m, v):
    rows, cols = w.shape
    tr = _row_tile(rows, sublanes=8)

    def body(w_ref, g_ref, m_ref, v_ref, d_ref, mo_ref, vo_ref):
        d_ref[...], mo_ref[...], vo_ref[...] = _adamw_update(w_ref[...], g_ref[...], m_ref[...], v_ref[...])

    spec = pl.BlockSpec((tr, cols), lambda i: (i, 0))
    shape = jax.ShapeDtypeStruct((rows, cols), F32)
    return pl.pallas_call(body, name=name, grid=(rows // tr,), in_specs=[spec] * 4, out_specs=[spec] * 3,
                          out_shape=[shape] * 3, compiler_params=_params("parallel"))(w, g, m, v)


def kernel(x, ev_w_in, ev_g_cq, ev_w_uq, ev_g_ckv, ev_w_ukv, ev_w_out, od_w_qkv, od_rel_bias, od_w_out, g_mix, g_ffn, w_gate, w_up, w_down, g_final, loss_target, m_ev_w_in, m_ev_g_cq, m_ev_w_uq, m_ev_g_ckv, m_ev_w_ukv, m_ev_w_out, m_od_w_qkv, m_od_rel_bias, m_od_w_out, m_g_mix, m_g_ffn, m_w_gate, m_w_up, m_w_down, m_g_final, v_ev_w_in, v_ev_g_cq, v_ev_w_uq, v_ev_g_ckv, v_ev_w_ukv, v_ev_w_out, v_od_w_qkv, v_od_rel_bias, v_od_w_out, v_g_mix, v_g_ffn, v_w_gate, v_w_up, v_w_down, v_g_final):
    w = dict(ev_w_in=ev_w_in, ev_g_cq=ev_g_cq, ev_w_uq=ev_w_uq, ev_g_ckv=ev_g_ckv, ev_w_ukv=ev_w_ukv, ev_w_out=ev_w_out,
             od_w_qkv=od_w_qkv, od_rel_bias=od_rel_bias, od_w_out=od_w_out, g_mix=g_mix, g_ffn=g_ffn, w_gate=w_gate,
             w_up=w_up, w_down=w_down, g_final=g_final)
    m = dict(ev_w_in=m_ev_w_in, ev_g_cq=m_ev_g_cq, ev_w_uq=m_ev_w_uq, ev_g_ckv=m_ev_g_ckv, ev_w_ukv=m_ev_w_ukv,
             ev_w_out=m_ev_w_out, od_w_qkv=m_od_w_qkv, od_rel_bias=m_od_rel_bias, od_w_out=m_od_w_out, g_mix=m_g_mix,
             g_ffn=m_g_ffn, w_gate=m_w_gate, w_up=m_w_up, w_down=m_w_down, g_final=m_g_final)
    v = dict(ev_w_in=v_ev_w_in, ev_g_cq=v_ev_g_cq, ev_w_uq=v_ev_w_uq, ev_g_ckv=v_ev_g_ckv, ev_w_ukv=v_ev_w_ukv,
             ev_w_out=v_ev_w_out, od_w_qkv=v_od_w_qkv, od_rel_bias=v_od_rel_bias, od_w_out=v_od_w_out, g_mix=v_g_mix,
             g_ffn=v_g_ffn, w_gate=v_w_gate, w_up=v_w_up, w_down=v_w_down, g_final=v_g_final)
    flat2d = lambda a: a.reshape(-1, a.shape[-1])
    for tree in (w, m, v):
        for n in TRANSPOSED:
            tree[n] = jnp.swapaxes(tree[n], 1, 2)

    pos = jnp.stack([2 * lax.axis_index("x") + lax.axis_index("y"), lax.axis_index("c")]).astype(jnp.int32)

    slots = {part: _cast_into_slot("cast_" + part, w[n], layer, pos) for part, n, layer in GRAD_PARTS
             if part in FIRST_WEIGHTS + NEXT_WEIGHTS}
    rest = [(part, n, layer) for part, n, layer in GRAD_PARTS if part not in slots]

    def cast_rest(carry):
        return dict(zip([part for part, _, _ in rest],
                        _cast_many_into_slots("cast_rest", [(w[n], layer) for _, n, layer in rest], pos, carry)))

    ex = _Exchanges(slots, pos, {n: w[n].shape for n in BIG}, cast_rest)

    loss_local, grad_x, small = _local_step(x[0], loss_target[0], {n: w[n] for n in SMALL}, ex)

    grads = ex.finish()
    delta, new_m, new_v = {}, {}, {}
    small_out, loss = _small_step("small_step", [flat2d(small[n]) for n in SMALL], loss_local,
                                  *([flat2d(t[n]) for n in SMALL] for t in (w, m, v)))
    for tree, outs in zip((grads, delta, new_m, new_v), small_out):
        tree.update({n: o.reshape(w[n].shape) for n, o in zip(SMALL, outs)})

    for n in BIG:
        turn = (lambda a: jnp.swapaxes(a, 1, 2)) if n in ADAMW_TRANSPOSED else (lambda a: a)
        shape = turn(w[n]).shape
        outs = _adamw("adamw_" + n, *(flat2d(turn(a)) for a in (w[n], grads[n], m[n], v[n])))
        delta[n], new_m[n], new_v[n] = (turn(o.reshape(shape)) for o in outs)
    for tree in (grads, delta, new_m, new_v):
        for n in TRANSPOSED:
            tree[n] = jnp.swapaxes(tree[n], 1, 2)

    return (loss[0, 0], grad_x[None], *[grads[n] for n in WEIGHTS], *[delta[n] for n in WEIGHTS],
            *[new_m[n] for n in WEIGHTS], *[new_v[n] for n in WEIGHTS])
```

```python
import functools

import jax
import jax.numpy as jnp
import numpy as np
from jax import lax
from jax.experimental import pallas as pl
from jax.experimental.pallas import tpu as pltpu

F32 = jnp.float32
BF16 = jnp.bfloat16

S = 2048
D = 1024
CHUNK = 64
MLA_H, MLA_NOPE, MLA_ROPE, MLA_V = 8, 64, 32, 64
Q_LORA, KV_LORA = 384, 256
ROPE_THETA = 10000.0
SB_H, SB_DIM = 8, 64
C_H, C_DIM = 16, 64
LEFT_CHUNKS = 8
REL_CLIP = 256
D_FF = 2816
EVEN_IN = 2208
RMS_EPS = 1e-6
ADAM_LR, ADAM_B1, ADAM_B2, ADAM_EPS, ADAM_WD, ADAM_STEP = 0.001, 0.9, 0.999, 1e-08, 0.01, 10

N_CHIPS = 4
FF_SHARD = D_FF // N_CHIPS
SCALE_A = (MLA_NOPE + MLA_ROPE) ** -0.5
SCALE_B = SB_DIM ** -0.5
SCALE_C = C_DIM ** -0.5
NEG = -1e30
LOG2_E = 1.4426950408889634

LANES = 128
MXU_W = 256
VMEM_LIMIT_BYTES = 56 * 1024 * 1024
TM = 512
TQ = 1024
TF = 1024
QB = 512
BQ = 256

P_CQ, P_CKV, P_QB, P_KB, P_VB, P_KR = 0, 512, 768, 1280, 1792, 2304
P_IN = 2432
KR_LANE = 64
BAND_W = BQ + LEFT_CHUNKS * CHUNK
BAND_PAD = 512
TOEP_W = 1024


def _params(*sem):
    return pltpu.CompilerParams(dimension_semantics=sem, vmem_limit_bytes=VMEM_LIMIT_BYTES)


MESH = pl.DeviceIdType.MESH
ANY = pl.BlockSpec(memory_space=pl.ANY)


def _position():
    x, y, c = lax.axis_index("x"), lax.axis_index("y"), lax.axis_index("c")
    other_chips = [(1 - x, y), (x, 1 - y), (1 - x, 1 - y)]
    return x, y, c, other_chips


def _half_rows(c, half):
    return pl.ds(pl.multiple_of(c * half, 16), half)


def _remote(ref_src, ref_dst, send, recv, k, device):
    return pltpu.make_async_remote_copy(src_ref=ref_src, dst_ref=ref_dst, send_sem=send.at[k], recv_sem=recv.at[k],
                                        device_id=device, device_id_type=MESH)


class _Carry:
    def __init__(self):
        self.operands, self.aliased, self.fresh = [], [], []
        self.n_sems = 0
        self.starts, self.finishes, self.on_done = [], [], []

    def operand(self, arr, aliased):
        for i, a in enumerate(self.operands):
            if a is arr:
                return i
        self.operands.append(arr)
        self.aliased.append(aliased)
        return len(self.operands) - 1

    def result(self, shape, dtype):
        self.fresh.append(jax.ShapeDtypeStruct(shape, dtype))
        return len(self.fresh) - 1

    def sems(self, k):
        base = self.n_sems
        self.n_sems += k
        return base

    def done(self, results):
        aliased, fresh = results
        for f in self.on_done:
            f(aliased, fresh)


def _carrier_call(body, *, name, grid, in_specs, out_specs, out_shape, args, sem, scratch_shapes=(), carry=None,
                  prefetch=()):
    in_specs, out_specs, out_shape, scratch = list(in_specs), list(out_specs), list(out_shape), list(scratch_shapes)
    n_pre = len(prefetch)

    def call(kernel, in_specs, out_specs, out_shape, scratch, aliases, sem):
        return pl.pallas_call(
            kernel, name=name, out_shape=out_shape, input_output_aliases=aliases, compiler_params=_params(*sem),
            grid_spec=pltpu.PrefetchScalarGridSpec(num_scalar_prefetch=n_pre, grid=grid, in_specs=in_specs,
                                                   out_specs=out_specs, scratch_shapes=scratch))

    if carry is None:
        return list(call(body, in_specs, out_specs, out_shape, scratch, {}, sem)(*prefetch, *args)), None
    ops = carry.operands
    alias_idx = [i for i, a in enumerate(carry.aliased) if a]
    c_shapes = [jax.ShapeDtypeStruct(ops[i].shape, ops[i].dtype) for i in alias_idx] + carry.fresh
    n_in, n_out, n_scr = len(args), len(out_shape), len(scratch)

    def wrapped(*refs):
        pre, refs = refs[:n_pre], refs[n_pre:]
        ins, c_ins = refs[:n_in], refs[n_in:n_in + len(ops)]
        o0 = n_in + len(ops)
        outs, c_outs = refs[o0:o0 + n_out], refs[o0 + n_out:o0 + n_out + len(c_shapes)]
        s0 = o0 + n_out + len(c_shapes)
        scr, send, recv = refs[s0:s0 + n_scr], refs[s0 + n_scr], refs[s0 + n_scr + 1]
        use = list(c_ins)
        for k, i in enumerate(alias_idx):
            use[i] = c_outs[k]
        fresh = c_outs[len(alias_idx):]

        def run(steps):
            for step in steps:
                step(use, fresh, send, recv)

        if not grid:
            run(carry.starts)
            if body is not None:
                body(*pre, *ins, *outs, *scr)
            run(carry.finishes)
            return
        ids = [pl.program_id(a) for a in range(len(grid))]
        first = functools.reduce(jnp.logical_and, [i == 0 for i in ids])
        last = functools.reduce(jnp.logical_and, [i == g - 1 for i, g in zip(ids, grid)])

        @pl.when(first)
        def _():
            run(carry.starts)

        body(*pre, *ins, *outs, *scr)

        @pl.when(last)
        def _():
            run(carry.finishes)

    res = call(wrapped, in_specs + [ANY] * len(ops), out_specs + [ANY] * len(c_shapes), out_shape + c_shapes,
               scratch + [pltpu.SemaphoreType.DMA((carry.n_sems,)), pltpu.SemaphoreType.DMA((carry.n_sems,))],
               {n_pre + n_in + i: n_out + k for k, i in enumerate(alias_idx)},
               ("arbitrary",) * len(grid))(*prefetch, *args, *ops)
    res = list(res)
    c_res = res[n_out:]
    return res[:n_out], ({i: c_res[k] for k, i in enumerate(alias_idx)}, c_res[len(alias_idx):])


_DIMS = {"nn": (((1,), (0,)), ((), ())), "nt": (((1,), (1,)), ((), ())), "tn": (((0,), (0,)), ((), ()))}


def _dot(a, b, kind="nn"):
    return lax.dot_general(a, b, _DIMS[kind], preferred_element_type=F32)


def _iota(shape, dim):
    return lax.broadcasted_iota(jnp.int32, shape, dim)


def _sigmoid(x):
    return 1.0 / (1.0 + jnp.exp(-x))


def _split_dot(x, tri):
    hi = x.astype(BF16)
    lo = (x - hi.astype(F32)).astype(BF16)
    both = _dot(jnp.concatenate([hi, lo], axis=0), tri)
    return both[:x.shape[0]] + both[x.shape[0]:]


def _running_sum(x, tri, reverse):
    n = x.shape[1] // MXU_W
    blocks = [x[:, b * MXU_W:(b + 1) * MXU_W] for b in range(n)]
    out = [None] * n
    carry = None
    for b in (range(n - 1, -1, -1) if reverse else range(n)):
        part = _split_dot(blocks[b], tri)
        out[b] = part if carry is None else part + carry
        total = jnp.sum(blocks[b], axis=-1, keepdims=True)
        carry = total if carry is None else carry + total
    return (jnp.concatenate(out, axis=1) if n > 1 else out[0]), carry


def _mm(name, a, b, *, kind, grid, a_spec, b_spec, o_spec, out_shape, out_dtype, acc_shape, resid=None, r_spec=None,
        carry=None):
    nk = grid[-1]
    has_r = resid is not None
    several = lambda x: list(x) if isinstance(x, (tuple, list)) else [x]
    a_specs, b_specs = several(a_spec), several(b_spec)
    na, nb = len(a_specs), len(b_specs)
    a_args = several(a) if isinstance(a, (tuple, list)) else [a] * na
    b_args = several(b) if isinstance(b, (tuple, list)) else [b] * nb

    def body(*refs):
        r_ref = refs[na + nb] if has_r else None
        o_ref = refs[na + nb + has_r]
        side_by_side = lambda rs: rs[0][...] if len(rs) == 1 else jnp.concatenate([r[...].astype(BF16) for r in rs], axis=1)
        part = _dot(side_by_side(refs[:na]).astype(BF16), side_by_side(refs[na:na + nb]).astype(BF16), kind)

        def finish(total):
            if has_r:
                total = total + r_ref[...].astype(F32)
            o_ref[...] = total.astype(out_dtype)

        if nk == 1:
            finish(part)
        else:
            acc_ref = refs[na + nb + has_r + 1]
            k = pl.program_id(len(grid) - 1)

            @pl.when(k == 0)
            def _():
                acc_ref[...] = part

            @pl.when(k > 0)
            def _():
                acc_ref[...] += part

            @pl.when(k == nk - 1)
            def _():
                finish(acc_ref[...])

    in_specs = a_specs + b_specs + ([r_spec] if has_r else [])
    args = (*a_args, *b_args) + ((resid,) if has_r else ())
    sem = ("parallel",) * (len(grid) - 1) + ("arbitrary",)
    res, copies = _carrier_call(
        body, name=name, grid=grid, in_specs=in_specs, out_specs=[o_spec],
        out_shape=[jax.ShapeDtypeStruct(out_shape, out_dtype)],
        scratch_shapes=[pltpu.VMEM(acc_shape, F32)] if nk > 1 else [], args=args, sem=sem, carry=carry)
    if carry is not None:
        carry.done(copies)
    return res[0]


def _rms_fwd(name, x, g, col_block=0):
    c = g.shape[1]

    def body(x_ref, g_ref, u_ref):
        xv = x_ref[...]
        r = lax.rsqrt(jnp.mean(xv * xv, axis=-1, keepdims=True) + RMS_EPS)
        u_ref[...] = (xv * r * g_ref[...]).astype(BF16)

    return pl.pallas_call(
        body, name=name, grid=(S // TM,),
        in_specs=[pl.BlockSpec((TM, c), lambda i: (i, col_block)), pl.BlockSpec((1, c), lambda i: (0, 0))],
        out_specs=pl.BlockSpec((TM, c), lambda i: (i, 0)),
        out_shape=jax.ShapeDtypeStruct((S, c), BF16),
        compiler_params=_params("parallel"),
    )(x, g)


def _rms_bwd(name, dy, x, g, resid, carry=None):
    def body(dy_ref, x_ref, g_ref, r_ref, dx_ref, dg_ref):
        i = pl.program_id(0)
        xv = x_ref[...]
        r = lax.rsqrt(jnp.mean(xv * xv, axis=-1, keepdims=True) + RMS_EPS)
        xh = xv * r
        dyv = dy_ref[...]
        dxh = dyv * g_ref[...]
        dx_ref[...] = r_ref[...] + r * (dxh - xh * jnp.mean(dxh * xh, axis=-1, keepdims=True))
        part = jnp.sum(dyv * xh, axis=0, keepdims=True)

        @pl.when(i == 0)
        def _():
            dg_ref[...] = part

        @pl.when(i > 0)
        def _():
            dg_ref[...] += part

    row = pl.BlockSpec((TM, D), lambda i: (i, 0))
    vec = pl.BlockSpec((1, D), lambda i: (0, 0))
    res, copies = _carrier_call(
        body, name=name, grid=(S // TM,), in_specs=[row, row, vec, row], out_specs=[row, vec],
        out_shape=[jax.ShapeDtypeStruct((S, D), F32), jax.ShapeDtypeStruct((1, D), F32)],
        args=(dy, x, g, resid), sem=("arbitrary",), carry=carry)
    if carry is not None:
        carry.done(copies)
    return res


def _loss_bwd(name, h, g, tgt):
    def body(h_ref, g_ref, t_ref, loss_ref, dh_ref, dg_ref):
        i = pl.program_id(0)
        xv = h_ref[...]
        gv = g_ref[...]
        r = lax.rsqrt(jnp.mean(xv * xv, axis=-1, keepdims=True) + RMS_EPS)
        xh = xv * r
        diff = xh * gv - t_ref[...]
        part_loss = 0.5 * jnp.sum(jnp.sum(diff * diff, axis=-1, keepdims=True) * (1.0 / D), axis=0, keepdims=True)
        dy = diff * (1.0 / D)
        dxh = dy * gv
        dh_ref[...] = r * (dxh - xh * jnp.mean(dxh * xh, axis=-1, keepdims=True))
        part_g = jnp.sum(dy * xh, axis=0, keepdims=True)

        @pl.when(i == 0)
        def _():
            dg_ref[...] = part_g
            loss_ref[...] = jnp.broadcast_to(part_loss, (1, LANES))

        @pl.when(i > 0)
        def _():
            dg_ref[...] += part_g
            loss_ref[...] += jnp.broadcast_to(part_loss, (1, LANES))

    row = pl.BlockSpec((TM, D), lambda i: (i, 0))
    vec = pl.BlockSpec((1, D), lambda i: (0, 0))
    return pl.pallas_call(
        body, name=name, grid=(S // TM,), in_specs=[row, vec, row],
        out_specs=[pl.BlockSpec((1, LANES), lambda i: (0, 0)), row, vec],
        out_shape=[jax.ShapeDtypeStruct((1, LANES), F32), jax.ShapeDtypeStruct((S, D), F32),
                   jax.ShapeDtypeStruct((1, D), F32)],
        compiler_params=_params("arbitrary"),
    )(h, g, tgt)


def _ffn_fwd(name, h, g, wg, wu, wd, carry=None):
    def body(h_ref, g_ref, wg_ref, wu_ref, wd_ref, o_ref, gate_ref, up_ref, u_scr):
        s = pl.program_id(1)

        @pl.when(s == 0)
        def _():
            xv = h_ref[...]
            r = lax.rsqrt(jnp.mean(xv * xv, axis=-1, keepdims=True) + RMS_EPS)
            u_scr[...] = (xv * r * g_ref[...]).astype(BF16)
            o_ref[...] = xv

        u = u_scr[...]
        gate = _dot(u, wg_ref[...], "nt")
        up = _dot(u, wu_ref[...], "nt")
        act = gate * _sigmoid(gate) * up
        o_ref[...] += _dot(act.astype(BF16), wd_ref[...])
        gate_ref[...] = gate.astype(BF16)
        up_ref[...] = up.astype(BF16)

    row = pl.BlockSpec((TF, D), lambda i, s: (i, 0))
    hid = pl.BlockSpec((None, TF, FF_SHARD), lambda i, s: (s, i, 0))
    return _carrier_call(
        body, name=name, grid=(S // TF, N_CHIPS),
        in_specs=[row, pl.BlockSpec((1, D), lambda i, s: (0, 0))]
        + [pl.BlockSpec((None, FF_SHARD, D), lambda i, s: (s, 0, 0))] * 3,
        out_specs=[row, hid, hid],
        out_shape=[jax.ShapeDtypeStruct((S, D), F32), jax.ShapeDtypeStruct((N_CHIPS, S, FF_SHARD), BF16),
                   jax.ShapeDtypeStruct((N_CHIPS, S, FF_SHARD), BF16)],
        scratch_shapes=[pltpu.VMEM((TF, D), BF16)], args=(h, g, wg, wu, wd), sem=("parallel", "arbitrary"), carry=carry)


def _ffn_bwd(name, dh, h, g, gate, up, wg, wu, wd):
    def body(dh_ref, h_ref, g_ref, gate_ref, up_ref, wg_ref, wu_ref, wd_ref,
             dhin_ref, dg_ref, u_ref, dgate_ref, dup_ref, act_ref, dhb_scr, du_scr):
        i = pl.program_id(0)
        s = pl.program_id(1)

        @pl.when(s == 0)
        def _():
            xv = h_ref[...]
            r = lax.rsqrt(jnp.mean(xv * xv, axis=-1, keepdims=True) + RMS_EPS)
            u_ref[...] = (xv * r * g_ref[...]).astype(BF16)
            dhb_scr[...] = dh_ref[...].astype(BF16)
            du_scr[...] = jnp.zeros_like(du_scr)

        dact = _dot(dhb_scr[...], wd_ref[...], "nt")
        gv = gate_ref[...].astype(F32)
        uv = up_ref[...].astype(F32)
        sig = _sigmoid(gv)
        sil = gv * sig
        dup = dact * sil
        dgate = dact * uv * (sig * (1.0 + gv * (1.0 - sig)))
        dgb = dgate.astype(BF16)
        dub = dup.astype(BF16)
        act_ref[...] = (sil * uv).astype(BF16)
        dgate_ref[...] = dgb
        dup_ref[...] = dub
        du_scr[...] += _dot(dgb, wg_ref[...]) + _dot(dub, wu_ref[...])

        @pl.when(s == N_CHIPS - 1)
        def _():
            xv = h_ref[...]
            r = lax.rsqrt(jnp.mean(xv * xv, axis=-1, keepdims=True) + RMS_EPS)
            xh = xv * r
            du = du_scr[...]
            dxh = du * g_ref[...]
            dhin_ref[...] = dh_ref[...] + r * (dxh - xh * jnp.mean(dxh * xh, axis=-1, keepdims=True))
            part = jnp.sum(du * xh, axis=0, keepdims=True)

            @pl.when(i == 0)
            def _():
                dg_ref[...] = part

            @pl.when(i > 0)
            def _():
                dg_ref[...] += part

    row = pl.BlockSpec((TM, D), lambda i, s: (i, 0))
    vec = pl.BlockSpec((1, D), lambda i, s: (0, 0))
    hid = pl.BlockSpec((None, TM, FF_SHARD), lambda i, s: (s, i, 0))
    hid_shape = jax.ShapeDtypeStruct((N_CHIPS, S, FF_SHARD), BF16)
    return pl.pallas_call(
        body, name=name, grid=(S // TM, N_CHIPS),
        in_specs=[row, row, vec, hid, hid] + [pl.BlockSpec((None, FF_SHARD, D), lambda i, s: (s, 0, 0))] * 3,
        out_specs=[row, vec, row, hid, hid, hid],
        out_shape=[jax.ShapeDtypeStruct((S, D), F32), jax.ShapeDtypeStruct((1, D), F32),
                   jax.ShapeDtypeStruct((S, D), BF16), hid_shape, hid_shape, hid_shape],
        scratch_shapes=[pltpu.VMEM((TM, D), BF16), pltpu.VMEM((TM, D), F32)],
        compiler_params=_params("arbitrary", "arbitrary"),
    )(dh, h, g, gate, up, wg, wu, wd)


def _ffn_wgrads(name, u, dgate, dup, act, dh):
    nk = S // TQ

    def body(u_ref, dh_ref, dgate_ref, dup_ref, act_ref, dg_ref, du_ref, dd_ref, acc_g, acc_u, acc_d):
        k = pl.program_id(1)
        u = u_ref[...]
        parts = (_dot(dgate_ref[...], u, "tn"), _dot(dup_ref[...], u, "tn"),
                 _dot(act_ref[...], dh_ref[...].astype(BF16), "tn"))
        accs = (acc_g, acc_u, acc_d)

        @pl.when(k == 0)
        def _():
            for acc, part in zip(accs, parts):
                acc[...] = part

        @pl.when(k > 0)
        def _():
            for acc, part in zip(accs, parts):
                acc[...] += part

        @pl.when(k == nk - 1)
        def _():
            for out, acc in zip((dg_ref, du_ref, dd_ref), accs):
                out[...] = acc[...].astype(BF16)

    tok = pl.BlockSpec((TQ, D), lambda s, k: (k, 0))
    hid = pl.BlockSpec((None, TQ, FF_SHARD), lambda s, k: (s, k, 0))
    out = pl.BlockSpec((None, FF_SHARD, D), lambda s, k: (s, 0, 0))
    shape = jax.ShapeDtypeStruct((N_CHIPS, FF_SHARD, D), BF16)
    return pl.pallas_call(
        body, name=name, grid=(N_CHIPS, nk), in_specs=[tok, tok, hid, hid, hid], out_specs=[out, out, out],
        out_shape=[shape, shape, shape], scratch_shapes=[pltpu.VMEM((FF_SHARD, D), F32)] * 3,
        compiler_params=_params("parallel", "arbitrary"))(u, dh, dgate, dup, act)


def _rope_tables():
    pos = jnp.arange(S, dtype=F32)
    inv = ROPE_THETA ** (-jnp.arange(0, MLA_ROPE, 2, dtype=F32) / MLA_ROPE)
    ang = pos[:, None] * inv[None, :]
    half = MLA_ROPE // 2
    cos = jnp.cos(ang)
    sin = jnp.sin(ang)
    one = jnp.ones((S, KR_LANE), F32)
    zero = jnp.zeros((S, KR_LANE), F32)
    tail_one = jnp.ones((S, LANES - KR_LANE - MLA_ROPE), F32)
    tail_zero = jnp.zeros((S, LANES - KR_LANE - MLA_ROPE), F32)
    cos_t = jnp.concatenate([one, cos, cos, tail_one], axis=1)
    sin_t = jnp.concatenate([zero, -sin, sin, tail_zero], axis=1)
    assert cos_t.shape == (S, LANES) and half * 2 == MLA_ROPE
    return cos_t, sin_t


def _rope(x, cos_t, sin_t, sign):
    n = x.shape[1] // LANES
    half = MLA_ROPE // 2
    lane = _iota(x.shape, 1) & (LANES - 1)
    first = (lane >= KR_LANE) & (lane < KR_LANE + half)
    swapped = jnp.where(first, pltpu.roll(x, x.shape[1] - half, 1), pltpu.roll(x, half, 1))
    c = jnp.tile(cos_t, (1, n)) if n > 1 else cos_t
    s = jnp.tile(sin_t, (1, n)) if n > 1 else sin_t
    return x * c + swapped * (s * sign)


def _mla_prep_fwd(name, proj, g_cq, g_ckv, w_uq, w_uk, w_uv, cos_t, sin_t):
    nh = MLA_H * LANES

    def body(cq_ref, ckv_ref, kr_ref, gq_ref, gkv_ref, wq_ref, wk_ref, wv_ref, cos_ref, sin_ref,
             qa_ref, ka_ref, va_ref):
        cos_v, sin_v = cos_ref[...], sin_ref[...]
        cq = cq_ref[...]
        r = lax.rsqrt(jnp.mean(cq * cq, axis=-1, keepdims=True) + RMS_EPS)
        cqn = (cq * r * gq_ref[...]).astype(BF16)
        qa_ref[...] = _rope(_dot(cqn, wq_ref[...]), cos_v, sin_v, 1.0).astype(BF16)
        ckv = ckv_ref[...]
        r = lax.rsqrt(jnp.mean(ckv * ckv, axis=-1, keepdims=True) + RMS_EPS)
        ckvn = (ckv * r * gkv_ref[...]).astype(BF16)
        lane = _iota((TM, LANES), 1)
        rot = (lane >= KR_LANE) & (lane < KR_LANE + MLA_ROPE)
        kr = jnp.where(rot, _rope(kr_ref[...], cos_v, sin_v, 1.0), 0.0)
        ka_ref[...] = (_dot(ckvn, wk_ref[...]) + jnp.tile(kr, (1, MLA_H))).astype(BF16)
        va_ref[...] = _dot(ckvn, wv_ref[...]).astype(BF16)

    full = lambda shape: pl.BlockSpec(shape, lambda i: (0, 0))
    return pl.pallas_call(
        body, name=name, grid=(S // TM,),
        in_specs=[pl.BlockSpec((TM, Q_LORA), lambda i: (i, P_CQ // Q_LORA)),
                  pl.BlockSpec((TM, KV_LORA), lambda i: (i, P_CKV // KV_LORA)),
                  pl.BlockSpec((TM, LANES), lambda i: (i, P_KR // LANES)),
                  full((1, Q_LORA)), full((1, KV_LORA)), full((Q_LORA, nh)), full((KV_LORA, nh)),
                  full((KV_LORA, MLA_H * MLA_V)),
                  pl.BlockSpec((TM, LANES), lambda i: (i, 0)), pl.BlockSpec((TM, LANES), lambda i: (i, 0))],
        out_specs=[pl.BlockSpec((TM, nh), lambda i: (i, 0)), pl.BlockSpec((TM, nh), lambda i: (i, 0)),
                   pl.BlockSpec((TM, MLA_H * MLA_V), lambda i: (i, 0))],
        out_shape=[jax.ShapeDtypeStruct((S, nh), BF16), jax.ShapeDtypeStruct((S, nh), BF16),
                   jax.ShapeDtypeStruct((S, MLA_H * MLA_V), BF16)],
        compiler_params=_params("parallel"),
    )(proj, proj, proj, g_cq, g_ckv, w_uq, w_uk, w_uv, cos_t, sin_t)


def _mla_prep_bwd(name, dqa, dka, dva, proj, g_cq, g_ckv, w_uq, w_uk, w_uv, cos_t, sin_t):
    nh = MLA_H * LANES

    def body(dqa_ref, dka_ref, dva_ref, cq_ref, ckv_ref, gq_ref, gkv_ref, wq_ref, wk_ref, wv_ref, cos_ref, sin_ref,
             dcq_ref, dckv_ref, dkr_ref, dwq_ref, dwk_ref, dwv_ref, dgq_ref, dgkv_ref):
        i = pl.program_id(0)
        cos_v, sin_v = cos_ref[...], sin_ref[...]

        def norm_bwd(x, g, dn):
            r = lax.rsqrt(jnp.mean(x * x, axis=-1, keepdims=True) + RMS_EPS)
            xh = x * r
            dxh = dn * g
            dx = r * (dxh - xh * jnp.mean(dxh * xh, axis=-1, keepdims=True))
            return dx, jnp.sum(dn * xh, axis=0, keepdims=True), (xh * g).astype(BF16)

        dq = _rope(dqa_ref[...], cos_v, sin_v, -1.0).astype(BF16)
        dcqn = _dot(dq, wq_ref[...], "nt")
        dcq, dgq, cqn = norm_bwd(cq_ref[...], gq_ref[...], dcqn)
        dcq_ref[...] = dcq.astype(BF16)
        dwq = _dot(cqn, dq, "tn")

        dka = dka_ref[...]
        dkab = dka.astype(BF16)
        dvab = dva_ref[...].astype(BF16)
        dckvn = _dot(dkab, wk_ref[...], "nt") + _dot(dvab, wv_ref[...], "nt")
        dckv, dgkv, ckvn = norm_bwd(ckv_ref[...], gkv_ref[...], dckvn)
        dckv_ref[...] = dckv.astype(BF16)
        dwk = _dot(ckvn, dkab, "tn")
        dwv = _dot(ckvn, dvab, "tn")

        fold = dka[:, 0:LANES]
        for hh in range(1, MLA_H):
            fold = fold + dka[:, hh * LANES:(hh + 1) * LANES]
        lane = _iota((TM, LANES), 1)
        rot = (lane >= KR_LANE) & (lane < KR_LANE + MLA_ROPE)
        dkr = _rope(jnp.where(rot, fold, 0.0), cos_v, sin_v, -1.0)
        dkr_ref[...] = jnp.where(rot, dkr, 0.0).astype(BF16)

        @pl.when(i == 0)
        def _():
            dwq_ref[...] = dwq
            dwk_ref[...] = dwk
            dwv_ref[...] = dwv
            dgq_ref[...] = dgq
            dgkv_ref[...] = dgkv

        @pl.when(i > 0)
        def _():
            dwq_ref[...] += dwq
            dwk_ref[...] += dwk
            dwv_ref[...] += dwv
            dgq_ref[...] += dgq
            dgkv_ref[...] += dgkv

    full = lambda shape: pl.BlockSpec(shape, lambda i: (0, 0))
    rows = lambda c: pl.BlockSpec((TM, c), lambda i: (i, 0))
    nv = MLA_H * MLA_V
    return pl.pallas_call(
        body, name=name, grid=(S // TM,),
        in_specs=[rows(nh), rows(nh), rows(nv),
                  pl.BlockSpec((TM, Q_LORA), lambda i: (i, P_CQ // Q_LORA)),
                  pl.BlockSpec((TM, KV_LORA), lambda i: (i, P_CKV // KV_LORA)),
                  full((1, Q_LORA)), full((1, KV_LORA)), full((Q_LORA, nh)), full((KV_LORA, nh)), full((KV_LORA, nv)),
                  rows(LANES), rows(LANES)],
        out_specs=[rows(Q_LORA), rows(KV_LORA), rows(LANES), full((Q_LORA, nh)), full((KV_LORA, nh)),
                   full((KV_LORA, nv)), full((1, Q_LORA)), full((1, KV_LORA))],
        out_shape=[jax.ShapeDtypeStruct((S, Q_LORA), BF16), jax.ShapeDtypeStruct((S, KV_LORA), BF16),
                   jax.ShapeDtypeStruct((S, LANES), BF16), jax.ShapeDtypeStruct((Q_LORA, nh), F32),
                   jax.ShapeDtypeStruct((KV_LORA, nh), F32), jax.ShapeDtypeStruct((KV_LORA, nv), F32),
                   jax.ShapeDtypeStruct((1, Q_LORA), F32), jax.ShapeDtypeStruct((1, KV_LORA), F32)],
        compiler_params=_params("arbitrary"),
    )(dqa, dka, dva, proj, proj, g_cq, g_ckv, w_uq, w_uk, w_uv, cos_t, sin_t)


def _head_masks(dtype):
    lane = _iota((1, LANES), 1)
    return (lane < 64).astype(dtype), (lane >= 64).astype(dtype)


def _mla_fwd(name, qa, ka, va, carry=None):
    def body(q_ref, k_ref, v_ref, o_ref, lse_ref):
        m0b, m1b = _head_masks(BF16)
        lane = _iota((QB, LANES), 1)
        left = lane < 64

        def qblock(i, _):
            r0 = pl.multiple_of(i * QB, QB)
            qs = [q_ref[pl.ds(r0, QB), hh * LANES:(hh + 1) * LANES] for hh in range(2)]
            rowc = lax.shift_right_logical(r0 + _iota((QB, QB), 0), 6)

            def kv(kb, carry):
                ms, ls, acc = carry
                c0 = pl.multiple_of(kb * QB, QB)
                v = v_ref[pl.ds(c0, QB), :]
                ok = lax.shift_right_logical(c0 + _iota((QB, QB), 1), 6) <= rowc
                new_m, new_l, alphas = [], [], []
                pv = None
                for hh in range(2):
                    k = k_ref[pl.ds(c0, QB), hh * LANES:(hh + 1) * LANES]
                    s = jnp.where(ok, _dot(qs[hh], k, "nt") * (SCALE_A * LOG2_E), NEG)
                    mn = jnp.maximum(ms[hh], jnp.max(s, axis=-1, keepdims=True))
                    p = jnp.exp2(s - mn)
                    a = jnp.exp2(ms[hh] - mn)
                    new_m.append(mn)
                    new_l.append(a * ls[hh] + jnp.sum(p, axis=-1, keepdims=True))
                    alphas.append(a)
                    part = _dot(p.astype(BF16), v * (m0b if hh == 0 else m1b))
                    pv = part if pv is None else pv + part
                acc = acc * jnp.where(left, alphas[0], alphas[1]) + pv
                return tuple(new_m), tuple(new_l), acc

            init = ((jnp.full((QB, 1), NEG, F32),) * 2, (jnp.zeros((QB, 1), F32),) * 2, jnp.zeros((QB, LANES), F32))
            ms, ls, acc = lax.fori_loop(0, i + 1, kv, init)
            o_ref[pl.ds(r0, QB), :] = acc * jnp.where(left, 1.0 / ls[0], 1.0 / ls[1])
            lse_ref[pl.ds(r0, QB), :] = jnp.where(left, ms[0] + jnp.log(ls[0]) * LOG2_E, ms[1] + jnp.log(ls[1]) * LOG2_E)
            return 0

        lax.fori_loop(0, S // QB, qblock, 0)

    pair = lambda w: pl.BlockSpec((S, w), lambda p: (0, p))
    return _carrier_call(
        body, name=name, grid=(MLA_H // 2,), in_specs=[pair(2 * LANES), pair(2 * LANES), pair(LANES)],
        out_specs=[pair(LANES), pair(LANES)],
        out_shape=[jax.ShapeDtypeStruct((S, MLA_H * MLA_V), F32), jax.ShapeDtypeStruct((S, MLA_H * MLA_V), F32)],
        args=(qa, ka, va), sem=("parallel",), carry=carry)


def _mla_bwd(name, qa, ka, va, o, lse, do, do_block0, carry=None):
    def body(q_ref, k_ref, v_ref, o_ref, lse_ref, do_ref, dq_ref, dk_ref, dv_ref):
        m0f, m1f = _head_masks(F32)
        m0b, m1b = _head_masks(BF16)
        dk_ref[...] = jnp.zeros_like(dk_ref)
        dv_ref[...] = jnp.zeros_like(dv_ref)

        def qblock(i, _):
            r0 = pl.multiple_of(i * QB, QB)
            rows = pl.ds(r0, QB)
            do_f = do_ref[rows, :]
            prod = do_f * o_ref[rows, :]
            deltas = [jnp.sum(prod * m0f, axis=-1, keepdims=True), jnp.sum(prod * m1f, axis=-1, keepdims=True)]
            lse_v = lse_ref[rows, :]
            lses = [lse_v[:, 0:1], lse_v[:, 64:65]]
            dob = do_f.astype(BF16)
            dos = [dob * m0b, dob * m1b]
            qs = [q_ref[rows, hh * LANES:(hh + 1) * LANES] for hh in range(2)]
            rowc = lax.shift_right_logical(r0 + _iota((QB, QB), 0), 6)

            def kv(kb, dqs):
                c0 = pl.multiple_of(kb * QB, QB)
                cols = pl.ds(c0, QB)
                v = v_ref[cols, :]
                ok = lax.shift_right_logical(c0 + _iota((QB, QB), 1), 6) <= rowc
                out = []
                dv = None
                for hh in range(2):
                    k = k_ref[cols, hh * LANES:(hh + 1) * LANES]
                    s = _dot(qs[hh], k, "nt") * (SCALE_A * LOG2_E)
                    p = jnp.where(ok, jnp.exp2(s - lses[hh]), 0.0)
                    dp = _dot(dos[hh], v, "nt")
                    ds = (p * (dp - deltas[hh]) * SCALE_A).astype(BF16)
                    out.append(dqs[hh] + _dot(ds, k))
                    dk_ref[cols, hh * LANES:(hh + 1) * LANES] += _dot(ds, qs[hh], "tn")
                    part = _dot(p.astype(BF16), dos[hh], "tn")
                    dv = part if dv is None else dv + part
                dv_ref[cols, :] += dv
                return tuple(out)

            dqs = lax.fori_loop(0, i + 1, kv, (jnp.zeros((QB, LANES), F32),) * 2)
            for hh in range(2):
                dq_ref[rows, hh * LANES:(hh + 1) * LANES] = dqs[hh]
            return 0

        lax.fori_loop(0, S // QB, qblock, 0)

    pair = lambda w: pl.BlockSpec((S, w), lambda p: (0, p))
    return _carrier_call(
        body, name=name, grid=(MLA_H // 2,),
        in_specs=[pair(2 * LANES), pair(2 * LANES), pair(LANES), pair(LANES), pair(LANES),
                  pl.BlockSpec((S, LANES), lambda p: (0, do_block0 + p))],
        out_specs=[pair(2 * LANES), pair(2 * LANES), pair(LANES)],
        out_shape=[jax.ShapeDtypeStruct((S, MLA_H * LANES), F32), jax.ShapeDtypeStruct((S, MLA_H * LANES), F32),
                   jax.ShapeDtypeStruct((S, MLA_H * MLA_V), F32)],
        args=(qa, ka, va, o, lse, do), sem=("parallel",), carry=carry)


def _sb_weights(q_h, k, c, before, tri_suffix):
    z = _dot(q_h, k, "nt") * (SCALE_B * LOG2_E)
    sp = jnp.maximum(z, 0.0) + jnp.log(1.0 + jnp.exp2(-jnp.abs(z))) * LOG2_E
    log_keep = jnp.where(before, -sp, 0.0)
    to_the_right, total = _running_sum(log_keep, tri_suffix, True)
    w = jnp.where(before, jnp.exp2(z - sp + to_the_right + c), 0.0)
    return w, jnp.exp2(z - sp), total


def _sb_fwd(name, proj, carry=None):
    def body(q_ref, k_ref, v_ref, o_ref):
        m0b, m1b = _head_masks(BF16)
        tri_suffix = (_iota((MXU_W, MXU_W), 0) > _iota((MXU_W, MXU_W), 1)).astype(BF16)

        def qblock(i, _):
            r0 = pl.multiple_of(i * QB, QB)
            q = q_ref[pl.ds(r0, QB), :].astype(BF16)
            qs = [q * m0b, q * m1b]
            rowg = r0 + _iota((QB, QB), 0)

            def kv(step, carry):
                cs, acc = carry
                c0 = pl.multiple_of((i - step) * QB, QB)
                k = k_ref[pl.ds(c0, QB), :].astype(BF16)
                v = v_ref[pl.ds(c0, QB), :].astype(BF16)
                before = (c0 + _iota((QB, QB), 1)) < rowg
                new_c = []
                for hh in range(2):
                    w, _, tot = _sb_weights(qs[hh], k, cs[hh], before, tri_suffix)
                    new_c.append(cs[hh] + tot)
                    acc = acc + _dot(w.astype(BF16), v * (m0b if hh == 0 else m1b))
                return tuple(new_c), acc

            init = ((jnp.zeros((QB, 1), F32),) * 2, jnp.zeros((QB, LANES), F32))
            _, acc = lax.fori_loop(0, i + 1, kv, init)
            o_ref[pl.ds(r0, QB), :] = acc.astype(BF16)
            return 0

        lax.fori_loop(0, S // QB, qblock, 0)

    col = lambda base: pl.BlockSpec((S, LANES), lambda p: (0, base // LANES + p))
    return _carrier_call(
        body, name=name, grid=(SB_H // 2,), in_specs=[col(P_QB), col(P_KB), col(P_VB)],
        out_specs=[pl.BlockSpec((S, LANES), lambda p: (0, p))],
        out_shape=[jax.ShapeDtypeStruct((S, SB_H * SB_DIM), BF16)],
        args=(proj, proj, proj), sem=("parallel",), carry=carry)


def _sb_bwd(name, proj, do, do_block0, carry=None):
    nb = S // QB

    def body(q_ref, k_ref, v_ref, do_ref, dq_ref, dk_ref, dv_ref, sig_scr, dl_scr, dk_acc, dv_acc):
        m0b, m1b = _head_masks(BF16)
        tri_suffix = (_iota((MXU_W, MXU_W), 0) > _iota((MXU_W, MXU_W), 1)).astype(BF16)
        tri_prefix = (_iota((MXU_W, MXU_W), 0) < _iota((MXU_W, MXU_W), 1)).astype(BF16)
        dk_acc[...] = jnp.zeros_like(dk_acc)
        dv_acc[...] = jnp.zeros_like(dv_acc)

        def qblock(i, _):
            r0 = pl.multiple_of(i * QB, QB)
            rows = pl.ds(r0, QB)
            q = q_ref[rows, :].astype(BF16)
            qs = [q * m0b, q * m1b]
            dob = do_ref[rows, :].astype(BF16)
            dos = [dob * m0b, dob * m1b]
            rowg = r0 + _iota((QB, QB), 0)

            def sweep_left(step, cs):
                kb = i - step
                c0 = pl.multiple_of(kb * QB, QB)
                cols = pl.ds(c0, QB)
                k = k_ref[cols, :].astype(BF16)
                v = v_ref[cols, :].astype(BF16)
                before = (c0 + _iota((QB, QB), 1)) < rowg
                new_c = []
                dv = None
                for hh in range(2):
                    w, sig, tot = _sb_weights(qs[hh], k, cs[hh], before, tri_suffix)
                    new_c.append(cs[hh] + tot)
                    sig_scr[hh, kb] = sig
                    dl_scr[hh, kb] = _dot(dos[hh], v, "nt") * w
                    part = _dot(w.astype(BF16), dos[hh], "tn")
                    dv = part if dv is None else dv + part
                dv_acc[cols, :] += dv
                return tuple(new_c)

            lax.fori_loop(0, i + 1, sweep_left, (jnp.zeros((QB, 1), F32),) * 2)

            def sweep_right(kb, carry):
                ps, dq = carry
                c0 = pl.multiple_of(kb * QB, QB)
                cols = pl.ds(c0, QB)
                k = k_ref[cols, :].astype(BF16)
                before = (c0 + _iota((QB, QB), 1)) < rowg
                new_p = []
                dk = None
                for hh in range(2):
                    dl = dl_scr[hh, kb]
                    sig = sig_scr[hh, kb]
                    to_the_left, total = _running_sum(dl, tri_prefix, False)
                    earlier = to_the_left + ps[hh]
                    new_p.append(ps[hh] + total)
                    dz = (jnp.where(before, dl * (1.0 - sig) - earlier * sig, 0.0) * SCALE_B).astype(BF16)
                    dq = dq + _dot(dz, k * (m0b if hh == 0 else m1b))
                    part = _dot(dz, qs[hh], "tn")
                    dk = part if dk is None else dk + part
                dk_acc[cols, :] += dk
                return tuple(new_p), dq

            init = ((jnp.zeros((QB, 1), F32),) * 2, jnp.zeros((QB, LANES), F32))
            _, dq = lax.fori_loop(0, i + 1, sweep_right, init)
            dq_ref[rows, :] = dq.astype(BF16)
            return 0

        lax.fori_loop(0, nb, qblock, 0)
        dk_ref[...] = dk_acc[...].astype(BF16)
        dv_ref[...] = dv_acc[...].astype(BF16)

    col = lambda base: pl.BlockSpec((S, LANES), lambda p: (0, base // LANES + p))
    out = pl.BlockSpec((S, LANES), lambda p: (0, p))
    shape = jax.ShapeDtypeStruct((S, SB_H * SB_DIM), BF16)
    return _carrier_call(
        body, name=name, grid=(SB_H // 2,),
        in_specs=[col(P_QB), col(P_KB), col(P_VB), pl.BlockSpec((S, LANES), lambda p: (0, do_block0 + p))],
        out_specs=[out, out, out], out_shape=[shape, shape, shape],
        scratch_shapes=[pltpu.VMEM((2, nb, QB, QB), F32), pltpu.VMEM((2, nb, QB, QB), F32),
                        pltpu.VMEM((S, LANES), F32), pltpu.VMEM((S, LANES), F32)],
        args=(proj, proj, proj, do), sem=("parallel",), carry=carry)


def _band_row_index():
    j = np.arange(TOEP_W)
    rel = np.clip(LEFT_CHUNKS * CHUNK - j, -REL_CLIP, REL_CLIP) + REL_CLIP
    rel[BAND_W:] = 2 * REL_CLIP
    return rel.astype(np.int32)


def _band_tiles(r0_ref, q_ref, kpad, vpad, m, m0b, m1b, static_ok, bias):
    r0 = pl.multiple_of(m * BQ, BQ)
    q = q_ref[0, pl.ds(r0, BQ), :]
    kw = kpad[pl.ds(r0, BAND_W), :]
    vw = vpad[pl.ds(r0, BAND_W), :]
    ok = static_ok & ((r0 - BAND_PAD + _iota((BQ, BAND_W), 1)) >= 0)
    qs = [q * m0b, q * m1b]
    ps = []
    for hh in range(2):
        s = jnp.where(ok, _dot(qs[hh], kw, "nt") * (SCALE_C * LOG2_E) + bias[hh], NEG)
        e = jnp.exp2(s - jnp.max(s, axis=-1, keepdims=True))
        ps.append(e * (1.0 / jnp.sum(e, axis=-1, keepdims=True)))
    return r0, qs, kw, vw, ps


def _band_setup(qkv_ref, r0_ref, kpad, vpad):
    kpad[0:BAND_PAD, :] = jnp.zeros((BAND_PAD, LANES), BF16)
    vpad[0:BAND_PAD, :] = jnp.zeros((BAND_PAD, LANES), BF16)
    kpad[BAND_PAD:, :] = qkv_ref[1]
    vpad[BAND_PAD:, :] = qkv_ref[2]
    jc = lax.shift_right_logical(_iota((BQ, BAND_W), 1), 6)
    rc = lax.shift_right_logical(_iota((BQ, BAND_W), 0), 6)
    static_ok = (jc >= rc) & (jc <= rc + LEFT_CHUNKS)
    bias = []
    for hh in range(2):
        row = jnp.broadcast_to(r0_ref[hh:hh + 1, :] * LOG2_E, (BQ, TOEP_W))
        bias.append(pltpu.roll(row, 0, 1, stride=1, stride_axis=0)[:, :BAND_W])
    return static_ok, bias


def _band_fwd(name, qkv, r0, carry=None):
    def body(qkv_ref, r0_ref, o_ref, kpad, vpad):
        m0b, m1b = _head_masks(BF16)
        static_ok, bias = _band_setup(qkv_ref, r0_ref, kpad, vpad)

        def qblock(m, _):
            r0_, _, _, vw, ps = _band_tiles(r0_ref, qkv_ref, kpad, vpad, m, m0b, m1b, static_ok, bias)
            o = _dot(ps[0].astype(BF16), vw * m0b) + _dot(ps[1].astype(BF16), vw * m1b)
            o_ref[pl.ds(r0_, BQ), :] = o.astype(BF16)
            return 0

        lax.fori_loop(0, S // BQ, qblock, 0)

    return _carrier_call(
        body, name=name, grid=(C_H // 2,),
        in_specs=[pl.BlockSpec((3, S, LANES), lambda p: (0, 0, p)), pl.BlockSpec((None, 2, TOEP_W), lambda p: (p, 0, 0))],
        out_specs=[pl.BlockSpec((S, LANES), lambda p: (0, p))],
        out_shape=[jax.ShapeDtypeStruct((S, C_H * C_DIM), BF16)],
        scratch_shapes=[pltpu.VMEM((S + BAND_PAD, LANES), BF16), pltpu.VMEM((S + BAND_PAD, LANES), BF16)],
        args=(qkv, r0), sem=("parallel",), carry=carry)


def _band_bwd(name, qkv, r0, do, carry=None):
    def body(qkv_ref, r0_ref, do_ref, dqkv_ref, dr0_ref, kpad, vpad, dkpad, dvpad, db_acc):
        m0b, m1b = _head_masks(BF16)
        static_ok, bias = _band_setup(qkv_ref, r0_ref, kpad, vpad)
        dkpad[...] = jnp.zeros_like(dkpad)
        dvpad[...] = jnp.zeros_like(dvpad)
        db_acc[...] = jnp.zeros_like(db_acc)

        def qblock(m, _):
            r0_, qs, kw, vw, ps = _band_tiles(r0_ref, qkv_ref, kpad, vpad, m, m0b, m1b, static_ok, bias)
            dob = do_ref[pl.ds(r0_, BQ), :].astype(BF16)
            dos = [dob * m0b, dob * m1b]
            dq = None
            dk = None
            dv = None
            for hh in range(2):
                p = ps[hh]
                dp = _dot(dos[hh], vw, "nt")
                ds = p * (dp - jnp.sum(dp * p, axis=-1, keepdims=True))
                db_acc[hh, :, 0:BAND_W] += ds
                dsb = (ds * SCALE_C).astype(BF16)
                t = _dot(dsb, kw * (m0b if hh == 0 else m1b))
                dq = t if dq is None else dq + t
                t = _dot(dsb, qs[hh], "tn")
                dk = t if dk is None else dk + t
                t = _dot(p.astype(BF16), dos[hh], "tn")
                dv = t if dv is None else dv + t
            dqkv_ref[0, pl.ds(r0_, BQ), :] = dq.astype(BF16)
            dkpad[pl.ds(r0_, BAND_W), :] += dk
            dvpad[pl.ds(r0_, BAND_W), :] += dv
            return 0

        lax.fori_loop(0, S // BQ, qblock, 0)
        dqkv_ref[1] = dkpad[BAND_PAD:, :].astype(BF16)
        dqkv_ref[2] = dvpad[BAND_PAD:, :].astype(BF16)
        sub = _iota((8, TOEP_W), 0)
        for hh in range(2):
            folded = db_acc[hh, 0:8, :]
            for a in range(1, BQ // 8):
                folded = folded + pltpu.roll(db_acc[hh, 8 * a:8 * a + 8, :], TOEP_W - 8 * a, 1)
            for bit in range(3):
                moved = pltpu.roll(folded, TOEP_W - (1 << bit), 1)
                folded = jnp.where((sub & (1 << bit)) != 0, moved, folded)
            dr0_ref[hh:hh + 1, :] = jnp.sum(folded, axis=0, keepdims=True)

    return _carrier_call(
        body, name=name, grid=(C_H // 2,),
        in_specs=[pl.BlockSpec((3, S, LANES), lambda p: (0, 0, p)), pl.BlockSpec((None, 2, TOEP_W), lambda p: (p, 0, 0)),
                  pl.BlockSpec((S, LANES), lambda p: (0, p))],
        out_specs=[pl.BlockSpec((3, S, LANES), lambda p: (0, 0, p)), pl.BlockSpec((None, 2, TOEP_W), lambda p: (p, 0, 0))],
        out_shape=[jax.ShapeDtypeStruct((3, S, C_H * C_DIM), BF16), jax.ShapeDtypeStruct((C_H // 2, 2, TOEP_W), F32)],
        scratch_shapes=[pltpu.VMEM((S + BAND_PAD, LANES), BF16), pltpu.VMEM((S + BAND_PAD, LANES), BF16),
                        pltpu.VMEM((S + BAND_PAD, LANES), F32), pltpu.VMEM((S + BAND_PAD, LANES), F32),
                        pltpu.VMEM((2, BQ, TOEP_W), F32)],
        args=(qkv, r0, do), sem=("parallel",), carry=carry)


def _bias_table_grad(name, dr0):
    w_out = 5 * LANES

    def body(d_ref, o_ref):
        j = _iota((TOEP_W, w_out), 0)
        rel = jnp.clip(LEFT_CHUNKS * CHUNK - j, -REL_CLIP, REL_CLIP) + REL_CLIP
        rel = jnp.where(j >= BAND_W, 2 * REL_CLIP, rel)
        onehot = (rel == _iota((TOEP_W, w_out), 1)).astype(BF16)
        d = d_ref[...]
        hi = d.astype(BF16)
        mid = (d - hi.astype(F32))
        mid_b = mid.astype(BF16)
        lo = (mid - mid_b.astype(F32)).astype(BF16)
        o_ref[...] = _dot(hi, onehot) + _dot(mid_b, onehot) + _dot(lo, onehot)

    return pl.pallas_call(
        body, name=name, out_shape=jax.ShapeDtypeStruct((C_H, w_out), F32),
        in_specs=[pl.BlockSpec((C_H, TOEP_W), lambda: (0, 0))], out_specs=pl.BlockSpec((C_H, w_out), lambda: (0, 0)),
        grid=(),
    )(dr0)


def _carry_gather(cy, slots, names, ici, d2d):
    idx = [cy.operand(slots[n], True) for n in names]
    n = len(names)
    base_i = cy.sems(3 * n) if ici else 0
    base_d = cy.sems(3 * n) if d2d else 0

    def piece(refs, t, slot, cc):
        return refs[idx[t]].at[slot, _half_rows(cc, slots[names[t]].shape[1] // 2), :]

    def over_ici(refs, send, recv, arriving):
        x, y, c, chips = _position()
        out = []
        for t in range(n):
            for j in range(3):
                r = piece(refs, t, 2 * chips[j][0] + chips[j][1] if arriving else 2 * x + y, c)
                out.append(_remote(r, r, send, recv, base_i + 3 * t + j, (*chips[j], c)))
        return out

    def over_d2d(refs, send, recv, arriving):
        x, y, c, chips = _position()
        out = []
        for t in range(n):
            for j in range(3):
                r = piece(refs, t, 2 * chips[j][0] + chips[j][1], 1 - c if arriving else c)
                out.append(_remote(r, r, send, recv, base_d + 3 * t + j, (x, y, 1 - c)))
        return out

    def start_ici(refs, fresh, send, recv):
        for cp in over_ici(refs, send, recv, False):
            cp.start()

    def wait_ici(refs, fresh, send, recv):
        for cp in over_ici(refs, send, recv, True):
            cp.wait_recv()
        for cp in over_ici(refs, send, recv, False):
            cp.wait_send()

    def start_d2d(refs, fresh, send, recv):
        for cp in over_d2d(refs, send, recv, False):
            cp.start()

    def wait_d2d(refs, fresh, send, recv):
        for cp in over_d2d(refs, send, recv, True):
            cp.wait_recv()
        for cp in over_d2d(refs, send, recv, False):
            cp.wait_send()

    def wait_ici_and_forward(refs, fresh, send, recv):
        forwards = over_d2d(refs, send, recv, False)
        for k, cp in enumerate(over_ici(refs, send, recv, True)):
            cp.wait_recv()
            forwards[k].start()
        for cp in over_ici(refs, send, recv, False):
            cp.wait_send()

    if ici and d2d:
        cy.starts.append(start_ici)
        cy.finishes += [wait_ici_and_forward, wait_d2d]
    elif ici:
        cy.starts.append(start_ici)
        cy.finishes.append(wait_ici)
    else:
        cy.starts.append(start_d2d)
        cy.finishes.append(wait_d2d)

    def done(aliased, fresh):
        for t, name in enumerate(names):
            slots[name] = aliased[idx[t]]

    cy.on_done.append(done)


def _carry_chip_exchange(cy, sums, got, names):
    idx = [cy.operand(sums[n], False) for n in names]
    out = [cy.result((3,) + sums[n].shape[1:], BF16) for n in names]
    base = cy.sems(3 * len(names))

    def copies(refs, fresh, send, recv):
        x, y, c, chips = _position()
        return [_remote(refs[idx[t]].at[2 * chips[j][0] + chips[j][1]], fresh[out[t]].at[j], send, recv, base + 3 * t + j,
                        (*chips[j], c)) for t in range(len(names)) for j in range(3)]

    def start(refs, fresh, send, recv):
        for cp in copies(refs, fresh, send, recv):
            cp.start()

    def wait(refs, fresh, send, recv):
        for cp in copies(refs, fresh, send, recv):
            cp.wait()

    cy.starts.append(start)
    cy.finishes.append(wait)

    def done(aliased, fresh):
        for t, name in enumerate(names):
            got[name] = fresh[out[t]]

    cy.on_done.append(done)


def _run_carry(name, cy):
    _, res = _carrier_call(None, name=name, grid=(), in_specs=[], out_specs=[], out_shape=[], args=(), sem=(), carry=cy)
    cy.done(res)


FIRST_WEIGHTS = ("ev_w_in",)
NEXT_WEIGHTS = ("ev_w_uq", "ev_w_ukv")
WEIGHTS_A = ("ev_w_out", "w_gate0", "w_up0")
WEIGHTS_B = ("w_down0", "od_w_qkv", "od_w_out")
WEIGHTS_C = ("w_gate1",)
WEIGHTS_D = ("w_up1", "w_down1")
GRAD_GROUPS = {"ffn1": ("w_gate1", "w_up1", "w_down1"), "od": ("od_w_qkv", "od_w_out"),
               "ffn0": ("w_gate0", "w_up0", "w_down0"), "ev_out": ("ev_w_out",),
               "ev": ("ev_w_in", "ev_w_uq", "ev_w_ukv")}


def _carry_pair_exchange(cy, parts, theirs, names):
    idx = [cy.operand(parts[n], False) for n in names]
    out = [cy.result((N_CHIPS, parts[n].shape[1] // 2, parts[n].shape[2]), BF16) for n in names]
    base = cy.sems(len(names))

    def copies(refs, fresh, send, recv):
        x, y, c, _ = _position()
        return [_remote(refs[idx[t]].at[:, _half_rows(1 - c, parts[n].shape[1] // 2), :], fresh[out[t]], send, recv,
                        base + t, (x, y, 1 - c)) for t, n in enumerate(names)]

    cy.starts.append(lambda refs, fresh, send, recv: [cp.start() for cp in copies(refs, fresh, send, recv)])
    cy.finishes.append(lambda refs, fresh, send, recv: [cp.wait() for cp in copies(refs, fresh, send, recv)])

    def done(aliased, fresh):
        for t, name in enumerate(names):
            theirs[name] = fresh[out[t]]

    cy.on_done.append(done)


def _carry_sibling_exchange(cy, fulls, pieces):
    idx = [cy.operand(fulls[p], True) for p, _ in pieces]
    base = cy.sems(len(pieces))

    def copies(refs, send, recv, arriving):
        x, y, c, _ = _position()
        out = []
        for t, (p, layer) in enumerate(pieces):
            r = refs[idx[t]].at[layer, _half_rows(1 - c if arriving else c, fulls[p].shape[1] // 2), :]
            out.append(_remote(r, r, send, recv, base + t, (x, y, 1 - c)))
        return out

    def start(refs, fresh, send, recv):
        for cp in copies(refs, send, recv, False):
            cp.start()

    def wait(refs, fresh, send, recv):
        for cp in copies(refs, send, recv, True):
            cp.wait_recv()
        for cp in copies(refs, send, recv, False):
            cp.wait_send()

    cy.starts.append(start)
    cy.finishes.append(wait)

    def done(aliased, fresh):
        for t, (p, _) in enumerate(pieces):
            fulls[p] = aliased[idx[t]]

    cy.on_done.append(done)


RIDES = {
    "cast_rest": (("gather", FIRST_WEIGHTS),),
    "proj_in": (("gather", NEXT_WEIGHTS),),
    "mla_attn": (("gather_ici", WEIGHTS_A),),
    "sb_attn": (("gather_d2d", WEIGHTS_A), ("gather_ici", WEIGHTS_B)),
    "ev_out": (("gather_d2d", WEIGHTS_B),),
    "ffn0": (("gather_ici", WEIGHTS_C),),
    "qkv": (("gather_d2d", WEIGHTS_C),),
    "band_attn": (("gather_ici", WEIGHTS_D),),
    "od_out": (("gather_d2d", WEIGHTS_D),),
    "od_out_bwd_w": (("pair", "ffn1"),),
    "band_attn_bwd": (("chips", "ffn1"),),
    "rms_mix1_bwd": (("pair", "od"),),
    "ev_out_bwd_w": (("pair", "ffn0"),),
    "mla_attn_bwd": (("chips", "od"), ("sibling", "ffn1"), ("pair", "ev_out")),
    "sb_attn_bwd": (("chips", "ffn0"), ("sibling", "od"), ("chips", "ev_out")),
    "proj_in_bwd_w": (("sibling", "ffn0"), ("sibling", "ev_out")),
    "grads_pair_ev": (("pair", "ev"),),
    "proj_in_bwd_x": (("chips", "ev"),),
    "grads_sibling_ev": (("sibling", "ev"),),
}


class _Exchanges:
    def __init__(self, slots, pos, shapes, cast_rest):
        self.slots, self.pos, self.shapes, self.cast_rest = dict(slots), pos, shapes, cast_rest
        self.parts, self.theirs, self.sums, self.got, self.fulls = {}, {}, {}, {}, {}

    def begin(self):
        self.slots.update(self.cast_rest(self.carry("cast_rest")))

    def weights(self, *names):
        return [self.slots[n] for n in names]

    def _pair_sums(self, group):
        names = GRAD_GROUPS[group]
        self.sums.update(zip(names, _pair_sums("pair_sums_" + group, [self.parts[n] for n in names],
                                               [self.theirs[n] for n in names], self.pos)))

    def _chip_sums(self, group):
        names = GRAD_GROUPS[group]
        items = [(self.sums[n], self.got[n], PART_OF[n][1], self.shapes[PART_OF[n][0]], self.fulls.get(PART_OF[n][0]))
                 for n in names]
        self.fulls.update(zip([PART_OF[n][0] for n in names], _chip_sums("chip_sums_" + group, items, self.pos)))

    def carry(self, stage):
        cy = _Carry()
        for step, what in RIDES[stage]:
            if step == "gather":
                _carry_gather(cy, self.slots, what, True, True)
            elif step == "gather_ici":
                _carry_gather(cy, self.slots, what, True, False)
            elif step == "gather_d2d":
                _carry_gather(cy, self.slots, what, False, True)
            elif step == "pair":
                _carry_pair_exchange(cy, self.parts, self.theirs, GRAD_GROUPS[what])
            elif step == "chips":
                self._pair_sums(what)
                _carry_chip_exchange(cy, self.sums, self.got, GRAD_GROUPS[what])
            elif step == "sibling":
                self._chip_sums(what)
                _carry_sibling_exchange(cy, self.fulls, [PART_OF[n] for n in GRAD_GROUPS[what]])
        return cy

    def grads(self, group, parts):
        self.parts.update(parts)
        if group == "ev":
            _run_carry("grads_pair_ev", self.carry("grads_pair_ev"))

    def finish(self):
        _run_carry("grads_sibling_ev", self.carry("grads_sibling_ev"))
        return {n: self.fulls[n] for n in BIG}


class _NoExchanges:
    def __init__(self, slots):
        self.slots, self.parts = dict(slots), {}

    def begin(self):
        pass

    def weights(self, *names):
        return [self.slots[n] for n in names]

    def carry(self, stage):
        return None

    def grads(self, group, parts):
        self.parts.update(parts)


def _w_in_pieces():
    segments = ((0, Q_LORA, P_CQ), (Q_LORA, Q_LORA + KV_LORA, P_CKV),
                (Q_LORA + KV_LORA, Q_LORA + KV_LORA + MLA_ROPE, P_KR + KR_LANE),
                (Q_LORA + KV_LORA + MLA_ROPE, EVEN_IN, P_QB))
    width = EVEN_IN // N_CHIPS
    pieces = []
    for lo, hi, at in segments:
        for k in range(N_CHIPS):
            a, b = max(lo, k * width), min(hi, (k + 1) * width)
            if a < b:
                pieces.append((k, a - k * width, b - a, at + a - lo))
    return pieces


def _w_in_padded(name, w_in_s):
    tr = MXU_W

    def body(s_ref, o_ref):
        o_ref[...] = jnp.zeros(o_ref.shape, BF16)
        for k, a, n, at in _w_in_pieces():
            o_ref[:, at:at + n] = s_ref[k, :, a:a + n]

    return pl.pallas_call(
        body, name=name, grid=(D // tr,),
        in_specs=[pl.BlockSpec((N_CHIPS, tr, EVEN_IN // N_CHIPS), lambda i: (0, i, 0))],
        out_specs=pl.BlockSpec((tr, P_IN), lambda i: (i, 0)), out_shape=jax.ShapeDtypeStruct((D, P_IN), BF16),
        compiler_params=_params("parallel"))(w_in_s)


def _w_in_sharded(name, d_w_in_p):
    tr = MXU_W

    def body(p_ref, o_ref):
        for k, a, n, at in _w_in_pieces():
            o_ref[k, :, a:a + n] = p_ref[:, at:at + n]

    return pl.pallas_call(
        body, name=name, grid=(D // tr,),
        in_specs=[pl.BlockSpec((tr, P_IN), lambda i: (i, 0))],
        out_specs=pl.BlockSpec((N_CHIPS, tr, EVEN_IN // N_CHIPS), lambda i: (0, i, 0)),
        out_shape=jax.ShapeDtypeStruct((N_CHIPS, D, EVEN_IN // N_CHIPS), BF16),
        compiler_params=_params("parallel"))(d_w_in_p)


def _mla_weights(w_uq_s, w_ukv_s):
    w_uq = jnp.moveaxis(w_uq_s, 0, 1).reshape(Q_LORA, MLA_H, MLA_NOPE + MLA_ROPE)
    w_uq_p = jnp.concatenate([w_uq, jnp.zeros((Q_LORA, MLA_H, LANES - MLA_NOPE - MLA_ROPE), BF16)], axis=2)
    w_ukv = jnp.moveaxis(w_ukv_s, 0, 1).reshape(KV_LORA, MLA_H, MLA_NOPE + MLA_V)
    w_uk_p = jnp.concatenate([w_ukv[:, :, :MLA_NOPE], jnp.zeros((KV_LORA, MLA_H, LANES - MLA_NOPE), BF16)], axis=2)
    return dict(
        w_uq=w_uq_p.reshape(Q_LORA, MLA_H * LANES), w_uk=w_uk_p.reshape(KV_LORA, MLA_H * LANES),
        w_uv=w_ukv[:, :, MLA_NOPE:].reshape(KV_LORA, MLA_H * MLA_V))


def _proj_mm(name, u, w_in, carry=None):
    return _mm(name, u, w_in, kind="nn", grid=(S // TM, 1, 1),
               a_spec=pl.BlockSpec((TM, D), lambda i, j, k: (i, 0)), b_spec=pl.BlockSpec((D, P_IN), lambda i, j, k: (0, 0)),
               o_spec=pl.BlockSpec((TM, P_IN), lambda i, j, k: (i, 0)), out_shape=(S, P_IN), out_dtype=F32, acc_shape=None,
               carry=carry)


def _out_proj(name, o, w, resid, carry=None):
    return _mm(name, o, w, kind="nn", grid=(S // TQ, 1, 1),
               a_spec=pl.BlockSpec((TQ, D), lambda i, j, k: (i, 0)), b_spec=pl.BlockSpec((D, D), lambda i, j, k: (0, 0)),
               o_spec=pl.BlockSpec((TQ, D), lambda i, j, k: (i, 0)), out_shape=(S, D), out_dtype=F32, acc_shape=None,
               resid=resid, r_spec=pl.BlockSpec((TQ, D), lambda i, j, k: (i, 0)), carry=carry)


def _out_proj_bwd(name, dh, o, w, ex):
    d_o = _mm(name + "_x", dh, w, kind="nt", grid=(S // TQ, 1, 1),
              a_spec=pl.BlockSpec((TQ, D), lambda i, j, k: (i, 0)), b_spec=pl.BlockSpec((D, D), lambda i, j, k: (0, 0)),
              o_spec=pl.BlockSpec((TQ, D), lambda i, j, k: (i, 0)), out_shape=(S, D), out_dtype=F32, acc_shape=None)
    d_w = _mm(name + "_w", o, dh, kind="tn", grid=(2, S // TQ),
              a_spec=pl.BlockSpec((TQ, TM), lambda j, k: (k, j)), b_spec=pl.BlockSpec((TQ, D), lambda j, k: (k, 0)),
              o_spec=pl.BlockSpec((TM, D), lambda j, k: (j, 0)), out_shape=(D, D), out_dtype=BF16, acc_shape=(TM, D),
              carry=ex.carry(name + "_w"))
    return d_o, d_w


def _local_step(x, tgt, sm, ex):
    def riding(stage, fn, *args):
        cy = ex.carry(stage)
        res, copies = fn(stage, *args, carry=cy)
        if cy is not None:
            cy.done(copies)
        return res

    cos_t, sin_t = _rope_tables()
    g_mix, g_ffn = sm["g_mix"], sm["g_ffn"]
    r0 = sm["od_rel_bias"][0][:, _band_row_index()].reshape(C_H // 2, 2, TOEP_W)
    nt = 3

    ex.begin()
    w = {"w_in": _w_in_padded("w_in_padded", *ex.weights(*FIRST_WEIGHTS))}
    u0 = _rms_fwd("rms_mix0", x, g_mix[0:1])
    proj = _proj_mm("proj_in", u0, w["w_in"], ex.carry("proj_in"))
    w.update(_mla_weights(*ex.weights(*NEXT_WEIGHTS)))
    qa, ka, va = _mla_prep_fwd("mla_prep", proj, sm["ev_g_cq"], sm["ev_g_ckv"], w["w_uq"], w["w_uk"], w["w_uv"], cos_t, sin_t)
    o_a, lse = riding("mla_attn", _mla_fwd, qa, ka, va)
    o_b, = riding("sb_attn", _sb_fwd, proj)
    o_ev = jnp.concatenate([o_a.astype(BF16), o_b], axis=1)
    w["ev_w_out"] = ex.weights("ev_w_out")[0].reshape(D, D)
    h1 = _out_proj("ev_out", o_ev, w["ev_w_out"], x, ex.carry("ev_out"))
    w["w_gate0"], w["w_up0"], w["w_down0"] = ex.weights("w_gate0", "w_up0", "w_down0")
    h2, gate0, up0 = riding("ffn0", _ffn_fwd, h1, g_ffn[0:1], w["w_gate0"], w["w_up0"], w["w_down0"])
    w["w_qkv"] = jnp.moveaxis(ex.weights("od_w_qkv")[0], 0, 1).reshape(D, nt * D)
    u2 = _rms_fwd("rms_mix1", h2, g_mix[1:2])
    qkv = _mm("qkv", u2, w["w_qkv"], kind="nn", grid=(S // TQ, nt, 1),
              a_spec=pl.BlockSpec((TQ, D), lambda i, t, k: (i, 0)), b_spec=pl.BlockSpec((D, D), lambda i, t, k: (0, t)),
              o_spec=pl.BlockSpec((None, TQ, D), lambda i, t, k: (t, i, 0)),
              out_shape=(nt, S, D), out_dtype=BF16, acc_shape=None, carry=ex.carry("qkv"))
    o_od, = riding("band_attn", _band_fwd, qkv, r0)
    w["od_w_out"] = ex.weights("od_w_out")[0].reshape(D, D)
    h3 = _out_proj("od_out", o_od, w["od_w_out"], h2, ex.carry("od_out"))
    w["w_gate1"], w["w_up1"], w["w_down1"] = ex.weights("w_gate1", "w_up1", "w_down1")
    (h4, gate1, up1), _ = _ffn_fwd("ffn1", h3, g_ffn[1:2], w["w_gate1"], w["w_up1"], w["w_down1"])

    loss, dh4, dg_final = _loss_bwd("loss", h4, sm["g_final"].reshape(1, D), tgt)

    dh3, dg_ffn1, u3, dgate, dup, act = _ffn_bwd("ffn1_bwd", dh4, h3, g_ffn[1:2], gate1, up1,
                                                 w["w_gate1"], w["w_up1"], w["w_down1"])
    d_wg1, d_wu1, d_wd1 = _ffn_wgrads("ffn1_dw", u3, dgate, dup, act, dh4)
    ex.grads("ffn1", {"w_gate1": d_wg1, "w_up1": d_wu1, "w_down1": d_wd1})

    d_ood, d_w_od_out = _out_proj_bwd("od_out_bwd", dh3, o_od, w["od_w_out"], ex)
    dqkv, dr0 = riding("band_attn_bwd", _band_bwd, qkv, r0, d_ood)
    du2 = _mm("qkv_bwd_x", dqkv, w["w_qkv"], kind="nt", grid=(S // TQ, nt),
              a_spec=pl.BlockSpec((None, TQ, D), lambda i, t: (t, i, 0)), b_spec=pl.BlockSpec((D, D), lambda i, t: (0, t)),
              o_spec=pl.BlockSpec((TQ, D), lambda i, t: (i, 0)), out_shape=(S, D), out_dtype=F32, acc_shape=(TQ, D))
    wide, per = D // MXU_W, nt * D // N_CHIPS // MXU_W
    piece = lambda r: pl.BlockSpec((None, TQ, MXU_W), lambda j, k: ((per * j + r) // wide, k, (per * j + r) % wide))
    d_w_qkv = _mm("qkv_bwd_w", u2, dqkv, kind="tn", grid=(N_CHIPS, S // TQ),
                  a_spec=pl.BlockSpec((TQ, D), lambda j, k: (k, 0)), b_spec=[piece(r) for r in range(per)],
                  o_spec=pl.BlockSpec((None, D, per * MXU_W), lambda j, k: (j, 0, 0)),
                  out_shape=(N_CHIPS, D, per * MXU_W), out_dtype=BF16, acc_shape=(D, per * MXU_W))
    shard_cols = lambda a: jnp.moveaxis(a.reshape(a.shape[0], N_CHIPS, a.shape[1] // N_CHIPS), 1, 0)
    ex.grads("od", {"od_w_qkv": d_w_qkv, "od_w_out": d_w_od_out.reshape(N_CHIPS, D // N_CHIPS, D)})
    dh2, dg_mix1 = _rms_bwd("rms_mix1_bwd", du2, h2, g_mix[1:2], dh3, carry=ex.carry("rms_mix1_bwd"))
    d_rel = _bias_table_grad("rel_bias_grad", dr0.reshape(C_H, TOEP_W))[:, :2 * REL_CLIP + 1]

    dh1, dg_ffn0, u1, dgate, dup, act = _ffn_bwd("ffn0_bwd", dh2, h1, g_ffn[0:1], gate0, up0,
                                                 w["w_gate0"], w["w_up0"], w["w_down0"])
    d_wg0, d_wu0, d_wd0 = _ffn_wgrads("ffn0_dw", u1, dgate, dup, act, dh2)
    ex.grads("ffn0", {"w_gate0": d_wg0, "w_up0": d_wu0, "w_down0": d_wd0})

    d_oev, d_w_ev_out = _out_proj_bwd("ev_out_bwd", dh1, o_ev, w["ev_w_out"], ex)
    ex.grads("ev_out", {"ev_w_out": d_w_ev_out.reshape(N_CHIPS, D // N_CHIPS, D)})
    dqa, dka, dva = riding("mla_attn_bwd", _mla_bwd, qa, ka, va, o_a, lse, d_oev, 0)
    dqb, dkb, dvb = riding("sb_attn_bwd", _sb_bwd, proj, d_oev, MLA_H * MLA_V // LANES)
    dcq, dckv, dkr, d_w_uq, d_w_uk, d_w_uv, dg_cq, dg_ckv = _mla_prep_bwd(
        "mla_prep_bwd", dqa, dka, dva, proj, sm["ev_g_cq"], sm["ev_g_ckv"], w["w_uq"], w["w_uk"], w["w_uv"], cos_t, sin_t)
    dproj = [dcq, jnp.zeros((S, LANES), BF16), dckv, dqb, dkb, dvb, dkr]
    d_w_in_p = _mm("proj_in_bwd_w", u0, dproj, kind="tn", grid=(1, S // TQ),
                   a_spec=pl.BlockSpec((TQ, D), lambda j, k: (k, 0)),
                   b_spec=[pl.BlockSpec((TQ, p.shape[1]), lambda j, k: (k, 0)) for p in dproj],
                   o_spec=pl.BlockSpec((D, P_IN), lambda j, k: (0, 0)), out_shape=(D, P_IN), out_dtype=BF16,
                   acc_shape=(D, P_IN), carry=ex.carry("proj_in_bwd_w"))
    d_w_uq_std = d_w_uq.reshape(Q_LORA, MLA_H, LANES)[:, :, :MLA_NOPE + MLA_ROPE].reshape(Q_LORA, -1)
    d_w_ukv = jnp.concatenate([d_w_uk.reshape(KV_LORA, MLA_H, LANES)[:, :, :MLA_NOPE],
                               d_w_uv.reshape(KV_LORA, MLA_H, MLA_V)], axis=2).reshape(KV_LORA, -1)
    ex.grads("ev", {"ev_w_in": _w_in_sharded("w_in_sharded", d_w_in_p), "ev_w_uq": shard_cols(d_w_uq_std.astype(BF16)),
                    "ev_w_ukv": shard_cols(d_w_ukv.astype(BF16))})
    du0 = _mm("proj_in_bwd_x", dproj, w["w_in"], kind="nt", grid=(S // TM, 1, 1),
              a_spec=[pl.BlockSpec((TM, p.shape[1]), lambda i, j, k: (i, 0)) for p in dproj],
              b_spec=pl.BlockSpec((D, P_IN), lambda i, j, k: (0, 0)),
              o_spec=pl.BlockSpec((TM, D), lambda i, j, k: (i, 0)), out_shape=(S, D), out_dtype=F32, acc_shape=None,
              carry=ex.carry("proj_in_bwd_x"))
    grad_x, dg_mix0 = _rms_bwd("rms_mix0_bwd", du0, x, g_mix[0:1], dh1)
    small = {
        "ev_g_cq": dg_cq, "ev_g_ckv": dg_ckv, "od_rel_bias": d_rel.reshape(1, C_H, 2 * REL_CLIP + 1),
        "g_mix": jnp.concatenate([dg_mix0, dg_mix1], axis=0), "g_ffn": jnp.concatenate([dg_ffn0, dg_ffn1], axis=0),
        "g_final": dg_final.reshape(D),
    }
    return loss, grad_x, small


BIG = ("ev_w_in", "ev_w_uq", "ev_w_ukv", "ev_w_out", "od_w_qkv", "od_w_out", "w_gate", "w_up", "w_down")
SMALL = ("ev_g_cq", "ev_g_ckv", "od_rel_bias", "g_mix", "g_ffn", "g_final")
WEIGHTS = ("ev_w_in", "ev_g_cq", "ev_w_uq", "ev_g_ckv", "ev_w_ukv", "ev_w_out", "od_w_qkv", "od_rel_bias", "od_w_out",
           "g_mix", "g_ffn", "w_gate", "w_up", "w_down", "g_final")
GRAD_PARTS = (("ev_w_in", "ev_w_in", 0), ("ev_w_uq", "ev_w_uq", 0), ("ev_w_ukv", "ev_w_ukv", 0),
              ("ev_w_out", "ev_w_out", 0), ("od_w_qkv", "od_w_qkv", 0), ("od_w_out", "od_w_out", 0),
              ("w_gate0", "w_gate", 0), ("w_gate1", "w_gate", 1), ("w_up0", "w_up", 0), ("w_up1", "w_up", 1),
              ("w_down0", "w_down", 0), ("w_down1", "w_down", 1))
PART_OF = {part: (param, layer) for part, param, layer in GRAD_PARTS}
TRANSPOSED = ("w_gate", "w_up")
ADAMW_TRANSPOSED = ("ev_w_in", "ev_w_uq")


def _row_tile(rows, cap=512, sublanes=16):
    for t in range(min(rows, cap), 0, -1):
        if rows % t == 0 and t % sublanes == 0:
            return t
    return rows


def _cast_into_slot(name, w, layer, pos):
    _, rows, cols = w.shape
    tr = _row_tile(rows)

    def body(pos_ref, w_ref, o_ref):
        o_ref[...] = w_ref[...].astype(BF16)

    return pl.pallas_call(
        body, name=name,
        grid_spec=pltpu.PrefetchScalarGridSpec(
            num_scalar_prefetch=1, grid=(rows // tr,),
            in_specs=[pl.BlockSpec((None, tr, cols), lambda i, p: (layer, i, 0))],
            out_specs=pl.BlockSpec((None, tr, cols), lambda i, p: (p[0], i, 0))),
        out_shape=jax.ShapeDtypeStruct((N_CHIPS, rows, cols), BF16), compiler_params=_params("arbitrary"))(pos, w)


def _cast_many_into_slots(name, items, pos, carry):
    tiles = [_row_tile(w.shape[1]) for w, _ in items]
    turns = _Turns([w.shape[1] // tr for (w, _), tr in zip(items, tiles)])

    def body(pos_ref, *refs):
        i = pl.program_id(0)
        for t in range(len(items)):
            @pl.when(turns.mine(t, i))
            def _(w_ref=refs[t], o_ref=refs[len(items) + t]):
                o_ref[...] = w_ref[...].astype(BF16)

    in_specs, out_specs, out_shape = [], [], []
    for t, ((w, layer), tr) in enumerate(zip(items, tiles)):
        _, rows, cols = w.shape
        at = turns.step(t)
        in_specs.append(pl.BlockSpec((None, tr, cols), lambda i, p, at=at, layer=layer: (layer, at(i), 0)))
        out_specs.append(pl.BlockSpec((None, tr, cols), lambda i, p, at=at: (p[0], at(i), 0)))
        out_shape.append(jax.ShapeDtypeStruct((N_CHIPS, rows, cols), BF16))
    res, copies = _carrier_call(body, name=name, grid=(turns.total,), in_specs=in_specs, out_specs=out_specs,
                                out_shape=out_shape, args=[w for w, _ in items], sem=("arbitrary",), carry=carry,
                                prefetch=(pos,))
    if carry is not None:
        carry.done(copies)
    return res


class _Turns:
    def __init__(self, counts):
        self.counts = list(counts)
        self.starts = [sum(self.counts[:t]) for t in range(len(self.counts))]
        self.total = sum(self.counts)

    def step(self, t):
        start, n = self.starts[t], self.counts[t]
        return lambda i: jnp.clip(i - start, 0, n - 1)

    def mine(self, t, i):
        return (i >= self.starts[t]) & (i < self.starts[t] + self.counts[t])


def _pair_sums(name, parts, theirs, pos):
    n, pair = len(parts), 2
    tiles = [_row_tile(b.shape[1]) for b in theirs]
    blocks = [b.shape[1] // tr for b, tr in zip(theirs, tiles)]
    turns = _Turns([N_CHIPS // pair * nb for nb in blocks])

    def body(pos_ref, *refs):
        i = pl.program_id(0)
        for t in range(n):
            @pl.when(turns.mine(t, i))
            def _(a_ref=refs[2 * t], b_ref=refs[2 * t + 1], o_ref=refs[2 * n + t]):
                o_ref[...] = (a_ref[...].astype(F32) + b_ref[...].astype(F32)).astype(BF16)

    in_specs, out_specs = [], []
    for t, (b, tr, nb) in enumerate(zip(theirs, tiles, blocks)):
        at, block = turns.step(t), (pair, tr, b.shape[2])
        in_specs.append(pl.BlockSpec(block, lambda i, p, at=at, nb=nb: (at(i) // nb, p[1] * nb + at(i) % nb, 0)))
        in_specs.append(pl.BlockSpec(block, lambda i, p, at=at, nb=nb: (at(i) // nb, at(i) % nb, 0)))
        out_specs.append(pl.BlockSpec(block, lambda i, p, at=at, nb=nb: (at(i) // nb, at(i) % nb, 0)))
    return pl.pallas_call(
        body, name=name,
        grid_spec=pltpu.PrefetchScalarGridSpec(num_scalar_prefetch=1, grid=(turns.total,), in_specs=in_specs,
                                               out_specs=out_specs),
        out_shape=[jax.ShapeDtypeStruct(b.shape, BF16) for b in theirs],
        compiler_params=_params("arbitrary"))(pos, *[a for pair in zip(parts, theirs) for a in pair])


def _chip_sums(name, items, pos):
    n = len(items)
    tiles = [_row_tile(s.shape[1]) for s, *_ in items]
    turns = _Turns([s.shape[1] // tr for (s, *_), tr in zip(items, tiles)])
    carried = [t for t, item in enumerate(items) if item[4] is not None]

    def body(pos_ref, *refs):
        i = pl.program_id(0)
        for t in range(n):
            @pl.when(turns.mine(t, i))
            def _(s_ref=refs[2 * t], g_ref=refs[2 * t + 1], o_ref=refs[2 * n + len(carried) + t]):
                o_ref[...] = ((s_ref[...].astype(F32) + g_ref[0].astype(F32)) + g_ref[1].astype(F32)) + g_ref[2].astype(F32)

    in_specs, out_specs = [], []
    for t, ((s, got, layer, full_shape, full), tr) in enumerate(zip(items, tiles)):
        at, cols, nb = turns.step(t), s.shape[2], turns.counts[t]
        in_specs.append(pl.BlockSpec((None, tr, cols), lambda i, p, at=at: (p[0], at(i), 0)))
        in_specs.append(pl.BlockSpec((3, tr, cols), lambda i, p, at=at: (0, at(i), 0)))
        out_specs.append(pl.BlockSpec((None, tr, cols), lambda i, p, at=at, nb=nb, layer=layer: (layer, p[1] * nb + at(i), 0)))
    return pl.pallas_call(
        body, name=name,
        grid_spec=pltpu.PrefetchScalarGridSpec(num_scalar_prefetch=1, grid=(turns.total,),
                                               in_specs=in_specs + [ANY] * len(carried), out_specs=out_specs),
        out_shape=[jax.ShapeDtypeStruct(item[3], F32) for item in items],
        input_output_aliases={1 + 2 * n + k: t for k, t in enumerate(carried)},
        compiler_params=_params("arbitrary"))(
            pos, *[a for item in items for a in item[:2]], *[items[t][4] for t in carried])


def _adamw_update(w, g, m, v):
    m_new = ADAM_B1 * m + (1.0 - ADAM_B1) * g
    v_new = ADAM_B2 * v + (1.0 - ADAM_B2) * (g * g)
    m_hat = m_new / (1.0 - ADAM_B1 ** ADAM_STEP)
    v_hat = v_new / (1.0 - ADAM_B2 ** ADAM_STEP)
    return -ADAM_LR * (m_hat / (jnp.sqrt(v_hat) + ADAM_EPS) + ADAM_WD * w), m_new, v_new


def _small_step(name, grads, loss, w, m, v):
    n, n_dev = len(grads), 8
    offs = [sum(g.shape[0] for g in grads[:t]) for t in range(n + 1)]
    rows = -(-(offs[n] + 1) // 8) * 8
    width = max(g.shape[1] for g in grads)

    def body(*refs):
        g_refs, loss_ref = refs[:n], refs[n]
        w_refs, m_refs, v_refs = (refs[1 + k * n:1 + (k + 1) * n] for k in (1, 2, 3))
        outs = refs[4 * n + 1:8 * n + 2]
        mine, slots, send_sem, recv_sem = refs[8 * n + 2:]
        x, y, c, _ = _position()
        me = 4 * x + 2 * y + c

        def peer(k):
            return (1 - x if k & 4 else x, 1 - y if k & 2 else y, 1 - c if k & 1 else c)

        def logical(k):
            px, py, pc = peer(k)
            return 4 * px + 2 * py + pc

        mine[...] = jnp.zeros(mine.shape, F32)
        for t in range(n):
            mine[offs[t]:offs[t + 1], 0:grads[t].shape[1]] = g_refs[t][...]
        mine[offs[n]:offs[n] + 1, 0:LANES] = loss_ref[...]
        slots[me] = mine[...]
        sends = [pltpu.make_async_remote_copy(
            src_ref=mine, dst_ref=slots.at[me], send_sem=send_sem.at[k], recv_sem=recv_sem.at[k],
            device_id=peer(k), device_id_type=MESH) for k in range(1, n_dev)]
        for cp in sends:
            cp.start()
        for k in range(1, n_dev):
            pltpu.make_async_remote_copy(
                src_ref=mine, dst_ref=slots.at[logical(k)], send_sem=send_sem.at[k], recv_sem=recv_sem.at[k],
                device_id=peer(k), device_id_type=MESH).wait_recv()
        for cp in sends:
            cp.wait_send()
        total = slots[0]
        for d in range(1, n_dev):
            total = total + slots[d]
        for t in range(n):
            gv = total[offs[t]:offs[t + 1], 0:grads[t].shape[1]]
            outs[t][...] = gv
            outs[n + t][...], outs[2 * n + t][...], outs[3 * n + t][...] = _adamw_update(
                w_refs[t][...], gv, m_refs[t][...], v_refs[t][...])
        outs[4 * n][...] = total[offs[n]:offs[n] + 1, 0:LANES]

    vm = pl.BlockSpec(memory_space=pltpu.VMEM)
    shapes = [jax.ShapeDtypeStruct(g.shape, F32) for g in grads]
    res = pl.pallas_call(
        body, name=name, in_specs=[vm] * (4 * n + 1), out_specs=[vm] * (4 * n + 1),
        out_shape=shapes * 4 + [jax.ShapeDtypeStruct(loss.shape, F32)],
        scratch_shapes=[pltpu.VMEM((rows, width), F32), pltpu.VMEM((n_dev, rows, width), F32),
                        pltpu.SemaphoreType.DMA((n_dev,)), pltpu.SemaphoreType.DMA((n_dev,))],
    )(*grads, loss, *w, *m, *v)
    return [res[k * n:(k + 1) * n] for k in range(4)], res[4 * n]


def _adamw(name, w, g, m, v):
    rows, cols = w.shape
    tr = _row_tile(rows, sublanes=8)

    def body(w_ref, g_ref, m_ref, v_ref, d_ref, mo_ref, vo_ref):
        d_ref[...], mo_ref[...], vo_ref[...] = _adamw_update(w_ref[...], g_ref[...], m_ref[...], v_ref[...])

    spec = pl.BlockSpec((tr, cols), lambda i: (i, 0))
    shape = jax.ShapeDtypeStruct((rows, cols), F32)
    return pl.pallas_call(body, name=name, grid=(rows // tr,), in_specs=[spec] * 4, out_specs=[spec] * 3,
                          out_shape=[shape] * 3, compiler_params=_params("parallel"))(w, g, m, v)


def kernel(x, ev_w_in, ev_g_cq, ev_w_uq, ev_g_ckv, ev_w_ukv, ev_w_out, od_w_qkv, od_rel_bias, od_w_out, g_mix, g_ffn, w_gate, w_up, w_down, g_final, loss_target, m_ev_w_in, m_ev_g_cq, m_ev_w_uq, m_ev_g_ckv, m_ev_w_ukv, m_ev_w_out, m_od_w_qkv, m_od_rel_bias, m_od_w_out, m_g_mix, m_g_ffn, m_w_gate, m_w_up, m_w_down, m_g_final, v_ev_w_in, v_ev_g_cq, v_ev_w_uq, v_ev_g_ckv, v_ev_w_ukv, v_ev_w_out, v_od_w_qkv, v_od_rel_bias, v_od_w_out, v_g_mix, v_g_ffn, v_w_gate, v_w_up, v_w_down, v_g_final):
    w = dict(ev_w_in=ev_w_in, ev_g_cq=ev_g_cq, ev_w_uq=ev_w_uq, ev_g_ckv=ev_g_ckv, ev_w_ukv=ev_w_ukv, ev_w_out=ev_w_out,
             od_w_qkv=od_w_qkv, od_rel_bias=od_rel_bias, od_w_out=od_w_out, g_mix=g_mix, g_ffn=g_ffn, w_gate=w_gate,
             w_up=w_up, w_down=w_down, g_final=g_final)
    m = dict(ev_w_in=m_ev_w_in, ev_g_cq=m_ev_g_cq, ev_w_uq=m_ev_w_uq, ev_g_ckv=m_ev_g_ckv, ev_w_ukv=m_ev_w_ukv,
             ev_w_out=m_ev_w_out, od_w_qkv=m_od_w_qkv, od_rel_bias=m_od_rel_bias, od_w_out=m_od_w_out, g_mix=m_g_mix,
             g_ffn=m_g_ffn, w_gate=m_w_gate, w_up=m_w_up, w_down=m_w_down, g_final=m_g_final)
    v = dict(ev_w_in=v_ev_w_in, ev_g_cq=v_ev_g_cq, ev_w_uq=v_ev_w_uq, ev_g_ckv=v_ev_g_ckv, ev_w_ukv=v_ev_w_ukv,
             ev_w_out=v_ev_w_out, od_w_qkv=v_od_w_qkv, od_rel_bias=v_od_rel_bias, od_w_out=v_od_w_out, g_mix=v_g_mix,
             g_ffn=v_g_ffn, w_gate=v_w_gate, w_up=v_w_up, w_down=v_w_down, g_final=v_g_final)
    flat2d = lambda a: a.reshape(-1, a.shape[-1])
    for tree in (w, m, v):
        for n in TRANSPOSED:
            tree[n] = jnp.swapaxes(tree[n], 1, 2)

    pos = jnp.stack([2 * lax.axis_index("x") + lax.axis_index("y"), lax.axis_index("c")]).astype(jnp.int32)

    slots = {part: _cast_into_slot("cast_" + part, w[n], layer, pos) for part, n, layer in GRAD_PARTS
             if part in FIRST_WEIGHTS + NEXT_WEIGHTS}
    rest = [(part, n, layer) for part, n, layer in GRAD_PARTS if part not in slots]

    def cast_rest(carry):
        return dict(zip([part for part, _, _ in rest],
                        _cast_many_into_slots("cast_rest", [(w[n], layer) for _, n, layer in rest], pos, carry)))

    ex = _Exchanges(slots, pos, {n: w[n].shape for n in BIG}, cast_rest)

    loss_local, grad_x, small = _local_step(x[0], loss_target[0], {n: w[n] for n in SMALL}, ex)

    grads = ex.finish()
    delta, new_m, new_v = {}, {}, {}
    small_out, loss = _small_step("small_step", [flat2d(small[n]) for n in SMALL], loss_local,
                                  *([flat2d(t[n]) for n in SMALL] for t in (w, m, v)))
    for tree, outs in zip((grads, delta, new_m, new_v), small_out):
        tree.update({n: o.reshape(w[n].shape) for n, o in zip(SMALL, outs)})

    for n in BIG:
        turn = (lambda a: jnp.swapaxes(a, 1, 2)) if n in ADAMW_TRANSPOSED else (lambda a: a)
        shape = turn(w[n]).shape
        outs = _adamw("adamw_" + n, *(flat2d(turn(a)) for a in (w[n], grads[n], m[n], v[n])))
        delta[n], new_m[n], new_v[n] = (turn(o.reshape(shape)) for o in outs)
    for tree in (grads, delta, new_m, new_v):
        for n in TRANSPOSED:
            tree[n] = jnp.swapaxes(tree[n], 1, 2)

    return (loss[0, 0], grad_x[None], *[grads[n] for n in WEIGHTS], *[delta[n] for n in WEIGHTS],
            *[new_m[n] for n in WEIGHTS], *[new_v[n] for n in WEIGHTS])
```

```python
import functools

import jax
import jax.numpy as jnp
import numpy as np
from jax import lax
from jax.experimental import pallas as pl
from jax.experimental.pallas import tpu as pltpu

F32 = jnp.float32
BF16 = jnp.bfloat16

S = 2048
D = 1024
CHUNK = 64
MLA_H, MLA_NOPE, MLA_ROPE, MLA_V = 8, 64, 32, 64
Q_LORA, KV_LORA = 384, 256
ROPE_THETA = 10000.0
SB_H, SB_DIM = 8, 64
C_H, C_DIM = 16, 64
LEFT_CHUNKS = 8
REL_CLIP = 256
D_FF = 2816
EVEN_IN = 2208
RMS_EPS = 1e-6
ADAM_LR, ADAM_B1, ADAM_B2, ADAM_EPS, ADAM_WD, ADAM_STEP = 0.001, 0.9, 0.999, 1e-08, 0.01, 10

N_CHIPS = 4
FF_SHARD = D_FF // N_CHIPS
SCALE_A = (MLA_NOPE + MLA_ROPE) ** -0.5
SCALE_B = SB_DIM ** -0.5
SCALE_C = C_DIM ** -0.5
NEG = -1e30
LOG2_E = 1.4426950408889634

LANES = 128
MXU_W = 256
VMEM_LIMIT_BYTES = 56 * 1024 * 1024
TM = 512
TQ = 1024
TF = 1024
QB = 512
BQ = 256

P_CQ, P_CKV, P_QB, P_KB, P_VB, P_KR = 0, 512, 768, 1280, 1792, 2304
P_IN = 2432
KR_LANE = 64
BAND_W = BQ + LEFT_CHUNKS * CHUNK
BAND_PAD = 512
TOEP_W = 1024


def _params(*sem):
    return pltpu.CompilerParams(dimension_semantics=sem, vmem_limit_bytes=VMEM_LIMIT_BYTES)


MESH = pl.DeviceIdType.MESH
ANY = pl.BlockSpec(memory_space=pl.ANY)


def _position():
    x, y, c = lax.axis_index("x"), lax.axis_index("y"), lax.axis_index("c")
    other_chips = [(1 - x, y), (x, 1 - y), (1 - x, 1 - y)]
    return x, y, c, other_chips


def _half_rows(c, half):
    return pl.ds(pl.multiple_of(c * half, 16), half)


def _remote(ref_src, ref_dst, send, recv, k, device):
    return pltpu.make_async_remote_copy(src_ref=ref_src, dst_ref=ref_dst, send_sem=send.at[k], recv_sem=recv.at[k],
                                        device_id=device, device_id_type=MESH)


class _Carry:
    def __init__(self):
        self.operands, self.aliased, self.fresh = [], [], []
        self.n_sems = 0
        self.starts, self.finishes, self.on_done = [], [], []

    def operand(self, arr, aliased):
        for i, a in enumerate(self.operands):
            if a is arr:
                return i
        self.operands.append(arr)
        self.aliased.append(aliased)
        return len(self.operands) - 1

    def result(self, shape, dtype):
        self.fresh.append(jax.ShapeDtypeStruct(shape, dtype))
        return len(self.fresh) - 1

    def sems(self, k):
        base = self.n_sems
        self.n_sems += k
        return base

    def done(self, results):
        aliased, fresh = results
        for f in self.on_done:
            f(aliased, fresh)


def _carrier_call(body, *, name, grid, in_specs, out_specs, out_shape, args, sem, scratch_shapes=(), carry=None,
                  prefetch=()):
    in_specs, out_specs, out_shape, scratch = list(in_specs), list(out_specs), list(out_shape), list(scratch_shapes)
    n_pre = len(prefetch)

    def call(kernel, in_specs, out_specs, out_shape, scratch, aliases, sem):
        return pl.pallas_call(
            kernel, name=name, out_shape=out_shape, input_output_aliases=aliases, compiler_params=_params(*sem),
            grid_spec=pltpu.PrefetchScalarGridSpec(num_scalar_prefetch=n_pre, grid=grid, in_specs=in_specs,
                                                   out_specs=out_specs, scratch_shapes=scratch))

    if carry is None:
        return list(call(body, in_specs, out_specs, out_shape, scratch, {}, sem)(*prefetch, *args)), None
    ops = carry.operands
    alias_idx = [i for i, a in enumerate(carry.aliased) if a]
    c_shapes = [jax.ShapeDtypeStruct(ops[i].shape, ops[i].dtype) for i in alias_idx] + carry.fresh
    n_in, n_out, n_scr = len(args), len(out_shape), len(scratch)

    def wrapped(*refs):
        pre, refs = refs[:n_pre], refs[n_pre:]
        ins, c_ins = refs[:n_in], refs[n_in:n_in + len(ops)]
        o0 = n_in + len(ops)
        outs, c_outs = refs[o0:o0 + n_out], refs[o0 + n_out:o0 + n_out + len(c_shapes)]
        s0 = o0 + n_out + len(c_shapes)
        scr, send, recv = refs[s0:s0 + n_scr], refs[s0 + n_scr], refs[s0 + n_scr + 1]
        use = list(c_ins)
        for k, i in enumerate(alias_idx):
            use[i] = c_outs[k]
        fresh = c_outs[len(alias_idx):]

        def run(steps):
            for step in steps:
                step(use, fresh, send, recv)

        if not grid:
            run(carry.starts)
            if body is not None:
                body(*pre, *ins, *outs, *scr)
            run(carry.finishes)
            return
        ids = [pl.program_id(a) for a in range(len(grid))]
        first = functools.reduce(jnp.logical_and, [i == 0 for i in ids])
        last = functools.reduce(jnp.logical_and, [i == g - 1 for i, g in zip(ids, grid)])

        @pl.when(first)
        def _():
            run(carry.starts)

        body(*pre, *ins, *outs, *scr)

        @pl.when(last)
        def _():
            run(carry.finishes)

    res = call(wrapped, in_specs + [ANY] * len(ops), out_specs + [ANY] * len(c_shapes), out_shape + c_shapes,
               scratch + [pltpu.SemaphoreType.DMA((carry.n_sems,)), pltpu.SemaphoreType.DMA((carry.n_sems,))],
               {n_pre + n_in + i: n_out + k for k, i in enumerate(alias_idx)},
               ("arbitrary",) * len(grid))(*prefetch, *args, *ops)
    res = list(res)
    c_res = res[n_out:]
    return res[:n_out], ({i: c_res[k] for k, i in enumerate(alias_idx)}, c_res[len(alias_idx):])


_DIMS = {"nn": (((1,), (0,)), ((), ())), "nt": (((1,), (1,)), ((), ())), "tn": (((0,), (0,)), ((), ()))}


def _dot(a, b, kind="nn"):
    return lax.dot_general(a, b, _DIMS[kind], preferred_element_type=F32)


def _iota(shape, dim):
    return lax.broadcasted_iota(jnp.int32, shape, dim)


def _sigmoid(x):
    return 1.0 / (1.0 + jnp.exp(-x))


def _split_dot(x, tri):
    hi = x.astype(BF16)
    lo = (x - hi.astype(F32)).astype(BF16)
    both = _dot(jnp.concatenate([hi, lo], axis=0), tri)
    return both[:x.shape[0]] + both[x.shape[0]:]


def _running_sum(x, tri, reverse):
    n = x.shape[1] // MXU_W
    blocks = [x[:, b * MXU_W:(b + 1) * MXU_W] for b in range(n)]
    out = [None] * n
    carry = None
    for b in (range(n - 1, -1, -1) if reverse else range(n)):
        part = _split_dot(blocks[b], tri)
        out[b] = part if carry is None else part + carry
        total = jnp.sum(blocks[b], axis=-1, keepdims=True)
        carry = total if carry is None else carry + total
    return (jnp.concatenate(out, axis=1) if n > 1 else out[0]), carry


def _mm(name, a, b, *, kind, grid, a_spec, b_spec, o_spec, out_shape, out_dtype, acc_shape, resid=None, r_spec=None,
        carry=None):
    nk = grid[-1]
    has_r = resid is not None
    several = lambda x: list(x) if isinstance(x, (tuple, list)) else [x]
    a_specs, b_specs = several(a_spec), several(b_spec)
    na, nb = len(a_specs), len(b_specs)
    a_args = several(a) if isinstance(a, (tuple, list)) else [a] * na
    b_args = several(b) if isinstance(b, (tuple, list)) else [b] * nb

    def body(*refs):
        r_ref = refs[na + nb] if has_r else None
        o_ref = refs[na + nb + has_r]
        side_by_side = lambda rs: rs[0][...] if len(rs) == 1 else jnp.concatenate([r[...].astype(BF16) for r in rs], axis=1)
        part = _dot(side_by_side(refs[:na]).astype(BF16), side_by_side(refs[na:na + nb]).astype(BF16), kind)

        def finish(total):
            if has_r:
                total = total + r_ref[...].astype(F32)
            o_ref[...] = total.astype(out_dtype)

        if nk == 1:
            finish(part)
        else:
            acc_ref = refs[na + nb + has_r + 1]
            k = pl.program_id(len(grid) - 1)

            @pl.when(k == 0)
            def _():
                acc_ref[...] = part

            @pl.when(k > 0)
            def _():
                acc_ref[...] += part

            @pl.when(k == nk - 1)
            def _():
                finish(acc_ref[...])

    in_specs = a_specs + b_specs + ([r_spec] if has_r else [])
    args = (*a_args, *b_args) + ((resid,) if has_r else ())
    sem = ("parallel",) * (len(grid) - 1) + ("arbitrary",)
    res, copies = _carrier_call(
        body, name=name, grid=grid, in_specs=in_specs, out_specs=[o_spec],
        out_shape=[jax.ShapeDtypeStruct(out_shape, out_dtype)],
        scratch_shapes=[pltpu.VMEM(acc_shape, F32)] if nk > 1 else [], args=args, sem=sem, carry=carry)
    if carry is not None:
        carry.done(copies)
    return res[0]


def _rms_fwd(name, x, g, col_block=0):
    c = g.shape[1]

    def body(x_ref, g_ref, u_ref):
        xv = x_ref[...]
        r = lax.rsqrt(jnp.mean(xv * xv, axis=-1, keepdims=True) + RMS_EPS)
        u_ref[...] = (xv * r * g_ref[...]).astype(BF16)

    return pl.pallas_call(
        body, name=name, grid=(S // TM,),
        in_specs=[pl.BlockSpec((TM, c), lambda i: (i, col_block)), pl.BlockSpec((1, c), lambda i: (0, 0))],
        out_specs=pl.BlockSpec((TM, c), lambda i: (i, 0)),
        out_shape=jax.ShapeDtypeStruct((S, c), BF16),
        compiler_params=_params("parallel"),
    )(x, g)


def _rms_bwd(name, dy, x, g, resid, carry=None):
    def body(dy_ref, x_ref, g_ref, r_ref, dx_ref, dg_ref):
        i = pl.program_id(0)
        xv = x_ref[...]
        r = lax.rsqrt(jnp.mean(xv * xv, axis=-1, keepdims=True) + RMS_EPS)
        xh = xv * r
        dyv = dy_ref[...]
        dxh = dyv * g_ref[...]
        dx_ref[...] = r_ref[...] + r * (dxh - xh * jnp.mean(dxh * xh, axis=-1, keepdims=True))
        part = jnp.sum(dyv * xh, axis=0, keepdims=True)

        @pl.when(i == 0)
        def _():
            dg_ref[...] = part

        @pl.when(i > 0)
        def _():
            dg_ref[...] += part

    row = pl.BlockSpec((TM, D), lambda i: (i, 0))
    vec = pl.BlockSpec((1, D), lambda i: (0, 0))
    res, copies = _carrier_call(
        body, name=name, grid=(S // TM,), in_specs=[row, row, vec, row], out_specs=[row, vec],
        out_shape=[jax.ShapeDtypeStruct((S, D), F32), jax.ShapeDtypeStruct((1, D), F32)],
        args=(dy, x, g, resid), sem=("arbitrary",), carry=carry)
    if carry is not None:
        carry.done(copies)
    return res


def _loss_bwd(name, h, g, tgt):
    def body(h_ref, g_ref, t_ref, loss_ref, dh_ref, dg_ref):
        i = pl.program_id(0)
        xv = h_ref[...]
        gv = g_ref[...]
        r = lax.rsqrt(jnp.mean(xv * xv, axis=-1, keepdims=True) + RMS_EPS)
        xh = xv * r
        diff = xh * gv - t_ref[...]
        part_loss = 0.5 * jnp.sum(jnp.sum(diff * diff, axis=-1, keepdims=True) * (1.0 / D), axis=0, keepdims=True)
        dy = diff * (1.0 / D)
        dxh = dy * gv
        dh_ref[...] = r * (dxh - xh * jnp.mean(dxh * xh, axis=-1, keepdims=True))
        part_g = jnp.sum(dy * xh, axis=0, keepdims=True)

        @pl.when(i == 0)
        def _():
            dg_ref[...] = part_g
            loss_ref[...] = jnp.broadcast_to(part_loss, (1, LANES))

        @pl.when(i > 0)
        def _():
            dg_ref[...] += part_g
            loss_ref[...] += jnp.broadcast_to(part_loss, (1, LANES))

    row = pl.BlockSpec((TM, D), lambda i: (i, 0))
    vec = pl.BlockSpec((1, D), lambda i: (0, 0))
    return pl.pallas_call(
        body, name=name, grid=(S // TM,), in_specs=[row, vec, row],
        out_specs=[pl.BlockSpec((1, LANES), lambda i: (0, 0)), row, vec],
        out_shape=[jax.ShapeDtypeStruct((1, LANES), F32), jax.ShapeDtypeStruct((S, D), F32),
                   jax.ShapeDtypeStruct((1, D), F32)],
        compiler_params=_params("arbitrary"),
    )(h, g, tgt)


def _ffn_fwd(name, h, g, wg, wu, wd, carry=None):
    def body(h_ref, g_ref, wg_ref, wu_ref, wd_ref, o_ref, gate_ref, up_ref, u_scr):
        s = pl.program_id(1)

        @pl.when(s == 0)
        def _():
            xv = h_ref[...]
            r = lax.rsqrt(jnp.mean(xv * xv, axis=-1, keepdims=True) + RMS_EPS)
            u_scr[...] = (xv * r * g_ref[...]).astype(BF16)
            o_ref[...] = xv

        u = u_scr[...]
        gate = _dot(u, wg_ref[...], "nt")
        up = _dot(u, wu_ref[...], "nt")
        act = gate * _sigmoid(gate) * up
        o_ref[...] += _dot(act.astype(BF16), wd_ref[...])
        gate_ref[...] = gate.astype(BF16)
        up_ref[...] = up.astype(BF16)

    row = pl.BlockSpec((TF, D), lambda i, s: (i, 0))
    hid = pl.BlockSpec((None, TF, FF_SHARD), lambda i, s: (s, i, 0))
    return _carrier_call(
        body, name=name, grid=(S // TF, N_CHIPS),
        in_specs=[row, pl.BlockSpec((1, D), lambda i, s: (0, 0))]
        + [pl.BlockSpec((None, FF_SHARD, D), lambda i, s: (s, 0, 0))] * 3,
        out_specs=[row, hid, hid],
        out_shape=[jax.ShapeDtypeStruct((S, D), F32), jax.ShapeDtypeStruct((N_CHIPS, S, FF_SHARD), BF16),
                   jax.ShapeDtypeStruct((N_CHIPS, S, FF_SHARD), BF16)],
        scratch_shapes=[pltpu.VMEM((TF, D), BF16)], args=(h, g, wg, wu, wd), sem=("parallel", "arbitrary"), carry=carry)


def _ffn_bwd(name, dh, h, g, gate, up, wg, wu, wd):
    def body(dh_ref, h_ref, g_ref, gate_ref, up_ref, wg_ref, wu_ref, wd_ref,
             dhin_ref, dg_ref, u_ref, dgate_ref, dup_ref, act_ref, dhb_scr, du_scr):
        i = pl.program_id(0)
        s = pl.program_id(1)

        @pl.when(s == 0)
        def _():
            xv = h_ref[...]
            r = lax.rsqrt(jnp.mean(xv * xv, axis=-1, keepdims=True) + RMS_EPS)
            u_ref[...] = (xv * r * g_ref[...]).astype(BF16)
            dhb_scr[...] = dh_ref[...].astype(BF16)
            du_scr[...] = jnp.zeros_like(du_scr)

        dact = _dot(dhb_scr[...], wd_ref[...], "nt")
        gv = gate_ref[...].astype(F32)
        uv = up_ref[...].astype(F32)
        sig = _sigmoid(gv)
        sil = gv * sig
        dup = dact * sil
        dgate = dact * uv * (sig * (1.0 + gv * (1.0 - sig)))
        dgb = dgate.astype(BF16)
        dub = dup.astype(BF16)
        act_ref[...] = (sil * uv).astype(BF16)
        dgate_ref[...] = dgb
        dup_ref[...] = dub
        du_scr[...] += _dot(dgb, wg_ref[...]) + _dot(dub, wu_ref[...])

        @pl.when(s == N_CHIPS - 1)
        def _():
            xv = h_ref[...]
            r = lax.rsqrt(jnp.mean(xv * xv, axis=-1, keepdims=True) + RMS_EPS)
            xh = xv * r
            du = du_scr[...]
            dxh = du * g_ref[...]
            dhin_ref[...] = dh_ref[...] + r * (dxh - xh * jnp.mean(dxh * xh, axis=-1, keepdims=True))
            part = jnp.sum(du * xh, axis=0, keepdims=True)

            @pl.when(i == 0)
            def _():
                dg_ref[...] = part

            @pl.when(i > 0)
            def _():
                dg_ref[...] += part

    row = pl.BlockSpec((TF, D), lambda i, s: (i, 0), pipeline_mode=pl.Buffered(1))
    vec = pl.BlockSpec((1, D), lambda i, s: (0, 0))
    hid = pl.BlockSpec((None, TF, FF_SHARD), lambda i, s: (s, i, 0))
    hid_shape = jax.ShapeDtypeStruct((N_CHIPS, S, FF_SHARD), BF16)
    return pl.pallas_call(
        body, name=name, grid=(S // TF, N_CHIPS),
        in_specs=[row, row, vec, hid, hid] + [pl.BlockSpec((None, FF_SHARD, D), lambda i, s: (s, 0, 0))] * 3,
        out_specs=[row, vec, row, hid, hid, hid],
        out_shape=[jax.ShapeDtypeStruct((S, D), F32), jax.ShapeDtypeStruct((1, D), F32),
                   jax.ShapeDtypeStruct((S, D), BF16), hid_shape, hid_shape, hid_shape],
        scratch_shapes=[pltpu.VMEM((TF, D), BF16), pltpu.VMEM((TF, D), F32)],
        compiler_params=_params("arbitrary", "arbitrary"),
    )(dh, h, g, gate, up, wg, wu, wd)


def _ffn_wgrads(name, u, dgate, dup, act, dh):
    nk = S // TQ

    def body(u_ref, dh_ref, dgate_ref, dup_ref, act_ref, dg_ref, du_ref, dd_ref, acc_g, acc_u, acc_d):
        k = pl.program_id(1)
        u = u_ref[...]
        parts = (_dot(dgate_ref[...], u, "tn"), _dot(dup_ref[...], u, "tn"),
                 _dot(act_ref[...], dh_ref[...].astype(BF16), "tn"))
        accs = (acc_g, acc_u, acc_d)

        @pl.when(k == 0)
        def _():
            for acc, part in zip(accs, parts):
                acc[...] = part

        @pl.when(k > 0)
        def _():
            for acc, part in zip(accs, parts):
                acc[...] += part

        @pl.when(k == nk - 1)
        def _():
            for out, acc in zip((dg_ref, du_ref, dd_ref), accs):
                out[...] = acc[...].astype(BF16)

    tok = pl.BlockSpec((TQ, D), lambda s, k: (k, 0))
    hid = pl.BlockSpec((None, TQ, FF_SHARD), lambda s, k: (s, k, 0))
    out = pl.BlockSpec((None, FF_SHARD, D), lambda s, k: (s, 0, 0))
    shape = jax.ShapeDtypeStruct((N_CHIPS, FF_SHARD, D), BF16)
    return pl.pallas_call(
        body, name=name, grid=(N_CHIPS, nk), in_specs=[tok, tok, hid, hid, hid], out_specs=[out, out, out],
        out_shape=[shape, shape, shape], scratch_shapes=[pltpu.VMEM((FF_SHARD, D), F32)] * 3,
        compiler_params=_params("parallel", "arbitrary"))(u, dh, dgate, dup, act)


def _rope_tables():
    pos = jnp.arange(S, dtype=F32)
    inv = ROPE_THETA ** (-jnp.arange(0, MLA_ROPE, 2, dtype=F32) / MLA_ROPE)
    ang = pos[:, None] * inv[None, :]
    half = MLA_ROPE // 2
    cos = jnp.cos(ang)
    sin = jnp.sin(ang)
    one = jnp.ones((S, KR_LANE), F32)
    zero = jnp.zeros((S, KR_LANE), F32)
    tail_one = jnp.ones((S, LANES - KR_LANE - MLA_ROPE), F32)
    tail_zero = jnp.zeros((S, LANES - KR_LANE - MLA_ROPE), F32)
    cos_t = jnp.concatenate([one, cos, cos, tail_one], axis=1)
    sin_t = jnp.concatenate([zero, -sin, sin, tail_zero], axis=1)
    assert cos_t.shape == (S, LANES) and half * 2 == MLA_ROPE
    return cos_t, sin_t


def _rope(x, cos_t, sin_t, sign):
    n = x.shape[1] // LANES
    half = MLA_ROPE // 2
    lane = _iota(x.shape, 1) & (LANES - 1)
    first = (lane >= KR_LANE) & (lane < KR_LANE + half)
    swapped = jnp.where(first, pltpu.roll(x, x.shape[1] - half, 1), pltpu.roll(x, half, 1))
    c = jnp.tile(cos_t, (1, n)) if n > 1 else cos_t
    s = jnp.tile(sin_t, (1, n)) if n > 1 else sin_t
    return x * c + swapped * (s * sign)


def _mla_prep_fwd(name, proj, g_cq, g_ckv, w_uq, w_uk, w_uv, cos_t, sin_t):
    nh = MLA_H * LANES

    def body(cq_ref, ckv_ref, kr_ref, gq_ref, gkv_ref, wq_ref, wk_ref, wv_ref, cos_ref, sin_ref,
             qa_ref, ka_ref, va_ref):
        cos_v, sin_v = cos_ref[...], sin_ref[...]
        cq = cq_ref[...]
        r = lax.rsqrt(jnp.mean(cq * cq, axis=-1, keepdims=True) + RMS_EPS)
        cqn = (cq * r * gq_ref[...]).astype(BF16)
        qa_ref[...] = _rope(_dot(cqn, wq_ref[...]), cos_v, sin_v, 1.0).astype(BF16)
        ckv = ckv_ref[...]
        r = lax.rsqrt(jnp.mean(ckv * ckv, axis=-1, keepdims=True) + RMS_EPS)
        ckvn = (ckv * r * gkv_ref[...]).astype(BF16)
        lane = _iota((TM, LANES), 1)
        rot = (lane >= KR_LANE) & (lane < KR_LANE + MLA_ROPE)
        kr = jnp.where(rot, _rope(kr_ref[...], cos_v, sin_v, 1.0), 0.0)
        ka_ref[...] = (_dot(ckvn, wk_ref[...]) + jnp.tile(kr, (1, MLA_H))).astype(BF16)
        va_ref[...] = _dot(ckvn, wv_ref[...]).astype(BF16)

    full = lambda shape: pl.BlockSpec(shape, lambda i: (0, 0))
    return pl.pallas_call(
        body, name=name, grid=(S // TM,),
        in_specs=[pl.BlockSpec((TM, Q_LORA), lambda i: (i, P_CQ // Q_LORA)),
                  pl.BlockSpec((TM, KV_LORA), lambda i: (i, P_CKV // KV_LORA)),
                  pl.BlockSpec((TM, LANES), lambda i: (i, P_KR // LANES)),
                  full((1, Q_LORA)), full((1, KV_LORA)), full((Q_LORA, nh)), full((KV_LORA, nh)),
                  full((KV_LORA, MLA_H * MLA_V)),
                  pl.BlockSpec((TM, LANES), lambda i: (i, 0)), pl.BlockSpec((TM, LANES), lambda i: (i, 0))],
        out_specs=[pl.BlockSpec((TM, nh), lambda i: (i, 0)), pl.BlockSpec((TM, nh), lambda i: (i, 0)),
                   pl.BlockSpec((TM, MLA_H * MLA_V), lambda i: (i, 0))],
        out_shape=[jax.ShapeDtypeStruct((S, nh), BF16), jax.ShapeDtypeStruct((S, nh), BF16),
                   jax.ShapeDtypeStruct((S, MLA_H * MLA_V), BF16)],
        compiler_params=_params("parallel"),
    )(proj, proj, proj, g_cq, g_ckv, w_uq, w_uk, w_uv, cos_t, sin_t)


def _mla_prep_bwd(name, dqa, dka, dva, proj, g_cq, g_ckv, w_uq, w_uk, w_uv, cos_t, sin_t):
    nh = MLA_H * LANES

    def body(dqa_ref, dka_ref, dva_ref, cq_ref, ckv_ref, gq_ref, gkv_ref, wq_ref, wk_ref, wv_ref, cos_ref, sin_ref,
             dcq_ref, dckv_ref, dkr_ref, dwq_ref, dwk_ref, dwv_ref, dgq_ref, dgkv_ref):
        i = pl.program_id(0)
        cos_v, sin_v = cos_ref[...], sin_ref[...]

        def norm_bwd(x, g, dn):
            r = lax.rsqrt(jnp.mean(x * x, axis=-1, keepdims=True) + RMS_EPS)
            xh = x * r
            dxh = dn * g
            dx = r * (dxh - xh * jnp.mean(dxh * xh, axis=-1, keepdims=True))
            return dx, jnp.sum(dn * xh, axis=0, keepdims=True), (xh * g).astype(BF16)

        dq = _rope(dqa_ref[...], cos_v, sin_v, -1.0).astype(BF16)
        dcqn = _dot(dq, wq_ref[...], "nt")
        dcq, dgq, cqn = norm_bwd(cq_ref[...], gq_ref[...], dcqn)
        dcq_ref[...] = dcq.astype(BF16)
        dwq = _dot(cqn, dq, "tn")

        dka = dka_ref[...]
        dkab = dka.astype(BF16)
        dvab = dva_ref[...].astype(BF16)
        dckvn = _dot(dkab, wk_ref[...], "nt") + _dot(dvab, wv_ref[...], "nt")
        dckv, dgkv, ckvn = norm_bwd(ckv_ref[...], gkv_ref[...], dckvn)
        dckv_ref[...] = dckv.astype(BF16)
        dwk = _dot(ckvn, dkab, "tn")
        dwv = _dot(ckvn, dvab, "tn")

        fold = dka[:, 0:LANES]
        for hh in range(1, MLA_H):
            fold = fold + dka[:, hh * LANES:(hh + 1) * LANES]
        lane = _iota((TM, LANES), 1)
        rot = (lane >= KR_LANE) & (lane < KR_LANE + MLA_ROPE)
        dkr = _rope(jnp.where(rot, fold, 0.0), cos_v, sin_v, -1.0)
        dkr_ref[...] = jnp.where(rot, dkr, 0.0).astype(BF16)

        @pl.when(i == 0)
        def _():
            dwq_ref[...] = dwq
            dwk_ref[...] = dwk
            dwv_ref[...] = dwv
            dgq_ref[...] = dgq
            dgkv_ref[...] = dgkv

        @pl.when(i > 0)
        def _():
            dwq_ref[...] += dwq
            dwk_ref[...] += dwk
            dwv_ref[...] += dwv
            dgq_ref[...] += dgq
            dgkv_ref[...] += dgkv

    full = lambda shape: pl.BlockSpec(shape, lambda i: (0, 0))
    rows = lambda c: pl.BlockSpec((TM, c), lambda i: (i, 0))
    nv = MLA_H * MLA_V
    return pl.pallas_call(
        body, name=name, grid=(S // TM,),
        in_specs=[rows(nh), rows(nh), rows(nv),
                  pl.BlockSpec((TM, Q_LORA), lambda i: (i, P_CQ // Q_LORA)),
                  pl.BlockSpec((TM, KV_LORA), lambda i: (i, P_CKV // KV_LORA)),
                  full((1, Q_LORA)), full((1, KV_LORA)), full((Q_LORA, nh)), full((KV_LORA, nh)), full((KV_LORA, nv)),
                  rows(LANES), rows(LANES)],
        out_specs=[rows(Q_LORA), rows(KV_LORA), rows(LANES), full((Q_LORA, nh)), full((KV_LORA, nh)),
                   full((KV_LORA, nv)), full((1, Q_LORA)), full((1, KV_LORA))],
        out_shape=[jax.ShapeDtypeStruct((S, Q_LORA), BF16), jax.ShapeDtypeStruct((S, KV_LORA), BF16),
                   jax.ShapeDtypeStruct((S, LANES), BF16), jax.ShapeDtypeStruct((Q_LORA, nh), F32),
                   jax.ShapeDtypeStruct((KV_LORA, nh), F32), jax.ShapeDtypeStruct((KV_LORA, nv), F32),
                   jax.ShapeDtypeStruct((1, Q_LORA), F32), jax.ShapeDtypeStruct((1, KV_LORA), F32)],
        compiler_params=_params("arbitrary"),
    )(dqa, dka, dva, proj, proj, g_cq, g_ckv, w_uq, w_uk, w_uv, cos_t, sin_t)


def _head_masks(dtype):
    lane = _iota((1, LANES), 1)
    return (lane < 64).astype(dtype), (lane >= 64).astype(dtype)


def _mla_fwd(name, qa, ka, va, carry=None):
    def body(q_ref, k_ref, v_ref, o_ref, lse_ref):
        m0b, m1b = _head_masks(BF16)
        lane = _iota((QB, LANES), 1)
        left = lane < 64

        def qblock(i, _):
            r0 = pl.multiple_of(i * QB, QB)
            qs = [q_ref[pl.ds(r0, QB), hh * LANES:(hh + 1) * LANES] for hh in range(2)]
            rowc = lax.shift_right_logical(r0 + _iota((QB, QB), 0), 6)

            def kv(kb, carry):
                ms, ls, acc = carry
                c0 = pl.multiple_of(kb * QB, QB)
                v = v_ref[pl.ds(c0, QB), :]
                ok = lax.shift_right_logical(c0 + _iota((QB, QB), 1), 6) <= rowc
                new_m, new_l, alphas = [], [], []
                pv = None
                for hh in range(2):
                    k = k_ref[pl.ds(c0, QB), hh * LANES:(hh + 1) * LANES]
                    s = jnp.where(ok, _dot(qs[hh], k, "nt") * (SCALE_A * LOG2_E), NEG)
                    mn = jnp.maximum(ms[hh], jnp.max(s, axis=-1, keepdims=True))
                    p = jnp.exp2(s - mn)
                    a = jnp.exp2(ms[hh] - mn)
                    new_m.append(mn)
                    new_l.append(a * ls[hh] + jnp.sum(p, axis=-1, keepdims=True))
                    alphas.append(a)
                    part = _dot(p.astype(BF16), v * (m0b if hh == 0 else m1b))
                    pv = part if pv is None else pv + part
                acc = acc * jnp.where(left, alphas[0], alphas[1]) + pv
                return tuple(new_m), tuple(new_l), acc

            init = ((jnp.full((QB, 1), NEG, F32),) * 2, (jnp.zeros((QB, 1), F32),) * 2, jnp.zeros((QB, LANES), F32))
            ms, ls, acc = lax.fori_loop(0, i + 1, kv, init)
            o_ref[pl.ds(r0, QB), :] = acc * jnp.where(left, 1.0 / ls[0], 1.0 / ls[1])
            lse_ref[pl.ds(r0, QB), :] = jnp.where(left, ms[0] + jnp.log(ls[0]) * LOG2_E, ms[1] + jnp.log(ls[1]) * LOG2_E)
            return 0

        lax.fori_loop(0, S // QB, qblock, 0)

    pair = lambda w: pl.BlockSpec((S, w), lambda p: (0, p))
    return _carrier_call(
        body, name=name, grid=(MLA_H // 2,), in_specs=[pair(2 * LANES), pair(2 * LANES), pair(LANES)],
        out_specs=[pair(LANES), pair(LANES)],
        out_shape=[jax.ShapeDtypeStruct((S, MLA_H * MLA_V), F32), jax.ShapeDtypeStruct((S, MLA_H * MLA_V), F32)],
        args=(qa, ka, va), sem=("parallel",), carry=carry)


def _mla_bwd(name, qa, ka, va, o, lse, do, do_block0, carry=None):
    def body(q_ref, k_ref, v_ref, o_ref, lse_ref, do_ref, dq_ref, dk_ref, dv_ref):
        m0f, m1f = _head_masks(F32)
        m0b, m1b = _head_masks(BF16)
        dk_ref[...] = jnp.zeros_like(dk_ref)
        dv_ref[...] = jnp.zeros_like(dv_ref)

        def qblock(i, _):
            r0 = pl.multiple_of(i * QB, QB)
            rows = pl.ds(r0, QB)
            do_f = do_ref[rows, :]
            prod = do_f * o_ref[rows, :]
            deltas = [jnp.sum(prod * m0f, axis=-1, keepdims=True), jnp.sum(prod * m1f, axis=-1, keepdims=True)]
            lse_v = lse_ref[rows, :]
            lses = [lse_v[:, 0:1], lse_v[:, 64:65]]
            dob = do_f.astype(BF16)
            dos = [dob * m0b, dob * m1b]
            qs = [q_ref[rows, hh * LANES:(hh + 1) * LANES] for hh in range(2)]
            rowc = lax.shift_right_logical(r0 + _iota((QB, QB), 0), 6)

            def kv(kb, dqs):
                c0 = pl.multiple_of(kb * QB, QB)
                cols = pl.ds(c0, QB)
                v = v_ref[cols, :]
                ok = lax.shift_right_logical(c0 + _iota((QB, QB), 1), 6) <= rowc
                out = []
                dv = None
                for hh in range(2):
                    k = k_ref[cols, hh * LANES:(hh + 1) * LANES]
                    s = _dot(qs[hh], k, "nt") * (SCALE_A * LOG2_E)
                    p = jnp.where(ok, jnp.exp2(s - lses[hh]), 0.0)
                    dp = _dot(dos[hh], v, "nt")
                    ds = (p * (dp - deltas[hh]) * SCALE_A).astype(BF16)
                    out.append(dqs[hh] + _dot(ds, k))
                    dk_ref[cols, hh * LANES:(hh + 1) * LANES] += _dot(ds, qs[hh], "tn")
                    part = _dot(p.astype(BF16), dos[hh], "tn")
                    dv = part if dv is None else dv + part
                dv_ref[cols, :] += dv
                return tuple(out)

            dqs = lax.fori_loop(0, i + 1, kv, (jnp.zeros((QB, LANES), F32),) * 2)
            for hh in range(2):
                dq_ref[rows, hh * LANES:(hh + 1) * LANES] = dqs[hh]
            return 0

        lax.fori_loop(0, S // QB, qblock, 0)

    pair = lambda w: pl.BlockSpec((S, w), lambda p: (0, p))
    return _carrier_call(
        body, name=name, grid=(MLA_H // 2,),
        in_specs=[pair(2 * LANES), pair(2 * LANES), pair(LANES), pair(LANES), pair(LANES),
                  pl.BlockSpec((S, LANES), lambda p: (0, do_block0 + p))],
        out_specs=[pair(2 * LANES), pair(2 * LANES), pair(LANES)],
        out_shape=[jax.ShapeDtypeStruct((S, MLA_H * LANES), F32), jax.ShapeDtypeStruct((S, MLA_H * LANES), F32),
                   jax.ShapeDtypeStruct((S, MLA_H * MLA_V), F32)],
        args=(qa, ka, va, o, lse, do), sem=("parallel",), carry=carry)


def _sb_weights(q_h, k, c, before, tri_suffix):
    z = _dot(q_h, k, "nt") * (SCALE_B * LOG2_E)
    sp = jnp.maximum(z, 0.0) + jnp.log(1.0 + jnp.exp2(-jnp.abs(z))) * LOG2_E
    log_keep = jnp.where(before, -sp, 0.0)
    to_the_right, total = _running_sum(log_keep, tri_suffix, True)
    w = jnp.where(before, jnp.exp2(z - sp + to_the_right + c), 0.0)
    return w, jnp.exp2(z - sp), total


def _sb_fwd(name, proj, carry=None):
    def body(q_ref, k_ref, v_ref, o_ref):
        m0b, m1b = _head_masks(BF16)
        tri_suffix = (_iota((MXU_W, MXU_W), 0) > _iota((MXU_W, MXU_W), 1)).astype(BF16)

        def qblock(i, _):
            r0 = pl.multiple_of(i * QB, QB)
            q = q_ref[pl.ds(r0, QB), :].astype(BF16)
            qs = [q * m0b, q * m1b]
            rowg = r0 + _iota((QB, QB), 0)

            def kv(step, carry):
                cs, acc = carry
                c0 = pl.multiple_of((i - step) * QB, QB)
                k = k_ref[pl.ds(c0, QB), :].astype(BF16)
                v = v_ref[pl.ds(c0, QB), :].astype(BF16)
                before = (c0 + _iota((QB, QB), 1)) < rowg
                new_c = []
                for hh in range(2):
                    w, _, tot = _sb_weights(qs[hh], k, cs[hh], before, tri_suffix)
                    new_c.append(cs[hh] + tot)
                    acc = acc + _dot(w.astype(BF16), v * (m0b if hh == 0 else m1b))
                return tuple(new_c), acc

            init = ((jnp.zeros((QB, 1), F32),) * 2, jnp.zeros((QB, LANES), F32))
            _, acc = lax.fori_loop(0, i + 1, kv, init)
            o_ref[pl.ds(r0, QB), :] = acc.astype(BF16)
            return 0

        lax.fori_loop(0, S // QB, qblock, 0)

    col = lambda base: pl.BlockSpec((S, LANES), lambda p: (0, base // LANES + p))
    return _carrier_call(
        body, name=name, grid=(SB_H // 2,), in_specs=[col(P_QB), col(P_KB), col(P_VB)],
        out_specs=[pl.BlockSpec((S, LANES), lambda p: (0, p))],
        out_shape=[jax.ShapeDtypeStruct((S, SB_H * SB_DIM), BF16)],
        args=(proj, proj, proj), sem=("parallel",), carry=carry)


def _sb_bwd(name, proj, do, do_block0, carry=None):
    nb = S // QB

    def body(q_ref, k_ref, v_ref, do_ref, dq_ref, dk_ref, dv_ref, sig_scr, dl_scr, dk_acc, dv_acc):
        m0b, m1b = _head_masks(BF16)
        tri_suffix = (_iota((MXU_W, MXU_W), 0) > _iota((MXU_W, MXU_W), 1)).astype(BF16)
        tri_prefix = (_iota((MXU_W, MXU_W), 0) < _iota((MXU_W, MXU_W), 1)).astype(BF16)
        dk_acc[...] = jnp.zeros_like(dk_acc)
        dv_acc[...] = jnp.zeros_like(dv_acc)

        def qblock(i, _):
            r0 = pl.multiple_of(i * QB, QB)
            rows = pl.ds(r0, QB)
            q = q_ref[rows, :].astype(BF16)
            qs = [q * m0b, q * m1b]
            dob = do_ref[rows, :].astype(BF16)
            dos = [dob * m0b, dob * m1b]
            rowg = r0 + _iota((QB, QB), 0)

            def sweep_left(step, cs):
                kb = i - step
                c0 = pl.multiple_of(kb * QB, QB)
                cols = pl.ds(c0, QB)
                k = k_ref[cols, :].astype(BF16)
                v = v_ref[cols, :].astype(BF16)
                before = (c0 + _iota((QB, QB), 1)) < rowg
                new_c = []
                dv = None
                for hh in range(2):
                    w, sig, tot = _sb_weights(qs[hh], k, cs[hh], before, tri_suffix)
                    new_c.append(cs[hh] + tot)
                    sig_scr[hh, kb] = sig
                    dl_scr[hh, kb] = _dot(dos[hh], v, "nt") * w
                    part = _dot(w.astype(BF16), dos[hh], "tn")
                    dv = part if dv is None else dv + part
                dv_acc[cols, :] += dv
                return tuple(new_c)

            lax.fori_loop(0, i + 1, sweep_left, (jnp.zeros((QB, 1), F32),) * 2)

            def sweep_right(kb, carry):
                ps, dq = carry
                c0 = pl.multiple_of(kb * QB, QB)
                cols = pl.ds(c0, QB)
                k = k_ref[cols, :].astype(BF16)
                before = (c0 + _iota((QB, QB), 1)) < rowg
                new_p = []
                dk = None
                for hh in range(2):
                    dl = dl_scr[hh, kb]
                    sig = sig_scr[hh, kb]
                    to_the_left, total = _running_sum(dl, tri_prefix, False)
                    earlier = to_the_left + ps[hh]
                    new_p.append(ps[hh] + total)
                    dz = (jnp.where(before, dl * (1.0 - sig) - earlier * sig, 0.0) * SCALE_B).astype(BF16)
                    dq = dq + _dot(dz, k * (m0b if hh == 0 else m1b))
                    part = _dot(dz, qs[hh], "tn")
                    dk = part if dk is None else dk + part
                dk_acc[cols, :] += dk
                return tuple(new_p), dq

            init = ((jnp.zeros((QB, 1), F32),) * 2, jnp.zeros((QB, LANES), F32))
            _, dq = lax.fori_loop(0, i + 1, sweep_right, init)
            dq_ref[rows, :] = dq.astype(BF16)
            return 0

        lax.fori_loop(0, nb, qblock, 0)
        dk_ref[...] = dk_acc[...].astype(BF16)
        dv_ref[...] = dv_acc[...].astype(BF16)

    col = lambda base: pl.BlockSpec((S, LANES), lambda p: (0, base // LANES + p))
    out = pl.BlockSpec((S, LANES), lambda p: (0, p))
    shape = jax.ShapeDtypeStruct((S, SB_H * SB_DIM), BF16)
    return _carrier_call(
        body, name=name, grid=(SB_H // 2,),
        in_specs=[col(P_QB), col(P_KB), col(P_VB), pl.BlockSpec((S, LANES), lambda p: (0, do_block0 + p))],
        out_specs=[out, out, out], out_shape=[shape, shape, shape],
        scratch_shapes=[pltpu.VMEM((2, nb, QB, QB), F32), pltpu.VMEM((2, nb, QB, QB), F32),
                        pltpu.VMEM((S, LANES), F32), pltpu.VMEM((S, LANES), F32)],
        args=(proj, proj, proj, do), sem=("parallel",), carry=carry)


def _band_row_index():
    j = np.arange(TOEP_W)
    rel = np.clip(LEFT_CHUNKS * CHUNK - j, -REL_CLIP, REL_CLIP) + REL_CLIP
    rel[BAND_W:] = 2 * REL_CLIP
    return rel.astype(np.int32)


def _band_tiles(r0_ref, q_ref, kpad, vpad, m, m0b, m1b, static_ok, bias):
    r0 = pl.multiple_of(m * BQ, BQ)
    q = q_ref[0, pl.ds(r0, BQ), :]
    kw = kpad[pl.ds(r0, BAND_W), :]
    vw = vpad[pl.ds(r0, BAND_W), :]
    ok = static_ok & ((r0 - BAND_PAD + _iota((BQ, BAND_W), 1)) >= 0)
    qs = [q * m0b, q * m1b]
    ps = []
    for hh in range(2):
        s = jnp.where(ok, _dot(qs[hh], kw, "nt") * (SCALE_C * LOG2_E) + bias[hh], NEG)
        e = jnp.exp2(s - jnp.max(s, axis=-1, keepdims=True))
        ps.append(e * (1.0 / jnp.sum(e, axis=-1, keepdims=True)))
    return r0, qs, kw, vw, ps


def _band_setup(qkv_ref, r0_ref, kpad, vpad):
    kpad[0:BAND_PAD, :] = jnp.zeros((BAND_PAD, LANES), BF16)
    vpad[0:BAND_PAD, :] = jnp.zeros((BAND_PAD, LANES), BF16)
    kpad[BAND_PAD:, :] = qkv_ref[1]
    vpad[BAND_PAD:, :] = qkv_ref[2]
    jc = lax.shift_right_logical(_iota((BQ, BAND_W), 1), 6)
    rc = lax.shift_right_logical(_iota((BQ, BAND_W), 0), 6)
    static_ok = (jc >= rc) & (jc <= rc + LEFT_CHUNKS)
    bias = []
    for hh in range(2):
        row = jnp.broadcast_to(r0_ref[hh:hh + 1, :] * LOG2_E, (BQ, TOEP_W))
        bias.append(pltpu.roll(row, 0, 1, stride=1, stride_axis=0)[:, :BAND_W])
    return static_ok, bias


def _band_fwd(name, qkv, r0, carry=None):
    def body(qkv_ref, r0_ref, o_ref, kpad, vpad):
        m0b, m1b = _head_masks(BF16)
        static_ok, bias = _band_setup(qkv_ref, r0_ref, kpad, vpad)

        def qblock(m, _):
            r0_, _, _, vw, ps = _band_tiles(r0_ref, qkv_ref, kpad, vpad, m, m0b, m1b, static_ok, bias)
            o = _dot(ps[0].astype(BF16), vw * m0b) + _dot(ps[1].astype(BF16), vw * m1b)
            o_ref[pl.ds(r0_, BQ), :] = o.astype(BF16)
            return 0

        lax.fori_loop(0, S // BQ, qblock, 0)

    return _carrier_call(
        body, name=name, grid=(C_H // 2,),
        in_specs=[pl.BlockSpec((3, S, LANES), lambda p: (0, 0, p)), pl.BlockSpec((None, 2, TOEP_W), lambda p: (p, 0, 0))],
        out_specs=[pl.BlockSpec((S, LANES), lambda p: (0, p))],
        out_shape=[jax.ShapeDtypeStruct((S, C_H * C_DIM), BF16)],
        scratch_shapes=[pltpu.VMEM((S + BAND_PAD, LANES), BF16), pltpu.VMEM((S + BAND_PAD, LANES), BF16)],
        args=(qkv, r0), sem=("parallel",), carry=carry)


def _band_bwd(name, qkv, r0, do, carry=None):
    def body(qkv_ref, r0_ref, do_ref, dqkv_ref, dr0_ref, kpad, vpad, dkpad, dvpad, db_acc):
        m0b, m1b = _head_masks(BF16)
        static_ok, bias = _band_setup(qkv_ref, r0_ref, kpad, vpad)
        dkpad[...] = jnp.zeros_like(dkpad)
        dvpad[...] = jnp.zeros_like(dvpad)
        db_acc[...] = jnp.zeros_like(db_acc)

        def qblock(m, _):
            r0_, qs, kw, vw, ps = _band_tiles(r0_ref, qkv_ref, kpad, vpad, m, m0b, m1b, static_ok, bias)
            dob = do_ref[pl.ds(r0_, BQ), :].astype(BF16)
            dos = [dob * m0b, dob * m1b]
            dq = None
            dk = None
            dv = None
            for hh in range(2):
                p = ps[hh]
                dp = _dot(dos[hh], vw, "nt")
                ds = p * (dp - jnp.sum(dp * p, axis=-1, keepdims=True))
                db_acc[hh, :, 0:BAND_W] += ds
                dsb = (ds * SCALE_C).astype(BF16)
                t = _dot(dsb, kw * (m0b if hh == 0 else m1b))
                dq = t if dq is None else dq + t
                t = _dot(dsb, qs[hh], "tn")
                dk = t if dk is None else dk + t
                t = _dot(p.astype(BF16), dos[hh], "tn")
                dv = t if dv is None else dv + t
            dqkv_ref[0, pl.ds(r0_, BQ), :] = dq.astype(BF16)
            dkpad[pl.ds(r0_, BAND_W), :] += dk
            dvpad[pl.ds(r0_, BAND_W), :] += dv
            return 0

        lax.fori_loop(0, S // BQ, qblock, 0)
        dqkv_ref[1] = dkpad[BAND_PAD:, :].astype(BF16)
        dqkv_ref[2] = dvpad[BAND_PAD:, :].astype(BF16)
        sub = _iota((8, TOEP_W), 0)
        for hh in range(2):
            folded = db_acc[hh, 0:8, :]
            for a in range(1, BQ // 8):
                folded = folded + pltpu.roll(db_acc[hh, 8 * a:8 * a + 8, :], TOEP_W - 8 * a, 1)
            for bit in range(3):
                moved = pltpu.roll(folded, TOEP_W - (1 << bit), 1)
                folded = jnp.where((sub & (1 << bit)) != 0, moved, folded)
            dr0_ref[hh:hh + 1, :] = jnp.sum(folded, axis=0, keepdims=True)

    return _carrier_call(
        body, name=name, grid=(C_H // 2,),
        in_specs=[pl.BlockSpec((3, S, LANES), lambda p: (0, 0, p)), pl.BlockSpec((None, 2, TOEP_W), lambda p: (p, 0, 0)),
                  pl.BlockSpec((S, LANES), lambda p: (0, p))],
        out_specs=[pl.BlockSpec((3, S, LANES), lambda p: (0, 0, p)), pl.BlockSpec((None, 2, TOEP_W), lambda p: (p, 0, 0))],
        out_shape=[jax.ShapeDtypeStruct((3, S, C_H * C_DIM), BF16), jax.ShapeDtypeStruct((C_H // 2, 2, TOEP_W), F32)],
        scratch_shapes=[pltpu.VMEM((S + BAND_PAD, LANES), BF16), pltpu.VMEM((S + BAND_PAD, LANES), BF16),
                        pltpu.VMEM((S + BAND_PAD, LANES), F32), pltpu.VMEM((S + BAND_PAD, LANES), F32),
                        pltpu.VMEM((2, BQ, TOEP_W), F32)],
        args=(qkv, r0, do), sem=("parallel",), carry=carry)


def _bias_table_grad(name, dr0):
    w_out = 5 * LANES

    def body(d_ref, o_ref):
        j = _iota((TOEP_W, w_out), 0)
        rel = jnp.clip(LEFT_CHUNKS * CHUNK - j, -REL_CLIP, REL_CLIP) + REL_CLIP
        rel = jnp.where(j >= BAND_W, 2 * REL_CLIP, rel)
        onehot = (rel == _iota((TOEP_W, w_out), 1)).astype(BF16)
        d = d_ref[...]
        hi = d.astype(BF16)
        mid = (d - hi.astype(F32))
        mid_b = mid.astype(BF16)
        lo = (mid - mid_b.astype(F32)).astype(BF16)
        o_ref[...] = _dot(hi, onehot) + _dot(mid_b, onehot) + _dot(lo, onehot)

    return pl.pallas_call(
        body, name=name, out_shape=jax.ShapeDtypeStruct((C_H, w_out), F32),
        in_specs=[pl.BlockSpec((C_H, TOEP_W), lambda: (0, 0))], out_specs=pl.BlockSpec((C_H, w_out), lambda: (0, 0)),
        grid=(),
    )(dr0)


def _carry_gather(cy, slots, names, ici, d2d):
    idx = [cy.operand(slots[n], True) for n in names]
    n = len(names)
    base_i = cy.sems(3 * n) if ici else 0
    base_d = cy.sems(3 * n) if d2d else 0

    def piece(refs, t, slot, cc):
        return refs[idx[t]].at[slot, _half_rows(cc, slots[names[t]].shape[1] // 2), :]

    def over_ici(refs, send, recv, arriving):
        x, y, c, chips = _position()
        out = []
        for t in range(n):
            for j in range(3):
                r = piece(refs, t, 2 * chips[j][0] + chips[j][1] if arriving else 2 * x + y, c)
                out.append(_remote(r, r, send, recv, base_i + 3 * t + j, (*chips[j], c)))
        return out

    def over_d2d(refs, send, recv, arriving):
        x, y, c, chips = _position()
        out = []
        for t in range(n):
            for j in range(3):
                r = piece(refs, t, 2 * chips[j][0] + chips[j][1], 1 - c if arriving else c)
                out.append(_remote(r, r, send, recv, base_d + 3 * t + j, (x, y, 1 - c)))
        return out

    def start_ici(refs, fresh, send, recv):
        for cp in over_ici(refs, send, recv, False):
            cp.start()

    def wait_ici(refs, fresh, send, recv):
        for cp in over_ici(refs, send, recv, True):
            cp.wait_recv()
        for cp in over_ici(refs, send, recv, False):
            cp.wait_send()

    def start_d2d(refs, fresh, send, recv):
        for cp in over_d2d(refs, send, recv, False):
            cp.start()

    def wait_d2d(refs, fresh, send, recv):
        for cp in over_d2d(refs, send, recv, True):
            cp.wait_recv()
        for cp in over_d2d(refs, send, recv, False):
            cp.wait_send()

    def wait_ici_and_forward(refs, fresh, send, recv):
        forwards = over_d2d(refs, send, recv, False)
        for k, cp in enumerate(over_ici(refs, send, recv, True)):
            cp.wait_recv()
            forwards[k].start()
        for cp in over_ici(refs, send, recv, False):
            cp.wait_send()

    if ici and d2d:
        cy.starts.append(start_ici)
        cy.finishes += [wait_ici_and_forward, wait_d2d]
    elif ici:
        cy.starts.append(start_ici)
        cy.finishes.append(wait_ici)
    else:
        cy.starts.append(start_d2d)
        cy.finishes.append(wait_d2d)

    def done(aliased, fresh):
        for t, name in enumerate(names):
            slots[name] = aliased[idx[t]]

    cy.on_done.append(done)


def _carry_chip_exchange(cy, sums, got, names):
    idx = [cy.operand(sums[n], False) for n in names]
    out = [cy.result((3,) + sums[n].shape[1:], BF16) for n in names]
    base = cy.sems(3 * len(names))

    def copies(refs, fresh, send, recv):
        x, y, c, chips = _position()
        return [_remote(refs[idx[t]].at[2 * chips[j][0] + chips[j][1]], fresh[out[t]].at[j], send, recv, base + 3 * t + j,
                        (*chips[j], c)) for t in range(len(names)) for j in range(3)]

    def start(refs, fresh, send, recv):
        for cp in copies(refs, fresh, send, recv):
            cp.start()

    def wait(refs, fresh, send, recv):
        for cp in copies(refs, fresh, send, recv):
            cp.wait()

    cy.starts.append(start)
    cy.finishes.append(wait)

    def done(aliased, fresh):
        for t, name in enumerate(names):
            got[name] = fresh[out[t]]

    cy.on_done.append(done)


def _run_carry(name, cy):
    _, res = _carrier_call(None, name=name, grid=(), in_specs=[], out_specs=[], out_shape=[], args=(), sem=(), carry=cy)
    cy.done(res)


FIRST_WEIGHTS = ("ev_w_in",)
NEXT_WEIGHTS = ("ev_w_uq", "ev_w_ukv")
WEIGHTS_A = ("ev_w_out", "w_gate0", "w_up0")
WEIGHTS_B = ("w_down0", "od_w_qkv", "od_w_out")
WEIGHTS_C = ("w_gate1",)
WEIGHTS_D = ("w_up1", "w_down1")
GRAD_GROUPS = {"ffn1": ("w_gate1", "w_up1", "w_down1"), "od": ("od_w_qkv", "od_w_out"),
               "ffn0": ("w_gate0", "w_up0", "w_down0"), "ev_out": ("ev_w_out",),
               "ev": ("ev_w_in", "ev_w_uq", "ev_w_ukv")}


def _carry_pair_exchange(cy, parts, theirs, names):
    idx = [cy.operand(parts[n], False) for n in names]
    out = [cy.result((N_CHIPS, parts[n].shape[1] // 2, parts[n].shape[2]), BF16) for n in names]
    base = cy.sems(len(names))

    def copies(refs, fresh, send, recv):
        x, y, c, _ = _position()
        return [_remote(refs[idx[t]].at[:, _half_rows(1 - c, parts[n].shape[1] // 2), :], fresh[out[t]], send, recv,
                        base + t, (x, y, 1 - c)) for t, n in enumerate(names)]

    cy.starts.append(lambda refs, fresh, send, recv: [cp.start() for cp in copies(refs, fresh, send, recv)])
    cy.finishes.append(lambda refs, fresh, send, recv: [cp.wait() for cp in copies(refs, fresh, send, recv)])

    def done(aliased, fresh):
        for t, name in enumerate(names):
            theirs[name] = fresh[out[t]]

    cy.on_done.append(done)


def _carry_sibling_exchange(cy, fulls, pieces):
    idx = [cy.operand(fulls[p], True) for p, _ in pieces]
    base = cy.sems(len(pieces))

    def copies(refs, send, recv, arriving):
        x, y, c, _ = _position()
        out = []
        for t, (p, layer) in enumerate(pieces):
            r = refs[idx[t]].at[layer, _half_rows(1 - c if arriving else c, fulls[p].shape[1] // 2), :]
            out.append(_remote(r, r, send, recv, base + t, (x, y, 1 - c)))
        return out

    def start(refs, fresh, send, recv):
        for cp in copies(refs, send, recv, False):
            cp.start()

    def wait(refs, fresh, send, recv):
        for cp in copies(refs, send, recv, True):
            cp.wait_recv()
        for cp in copies(refs, send, recv, False):
            cp.wait_send()

    cy.starts.append(start)
    cy.finishes.append(wait)

    def done(aliased, fresh):
        for t, (p, _) in enumerate(pieces):
            fulls[p] = aliased[idx[t]]

    cy.on_done.append(done)


RIDES = {
    "cast_rest": (("gather", FIRST_WEIGHTS),),
    "proj_in": (("gather", NEXT_WEIGHTS),),
    "mla_attn": (("gather_ici", WEIGHTS_A),),
    "sb_attn": (("gather_d2d", WEIGHTS_A), ("gather_ici", WEIGHTS_B)),
    "ev_out": (("gather_d2d", WEIGHTS_B),),
    "ffn0": (("gather_ici", WEIGHTS_C),),
    "qkv": (("gather_d2d", WEIGHTS_C),),
    "band_attn": (("gather_ici", WEIGHTS_D),),
    "od_out": (("gather_d2d", WEIGHTS_D),),
    "od_out_bwd_w": (("pair", "ffn1"),),
    "band_attn_bwd": (("chips", "ffn1"),),
    "rms_mix1_bwd": (("pair", "od"),),
    "ev_out_bwd_w": (("pair", "ffn0"),),
    "mla_attn_bwd": (("chips", "od"), ("sibling", "ffn1"), ("pair", "ev_out")),
    "sb_attn_bwd": (("chips", "ffn0"), ("sibling", "od"), ("chips", "ev_out")),
    "proj_in_bwd_w": (("sibling", "ffn0"), ("sibling", "ev_out")),
    "grads_pair_ev": (("pair", "ev"),),
    "proj_in_bwd_x": (("chips", "ev"),),
    "grads_sibling_ev": (("sibling", "ev"),),
}


class _Exchanges:
    def __init__(self, slots, pos, shapes, cast_rest):
        self.slots, self.pos, self.shapes, self.cast_rest = dict(slots), pos, shapes, cast_rest
        self.parts, self.theirs, self.sums, self.got, self.fulls = {}, {}, {}, {}, {}

    def begin(self):
        self.slots.update(self.cast_rest(self.carry("cast_rest")))

    def weights(self, *names):
        return [self.slots[n] for n in names]

    def _pair_sums(self, group):
        names = GRAD_GROUPS[group]
        self.sums.update(zip(names, _pair_sums("pair_sums_" + group, [self.parts[n] for n in names],
                                               [self.theirs[n] for n in names], self.pos)))

    def _chip_sums(self, group):
        names = GRAD_GROUPS[group]
        items = [(self.sums[n], self.got[n], PART_OF[n][1], self.shapes[PART_OF[n][0]], self.fulls.get(PART_OF[n][0]))
                 for n in names]
        self.fulls.update(zip([PART_OF[n][0] for n in names], _chip_sums("chip_sums_" + group, items, self.pos)))

    def carry(self, stage):
        cy = _Carry()
        for step, what in RIDES[stage]:
            if step == "gather":
                _carry_gather(cy, self.slots, what, True, True)
            elif step == "gather_ici":
                _carry_gather(cy, self.slots, what, True, False)
            elif step == "gather_d2d":
                _carry_gather(cy, self.slots, what, False, True)
            elif step == "pair":
                _carry_pair_exchange(cy, self.parts, self.theirs, GRAD_GROUPS[what])
            elif step == "chips":
                self._pair_sums(what)
                _carry_chip_exchange(cy, self.sums, self.got, GRAD_GROUPS[what])
            elif step == "sibling":
                self._chip_sums(what)
                _carry_sibling_exchange(cy, self.fulls, [PART_OF[n] for n in GRAD_GROUPS[what]])
        return cy

    def grads(self, group, parts):
        self.parts.update(parts)
        if group == "ev":
            _run_carry("grads_pair_ev", self.carry("grads_pair_ev"))

    def finish(self):
        _run_carry("grads_sibling_ev", self.carry("grads_sibling_ev"))
        return {n: self.fulls[n] for n in BIG}


class _NoExchanges:
    def __init__(self, slots):
        self.slots, self.parts = dict(slots), {}

    def begin(self):
        pass

    def weights(self, *names):
        return [self.slots[n] for n in names]

    def carry(self, stage):
        return None

    def grads(self, group, parts):
        self.parts.update(parts)


def _w_in_pieces():
    segments = ((0, Q_LORA, P_CQ), (Q_LORA, Q_LORA + KV_LORA, P_CKV),
                (Q_LORA + KV_LORA, Q_LORA + KV_LORA + MLA_ROPE, P_KR + KR_LANE),
                (Q_LORA + KV_LORA + MLA_ROPE, EVEN_IN, P_QB))
    width = EVEN_IN // N_CHIPS
    pieces = []
    for lo, hi, at in segments:
        for k in range(N_CHIPS):
            a, b = max(lo, k * width), min(hi, (k + 1) * width)
            if a < b:
                pieces.append((k, a - k * width, b - a, at + a - lo))
    return pieces


def _w_in_padded(name, w_in_s):
    tr = MXU_W

    def body(s_ref, o_ref):
        o_ref[...] = jnp.zeros(o_ref.shape, BF16)
        for k, a, n, at in _w_in_pieces():
            o_ref[:, at:at + n] = s_ref[k, :, a:a + n]

    return pl.pallas_call(
        body, name=name, grid=(D // tr,),
        in_specs=[pl.BlockSpec((N_CHIPS, tr, EVEN_IN // N_CHIPS), lambda i: (0, i, 0))],
        out_specs=pl.BlockSpec((tr, P_IN), lambda i: (i, 0)), out_shape=jax.ShapeDtypeStruct((D, P_IN), BF16),
        compiler_params=_params("parallel"))(w_in_s)


def _w_in_sharded(name, d_w_in_p):
    tr = MXU_W

    def body(p_ref, o_ref):
        for k, a, n, at in _w_in_pieces():
            o_ref[k, :, a:a + n] = p_ref[:, at:at + n]

    return pl.pallas_call(
        body, name=name, grid=(D // tr,),
        in_specs=[pl.BlockSpec((tr, P_IN), lambda i: (i, 0))],
        out_specs=pl.BlockSpec((N_CHIPS, tr, EVEN_IN // N_CHIPS), lambda i: (0, i, 0)),
        out_shape=jax.ShapeDtypeStruct((N_CHIPS, D, EVEN_IN // N_CHIPS), BF16),
        compiler_params=_params("parallel"))(d_w_in_p)


def _mla_weights(w_uq_s, w_ukv_s):
    w_uq = jnp.moveaxis(w_uq_s, 0, 1).reshape(Q_LORA, MLA_H, MLA_NOPE + MLA_ROPE)
    w_uq_p = jnp.concatenate([w_uq, jnp.zeros((Q_LORA, MLA_H, LANES - MLA_NOPE - MLA_ROPE), BF16)], axis=2)
    w_ukv = jnp.moveaxis(w_ukv_s, 0, 1).reshape(KV_LORA, MLA_H, MLA_NOPE + MLA_V)
    w_uk_p = jnp.concatenate([w_ukv[:, :, :MLA_NOPE], jnp.zeros((KV_LORA, MLA_H, LANES - MLA_NOPE), BF16)], axis=2)
    return dict(
        w_uq=w_uq_p.reshape(Q_LORA, MLA_H * LANES), w_uk=w_uk_p.reshape(KV_LORA, MLA_H * LANES),
        w_uv=w_ukv[:, :, MLA_NOPE:].reshape(KV_LORA, MLA_H * MLA_V))


def _proj_mm(name, u, w_in, carry=None):
    return _mm(name, u, w_in, kind="nn", grid=(S // TM, 1, 1),
               a_spec=pl.BlockSpec((TM, D), lambda i, j, k: (i, 0)), b_spec=pl.BlockSpec((D, P_IN), lambda i, j, k: (0, 0)),
               o_spec=pl.BlockSpec((TM, P_IN), lambda i, j, k: (i, 0)), out_shape=(S, P_IN), out_dtype=F32, acc_shape=None,
               carry=carry)


def _out_proj(name, o, w, resid, carry=None):
    return _mm(name, o, w, kind="nn", grid=(S // TQ, 1, 1),
               a_spec=pl.BlockSpec((TQ, D), lambda i, j, k: (i, 0)), b_spec=pl.BlockSpec((D, D), lambda i, j, k: (0, 0)),
               o_spec=pl.BlockSpec((TQ, D), lambda i, j, k: (i, 0)), out_shape=(S, D), out_dtype=F32, acc_shape=None,
               resid=resid, r_spec=pl.BlockSpec((TQ, D), lambda i, j, k: (i, 0)), carry=carry)


def _out_proj_bwd(name, dh, o, w, ex):
    d_o = _mm(name + "_x", dh, w, kind="nt", grid=(S // TQ, 1, 1),
              a_spec=pl.BlockSpec((TQ, D), lambda i, j, k: (i, 0)), b_spec=pl.BlockSpec((D, D), lambda i, j, k: (0, 0)),
              o_spec=pl.BlockSpec((TQ, D), lambda i, j, k: (i, 0)), out_shape=(S, D), out_dtype=F32, acc_shape=None)
    d_w = _mm(name + "_w", o, dh, kind="tn", grid=(2, S // TQ),
              a_spec=pl.BlockSpec((TQ, TM), lambda j, k: (k, j)), b_spec=pl.BlockSpec((TQ, D), lambda j, k: (k, 0)),
              o_spec=pl.BlockSpec((TM, D), lambda j, k: (j, 0)), out_shape=(D, D), out_dtype=BF16, acc_shape=(TM, D),
              carry=ex.carry(name + "_w"))
    return d_o, d_w


def _local_step(x, tgt, sm, ex):
    def riding(stage, fn, *args):
        cy = ex.carry(stage)
        res, copies = fn(stage, *args, carry=cy)
        if cy is not None:
            cy.done(copies)
        return res

    cos_t, sin_t = _rope_tables()
    g_mix, g_ffn = sm["g_mix"], sm["g_ffn"]
    r0 = sm["od_rel_bias"][0][:, _band_row_index()].reshape(C_H // 2, 2, TOEP_W)
    nt = 3

    ex.begin()
    w = {"w_in": _w_in_padded("w_in_padded", *ex.weights(*FIRST_WEIGHTS))}
    u0 = _rms_fwd("rms_mix0", x, g_mix[0:1])
    proj = _proj_mm("proj_in", u0, w["w_in"], ex.carry("proj_in"))
    w.update(_mla_weights(*ex.weights(*NEXT_WEIGHTS)))
    qa, ka, va = _mla_prep_fwd("mla_prep", proj, sm["ev_g_cq"], sm["ev_g_ckv"], w["w_uq"], w["w_uk"], w["w_uv"], cos_t, sin_t)
    o_a, lse = riding("mla_attn", _mla_fwd, qa, ka, va)
    o_b, = riding("sb_attn", _sb_fwd, proj)
    o_ev = jnp.concatenate([o_a.astype(BF16), o_b], axis=1)
    w["ev_w_out"] = ex.weights("ev_w_out")[0].reshape(D, D)
    h1 = _out_proj("ev_out", o_ev, w["ev_w_out"], x, ex.carry("ev_out"))
    w["w_gate0"], w["w_up0"], w["w_down0"] = ex.weights("w_gate0", "w_up0", "w_down0")
    h2, gate0, up0 = riding("ffn0", _ffn_fwd, h1, g_ffn[0:1], w["w_gate0"], w["w_up0"], w["w_down0"])
    w["w_qkv"] = jnp.moveaxis(ex.weights("od_w_qkv")[0], 0, 1).reshape(D, nt * D)
    u2 = _rms_fwd("rms_mix1", h2, g_mix[1:2])
    qkv = _mm("qkv", u2, w["w_qkv"], kind="nn", grid=(S // TQ, nt, 1),
              a_spec=pl.BlockSpec((TQ, D), lambda i, t, k: (i, 0)), b_spec=pl.BlockSpec((D, D), lambda i, t, k: (0, t)),
              o_spec=pl.BlockSpec((None, TQ, D), lambda i, t, k: (t, i, 0)),
              out_shape=(nt, S, D), out_dtype=BF16, acc_shape=None, carry=ex.carry("qkv"))
    o_od, = riding("band_attn", _band_fwd, qkv, r0)
    w["od_w_out"] = ex.weights("od_w_out")[0].reshape(D, D)
    h3 = _out_proj("od_out", o_od, w["od_w_out"], h2, ex.carry("od_out"))
    w["w_gate1"], w["w_up1"], w["w_down1"] = ex.weights("w_gate1", "w_up1", "w_down1")
    (h4, gate1, up1), _ = _ffn_fwd("ffn1", h3, g_ffn[1:2], w["w_gate1"], w["w_up1"], w["w_down1"])

    loss, dh4, dg_final = _loss_bwd("loss", h4, sm["g_final"].reshape(1, D), tgt)

    dh3, dg_ffn1, u3, dgate, dup, act = _ffn_bwd("ffn1_bwd", dh4, h3, g_ffn[1:2], gate1, up1,
                                                 w["w_gate1"], w["w_up1"], w["w_down1"])
    d_wg1, d_wu1, d_wd1 = _ffn_wgrads("ffn1_dw", u3, dgate, dup, act, dh4)
    ex.grads("ffn1", {"w_gate1": d_wg1, "w_up1": d_wu1, "w_down1": d_wd1})

    d_ood, d_w_od_out = _out_proj_bwd("od_out_bwd", dh3, o_od, w["od_w_out"], ex)
    dqkv, dr0 = riding("band_attn_bwd", _band_bwd, qkv, r0, d_ood)
    du2 = _mm("qkv_bwd_x", dqkv, w["w_qkv"], kind="nt", grid=(S // TQ, nt),
              a_spec=pl.BlockSpec((None, TQ, D), lambda i, t: (t, i, 0)), b_spec=pl.BlockSpec((D, D), lambda i, t: (0, t)),
              o_spec=pl.BlockSpec((TQ, D), lambda i, t: (i, 0)), out_shape=(S, D), out_dtype=F32, acc_shape=(TQ, D))
    wide, per = D // MXU_W, nt * D // N_CHIPS // MXU_W
    piece = lambda r: pl.BlockSpec((None, TQ, MXU_W), lambda j, k: ((per * j + r) // wide, k, (per * j + r) % wide))
    d_w_qkv = _mm("qkv_bwd_w", u2, dqkv, kind="tn", grid=(N_CHIPS, S // TQ),
                  a_spec=pl.BlockSpec((TQ, D), lambda j, k: (k, 0)), b_spec=[piece(r) for r in range(per)],
                  o_spec=pl.BlockSpec((None, D, per * MXU_W), lambda j, k: (j, 0, 0)),
                  out_shape=(N_CHIPS, D, per * MXU_W), out_dtype=BF16, acc_shape=(D, per * MXU_W))
    shard_cols = lambda a: jnp.moveaxis(a.reshape(a.shape[0], N_CHIPS, a.shape[1] // N_CHIPS), 1, 0)
    ex.grads("od", {"od_w_qkv": d_w_qkv, "od_w_out": d_w_od_out.reshape(N_CHIPS, D // N_CHIPS, D)})
    dh2, dg_mix1 = _rms_bwd("rms_mix1_bwd", du2, h2, g_mix[1:2], dh3, carry=ex.carry("rms_mix1_bwd"))
    d_rel = _bias_table_grad("rel_bias_grad", dr0.reshape(C_H, TOEP_W))[:, :2 * REL_CLIP + 1]

    dh1, dg_ffn0, u1, dgate, dup, act = _ffn_bwd("ffn0_bwd", dh2, h1, g_ffn[0:1], gate0, up0,
                                                 w["w_gate0"], w["w_up0"], w["w_down0"])
    d_wg0, d_wu0, d_wd0 = _ffn_wgrads("ffn0_dw", u1, dgate, dup, act, dh2)
    ex.grads("ffn0", {"w_gate0": d_wg0, "w_up0": d_wu0, "w_down0": d_wd0})

    d_oev, d_w_ev_out = _out_proj_bwd("ev_out_bwd", dh1, o_ev, w["ev_w_out"], ex)
    ex.grads("ev_out", {"ev_w_out": d_w_ev_out.reshape(N_CHIPS, D // N_CHIPS, D)})
    dqa, dka, dva = riding("mla_attn_bwd", _mla_bwd, qa, ka, va, o_a, lse, d_oev, 0)
    dqb, dkb, dvb = riding("sb_attn_bwd", _sb_bwd, proj, d_oev, MLA_H * MLA_V // LANES)
    dcq, dckv, dkr, d_w_uq, d_w_uk, d_w_uv, dg_cq, dg_ckv = _mla_prep_bwd(
        "mla_prep_bwd", dqa, dka, dva, proj, sm["ev_g_cq"], sm["ev_g_ckv"], w["w_uq"], w["w_uk"], w["w_uv"], cos_t, sin_t)
    dproj = [dcq, jnp.zeros((S, LANES), BF16), dckv, dqb, dkb, dvb, dkr]
    d_w_in_p = _mm("proj_in_bwd_w", u0, dproj, kind="tn", grid=(1, S // TQ),
                   a_spec=pl.BlockSpec((TQ, D), lambda j, k: (k, 0)),
                   b_spec=[pl.BlockSpec((TQ, p.shape[1]), lambda j, k: (k, 0)) for p in dproj],
                   o_spec=pl.BlockSpec((D, P_IN), lambda j, k: (0, 0)), out_shape=(D, P_IN), out_dtype=BF16,
                   acc_shape=(D, P_IN), carry=ex.carry("proj_in_bwd_w"))
    d_w_uq_std = d_w_uq.reshape(Q_LORA, MLA_H, LANES)[:, :, :MLA_NOPE + MLA_ROPE].reshape(Q_LORA, -1)
    d_w_ukv = jnp.concatenate([d_w_uk.reshape(KV_LORA, MLA_H, LANES)[:, :, :MLA_NOPE],
                               d_w_uv.reshape(KV_LORA, MLA_H, MLA_V)], axis=2).reshape(KV_LORA, -1)
    ex.grads("ev", {"ev_w_in": _w_in_sharded("w_in_sharded", d_w_in_p), "ev_w_uq": shard_cols(d_w_uq_std.astype(BF16)),
                    "ev_w_ukv": shard_cols(d_w_ukv.astype(BF16))})
    du0 = _mm("proj_in_bwd_x", dproj, w["w_in"], kind="nt", grid=(S // TM, 1, 1),
              a_spec=[pl.BlockSpec((TM, p.shape[1]), lambda i, j, k: (i, 0)) for p in dproj],
              b_spec=pl.BlockSpec((D, P_IN), lambda i, j, k: (0, 0)),
              o_spec=pl.BlockSpec((TM, D), lambda i, j, k: (i, 0)), out_shape=(S, D), out_dtype=F32, acc_shape=None,
              carry=ex.carry("proj_in_bwd_x"))
    grad_x, dg_mix0 = _rms_bwd("rms_mix0_bwd", du0, x, g_mix[0:1], dh1)
    small = {
        "ev_g_cq": dg_cq, "ev_g_ckv": dg_ckv, "od_rel_bias": d_rel.reshape(1, C_H, 2 * REL_CLIP + 1),
        "g_mix": jnp.concatenate([dg_mix0, dg_mix1], axis=0), "g_ffn": jnp.concatenate([dg_ffn0, dg_ffn1], axis=0),
        "g_final": dg_final.reshape(D),
    }
    return loss, grad_x, small


BIG = ("ev_w_in", "ev_w_uq", "ev_w_ukv", "ev_w_out", "od_w_qkv", "od_w_out", "w_gate", "w_up", "w_down")
SMALL = ("ev_g_cq", "ev_g_ckv", "od_rel_bias", "g_mix", "g_ffn", "g_final")
WEIGHTS = ("ev_w_in", "ev_g_cq", "ev_w_uq", "ev_g_ckv", "ev_w_ukv", "ev_w_out", "od_w_qkv", "od_rel_bias", "od_w_out",
           "g_mix", "g_ffn", "w_gate", "w_up", "w_down", "g_final")
GRAD_PARTS = (("ev_w_in", "ev_w_in", 0), ("ev_w_uq", "ev_w_uq", 0), ("ev_w_ukv", "ev_w_ukv", 0),
              ("ev_w_out", "ev_w_out", 0), ("od_w_qkv", "od_w_qkv", 0), ("od_w_out", "od_w_out", 0),
              ("w_gate0", "w_gate", 0), ("w_gate1", "w_gate", 1), ("w_up0", "w_up", 0), ("w_up1", "w_up", 1),
              ("w_down0", "w_down", 0), ("w_down1", "w_down", 1))
PART_OF = {part: (param, layer) for part, param, layer in GRAD_PARTS}
TRANSPOSED = ("w_gate", "w_up")
ADAMW_TRANSPOSED = ("ev_w_in", "ev_w_uq")


def _row_tile(rows, cap=512, sublanes=16):
    for t in range(min(rows, cap), 0, -1):
        if rows % t == 0 and t % sublanes == 0:
            return t
    return rows


def _cast_into_slot(name, w, layer, pos):
    _, rows, cols = w.shape
    tr = _row_tile(rows)

    def body(pos_ref, w_ref, o_ref):
        o_ref[...] = w_ref[...].astype(BF16)

    return pl.pallas_call(
        body, name=name,
        grid_spec=pltpu.PrefetchScalarGridSpec(
            num_scalar_prefetch=1, grid=(rows // tr,),
            in_specs=[pl.BlockSpec((None, tr, cols), lambda i, p: (layer, i, 0))],
            out_specs=pl.BlockSpec((None, tr, cols), lambda i, p: (p[0], i, 0))),
        out_shape=jax.ShapeDtypeStruct((N_CHIPS, rows, cols), BF16), compiler_params=_params("arbitrary"))(pos, w)


def _cast_many_into_slots(name, items, pos, carry):
    tiles = [_row_tile(w.shape[1]) for w, _ in items]
    turns = _Turns([w.shape[1] // tr for (w, _), tr in zip(items, tiles)])

    def body(pos_ref, *refs):
        i = pl.program_id(0)
        for t in range(len(items)):
            @pl.when(turns.mine(t, i))
            def _(w_ref=refs[t], o_ref=refs[len(items) + t]):
                o_ref[...] = w_ref[...].astype(BF16)

    in_specs, out_specs, out_shape = [], [], []
    for t, ((w, layer), tr) in enumerate(zip(items, tiles)):
        _, rows, cols = w.shape
        at = turns.step(t)
        in_specs.append(pl.BlockSpec((None, tr, cols), lambda i, p, at=at, layer=layer: (layer, at(i), 0)))
        out_specs.append(pl.BlockSpec((None, tr, cols), lambda i, p, at=at: (p[0], at(i), 0)))
        out_shape.append(jax.ShapeDtypeStruct((N_CHIPS, rows, cols), BF16))
    res, copies = _carrier_call(body, name=name, grid=(turns.total,), in_specs=in_specs, out_specs=out_specs,
                                out_shape=out_shape, args=[w for w, _ in items], sem=("arbitrary",), carry=carry,
                                prefetch=(pos,))
    if carry is not None:
        carry.done(copies)
    return res


class _Turns:
    def __init__(self, counts):
        self.counts = list(counts)
        self.starts = [sum(self.counts[:t]) for t in range(len(self.counts))]
        self.total = sum(self.counts)

    def step(self, t):
        start, n = self.starts[t], self.counts[t]
        return lambda i: jnp.clip(i - start, 0, n - 1)

    def mine(self, t, i):
        return (i >= self.starts[t]) & (i < self.starts[t] + self.counts[t])


def _pair_sums(name, parts, theirs, pos):
    n, pair = len(parts), 2
    tiles = [_row_tile(b.shape[1]) for b in theirs]
    blocks = [b.shape[1] // tr for b, tr in zip(theirs, tiles)]
    turns = _Turns([N_CHIPS // pair * nb for nb in blocks])

    def body(pos_ref, *refs):
        i = pl.program_id(0)
        for t in range(n):
            @pl.when(turns.mine(t, i))
            def _(a_ref=refs[2 * t], b_ref=refs[2 * t + 1], o_ref=refs[2 * n + t]):
                o_ref[...] = (a_ref[...].astype(F32) + b_ref[...].astype(F32)).astype(BF16)

    in_specs, out_specs = [], []
    for t, (b, tr, nb) in enumerate(zip(theirs, tiles, blocks)):
        at, block = turns.step(t), (pair, tr, b.shape[2])
        in_specs.append(pl.BlockSpec(block, lambda i, p, at=at, nb=nb: (at(i) // nb, p[1] * nb + at(i) % nb, 0)))
        in_specs.append(pl.BlockSpec(block, lambda i, p, at=at, nb=nb: (at(i) // nb, at(i) % nb, 0)))
        out_specs.append(pl.BlockSpec(block, lambda i, p, at=at, nb=nb: (at(i) // nb, at(i) % nb, 0)))
    return pl.pallas_call(
        body, name=name,
        grid_spec=pltpu.PrefetchScalarGridSpec(num_scalar_prefetch=1, grid=(turns.total,), in_specs=in_specs,
                                               out_specs=out_specs),
        out_shape=[jax.ShapeDtypeStruct(b.shape, BF16) for b in theirs],
        compiler_params=_params("arbitrary"))(pos, *[a for pair in zip(parts, theirs) for a in pair])


def _chip_sums(name, items, pos):
    n = len(items)
    tiles = [_row_tile(s.shape[1]) for s, *_ in items]
    turns = _Turns([s.shape[1] // tr for (s, *_), tr in zip(items, tiles)])
    carried = [t for t, item in enumerate(items) if item[4] is not None]

    def body(pos_ref, *refs):
        i = pl.program_id(0)
        for t in range(n):
            @pl.when(turns.mine(t, i))
            def _(s_ref=refs[2 * t], g_ref=refs[2 * t + 1], o_ref=refs[2 * n + len(carried) + t]):
                o_ref[...] = ((s_ref[...].astype(F32) + g_ref[0].astype(F32)) + g_ref[1].astype(F32)) + g_ref[2].astype(F32)

    in_specs, out_specs = [], []
    for t, ((s, got, layer, full_shape, full), tr) in enumerate(zip(items, tiles)):
        at, cols, nb = turns.step(t), s.shape[2], turns.counts[t]
        in_specs.append(pl.BlockSpec((None, tr, cols), lambda i, p, at=at: (p[0], at(i), 0)))
        in_specs.append(pl.BlockSpec((3, tr, cols), lambda i, p, at=at: (0, at(i), 0)))
        out_specs.append(pl.BlockSpec((None, tr, cols), lambda i, p, at=at, nb=nb, layer=layer: (layer, p[1] * nb + at(i), 0)))
    return pl.pallas_call(
        body, name=name,
        grid_spec=pltpu.PrefetchScalarGridSpec(num_scalar_prefetch=1, grid=(turns.total,),
                                               in_specs=in_specs + [ANY] * len(carried), out_specs=out_specs),
        out_shape=[jax.ShapeDtypeStruct(item[3], F32) for item in items],
        input_output_aliases={1 + 2 * n + k: t for k, t in enumerate(carried)},
        compiler_params=_params("arbitrary"))(
            pos, *[a for item in items for a in item[:2]], *[items[t][4] for t in carried])


def _adamw_update(w, g, m, v):
    m_new = ADAM_B1 * m + (1.0 - ADAM_B1) * g
    v_new = ADAM_B2 * v + (1.0 - ADAM_B2) * (g * g)
    m_hat = m_new / (1.0 - ADAM_B1 ** ADAM_STEP)
    v_hat = v_new / (1.0 - ADAM_B2 ** ADAM_STEP)
    return -ADAM_LR * (m_hat / (jnp.sqrt(v_hat) + ADAM_EPS) + ADAM_WD * w), m_new, v_new


def _small_step(name, grads, loss, w, m, v):
    n, n_dev = len(grads), 8
    offs = [sum(g.shape[0] for g in grads[:t]) for t in range(n + 1)]
    rows = -(-(offs[n] + 1) // 8) * 8
    width = max(g.shape[1] for g in grads)

    def body(*refs):
        g_refs, loss_ref = refs[:n], refs[n]
        w_refs, m_refs, v_refs = (refs[1 + k * n:1 + (k + 1) * n] for k in (1, 2, 3))
        outs = refs[4 * n + 1:8 * n + 2]
        mine, slots, send_sem, recv_sem = refs[8 * n + 2:]
        x, y, c, _ = _position()
        me = 4 * x + 2 * y + c

        def peer(k):
            return (1 - x if k & 4 else x, 1 - y if k & 2 else y, 1 - c if k & 1 else c)

        def logical(k):
            px, py, pc = peer(k)
            return 4 * px + 2 * py + pc

        mine[...] = jnp.zeros(mine.shape, F32)
        for t in range(n):
            mine[offs[t]:offs[t + 1], 0:grads[t].shape[1]] = g_refs[t][...]
        mine[offs[n]:offs[n] + 1, 0:LANES] = loss_ref[...]
        slots[me] = mine[...]
        sends = [pltpu.make_async_remote_copy(
            src_ref=mine, dst_ref=slots.at[me], send_sem=send_sem.at[k], recv_sem=recv_sem.at[k],
            device_id=peer(k), device_id_type=MESH) for k in range(1, n_dev)]
        for cp in sends:
            cp.start()
        for k in range(1, n_dev):
            pltpu.make_async_remote_copy(
                src_ref=mine, dst_ref=slots.at[logical(k)], send_sem=send_sem.at[k], recv_sem=recv_sem.at[k],
                device_id=peer(k), device_id_type=MESH).wait_recv()
        for cp in sends:
            cp.wait_send()
        total = slots[0]
        for d in range(1, n_dev):
            total = total + slots[d]
        for t in range(n):
            gv = total[offs[t]:offs[t + 1], 0:grads[t].shape[1]]
            outs[t][...] = gv
            outs[n + t][...], outs[2 * n + t][...], outs[3 * n + t][...] = _adamw_update(
                w_refs[t][...], gv, m_refs[t][...], v_refs[t][...])
        outs[4 * n][...] = total[offs[n]:offs[n] + 1, 0:LANES]

    vm = pl.BlockSpec(memory_space=pltpu.VMEM)
    shapes = [jax.ShapeDtypeStruct(g.shape, F32) for g in grads]
    res = pl.pallas_call(
        body, name=name, in_specs=[vm] * (4 * n + 1), out_specs=[vm] * (4 * n + 1),
        out_shape=shapes * 4 + [jax.ShapeDtypeStruct(loss.shape, F32)],
        scratch_shapes=[pltpu.VMEM((rows, width), F32), pltpu.VMEM((n_dev, rows, width), F32),
                        pltpu.SemaphoreType.DMA((n_dev,)), pltpu.SemaphoreType.DMA((n_dev,))],
    )(*grads, loss, *w, *m, *v)
    return [res[k * n:(k + 1) * n] for k in range(4)], res[4 * n]


def _adamw(name, w, g, m, v):
    rows, cols = w.shape
    tr = _row_tile(rows, sublanes=8)

    def body(w_ref, g_ref, m_ref, v_ref, d_ref, mo_ref, vo_ref):
        d_ref[...], mo_ref[...], vo_ref[...] = _adamw_update(w_ref[...], g_ref[...], m_ref[...], v_ref[...])

    spec = pl.BlockSpec((tr, cols), lambda i: (i, 0))
    shape = jax.ShapeDtypeStruct((rows, cols), F32)
    return pl.pallas_call(body, name=name, grid=(rows // tr,), in_specs=[spec] * 4, out_specs=[spec] * 3,
                          out_shape=[shape] * 3, compiler_params=_params("parallel"))(w, g, m, v)


def kernel(x, ev_w_in, ev_g_cq, ev_w_uq, ev_g_ckv, ev_w_ukv, ev_w_out, od_w_qkv, od_rel_bias, od_w_out, g_mix, g_ffn, w_gate, w_up, w_down, g_final, loss_target, m_ev_w_in, m_ev_g_cq, m_ev_w_uq, m_ev_g_ckv, m_ev_w_ukv, m_ev_w_out, m_od_w_qkv, m_od_rel_bias, m_od_w_out, m_g_mix, m_g_ffn, m_w_gate, m_w_up, m_w_down, m_g_final, v_ev_w_in, v_ev_g_cq, v_ev_w_uq, v_ev_g_ckv, v_ev_w_ukv, v_ev_w_out, v_od_w_qkv, v_od_rel_bias, v_od_w_out, v_g_mix, v_g_ffn, v_w_gate, v_w_up, v_w_down, v_g_final):
    w = dict(ev_w_in=ev_w_in, ev_g_cq=ev_g_cq, ev_w_uq=ev_w_uq, ev_g_ckv=ev_g_ckv, ev_w_ukv=ev_w_ukv, ev_w_out=ev_w_out,
             od_w_qkv=od_w_qkv, od_rel_bias=od_rel_bias, od_w_out=od_w_out, g_mix=g_mix, g_ffn=g_ffn, w_gate=w_gate,
             w_up=w_up, w_down=w_down, g_final=g_final)
    m = dict(ev_w_in=m_ev_w_in, ev_g_cq=m_ev_g_cq, ev_w_uq=m_ev_w_uq, ev_g_ckv=m_ev_g_ckv, ev_w_ukv=m_ev_w_ukv,
             ev_w_out=m_ev_w_out, od_w_qkv=m_od_w_qkv, od_rel_bias=m_od_rel_bias, od_w_out=m_od_w_out, g_mix=m_g_mix,
             g_ffn=m_g_ffn, w_gate=m_w_gate, w_up=m_w_up, w_down=m_w_down, g_final=m_g_final)
    v = dict(ev_w_in=v_ev_w_in, ev_g_cq=v_ev_g_cq, ev_w_uq=v_ev_w_uq, ev_g_ckv=v_ev_g_ckv, ev_w_ukv=v_ev_w_ukv,
             ev_w_out=v_ev_w_out, od_w_qkv=v_od_w_qkv, od_rel_bias=v_od_rel_bias, od_w_out=v_od_w_out, g_mix=v_g_mix,
             g_ffn=v_g_ffn, w_gate=v_w_gate, w_up=v_w_up, w_down=v_w_down, g_final=v_g_final)
    flat2d = lambda a: a.reshape(-1, a.shape[-1])
    for tree in (w, m, v):
        for n in TRANSPOSED:
            tree[n] = jnp.swapaxes(tree[n], 1, 2)

    pos = jnp.stack([2 * lax.axis_index("x") + lax.axis_index("y"), lax.axis_index("c")]).astype(jnp.int32)

    slots = {part: _cast_into_slot("cast_" + part, w[n], layer, pos) for part, n, layer in GRAD_PARTS
             if part in FIRST_WEIGHTS + NEXT_WEIGHTS}
    rest = [(part, n, layer) for part, n, layer in GRAD_PARTS if part not in slots]

    def cast_rest(carry):
        return dict(zip([part for part, _, _ in rest],
                        _cast_many_into_slots("cast_rest", [(w[n], layer) for _, n, layer in rest], pos, carry)))

    ex = _Exchanges(slots, pos, {n: w[n].shape for n in BIG}, cast_rest)

    loss_local, grad_x, small = _local_step(x[0], loss_target[0], {n: w[n] for n in SMALL}, ex)

    grads = ex.finish()
    delta, new_m, new_v = {}, {}, {}
    small_out, loss = _small_step("small_step", [flat2d(small[n]) for n in SMALL], loss_local,
                                  *([flat2d(t[n]) for n in SMALL] for t in (w, m, v)))
    for tree, outs in zip((grads, delta, new_m, new_v), small_out):
        tree.update({n: o.reshape(w[n].shape) for n, o in zip(SMALL, outs)})

    for n in BIG:
        turn = (lambda a: jnp.swapaxes(a, 1, 2)) if n in ADAMW_TRANSPOSED else (lambda a: a)
        shape = turn(w[n]).shape
        outs = _adamw("adamw_" + n, *(flat2d(turn(a)) for a in (w[n], grads[n], m[n], v[n])))
        delta[n], new_m[n], new_v[n] = (turn(o.reshape(shape)) for o in outs)
    for tree in (grads, delta, new_m, new_v):
        for n in TRANSPOSED:
            tree[n] = jnp.swapaxes(tree[n], 1, 2)

    return (loss[0, 0], grad_x[None], *[grads[n] for n in WEIGHTS], *[delta[n] for n in WEIGHTS],
            *[new_m[n] for n in WEIGHTS], *[new_v[n] for n in WEIGHTS])
```
